```python
import jax, jax.numpy as jnp
from jax import lax
import numpy as np

D_MODEL = 1024
BATCH = 8
SEQ = 4096
DEPTH = 1

CHUNK = 64
D_MIX = D_MODEL
SWA_HEAD_DIM = 64
SWA_HEADS = (D_MIX // 2) // SWA_HEAD_DIM
SWA_KV_HEADS = 2
SWA_WIDTH = SWA_HEADS * SWA_HEAD_DIM
SWA_KV_WIDTH = SWA_KV_HEADS * SWA_HEAD_DIM
WINDOW = 128
WINDOW_CHUNKS = WINDOW // CHUNK
HGRN_HEAD_DIM = 128
HGRN_WIDTH = D_MIX - SWA_WIDTH
HGRN_HEADS = HGRN_WIDTH // HGRN_HEAD_DIM
IN_SIZES = (SWA_WIDTH, SWA_KV_WIDTH, SWA_KV_WIDTH,
            HGRN_WIDTH, HGRN_WIDTH, HGRN_WIDTH, HGRN_WIDTH)
D_IN = sum(IN_SIZES)
IN_SPLITS = [int(v) for v in np.cumsum(IN_SIZES)[:-1]]
MEM_LEN = 256
XATTN_HEADS = 4
XATTN_HEAD_DIM = D_MODEL // XATTN_HEADS
D_FF = ((8 * D_MODEL // 3 + 255) // 256) * 256
RMS_EPS = 1e-6
NEG_INF = -1e30

kernel_name = "hymba_swa_sink_hgrn2_xattn_layer"


def rms_norm(x, g):
    xf = x.astype(jnp.float32)
    y = xf * lax.rsqrt(jnp.mean(xf * xf, axis=-1, keepdims=True) + RMS_EPS)
    return (y * g.astype(jnp.float32)).astype(x.dtype)


def swa_with_sinks(q, k, v, sinks):
    B, T, Hq, Dh = q.shape
    Hkv = k.shape[2]
    G = Hq // Hkv
    NC = T // CHUNK
    WC = WINDOW_CHUNKS
    L = (WC + 1) * CHUNK
    qc = q.reshape(B, NC, CHUNK, Hkv, G, Dh)
    pad = ((0, 0), (WC * CHUNK, 0), (0, 0), (0, 0))
    kc = jnp.pad(k, pad).reshape(B, NC + WC, CHUNK, Hkv, Dh)
    vc = jnp.pad(v, pad).reshape(B, NC + WC, CHUNK, Hkv, Dh)
    kband = jnp.concatenate([kc[:, j:j + NC] for j in range(WC + 1)], axis=2)
    vband = jnp.concatenate([vc[:, j:j + NC] for j in range(WC + 1)], axis=2)
    band_chunk = jnp.arange(NC)[:, None] - WC + jnp.arange(WC + 1)[None, :]
    valid = jnp.repeat(band_chunk >= 0, CHUNK, axis=1)
    s = jnp.einsum('bnqhgd,bnkhd->bnhgqk', qc, kband).astype(jnp.float32) * (Dh ** -0.5)
    s = jnp.where(valid[None, :, None, None, None, :], s, NEG_INF)
    sink = jnp.broadcast_to(sinks.astype(jnp.float32).reshape(1, 1, Hkv, G, 1, 1),
                            (B, NC, Hkv, G, CHUNK, 1))
    p = jax.nn.softmax(jnp.concatenate([s, sink], axis=-1), axis=-1)[..., :L]
    o = jnp.einsum('bnhgqk,bnkhd->bnqhgd', p.astype(v.dtype), vband)
    return o.reshape(B, T, Hq * Dh)


def hgrn2(q, f_logit, i, g, lb, onorm_g):
    B, T, H, Dk = q.shape
    Dv = i.shape[-1]
    NC = T // CHUNK
    f32 = jnp.float32
    qf = jax.nn.silu(q.astype(f32)) * (Dk ** -0.5)
    lbf = lb.astype(f32)
    f = lbf + (1.0 - lbf) * jax.nn.sigmoid(f_logit.astype(f32))
    kf = 1.0 - f
    logf = jnp.log(f)

    def chunks(a):
        return a.reshape(B, NC, CHUNK, H, a.shape[-1]).transpose(0, 3, 1, 2, 4)

    qc, kc, vc, lc = chunks(qf), chunks(kf), chunks(i.astype(f32)), chunks(logf)
    b = jnp.cumsum(lc, axis=3)
    b_mid = b[:, :, :, CHUNK // 2 - 1:CHUNK // 2]
    b_last = b[:, :, :, CHUNK - 1:CHUNK]
    A = jnp.einsum('bhnqd,bhnkd->bhnqk', qc * jnp.exp(b - b_mid), kc * jnp.exp(b_mid - b))
    causal = jnp.tril(jnp.ones((CHUNK, CHUNK), dtype=bool))
    A = jnp.where(causal, A, 0.0)
    o_intra = jnp.einsum('bhnqk,bhnkv->bhnqv', A, vc)
    kv = jnp.einsum('bhnkd,bhnkv->bhndv', kc * jnp.exp(b_last - b), vc)
    decay = jnp.exp(b_last[:, :, :, 0, :])

    def step(S, inp):
        d, u = inp
        return d[..., None] * S + u, S

    S0 = jnp.zeros((B, H, Dk, Dv), f32)
    _, S_prev = lax.scan(step, S0, (jnp.moveaxis(decay, 2, 0), jnp.moveaxis(kv, 2, 0)))
    S_prev = jnp.moveaxis(S_prev, 0, 2)
    o_inter = jnp.einsum('bhnqd,bhndv->bhnqv', qc * jnp.exp(b), S_prev)
    o = (o_intra + o_inter).transpose(0, 2, 3, 1, 4).reshape(B, T, H, Dv)
    o = rms_norm(o, onorm_g) * jax.nn.silu(g.astype(f32))
    return o.reshape(B, T, H * Dv).astype(q.dtype)


def cross_attention(u, m, wq, wk, wv, wo):
    B, T, _ = u.shape
    M = m.shape[1]
    q = (u @ wq).reshape(B, T, XATTN_HEADS, XATTN_HEAD_DIM)
    k = (m @ wk).reshape(B, M, XATTN_HEADS, XATTN_HEAD_DIM)
    v = (m @ wv).reshape(B, M, XATTN_HEADS, XATTN_HEAD_DIM)
    s = jnp.einsum('bthd,bmhd->bhtm', q, k).astype(jnp.float32) * (XATTN_HEAD_DIM ** -0.5)
    p = jax.nn.softmax(s, axis=-1).astype(v.dtype)
    o = jnp.einsum('bhtm,bmhd->bthd', p, v).reshape(B, T, D_MODEL)
    return o @ wo


def _fwd_setup_inputs(seed: int = 0) -> dict:
    key = jax.random.key(seed)
    ks = jax.random.split(key, 24)
    f32 = jnp.float32

    def nrm(k, shape, scale):
        return jax.random.normal(k, shape, f32) * scale

    def gain(k, shape):
        return 1.0 + 0.05 * jax.random.normal(k, shape, f32)

    return {
        "x": nrm(ks[0], (BATCH, SEQ, D_MODEL), 1.0),
        "mem": nrm(ks[1], (BATCH, MEM_LEN, D_MODEL), 1.0),
        "w_in": nrm(ks[2], (DEPTH, D_MODEL, D_IN), D_MODEL ** -0.5),
        "sinks": nrm(ks[3], (DEPTH, SWA_HEADS), 0.5),
        "hgrn_lb": nrm(ks[4], (DEPTH + 1, HGRN_WIDTH), 0.1),
        "hgrn_onorm": gain(ks[5], (DEPTH, HGRN_HEAD_DIM)),
        "w_out": nrm(ks[6], (DEPTH, D_MIX, D_MODEL), D_MIX ** -0.5),
        "g_mix_pre": gain(ks[7], (DEPTH, D_MODEL)),
        "g_mix_post": gain(ks[8], (DEPTH, D_MODEL)),
        "g_mem": gain(ks[9], (DEPTH, D_MODEL)),
        "g_x_pre": gain(ks[10], (DEPTH, D_MODEL)),
        "g_x_post": gain(ks[11], (DEPTH, D_MODEL)),
        "wq_x": nrm(ks[12], (DEPTH, D_MODEL, D_MODEL), D_MODEL ** -0.5),
        "wk_x": nrm(ks[13], (DEPTH, D_MODEL, D_MODEL), D_MODEL ** -0.5),
        "wv_x": nrm(ks[14], (DEPTH, D_MODEL, D_MODEL), D_MODEL ** -0.5),
        "wo_x": nrm(ks[15], (DEPTH, D_MODEL, D_MODEL), D_MODEL ** -0.5),
        "g_ffn_pre": gain(ks[16], (DEPTH, D_MODEL)),
        "g_ffn_post": gain(ks[17], (DEPTH, D_MODEL)),
        "w_gate": nrm(ks[18], (DEPTH, D_MODEL, D_FF), D_MODEL ** -0.5),
        "w_up": nrm(ks[19], (DEPTH, D_MODEL, D_FF), D_MODEL ** -0.5),
        "w_down": nrm(ks[20], (DEPTH, D_FF, D_MODEL), D_FF ** -0.5),
    }


def _fwd_reference(x, mem, w_in, sinks, hgrn_lb, hgrn_onorm, w_out, g_mix_pre, g_mix_post,
              g_mem, g_x_pre, g_x_post, wq_x, wk_x, wv_x, wo_x, g_ffn_pre, g_ffn_post,
              w_gate, w_up, w_down):
    B, T, _ = x.shape
    lb_all = jnp.cumsum(jax.nn.softmax(hgrn_lb.astype(jnp.float32), axis=0), axis=0)
    h = x
    for l in range(DEPTH):
        u = rms_norm(h, g_mix_pre[l])
        z = u @ w_in[l]
        qa, ka, va, qh, fh, ih, gh = jnp.split(z, IN_SPLITS, axis=-1)
        ya = swa_with_sinks(qa.reshape(B, T, SWA_HEADS, SWA_HEAD_DIM),
                            ka.reshape(B, T, SWA_KV_HEADS, SWA_HEAD_DIM),
                            va.reshape(B, T, SWA_KV_HEADS, SWA_HEAD_DIM),
                            sinks[l])
        hv = HGRN_WIDTH // HGRN_HEADS
        yh = hgrn2(qh.reshape(B, T, HGRN_HEADS, HGRN_HEAD_DIM),
                   fh.reshape(B, T, HGRN_HEADS, HGRN_HEAD_DIM),
                   ih.reshape(B, T, HGRN_HEADS, hv),
                   gh.reshape(B, T, HGRN_HEADS, hv),
                   lb_all[l].reshape(HGRN_HEADS, HGRN_HEAD_DIM),
                   hgrn_onorm[l])
        y = jnp.concatenate([ya, yh.astype(ya.dtype)], axis=-1) @ w_out[l]
        h = h + rms_norm(y, g_mix_post[l])
        u = rms_norm(h, g_x_pre[l])
        m = rms_norm(mem, g_mem[l])
        y = cross_attention(u, m, wq_x[l], wk_x[l], wv_x[l], wo_x[l])
        h = h + rms_norm(y, g_x_post[l])
        u = rms_norm(h, g_ffn_pre[l])
        y = (jax.nn.silu(u @ w_gate[l]) * (u @ w_up[l])) @ w_down[l]
        h = h + rms_norm(y, g_ffn_post[l])
    return h


import jax as _jax
import jax.numpy as _jnp

TWIN_FORMAT = 'train_step'
FWD_PARAMS = ['x', 'mem', 'w_in', 'sinks', 'hgrn_lb', 'hgrn_onorm', 'w_out', 'g_mix_pre', 'g_mix_post', 'g_mem', 'g_x_pre', 'g_x_post', 'wq_x', 'wk_x', 'wv_x', 'wo_x', 'g_ffn_pre', 'g_ffn_post', 'w_gate', 'w_up', 'w_down']
TWIN_WEIGHTS = ['w_in', 'sinks', 'hgrn_lb', 'hgrn_onorm', 'w_out', 'g_mix_pre', 'g_mix_post', 'g_mem', 'g_x_pre', 'g_x_post', 'wq_x', 'wk_x', 'wv_x', 'wo_x', 'g_ffn_pre', 'g_ffn_post', 'w_gate', 'w_up', 'w_down']
TWIN_DIFF_INPUT = 'x'
TWIN_INPUTS = ['x', 'mem', 'w_in', 'sinks', 'hgrn_lb', 'hgrn_onorm', 'w_out', 'g_mix_pre', 'g_mix_post', 'g_mem', 'g_x_pre', 'g_x_post', 'wq_x', 'wk_x', 'wv_x', 'wo_x', 'g_ffn_pre', 'g_ffn_post', 'w_gate', 'w_up', 'w_down', 'loss_target', 'm_w_in', 'm_sinks', 'm_hgrn_lb', 'm_hgrn_onorm', 'm_w_out', 'm_g_mix_pre', 'm_g_mix_post', 'm_g_mem', 'm_g_x_pre', 'm_g_x_post', 'm_wq_x', 'm_wk_x', 'm_wv_x', 'm_wo_x', 'm_g_ffn_pre', 'm_g_ffn_post', 'm_w_gate', 'm_w_up', 'm_w_down', 'v_w_in', 'v_sinks', 'v_hgrn_lb', 'v_hgrn_onorm', 'v_w_out', 'v_g_mix_pre', 'v_g_mix_post', 'v_g_mem', 'v_g_x_pre', 'v_g_x_post', 'v_wq_x', 'v_wk_x', 'v_wv_x', 'v_wo_x', 'v_g_ffn_pre', 'v_g_ffn_post', 'v_w_gate', 'v_w_up', 'v_w_down']
TWIN_OUTPUTS = ['loss', 'grad_x', 'grad_w_in', 'grad_sinks', 'grad_hgrn_lb', 'grad_hgrn_onorm', 'grad_w_out', 'grad_g_mix_pre', 'grad_g_mix_post', 'grad_g_mem', 'grad_g_x_pre', 'grad_g_x_post', 'grad_wq_x', 'grad_wk_x', 'grad_wv_x', 'grad_wo_x', 'grad_g_ffn_pre', 'grad_g_ffn_post', 'grad_w_gate', 'grad_w_up', 'grad_w_down', 'delta_w_in', 'delta_sinks', 'delta_hgrn_lb', 'delta_hgrn_onorm', 'delta_w_out', 'delta_g_mix_pre', 'delta_g_mix_post', 'delta_g_mem', 'delta_g_x_pre', 'delta_g_x_post', 'delta_wq_x', 'delta_wk_x', 'delta_wv_x', 'delta_wo_x', 'delta_g_ffn_pre', 'delta_g_ffn_post', 'delta_w_gate', 'delta_w_up', 'delta_w_down', 'new_m_w_in', 'new_m_sinks', 'new_m_hgrn_lb', 'new_m_hgrn_onorm', 'new_m_w_out', 'new_m_g_mix_pre', 'new_m_g_mix_post', 'new_m_g_mem', 'new_m_g_x_pre', 'new_m_g_x_post', 'new_m_wq_x', 'new_m_wk_x', 'new_m_wv_x', 'new_m_wo_x', 'new_m_g_ffn_pre', 'new_m_g_ffn_post', 'new_m_w_gate', 'new_m_w_up', 'new_m_w_down', 'new_v_w_in', 'new_v_sinks', 'new_v_hgrn_lb', 'new_v_hgrn_onorm', 'new_v_w_out', 'new_v_g_mix_pre', 'new_v_g_mix_post', 'new_v_g_mem', 'new_v_g_x_pre', 'new_v_g_x_post', 'new_v_wq_x', 'new_v_wk_x', 'new_v_wv_x', 'new_v_wo_x', 'new_v_g_ffn_pre', 'new_v_g_ffn_post', 'new_v_w_gate', 'new_v_w_up', 'new_v_w_down']
TWIN_LEAF_KINDS = {'loss': 'loss', 'grad_x': 'grad_x', 'grad_w_in': 'grad_w', 'grad_sinks': 'grad_w', 'grad_hgrn_lb': 'grad_w', 'grad_hgrn_onorm': 'grad_w', 'grad_w_out': 'grad_w', 'grad_g_mix_pre': 'grad_w', 'grad_g_mix_post': 'grad_w', 'grad_g_mem': 'grad_w', 'grad_g_x_pre': 'grad_w', 'grad_g_x_post': 'grad_w', 'grad_wq_x': 'grad_w', 'grad_wk_x': 'grad_w', 'grad_wv_x': 'grad_w', 'grad_wo_x': 'grad_w', 'grad_g_ffn_pre': 'grad_w', 'grad_g_ffn_post': 'grad_w', 'grad_w_gate': 'grad_w', 'grad_w_up': 'grad_w', 'grad_w_down': 'grad_w', 'delta_w_in': 'delta_w', 'delta_sinks': 'delta_w', 'delta_hgrn_lb': 'delta_w', 'delta_hgrn_onorm': 'delta_w', 'delta_w_out': 'delta_w', 'delta_g_mix_pre': 'delta_w', 'delta_g_mix_post': 'delta_w', 'delta_g_mem': 'delta_w', 'delta_g_x_pre': 'delta_w', 'delta_g_x_post': 'delta_w', 'delta_wq_x': 'delta_w', 'delta_wk_x': 'delta_w', 'delta_wv_x': 'delta_w', 'delta_wo_x': 'delta_w', 'delta_g_ffn_pre': 'delta_w', 'delta_g_ffn_post': 'delta_w', 'delta_w_gate': 'delta_w', 'delta_w_up': 'delta_w', 'delta_w_down': 'delta_w', 'new_m_w_in': 'new_m', 'new_m_sinks': 'new_m', 'new_m_hgrn_lb': 'new_m', 'new_m_hgrn_onorm': 'new_m', 'new_m_w_out': 'new_m', 'new_m_g_mix_pre': 'new_m', 'new_m_g_mix_post': 'new_m', 'new_m_g_mem': 'new_m', 'new_m_g_x_pre': 'new_m', 'new_m_g_x_post': 'new_m', 'new_m_wq_x': 'new_m', 'new_m_wk_x': 'new_m', 'new_m_wv_x': 'new_m', 'new_m_wo_x': 'new_m', 'new_m_g_ffn_pre': 'new_m', 'new_m_g_ffn_post': 'new_m', 'new_m_w_gate': 'new_m', 'new_m_w_up': 'new_m', 'new_m_w_down': 'new_m', 'new_v_w_in': 'new_v', 'new_v_sinks': 'new_v', 'new_v_hgrn_lb': 'new_v', 'new_v_hgrn_onorm': 'new_v', 'new_v_w_out': 'new_v', 'new_v_g_mix_pre': 'new_v', 'new_v_g_mix_post': 'new_v', 'new_v_g_mem': 'new_v', 'new_v_g_x_pre': 'new_v', 'new_v_g_x_post': 'new_v', 'new_v_wq_x': 'new_v', 'new_v_wk_x': 'new_v', 'new_v_wv_x': 'new_v', 'new_v_wo_x': 'new_v', 'new_v_g_ffn_pre': 'new_v', 'new_v_g_ffn_post': 'new_v', 'new_v_w_gate': 'new_v', 'new_v_w_up': 'new_v', 'new_v_w_down': 'new_v'}


def _forward(args):
    return _fwd_reference(*[args[k] for k in FWD_PARAMS])


def _output_shape():
    out = _jax.eval_shape(lambda: _forward(_fwd_setup_inputs(0)))
    return out.shape, out.dtype

N_MICROBATCH = 1
ADAM_LR = 0.001
ADAM_B1 = 0.9
ADAM_B2 = 0.999
ADAM_EPS = 1e-08
ADAM_WD = 0.01
ADAM_STEP = 10
PER_EXAMPLE_BATCH_AXIS = {'x': 0, 'mem': 0, 'loss_target': 0}
SHARED_INPUTS = []
_WEIGHT_DTYPES = {'w_in': _jnp.float32, 'sinks': _jnp.float32, 'hgrn_lb': _jnp.float32, 'hgrn_onorm': _jnp.float32, 'w_out': _jnp.float32, 'g_mix_pre': _jnp.float32, 'g_mix_post': _jnp.float32, 'g_mem': _jnp.float32, 'g_x_pre': _jnp.float32, 'g_x_post': _jnp.float32, 'wq_x': _jnp.float32, 'wk_x': _jnp.float32, 'wv_x': _jnp.float32, 'wo_x': _jnp.float32, 'g_ffn_pre': _jnp.float32, 'g_ffn_post': _jnp.float32, 'w_gate': _jnp.float32, 'w_up': _jnp.float32, 'w_down': _jnp.float32}
MOMENT_SCALE = {'w_in': 5.299883e-01, 'sinks': 2.683592e-02, 'hgrn_lb': 7.320489e-02, 'hgrn_onorm': 2.352895e+00, 'w_out': 7.493007e-01, 'g_mix_pre': 8.671289e-01, 'g_mix_post': 3.172919e+01, 'g_mem': 1.908208e+00, 'g_x_pre': 6.702428e-01, 'g_x_post': 3.280860e+01, 'wq_x': 6.416475e-01, 'wk_x': 6.480043e-01, 'wv_x': 1.819466e+00, 'wo_x': 1.881898e+00, 'g_ffn_pre': 1.625076e+00, 'g_ffn_post': 3.183154e+01, 'w_gate': 4.773552e-01, 'w_up': 7.918618e-01, 'w_down': 1.283307e+00}


def _to_microbatches(a, axis):
    t = _jnp.moveaxis(a, axis, 0)
    t = t.reshape((N_MICROBATCH, t.shape[0] // N_MICROBATCH) + t.shape[1:])
    return _jnp.moveaxis(t, 1, axis + 1)


def setup_inputs(seed: int = 0) -> dict:
    inp = _fwd_setup_inputs(seed)
    key = _jax.random.fold_in(_jax.random.key(seed), 7919)
    shape, _ = _output_shape()
    out = dict(inp)
    out["loss_target"] = _jax.random.normal(_jax.random.fold_in(key, 0), shape, _jnp.float32)
    for i, name in enumerate(TWIN_WEIGHTS):
        w = inp[name].astype(_jnp.float32)
        if MOMENT_SCALE is None:
            s = _jnp.sqrt(_jnp.mean(_jnp.square(w)) + 1e-30)
        else:
            s = MOMENT_SCALE[name]
        km, kv = _jax.random.split(_jax.random.fold_in(key, i + 1))
        out[name] = w
        out["m_" + name] = s * _jax.random.normal(km, w.shape, _jnp.float32)
        out["v_" + name] = (s * s) * _jax.random.uniform(kv, w.shape, _jnp.float32, 0.5, 1.5)
    if N_MICROBATCH > 1:
        for name, axis in PER_EXAMPLE_BATCH_AXIS.items():
            out[name] = _to_microbatches(out[name], axis)
    return {'x': out['x'], 'mem': out['mem'], 'w_in': out['w_in'], 'sinks': out['sinks'], 'hgrn_lb': out['hgrn_lb'], 'hgrn_onorm': out['hgrn_onorm'], 'w_out': out['w_out'], 'g_mix_pre': out['g_mix_pre'], 'g_mix_post': out['g_mix_post'], 'g_mem': out['g_mem'], 'g_x_pre': out['g_x_pre'], 'g_x_post': out['g_x_post'], 'wq_x': out['wq_x'], 'wk_x': out['wk_x'], 'wv_x': out['wv_x'], 'wo_x': out['wo_x'], 'g_ffn_pre': out['g_ffn_pre'], 'g_ffn_post': out['g_ffn_post'], 'w_gate': out['w_gate'], 'w_up': out['w_up'], 'w_down': out['w_down'], 'loss_target': out['loss_target'], 'm_w_in': out['m_w_in'], 'm_sinks': out['m_sinks'], 'm_hgrn_lb': out['m_hgrn_lb'], 'm_hgrn_onorm': out['m_hgrn_onorm'], 'm_w_out': out['m_w_out'], 'm_g_mix_pre': out['m_g_mix_pre'], 'm_g_mix_post': out['m_g_mix_post'], 'm_g_mem': out['m_g_mem'], 'm_g_x_pre': out['m_g_x_pre'], 'm_g_x_post': out['m_g_x_post'], 'm_wq_x': out['m_wq_x'], 'm_wk_x': out['m_wk_x'], 'm_wv_x': out['m_wv_x'], 'm_wo_x': out['m_wo_x'], 'm_g_ffn_pre': out['m_g_ffn_pre'], 'm_g_ffn_post': out['m_g_ffn_post'], 'm_w_gate': out['m_w_gate'], 'm_w_up': out['m_w_up'], 'm_w_down': out['m_w_down'], 'v_w_in': out['v_w_in'], 'v_sinks': out['v_sinks'], 'v_hgrn_lb': out['v_hgrn_lb'], 'v_hgrn_onorm': out['v_hgrn_onorm'], 'v_w_out': out['v_w_out'], 'v_g_mix_pre': out['v_g_mix_pre'], 'v_g_mix_post': out['v_g_mix_post'], 'v_g_mem': out['v_g_mem'], 'v_g_x_pre': out['v_g_x_pre'], 'v_g_x_post': out['v_g_x_post'], 'v_wq_x': out['v_wq_x'], 'v_wk_x': out['v_wk_x'], 'v_wv_x': out['v_wv_x'], 'v_wo_x': out['v_wo_x'], 'v_g_ffn_pre': out['v_g_ffn_pre'], 'v_g_ffn_post': out['v_g_ffn_post'], 'v_w_gate': out['v_w_gate'], 'v_w_up': out['v_w_up'], 'v_w_down': out['v_w_down']}


def _loss(weights, diff, rest, loss_target):
    with _jax.named_scope("forward"):
        args = {**rest, TWIN_DIFF_INPUT: diff, **{k: w.astype(_WEIGHT_DTYPES[k]) for k, w in weights.items()}}
        y = _forward(args)
    with _jax.named_scope("loss_head"):
        err = _jnp.square(y.astype(_jnp.float32) - loss_target)
        return 0.5 * _jnp.sum(_jnp.mean(err, axis=-1)) if err.ndim else 0.5 * err


def _adamw(w, g, m, v):
    m = ADAM_B1 * m + (1.0 - ADAM_B1) * g
    v = ADAM_B2 * v + (1.0 - ADAM_B2) * _jnp.square(g)
    m_hat = m / (1.0 - ADAM_B1 ** ADAM_STEP)
    v_hat = v / (1.0 - ADAM_B2 ** ADAM_STEP)
    delta = -ADAM_LR * (m_hat / (_jnp.sqrt(v_hat) + ADAM_EPS) + ADAM_WD * w)
    return delta, m, v


def reference(x, mem, w_in, sinks, hgrn_lb, hgrn_onorm, w_out, g_mix_pre, g_mix_post, g_mem, g_x_pre, g_x_post, wq_x, wk_x, wv_x, wo_x, g_ffn_pre, g_ffn_post, w_gate, w_up, w_down, loss_target, m_w_in, m_sinks, m_hgrn_lb, m_hgrn_onorm, m_w_out, m_g_mix_pre, m_g_mix_post, m_g_mem, m_g_x_pre, m_g_x_post, m_wq_x, m_wk_x, m_wv_x, m_wo_x, m_g_ffn_pre, m_g_ffn_post, m_w_gate, m_w_up, m_w_down, v_w_in, v_sinks, v_hgrn_lb, v_hgrn_onorm, v_w_out, v_g_mix_pre, v_g_mix_post, v_g_mem, v_g_x_pre, v_g_x_post, v_wq_x, v_wk_x, v_wv_x, v_wo_x, v_g_ffn_pre, v_g_ffn_post, v_w_gate, v_w_up, v_w_down):
    given = dict(x=x, mem=mem, w_in=w_in, sinks=sinks, hgrn_lb=hgrn_lb, hgrn_onorm=hgrn_onorm, w_out=w_out, g_mix_pre=g_mix_pre, g_mix_post=g_mix_post, g_mem=g_mem, g_x_pre=g_x_pre, g_x_post=g_x_post, wq_x=wq_x, wk_x=wk_x, wv_x=wv_x, wo_x=wo_x, g_ffn_pre=g_ffn_pre, g_ffn_post=g_ffn_post, w_gate=w_gate, w_up=w_up, w_down=w_down, loss_target=loss_target, m_w_in=m_w_in, m_sinks=m_sinks, m_hgrn_lb=m_hgrn_lb, m_hgrn_onorm=m_hgrn_onorm, m_w_out=m_w_out, m_g_mix_pre=m_g_mix_pre, m_g_mix_post=m_g_mix_post, m_g_mem=m_g_mem, m_g_x_pre=m_g_x_pre, m_g_x_post=m_g_x_post, m_wq_x=m_wq_x, m_wk_x=m_wk_x, m_wv_x=m_wv_x, m_wo_x=m_wo_x, m_g_ffn_pre=m_g_ffn_pre, m_g_ffn_post=m_g_ffn_post, m_w_gate=m_w_gate, m_w_up=m_w_up, m_w_down=m_w_down, v_w_in=v_w_in, v_sinks=v_sinks, v_hgrn_lb=v_hgrn_lb, v_hgrn_onorm=v_hgrn_onorm, v_w_out=v_w_out, v_g_mix_pre=v_g_mix_pre, v_g_mix_post=v_g_mix_post, v_g_mem=v_g_mem, v_g_x_pre=v_g_x_pre, v_g_x_post=v_g_x_post, v_wq_x=v_wq_x, v_wk_x=v_wk_x, v_wv_x=v_wv_x, v_wo_x=v_wo_x, v_g_ffn_pre=v_g_ffn_pre, v_g_ffn_post=v_g_ffn_post, v_w_gate=v_w_gate, v_w_up=v_w_up, v_w_down=v_w_down)
    weights = {n: given[n] for n in TWIN_WEIGHTS}
    shared = {n: given[n] for n in SHARED_INPUTS}
    per_example = {n: given[n] for n in ['x', 'mem']}
    grad_fn = _jax.value_and_grad(_loss, argnums=(0, 1))

    def one_microbatch(ex, loss_target):
        ex = dict(ex)
        diff = ex.pop(TWIN_DIFF_INPUT)
        return grad_fn(weights, diff, {**shared, **ex}, loss_target)

    if N_MICROBATCH == 1:
        loss, (grad_w, grad_x) = one_microbatch(per_example, given["loss_target"])
    else:
        def body(carry, xs):
            loss_sum, grad_sum = carry
            l_k, (gw_k, gx_k) = one_microbatch(xs[0], xs[1])
            with _jax.named_scope("update"):
                return (loss_sum + l_k, _jax.tree.map(_jnp.add, grad_sum, gw_k)), gx_k

        init = (_jnp.zeros((), _jnp.float32), _jax.tree.map(_jnp.zeros_like, weights))
        (loss, grad_w), grad_x = _jax.lax.scan(body, init, (per_example, given["loss_target"]))
    with _jax.named_scope("update"):
        delta_w, new_m, new_v = {}, {}, {}
        for n in TWIN_WEIGHTS:
            delta_w[n], new_m[n], new_v[n] = _adamw(weights[n], grad_w[n], given["m_" + n], given["v_" + n])
    return (loss, grad_x, *[grad_w[n] for n in TWIN_WEIGHTS], *[delta_w[n] for n in TWIN_WEIGHTS],
            *[new_m[n] for n in TWIN_WEIGHTS], *[new_v[n] for n in TWIN_WEIGHTS])
```

```python
import functools

import jax
import jax.numpy as jnp
from jax import lax
from jax.experimental import pallas as pl
from jax.experimental.pallas import tpu as pltpu

F32 = jnp.float32
BF16 = jnp.bfloat16

D = 1024
D_IN = 2816
D_FF = 2816
CHUNK = 64
SWA_W = 512
KV_W = 128
HG_W = 512
HD = 128
ZQH, ZFH, ZIH, ZGH = 768, 1280, 1792, 2304
XH, XD = 4, 256
EPS = 1e-6
NEG = -1e30
N_DEV = 8
MESH = pl.DeviceIdType.MESH

LR, B1, B2, AEPS, WD, STEP = 0.001, 0.9, 0.999, 1e-08, 0.01, 10
C1 = 1.0 - B1 ** STEP
C2 = 1.0 - B2 ** STEP

VMEM_LIMIT = 56 * 1024 * 1024


def _params(**kw):
    return pltpu.CompilerParams(vmem_limit_bytes=VMEM_LIMIT, **kw)


def _sig(x):
    return 1.0 / (1.0 + jnp.exp(-x))


def _rowsum8(x):
    r, w = x.shape
    return jnp.sum(x.reshape(r // 8, 8, w), axis=0)


def _dot(a, b, ca, cb, precision=None):
    return lax.dot_general(a, b, (((ca,), (cb,)), ((), ())), preferred_element_type=F32,
                           precision=precision)


def _mm(a, b, *, ta=False, tb=False, out_dtype, tm, tn, tk=None, name):
    m = a.shape[1] if ta else a.shape[0]
    k = a.shape[0] if ta else a.shape[1]
    n = b.shape[0] if tb else b.shape[1]
    tm, tn = min(tm, m), min(tn, n)
    tk = k if tk is None else min(tk, k)
    nk = k // tk
    assert m % tm == 0 and n % tn == 0 and k % tk == 0, (name, m, n, k, tm, tn, tk)
    a_spec = pl.BlockSpec((tk, tm), lambda i, j, kk: (kk, i)) if ta else pl.BlockSpec((tm, tk), lambda i, j, kk: (i, kk))
    b_spec = pl.BlockSpec((tn, tk), lambda i, j, kk: (j, kk)) if tb else pl.BlockSpec((tk, tn), lambda i, j, kk: (kk, j))
    ca, cb = (0 if ta else 1), (1 if tb else 0)

    def body(a_ref, b_ref, o_ref, *acc):
        p = _dot(a_ref[...].astype(BF16), b_ref[...].astype(BF16), ca, cb)
        if nk == 1:
            o_ref[...] = p.astype(out_dtype)
        else:
            acc_ref, = acc
            kk = pl.program_id(2)

            @pl.when(kk == 0)
            def _():
                acc_ref[...] = p

            @pl.when(kk > 0)
            def _():
                acc_ref[...] += p

            @pl.when(kk == nk - 1)
            def _():
                o_ref[...] = acc_ref[...].astype(out_dtype)

    return pl.pallas_call(
        body, name=name, out_shape=jax.ShapeDtypeStruct((m, n), out_dtype),
        grid=(m // tm, n // tn, nk), in_specs=[a_spec, b_spec],
        out_specs=pl.BlockSpec((tm, tn), lambda i, j, kk: (i, j)),
        scratch_shapes=[pltpu.VMEM((tm, tn), F32)] if nk > 1 else [],
        compiler_params=_params(dimension_semantics=("parallel", "parallel", "arbitrary")),
    )(a, b)


def _mm2(a1, b1, a2, b2, *, tb=False, tm, tk, name):
    m, k = a1.shape
    n = b1.shape[0] if tb else b1.shape[1]
    tm, tk = min(tm, m), min(tk, k)
    nk = k // tk
    assert m % tm == 0 and k % tk == 0
    cb = 1 if tb else 0

    def body(a1_ref, b1_ref, a2_ref, b2_ref, o_ref):
        p = (_dot(a1_ref[...].astype(BF16), b1_ref[...], 1, cb)
             + _dot(a2_ref[...].astype(BF16), b2_ref[...], 1, cb))
        kk = pl.program_id(1)

        @pl.when(kk == 0)
        def _():
            o_ref[...] = p

        @pl.when(kk > 0)
        def _():
            o_ref[...] += p

    a_spec = pl.BlockSpec((tm, tk), lambda i, kk: (i, kk))
    b_spec = pl.BlockSpec((n, tk), lambda i, kk: (0, kk)) if tb else pl.BlockSpec((tk, n), lambda i, kk: (kk, 0))
    return pl.pallas_call(
        body, name=name, out_shape=jax.ShapeDtypeStruct((m, n), F32),
        grid=(m // tm, nk), in_specs=[a_spec, b_spec, a_spec, b_spec],
        out_specs=pl.BlockSpec((tm, n), lambda i, kk: (i, 0)),
        compiler_params=_params(dimension_semantics=("parallel", "arbitrary")),
    )(a1, b1, a2, b2)


def _rstd(x):
    return lax.rsqrt(jnp.mean(x * x, axis=-1, keepdims=True) + EPS)


def _norm_bwd(xh, r, t):
    return r * (t - xh * jnp.mean(xh * t, axis=-1, keepdims=True))


def _prenorm(x, g, *, name):
    t, d = x.shape
    tb = min(512, t)

    def body(x_ref, g_ref, o_ref):
        xf = x_ref[...]
        o_ref[...] = (xf * _rstd(xf) * g_ref[...]).astype(BF16)

    return pl.pallas_call(
        body, name=name, out_shape=jax.ShapeDtypeStruct((t, d), BF16), grid=(t // tb,),
        in_specs=[pl.BlockSpec((tb, d), lambda i: (i, 0)), pl.BlockSpec((1, d), lambda i: (0, 0))],
        out_specs=pl.BlockSpec((tb, d), lambda i: (i, 0)), compiler_params=_params(),
    )(x, g)


def _post_pre(h, y, g_post, g_pre, *, name):
    t, d = h.shape
    tb = min(512, t)

    def body(h_ref, y_ref, gp_ref, gn_ref, hn_ref, u_ref):
        y_ = y_ref[...]
        hn = h_ref[...] + y_ * _rstd(y_) * gp_ref[...]
        hn_ref[...] = hn
        u_ref[...] = (hn * _rstd(hn) * gn_ref[...]).astype(BF16)

    row = pl.BlockSpec((tb, d), lambda i: (i, 0))
    vec = pl.BlockSpec((1, d), lambda i: (0, 0))
    return pl.pallas_call(
        body, name=name, out_shape=(jax.ShapeDtypeStruct((t, d), F32), jax.ShapeDtypeStruct((t, d), BF16)),
        grid=(t // tb,), in_specs=[row, row, vec, vec], out_specs=(row, row), compiler_params=_params(),
    )(h, y, g_post, g_pre)


def _final_loss(h, y, g_post, target, *, name):
    t, d = h.shape
    tb = min(512, t)

    def body(h_ref, y_ref, g_ref, t_ref, sq_ref, dh_ref, dy_ref, dg_ref):
        @pl.when(pl.program_id(0) == 0)
        def _():
            sq_ref[...] = jnp.zeros_like(sq_ref)
            dg_ref[...] = jnp.zeros_like(dg_ref)

        y_ = y_ref[...]
        r = _rstd(y_)
        yh = y_ * r
        g = g_ref[...]
        err = h_ref[...] + yh * g - t_ref[...]
        sq_ref[...] += _rowsum8(err * err)
        dh = err * (1.0 / d)
        dh_ref[...] = dh
        dg_ref[...] += _rowsum8(dh * yh)
        dy_ref[...] = _norm_bwd(yh, r, dh * g).astype(BF16)

    row = pl.BlockSpec((tb, d), lambda i: (i, 0))
    vec = pl.BlockSpec((1, d), lambda i: (0, 0))
    acc = pl.BlockSpec((8, d), lambda i: (0, 0))
    return pl.pallas_call(
        body, name=name,
        out_shape=(jax.ShapeDtypeStruct((8, d), F32), jax.ShapeDtypeStruct((t, d), F32),
                   jax.ShapeDtypeStruct((t, d), BF16), jax.ShapeDtypeStruct((8, d), F32)),
        grid=(t // tb,), in_specs=[row, row, vec, row], out_specs=(acc, row, row, acc),
        compiler_params=_params(dimension_semantics=("arbitrary",)),
    )(h, y, g_post, target)


def _post_pre_bwd(dh_out, du, hn, y, g_post, g_pre, *, name):
    t, d = hn.shape
    tb = min(512, t)

    def body(dho_ref, du_ref, hn_ref, y_ref, gp_ref, gn_ref, dh_ref, dy_ref, dgn_ref, dgp_ref):
        @pl.when(pl.program_id(0) == 0)
        def _():
            dgn_ref[...] = jnp.zeros_like(dgn_ref)
            dgp_ref[...] = jnp.zeros_like(dgp_ref)

        hn_ = hn_ref[...]
        r2 = _rstd(hn_)
        xh = hn_ * r2
        du_ = du_ref[...]
        dgn_ref[...] += _rowsum8(du_ * xh)
        dh = dho_ref[...] + _norm_bwd(xh, r2, du_ * gn_ref[...])
        dh_ref[...] = dh
        y_ = y_ref[...]
        r1 = _rstd(y_)
        yh = y_ * r1
        dgp_ref[...] += _rowsum8(dh * yh)
        dy_ref[...] = _norm_bwd(yh, r1, dh * gp_ref[...]).astype(BF16)

    row = pl.BlockSpec((tb, d), lambda i: (i, 0))
    vec = pl.BlockSpec((1, d), lambda i: (0, 0))
    acc = pl.BlockSpec((8, d), lambda i: (0, 0))
    return pl.pallas_call(
        body, name=name,
        out_shape=(jax.ShapeDtypeStruct((t, d), F32), jax.ShapeDtypeStruct((t, d), BF16),
                   jax.ShapeDtypeStruct((8, d), F32), jax.ShapeDtypeStruct((8, d), F32)),
        grid=(t // tb,), in_specs=[row, row, row, row, vec, vec], out_specs=(row, row, acc, acc),
        compiler_params=_params(dimension_semantics=("arbitrary",)),
    )(dh_out, du, hn, y, g_post, g_pre)


def _pre_bwd(dh_out, du, x, g, *, name):
    t, d = x.shape
    tb = min(512, t)
    has_res = dh_out is not None

    def body(*refs):
        if has_res:
            dho_ref, du_ref, x_ref, g_ref, dx_ref, dg_ref = refs
        else:
            du_ref, x_ref, g_ref, dx_ref, dg_ref = refs

        @pl.when(pl.program_id(0) == 0)
        def _():
            dg_ref[...] = jnp.zeros_like(dg_ref)

        x_ = x_ref[...]
        r = _rstd(x_)
        xh = x_ * r
        du_ = du_ref[...]
        dg_ref[...] += _rowsum8(du_ * xh)
        dx = _norm_bwd(xh, r, du_ * g_ref[...])
        if has_res:
            dx = dx + dho_ref[...]
        dx_ref[...] = dx

    row = pl.BlockSpec((tb, d), lambda i: (i, 0))
    vec = pl.BlockSpec((1, d), lambda i: (0, 0))
    acc = pl.BlockSpec((8, d), lambda i: (0, 0))
    ins = ([dh_out] if has_res else []) + [du, x, g]
    return pl.pallas_call(
        body, name=name,
        out_shape=(jax.ShapeDtypeStruct((t, d), F32), jax.ShapeDtypeStruct((8, d), F32)),
        grid=(t // tb,), in_specs=[row] * (len(ins) - 1) + [vec], out_specs=(row, acc),
        compiler_params=_params(dimension_semantics=("arbitrary",)),
    )(*ins)


QB = 256


def _half_mask(shape, e):
    lane = lax.broadcasted_iota(jnp.int32, shape, len(shape) - 1)
    return (lane // 64) == e


def _place(kv):
    sw = pltpu.roll(kv, 64, 1)
    m0 = _half_mask(kv.shape, 0)
    return [[jnp.where(m0, kv, 0.0).astype(BF16), jnp.where(m0, 0.0, sw).astype(BF16)],
            [jnp.where(m0, sw, 0.0).astype(BF16), jnp.where(m0, 0.0, kv).astype(BF16)]]


def _swa_valid_q(i, nq, nk):
    qc = lax.broadcasted_iota(jnp.int32, (nq, nk), 0) // CHUNK
    kc = lax.broadcasted_iota(jnp.int32, (nq, nk), 1) // CHUNK - 2
    return (kc <= qc) & (qc <= kc + 2) & (4 * i + kc >= 0)


def _swa_fwd(z, sinks, t):
    nb = t // QB

    def body(s_ref, q_ref, kp_ref, kc_ref, vp_ref, vc_ref, o_ref, lse_ref):
        i = pl.program_id(0)
        kpl = _place(jnp.concatenate([kp_ref[...], kc_ref[...]], axis=0))
        vpl = _place(jnp.concatenate([vp_ref[...], vc_ref[...]], axis=0))
        valid = _swa_valid_q(i, QB, QB + 128)
        lane = lax.broadcasted_iota(jnp.int32, (QB, 128), 1)
        lse_out = jnp.zeros((QB, 128), F32)
        for j in range(4):
            qp = q_ref[:, 128 * j:128 * (j + 1)].astype(BF16)
            acc = jnp.zeros((QB, 128), F32)
            for e in range(2):
                h = 2 * j + e
                kvh = h // 4
                qm = jnp.where(_half_mask(qp.shape, e), qp, jnp.zeros_like(qp))
                s = _dot(qm, kpl[kvh][e], 1, 1) * 0.125
                s = jnp.where(valid, s, NEG)
                sink = s_ref[0, h]
                m = jnp.maximum(jnp.max(s, axis=-1, keepdims=True), sink)
                p = jnp.exp(s - m)
                l = jnp.sum(p, axis=-1, keepdims=True) + jnp.exp(sink - m)
                acc = acc + _dot(p.astype(BF16), vpl[kvh][e], 1, 0) * (1.0 / l)
                lse_out = jnp.where(lane == h, m + jnp.log(l), lse_out)
            o_ref[:, 128 * j:128 * (j + 1)] = acc.astype(BF16)
        lse_ref[...] = lse_out

    prev = lambda c: pl.BlockSpec((128, 128), lambda i: (jnp.maximum(2 * i - 1, 0), c))
    cur = lambda c: pl.BlockSpec((QB, 128), lambda i: (i, c))
    return pl.pallas_call(
        body, name="swa_fwd",
        out_shape=(jax.ShapeDtypeStruct((t, D), BF16), jax.ShapeDtypeStruct((t, 128), F32)),
        grid=(nb,),
        in_specs=[pl.BlockSpec(memory_space=pltpu.SMEM),
                  pl.BlockSpec((QB, SWA_W), lambda i: (i, 0)), prev(4), cur(4), prev(5), cur(5)],
        out_specs=(pl.BlockSpec((QB, SWA_W), lambda i: (i, 0)), pl.BlockSpec((QB, 128), lambda i: (i, 0))),
        compiler_params=_params(),
    )(sinks, z, z, z, z, z)


def _swa_bwd(z, sinks, ymix, lse, dymix, t):
    nb = t // QB
    nq2 = QB + 128

    def body(s_ref, qc_ref, qn_ref, kp_ref, kc_ref, vp_ref, vc_ref, oc_ref, on_ref, doc_ref, don_ref,
             lc_ref, ln_ref, dz_ref, ds_ref):
        i = pl.program_id(0)

        @pl.when(i == 0)
        def _():
            ds_ref[...] = jnp.zeros_like(ds_ref)

        lane = lax.broadcasted_iota(jnp.int32, (8, 128), 1)
        kpl = _place(jnp.concatenate([kp_ref[...], kc_ref[...]], axis=0))
        vpl = _place(jnp.concatenate([vp_ref[...], vc_ref[...]], axis=0))
        valid = _swa_valid_q(i, QB, nq2)
        lse_c = lc_ref[...]
        dsink = jnp.zeros((8, 128), F32)
        for j in range(4):
            cols = slice(128 * j, 128 * (j + 1))
            qp = qc_ref[:, cols].astype(BF16)
            dop = doc_ref[:, cols]
            prod = dop.astype(F32) * oc_ref[:, cols].astype(F32)
            acc = jnp.zeros((QB, 128), F32)
            for e in range(2):
                h = 2 * j + e
                kvh = h // 4
                hm = _half_mask(qp.shape, e)
                qm = jnp.where(hm, qp, jnp.zeros_like(qp))
                dom = jnp.where(hm, dop, jnp.zeros_like(dop))
                dd = jnp.sum(jnp.where(hm, prod, 0.0), axis=-1, keepdims=True)
                lse_h = lse_c[:, h:h + 1]
                s = _dot(qm, kpl[kvh][e], 1, 1) * 0.125
                p = jnp.where(valid, jnp.exp(s - lse_h), 0.0)
                dp = _dot(dom, vpl[kvh][e], 1, 1)
                ds = p * (dp - dd) * 0.125
                acc = acc + _dot(ds.astype(BF16), kpl[kvh][e], 1, 0)
                ps = jnp.exp(s_ref[0, h] - lse_h) * dd
                dsink = dsink - jnp.where(lane == h, _rowsum8(jnp.broadcast_to(ps, (QB, 128))), 0.0)
            dz_ref[:, cols] = acc.astype(BF16)
        ds_ref[...] += dsink
        kpl, vpl = _place(kc_ref[...]), _place(vc_ref[...])
        qr = lax.broadcasted_iota(jnp.int32, (nq2, QB), 0) // CHUNK
        kr = lax.broadcasted_iota(jnp.int32, (nq2, QB), 1) // CHUNK
        valid2 = (kr <= qr) & (qr <= kr + 2) & (4 * i + qr < t // CHUNK)
        lse_a = jnp.concatenate([lse_c, ln_ref[...]], axis=0)
        dk_acc = [[jnp.zeros((QB, 128), F32) for _ in range(2)] for _ in range(2)]
        dv_acc = [[jnp.zeros((QB, 128), F32) for _ in range(2)] for _ in range(2)]
        for j in range(4):
            cols = slice(128 * j, 128 * (j + 1))
            qp = jnp.concatenate([qc_ref[:, cols], qn_ref[:, cols]], axis=0).astype(BF16)
            dop = jnp.concatenate([doc_ref[:, cols], don_ref[:, cols]], axis=0)
            op = jnp.concatenate([oc_ref[:, cols], on_ref[:, cols]], axis=0)
            prod = dop.astype(F32) * op.astype(F32)
            for e in range(2):
                h = 2 * j + e
                kvh = h // 4
                hm = _half_mask(qp.shape, e)
                qm = jnp.where(hm, qp, jnp.zeros_like(qp))
                dom = jnp.where(hm, dop, jnp.zeros_like(dop))
                dd = jnp.sum(jnp.where(hm, prod, 0.0), axis=-1, keepdims=True)
                s = _dot(qm, kpl[kvh][e], 1, 1) * 0.125
                p = jnp.where(valid2, jnp.exp(s - lse_a[:, h:h + 1]), 0.0)
                dv_acc[kvh][e] = dv_acc[kvh][e] + _dot(p.astype(BF16), dom, 0, 0)
                dp = _dot(dom, vpl[kvh][e], 1, 1)
                ds = p * (dp - dd) * 0.125
                dk_acc[kvh][e] = dk_acc[kvh][e] + _dot(ds.astype(BF16), qm, 0, 0)
        dk = dk_acc[0][0] + dk_acc[1][1] + pltpu.roll(dk_acc[0][1] + dk_acc[1][0], 64, 1)
        dv = dv_acc[0][0] + dv_acc[1][1] + pltpu.roll(dv_acc[0][1] + dv_acc[1][0], 64, 1)
        dz_ref[:, 512:640] = dk.astype(BF16)
        dz_ref[:, 640:768] = dv.astype(BF16)

    last = 2 * nb - 1
    prev = lambda c: pl.BlockSpec((128, 128), lambda i: (jnp.maximum(2 * i - 1, 0), c))
    cur = lambda w, c: pl.BlockSpec((QB, w), lambda i: (i, c))
    nxt = lambda w: pl.BlockSpec((128, w), lambda i: (jnp.minimum(2 * i + 2, last), 0))
    return pl.pallas_call(
        body, name="swa_bwd",
        out_shape=(jax.ShapeDtypeStruct((t, 768), BF16), jax.ShapeDtypeStruct((8, 128), F32)),
        grid=(nb,),
        in_specs=[pl.BlockSpec(memory_space=pltpu.SMEM),
                  cur(SWA_W, 0), nxt(SWA_W), prev(4), cur(128, 4), prev(5), cur(128, 5),
                  cur(SWA_W, 0), nxt(SWA_W), cur(SWA_W, 0), nxt(SWA_W), cur(128, 0), nxt(128)],
        out_specs=(pl.BlockSpec((QB, 768), lambda i: (i, 0)), pl.BlockSpec((8, 128), lambda i: (0, 0))),
        compiler_params=_params(dimension_semantics=("arbitrary",)),
    )(sinks, z, z, z, z, z, z, ymix, ymix, dymix, dymix, lse, lse)


HB = 256
HI = lax.Precision.HIGHEST


def _lower_bound(lb_ref):
    a = lb_ref[...]
    a0, a1 = a[0:1], a[1:2]
    mx = jnp.maximum(a0, a1)
    e0, e1 = jnp.exp(a0 - mx), jnp.exp(a1 - mx)
    return e0 / (e0 + e1)


def _tri(lower):
    r = lax.broadcasted_iota(jnp.int32, (CHUNK, CHUNK), 0)
    c = lax.broadcasted_iota(jnp.int32, (CHUNK, CHUNK), 1)
    return (c <= r) if lower else (c >= r)


def _hgrn_chunk(q, fl, lb):
    sq = _sig(q)
    qf = q * sq * (HD ** -0.5)
    sg = _sig(fl)
    f = lb + (1.0 - lb) * sg
    kf = 1.0 - f
    b = _dot(_tri(True).astype(F32), jnp.log(f), 1, 0, precision=HI)
    b_mid = b[CHUNK // 2 - 1:CHUNK // 2]
    b_last = b[CHUNK - 1:CHUNK]
    qm = qf * jnp.exp(b - b_mid)
    km = kf * jnp.exp(b_mid - b)
    kl = kf * jnp.exp(b_last - b)
    qb = qf * jnp.exp(b)
    return dict(sq=sq, qf=qf, sg=sg, f=f, kf=kf, b=b, b_mid=b_mid, b_last=b_last, qm=qm, km=km, kl=kl, qb=qb)


def _hgrn_fwd(z, hgrn_lb, onorm, ymix, t):
    nb = t // HB
    nc = HB // CHUNK

    def body(zq_ref, zf_ref, zi_ref, zg_ref, lb_ref, on_ref, ymix_in, y_ref, o_ref, sp_ref, st_ref):
        del ymix_in

        @pl.when(pl.program_id(1) == 0)
        def _():
            st_ref[...] = jnp.zeros_like(st_ref)

        lb = _lower_bound(lb_ref)
        gn = on_ref[...]

        def chunk(c, carry):
            rows = pl.ds(pl.multiple_of(c * CHUNK, CHUNK), CHUNK)
            w = _hgrn_chunk(zq_ref[rows, :], zf_ref[rows, :], lb)
            iv = zi_ref[rows, :].astype(BF16)
            st = st_ref[...]
            sp_ref[0, c] = st
            a = jnp.where(_tri(True), _dot(w["qm"].astype(BF16), w["km"].astype(BF16), 1, 1), 0.0)
            o = _dot(a.astype(BF16), iv, 1, 0) + _dot(w["qb"].astype(BF16), st.astype(BF16), 1, 1)
            st_ref[...] = st * jnp.exp(w["b_last"]) + _dot(iv, w["kl"].astype(BF16), 0, 0)
            o_ref[rows, :] = o
            gg = zg_ref[rows, :]
            y_ref[rows, :] = (o * _rstd(o) * gn * (gg * _sig(gg))).astype(BF16)
            return carry

        lax.fori_loop(0, nc, chunk, 0)

    col = lambda base: pl.BlockSpec((HB, HD), lambda h, j: (j, base // HD + h))
    return pl.pallas_call(
        body, name="hgrn_fwd",
        out_shape=(jax.ShapeDtypeStruct((t, D), BF16), jax.ShapeDtypeStruct((t, HG_W), F32),
                   jax.ShapeDtypeStruct((4, t // CHUNK, HD, HD), F32)),
        grid=(4, nb),
        in_specs=[col(ZQH), col(ZFH), col(ZIH), col(ZGH),
                  pl.BlockSpec((2, HD), lambda h, j: (0, h)), pl.BlockSpec((1, HD), lambda h, j: (0, 0)),
                  pl.BlockSpec(memory_space=pl.ANY)],
        out_specs=(pl.BlockSpec((HB, HD), lambda h, j: (j, SWA_W // HD + h)),
                   pl.BlockSpec((HB, HD), lambda h, j: (j, h)),
                   pl.BlockSpec((1, nc, HD, HD), lambda h, j: (h, j, 0, 0))),
        scratch_shapes=[pltpu.VMEM((HD, HD), F32)],
        input_output_aliases={6: 0},
        compiler_params=_params(dimension_semantics=("arbitrary", "arbitrary")),
    )(z, z, z, z, hgrn_lb, onorm, ymix)


def _hgrn_bwd(z, hgrn_lb, onorm, o_save, sprev, dymix, t):
    nb = t // HB
    nc = HB // CHUNK

    def body(zq_ref, zf_ref, zi_ref, zg_ref, lb_ref, on_ref, o_ref, sp_ref, dy_ref,
             dq_ref, df_ref, di_ref, dg_ref, dlb_ref, don_ref, dst_ref):
        hh, jj = pl.program_id(0), pl.program_id(1)

        @pl.when(jj == 0)
        def _():
            dst_ref[...] = jnp.zeros_like(dst_ref)
            dlb_ref[...] = jnp.zeros_like(dlb_ref)

        @pl.when((jj == 0) & (hh == 0))
        def _():
            don_ref[...] = jnp.zeros_like(don_ref)

        lb = _lower_bound(lb_ref)
        gn = on_ref[...]
        row = lax.broadcasted_iota(jnp.int32, (CHUNK, HD), 0)

        def chunk(cc, carry):
            c = nc - 1 - cc
            rows = pl.ds(pl.multiple_of(c * CHUNK, CHUNK), CHUNK)
            q = zq_ref[rows, :]
            w = _hgrn_chunk(q, zf_ref[rows, :], lb)
            iv = zi_ref[rows, :].astype(BF16)
            gg = zg_ref[rows, :]
            o = o_ref[rows, :]
            st = sp_ref[0, c]
            dst = dst_ref[...]
            dout = dy_ref[rows, :].astype(F32)
            sgg = _sig(gg)
            r = _rstd(o)
            oh = o * r
            dyn = dout * (gg * sgg)
            dg_ref[rows, :] = (dout * oh * gn * (sgg * (1.0 + gg * (1.0 - sgg)))).astype(BF16)
            don_ref[...] += _rowsum8(dyn * oh)
            do = _norm_bwd(oh, r, dyn * gn).astype(BF16)
            qm, km, kl, qb = (w[n].astype(BF16) for n in ("qm", "km", "kl", "qb"))
            dstb = dst.astype(BF16)
            d_row = jnp.exp(w["b_last"])
            dqb = _dot(do, st.astype(BF16), 1, 0)
            dst_ref[...] = dst * d_row + _dot(do, qb, 0, 0)
            dd_row = jnp.sum(dst * st, axis=0, keepdims=True)
            at = jnp.where(_tri(False), _dot(km, qm, 1, 1), 0.0)
            di_ref[rows, :] = (_dot(at.astype(BF16), do, 1, 0) + _dot(kl, dstb, 1, 1)).astype(BF16)
            dkl = _dot(iv, dstb, 1, 0)
            da = jnp.where(_tri(True), _dot(do, iv, 1, 1), 0.0).astype(BF16)
            dat = jnp.where(_tri(False), _dot(iv, do, 1, 1), 0.0).astype(BF16)
            dqm = _dot(da, km, 1, 0)
            dkm = _dot(dat, qm, 1, 0)
            e1, e2 = jnp.exp(w["b"] - w["b_mid"]), jnp.exp(w["b_mid"] - w["b"])
            e3, e4 = jnp.exp(w["b_last"] - w["b"]), jnp.exp(w["b"])
            dqf = dqm * e1 + dqb * e4
            dkf = dkm * e2 + dkl * e3
            t_qm, t_km, t_kl = dqm * w["qm"], dkm * w["km"], dkl * w["kl"]
            db = t_qm - t_km - t_kl + dqb * w["qb"]
            db_mid = jnp.sum(t_km - t_qm, axis=0, keepdims=True)
            db_last = jnp.sum(t_kl, axis=0, keepdims=True) + dd_row * d_row
            db = db + jnp.where(row == CHUNK // 2 - 1, db_mid, 0.0) + jnp.where(row == CHUNK - 1, db_last, 0.0)
            dlogf = _dot(_tri(False).astype(F32), db, 1, 0, precision=HI)
            dfv = dlogf / w["f"] - dkf
            sg = w["sg"]
            df_ref[rows, :] = (dfv * (1.0 - lb) * sg * (1.0 - sg)).astype(BF16)
            dlb_ref[...] += _rowsum8(dfv * (1.0 - sg))
            sq = w["sq"]
            dq_ref[rows, :] = (dqf * (HD ** -0.5) * (sq * (1.0 + q * (1.0 - sq)))).astype(BF16)
            return carry

        lax.fori_loop(0, nc, chunk, 0)

    rev = lambda j: nb - 1 - j
    col = lambda base: pl.BlockSpec((HB, HD), lambda h, j: (rev(j), base // HD + h))
    out = pl.BlockSpec((HB, HD), lambda h, j: (rev(j), h))
    return pl.pallas_call(
        body, name="hgrn_bwd",
        out_shape=(jax.ShapeDtypeStruct((t, HG_W), BF16),) * 4
        + (jax.ShapeDtypeStruct((8, HG_W), F32), jax.ShapeDtypeStruct((8, HD), F32)),
        grid=(4, nb),
        in_specs=[col(ZQH), col(ZFH), col(ZIH), col(ZGH),
                  pl.BlockSpec((2, HD), lambda h, j: (0, h)), pl.BlockSpec((1, HD), lambda h, j: (0, 0)),
                  out, pl.BlockSpec((1, nc, HD, HD), lambda h, j: (h, rev(j), 0, 0)),
                  pl.BlockSpec((HB, HD), lambda h, j: (rev(j), SWA_W // HD + h))],
        out_specs=(out, out, out, out, pl.BlockSpec((8, HD), lambda h, j: (0, h)),
                   pl.BlockSpec((8, HD), lambda h, j: (0, 0))),
        scratch_shapes=[pltpu.VMEM((HD, HD), F32)],
        compiler_params=_params(dimension_semantics=("arbitrary", "arbitrary")),
    )(z, z, z, z, hgrn_lb, onorm, o_save, sprev, dymix)


def _assemble_dz(dza, dq, df, di, dg, t):
    tb = min(512, t)

    def body(a_ref, q_ref, f_ref, i_ref, g_ref, o_ref):
        o_ref[:, 0:ZQH] = a_ref[...]
        o_ref[:, ZQH:ZFH] = q_ref[...]
        o_ref[:, ZFH:ZIH] = f_ref[...]
        o_ref[:, ZIH:ZGH] = i_ref[...]
        o_ref[:, ZGH:D_IN] = g_ref[...]

    row = lambda w: pl.BlockSpec((tb, w), lambda i: (i, 0))
    return pl.pallas_call(
        body, name="assemble_dz", out_shape=jax.ShapeDtypeStruct((t, D_IN), BF16), grid=(t // tb,),
        in_specs=[row(ZQH), row(HG_W), row(HG_W), row(HG_W), row(HG_W)], out_specs=row(D_IN),
        compiler_params=_params(),
    )(dza, dq, df, di, dg)


XB = 512


def _xattn_fwd(q, k, v, t):
    tb = min(XB, t)

    def body(q_ref, k_ref, v_ref, o_ref):
        for h in range(XH):
            cols = slice(XD * h, XD * (h + 1))
            s = _dot(q_ref[:, cols], k_ref[:, cols], 1, 1) * (XD ** -0.5)
            p = jnp.exp(s - jnp.max(s, axis=-1, keepdims=True))
            l = jnp.sum(p, axis=-1, keepdims=True)
            o_ref[:, cols] = (_dot(p.astype(BF16), v_ref[:, cols], 1, 0) * (1.0 / l)).astype(BF16)

    row = pl.BlockSpec((tb, D), lambda i: (i, 0))
    mem = pl.BlockSpec(k.shape, lambda i: (0, 0))
    return pl.pallas_call(
        body, name="xattn_fwd", out_shape=jax.ShapeDtypeStruct((t, D), BF16), grid=(t // tb,),
        in_specs=[row, mem, mem], out_specs=row, compiler_params=_params(),
    )(q, k, v)


def _xattn_bwd(q, k, v, do, t):
    tb = min(XB, t)

    def body(q_ref, k_ref, v_ref, do_ref, dq_ref, dk_ref, dv_ref):
        @pl.when(pl.program_id(0) == 0)
        def _():
            dk_ref[...] = jnp.zeros_like(dk_ref)
            dv_ref[...] = jnp.zeros_like(dv_ref)

        for h in range(XH):
            cols = slice(XD * h, XD * (h + 1))
            qh, kh, vh, doh = q_ref[:, cols], k_ref[:, cols], v_ref[:, cols], do_ref[:, cols]
            s = _dot(qh, kh, 1, 1) * (XD ** -0.5)
            p = jnp.exp(s - jnp.max(s, axis=-1, keepdims=True))
            p = p * (1.0 / jnp.sum(p, axis=-1, keepdims=True))
            dp = _dot(doh, vh, 1, 1)
            ds = (p * (dp - jnp.sum(p * dp, axis=-1, keepdims=True)) * (XD ** -0.5)).astype(BF16)
            dq_ref[:, cols] = _dot(ds, kh, 1, 0).astype(BF16)
            dk_ref[:, cols] += _dot(ds, qh, 0, 0)
            dv_ref[:, cols] += _dot(p.astype(BF16), doh, 0, 0)

    row = pl.BlockSpec((tb, D), lambda i: (i, 0))
    mem = pl.BlockSpec(k.shape, lambda i: (0, 0))
    return pl.pallas_call(
        body, name="xattn_bwd",
        out_shape=(jax.ShapeDtypeStruct((t, D), BF16), jax.ShapeDtypeStruct(k.shape, F32),
                   jax.ShapeDtypeStruct(k.shape, F32)),
        grid=(t // tb,), in_specs=[row, mem, mem, row], out_specs=(row, mem, mem),
        compiler_params=_params(dimension_semantics=("arbitrary",)),
    )(q, k, v, do)


def _mem_gain_bwd(dm, mem, *, name):
    def body(dm_ref, m_ref, dg_ref):
        m_ = m_ref[...]
        dg_ref[...] = _rowsum8(dm_ref[...] * (m_ * _rstd(m_)))

    return pl.pallas_call(body, name=name, out_shape=jax.ShapeDtypeStruct((8, D), F32),
                          compiler_params=_params())(dm, mem)


FM, FN = 512, 1408


def _ffn_up(u, wgt, wut, t):
    tm = min(FM, t)

    def body(u_ref, wg_ref, wu_ref, g_ref, up_ref, a_ref):
        u_ = u_ref[...]
        g = _dot(u_, wg_ref[...], 1, 1)
        up = _dot(u_, wu_ref[...], 1, 1)
        g_ref[...] = g.astype(BF16)
        up_ref[...] = up.astype(BF16)
        a_ref[...] = (g * _sig(g) * up).astype(BF16)

    w = pl.BlockSpec((FN, D), lambda i, j: (j, 0))
    o = pl.BlockSpec((tm, FN), lambda i, j: (i, j))
    return pl.pallas_call(
        body, name="ffn_up", out_shape=(jax.ShapeDtypeStruct((t, D_FF), BF16),) * 3,
        grid=(t // tm, D_FF // FN), in_specs=[pl.BlockSpec((tm, D), lambda i, j: (i, 0)), w, w],
        out_specs=(o, o, o), compiler_params=_params(),
    )(u, wgt, wut)


def _ffn_down_bwd(dy, wd, gate, up, t):
    tm = min(FM, t)

    def body(dy_ref, w_ref, g_ref, up_ref, dg_ref, dup_ref):
        da = _dot(dy_ref[...], w_ref[...], 1, 1)
        g = g_ref[...].astype(F32)
        sg = _sig(g)
        dup_ref[...] = (da * g * sg).astype(BF16)
        dg_ref[...] = (da * up_ref[...].astype(F32) * (sg * (1.0 + g * (1.0 - sg)))).astype(BF16)

    o = pl.BlockSpec((tm, FN), lambda i, j: (i, j))
    return pl.pallas_call(
        body, name="ffn_down_bwd", out_shape=(jax.ShapeDtypeStruct((t, D_FF), BF16),) * 2,
        grid=(t // tm, D_FF // FN),
        in_specs=[pl.BlockSpec((tm, D), lambda i, j: (i, 0)), pl.BlockSpec((FN, D), lambda i, j: (j, 0)), o, o],
        out_specs=(o, o), compiler_params=_params(),
    )(dy, wd, gate, up)


def _local_step(x, mem, target, w, sm):
    t = x.shape[0]
    gw = {}
    u1 = _prenorm(x, sm["g_mix_pre"], name="prenorm_mix")
    z = _mm(u1, w["winT"], tb=True, out_dtype=F32, tm=1024, tn=1408, name="mm_z")
    ymix, lse = _swa_fwd(z, sm["sinks"], t)
    ymix, o_h, sprev = _hgrn_fwd(z, sm["hgrn_lb"], sm["hgrn_onorm"], ymix, t)
    y1 = _mm(ymix, w["wout"], out_dtype=F32, tm=1024, tn=1024, name="mm_y1")
    h1, u2 = _post_pre(x, y1, sm["g_mix_post"], sm["g_x_pre"], name="post_mix")
    mn = _prenorm(mem, sm["g_mem"], name="prenorm_mem")
    qx = _mm(u2, w["wq"], out_dtype=BF16, tm=1024, tn=1024, name="mm_qx")
    kx = _mm(mn, w["wk"], out_dtype=BF16, tm=1024, tn=1024, name="mm_kx")
    vx = _mm(mn, w["wv"], out_dtype=BF16, tm=1024, tn=1024, name="mm_vx")
    ox = _xattn_fwd(qx, kx, vx, t)
    y2 = _mm(ox, w["wo"], out_dtype=F32, tm=1024, tn=1024, name="mm_y2")
    h2, u3 = _post_pre(h1, y2, sm["g_x_post"], sm["g_ffn_pre"], name="post_x")
    gate, up, act = _ffn_up(u3, w["wgT"], w["wuT"], t)
    y3 = _mm(act, w["wd"], out_dtype=F32, tm=1024, tn=1024, tk=1408, name="mm_y3")
    sq, dh3, dy3, dg_ffn_post = _final_loss(h2, y3, sm["g_ffn_post"], target, name="final_loss")
    gw["wd"] = _mm(act, dy3, ta=True, out_dtype=BF16, tm=1408, tn=1024, tk=512, name="mm_gwd")
    dgate, dup = _ffn_down_bwd(dy3, w["wd"], gate, up, t)
    gw["wgT"] = _mm(dgate, u3, ta=True, out_dtype=BF16, tm=1408, tn=1024, tk=512, name="mm_gwg")
    gw["wuT"] = _mm(dup, u3, ta=True, out_dtype=BF16, tm=1408, tn=1024, tk=512, name="mm_gwu")
    du3 = _mm2(dgate, w["wgT"], dup, w["wuT"], tm=512, tk=1408, name="mm_du3")
    dh2, dy2, dg_ffn_pre, dg_x_post = _post_pre_bwd(dh3, du3, h2, y2, sm["g_x_post"], sm["g_ffn_pre"], name="post_x_bwd")
    gw["wo"] = _mm(ox, dy2, ta=True, out_dtype=BF16, tm=1024, tn=1024, tk=512, name="mm_gwo")
    dox = _mm(dy2, w["wo"], tb=True, out_dtype=BF16, tm=1024, tn=1024, name="mm_dox")
    dqx, dkx, dvx = _xattn_bwd(qx, kx, vx, dox, t)
    gw["wq"] = _mm(u2, dqx, ta=True, out_dtype=BF16, tm=1024, tn=1024, tk=512, name="mm_gwq")
    gw["wk"] = _mm(mn, dkx, ta=True, out_dtype=BF16, tm=1024, tn=1024, name="mm_gwk")
    gw["wv"] = _mm(mn, dvx, ta=True, out_dtype=BF16, tm=1024, tn=1024, name="mm_gwv")
    dmn = _mm2(dkx, w["wk"], dvx, w["wv"], tb=True, tm=256, tk=1024, name="mm_dmn")
    dg_mem = _mem_gain_bwd(dmn, mem, name="mem_gain_bwd")
    du2 = _mm(dqx, w["wq"], tb=True, out_dtype=F32, tm=1024, tn=1024, name="mm_du2")
    dh1, dy1, dg_x_pre, dg_mix_post = _post_pre_bwd(dh2, du2, h1, y1, sm["g_mix_post"], sm["g_x_pre"], name="post_mix_bwd")
    gw["wout"] = _mm(ymix, dy1, ta=True, out_dtype=BF16, tm=1024, tn=1024, tk=512, name="mm_gwout")
    dymix = _mm(dy1, w["wout"], tb=True, out_dtype=BF16, tm=1024, tn=1024, name="mm_dymix")
    dza, dsinks = _swa_bwd(z, sm["sinks"], ymix, lse, dymix, t)
    dqh, dfh, dih, dgh, dlb, donorm = _hgrn_bwd(z, sm["hgrn_lb"], sm["hgrn_onorm"], o_h, sprev, dymix, t)
    dz = _assemble_dz(dza, dqh, dfh, dih, dgh, t)
    gw["winT"] = _mm(dz, u1, ta=True, out_dtype=BF16, tm=1408, tn=1024, tk=512, name="mm_gwin")
    du1 = _mm(dz, w["winT"], out_dtype=F32, tm=512, tn=1024, tk=1408, name="mm_du1")
    grad_x, dg_mix_pre = _pre_bwd(dh1, du1, x, sm["g_mix_pre"], name="pre_mix_bwd")
    parts = dict(g_mix_pre=dg_mix_pre, g_mix_post=dg_mix_post, g_mem=dg_mem, g_x_pre=dg_x_pre,
                 g_x_post=dg_x_post, g_ffn_pre=dg_ffn_pre, g_ffn_post=dg_ffn_post,
                 hgrn_onorm=donorm, hgrn_lb=dlb, sinks=dsinks, sq=sq)
    return grad_x, gw, parts


def _position():
    return lax.axis_index("x"), lax.axis_index("y"), lax.axis_index("c")


def _peer(pos, k):
    x, y, c = pos
    return (1 - x if k & 4 else x, 1 - y if k & 2 else y, 1 - c if k & 1 else c)


def _linear(pos):
    x, y, c = pos
    return 4 * x + 2 * y + c


def _all_gather(shards):
    n = len(shards)
    rows = [s.shape[0] for s in shards]

    def body(*refs):
        ins, outs = refs[:n], refs[n:2 * n]
        send_sems, recv_sems, local_sems = refs[2 * n:]
        me = _position()
        x, y, c = me
        sibling = (x, y, 1 - c)
        chips = [(1 - x, y), (x, 1 - y), (1 - x, 1 - y)]

        def blk(a, pos):
            return outs[a].at[pl.ds(_linear(pos) * rows[a], rows[a]), :]

        def copy(a, k, block, to, src=None):
            return pltpu.make_async_remote_copy(
                src_ref=blk(a, block) if src is None else src, dst_ref=blk(a, block),
                send_sem=send_sems.at[7 * a + k], recv_sem=recv_sems.at[7 * a + k],
                device_id=to, device_id_type=MESH)

        mine = [pltpu.make_async_copy(ins[a], blk(a, me), local_sems.at[a]) for a in range(n)]
        for cp in mine:
            cp.start()
        first = []
        for a in range(n):
            first.append(copy(a, 0, me, sibling, src=ins[a]))
            first += [copy(a, 1 + j, me, (*chip, c), src=ins[a]) for j, chip in enumerate(chips)]
        for cp in first:
            cp.start()
        passed = []
        for j, chip in enumerate(chips):
            for a in range(n):
                copy(a, 1 + j, (*chip, c), me).wait_recv()
                fwd = copy(a, 4 + j, (*chip, c), sibling)
                fwd.start()
                passed.append(fwd)
        for a in range(n):
            copy(a, 0, sibling, me).wait_recv()
        for j, chip in enumerate(chips):
            for a in range(n):
                copy(a, 4 + j, (*chip, 1 - c), me).wait_recv()
        for cp in first + passed:
            cp.wait_send()
        for cp in mine:
            cp.wait()

    hbm = pl.BlockSpec(memory_space=pl.ANY)
    return pl.pallas_call(
        body, name="all_gather_weights",
        out_shape=tuple(jax.ShapeDtypeStruct((N_DEV * s.shape[0], s.shape[1]), s.dtype) for s in shards),
        in_specs=[hbm] * n, out_specs=tuple([hbm] * n),
        scratch_shapes=[pltpu.SemaphoreType.DMA((7 * n,)), pltpu.SemaphoreType.DMA((7 * n,)),
                        pltpu.SemaphoreType.DMA((n,))],
        compiler_params=pltpu.CompilerParams(has_side_effects=True),
    )(*shards)


def _exchange_grads(grads):
    n = len(grads)
    rows = [g.shape[0] // N_DEV for g in grads]
    offs = [sum(rows[:a]) for a in range(n)]
    total = sum(rows)
    width, dtype = grads[0].shape[1], grads[0].dtype

    def body(*refs):
        ins, out = refs[:n], refs[n]
        send_sems, recv_sems, local_sems = refs[n + 1:]
        me = _position()

        def src(a, pos):
            return ins[a].at[pl.ds(_linear(pos) * rows[a], rows[a]), :]

        def dst(a, pos):
            return out.at[_linear(pos), pl.ds(offs[a], rows[a]), :]

        def copy(a, k):
            peer = _peer(me, k)
            return pltpu.make_async_remote_copy(
                src_ref=src(a, peer), dst_ref=dst(a, me),
                send_sem=send_sems.at[7 * a + k - 1], recv_sem=recv_sems.at[7 * a + k - 1],
                device_id=peer, device_id_type=MESH)

        def arrival(a, k):
            return pltpu.make_async_remote_copy(
                src_ref=src(a, me), dst_ref=dst(a, _peer(me, k)),
                send_sem=send_sems.at[7 * a + k - 1], recv_sem=recv_sems.at[7 * a + k - 1],
                device_id=_peer(me, k), device_id_type=MESH)

        mine = [pltpu.make_async_copy(src(a, me), dst(a, me), local_sems.at[a]) for a in range(n)]
        for cp in mine:
            cp.start()
        sent = [copy(a, k) for k in (1, 2, 4, 3, 5, 6, 7) for a in range(n)]
        for cp in sent:
            cp.start()
        for k in range(1, 8):
            for a in range(n):
                arrival(a, k).wait_recv()
        for cp in sent:
            cp.wait_send()
        for cp in mine:
            cp.wait()

    hbm = pl.BlockSpec(memory_space=pl.ANY)
    slots = pl.pallas_call(
        body, name="exchange_grads", out_shape=jax.ShapeDtypeStruct((N_DEV, total, width), dtype),
        in_specs=[hbm] * n, out_specs=hbm,
        scratch_shapes=[pltpu.SemaphoreType.DMA((7 * n,)), pltpu.SemaphoreType.DMA((7 * n,)),
                        pltpu.SemaphoreType.DMA((n,))],
        compiler_params=pltpu.CompilerParams(has_side_effects=True),
    )(*grads)
    return slots, offs, rows


def _sum_slots(slots):
    _, r, d = slots.shape
    tb = 256

    def body(s_ref, o_ref):
        acc = s_ref[0].astype(F32)
        for s in range(1, N_DEV):
            acc = acc + s_ref[s].astype(F32)
        o_ref[...] = acc

    return pl.pallas_call(
        body, name="sum_slots", out_shape=jax.ShapeDtypeStruct((r, d), F32), grid=(r // tb,),
        in_specs=[pl.BlockSpec((N_DEV, tb, d), lambda i: (0, i, 0))],
        out_specs=pl.BlockSpec((tb, d), lambda i: (i, 0)), compiler_params=_params(),
    )(slots)


def _adamw_math(w, g, m, v):
    m = B1 * m + (1.0 - B1) * g
    v = B2 * v + (1.0 - B2) * (g * g)
    delta = -LR * ((m / C1) / (jnp.sqrt(v / C2) + AEPS) + WD * w)
    return delta, m, v


def _adamw(g, w, m, v, *, name):
    def body(g_ref, w_ref, m_ref, v_ref, d_ref, nm_ref, nv_ref):
        d_ref[...], nm_ref[...], nv_ref[...] = _adamw_math(w_ref[...], g_ref[...], m_ref[...], v_ref[...])

    return pl.pallas_call(body, name=name, out_shape=(jax.ShapeDtypeStruct(w.shape, F32),) * 3,
                          compiler_params=_params())(g, w, m, v)


SMALL = ("g_mix_pre", "g_mix_post", "g_mem", "g_x_pre", "g_x_post", "g_ffn_pre", "g_ffn_post",
         "hgrn_onorm", "hgrn_lb", "sinks")
SMALL_W = dict(hgrn_onorm=HD, hgrn_lb=HG_W, sinks=8)
SQ_ROW = len(SMALL)
PACK_ROWS = 16


def _small_step(parts, sm, m_sm, v_sm):
    ns = len(SMALL)

    def body(*refs):
        part = refs[:ns + 1]
        w_refs = refs[ns + 1:2 * ns + 1]
        m_refs = refs[2 * ns + 1:3 * ns + 1]
        v_refs = refs[3 * ns + 1:4 * ns + 1]
        outs = refs[4 * ns + 1:8 * ns + 2]
        loss_ref = outs[0]
        g_out, d_out = outs[1:ns + 1], outs[ns + 1:2 * ns + 1]
        nm_out, nv_out = outs[2 * ns + 1:3 * ns + 1], outs[3 * ns + 1:4 * ns + 1]
        gath, send_sems, recv_sems = refs[8 * ns + 2:]
        me = _position()
        mine = gath.at[_linear(me)]
        mine[...] = jnp.zeros((PACK_ROWS, D), F32)
        for r, name in enumerate(SMALL):
            wd = SMALL_W.get(name, D)
            mine[r:r + 1, 0:wd] = jnp.sum(part[r][...], axis=0, keepdims=True)[:, 0:wd]
        sq = jnp.sum(part[ns][...]) * (0.5 / D)
        mine[SQ_ROW:SQ_ROW + 1, :] = jnp.full((1, D), sq, F32)

        def copy(k):
            peer = _peer(me, k)
            return pltpu.make_async_remote_copy(
                src_ref=mine, dst_ref=mine, send_sem=send_sems.at[k - 1], recv_sem=recv_sems.at[k - 1],
                device_id=peer, device_id_type=MESH)

        def arrival(k):
            slot = gath.at[_linear(_peer(me, k))]
            return pltpu.make_async_remote_copy(
                src_ref=slot, dst_ref=slot, send_sem=send_sems.at[k - 1], recv_sem=recv_sems.at[k - 1],
                device_id=_peer(me, k), device_id_type=MESH)

        sent = [copy(k) for k in range(1, 8)]
        for cp in sent:
            cp.start()
        for k in range(1, 8):
            arrival(k).wait_recv()
        for cp in sent:
            cp.wait_send()
        tot = gath[0]
        for s in range(1, N_DEV):
            tot = tot + gath[s]
        loss_ref[...] = tot[SQ_ROW:SQ_ROW + 1, 0:1]
        for r, name in enumerate(SMALL):
            wd = SMALL_W.get(name, D)
            g = tot[r:r + 1, 0:wd]
            w = w_refs[r][...]
            if name == "hgrn_lb":
                mx = jnp.maximum(w[0:1], w[1:2])
                e0, e1 = jnp.exp(w[0:1] - mx), jnp.exp(w[1:2] - mx)
                lb0 = e0 / (e0 + e1)
                g0 = g * lb0 * (1.0 - lb0)
                for i, gi in enumerate((g0, -g0)):
                    d, nm, nv = _adamw_math(w[i:i + 1], gi, m_refs[r][i:i + 1, :], v_refs[r][i:i + 1, :])
                    g_out[r][i:i + 1, :] = gi
                    d_out[r][i:i + 1, :], nm_out[r][i:i + 1, :], nv_out[r][i:i + 1, :] = d, nm, nv
            else:
                d, nm, nv = _adamw_math(w, g, m_refs[r][...], v_refs[r][...])
                g_out[r][...] = g
                d_out[r][...], nm_out[r][...], nv_out[r][...] = d, nm, nv

    ins = [parts[n] for n in SMALL] + [parts["sq"]] + [sm[n] for n in SMALL] + [m_sm[n] for n in SMALL] \
        + [v_sm[n] for n in SMALL]
    shapes = [jax.ShapeDtypeStruct(sm[n].shape, F32) for n in SMALL]
    res = pl.pallas_call(
        body, name="small_step",
        out_shape=tuple([jax.ShapeDtypeStruct((1, 1), F32)] + shapes * 4),
        scratch_shapes=[pltpu.VMEM((N_DEV, PACK_ROWS, D), F32), pltpu.SemaphoreType.DMA((7,)),
                        pltpu.SemaphoreType.DMA((7,))],
        compiler_params=_params(has_side_effects=True),
    )(*ins)
    loss = res[0]
    groups = [dict(zip(SMALL, res[1 + i * ns:1 + (i + 1) * ns])) for i in range(4)]
    return loss, groups[0], groups[1], groups[2], groups[3]


BIG = ("w_in", "w_gate", "w_up", "w_down", "w_out", "wq_x", "wk_x", "wv_x", "wo_x")
BIG_KEY = dict(w_in="winT", w_gate="wgT", w_up="wuT", w_down="wd", w_out="wout", wq_x="wq", wk_x="wk",
               wv_x="wv", wo_x="wo")
TRANSPOSED = ("w_in", "w_gate", "w_up")
WEIGHTS = ("w_in", "sinks", "hgrn_lb", "hgrn_onorm", "w_out", "g_mix_pre", "g_mix_post", "g_mem", "g_x_pre",
           "g_x_post", "wq_x", "wk_x", "wv_x", "wo_x", "g_ffn_pre", "g_ffn_post", "w_gate", "w_up", "w_down")


def kernel(x, mem, w_in, sinks, hgrn_lb, hgrn_onorm, w_out, g_mix_pre, g_mix_post, g_mem, g_x_pre, g_x_post, wq_x, wk_x, wv_x, wo_x, g_ffn_pre, g_ffn_post, w_gate, w_up, w_down, loss_target, m_w_in, m_sinks, m_hgrn_lb, m_hgrn_onorm, m_w_out, m_g_mix_pre, m_g_mix_post, m_g_mem, m_g_x_pre, m_g_x_post, m_wq_x, m_wk_x, m_wv_x, m_wo_x, m_g_ffn_pre, m_g_ffn_post, m_w_gate, m_w_up, m_w_down, v_w_in, v_sinks, v_hgrn_lb, v_hgrn_onorm, v_w_out, v_g_mix_pre, v_g_mix_post, v_g_mem, v_g_x_pre, v_g_x_post, v_wq_x, v_wk_x, v_wv_x, v_wo_x, v_g_ffn_pre, v_g_ffn_post, v_w_gate, v_w_up, v_w_down):
    given = dict(locals())
    wts = {n: given[n] for n in WEIGHTS}
    ms = {n: given["m_" + n] for n in WEIGHTS}
    vs = {n: given["v_" + n] for n in WEIGHTS}

    def mat(a, name):
        a = a[0]
        return a.T if name in TRANSPOSED else a

    shards = [mat(wts[n], n).astype(BF16) for n in BIG]
    full = _all_gather(shards)
    w = {BIG_KEY[n]: f for n, f in zip(BIG, full)}
    sm = {n: wts[n] for n in SMALL}
    grad_x, gw, parts = _local_step(x[0], mem[0], loss_target[0], w, sm)
    slots, offs, rows = _exchange_grads([gw[BIG_KEY[n]] for n in BIG])
    gsum = _sum_slots(slots)
    grads, deltas, new_m, new_v = {}, {}, {}, {}
    for a, n in enumerate(BIG):
        g = gsum[offs[a]:offs[a] + rows[a]]
        g = g.T if n in TRANSPOSED else g
        d, nm, nv = _adamw(g, wts[n][0], ms[n][0], vs[n][0], name="adamw_" + n)
        grads[n], deltas[n], new_m[n], new_v[n] = g[None], d[None], nm[None], nv[None]
    loss, g_s, d_s, m_s, v_s = _small_step(parts, sm, {n: ms[n] for n in SMALL}, {n: vs[n] for n in SMALL})
    grads.update(g_s), deltas.update(d_s), new_m.update(m_s), new_v.update(v_s)
    return (loss[0, 0], grad_x[None], *[grads[n] for n in WEIGHTS], *[deltas[n] for n in WEIGHTS],
            *[new_m[n] for n in WEIGHTS], *[new_v[n] for n in WEIGHTS])
```

```python
import functools

import jax
import jax.numpy as jnp
from jax import lax
from jax.experimental import pallas as pl
from jax.experimental.pallas import tpu as pltpu

F32 = jnp.float32
BF16 = jnp.bfloat16

D = 1024
D_IN = 2816
D_FF = 2816
CHUNK = 64
SWA_W = 512
KV_W = 128
HG_W = 512
HD = 128
ZQH, ZFH, ZIH, ZGH = 768, 1280, 1792, 2304
XH, XD = 4, 256
EPS = 1e-6
NEG = -1e30
N_DEV = 8
MESH = pl.DeviceIdType.MESH

LR, B1, B2, AEPS, WD, STEP = 0.001, 0.9, 0.999, 1e-08, 0.01, 10
C1 = 1.0 - B1 ** STEP
C2 = 1.0 - B2 ** STEP

VMEM_LIMIT = 56 * 1024 * 1024


def _params(**kw):
    return pltpu.CompilerParams(vmem_limit_bytes=VMEM_LIMIT, **kw)


def _sig(x):
    return 1.0 / (1.0 + jnp.exp(-x))


def _rowsum8(x):
    r, w = x.shape
    return jnp.sum(x.reshape(r // 8, 8, w), axis=0)


def _dot(a, b, ca, cb, precision=None):
    return lax.dot_general(a, b, (((ca,), (cb,)), ((), ())), preferred_element_type=F32,
                           precision=precision)


ANY_SPEC = pl.BlockSpec(memory_space=pl.ANY)


def _mm(a, b, *, ta=False, tb=False, out_dtype, tm, tn, tk=None, name, dep=None):
    m = a.shape[1] if ta else a.shape[0]
    k = a.shape[0] if ta else a.shape[1]
    n = b.shape[0] if tb else b.shape[1]
    tm, tn = min(tm, m), min(tn, n)
    tk = k if tk is None else min(tk, k)
    nk = k // tk
    assert m % tm == 0 and n % tn == 0 and k % tk == 0, (name, m, n, k, tm, tn, tk)
    a_spec = pl.BlockSpec((tk, tm), lambda i, j, kk: (kk, i)) if ta else pl.BlockSpec((tm, tk), lambda i, j, kk: (i, kk))
    b_spec = pl.BlockSpec((tn, tk), lambda i, j, kk: (j, kk)) if tb else pl.BlockSpec((tk, tn), lambda i, j, kk: (kk, j))
    ca, cb = (0 if ta else 1), (1 if tb else 0)

    deps = [] if dep is None else [dep]

    def body(a_ref, b_ref, *rest):
        o_ref, acc = rest[len(deps)], rest[len(deps) + 1:]
        p = _dot(a_ref[...].astype(BF16), b_ref[...].astype(BF16), ca, cb)
        if nk == 1:
            o_ref[...] = p.astype(out_dtype)
        else:
            acc_ref, = acc
            kk = pl.program_id(2)

            @pl.when(kk == 0)
            def _():
                acc_ref[...] = p

            @pl.when(kk > 0)
            def _():
                acc_ref[...] += p

            @pl.when(kk == nk - 1)
            def _():
                o_ref[...] = acc_ref[...].astype(out_dtype)

    return pl.pallas_call(
        body, name=name, out_shape=jax.ShapeDtypeStruct((m, n), out_dtype),
        grid=(m // tm, n // tn, nk), in_specs=[a_spec, b_spec] + [ANY_SPEC] * len(deps),
        out_specs=pl.BlockSpec((tm, tn), lambda i, j, kk: (i, j)),
        scratch_shapes=[pltpu.VMEM((tm, tn), F32)] if nk > 1 else [],
        compiler_params=_params(dimension_semantics=("parallel", "parallel", "arbitrary")),
    )(a, b, *deps)


def _mm2(a1, b1, a2, b2, *, tb=False, tm, tk, name, dep=None):
    m, k = a1.shape
    n = b1.shape[0] if tb else b1.shape[1]
    tm, tk = min(tm, m), min(tk, k)
    nk = k // tk
    assert m % tm == 0 and k % tk == 0
    cb = 1 if tb else 0
    deps = [] if dep is None else [dep]

    def body(a1_ref, b1_ref, a2_ref, b2_ref, *rest):
        o_ref = rest[len(deps)]
        p = (_dot(a1_ref[...].astype(BF16), b1_ref[...], 1, cb)
             + _dot(a2_ref[...].astype(BF16), b2_ref[...], 1, cb))
        kk = pl.program_id(1)

        @pl.when(kk == 0)
        def _():
            o_ref[...] = p

        @pl.when(kk > 0)
        def _():
            o_ref[...] += p

    a_spec = pl.BlockSpec((tm, tk), lambda i, kk: (i, kk))
    b_spec = pl.BlockSpec((n, tk), lambda i, kk: (0, kk)) if tb else pl.BlockSpec((tk, n), lambda i, kk: (kk, 0))
    return pl.pallas_call(
        body, name=name, out_shape=jax.ShapeDtypeStruct((m, n), F32),
        grid=(m // tm, nk), in_specs=[a_spec, b_spec, a_spec, b_spec] + [ANY_SPEC] * len(deps),
        out_specs=pl.BlockSpec((tm, n), lambda i, kk: (i, 0)),
        compiler_params=_params(dimension_semantics=("parallel", "arbitrary")),
    )(a1, b1, a2, b2, *deps)


def _rstd(x):
    return lax.rsqrt(jnp.mean(x * x, axis=-1, keepdims=True) + EPS)


def _norm_bwd(xh, r, t):
    return r * (t - xh * jnp.mean(xh * t, axis=-1, keepdims=True))


def _prenorm(x, g, *, name):
    t, d = x.shape
    tb = min(512, t)

    def body(x_ref, g_ref, o_ref):
        xf = x_ref[...]
        o_ref[...] = (xf * _rstd(xf) * g_ref[...]).astype(BF16)

    return pl.pallas_call(
        body, name=name, out_shape=jax.ShapeDtypeStruct((t, d), BF16), grid=(t // tb,),
        in_specs=[pl.BlockSpec((tb, d), lambda i: (i, 0)), pl.BlockSpec((1, d), lambda i: (0, 0))],
        out_specs=pl.BlockSpec((tb, d), lambda i: (i, 0)), compiler_params=_params(),
    )(x, g)


def _post_pre(h, y, g_post, g_pre, *, name):
    t, d = h.shape
    tb = min(512, t)

    def body(h_ref, y_ref, gp_ref, gn_ref, hn_ref, u_ref):
        y_ = y_ref[...]
        hn = h_ref[...] + y_ * _rstd(y_) * gp_ref[...]
        hn_ref[...] = hn
        u_ref[...] = (hn * _rstd(hn) * gn_ref[...]).astype(BF16)

    row = pl.BlockSpec((tb, d), lambda i: (i, 0))
    vec = pl.BlockSpec((1, d), lambda i: (0, 0))
    return pl.pallas_call(
        body, name=name, out_shape=(jax.ShapeDtypeStruct((t, d), F32), jax.ShapeDtypeStruct((t, d), BF16)),
        grid=(t // tb,), in_specs=[row, row, vec, vec], out_specs=(row, row), compiler_params=_params(),
    )(h, y, g_post, g_pre)


def _final_loss(h, y, g_post, target, *, name):
    t, d = h.shape
    tb = min(512, t)

    def body(h_ref, y_ref, g_ref, t_ref, sq_ref, dh_ref, dy_ref, dg_ref):
        @pl.when(pl.program_id(0) == 0)
        def _():
            sq_ref[...] = jnp.zeros_like(sq_ref)
            dg_ref[...] = jnp.zeros_like(dg_ref)

        y_ = y_ref[...]
        r = _rstd(y_)
        yh = y_ * r
        g = g_ref[...]
        err = h_ref[...] + yh * g - t_ref[...]
        sq_ref[...] += _rowsum8(err * err)
        dh = err * (1.0 / d)
        dh_ref[...] = dh
        dg_ref[...] += _rowsum8(dh * yh)
        dy_ref[...] = _norm_bwd(yh, r, dh * g).astype(BF16)

    row = pl.BlockSpec((tb, d), lambda i: (i, 0))
    vec = pl.BlockSpec((1, d), lambda i: (0, 0))
    acc = pl.BlockSpec((8, d), lambda i: (0, 0))
    return pl.pallas_call(
        body, name=name,
        out_shape=(jax.ShapeDtypeStruct((8, d), F32), jax.ShapeDtypeStruct((t, d), F32),
                   jax.ShapeDtypeStruct((t, d), BF16), jax.ShapeDtypeStruct((8, d), F32)),
        grid=(t // tb,), in_specs=[row, row, vec, row], out_specs=(acc, row, row, acc),
        compiler_params=_params(dimension_semantics=("arbitrary",)),
    )(h, y, g_post, target)


def _post_pre_bwd(dh_out, du, hn, y, g_post, g_pre, *, name):
    t, d = hn.shape
    tb = min(512, t)

    def body(dho_ref, du_ref, hn_ref, y_ref, gp_ref, gn_ref, dh_ref, dy_ref, dgn_ref, dgp_ref):
        @pl.when(pl.program_id(0) == 0)
        def _():
            dgn_ref[...] = jnp.zeros_like(dgn_ref)
            dgp_ref[...] = jnp.zeros_like(dgp_ref)

        hn_ = hn_ref[...]
        r2 = _rstd(hn_)
        xh = hn_ * r2
        du_ = du_ref[...]
        dgn_ref[...] += _rowsum8(du_ * xh)
        dh = dho_ref[...] + _norm_bwd(xh, r2, du_ * gn_ref[...])
        dh_ref[...] = dh
        y_ = y_ref[...]
        r1 = _rstd(y_)
        yh = y_ * r1
        dgp_ref[...] += _rowsum8(dh * yh)
        dy_ref[...] = _norm_bwd(yh, r1, dh * gp_ref[...]).astype(BF16)

    row = pl.BlockSpec((tb, d), lambda i: (i, 0))
    vec = pl.BlockSpec((1, d), lambda i: (0, 0))
    acc = pl.BlockSpec((8, d), lambda i: (0, 0))
    return pl.pallas_call(
        body, name=name,
        out_shape=(jax.ShapeDtypeStruct((t, d), F32), jax.ShapeDtypeStruct((t, d), BF16),
                   jax.ShapeDtypeStruct((8, d), F32), jax.ShapeDtypeStruct((8, d), F32)),
        grid=(t // tb,), in_specs=[row, row, row, row, vec, vec], out_specs=(row, row, acc, acc),
        compiler_params=_params(dimension_semantics=("arbitrary",)),
    )(dh_out, du, hn, y, g_post, g_pre)


def _pre_bwd(dh_out, du, x, g, *, name):
    t, d = x.shape
    tb = min(512, t)
    has_res = dh_out is not None

    def body(*refs):
        if has_res:
            dho_ref, du_ref, x_ref, g_ref, dx_ref, dg_ref = refs
        else:
            du_ref, x_ref, g_ref, dx_ref, dg_ref = refs

        @pl.when(pl.program_id(0) == 0)
        def _():
            dg_ref[...] = jnp.zeros_like(dg_ref)

        x_ = x_ref[...]
        r = _rstd(x_)
        xh = x_ * r
        du_ = du_ref[...]
        dg_ref[...] += _rowsum8(du_ * xh)
        dx = _norm_bwd(xh, r, du_ * g_ref[...])
        if has_res:
            dx = dx + dho_ref[...]
        dx_ref[...] = dx

    row = pl.BlockSpec((tb, d), lambda i: (i, 0))
    vec = pl.BlockSpec((1, d), lambda i: (0, 0))
    acc = pl.BlockSpec((8, d), lambda i: (0, 0))
    ins = ([dh_out] if has_res else []) + [du, x, g]
    return pl.pallas_call(
        body, name=name,
        out_shape=(jax.ShapeDtypeStruct((t, d), F32), jax.ShapeDtypeStruct((8, d), F32)),
        grid=(t // tb,), in_specs=[row] * (len(ins) - 1) + [vec], out_specs=(row, acc),
        compiler_params=_params(dimension_semantics=("arbitrary",)),
    )(*ins)


QB = 256


def _half_mask(shape, e):
    lane = lax.broadcasted_iota(jnp.int32, shape, len(shape) - 1)
    return (lane // 64) == e


def _place(kv):
    sw = pltpu.roll(kv, 64, 1)
    m0 = _half_mask(kv.shape, 0)
    return [[jnp.where(m0, kv, 0.0).astype(BF16), jnp.where(m0, 0.0, sw).astype(BF16)],
            [jnp.where(m0, sw, 0.0).astype(BF16), jnp.where(m0, 0.0, kv).astype(BF16)]]


def _swa_valid_q(i, nq, nk):
    qc = lax.broadcasted_iota(jnp.int32, (nq, nk), 0) // CHUNK
    kc = lax.broadcasted_iota(jnp.int32, (nq, nk), 1) // CHUNK - 2
    return (kc <= qc) & (qc <= kc + 2) & (4 * i + kc >= 0)


def _swa_fwd(z, sinks, t):
    nb = t // QB

    def body(s_ref, q_ref, kp_ref, kc_ref, vp_ref, vc_ref, o_ref, lse_ref):
        i = pl.program_id(0)
        kpl = _place(jnp.concatenate([kp_ref[...], kc_ref[...]], axis=0))
        vpl = _place(jnp.concatenate([vp_ref[...], vc_ref[...]], axis=0))
        valid = _swa_valid_q(i, QB, QB + 128)
        lane = lax.broadcasted_iota(jnp.int32, (QB, 128), 1)
        lse_out = jnp.zeros((QB, 128), F32)
        for j in range(4):
            qp = q_ref[:, 128 * j:128 * (j + 1)].astype(BF16)
            acc = jnp.zeros((QB, 128), F32)
            for e in range(2):
                h = 2 * j + e
                kvh = h // 4
                qm = jnp.where(_half_mask(qp.shape, e), qp, jnp.zeros_like(qp))
                s = _dot(qm, kpl[kvh][e], 1, 1) * 0.125
                s = jnp.where(valid, s, NEG)
                sink = s_ref[0, h]
                m = jnp.maximum(jnp.max(s, axis=-1, keepdims=True), sink)
                p = jnp.exp(s - m)
                l = jnp.sum(p, axis=-1, keepdims=True) + jnp.exp(sink - m)
                acc = acc + _dot(p.astype(BF16), vpl[kvh][e], 1, 0) * (1.0 / l)
                lse_out = jnp.where(lane == h, m + jnp.log(l), lse_out)
            o_ref[:, 128 * j:128 * (j + 1)] = acc.astype(BF16)
        lse_ref[...] = lse_out

    prev = lambda c: pl.BlockSpec((128, 128), lambda i: (jnp.maximum(2 * i - 1, 0), c))
    cur = lambda c: pl.BlockSpec((QB, 128), lambda i: (i, c))
    return pl.pallas_call(
        body, name="swa_fwd",
        out_shape=(jax.ShapeDtypeStruct((t, D), BF16), jax.ShapeDtypeStruct((t, 128), F32)),
        grid=(nb,),
        in_specs=[pl.BlockSpec(memory_space=pltpu.SMEM),
                  pl.BlockSpec((QB, SWA_W), lambda i: (i, 0)), prev(4), cur(4), prev(5), cur(5)],
        out_specs=(pl.BlockSpec((QB, SWA_W), lambda i: (i, 0)), pl.BlockSpec((QB, 128), lambda i: (i, 0))),
        compiler_params=_params(),
    )(sinks, z, z, z, z, z)


def _swa_bwd(z, sinks, ymix, lse, dymix, t):
    nb = t // QB
    nq2 = QB + 128

    def body(s_ref, qc_ref, qn_ref, kp_ref, kc_ref, vp_ref, vc_ref, oc_ref, on_ref, doc_ref, don_ref,
             lc_ref, ln_ref, dz_ref, ds_ref):
        i = pl.program_id(0)

        @pl.when(i == 0)
        def _():
            ds_ref[...] = jnp.zeros_like(ds_ref)

        lane = lax.broadcasted_iota(jnp.int32, (8, 128), 1)
        kpl = _place(jnp.concatenate([kp_ref[...], kc_ref[...]], axis=0))
        vpl = _place(jnp.concatenate([vp_ref[...], vc_ref[...]], axis=0))
        valid = _swa_valid_q(i, QB, nq2)
        lse_c = lc_ref[...]
        dsink = jnp.zeros((8, 128), F32)
        for j in range(4):
            cols = slice(128 * j, 128 * (j + 1))
            qp = qc_ref[:, cols].astype(BF16)
            dop = doc_ref[:, cols]
            prod = dop.astype(F32) * oc_ref[:, cols].astype(F32)
            acc = jnp.zeros((QB, 128), F32)
            for e in range(2):
                h = 2 * j + e
                kvh = h // 4
                hm = _half_mask(qp.shape, e)
                qm = jnp.where(hm, qp, jnp.zeros_like(qp))
                dom = jnp.where(hm, dop, jnp.zeros_like(dop))
                dd = jnp.sum(jnp.where(hm, prod, 0.0), axis=-1, keepdims=True)
                lse_h = lse_c[:, h:h + 1]
                s = _dot(qm, kpl[kvh][e], 1, 1) * 0.125
                p = jnp.where(valid, jnp.exp(s - lse_h), 0.0)
                dp = _dot(dom, vpl[kvh][e], 1, 1)
                ds = p * (dp - dd) * 0.125
                acc = acc + _dot(ds.astype(BF16), kpl[kvh][e], 1, 0)
                ps = jnp.exp(s_ref[0, h] - lse_h) * dd
                dsink = dsink - jnp.where(lane == h, _rowsum8(jnp.broadcast_to(ps, (QB, 128))), 0.0)
            dz_ref[:, cols] = acc.astype(BF16)
        ds_ref[...] += dsink
        kpl, vpl = _place(kc_ref[...]), _place(vc_ref[...])
        qr = lax.broadcasted_iota(jnp.int32, (nq2, QB), 0) // CHUNK
        kr = lax.broadcasted_iota(jnp.int32, (nq2, QB), 1) // CHUNK
        valid2 = (kr <= qr) & (qr <= kr + 2) & (4 * i + qr < t // CHUNK)
        lse_a = jnp.concatenate([lse_c, ln_ref[...]], axis=0)
        dk_acc = [[jnp.zeros((QB, 128), F32) for _ in range(2)] for _ in range(2)]
        dv_acc = [[jnp.zeros((QB, 128), F32) for _ in range(2)] for _ in range(2)]
        for j in range(4):
            cols = slice(128 * j, 128 * (j + 1))
            qp = jnp.concatenate([qc_ref[:, cols], qn_ref[:, cols]], axis=0).astype(BF16)
            dop = jnp.concatenate([doc_ref[:, cols], don_ref[:, cols]], axis=0)
            op = jnp.concatenate([oc_ref[:, cols], on_ref[:, cols]], axis=0)
            prod = dop.astype(F32) * op.astype(F32)
            for e in range(2):
                h = 2 * j + e
                kvh = h // 4
                hm = _half_mask(qp.shape, e)
                qm = jnp.where(hm, qp, jnp.zeros_like(qp))
                dom = jnp.where(hm, dop, jnp.zeros_like(dop))
                dd = jnp.sum(jnp.where(hm, prod, 0.0), axis=-1, keepdims=True)
                s = _dot(qm, kpl[kvh][e], 1, 1) * 0.125
                p = jnp.where(valid2, jnp.exp(s - lse_a[:, h:h + 1]), 0.0)
                dv_acc[kvh][e] = dv_acc[kvh][e] + _dot(p.astype(BF16), dom, 0, 0)
                dp = _dot(dom, vpl[kvh][e], 1, 1)
                ds = p * (dp - dd) * 0.125
                dk_acc[kvh][e] = dk_acc[kvh][e] + _dot(ds.astype(BF16), qm, 0, 0)
        dk = dk_acc[0][0] + dk_acc[1][1] + pltpu.roll(dk_acc[0][1] + dk_acc[1][0], 64, 1)
        dv = dv_acc[0][0] + dv_acc[1][1] + pltpu.roll(dv_acc[0][1] + dv_acc[1][0], 64, 1)
        dz_ref[:, 512:640] = dk.astype(BF16)
        dz_ref[:, 640:768] = dv.astype(BF16)

    last = 2 * nb - 1
    prev = lambda c: pl.BlockSpec((128, 128), lambda i: (jnp.maximum(2 * i - 1, 0), c))
    cur = lambda w, c: pl.BlockSpec((QB, w), lambda i: (i, c))
    nxt = lambda w: pl.BlockSpec((128, w), lambda i: (jnp.minimum(2 * i + 2, last), 0))
    return pl.pallas_call(
        body, name="swa_bwd",
        out_shape=(jax.ShapeDtypeStruct((t, 768), BF16), jax.ShapeDtypeStruct((8, 128), F32)),
        grid=(nb,),
        in_specs=[pl.BlockSpec(memory_space=pltpu.SMEM),
                  cur(SWA_W, 0), nxt(SWA_W), prev(4), cur(128, 4), prev(5), cur(128, 5),
                  cur(SWA_W, 0), nxt(SWA_W), cur(SWA_W, 0), nxt(SWA_W), cur(128, 0), nxt(128)],
        out_specs=(pl.BlockSpec((QB, 768), lambda i: (i, 0)), pl.BlockSpec((8, 128), lambda i: (0, 0))),
        compiler_params=_params(dimension_semantics=("arbitrary",)),
    )(sinks, z, z, z, z, z, z, ymix, ymix, dymix, dymix, lse, lse)


HB = 256
HI = lax.Precision.HIGHEST


def _lower_bound(lb_ref):
    a = lb_ref[...]
    a0, a1 = a[0:1], a[1:2]
    mx = jnp.maximum(a0, a1)
    e0, e1 = jnp.exp(a0 - mx), jnp.exp(a1 - mx)
    return e0 / (e0 + e1)


def _tri(lower):
    r = lax.broadcasted_iota(jnp.int32, (CHUNK, CHUNK), 0)
    c = lax.broadcasted_iota(jnp.int32, (CHUNK, CHUNK), 1)
    return (c <= r) if lower else (c >= r)


def _hgrn_chunk(q, fl, lb):
    sq = _sig(q)
    qf = q * sq * (HD ** -0.5)
    sg = _sig(fl)
    f = lb + (1.0 - lb) * sg
    kf = 1.0 - f
    b = _dot(_tri(True).astype(F32), jnp.log(f), 1, 0, precision=HI)
    b_mid = b[CHUNK // 2 - 1:CHUNK // 2]
    b_last = b[CHUNK - 1:CHUNK]
    qm = qf * jnp.exp(b - b_mid)
    km = kf * jnp.exp(b_mid - b)
    kl = kf * jnp.exp(b_last - b)
    qb = qf * jnp.exp(b)
    return dict(sq=sq, qf=qf, sg=sg, f=f, kf=kf, b=b, b_mid=b_mid, b_last=b_last, qm=qm, km=km, kl=kl, qb=qb)


def _hgrn_fwd(z, hgrn_lb, onorm, ymix, t):
    nb = t // HB
    nc = HB // CHUNK

    def body(zq_ref, zf_ref, zi_ref, zg_ref, lb_ref, on_ref, ymix_in, y_ref, o_ref, sp_ref, st_ref):
        del ymix_in

        @pl.when(pl.program_id(1) == 0)
        def _():
            st_ref[...] = jnp.zeros_like(st_ref)

        lb = _lower_bound(lb_ref)
        gn = on_ref[...]

        def chunk(c, carry):
            rows = pl.ds(pl.multiple_of(c * CHUNK, CHUNK), CHUNK)
            w = _hgrn_chunk(zq_ref[rows, :], zf_ref[rows, :], lb)
            iv = zi_ref[rows, :].astype(BF16)
            st = st_ref[...]
            sp_ref[0, c] = st
            a = jnp.where(_tri(True), _dot(w["qm"].astype(BF16), w["km"].astype(BF16), 1, 1), 0.0)
            o = _dot(a.astype(BF16), iv, 1, 0) + _dot(w["qb"].astype(BF16), st.astype(BF16), 1, 1)
            st_ref[...] = st * jnp.exp(w["b_last"]) + _dot(iv, w["kl"].astype(BF16), 0, 0)
            o_ref[rows, :] = o
            gg = zg_ref[rows, :]
            y_ref[rows, :] = (o * _rstd(o) * gn * (gg * _sig(gg))).astype(BF16)
            return carry

        lax.fori_loop(0, nc, chunk, 0)

    col = lambda base: pl.BlockSpec((HB, HD), lambda h, j: (j, base // HD + h))
    return pl.pallas_call(
        body, name="hgrn_fwd",
        out_shape=(jax.ShapeDtypeStruct((t, D), BF16), jax.ShapeDtypeStruct((t, HG_W), F32),
                   jax.ShapeDtypeStruct((4, t // CHUNK, HD, HD), F32)),
        grid=(4, nb),
        in_specs=[col(ZQH), col(ZFH), col(ZIH), col(ZGH),
                  pl.BlockSpec((2, HD), lambda h, j: (0, h)), pl.BlockSpec((1, HD), lambda h, j: (0, 0)),
                  pl.BlockSpec(memory_space=pl.ANY)],
        out_specs=(pl.BlockSpec((HB, HD), lambda h, j: (j, SWA_W // HD + h)),
                   pl.BlockSpec((HB, HD), lambda h, j: (j, h)),
                   pl.BlockSpec((1, nc, HD, HD), lambda h, j: (h, j, 0, 0))),
        scratch_shapes=[pltpu.VMEM((HD, HD), F32)],
        input_output_aliases={6: 0},
        compiler_params=_params(dimension_semantics=("arbitrary", "arbitrary")),
    )(z, z, z, z, hgrn_lb, onorm, ymix)


def _hgrn_bwd(z, hgrn_lb, onorm, o_save, sprev, dymix, t):
    nb = t // HB
    nc = HB // CHUNK

    def body(zq_ref, zf_ref, zi_ref, zg_ref, lb_ref, on_ref, o_ref, sp_ref, dy_ref,
             dq_ref, df_ref, di_ref, dg_ref, dlb_ref, don_ref, dst_ref):
        hh, jj = pl.program_id(0), pl.program_id(1)

        @pl.when(jj == 0)
        def _():
            dst_ref[...] = jnp.zeros_like(dst_ref)
            dlb_ref[...] = jnp.zeros_like(dlb_ref)

        @pl.when((jj == 0) & (hh == 0))
        def _():
            don_ref[...] = jnp.zeros_like(don_ref)

        lb = _lower_bound(lb_ref)
        gn = on_ref[...]
        row = lax.broadcasted_iota(jnp.int32, (CHUNK, HD), 0)

        def chunk(cc, carry):
            c = nc - 1 - cc
            rows = pl.ds(pl.multiple_of(c * CHUNK, CHUNK), CHUNK)
            q = zq_ref[rows, :]
            w = _hgrn_chunk(q, zf_ref[rows, :], lb)
            iv = zi_ref[rows, :].astype(BF16)
            gg = zg_ref[rows, :]
            o = o_ref[rows, :]
            st = sp_ref[0, c]
            dst = dst_ref[...]
            dout = dy_ref[rows, :].astype(F32)
            sgg = _sig(gg)
            r = _rstd(o)
            oh = o * r
            dyn = dout * (gg * sgg)
            dg_ref[rows, :] = (dout * oh * gn * (sgg * (1.0 + gg * (1.0 - sgg)))).astype(BF16)
            don_ref[...] += _rowsum8(dyn * oh)
            do = _norm_bwd(oh, r, dyn * gn).astype(BF16)
            qm, km, kl, qb = (w[n].astype(BF16) for n in ("qm", "km", "kl", "qb"))
            dstb = dst.astype(BF16)
            d_row = jnp.exp(w["b_last"])
            dqb = _dot(do, st.astype(BF16), 1, 0)
            dst_ref[...] = dst * d_row + _dot(do, qb, 0, 0)
            dd_row = jnp.sum(dst * st, axis=0, keepdims=True)
            at = jnp.where(_tri(False), _dot(km, qm, 1, 1), 0.0)
            di_ref[rows, :] = (_dot(at.astype(BF16), do, 1, 0) + _dot(kl, dstb, 1, 1)).astype(BF16)
            dkl = _dot(iv, dstb, 1, 0)
            da = jnp.where(_tri(True), _dot(do, iv, 1, 1), 0.0).astype(BF16)
            dat = jnp.where(_tri(False), _dot(iv, do, 1, 1), 0.0).astype(BF16)
            dqm = _dot(da, km, 1, 0)
            dkm = _dot(dat, qm, 1, 0)
            e1, e2 = jnp.exp(w["b"] - w["b_mid"]), jnp.exp(w["b_mid"] - w["b"])
            e3, e4 = jnp.exp(w["b_last"] - w["b"]), jnp.exp(w["b"])
            dqf = dqm * e1 + dqb * e4
            dkf = dkm * e2 + dkl * e3
            t_qm, t_km, t_kl = dqm * w["qm"], dkm * w["km"], dkl * w["kl"]
            db = t_qm - t_km - t_kl + dqb * w["qb"]
            db_mid = jnp.sum(t_km - t_qm, axis=0, keepdims=True)
            db_last = jnp.sum(t_kl, axis=0, keepdims=True) + dd_row * d_row
            db = db + jnp.where(row == CHUNK // 2 - 1, db_mid, 0.0) + jnp.where(row == CHUNK - 1, db_last, 0.0)
            dlogf = _dot(_tri(False).astype(F32), db, 1, 0, precision=HI)
            dfv = dlogf / w["f"] - dkf
            sg = w["sg"]
            df_ref[rows, :] = (dfv * (1.0 - lb) * sg * (1.0 - sg)).astype(BF16)
            dlb_ref[...] += _rowsum8(dfv * (1.0 - sg))
            sq = w["sq"]
            dq_ref[rows, :] = (dqf * (HD ** -0.5) * (sq * (1.0 + q * (1.0 - sq)))).astype(BF16)
            return carry

        lax.fori_loop(0, nc, chunk, 0)

    rev = lambda j: nb - 1 - j
    col = lambda base: pl.BlockSpec((HB, HD), lambda h, j: (rev(j), base // HD + h))
    out = pl.BlockSpec((HB, HD), lambda h, j: (rev(j), h))
    return pl.pallas_call(
        body, name="hgrn_bwd",
        out_shape=(jax.ShapeDtypeStruct((t, HG_W), BF16),) * 4
        + (jax.ShapeDtypeStruct((8, HG_W), F32), jax.ShapeDtypeStruct((8, HD), F32)),
        grid=(4, nb),
        in_specs=[col(ZQH), col(ZFH), col(ZIH), col(ZGH),
                  pl.BlockSpec((2, HD), lambda h, j: (0, h)), pl.BlockSpec((1, HD), lambda h, j: (0, 0)),
                  out, pl.BlockSpec((1, nc, HD, HD), lambda h, j: (h, rev(j), 0, 0)),
                  pl.BlockSpec((HB, HD), lambda h, j: (rev(j), SWA_W // HD + h))],
        out_specs=(out, out, out, out, pl.BlockSpec((8, HD), lambda h, j: (0, h)),
                   pl.BlockSpec((8, HD), lambda h, j: (0, 0))),
        scratch_shapes=[pltpu.VMEM((HD, HD), F32)],
        compiler_params=_params(dimension_semantics=("arbitrary", "arbitrary")),
    )(z, z, z, z, hgrn_lb, onorm, o_save, sprev, dymix)


def _assemble_dz(dza, dq, df, di, dg, t):
    tb = min(512, t)

    def body(a_ref, q_ref, f_ref, i_ref, g_ref, o_ref):
        o_ref[:, 0:ZQH] = a_ref[...]
        o_ref[:, ZQH:ZFH] = q_ref[...]
        o_ref[:, ZFH:ZIH] = f_ref[...]
        o_ref[:, ZIH:ZGH] = i_ref[...]
        o_ref[:, ZGH:D_IN] = g_ref[...]

    row = lambda w: pl.BlockSpec((tb, w), lambda i: (i, 0))
    return pl.pallas_call(
        body, name="assemble_dz", out_shape=jax.ShapeDtypeStruct((t, D_IN), BF16), grid=(t // tb,),
        in_specs=[row(ZQH), row(HG_W), row(HG_W), row(HG_W), row(HG_W)], out_specs=row(D_IN),
        compiler_params=_params(),
    )(dza, dq, df, di, dg)


XB = 512


def _xattn_fwd(q, k, v, t):
    tb = min(XB, t)

    def body(q_ref, k_ref, v_ref, o_ref):
        for h in range(XH):
            cols = slice(XD * h, XD * (h + 1))
            s = _dot(q_ref[:, cols], k_ref[:, cols], 1, 1) * (XD ** -0.5)
            p = jnp.exp(s - jnp.max(s, axis=-1, keepdims=True))
            l = jnp.sum(p, axis=-1, keepdims=True)
            o_ref[:, cols] = (_dot(p.astype(BF16), v_ref[:, cols], 1, 0) * (1.0 / l)).astype(BF16)

    row = pl.BlockSpec((tb, D), lambda i: (i, 0))
    mem = pl.BlockSpec(k.shape, lambda i: (0, 0))
    return pl.pallas_call(
        body, name="xattn_fwd", out_shape=jax.ShapeDtypeStruct((t, D), BF16), grid=(t // tb,),
        in_specs=[row, mem, mem], out_specs=row, compiler_params=_params(),
    )(q, k, v)


def _xattn_bwd(q, k, v, do, t):
    tb = min(XB, t)

    def body(q_ref, k_ref, v_ref, do_ref, dq_ref, dk_ref, dv_ref):
        @pl.when(pl.program_id(0) == 0)
        def _():
            dk_ref[...] = jnp.zeros_like(dk_ref)
            dv_ref[...] = jnp.zeros_like(dv_ref)

        for h in range(XH):
            cols = slice(XD * h, XD * (h + 1))
            qh, kh, vh, doh = q_ref[:, cols], k_ref[:, cols], v_ref[:, cols], do_ref[:, cols]
            s = _dot(qh, kh, 1, 1) * (XD ** -0.5)
            p = jnp.exp(s - jnp.max(s, axis=-1, keepdims=True))
            p = p * (1.0 / jnp.sum(p, axis=-1, keepdims=True))
            dp = _dot(doh, vh, 1, 1)
            ds = (p * (dp - jnp.sum(p * dp, axis=-1, keepdims=True)) * (XD ** -0.5)).astype(BF16)
            dq_ref[:, cols] = _dot(ds, kh, 1, 0).astype(BF16)
            dk_ref[:, cols] += _dot(ds, qh, 0, 0)
            dv_ref[:, cols] += _dot(p.astype(BF16), doh, 0, 0)

    row = pl.BlockSpec((tb, D), lambda i: (i, 0))
    mem = pl.BlockSpec(k.shape, lambda i: (0, 0))
    return pl.pallas_call(
        body, name="xattn_bwd",
        out_shape=(jax.ShapeDtypeStruct((t, D), BF16), jax.ShapeDtypeStruct(k.shape, F32),
                   jax.ShapeDtypeStruct(k.shape, F32)),
        grid=(t // tb,), in_specs=[row, mem, mem, row], out_specs=(row, mem, mem),
        compiler_params=_params(dimension_semantics=("arbitrary",)),
    )(q, k, v, do)


def _mem_gain_bwd(dm, mem, *, name):
    def body(dm_ref, m_ref, dg_ref):
        m_ = m_ref[...]
        dg_ref[...] = _rowsum8(dm_ref[...] * (m_ * _rstd(m_)))

    return pl.pallas_call(body, name=name, out_shape=jax.ShapeDtypeStruct((8, D), F32),
                          compiler_params=_params())(dm, mem)


FM, FN = 512, 1408


def _ffn_up(u, wgt, wut, t):
    tm = min(FM, t)

    def body(u_ref, wg_ref, wu_ref, g_ref, up_ref, a_ref):
        u_ = u_ref[...]
        g = _dot(u_, wg_ref[...], 1, 1)
        up = _dot(u_, wu_ref[...], 1, 1)
        g_ref[...] = g.astype(BF16)
        up_ref[...] = up.astype(BF16)
        a_ref[...] = (g * _sig(g) * up).astype(BF16)

    w = pl.BlockSpec((FN, D), lambda i, j: (j, 0))
    o = pl.BlockSpec((tm, FN), lambda i, j: (i, j))
    return pl.pallas_call(
        body, name="ffn_up", out_shape=(jax.ShapeDtypeStruct((t, D_FF), BF16),) * 3,
        grid=(t // tm, D_FF // FN), in_specs=[pl.BlockSpec((tm, D), lambda i, j: (i, 0)), w, w],
        out_specs=(o, o, o), compiler_params=_params(),
    )(u, wgt, wut)


def _ffn_down_bwd(dy, wd, gate, up, t, dep=None):
    tm = min(FM, t)
    deps = [] if dep is None else [dep]

    def body(dy_ref, w_ref, g_ref, up_ref, *rest):
        dg_ref, dup_ref = rest[len(deps):]
        da = _dot(dy_ref[...], w_ref[...], 1, 1)
        g = g_ref[...].astype(F32)
        sg = _sig(g)
        dup_ref[...] = (da * g * sg).astype(BF16)
        dg_ref[...] = (da * up_ref[...].astype(F32) * (sg * (1.0 + g * (1.0 - sg)))).astype(BF16)

    o = pl.BlockSpec((tm, FN), lambda i, j: (i, j))
    return pl.pallas_call(
        body, name="ffn_down_bwd", out_shape=(jax.ShapeDtypeStruct((t, D_FF), BF16),) * 2,
        grid=(t // tm, D_FF // FN),
        in_specs=[pl.BlockSpec((tm, D), lambda i, j: (i, 0)), pl.BlockSpec((FN, D), lambda i, j: (j, 0)), o, o]
        + [ANY_SPEC] * len(deps),
        out_specs=(o, o), compiler_params=_params(),
    )(dy, wd, gate, up, *deps)


def _local_step(x, mem, target, w, sm, emit=None):
    t = x.shape[0]
    gw = {}

    def out(key, g):
        gw[key] = g
        return None if emit is None else emit(key, g)
    u1 = _prenorm(x, sm["g_mix_pre"], name="prenorm_mix")
    z = _mm(u1, w["winT"], tb=True, out_dtype=F32, tm=1024, tn=1408, name="mm_z")
    ymix, lse = _swa_fwd(z, sm["sinks"], t)
    ymix, o_h, sprev = _hgrn_fwd(z, sm["hgrn_lb"], sm["hgrn_onorm"], ymix, t)
    y1 = _mm(ymix, w["wout"], out_dtype=F32, tm=1024, tn=1024, name="mm_y1")
    h1, u2 = _post_pre(x, y1, sm["g_mix_post"], sm["g_x_pre"], name="post_mix")
    mn = _prenorm(mem, sm["g_mem"], name="prenorm_mem")
    qx = _mm(u2, w["wq"], out_dtype=BF16, tm=1024, tn=1024, name="mm_qx")
    kx = _mm(mn, w["wk"], out_dtype=BF16, tm=1024, tn=1024, name="mm_kx")
    vx = _mm(mn, w["wv"], out_dtype=BF16, tm=1024, tn=1024, name="mm_vx")
    ox = _xattn_fwd(qx, kx, vx, t)
    y2 = _mm(ox, w["wo"], out_dtype=F32, tm=1024, tn=1024, name="mm_y2")
    h2, u3 = _post_pre(h1, y2, sm["g_x_post"], sm["g_ffn_pre"], name="post_x")
    gate, up, act = _ffn_up(u3, w["wgT"], w["wuT"], t)
    y3 = _mm(act, w["wd"], out_dtype=F32, tm=1024, tn=1024, tk=1408, name="mm_y3")
    sq, dh3, dy3, dg_ffn_post = _final_loss(h2, y3, sm["g_ffn_post"], target, name="final_loss")
    dep = out("wd", _mm(act, dy3, ta=True, out_dtype=BF16, tm=1408, tn=1024, tk=512, name="mm_gwd"))
    dgate, dup = _ffn_down_bwd(dy3, w["wd"], gate, up, t, dep=dep)
    dep = out("wgT", _mm(dgate, u3, ta=True, out_dtype=BF16, tm=1408, tn=1024, tk=512, name="mm_gwg"))
    dep = out("wuT", _mm(dup, u3, ta=True, out_dtype=BF16, tm=1408, tn=1024, tk=512, name="mm_gwu", dep=dep))
    du3 = _mm2(dgate, w["wgT"], dup, w["wuT"], tm=512, tk=1408, name="mm_du3", dep=dep)
    dh2, dy2, dg_ffn_pre, dg_x_post = _post_pre_bwd(dh3, du3, h2, y2, sm["g_x_post"], sm["g_ffn_pre"], name="post_x_bwd")
    dep = out("wo", _mm(ox, dy2, ta=True, out_dtype=BF16, tm=1024, tn=1024, tk=512, name="mm_gwo"))
    dox = _mm(dy2, w["wo"], tb=True, out_dtype=BF16, tm=1024, tn=1024, name="mm_dox", dep=dep)
    dqx, dkx, dvx = _xattn_bwd(qx, kx, vx, dox, t)
    dep = out("wq", _mm(u2, dqx, ta=True, out_dtype=BF16, tm=1024, tn=1024, tk=512, name="mm_gwq"))
    dep = out("wk", _mm(mn, dkx, ta=True, out_dtype=BF16, tm=1024, tn=1024, name="mm_gwk", dep=dep))
    dep = out("wv", _mm(mn, dvx, ta=True, out_dtype=BF16, tm=1024, tn=1024, name="mm_gwv", dep=dep))
    du2 = _mm(dqx, w["wq"], tb=True, out_dtype=F32, tm=1024, tn=1024, name="mm_du2", dep=dep)
    dmn = _mm2(dkx, w["wk"], dvx, w["wv"], tb=True, tm=256, tk=1024, name="mm_dmn")
    dg_mem = _mem_gain_bwd(dmn, mem, name="mem_gain_bwd")
    dh1, dy1, dg_x_pre, dg_mix_post = _post_pre_bwd(dh2, du2, h1, y1, sm["g_mix_post"], sm["g_x_pre"], name="post_mix_bwd")
    dep = out("wout", _mm(ymix, dy1, ta=True, out_dtype=BF16, tm=1024, tn=1024, tk=512, name="mm_gwout"))
    dymix = _mm(dy1, w["wout"], tb=True, out_dtype=BF16, tm=1024, tn=1024, name="mm_dymix", dep=dep)
    dza, dsinks = _swa_bwd(z, sm["sinks"], ymix, lse, dymix, t)
    dqh, dfh, dih, dgh, dlb, donorm = _hgrn_bwd(z, sm["hgrn_lb"], sm["hgrn_onorm"], o_h, sprev, dymix, t)
    dz = _assemble_dz(dza, dqh, dfh, dih, dgh, t)
    dep = out("winT", _mm(dz, u1, ta=True, out_dtype=BF16, tm=1408, tn=1024, tk=512, name="mm_gwin"))
    du1 = _mm(dz, w["winT"], out_dtype=F32, tm=512, tn=1024, tk=1408, name="mm_du1", dep=dep)
    grad_x, dg_mix_pre = _pre_bwd(dh1, du1, x, sm["g_mix_pre"], name="pre_mix_bwd")
    parts = dict(g_mix_pre=dg_mix_pre, g_mix_post=dg_mix_post, g_mem=dg_mem, g_x_pre=dg_x_pre,
                 g_x_post=dg_x_post, g_ffn_pre=dg_ffn_pre, g_ffn_post=dg_ffn_post,
                 hgrn_onorm=donorm, hgrn_lb=dlb, sinks=dsinks, sq=sq)
    return grad_x, gw, parts


def _position():
    return lax.axis_index("x"), lax.axis_index("y"), lax.axis_index("c")


def _peer(pos, k):
    x, y, c = pos
    return (1 - x if k & 4 else x, 1 - y if k & 2 else y, 1 - c if k & 1 else c)


def _linear(pos):
    x, y, c = pos
    return 4 * x + 2 * y + c


def _all_gather(shards):
    n = len(shards)
    rows = [s.shape[0] for s in shards]

    def body(*refs):
        ins, outs = refs[:n], refs[n:2 * n]
        send_sems, recv_sems, local_sems = refs[2 * n:]
        me = _position()
        x, y, c = me
        sibling = (x, y, 1 - c)
        chips = [(1 - x, y), (x, 1 - y), (1 - x, 1 - y)]

        def blk(a, pos):
            return outs[a].at[pl.ds(_linear(pos) * rows[a], rows[a]), :]

        def copy(a, k, block, to, src=None):
            return pltpu.make_async_remote_copy(
                src_ref=blk(a, block) if src is None else src, dst_ref=blk(a, block),
                send_sem=send_sems.at[7 * a + k], recv_sem=recv_sems.at[7 * a + k],
                device_id=to, device_id_type=MESH)

        mine = [pltpu.make_async_copy(ins[a], blk(a, me), local_sems.at[a]) for a in range(n)]
        for cp in mine:
            cp.start()
        first = []
        for a in range(n):
            first.append(copy(a, 0, me, sibling, src=ins[a]))
            first += [copy(a, 1 + j, me, (*chip, c), src=ins[a]) for j, chip in enumerate(chips)]
        for cp in first:
            cp.start()
        passed = []
        for j, chip in enumerate(chips):
            for a in range(n):
                copy(a, 1 + j, (*chip, c), me).wait_recv()
                fwd = copy(a, 4 + j, (*chip, c), sibling)
                fwd.start()
                passed.append(fwd)
        for a in range(n):
            copy(a, 0, sibling, me).wait_recv()
        for j, chip in enumerate(chips):
            for a in range(n):
                copy(a, 4 + j, (*chip, 1 - c), me).wait_recv()
        for cp in first + passed:
            cp.wait_send()
        for cp in mine:
            cp.wait()

    hbm = pl.BlockSpec(memory_space=pl.ANY)
    return pl.pallas_call(
        body, name="all_gather_weights",
        out_shape=tuple(jax.ShapeDtypeStruct((N_DEV * s.shape[0], s.shape[1]), s.dtype) for s in shards),
        in_specs=[hbm] * n, out_specs=tuple([hbm] * n),
        scratch_shapes=[pltpu.SemaphoreType.DMA((7 * n,)), pltpu.SemaphoreType.DMA((7 * n,)),
                        pltpu.SemaphoreType.DMA((n,))],
        compiler_params=pltpu.CompilerParams(has_side_effects=True),
    )(*shards)


HBM_SPEC = pl.BlockSpec(memory_space=pltpu.HBM)
SEM_SPEC = pl.BlockSpec(memory_space=pltpu.SEMAPHORE)
DATAFLOW = pltpu.SideEffectType.DATAFLOW_SIDE_EFFECTING
SEND_ORDER = (1, 2, 4, 3, 5, 6, 7)


def _in_hbm(a):
    return pltpu.with_memory_space_constraint(a, pltpu.HBM)


def _exchange_start(g, *, name):
    r = g.shape[0] // N_DEV
    land_shape = (N_DEV - 1, r, g.shape[1])

    def body(g_ref, land_ref, send_sems, recv_sems, g_thru, land_thru):
        del g_thru, land_thru
        me = _position()
        for k in SEND_ORDER:
            peer = _peer(me, k)
            pltpu.make_async_remote_copy(
                src_ref=g_ref.at[pl.ds(_linear(peer) * r, r), :], dst_ref=land_ref.at[k - 1],
                send_sem=send_sems.at[k - 1], recv_sem=recv_sems.at[k - 1],
                device_id=peer, device_id_type=MESH).start()

    return pl.pallas_call(
        body, name=name,
        out_shape=(pltpu.SemaphoreType.DMA((N_DEV - 1,)), pltpu.SemaphoreType.DMA((N_DEV - 1,)),
                   pltpu.HBM(g.shape, g.dtype), pltpu.HBM(land_shape, g.dtype)),
        in_specs=(HBM_SPEC, HBM_SPEC), out_specs=(SEM_SPEC, SEM_SPEC, HBM_SPEC, HBM_SPEC),
        input_output_aliases={0: 2, 1: 3},
        compiler_params=pltpu.CompilerParams(has_side_effects=DATAFLOW),
    )(_in_hbm(g), _in_hbm(lax.empty(land_shape, g.dtype)))


def _exchange_wait(send_sems, recv_sems, g_thru, land_thru, after, *, name):
    r = land_thru.shape[1]

    def body(g_ref, land_ref, send_sems, recv_sems, after_ref, g_dead, got_ref):
        del after_ref, g_dead, got_ref
        me = _position()
        for k in SEND_ORDER:
            peer = _peer(me, k)
            copy = pltpu.make_async_remote_copy(
                src_ref=g_ref.at[pl.ds(_linear(peer) * r, r), :], dst_ref=land_ref.at[k - 1],
                send_sem=send_sems.at[k - 1], recv_sem=recv_sems.at[k - 1],
                device_id=peer, device_id_type=MESH)
            copy.wait_send()
            copy.wait_recv()

    return pl.pallas_call(
        body, name=name,
        out_shape=(pltpu.HBM(g_thru.shape, g_thru.dtype), pltpu.HBM(land_thru.shape, land_thru.dtype)),
        in_specs=(HBM_SPEC, HBM_SPEC, SEM_SPEC, SEM_SPEC, pl.BlockSpec(memory_space=pl.ANY)),
        out_specs=(HBM_SPEC, HBM_SPEC), input_output_aliases={0: 0, 1: 1},
        compiler_params=pltpu.CompilerParams(has_side_effects=DATAFLOW),
    )(g_thru, land_thru, send_sems, recv_sems, after)


def _sum_grad(own, land, *, name):
    def body(own_ref, land_ref, g_ref):
        g = own_ref[...].astype(F32)
        for s in range(N_DEV - 1):
            g = g + land_ref[s].astype(F32)
        g_ref[...] = g

    return pl.pallas_call(body, name=name, out_shape=jax.ShapeDtypeStruct(own.shape, F32),
                          compiler_params=_params())(own, land)


def _adamw_math(w, g, m, v):
    m = B1 * m + (1.0 - B1) * g
    v = B2 * v + (1.0 - B2) * (g * g)
    delta = -LR * ((m / C1) / (jnp.sqrt(v / C2) + AEPS) + WD * w)
    return delta, m, v


def _adamw(g, w, m, v, *, name):
    def body(g_ref, w_ref, m_ref, v_ref, d_ref, nm_ref, nv_ref):
        d_ref[...], nm_ref[...], nv_ref[...] = _adamw_math(w_ref[...], g_ref[...], m_ref[...], v_ref[...])

    return pl.pallas_call(body, name=name, out_shape=(jax.ShapeDtypeStruct(w.shape, F32),) * 3,
                          compiler_params=_params())(g, w, m, v)


def _sum_adamw(own, land, w, m, v, *, name):
    def body(own_ref, land_ref, w_ref, m_ref, v_ref, g_ref, d_ref, nm_ref, nv_ref):
        g = own_ref[...].astype(F32)
        for s in range(N_DEV - 1):
            g = g + land_ref[s].astype(F32)
        g_ref[...] = g
        d_ref[...], nm_ref[...], nv_ref[...] = _adamw_math(w_ref[...], g, m_ref[...], v_ref[...])

    return pl.pallas_call(body, name=name, out_shape=(jax.ShapeDtypeStruct(w.shape, F32),) * 4,
                          compiler_params=_params())(own, land, w, m, v)


SMALL = ("g_mix_pre", "g_mix_post", "g_mem", "g_x_pre", "g_x_post", "g_ffn_pre", "g_ffn_post",
         "hgrn_onorm", "hgrn_lb", "sinks")
SMALL_W = dict(hgrn_onorm=HD, hgrn_lb=HG_W, sinks=8)
SQ_ROW = len(SMALL)
PACK_ROWS = 16


def _small_step(parts, sm, m_sm, v_sm):
    ns = len(SMALL)

    def body(*refs):
        part = refs[:ns + 1]
        w_refs = refs[ns + 1:2 * ns + 1]
        m_refs = refs[2 * ns + 1:3 * ns + 1]
        v_refs = refs[3 * ns + 1:4 * ns + 1]
        outs = refs[4 * ns + 1:8 * ns + 2]
        loss_ref = outs[0]
        g_out, d_out = outs[1:ns + 1], outs[ns + 1:2 * ns + 1]
        nm_out, nv_out = outs[2 * ns + 1:3 * ns + 1], outs[3 * ns + 1:4 * ns + 1]
        gath, send_sems, recv_sems = refs[8 * ns + 2:]
        me = _position()
        mine = gath.at[_linear(me)]
        mine[...] = jnp.zeros((PACK_ROWS, D), F32)
        for r, name in enumerate(SMALL):
            wd = SMALL_W.get(name, D)
            mine[r:r + 1, 0:wd] = jnp.sum(part[r][...], axis=0, keepdims=True)[:, 0:wd]
        sq = jnp.sum(part[ns][...]) * (0.5 / D)
        mine[SQ_ROW:SQ_ROW + 1, :] = jnp.full((1, D), sq, F32)

        def copy(k):
            peer = _peer(me, k)
            return pltpu.make_async_remote_copy(
                src_ref=mine, dst_ref=mine, send_sem=send_sems.at[k - 1], recv_sem=recv_sems.at[k - 1],
                device_id=peer, device_id_type=MESH)

        def arrival(k):
            slot = gath.at[_linear(_peer(me, k))]
            return pltpu.make_async_remote_copy(
                src_ref=slot, dst_ref=slot, send_sem=send_sems.at[k - 1], recv_sem=recv_sems.at[k - 1],
                device_id=_peer(me, k), device_id_type=MESH)

        sent = [copy(k) for k in range(1, 8)]
        for cp in sent:
            cp.start()
        for k in range(1, 8):
            arrival(k).wait_recv()
        for cp in sent:
            cp.wait_send()
        tot = gath[0]
        for s in range(1, N_DEV):
            tot = tot + gath[s]
        loss_ref[...] = tot[SQ_ROW:SQ_ROW + 1, 0:1]
        for r, name in enumerate(SMALL):
            wd = SMALL_W.get(name, D)
            g = tot[r:r + 1, 0:wd]
            w = w_refs[r][...]
            if name == "hgrn_lb":
                mx = jnp.maximum(w[0:1], w[1:2])
                e0, e1 = jnp.exp(w[0:1] - mx), jnp.exp(w[1:2] - mx)
                lb0 = e0 / (e0 + e1)
                g0 = g * lb0 * (1.0 - lb0)
                for i, gi in enumerate((g0, -g0)):
                    d, nm, nv = _adamw_math(w[i:i + 1], gi, m_refs[r][i:i + 1, :], v_refs[r][i:i + 1, :])
                    g_out[r][i:i + 1, :] = gi
                    d_out[r][i:i + 1, :], nm_out[r][i:i + 1, :], nv_out[r][i:i + 1, :] = d, nm, nv
            else:
                d, nm, nv = _adamw_math(w, g, m_refs[r][...], v_refs[r][...])
                g_out[r][...] = g
                d_out[r][...], nm_out[r][...], nv_out[r][...] = d, nm, nv

    ins = [parts[n] for n in SMALL] + [parts["sq"]] + [sm[n] for n in SMALL] + [m_sm[n] for n in SMALL] \
        + [v_sm[n] for n in SMALL]
    shapes = [jax.ShapeDtypeStruct(sm[n].shape, F32) for n in SMALL]
    res = pl.pallas_call(
        body, name="small_step",
        out_shape=tuple([jax.ShapeDtypeStruct((1, 1), F32)] + shapes * 4),
        scratch_shapes=[pltpu.VMEM((N_DEV, PACK_ROWS, D), F32), pltpu.SemaphoreType.DMA((7,)),
                        pltpu.SemaphoreType.DMA((7,))],
        compiler_params=_params(has_side_effects=True),
    )(*ins)
    loss = res[0]
    groups = [dict(zip(SMALL, res[1 + i * ns:1 + (i + 1) * ns])) for i in range(4)]
    return loss, groups[0], groups[1], groups[2], groups[3]


BIG = ("w_in", "w_gate", "w_up", "w_down", "w_out", "wq_x", "wk_x", "wv_x", "wo_x")
BIG_KEY = dict(w_in="winT", w_gate="wgT", w_up="wuT", w_down="wd", w_out="wout", wq_x="wq", wk_x="wk",
               wv_x="wv", wo_x="wo")
TRANSPOSED = ("w_in", "w_gate", "w_up")
WEIGHTS = ("w_in", "sinks", "hgrn_lb", "hgrn_onorm", "w_out", "g_mix_pre", "g_mix_post", "g_mem", "g_x_pre",
           "g_x_post", "wq_x", "wk_x", "wv_x", "wo_x", "g_ffn_pre", "g_ffn_post", "w_gate", "w_up", "w_down")


def kernel(x, mem, w_in, sinks, hgrn_lb, hgrn_onorm, w_out, g_mix_pre, g_mix_post, g_mem, g_x_pre, g_x_post, wq_x, wk_x, wv_x, wo_x, g_ffn_pre, g_ffn_post, w_gate, w_up, w_down, loss_target, m_w_in, m_sinks, m_hgrn_lb, m_hgrn_onorm, m_w_out, m_g_mix_pre, m_g_mix_post, m_g_mem, m_g_x_pre, m_g_x_post, m_wq_x, m_wk_x, m_wv_x, m_wo_x, m_g_ffn_pre, m_g_ffn_post, m_w_gate, m_w_up, m_w_down, v_w_in, v_sinks, v_hgrn_lb, v_hgrn_onorm, v_w_out, v_g_mix_pre, v_g_mix_post, v_g_mem, v_g_x_pre, v_g_x_post, v_wq_x, v_wk_x, v_wv_x, v_wo_x, v_g_ffn_pre, v_g_ffn_post, v_w_gate, v_w_up, v_w_down):
    given = dict(locals())
    wts = {n: given[n] for n in WEIGHTS}
    ms = {n: given["m_" + n] for n in WEIGHTS}
    vs = {n: given["v_" + n] for n in WEIGHTS}

    def mat(a, name):
        a = a[0]
        return a.T if name in TRANSPOSED else a

    shards = [mat(wts[n], n).astype(BF16) for n in BIG]
    full = _all_gather(shards)
    w = {BIG_KEY[n]: f for n, f in zip(BIG, full)}
    sm = {n: wts[n] for n in SMALL}
    name_of = {k: n for n, k in BIG_KEY.items()}
    started = {}

    def emit(key, g):
        started[name_of[key]] = _exchange_start(g, name="grad_send_" + name_of[key])
        return started[name_of[key]][2]

    grad_x, _, parts = _local_step(x[0], mem[0], loss_target[0], w, sm, emit)
    me_lin = _linear(_position())
    grads, deltas, new_m, new_v = {}, {}, {}, {}
    after = grad_x
    for n in ("w_down", "w_gate", "w_up", "wo_x", "wq_x", "wk_x", "wv_x", "w_out", "w_in"):
        g_all, land = _exchange_wait(*started[n], after, name="grad_recv_" + n)
        r = land.shape[1]
        own = lax.dynamic_slice_in_dim(g_all, me_lin * r, r, 0)
        if n in TRANSPOSED:
            g = _sum_grad(own, land, name="sum_" + n).T
            d, nm, nv = _adamw(g, wts[n][0], ms[n][0], vs[n][0], name="adamw_" + n)
        else:
            g, d, nm, nv = _sum_adamw(own, land, wts[n][0], ms[n][0], vs[n][0], name="adamw_" + n)
        grads[n], deltas[n], new_m[n], new_v[n] = g[None], d[None], nm[None], nv[None]
        after = d
    loss, g_s, d_s, m_s, v_s = _small_step(parts, sm, {n: ms[n] for n in SMALL}, {n: vs[n] for n in SMALL})
    grads.update(g_s), deltas.update(d_s), new_m.update(m_s), new_v.update(v_s)
    return (loss[0, 0], grad_x[None], *[grads[n] for n in WEIGHTS], *[deltas[n] for n in WEIGHTS],
            *[new_m[n] for n in WEIGHTS], *[new_v[n] for n in WEIGHTS])
```

```python
import functools

import jax
import jax.numpy as jnp
from jax import lax
from jax.experimental import pallas as pl
from jax.experimental.pallas import tpu as pltpu

F32 = jnp.float32
BF16 = jnp.bfloat16

D = 1024
D_IN = 2816
D_FF = 2816
CHUNK = 64
SWA_W = 512
KV_W = 128
HG_W = 512
HD = 128
ZQH, ZFH, ZIH, ZGH = 768, 1280, 1792, 2304
XH, XD = 4, 256
EPS = 1e-6
NEG = -1e30
N_DEV = 8
MESH = pl.DeviceIdType.MESH

LR, B1, B2, AEPS, WD, STEP = 0.001, 0.9, 0.999, 1e-08, 0.01, 10
C1 = 1.0 - B1 ** STEP
C2 = 1.0 - B2 ** STEP

VMEM_LIMIT = 56 * 1024 * 1024


def _params(**kw):
    return pltpu.CompilerParams(vmem_limit_bytes=VMEM_LIMIT, **kw)


def _sig(x):
    return 1.0 / (1.0 + jnp.exp(-x))


def _rowsum8(x):
    r, w = x.shape
    return jnp.sum(x.reshape(r // 8, 8, w), axis=0)


def _dot(a, b, ca, cb, precision=None):
    return lax.dot_general(a, b, (((ca,), (cb,)), ((), ())), preferred_element_type=F32,
                           precision=precision)


ANY_SPEC = pl.BlockSpec(memory_space=pl.ANY)


def _mm(a, b, *, ta=False, tb=False, out_dtype, tm, tn, tk=None, name, dep=None):
    m = a.shape[1] if ta else a.shape[0]
    k = a.shape[0] if ta else a.shape[1]
    n = b.shape[0] if tb else b.shape[1]
    tm, tn = min(tm, m), min(tn, n)
    tk = k if tk is None else min(tk, k)
    nk = k // tk
    assert m % tm == 0 and n % tn == 0 and k % tk == 0, (name, m, n, k, tm, tn, tk)
    a_spec = pl.BlockSpec((tk, tm), lambda i, j, kk: (kk, i)) if ta else pl.BlockSpec((tm, tk), lambda i, j, kk: (i, kk))
    b_spec = pl.BlockSpec((tn, tk), lambda i, j, kk: (j, kk)) if tb else pl.BlockSpec((tk, tn), lambda i, j, kk: (kk, j))
    ca, cb = (0 if ta else 1), (1 if tb else 0)

    deps = [] if dep is None else [dep]

    def body(a_ref, b_ref, *rest):
        o_ref, acc = rest[len(deps)], rest[len(deps) + 1:]
        p = _dot(a_ref[...].astype(BF16), b_ref[...].astype(BF16), ca, cb)
        if nk == 1:
            o_ref[...] = p.astype(out_dtype)
        else:
            acc_ref, = acc
            kk = pl.program_id(2)

            @pl.when(kk == 0)
            def _():
                acc_ref[...] = p

            @pl.when(kk > 0)
            def _():
                acc_ref[...] += p

            @pl.when(kk == nk - 1)
            def _():
                o_ref[...] = acc_ref[...].astype(out_dtype)

    return pl.pallas_call(
        body, name=name, out_shape=jax.ShapeDtypeStruct((m, n), out_dtype),
        grid=(m // tm, n // tn, nk), in_specs=[a_spec, b_spec] + [ANY_SPEC] * len(deps),
        out_specs=pl.BlockSpec((tm, tn), lambda i, j, kk: (i, j)),
        scratch_shapes=[pltpu.VMEM((tm, tn), F32)] if nk > 1 else [],
        compiler_params=_params(dimension_semantics=("parallel", "parallel", "arbitrary")),
    )(a, b, *deps)


def _mm2(a1, b1, a2, b2, *, tb=False, tm, tk, name, dep=None):
    m, k = a1.shape
    n = b1.shape[0] if tb else b1.shape[1]
    tm, tk = min(tm, m), min(tk, k)
    nk = k // tk
    assert m % tm == 0 and k % tk == 0
    cb = 1 if tb else 0
    deps = [] if dep is None else [dep]

    def body(a1_ref, b1_ref, a2_ref, b2_ref, *rest):
        o_ref = rest[len(deps)]
        p = (_dot(a1_ref[...].astype(BF16), b1_ref[...], 1, cb)
             + _dot(a2_ref[...].astype(BF16), b2_ref[...], 1, cb))
        kk = pl.program_id(1)

        @pl.when(kk == 0)
        def _():
            o_ref[...] = p

        @pl.when(kk > 0)
        def _():
            o_ref[...] += p

    a_spec = pl.BlockSpec((tm, tk), lambda i, kk: (i, kk))
    b_spec = pl.BlockSpec((n, tk), lambda i, kk: (0, kk)) if tb else pl.BlockSpec((tk, n), lambda i, kk: (kk, 0))
    return pl.pallas_call(
        body, name=name, out_shape=jax.ShapeDtypeStruct((m, n), F32),
        grid=(m // tm, nk), in_specs=[a_spec, b_spec, a_spec, b_spec] + [ANY_SPEC] * len(deps),
        out_specs=pl.BlockSpec((tm, n), lambda i, kk: (i, 0)),
        compiler_params=_params(dimension_semantics=("parallel", "arbitrary")),
    )(a1, b1, a2, b2, *deps)


def _rstd(x):
    return lax.rsqrt(jnp.mean(x * x, axis=-1, keepdims=True) + EPS)


def _norm_bwd(xh, r, t):
    return r * (t - xh * jnp.mean(xh * t, axis=-1, keepdims=True))


def _prenorm(x, g, *, name):
    t, d = x.shape
    tb = min(512, t)

    def body(x_ref, g_ref, o_ref):
        xf = x_ref[...]
        o_ref[...] = (xf * _rstd(xf) * g_ref[...]).astype(BF16)

    return pl.pallas_call(
        body, name=name, out_shape=jax.ShapeDtypeStruct((t, d), BF16), grid=(t // tb,),
        in_specs=[pl.BlockSpec((tb, d), lambda i: (i, 0)), pl.BlockSpec((1, d), lambda i: (0, 0))],
        out_specs=pl.BlockSpec((tb, d), lambda i: (i, 0)), compiler_params=_params(),
    )(x, g)


def _post_pre(h, y, g_post, g_pre, *, name):
    t, d = h.shape
    tb = min(512, t)

    def body(h_ref, y_ref, gp_ref, gn_ref, hn_ref, u_ref):
        y_ = y_ref[...]
        hn = h_ref[...] + y_ * _rstd(y_) * gp_ref[...]
        hn_ref[...] = hn
        u_ref[...] = (hn * _rstd(hn) * gn_ref[...]).astype(BF16)

    row = pl.BlockSpec((tb, d), lambda i: (i, 0))
    vec = pl.BlockSpec((1, d), lambda i: (0, 0))
    return pl.pallas_call(
        body, name=name, out_shape=(jax.ShapeDtypeStruct((t, d), F32), jax.ShapeDtypeStruct((t, d), BF16)),
        grid=(t // tb,), in_specs=[row, row, vec, vec], out_specs=(row, row), compiler_params=_params(),
    )(h, y, g_post, g_pre)


def _final_loss(h, y, g_post, target, *, name):
    t, d = h.shape
    tb = min(512, t)

    def body(h_ref, y_ref, g_ref, t_ref, sq_ref, dh_ref, dy_ref, dg_ref):
        @pl.when(pl.program_id(0) == 0)
        def _():
            sq_ref[...] = jnp.zeros_like(sq_ref)
            dg_ref[...] = jnp.zeros_like(dg_ref)

        y_ = y_ref[...]
        r = _rstd(y_)
        yh = y_ * r
        g = g_ref[...]
        err = h_ref[...] + yh * g - t_ref[...]
        sq_ref[...] += _rowsum8(err * err)
        dh = err * (1.0 / d)
        dh_ref[...] = dh
        dg_ref[...] += _rowsum8(dh * yh)
        dy_ref[...] = _norm_bwd(yh, r, dh * g).astype(BF16)

    row = pl.BlockSpec((tb, d), lambda i: (i, 0))
    vec = pl.BlockSpec((1, d), lambda i: (0, 0))
    acc = pl.BlockSpec((8, d), lambda i: (0, 0))
    return pl.pallas_call(
        body, name=name,
        out_shape=(jax.ShapeDtypeStruct((8, d), F32), jax.ShapeDtypeStruct((t, d), F32),
                   jax.ShapeDtypeStruct((t, d), BF16), jax.ShapeDtypeStruct((8, d), F32)),
        grid=(t // tb,), in_specs=[row, row, vec, row], out_specs=(acc, row, row, acc),
        compiler_params=_params(dimension_semantics=("arbitrary",)),
    )(h, y, g_post, target)


def _post_pre_bwd(dh_out, du, hn, y, g_post, g_pre, *, name):
    t, d = hn.shape
    tb = min(512, t)

    def body(dho_ref, du_ref, hn_ref, y_ref, gp_ref, gn_ref, dh_ref, dy_ref, dgn_ref, dgp_ref):
        @pl.when(pl.program_id(0) == 0)
        def _():
            dgn_ref[...] = jnp.zeros_like(dgn_ref)
            dgp_ref[...] = jnp.zeros_like(dgp_ref)

        hn_ = hn_ref[...]
        r2 = _rstd(hn_)
        xh = hn_ * r2
        du_ = du_ref[...]
        dgn_ref[...] += _rowsum8(du_ * xh)
        dh = dho_ref[...] + _norm_bwd(xh, r2, du_ * gn_ref[...])
        dh_ref[...] = dh
        y_ = y_ref[...]
        r1 = _rstd(y_)
        yh = y_ * r1
        dgp_ref[...] += _rowsum8(dh * yh)
        dy_ref[...] = _norm_bwd(yh, r1, dh * gp_ref[...]).astype(BF16)

    row = pl.BlockSpec((tb, d), lambda i: (i, 0))
    vec = pl.BlockSpec((1, d), lambda i: (0, 0))
    acc = pl.BlockSpec((8, d), lambda i: (0, 0))
    return pl.pallas_call(
        body, name=name,
        out_shape=(jax.ShapeDtypeStruct((t, d), F32), jax.ShapeDtypeStruct((t, d), BF16),
                   jax.ShapeDtypeStruct((8, d), F32), jax.ShapeDtypeStruct((8, d), F32)),
        grid=(t // tb,), in_specs=[row, row, row, row, vec, vec], out_specs=(row, row, acc, acc),
        compiler_params=_params(dimension_semantics=("arbitrary",)),
    )(dh_out, du, hn, y, g_post, g_pre)


def _pre_bwd(dh_out, du, x, g, *, name):
    t, d = x.shape
    tb = min(512, t)
    has_res = dh_out is not None

    def body(*refs):
        if has_res:
            dho_ref, du_ref, x_ref, g_ref, dx_ref, dg_ref = refs
        else:
            du_ref, x_ref, g_ref, dx_ref, dg_ref = refs

        @pl.when(pl.program_id(0) == 0)
        def _():
            dg_ref[...] = jnp.zeros_like(dg_ref)

        x_ = x_ref[...]
        r = _rstd(x_)
        xh = x_ * r
        du_ = du_ref[...]
        dg_ref[...] += _rowsum8(du_ * xh)
        dx = _norm_bwd(xh, r, du_ * g_ref[...])
        if has_res:
            dx = dx + dho_ref[...]
        dx_ref[...] = dx

    row = pl.BlockSpec((tb, d), lambda i: (i, 0))
    vec = pl.BlockSpec((1, d), lambda i: (0, 0))
    acc = pl.BlockSpec((8, d), lambda i: (0, 0))
    ins = ([dh_out] if has_res else []) + [du, x, g]
    return pl.pallas_call(
        body, name=name,
        out_shape=(jax.ShapeDtypeStruct((t, d), F32), jax.ShapeDtypeStruct((8, d), F32)),
        grid=(t // tb,), in_specs=[row] * (len(ins) - 1) + [vec], out_specs=(row, acc),
        compiler_params=_params(dimension_semantics=("arbitrary",)),
    )(*ins)


QB = 256


def _half_mask(shape, e):
    lane = lax.broadcasted_iota(jnp.int32, shape, len(shape) - 1)
    return (lane // 64) == e


def _place(kv):
    sw = pltpu.roll(kv, 64, 1)
    m0 = _half_mask(kv.shape, 0)
    return [[jnp.where(m0, kv, 0.0).astype(BF16), jnp.where(m0, 0.0, sw).astype(BF16)],
            [jnp.where(m0, sw, 0.0).astype(BF16), jnp.where(m0, 0.0, kv).astype(BF16)]]


def _swa_valid_q(i, nq, nk):
    qc = lax.broadcasted_iota(jnp.int32, (nq, nk), 0) // CHUNK
    kc = lax.broadcasted_iota(jnp.int32, (nq, nk), 1) // CHUNK - 2
    return (kc <= qc) & (qc <= kc + 2) & (4 * i + kc >= 0)


def _swa_fwd(z, sinks, t):
    nb = t // QB

    def body(s_ref, q_ref, kp_ref, kc_ref, vp_ref, vc_ref, o_ref, lse_ref):
        i = pl.program_id(0)
        kpl = _place(jnp.concatenate([kp_ref[...], kc_ref[...]], axis=0))
        vpl = _place(jnp.concatenate([vp_ref[...], vc_ref[...]], axis=0))
        valid = _swa_valid_q(i, QB, QB + 128)
        lane = lax.broadcasted_iota(jnp.int32, (QB, 128), 1)
        lse_out = jnp.zeros((QB, 128), F32)
        for j in range(4):
            qp = q_ref[:, 128 * j:128 * (j + 1)].astype(BF16)
            acc = jnp.zeros((QB, 128), F32)
            for e in range(2):
                h = 2 * j + e
                kvh = h // 4
                qm = jnp.where(_half_mask(qp.shape, e), qp, jnp.zeros_like(qp))
                s = _dot(qm, kpl[kvh][e], 1, 1) * 0.125
                s = jnp.where(valid, s, NEG)
                sink = s_ref[0, h]
                m = jnp.maximum(jnp.max(s, axis=-1, keepdims=True), sink)
                p = jnp.exp(s - m)
                l = jnp.sum(p, axis=-1, keepdims=True) + jnp.exp(sink - m)
                acc = acc + _dot(p.astype(BF16), vpl[kvh][e], 1, 0) * (1.0 / l)
                lse_out = jnp.where(lane == h, m + jnp.log(l), lse_out)
            o_ref[:, 128 * j:128 * (j + 1)] = acc.astype(BF16)
        lse_ref[...] = lse_out

    prev = lambda c: pl.BlockSpec((128, 128), lambda i: (jnp.maximum(2 * i - 1, 0), c))
    cur = lambda c: pl.BlockSpec((QB, 128), lambda i: (i, c))
    return pl.pallas_call(
        body, name="swa_fwd",
        out_shape=(jax.ShapeDtypeStruct((t, D), BF16), jax.ShapeDtypeStruct((t, 128), F32)),
        grid=(nb,),
        in_specs=[pl.BlockSpec(memory_space=pltpu.SMEM),
                  pl.BlockSpec((QB, SWA_W), lambda i: (i, 0)), prev(4), cur(4), prev(5), cur(5)],
        out_specs=(pl.BlockSpec((QB, SWA_W), lambda i: (i, 0)), pl.BlockSpec((QB, 128), lambda i: (i, 0))),
        compiler_params=_params(),
    )(sinks, z, z, z, z, z)


def _swa_bwd(z, sinks, ymix, lse, dymix, t):
    nb = t // QB
    nq2 = QB + 128

    def body(s_ref, qc_ref, qn_ref, kp_ref, kc_ref, vp_ref, vc_ref, oc_ref, on_ref, doc_ref, don_ref,
             lc_ref, ln_ref, dz_ref, ds_ref):
        i = pl.program_id(0)

        @pl.when(i == 0)
        def _():
            ds_ref[...] = jnp.zeros_like(ds_ref)

        lane = lax.broadcasted_iota(jnp.int32, (8, 128), 1)
        kpl = _place(jnp.concatenate([kp_ref[...], kc_ref[...]], axis=0))
        vpl = _place(jnp.concatenate([vp_ref[...], vc_ref[...]], axis=0))
        valid = _swa_valid_q(i, QB, nq2)
        lse_c = lc_ref[...]
        dsink = jnp.zeros((8, 128), F32)
        for j in range(4):
            cols = slice(128 * j, 128 * (j + 1))
            qp = qc_ref[:, cols].astype(BF16)
            dop = doc_ref[:, cols]
            prod = dop.astype(F32) * oc_ref[:, cols].astype(F32)
            acc = jnp.zeros((QB, 128), F32)
            for e in range(2):
                h = 2 * j + e
                kvh = h // 4
                hm = _half_mask(qp.shape, e)
                qm = jnp.where(hm, qp, jnp.zeros_like(qp))
                dom = jnp.where(hm, dop, jnp.zeros_like(dop))
                dd = jnp.sum(jnp.where(hm, prod, 0.0), axis=-1, keepdims=True)
                lse_h = lse_c[:, h:h + 1]
                s = _dot(qm, kpl[kvh][e], 1, 1) * 0.125
                p = jnp.where(valid, jnp.exp(s - lse_h), 0.0)
                dp = _dot(dom, vpl[kvh][e], 1, 1)
                ds = p * (dp - dd) * 0.125
                acc = acc + _dot(ds.astype(BF16), kpl[kvh][e], 1, 0)
                ps = jnp.exp(s_ref[0, h] - lse_h) * dd
                dsink = dsink - jnp.where(lane == h, _rowsum8(jnp.broadcast_to(ps, (QB, 128))), 0.0)
            dz_ref[:, cols] = acc.astype(BF16)
        ds_ref[...] += dsink
        kpl, vpl = _place(kc_ref[...]), _place(vc_ref[...])
        qr = lax.broadcasted_iota(jnp.int32, (nq2, QB), 0) // CHUNK
        kr = lax.broadcasted_iota(jnp.int32, (nq2, QB), 1) // CHUNK
        valid2 = (kr <= qr) & (qr <= kr + 2) & (4 * i + qr < t // CHUNK)
        lse_a = jnp.concatenate([lse_c, ln_ref[...]], axis=0)
        dk_acc = [[jnp.zeros((QB, 128), F32) for _ in range(2)] for _ in range(2)]
        dv_acc = [[jnp.zeros((QB, 128), F32) for _ in range(2)] for _ in range(2)]
        for j in range(4):
            cols = slice(128 * j, 128 * (j + 1))
            qp = jnp.concatenate([qc_ref[:, cols], qn_ref[:, cols]], axis=0).astype(BF16)
            dop = jnp.concatenate([doc_ref[:, cols], don_ref[:, cols]], axis=0)
            op = jnp.concatenate([oc_ref[:, cols], on_ref[:, cols]], axis=0)
            prod = dop.astype(F32) * op.astype(F32)
            for e in range(2):
                h = 2 * j + e
                kvh = h // 4
                hm = _half_mask(qp.shape, e)
                qm = jnp.where(hm, qp, jnp.zeros_like(qp))
                dom = jnp.where(hm, dop, jnp.zeros_like(dop))
                dd = jnp.sum(jnp.where(hm, prod, 0.0), axis=-1, keepdims=True)
                s = _dot(qm, kpl[kvh][e], 1, 1) * 0.125
                p = jnp.where(valid2, jnp.exp(s - lse_a[:, h:h + 1]), 0.0)
                dv_acc[kvh][e] = dv_acc[kvh][e] + _dot(p.astype(BF16), dom, 0, 0)
                dp = _dot(dom, vpl[kvh][e], 1, 1)
                ds = p * (dp - dd) * 0.125
                dk_acc[kvh][e] = dk_acc[kvh][e] + _dot(ds.astype(BF16), qm, 0, 0)
        dk = dk_acc[0][0] + dk_acc[1][1] + pltpu.roll(dk_acc[0][1] + dk_acc[1][0], 64, 1)
        dv = dv_acc[0][0] + dv_acc[1][1] + pltpu.roll(dv_acc[0][1] + dv_acc[1][0], 64, 1)
        dz_ref[:, 512:640] = dk.astype(BF16)
        dz_ref[:, 640:768] = dv.astype(BF16)

    last = 2 * nb - 1
    prev = lambda c: pl.BlockSpec((128, 128), lambda i: (jnp.maximum(2 * i - 1, 0), c))
    cur = lambda w, c: pl.BlockSpec((QB, w), lambda i: (i, c))
    nxt = lambda w: pl.BlockSpec((128, w), lambda i: (jnp.minimum(2 * i + 2, last), 0))
    return pl.pallas_call(
        body, name="swa_bwd",
        out_shape=(jax.ShapeDtypeStruct((t, 768), BF16), jax.ShapeDtypeStruct((8, 128), F32)),
        grid=(nb,),
        in_specs=[pl.BlockSpec(memory_space=pltpu.SMEM),
                  cur(SWA_W, 0), nxt(SWA_W), prev(4), cur(128, 4), prev(5), cur(128, 5),
                  cur(SWA_W, 0), nxt(SWA_W), cur(SWA_W, 0), nxt(SWA_W), cur(128, 0), nxt(128)],
        out_specs=(pl.BlockSpec((QB, 768), lambda i: (i, 0)), pl.BlockSpec((8, 128), lambda i: (0, 0))),
        compiler_params=_params(dimension_semantics=("arbitrary",)),
    )(sinks, z, z, z, z, z, z, ymix, ymix, dymix, dymix, lse, lse)


HB = 256
HI = lax.Precision.HIGHEST


def _lower_bound(lb_ref):
    a = lb_ref[...]
    a0, a1 = a[0:1], a[1:2]
    mx = jnp.maximum(a0, a1)
    e0, e1 = jnp.exp(a0 - mx), jnp.exp(a1 - mx)
    return e0 / (e0 + e1)


def _tri(lower):
    r = lax.broadcasted_iota(jnp.int32, (CHUNK, CHUNK), 0)
    c = lax.broadcasted_iota(jnp.int32, (CHUNK, CHUNK), 1)
    return (c <= r) if lower else (c >= r)


def _hgrn_chunk(q, fl, lb):
    sq = _sig(q)
    qf = q * sq * (HD ** -0.5)
    sg = _sig(fl)
    f = lb + (1.0 - lb) * sg
    kf = 1.0 - f
    b = _dot(_tri(True).astype(F32), jnp.log(f), 1, 0, precision=HI)
    b_mid = b[CHUNK // 2 - 1:CHUNK // 2]
    b_last = b[CHUNK - 1:CHUNK]
    qm = qf * jnp.exp(b - b_mid)
    km = kf * jnp.exp(b_mid - b)
    kl = kf * jnp.exp(b_last - b)
    qb = qf * jnp.exp(b)
    return dict(sq=sq, qf=qf, sg=sg, f=f, kf=kf, b=b, b_mid=b_mid, b_last=b_last, qm=qm, km=km, kl=kl, qb=qb)


def _hgrn_fwd(z, hgrn_lb, onorm, ymix, t):
    nb = t // HB
    nc = HB // CHUNK

    def body(zq_ref, zf_ref, zi_ref, zg_ref, lb_ref, on_ref, ymix_in, y_ref, o_ref, sp_ref, st_ref):
        del ymix_in

        @pl.when(pl.program_id(1) == 0)
        def _():
            st_ref[...] = jnp.zeros_like(st_ref)

        lb = _lower_bound(lb_ref)
        gn = on_ref[...]

        def chunk(c, carry):
            rows = pl.ds(pl.multiple_of(c * CHUNK, CHUNK), CHUNK)
            w = _hgrn_chunk(zq_ref[rows, :], zf_ref[rows, :], lb)
            iv = zi_ref[rows, :].astype(BF16)
            st = st_ref[...]
            sp_ref[0, c] = st
            a = jnp.where(_tri(True), _dot(w["qm"].astype(BF16), w["km"].astype(BF16), 1, 1), 0.0)
            o = _dot(a.astype(BF16), iv, 1, 0) + _dot(w["qb"].astype(BF16), st.astype(BF16), 1, 1)
            st_ref[...] = st * jnp.exp(w["b_last"]) + _dot(iv, w["kl"].astype(BF16), 0, 0)
            o_ref[rows, :] = o
            gg = zg_ref[rows, :]
            y_ref[rows, :] = (o * _rstd(o) * gn * (gg * _sig(gg))).astype(BF16)
            return carry

        lax.fori_loop(0, nc, chunk, 0)

    col = lambda base: pl.BlockSpec((HB, HD), lambda h, j: (j, base // HD + h))
    return pl.pallas_call(
        body, name="hgrn_fwd",
        out_shape=(jax.ShapeDtypeStruct((t, D), BF16), jax.ShapeDtypeStruct((t, HG_W), F32),
                   jax.ShapeDtypeStruct((4, t // CHUNK, HD, HD), F32)),
        grid=(4, nb),
        in_specs=[col(ZQH), col(ZFH), col(ZIH), col(ZGH),
                  pl.BlockSpec((2, HD), lambda h, j: (0, h)), pl.BlockSpec((1, HD), lambda h, j: (0, 0)),
                  pl.BlockSpec(memory_space=pl.ANY)],
        out_specs=(pl.BlockSpec((HB, HD), lambda h, j: (j, SWA_W // HD + h)),
                   pl.BlockSpec((HB, HD), lambda h, j: (j, h)),
                   pl.BlockSpec((1, nc, HD, HD), lambda h, j: (h, j, 0, 0))),
        scratch_shapes=[pltpu.VMEM((HD, HD), F32)],
        input_output_aliases={6: 0},
        compiler_params=_params(dimension_semantics=("arbitrary", "arbitrary")),
    )(z, z, z, z, hgrn_lb, onorm, ymix)


def _hgrn_bwd(z, hgrn_lb, onorm, o_save, sprev, dymix, t):
    nb = t // HB
    nc = HB // CHUNK

    def body(zq_ref, zf_ref, zi_ref, zg_ref, lb_ref, on_ref, o_ref, sp_ref, dy_ref,
             dq_ref, df_ref, di_ref, dg_ref, dlb_ref, don_ref, dst_ref):
        hh, jj = pl.program_id(0), pl.program_id(1)

        @pl.when(jj == 0)
        def _():
            dst_ref[...] = jnp.zeros_like(dst_ref)
            dlb_ref[...] = jnp.zeros_like(dlb_ref)

        @pl.when((jj == 0) & (hh == 0))
        def _():
            don_ref[...] = jnp.zeros_like(don_ref)

        lb = _lower_bound(lb_ref)
        gn = on_ref[...]
        row = lax.broadcasted_iota(jnp.int32, (CHUNK, HD), 0)

        def chunk(cc, carry):
            c = nc - 1 - cc
            rows = pl.ds(pl.multiple_of(c * CHUNK, CHUNK), CHUNK)
            q = zq_ref[rows, :]
            w = _hgrn_chunk(q, zf_ref[rows, :], lb)
            iv = zi_ref[rows, :].astype(BF16)
            gg = zg_ref[rows, :]
            o = o_ref[rows, :]
            st = sp_ref[0, c]
            dst = dst_ref[...]
            dout = dy_ref[rows, :].astype(F32)
            sgg = _sig(gg)
            r = _rstd(o)
            oh = o * r
            dyn = dout * (gg * sgg)
            dg_ref[rows, :] = (dout * oh * gn * (sgg * (1.0 + gg * (1.0 - sgg)))).astype(BF16)
            don_ref[...] += _rowsum8(dyn * oh)
            do = _norm_bwd(oh, r, dyn * gn).astype(BF16)
            qm, km, kl, qb = (w[n].astype(BF16) for n in ("qm", "km", "kl", "qb"))
            dstb = dst.astype(BF16)
            d_row = jnp.exp(w["b_last"])
            dqb = _dot(do, st.astype(BF16), 1, 0)
            dst_ref[...] = dst * d_row + _dot(do, qb, 0, 0)
            dd_row = jnp.sum(dst * st, axis=0, keepdims=True)
            at = jnp.where(_tri(False), _dot(km, qm, 1, 1), 0.0)
            di_ref[rows, :] = (_dot(at.astype(BF16), do, 1, 0) + _dot(kl, dstb, 1, 1)).astype(BF16)
            dkl = _dot(iv, dstb, 1, 0)
            da = jnp.where(_tri(True), _dot(do, iv, 1, 1), 0.0).astype(BF16)
            dat = jnp.where(_tri(False), _dot(iv, do, 1, 1), 0.0).astype(BF16)
            dqm = _dot(da, km, 1, 0)
            dkm = _dot(dat, qm, 1, 0)
            e1, e2 = jnp.exp(w["b"] - w["b_mid"]), jnp.exp(w["b_mid"] - w["b"])
            e3, e4 = jnp.exp(w["b_last"] - w["b"]), jnp.exp(w["b"])
            dqf = dqm * e1 + dqb * e4
            dkf = dkm * e2 + dkl * e3
            t_qm, t_km, t_kl = dqm * w["qm"], dkm * w["km"], dkl * w["kl"]
            db = t_qm - t_km - t_kl + dqb * w["qb"]
            db_mid = jnp.sum(t_km - t_qm, axis=0, keepdims=True)
            db_last = jnp.sum(t_kl, axis=0, keepdims=True) + dd_row * d_row
            db = db + jnp.where(row == CHUNK // 2 - 1, db_mid, 0.0) + jnp.where(row == CHUNK - 1, db_last, 0.0)
            dlogf = _dot(_tri(False).astype(F32), db, 1, 0, precision=HI)
            dfv = dlogf / w["f"] - dkf
            sg = w["sg"]
            df_ref[rows, :] = (dfv * (1.0 - lb) * sg * (1.0 - sg)).astype(BF16)
            dlb_ref[...] += _rowsum8(dfv * (1.0 - sg))
            sq = w["sq"]
            dq_ref[rows, :] = (dqf * (HD ** -0.5) * (sq * (1.0 + q * (1.0 - sq)))).astype(BF16)
            return carry

        lax.fori_loop(0, nc, chunk, 0)

    rev = lambda j: nb - 1 - j
    col = lambda base: pl.BlockSpec((HB, HD), lambda h, j: (rev(j), base // HD + h))
    out = pl.BlockSpec((HB, HD), lambda h, j: (rev(j), h))
    return pl.pallas_call(
        body, name="hgrn_bwd",
        out_shape=(jax.ShapeDtypeStruct((t, HG_W), BF16),) * 4
        + (jax.ShapeDtypeStruct((8, HG_W), F32), jax.ShapeDtypeStruct((8, HD), F32)),
        grid=(4, nb),
        in_specs=[col(ZQH), col(ZFH), col(ZIH), col(ZGH),
                  pl.BlockSpec((2, HD), lambda h, j: (0, h)), pl.BlockSpec((1, HD), lambda h, j: (0, 0)),
                  out, pl.BlockSpec((1, nc, HD, HD), lambda h, j: (h, rev(j), 0, 0)),
                  pl.BlockSpec((HB, HD), lambda h, j: (rev(j), SWA_W // HD + h))],
        out_specs=(out, out, out, out, pl.BlockSpec((8, HD), lambda h, j: (0, h)),
                   pl.BlockSpec((8, HD), lambda h, j: (0, 0))),
        scratch_shapes=[pltpu.VMEM((HD, HD), F32)],
        compiler_params=_params(dimension_semantics=("arbitrary", "arbitrary")),
    )(z, z, z, z, hgrn_lb, onorm, o_save, sprev, dymix)


def _assemble_dz(dza, dq, df, di, dg, t):
    tb = min(512, t)

    def body(a_ref, q_ref, f_ref, i_ref, g_ref, o_ref):
        o_ref[:, 0:ZQH] = a_ref[...]
        o_ref[:, ZQH:ZFH] = q_ref[...]
        o_ref[:, ZFH:ZIH] = f_ref[...]
        o_ref[:, ZIH:ZGH] = i_ref[...]
        o_ref[:, ZGH:D_IN] = g_ref[...]

    row = lambda w: pl.BlockSpec((tb, w), lambda i: (i, 0))
    return pl.pallas_call(
        body, name="assemble_dz", out_shape=jax.ShapeDtypeStruct((t, D_IN), BF16), grid=(t // tb,),
        in_specs=[row(ZQH), row(HG_W), row(HG_W), row(HG_W), row(HG_W)], out_specs=row(D_IN),
        compiler_params=_params(),
    )(dza, dq, df, di, dg)


XB = 512


def _xattn_fwd(q, k, v, t):
    tb = min(XB, t)

    def body(q_ref, k_ref, v_ref, o_ref):
        for h in range(XH):
            cols = slice(XD * h, XD * (h + 1))
            s = _dot(q_ref[:, cols], k_ref[:, cols], 1, 1) * (XD ** -0.5)
            p = jnp.exp(s - jnp.max(s, axis=-1, keepdims=True))
            l = jnp.sum(p, axis=-1, keepdims=True)
            o_ref[:, cols] = (_dot(p.astype(BF16), v_ref[:, cols], 1, 0) * (1.0 / l)).astype(BF16)

    row = pl.BlockSpec((tb, D), lambda i: (i, 0))
    mem = pl.BlockSpec(k.shape, lambda i: (0, 0))
    return pl.pallas_call(
        body, name="xattn_fwd", out_shape=jax.ShapeDtypeStruct((t, D), BF16), grid=(t // tb,),
        in_specs=[row, mem, mem], out_specs=row, compiler_params=_params(),
    )(q, k, v)


def _xattn_bwd(q, k, v, do, t):
    tb = min(XB, t)

    def body(q_ref, k_ref, v_ref, do_ref, dq_ref, dk_ref, dv_ref):
        @pl.when(pl.program_id(0) == 0)
        def _():
            dk_ref[...] = jnp.zeros_like(dk_ref)
            dv_ref[...] = jnp.zeros_like(dv_ref)

        for h in range(XH):
            cols = slice(XD * h, XD * (h + 1))
            qh, kh, vh, doh = q_ref[:, cols], k_ref[:, cols], v_ref[:, cols], do_ref[:, cols]
            s = _dot(qh, kh, 1, 1) * (XD ** -0.5)
            p = jnp.exp(s - jnp.max(s, axis=-1, keepdims=True))
            p = p * (1.0 / jnp.sum(p, axis=-1, keepdims=True))
            dp = _dot(doh, vh, 1, 1)
            ds = (p * (dp - jnp.sum(p * dp, axis=-1, keepdims=True)) * (XD ** -0.5)).astype(BF16)
            dq_ref[:, cols] = _dot(ds, kh, 1, 0).astype(BF16)
            dk_ref[:, cols] += _dot(ds, qh, 0, 0)
            dv_ref[:, cols] += _dot(p.astype(BF16), doh, 0, 0)

    row = pl.BlockSpec((tb, D), lambda i: (i, 0))
    mem = pl.BlockSpec(k.shape, lambda i: (0, 0))
    return pl.pallas_call(
        body, name="xattn_bwd",
        out_shape=(jax.ShapeDtypeStruct((t, D), BF16), jax.ShapeDtypeStruct(k.shape, F32),
                   jax.ShapeDtypeStruct(k.shape, F32)),
        grid=(t // tb,), in_specs=[row, mem, mem, row], out_specs=(row, mem, mem),
        compiler_params=_params(dimension_semantics=("arbitrary",)),
    )(q, k, v, do)


def _mem_gain_bwd(dm, mem, *, name):
    def body(dm_ref, m_ref, dg_ref):
        m_ = m_ref[...]
        dg_ref[...] = _rowsum8(dm_ref[...] * (m_ * _rstd(m_)))

    return pl.pallas_call(body, name=name, out_shape=jax.ShapeDtypeStruct((8, D), F32),
                          compiler_params=_params())(dm, mem)


FM, FN = 512, 1408


def _ffn_up(u, wgt, wut, t):
    tm = min(FM, t)

    def body(u_ref, wg_ref, wu_ref, g_ref, up_ref, a_ref):
        u_ = u_ref[...]
        g = _dot(u_, wg_ref[...], 1, 1)
        up = _dot(u_, wu_ref[...], 1, 1)
        g_ref[...] = g.astype(BF16)
        up_ref[...] = up.astype(BF16)
        a_ref[...] = (g * _sig(g) * up).astype(BF16)

    w = pl.BlockSpec((FN, D), lambda i, j: (j, 0))
    o = pl.BlockSpec((tm, FN), lambda i, j: (i, j))
    return pl.pallas_call(
        body, name="ffn_up", out_shape=(jax.ShapeDtypeStruct((t, D_FF), BF16),) * 3,
        grid=(t // tm, D_FF // FN), in_specs=[pl.BlockSpec((tm, D), lambda i, j: (i, 0)), w, w],
        out_specs=(o, o, o), compiler_params=_params(),
    )(u, wgt, wut)


def _ffn_down_bwd(dy, wd, gate, up, t, dep=None):
    tm = min(FM, t)
    deps = [] if dep is None else [dep]

    def body(dy_ref, w_ref, g_ref, up_ref, *rest):
        dg_ref, dup_ref = rest[len(deps):]
        da = _dot(dy_ref[...], w_ref[...], 1, 1)
        g = g_ref[...].astype(F32)
        sg = _sig(g)
        dup_ref[...] = (da * g * sg).astype(BF16)
        dg_ref[...] = (da * up_ref[...].astype(F32) * (sg * (1.0 + g * (1.0 - sg)))).astype(BF16)

    o = pl.BlockSpec((tm, FN), lambda i, j: (i, j))
    return pl.pallas_call(
        body, name="ffn_down_bwd", out_shape=(jax.ShapeDtypeStruct((t, D_FF), BF16),) * 2,
        grid=(t // tm, D_FF // FN),
        in_specs=[pl.BlockSpec((tm, D), lambda i, j: (i, 0)), pl.BlockSpec((FN, D), lambda i, j: (j, 0)), o, o]
        + [ANY_SPEC] * len(deps),
        out_specs=(o, o), compiler_params=_params(),
    )(dy, wd, gate, up, *deps)


def _local_step(x, mem, target, fetch, sm, emit=None):
    t = x.shape[0]
    w, gw = {}, {}

    def out(key, g):
        gw[key] = g
        return None if emit is None else emit(key, g)
    u1 = _prenorm(x, sm["g_mix_pre"], name="prenorm_mix")
    w["winT"] = fetch("winT", u1)
    z = _mm(u1, w["winT"], tb=True, out_dtype=F32, tm=1024, tn=1408, name="mm_z")
    ymix, lse = _swa_fwd(z, sm["sinks"], t)
    ymix, o_h, sprev = _hgrn_fwd(z, sm["hgrn_lb"], sm["hgrn_onorm"], ymix, t)
    w["wout"] = fetch("wout", ymix)
    y1 = _mm(ymix, w["wout"], out_dtype=F32, tm=1024, tn=1024, name="mm_y1")
    h1, u2 = _post_pre(x, y1, sm["g_mix_post"], sm["g_x_pre"], name="post_mix")
    mn = _prenorm(mem, sm["g_mem"], name="prenorm_mem")
    for key in ("wq", "wk", "wv"):
        w[key] = fetch(key, u2)
    qx = _mm(u2, w["wq"], out_dtype=BF16, tm=1024, tn=1024, name="mm_qx")
    kx = _mm(mn, w["wk"], out_dtype=BF16, tm=1024, tn=1024, name="mm_kx")
    vx = _mm(mn, w["wv"], out_dtype=BF16, tm=1024, tn=1024, name="mm_vx")
    ox = _xattn_fwd(qx, kx, vx, t)
    w["wo"] = fetch("wo", ox)
    y2 = _mm(ox, w["wo"], out_dtype=F32, tm=1024, tn=1024, name="mm_y2")
    h2, u3 = _post_pre(h1, y2, sm["g_x_post"], sm["g_ffn_pre"], name="post_x")
    w["wgT"], w["wuT"] = fetch("wgT", u3), fetch("wuT", u3)
    gate, up, act = _ffn_up(u3, w["wgT"], w["wuT"], t)
    w["wd"] = fetch("wd", act)
    y3 = _mm(act, w["wd"], out_dtype=F32, tm=1024, tn=1024, tk=1408, name="mm_y3")
    sq, dh3, dy3, dg_ffn_post = _final_loss(h2, y3, sm["g_ffn_post"], target, name="final_loss")
    dep = out("wd", _mm(act, dy3, ta=True, out_dtype=BF16, tm=1408, tn=1024, tk=512, name="mm_gwd"))
    dgate, dup = _ffn_down_bwd(dy3, w["wd"], gate, up, t, dep=dep)
    dep = out("wgT", _mm(dgate, u3, ta=True, out_dtype=BF16, tm=1408, tn=1024, tk=512, name="mm_gwg"))
    dep = out("wuT", _mm(dup, u3, ta=True, out_dtype=BF16, tm=1408, tn=1024, tk=512, name="mm_gwu", dep=dep))
    du3 = _mm2(dgate, w["wgT"], dup, w["wuT"], tm=512, tk=1408, name="mm_du3", dep=dep)
    dh2, dy2, dg_ffn_pre, dg_x_post = _post_pre_bwd(dh3, du3, h2, y2, sm["g_x_post"], sm["g_ffn_pre"], name="post_x_bwd")
    dep = out("wo", _mm(ox, dy2, ta=True, out_dtype=BF16, tm=1024, tn=1024, tk=512, name="mm_gwo"))
    dox = _mm(dy2, w["wo"], tb=True, out_dtype=BF16, tm=1024, tn=1024, name="mm_dox", dep=dep)
    dqx, dkx, dvx = _xattn_bwd(qx, kx, vx, dox, t)
    dep = out("wq", _mm(u2, dqx, ta=True, out_dtype=BF16, tm=1024, tn=1024, tk=512, name="mm_gwq"))
    dep = out("wk", _mm(mn, dkx, ta=True, out_dtype=BF16, tm=1024, tn=1024, name="mm_gwk", dep=dep))
    dep = out("wv", _mm(mn, dvx, ta=True, out_dtype=BF16, tm=1024, tn=1024, name="mm_gwv", dep=dep))
    du2 = _mm(dqx, w["wq"], tb=True, out_dtype=F32, tm=1024, tn=1024, name="mm_du2", dep=dep)
    dmn = _mm2(dkx, w["wk"], dvx, w["wv"], tb=True, tm=256, tk=1024, name="mm_dmn")
    dg_mem = _mem_gain_bwd(dmn, mem, name="mem_gain_bwd")
    dh1, dy1, dg_x_pre, dg_mix_post = _post_pre_bwd(dh2, du2, h1, y1, sm["g_mix_post"], sm["g_x_pre"], name="post_mix_bwd")
    dep = out("wout", _mm(ymix, dy1, ta=True, out_dtype=BF16, tm=1024, tn=1024, tk=512, name="mm_gwout"))
    dymix = _mm(dy1, w["wout"], tb=True, out_dtype=BF16, tm=1024, tn=1024, name="mm_dymix", dep=dep)
    dza, dsinks = _swa_bwd(z, sm["sinks"], ymix, lse, dymix, t)
    dqh, dfh, dih, dgh, dlb, donorm = _hgrn_bwd(z, sm["hgrn_lb"], sm["hgrn_onorm"], o_h, sprev, dymix, t)
    dz = _assemble_dz(dza, dqh, dfh, dih, dgh, t)
    dep = out("winT", _mm(dz, u1, ta=True, out_dtype=BF16, tm=1408, tn=1024, tk=512, name="mm_gwin"))
    du1 = _mm(dz, w["winT"], out_dtype=F32, tm=512, tn=1024, tk=1408, name="mm_du1", dep=dep)
    grad_x, dg_mix_pre = _pre_bwd(dh1, du1, x, sm["g_mix_pre"], name="pre_mix_bwd")
    parts = dict(g_mix_pre=dg_mix_pre, g_mix_post=dg_mix_post, g_mem=dg_mem, g_x_pre=dg_x_pre,
                 g_x_post=dg_x_post, g_ffn_pre=dg_ffn_pre, g_ffn_post=dg_ffn_post,
                 hgrn_onorm=donorm, hgrn_lb=dlb, sinks=dsinks, sq=sq)
    return grad_x, gw, parts


def _position():
    return lax.axis_index("x"), lax.axis_index("y"), lax.axis_index("c")


def _peer(pos, k):
    x, y, c = pos
    return (1 - x if k & 4 else x, 1 - y if k & 2 else y, 1 - c if k & 1 else c)


def _linear(pos):
    x, y, c = pos
    return 4 * x + 2 * y + c


HBM_SPEC = pl.BlockSpec(memory_space=pltpu.HBM)
SEM_SPEC = pl.BlockSpec(memory_space=pltpu.SEMAPHORE)
DATAFLOW = pltpu.SideEffectType.DATAFLOW_SIDE_EFFECTING
SEND_ORDER = (1, 2, 4, 3, 5, 6, 7)


def _in_hbm(a):
    return pltpu.with_memory_space_constraint(a, pltpu.HBM)


def _gather_start(shards):
    n = len(shards)
    rows = [s.shape[0] for s in shards]
    me_lin = _linear(_position())
    lands = [lax.dynamic_update_slice_in_dim(lax.empty((N_DEV * s.shape[0], s.shape[1]), s.dtype), s,
                                             me_lin * s.shape[0], 0) for s in shards]

    def body(*refs):
        srcs, land = refs[:n], refs[n:2 * n]
        send_sems, recv_sems = refs[2 * n:3 * n], refs[3 * n:4 * n]
        me = _position()
        for a in range(n):
            mine = land[a].at[pl.ds(_linear(me) * rows[a], rows[a]), :]
            for k in SEND_ORDER:
                pltpu.make_async_remote_copy(
                    src_ref=srcs[a], dst_ref=mine, send_sem=send_sems[a].at[k - 1], recv_sem=recv_sems[a].at[k - 1],
                    device_id=_peer(me, k), device_id_type=MESH).start()

    sems = tuple(pltpu.SemaphoreType.DMA((N_DEV - 1,)) for _ in range(2 * n))
    res = pl.pallas_call(
        body, name="weights_send",
        out_shape=sems + tuple(pltpu.HBM(s.shape, s.dtype) for s in shards)
        + tuple(pltpu.HBM(l.shape, l.dtype) for l in lands),
        in_specs=(HBM_SPEC,) * (2 * n), out_specs=(SEM_SPEC,) * (2 * n) + (HBM_SPEC,) * (2 * n),
        input_output_aliases={i: 2 * n + i for i in range(2 * n)},
        compiler_params=pltpu.CompilerParams(has_side_effects=DATAFLOW),
    )(*[_in_hbm(s) for s in shards], *[_in_hbm(l) for l in lands])
    return [(res[a], res[n + a], res[2 * n + a], res[3 * n + a]) for a in range(n)]


def _gather_wait(send_sems, recv_sems, shard_thru, land_thru, after, *, name):
    r = shard_thru.shape[0]

    def body(src_ref, land_ref, send_sems, recv_sems, after_ref, src_dead, got_ref):
        del after_ref, src_dead, got_ref
        me = _position()
        for k in SEND_ORDER:
            peer = _peer(me, k)
            copy = pltpu.make_async_remote_copy(
                src_ref=src_ref, dst_ref=land_ref.at[pl.ds(_linear(peer) * r, r), :],
                send_sem=send_sems.at[k - 1], recv_sem=recv_sems.at[k - 1],
                device_id=peer, device_id_type=MESH)
            copy.wait_send()
            copy.wait_recv()

    return pl.pallas_call(
        body, name=name,
        out_shape=(pltpu.HBM(shard_thru.shape, shard_thru.dtype), pltpu.HBM(land_thru.shape, land_thru.dtype)),
        in_specs=(HBM_SPEC, HBM_SPEC, SEM_SPEC, SEM_SPEC, ANY_SPEC),
        out_specs=(HBM_SPEC, HBM_SPEC), input_output_aliases={0: 0, 1: 1},
        compiler_params=pltpu.CompilerParams(has_side_effects=DATAFLOW),
    )(shard_thru, land_thru, send_sems, recv_sems, after)[1]


def _exchange_start(g, *, name):
    r = g.shape[0] // N_DEV
    land_shape = (N_DEV - 1, r, g.shape[1])

    def body(g_ref, land_ref, send_sems, recv_sems, g_thru, land_thru):
        del g_thru, land_thru
        me = _position()
        for k in SEND_ORDER:
            peer = _peer(me, k)
            pltpu.make_async_remote_copy(
                src_ref=g_ref.at[pl.ds(_linear(peer) * r, r), :], dst_ref=land_ref.at[k - 1],
                send_sem=send_sems.at[k - 1], recv_sem=recv_sems.at[k - 1],
                device_id=peer, device_id_type=MESH).start()

    return pl.pallas_call(
        body, name=name,
        out_shape=(pltpu.SemaphoreType.DMA((N_DEV - 1,)), pltpu.SemaphoreType.DMA((N_DEV - 1,)),
                   pltpu.HBM(g.shape, g.dtype), pltpu.HBM(land_shape, g.dtype)),
        in_specs=(HBM_SPEC, HBM_SPEC), out_specs=(SEM_SPEC, SEM_SPEC, HBM_SPEC, HBM_SPEC),
        input_output_aliases={0: 2, 1: 3},
        compiler_params=pltpu.CompilerParams(has_side_effects=DATAFLOW),
    )(_in_hbm(g), _in_hbm(lax.empty(land_shape, g.dtype)))


def _exchange_wait(send_sems, recv_sems, g_thru, land_thru, after, *, name):
    r = land_thru.shape[1]

    def body(g_ref, land_ref, send_sems, recv_sems, after_ref, g_dead, got_ref):
        del after_ref, g_dead, got_ref
        me = _position()
        for k in SEND_ORDER:
            peer = _peer(me, k)
            copy = pltpu.make_async_remote_copy(
                src_ref=g_ref.at[pl.ds(_linear(peer) * r, r), :], dst_ref=land_ref.at[k - 1],
                send_sem=send_sems.at[k - 1], recv_sem=recv_sems.at[k - 1],
                device_id=peer, device_id_type=MESH)
            copy.wait_send()
            copy.wait_recv()

    return pl.pallas_call(
        body, name=name,
        out_shape=(pltpu.HBM(g_thru.shape, g_thru.dtype), pltpu.HBM(land_thru.shape, land_thru.dtype)),
        in_specs=(HBM_SPEC, HBM_SPEC, SEM_SPEC, SEM_SPEC, pl.BlockSpec(memory_space=pl.ANY)),
        out_specs=(HBM_SPEC, HBM_SPEC), input_output_aliases={0: 0, 1: 1},
        compiler_params=pltpu.CompilerParams(has_side_effects=DATAFLOW),
    )(g_thru, land_thru, send_sems, recv_sems, after)


def _adamw_math(w, g, m, v):
    m = B1 * m + (1.0 - B1) * g
    v = B2 * v + (1.0 - B2) * (g * g)
    delta = -LR * ((m / C1) / (jnp.sqrt(v / C2) + AEPS) + WD * w)
    return delta, m, v


def _sum_adamw(own, land, w, m, v, *, name):
    def body(own_ref, land_ref, w_ref, m_ref, v_ref, g_ref, d_ref, nm_ref, nv_ref):
        g = own_ref[...].astype(F32)
        for s in range(N_DEV - 1):
            g = g + land_ref[s].astype(F32)
        g_ref[...] = g
        d_ref[...], nm_ref[...], nv_ref[...] = _adamw_math(w_ref[...], g, m_ref[...], v_ref[...])

    return pl.pallas_call(body, name=name, out_shape=(jax.ShapeDtypeStruct(w.shape, F32),) * 4,
                          compiler_params=_params())(own, land, w, m, v)


SMALL = ("g_mix_pre", "g_mix_post", "g_mem", "g_x_pre", "g_x_post", "g_ffn_pre", "g_ffn_post",
         "hgrn_onorm", "hgrn_lb", "sinks")
SMALL_W = dict(hgrn_onorm=HD, hgrn_lb=HG_W, sinks=8)
SQ_ROW = len(SMALL)
PACK_ROWS = 16


def _small_allreduce(parts):
    ns = len(SMALL)

    def body(*refs):
        part, tot_ref = refs[:ns + 1], refs[ns + 1]
        gath, send_sems, recv_sems = refs[ns + 2:]
        me = _position()
        mine = gath.at[_linear(me)]
        mine[...] = jnp.zeros((PACK_ROWS, D), F32)
        for r, name in enumerate(SMALL):
            wd = SMALL_W.get(name, D)
            mine[r:r + 1, 0:wd] = jnp.sum(part[r][...], axis=0, keepdims=True)[:, 0:wd]
        sq = jnp.sum(part[ns][...]) * (0.5 / D)
        mine[SQ_ROW:SQ_ROW + 1, :] = jnp.full((1, D), sq, F32)

        def copy(k):
            peer = _peer(me, k)
            return pltpu.make_async_remote_copy(
                src_ref=mine, dst_ref=mine, send_sem=send_sems.at[k - 1], recv_sem=recv_sems.at[k - 1],
                device_id=peer, device_id_type=MESH)

        def arrival(k):
            slot = gath.at[_linear(_peer(me, k))]
            return pltpu.make_async_remote_copy(
                src_ref=slot, dst_ref=slot, send_sem=send_sems.at[k - 1], recv_sem=recv_sems.at[k - 1],
                device_id=_peer(me, k), device_id_type=MESH)

        sent = [copy(k) for k in range(1, 8)]
        for cp in sent:
            cp.start()
        for k in range(1, 8):
            arrival(k).wait_recv()
        for cp in sent:
            cp.wait_send()
        tot = gath[0]
        for s in range(1, N_DEV):
            tot = tot + gath[s]
        tot_ref[...] = tot

    return pl.pallas_call(
        body, name="small_allreduce", out_shape=jax.ShapeDtypeStruct((PACK_ROWS, D), F32),
        scratch_shapes=[pltpu.VMEM((N_DEV, PACK_ROWS, D), F32), pltpu.SemaphoreType.DMA((7,)),
                        pltpu.SemaphoreType.DMA((7,))],
        compiler_params=_params(has_side_effects=True),
    )(*[parts[n] for n in SMALL], parts["sq"])


def _small_update(tot, sm, m_sm, v_sm):
    ns = len(SMALL)

    def body(*refs):
        tot = refs[0][...]
        w_refs, m_refs, v_refs = refs[1:ns + 1], refs[ns + 1:2 * ns + 1], refs[2 * ns + 1:3 * ns + 1]
        outs = refs[3 * ns + 1:]
        loss_ref = outs[0]
        g_out, d_out = outs[1:ns + 1], outs[ns + 1:2 * ns + 1]
        nm_out, nv_out = outs[2 * ns + 1:3 * ns + 1], outs[3 * ns + 1:4 * ns + 1]
        loss_ref[...] = tot[SQ_ROW:SQ_ROW + 1, 0:1]
        for r, name in enumerate(SMALL):
            wd = SMALL_W.get(name, D)
            g = tot[r:r + 1, 0:wd]
            w = w_refs[r][...]
            if name == "hgrn_lb":
                mx = jnp.maximum(w[0:1], w[1:2])
                e0, e1 = jnp.exp(w[0:1] - mx), jnp.exp(w[1:2] - mx)
                lb0 = e0 / (e0 + e1)
                g0 = g * lb0 * (1.0 - lb0)
                for i, gi in enumerate((g0, -g0)):
                    d, nm, nv = _adamw_math(w[i:i + 1], gi, m_refs[r][i:i + 1, :], v_refs[r][i:i + 1, :])
                    g_out[r][i:i + 1, :] = gi
                    d_out[r][i:i + 1, :], nm_out[r][i:i + 1, :], nv_out[r][i:i + 1, :] = d, nm, nv
            else:
                d, nm, nv = _adamw_math(w, g, m_refs[r][...], v_refs[r][...])
                g_out[r][...] = g
                d_out[r][...], nm_out[r][...], nv_out[r][...] = d, nm, nv

    shapes = [jax.ShapeDtypeStruct(sm[n].shape, F32) for n in SMALL]
    res = pl.pallas_call(
        body, name="small_update", out_shape=tuple([jax.ShapeDtypeStruct((1, 1), F32)] + shapes * 4),
        compiler_params=_params(),
    )(tot, *[sm[n] for n in SMALL], *[m_sm[n] for n in SMALL], *[v_sm[n] for n in SMALL])
    groups = [dict(zip(SMALL, res[1 + i * ns:1 + (i + 1) * ns])) for i in range(4)]
    return res[0], groups[0], groups[1], groups[2], groups[3]


BIG = ("w_in", "w_gate", "w_up", "w_down", "w_out", "wq_x", "wk_x", "wv_x", "wo_x")
BIG_KEY = dict(w_in="winT", w_gate="wgT", w_up="wuT", w_down="wd", w_out="wout", wq_x="wq", wk_x="wk",
               wv_x="wv", wo_x="wo")
TRANSPOSED = ("w_in", "w_gate", "w_up")
WEIGHTS = ("w_in", "sinks", "hgrn_lb", "hgrn_onorm", "w_out", "g_mix_pre", "g_mix_post", "g_mem", "g_x_pre",
           "g_x_post", "wq_x", "wk_x", "wv_x", "wo_x", "g_ffn_pre", "g_ffn_post", "w_gate", "w_up", "w_down")


def kernel(x, mem, w_in, sinks, hgrn_lb, hgrn_onorm, w_out, g_mix_pre, g_mix_post, g_mem, g_x_pre, g_x_post, wq_x, wk_x, wv_x, wo_x, g_ffn_pre, g_ffn_post, w_gate, w_up, w_down, loss_target, m_w_in, m_sinks, m_hgrn_lb, m_hgrn_onorm, m_w_out, m_g_mix_pre, m_g_mix_post, m_g_mem, m_g_x_pre, m_g_x_post, m_wq_x, m_wk_x, m_wv_x, m_wo_x, m_g_ffn_pre, m_g_ffn_post, m_w_gate, m_w_up, m_w_down, v_w_in, v_sinks, v_hgrn_lb, v_hgrn_onorm, v_w_out, v_g_mix_pre, v_g_mix_post, v_g_mem, v_g_x_pre, v_g_x_post, v_wq_x, v_wk_x, v_wv_x, v_wo_x, v_g_ffn_pre, v_g_ffn_post, v_w_gate, v_w_up, v_w_down):
    given = dict(locals())
    wts = {n: given[n] for n in WEIGHTS}
    ms = {n: given["m_" + n] for n in WEIGHTS}
    vs = {n: given["v_" + n] for n in WEIGHTS}

    def mat(a, name):
        a = a[0]
        return a.T if name in TRANSPOSED else a

    order = ("w_in", "w_out", "wq_x", "wk_x", "wv_x", "wo_x", "w_gate", "w_up", "w_down")
    flying = dict(zip(order, _gather_start([mat(wts[n], n).astype(BF16) for n in order])))
    name_of = {k: n for n, k in BIG_KEY.items()}

    def fetch(key, after):
        return _gather_wait(*flying[name_of[key]], after, name="weights_recv_" + name_of[key])

    sm = {n: wts[n] for n in SMALL}
    started = {}

    def emit(key, g):
        started[name_of[key]] = _exchange_start(g, name="grad_send_" + name_of[key])
        return started[name_of[key]][2]

    grad_x, _, parts = _local_step(x[0], mem[0], loss_target[0], fetch, sm, emit)
    me_lin = _linear(_position())
    grads, deltas, new_m, new_v = {}, {}, {}, {}
    after = grad_x
    for n in ("w_down", "w_gate", "w_up", "wo_x", "wq_x", "wk_x", "wv_x", "w_out", "w_in"):
        g_all, land = _exchange_wait(*started[n], after, name="grad_recv_" + n)
        r = land.shape[1]
        own = lax.dynamic_slice_in_dim(g_all, me_lin * r, r, 0)
        res = _sum_adamw(own, land, mat(wts[n], n), mat(ms[n], n), mat(vs[n], n), name="adamw_" + n)
        after = res[1]
        if n in TRANSPOSED:
            res = [a.T for a in res]
        grads[n], deltas[n], new_m[n], new_v[n] = [a[None] for a in res]
    loss, g_s, d_s, m_s, v_s = _small_update(_small_allreduce(parts), sm, {n: ms[n] for n in SMALL},
                                             {n: vs[n] for n in SMALL})
    grads.update(g_s), deltas.update(d_s), new_m.update(m_s), new_v.update(v_s)
    return (loss[0, 0], grad_x[None], *[grads[n] for n in WEIGHTS], *[deltas[n] for n in WEIGHTS],
            *[new_m[n] for n in WEIGHTS], *[new_v[n] for n in WEIGHTS])
```

```python
import functools

import jax
import jax.numpy as jnp
from jax import lax
from jax.experimental import pallas as pl
from jax.experimental.pallas import tpu as pltpu

F32 = jnp.float32
BF16 = jnp.bfloat16

D = 1024
D_IN = 2816
D_FF = 2816
CHUNK = 64
SWA_W = 512
KV_W = 128
HG_W = 512
HD = 128
ZQH, ZFH, ZIH, ZGH = 768, 1280, 1792, 2304
XH, XD = 4, 256
EPS = 1e-6
NEG = -1e30
N_DEV = 8
MESH = pl.DeviceIdType.MESH

LR, B1, B2, AEPS, WD, STEP = 0.001, 0.9, 0.999, 1e-08, 0.01, 10
C1 = 1.0 - B1 ** STEP
C2 = 1.0 - B2 ** STEP

VMEM_LIMIT = 56 * 1024 * 1024


def _params(**kw):
    return pltpu.CompilerParams(vmem_limit_bytes=VMEM_LIMIT, **kw)


def _sig(x):
    return 1.0 / (1.0 + jnp.exp(-x))


def _rowsum8(x):
    r, w = x.shape
    return jnp.sum(x.reshape(r // 8, 8, w), axis=0)


def _dot(a, b, ca, cb, precision=None):
    return lax.dot_general(a, b, (((ca,), (cb,)), ((), ())), preferred_element_type=F32,
                           precision=precision)


ANY_SPEC = pl.BlockSpec(memory_space=pl.ANY)


def _mm(a, b, *, ta=False, tb=False, out_dtype, tm, tn, tk=None, name, dep=None):
    m = a.shape[1] if ta else a.shape[0]
    k = a.shape[0] if ta else a.shape[1]
    n = b.shape[0] if tb else b.shape[1]
    tm, tn = min(tm, m), min(tn, n)
    tk = k if tk is None else min(tk, k)
    nk = k // tk
    assert m % tm == 0 and n % tn == 0 and k % tk == 0, (name, m, n, k, tm, tn, tk)
    a_spec = pl.BlockSpec((tk, tm), lambda i, j, kk: (kk, i)) if ta else pl.BlockSpec((tm, tk), lambda i, j, kk: (i, kk))
    b_spec = pl.BlockSpec((tn, tk), lambda i, j, kk: (j, kk)) if tb else pl.BlockSpec((tk, tn), lambda i, j, kk: (kk, j))
    ca, cb = (0 if ta else 1), (1 if tb else 0)

    deps = [] if dep is None else [dep]

    def body(a_ref, b_ref, *rest):
        o_ref, acc = rest[len(deps)], rest[len(deps) + 1:]
        p = _dot(a_ref[...].astype(BF16), b_ref[...].astype(BF16), ca, cb)
        if nk == 1:
            o_ref[...] = p.astype(out_dtype)
        else:
            acc_ref, = acc
            kk = pl.program_id(2)

            @pl.when(kk == 0)
            def _():
                acc_ref[...] = p

            @pl.when(kk > 0)
            def _():
                acc_ref[...] += p

            @pl.when(kk == nk - 1)
            def _():
                o_ref[...] = acc_ref[...].astype(out_dtype)

    return pl.pallas_call(
        body, name=name, out_shape=jax.ShapeDtypeStruct((m, n), out_dtype),
        grid=(m // tm, n // tn, nk), in_specs=[a_spec, b_spec] + [ANY_SPEC] * len(deps),
        out_specs=pl.BlockSpec((tm, tn), lambda i, j, kk: (i, j)),
        scratch_shapes=[pltpu.VMEM((tm, tn), F32)] if nk > 1 else [],
        compiler_params=_params(dimension_semantics=("parallel", "parallel", "arbitrary")),
    )(a, b, *deps)


def _mm2(a1, b1, a2, b2, *, tb=False, tm, tk, name, dep=None):
    m, k = a1.shape
    n = b1.shape[0] if tb else b1.shape[1]
    tm, tk = min(tm, m), min(tk, k)
    nk = k // tk
    assert m % tm == 0 and k % tk == 0
    cb = 1 if tb else 0
    deps = [] if dep is None else [dep]

    def body(a1_ref, b1_ref, a2_ref, b2_ref, *rest):
        o_ref = rest[len(deps)]
        p = (_dot(a1_ref[...].astype(BF16), b1_ref[...], 1, cb)
             + _dot(a2_ref[...].astype(BF16), b2_ref[...], 1, cb))
        kk = pl.program_id(1)

        @pl.when(kk == 0)
        def _():
            o_ref[...] = p

        @pl.when(kk > 0)
        def _():
            o_ref[...] += p

    a_spec = pl.BlockSpec((tm, tk), lambda i, kk: (i, kk))
    b_spec = pl.BlockSpec((n, tk), lambda i, kk: (0, kk)) if tb else pl.BlockSpec((tk, n), lambda i, kk: (kk, 0))
    return pl.pallas_call(
        body, name=name, out_shape=jax.ShapeDtypeStruct((m, n), F32),
        grid=(m // tm, nk), in_specs=[a_spec, b_spec, a_spec, b_spec] + [ANY_SPEC] * len(deps),
        out_specs=pl.BlockSpec((tm, n), lambda i, kk: (i, 0)),
        compiler_params=_params(dimension_semantics=("parallel", "arbitrary")),
    )(a1, b1, a2, b2, *deps)


def _rstd(x):
    return lax.rsqrt(jnp.mean(x * x, axis=-1, keepdims=True) + EPS)


def _norm_bwd(xh, r, t):
    return r * (t - xh * jnp.mean(xh * t, axis=-1, keepdims=True))


def _prenorm(x, g, *, name):
    t, d = x.shape
    tb = min(512, t)

    def body(x_ref, g_ref, o_ref):
        xf = x_ref[...]
        o_ref[...] = (xf * _rstd(xf) * g_ref[...]).astype(BF16)

    return pl.pallas_call(
        body, name=name, out_shape=jax.ShapeDtypeStruct((t, d), BF16), grid=(t // tb,),
        in_specs=[pl.BlockSpec((tb, d), lambda i: (i, 0)), pl.BlockSpec((1, d), lambda i: (0, 0))],
        out_specs=pl.BlockSpec((tb, d), lambda i: (i, 0)), compiler_params=_params(),
    )(x, g)


def _post_pre(h, y, g_post, g_pre, *, name):
    t, d = h.shape
    tb = min(512, t)

    def body(h_ref, y_ref, gp_ref, gn_ref, hn_ref, u_ref):
        y_ = y_ref[...]
        hn = h_ref[...] + y_ * _rstd(y_) * gp_ref[...]
        hn_ref[...] = hn
        u_ref[...] = (hn * _rstd(hn) * gn_ref[...]).astype(BF16)

    row = pl.BlockSpec((tb, d), lambda i: (i, 0))
    vec = pl.BlockSpec((1, d), lambda i: (0, 0))
    return pl.pallas_call(
        body, name=name, out_shape=(jax.ShapeDtypeStruct((t, d), F32), jax.ShapeDtypeStruct((t, d), BF16)),
        grid=(t // tb,), in_specs=[row, row, vec, vec], out_specs=(row, row), compiler_params=_params(),
    )(h, y, g_post, g_pre)


def _final_loss(h, y, g_post, target, *, name):
    t, d = h.shape
    tb = min(512, t)

    def body(h_ref, y_ref, g_ref, t_ref, sq_ref, dh_ref, dy_ref, dg_ref):
        @pl.when(pl.program_id(0) == 0)
        def _():
            sq_ref[...] = jnp.zeros_like(sq_ref)
            dg_ref[...] = jnp.zeros_like(dg_ref)

        y_ = y_ref[...]
        r = _rstd(y_)
        yh = y_ * r
        g = g_ref[...]
        err = h_ref[...] + yh * g - t_ref[...]
        sq_ref[...] += _rowsum8(err * err)
        dh = err * (1.0 / d)
        dh_ref[...] = dh
        dg_ref[...] += _rowsum8(dh * yh)
        dy_ref[...] = _norm_bwd(yh, r, dh * g).astype(BF16)

    row = pl.BlockSpec((tb, d), lambda i: (i, 0))
    vec = pl.BlockSpec((1, d), lambda i: (0, 0))
    acc = pl.BlockSpec((8, d), lambda i: (0, 0))
    return pl.pallas_call(
        body, name=name,
        out_shape=(jax.ShapeDtypeStruct((8, d), F32), jax.ShapeDtypeStruct((t, d), F32),
                   jax.ShapeDtypeStruct((t, d), BF16), jax.ShapeDtypeStruct((8, d), F32)),
        grid=(t // tb,), in_specs=[row, row, vec, row], out_specs=(acc, row, row, acc),
        compiler_params=_params(dimension_semantics=("arbitrary",)),
    )(h, y, g_post, target)


def _post_pre_bwd(dh_out, du, hn, y, g_post, g_pre, *, name):
    t, d = hn.shape
    tb = min(512, t)

    def body(dho_ref, du_ref, hn_ref, y_ref, gp_ref, gn_ref, dh_ref, dy_ref, dgn_ref, dgp_ref):
        @pl.when(pl.program_id(0) == 0)
        def _():
            dgn_ref[...] = jnp.zeros_like(dgn_ref)
            dgp_ref[...] = jnp.zeros_like(dgp_ref)

        hn_ = hn_ref[...]
        r2 = _rstd(hn_)
        xh = hn_ * r2
        du_ = du_ref[...]
        dgn_ref[...] += _rowsum8(du_ * xh)
        dh = dho_ref[...] + _norm_bwd(xh, r2, du_ * gn_ref[...])
        dh_ref[...] = dh
        y_ = y_ref[...]
        r1 = _rstd(y_)
        yh = y_ * r1
        dgp_ref[...] += _rowsum8(dh * yh)
        dy_ref[...] = _norm_bwd(yh, r1, dh * gp_ref[...]).astype(BF16)

    row = pl.BlockSpec((tb, d), lambda i: (i, 0))
    vec = pl.BlockSpec((1, d), lambda i: (0, 0))
    acc = pl.BlockSpec((8, d), lambda i: (0, 0))
    return pl.pallas_call(
        body, name=name,
        out_shape=(jax.ShapeDtypeStruct((t, d), F32), jax.ShapeDtypeStruct((t, d), BF16),
                   jax.ShapeDtypeStruct((8, d), F32), jax.ShapeDtypeStruct((8, d), F32)),
        grid=(t // tb,), in_specs=[row, row, row, row, vec, vec], out_specs=(row, row, acc, acc),
        compiler_params=_params(dimension_semantics=("arbitrary",)),
    )(dh_out, du, hn, y, g_post, g_pre)


def _pre_bwd(dh_out, du, x, g, *, name):
    t, d = x.shape
    tb = min(512, t)
    has_res = dh_out is not None

    def body(*refs):
        if has_res:
            dho_ref, du_ref, x_ref, g_ref, dx_ref, dg_ref = refs
        else:
            du_ref, x_ref, g_ref, dx_ref, dg_ref = refs

        @pl.when(pl.program_id(0) == 0)
        def _():
            dg_ref[...] = jnp.zeros_like(dg_ref)

        x_ = x_ref[...]
        r = _rstd(x_)
        xh = x_ * r
        du_ = du_ref[...]
        dg_ref[...] += _rowsum8(du_ * xh)
        dx = _norm_bwd(xh, r, du_ * g_ref[...])
        if has_res:
            dx = dx + dho_ref[...]
        dx_ref[...] = dx

    row = pl.BlockSpec((tb, d), lambda i: (i, 0))
    vec = pl.BlockSpec((1, d), lambda i: (0, 0))
    acc = pl.BlockSpec((8, d), lambda i: (0, 0))
    ins = ([dh_out] if has_res else []) + [du, x, g]
    return pl.pallas_call(
        body, name=name,
        out_shape=(jax.ShapeDtypeStruct((t, d), F32), jax.ShapeDtypeStruct((8, d), F32)),
        grid=(t // tb,), in_specs=[row] * (len(ins) - 1) + [vec], out_specs=(row, acc),
        compiler_params=_params(dimension_semantics=("arbitrary",)),
    )(*ins)


QB = 256


def _half_mask(shape, e):
    lane = lax.broadcasted_iota(jnp.int32, shape, len(shape) - 1)
    return (lane // 64) == e


def _place(kv):
    sw = pltpu.roll(kv, 64, 1)
    m0 = _half_mask(kv.shape, 0)
    return [[jnp.where(m0, kv, 0.0).astype(BF16), jnp.where(m0, 0.0, sw).astype(BF16)],
            [jnp.where(m0, sw, 0.0).astype(BF16), jnp.where(m0, 0.0, kv).astype(BF16)]]


def _swa_valid_q(i, nq, nk):
    qc = lax.broadcasted_iota(jnp.int32, (nq, nk), 0) // CHUNK
    kc = lax.broadcasted_iota(jnp.int32, (nq, nk), 1) // CHUNK - 2
    return (kc <= qc) & (qc <= kc + 2) & (4 * i + kc >= 0)


def _swa_fwd(z, sinks, t):
    nb = t // QB

    def body(s_ref, q_ref, kp_ref, kc_ref, vp_ref, vc_ref, o_ref, lse_ref):
        i = pl.program_id(0)
        kpl = _place(jnp.concatenate([kp_ref[...], kc_ref[...]], axis=0))
        vpl = _place(jnp.concatenate([vp_ref[...], vc_ref[...]], axis=0))
        valid = _swa_valid_q(i, QB, QB + 128)
        lane = lax.broadcasted_iota(jnp.int32, (QB, 128), 1)
        lse_out = jnp.zeros((QB, 128), F32)
        for j in range(4):
            qp = q_ref[:, 128 * j:128 * (j + 1)].astype(BF16)
            acc = jnp.zeros((QB, 128), F32)
            for e in range(2):
                h = 2 * j + e
                kvh = h // 4
                qm = jnp.where(_half_mask(qp.shape, e), qp, jnp.zeros_like(qp))
                s = _dot(qm, kpl[kvh][e], 1, 1) * 0.125
                s = jnp.where(valid, s, NEG)
                sink = s_ref[0, h]
                m = jnp.maximum(jnp.max(s, axis=-1, keepdims=True), sink)
                p = jnp.exp(s - m)
                l = jnp.sum(p, axis=-1, keepdims=True) + jnp.exp(sink - m)
                acc = acc + _dot(p.astype(BF16), vpl[kvh][e], 1, 0) * (1.0 / l)
                lse_out = jnp.where(lane == h, m + jnp.log(l), lse_out)
            o_ref[:, 128 * j:128 * (j + 1)] = acc.astype(BF16)
        lse_ref[...] = lse_out

    prev = lambda c: pl.BlockSpec((128, 128), lambda i: (jnp.maximum(2 * i - 1, 0), c))
    cur = lambda c: pl.BlockSpec((QB, 128), lambda i: (i, c))
    return pl.pallas_call(
        body, name="swa_fwd",
        out_shape=(jax.ShapeDtypeStruct((t, D), BF16), jax.ShapeDtypeStruct((t, 128), F32)),
        grid=(nb,),
        in_specs=[pl.BlockSpec(memory_space=pltpu.SMEM),
                  pl.BlockSpec((QB, SWA_W), lambda i: (i, 0)), prev(4), cur(4), prev(5), cur(5)],
        out_specs=(pl.BlockSpec((QB, SWA_W), lambda i: (i, 0)), pl.BlockSpec((QB, 128), lambda i: (i, 0))),
        compiler_params=_params(),
    )(sinks, z, z, z, z, z)


def _swa_bwd(z, sinks, ymix, lse, dymix, t):
    nb = t // QB
    nq2 = QB + 128

    def body(s_ref, qc_ref, qn_ref, kp_ref, kc_ref, vp_ref, vc_ref, oc_ref, on_ref, doc_ref, don_ref,
             lc_ref, ln_ref, dz_ref, ds_ref):
        i = pl.program_id(0)

        @pl.when(i == 0)
        def _():
            ds_ref[...] = jnp.zeros_like(ds_ref)

        lane = lax.broadcasted_iota(jnp.int32, (8, 128), 1)
        kpl = _place(jnp.concatenate([kp_ref[...], kc_ref[...]], axis=0))
        vpl = _place(jnp.concatenate([vp_ref[...], vc_ref[...]], axis=0))
        valid = _swa_valid_q(i, QB, nq2)
        lse_c = lc_ref[...]
        dsink = jnp.zeros((8, 128), F32)
        for j in range(4):
            cols = slice(128 * j, 128 * (j + 1))
            qp = qc_ref[:, cols].astype(BF16)
            dop = doc_ref[:, cols]
            prod = dop.astype(F32) * oc_ref[:, cols].astype(F32)
            acc = jnp.zeros((QB, 128), F32)
            for e in range(2):
                h = 2 * j + e
                kvh = h // 4
                hm = _half_mask(qp.shape, e)
                qm = jnp.where(hm, qp, jnp.zeros_like(qp))
                dom = jnp.where(hm, dop, jnp.zeros_like(dop))
                dd = jnp.sum(jnp.where(hm, prod, 0.0), axis=-1, keepdims=True)
                lse_h = lse_c[:, h:h + 1]
                s = _dot(qm, kpl[kvh][e], 1, 1) * 0.125
                p = jnp.where(valid, jnp.exp(s - lse_h), 0.0)
                dp = _dot(dom, vpl[kvh][e], 1, 1)
                ds = p * (dp - dd) * 0.125
                acc = acc + _dot(ds.astype(BF16), kpl[kvh][e], 1, 0)
                ps = jnp.exp(s_ref[0, h] - lse_h) * dd
                dsink = dsink - jnp.where(lane == h, _rowsum8(jnp.broadcast_to(ps, (QB, 128))), 0.0)
            dz_ref[:, cols] = acc.astype(BF16)
        ds_ref[...] += dsink
        kpl, vpl = _place(kc_ref[...]), _place(vc_ref[...])
        qr = lax.broadcasted_iota(jnp.int32, (nq2, QB), 0) // CHUNK
        kr = lax.broadcasted_iota(jnp.int32, (nq2, QB), 1) // CHUNK
        valid2 = (kr <= qr) & (qr <= kr + 2) & (4 * i + qr < t // CHUNK)
        lse_a = jnp.concatenate([lse_c, ln_ref[...]], axis=0)
        dk_acc = [[jnp.zeros((QB, 128), F32) for _ in range(2)] for _ in range(2)]
        dv_acc = [[jnp.zeros((QB, 128), F32) for _ in range(2)] for _ in range(2)]
        for j in range(4):
            cols = slice(128 * j, 128 * (j + 1))
            qp = jnp.concatenate([qc_ref[:, cols], qn_ref[:, cols]], axis=0).astype(BF16)
            dop = jnp.concatenate([doc_ref[:, cols], don_ref[:, cols]], axis=0)
            op = jnp.concatenate([oc_ref[:, cols], on_ref[:, cols]], axis=0)
            prod = dop.astype(F32) * op.astype(F32)
            for e in range(2):
                h = 2 * j + e
                kvh = h // 4
                hm = _half_mask(qp.shape, e)
                qm = jnp.where(hm, qp, jnp.zeros_like(qp))
                dom = jnp.where(hm, dop, jnp.zeros_like(dop))
                dd = jnp.sum(jnp.where(hm, prod, 0.0), axis=-1, keepdims=True)
                s = _dot(qm, kpl[kvh][e], 1, 1) * 0.125
                p = jnp.where(valid2, jnp.exp(s - lse_a[:, h:h + 1]), 0.0)
                dv_acc[kvh][e] = dv_acc[kvh][e] + _dot(p.astype(BF16), dom, 0, 0)
                dp = _dot(dom, vpl[kvh][e], 1, 1)
                ds = p * (dp - dd) * 0.125
                dk_acc[kvh][e] = dk_acc[kvh][e] + _dot(ds.astype(BF16), qm, 0, 0)
        dk = dk_acc[0][0] + dk_acc[1][1] + pltpu.roll(dk_acc[0][1] + dk_acc[1][0], 64, 1)
        dv = dv_acc[0][0] + dv_acc[1][1] + pltpu.roll(dv_acc[0][1] + dv_acc[1][0], 64, 1)
        dz_ref[:, 512:640] = dk.astype(BF16)
        dz_ref[:, 640:768] = dv.astype(BF16)

    last = 2 * nb - 1
    prev = lambda c: pl.BlockSpec((128, 128), lambda i: (jnp.maximum(2 * i - 1, 0), c))
    cur = lambda w, c: pl.BlockSpec((QB, w), lambda i: (i, c))
    nxt = lambda w: pl.BlockSpec((128, w), lambda i: (jnp.minimum(2 * i + 2, last), 0))
    return pl.pallas_call(
        body, name="swa_bwd",
        out_shape=(jax.ShapeDtypeStruct((t, 768), BF16), jax.ShapeDtypeStruct((8, 128), F32)),
        grid=(nb,),
        in_specs=[pl.BlockSpec(memory_space=pltpu.SMEM),
                  cur(SWA_W, 0), nxt(SWA_W), prev(4), cur(128, 4), prev(5), cur(128, 5),
                  cur(SWA_W, 0), nxt(SWA_W), cur(SWA_W, 0), nxt(SWA_W), cur(128, 0), nxt(128)],
        out_specs=(pl.BlockSpec((QB, 768), lambda i: (i, 0)), pl.BlockSpec((8, 128), lambda i: (0, 0))),
        compiler_params=_params(dimension_semantics=("arbitrary",)),
    )(sinks, z, z, z, z, z, z, ymix, ymix, dymix, dymix, lse, lse)


HB = 256
HI = lax.Precision.HIGHEST


def _lower_bound(lb_ref):
    a = lb_ref[...]
    a0, a1 = a[0:1], a[1:2]
    mx = jnp.maximum(a0, a1)
    e0, e1 = jnp.exp(a0 - mx), jnp.exp(a1 - mx)
    return e0 / (e0 + e1)


def _tri(lower):
    r = lax.broadcasted_iota(jnp.int32, (CHUNK, CHUNK), 0)
    c = lax.broadcasted_iota(jnp.int32, (CHUNK, CHUNK), 1)
    return (c <= r) if lower else (c >= r)


def _hgrn_chunk(q, fl, lb):
    sq = _sig(q)
    qf = q * sq * (HD ** -0.5)
    sg = _sig(fl)
    f = lb + (1.0 - lb) * sg
    kf = 1.0 - f
    b = _dot(_tri(True).astype(F32), jnp.log(f), 1, 0, precision=HI)
    b_mid = b[CHUNK // 2 - 1:CHUNK // 2]
    b_last = b[CHUNK - 1:CHUNK]
    qm = qf * jnp.exp(b - b_mid)
    km = kf * jnp.exp(b_mid - b)
    kl = kf * jnp.exp(b_last - b)
    qb = qf * jnp.exp(b)
    return dict(sq=sq, qf=qf, sg=sg, f=f, kf=kf, b=b, b_mid=b_mid, b_last=b_last, qm=qm, km=km, kl=kl, qb=qb)


def _hgrn_cols(row_block):
    return [pl.BlockSpec((HB, 2 * HD), lambda j, c=base // (2 * HD) + p: (row_block(j), c))
            for base in (ZQH, ZFH, ZIH, ZGH) for p in range(2)]


def _hgrn_fwd(z, hgrn_lb, onorm, ymix, t):
    nb = t // HB
    nc = HB // CHUNK

    def body(*refs):
        zq, zf, zi, zg = refs[0:2], refs[2:4], refs[4:6], refs[6:8]
        lb_ref, on_ref, _, y_ref, o_ref, sp_ref, st_ref = refs[8:]

        @pl.when(pl.program_id(0) == 0)
        def _():
            st_ref[...] = jnp.zeros_like(st_ref)

        lb_all = _lower_bound(lb_ref)
        gn = on_ref[...]

        def chunk(c, carry):
            rows = pl.ds(pl.multiple_of(c * CHUNK, CHUNK), CHUNK)
            for h in range(4):
                p, ls, hs = h // 2, pl.ds((h % 2) * HD, HD), pl.ds(h * HD, HD)
                w = _hgrn_chunk(zq[p][rows, ls], zf[p][rows, ls], lb_all[:, h * HD:(h + 1) * HD])
                iv = zi[p][rows, ls].astype(BF16)
                st = st_ref[h]
                sp_ref[h, c] = st
                a = jnp.where(_tri(True), _dot(w["qm"].astype(BF16), w["km"].astype(BF16), 1, 1), 0.0)
                o = _dot(a.astype(BF16), iv, 1, 0) + _dot(w["qb"].astype(BF16), st.astype(BF16), 1, 1)
                st_ref[h] = st * jnp.exp(w["b_last"]) + _dot(iv, w["kl"].astype(BF16), 0, 0)
                o_ref[rows, hs] = o
                gg = zg[p][rows, ls]
                y_ref[rows, hs] = (o * _rstd(o) * gn * (gg * _sig(gg))).astype(BF16)
            return carry

        lax.fori_loop(0, nc, chunk, 0)

    return pl.pallas_call(
        body, name="hgrn_fwd",
        out_shape=(jax.ShapeDtypeStruct((t, D), BF16), jax.ShapeDtypeStruct((t, HG_W), F32),
                   jax.ShapeDtypeStruct((4, t // CHUNK, HD, HD), F32)),
        grid=(nb,),
        in_specs=_hgrn_cols(lambda j: j) + [pl.BlockSpec((2, HG_W), lambda j: (0, 0)),
                                            pl.BlockSpec((1, HD), lambda j: (0, 0)), ANY_SPEC],
        out_specs=(pl.BlockSpec((HB, HG_W), lambda j: (j, 1)),
                   pl.BlockSpec((HB, HG_W), lambda j: (j, 0)),
                   pl.BlockSpec((4, nc, HD, HD), lambda j: (0, j, 0, 0))),
        scratch_shapes=[pltpu.VMEM((4, HD, HD), F32)],
        input_output_aliases={10: 0},
        compiler_params=_params(dimension_semantics=("arbitrary",)),
    )(*[z] * 8, hgrn_lb, onorm, ymix)


def _hgrn_bwd(z, hgrn_lb, onorm, o_save, sprev, dymix, t):
    nb = t // HB
    nc = HB // CHUNK

    def body(*refs):
        zq, zf, zi, zg = refs[0:2], refs[2:4], refs[4:6], refs[6:8]
        lb_ref, on_ref, o_ref, sp_ref, dy_ref, dz_ref, dlb_ref, don_ref, dst_ref = refs[8:]

        @pl.when(pl.program_id(0) == 0)
        def _():
            dst_ref[...] = jnp.zeros_like(dst_ref)
            dlb_ref[...] = jnp.zeros_like(dlb_ref)
            don_ref[...] = jnp.zeros_like(don_ref)

        lb_all = _lower_bound(lb_ref)
        gn = on_ref[...]
        row = lax.broadcasted_iota(jnp.int32, (CHUNK, HD), 0)

        def head(h, c, rows):
            p, ls, hs = h // 2, pl.ds((h % 2) * HD, HD), pl.ds(h * HD, HD)
            lb = lb_all[:, h * HD:(h + 1) * HD]
            q = zq[p][rows, ls]
            w = _hgrn_chunk(q, zf[p][rows, ls], lb)
            iv = zi[p][rows, ls].astype(BF16)
            gg = zg[p][rows, ls]
            o = o_ref[rows, hs]
            st = sp_ref[h, c]
            dst = dst_ref[h]
            dout = dy_ref[rows, hs].astype(F32)
            sgg = _sig(gg)
            r = _rstd(o)
            oh = o * r
            dyn = dout * (gg * sgg)
            dz_ref[rows, pl.ds(3 * HG_W + h * HD, HD)] = (dout * oh * gn * (sgg * (1.0 + gg * (1.0 - sgg)))).astype(BF16)
            don_ref[...] += _rowsum8(dyn * oh)
            do = _norm_bwd(oh, r, dyn * gn).astype(BF16)
            qm, km, kl, qb = (w[n].astype(BF16) for n in ("qm", "km", "kl", "qb"))
            dstb = dst.astype(BF16)
            d_row = jnp.exp(w["b_last"])
            dqb = _dot(do, st.astype(BF16), 1, 0)
            dst_ref[h] = dst * d_row + _dot(do, qb, 0, 0)
            dd_row = jnp.sum(dst * st, axis=0, keepdims=True)
            at = jnp.where(_tri(False), _dot(km, qm, 1, 1), 0.0)
            dz_ref[rows, pl.ds(2 * HG_W + h * HD, HD)] = (_dot(at.astype(BF16), do, 1, 0) + _dot(kl, dstb, 1, 1)).astype(BF16)
            dkl = _dot(iv, dstb, 1, 0)
            da = jnp.where(_tri(True), _dot(do, iv, 1, 1), 0.0).astype(BF16)
            dat = jnp.where(_tri(False), _dot(iv, do, 1, 1), 0.0).astype(BF16)
            dqm = _dot(da, km, 1, 0)
            dkm = _dot(dat, qm, 1, 0)
            e1, e2 = jnp.exp(w["b"] - w["b_mid"]), jnp.exp(w["b_mid"] - w["b"])
            e3, e4 = jnp.exp(w["b_last"] - w["b"]), jnp.exp(w["b"])
            dqf = dqm * e1 + dqb * e4
            dkf = dkm * e2 + dkl * e3
            t_qm, t_km, t_kl = dqm * w["qm"], dkm * w["km"], dkl * w["kl"]
            db = t_qm - t_km - t_kl + dqb * w["qb"]
            db_mid = jnp.sum(t_km - t_qm, axis=0, keepdims=True)
            db_last = jnp.sum(t_kl, axis=0, keepdims=True) + dd_row * d_row
            db = db + jnp.where(row == CHUNK // 2 - 1, db_mid, 0.0) + jnp.where(row == CHUNK - 1, db_last, 0.0)
            dlogf = _dot(_tri(False).astype(F32), db, 1, 0, precision=HI)
            dfv = dlogf / w["f"] - dkf
            sg = w["sg"]
            dz_ref[rows, pl.ds(HG_W + h * HD, HD)] = (dfv * (1.0 - lb) * sg * (1.0 - sg)).astype(BF16)
            dlb_ref[:, hs] += _rowsum8(dfv * (1.0 - sg))
            sq = w["sq"]
            dz_ref[rows, hs] = (dqf * (HD ** -0.5) * (sq * (1.0 + q * (1.0 - sq)))).astype(BF16)

        def chunk(cc, carry):
            c = nc - 1 - cc
            rows = pl.ds(pl.multiple_of(c * CHUNK, CHUNK), CHUNK)
            for h in range(4):
                head(h, c, rows)
            return carry

        lax.fori_loop(0, nc, chunk, 0)

    rev = lambda j: nb - 1 - j
    return pl.pallas_call(
        body, name="hgrn_bwd",
        out_shape=(jax.ShapeDtypeStruct((t, 4 * HG_W), BF16), jax.ShapeDtypeStruct((8, HG_W), F32),
                   jax.ShapeDtypeStruct((8, HD), F32)),
        grid=(nb,),
        in_specs=_hgrn_cols(rev) + [pl.BlockSpec((2, HG_W), lambda j: (0, 0)), pl.BlockSpec((1, HD), lambda j: (0, 0)),
                                    pl.BlockSpec((HB, HG_W), lambda j: (rev(j), 0)),
                                    pl.BlockSpec((4, nc, HD, HD), lambda j: (0, rev(j), 0, 0)),
                                    pl.BlockSpec((HB, HG_W), lambda j: (rev(j), 1))],
        out_specs=(pl.BlockSpec((HB, 4 * HG_W), lambda j: (rev(j), 0)), pl.BlockSpec((8, HG_W), lambda j: (0, 0)),
                   pl.BlockSpec((8, HD), lambda j: (0, 0))),
        scratch_shapes=[pltpu.VMEM((4, HD, HD), F32)],
        compiler_params=_params(dimension_semantics=("arbitrary",)),
    )(*[z] * 8, hgrn_lb, onorm, o_save, sprev, dymix)


def _assemble_dz(dza, dzb, t):
    tb = min(512, t)

    def body(a_ref, b_ref, o_ref):
        o_ref[:, 0:ZQH] = a_ref[...]
        o_ref[:, ZQH:D_IN] = b_ref[...]

    row = lambda w: pl.BlockSpec((tb, w), lambda i: (i, 0))
    return pl.pallas_call(
        body, name="assemble_dz", out_shape=jax.ShapeDtypeStruct((t, D_IN), BF16), grid=(t // tb,),
        in_specs=[row(ZQH), row(D_IN - ZQH)], out_specs=row(D_IN), compiler_params=_params(),
    )(dza, dzb)


XB = 512


def _xattn_fwd(q, k, v, t):
    tb = min(XB, t)

    def body(q_ref, k_ref, v_ref, o_ref):
        for h in range(XH):
            cols = slice(XD * h, XD * (h + 1))
            s = _dot(q_ref[:, cols], k_ref[:, cols], 1, 1) * (XD ** -0.5)
            p = jnp.exp(s - jnp.max(s, axis=-1, keepdims=True))
            l = jnp.sum(p, axis=-1, keepdims=True)
            o_ref[:, cols] = (_dot(p.astype(BF16), v_ref[:, cols], 1, 0) * (1.0 / l)).astype(BF16)

    row = pl.BlockSpec((tb, D), lambda i: (i, 0))
    mem = pl.BlockSpec(k.shape, lambda i: (0, 0))
    return pl.pallas_call(
        body, name="xattn_fwd", out_shape=jax.ShapeDtypeStruct((t, D), BF16), grid=(t // tb,),
        in_specs=[row, mem, mem], out_specs=row, compiler_params=_params(),
    )(q, k, v)


def _xattn_bwd(q, k, v, do, t):
    tb = min(XB, t)

    def body(q_ref, k_ref, v_ref, do_ref, dq_ref, dk_ref, dv_ref):
        @pl.when(pl.program_id(0) == 0)
        def _():
            dk_ref[...] = jnp.zeros_like(dk_ref)
            dv_ref[...] = jnp.zeros_like(dv_ref)

        for h in range(XH):
            cols = slice(XD * h, XD * (h + 1))
            qh, kh, vh, doh = q_ref[:, cols], k_ref[:, cols], v_ref[:, cols], do_ref[:, cols]
            s = _dot(qh, kh, 1, 1) * (XD ** -0.5)
            p = jnp.exp(s - jnp.max(s, axis=-1, keepdims=True))
            p = p * (1.0 / jnp.sum(p, axis=-1, keepdims=True))
            dp = _dot(doh, vh, 1, 1)
            ds = (p * (dp - jnp.sum(p * dp, axis=-1, keepdims=True)) * (XD ** -0.5)).astype(BF16)
            dq_ref[:, cols] = _dot(ds, kh, 1, 0).astype(BF16)
            dk_ref[:, cols] += _dot(ds, qh, 0, 0)
            dv_ref[:, cols] += _dot(p.astype(BF16), doh, 0, 0)

    row = pl.BlockSpec((tb, D), lambda i: (i, 0))
    mem = pl.BlockSpec(k.shape, lambda i: (0, 0))
    return pl.pallas_call(
        body, name="xattn_bwd",
        out_shape=(jax.ShapeDtypeStruct((t, D), BF16), jax.ShapeDtypeStruct(k.shape, F32),
                   jax.ShapeDtypeStruct(k.shape, F32)),
        grid=(t // tb,), in_specs=[row, mem, mem, row], out_specs=(row, mem, mem),
        compiler_params=_params(dimension_semantics=("arbitrary",)),
    )(q, k, v, do)


def _mem_gain_bwd(dm, mem, *, name):
    def body(dm_ref, m_ref, dg_ref):
        m_ = m_ref[...]
        dg_ref[...] = _rowsum8(dm_ref[...] * (m_ * _rstd(m_)))

    return pl.pallas_call(body, name=name, out_shape=jax.ShapeDtypeStruct((8, D), F32),
                          compiler_params=_params())(dm, mem)


FM, FN = 512, 1408


def _ffn_up(u, wgt, wut, t):
    tm = min(FM, t)

    def body(u_ref, wg_ref, wu_ref, g_ref, up_ref, a_ref):
        u_ = u_ref[...]
        g = _dot(u_, wg_ref[...], 1, 1)
        up = _dot(u_, wu_ref[...], 1, 1)
        g_ref[...] = g.astype(BF16)
        up_ref[...] = up.astype(BF16)
        a_ref[...] = (g * _sig(g) * up).astype(BF16)

    w = pl.BlockSpec((FN, D), lambda i, j: (j, 0))
    o = pl.BlockSpec((tm, FN), lambda i, j: (i, j))
    return pl.pallas_call(
        body, name="ffn_up", out_shape=(jax.ShapeDtypeStruct((t, D_FF), BF16),) * 3,
        grid=(t // tm, D_FF // FN), in_specs=[pl.BlockSpec((tm, D), lambda i, j: (i, 0)), w, w],
        out_specs=(o, o, o), compiler_params=_params(),
    )(u, wgt, wut)


def _ffn_down_bwd(dy, wd, gate, up, t, dep=None):
    tm = min(FM, t)
    deps = [] if dep is None else [dep]

    def body(dy_ref, w_ref, g_ref, up_ref, *rest):
        dg_ref, dup_ref = rest[len(deps):]
        da = _dot(dy_ref[...], w_ref[...], 1, 1)
        g = g_ref[...].astype(F32)
        sg = _sig(g)
        dup_ref[...] = (da * g * sg).astype(BF16)
        dg_ref[...] = (da * up_ref[...].astype(F32) * (sg * (1.0 + g * (1.0 - sg)))).astype(BF16)

    o = pl.BlockSpec((tm, FN), lambda i, j: (i, j))
    return pl.pallas_call(
        body, name="ffn_down_bwd", out_shape=(jax.ShapeDtypeStruct((t, D_FF), BF16),) * 2,
        grid=(t // tm, D_FF // FN),
        in_specs=[pl.BlockSpec((tm, D), lambda i, j: (i, 0)), pl.BlockSpec((FN, D), lambda i, j: (j, 0)), o, o]
        + [ANY_SPEC] * len(deps),
        out_specs=(o, o), compiler_params=_params(),
    )(dy, wd, gate, up, *deps)


def _local_step(x, mem, target, fetch, sm, emit=None):
    t = x.shape[0]
    w, gw = {}, {}

    def out(key, g):
        gw[key] = g
        return None if emit is None else emit(key, g)
    u1 = _prenorm(x, sm["g_mix_pre"], name="prenorm_mix")
    w["winT"] = fetch("winT", u1)
    z = _mm(u1, w["winT"], tb=True, out_dtype=F32, tm=1024, tn=1408, name="mm_z")
    ymix, lse = _swa_fwd(z, sm["sinks"], t)
    ymix, o_h, sprev = _hgrn_fwd(z, sm["hgrn_lb"], sm["hgrn_onorm"], ymix, t)
    w["wout"] = fetch("wout", ymix)
    y1 = _mm(ymix, w["wout"], out_dtype=F32, tm=1024, tn=1024, name="mm_y1")
    h1, u2 = _post_pre(x, y1, sm["g_mix_post"], sm["g_x_pre"], name="post_mix")
    mn = _prenorm(mem, sm["g_mem"], name="prenorm_mem")
    for key in ("wq", "wk", "wv"):
        w[key] = fetch(key, u2)
    qx = _mm(u2, w["wq"], out_dtype=BF16, tm=1024, tn=1024, name="mm_qx")
    kx = _mm(mn, w["wk"], out_dtype=BF16, tm=1024, tn=1024, name="mm_kx")
    vx = _mm(mn, w["wv"], out_dtype=BF16, tm=1024, tn=1024, name="mm_vx")
    ox = _xattn_fwd(qx, kx, vx, t)
    w["wo"] = fetch("wo", ox)
    y2 = _mm(ox, w["wo"], out_dtype=F32, tm=1024, tn=1024, name="mm_y2")
    h2, u3 = _post_pre(h1, y2, sm["g_x_post"], sm["g_ffn_pre"], name="post_x")
    w["wgT"], w["wuT"] = fetch("wgT", u3), fetch("wuT", u3)
    gate, up, act = _ffn_up(u3, w["wgT"], w["wuT"], t)
    w["wd"] = fetch("wd", act)
    y3 = _mm(act, w["wd"], out_dtype=F32, tm=1024, tn=1024, tk=1408, name="mm_y3")
    sq, dh3, dy3, dg_ffn_post = _final_loss(h2, y3, sm["g_ffn_post"], target, name="final_loss")
    dep = out("wd", _mm(act, dy3, ta=True, out_dtype=BF16, tm=1408, tn=1024, tk=512, name="mm_gwd"))
    dgate, dup = _ffn_down_bwd(dy3, w["wd"], gate, up, t, dep=dep)
    dep = out("wgT", _mm(dgate, u3, ta=True, out_dtype=BF16, tm=1408, tn=1024, tk=512, name="mm_gwg"))
    dep = out("wuT", _mm(dup, u3, ta=True, out_dtype=BF16, tm=1408, tn=1024, tk=512, name="mm_gwu", dep=dep))
    du3 = _mm2(dgate, w["wgT"], dup, w["wuT"], tm=512, tk=1408, name="mm_du3", dep=dep)
    dh2, dy2, dg_ffn_pre, dg_x_post = _post_pre_bwd(dh3, du3, h2, y2, sm["g_x_post"], sm["g_ffn_pre"], name="post_x_bwd")
    dep = out("wo", _mm(ox, dy2, ta=True, out_dtype=BF16, tm=1024, tn=1024, tk=512, name="mm_gwo"))
    dox = _mm(dy2, w["wo"], tb=True, out_dtype=BF16, tm=1024, tn=1024, name="mm_dox", dep=dep)
    dqx, dkx, dvx = _xattn_bwd(qx, kx, vx, dox, t)
    dep = out("wq", _mm(u2, dqx, ta=True, out_dtype=BF16, tm=1024, tn=1024, tk=512, name="mm_gwq"))
    dep = out("wk", _mm(mn, dkx, ta=True, out_dtype=BF16, tm=1024, tn=1024, name="mm_gwk", dep=dep))
    dep = out("wv", _mm(mn, dvx, ta=True, out_dtype=BF16, tm=1024, tn=1024, name="mm_gwv", dep=dep))
    du2 = _mm(dqx, w["wq"], tb=True, out_dtype=F32, tm=1024, tn=1024, name="mm_du2", dep=dep)
    dmn = _mm2(dkx, w["wk"], dvx, w["wv"], tb=True, tm=256, tk=1024, name="mm_dmn")
    dg_mem = _mem_gain_bwd(dmn, mem, name="mem_gain_bwd")
    dh1, dy1, dg_x_pre, dg_mix_post = _post_pre_bwd(dh2, du2, h1, y1, sm["g_mix_post"], sm["g_x_pre"], name="post_mix_bwd")
    dep = out("wout", _mm(ymix, dy1, ta=True, out_dtype=BF16, tm=1024, tn=1024, tk=512, name="mm_gwout"))
    dymix = _mm(dy1, w["wout"], tb=True, out_dtype=BF16, tm=1024, tn=1024, name="mm_dymix", dep=dep)
    dza, dsinks = _swa_bwd(z, sm["sinks"], ymix, lse, dymix, t)
    dzb, dlb, donorm = _hgrn_bwd(z, sm["hgrn_lb"], sm["hgrn_onorm"], o_h, sprev, dymix, t)
    dz = _assemble_dz(dza, dzb, t)
    dep = out("winT", _mm(dz, u1, ta=True, out_dtype=BF16, tm=1408, tn=1024, tk=512, name="mm_gwin"))
    du1 = _mm(dz, w["winT"], out_dtype=F32, tm=512, tn=1024, tk=1408, name="mm_du1", dep=dep)
    grad_x, dg_mix_pre = _pre_bwd(dh1, du1, x, sm["g_mix_pre"], name="pre_mix_bwd")
    parts = dict(g_mix_pre=dg_mix_pre, g_mix_post=dg_mix_post, g_mem=dg_mem, g_x_pre=dg_x_pre,
                 g_x_post=dg_x_post, g_ffn_pre=dg_ffn_pre, g_ffn_post=dg_ffn_post,
                 hgrn_onorm=donorm, hgrn_lb=dlb, sinks=dsinks, sq=sq)
    return grad_x, gw, parts


def _position():
    return lax.axis_index("x"), lax.axis_index("y"), lax.axis_index("c")


def _peer(pos, k):
    x, y, c = pos
    return (1 - x if k & 4 else x, 1 - y if k & 2 else y, 1 - c if k & 1 else c)


def _linear(pos):
    x, y, c = pos
    return 4 * x + 2 * y + c


HBM_SPEC = pl.BlockSpec(memory_space=pltpu.HBM)
SEM_SPEC = pl.BlockSpec(memory_space=pltpu.SEMAPHORE)
DATAFLOW = pltpu.SideEffectType.DATAFLOW_SIDE_EFFECTING
SEND_ORDER = (1, 2, 4, 3, 5, 6, 7)


def _in_hbm(a):
    return pltpu.with_memory_space_constraint(a, pltpu.HBM)


def _prepare_weights(shards):
    n = len(shards)

    def body(*refs):
        ins, outs, lands, sem = refs[:n], refs[n:2 * n], refs[2 * n:3 * n], refs[3 * n]
        me_lin = _linear(_position())
        copies = []
        for a in range(n):
            r = ins[a].shape[0]
            outs[a][...] = ins[a][...].astype(BF16)
            copies.append(pltpu.make_async_copy(outs[a], lands[a].at[pl.ds(me_lin * r, r), :], sem.at[a]))
            copies[-1].start()
        for cp in copies:
            cp.wait()

    vmem = pl.BlockSpec(memory_space=pltpu.VMEM)
    res = pl.pallas_call(
        body, name="prepare_weights",
        out_shape=tuple(jax.ShapeDtypeStruct(s.shape, BF16) for s in shards)
        + tuple(jax.ShapeDtypeStruct((N_DEV * s.shape[0], s.shape[1]), BF16) for s in shards),
        in_specs=[vmem] * n, out_specs=tuple([vmem] * n + [ANY_SPEC] * n),
        scratch_shapes=[pltpu.SemaphoreType.DMA((n,))], compiler_params=_params(),
    )(*shards)
    return res[:n], res[n:]


def _gather_start(shards, lands):
    n = len(shards)
    rows = [s.shape[0] for s in shards]

    def body(*refs):
        srcs, land = refs[:n], refs[n:2 * n]
        send_sems, recv_sems = refs[2 * n:3 * n], refs[3 * n:4 * n]
        me = _position()
        for a in range(n):
            mine = land[a].at[pl.ds(_linear(me) * rows[a], rows[a]), :]
            for k in SEND_ORDER:
                pltpu.make_async_remote_copy(
                    src_ref=srcs[a], dst_ref=mine, send_sem=send_sems[a].at[k - 1], recv_sem=recv_sems[a].at[k - 1],
                    device_id=_peer(me, k), device_id_type=MESH).start()

    sems = tuple(pltpu.SemaphoreType.DMA((N_DEV - 1,)) for _ in range(2 * n))
    res = pl.pallas_call(
        body, name="weights_send",
        out_shape=sems + tuple(pltpu.HBM(s.shape, s.dtype) for s in shards)
        + tuple(pltpu.HBM(l.shape, l.dtype) for l in lands),
        in_specs=(HBM_SPEC,) * (2 * n), out_specs=(SEM_SPEC,) * (2 * n) + (HBM_SPEC,) * (2 * n),
        input_output_aliases={i: 2 * n + i for i in range(2 * n)},
        compiler_params=pltpu.CompilerParams(has_side_effects=DATAFLOW),
    )(*[_in_hbm(s) for s in shards], *[_in_hbm(l) for l in lands])
    return [(res[a], res[n + a], res[2 * n + a], res[3 * n + a]) for a in range(n)]


def _gather_wait(send_sems, recv_sems, shard_thru, land_thru, after, *, name):
    r = shard_thru.shape[0]

    def body(src_ref, land_ref, send_sems, recv_sems, after_ref, src_dead, got_ref):
        del after_ref, src_dead, got_ref
        me = _position()
        for k in SEND_ORDER:
            peer = _peer(me, k)
            copy = pltpu.make_async_remote_copy(
                src_ref=src_ref, dst_ref=land_ref.at[pl.ds(_linear(peer) * r, r), :],
                send_sem=send_sems.at[k - 1], recv_sem=recv_sems.at[k - 1],
                device_id=peer, device_id_type=MESH)
            copy.wait_send()
            copy.wait_recv()

    return pl.pallas_call(
        body, name=name,
        out_shape=(pltpu.HBM(shard_thru.shape, shard_thru.dtype), pltpu.HBM(land_thru.shape, land_thru.dtype)),
        in_specs=(HBM_SPEC, HBM_SPEC, SEM_SPEC, SEM_SPEC, ANY_SPEC),
        out_specs=(HBM_SPEC, HBM_SPEC), input_output_aliases={0: 0, 1: 1},
        compiler_params=pltpu.CompilerParams(has_side_effects=DATAFLOW),
    )(shard_thru, land_thru, send_sems, recv_sems, after)[1]


def _exchange_start(g, *, name):
    r = g.shape[0] // N_DEV
    land_shape = (N_DEV - 1, r, g.shape[1])

    def body(g_ref, land_ref, send_sems, recv_sems, g_thru, land_thru):
        del g_thru, land_thru
        me = _position()
        for k in SEND_ORDER:
            peer = _peer(me, k)
            pltpu.make_async_remote_copy(
                src_ref=g_ref.at[pl.ds(_linear(peer) * r, r), :], dst_ref=land_ref.at[k - 1],
                send_sem=send_sems.at[k - 1], recv_sem=recv_sems.at[k - 1],
                device_id=peer, device_id_type=MESH).start()

    return pl.pallas_call(
        body, name=name,
        out_shape=(pltpu.SemaphoreType.DMA((N_DEV - 1,)), pltpu.SemaphoreType.DMA((N_DEV - 1,)),
                   pltpu.HBM(g.shape, g.dtype), pltpu.HBM(land_shape, g.dtype)),
        in_specs=(HBM_SPEC, HBM_SPEC), out_specs=(SEM_SPEC, SEM_SPEC, HBM_SPEC, HBM_SPEC),
        input_output_aliases={0: 2, 1: 3},
        compiler_params=pltpu.CompilerParams(has_side_effects=DATAFLOW),
    )(_in_hbm(g), _in_hbm(lax.empty(land_shape, g.dtype)))


def _exchange_wait(send_sems, recv_sems, g_thru, land_thru, after, *, name):
    r = land_thru.shape[1]

    def body(g_ref, land_ref, send_sems, recv_sems, after_ref, g_dead, got_ref):
        del after_ref, g_dead, got_ref
        me = _position()
        for k in SEND_ORDER:
            peer = _peer(me, k)
            copy = pltpu.make_async_remote_copy(
                src_ref=g_ref.at[pl.ds(_linear(peer) * r, r), :], dst_ref=land_ref.at[k - 1],
                send_sem=send_sems.at[k - 1], recv_sem=recv_sems.at[k - 1],
                device_id=peer, device_id_type=MESH)
            copy.wait_send()
            copy.wait_recv()

    return pl.pallas_call(
        body, name=name,
        out_shape=(pltpu.HBM(g_thru.shape, g_thru.dtype), pltpu.HBM(land_thru.shape, land_thru.dtype)),
        in_specs=(HBM_SPEC, HBM_SPEC, SEM_SPEC, SEM_SPEC, pl.BlockSpec(memory_space=pl.ANY)),
        out_specs=(HBM_SPEC, HBM_SPEC), input_output_aliases={0: 0, 1: 1},
        compiler_params=pltpu.CompilerParams(has_side_effects=DATAFLOW),
    )(g_thru, land_thru, send_sems, recv_sems, after)


def _adamw_math(w, g, m, v):
    m = B1 * m + (1.0 - B1) * g
    v = B2 * v + (1.0 - B2) * (g * g)
    delta = -LR * ((m / C1) / (jnp.sqrt(v / C2) + AEPS) + WD * w)
    return delta, m, v


def _sum_adamw(own, land, w, m, v, *, name):
    def body(own_ref, land_ref, w_ref, m_ref, v_ref, g_ref, d_ref, nm_ref, nv_ref):
        g = own_ref[...].astype(F32)
        for s in range(N_DEV - 1):
            g = g + land_ref[s].astype(F32)
        g_ref[...] = g
        d_ref[...], nm_ref[...], nv_ref[...] = _adamw_math(w_ref[...], g, m_ref[...], v_ref[...])

    return pl.pallas_call(body, name=name, out_shape=(jax.ShapeDtypeStruct(w.shape, F32),) * 4,
                          compiler_params=_params())(own, land, w, m, v)


SMALL = ("g_mix_pre", "g_mix_post", "g_mem", "g_x_pre", "g_x_post", "g_ffn_pre", "g_ffn_post",
         "hgrn_onorm", "hgrn_lb", "sinks")
SMALL_W = dict(hgrn_onorm=HD, hgrn_lb=HG_W, sinks=8)
SQ_ROW = len(SMALL)
PACK_ROWS = 16


def _small_allreduce(parts):
    ns = len(SMALL)

    def body(*refs):
        part, tot_ref = refs[:ns + 1], refs[ns + 1]
        gath, send_sems, recv_sems = refs[ns + 2:]
        me = _position()
        mine = gath.at[_linear(me)]
        mine[...] = jnp.zeros((PACK_ROWS, D), F32)
        for r, name in enumerate(SMALL):
            wd = SMALL_W.get(name, D)
            mine[r:r + 1, 0:wd] = jnp.sum(part[r][...], axis=0, keepdims=True)[:, 0:wd]
        sq = jnp.sum(part[ns][...]) * (0.5 / D)
        mine[SQ_ROW:SQ_ROW + 1, :] = jnp.full((1, D), sq, F32)

        def copy(k):
            peer = _peer(me, k)
            return pltpu.make_async_remote_copy(
                src_ref=mine, dst_ref=mine, send_sem=send_sems.at[k - 1], recv_sem=recv_sems.at[k - 1],
                device_id=peer, device_id_type=MESH)

        def arrival(k):
            slot = gath.at[_linear(_peer(me, k))]
            return pltpu.make_async_remote_copy(
                src_ref=slot, dst_ref=slot, send_sem=send_sems.at[k - 1], recv_sem=recv_sems.at[k - 1],
                device_id=_peer(me, k), device_id_type=MESH)

        sent = [copy(k) for k in range(1, 8)]
        for cp in sent:
            cp.start()
        for k in range(1, 8):
            arrival(k).wait_recv()
        for cp in sent:
            cp.wait_send()
        tot = gath[0]
        for s in range(1, N_DEV):
            tot = tot + gath[s]
        tot_ref[...] = tot

    return pl.pallas_call(
        body, name="small_allreduce", out_shape=jax.ShapeDtypeStruct((PACK_ROWS, D), F32),
        scratch_shapes=[pltpu.VMEM((N_DEV, PACK_ROWS, D), F32), pltpu.SemaphoreType.DMA((7,)),
                        pltpu.SemaphoreType.DMA((7,))],
        compiler_params=_params(has_side_effects=True),
    )(*[parts[n] for n in SMALL], parts["sq"])


def _small_update(tot, sm, m_sm, v_sm):
    ns = len(SMALL)

    def body(*refs):
        tot = refs[0][...]
        w_refs, m_refs, v_refs = refs[1:ns + 1], refs[ns + 1:2 * ns + 1], refs[2 * ns + 1:3 * ns + 1]
        outs = refs[3 * ns + 1:]
        loss_ref = outs[0]
        g_out, d_out = outs[1:ns + 1], outs[ns + 1:2 * ns + 1]
        nm_out, nv_out = outs[2 * ns + 1:3 * ns + 1], outs[3 * ns + 1:4 * ns + 1]
        loss_ref[...] = tot[SQ_ROW:SQ_ROW + 1, 0:1]
        for r, name in enumerate(SMALL):
            wd = SMALL_W.get(name, D)
            g = tot[r:r + 1, 0:wd]
            w = w_refs[r][...]
            if name == "hgrn_lb":
                mx = jnp.maximum(w[0:1], w[1:2])
                e0, e1 = jnp.exp(w[0:1] - mx), jnp.exp(w[1:2] - mx)
                lb0 = e0 / (e0 + e1)
                g0 = g * lb0 * (1.0 - lb0)
                for i, gi in enumerate((g0, -g0)):
                    d, nm, nv = _adamw_math(w[i:i + 1], gi, m_refs[r][i:i + 1, :], v_refs[r][i:i + 1, :])
                    g_out[r][i:i + 1, :] = gi
                    d_out[r][i:i + 1, :], nm_out[r][i:i + 1, :], nv_out[r][i:i + 1, :] = d, nm, nv
            else:
                d, nm, nv = _adamw_math(w, g, m_refs[r][...], v_refs[r][...])
                g_out[r][...] = g
                d_out[r][...], nm_out[r][...], nv_out[r][...] = d, nm, nv

    shapes = [jax.ShapeDtypeStruct(sm[n].shape, F32) for n in SMALL]
    res = pl.pallas_call(
        body, name="small_update", out_shape=tuple([jax.ShapeDtypeStruct((1, 1), F32)] + shapes * 4),
        compiler_params=_params(),
    )(tot, *[sm[n] for n in SMALL], *[m_sm[n] for n in SMALL], *[v_sm[n] for n in SMALL])
    groups = [dict(zip(SMALL, res[1 + i * ns:1 + (i + 1) * ns])) for i in range(4)]
    return res[0], groups[0], groups[1], groups[2], groups[3]


BIG = ("w_in", "w_gate", "w_up", "w_down", "w_out", "wq_x", "wk_x", "wv_x", "wo_x")
BIG_KEY = dict(w_in="winT", w_gate="wgT", w_up="wuT", w_down="wd", w_out="wout", wq_x="wq", wk_x="wk",
               wv_x="wv", wo_x="wo")
TRANSPOSED = ("w_in", "w_gate", "w_up")
WEIGHTS = ("w_in", "sinks", "hgrn_lb", "hgrn_onorm", "w_out", "g_mix_pre", "g_mix_post", "g_mem", "g_x_pre",
           "g_x_post", "wq_x", "wk_x", "wv_x", "wo_x", "g_ffn_pre", "g_ffn_post", "w_gate", "w_up", "w_down")


def kernel(x, mem, w_in, sinks, hgrn_lb, hgrn_onorm, w_out, g_mix_pre, g_mix_post, g_mem, g_x_pre, g_x_post, wq_x, wk_x, wv_x, wo_x, g_ffn_pre, g_ffn_post, w_gate, w_up, w_down, loss_target, m_w_in, m_sinks, m_hgrn_lb, m_hgrn_onorm, m_w_out, m_g_mix_pre, m_g_mix_post, m_g_mem, m_g_x_pre, m_g_x_post, m_wq_x, m_wk_x, m_wv_x, m_wo_x, m_g_ffn_pre, m_g_ffn_post, m_w_gate, m_w_up, m_w_down, v_w_in, v_sinks, v_hgrn_lb, v_hgrn_onorm, v_w_out, v_g_mix_pre, v_g_mix_post, v_g_mem, v_g_x_pre, v_g_x_post, v_wq_x, v_wk_x, v_wv_x, v_wo_x, v_g_ffn_pre, v_g_ffn_post, v_w_gate, v_w_up, v_w_down):
    given = dict(locals())
    wts = {n: given[n] for n in WEIGHTS}
    ms = {n: given["m_" + n] for n in WEIGHTS}
    vs = {n: given["v_" + n] for n in WEIGHTS}

    def mat(a, name):
        a = a[0]
        return a.T if name in TRANSPOSED else a

    order = ("w_in", "w_out", "wq_x", "wk_x", "wv_x", "wo_x", "w_gate", "w_up", "w_down")
    flying = dict(zip(order, _gather_start(*_prepare_weights([mat(wts[n], n) for n in order]))))
    name_of = {k: n for n, k in BIG_KEY.items()}

    def fetch(key, after):
        return _gather_wait(*flying[name_of[key]], after, name="weights_recv_" + name_of[key])

    sm = {n: wts[n] for n in SMALL}
    started = {}

    def emit(key, g):
        started[name_of[key]] = _exchange_start(g, name="grad_send_" + name_of[key])
        return started[name_of[key]][2]

    grad_x, _, parts = _local_step(x[0], mem[0], loss_target[0], fetch, sm, emit)
    me_lin = _linear(_position())
    grads, deltas, new_m, new_v = {}, {}, {}, {}
    after = grad_x
    for n in ("w_down", "w_gate", "w_up", "wo_x", "wq_x", "wk_x", "wv_x", "w_out", "w_in"):
        g_all, land = _exchange_wait(*started[n], after, name="grad_recv_" + n)
        r = land.shape[1]
        own = lax.dynamic_slice_in_dim(g_all, me_lin * r, r, 0)
        res = _sum_adamw(own, land, mat(wts[n], n), mat(ms[n], n), mat(vs[n], n), name="adamw_" + n)
        after = res[1]
        if n in TRANSPOSED:
            res = [a.T for a in res]
        grads[n], deltas[n], new_m[n], new_v[n] = [a[None] for a in res]
    loss, g_s, d_s, m_s, v_s = _small_update(_small_allreduce(parts), sm, {n: ms[n] for n in SMALL},
                                             {n: vs[n] for n in SMALL})
    grads.update(g_s), deltas.update(d_s), new_m.update(m_s), new_v.update(v_s)
    return (loss[0, 0], grad_x[None], *[grads[n] for n in WEIGHTS], *[deltas[n] for n in WEIGHTS],
            *[new_m[n] for n in WEIGHTS], *[new_v[n] for n in WEIGHTS])
```

```python
import functools

import jax
import jax.numpy as jnp
from jax import lax
from jax.experimental import pallas as pl
from jax.experimental.pallas import tpu as pltpu

F32 = jnp.float32
BF16 = jnp.bfloat16

D = 1024
D_IN = 2816
D_FF = 2816
CHUNK = 64
SWA_W = 512
KV_W = 128
HG_W = 512
HD = 128
ZQH, ZFH, ZIH, ZGH = 768, 1280, 1792, 2304
XH, XD = 4, 256
EPS = 1e-6
NEG = -1e30
N_DEV = 8
MESH = pl.DeviceIdType.MESH

LR, B1, B2, AEPS, WD, STEP = 0.001, 0.9, 0.999, 1e-08, 0.01, 10
C1 = 1.0 - B1 ** STEP
C2 = 1.0 - B2 ** STEP

VMEM_LIMIT = 56 * 1024 * 1024


def _params(**kw):
    return pltpu.CompilerParams(vmem_limit_bytes=VMEM_LIMIT, **kw)


def _sig(x):
    return 1.0 / (1.0 + jnp.exp(-x))


def _rowsum8(x):
    r, w = x.shape
    return jnp.sum(x.reshape(r // 8, 8, w), axis=0)


def _dot(a, b, ca, cb, precision=None):
    return lax.dot_general(a, b, (((ca,), (cb,)), ((), ())), preferred_element_type=F32,
                           precision=precision)


ANY_SPEC = pl.BlockSpec(memory_space=pl.ANY)


def _mm(a, b, *, ta=False, tb=False, out_dtype, tm, tn, tk=None, name, dep=None, n_outer=False):
    m = a.shape[1] if ta else a.shape[0]
    k = a.shape[0] if ta else a.shape[1]
    n = b.shape[0] if tb else b.shape[1]
    tm, tn = min(tm, m), min(tn, n)
    tk = k if tk is None else min(tk, k)
    nk = k // tk
    assert m % tm == 0 and n % tn == 0 and k % tk == 0, (name, m, n, k, tm, tn, tk)
    ij = (lambda g0, g1: (g1, g0)) if n_outer else (lambda g0, g1: (g0, g1))
    a_spec = (pl.BlockSpec((tk, tm), lambda g0, g1, kk: (kk, ij(g0, g1)[0])) if ta
              else pl.BlockSpec((tm, tk), lambda g0, g1, kk: (ij(g0, g1)[0], kk)))
    b_spec = (pl.BlockSpec((tn, tk), lambda g0, g1, kk: (ij(g0, g1)[1], kk)) if tb
              else pl.BlockSpec((tk, tn), lambda g0, g1, kk: (kk, ij(g0, g1)[1])))
    ca, cb = (0 if ta else 1), (1 if tb else 0)

    deps = [] if dep is None else [dep]

    def body(a_ref, b_ref, *rest):
        o_ref, acc = rest[len(deps)], rest[len(deps) + 1:]
        p = _dot(a_ref[...].astype(BF16), b_ref[...].astype(BF16), ca, cb)
        if nk == 1:
            o_ref[...] = p.astype(out_dtype)
        else:
            acc_ref, = acc
            kk = pl.program_id(2)

            @pl.when(kk == 0)
            def _():
                acc_ref[...] = p

            @pl.when(kk > 0)
            def _():
                acc_ref[...] += p

            @pl.when(kk == nk - 1)
            def _():
                o_ref[...] = acc_ref[...].astype(out_dtype)

    return pl.pallas_call(
        body, name=name, out_shape=jax.ShapeDtypeStruct((m, n), out_dtype),
        grid=(n // tn, m // tm, nk) if n_outer else (m // tm, n // tn, nk),
        in_specs=[a_spec, b_spec] + [ANY_SPEC] * len(deps),
        out_specs=pl.BlockSpec((tm, tn), lambda g0, g1, kk: ij(g0, g1)),
        scratch_shapes=[pltpu.VMEM((tm, tn), F32)] if nk > 1 else [],
        compiler_params=_params(dimension_semantics=("parallel", "parallel", "arbitrary")),
    )(a, b, *deps)


def _mm2(a1, b1, a2, b2, *, tb=False, tm, tk, name, dep=None):
    m, k = a1.shape
    n = b1.shape[0] if tb else b1.shape[1]
    tm, tk = min(tm, m), min(tk, k)
    nk = k // tk
    assert m % tm == 0 and k % tk == 0
    cb = 1 if tb else 0
    deps = [] if dep is None else [dep]

    def body(a1_ref, b1_ref, a2_ref, b2_ref, *rest):
        o_ref = rest[len(deps)]
        p = (_dot(a1_ref[...].astype(BF16), b1_ref[...], 1, cb)
             + _dot(a2_ref[...].astype(BF16), b2_ref[...], 1, cb))
        kk = pl.program_id(1)

        @pl.when(kk == 0)
        def _():
            o_ref[...] = p

        @pl.when(kk > 0)
        def _():
            o_ref[...] += p

    a_spec = pl.BlockSpec((tm, tk), lambda i, kk: (i, kk))
    b_spec = pl.BlockSpec((n, tk), lambda i, kk: (0, kk)) if tb else pl.BlockSpec((tk, n), lambda i, kk: (kk, 0))
    return pl.pallas_call(
        body, name=name, out_shape=jax.ShapeDtypeStruct((m, n), F32),
        grid=(m // tm, nk), in_specs=[a_spec, b_spec, a_spec, b_spec] + [ANY_SPEC] * len(deps),
        out_specs=pl.BlockSpec((tm, n), lambda i, kk: (i, 0)),
        compiler_params=_params(dimension_semantics=("parallel", "arbitrary")),
    )(a1, b1, a2, b2, *deps)


def _rstd(x):
    return lax.rsqrt(jnp.mean(x * x, axis=-1, keepdims=True) + EPS)


def _norm_bwd(xh, r, t):
    return r * (t - xh * jnp.mean(xh * t, axis=-1, keepdims=True))


def _prenorm(x, g, *, name):
    t, d = x.shape
    tb = min(512, t)

    def body(x_ref, g_ref, o_ref):
        xf = x_ref[...]
        o_ref[...] = (xf * _rstd(xf) * g_ref[...]).astype(BF16)

    return pl.pallas_call(
        body, name=name, out_shape=jax.ShapeDtypeStruct((t, d), BF16), grid=(t // tb,),
        in_specs=[pl.BlockSpec((tb, d), lambda i: (i, 0)), pl.BlockSpec((1, d), lambda i: (0, 0))],
        out_specs=pl.BlockSpec((tb, d), lambda i: (i, 0)), compiler_params=_params(),
    )(x, g)


def _post_pre(h, y, g_post, g_pre, *, name):
    t, d = h.shape
    tb = min(512, t)

    def body(h_ref, y_ref, gp_ref, gn_ref, hn_ref, u_ref):
        y_ = y_ref[...]
        hn = h_ref[...] + y_ * _rstd(y_) * gp_ref[...]
        hn_ref[...] = hn
        u_ref[...] = (hn * _rstd(hn) * gn_ref[...]).astype(BF16)

    row = pl.BlockSpec((tb, d), lambda i: (i, 0))
    vec = pl.BlockSpec((1, d), lambda i: (0, 0))
    return pl.pallas_call(
        body, name=name, out_shape=(jax.ShapeDtypeStruct((t, d), F32), jax.ShapeDtypeStruct((t, d), BF16)),
        grid=(t // tb,), in_specs=[row, row, vec, vec], out_specs=(row, row), compiler_params=_params(),
    )(h, y, g_post, g_pre)


def _final_loss(h, y, g_post, target, *, name):
    t, d = h.shape
    tb = min(512, t)

    def body(h_ref, y_ref, g_ref, t_ref, sq_ref, dh_ref, dy_ref, dg_ref):
        @pl.when(pl.program_id(0) == 0)
        def _():
            sq_ref[...] = jnp.zeros_like(sq_ref)
            dg_ref[...] = jnp.zeros_like(dg_ref)

        y_ = y_ref[...]
        r = _rstd(y_)
        yh = y_ * r
        g = g_ref[...]
        err = h_ref[...] + yh * g - t_ref[...]
        sq_ref[...] += _rowsum8(err * err)
        dh = err * (1.0 / d)
        dh_ref[...] = dh
        dg_ref[...] += _rowsum8(dh * yh)
        dy_ref[...] = _norm_bwd(yh, r, dh * g).astype(BF16)

    row = pl.BlockSpec((tb, d), lambda i: (i, 0))
    vec = pl.BlockSpec((1, d), lambda i: (0, 0))
    acc = pl.BlockSpec((8, d), lambda i: (0, 0))
    return pl.pallas_call(
        body, name=name,
        out_shape=(jax.ShapeDtypeStruct((8, d), F32), jax.ShapeDtypeStruct((t, d), F32),
                   jax.ShapeDtypeStruct((t, d), BF16), jax.ShapeDtypeStruct((8, d), F32)),
        grid=(t // tb,), in_specs=[row, row, vec, row], out_specs=(acc, row, row, acc),
        compiler_params=_params(dimension_semantics=("arbitrary",)),
    )(h, y, g_post, target)


def _post_pre_bwd(dh_out, du, hn, y, g_post, g_pre, *, name):
    t, d = hn.shape
    tb = min(512, t)

    def body(dho_ref, du_ref, hn_ref, y_ref, gp_ref, gn_ref, dh_ref, dy_ref, dgn_ref, dgp_ref):
        @pl.when(pl.program_id(0) == 0)
        def _():
            dgn_ref[...] = jnp.zeros_like(dgn_ref)
            dgp_ref[...] = jnp.zeros_like(dgp_ref)

        hn_ = hn_ref[...]
        r2 = _rstd(hn_)
        xh = hn_ * r2
        du_ = du_ref[...]
        dgn_ref[...] += _rowsum8(du_ * xh)
        dh = dho_ref[...] + _norm_bwd(xh, r2, du_ * gn_ref[...])
        dh_ref[...] = dh
        y_ = y_ref[...]
        r1 = _rstd(y_)
        yh = y_ * r1
        dgp_ref[...] += _rowsum8(dh * yh)
        dy_ref[...] = _norm_bwd(yh, r1, dh * gp_ref[...]).astype(BF16)

    row = pl.BlockSpec((tb, d), lambda i: (i, 0))
    vec = pl.BlockSpec((1, d), lambda i: (0, 0))
    acc = pl.BlockSpec((8, d), lambda i: (0, 0))
    return pl.pallas_call(
        body, name=name,
        out_shape=(jax.ShapeDtypeStruct((t, d), F32), jax.ShapeDtypeStruct((t, d), BF16),
                   jax.ShapeDtypeStruct((8, d), F32), jax.ShapeDtypeStruct((8, d), F32)),
        grid=(t // tb,), in_specs=[row, row, row, row, vec, vec], out_specs=(row, row, acc, acc),
        compiler_params=_params(dimension_semantics=("arbitrary",)),
    )(dh_out, du, hn, y, g_post, g_pre)


def _pre_bwd(dh_out, du, x, g, *, name):
    t, d = x.shape
    tb = min(512, t)
    has_res = dh_out is not None

    def body(*refs):
        if has_res:
            dho_ref, du_ref, x_ref, g_ref, dx_ref, dg_ref = refs
        else:
            du_ref, x_ref, g_ref, dx_ref, dg_ref = refs

        @pl.when(pl.program_id(0) == 0)
        def _():
            dg_ref[...] = jnp.zeros_like(dg_ref)

        x_ = x_ref[...]
        r = _rstd(x_)
        xh = x_ * r
        du_ = du_ref[...]
        dg_ref[...] += _rowsum8(du_ * xh)
        dx = _norm_bwd(xh, r, du_ * g_ref[...])
        if has_res:
            dx = dx + dho_ref[...]
        dx_ref[...] = dx

    row = pl.BlockSpec((tb, d), lambda i: (i, 0))
    vec = pl.BlockSpec((1, d), lambda i: (0, 0))
    acc = pl.BlockSpec((8, d), lambda i: (0, 0))
    ins = ([dh_out] if has_res else []) + [du, x, g]
    return pl.pallas_call(
        body, name=name,
        out_shape=(jax.ShapeDtypeStruct((t, d), F32), jax.ShapeDtypeStruct((8, d), F32)),
        grid=(t // tb,), in_specs=[row] * (len(ins) - 1) + [vec], out_specs=(row, acc),
        compiler_params=_params(dimension_semantics=("arbitrary",)),
    )(*ins)


QB = 256


def _half_mask(shape, e):
    lane = lax.broadcasted_iota(jnp.int32, shape, len(shape) - 1)
    return (lane // 64) == e


def _place(kv):
    sw = pltpu.roll(kv, 64, 1)
    m0 = _half_mask(kv.shape, 0)
    return [[jnp.where(m0, kv, 0.0).astype(BF16), jnp.where(m0, 0.0, sw).astype(BF16)],
            [jnp.where(m0, sw, 0.0).astype(BF16), jnp.where(m0, 0.0, kv).astype(BF16)]]


def _swa_valid_q(i, nq, nk):
    qc = lax.broadcasted_iota(jnp.int32, (nq, nk), 0) // CHUNK
    kc = lax.broadcasted_iota(jnp.int32, (nq, nk), 1) // CHUNK - 2
    return (kc <= qc) & (qc <= kc + 2) & (4 * i + kc >= 0)


def _swa_fwd(z, sinks, t):
    nb = t // QB

    def body(s_ref, q_ref, kp_ref, kc_ref, vp_ref, vc_ref, o_ref, lse_ref):
        i = pl.program_id(0)
        kpl = _place(jnp.concatenate([kp_ref[...], kc_ref[...]], axis=0))
        vpl = _place(jnp.concatenate([vp_ref[...], vc_ref[...]], axis=0))
        valid = _swa_valid_q(i, QB, QB + 128)
        lane = lax.broadcasted_iota(jnp.int32, (QB, 128), 1)
        lse_out = jnp.zeros((QB, 128), F32)
        for j in range(4):
            qp = q_ref[:, 128 * j:128 * (j + 1)].astype(BF16)
            acc = jnp.zeros((QB, 128), F32)
            for e in range(2):
                h = 2 * j + e
                kvh = h // 4
                qm = jnp.where(_half_mask(qp.shape, e), qp, jnp.zeros_like(qp))
                s = _dot(qm, kpl[kvh][e], 1, 1) * 0.125
                s = jnp.where(valid, s, NEG)
                sink = s_ref[0, h]
                m = jnp.maximum(jnp.max(s, axis=-1, keepdims=True), sink)
                p = jnp.exp(s - m)
                l = jnp.sum(p, axis=-1, keepdims=True) + jnp.exp(sink - m)
                acc = acc + _dot(p.astype(BF16), vpl[kvh][e], 1, 0) * (1.0 / l)
                lse_out = jnp.where(lane == h, m + jnp.log(l), lse_out)
            o_ref[:, 128 * j:128 * (j + 1)] = acc.astype(BF16)
        lse_ref[...] = lse_out

    prev = lambda c: pl.BlockSpec((128, 128), lambda i: (jnp.maximum(2 * i - 1, 0), c))
    cur = lambda c: pl.BlockSpec((QB, 128), lambda i: (i, c))
    return pl.pallas_call(
        body, name="swa_fwd",
        out_shape=(jax.ShapeDtypeStruct((t, D), BF16), jax.ShapeDtypeStruct((t, 128), F32)),
        grid=(nb,),
        in_specs=[pl.BlockSpec(memory_space=pltpu.SMEM),
                  pl.BlockSpec((QB, SWA_W), lambda i: (i, 0)), prev(4), cur(4), prev(5), cur(5)],
        out_specs=(pl.BlockSpec((QB, SWA_W), lambda i: (i, 0)), pl.BlockSpec((QB, 128), lambda i: (i, 0))),
        compiler_params=_params(),
    )(sinks, z, z, z, z, z)


def _swa_bwd(z, sinks, ymix, lse, dymix, t):
    nb = t // QB
    nq2 = QB + 128

    def body(s_ref, qc_ref, qn_ref, kp_ref, kc_ref, vp_ref, vc_ref, oc_ref, on_ref, doc_ref, don_ref,
             lc_ref, ln_ref, dz_ref, ds_ref):
        i = pl.program_id(0)

        @pl.when(i == 0)
        def _():
            ds_ref[...] = jnp.zeros_like(ds_ref)

        lane = lax.broadcasted_iota(jnp.int32, (8, 128), 1)
        kpl = _place(jnp.concatenate([kp_ref[...], kc_ref[...]], axis=0))
        vpl = _place(jnp.concatenate([vp_ref[...], vc_ref[...]], axis=0))
        valid = _swa_valid_q(i, QB, nq2)
        lse_c = lc_ref[...]
        dsink = jnp.zeros((8, 128), F32)
        for j in range(4):
            cols = slice(128 * j, 128 * (j + 1))
            qp = qc_ref[:, cols].astype(BF16)
            dop = doc_ref[:, cols]
            prod = dop.astype(F32) * oc_ref[:, cols].astype(F32)
            acc = jnp.zeros((QB, 128), F32)
            for e in range(2):
                h = 2 * j + e
                kvh = h // 4
                hm = _half_mask(qp.shape, e)
                qm = jnp.where(hm, qp, jnp.zeros_like(qp))
                dom = jnp.where(hm, dop, jnp.zeros_like(dop))
                dd = jnp.sum(jnp.where(hm, prod, 0.0), axis=-1, keepdims=True)
                lse_h = lse_c[:, h:h + 1]
                s = _dot(qm, kpl[kvh][e], 1, 1) * 0.125
                p = jnp.where(valid, jnp.exp(s - lse_h), 0.0)
                dp = _dot(dom, vpl[kvh][e], 1, 1)
                ds = p * (dp - dd) * 0.125
                acc = acc + _dot(ds.astype(BF16), kpl[kvh][e], 1, 0)
                ps = jnp.exp(s_ref[0, h] - lse_h) * dd
                dsink = dsink - jnp.where(lane == h, _rowsum8(jnp.broadcast_to(ps, (QB, 128))), 0.0)
            dz_ref[:, cols] = acc.astype(BF16)
        ds_ref[...] += dsink
        kpl, vpl = _place(kc_ref[...]), _place(vc_ref[...])
        qr = lax.broadcasted_iota(jnp.int32, (nq2, QB), 0) // CHUNK
        kr = lax.broadcasted_iota(jnp.int32, (nq2, QB), 1) // CHUNK
        valid2 = (kr <= qr) & (qr <= kr + 2) & (4 * i + qr < t // CHUNK)
        lse_a = jnp.concatenate([lse_c, ln_ref[...]], axis=0)
        dk_acc = [[jnp.zeros((QB, 128), F32) for _ in range(2)] for _ in range(2)]
        dv_acc = [[jnp.zeros((QB, 128), F32) for _ in range(2)] for _ in range(2)]
        for j in range(4):
            cols = slice(128 * j, 128 * (j + 1))
            qp = jnp.concatenate([qc_ref[:, cols], qn_ref[:, cols]], axis=0).astype(BF16)
            dop = jnp.concatenate([doc_ref[:, cols], don_ref[:, cols]], axis=0)
            op = jnp.concatenate([oc_ref[:, cols], on_ref[:, cols]], axis=0)
            prod = dop.astype(F32) * op.astype(F32)
            for e in range(2):
                h = 2 * j + e
                kvh = h // 4
                hm = _half_mask(qp.shape, e)
                qm = jnp.where(hm, qp, jnp.zeros_like(qp))
                dom = jnp.where(hm, dop, jnp.zeros_like(dop))
                dd = jnp.sum(jnp.where(hm, prod, 0.0), axis=-1, keepdims=True)
                s = _dot(qm, kpl[kvh][e], 1, 1) * 0.125
                p = jnp.where(valid2, jnp.exp(s - lse_a[:, h:h + 1]), 0.0)
                dv_acc[kvh][e] = dv_acc[kvh][e] + _dot(p.astype(BF16), dom, 0, 0)
                dp = _dot(dom, vpl[kvh][e], 1, 1)
                ds = p * (dp - dd) * 0.125
                dk_acc[kvh][e] = dk_acc[kvh][e] + _dot(ds.astype(BF16), qm, 0, 0)
        dk = dk_acc[0][0] + dk_acc[1][1] + pltpu.roll(dk_acc[0][1] + dk_acc[1][0], 64, 1)
        dv = dv_acc[0][0] + dv_acc[1][1] + pltpu.roll(dv_acc[0][1] + dv_acc[1][0], 64, 1)
        dz_ref[:, 512:640] = dk.astype(BF16)
        dz_ref[:, 640:768] = dv.astype(BF16)

    last = 2 * nb - 1
    prev = lambda c: pl.BlockSpec((128, 128), lambda i: (jnp.maximum(2 * i - 1, 0), c))
    cur = lambda w, c: pl.BlockSpec((QB, w), lambda i: (i, c))
    nxt = lambda w: pl.BlockSpec((128, w), lambda i: (jnp.minimum(2 * i + 2, last), 0))
    return pl.pallas_call(
        body, name="swa_bwd",
        out_shape=(jax.ShapeDtypeStruct((t, 768), BF16), jax.ShapeDtypeStruct((8, 128), F32)),
        grid=(nb,),
        in_specs=[pl.BlockSpec(memory_space=pltpu.SMEM),
                  cur(SWA_W, 0), nxt(SWA_W), prev(4), cur(128, 4), prev(5), cur(128, 5),
                  cur(SWA_W, 0), nxt(SWA_W), cur(SWA_W, 0), nxt(SWA_W), cur(128, 0), nxt(128)],
        out_specs=(pl.BlockSpec((QB, 768), lambda i: (i, 0)), pl.BlockSpec((8, 128), lambda i: (0, 0))),
        compiler_params=_params(dimension_semantics=("arbitrary",)),
    )(sinks, z, z, z, z, z, z, ymix, ymix, dymix, dymix, lse, lse)


HB = 256
HI = lax.Precision.HIGHEST


def _lower_bound(lb_ref):
    a = lb_ref[...]
    a0, a1 = a[0:1], a[1:2]
    mx = jnp.maximum(a0, a1)
    e0, e1 = jnp.exp(a0 - mx), jnp.exp(a1 - mx)
    return e0 / (e0 + e1)


def _tri(lower):
    r = lax.broadcasted_iota(jnp.int32, (CHUNK, CHUNK), 0)
    c = lax.broadcasted_iota(jnp.int32, (CHUNK, CHUNK), 1)
    return (c <= r) if lower else (c >= r)


def _hgrn_chunk(q, fl, lb):
    sq = _sig(q)
    qf = q * sq * (HD ** -0.5)
    sg = _sig(fl)
    f = lb + (1.0 - lb) * sg
    kf = 1.0 - f
    b = _dot(_tri(True).astype(F32), jnp.log(f), 1, 0, precision=HI)
    b_mid = b[CHUNK // 2 - 1:CHUNK // 2]
    b_last = b[CHUNK - 1:CHUNK]
    qm = qf * jnp.exp(b - b_mid)
    km = kf * jnp.exp(b_mid - b)
    kl = kf * jnp.exp(b_last - b)
    qb = qf * jnp.exp(b)
    return dict(sq=sq, qf=qf, sg=sg, f=f, kf=kf, b=b, b_mid=b_mid, b_last=b_last, qm=qm, km=km, kl=kl, qb=qb)


def _hgrn_cols(row_block):
    return [pl.BlockSpec((HB, 2 * HD), lambda j, c=base // (2 * HD) + p: (row_block(j), c))
            for base in (ZQH, ZFH, ZIH, ZGH) for p in range(2)]


def _hgrn_fwd(z, hgrn_lb, onorm, ymix, t):
    nb = t // HB
    nc = HB // CHUNK

    def body(*refs):
        zq, zf, zi, zg = refs[0:2], refs[2:4], refs[4:6], refs[6:8]
        lb_ref, on_ref, _, y_ref, o_ref, sp_ref, st_ref = refs[8:]

        @pl.when(pl.program_id(0) == 0)
        def _():
            st_ref[...] = jnp.zeros_like(st_ref)

        lb_all = _lower_bound(lb_ref)
        gn = on_ref[...]

        def chunk(c, carry):
            rows = pl.ds(pl.multiple_of(c * CHUNK, CHUNK), CHUNK)
            for h in range(4):
                p, ls, hs = h // 2, pl.ds((h % 2) * HD, HD), pl.ds(h * HD, HD)
                w = _hgrn_chunk(zq[p][rows, ls], zf[p][rows, ls], lb_all[:, h * HD:(h + 1) * HD])
                iv = zi[p][rows, ls].astype(BF16)
                st = st_ref[h]
                sp_ref[h, c] = st
                a = jnp.where(_tri(True), _dot(w["qm"].astype(BF16), w["km"].astype(BF16), 1, 1), 0.0)
                o = _dot(a.astype(BF16), iv, 1, 0) + _dot(w["qb"].astype(BF16), st.astype(BF16), 1, 1)
                st_ref[h] = st * jnp.exp(w["b_last"]) + _dot(iv, w["kl"].astype(BF16), 0, 0)
                o_ref[rows, hs] = o
                gg = zg[p][rows, ls]
                y_ref[rows, hs] = (o * _rstd(o) * gn * (gg * _sig(gg))).astype(BF16)
            return carry

        lax.fori_loop(0, nc, chunk, 0, unroll=True)

    return pl.pallas_call(
        body, name="hgrn_fwd",
        out_shape=(jax.ShapeDtypeStruct((t, D), BF16), jax.ShapeDtypeStruct((t, HG_W), F32),
                   jax.ShapeDtypeStruct((4, t // CHUNK, HD, HD), F32)),
        grid=(nb,),
        in_specs=_hgrn_cols(lambda j: j) + [pl.BlockSpec((2, HG_W), lambda j: (0, 0)),
                                            pl.BlockSpec((1, HD), lambda j: (0, 0)), ANY_SPEC],
        out_specs=(pl.BlockSpec((HB, HG_W), lambda j: (j, 1)),
                   pl.BlockSpec((HB, HG_W), lambda j: (j, 0)),
                   pl.BlockSpec((4, nc, HD, HD), lambda j: (0, j, 0, 0))),
        scratch_shapes=[pltpu.VMEM((4, HD, HD), F32)],
        input_output_aliases={10: 0},
        compiler_params=_params(dimension_semantics=("arbitrary",)),
    )(*[z] * 8, hgrn_lb, onorm, ymix)


def _hgrn_bwd(z, hgrn_lb, onorm, o_save, sprev, dymix, t):
    nb = t // HB
    nc = HB // CHUNK

    def body(*refs):
        zq, zf, zi, zg = refs[0:2], refs[2:4], refs[4:6], refs[6:8]
        lb_ref, on_ref, o_ref, sp_ref, dy_ref, dz_ref, dlb_ref, don_ref, dst_ref = refs[8:]

        @pl.when(pl.program_id(0) == 0)
        def _():
            dst_ref[...] = jnp.zeros_like(dst_ref)
            dlb_ref[...] = jnp.zeros_like(dlb_ref)
            don_ref[...] = jnp.zeros_like(don_ref)

        lb_all = _lower_bound(lb_ref)
        gn = on_ref[...]
        row = lax.broadcasted_iota(jnp.int32, (CHUNK, HD), 0)

        def head(h, c, rows):
            p, ls, hs = h // 2, pl.ds((h % 2) * HD, HD), pl.ds(h * HD, HD)
            lb = lb_all[:, h * HD:(h + 1) * HD]
            q = zq[p][rows, ls]
            w = _hgrn_chunk(q, zf[p][rows, ls], lb)
            iv = zi[p][rows, ls].astype(BF16)
            gg = zg[p][rows, ls]
            o = o_ref[rows, hs]
            st = sp_ref[h, c]
            dst = dst_ref[h]
            dout = dy_ref[rows, hs].astype(F32)
            sgg = _sig(gg)
            r = _rstd(o)
            oh = o * r
            dyn = dout * (gg * sgg)
            dz_ref[rows, pl.ds(3 * HG_W + h * HD, HD)] = (dout * oh * gn * (sgg * (1.0 + gg * (1.0 - sgg)))).astype(BF16)
            don_ref[...] += _rowsum8(dyn * oh)
            do = _norm_bwd(oh, r, dyn * gn).astype(BF16)
            qm, km, kl, qb = (w[n].astype(BF16) for n in ("qm", "km", "kl", "qb"))
            dstb = dst.astype(BF16)
            d_row = jnp.exp(w["b_last"])
            dqb = _dot(do, st.astype(BF16), 1, 0)
            dst_ref[h] = dst * d_row + _dot(do, qb, 0, 0)
            dd_row = jnp.sum(dst * st, axis=0, keepdims=True)
            at = jnp.where(_tri(False), _dot(km, qm, 1, 1), 0.0)
            dz_ref[rows, pl.ds(2 * HG_W + h * HD, HD)] = (_dot(at.astype(BF16), do, 1, 0) + _dot(kl, dstb, 1, 1)).astype(BF16)
            dkl = _dot(iv, dstb, 1, 0)
            da = jnp.where(_tri(True), _dot(do, iv, 1, 1), 0.0).astype(BF16)
            dat = jnp.where(_tri(False), _dot(iv, do, 1, 1), 0.0).astype(BF16)
            dqm = _dot(da, km, 1, 0)
            dkm = _dot(dat, qm, 1, 0)
            e1, e2 = jnp.exp(w["b"] - w["b_mid"]), jnp.exp(w["b_mid"] - w["b"])
            e3, e4 = jnp.exp(w["b_last"] - w["b"]), jnp.exp(w["b"])
            dqf = dqm * e1 + dqb * e4
            dkf = dkm * e2 + dkl * e3
            t_qm, t_km, t_kl = dqm * w["qm"], dkm * w["km"], dkl * w["kl"]
            db = t_qm - t_km - t_kl + dqb * w["qb"]
            db_mid = jnp.sum(t_km - t_qm, axis=0, keepdims=True)
            db_last = jnp.sum(t_kl, axis=0, keepdims=True) + dd_row * d_row
            db = db + jnp.where(row == CHUNK // 2 - 1, db_mid, 0.0) + jnp.where(row == CHUNK - 1, db_last, 0.0)
            dlogf = _dot(_tri(False).astype(F32), db, 1, 0, precision=HI)
            dfv = dlogf / w["f"] - dkf
            sg = w["sg"]
            dz_ref[rows, pl.ds(HG_W + h * HD, HD)] = (dfv * (1.0 - lb) * sg * (1.0 - sg)).astype(BF16)
            dlb_ref[:, hs] += _rowsum8(dfv * (1.0 - sg))
            sq = w["sq"]
            dz_ref[rows, hs] = (dqf * (HD ** -0.5) * (sq * (1.0 + q * (1.0 - sq)))).astype(BF16)

        def chunk(cc, carry):
            c = nc - 1 - cc
            rows = pl.ds(pl.multiple_of(c * CHUNK, CHUNK), CHUNK)
            for h in range(4):
                head(h, c, rows)
            return carry

        lax.fori_loop(0, nc, chunk, 0, unroll=True)

    rev = lambda j: nb - 1 - j
    return pl.pallas_call(
        body, name="hgrn_bwd",
        out_shape=(jax.ShapeDtypeStruct((t, 4 * HG_W), BF16), jax.ShapeDtypeStruct((8, HG_W), F32),
                   jax.ShapeDtypeStruct((8, HD), F32)),
        grid=(nb,),
        in_specs=_hgrn_cols(rev) + [pl.BlockSpec((2, HG_W), lambda j: (0, 0)), pl.BlockSpec((1, HD), lambda j: (0, 0)),
                                    pl.BlockSpec((HB, HG_W), lambda j: (rev(j), 0)),
                                    pl.BlockSpec((4, nc, HD, HD), lambda j: (0, rev(j), 0, 0)),
                                    pl.BlockSpec((HB, HG_W), lambda j: (rev(j), 1))],
        out_specs=(pl.BlockSpec((HB, 4 * HG_W), lambda j: (rev(j), 0)), pl.BlockSpec((8, HG_W), lambda j: (0, 0)),
                   pl.BlockSpec((8, HD), lambda j: (0, 0))),
        scratch_shapes=[pltpu.VMEM((4, HD, HD), F32)],
        compiler_params=_params(dimension_semantics=("arbitrary",)),
    )(*[z] * 8, hgrn_lb, onorm, o_save, sprev, dymix)


def _assemble_dz(dza, dzb, t):
    tb = min(512, t)

    def body(a_ref, b_ref, o_ref):
        o_ref[:, 0:ZQH] = a_ref[...]
        o_ref[:, ZQH:D_IN] = b_ref[...]

    row = lambda w: pl.BlockSpec((tb, w), lambda i: (i, 0))
    return pl.pallas_call(
        body, name="assemble_dz", out_shape=jax.ShapeDtypeStruct((t, D_IN), BF16), grid=(t // tb,),
        in_specs=[row(ZQH), row(D_IN - ZQH)], out_specs=row(D_IN), compiler_params=_params(),
    )(dza, dzb)


XB = 512


def _xattn_fwd(q, k, v, t):
    tb = min(XB, t)

    def body(q_ref, k_ref, v_ref, o_ref):
        for h in range(XH):
            cols = slice(XD * h, XD * (h + 1))
            s = _dot(q_ref[:, cols], k_ref[:, cols], 1, 1) * (XD ** -0.5)
            p = jnp.exp(s - jnp.max(s, axis=-1, keepdims=True))
            l = jnp.sum(p, axis=-1, keepdims=True)
            o_ref[:, cols] = (_dot(p.astype(BF16), v_ref[:, cols], 1, 0) * (1.0 / l)).astype(BF16)

    row = pl.BlockSpec((tb, D), lambda i: (i, 0))
    mem = pl.BlockSpec(k.shape, lambda i: (0, 0))
    return pl.pallas_call(
        body, name="xattn_fwd", out_shape=jax.ShapeDtypeStruct((t, D), BF16), grid=(t // tb,),
        in_specs=[row, mem, mem], out_specs=row, compiler_params=_params(),
    )(q, k, v)


def _xattn_bwd(q, k, v, do, t):
    tb = min(XB, t)

    def body(q_ref, k_ref, v_ref, do_ref, dq_ref, dk_ref, dv_ref):
        @pl.when(pl.program_id(0) == 0)
        def _():
            dk_ref[...] = jnp.zeros_like(dk_ref)
            dv_ref[...] = jnp.zeros_like(dv_ref)

        for h in range(XH):
            cols = slice(XD * h, XD * (h + 1))
            qh, kh, vh, doh = q_ref[:, cols], k_ref[:, cols], v_ref[:, cols], do_ref[:, cols]
            s = _dot(qh, kh, 1, 1) * (XD ** -0.5)
            p = jnp.exp(s - jnp.max(s, axis=-1, keepdims=True))
            p = p * (1.0 / jnp.sum(p, axis=-1, keepdims=True))
            dp = _dot(doh, vh, 1, 1)
            ds = (p * (dp - jnp.sum(p * dp, axis=-1, keepdims=True)) * (XD ** -0.5)).astype(BF16)
            dq_ref[:, cols] = _dot(ds, kh, 1, 0).astype(BF16)
            dk_ref[:, cols] += _dot(ds, qh, 0, 0)
            dv_ref[:, cols] += _dot(p.astype(BF16), doh, 0, 0)

    row = pl.BlockSpec((tb, D), lambda i: (i, 0))
    mem = pl.BlockSpec(k.shape, lambda i: (0, 0))
    return pl.pallas_call(
        body, name="xattn_bwd",
        out_shape=(jax.ShapeDtypeStruct((t, D), BF16), jax.ShapeDtypeStruct(k.shape, F32),
                   jax.ShapeDtypeStruct(k.shape, F32)),
        grid=(t // tb,), in_specs=[row, mem, mem, row], out_specs=(row, mem, mem),
        compiler_params=_params(dimension_semantics=("arbitrary",)),
    )(q, k, v, do)


def _mem_gain_bwd(dm, mem, *, name):
    def body(dm_ref, m_ref, dg_ref):
        m_ = m_ref[...]
        dg_ref[...] = _rowsum8(dm_ref[...] * (m_ * _rstd(m_)))

    return pl.pallas_call(body, name=name, out_shape=jax.ShapeDtypeStruct((8, D), F32),
                          compiler_params=_params())(dm, mem)


FM, FN = 512, 1408


def _ffn_up(u, wgt, wut, t):
    tm = min(FM, t)

    def body(u_ref, wg_ref, wu_ref, g_ref, up_ref, a_ref):
        u_ = u_ref[...]
        g = _dot(u_, wg_ref[...], 1, 1)
        up = _dot(u_, wu_ref[...], 1, 1)
        g_ref[...] = g.astype(BF16)
        up_ref[...] = up.astype(BF16)
        a_ref[...] = (g * _sig(g) * up).astype(BF16)

    w = pl.BlockSpec((FN, D), lambda j, i: (j, 0))
    o = pl.BlockSpec((tm, FN), lambda j, i: (i, j))
    return pl.pallas_call(
        body, name="ffn_up", out_shape=(jax.ShapeDtypeStruct((t, D_FF), BF16),) * 3,
        grid=(D_FF // FN, t // tm), in_specs=[pl.BlockSpec((tm, D), lambda j, i: (i, 0)), w, w],
        out_specs=(o, o, o), compiler_params=_params(),
    )(u, wgt, wut)


def _ffn_down_bwd(dy, wd, gate, up, t, dep=None):
    tm = min(FM, t)
    deps = [] if dep is None else [dep]

    def body(dy_ref, w_ref, g_ref, up_ref, *rest):
        dg_ref, dup_ref = rest[len(deps):]
        da = _dot(dy_ref[...], w_ref[...], 1, 1)
        g = g_ref[...].astype(F32)
        sg = _sig(g)
        dup_ref[...] = (da * g * sg).astype(BF16)
        dg_ref[...] = (da * up_ref[...].astype(F32) * (sg * (1.0 + g * (1.0 - sg)))).astype(BF16)

    o = pl.BlockSpec((tm, FN), lambda j, i: (i, j))
    return pl.pallas_call(
        body, name="ffn_down_bwd", out_shape=(jax.ShapeDtypeStruct((t, D_FF), BF16),) * 2,
        grid=(D_FF // FN, t // tm),
        in_specs=[pl.BlockSpec((tm, D), lambda j, i: (i, 0)), pl.BlockSpec((FN, D), lambda j, i: (j, 0)), o, o]
        + [ANY_SPEC] * len(deps),
        out_specs=(o, o), compiler_params=_params(),
    )(dy, wd, gate, up, *deps)


def _local_step(x, mem, target, fetch, sm, emit=None):
    t = x.shape[0]
    w, gw = {}, {}

    def out(key, g):
        gw[key] = g
        return None if emit is None else emit(key, g)
    u1 = _prenorm(x, sm["g_mix_pre"], name="prenorm_mix")
    w["winT"] = fetch("winT", u1)
    z = _mm(u1, w["winT"], tb=True, out_dtype=F32, tm=1024, tn=1408, name="mm_z", n_outer=True)
    ymix, lse = _swa_fwd(z, sm["sinks"], t)
    ymix, o_h, sprev = _hgrn_fwd(z, sm["hgrn_lb"], sm["hgrn_onorm"], ymix, t)
    w["wout"] = fetch("wout", ymix)
    y1 = _mm(ymix, w["wout"], out_dtype=F32, tm=1024, tn=1024, name="mm_y1")
    h1, u2 = _post_pre(x, y1, sm["g_mix_post"], sm["g_x_pre"], name="post_mix")
    mn = _prenorm(mem, sm["g_mem"], name="prenorm_mem")
    for key in ("wq", "wk", "wv"):
        w[key] = fetch(key, u2)
    qx = _mm(u2, w["wq"], out_dtype=BF16, tm=1024, tn=1024, name="mm_qx")
    kx = _mm(mn, w["wk"], out_dtype=BF16, tm=1024, tn=1024, name="mm_kx")
    vx = _mm(mn, w["wv"], out_dtype=BF16, tm=1024, tn=1024, name="mm_vx")
    ox = _xattn_fwd(qx, kx, vx, t)
    w["wo"] = fetch("wo", ox)
    y2 = _mm(ox, w["wo"], out_dtype=F32, tm=1024, tn=1024, name="mm_y2")
    h2, u3 = _post_pre(h1, y2, sm["g_x_post"], sm["g_ffn_pre"], name="post_x")
    w["wgT"], w["wuT"] = fetch("wgT", u3), fetch("wuT", u3)
    gate, up, act = _ffn_up(u3, w["wgT"], w["wuT"], t)
    w["wd"] = fetch("wd", act)
    y3 = _mm(act, w["wd"], out_dtype=F32, tm=1024, tn=1024, name="mm_y3")
    sq, dh3, dy3, dg_ffn_post = _final_loss(h2, y3, sm["g_ffn_post"], target, name="final_loss")
    dep = out("wd", _mm(act, dy3, ta=True, out_dtype=BF16, tm=1408, tn=1024, tk=512, name="mm_gwd"))
    dgate, dup = _ffn_down_bwd(dy3, w["wd"], gate, up, t, dep=dep)
    dep = out("wgT", _mm(dgate, u3, ta=True, out_dtype=BF16, tm=1408, tn=1024, tk=512, name="mm_gwg"))
    dep = out("wuT", _mm(dup, u3, ta=True, out_dtype=BF16, tm=1408, tn=1024, tk=512, name="mm_gwu", dep=dep))
    du3 = _mm2(dgate, w["wgT"], dup, w["wuT"], tm=512, tk=D_FF, name="mm_du3", dep=dep)
    dh2, dy2, dg_ffn_pre, dg_x_post = _post_pre_bwd(dh3, du3, h2, y2, sm["g_x_post"], sm["g_ffn_pre"], name="post_x_bwd")
    dep = out("wo", _mm(ox, dy2, ta=True, out_dtype=BF16, tm=1024, tn=1024, tk=512, name="mm_gwo"))
    dox = _mm(dy2, w["wo"], tb=True, out_dtype=BF16, tm=1024, tn=1024, name="mm_dox", dep=dep)
    dqx, dkx, dvx = _xattn_bwd(qx, kx, vx, dox, t)
    dep = out("wq", _mm(u2, dqx, ta=True, out_dtype=BF16, tm=1024, tn=1024, tk=512, name="mm_gwq"))
    dep = out("wk", _mm(mn, dkx, ta=True, out_dtype=BF16, tm=1024, tn=1024, name="mm_gwk", dep=dep))
    dep = out("wv", _mm(mn, dvx, ta=True, out_dtype=BF16, tm=1024, tn=1024, name="mm_gwv", dep=dep))
    du2 = _mm(dqx, w["wq"], tb=True, out_dtype=F32, tm=1024, tn=1024, name="mm_du2", dep=dep)
    dmn = _mm2(dkx, w["wk"], dvx, w["wv"], tb=True, tm=256, tk=1024, name="mm_dmn")
    dg_mem = _mem_gain_bwd(dmn, mem, name="mem_gain_bwd")
    dh1, dy1, dg_x_pre, dg_mix_post = _post_pre_bwd(dh2, du2, h1, y1, sm["g_mix_post"], sm["g_x_pre"], name="post_mix_bwd")
    dep = out("wout", _mm(ymix, dy1, ta=True, out_dtype=BF16, tm=1024, tn=1024, tk=512, name="mm_gwout"))
    dymix = _mm(dy1, w["wout"], tb=True, out_dtype=BF16, tm=1024, tn=1024, name="mm_dymix", dep=dep)
    dza, dsinks = _swa_bwd(z, sm["sinks"], ymix, lse, dymix, t)
    dzb, dlb, donorm = _hgrn_bwd(z, sm["hgrn_lb"], sm["hgrn_onorm"], o_h, sprev, dymix, t)
    dz = _assemble_dz(dza, dzb, t)
    dep = out("winT", _mm(dz, u1, ta=True, out_dtype=BF16, tm=1408, tn=1024, tk=512, name="mm_gwin"))
    du1 = _mm(dz, w["winT"], out_dtype=F32, tm=512, tn=1024, name="mm_du1", dep=dep)
    grad_x, dg_mix_pre = _pre_bwd(dh1, du1, x, sm["g_mix_pre"], name="pre_mix_bwd")
    parts = dict(g_mix_pre=dg_mix_pre, g_mix_post=dg_mix_post, g_mem=dg_mem, g_x_pre=dg_x_pre,
                 g_x_post=dg_x_post, g_ffn_pre=dg_ffn_pre, g_ffn_post=dg_ffn_post,
                 hgrn_onorm=donorm, hgrn_lb=dlb, sinks=dsinks, sq=sq)
    return grad_x, gw, parts


def _position():
    return lax.axis_index("x"), lax.axis_index("y"), lax.axis_index("c")


def _peer(pos, k):
    x, y, c = pos
    return (1 - x if k & 4 else x, 1 - y if k & 2 else y, 1 - c if k & 1 else c)


def _linear(pos):
    x, y, c = pos
    return 4 * x + 2 * y + c


HBM_SPEC = pl.BlockSpec(memory_space=pltpu.HBM)
SEM_SPEC = pl.BlockSpec(memory_space=pltpu.SEMAPHORE)
DATAFLOW = pltpu.SideEffectType.DATAFLOW_SIDE_EFFECTING
SEND_ORDER = (1, 2, 4, 3, 5, 6, 7)


def _in_hbm(a):
    return pltpu.with_memory_space_constraint(a, pltpu.HBM)


def _prepare_weights(shards):
    n = len(shards)

    def body(*refs):
        ins, outs, lands, sem = refs[:n], refs[n:2 * n], refs[2 * n:3 * n], refs[3 * n]
        me_lin = _linear(_position())
        copies = []
        for a in range(n):
            r = ins[a].shape[0]
            outs[a][...] = ins[a][...].astype(BF16)
            copies.append(pltpu.make_async_copy(outs[a], lands[a].at[pl.ds(me_lin * r, r), :], sem.at[a]))
            copies[-1].start()
        for cp in copies:
            cp.wait()

    vmem = pl.BlockSpec(memory_space=pltpu.VMEM)
    res = pl.pallas_call(
        body, name="prepare_weights",
        out_shape=tuple(jax.ShapeDtypeStruct(s.shape, BF16) for s in shards)
        + tuple(jax.ShapeDtypeStruct((N_DEV * s.shape[0], s.shape[1]), BF16) for s in shards),
        in_specs=[vmem] * n, out_specs=tuple([vmem] * n + [ANY_SPEC] * n),
        scratch_shapes=[pltpu.SemaphoreType.DMA((n,))], compiler_params=_params(),
    )(*shards)
    return res[:n], res[n:]


def _gather_start(shards, lands):
    n = len(shards)
    rows = [s.shape[0] for s in shards]

    def body(*refs):
        srcs, land = refs[:n], refs[n:2 * n]
        send_sems, recv_sems = refs[2 * n:3 * n], refs[3 * n:4 * n]
        me = _position()
        for a in range(n):
            mine = land[a].at[pl.ds(_linear(me) * rows[a], rows[a]), :]
            for k in SEND_ORDER:
                pltpu.make_async_remote_copy(
                    src_ref=srcs[a], dst_ref=mine, send_sem=send_sems[a].at[k - 1], recv_sem=recv_sems[a].at[k - 1],
                    device_id=_peer(me, k), device_id_type=MESH).start()

    sems = tuple(pltpu.SemaphoreType.DMA((N_DEV - 1,)) for _ in range(2 * n))
    res = pl.pallas_call(
        body, name="weights_send",
        out_shape=sems + tuple(pltpu.HBM(s.shape, s.dtype) for s in shards)
        + tuple(pltpu.HBM(l.shape, l.dtype) for l in lands),
        in_specs=(HBM_SPEC,) * (2 * n), out_specs=(SEM_SPEC,) * (2 * n) + (HBM_SPEC,) * (2 * n),
        input_output_aliases={i: 2 * n + i for i in range(2 * n)},
        compiler_params=pltpu.CompilerParams(has_side_effects=DATAFLOW),
    )(*[_in_hbm(s) for s in shards], *[_in_hbm(l) for l in lands])
    return [(res[a], res[n + a], res[2 * n + a], res[3 * n + a]) for a in range(n)]


def _gather_wait(send_sems, recv_sems, shard_thru, land_thru, after, *, name):
    r = shard_thru.shape[0]

    def body(src_ref, land_ref, send_sems, recv_sems, after_ref, src_dead, got_ref):
        del after_ref, src_dead, got_ref
        me = _position()
        for k in SEND_ORDER:
            peer = _peer(me, k)
            copy = pltpu.make_async_remote_copy(
                src_ref=src_ref, dst_ref=land_ref.at[pl.ds(_linear(peer) * r, r), :],
                send_sem=send_sems.at[k - 1], recv_sem=recv_sems.at[k - 1],
                device_id=peer, device_id_type=MESH)
            copy.wait_send()
            copy.wait_recv()

    return pl.pallas_call(
        body, name=name,
        out_shape=(pltpu.HBM(shard_thru.shape, shard_thru.dtype), pltpu.HBM(land_thru.shape, land_thru.dtype)),
        in_specs=(HBM_SPEC, HBM_SPEC, SEM_SPEC, SEM_SPEC, ANY_SPEC),
        out_specs=(HBM_SPEC, HBM_SPEC), input_output_aliases={0: 0, 1: 1},
        compiler_params=pltpu.CompilerParams(has_side_effects=DATAFLOW),
    )(shard_thru, land_thru, send_sems, recv_sems, after)[1]


def _exchange_start(g, *, name):
    r = g.shape[0] // N_DEV
    land_shape = (N_DEV - 1, r, g.shape[1])

    def body(g_ref, land_ref, send_sems, recv_sems, g_thru, land_thru):
        del g_thru, land_thru
        me = _position()
        for k in SEND_ORDER:
            peer = _peer(me, k)
            pltpu.make_async_remote_copy(
                src_ref=g_ref.at[pl.ds(_linear(peer) * r, r), :], dst_ref=land_ref.at[k - 1],
                send_sem=send_sems.at[k - 1], recv_sem=recv_sems.at[k - 1],
                device_id=peer, device_id_type=MESH).start()

    return pl.pallas_call(
        body, name=name,
        out_shape=(pltpu.SemaphoreType.DMA((N_DEV - 1,)), pltpu.SemaphoreType.DMA((N_DEV - 1,)),
                   pltpu.HBM(g.shape, g.dtype), pltpu.HBM(land_shape, g.dtype)),
        in_specs=(HBM_SPEC, HBM_SPEC), out_specs=(SEM_SPEC, SEM_SPEC, HBM_SPEC, HBM_SPEC),
        input_output_aliases={0: 2, 1: 3},
        compiler_params=pltpu.CompilerParams(has_side_effects=DATAFLOW),
    )(_in_hbm(g), _in_hbm(lax.empty(land_shape, g.dtype)))


def _exchange_wait(send_sems, recv_sems, g_thru, land_thru, after, *, name):
    r = land_thru.shape[1]

    def body(g_ref, land_ref, send_sems, recv_sems, after_ref, g_dead, got_ref):
        del after_ref, g_dead, got_ref
        me = _position()
        for k in SEND_ORDER:
            peer = _peer(me, k)
            copy = pltpu.make_async_remote_copy(
                src_ref=g_ref.at[pl.ds(_linear(peer) * r, r), :], dst_ref=land_ref.at[k - 1],
                send_sem=send_sems.at[k - 1], recv_sem=recv_sems.at[k - 1],
                device_id=peer, device_id_type=MESH)
            copy.wait_send()
            copy.wait_recv()

    return pl.pallas_call(
        body, name=name,
        out_shape=(pltpu.HBM(g_thru.shape, g_thru.dtype), pltpu.HBM(land_thru.shape, land_thru.dtype)),
        in_specs=(HBM_SPEC, HBM_SPEC, SEM_SPEC, SEM_SPEC, pl.BlockSpec(memory_space=pl.ANY)),
        out_specs=(HBM_SPEC, HBM_SPEC), input_output_aliases={0: 0, 1: 1},
        compiler_params=pltpu.CompilerParams(has_side_effects=DATAFLOW),
    )(g_thru, land_thru, send_sems, recv_sems, after)


def _adamw_math(w, g, m, v):
    m = B1 * m + (1.0 - B1) * g
    v = B2 * v + (1.0 - B2) * (g * g)
    delta = -LR * ((m / C1) / (jnp.sqrt(v / C2) + AEPS) + WD * w)
    return delta, m, v


def _sum_adamw(own, land, w, m, v, *, name):
    def body(own_ref, land_ref, w_ref, m_ref, v_ref, g_ref, d_ref, nm_ref, nv_ref):
        g = own_ref[...].astype(F32)
        for s in range(N_DEV - 1):
            g = g + land_ref[s].astype(F32)
        g_ref[...] = g
        d_ref[...], nm_ref[...], nv_ref[...] = _adamw_math(w_ref[...], g, m_ref[...], v_ref[...])

    return pl.pallas_call(body, name=name, out_shape=(jax.ShapeDtypeStruct(w.shape, F32),) * 4,
                          compiler_params=_params())(own, land, w, m, v)


SMALL = ("g_mix_pre", "g_mix_post", "g_mem", "g_x_pre", "g_x_post", "g_ffn_pre", "g_ffn_post",
         "hgrn_onorm", "hgrn_lb", "sinks")
SMALL_W = dict(hgrn_onorm=HD, hgrn_lb=HG_W, sinks=8)
SQ_ROW = len(SMALL)
PACK_ROWS = 16


def _small_allreduce(parts):
    ns = len(SMALL)

    def body(*refs):
        part, tot_ref = refs[:ns + 1], refs[ns + 1]
        gath, send_sems, recv_sems = refs[ns + 2:]
        me = _position()
        mine = gath.at[_linear(me)]
        mine[...] = jnp.zeros((PACK_ROWS, D), F32)
        for r, name in enumerate(SMALL):
            wd = SMALL_W.get(name, D)
            mine[r:r + 1, 0:wd] = jnp.sum(part[r][...], axis=0, keepdims=True)[:, 0:wd]
        sq = jnp.sum(part[ns][...]) * (0.5 / D)
        mine[SQ_ROW:SQ_ROW + 1, :] = jnp.full((1, D), sq, F32)

        def copy(k):
            peer = _peer(me, k)
            return pltpu.make_async_remote_copy(
                src_ref=mine, dst_ref=mine, send_sem=send_sems.at[k - 1], recv_sem=recv_sems.at[k - 1],
                device_id=peer, device_id_type=MESH)

        def arrival(k):
            slot = gath.at[_linear(_peer(me, k))]
            return pltpu.make_async_remote_copy(
                src_ref=slot, dst_ref=slot, send_sem=send_sems.at[k - 1], recv_sem=recv_sems.at[k - 1],
                device_id=_peer(me, k), device_id_type=MESH)

        sent = [copy(k) for k in range(1, 8)]
        for cp in sent:
            cp.start()
        for k in range(1, 8):
            arrival(k).wait_recv()
        for cp in sent:
            cp.wait_send()
        tot = gath[0]
        for s in range(1, N_DEV):
            tot = tot + gath[s]
        tot_ref[...] = tot

    return pl.pallas_call(
        body, name="small_allreduce", out_shape=jax.ShapeDtypeStruct((PACK_ROWS, D), F32),
        scratch_shapes=[pltpu.VMEM((N_DEV, PACK_ROWS, D), F32), pltpu.SemaphoreType.DMA((7,)),
                        pltpu.SemaphoreType.DMA((7,))],
        compiler_params=_params(has_side_effects=True),
    )(*[parts[n] for n in SMALL], parts["sq"])


def _small_update(tot, sm, m_sm, v_sm):
    ns = len(SMALL)

    def body(*refs):
        tot = refs[0][...]
        w_refs, m_refs, v_refs = refs[1:ns + 1], refs[ns + 1:2 * ns + 1], refs[2 * ns + 1:3 * ns + 1]
        outs = refs[3 * ns + 1:]
        loss_ref = outs[0]
        g_out, d_out = outs[1:ns + 1], outs[ns + 1:2 * ns + 1]
        nm_out, nv_out = outs[2 * ns + 1:3 * ns + 1], outs[3 * ns + 1:4 * ns + 1]
        loss_ref[...] = tot[SQ_ROW:SQ_ROW + 1, 0:1]
        for r, name in enumerate(SMALL):
            wd = SMALL_W.get(name, D)
            g = tot[r:r + 1, 0:wd]
            w = w_refs[r][...]
            if name == "hgrn_lb":
                mx = jnp.maximum(w[0:1], w[1:2])
                e0, e1 = jnp.exp(w[0:1] - mx), jnp.exp(w[1:2] - mx)
                lb0 = e0 / (e0 + e1)
                g0 = g * lb0 * (1.0 - lb0)
                for i, gi in enumerate((g0, -g0)):
                    d, nm, nv = _adamw_math(w[i:i + 1], gi, m_refs[r][i:i + 1, :], v_refs[r][i:i + 1, :])
                    g_out[r][i:i + 1, :] = gi
                    d_out[r][i:i + 1, :], nm_out[r][i:i + 1, :], nv_out[r][i:i + 1, :] = d, nm, nv
            else:
                d, nm, nv = _adamw_math(w, g, m_refs[r][...], v_refs[r][...])
                g_out[r][...] = g
                d_out[r][...], nm_out[r][...], nv_out[r][...] = d, nm, nv

    shapes = [jax.ShapeDtypeStruct(sm[n].shape, F32) for n in SMALL]
    res = pl.pallas_call(
        body, name="small_update", out_shape=tuple([jax.ShapeDtypeStruct((1, 1), F32)] + shapes * 4),
        compiler_params=_params(),
    )(tot, *[sm[n] for n in SMALL], *[m_sm[n] for n in SMALL], *[v_sm[n] for n in SMALL])
    groups = [dict(zip(SMALL, res[1 + i * ns:1 + (i + 1) * ns])) for i in range(4)]
    return res[0], groups[0], groups[1], groups[2], groups[3]


BIG = ("w_in", "w_gate", "w_up", "w_down", "w_out", "wq_x", "wk_x", "wv_x", "wo_x")
BIG_KEY = dict(w_in="winT", w_gate="wgT", w_up="wuT", w_down="wd", w_out="wout", wq_x="wq", wk_x="wk",
               wv_x="wv", wo_x="wo")
TRANSPOSED = ("w_in", "w_gate", "w_up")
WEIGHTS = ("w_in", "sinks", "hgrn_lb", "hgrn_onorm", "w_out", "g_mix_pre", "g_mix_post", "g_mem", "g_x_pre",
           "g_x_post", "wq_x", "wk_x", "wv_x", "wo_x", "g_ffn_pre", "g_ffn_post", "w_gate", "w_up", "w_down")


def kernel(x, mem, w_in, sinks, hgrn_lb, hgrn_onorm, w_out, g_mix_pre, g_mix_post, g_mem, g_x_pre, g_x_post, wq_x, wk_x, wv_x, wo_x, g_ffn_pre, g_ffn_post, w_gate, w_up, w_down, loss_target, m_w_in, m_sinks, m_hgrn_lb, m_hgrn_onorm, m_w_out, m_g_mix_pre, m_g_mix_post, m_g_mem, m_g_x_pre, m_g_x_post, m_wq_x, m_wk_x, m_wv_x, m_wo_x, m_g_ffn_pre, m_g_ffn_post, m_w_gate, m_w_up, m_w_down, v_w_in, v_sinks, v_hgrn_lb, v_hgrn_onorm, v_w_out, v_g_mix_pre, v_g_mix_post, v_g_mem, v_g_x_pre, v_g_x_post, v_wq_x, v_wk_x, v_wv_x, v_wo_x, v_g_ffn_pre, v_g_ffn_post, v_w_gate, v_w_up, v_w_down):
    given = dict(locals())
    wts = {n: given[n] for n in WEIGHTS}
    ms = {n: given["m_" + n] for n in WEIGHTS}
    vs = {n: given["v_" + n] for n in WEIGHTS}

    def mat(a, name):
        a = a[0]
        return a.T if name in TRANSPOSED else a

    order = ("w_in", "w_out", "wq_x", "wk_x", "wv_x", "wo_x", "w_gate", "w_up", "w_down")
    flying = dict(zip(order, _gather_start(*_prepare_weights([mat(wts[n], n) for n in order]))))
    name_of = {k: n for n, k in BIG_KEY.items()}

    def fetch(key, after):
        return _gather_wait(*flying[name_of[key]], after, name="weights_recv_" + name_of[key])

    sm = {n: wts[n] for n in SMALL}
    started = {}

    def emit(key, g):
        started[name_of[key]] = _exchange_start(g, name="grad_send_" + name_of[key])
        return started[name_of[key]][2]

    grad_x, _, parts = _local_step(x[0], mem[0], loss_target[0], fetch, sm, emit)
    me_lin = _linear(_position())
    grads, deltas, new_m, new_v = {}, {}, {}, {}
    after = grad_x
    for n in ("w_down", "w_gate", "w_up", "wo_x", "wq_x", "wk_x", "wv_x", "w_out", "w_in"):
        g_all, land = _exchange_wait(*started[n], after, name="grad_recv_" + n)
        r = land.shape[1]
        own = lax.dynamic_slice_in_dim(g_all, me_lin * r, r, 0)
        res = _sum_adamw(own, land, mat(wts[n], n), mat(ms[n], n), mat(vs[n], n), name="adamw_" + n)
        after = res[1]
        if n in TRANSPOSED:
            res = [a.T for a in res]
        grads[n], deltas[n], new_m[n], new_v[n] = [a[None] for a in res]
    loss, g_s, d_s, m_s, v_s = _small_update(_small_allreduce(parts), sm, {n: ms[n] for n in SMALL},
                                             {n: vs[n] for n in SMALL})
    grads.update(g_s), deltas.update(d_s), new_m.update(m_s), new_v.update(v_s)
    return (loss[0, 0], grad_x[None], *[grads[n] for n in WEIGHTS], *[deltas[n] for n in WEIGHTS],
            *[new_m[n] for n in WEIGHTS], *[new_v[n] for n in WEIGHTS])
```

```python
import functools

import jax
import jax.numpy as jnp
from jax import lax
from jax.experimental import pallas as pl
from jax.experimental.pallas import tpu as pltpu

F32 = jnp.float32
BF16 = jnp.bfloat16

D = 1024
D_IN = 2816
D_FF = 2816
CHUNK = 64
SWA_W = 512
KV_W = 128
HG_W = 512
HD = 128
ZQH, ZFH, ZIH, ZGH = 768, 1280, 1792, 2304
XH, XD = 4, 256
EPS = 1e-6
NEG = -1e30
N_DEV = 8
MESH = pl.DeviceIdType.MESH

LR, B1, B2, AEPS, WD, STEP = 0.001, 0.9, 0.999, 1e-08, 0.01, 10
C1 = 1.0 - B1 ** STEP
C2 = 1.0 - B2 ** STEP

VMEM_LIMIT = 56 * 1024 * 1024


def _params(**kw):
    return pltpu.CompilerParams(vmem_limit_bytes=VMEM_LIMIT, **kw)


def _sig(x):
    return 1.0 / (1.0 + jnp.exp(-x))


def _rowsum8(x):
    r, w = x.shape
    return jnp.sum(x.reshape(r // 8, 8, w), axis=0)


def _dot(a, b, ca, cb, precision=None):
    return lax.dot_general(a, b, (((ca,), (cb,)), ((), ())), preferred_element_type=F32,
                           precision=precision)


ANY_SPEC = pl.BlockSpec(memory_space=pl.ANY)


def _mm(a, b, *, ta=False, tb=False, out_dtype, tm, tn, tk=None, name, dep=None, n_outer=False):
    m = a.shape[1] if ta else a.shape[0]
    k = a.shape[0] if ta else a.shape[1]
    n = b.shape[0] if tb else b.shape[1]
    tm, tn = min(tm, m), min(tn, n)
    tk = k if tk is None else min(tk, k)
    nk = k // tk
    assert m % tm == 0 and n % tn == 0 and k % tk == 0, (name, m, n, k, tm, tn, tk)
    ij = (lambda g0, g1: (g1, g0)) if n_outer else (lambda g0, g1: (g0, g1))
    a_spec = (pl.BlockSpec((tk, tm), lambda g0, g1, kk: (kk, ij(g0, g1)[0])) if ta
              else pl.BlockSpec((tm, tk), lambda g0, g1, kk: (ij(g0, g1)[0], kk)))
    b_spec = (pl.BlockSpec((tn, tk), lambda g0, g1, kk: (ij(g0, g1)[1], kk)) if tb
              else pl.BlockSpec((tk, tn), lambda g0, g1, kk: (kk, ij(g0, g1)[1])))
    ca, cb = (0 if ta else 1), (1 if tb else 0)

    deps = [] if dep is None else [dep]

    def body(a_ref, b_ref, *rest):
        o_ref, acc = rest[len(deps)], rest[len(deps) + 1:]
        p = _dot(a_ref[...].astype(BF16), b_ref[...].astype(BF16), ca, cb)
        if nk == 1:
            o_ref[...] = p.astype(out_dtype)
        else:
            acc_ref, = acc
            kk = pl.program_id(2)

            @pl.when(kk == 0)
            def _():
                acc_ref[...] = p

            @pl.when(kk > 0)
            def _():
                acc_ref[...] += p

            @pl.when(kk == nk - 1)
            def _():
                o_ref[...] = acc_ref[...].astype(out_dtype)

    return pl.pallas_call(
        body, name=name, out_shape=jax.ShapeDtypeStruct((m, n), out_dtype),
        grid=(n // tn, m // tm, nk) if n_outer else (m // tm, n // tn, nk),
        in_specs=[a_spec, b_spec] + [ANY_SPEC] * len(deps),
        out_specs=pl.BlockSpec((tm, tn), lambda g0, g1, kk: ij(g0, g1)),
        scratch_shapes=[pltpu.VMEM((tm, tn), F32)] if nk > 1 else [],
        compiler_params=_params(dimension_semantics=("parallel", "parallel", "arbitrary")),
    )(a, b, *deps)


def _mm2(a1, b1, a2, b2, *, tb=False, tm, tk, name, dep=None):
    m, k = a1.shape
    n = b1.shape[0] if tb else b1.shape[1]
    tm, tk = min(tm, m), min(tk, k)
    nk = k // tk
    assert m % tm == 0 and k % tk == 0
    cb = 1 if tb else 0
    deps = [] if dep is None else [dep]

    def body(a1_ref, b1_ref, a2_ref, b2_ref, *rest):
        o_ref = rest[len(deps)]
        p = (_dot(a1_ref[...].astype(BF16), b1_ref[...], 1, cb)
             + _dot(a2_ref[...].astype(BF16), b2_ref[...], 1, cb))
        kk = pl.program_id(1)

        @pl.when(kk == 0)
        def _():
            o_ref[...] = p

        @pl.when(kk > 0)
        def _():
            o_ref[...] += p

    a_spec = pl.BlockSpec((tm, tk), lambda i, kk: (i, kk))
    b_spec = pl.BlockSpec((n, tk), lambda i, kk: (0, kk)) if tb else pl.BlockSpec((tk, n), lambda i, kk: (kk, 0))
    return pl.pallas_call(
        body, name=name, out_shape=jax.ShapeDtypeStruct((m, n), F32),
        grid=(m // tm, nk), in_specs=[a_spec, b_spec, a_spec, b_spec] + [ANY_SPEC] * len(deps),
        out_specs=pl.BlockSpec((tm, n), lambda i, kk: (i, 0)),
        compiler_params=_params(dimension_semantics=("parallel", "arbitrary")),
    )(a1, b1, a2, b2, *deps)


def _rstd(x):
    return lax.rsqrt(jnp.mean(x * x, axis=-1, keepdims=True) + EPS)


def _norm_bwd(xh, r, t):
    return r * (t - xh * jnp.mean(xh * t, axis=-1, keepdims=True))


def _prenorm(x, g, *, name):
    t, d = x.shape
    tb = min(512, t)

    def body(x_ref, g_ref, o_ref):
        xf = x_ref[...]
        o_ref[...] = (xf * _rstd(xf) * g_ref[...]).astype(BF16)

    return pl.pallas_call(
        body, name=name, out_shape=jax.ShapeDtypeStruct((t, d), BF16), grid=(t // tb,),
        in_specs=[pl.BlockSpec((tb, d), lambda i: (i, 0)), pl.BlockSpec((1, d), lambda i: (0, 0))],
        out_specs=pl.BlockSpec((tb, d), lambda i: (i, 0)), compiler_params=_params(),
    )(x, g)


def _post_pre(h, y, g_post, g_pre, *, name):
    t, d = h.shape
    tb = min(512, t)

    def body(h_ref, y_ref, gp_ref, gn_ref, hn_ref, u_ref):
        y_ = y_ref[...]
        hn = h_ref[...] + y_ * _rstd(y_) * gp_ref[...]
        hn_ref[...] = hn
        u_ref[...] = (hn * _rstd(hn) * gn_ref[...]).astype(BF16)

    row = pl.BlockSpec((tb, d), lambda i: (i, 0))
    vec = pl.BlockSpec((1, d), lambda i: (0, 0))
    return pl.pallas_call(
        body, name=name, out_shape=(jax.ShapeDtypeStruct((t, d), F32), jax.ShapeDtypeStruct((t, d), BF16)),
        grid=(t // tb,), in_specs=[row, row, vec, vec], out_specs=(row, row), compiler_params=_params(),
    )(h, y, g_post, g_pre)


def _final_loss(h, y, g_post, target, *, name):
    t, d = h.shape
    tb = min(512, t)

    def body(h_ref, y_ref, g_ref, t_ref, sq_ref, dh_ref, dy_ref, dg_ref):
        @pl.when(pl.program_id(0) == 0)
        def _():
            sq_ref[...] = jnp.zeros_like(sq_ref)
            dg_ref[...] = jnp.zeros_like(dg_ref)

        y_ = y_ref[...]
        r = _rstd(y_)
        yh = y_ * r
        g = g_ref[...]
        err = h_ref[...] + yh * g - t_ref[...]
        sq_ref[...] += _rowsum8(err * err)
        dh = err * (1.0 / d)
        dh_ref[...] = dh
        dg_ref[...] += _rowsum8(dh * yh)
        dy_ref[...] = _norm_bwd(yh, r, dh * g).astype(BF16)

    row = pl.BlockSpec((tb, d), lambda i: (i, 0))
    vec = pl.BlockSpec((1, d), lambda i: (0, 0))
    acc = pl.BlockSpec((8, d), lambda i: (0, 0))
    return pl.pallas_call(
        body, name=name,
        out_shape=(jax.ShapeDtypeStruct((8, d), F32), jax.ShapeDtypeStruct((t, d), F32),
                   jax.ShapeDtypeStruct((t, d), BF16), jax.ShapeDtypeStruct((8, d), F32)),
        grid=(t // tb,), in_specs=[row, row, vec, row], out_specs=(acc, row, row, acc),
        compiler_params=_params(dimension_semantics=("arbitrary",)),
    )(h, y, g_post, target)


def _post_pre_bwd(dh_out, du, hn, y, g_post, g_pre, *, name):
    t, d = hn.shape
    tb = min(512, t)

    def body(dho_ref, du_ref, hn_ref, y_ref, gp_ref, gn_ref, dh_ref, dy_ref, dgn_ref, dgp_ref):
        @pl.when(pl.program_id(0) == 0)
        def _():
            dgn_ref[...] = jnp.zeros_like(dgn_ref)
            dgp_ref[...] = jnp.zeros_like(dgp_ref)

        hn_ = hn_ref[...]
        r2 = _rstd(hn_)
        xh = hn_ * r2
        du_ = du_ref[...]
        dgn_ref[...] += _rowsum8(du_ * xh)
        dh = dho_ref[...] + _norm_bwd(xh, r2, du_ * gn_ref[...])
        dh_ref[...] = dh
        y_ = y_ref[...]
        r1 = _rstd(y_)
        yh = y_ * r1
        dgp_ref[...] += _rowsum8(dh * yh)
        dy_ref[...] = _norm_bwd(yh, r1, dh * gp_ref[...]).astype(BF16)

    row = pl.BlockSpec((tb, d), lambda i: (i, 0))
    vec = pl.BlockSpec((1, d), lambda i: (0, 0))
    acc = pl.BlockSpec((8, d), lambda i: (0, 0))
    return pl.pallas_call(
        body, name=name,
        out_shape=(jax.ShapeDtypeStruct((t, d), F32), jax.ShapeDtypeStruct((t, d), BF16),
                   jax.ShapeDtypeStruct((8, d), F32), jax.ShapeDtypeStruct((8, d), F32)),
        grid=(t // tb,), in_specs=[row, row, row, row, vec, vec], out_specs=(row, row, acc, acc),
        compiler_params=_params(dimension_semantics=("arbitrary",)),
    )(dh_out, du, hn, y, g_post, g_pre)


def _pre_bwd(dh_out, du, x, g, *, name):
    t, d = x.shape
    tb = min(512, t)
    has_res = dh_out is not None

    def body(*refs):
        if has_res:
            dho_ref, du_ref, x_ref, g_ref, dx_ref, dg_ref = refs
        else:
            du_ref, x_ref, g_ref, dx_ref, dg_ref = refs

        @pl.when(pl.program_id(0) == 0)
        def _():
            dg_ref[...] = jnp.zeros_like(dg_ref)

        x_ = x_ref[...]
        r = _rstd(x_)
        xh = x_ * r
        du_ = du_ref[...]
        dg_ref[...] += _rowsum8(du_ * xh)
        dx = _norm_bwd(xh, r, du_ * g_ref[...])
        if has_res:
            dx = dx + dho_ref[...]
        dx_ref[...] = dx

    row = pl.BlockSpec((tb, d), lambda i: (i, 0))
    vec = pl.BlockSpec((1, d), lambda i: (0, 0))
    acc = pl.BlockSpec((8, d), lambda i: (0, 0))
    ins = ([dh_out] if has_res else []) + [du, x, g]
    return pl.pallas_call(
        body, name=name,
        out_shape=(jax.ShapeDtypeStruct((t, d), F32), jax.ShapeDtypeStruct((8, d), F32)),
        grid=(t // tb,), in_specs=[row] * (len(ins) - 1) + [vec], out_specs=(row, acc),
        compiler_params=_params(dimension_semantics=("arbitrary",)),
    )(*ins)


QB = 256


def _half_mask(shape, e):
    lane = lax.broadcasted_iota(jnp.int32, shape, len(shape) - 1)
    return (lane // 64) == e


def _place(kv):
    sw = pltpu.roll(kv, 64, 1)
    m0 = _half_mask(kv.shape, 0)
    return [[jnp.where(m0, kv, 0.0).astype(BF16), jnp.where(m0, 0.0, sw).astype(BF16)],
            [jnp.where(m0, sw, 0.0).astype(BF16), jnp.where(m0, 0.0, kv).astype(BF16)]]


def _swa_valid_q(i, nq, nk):
    qc = lax.broadcasted_iota(jnp.int32, (nq, nk), 0) // CHUNK
    kc = lax.broadcasted_iota(jnp.int32, (nq, nk), 1) // CHUNK - 2
    return (kc <= qc) & (qc <= kc + 2) & (4 * i + kc >= 0)


def _swa_fwd(z, sinks, t):
    nb = t // QB

    def body(s_ref, q_ref, kp_ref, kc_ref, vp_ref, vc_ref, o_ref, lse_ref):
        i = pl.program_id(0)
        kpl = _place(jnp.concatenate([kp_ref[...], kc_ref[...]], axis=0))
        vpl = _place(jnp.concatenate([vp_ref[...], vc_ref[...]], axis=0))
        valid = _swa_valid_q(i, QB, QB + 128)
        lane = lax.broadcasted_iota(jnp.int32, (QB, 128), 1)
        lse_out = jnp.zeros((QB, 128), F32)
        for j in range(4):
            qp = q_ref[:, 128 * j:128 * (j + 1)].astype(BF16)
            acc = jnp.zeros((QB, 128), F32)
            for e in range(2):
                h = 2 * j + e
                kvh = h // 4
                qm = jnp.where(_half_mask(qp.shape, e), qp, jnp.zeros_like(qp))
                s = _dot(qm, kpl[kvh][e], 1, 1) * 0.125
                s = jnp.where(valid, s, NEG)
                sink = s_ref[0, h]
                m = jnp.maximum(jnp.max(s, axis=-1, keepdims=True), sink)
                p = jnp.exp(s - m)
                l = jnp.sum(p, axis=-1, keepdims=True) + jnp.exp(sink - m)
                acc = acc + _dot(p.astype(BF16), vpl[kvh][e], 1, 0) * (1.0 / l)
                lse_out = jnp.where(lane == h, m + jnp.log(l), lse_out)
            o_ref[:, 128 * j:128 * (j + 1)] = acc.astype(BF16)
        lse_ref[...] = lse_out

    prev = lambda c: pl.BlockSpec((128, 128), lambda i: (jnp.maximum(2 * i - 1, 0), c))
    cur = lambda c: pl.BlockSpec((QB, 128), lambda i: (i, c))
    return pl.pallas_call(
        body, name="swa_fwd",
        out_shape=(jax.ShapeDtypeStruct((t, D), BF16), jax.ShapeDtypeStruct((t, 128), F32)),
        grid=(nb,),
        in_specs=[pl.BlockSpec(memory_space=pltpu.SMEM),
                  pl.BlockSpec((QB, SWA_W), lambda i: (i, 0)), prev(4), cur(4), prev(5), cur(5)],
        out_specs=(pl.BlockSpec((QB, SWA_W), lambda i: (i, 0)), pl.BlockSpec((QB, 128), lambda i: (i, 0))),
        compiler_params=_params(),
    )(sinks, z, z, z, z, z)


def _swa_bwd(z, sinks, ymix, lse, dymix, t):
    nb = t // QB
    nq2 = QB + 128

    def body(s_ref, qc_ref, qn_ref, kp_ref, kc_ref, vp_ref, vc_ref, oc_ref, on_ref, doc_ref, don_ref,
             lc_ref, ln_ref, dz_ref, ds_ref):
        i = pl.program_id(0)

        @pl.when(i == 0)
        def _():
            ds_ref[...] = jnp.zeros_like(ds_ref)

        lane = lax.broadcasted_iota(jnp.int32, (8, 128), 1)
        kpl = _place(jnp.concatenate([kp_ref[...], kc_ref[...]], axis=0))
        vpl = _place(jnp.concatenate([vp_ref[...], vc_ref[...]], axis=0))
        valid = _swa_valid_q(i, QB, nq2)
        lse_c = lc_ref[...]
        dsink = jnp.zeros((8, 128), F32)
        for j in range(4):
            cols = slice(128 * j, 128 * (j + 1))
            qp = qc_ref[:, cols].astype(BF16)
            dop = doc_ref[:, cols]
            prod = dop.astype(F32) * oc_ref[:, cols].astype(F32)
            acc = jnp.zeros((QB, 128), F32)
            for e in range(2):
                h = 2 * j + e
                kvh = h // 4
                hm = _half_mask(qp.shape, e)
                qm = jnp.where(hm, qp, jnp.zeros_like(qp))
                dom = jnp.where(hm, dop, jnp.zeros_like(dop))
                dd = jnp.sum(jnp.where(hm, prod, 0.0), axis=-1, keepdims=True)
                lse_h = lse_c[:, h:h + 1]
                s = _dot(qm, kpl[kvh][e], 1, 1) * 0.125
                p = jnp.where(valid, jnp.exp(s - lse_h), 0.0)
                dp = _dot(dom, vpl[kvh][e], 1, 1)
                ds = p * (dp - dd) * 0.125
                acc = acc + _dot(ds.astype(BF16), kpl[kvh][e], 1, 0)
                ps = jnp.exp(s_ref[0, h] - lse_h) * dd
                dsink = dsink - jnp.where(lane == h, _rowsum8(jnp.broadcast_to(ps, (QB, 128))), 0.0)
            dz_ref[:, cols] = acc.astype(BF16)
        ds_ref[...] += dsink
        kpl, vpl = _place(kc_ref[...]), _place(vc_ref[...])
        qr = lax.broadcasted_iota(jnp.int32, (nq2, QB), 0) // CHUNK
        kr = lax.broadcasted_iota(jnp.int32, (nq2, QB), 1) // CHUNK
        valid2 = (kr <= qr) & (qr <= kr + 2) & (4 * i + qr < t // CHUNK)
        lse_a = jnp.concatenate([lse_c, ln_ref[...]], axis=0)
        dk_acc = [[jnp.zeros((QB, 128), F32) for _ in range(2)] for _ in range(2)]
        dv_acc = [[jnp.zeros((QB, 128), F32) for _ in range(2)] for _ in range(2)]
        for j in range(4):
            cols = slice(128 * j, 128 * (j + 1))
            qp = jnp.concatenate([qc_ref[:, cols], qn_ref[:, cols]], axis=0).astype(BF16)
            dop = jnp.concatenate([doc_ref[:, cols], don_ref[:, cols]], axis=0)
            op = jnp.concatenate([oc_ref[:, cols], on_ref[:, cols]], axis=0)
            prod = dop.astype(F32) * op.astype(F32)
            for e in range(2):
                h = 2 * j + e
                kvh = h // 4
                hm = _half_mask(qp.shape, e)
                qm = jnp.where(hm, qp, jnp.zeros_like(qp))
                dom = jnp.where(hm, dop, jnp.zeros_like(dop))
                dd = jnp.sum(jnp.where(hm, prod, 0.0), axis=-1, keepdims=True)
                s = _dot(qm, kpl[kvh][e], 1, 1) * 0.125
                p = jnp.where(valid2, jnp.exp(s - lse_a[:, h:h + 1]), 0.0)
                dv_acc[kvh][e] = dv_acc[kvh][e] + _dot(p.astype(BF16), dom, 0, 0)
                dp = _dot(dom, vpl[kvh][e], 1, 1)
                ds = p * (dp - dd) * 0.125
                dk_acc[kvh][e] = dk_acc[kvh][e] + _dot(ds.astype(BF16), qm, 0, 0)
        dk = dk_acc[0][0] + dk_acc[1][1] + pltpu.roll(dk_acc[0][1] + dk_acc[1][0], 64, 1)
        dv = dv_acc[0][0] + dv_acc[1][1] + pltpu.roll(dv_acc[0][1] + dv_acc[1][0], 64, 1)
        dz_ref[:, 512:640] = dk.astype(BF16)
        dz_ref[:, 640:768] = dv.astype(BF16)

    last = 2 * nb - 1
    prev = lambda c: pl.BlockSpec((128, 128), lambda i: (jnp.maximum(2 * i - 1, 0), c))
    cur = lambda w, c: pl.BlockSpec((QB, w), lambda i: (i, c))
    nxt = lambda w: pl.BlockSpec((128, w), lambda i: (jnp.minimum(2 * i + 2, last), 0))
    return pl.pallas_call(
        body, name="swa_bwd",
        out_shape=(jax.ShapeDtypeStruct((t, 768), BF16), jax.ShapeDtypeStruct((8, 128), F32)),
        grid=(nb,),
        in_specs=[pl.BlockSpec(memory_space=pltpu.SMEM),
                  cur(SWA_W, 0), nxt(SWA_W), prev(4), cur(128, 4), prev(5), cur(128, 5),
                  cur(SWA_W, 0), nxt(SWA_W), cur(SWA_W, 0), nxt(SWA_W), cur(128, 0), nxt(128)],
        out_specs=(pl.BlockSpec((QB, 768), lambda i: (i, 0)), pl.BlockSpec((8, 128), lambda i: (0, 0))),
        compiler_params=_params(dimension_semantics=("arbitrary",)),
    )(sinks, z, z, z, z, z, z, ymix, ymix, dymix, dymix, lse, lse)


HB = 256


def _lower_bound(lb_ref):
    a = lb_ref[...]
    a0, a1 = a[0:1], a[1:2]
    mx = jnp.maximum(a0, a1)
    e0, e1 = jnp.exp(a0 - mx), jnp.exp(a1 - mx)
    return e0 / (e0 + e1)


def _hgrn_cols(row_block):
    return [pl.BlockSpec((HB, 2 * HD), lambda j, c=base // (2 * HD) + p: (row_block(j), c))
            for base in (ZQH, ZFH, ZIH, ZGH) for p in range(2)]


NCH = HB // CHUNK


def _split3(x):
    hi = x.astype(BF16)
    r1 = x - hi.astype(F32)
    mid = r1.astype(BF16)
    return hi, mid, (r1 - mid.astype(F32)).astype(BF16)


def _blockdiag(lower):
    r = lax.broadcasted_iota(jnp.int32, (HB, HB), 0)
    c = lax.broadcasted_iota(jnp.int32, (HB, HB), 1)
    return (r // CHUNK == c // CHUNK) & ((c <= r) if lower else (c >= r))


def _chunk_sums(mask_bf16, x):
    return sum(_dot(mask_bf16, part, 1, 0) for part in _split3(x))


def _per_chunk_rows(x, row):
    w = x.shape[1]
    picked = x.reshape(NCH, CHUNK, w)[:, row:row + 1, :]
    return jnp.broadcast_to(picked, (NCH, CHUNK, w)).reshape(HB, w)


def _chunk_stack(x, chunk_of_row):
    return jnp.concatenate([jnp.where(chunk_of_row == c, x, jnp.zeros_like(x)) for c in range(NCH)], axis=1)


def _chunk_pick(x, chunk_of_row):
    w = x.shape[1] // NCH
    out = jnp.zeros((HB, w), x.dtype)
    for c in range(NCH):
        out = jnp.where(chunk_of_row == c, x[:, c * w:(c + 1) * w], out)
    return out


def _hgrn_local(q, f, kf, b):
    sq = _sig(q)
    qf = q * sq * (HD ** -0.5)
    b_mid = _per_chunk_rows(b, CHUNK // 2 - 1)
    b_last = _per_chunk_rows(b, CHUNK - 1)
    qm = qf * jnp.exp(b - b_mid)
    km = kf * jnp.exp(b_mid - b)
    kl = kf * jnp.exp(b_last - b)
    qb = qf * jnp.exp(b)
    return dict(sq=sq, b_mid=b_mid, b_last=b_last, qm=qm, km=km, kl=kl, qb=qb)


def _hgrn2_fwd(z, hgrn_lb, onorm, ymix, t):
    nb = t // HB

    def body(*refs):
        zq, zf, zi, zg = refs[0:2], refs[2:4], refs[4:6], refs[6:8]
        lb_ref, on_ref, _, y_ref, o_ref, sp_ref, st_ref = refs[8:]

        @pl.when(pl.program_id(0) == 0)
        def _():
            st_ref[...] = jnp.zeros_like(st_ref)

        lb_all = _lower_bound(lb_ref)
        gn = on_ref[...]
        low = _blockdiag(True)
        low_b = low.astype(BF16)
        chunk_of_row = lax.broadcasted_iota(jnp.int32, (HB, HD), 0) // CHUNK
        for p in range(2):
            lbp = lb_all[:, 2 * HD * p:2 * HD * (p + 1)]
            fp = lbp + (1.0 - lbp) * _sig(zf[p][...])
            bp = _chunk_sums(low_b, jnp.log(fp))
            for e in range(2):
                h, ls = 2 * p + e, slice(e * HD, (e + 1) * HD)
                f = fp[:, ls]
                w = _hgrn_local(zq[p][:, ls], f, 1.0 - f, bp[:, ls])
                iv = zi[p][:, ls].astype(BF16)
                a = jnp.where(low, _dot(w["qm"].astype(BF16), w["km"].astype(BF16), 1, 1), 0.0)
                o = _dot(a.astype(BF16), iv, 1, 0)
                u = _dot(iv, _chunk_stack(w["kl"].astype(BF16), chunk_of_row), 0, 0)
                decay = jnp.exp(w["b_last"])
                st = st_ref[h]
                states = []
                for c in range(NCH):
                    sp_ref[h, c] = st
                    states.append(st.astype(BF16))
                    st = st * decay[c * CHUNK:c * CHUNK + 1] + u[:, c * HD:(c + 1) * HD]
                st_ref[h] = st
                inter = _dot(w["qb"].astype(BF16), jnp.concatenate(states, axis=0), 1, 1)
                o = o + _chunk_pick(inter, chunk_of_row)
                hs = slice(h * HD, (h + 1) * HD)
                o_ref[:, hs] = o
                gg = zg[p][:, ls]
                y_ref[:, hs] = (o * _rstd(o) * gn * (gg * _sig(gg))).astype(BF16)

    return pl.pallas_call(
        body, name="hgrn_fwd",
        out_shape=(jax.ShapeDtypeStruct((t, D), BF16), jax.ShapeDtypeStruct((t, HG_W), F32),
                   jax.ShapeDtypeStruct((4, t // CHUNK, HD, HD), F32)),
        grid=(nb,),
        in_specs=_hgrn_cols(lambda j: j) + [pl.BlockSpec((2, HG_W), lambda j: (0, 0)),
                                            pl.BlockSpec((1, HD), lambda j: (0, 0)), ANY_SPEC],
        out_specs=(pl.BlockSpec((HB, HG_W), lambda j: (j, 1)),
                   pl.BlockSpec((HB, HG_W), lambda j: (j, 0)),
                   pl.BlockSpec((4, NCH, HD, HD), lambda j: (0, j, 0, 0))),
        scratch_shapes=[pltpu.VMEM((4, HD, HD), F32)],
        input_output_aliases={10: 0},
        compiler_params=_params(dimension_semantics=("arbitrary",)),
    )(*[z] * 8, hgrn_lb, onorm, ymix)


def _hgrn2_bwd(z, hgrn_lb, onorm, o_save, sprev, dymix, dza, t):
    nb = t // HB

    def body(*refs):
        zq, zf, zi, zg = refs[0:2], refs[2:4], refs[4:6], refs[6:8]
        lb_ref, on_ref, o_ref, sp_ref, dy_ref, dza_ref, dz_ref, dlb_ref, don_ref, dst_ref = refs[8:]

        @pl.when(pl.program_id(0) == 0)
        def _():
            dst_ref[...] = jnp.zeros_like(dst_ref)
            dlb_ref[...] = jnp.zeros_like(dlb_ref)
            don_ref[...] = jnp.zeros_like(don_ref)

        dz_ref[:, 0:ZQH] = dza_ref[...]
        lb_all = _lower_bound(lb_ref)
        gn = on_ref[...]
        low, upp = _blockdiag(True), _blockdiag(False)
        upp_b = upp.astype(BF16)
        low_b = low.astype(BF16)
        row = lax.broadcasted_iota(jnp.int32, (HB, HD), 0)
        chunk_of_row = row // CHUNK
        in_chunk = row % CHUNK
        for p in range(2):
            lbp = lb_all[:, 2 * HD * p:2 * HD * (p + 1)]
            sgp = _sig(zf[p][...])
            fp = lbp + (1.0 - lbp) * sgp
            bp = _chunk_sums(low_b, jnp.log(fp))
            db_pair, dkf_pair = [], []
            for e in range(2):
                h, ls, hs = 2 * p + e, slice(e * HD, (e + 1) * HD), slice((2 * p + e) * HD, (2 * p + e + 1) * HD)
                f = fp[:, ls]
                q = zq[p][:, ls]
                w = _hgrn_local(q, f, 1.0 - f, bp[:, ls])
                iv = zi[p][:, ls].astype(BF16)
                gg = zg[p][:, ls]
                o = o_ref[:, hs]
                dout = dy_ref[:, hs].astype(F32)
                sgg = _sig(gg)
                r = _rstd(o)
                oh = o * r
                dyn = dout * (gg * sgg)
                dz_ref[:, ZGH + h * HD:ZGH + (h + 1) * HD] = (
                    dout * oh * gn * (sgg * (1.0 + gg * (1.0 - sgg)))).astype(BF16)
                don_ref[...] += _rowsum8(dyn * oh)
                do = _norm_bwd(oh, r, dyn * gn).astype(BF16)
                qm, km, kl, qb = (w[n].astype(BF16) for n in ("qm", "km", "kl", "qb"))
                decay = jnp.exp(w["b_last"])
                grads_in = _dot(do, _chunk_stack(qb, chunk_of_row), 0, 0)
                dst = dst_ref[h]
                dstn, dd_rows = [None] * NCH, [None] * NCH
                for c in reversed(range(NCH)):
                    dstn[c] = dst.astype(BF16)
                    dd_rows[c] = jnp.sum(dst * sp_ref[h, c], axis=0, keepdims=True)
                    dst = dst * decay[c * CHUNK:c * CHUNK + 1] + grads_in[:, c * HD:(c + 1) * HD]
                dst_ref[h] = dst
                states = jnp.concatenate([sp_ref[h, c].astype(BF16) for c in range(NCH)], axis=0)
                dstn_all = jnp.concatenate(dstn, axis=0)
                dqb = _dot(_chunk_stack(do, chunk_of_row), states, 1, 0)
                at = jnp.where(upp, _dot(km, qm, 1, 1), 0.0)
                di = _dot(at.astype(BF16), do, 1, 0) + _chunk_pick(_dot(kl, dstn_all, 1, 1), chunk_of_row)
                dz_ref[:, ZIH + h * HD:ZIH + (h + 1) * HD] = di.astype(BF16)
                dkl = _dot(_chunk_stack(iv, chunk_of_row), dstn_all, 1, 0)
                da = jnp.where(low, _dot(do, iv, 1, 1), 0.0).astype(BF16)
                dat = jnp.where(upp, _dot(iv, do, 1, 1), 0.0).astype(BF16)
                dqm = _dot(da, km, 1, 0)
                dkm = _dot(dat, qm, 1, 0)
                b = bp[:, ls]
                e1, e2 = jnp.exp(b - w["b_mid"]), jnp.exp(w["b_mid"] - b)
                e3, e4 = jnp.exp(w["b_last"] - b), jnp.exp(b)
                dqf = dqm * e1 + dqb * e4
                dkf_pair.append(dkm * e2 + dkl * e3)
                t_qm, t_km, t_kl = dqm * w["qm"], dkm * w["km"], dkl * w["kl"]
                db = t_qm - t_km - t_kl + dqb * w["qb"]
                db_mid = jnp.sum((t_km - t_qm).reshape(NCH, CHUNK, HD), axis=1, keepdims=True)
                db_last = jnp.sum(t_kl.reshape(NCH, CHUNK, HD), axis=1, keepdims=True)
                db_last = db_last + jnp.stack(dd_rows, axis=0) * jnp.exp(
                    bp[:, ls].reshape(NCH, CHUNK, HD)[:, CHUNK - 1:CHUNK, :])
                spread = lambda v: jnp.broadcast_to(v, (NCH, CHUNK, HD)).reshape(HB, HD)
                db = (db + jnp.where(in_chunk == CHUNK // 2 - 1, spread(db_mid), 0.0)
                      + jnp.where(in_chunk == CHUNK - 1, spread(db_last), 0.0))
                db_pair.append(db)
                sq = w["sq"]
                dz_ref[:, ZQH + h * HD:ZQH + (h + 1) * HD] = (
                    dqf * (HD ** -0.5) * (sq * (1.0 + q * (1.0 - sq)))).astype(BF16)
            dlogf = _chunk_sums(upp_b, jnp.concatenate(db_pair, axis=1))
            dfv = dlogf / fp - jnp.concatenate(dkf_pair, axis=1)
            dz_ref[:, ZFH + 2 * HD * p:ZFH + 2 * HD * (p + 1)] = (dfv * (1.0 - lbp) * sgp * (1.0 - sgp)).astype(BF16)
            dlb_ref[:, 2 * HD * p:2 * HD * (p + 1)] += _rowsum8(dfv * (1.0 - sgp))

    rev = lambda j: nb - 1 - j
    return pl.pallas_call(
        body, name="hgrn_bwd",
        out_shape=(jax.ShapeDtypeStruct((t, D_IN), BF16), jax.ShapeDtypeStruct((8, HG_W), F32),
                   jax.ShapeDtypeStruct((8, HD), F32)),
        grid=(nb,),
        in_specs=_hgrn_cols(rev) + [pl.BlockSpec((2, HG_W), lambda j: (0, 0)), pl.BlockSpec((1, HD), lambda j: (0, 0)),
                                    pl.BlockSpec((HB, HG_W), lambda j: (rev(j), 0)),
                                    pl.BlockSpec((4, NCH, HD, HD), lambda j: (0, rev(j), 0, 0)),
                                    pl.BlockSpec((HB, HG_W), lambda j: (rev(j), 1)),
                                    pl.BlockSpec((HB, ZQH), lambda j: (rev(j), 0))],
        out_specs=(pl.BlockSpec((HB, D_IN), lambda j: (rev(j), 0)), pl.BlockSpec((8, HG_W), lambda j: (0, 0)),
                   pl.BlockSpec((8, HD), lambda j: (0, 0))),
        scratch_shapes=[pltpu.VMEM((4, HD, HD), F32)],
        compiler_params=_params(dimension_semantics=("arbitrary",)),
    )(*[z] * 8, hgrn_lb, onorm, o_save, sprev, dymix, dza)


XB = 512


def _xattn_fwd(q, k, v, t):
    tb = min(XB, t)

    def body(q_ref, k_ref, v_ref, o_ref):
        for h in range(XH):
            cols = slice(XD * h, XD * (h + 1))
            s = _dot(q_ref[:, cols], k_ref[:, cols], 1, 1) * (XD ** -0.5)
            p = jnp.exp(s - jnp.max(s, axis=-1, keepdims=True))
            l = jnp.sum(p, axis=-1, keepdims=True)
            o_ref[:, cols] = (_dot(p.astype(BF16), v_ref[:, cols], 1, 0) * (1.0 / l)).astype(BF16)

    row = pl.BlockSpec((tb, D), lambda i: (i, 0))
    mem = pl.BlockSpec(k.shape, lambda i: (0, 0))
    return pl.pallas_call(
        body, name="xattn_fwd", out_shape=jax.ShapeDtypeStruct((t, D), BF16), grid=(t // tb,),
        in_specs=[row, mem, mem], out_specs=row, compiler_params=_params(),
    )(q, k, v)


def _xattn_bwd(q, k, v, do, t):
    tb = min(XB, t)

    def body(q_ref, k_ref, v_ref, do_ref, dq_ref, dk_ref, dv_ref):
        @pl.when(pl.program_id(0) == 0)
        def _():
            dk_ref[...] = jnp.zeros_like(dk_ref)
            dv_ref[...] = jnp.zeros_like(dv_ref)

        for h in range(XH):
            cols = slice(XD * h, XD * (h + 1))
            qh, kh, vh, doh = q_ref[:, cols], k_ref[:, cols], v_ref[:, cols], do_ref[:, cols]
            s = _dot(qh, kh, 1, 1) * (XD ** -0.5)
            p = jnp.exp(s - jnp.max(s, axis=-1, keepdims=True))
            p = p * (1.0 / jnp.sum(p, axis=-1, keepdims=True))
            dp = _dot(doh, vh, 1, 1)
            ds = (p * (dp - jnp.sum(p * dp, axis=-1, keepdims=True)) * (XD ** -0.5)).astype(BF16)
            dq_ref[:, cols] = _dot(ds, kh, 1, 0).astype(BF16)
            dk_ref[:, cols] += _dot(ds, qh, 0, 0)
            dv_ref[:, cols] += _dot(p.astype(BF16), doh, 0, 0)

    row = pl.BlockSpec((tb, D), lambda i: (i, 0))
    mem = pl.BlockSpec(k.shape, lambda i: (0, 0))
    return pl.pallas_call(
        body, name="xattn_bwd",
        out_shape=(jax.ShapeDtypeStruct((t, D), BF16), jax.ShapeDtypeStruct(k.shape, F32),
                   jax.ShapeDtypeStruct(k.shape, F32)),
        grid=(t // tb,), in_specs=[row, mem, mem, row], out_specs=(row, mem, mem),
        compiler_params=_params(dimension_semantics=("arbitrary",)),
    )(q, k, v, do)


def _mem_gain_bwd(dm, mem, *, name):
    def body(dm_ref, m_ref, dg_ref):
        m_ = m_ref[...]
        dg_ref[...] = _rowsum8(dm_ref[...] * (m_ * _rstd(m_)))

    return pl.pallas_call(body, name=name, out_shape=jax.ShapeDtypeStruct((8, D), F32),
                          compiler_params=_params())(dm, mem)


FM, FN = 512, 1408


def _ffn_up(u, wgt, wut, t):
    tm = min(FM, t)

    def body(u_ref, wg_ref, wu_ref, g_ref, up_ref, a_ref):
        u_ = u_ref[...]
        g = _dot(u_, wg_ref[...], 1, 1)
        up = _dot(u_, wu_ref[...], 1, 1)
        g_ref[...] = g.astype(BF16)
        up_ref[...] = up.astype(BF16)
        a_ref[...] = (g * _sig(g) * up).astype(BF16)

    w = pl.BlockSpec((FN, D), lambda j, i: (j, 0))
    o = pl.BlockSpec((tm, FN), lambda j, i: (i, j))
    return pl.pallas_call(
        body, name="ffn_up", out_shape=(jax.ShapeDtypeStruct((t, D_FF), BF16),) * 3,
        grid=(D_FF // FN, t // tm), in_specs=[pl.BlockSpec((tm, D), lambda j, i: (i, 0)), w, w],
        out_specs=(o, o, o), compiler_params=_params(),
    )(u, wgt, wut)


def _ffn_down_bwd(dy, wd, gate, up, t, dep=None):
    tm = min(FM, t)
    deps = [] if dep is None else [dep]

    def body(dy_ref, w_ref, g_ref, up_ref, *rest):
        dg_ref, dup_ref = rest[len(deps):]
        da = _dot(dy_ref[...], w_ref[...], 1, 1)
        g = g_ref[...].astype(F32)
        sg = _sig(g)
        dup_ref[...] = (da * g * sg).astype(BF16)
        dg_ref[...] = (da * up_ref[...].astype(F32) * (sg * (1.0 + g * (1.0 - sg)))).astype(BF16)

    o = pl.BlockSpec((tm, FN), lambda j, i: (i, j))
    return pl.pallas_call(
        body, name="ffn_down_bwd", out_shape=(jax.ShapeDtypeStruct((t, D_FF), BF16),) * 2,
        grid=(D_FF // FN, t // tm),
        in_specs=[pl.BlockSpec((tm, D), lambda j, i: (i, 0)), pl.BlockSpec((FN, D), lambda j, i: (j, 0)), o, o]
        + [ANY_SPEC] * len(deps),
        out_specs=(o, o), compiler_params=_params(),
    )(dy, wd, gate, up, *deps)


def _local_step(x, mem, target, fetch, sm, emit=None):
    t = x.shape[0]
    w, gw = {}, {}

    def out(key, g):
        gw[key] = g
        return None if emit is None else emit(key, g)
    u1 = _prenorm(x, sm["g_mix_pre"], name="prenorm_mix")
    w["winT"] = fetch("winT", u1)
    z = _mm(u1, w["winT"], tb=True, out_dtype=F32, tm=1024, tn=1408, name="mm_z", n_outer=True)
    ymix, lse = _swa_fwd(z, sm["sinks"], t)
    ymix, o_h, sprev = _hgrn2_fwd(z, sm["hgrn_lb"], sm["hgrn_onorm"], ymix, t)
    w["wout"] = fetch("wout", ymix)
    y1 = _mm(ymix, w["wout"], out_dtype=F32, tm=1024, tn=1024, name="mm_y1")
    h1, u2 = _post_pre(x, y1, sm["g_mix_post"], sm["g_x_pre"], name="post_mix")
    mn = _prenorm(mem, sm["g_mem"], name="prenorm_mem")
    for key in ("wq", "wk", "wv"):
        w[key] = fetch(key, u2)
    qx = _mm(u2, w["wq"], out_dtype=BF16, tm=1024, tn=1024, name="mm_qx")
    kx = _mm(mn, w["wk"], out_dtype=BF16, tm=1024, tn=1024, name="mm_kx")
    vx = _mm(mn, w["wv"], out_dtype=BF16, tm=1024, tn=1024, name="mm_vx")
    ox = _xattn_fwd(qx, kx, vx, t)
    w["wo"] = fetch("wo", ox)
    y2 = _mm(ox, w["wo"], out_dtype=F32, tm=1024, tn=1024, name="mm_y2")
    h2, u3 = _post_pre(h1, y2, sm["g_x_post"], sm["g_ffn_pre"], name="post_x")
    w["wgT"], w["wuT"] = fetch("wgT", u3), fetch("wuT", u3)
    gate, up, act = _ffn_up(u3, w["wgT"], w["wuT"], t)
    w["wd"] = fetch("wd", act)
    y3 = _mm(act, w["wd"], out_dtype=F32, tm=1024, tn=1024, name="mm_y3")
    sq, dh3, dy3, dg_ffn_post = _final_loss(h2, y3, sm["g_ffn_post"], target, name="final_loss")
    dep = out("wd", _mm(act, dy3, ta=True, out_dtype=BF16, tm=1408, tn=1024, tk=512, name="mm_gwd"))
    dgate, dup = _ffn_down_bwd(dy3, w["wd"], gate, up, t, dep=dep)
    dep = out("wgT", _mm(dgate, u3, ta=True, out_dtype=BF16, tm=1408, tn=1024, tk=512, name="mm_gwg"))
    dep = out("wuT", _mm(dup, u3, ta=True, out_dtype=BF16, tm=1408, tn=1024, tk=512, name="mm_gwu", dep=dep))
    du3 = _mm2(dgate, w["wgT"], dup, w["wuT"], tm=512, tk=D_FF, name="mm_du3", dep=dep)
    dh2, dy2, dg_ffn_pre, dg_x_post = _post_pre_bwd(dh3, du3, h2, y2, sm["g_x_post"], sm["g_ffn_pre"], name="post_x_bwd")
    dep = out("wo", _mm(ox, dy2, ta=True, out_dtype=BF16, tm=1024, tn=1024, tk=512, name="mm_gwo"))
    dox = _mm(dy2, w["wo"], tb=True, out_dtype=BF16, tm=1024, tn=1024, name="mm_dox", dep=dep)
    dqx, dkx, dvx = _xattn_bwd(qx, kx, vx, dox, t)
    dep = out("wq", _mm(u2, dqx, ta=True, out_dtype=BF16, tm=1024, tn=1024, tk=512, name="mm_gwq"))
    dep = out("wk", _mm(mn, dkx, ta=True, out_dtype=BF16, tm=1024, tn=1024, name="mm_gwk", dep=dep))
    dep = out("wv", _mm(mn, dvx, ta=True, out_dtype=BF16, tm=1024, tn=1024, name="mm_gwv", dep=dep))
    du2 = _mm(dqx, w["wq"], tb=True, out_dtype=F32, tm=1024, tn=1024, name="mm_du2", dep=dep)
    dmn = _mm2(dkx, w["wk"], dvx, w["wv"], tb=True, tm=256, tk=1024, name="mm_dmn")
    dg_mem = _mem_gain_bwd(dmn, mem, name="mem_gain_bwd")
    dh1, dy1, dg_x_pre, dg_mix_post = _post_pre_bwd(dh2, du2, h1, y1, sm["g_mix_post"], sm["g_x_pre"], name="post_mix_bwd")
    dep = out("wout", _mm(ymix, dy1, ta=True, out_dtype=BF16, tm=1024, tn=1024, tk=512, name="mm_gwout"))
    dymix = _mm(dy1, w["wout"], tb=True, out_dtype=BF16, tm=1024, tn=1024, name="mm_dymix", dep=dep)
    dza, dsinks = _swa_bwd(z, sm["sinks"], ymix, lse, dymix, t)
    dz, dlb, donorm = _hgrn2_bwd(z, sm["hgrn_lb"], sm["hgrn_onorm"], o_h, sprev, dymix, dza, t)
    dep = out("winT", _mm(dz, u1, ta=True, out_dtype=BF16, tm=1408, tn=1024, tk=512, name="mm_gwin"))
    du1 = _mm(dz, w["winT"], out_dtype=F32, tm=512, tn=1024, name="mm_du1", dep=dep)
    grad_x, dg_mix_pre = _pre_bwd(dh1, du1, x, sm["g_mix_pre"], name="pre_mix_bwd")
    parts = dict(g_mix_pre=dg_mix_pre, g_mix_post=dg_mix_post, g_mem=dg_mem, g_x_pre=dg_x_pre,
                 g_x_post=dg_x_post, g_ffn_pre=dg_ffn_pre, g_ffn_post=dg_ffn_post,
                 hgrn_onorm=donorm, hgrn_lb=dlb, sinks=dsinks, sq=sq)
    return grad_x, gw, parts


def _position():
    return lax.axis_index("x"), lax.axis_index("y"), lax.axis_index("c")


def _peer(pos, k):
    x, y, c = pos
    return (1 - x if k & 4 else x, 1 - y if k & 2 else y, 1 - c if k & 1 else c)


def _linear(pos):
    x, y, c = pos
    return 4 * x + 2 * y + c


HBM_SPEC = pl.BlockSpec(memory_space=pltpu.HBM)
SEM_SPEC = pl.BlockSpec(memory_space=pltpu.SEMAPHORE)
DATAFLOW = pltpu.SideEffectType.DATAFLOW_SIDE_EFFECTING
SEND_ORDER = (1, 2, 4, 3, 5, 6, 7)


def _in_hbm(a):
    return pltpu.with_memory_space_constraint(a, pltpu.HBM)


def _prepare_weights(shards):
    n = len(shards)

    def body(*refs):
        ins, outs, lands, sem = refs[:n], refs[n:2 * n], refs[2 * n:3 * n], refs[3 * n]
        me_lin = _linear(_position())
        copies = []
        for a in range(n):
            r = ins[a].shape[0]
            outs[a][...] = ins[a][...].astype(BF16)
            copies.append(pltpu.make_async_copy(outs[a], lands[a].at[pl.ds(me_lin * r, r), :], sem.at[a]))
            copies[-1].start()
        for cp in copies:
            cp.wait()

    vmem = pl.BlockSpec(memory_space=pltpu.VMEM)
    res = pl.pallas_call(
        body, name="prepare_weights",
        out_shape=tuple(jax.ShapeDtypeStruct(s.shape, BF16) for s in shards)
        + tuple(jax.ShapeDtypeStruct((N_DEV * s.shape[0], s.shape[1]), BF16) for s in shards),
        in_specs=[vmem] * n, out_specs=tuple([vmem] * n + [ANY_SPEC] * n),
        scratch_shapes=[pltpu.SemaphoreType.DMA((n,))], compiler_params=_params(),
    )(*shards)
    return res[:n], res[n:]


def _gather_start(shards, lands):
    n = len(shards)
    rows = [s.shape[0] for s in shards]

    def body(*refs):
        srcs, land = refs[:n], refs[n:2 * n]
        send_sems, recv_sems = refs[2 * n:3 * n], refs[3 * n:4 * n]
        me = _position()
        for a in range(n):
            mine = land[a].at[pl.ds(_linear(me) * rows[a], rows[a]), :]
            for k in SEND_ORDER:
                pltpu.make_async_remote_copy(
                    src_ref=srcs[a], dst_ref=mine, send_sem=send_sems[a].at[k - 1], recv_sem=recv_sems[a].at[k - 1],
                    device_id=_peer(me, k), device_id_type=MESH).start()

    sems = tuple(pltpu.SemaphoreType.DMA((N_DEV - 1,)) for _ in range(2 * n))
    res = pl.pallas_call(
        body, name="weights_send",
        out_shape=sems + tuple(pltpu.HBM(s.shape, s.dtype) for s in shards)
        + tuple(pltpu.HBM(l.shape, l.dtype) for l in lands),
        in_specs=(HBM_SPEC,) * (2 * n), out_specs=(SEM_SPEC,) * (2 * n) + (HBM_SPEC,) * (2 * n),
        input_output_aliases={i: 2 * n + i for i in range(2 * n)},
        compiler_params=pltpu.CompilerParams(has_side_effects=DATAFLOW),
    )(*[_in_hbm(s) for s in shards], *[_in_hbm(l) for l in lands])
    return [(res[a], res[n + a], res[2 * n + a], res[3 * n + a]) for a in range(n)]


def _gather_wait(send_sems, recv_sems, shard_thru, land_thru, after, *, name):
    r = shard_thru.shape[0]

    def body(src_ref, land_ref, send_sems, recv_sems, after_ref, src_dead, got_ref):
        del after_ref, src_dead, got_ref
        me = _position()
        for k in SEND_ORDER:
            peer = _peer(me, k)
            copy = pltpu.make_async_remote_copy(
                src_ref=src_ref, dst_ref=land_ref.at[pl.ds(_linear(peer) * r, r), :],
                send_sem=send_sems.at[k - 1], recv_sem=recv_sems.at[k - 1],
                device_id=peer, device_id_type=MESH)
            copy.wait_send()
            copy.wait_recv()

    return pl.pallas_call(
        body, name=name,
        out_shape=(pltpu.HBM(shard_thru.shape, shard_thru.dtype), pltpu.HBM(land_thru.shape, land_thru.dtype)),
        in_specs=(HBM_SPEC, HBM_SPEC, SEM_SPEC, SEM_SPEC, ANY_SPEC),
        out_specs=(HBM_SPEC, HBM_SPEC), input_output_aliases={0: 0, 1: 1},
        compiler_params=pltpu.CompilerParams(has_side_effects=DATAFLOW),
    )(shard_thru, land_thru, send_sems, recv_sems, after)[1]


def _exchange_start(g, *, name):
    r = g.shape[0] // N_DEV
    land_shape = (N_DEV - 1, r, g.shape[1])

    def body(g_ref, land_ref, send_sems, recv_sems, g_thru, land_thru):
        del g_thru, land_thru
        me = _position()
        for k in SEND_ORDER:
            peer = _peer(me, k)
            pltpu.make_async_remote_copy(
                src_ref=g_ref.at[pl.ds(_linear(peer) * r, r), :], dst_ref=land_ref.at[k - 1],
                send_sem=send_sems.at[k - 1], recv_sem=recv_sems.at[k - 1],
                device_id=peer, device_id_type=MESH).start()

    return pl.pallas_call(
        body, name=name,
        out_shape=(pltpu.SemaphoreType.DMA((N_DEV - 1,)), pltpu.SemaphoreType.DMA((N_DEV - 1,)),
                   pltpu.HBM(g.shape, g.dtype), pltpu.HBM(land_shape, g.dtype)),
        in_specs=(HBM_SPEC, HBM_SPEC), out_specs=(SEM_SPEC, SEM_SPEC, HBM_SPEC, HBM_SPEC),
        input_output_aliases={0: 2, 1: 3},
        compiler_params=pltpu.CompilerParams(has_side_effects=DATAFLOW),
    )(_in_hbm(g), _in_hbm(lax.empty(land_shape, g.dtype)))


def _exchange_wait(send_sems, recv_sems, g_thru, land_thru, after, *, name):
    r = land_thru.shape[1]

    def body(g_ref, land_ref, send_sems, recv_sems, after_ref, g_dead, got_ref):
        del after_ref, g_dead, got_ref
        me = _position()
        for k in SEND_ORDER:
            peer = _peer(me, k)
            copy = pltpu.make_async_remote_copy(
                src_ref=g_ref.at[pl.ds(_linear(peer) * r, r), :], dst_ref=land_ref.at[k - 1],
                send_sem=send_sems.at[k - 1], recv_sem=recv_sems.at[k - 1],
                device_id=peer, device_id_type=MESH)
            copy.wait_send()
            copy.wait_recv()

    return pl.pallas_call(
        body, name=name,
        out_shape=(pltpu.HBM(g_thru.shape, g_thru.dtype), pltpu.HBM(land_thru.shape, land_thru.dtype)),
        in_specs=(HBM_SPEC, HBM_SPEC, SEM_SPEC, SEM_SPEC, pl.BlockSpec(memory_space=pl.ANY)),
        out_specs=(HBM_SPEC, HBM_SPEC), input_output_aliases={0: 0, 1: 1},
        compiler_params=pltpu.CompilerParams(has_side_effects=DATAFLOW),
    )(g_thru, land_thru, send_sems, recv_sems, after)


def _adamw_math(w, g, m, v):
    m = B1 * m + (1.0 - B1) * g
    v = B2 * v + (1.0 - B2) * (g * g)
    delta = -LR * ((m / C1) / (jnp.sqrt(v / C2) + AEPS) + WD * w)
    return delta, m, v


def _sum_adamw(own, land, w, m, v, *, name):
    def body(own_ref, land_ref, w_ref, m_ref, v_ref, g_ref, d_ref, nm_ref, nv_ref):
        g = own_ref[...].astype(F32)
        for s in range(N_DEV - 1):
            g = g + land_ref[s].astype(F32)
        g_ref[...] = g
        d_ref[...], nm_ref[...], nv_ref[...] = _adamw_math(w_ref[...], g, m_ref[...], v_ref[...])

    return pl.pallas_call(body, name=name, out_shape=(jax.ShapeDtypeStruct(w.shape, F32),) * 4,
                          compiler_params=_params())(own, land, w, m, v)


SMALL = ("g_mix_pre", "g_mix_post", "g_mem", "g_x_pre", "g_x_post", "g_ffn_pre", "g_ffn_post",
         "hgrn_onorm", "hgrn_lb", "sinks")
SMALL_W = dict(hgrn_onorm=HD, hgrn_lb=HG_W, sinks=8)
SQ_ROW = len(SMALL)
PACK_ROWS = 16


def _small_allreduce(parts):
    ns = len(SMALL)

    def body(*refs):
        part, tot_ref = refs[:ns + 1], refs[ns + 1]
        gath, send_sems, recv_sems = refs[ns + 2:]
        me = _position()
        mine = gath.at[_linear(me)]
        mine[...] = jnp.zeros((PACK_ROWS, D), F32)
        for r, name in enumerate(SMALL):
            wd = SMALL_W.get(name, D)
            mine[r:r + 1, 0:wd] = jnp.sum(part[r][...], axis=0, keepdims=True)[:, 0:wd]
        sq = jnp.sum(part[ns][...]) * (0.5 / D)
        mine[SQ_ROW:SQ_ROW + 1, :] = jnp.full((1, D), sq, F32)

        def copy(k):
            peer = _peer(me, k)
            return pltpu.make_async_remote_copy(
                src_ref=mine, dst_ref=mine, send_sem=send_sems.at[k - 1], recv_sem=recv_sems.at[k - 1],
                device_id=peer, device_id_type=MESH)

        def arrival(k):
            slot = gath.at[_linear(_peer(me, k))]
            return pltpu.make_async_remote_copy(
                src_ref=slot, dst_ref=slot, send_sem=send_sems.at[k - 1], recv_sem=recv_sems.at[k - 1],
                device_id=_peer(me, k), device_id_type=MESH)

        sent = [copy(k) for k in range(1, 8)]
        for cp in sent:
            cp.start()
        for k in range(1, 8):
            arrival(k).wait_recv()
        for cp in sent:
            cp.wait_send()
        tot = gath[0]
        for s in range(1, N_DEV):
            tot = tot + gath[s]
        tot_ref[...] = tot

    return pl.pallas_call(
        body, name="small_allreduce", out_shape=jax.ShapeDtypeStruct((PACK_ROWS, D), F32),
        scratch_shapes=[pltpu.VMEM((N_DEV, PACK_ROWS, D), F32), pltpu.SemaphoreType.DMA((7,)),
                        pltpu.SemaphoreType.DMA((7,))],
        compiler_params=_params(has_side_effects=True),
    )(*[parts[n] for n in SMALL], parts["sq"])


def _small_update(tot, sm, m_sm, v_sm):
    ns = len(SMALL)

    def body(*refs):
        tot = refs[0][...]
        w_refs, m_refs, v_refs = refs[1:ns + 1], refs[ns + 1:2 * ns + 1], refs[2 * ns + 1:3 * ns + 1]
        outs = refs[3 * ns + 1:]
        loss_ref = outs[0]
        g_out, d_out = outs[1:ns + 1], outs[ns + 1:2 * ns + 1]
        nm_out, nv_out = outs[2 * ns + 1:3 * ns + 1], outs[3 * ns + 1:4 * ns + 1]
        loss_ref[...] = tot[SQ_ROW:SQ_ROW + 1, 0:1]
        for r, name in enumerate(SMALL):
            wd = SMALL_W.get(name, D)
            g = tot[r:r + 1, 0:wd]
            w = w_refs[r][...]
            if name == "hgrn_lb":
                mx = jnp.maximum(w[0:1], w[1:2])
                e0, e1 = jnp.exp(w[0:1] - mx), jnp.exp(w[1:2] - mx)
                lb0 = e0 / (e0 + e1)
                g0 = g * lb0 * (1.0 - lb0)
                for i, gi in enumerate((g0, -g0)):
                    d, nm, nv = _adamw_math(w[i:i + 1], gi, m_refs[r][i:i + 1, :], v_refs[r][i:i + 1, :])
                    g_out[r][i:i + 1, :] = gi
                    d_out[r][i:i + 1, :], nm_out[r][i:i + 1, :], nv_out[r][i:i + 1, :] = d, nm, nv
            else:
                d, nm, nv = _adamw_math(w, g, m_refs[r][...], v_refs[r][...])
                g_out[r][...] = g
                d_out[r][...], nm_out[r][...], nv_out[r][...] = d, nm, nv

    shapes = [jax.ShapeDtypeStruct(sm[n].shape, F32) for n in SMALL]
    res = pl.pallas_call(
        body, name="small_update", out_shape=tuple([jax.ShapeDtypeStruct((1, 1), F32)] + shapes * 4),
        compiler_params=_params(),
    )(tot, *[sm[n] for n in SMALL], *[m_sm[n] for n in SMALL], *[v_sm[n] for n in SMALL])
    groups = [dict(zip(SMALL, res[1 + i * ns:1 + (i + 1) * ns])) for i in range(4)]
    return res[0], groups[0], groups[1], groups[2], groups[3]


BIG = ("w_in", "w_gate", "w_up", "w_down", "w_out", "wq_x", "wk_x", "wv_x", "wo_x")
BIG_KEY = dict(w_in="winT", w_gate="wgT", w_up="wuT", w_down="wd", w_out="wout", wq_x="wq", wk_x="wk",
               wv_x="wv", wo_x="wo")
TRANSPOSED = ("w_in", "w_gate", "w_up")
WEIGHTS = ("w_in", "sinks", "hgrn_lb", "hgrn_onorm", "w_out", "g_mix_pre", "g_mix_post", "g_mem", "g_x_pre",
           "g_x_post", "wq_x", "wk_x", "wv_x", "wo_x", "g_ffn_pre", "g_ffn_post", "w_gate", "w_up", "w_down")


def kernel(x, mem, w_in, sinks, hgrn_lb, hgrn_onorm, w_out, g_mix_pre, g_mix_post, g_mem, g_x_pre, g_x_post, wq_x, wk_x, wv_x, wo_x, g_ffn_pre, g_ffn_post, w_gate, w_up, w_down, loss_target, m_w_in, m_sinks, m_hgrn_lb, m_hgrn_onorm, m_w_out, m_g_mix_pre, m_g_mix_post, m_g_mem, m_g_x_pre, m_g_x_post, m_wq_x, m_wk_x, m_wv_x, m_wo_x, m_g_ffn_pre, m_g_ffn_post, m_w_gate, m_w_up, m_w_down, v_w_in, v_sinks, v_hgrn_lb, v_hgrn_onorm, v_w_out, v_g_mix_pre, v_g_mix_post, v_g_mem, v_g_x_pre, v_g_x_post, v_wq_x, v_wk_x, v_wv_x, v_wo_x, v_g_ffn_pre, v_g_ffn_post, v_w_gate, v_w_up, v_w_down):
    given = dict(locals())
    wts = {n: given[n] for n in WEIGHTS}
    ms = {n: given["m_" + n] for n in WEIGHTS}
    vs = {n: given["v_" + n] for n in WEIGHTS}

    def mat(a, name):
        a = a[0]
        return a.T if name in TRANSPOSED else a

    order = ("w_in", "w_out", "wq_x", "wk_x", "wv_x", "wo_x", "w_gate", "w_up", "w_down")
    flying = dict(zip(order, _gather_start(*_prepare_weights([mat(wts[n], n) for n in order]))))
    name_of = {k: n for n, k in BIG_KEY.items()}

    def fetch(key, after):
        return _gather_wait(*flying[name_of[key]], after, name="weights_recv_" + name_of[key])

    sm = {n: wts[n] for n in SMALL}
    started = {}

    def emit(key, g):
        started[name_of[key]] = _exchange_start(g, name="grad_send_" + name_of[key])
        return started[name_of[key]][2]

    grad_x, _, parts = _local_step(x[0], mem[0], loss_target[0], fetch, sm, emit)
    me_lin = _linear(_position())
    grads, deltas, new_m, new_v = {}, {}, {}, {}
    after = grad_x
    for n in ("w_down", "w_gate", "w_up", "wo_x", "wq_x", "wk_x", "wv_x", "w_out", "w_in"):
        g_all, land = _exchange_wait(*started[n], after, name="grad_recv_" + n)
        r = land.shape[1]
        own = lax.dynamic_slice_in_dim(g_all, me_lin * r, r, 0)
        res = _sum_adamw(own, land, mat(wts[n], n), mat(ms[n], n), mat(vs[n], n), name="adamw_" + n)
        after = res[1]
        if n in TRANSPOSED:
            res = [a.T for a in res]
        grads[n], deltas[n], new_m[n], new_v[n] = [a[None] for a in res]
    loss, g_s, d_s, m_s, v_s = _small_update(_small_allreduce(parts), sm, {n: ms[n] for n in SMALL},
                                             {n: vs[n] for n in SMALL})
    grads.update(g_s), deltas.update(d_s), new_m.update(m_s), new_v.update(v_s)
    return (loss[0, 0], grad_x[None], *[grads[n] for n in WEIGHTS], *[deltas[n] for n in WEIGHTS],
            *[new_m[n] for n in WEIGHTS], *[new_v[n] for n in WEIGHTS])
```

```python
import functools

import jax
import jax.numpy as jnp
from jax import lax
from jax.experimental import pallas as pl
from jax.experimental.pallas import tpu as pltpu

F32 = jnp.float32
BF16 = jnp.bfloat16

D = 1024
D_IN = 2816
D_FF = 2816
CHUNK = 64
SWA_W = 512
KV_W = 128
HG_W = 512
HD = 128
ZQH, ZFH, ZIH, ZGH = 768, 1280, 1792, 2304
XH, XD = 4, 256
EPS = 1e-6
NEG = -1e30
N_DEV = 8
MESH = pl.DeviceIdType.MESH

LR, B1, B2, AEPS, WD, STEP = 0.001, 0.9, 0.999, 1e-08, 0.01, 10
C1 = 1.0 - B1 ** STEP
C2 = 1.0 - B2 ** STEP

VMEM_LIMIT = 56 * 1024 * 1024


def _params(**kw):
    return pltpu.CompilerParams(vmem_limit_bytes=VMEM_LIMIT, **kw)


def _sig(x):
    return 1.0 / (1.0 + jnp.exp(-x))


def _rowsum8(x):
    r, w = x.shape
    return jnp.sum(x.reshape(r // 8, 8, w), axis=0)


def _dot(a, b, ca, cb, precision=None):
    return lax.dot_general(a, b, (((ca,), (cb,)), ((), ())), preferred_element_type=F32,
                           precision=precision)


ANY_SPEC = pl.BlockSpec(memory_space=pl.ANY)


def _mm(a, b, *, ta=False, tb=False, out_dtype, tm, tn, tk=None, name, dep=None, n_outer=False):
    m = a.shape[1] if ta else a.shape[0]
    k = a.shape[0] if ta else a.shape[1]
    n = b.shape[0] if tb else b.shape[1]
    tm, tn = min(tm, m), min(tn, n)
    tk = k if tk is None else min(tk, k)
    nk = k // tk
    assert m % tm == 0 and n % tn == 0 and k % tk == 0, (name, m, n, k, tm, tn, tk)
    ij = (lambda g0, g1: (g1, g0)) if n_outer else (lambda g0, g1: (g0, g1))
    a_spec = (pl.BlockSpec((tk, tm), lambda g0, g1, kk: (kk, ij(g0, g1)[0])) if ta
              else pl.BlockSpec((tm, tk), lambda g0, g1, kk: (ij(g0, g1)[0], kk)))
    b_spec = (pl.BlockSpec((tn, tk), lambda g0, g1, kk: (ij(g0, g1)[1], kk)) if tb
              else pl.BlockSpec((tk, tn), lambda g0, g1, kk: (kk, ij(g0, g1)[1])))
    ca, cb = (0 if ta else 1), (1 if tb else 0)

    deps = [] if dep is None else [dep]

    def body(a_ref, b_ref, *rest):
        o_ref, acc = rest[len(deps)], rest[len(deps) + 1:]
        p = _dot(a_ref[...].astype(BF16), b_ref[...].astype(BF16), ca, cb)
        if nk == 1:
            o_ref[...] = p.astype(out_dtype)
        else:
            acc_ref, = acc
            kk = pl.program_id(2)

            @pl.when(kk == 0)
            def _():
                acc_ref[...] = p

            @pl.when(kk > 0)
            def _():
                acc_ref[...] += p

            @pl.when(kk == nk - 1)
            def _():
                o_ref[...] = acc_ref[...].astype(out_dtype)

    return pl.pallas_call(
        body, name=name, out_shape=jax.ShapeDtypeStruct((m, n), out_dtype),
        grid=(n // tn, m // tm, nk) if n_outer else (m // tm, n // tn, nk),
        in_specs=[a_spec, b_spec] + [ANY_SPEC] * len(deps),
        out_specs=pl.BlockSpec((tm, tn), lambda g0, g1, kk: ij(g0, g1)),
        scratch_shapes=[pltpu.VMEM((tm, tn), F32)] if nk > 1 else [],
        compiler_params=_params(dimension_semantics=("parallel", "parallel", "arbitrary")),
    )(a, b, *deps)


def _mm2(a1, b1, a2, b2, *, tb=False, tm, tk, name, dep=None):
    m, k = a1.shape
    n = b1.shape[0] if tb else b1.shape[1]
    tm, tk = min(tm, m), min(tk, k)
    nk = k // tk
    assert m % tm == 0 and k % tk == 0
    cb = 1 if tb else 0
    deps = [] if dep is None else [dep]

    def body(a1_ref, b1_ref, a2_ref, b2_ref, *rest):
        o_ref = rest[len(deps)]
        p = (_dot(a1_ref[...].astype(BF16), b1_ref[...], 1, cb)
             + _dot(a2_ref[...].astype(BF16), b2_ref[...], 1, cb))
        kk = pl.program_id(1)

        @pl.when(kk == 0)
        def _():
            o_ref[...] = p

        @pl.when(kk > 0)
        def _():
            o_ref[...] += p

    a_spec = pl.BlockSpec((tm, tk), lambda i, kk: (i, kk))
    b_spec = pl.BlockSpec((n, tk), lambda i, kk: (0, kk)) if tb else pl.BlockSpec((tk, n), lambda i, kk: (kk, 0))
    return pl.pallas_call(
        body, name=name, out_shape=jax.ShapeDtypeStruct((m, n), F32),
        grid=(m // tm, nk), in_specs=[a_spec, b_spec, a_spec, b_spec] + [ANY_SPEC] * len(deps),
        out_specs=pl.BlockSpec((tm, n), lambda i, kk: (i, 0)),
        compiler_params=_params(dimension_semantics=("parallel", "arbitrary")),
    )(a1, b1, a2, b2, *deps)


def _rstd(x):
    return lax.rsqrt(jnp.mean(x * x, axis=-1, keepdims=True) + EPS)


def _norm_bwd(xh, r, t):
    return r * (t - xh * jnp.mean(xh * t, axis=-1, keepdims=True))


def _prenorm(x, g, *, name):
    t, d = x.shape
    tb = min(512, t)

    def body(x_ref, g_ref, o_ref):
        xf = x_ref[...]
        o_ref[...] = (xf * _rstd(xf) * g_ref[...]).astype(BF16)

    return pl.pallas_call(
        body, name=name, out_shape=jax.ShapeDtypeStruct((t, d), BF16), grid=(t // tb,),
        in_specs=[pl.BlockSpec((tb, d), lambda i: (i, 0)), pl.BlockSpec((1, d), lambda i: (0, 0))],
        out_specs=pl.BlockSpec((tb, d), lambda i: (i, 0)), compiler_params=_params(),
    )(x, g)


def _post_pre(h, y, g_post, g_pre, *, name):
    t, d = h.shape
    tb = min(512, t)

    def body(h_ref, y_ref, gp_ref, gn_ref, hn_ref, u_ref):
        y_ = y_ref[...]
        hn = h_ref[...] + y_ * _rstd(y_) * gp_ref[...]
        hn_ref[...] = hn
        u_ref[...] = (hn * _rstd(hn) * gn_ref[...]).astype(BF16)

    row = pl.BlockSpec((tb, d), lambda i: (i, 0))
    vec = pl.BlockSpec((1, d), lambda i: (0, 0))
    return pl.pallas_call(
        body, name=name, out_shape=(jax.ShapeDtypeStruct((t, d), F32), jax.ShapeDtypeStruct((t, d), BF16)),
        grid=(t // tb,), in_specs=[row, row, vec, vec], out_specs=(row, row), compiler_params=_params(),
    )(h, y, g_post, g_pre)


def _final_loss(h, y, g_post, target, *, name):
    t, d = h.shape
    tb = min(512, t)

    def body(h_ref, y_ref, g_ref, t_ref, sq_ref, dh_ref, dy_ref, dg_ref):
        @pl.when(pl.program_id(0) == 0)
        def _():
            sq_ref[...] = jnp.zeros_like(sq_ref)
            dg_ref[...] = jnp.zeros_like(dg_ref)

        y_ = y_ref[...]
        r = _rstd(y_)
        yh = y_ * r
        g = g_ref[...]
        err = h_ref[...] + yh * g - t_ref[...]
        sq_ref[...] += _rowsum8(err * err)
        dh = err * (1.0 / d)
        dh_ref[...] = dh
        dg_ref[...] += _rowsum8(dh * yh)
        dy_ref[...] = _norm_bwd(yh, r, dh * g).astype(BF16)

    row = pl.BlockSpec((tb, d), lambda i: (i, 0))
    vec = pl.BlockSpec((1, d), lambda i: (0, 0))
    acc = pl.BlockSpec((8, d), lambda i: (0, 0))
    return pl.pallas_call(
        body, name=name,
        out_shape=(jax.ShapeDtypeStruct((8, d), F32), jax.ShapeDtypeStruct((t, d), F32),
                   jax.ShapeDtypeStruct((t, d), BF16), jax.ShapeDtypeStruct((8, d), F32)),
        grid=(t // tb,), in_specs=[row, row, vec, row], out_specs=(acc, row, row, acc),
        compiler_params=_params(dimension_semantics=("arbitrary",)),
    )(h, y, g_post, target)


def _post_pre_bwd(dh_out, du, hn, y, g_post, g_pre, *, name):
    t, d = hn.shape
    tb = min(512, t)

    def body(dho_ref, du_ref, hn_ref, y_ref, gp_ref, gn_ref, dh_ref, dy_ref, dgn_ref, dgp_ref):
        @pl.when(pl.program_id(0) == 0)
        def _():
            dgn_ref[...] = jnp.zeros_like(dgn_ref)
            dgp_ref[...] = jnp.zeros_like(dgp_ref)

        hn_ = hn_ref[...]
        r2 = _rstd(hn_)
        xh = hn_ * r2
        du_ = du_ref[...]
        dgn_ref[...] += _rowsum8(du_ * xh)
        dh = dho_ref[...] + _norm_bwd(xh, r2, du_ * gn_ref[...])
        dh_ref[...] = dh
        y_ = y_ref[...]
        r1 = _rstd(y_)
        yh = y_ * r1
        dgp_ref[...] += _rowsum8(dh * yh)
        dy_ref[...] = _norm_bwd(yh, r1, dh * gp_ref[...]).astype(BF16)

    row = pl.BlockSpec((tb, d), lambda i: (i, 0))
    vec = pl.BlockSpec((1, d), lambda i: (0, 0))
    acc = pl.BlockSpec((8, d), lambda i: (0, 0))
    return pl.pallas_call(
        body, name=name,
        out_shape=(jax.ShapeDtypeStruct((t, d), F32), jax.ShapeDtypeStruct((t, d), BF16),
                   jax.ShapeDtypeStruct((8, d), F32), jax.ShapeDtypeStruct((8, d), F32)),
        grid=(t // tb,), in_specs=[row, row, row, row, vec, vec], out_specs=(row, row, acc, acc),
        compiler_params=_params(dimension_semantics=("arbitrary",)),
    )(dh_out, du, hn, y, g_post, g_pre)


def _pre_bwd(dh_out, du, x, g, *, name):
    t, d = x.shape
    tb = min(512, t)
    has_res = dh_out is not None

    def body(*refs):
        if has_res:
            dho_ref, du_ref, x_ref, g_ref, dx_ref, dg_ref = refs
        else:
            du_ref, x_ref, g_ref, dx_ref, dg_ref = refs

        @pl.when(pl.program_id(0) == 0)
        def _():
            dg_ref[...] = jnp.zeros_like(dg_ref)

        x_ = x_ref[...]
        r = _rstd(x_)
        xh = x_ * r
        du_ = du_ref[...]
        dg_ref[...] += _rowsum8(du_ * xh)
        dx = _norm_bwd(xh, r, du_ * g_ref[...])
        if has_res:
            dx = dx + dho_ref[...]
        dx_ref[...] = dx

    row = pl.BlockSpec((tb, d), lambda i: (i, 0))
    vec = pl.BlockSpec((1, d), lambda i: (0, 0))
    acc = pl.BlockSpec((8, d), lambda i: (0, 0))
    ins = ([dh_out] if has_res else []) + [du, x, g]
    return pl.pallas_call(
        body, name=name,
        out_shape=(jax.ShapeDtypeStruct((t, d), F32), jax.ShapeDtypeStruct((8, d), F32)),
        grid=(t // tb,), in_specs=[row] * (len(ins) - 1) + [vec], out_specs=(row, acc),
        compiler_params=_params(dimension_semantics=("arbitrary",)),
    )(*ins)


QB = 256


def _half_mask(shape, e):
    lane = lax.broadcasted_iota(jnp.int32, shape, len(shape) - 1)
    return (lane // 64) == e


def _place(kv):
    sw = pltpu.roll(kv, 64, 1)
    m0 = _half_mask(kv.shape, 0)
    return [[jnp.where(m0, kv, 0.0).astype(BF16), jnp.where(m0, 0.0, sw).astype(BF16)],
            [jnp.where(m0, sw, 0.0).astype(BF16), jnp.where(m0, 0.0, kv).astype(BF16)]]


def _swa_valid_q(i, nq, nk):
    qc = lax.broadcasted_iota(jnp.int32, (nq, nk), 0) // CHUNK
    kc = lax.broadcasted_iota(jnp.int32, (nq, nk), 1) // CHUNK - 2
    return (kc <= qc) & (qc <= kc + 2) & (4 * i + kc >= 0)


def _swa_fwd(z, sinks, t):
    nb = t // QB

    def body(s_ref, q_ref, kp_ref, kc_ref, vp_ref, vc_ref, o_ref, lse_ref):
        i = pl.program_id(0)
        kpl = _place(jnp.concatenate([kp_ref[...], kc_ref[...]], axis=0))
        vpl = _place(jnp.concatenate([vp_ref[...], vc_ref[...]], axis=0))
        valid = _swa_valid_q(i, QB, QB + 128)
        lane = lax.broadcasted_iota(jnp.int32, (QB, 128), 1)
        lse_out = jnp.zeros((QB, 128), F32)
        for j in range(4):
            qp = q_ref[:, 128 * j:128 * (j + 1)].astype(BF16)
            acc = jnp.zeros((QB, 128), F32)
            for e in range(2):
                h = 2 * j + e
                kvh = h // 4
                qm = jnp.where(_half_mask(qp.shape, e), qp, jnp.zeros_like(qp))
                s = _dot(qm, kpl[kvh][e], 1, 1) * 0.125
                s = jnp.where(valid, s, NEG)
                sink = s_ref[0, h]
                m = jnp.maximum(jnp.max(s, axis=-1, keepdims=True), sink)
                p = jnp.exp(s - m)
                l = jnp.sum(p, axis=-1, keepdims=True) + jnp.exp(sink - m)
                acc = acc + _dot(p.astype(BF16), vpl[kvh][e], 1, 0) * (1.0 / l)
                lse_out = jnp.where(lane == h, m + jnp.log(l), lse_out)
            o_ref[:, 128 * j:128 * (j + 1)] = acc.astype(BF16)
        lse_ref[...] = lse_out

    prev = lambda c: pl.BlockSpec((128, 128), lambda i: (jnp.maximum(2 * i - 1, 0), c))
    cur = lambda c: pl.BlockSpec((QB, 128), lambda i: (i, c))
    return pl.pallas_call(
        body, name="swa_fwd",
        out_shape=(jax.ShapeDtypeStruct((t, D), BF16), jax.ShapeDtypeStruct((t, 128), F32)),
        grid=(nb,),
        in_specs=[pl.BlockSpec(memory_space=pltpu.SMEM),
                  pl.BlockSpec((QB, SWA_W), lambda i: (i, 0)), prev(4), cur(4), prev(5), cur(5)],
        out_specs=(pl.BlockSpec((QB, SWA_W), lambda i: (i, 0)), pl.BlockSpec((QB, 128), lambda i: (i, 0))),
        compiler_params=_params(),
    )(sinks, z, z, z, z, z)


def _swa_bwd(z, sinks, ymix, lse, dymix, t):
    nb = t // QB
    nk = QB + 128

    def body(s_ref, q_ref, kp_ref, kc_ref, vp_ref, vc_ref, o_ref, do_ref, l_ref,
             dq_ref, first_ref, second_ref, ds_ref, carry_ref):
        i = pl.program_id(0)
        live = i < nb

        @pl.when(i == 0)
        def _():
            ds_ref[...] = jnp.zeros_like(ds_ref)
            carry_ref[...] = jnp.zeros_like(carry_ref)

        lane = lax.broadcasted_iota(jnp.int32, (8, 128), 1)
        kpl = _place(jnp.concatenate([kp_ref[...], kc_ref[...]], axis=0))
        vpl = _place(jnp.concatenate([vp_ref[...], vc_ref[...]], axis=0))
        valid = _swa_valid_q(i, QB, nk) & live
        lse_c = l_ref[...]
        dsink = jnp.zeros((8, 128), F32)
        dk_acc = [[jnp.zeros((nk, 128), F32) for _ in range(2)] for _ in range(2)]
        dv_acc = [[jnp.zeros((nk, 128), F32) for _ in range(2)] for _ in range(2)]
        dq = []
        for j in range(4):
            cols = slice(128 * j, 128 * (j + 1))
            qp = q_ref[:, cols].astype(BF16)
            dop = do_ref[:, cols]
            prod = dop.astype(F32) * o_ref[:, cols].astype(F32)
            acc = jnp.zeros((QB, 128), F32)
            for e in range(2):
                h = 2 * j + e
                kvh = h // 4
                hm = _half_mask(qp.shape, e)
                qm = jnp.where(hm, qp, jnp.zeros_like(qp))
                dom = jnp.where(hm, dop, jnp.zeros_like(dop))
                dd = jnp.sum(jnp.where(hm, prod, 0.0), axis=-1, keepdims=True)
                lse_h = lse_c[:, h:h + 1]
                s = _dot(qm, kpl[kvh][e], 1, 1) * 0.125
                p = jnp.where(valid, jnp.exp(s - lse_h), 0.0)
                dp = _dot(dom, vpl[kvh][e], 1, 1)
                ds = (p * (dp - dd) * 0.125).astype(BF16)
                acc = acc + _dot(ds, kpl[kvh][e], 1, 0)
                dk_acc[kvh][e] = dk_acc[kvh][e] + _dot(ds, qm, 0, 0)
                dv_acc[kvh][e] = dv_acc[kvh][e] + _dot(p.astype(BF16), dom, 0, 0)
                ps = jnp.where(live, jnp.exp(s_ref[0, h] - lse_h) * dd, 0.0)
                dsink = dsink - jnp.where(lane == h, _rowsum8(jnp.broadcast_to(ps, (QB, 128))), 0.0)
            dq.append(acc.astype(BF16))
        ds_ref[...] += dsink
        dk = dk_acc[0][0] + dk_acc[1][1] + pltpu.roll(dk_acc[0][1] + dk_acc[1][0], 64, 1)
        dv = dv_acc[0][0] + dv_acc[1][1] + pltpu.roll(dv_acc[0][1] + dv_acc[1][0], 64, 1)
        dkv = jnp.concatenate([dk, dv], axis=1)
        second_ref[...] = (carry_ref[...] + dkv[0:128]).astype(BF16)
        carry_ref[...] = dkv[256:384]

        @pl.when(live)
        def _():
            for j in range(4):
                dq_ref[:, 128 * j:128 * (j + 1)] = dq[j]
            first_ref[...] = dkv[128:256].astype(BF16)

    blk = lambda i: jnp.minimum(i, nb - 1)
    prev = lambda c: pl.BlockSpec((128, 128), lambda i: (jnp.maximum(2 * blk(i) - 1, 0), c))
    cur = lambda w, c: pl.BlockSpec((QB, w), lambda i: (blk(i), c))
    half = lambda index: pl.BlockSpec((128, 256), lambda i: (index(i), 0))
    return pl.pallas_call(
        body, name="swa_bwd",
        out_shape=(jax.ShapeDtypeStruct((t, SWA_W), BF16), jax.ShapeDtypeStruct((t // 2, 256), BF16),
                   jax.ShapeDtypeStruct((t // 2, 256), BF16), jax.ShapeDtypeStruct((8, 128), F32)),
        grid=(nb + 1,),
        in_specs=[pl.BlockSpec(memory_space=pltpu.SMEM),
                  cur(SWA_W, 0), prev(4), cur(128, 4), prev(5), cur(128, 5),
                  cur(SWA_W, 0), cur(SWA_W, 0), cur(128, 0)],
        out_specs=(cur(SWA_W, 0), half(blk), half(lambda i: jnp.maximum(i - 1, 0)),
                   pl.BlockSpec((8, 128), lambda i: (0, 0))),
        scratch_shapes=[pltpu.VMEM((128, 256), F32)],
        compiler_params=_params(dimension_semantics=("arbitrary",)),
    )(sinks, z, z, z, z, z, ymix, dymix, lse)


HB = 256


def _lower_bound(lb_ref):
    a = lb_ref[...]
    a0, a1 = a[0:1], a[1:2]
    mx = jnp.maximum(a0, a1)
    e0, e1 = jnp.exp(a0 - mx), jnp.exp(a1 - mx)
    return e0 / (e0 + e1)


def _hgrn_cols(row_block):
    return [pl.BlockSpec((HB, 2 * HD), lambda j, c=base // (2 * HD) + p: (row_block(j), c))
            for base in (ZQH, ZFH, ZIH, ZGH) for p in range(2)]


NCH = HB // CHUNK


def _split3(x):
    hi = x.astype(BF16)
    r1 = x - hi.astype(F32)
    mid = r1.astype(BF16)
    return hi, mid, (r1 - mid.astype(F32)).astype(BF16)


def _blockdiag(lower):
    r = lax.broadcasted_iota(jnp.int32, (HB, HB), 0)
    c = lax.broadcasted_iota(jnp.int32, (HB, HB), 1)
    return (r // CHUNK == c // CHUNK) & ((c <= r) if lower else (c >= r))


def _chunk_sums(mask_bf16, x):
    return sum(_dot(mask_bf16, part, 1, 0) for part in _split3(x))


def _per_chunk_rows(x, row):
    w = x.shape[1]
    picked = x.reshape(NCH, CHUNK, w)[:, row:row + 1, :]
    return jnp.broadcast_to(picked, (NCH, CHUNK, w)).reshape(HB, w)


def _chunk_stack(x, chunk_of_row):
    return jnp.concatenate([jnp.where(chunk_of_row == c, x, jnp.zeros_like(x)) for c in range(NCH)], axis=1)


def _chunk_pick(x, chunk_of_row):
    w = x.shape[1] // NCH
    out = jnp.zeros((HB, w), x.dtype)
    for c in range(NCH):
        out = jnp.where(chunk_of_row == c, x[:, c * w:(c + 1) * w], out)
    return out


def _hgrn_local(q, f, kf, b):
    sq = _sig(q)
    qf = q * sq * (HD ** -0.5)
    b_mid = _per_chunk_rows(b, CHUNK // 2 - 1)
    b_last = _per_chunk_rows(b, CHUNK - 1)
    qm = qf * jnp.exp(b - b_mid)
    km = kf * jnp.exp(b_mid - b)
    kl = kf * jnp.exp(b_last - b)
    qb = qf * jnp.exp(b)
    return dict(sq=sq, b_mid=b_mid, b_last=b_last, qm=qm, km=km, kl=kl, qb=qb)


def _hgrn2_fwd(z, hgrn_lb, onorm, ymix, t):
    nb = t // HB

    def body(*refs):
        zq, zf, zi, zg = refs[0:2], refs[2:4], refs[4:6], refs[6:8]
        lb_ref, on_ref, _, y_ref, o_ref, sp_ref, st_ref = refs[8:]

        @pl.when(pl.program_id(0) == 0)
        def _():
            st_ref[...] = jnp.zeros_like(st_ref)

        lb_all = _lower_bound(lb_ref)
        gn = on_ref[...]
        low = _blockdiag(True)
        low_b = low.astype(BF16)
        chunk_of_row = lax.broadcasted_iota(jnp.int32, (HB, HD), 0) // CHUNK
        for p in range(2):
            lbp = lb_all[:, 2 * HD * p:2 * HD * (p + 1)]
            fp = lbp + (1.0 - lbp) * _sig(zf[p][...])
            bp = _chunk_sums(low_b, jnp.log(fp))
            for e in range(2):
                h, ls = 2 * p + e, slice(e * HD, (e + 1) * HD)
                f = fp[:, ls]
                w = _hgrn_local(zq[p][:, ls], f, 1.0 - f, bp[:, ls])
                iv = zi[p][:, ls].astype(BF16)
                a = jnp.where(low, _dot(w["qm"].astype(BF16), w["km"].astype(BF16), 1, 1), 0.0)
                o = _dot(a.astype(BF16), iv, 1, 0)
                u = _dot(iv, _chunk_stack(w["kl"].astype(BF16), chunk_of_row), 0, 0)
                decay = jnp.exp(w["b_last"])
                st = st_ref[h]
                states = []
                for c in range(NCH):
                    sp_ref[h, c] = st
                    states.append(st.astype(BF16))
                    st = st * decay[c * CHUNK:c * CHUNK + 1] + u[:, c * HD:(c + 1) * HD]
                st_ref[h] = st
                inter = _dot(w["qb"].astype(BF16), jnp.concatenate(states, axis=0), 1, 1)
                o = o + _chunk_pick(inter, chunk_of_row)
                hs = slice(h * HD, (h + 1) * HD)
                o_ref[:, hs] = o
                gg = zg[p][:, ls]
                y_ref[:, hs] = (o * _rstd(o) * gn * (gg * _sig(gg))).astype(BF16)

    return pl.pallas_call(
        body, name="hgrn_fwd",
        out_shape=(jax.ShapeDtypeStruct((t, D), BF16), jax.ShapeDtypeStruct((t, HG_W), F32),
                   jax.ShapeDtypeStruct((4, t // CHUNK, HD, HD), F32)),
        grid=(nb,),
        in_specs=_hgrn_cols(lambda j: j) + [pl.BlockSpec((2, HG_W), lambda j: (0, 0)),
                                            pl.BlockSpec((1, HD), lambda j: (0, 0)), ANY_SPEC],
        out_specs=(pl.BlockSpec((HB, HG_W), lambda j: (j, 1)),
                   pl.BlockSpec((HB, HG_W), lambda j: (j, 0)),
                   pl.BlockSpec((4, NCH, HD, HD), lambda j: (0, j, 0, 0))),
        scratch_shapes=[pltpu.VMEM((4, HD, HD), F32)],
        input_output_aliases={10: 0},
        compiler_params=_params(dimension_semantics=("arbitrary",)),
    )(*[z] * 8, hgrn_lb, onorm, ymix)


def _hgrn2_bwd(z, hgrn_lb, onorm, o_save, sprev, dymix, dza, t):
    nb = t // HB

    def body(*refs):
        zq, zf, zi, zg = refs[0:2], refs[2:4], refs[4:6], refs[6:8]
        (lb_ref, on_ref, o_ref, sp_ref, dy_ref, dqa_ref, first_ref, second_ref,
         dz_ref, dlb_ref, don_ref, dst_ref) = refs[8:]

        @pl.when(pl.program_id(0) == 0)
        def _():
            dst_ref[...] = jnp.zeros_like(dst_ref)
            dlb_ref[...] = jnp.zeros_like(dlb_ref)
            don_ref[...] = jnp.zeros_like(don_ref)

        dz_ref[:, 0:SWA_W] = dqa_ref[...]
        dz_ref[0:HB // 2, SWA_W:ZQH] = first_ref[...]
        dz_ref[HB // 2:HB, SWA_W:ZQH] = second_ref[...]
        lb_all = _lower_bound(lb_ref)
        gn = on_ref[...]
        low, upp = _blockdiag(True), _blockdiag(False)
        upp_b = upp.astype(BF16)
        low_b = low.astype(BF16)
        row = lax.broadcasted_iota(jnp.int32, (HB, HD), 0)
        chunk_of_row = row // CHUNK
        in_chunk = row % CHUNK
        for p in range(2):
            lbp = lb_all[:, 2 * HD * p:2 * HD * (p + 1)]
            sgp = _sig(zf[p][...])
            fp = lbp + (1.0 - lbp) * sgp
            bp = _chunk_sums(low_b, jnp.log(fp))
            db_pair, dkf_pair = [], []
            for e in range(2):
                h, ls, hs = 2 * p + e, slice(e * HD, (e + 1) * HD), slice((2 * p + e) * HD, (2 * p + e + 1) * HD)
                f = fp[:, ls]
                q = zq[p][:, ls]
                w = _hgrn_local(q, f, 1.0 - f, bp[:, ls])
                iv = zi[p][:, ls].astype(BF16)
                gg = zg[p][:, ls]
                o = o_ref[:, hs]
                dout = dy_ref[:, hs].astype(F32)
                sgg = _sig(gg)
                r = _rstd(o)
                oh = o * r
                dyn = dout * (gg * sgg)
                dz_ref[:, ZGH + h * HD:ZGH + (h + 1) * HD] = (
                    dout * oh * gn * (sgg * (1.0 + gg * (1.0 - sgg)))).astype(BF16)
                don_ref[...] += _rowsum8(dyn * oh)
                do = _norm_bwd(oh, r, dyn * gn).astype(BF16)
                qm, km, kl, qb = (w[n].astype(BF16) for n in ("qm", "km", "kl", "qb"))
                decay = jnp.exp(w["b_last"])
                grads_in = _dot(do, _chunk_stack(qb, chunk_of_row), 0, 0)
                dst = dst_ref[h]
                dstn, dd_rows = [None] * NCH, [None] * NCH
                for c in reversed(range(NCH)):
                    dstn[c] = dst.astype(BF16)
                    dd_rows[c] = jnp.sum(dst * sp_ref[h, c], axis=0, keepdims=True)
                    dst = dst * decay[c * CHUNK:c * CHUNK + 1] + grads_in[:, c * HD:(c + 1) * HD]
                dst_ref[h] = dst
                states = jnp.concatenate([sp_ref[h, c].astype(BF16) for c in range(NCH)], axis=0)
                dstn_all = jnp.concatenate(dstn, axis=0)
                dqb = _dot(_chunk_stack(do, chunk_of_row), states, 1, 0)
                at = jnp.where(upp, _dot(km, qm, 1, 1), 0.0)
                di = _dot(at.astype(BF16), do, 1, 0) + _chunk_pick(_dot(kl, dstn_all, 1, 1), chunk_of_row)
                dz_ref[:, ZIH + h * HD:ZIH + (h + 1) * HD] = di.astype(BF16)
                dkl = _dot(_chunk_stack(iv, chunk_of_row), dstn_all, 1, 0)
                da = jnp.where(low, _dot(do, iv, 1, 1), 0.0).astype(BF16)
                dat = jnp.where(upp, _dot(iv, do, 1, 1), 0.0).astype(BF16)
                dqm = _dot(da, km, 1, 0)
                dkm = _dot(dat, qm, 1, 0)
                b = bp[:, ls]
                e1, e2 = jnp.exp(b - w["b_mid"]), jnp.exp(w["b_mid"] - b)
                e3, e4 = jnp.exp(w["b_last"] - b), jnp.exp(b)
                dqf = dqm * e1 + dqb * e4
                dkf_pair.append(dkm * e2 + dkl * e3)
                t_qm, t_km, t_kl = dqm * w["qm"], dkm * w["km"], dkl * w["kl"]
                db = t_qm - t_km - t_kl + dqb * w["qb"]
                db_mid = jnp.sum((t_km - t_qm).reshape(NCH, CHUNK, HD), axis=1, keepdims=True)
                db_last = jnp.sum(t_kl.reshape(NCH, CHUNK, HD), axis=1, keepdims=True)
                db_last = db_last + jnp.stack(dd_rows, axis=0) * jnp.exp(
                    bp[:, ls].reshape(NCH, CHUNK, HD)[:, CHUNK - 1:CHUNK, :])
                spread = lambda v: jnp.broadcast_to(v, (NCH, CHUNK, HD)).reshape(HB, HD)
                db = (db + jnp.where(in_chunk == CHUNK // 2 - 1, spread(db_mid), 0.0)
                      + jnp.where(in_chunk == CHUNK - 1, spread(db_last), 0.0))
                db_pair.append(db)
                sq = w["sq"]
                dz_ref[:, ZQH + h * HD:ZQH + (h + 1) * HD] = (
                    dqf * (HD ** -0.5) * (sq * (1.0 + q * (1.0 - sq)))).astype(BF16)
            dlogf = _chunk_sums(upp_b, jnp.concatenate(db_pair, axis=1))
            dfv = dlogf / fp - jnp.concatenate(dkf_pair, axis=1)
            dz_ref[:, ZFH + 2 * HD * p:ZFH + 2 * HD * (p + 1)] = (dfv * (1.0 - lbp) * sgp * (1.0 - sgp)).astype(BF16)
            dlb_ref[:, 2 * HD * p:2 * HD * (p + 1)] += _rowsum8(dfv * (1.0 - sgp))

    rev = lambda j: nb - 1 - j
    return pl.pallas_call(
        body, name="hgrn_bwd",
        out_shape=(jax.ShapeDtypeStruct((t, D_IN), BF16), jax.ShapeDtypeStruct((8, HG_W), F32),
                   jax.ShapeDtypeStruct((8, HD), F32)),
        grid=(nb,),
        in_specs=_hgrn_cols(rev) + [pl.BlockSpec((2, HG_W), lambda j: (0, 0)), pl.BlockSpec((1, HD), lambda j: (0, 0)),
                                    pl.BlockSpec((HB, HG_W), lambda j: (rev(j), 0)),
                                    pl.BlockSpec((4, NCH, HD, HD), lambda j: (0, rev(j), 0, 0)),
                                    pl.BlockSpec((HB, HG_W), lambda j: (rev(j), 1)),
                                    pl.BlockSpec((HB, SWA_W), lambda j: (rev(j), 0)),
                                    pl.BlockSpec((HB // 2, 2 * KV_W), lambda j: (rev(j), 0)),
                                    pl.BlockSpec((HB // 2, 2 * KV_W), lambda j: (rev(j), 0))],
        out_specs=(pl.BlockSpec((HB, D_IN), lambda j: (rev(j), 0)), pl.BlockSpec((8, HG_W), lambda j: (0, 0)),
                   pl.BlockSpec((8, HD), lambda j: (0, 0))),
        scratch_shapes=[pltpu.VMEM((4, HD, HD), F32)],
        compiler_params=_params(dimension_semantics=("arbitrary",)),
    )(*[z] * 8, hgrn_lb, onorm, o_save, sprev, dymix, *dza)


XB = 512


def _xattn_fwd(q, k, v, t):
    tb = min(XB, t)

    def body(q_ref, k_ref, v_ref, o_ref):
        for h in range(XH):
            cols = slice(XD * h, XD * (h + 1))
            s = _dot(q_ref[:, cols], k_ref[:, cols], 1, 1) * (XD ** -0.5)
            p = jnp.exp(s - jnp.max(s, axis=-1, keepdims=True))
            l = jnp.sum(p, axis=-1, keepdims=True)
            o_ref[:, cols] = (_dot(p.astype(BF16), v_ref[:, cols], 1, 0) * (1.0 / l)).astype(BF16)

    row = pl.BlockSpec((tb, D), lambda i: (i, 0))
    mem = pl.BlockSpec(k.shape, lambda i: (0, 0))
    return pl.pallas_call(
        body, name="xattn_fwd", out_shape=jax.ShapeDtypeStruct((t, D), BF16), grid=(t // tb,),
        in_specs=[row, mem, mem], out_specs=row, compiler_params=_params(),
    )(q, k, v)


def _xattn_bwd(q, k, v, do, t):
    tb = min(XB, t)

    def body(q_ref, k_ref, v_ref, do_ref, dq_ref, dk_ref, dv_ref):
        @pl.when(pl.program_id(0) == 0)
        def _():
            dk_ref[...] = jnp.zeros_like(dk_ref)
            dv_ref[...] = jnp.zeros_like(dv_ref)

        for h in range(XH):
            cols = slice(XD * h, XD * (h + 1))
            qh, kh, vh, doh = q_ref[:, cols], k_ref[:, cols], v_ref[:, cols], do_ref[:, cols]
            s = _dot(qh, kh, 1, 1) * (XD ** -0.5)
            p = jnp.exp(s - jnp.max(s, axis=-1, keepdims=True))
            p = p * (1.0 / jnp.sum(p, axis=-1, keepdims=True))
            dp = _dot(doh, vh, 1, 1)
            ds = (p * (dp - jnp.sum(p * dp, axis=-1, keepdims=True)) * (XD ** -0.5)).astype(BF16)
            dq_ref[:, cols] = _dot(ds, kh, 1, 0).astype(BF16)
            dk_ref[:, cols] += _dot(ds, qh, 0, 0)
            dv_ref[:, cols] += _dot(p.astype(BF16), doh, 0, 0)

    row = pl.BlockSpec((tb, D), lambda i: (i, 0))
    mem = pl.BlockSpec(k.shape, lambda i: (0, 0))
    return pl.pallas_call(
        body, name="xattn_bwd",
        out_shape=(jax.ShapeDtypeStruct((t, D), BF16), jax.ShapeDtypeStruct(k.shape, F32),
                   jax.ShapeDtypeStruct(k.shape, F32)),
        grid=(t // tb,), in_specs=[row, mem, mem, row], out_specs=(row, mem, mem),
        compiler_params=_params(dimension_semantics=("arbitrary",)),
    )(q, k, v, do)


def _mem_gain_bwd(dm, mem, *, name):
    def body(dm_ref, m_ref, dg_ref):
        m_ = m_ref[...]
        dg_ref[...] = _rowsum8(dm_ref[...] * (m_ * _rstd(m_)))

    return pl.pallas_call(body, name=name, out_shape=jax.ShapeDtypeStruct((8, D), F32),
                          compiler_params=_params())(dm, mem)


FM, FN = 512, 1408


def _ffn_up(u, wgt, wut, t):
    tm = min(FM, t)

    def body(u_ref, wg_ref, wu_ref, g_ref, up_ref, a_ref):
        u_ = u_ref[...]
        g = _dot(u_, wg_ref[...], 1, 1)
        up = _dot(u_, wu_ref[...], 1, 1)
        g_ref[...] = g.astype(BF16)
        up_ref[...] = up.astype(BF16)
        a_ref[...] = (g * _sig(g) * up).astype(BF16)

    w = pl.BlockSpec((FN, D), lambda j, i: (j, 0))
    o = pl.BlockSpec((tm, FN), lambda j, i: (i, j))
    return pl.pallas_call(
        body, name="ffn_up", out_shape=(jax.ShapeDtypeStruct((t, D_FF), BF16),) * 3,
        grid=(D_FF // FN, t // tm), in_specs=[pl.BlockSpec((tm, D), lambda j, i: (i, 0)), w, w],
        out_specs=(o, o, o), compiler_params=_params(),
    )(u, wgt, wut)


def _ffn_down_bwd(dy, wd, gate, up, t, dep=None):
    tm = min(FM, t)
    deps = [] if dep is None else [dep]

    def body(dy_ref, w_ref, g_ref, up_ref, *rest):
        dg_ref, dup_ref = rest[len(deps):]
        da = _dot(dy_ref[...], w_ref[...], 1, 1)
        g = g_ref[...].astype(F32)
        sg = _sig(g)
        dup_ref[...] = (da * g * sg).astype(BF16)
        dg_ref[...] = (da * up_ref[...].astype(F32) * (sg * (1.0 + g * (1.0 - sg)))).astype(BF16)

    o = pl.BlockSpec((tm, FN), lambda j, i: (i, j))
    return pl.pallas_call(
        body, name="ffn_down_bwd", out_shape=(jax.ShapeDtypeStruct((t, D_FF), BF16),) * 2,
        grid=(D_FF // FN, t // tm),
        in_specs=[pl.BlockSpec((tm, D), lambda j, i: (i, 0)), pl.BlockSpec((FN, D), lambda j, i: (j, 0)), o, o]
        + [ANY_SPEC] * len(deps),
        out_specs=(o, o), compiler_params=_params(),
    )(dy, wd, gate, up, *deps)


def _local_step(x, mem, target, fetch, sm, emit=None):
    t = x.shape[0]
    w, gw = {}, {}

    def out(key, g):
        gw[key] = g
        return None if emit is None else emit(key, g)
    u1 = _prenorm(x, sm["g_mix_pre"], name="prenorm_mix")
    w["winT"] = fetch("winT", u1)
    z = _mm(u1, w["winT"], tb=True, out_dtype=F32, tm=1024, tn=1408, name="mm_z", n_outer=True)
    ymix, lse = _swa_fwd(z, sm["sinks"], t)
    ymix, o_h, sprev = _hgrn2_fwd(z, sm["hgrn_lb"], sm["hgrn_onorm"], ymix, t)
    w["wout"] = fetch("wout", ymix)
    y1 = _mm(ymix, w["wout"], out_dtype=F32, tm=1024, tn=1024, name="mm_y1")
    h1, u2 = _post_pre(x, y1, sm["g_mix_post"], sm["g_x_pre"], name="post_mix")
    mn = _prenorm(mem, sm["g_mem"], name="prenorm_mem")
    for key in ("wq", "wk", "wv"):
        w[key] = fetch(key, u2)
    qx = _mm(u2, w["wq"], out_dtype=BF16, tm=1024, tn=1024, name="mm_qx")
    kx = _mm(mn, w["wk"], out_dtype=BF16, tm=1024, tn=1024, name="mm_kx")
    vx = _mm(mn, w["wv"], out_dtype=BF16, tm=1024, tn=1024, name="mm_vx")
    ox = _xattn_fwd(qx, kx, vx, t)
    w["wo"] = fetch("wo", ox)
    y2 = _mm(ox, w["wo"], out_dtype=F32, tm=1024, tn=1024, name="mm_y2")
    h2, u3 = _post_pre(h1, y2, sm["g_x_post"], sm["g_ffn_pre"], name="post_x")
    w["wgT"], w["wuT"] = fetch("wgT", u3), fetch("wuT", u3)
    gate, up, act = _ffn_up(u3, w["wgT"], w["wuT"], t)
    w["wd"] = fetch("wd", act)
    y3 = _mm(act, w["wd"], out_dtype=F32, tm=1024, tn=1024, name="mm_y3")
    sq, dh3, dy3, dg_ffn_post = _final_loss(h2, y3, sm["g_ffn_post"], target, name="final_loss")
    dep = out("wd", _mm(act, dy3, ta=True, out_dtype=BF16, tm=1408, tn=1024, tk=512, name="mm_gwd"))
    dgate, dup = _ffn_down_bwd(dy3, w["wd"], gate, up, t, dep=dep)
    dep = out("wgT", _mm(dgate, u3, ta=True, out_dtype=BF16, tm=1408, tn=1024, tk=512, name="mm_gwg"))
    dep = out("wuT", _mm(dup, u3, ta=True, out_dtype=BF16, tm=1408, tn=1024, tk=512, name="mm_gwu", dep=dep))
    du3 = _mm2(dgate, w["wgT"], dup, w["wuT"], tm=512, tk=D_FF, name="mm_du3", dep=dep)
    dh2, dy2, dg_ffn_pre, dg_x_post = _post_pre_bwd(dh3, du3, h2, y2, sm["g_x_post"], sm["g_ffn_pre"], name="post_x_bwd")
    dep = out("wo", _mm(ox, dy2, ta=True, out_dtype=BF16, tm=1024, tn=1024, tk=512, name="mm_gwo"))
    dox = _mm(dy2, w["wo"], tb=True, out_dtype=BF16, tm=1024, tn=1024, name="mm_dox", dep=dep)
    dqx, dkx, dvx = _xattn_bwd(qx, kx, vx, dox, t)
    dep = out("wq", _mm(u2, dqx, ta=True, out_dtype=BF16, tm=1024, tn=1024, tk=512, name="mm_gwq"))
    dep = out("wk", _mm(mn, dkx, ta=True, out_dtype=BF16, tm=1024, tn=1024, name="mm_gwk", dep=dep))
    dep = out("wv", _mm(mn, dvx, ta=True, out_dtype=BF16, tm=1024, tn=1024, name="mm_gwv", dep=dep))
    du2 = _mm(dqx, w["wq"], tb=True, out_dtype=F32, tm=1024, tn=1024, name="mm_du2", dep=dep)
    dmn = _mm2(dkx, w["wk"], dvx, w["wv"], tb=True, tm=256, tk=1024, name="mm_dmn")
    dg_mem = _mem_gain_bwd(dmn, mem, name="mem_gain_bwd")
    dh1, dy1, dg_x_pre, dg_mix_post = _post_pre_bwd(dh2, du2, h1, y1, sm["g_mix_post"], sm["g_x_pre"], name="post_mix_bwd")
    dep = out("wout", _mm(ymix, dy1, ta=True, out_dtype=BF16, tm=1024, tn=1024, tk=512, name="mm_gwout"))
    dymix = _mm(dy1, w["wout"], tb=True, out_dtype=BF16, tm=1024, tn=1024, name="mm_dymix", dep=dep)
    *dza, dsinks = _swa_bwd(z, sm["sinks"], ymix, lse, dymix, t)
    dz, dlb, donorm = _hgrn2_bwd(z, sm["hgrn_lb"], sm["hgrn_onorm"], o_h, sprev, dymix, dza, t)
    dep = out("winT", _mm(dz, u1, ta=True, out_dtype=BF16, tm=1408, tn=1024, tk=512, name="mm_gwin"))
    du1 = _mm(dz, w["winT"], out_dtype=F32, tm=512, tn=1024, name="mm_du1", dep=dep)
    grad_x, dg_mix_pre = _pre_bwd(dh1, du1, x, sm["g_mix_pre"], name="pre_mix_bwd")
    parts = dict(g_mix_pre=dg_mix_pre, g_mix_post=dg_mix_post, g_mem=dg_mem, g_x_pre=dg_x_pre,
                 g_x_post=dg_x_post, g_ffn_pre=dg_ffn_pre, g_ffn_post=dg_ffn_post,
                 hgrn_onorm=donorm, hgrn_lb=dlb, sinks=dsinks, sq=sq)
    return grad_x, gw, parts


def _position():
    return lax.axis_index("x"), lax.axis_index("y"), lax.axis_index("c")


def _peer(pos, k):
    x, y, c = pos
    return (1 - x if k & 4 else x, 1 - y if k & 2 else y, 1 - c if k & 1 else c)


def _linear(pos):
    x, y, c = pos
    return 4 * x + 2 * y + c


HBM_SPEC = pl.BlockSpec(memory_space=pltpu.HBM)
SEM_SPEC = pl.BlockSpec(memory_space=pltpu.SEMAPHORE)
DATAFLOW = pltpu.SideEffectType.DATAFLOW_SIDE_EFFECTING
SEND_ORDER = (1, 2, 4, 3, 5, 6, 7)


def _in_hbm(a):
    return pltpu.with_memory_space_constraint(a, pltpu.HBM)


def _prepare_weights(shards):
    n = len(shards)

    def body(*refs):
        ins, outs, lands, sem = refs[:n], refs[n:2 * n], refs[2 * n:3 * n], refs[3 * n]
        me_lin = _linear(_position())
        copies = []
        for a in range(n):
            r = ins[a].shape[0]
            outs[a][...] = ins[a][...].astype(BF16)
            copies.append(pltpu.make_async_copy(outs[a], lands[a].at[pl.ds(me_lin * r, r), :], sem.at[a]))
            copies[-1].start()
        for cp in copies:
            cp.wait()

    vmem = pl.BlockSpec(memory_space=pltpu.VMEM)
    res = pl.pallas_call(
        body, name="prepare_weights",
        out_shape=tuple(jax.ShapeDtypeStruct(s.shape, BF16) for s in shards)
        + tuple(jax.ShapeDtypeStruct((N_DEV * s.shape[0], s.shape[1]), BF16) for s in shards),
        in_specs=[vmem] * n, out_specs=tuple([vmem] * n + [ANY_SPEC] * n),
        scratch_shapes=[pltpu.SemaphoreType.DMA((n,))], compiler_params=_params(),
    )(*shards)
    return res[:n], res[n:]


def _gather_start(shards, lands):
    n = len(shards)
    rows = [s.shape[0] for s in shards]

    def body(*refs):
        srcs, land = refs[:n], refs[n:2 * n]
        send_sems, recv_sems = refs[2 * n:3 * n], refs[3 * n:4 * n]
        me = _position()
        for a in range(n):
            mine = land[a].at[pl.ds(_linear(me) * rows[a], rows[a]), :]
            for k in SEND_ORDER:
                pltpu.make_async_remote_copy(
                    src_ref=srcs[a], dst_ref=mine, send_sem=send_sems[a].at[k - 1], recv_sem=recv_sems[a].at[k - 1],
                    device_id=_peer(me, k), device_id_type=MESH).start()

    sems = tuple(pltpu.SemaphoreType.DMA((N_DEV - 1,)) for _ in range(2 * n))
    res = pl.pallas_call(
        body, name="weights_send",
        out_shape=sems + tuple(pltpu.HBM(s.shape, s.dtype) for s in shards)
        + tuple(pltpu.HBM(l.shape, l.dtype) for l in lands),
        in_specs=(HBM_SPEC,) * (2 * n), out_specs=(SEM_SPEC,) * (2 * n) + (HBM_SPEC,) * (2 * n),
        input_output_aliases={i: 2 * n + i for i in range(2 * n)},
        compiler_params=pltpu.CompilerParams(has_side_effects=DATAFLOW),
    )(*[_in_hbm(s) for s in shards], *[_in_hbm(l) for l in lands])
    return [(res[a], res[n + a], res[2 * n + a], res[3 * n + a]) for a in range(n)]


def _gather_wait(send_sems, recv_sems, shard_thru, land_thru, after, *, name):
    r = shard_thru.shape[0]

    def body(src_ref, land_ref, send_sems, recv_sems, after_ref, src_dead, got_ref):
        del after_ref, src_dead, got_ref
        me = _position()
        for k in SEND_ORDER:
            peer = _peer(me, k)
            copy = pltpu.make_async_remote_copy(
                src_ref=src_ref, dst_ref=land_ref.at[pl.ds(_linear(peer) * r, r), :],
                send_sem=send_sems.at[k - 1], recv_sem=recv_sems.at[k - 1],
                device_id=peer, device_id_type=MESH)
            copy.wait_send()
            copy.wait_recv()

    return pl.pallas_call(
        body, name=name,
        out_shape=(pltpu.HBM(shard_thru.shape, shard_thru.dtype), pltpu.HBM(land_thru.shape, land_thru.dtype)),
        in_specs=(HBM_SPEC, HBM_SPEC, SEM_SPEC, SEM_SPEC, ANY_SPEC),
        out_specs=(HBM_SPEC, HBM_SPEC), input_output_aliases={0: 0, 1: 1},
        compiler_params=pltpu.CompilerParams(has_side_effects=DATAFLOW),
    )(shard_thru, land_thru, send_sems, recv_sems, after)[1]


def _exchange_start(gs, *, name):
    n = len(gs)
    rows = [g.shape[0] // N_DEV for g in gs]
    lands = [lax.empty((N_DEV - 1, r, g.shape[1]), g.dtype) for g, r in zip(gs, rows)]

    def body(*refs):
        g_refs, land_refs = refs[:n], refs[n:2 * n]
        send_sems, recv_sems = refs[2 * n:3 * n], refs[3 * n:4 * n]
        me = _position()
        for a in range(n):
            for k in SEND_ORDER:
                peer = _peer(me, k)
                pltpu.make_async_remote_copy(
                    src_ref=g_refs[a].at[pl.ds(_linear(peer) * rows[a], rows[a]), :],
                    dst_ref=land_refs[a].at[k - 1],
                    send_sem=send_sems[a].at[k - 1], recv_sem=recv_sems[a].at[k - 1],
                    device_id=peer, device_id_type=MESH).start()

    res = pl.pallas_call(
        body, name=name,
        out_shape=tuple(pltpu.SemaphoreType.DMA((N_DEV - 1,)) for _ in range(2 * n))
        + tuple(pltpu.HBM(a.shape, a.dtype) for a in gs + lands),
        in_specs=(HBM_SPEC,) * (2 * n), out_specs=(SEM_SPEC,) * (2 * n) + (HBM_SPEC,) * (2 * n),
        input_output_aliases={i: 2 * n + i for i in range(2 * n)},
        compiler_params=pltpu.CompilerParams(has_side_effects=DATAFLOW),
    )(*[_in_hbm(a) for a in gs + lands])
    return [(res[a], res[n + a], res[2 * n + a], res[3 * n + a]) for a in range(n)]


def _exchange_wait(send_sems, recv_sems, g_thru, land_thru, after, *, name):
    r = land_thru.shape[1]

    def body(g_ref, land_ref, send_sems, recv_sems, after_ref, g_dead, got_ref):
        del after_ref, g_dead, got_ref
        me = _position()
        for k in SEND_ORDER:
            peer = _peer(me, k)
            copy = pltpu.make_async_remote_copy(
                src_ref=g_ref.at[pl.ds(_linear(peer) * r, r), :], dst_ref=land_ref.at[k - 1],
                send_sem=send_sems.at[k - 1], recv_sem=recv_sems.at[k - 1],
                device_id=peer, device_id_type=MESH)
            copy.wait_send()
            copy.wait_recv()

    return pl.pallas_call(
        body, name=name,
        out_shape=(pltpu.HBM(g_thru.shape, g_thru.dtype), pltpu.HBM(land_thru.shape, land_thru.dtype)),
        in_specs=(HBM_SPEC, HBM_SPEC, SEM_SPEC, SEM_SPEC, pl.BlockSpec(memory_space=pl.ANY)),
        out_specs=(HBM_SPEC, HBM_SPEC), input_output_aliases={0: 0, 1: 1},
        compiler_params=pltpu.CompilerParams(has_side_effects=DATAFLOW),
    )(g_thru, land_thru, send_sems, recv_sems, after)


def _adamw_math(w, g, m, v):
    m = B1 * m + (1.0 - B1) * g
    v = B2 * v + (1.0 - B2) * (g * g)
    delta = -LR * ((m / C1) / (jnp.sqrt(v / C2) + AEPS) + WD * w)
    return delta, m, v


def _sum_adamw(g_all, land, w, m, v, *, name):
    r = w.shape[0]

    def body(all_ref, land_ref, w_ref, m_ref, v_ref, g_ref, d_ref, nm_ref, nv_ref, own_ref, sem):
        mine = pltpu.make_async_copy(all_ref.at[pl.ds(_linear(_position()) * r, r), :], own_ref, sem)
        mine.start()
        g = land_ref[0].astype(F32)
        for s in range(1, N_DEV - 1):
            g = g + land_ref[s].astype(F32)
        mine.wait()
        g = own_ref[...].astype(F32) + g
        g_ref[...] = g
        d_ref[...], nm_ref[...], nv_ref[...] = _adamw_math(w_ref[...], g, m_ref[...], v_ref[...])

    vmem = pl.BlockSpec(memory_space=pltpu.VMEM)
    return pl.pallas_call(
        body, name=name, out_shape=(jax.ShapeDtypeStruct(w.shape, F32),) * 4,
        in_specs=[ANY_SPEC, vmem, vmem, vmem, vmem], out_specs=(vmem,) * 4,
        scratch_shapes=[pltpu.VMEM((r, w.shape[1]), BF16), pltpu.SemaphoreType.DMA(())],
        compiler_params=_params(),
    )(g_all, land, w, m, v)


SMALL = ("g_mix_pre", "g_mix_post", "g_mem", "g_x_pre", "g_x_post", "g_ffn_pre", "g_ffn_post",
         "hgrn_onorm", "hgrn_lb", "sinks")
SMALL_W = dict(hgrn_onorm=HD, hgrn_lb=HG_W, sinks=8)
SQ_ROW = len(SMALL)
PACK_ROWS = 16


def _small_allreduce(parts):
    ns = len(SMALL)

    def body(*refs):
        part, tot_ref = refs[:ns + 1], refs[ns + 1]
        gath, send_sems, recv_sems = refs[ns + 2:]
        me = _position()
        mine = gath.at[_linear(me)]
        mine[...] = jnp.zeros((PACK_ROWS, D), F32)
        for r, name in enumerate(SMALL):
            wd = SMALL_W.get(name, D)
            mine[r:r + 1, 0:wd] = jnp.sum(part[r][...], axis=0, keepdims=True)[:, 0:wd]
        sq = jnp.sum(part[ns][...]) * (0.5 / D)
        mine[SQ_ROW:SQ_ROW + 1, :] = jnp.full((1, D), sq, F32)

        def copy(k):
            peer = _peer(me, k)
            return pltpu.make_async_remote_copy(
                src_ref=mine, dst_ref=mine, send_sem=send_sems.at[k - 1], recv_sem=recv_sems.at[k - 1],
                device_id=peer, device_id_type=MESH)

        def arrival(k):
            slot = gath.at[_linear(_peer(me, k))]
            return pltpu.make_async_remote_copy(
                src_ref=slot, dst_ref=slot, send_sem=send_sems.at[k - 1], recv_sem=recv_sems.at[k - 1],
                device_id=_peer(me, k), device_id_type=MESH)

        sent = [copy(k) for k in range(1, 8)]
        for cp in sent:
            cp.start()
        for k in range(1, 8):
            arrival(k).wait_recv()
        for cp in sent:
            cp.wait_send()
        tot = gath[0]
        for s in range(1, N_DEV):
            tot = tot + gath[s]
        tot_ref[...] = tot

    return pl.pallas_call(
        body, name="small_allreduce", out_shape=jax.ShapeDtypeStruct((PACK_ROWS, D), F32),
        scratch_shapes=[pltpu.VMEM((N_DEV, PACK_ROWS, D), F32), pltpu.SemaphoreType.DMA((7,)),
                        pltpu.SemaphoreType.DMA((7,))],
        compiler_params=_params(has_side_effects=True),
    )(*[parts[n] for n in SMALL], parts["sq"])


def _small_update(tot, sm, m_sm, v_sm):
    ns = len(SMALL)

    def body(*refs):
        tot = refs[0][...]
        w_refs, m_refs, v_refs = refs[1:ns + 1], refs[ns + 1:2 * ns + 1], refs[2 * ns + 1:3 * ns + 1]
        outs = refs[3 * ns + 1:]
        loss_ref = outs[0]
        g_out, d_out = outs[1:ns + 1], outs[ns + 1:2 * ns + 1]
        nm_out, nv_out = outs[2 * ns + 1:3 * ns + 1], outs[3 * ns + 1:4 * ns + 1]
        loss_ref[...] = tot[SQ_ROW:SQ_ROW + 1, 0:1]
        for r, name in enumerate(SMALL):
            wd = SMALL_W.get(name, D)
            g = tot[r:r + 1, 0:wd]
            w = w_refs[r][...]
            if name == "hgrn_lb":
                mx = jnp.maximum(w[0:1], w[1:2])
                e0, e1 = jnp.exp(w[0:1] - mx), jnp.exp(w[1:2] - mx)
                lb0 = e0 / (e0 + e1)
                g0 = g * lb0 * (1.0 - lb0)
                for i, gi in enumerate((g0, -g0)):
                    d, nm, nv = _adamw_math(w[i:i + 1], gi, m_refs[r][i:i + 1, :], v_refs[r][i:i + 1, :])
                    g_out[r][i:i + 1, :] = gi
                    d_out[r][i:i + 1, :], nm_out[r][i:i + 1, :], nv_out[r][i:i + 1, :] = d, nm, nv
            else:
                d, nm, nv = _adamw_math(w, g, m_refs[r][...], v_refs[r][...])
                g_out[r][...] = g
                d_out[r][...], nm_out[r][...], nv_out[r][...] = d, nm, nv

    shapes = [jax.ShapeDtypeStruct(sm[n].shape, F32) for n in SMALL]
    res = pl.pallas_call(
        body, name="small_update", out_shape=tuple([jax.ShapeDtypeStruct((1, 1), F32)] + shapes * 4),
        compiler_params=_params(),
    )(tot, *[sm[n] for n in SMALL], *[m_sm[n] for n in SMALL], *[v_sm[n] for n in SMALL])
    groups = [dict(zip(SMALL, res[1 + i * ns:1 + (i + 1) * ns])) for i in range(4)]
    return res[0], groups[0], groups[1], groups[2], groups[3]


BIG = ("w_in", "w_gate", "w_up", "w_down", "w_out", "wq_x", "wk_x", "wv_x", "wo_x")
BIG_KEY = dict(w_in="winT", w_gate="wgT", w_up="wuT", w_down="wd", w_out="wout", wq_x="wq", wk_x="wk",
               wv_x="wv", wo_x="wo")
TRANSPOSED = ("w_in", "w_gate", "w_up")
WEIGHTS = ("w_in", "sinks", "hgrn_lb", "hgrn_onorm", "w_out", "g_mix_pre", "g_mix_post", "g_mem", "g_x_pre",
           "g_x_post", "wq_x", "wk_x", "wv_x", "wo_x", "g_ffn_pre", "g_ffn_post", "w_gate", "w_up", "w_down")


def kernel(x, mem, w_in, sinks, hgrn_lb, hgrn_onorm, w_out, g_mix_pre, g_mix_post, g_mem, g_x_pre, g_x_post, wq_x, wk_x, wv_x, wo_x, g_ffn_pre, g_ffn_post, w_gate, w_up, w_down, loss_target, m_w_in, m_sinks, m_hgrn_lb, m_hgrn_onorm, m_w_out, m_g_mix_pre, m_g_mix_post, m_g_mem, m_g_x_pre, m_g_x_post, m_wq_x, m_wk_x, m_wv_x, m_wo_x, m_g_ffn_pre, m_g_ffn_post, m_w_gate, m_w_up, m_w_down, v_w_in, v_sinks, v_hgrn_lb, v_hgrn_onorm, v_w_out, v_g_mix_pre, v_g_mix_post, v_g_mem, v_g_x_pre, v_g_x_post, v_wq_x, v_wk_x, v_wv_x, v_wo_x, v_g_ffn_pre, v_g_ffn_post, v_w_gate, v_w_up, v_w_down):
    given = dict(locals())
    wts = {n: given[n] for n in WEIGHTS}
    ms = {n: given["m_" + n] for n in WEIGHTS}
    vs = {n: given["v_" + n] for n in WEIGHTS}

    def mat(a, name):
        a = a[0]
        return a.T if name in TRANSPOSED else a

    order = ("w_in", "w_out", "wq_x", "wk_x", "wv_x", "wo_x", "w_gate", "w_up", "w_down")
    flying = dict(zip(order, _gather_start(*_prepare_weights([mat(wts[n], n) for n in order]))))
    name_of = {k: n for n, k in BIG_KEY.items()}

    def fetch(key, after):
        return _gather_wait(*flying[name_of[key]], after, name="weights_recv_" + name_of[key])

    sm = {n: wts[n] for n in SMALL}
    started, held = {}, {}
    send_with = {"wgT": ("wgT", "wuT"), "wuT": ("wgT", "wuT"), "wq": ("wq", "wk", "wv"), "wk": ("wq", "wk", "wv"),
                 "wv": ("wq", "wk", "wv")}

    def emit(key, g):
        held[key] = g
        group = send_with.get(key, (key,))
        if key != group[-1]:
            return None
        flights = _exchange_start([held[k] for k in group], name="grad_send_" + name_of[group[0]])
        started.update({name_of[k]: f for k, f in zip(group, flights)})
        return flights[-1][2]

    grad_x, _, parts = _local_step(x[0], mem[0], loss_target[0], fetch, sm, emit)
    grads, deltas, new_m, new_v = {}, {}, {}, {}
    after = grad_x
    for n in ("w_down", "w_gate", "w_up", "wo_x", "wq_x", "wk_x", "wv_x", "w_out", "w_in"):
        g_all, land = _exchange_wait(*started[n], after, name="grad_recv_" + n)
        res = _sum_adamw(g_all, land, mat(wts[n], n), mat(ms[n], n), mat(vs[n], n), name="adamw_" + n)
        after = res[1]
        if n in TRANSPOSED:
            res = [a.T for a in res]
        grads[n], deltas[n], new_m[n], new_v[n] = [a[None] for a in res]
    loss, g_s, d_s, m_s, v_s = _small_update(_small_allreduce(parts), sm, {n: ms[n] for n in SMALL},
                                             {n: vs[n] for n in SMALL})
    grads.update(g_s), deltas.update(d_s), new_m.update(m_s), new_v.update(v_s)
    return (loss[0, 0], grad_x[None], *[grads[n] for n in WEIGHTS], *[deltas[n] for n in WEIGHTS],
            *[new_m[n] for n in WEIGHTS], *[new_v[n] for n in WEIGHTS])
```

```python
import functools

import jax
import jax.numpy as jnp
from jax import lax
from jax.experimental import pallas as pl
from jax.experimental.pallas import tpu as pltpu

F32 = jnp.float32
BF16 = jnp.bfloat16

D = 1024
D_IN = 2816
D_FF = 2816
CHUNK = 64
SWA_W = 512
KV_W = 128
HG_W = 512
HD = 128
ZQH, ZFH, ZIH, ZGH = 768, 1280, 1792, 2304
XH, XD = 4, 256
EPS = 1e-6
NEG = -1e30
N_DEV = 8
MESH = pl.DeviceIdType.MESH

LR, B1, B2, AEPS, WD, STEP = 0.001, 0.9, 0.999, 1e-08, 0.01, 10
C1 = 1.0 - B1 ** STEP
C2 = 1.0 - B2 ** STEP

VMEM_LIMIT = 56 * 1024 * 1024


def _params(**kw):
    return pltpu.CompilerParams(vmem_limit_bytes=VMEM_LIMIT, **kw)


def _sig(x):
    return 1.0 / (1.0 + jnp.exp(-x))


def _rowsum8(x):
    r, w = x.shape
    return jnp.sum(x.reshape(r // 8, 8, w), axis=0)


def _dot(a, b, ca, cb, precision=None):
    return lax.dot_general(a, b, (((ca,), (cb,)), ((), ())), preferred_element_type=F32,
                           precision=precision)


ANY_SPEC = pl.BlockSpec(memory_space=pl.ANY)


def _mm(a, b, *, ta=False, tb=False, out_dtype, tm, tn, tk=None, name, dep=None, n_outer=False):
    m = a.shape[1] if ta else a.shape[0]
    k = a.shape[0] if ta else a.shape[1]
    n = b.shape[0] if tb else b.shape[1]
    tm, tn = min(tm, m), min(tn, n)
    tk = k if tk is None else min(tk, k)
    nk = k // tk
    assert m % tm == 0 and n % tn == 0 and k % tk == 0, (name, m, n, k, tm, tn, tk)
    ij = (lambda g0, g1: (g1, g0)) if n_outer else (lambda g0, g1: (g0, g1))
    a_spec = (pl.BlockSpec((tk, tm), lambda g0, g1, kk: (kk, ij(g0, g1)[0])) if ta
              else pl.BlockSpec((tm, tk), lambda g0, g1, kk: (ij(g0, g1)[0], kk)))
    b_spec = (pl.BlockSpec((tn, tk), lambda g0, g1, kk: (ij(g0, g1)[1], kk)) if tb
              else pl.BlockSpec((tk, tn), lambda g0, g1, kk: (kk, ij(g0, g1)[1])))
    ca, cb = (0 if ta else 1), (1 if tb else 0)

    deps = [] if dep is None else [dep]

    def body(a_ref, b_ref, *rest):
        o_ref, acc = rest[len(deps)], rest[len(deps) + 1:]
        p = _dot(a_ref[...].astype(BF16), b_ref[...].astype(BF16), ca, cb)
        if nk == 1:
            o_ref[...] = p.astype(out_dtype)
        else:
            acc_ref, = acc
            kk = pl.program_id(2)

            @pl.when(kk == 0)
            def _():
                acc_ref[...] = p

            @pl.when(kk > 0)
            def _():
                acc_ref[...] += p

            @pl.when(kk == nk - 1)
            def _():
                o_ref[...] = acc_ref[...].astype(out_dtype)

    return pl.pallas_call(
        body, name=name, out_shape=jax.ShapeDtypeStruct((m, n), out_dtype),
        grid=(n // tn, m // tm, nk) if n_outer else (m // tm, n // tn, nk),
        in_specs=[a_spec, b_spec] + [ANY_SPEC] * len(deps),
        out_specs=pl.BlockSpec((tm, tn), lambda g0, g1, kk: ij(g0, g1)),
        scratch_shapes=[pltpu.VMEM((tm, tn), F32)] if nk > 1 else [],
        compiler_params=_params(dimension_semantics=("parallel", "parallel", "arbitrary")),
    )(a, b, *deps)


def _mm2(a1, b1, a2, b2, *, tb=False, tm, tk, name, dep=None):
    m, k = a1.shape
    n = b1.shape[0] if tb else b1.shape[1]
    tm, tk = min(tm, m), min(tk, k)
    nk = k // tk
    assert m % tm == 0 and k % tk == 0
    cb = 1 if tb else 0
    deps = [] if dep is None else [dep]

    def body(a1_ref, b1_ref, a2_ref, b2_ref, *rest):
        o_ref = rest[len(deps)]
        p = (_dot(a1_ref[...].astype(BF16), b1_ref[...], 1, cb)
             + _dot(a2_ref[...].astype(BF16), b2_ref[...], 1, cb))
        kk = pl.program_id(1)

        @pl.when(kk == 0)
        def _():
            o_ref[...] = p

        @pl.when(kk > 0)
        def _():
            o_ref[...] += p

    a_spec = pl.BlockSpec((tm, tk), lambda i, kk: (i, kk))
    b_spec = pl.BlockSpec((n, tk), lambda i, kk: (0, kk)) if tb else pl.BlockSpec((tk, n), lambda i, kk: (kk, 0))
    return pl.pallas_call(
        body, name=name, out_shape=jax.ShapeDtypeStruct((m, n), F32),
        grid=(m // tm, nk), in_specs=[a_spec, b_spec, a_spec, b_spec] + [ANY_SPEC] * len(deps),
        out_specs=pl.BlockSpec((tm, n), lambda i, kk: (i, 0)),
        compiler_params=_params(dimension_semantics=("parallel", "arbitrary")),
    )(a1, b1, a2, b2, *deps)


def _rstd(x):
    return lax.rsqrt(jnp.mean(x * x, axis=-1, keepdims=True) + EPS)


def _norm_bwd(xh, r, t):
    return r * (t - xh * jnp.mean(xh * t, axis=-1, keepdims=True))


def _prenorm(x, g, *, name):
    t, d = x.shape
    tb = min(512, t)

    def body(x_ref, g_ref, o_ref):
        xf = x_ref[...]
        o_ref[...] = (xf * _rstd(xf) * g_ref[...]).astype(BF16)

    return pl.pallas_call(
        body, name=name, out_shape=jax.ShapeDtypeStruct((t, d), BF16), grid=(t // tb,),
        in_specs=[pl.BlockSpec((tb, d), lambda i: (i, 0)), pl.BlockSpec((1, d), lambda i: (0, 0))],
        out_specs=pl.BlockSpec((tb, d), lambda i: (i, 0)), compiler_params=_params(),
    )(x, g)


def _post_pre(h, y, g_post, g_pre, *, name):
    t, d = h.shape
    tb = min(512, t)

    def body(h_ref, y_ref, gp_ref, gn_ref, hn_ref, u_ref):
        y_ = y_ref[...]
        hn = h_ref[...] + y_ * _rstd(y_) * gp_ref[...]
        hn_ref[...] = hn
        u_ref[...] = (hn * _rstd(hn) * gn_ref[...]).astype(BF16)

    row = pl.BlockSpec((tb, d), lambda i: (i, 0))
    vec = pl.BlockSpec((1, d), lambda i: (0, 0))
    return pl.pallas_call(
        body, name=name, out_shape=(jax.ShapeDtypeStruct((t, d), F32), jax.ShapeDtypeStruct((t, d), BF16)),
        grid=(t // tb,), in_specs=[row, row, vec, vec], out_specs=(row, row), compiler_params=_params(),
    )(h, y, g_post, g_pre)


def _final_loss(h, y, g_post, target, *, name):
    t, d = h.shape
    tb = min(512, t)

    def body(h_ref, y_ref, g_ref, t_ref, sq_ref, dh_ref, dy_ref, dg_ref):
        @pl.when(pl.program_id(0) == 0)
        def _():
            sq_ref[...] = jnp.zeros_like(sq_ref)
            dg_ref[...] = jnp.zeros_like(dg_ref)

        y_ = y_ref[...]
        r = _rstd(y_)
        yh = y_ * r
        g = g_ref[...]
        err = h_ref[...] + yh * g - t_ref[...]
        sq_ref[...] += _rowsum8(err * err)
        dh = err * (1.0 / d)
        dh_ref[...] = dh
        dg_ref[...] += _rowsum8(dh * yh)
        dy_ref[...] = _norm_bwd(yh, r, dh * g).astype(BF16)

    row = pl.BlockSpec((tb, d), lambda i: (i, 0))
    vec = pl.BlockSpec((1, d), lambda i: (0, 0))
    acc = pl.BlockSpec((8, d), lambda i: (0, 0))
    return pl.pallas_call(
        body, name=name,
        out_shape=(jax.ShapeDtypeStruct((8, d), F32), jax.ShapeDtypeStruct((t, d), F32),
                   jax.ShapeDtypeStruct((t, d), BF16), jax.ShapeDtypeStruct((8, d), F32)),
        grid=(t // tb,), in_specs=[row, row, vec, row], out_specs=(acc, row, row, acc),
        compiler_params=_params(dimension_semantics=("arbitrary",)),
    )(h, y, g_post, target)


def _post_pre_bwd(dh_out, du, hn, y, g_post, g_pre, *, name):
    t, d = hn.shape
    tb = min(512, t)

    def body(dho_ref, du_ref, hn_ref, y_ref, gp_ref, gn_ref, dh_ref, dy_ref, dgn_ref, dgp_ref):
        @pl.when(pl.program_id(0) == 0)
        def _():
            dgn_ref[...] = jnp.zeros_like(dgn_ref)
            dgp_ref[...] = jnp.zeros_like(dgp_ref)

        hn_ = hn_ref[...]
        r2 = _rstd(hn_)
        xh = hn_ * r2
        du_ = du_ref[...]
        dgn_ref[...] += _rowsum8(du_ * xh)
        dh = dho_ref[...] + _norm_bwd(xh, r2, du_ * gn_ref[...])
        dh_ref[...] = dh
        y_ = y_ref[...]
        r1 = _rstd(y_)
        yh = y_ * r1
        dgp_ref[...] += _rowsum8(dh * yh)
        dy_ref[...] = _norm_bwd(yh, r1, dh * gp_ref[...]).astype(BF16)

    row = pl.BlockSpec((tb, d), lambda i: (i, 0))
    vec = pl.BlockSpec((1, d), lambda i: (0, 0))
    acc = pl.BlockSpec((8, d), lambda i: (0, 0))
    return pl.pallas_call(
        body, name=name,
        out_shape=(jax.ShapeDtypeStruct((t, d), F32), jax.ShapeDtypeStruct((t, d), BF16),
                   jax.ShapeDtypeStruct((8, d), F32), jax.ShapeDtypeStruct((8, d), F32)),
        grid=(t // tb,), in_specs=[row, row, row, row, vec, vec], out_specs=(row, row, acc, acc),
        compiler_params=_params(dimension_semantics=("arbitrary",)),
    )(dh_out, du, hn, y, g_post, g_pre)


def _pre_bwd(dh_out, du, x, g, *, name):
    t, d = x.shape
    tb = min(512, t)
    has_res = dh_out is not None

    def body(*refs):
        if has_res:
            dho_ref, du_ref, x_ref, g_ref, dx_ref, dg_ref = refs
        else:
            du_ref, x_ref, g_ref, dx_ref, dg_ref = refs

        @pl.when(pl.program_id(0) == 0)
        def _():
            dg_ref[...] = jnp.zeros_like(dg_ref)

        x_ = x_ref[...]
        r = _rstd(x_)
        xh = x_ * r
        du_ = du_ref[...]
        dg_ref[...] += _rowsum8(du_ * xh)
        dx = _norm_bwd(xh, r, du_ * g_ref[...])
        if has_res:
            dx = dx + dho_ref[...]
        dx_ref[...] = dx

    row = pl.BlockSpec((tb, d), lambda i: (i, 0))
    vec = pl.BlockSpec((1, d), lambda i: (0, 0))
    acc = pl.BlockSpec((8, d), lambda i: (0, 0))
    ins = ([dh_out] if has_res else []) + [du, x, g]
    return pl.pallas_call(
        body, name=name,
        out_shape=(jax.ShapeDtypeStruct((t, d), F32), jax.ShapeDtypeStruct((8, d), F32)),
        grid=(t // tb,), in_specs=[row] * (len(ins) - 1) + [vec], out_specs=(row, acc),
        compiler_params=_params(dimension_semantics=("arbitrary",)),
    )(*ins)


QB = 256


def _half_mask(shape, e):
    lane = lax.broadcasted_iota(jnp.int32, shape, len(shape) - 1)
    return (lane // 64) == e


def _place(kv):
    sw = pltpu.roll(kv, 64, 1)
    m0 = _half_mask(kv.shape, 0)
    return [[jnp.where(m0, kv, 0.0).astype(BF16), jnp.where(m0, 0.0, sw).astype(BF16)],
            [jnp.where(m0, sw, 0.0).astype(BF16), jnp.where(m0, 0.0, kv).astype(BF16)]]


def _swa_valid_q(i, nq, nk):
    qc = lax.broadcasted_iota(jnp.int32, (nq, nk), 0) // CHUNK
    kc = lax.broadcasted_iota(jnp.int32, (nq, nk), 1) // CHUNK - 2
    return (kc <= qc) & (qc <= kc + 2) & (4 * i + kc >= 0)


def _swa_fwd(z, sinks, t):
    nb = t // QB

    def body(s_ref, q_ref, kp_ref, kc_ref, vp_ref, vc_ref, o_ref, lse_ref):
        i = pl.program_id(0)
        kpl = _place(jnp.concatenate([kp_ref[...], kc_ref[...]], axis=0))
        vpl = _place(jnp.concatenate([vp_ref[...], vc_ref[...]], axis=0))
        valid = _swa_valid_q(i, QB, QB + 128)
        lane = lax.broadcasted_iota(jnp.int32, (QB, 128), 1)
        lse_out = jnp.zeros((QB, 128), F32)
        for j in range(4):
            qp = q_ref[:, 128 * j:128 * (j + 1)].astype(BF16)
            acc = jnp.zeros((QB, 128), F32)
            for e in range(2):
                h = 2 * j + e
                kvh = h // 4
                qm = jnp.where(_half_mask(qp.shape, e), qp, jnp.zeros_like(qp))
                s = _dot(qm, kpl[kvh][e], 1, 1) * 0.125
                s = jnp.where(valid, s, NEG)
                sink = s_ref[0, h]
                m = jnp.maximum(jnp.max(s, axis=-1, keepdims=True), sink)
                p = jnp.exp(s - m)
                l = jnp.sum(p, axis=-1, keepdims=True) + jnp.exp(sink - m)
                acc = acc + _dot(p.astype(BF16), vpl[kvh][e], 1, 0) * (1.0 / l)
                lse_out = jnp.where(lane == h, m + jnp.log(l), lse_out)
            o_ref[:, 128 * j:128 * (j + 1)] = acc.astype(BF16)
        lse_ref[...] = lse_out

    prev = lambda c: pl.BlockSpec((128, 128), lambda i: (jnp.maximum(2 * i - 1, 0), c))
    cur = lambda c: pl.BlockSpec((QB, 128), lambda i: (i, c))
    return pl.pallas_call(
        body, name="swa_fwd",
        out_shape=(jax.ShapeDtypeStruct((t, D), BF16), jax.ShapeDtypeStruct((t, 128), F32)),
        grid=(nb,),
        in_specs=[pl.BlockSpec(memory_space=pltpu.SMEM),
                  pl.BlockSpec((QB, SWA_W), lambda i: (i, 0)), prev(4), cur(4), prev(5), cur(5)],
        out_specs=(pl.BlockSpec((QB, SWA_W), lambda i: (i, 0)), pl.BlockSpec((QB, 128), lambda i: (i, 0))),
        compiler_params=_params(),
    )(sinks, z, z, z, z, z)


def _swa_bwd(z, sinks, ymix, lse, dymix, t):
    nb = t // QB
    nk = QB + 128

    def body(s_ref, q_ref, kp_ref, kc_ref, vp_ref, vc_ref, o_ref, do_ref, l_ref,
             dq_ref, first_ref, second_ref, ds_ref, carry_ref):
        i = pl.program_id(0)
        live = i < nb

        @pl.when(i == 0)
        def _():
            ds_ref[...] = jnp.zeros_like(ds_ref)
            carry_ref[...] = jnp.zeros_like(carry_ref)

        lane = lax.broadcasted_iota(jnp.int32, (8, 128), 1)
        kpl = _place(jnp.concatenate([kp_ref[...], kc_ref[...]], axis=0))
        vpl = _place(jnp.concatenate([vp_ref[...], vc_ref[...]], axis=0))
        valid = _swa_valid_q(i, QB, nk) & live
        lse_c = l_ref[...]
        dsink = jnp.zeros((8, 128), F32)
        dk_acc = [[jnp.zeros((nk, 128), F32) for _ in range(2)] for _ in range(2)]
        dv_acc = [[jnp.zeros((nk, 128), F32) for _ in range(2)] for _ in range(2)]
        dq = []
        for j in range(4):
            cols = slice(128 * j, 128 * (j + 1))
            qp = q_ref[:, cols].astype(BF16)
            dop = do_ref[:, cols]
            prod = dop.astype(F32) * o_ref[:, cols].astype(F32)
            acc = jnp.zeros((QB, 128), F32)
            for e in range(2):
                h = 2 * j + e
                kvh = h // 4
                hm = _half_mask(qp.shape, e)
                qm = jnp.where(hm, qp, jnp.zeros_like(qp))
                dom = jnp.where(hm, dop, jnp.zeros_like(dop))
                dd = jnp.sum(jnp.where(hm, prod, 0.0), axis=-1, keepdims=True)
                lse_h = lse_c[:, h:h + 1]
                s = _dot(qm, kpl[kvh][e], 1, 1) * 0.125
                p = jnp.where(valid, jnp.exp(s - lse_h), 0.0)
                dp = _dot(dom, vpl[kvh][e], 1, 1)
                ds = (p * (dp - dd) * 0.125).astype(BF16)
                acc = acc + _dot(ds, kpl[kvh][e], 1, 0)
                dk_acc[kvh][e] = dk_acc[kvh][e] + _dot(ds, qm, 0, 0)
                dv_acc[kvh][e] = dv_acc[kvh][e] + _dot(p.astype(BF16), dom, 0, 0)
                ps = jnp.where(live, jnp.exp(s_ref[0, h] - lse_h) * dd, 0.0)
                dsink = dsink - jnp.where(lane == h, _rowsum8(jnp.broadcast_to(ps, (QB, 128))), 0.0)
            dq.append(acc.astype(BF16))
        ds_ref[...] += dsink
        dk = dk_acc[0][0] + dk_acc[1][1] + pltpu.roll(dk_acc[0][1] + dk_acc[1][0], 64, 1)
        dv = dv_acc[0][0] + dv_acc[1][1] + pltpu.roll(dv_acc[0][1] + dv_acc[1][0], 64, 1)
        dkv = jnp.concatenate([dk, dv], axis=1)
        second_ref[...] = (carry_ref[...] + dkv[0:128]).astype(BF16)
        carry_ref[...] = dkv[256:384]

        @pl.when(live)
        def _():
            for j in range(4):
                dq_ref[:, 128 * j:128 * (j + 1)] = dq[j]
            first_ref[...] = dkv[128:256].astype(BF16)

    blk = lambda i: jnp.minimum(i, nb - 1)
    prev = lambda c: pl.BlockSpec((128, 128), lambda i: (jnp.maximum(2 * blk(i) - 1, 0), c))
    cur = lambda w, c: pl.BlockSpec((QB, w), lambda i: (blk(i), c))
    half = lambda index: pl.BlockSpec((128, 256), lambda i: (index(i), 0))
    return pl.pallas_call(
        body, name="swa_bwd",
        out_shape=(jax.ShapeDtypeStruct((t, SWA_W), BF16), jax.ShapeDtypeStruct((t // 2, 256), BF16),
                   jax.ShapeDtypeStruct((t // 2, 256), BF16), jax.ShapeDtypeStruct((8, 128), F32)),
        grid=(nb + 1,),
        in_specs=[pl.BlockSpec(memory_space=pltpu.SMEM),
                  cur(SWA_W, 0), prev(4), cur(128, 4), prev(5), cur(128, 5),
                  cur(SWA_W, 0), cur(SWA_W, 0), cur(128, 0)],
        out_specs=(cur(SWA_W, 0), half(blk), half(lambda i: jnp.maximum(i - 1, 0)),
                   pl.BlockSpec((8, 128), lambda i: (0, 0))),
        scratch_shapes=[pltpu.VMEM((128, 256), F32)],
        compiler_params=_params(dimension_semantics=("arbitrary",)),
    )(sinks, z, z, z, z, z, ymix, dymix, lse)


HB = 256


def _lower_bound(lb_ref):
    a = lb_ref[...]
    a0, a1 = a[0:1], a[1:2]
    mx = jnp.maximum(a0, a1)
    e0, e1 = jnp.exp(a0 - mx), jnp.exp(a1 - mx)
    return e0 / (e0 + e1)


def _hgrn_cols(row_block):
    return [pl.BlockSpec((HB, 2 * HD), lambda j, c=base // (2 * HD) + p: (row_block(j), c))
            for base in (ZQH, ZFH, ZIH, ZGH) for p in range(2)]


NCH = HB // CHUNK


def _split3(x):
    hi = x.astype(BF16)
    r1 = x - hi.astype(F32)
    mid = r1.astype(BF16)
    return hi, mid, (r1 - mid.astype(F32)).astype(BF16)


def _blockdiag(lower):
    r = lax.broadcasted_iota(jnp.int32, (HB, HB), 0)
    c = lax.broadcasted_iota(jnp.int32, (HB, HB), 1)
    return (r // CHUNK == c // CHUNK) & ((c <= r) if lower else (c >= r))


def _chunk_sums(mask_bf16, x):
    return sum(_dot(mask_bf16, part, 1, 0) for part in _split3(x))


def _per_chunk_rows(x, row):
    w = x.shape[1]
    picked = x.reshape(NCH, CHUNK, w)[:, row:row + 1, :]
    return jnp.broadcast_to(picked, (NCH, CHUNK, w)).reshape(HB, w)


def _chunk_stack(x, chunk_of_row):
    return jnp.concatenate([jnp.where(chunk_of_row == c, x, jnp.zeros_like(x)) for c in range(NCH)], axis=1)


def _chunk_pick(x, chunk_of_row):
    w = x.shape[1] // NCH
    out = jnp.zeros((HB, w), x.dtype)
    for c in range(NCH):
        out = jnp.where(chunk_of_row == c, x[:, c * w:(c + 1) * w], out)
    return out


def _hgrn_local(q, f, kf, b):
    sq = _sig(q)
    qf = q * sq * (HD ** -0.5)
    b_mid = _per_chunk_rows(b, CHUNK // 2 - 1)
    b_last = _per_chunk_rows(b, CHUNK - 1)
    qm = qf * jnp.exp(b - b_mid)
    km = kf * jnp.exp(b_mid - b)
    kl = kf * jnp.exp(b_last - b)
    qb = qf * jnp.exp(b)
    return dict(sq=sq, b_mid=b_mid, b_last=b_last, qm=qm, km=km, kl=kl, qb=qb)


def _hgrn2_fwd(z, hgrn_lb, onorm, ymix, t):
    nb = t // HB

    def body(*refs):
        zq, zf, zi, zg = refs[0:2], refs[2:4], refs[4:6], refs[6:8]
        lb_ref, on_ref, _, y_ref, o_ref, sp_ref, st_ref = refs[8:]

        @pl.when(pl.program_id(0) == 0)
        def _():
            st_ref[...] = jnp.zeros_like(st_ref)

        lb_all = _lower_bound(lb_ref)
        gn = on_ref[...]
        low = _blockdiag(True)
        low_b = low.astype(BF16)
        chunk_of_row = lax.broadcasted_iota(jnp.int32, (HB, HD), 0) // CHUNK
        for p in range(2):
            lbp = lb_all[:, 2 * HD * p:2 * HD * (p + 1)]
            fp = lbp + (1.0 - lbp) * _sig(zf[p][...])
            bp = _chunk_sums(low_b, jnp.log(fp))
            for e in range(2):
                h, ls = 2 * p + e, slice(e * HD, (e + 1) * HD)
                f = fp[:, ls]
                w = _hgrn_local(zq[p][:, ls], f, 1.0 - f, bp[:, ls])
                iv = zi[p][:, ls].astype(BF16)
                a = jnp.where(low, _dot(w["qm"].astype(BF16), w["km"].astype(BF16), 1, 1), 0.0)
                o = _dot(a.astype(BF16), iv, 1, 0)
                u = _dot(iv, _chunk_stack(w["kl"].astype(BF16), chunk_of_row), 0, 0)
                decay = jnp.exp(w["b_last"])
                st = st_ref[h]
                states = []
                for c in range(NCH):
                    sp_ref[h, c] = st
                    states.append(st.astype(BF16))
                    st = st * decay[c * CHUNK:c * CHUNK + 1] + u[:, c * HD:(c + 1) * HD]
                st_ref[h] = st
                inter = _dot(w["qb"].astype(BF16), jnp.concatenate(states, axis=0), 1, 1)
                o = o + _chunk_pick(inter, chunk_of_row)
                hs = slice(h * HD, (h + 1) * HD)
                o_ref[:, hs] = o
                gg = zg[p][:, ls]
                y_ref[:, hs] = (o * _rstd(o) * gn * (gg * _sig(gg))).astype(BF16)

    return pl.pallas_call(
        body, name="hgrn_fwd",
        out_shape=(jax.ShapeDtypeStruct((t, D), BF16), jax.ShapeDtypeStruct((t, HG_W), F32),
                   jax.ShapeDtypeStruct((4, t // CHUNK, HD, HD), F32)),
        grid=(nb,),
        in_specs=_hgrn_cols(lambda j: j) + [pl.BlockSpec((2, HG_W), lambda j: (0, 0)),
                                            pl.BlockSpec((1, HD), lambda j: (0, 0)), ANY_SPEC],
        out_specs=(pl.BlockSpec((HB, HG_W), lambda j: (j, 1)),
                   pl.BlockSpec((HB, HG_W), lambda j: (j, 0)),
                   pl.BlockSpec((4, NCH, HD, HD), lambda j: (0, j, 0, 0))),
        scratch_shapes=[pltpu.VMEM((4, HD, HD), F32)],
        input_output_aliases={10: 0},
        compiler_params=_params(dimension_semantics=("arbitrary",)),
    )(*[z] * 8, hgrn_lb, onorm, ymix)


def _hgrn2_bwd(z, hgrn_lb, onorm, o_save, sprev, dymix, dza, t):
    nb = t // HB

    def body(*refs):
        zq, zf, zi, zg = refs[0:2], refs[2:4], refs[4:6], refs[6:8]
        (lb_ref, on_ref, o_ref, sp_ref, dy_ref, dqa_ref, first_ref, second_ref,
         dz_ref, dlb_ref, don_ref, dst_ref) = refs[8:]

        @pl.when(pl.program_id(0) == 0)
        def _():
            dst_ref[...] = jnp.zeros_like(dst_ref)
            dlb_ref[...] = jnp.zeros_like(dlb_ref)
            don_ref[...] = jnp.zeros_like(don_ref)

        dz_ref[:, 0:SWA_W] = dqa_ref[...]
        dz_ref[0:HB // 2, SWA_W:ZQH] = first_ref[...]
        dz_ref[HB // 2:HB, SWA_W:ZQH] = second_ref[...]
        lb_all = _lower_bound(lb_ref)
        gn = on_ref[...]
        low, upp = _blockdiag(True), _blockdiag(False)
        upp_b = upp.astype(BF16)
        low_b = low.astype(BF16)
        row = lax.broadcasted_iota(jnp.int32, (HB, HD), 0)
        chunk_of_row = row // CHUNK
        in_chunk = row % CHUNK
        for p in range(2):
            lbp = lb_all[:, 2 * HD * p:2 * HD * (p + 1)]
            sgp = _sig(zf[p][...])
            fp = lbp + (1.0 - lbp) * sgp
            bp = _chunk_sums(low_b, jnp.log(fp))
            db_pair, dkf_pair = [], []
            for e in range(2):
                h, ls, hs = 2 * p + e, slice(e * HD, (e + 1) * HD), slice((2 * p + e) * HD, (2 * p + e + 1) * HD)
                f = fp[:, ls]
                q = zq[p][:, ls]
                w = _hgrn_local(q, f, 1.0 - f, bp[:, ls])
                iv = zi[p][:, ls].astype(BF16)
                gg = zg[p][:, ls]
                o = o_ref[:, hs]
                dout = dy_ref[:, hs].astype(F32)
                sgg = _sig(gg)
                r = _rstd(o)
                oh = o * r
                dyn = dout * (gg * sgg)
                dz_ref[:, ZGH + h * HD:ZGH + (h + 1) * HD] = (
                    dout * oh * gn * (sgg * (1.0 + gg * (1.0 - sgg)))).astype(BF16)
                don_ref[...] += _rowsum8(dyn * oh)
                do = _norm_bwd(oh, r, dyn * gn).astype(BF16)
                qm, km, kl, qb = (w[n].astype(BF16) for n in ("qm", "km", "kl", "qb"))
                decay = jnp.exp(w["b_last"])
                grads_in = _dot(do, _chunk_stack(qb, chunk_of_row), 0, 0)
                dst = dst_ref[h]
                dstn, dd_rows = [None] * NCH, [None] * NCH
                for c in reversed(range(NCH)):
                    dstn[c] = dst.astype(BF16)
                    dd_rows[c] = jnp.sum(dst * sp_ref[h, c], axis=0, keepdims=True)
                    dst = dst * decay[c * CHUNK:c * CHUNK + 1] + grads_in[:, c * HD:(c + 1) * HD]
                dst_ref[h] = dst
                states = jnp.concatenate([sp_ref[h, c].astype(BF16) for c in range(NCH)], axis=0)
                dstn_all = jnp.concatenate(dstn, axis=0)
                dqb = _dot(_chunk_stack(do, chunk_of_row), states, 1, 0)
                at = jnp.where(upp, _dot(km, qm, 1, 1), 0.0)
                di = _dot(at.astype(BF16), do, 1, 0) + _chunk_pick(_dot(kl, dstn_all, 1, 1), chunk_of_row)
                dz_ref[:, ZIH + h * HD:ZIH + (h + 1) * HD] = di.astype(BF16)
                dkl = _dot(_chunk_stack(iv, chunk_of_row), dstn_all, 1, 0)
                da = jnp.where(low, _dot(do, iv, 1, 1), 0.0).astype(BF16)
                dat = jnp.where(upp, _dot(iv, do, 1, 1), 0.0).astype(BF16)
                dqm = _dot(da, km, 1, 0)
                dkm = _dot(dat, qm, 1, 0)
                b = bp[:, ls]
                e1, e2 = jnp.exp(b - w["b_mid"]), jnp.exp(w["b_mid"] - b)
                e3, e4 = jnp.exp(w["b_last"] - b), jnp.exp(b)
                dqf = dqm * e1 + dqb * e4
                dkf_pair.append(dkm * e2 + dkl * e3)
                t_qm, t_km, t_kl = dqm * w["qm"], dkm * w["km"], dkl * w["kl"]
                db = t_qm - t_km - t_kl + dqb * w["qb"]
                db_mid = jnp.sum((t_km - t_qm).reshape(NCH, CHUNK, HD), axis=1, keepdims=True)
                db_last = jnp.sum(t_kl.reshape(NCH, CHUNK, HD), axis=1, keepdims=True)
                db_last = db_last + jnp.stack(dd_rows, axis=0) * jnp.exp(
                    bp[:, ls].reshape(NCH, CHUNK, HD)[:, CHUNK - 1:CHUNK, :])
                spread = lambda v: jnp.broadcast_to(v, (NCH, CHUNK, HD)).reshape(HB, HD)
                db = (db + jnp.where(in_chunk == CHUNK // 2 - 1, spread(db_mid), 0.0)
                      + jnp.where(in_chunk == CHUNK - 1, spread(db_last), 0.0))
                db_pair.append(db)
                sq = w["sq"]
                dz_ref[:, ZQH + h * HD:ZQH + (h + 1) * HD] = (
                    dqf * (HD ** -0.5) * (sq * (1.0 + q * (1.0 - sq)))).astype(BF16)
            dlogf = _chunk_sums(upp_b, jnp.concatenate(db_pair, axis=1))
            dfv = dlogf / fp - jnp.concatenate(dkf_pair, axis=1)
            dz_ref[:, ZFH + 2 * HD * p:ZFH + 2 * HD * (p + 1)] = (dfv * (1.0 - lbp) * sgp * (1.0 - sgp)).astype(BF16)
            dlb_ref[:, 2 * HD * p:2 * HD * (p + 1)] += _rowsum8(dfv * (1.0 - sgp))

    rev = lambda j: nb - 1 - j
    return pl.pallas_call(
        body, name="hgrn_bwd",
        out_shape=(jax.ShapeDtypeStruct((t, D_IN), BF16), jax.ShapeDtypeStruct((8, HG_W), F32),
                   jax.ShapeDtypeStruct((8, HD), F32)),
        grid=(nb,),
        in_specs=_hgrn_cols(rev) + [pl.BlockSpec((2, HG_W), lambda j: (0, 0)), pl.BlockSpec((1, HD), lambda j: (0, 0)),
                                    pl.BlockSpec((HB, HG_W), lambda j: (rev(j), 0)),
                                    pl.BlockSpec((4, NCH, HD, HD), lambda j: (0, rev(j), 0, 0)),
                                    pl.BlockSpec((HB, HG_W), lambda j: (rev(j), 1)),
                                    pl.BlockSpec((HB, SWA_W), lambda j: (rev(j), 0)),
                                    pl.BlockSpec((HB // 2, 2 * KV_W), lambda j: (rev(j), 0)),
                                    pl.BlockSpec((HB // 2, 2 * KV_W), lambda j: (rev(j), 0))],
        out_specs=(pl.BlockSpec((HB, D_IN), lambda j: (rev(j), 0)), pl.BlockSpec((8, HG_W), lambda j: (0, 0)),
                   pl.BlockSpec((8, HD), lambda j: (0, 0))),
        scratch_shapes=[pltpu.VMEM((4, HD, HD), F32)],
        compiler_params=_params(dimension_semantics=("arbitrary",)),
    )(*[z] * 8, hgrn_lb, onorm, o_save, sprev, dymix, *dza)


XB = 512


def _xattn_fwd(q, k, v, t):
    tb = min(XB, t)

    def body(q_ref, k_ref, v_ref, o_ref):
        for h in range(XH):
            cols = slice(XD * h, XD * (h + 1))
            s = _dot(q_ref[:, cols], k_ref[:, cols], 1, 1) * (XD ** -0.5)
            p = jnp.exp(s - jnp.max(s, axis=-1, keepdims=True))
            l = jnp.sum(p, axis=-1, keepdims=True)
            o_ref[:, cols] = (_dot(p.astype(BF16), v_ref[:, cols], 1, 0) * (1.0 / l)).astype(BF16)

    row = pl.BlockSpec((tb, D), lambda i: (i, 0))
    mem = pl.BlockSpec(k.shape, lambda i: (0, 0))
    return pl.pallas_call(
        body, name="xattn_fwd", out_shape=jax.ShapeDtypeStruct((t, D), BF16), grid=(t // tb,),
        in_specs=[row, mem, mem], out_specs=row, compiler_params=_params(),
    )(q, k, v)


def _xattn_bwd(q, k, v, do, t):
    tb = min(XB, t)

    def body(q_ref, k_ref, v_ref, do_ref, dq_ref, dk_ref, dv_ref):
        @pl.when(pl.program_id(0) == 0)
        def _():
            dk_ref[...] = jnp.zeros_like(dk_ref)
            dv_ref[...] = jnp.zeros_like(dv_ref)

        for h in range(XH):
            cols = slice(XD * h, XD * (h + 1))
            qh, kh, vh, doh = q_ref[:, cols], k_ref[:, cols], v_ref[:, cols], do_ref[:, cols]
            s = _dot(qh, kh, 1, 1) * (XD ** -0.5)
            p = jnp.exp(s - jnp.max(s, axis=-1, keepdims=True))
            p = p * (1.0 / jnp.sum(p, axis=-1, keepdims=True))
            dp = _dot(doh, vh, 1, 1)
            ds = (p * (dp - jnp.sum(p * dp, axis=-1, keepdims=True)) * (XD ** -0.5)).astype(BF16)
            dq_ref[:, cols] = _dot(ds, kh, 1, 0).astype(BF16)
            dk_ref[:, cols] += _dot(ds, qh, 0, 0)
            dv_ref[:, cols] += _dot(p.astype(BF16), doh, 0, 0)

    row = pl.BlockSpec((tb, D), lambda i: (i, 0))
    mem = pl.BlockSpec(k.shape, lambda i: (0, 0))
    return pl.pallas_call(
        body, name="xattn_bwd",
        out_shape=(jax.ShapeDtypeStruct((t, D), BF16), jax.ShapeDtypeStruct(k.shape, F32),
                   jax.ShapeDtypeStruct(k.shape, F32)),
        grid=(t // tb,), in_specs=[row, mem, mem, row], out_specs=(row, mem, mem),
        compiler_params=_params(dimension_semantics=("arbitrary",)),
    )(q, k, v, do)


def _mem_gain_bwd(dm, mem, *, name):
    def body(dm_ref, m_ref, dg_ref):
        m_ = m_ref[...]
        dg_ref[...] = _rowsum8(dm_ref[...] * (m_ * _rstd(m_)))

    return pl.pallas_call(body, name=name, out_shape=jax.ShapeDtypeStruct((8, D), F32),
                          compiler_params=_params())(dm, mem)


FM, FN = 512, 1408


def _ffn_up(u, wgt, wut, t):
    tm = min(FM, t)

    def body(u_ref, wg_ref, wu_ref, g_ref, up_ref, a_ref):
        u_ = u_ref[...]
        g = _dot(u_, wg_ref[...], 1, 1)
        up = _dot(u_, wu_ref[...], 1, 1)
        g_ref[...] = g.astype(BF16)
        up_ref[...] = up.astype(BF16)
        a_ref[...] = (g * _sig(g) * up).astype(BF16)

    w = pl.BlockSpec((FN, D), lambda j, i: (j, 0))
    o = pl.BlockSpec((tm, FN), lambda j, i: (i, j))
    return pl.pallas_call(
        body, name="ffn_up", out_shape=(jax.ShapeDtypeStruct((t, D_FF), BF16),) * 3,
        grid=(D_FF // FN, t // tm), in_specs=[pl.BlockSpec((tm, D), lambda j, i: (i, 0)), w, w],
        out_specs=(o, o, o), compiler_params=_params(),
    )(u, wgt, wut)


def _ffn_down_bwd(dy, wd, gate, up, t, dep=None):
    tm = min(FM, t)
    deps = [] if dep is None else [dep]

    def body(dy_ref, w_ref, g_ref, up_ref, *rest):
        dg_ref, dup_ref = rest[len(deps):]
        da = _dot(dy_ref[...], w_ref[...], 1, 1)
        g = g_ref[...].astype(F32)
        sg = _sig(g)
        dup_ref[...] = (da * g * sg).astype(BF16)
        dg_ref[...] = (da * up_ref[...].astype(F32) * (sg * (1.0 + g * (1.0 - sg)))).astype(BF16)

    o = pl.BlockSpec((tm, FN), lambda j, i: (i, j))
    return pl.pallas_call(
        body, name="ffn_down_bwd", out_shape=(jax.ShapeDtypeStruct((t, D_FF), BF16),) * 2,
        grid=(D_FF // FN, t // tm),
        in_specs=[pl.BlockSpec((tm, D), lambda j, i: (i, 0)), pl.BlockSpec((FN, D), lambda j, i: (j, 0)), o, o]
        + [ANY_SPEC] * len(deps),
        out_specs=(o, o), compiler_params=_params(),
    )(dy, wd, gate, up, *deps)


def _local_step(x, mem, target, fetch, sm, emit=None):
    t = x.shape[0]
    w, gw = {}, {}

    def out(key, g):
        gw[key] = g
        return None if emit is None else emit(key, g)
    u1 = _prenorm(x, sm["g_mix_pre"], name="prenorm_mix")
    w["winT"] = fetch("winT", u1)
    z = _mm(u1, w["winT"], tb=True, out_dtype=F32, tm=1024, tn=1408, name="mm_z", n_outer=True)
    ymix, lse = _swa_fwd(z, sm["sinks"], t)
    ymix, o_h, sprev = _hgrn2_fwd(z, sm["hgrn_lb"], sm["hgrn_onorm"], ymix, t)
    w["wout"] = fetch("wout", ymix)
    y1 = _mm(ymix, w["wout"], out_dtype=F32, tm=1024, tn=1024, name="mm_y1")
    h1, u2 = _post_pre(x, y1, sm["g_mix_post"], sm["g_x_pre"], name="post_mix")
    mn = _prenorm(mem, sm["g_mem"], name="prenorm_mem")
    for key in ("wq", "wk", "wv"):
        w[key] = fetch(key, u2)
    qx = _mm(u2, w["wq"], out_dtype=BF16, tm=1024, tn=1024, name="mm_qx")
    kx = _mm(mn, w["wk"], out_dtype=BF16, tm=1024, tn=1024, name="mm_kx")
    vx = _mm(mn, w["wv"], out_dtype=BF16, tm=1024, tn=1024, name="mm_vx")
    ox = _xattn_fwd(qx, kx, vx, t)
    w["wo"] = fetch("wo", ox)
    y2 = _mm(ox, w["wo"], out_dtype=F32, tm=1024, tn=1024, name="mm_y2")
    h2, u3 = _post_pre(h1, y2, sm["g_x_post"], sm["g_ffn_pre"], name="post_x")
    w["wgT"], w["wuT"] = fetch("wgT", u3), fetch("wuT", u3)
    gate, up, act = _ffn_up(u3, w["wgT"], w["wuT"], t)
    w["wd"] = fetch("wd", act)
    y3 = _mm(act, w["wd"], out_dtype=F32, tm=1024, tn=1024, name="mm_y3")
    sq, dh3, dy3, dg_ffn_post = _final_loss(h2, y3, sm["g_ffn_post"], target, name="final_loss")
    dep = out("wd", _mm(act, dy3, ta=True, out_dtype=BF16, tm=1408, tn=1024, name="mm_gwd"))
    dgate, dup = _ffn_down_bwd(dy3, w["wd"], gate, up, t, dep=dep)
    dep = out("wgT", _mm(dgate, u3, ta=True, out_dtype=BF16, tm=1408, tn=1024, name="mm_gwg"))
    dep = out("wuT", _mm(dup, u3, ta=True, out_dtype=BF16, tm=1408, tn=1024, name="mm_gwu", dep=dep))
    du3 = _mm2(dgate, w["wgT"], dup, w["wuT"], tm=512, tk=D_FF, name="mm_du3", dep=dep)
    dh2, dy2, dg_ffn_pre, dg_x_post = _post_pre_bwd(dh3, du3, h2, y2, sm["g_x_post"], sm["g_ffn_pre"], name="post_x_bwd")
    dep = out("wo", _mm(ox, dy2, ta=True, out_dtype=BF16, tm=512, tn=1024, name="mm_gwo"))
    dox = _mm(dy2, w["wo"], tb=True, out_dtype=BF16, tm=1024, tn=1024, name="mm_dox", dep=dep)
    dqx, dkx, dvx = _xattn_bwd(qx, kx, vx, dox, t)
    dep = out("wq", _mm(u2, dqx, ta=True, out_dtype=BF16, tm=512, tn=1024, name="mm_gwq"))
    dep = out("wk", _mm(mn, dkx, ta=True, out_dtype=BF16, tm=1024, tn=1024, name="mm_gwk", dep=dep))
    dep = out("wv", _mm(mn, dvx, ta=True, out_dtype=BF16, tm=1024, tn=1024, name="mm_gwv", dep=dep))
    du2 = _mm(dqx, w["wq"], tb=True, out_dtype=F32, tm=1024, tn=1024, name="mm_du2", dep=dep)
    dmn = _mm2(dkx, w["wk"], dvx, w["wv"], tb=True, tm=256, tk=1024, name="mm_dmn")
    dg_mem = _mem_gain_bwd(dmn, mem, name="mem_gain_bwd")
    dh1, dy1, dg_x_pre, dg_mix_post = _post_pre_bwd(dh2, du2, h1, y1, sm["g_mix_post"], sm["g_x_pre"], name="post_mix_bwd")
    dep = out("wout", _mm(ymix, dy1, ta=True, out_dtype=BF16, tm=512, tn=1024, name="mm_gwout"))
    dymix = _mm(dy1, w["wout"], tb=True, out_dtype=BF16, tm=1024, tn=1024, name="mm_dymix", dep=dep)
    *dza, dsinks = _swa_bwd(z, sm["sinks"], ymix, lse, dymix, t)
    dz, dlb, donorm = _hgrn2_bwd(z, sm["hgrn_lb"], sm["hgrn_onorm"], o_h, sprev, dymix, dza, t)
    dep = out("winT", _mm(dz, u1, ta=True, out_dtype=BF16, tm=1408, tn=1024, name="mm_gwin"))
    du1 = _mm(dz, w["winT"], out_dtype=F32, tm=512, tn=1024, name="mm_du1", dep=dep)
    grad_x, dg_mix_pre = _pre_bwd(dh1, du1, x, sm["g_mix_pre"], name="pre_mix_bwd")
    parts = dict(g_mix_pre=dg_mix_pre, g_mix_post=dg_mix_post, g_mem=dg_mem, g_x_pre=dg_x_pre,
                 g_x_post=dg_x_post, g_ffn_pre=dg_ffn_pre, g_ffn_post=dg_ffn_post,
                 hgrn_onorm=donorm, hgrn_lb=dlb, sinks=dsinks, sq=sq)
    return grad_x, gw, parts


def _position():
    return lax.axis_index("x"), lax.axis_index("y"), lax.axis_index("c")


def _peer(pos, k):
    x, y, c = pos
    return (1 - x if k & 4 else x, 1 - y if k & 2 else y, 1 - c if k & 1 else c)


def _linear(pos):
    x, y, c = pos
    return 4 * x + 2 * y + c


HBM_SPEC = pl.BlockSpec(memory_space=pltpu.HBM)
SEM_SPEC = pl.BlockSpec(memory_space=pltpu.SEMAPHORE)
DATAFLOW = pltpu.SideEffectType.DATAFLOW_SIDE_EFFECTING
SEND_ORDER = (1, 2, 4, 3, 5, 6, 7)


def _in_hbm(a):
    return pltpu.with_memory_space_constraint(a, pltpu.HBM)


def _prepare_weights(shards):
    n = len(shards)

    def body(*refs):
        ins, outs, lands, sem = refs[:n], refs[n:2 * n], refs[2 * n:3 * n], refs[3 * n]
        me_lin = _linear(_position())
        copies = []
        for a in range(n):
            r = ins[a].shape[0]
            outs[a][...] = ins[a][...].astype(BF16)
            copies.append(pltpu.make_async_copy(outs[a], lands[a].at[pl.ds(me_lin * r, r), :], sem.at[a]))
            copies[-1].start()
        for cp in copies:
            cp.wait()

    vmem = pl.BlockSpec(memory_space=pltpu.VMEM)
    res = pl.pallas_call(
        body, name="prepare_weights",
        out_shape=tuple(jax.ShapeDtypeStruct(s.shape, BF16) for s in shards)
        + tuple(jax.ShapeDtypeStruct((N_DEV * s.shape[0], s.shape[1]), BF16) for s in shards),
        in_specs=[vmem] * n, out_specs=tuple([vmem] * n + [ANY_SPEC] * n),
        scratch_shapes=[pltpu.SemaphoreType.DMA((n,))], compiler_params=_params(),
    )(*shards)
    return res[:n], res[n:]


def _gather_start(shards, lands):
    n = len(shards)
    rows = [s.shape[0] for s in shards]

    def body(*refs):
        srcs, land = refs[:n], refs[n:2 * n]
        send_sems, recv_sems = refs[2 * n:3 * n], refs[3 * n:4 * n]
        me = _position()
        for a in range(n):
            mine = land[a].at[pl.ds(_linear(me) * rows[a], rows[a]), :]
            for k in SEND_ORDER:
                pltpu.make_async_remote_copy(
                    src_ref=srcs[a], dst_ref=mine, send_sem=send_sems[a].at[k - 1], recv_sem=recv_sems[a].at[k - 1],
                    device_id=_peer(me, k), device_id_type=MESH).start()

    sems = tuple(pltpu.SemaphoreType.DMA((N_DEV - 1,)) for _ in range(2 * n))
    res = pl.pallas_call(
        body, name="weights_send",
        out_shape=sems + tuple(pltpu.HBM(s.shape, s.dtype) for s in shards)
        + tuple(pltpu.HBM(l.shape, l.dtype) for l in lands),
        in_specs=(HBM_SPEC,) * (2 * n), out_specs=(SEM_SPEC,) * (2 * n) + (HBM_SPEC,) * (2 * n),
        input_output_aliases={i: 2 * n + i for i in range(2 * n)},
        compiler_params=pltpu.CompilerParams(has_side_effects=DATAFLOW),
    )(*[_in_hbm(s) for s in shards], *[_in_hbm(l) for l in lands])
    return [(res[a], res[n + a], res[2 * n + a], res[3 * n + a]) for a in range(n)]


def _gather_wait(send_sems, recv_sems, shard_thru, land_thru, after, *, name):
    r = shard_thru.shape[0]

    def body(src_ref, land_ref, send_sems, recv_sems, after_ref, src_dead, got_ref):
        del after_ref, src_dead, got_ref
        me = _position()
        for k in SEND_ORDER:
            peer = _peer(me, k)
            copy = pltpu.make_async_remote_copy(
                src_ref=src_ref, dst_ref=land_ref.at[pl.ds(_linear(peer) * r, r), :],
                send_sem=send_sems.at[k - 1], recv_sem=recv_sems.at[k - 1],
                device_id=peer, device_id_type=MESH)
            copy.wait_send()
            copy.wait_recv()

    return pl.pallas_call(
        body, name=name,
        out_shape=(pltpu.HBM(shard_thru.shape, shard_thru.dtype), pltpu.HBM(land_thru.shape, land_thru.dtype)),
        in_specs=(HBM_SPEC, HBM_SPEC, SEM_SPEC, SEM_SPEC, ANY_SPEC),
        out_specs=(HBM_SPEC, HBM_SPEC), input_output_aliases={0: 0, 1: 1},
        compiler_params=pltpu.CompilerParams(has_side_effects=DATAFLOW),
    )(shard_thru, land_thru, send_sems, recv_sems, after)[1]


def _exchange_start(gs, *, name):
    n = len(gs)
    rows = [g.shape[0] // N_DEV for g in gs]
    lands = [lax.empty((N_DEV - 1, r, g.shape[1]), g.dtype) for g, r in zip(gs, rows)]

    def body(*refs):
        g_refs, land_refs = refs[:n], refs[n:2 * n]
        send_sems, recv_sems = refs[2 * n:3 * n], refs[3 * n:4 * n]
        me = _position()
        for a in range(n):
            for k in SEND_ORDER:
                peer = _peer(me, k)
                pltpu.make_async_remote_copy(
                    src_ref=g_refs[a].at[pl.ds(_linear(peer) * rows[a], rows[a]), :],
                    dst_ref=land_refs[a].at[k - 1],
                    send_sem=send_sems[a].at[k - 1], recv_sem=recv_sems[a].at[k - 1],
                    device_id=peer, device_id_type=MESH).start()

    res = pl.pallas_call(
        body, name=name,
        out_shape=tuple(pltpu.SemaphoreType.DMA((N_DEV - 1,)) for _ in range(2 * n))
        + tuple(pltpu.HBM(a.shape, a.dtype) for a in gs + lands),
        in_specs=(HBM_SPEC,) * (2 * n), out_specs=(SEM_SPEC,) * (2 * n) + (HBM_SPEC,) * (2 * n),
        input_output_aliases={i: 2 * n + i for i in range(2 * n)},
        compiler_params=pltpu.CompilerParams(has_side_effects=DATAFLOW),
    )(*[_in_hbm(a) for a in gs + lands])
    return [(res[a], res[n + a], res[2 * n + a], res[3 * n + a]) for a in range(n)]


def _exchange_wait(send_sems, recv_sems, g_thru, land_thru, after, *, name):
    r = land_thru.shape[1]

    def body(g_ref, land_ref, send_sems, recv_sems, after_ref, g_dead, got_ref):
        del after_ref, g_dead, got_ref
        me = _position()
        for k in SEND_ORDER:
            peer = _peer(me, k)
            copy = pltpu.make_async_remote_copy(
                src_ref=g_ref.at[pl.ds(_linear(peer) * r, r), :], dst_ref=land_ref.at[k - 1],
                send_sem=send_sems.at[k - 1], recv_sem=recv_sems.at[k - 1],
                device_id=peer, device_id_type=MESH)
            copy.wait_send()
            copy.wait_recv()

    return pl.pallas_call(
        body, name=name,
        out_shape=(pltpu.HBM(g_thru.shape, g_thru.dtype), pltpu.HBM(land_thru.shape, land_thru.dtype)),
        in_specs=(HBM_SPEC, HBM_SPEC, SEM_SPEC, SEM_SPEC, pl.BlockSpec(memory_space=pl.ANY)),
        out_specs=(HBM_SPEC, HBM_SPEC), input_output_aliases={0: 0, 1: 1},
        compiler_params=pltpu.CompilerParams(has_side_effects=DATAFLOW),
    )(g_thru, land_thru, send_sems, recv_sems, after)


def _adamw_math(w, g, m, v):
    m = B1 * m + (1.0 - B1) * g
    v = B2 * v + (1.0 - B2) * (g * g)
    delta = -LR * ((m / C1) / (jnp.sqrt(v / C2) + AEPS) + WD * w)
    return delta, m, v


def _sum_adamw(g_all, land, w, m, v, *, name):
    r = w.shape[0]

    def body(all_ref, land_ref, w_ref, m_ref, v_ref, g_ref, d_ref, nm_ref, nv_ref, own_ref, sem):
        mine = pltpu.make_async_copy(all_ref.at[pl.ds(_linear(_position()) * r, r), :], own_ref, sem)
        mine.start()
        g = land_ref[0].astype(F32)
        for s in range(1, N_DEV - 1):
            g = g + land_ref[s].astype(F32)
        mine.wait()
        g = own_ref[...].astype(F32) + g
        g_ref[...] = g
        d_ref[...], nm_ref[...], nv_ref[...] = _adamw_math(w_ref[...], g, m_ref[...], v_ref[...])

    vmem = pl.BlockSpec(memory_space=pltpu.VMEM)
    return pl.pallas_call(
        body, name=name, out_shape=(jax.ShapeDtypeStruct(w.shape, F32),) * 4,
        in_specs=[ANY_SPEC, vmem, vmem, vmem, vmem], out_specs=(vmem,) * 4,
        scratch_shapes=[pltpu.VMEM((r, w.shape[1]), BF16), pltpu.SemaphoreType.DMA(())],
        compiler_params=_params(),
    )(g_all, land, w, m, v)


SMALL = ("g_mix_pre", "g_mix_post", "g_mem", "g_x_pre", "g_x_post", "g_ffn_pre", "g_ffn_post",
         "hgrn_onorm", "hgrn_lb", "sinks")
SMALL_W = dict(hgrn_onorm=HD, hgrn_lb=HG_W, sinks=8)
SQ_ROW = len(SMALL)
PACK_ROWS = 16


def _small_allreduce(parts):
    ns = len(SMALL)

    def body(*refs):
        part, tot_ref = refs[:ns + 1], refs[ns + 1]
        gath, send_sems, recv_sems = refs[ns + 2:]
        me = _position()
        mine = gath.at[_linear(me)]
        mine[...] = jnp.zeros((PACK_ROWS, D), F32)
        for r, name in enumerate(SMALL):
            wd = SMALL_W.get(name, D)
            mine[r:r + 1, 0:wd] = jnp.sum(part[r][...], axis=0, keepdims=True)[:, 0:wd]
        sq = jnp.sum(part[ns][...]) * (0.5 / D)
        mine[SQ_ROW:SQ_ROW + 1, :] = jnp.full((1, D), sq, F32)

        def copy(k):
            peer = _peer(me, k)
            return pltpu.make_async_remote_copy(
                src_ref=mine, dst_ref=mine, send_sem=send_sems.at[k - 1], recv_sem=recv_sems.at[k - 1],
                device_id=peer, device_id_type=MESH)

        def arrival(k):
            slot = gath.at[_linear(_peer(me, k))]
            return pltpu.make_async_remote_copy(
                src_ref=slot, dst_ref=slot, send_sem=send_sems.at[k - 1], recv_sem=recv_sems.at[k - 1],
                device_id=_peer(me, k), device_id_type=MESH)

        sent = [copy(k) for k in range(1, 8)]
        for cp in sent:
            cp.start()
        for k in range(1, 8):
            arrival(k).wait_recv()
        for cp in sent:
            cp.wait_send()
        tot = gath[0]
        for s in range(1, N_DEV):
            tot = tot + gath[s]
        tot_ref[...] = tot

    return pl.pallas_call(
        body, name="small_allreduce", out_shape=jax.ShapeDtypeStruct((PACK_ROWS, D), F32),
        scratch_shapes=[pltpu.VMEM((N_DEV, PACK_ROWS, D), F32), pltpu.SemaphoreType.DMA((7,)),
                        pltpu.SemaphoreType.DMA((7,))],
        compiler_params=_params(has_side_effects=True),
    )(*[parts[n] for n in SMALL], parts["sq"])


def _small_update(tot, sm, m_sm, v_sm):
    ns = len(SMALL)

    def body(*refs):
        tot = refs[0][...]
        w_refs, m_refs, v_refs = refs[1:ns + 1], refs[ns + 1:2 * ns + 1], refs[2 * ns + 1:3 * ns + 1]
        outs = refs[3 * ns + 1:]
        loss_ref = outs[0]
        g_out, d_out = outs[1:ns + 1], outs[ns + 1:2 * ns + 1]
        nm_out, nv_out = outs[2 * ns + 1:3 * ns + 1], outs[3 * ns + 1:4 * ns + 1]
        loss_ref[...] = tot[SQ_ROW:SQ_ROW + 1, 0:1]
        for r, name in enumerate(SMALL):
            wd = SMALL_W.get(name, D)
            g = tot[r:r + 1, 0:wd]
            w = w_refs[r][...]
            if name == "hgrn_lb":
                mx = jnp.maximum(w[0:1], w[1:2])
                e0, e1 = jnp.exp(w[0:1] - mx), jnp.exp(w[1:2] - mx)
                lb0 = e0 / (e0 + e1)
                g0 = g * lb0 * (1.0 - lb0)
                for i, gi in enumerate((g0, -g0)):
                    d, nm, nv = _adamw_math(w[i:i + 1], gi, m_refs[r][i:i + 1, :], v_refs[r][i:i + 1, :])
                    g_out[r][i:i + 1, :] = gi
                    d_out[r][i:i + 1, :], nm_out[r][i:i + 1, :], nv_out[r][i:i + 1, :] = d, nm, nv
            else:
                d, nm, nv = _adamw_math(w, g, m_refs[r][...], v_refs[r][...])
                g_out[r][...] = g
                d_out[r][...], nm_out[r][...], nv_out[r][...] = d, nm, nv

    shapes = [jax.ShapeDtypeStruct(sm[n].shape, F32) for n in SMALL]
    res = pl.pallas_call(
        body, name="small_update", out_shape=tuple([jax.ShapeDtypeStruct((1, 1), F32)] + shapes * 4),
        compiler_params=_params(),
    )(tot, *[sm[n] for n in SMALL], *[m_sm[n] for n in SMALL], *[v_sm[n] for n in SMALL])
    groups = [dict(zip(SMALL, res[1 + i * ns:1 + (i + 1) * ns])) for i in range(4)]
    return res[0], groups[0], groups[1], groups[2], groups[3]


BIG = ("w_in", "w_gate", "w_up", "w_down", "w_out", "wq_x", "wk_x", "wv_x", "wo_x")
BIG_KEY = dict(w_in="winT", w_gate="wgT", w_up="wuT", w_down="wd", w_out="wout", wq_x="wq", wk_x="wk",
               wv_x="wv", wo_x="wo")
TRANSPOSED = ("w_in", "w_gate", "w_up")
WEIGHTS = ("w_in", "sinks", "hgrn_lb", "hgrn_onorm", "w_out", "g_mix_pre", "g_mix_post", "g_mem", "g_x_pre",
           "g_x_post", "wq_x", "wk_x", "wv_x", "wo_x", "g_ffn_pre", "g_ffn_post", "w_gate", "w_up", "w_down")


def kernel(x, mem, w_in, sinks, hgrn_lb, hgrn_onorm, w_out, g_mix_pre, g_mix_post, g_mem, g_x_pre, g_x_post, wq_x, wk_x, wv_x, wo_x, g_ffn_pre, g_ffn_post, w_gate, w_up, w_down, loss_target, m_w_in, m_sinks, m_hgrn_lb, m_hgrn_onorm, m_w_out, m_g_mix_pre, m_g_mix_post, m_g_mem, m_g_x_pre, m_g_x_post, m_wq_x, m_wk_x, m_wv_x, m_wo_x, m_g_ffn_pre, m_g_ffn_post, m_w_gate, m_w_up, m_w_down, v_w_in, v_sinks, v_hgrn_lb, v_hgrn_onorm, v_w_out, v_g_mix_pre, v_g_mix_post, v_g_mem, v_g_x_pre, v_g_x_post, v_wq_x, v_wk_x, v_wv_x, v_wo_x, v_g_ffn_pre, v_g_ffn_post, v_w_gate, v_w_up, v_w_down):
    given = dict(locals())
    wts = {n: given[n] for n in WEIGHTS}
    ms = {n: given["m_" + n] for n in WEIGHTS}
    vs = {n: given["v_" + n] for n in WEIGHTS}

    def mat(a, name):
        a = a[0]
        return a.T if name in TRANSPOSED else a

    order = ("w_in", "w_out", "wq_x", "wk_x", "wv_x", "wo_x", "w_gate", "w_up", "w_down")
    flying = dict(zip(order, _gather_start(*_prepare_weights([mat(wts[n], n) for n in order]))))
    name_of = {k: n for n, k in BIG_KEY.items()}

    def fetch(key, after):
        return _gather_wait(*flying[name_of[key]], after, name="weights_recv_" + name_of[key])

    sm = {n: wts[n] for n in SMALL}
    started, held = {}, {}
    send_with = {"wgT": ("wgT", "wuT"), "wuT": ("wgT", "wuT"), "wq": ("wq", "wk", "wv"), "wk": ("wq", "wk", "wv"),
                 "wv": ("wq", "wk", "wv")}

    def emit(key, g):
        held[key] = g
        group = send_with.get(key, (key,))
        if key != group[-1]:
            return None
        flights = _exchange_start([held[k] for k in group], name="grad_send_" + name_of[group[0]])
        started.update({name_of[k]: f for k, f in zip(group, flights)})
        return flights[-1][2]

    grad_x, _, parts = _local_step(x[0], mem[0], loss_target[0], fetch, sm, emit)
    grads, deltas, new_m, new_v = {}, {}, {}, {}
    after = grad_x
    for n in ("w_down", "w_gate", "w_up", "wo_x", "wq_x", "wk_x", "wv_x", "w_out", "w_in"):
        g_all, land = _exchange_wait(*started[n], after, name="grad_recv_" + n)
        res = _sum_adamw(g_all, land, mat(wts[n], n), mat(ms[n], n), mat(vs[n], n), name="adamw_" + n)
        after = res[1]
        if n in TRANSPOSED:
            res = [a.T for a in res]
        grads[n], deltas[n], new_m[n], new_v[n] = [a[None] for a in res]
    loss, g_s, d_s, m_s, v_s = _small_update(_small_allreduce(parts), sm, {n: ms[n] for n in SMALL},
                                             {n: vs[n] for n in SMALL})
    grads.update(g_s), deltas.update(d_s), new_m.update(m_s), new_v.update(v_s)
    return (loss[0, 0], grad_x[None], *[grads[n] for n in WEIGHTS], *[deltas[n] for n in WEIGHTS],
            *[new_m[n] for n in WEIGHTS], *[new_v[n] for n in WEIGHTS])
```

```python
import functools

import jax
import jax.numpy as jnp
from jax import lax
from jax.experimental import pallas as pl
from jax.experimental.pallas import tpu as pltpu

F32 = jnp.float32
BF16 = jnp.bfloat16

D = 1024
D_IN = 2816
D_FF = 2816
CHUNK = 64
SWA_W = 512
KV_W = 128
HG_W = 512
HD = 128
ZQH, ZFH, ZIH, ZGH = 768, 1280, 1792, 2304
XH, XD = 4, 256
EPS = 1e-6
NEG = -1e30
N_DEV = 8
MESH = pl.DeviceIdType.MESH

LR, B1, B2, AEPS, WD, STEP = 0.001, 0.9, 0.999, 1e-08, 0.01, 10
C1 = 1.0 - B1 ** STEP
C2 = 1.0 - B2 ** STEP

VMEM_LIMIT = 56 * 1024 * 1024


def _params(**kw):
    return pltpu.CompilerParams(vmem_limit_bytes=VMEM_LIMIT, **kw)


def _sig(x):
    return 1.0 / (1.0 + jnp.exp(-x))


def _rowsum8(x):
    r, w = x.shape
    return jnp.sum(x.reshape(r // 8, 8, w), axis=0)


def _dot(a, b, ca, cb, precision=None):
    return lax.dot_general(a, b, (((ca,), (cb,)), ((), ())), preferred_element_type=F32,
                           precision=precision)


ANY_SPEC = pl.BlockSpec(memory_space=pl.ANY)


def _mm(a, b, *, ta=False, tb=False, out_dtype, tm, tn, tk=None, name, dep=None, n_outer=False):
    m = a.shape[1] if ta else a.shape[0]
    k = a.shape[0] if ta else a.shape[1]
    n = b.shape[0] if tb else b.shape[1]
    tm, tn = min(tm, m), min(tn, n)
    tk = k if tk is None else min(tk, k)
    nk = k // tk
    assert m % tm == 0 and n % tn == 0 and k % tk == 0, (name, m, n, k, tm, tn, tk)
    ij = (lambda g0, g1: (g1, g0)) if n_outer else (lambda g0, g1: (g0, g1))
    a_spec = (pl.BlockSpec((tk, tm), lambda g0, g1, kk: (kk, ij(g0, g1)[0])) if ta
              else pl.BlockSpec((tm, tk), lambda g0, g1, kk: (ij(g0, g1)[0], kk)))
    b_spec = (pl.BlockSpec((tn, tk), lambda g0, g1, kk: (ij(g0, g1)[1], kk)) if tb
              else pl.BlockSpec((tk, tn), lambda g0, g1, kk: (kk, ij(g0, g1)[1])))
    ca, cb = (0 if ta else 1), (1 if tb else 0)

    deps = [] if dep is None else [dep]

    def body(a_ref, b_ref, *rest):
        o_ref, acc = rest[len(deps)], rest[len(deps) + 1:]
        p = _dot(a_ref[...].astype(BF16), b_ref[...].astype(BF16), ca, cb)
        if nk == 1:
            o_ref[...] = p.astype(out_dtype)
        else:
            acc_ref, = acc
            kk = pl.program_id(2)

            @pl.when(kk == 0)
            def _():
                acc_ref[...] = p

            @pl.when(kk > 0)
            def _():
                acc_ref[...] += p

            @pl.when(kk == nk - 1)
            def _():
                o_ref[...] = acc_ref[...].astype(out_dtype)

    return pl.pallas_call(
        body, name=name, out_shape=jax.ShapeDtypeStruct((m, n), out_dtype),
        grid=(n // tn, m // tm, nk) if n_outer else (m // tm, n // tn, nk),
        in_specs=[a_spec, b_spec] + [ANY_SPEC] * len(deps),
        out_specs=pl.BlockSpec((tm, tn), lambda g0, g1, kk: ij(g0, g1)),
        scratch_shapes=[pltpu.VMEM((tm, tn), F32)] if nk > 1 else [],
        compiler_params=_params(dimension_semantics=("parallel", "parallel", "arbitrary")),
    )(a, b, *deps)


def _mm2(a1, b1, a2, b2, *, tb=False, out_dtype, tm, name, dep=None):
    m, k = a1.shape
    n = b1.shape[0] if tb else b1.shape[1]
    tm = min(tm, m)
    assert m % tm == 0
    cb = 1 if tb else 0
    deps = [] if dep is None else [dep]

    def body(a1_ref, b1_ref, a2_ref, b2_ref, *rest):
        o_ref = rest[len(deps)]
        o_ref[...] = (_dot(a1_ref[...].astype(BF16), b1_ref[...], 1, cb)
                      + _dot(a2_ref[...].astype(BF16), b2_ref[...], 1, cb)).astype(out_dtype)

    a_spec = pl.BlockSpec((tm, k), lambda i: (i, 0))
    b_spec = pl.BlockSpec(b1.shape, lambda i: (0, 0))
    return pl.pallas_call(
        body, name=name, out_shape=jax.ShapeDtypeStruct((m, n), out_dtype),
        grid=(m // tm,), in_specs=[a_spec, b_spec, a_spec, b_spec] + [ANY_SPEC] * len(deps),
        out_specs=pl.BlockSpec((tm, n), lambda i: (i, 0)),
        compiler_params=_params(dimension_semantics=("parallel",)),
    )(a1, b1, a2, b2, *deps)


def _rstd(x):
    return lax.rsqrt(jnp.mean(x * x, axis=-1, keepdims=True) + EPS)


def _norm_bwd(xh, r, t):
    return r * (t - xh * jnp.mean(xh * t, axis=-1, keepdims=True))


def _prenorm(x, g, *, name):
    t, d = x.shape
    tb = min(512, t)

    def body(x_ref, g_ref, o_ref):
        xf = x_ref[...]
        o_ref[...] = (xf * _rstd(xf) * g_ref[...]).astype(BF16)

    return pl.pallas_call(
        body, name=name, out_shape=jax.ShapeDtypeStruct((t, d), BF16), grid=(t // tb,),
        in_specs=[pl.BlockSpec((tb, d), lambda i: (i, 0)), pl.BlockSpec((1, d), lambda i: (0, 0))],
        out_specs=pl.BlockSpec((tb, d), lambda i: (i, 0)), compiler_params=_params(),
    )(x, g)


def _post_pre(h, y, g_post, g_pre, *, name):
    t, d = h.shape
    tb = min(512, t)

    def body(h_ref, y_ref, gp_ref, gn_ref, hn_ref, u_ref):
        y_ = y_ref[...].astype(F32)
        hn = h_ref[...] + y_ * _rstd(y_) * gp_ref[...]
        hn_ref[...] = hn
        u_ref[...] = (hn * _rstd(hn) * gn_ref[...]).astype(BF16)

    row = pl.BlockSpec((tb, d), lambda i: (i, 0))
    vec = pl.BlockSpec((1, d), lambda i: (0, 0))
    return pl.pallas_call(
        body, name=name, out_shape=(jax.ShapeDtypeStruct((t, d), F32), jax.ShapeDtypeStruct((t, d), BF16)),
        grid=(t // tb,), in_specs=[row, row, vec, vec], out_specs=(row, row), compiler_params=_params(),
    )(h, y, g_post, g_pre)


def _final_loss(h, y, g_post, target, *, name):
    t, d = h.shape
    tb = min(512, t)

    def body(h_ref, y_ref, g_ref, t_ref, sq_ref, dh_ref, dy_ref, dg_ref):
        @pl.when(pl.program_id(0) == 0)
        def _():
            sq_ref[...] = jnp.zeros_like(sq_ref)
            dg_ref[...] = jnp.zeros_like(dg_ref)

        y_ = y_ref[...].astype(F32)
        r = _rstd(y_)
        yh = y_ * r
        g = g_ref[...]
        err = h_ref[...] + yh * g - t_ref[...]
        sq_ref[...] += _rowsum8(err * err)
        dh = err * (1.0 / d)
        dh_ref[...] = dh
        dg_ref[...] += _rowsum8(dh * yh)
        dy_ref[...] = _norm_bwd(yh, r, dh * g).astype(BF16)

    row = pl.BlockSpec((tb, d), lambda i: (i, 0))
    vec = pl.BlockSpec((1, d), lambda i: (0, 0))
    acc = pl.BlockSpec((8, d), lambda i: (0, 0))
    return pl.pallas_call(
        body, name=name,
        out_shape=(jax.ShapeDtypeStruct((8, d), F32), jax.ShapeDtypeStruct((t, d), F32),
                   jax.ShapeDtypeStruct((t, d), BF16), jax.ShapeDtypeStruct((8, d), F32)),
        grid=(t // tb,), in_specs=[row, row, vec, row], out_specs=(acc, row, row, acc),
        compiler_params=_params(dimension_semantics=("arbitrary",)),
    )(h, y, g_post, target)


def _post_pre_bwd(dh_out, du, hn, y, g_post, g_pre, *, name):
    t, d = hn.shape
    tb = min(512, t)

    def body(dho_ref, du_ref, hn_ref, y_ref, gp_ref, gn_ref, dh_ref, dy_ref, dgn_ref, dgp_ref):
        @pl.when(pl.program_id(0) == 0)
        def _():
            dgn_ref[...] = jnp.zeros_like(dgn_ref)
            dgp_ref[...] = jnp.zeros_like(dgp_ref)

        hn_ = hn_ref[...]
        r2 = _rstd(hn_)
        xh = hn_ * r2
        du_ = du_ref[...].astype(F32)
        dgn_ref[...] += _rowsum8(du_ * xh)
        dh = dho_ref[...] + _norm_bwd(xh, r2, du_ * gn_ref[...])
        dh_ref[...] = dh
        y_ = y_ref[...].astype(F32)
        r1 = _rstd(y_)
        yh = y_ * r1
        dgp_ref[...] += _rowsum8(dh * yh)
        dy_ref[...] = _norm_bwd(yh, r1, dh * gp_ref[...]).astype(BF16)

    row = pl.BlockSpec((tb, d), lambda i: (i, 0))
    vec = pl.BlockSpec((1, d), lambda i: (0, 0))
    acc = pl.BlockSpec((8, d), lambda i: (0, 0))
    return pl.pallas_call(
        body, name=name,
        out_shape=(jax.ShapeDtypeStruct((t, d), F32), jax.ShapeDtypeStruct((t, d), BF16),
                   jax.ShapeDtypeStruct((8, d), F32), jax.ShapeDtypeStruct((8, d), F32)),
        grid=(t // tb,), in_specs=[row, row, row, row, vec, vec], out_specs=(row, row, acc, acc),
        compiler_params=_params(dimension_semantics=("arbitrary",)),
    )(dh_out, du, hn, y, g_post, g_pre)


def _pre_bwd(dh_out, du, x, g, *, name):
    t, d = x.shape
    tb = min(512, t)
    has_res = dh_out is not None

    def body(*refs):
        if has_res:
            dho_ref, du_ref, x_ref, g_ref, dx_ref, dg_ref = refs
        else:
            du_ref, x_ref, g_ref, dx_ref, dg_ref = refs

        @pl.when(pl.program_id(0) == 0)
        def _():
            dg_ref[...] = jnp.zeros_like(dg_ref)

        x_ = x_ref[...]
        r = _rstd(x_)
        xh = x_ * r
        du_ = du_ref[...].astype(F32)
        dg_ref[...] += _rowsum8(du_ * xh)
        dx = _norm_bwd(xh, r, du_ * g_ref[...])
        if has_res:
            dx = dx + dho_ref[...]
        dx_ref[...] = dx

    row = pl.BlockSpec((tb, d), lambda i: (i, 0))
    vec = pl.BlockSpec((1, d), lambda i: (0, 0))
    acc = pl.BlockSpec((8, d), lambda i: (0, 0))
    ins = ([dh_out] if has_res else []) + [du, x, g]
    return pl.pallas_call(
        body, name=name,
        out_shape=(jax.ShapeDtypeStruct((t, d), F32), jax.ShapeDtypeStruct((8, d), F32)),
        grid=(t // tb,), in_specs=[row] * (len(ins) - 1) + [vec], out_specs=(row, acc),
        compiler_params=_params(dimension_semantics=("arbitrary",)),
    )(*ins)


QB = 256


def _half_mask(shape, e):
    lane = lax.broadcasted_iota(jnp.int32, shape, len(shape) - 1)
    return (lane // 64) == e


def _place(kv):
    sw = pltpu.roll(kv, 64, 1)
    m0 = _half_mask(kv.shape, 0)
    return [[jnp.where(m0, kv, 0.0).astype(BF16), jnp.where(m0, 0.0, sw).astype(BF16)],
            [jnp.where(m0, sw, 0.0).astype(BF16), jnp.where(m0, 0.0, kv).astype(BF16)]]


def _swa_valid_q(i, nq, nk):
    qc = lax.broadcasted_iota(jnp.int32, (nq, nk), 0) // CHUNK
    kc = lax.broadcasted_iota(jnp.int32, (nq, nk), 1) // CHUNK - 2
    return (kc <= qc) & (qc <= kc + 2) & (4 * i + kc >= 0)


def _swa_fwd(z, sinks, t):
    nb = t // QB

    def body(s_ref, q_ref, kp_ref, kc_ref, vp_ref, vc_ref, o_ref, lse_ref):
        i = pl.program_id(0)
        kpl = _place(jnp.concatenate([kp_ref[...], kc_ref[...]], axis=0))
        vpl = _place(jnp.concatenate([vp_ref[...], vc_ref[...]], axis=0))
        valid = _swa_valid_q(i, QB, QB + 128)
        lane = lax.broadcasted_iota(jnp.int32, (QB, 128), 1)
        lse_out = jnp.zeros((QB, 128), F32)
        for j in range(4):
            qp = q_ref[:, 128 * j:128 * (j + 1)].astype(BF16)
            acc = jnp.zeros((QB, 128), F32)
            for e in range(2):
                h = 2 * j + e
                kvh = h // 4
                qm = jnp.where(_half_mask(qp.shape, e), qp, jnp.zeros_like(qp))
                s = _dot(qm, kpl[kvh][e], 1, 1) * 0.125
                s = jnp.where(valid, s, NEG)
                sink = s_ref[0, h]
                m = jnp.maximum(jnp.max(s, axis=-1, keepdims=True), sink)
                p = jnp.exp(s - m)
                l = jnp.sum(p, axis=-1, keepdims=True) + jnp.exp(sink - m)
                acc = acc + _dot(p.astype(BF16), vpl[kvh][e], 1, 0) * (1.0 / l)
                lse_out = jnp.where(lane == h, m + jnp.log(l), lse_out)
            o_ref[:, 128 * j:128 * (j + 1)] = acc.astype(BF16)
        lse_ref[...] = lse_out

    prev = lambda c: pl.BlockSpec((128, 128), lambda i: (jnp.maximum(2 * i - 1, 0), c))
    cur = lambda c: pl.BlockSpec((QB, 128), lambda i: (i, c))
    return pl.pallas_call(
        body, name="swa_fwd",
        out_shape=(jax.ShapeDtypeStruct((t, D), BF16), jax.ShapeDtypeStruct((t, 128), F32)),
        grid=(nb,),
        in_specs=[pl.BlockSpec(memory_space=pltpu.SMEM),
                  pl.BlockSpec((QB, SWA_W), lambda i: (i, 0)), prev(4), cur(4), prev(5), cur(5)],
        out_specs=(pl.BlockSpec((QB, SWA_W), lambda i: (i, 0)), pl.BlockSpec((QB, 128), lambda i: (i, 0))),
        compiler_params=_params(),
    )(sinks, z, z, z, z, z)


def _swa_bwd(z, sinks, ymix, lse, dymix, t):
    nb = t // QB
    nk = QB + 128

    def body(s_ref, q_ref, kp_ref, kc_ref, vp_ref, vc_ref, o_ref, do_ref, l_ref,
             dq_ref, first_ref, second_ref, ds_ref, carry_ref):
        i = pl.program_id(0)
        live = i < nb

        @pl.when(i == 0)
        def _():
            ds_ref[...] = jnp.zeros_like(ds_ref)
            carry_ref[...] = jnp.zeros_like(carry_ref)

        lane = lax.broadcasted_iota(jnp.int32, (8, 128), 1)
        kpl = _place(jnp.concatenate([kp_ref[...], kc_ref[...]], axis=0))
        vpl = _place(jnp.concatenate([vp_ref[...], vc_ref[...]], axis=0))
        valid = _swa_valid_q(i, QB, nk) & live
        lse_c = l_ref[...]
        dsink = jnp.zeros((8, 128), F32)
        dk_acc = [[jnp.zeros((nk, 128), F32) for _ in range(2)] for _ in range(2)]
        dv_acc = [[jnp.zeros((nk, 128), F32) for _ in range(2)] for _ in range(2)]
        dq = []
        for j in range(4):
            cols = slice(128 * j, 128 * (j + 1))
            qp = q_ref[:, cols].astype(BF16)
            dop = do_ref[:, cols]
            prod = dop.astype(F32) * o_ref[:, cols].astype(F32)
            acc = jnp.zeros((QB, 128), F32)
            for e in range(2):
                h = 2 * j + e
                kvh = h // 4
                hm = _half_mask(qp.shape, e)
                qm = jnp.where(hm, qp, jnp.zeros_like(qp))
                dom = jnp.where(hm, dop, jnp.zeros_like(dop))
                dd = jnp.sum(jnp.where(hm, prod, 0.0), axis=-1, keepdims=True)
                lse_h = lse_c[:, h:h + 1]
                s = _dot(qm, kpl[kvh][e], 1, 1) * 0.125
                p = jnp.where(valid, jnp.exp(s - lse_h), 0.0)
                dp = _dot(dom, vpl[kvh][e], 1, 1)
                ds = (p * (dp - dd) * 0.125).astype(BF16)
                acc = acc + _dot(ds, kpl[kvh][e], 1, 0)
                dk_acc[kvh][e] = dk_acc[kvh][e] + _dot(ds, qm, 0, 0)
                dv_acc[kvh][e] = dv_acc[kvh][e] + _dot(p.astype(BF16), dom, 0, 0)
                ps = jnp.where(live, jnp.exp(s_ref[0, h] - lse_h) * dd, 0.0)
                dsink = dsink - jnp.where(lane == h, _rowsum8(jnp.broadcast_to(ps, (QB, 128))), 0.0)
            dq.append(acc.astype(BF16))
        ds_ref[...] += dsink
        dk = dk_acc[0][0] + dk_acc[1][1] + pltpu.roll(dk_acc[0][1] + dk_acc[1][0], 64, 1)
        dv = dv_acc[0][0] + dv_acc[1][1] + pltpu.roll(dv_acc[0][1] + dv_acc[1][0], 64, 1)
        dkv = jnp.concatenate([dk, dv], axis=1)
        second_ref[...] = (carry_ref[...] + dkv[0:128]).astype(BF16)
        carry_ref[...] = dkv[256:384]

        @pl.when(live)
        def _():
            for j in range(4):
                dq_ref[:, 128 * j:128 * (j + 1)] = dq[j]
            first_ref[...] = dkv[128:256].astype(BF16)

    blk = lambda i: jnp.minimum(i, nb - 1)
    prev = lambda c: pl.BlockSpec((128, 128), lambda i: (jnp.maximum(2 * blk(i) - 1, 0), c))
    cur = lambda w, c: pl.BlockSpec((QB, w), lambda i: (blk(i), c))
    half = lambda index: pl.BlockSpec((128, 256), lambda i: (index(i), 0))
    return pl.pallas_call(
        body, name="swa_bwd",
        out_shape=(jax.ShapeDtypeStruct((t, SWA_W), BF16), jax.ShapeDtypeStruct((t // 2, 256), BF16),
                   jax.ShapeDtypeStruct((t // 2, 256), BF16), jax.ShapeDtypeStruct((8, 128), F32)),
        grid=(nb + 1,),
        in_specs=[pl.BlockSpec(memory_space=pltpu.SMEM),
                  cur(SWA_W, 0), prev(4), cur(128, 4), prev(5), cur(128, 5),
                  cur(SWA_W, 0), cur(SWA_W, 0), cur(128, 0)],
        out_specs=(cur(SWA_W, 0), half(blk), half(lambda i: jnp.maximum(i - 1, 0)),
                   pl.BlockSpec((8, 128), lambda i: (0, 0))),
        scratch_shapes=[pltpu.VMEM((128, 256), F32)],
        compiler_params=_params(dimension_semantics=("arbitrary",)),
    )(sinks, z, z, z, z, z, ymix, dymix, lse)


HB = 256


def _lower_bound(lb_ref):
    a = lb_ref[...]
    a0, a1 = a[0:1], a[1:2]
    mx = jnp.maximum(a0, a1)
    e0, e1 = jnp.exp(a0 - mx), jnp.exp(a1 - mx)
    return e0 / (e0 + e1)


def _hgrn_cols(row_block):
    return [pl.BlockSpec((HB, 2 * HD), lambda j, c=base // (2 * HD) + p: (row_block(j), c))
            for base in (ZQH, ZFH, ZIH, ZGH) for p in range(2)]


NCH = HB // CHUNK


def _split3(x):
    hi = x.astype(BF16)
    r1 = x - hi.astype(F32)
    mid = r1.astype(BF16)
    return hi, mid, (r1 - mid.astype(F32)).astype(BF16)


def _blockdiag(lower):
    r = lax.broadcasted_iota(jnp.int32, (HB, HB), 0)
    c = lax.broadcasted_iota(jnp.int32, (HB, HB), 1)
    return (r // CHUNK == c // CHUNK) & ((c <= r) if lower else (c >= r))


def _chunk_sums(mask_bf16, x):
    return sum(_dot(mask_bf16, part, 1, 0) for part in _split3(x))


def _per_chunk_rows(x, row):
    w = x.shape[1]
    picked = x.reshape(NCH, CHUNK, w)[:, row:row + 1, :]
    return jnp.broadcast_to(picked, (NCH, CHUNK, w)).reshape(HB, w)


def _chunk_stack(x, chunk_of_row):
    return jnp.concatenate([jnp.where(chunk_of_row == c, x, jnp.zeros_like(x)) for c in range(NCH)], axis=1)


def _chunk_pick(x, chunk_of_row):
    w = x.shape[1] // NCH
    out = jnp.zeros((HB, w), x.dtype)
    for c in range(NCH):
        out = jnp.where(chunk_of_row == c, x[:, c * w:(c + 1) * w], out)
    return out


def _hgrn_local(q, f, kf, b):
    sq = _sig(q)
    qf = q * sq * (HD ** -0.5)
    b_mid = _per_chunk_rows(b, CHUNK // 2 - 1)
    b_last = _per_chunk_rows(b, CHUNK - 1)
    qm = qf * jnp.exp(b - b_mid)
    km = kf * jnp.exp(b_mid - b)
    kl = kf * jnp.exp(b_last - b)
    qb = qf * jnp.exp(b)
    return dict(sq=sq, b_mid=b_mid, b_last=b_last, qm=qm, km=km, kl=kl, qb=qb)


def _hgrn2_fwd(z, hgrn_lb, onorm, ymix, t):
    nb = t // HB

    def body(*refs):
        zq, zf, zi, zg = refs[0:2], refs[2:4], refs[4:6], refs[6:8]
        lb_ref, on_ref, _, y_ref, o_ref, sp_ref, st_ref = refs[8:]

        @pl.when(pl.program_id(0) == 0)
        def _():
            st_ref[...] = jnp.zeros_like(st_ref)

        lb_all = _lower_bound(lb_ref)
        gn = on_ref[...]
        low = _blockdiag(True)
        low_b = low.astype(BF16)
        chunk_of_row = lax.broadcasted_iota(jnp.int32, (HB, HD), 0) // CHUNK
        for p in range(2):
            lbp = lb_all[:, 2 * HD * p:2 * HD * (p + 1)]
            fp = lbp + (1.0 - lbp) * _sig(zf[p][...])
            bp = _chunk_sums(low_b, jnp.log(fp))
            for e in range(2):
                h, ls = 2 * p + e, slice(e * HD, (e + 1) * HD)
                f = fp[:, ls]
                w = _hgrn_local(zq[p][:, ls], f, 1.0 - f, bp[:, ls])
                iv = zi[p][:, ls].astype(BF16)
                a = jnp.where(low, _dot(w["qm"].astype(BF16), w["km"].astype(BF16), 1, 1), 0.0)
                o = _dot(a.astype(BF16), iv, 1, 0)
                u = _dot(iv, _chunk_stack(w["kl"].astype(BF16), chunk_of_row), 0, 0)
                decay = jnp.exp(w["b_last"])
                st = st_ref[h]
                states = []
                for c in range(NCH):
                    sp_ref[h, c] = st
                    states.append(st.astype(BF16))
                    st = st * decay[c * CHUNK:c * CHUNK + 1] + u[:, c * HD:(c + 1) * HD]
                st_ref[h] = st
                inter = _dot(w["qb"].astype(BF16), jnp.concatenate(states, axis=0), 1, 1)
                o = o + _chunk_pick(inter, chunk_of_row)
                hs = slice(h * HD, (h + 1) * HD)
                o_ref[:, hs] = o
                gg = zg[p][:, ls]
                y_ref[:, hs] = (o * _rstd(o) * gn * (gg * _sig(gg))).astype(BF16)

    return pl.pallas_call(
        body, name="hgrn_fwd",
        out_shape=(jax.ShapeDtypeStruct((t, D), BF16), jax.ShapeDtypeStruct((t, HG_W), F32),
                   jax.ShapeDtypeStruct((4, t // CHUNK, HD, HD), F32)),
        grid=(nb,),
        in_specs=_hgrn_cols(lambda j: j) + [pl.BlockSpec((2, HG_W), lambda j: (0, 0)),
                                            pl.BlockSpec((1, HD), lambda j: (0, 0)), ANY_SPEC],
        out_specs=(pl.BlockSpec((HB, HG_W), lambda j: (j, 1)),
                   pl.BlockSpec((HB, HG_W), lambda j: (j, 0)),
                   pl.BlockSpec((4, NCH, HD, HD), lambda j: (0, j, 0, 0))),
        scratch_shapes=[pltpu.VMEM((4, HD, HD), F32)],
        input_output_aliases={10: 0},
        compiler_params=_params(dimension_semantics=("arbitrary",)),
    )(*[z] * 8, hgrn_lb, onorm, ymix)


def _hgrn2_bwd(z, hgrn_lb, onorm, o_save, sprev, dymix, dza, t):
    nb = t // HB

    def body(*refs):
        zq, zf, zi, zg = refs[0:2], refs[2:4], refs[4:6], refs[6:8]
        (lb_ref, on_ref, o_ref, sp_ref, dy_ref, dqa_ref, first_ref, second_ref,
         dz_ref, dlb_ref, don_ref, dst_ref) = refs[8:]

        @pl.when(pl.program_id(0) == 0)
        def _():
            dst_ref[...] = jnp.zeros_like(dst_ref)
            dlb_ref[...] = jnp.zeros_like(dlb_ref)
            don_ref[...] = jnp.zeros_like(don_ref)

        dz_ref[:, 0:SWA_W] = dqa_ref[...]
        dz_ref[0:HB // 2, SWA_W:ZQH] = first_ref[...]
        dz_ref[HB // 2:HB, SWA_W:ZQH] = second_ref[...]
        lb_all = _lower_bound(lb_ref)
        gn = on_ref[...]
        low, upp = _blockdiag(True), _blockdiag(False)
        upp_b = upp.astype(BF16)
        low_b = low.astype(BF16)
        row = lax.broadcasted_iota(jnp.int32, (HB, HD), 0)
        chunk_of_row = row // CHUNK
        in_chunk = row % CHUNK
        for p in range(2):
            lbp = lb_all[:, 2 * HD * p:2 * HD * (p + 1)]
            sgp = _sig(zf[p][...])
            fp = lbp + (1.0 - lbp) * sgp
            bp = _chunk_sums(low_b, jnp.log(fp))
            db_pair, dkf_pair = [], []
            for e in range(2):
                h, ls, hs = 2 * p + e, slice(e * HD, (e + 1) * HD), slice((2 * p + e) * HD, (2 * p + e + 1) * HD)
                f = fp[:, ls]
                q = zq[p][:, ls]
                w = _hgrn_local(q, f, 1.0 - f, bp[:, ls])
                iv = zi[p][:, ls].astype(BF16)
                gg = zg[p][:, ls]
                o = o_ref[:, hs]
                dout = dy_ref[:, hs].astype(F32)
                sgg = _sig(gg)
                r = _rstd(o)
                oh = o * r
                dyn = dout * (gg * sgg)
                dz_ref[:, ZGH + h * HD:ZGH + (h + 1) * HD] = (
                    dout * oh * gn * (sgg * (1.0 + gg * (1.0 - sgg)))).astype(BF16)
                don_ref[...] += _rowsum8(dyn * oh)
                do = _norm_bwd(oh, r, dyn * gn).astype(BF16)
                qm, km, kl, qb = (w[n].astype(BF16) for n in ("qm", "km", "kl", "qb"))
                decay = jnp.exp(w["b_last"])
                grads_in = _dot(do, _chunk_stack(qb, chunk_of_row), 0, 0)
                dst = dst_ref[h]
                dstn, dd_rows = [None] * NCH, [None] * NCH
                for c in reversed(range(NCH)):
                    dstn[c] = dst.astype(BF16)
                    dd_rows[c] = jnp.sum(dst * sp_ref[h, c], axis=0, keepdims=True)
                    dst = dst * decay[c * CHUNK:c * CHUNK + 1] + grads_in[:, c * HD:(c + 1) * HD]
                dst_ref[h] = dst
                states = jnp.concatenate([sp_ref[h, c].astype(BF16) for c in range(NCH)], axis=0)
                dstn_all = jnp.concatenate(dstn, axis=0)
                dqb = _dot(_chunk_stack(do, chunk_of_row), states, 1, 0)
                at = jnp.where(upp, _dot(km, qm, 1, 1), 0.0)
                di = _dot(at.astype(BF16), do, 1, 0) + _chunk_pick(_dot(kl, dstn_all, 1, 1), chunk_of_row)
                dz_ref[:, ZIH + h * HD:ZIH + (h + 1) * HD] = di.astype(BF16)
                dkl = _dot(_chunk_stack(iv, chunk_of_row), dstn_all, 1, 0)
                da = jnp.where(low, _dot(do, iv, 1, 1), 0.0).astype(BF16)
                dat = jnp.where(upp, _dot(iv, do, 1, 1), 0.0).astype(BF16)
                dqm = _dot(da, km, 1, 0)
                dkm = _dot(dat, qm, 1, 0)
                b = bp[:, ls]
                e1, e2 = jnp.exp(b - w["b_mid"]), jnp.exp(w["b_mid"] - b)
                e3, e4 = jnp.exp(w["b_last"] - b), jnp.exp(b)
                dqf = dqm * e1 + dqb * e4
                dkf_pair.append(dkm * e2 + dkl * e3)
                t_qm, t_km, t_kl = dqm * w["qm"], dkm * w["km"], dkl * w["kl"]
                db = t_qm - t_km - t_kl + dqb * w["qb"]
                db_mid = jnp.sum((t_km - t_qm).reshape(NCH, CHUNK, HD), axis=1, keepdims=True)
                db_last = jnp.sum(t_kl.reshape(NCH, CHUNK, HD), axis=1, keepdims=True)
                db_last = db_last + jnp.stack(dd_rows, axis=0) * jnp.exp(
                    bp[:, ls].reshape(NCH, CHUNK, HD)[:, CHUNK - 1:CHUNK, :])
                spread = lambda v: jnp.broadcast_to(v, (NCH, CHUNK, HD)).reshape(HB, HD)
                db = (db + jnp.where(in_chunk == CHUNK // 2 - 1, spread(db_mid), 0.0)
                      + jnp.where(in_chunk == CHUNK - 1, spread(db_last), 0.0))
                db_pair.append(db)
                sq = w["sq"]
                dz_ref[:, ZQH + h * HD:ZQH + (h + 1) * HD] = (
                    dqf * (HD ** -0.5) * (sq * (1.0 + q * (1.0 - sq)))).astype(BF16)
            dlogf = _chunk_sums(upp_b, jnp.concatenate(db_pair, axis=1))
            dfv = dlogf / fp - jnp.concatenate(dkf_pair, axis=1)
            dz_ref[:, ZFH + 2 * HD * p:ZFH + 2 * HD * (p + 1)] = (dfv * (1.0 - lbp) * sgp * (1.0 - sgp)).astype(BF16)
            dlb_ref[:, 2 * HD * p:2 * HD * (p + 1)] += _rowsum8(dfv * (1.0 - sgp))

    rev = lambda j: nb - 1 - j
    return pl.pallas_call(
        body, name="hgrn_bwd",
        out_shape=(jax.ShapeDtypeStruct((t, D_IN), BF16), jax.ShapeDtypeStruct((8, HG_W), F32),
                   jax.ShapeDtypeStruct((8, HD), F32)),
        grid=(nb,),
        in_specs=_hgrn_cols(rev) + [pl.BlockSpec((2, HG_W), lambda j: (0, 0)), pl.BlockSpec((1, HD), lambda j: (0, 0)),
                                    pl.BlockSpec((HB, HG_W), lambda j: (rev(j), 0)),
                                    pl.BlockSpec((4, NCH, HD, HD), lambda j: (0, rev(j), 0, 0)),
                                    pl.BlockSpec((HB, HG_W), lambda j: (rev(j), 1)),
                                    pl.BlockSpec((HB, SWA_W), lambda j: (rev(j), 0)),
                                    pl.BlockSpec((HB // 2, 2 * KV_W), lambda j: (rev(j), 0)),
                                    pl.BlockSpec((HB // 2, 2 * KV_W), lambda j: (rev(j), 0))],
        out_specs=(pl.BlockSpec((HB, D_IN), lambda j: (rev(j), 0)), pl.BlockSpec((8, HG_W), lambda j: (0, 0)),
                   pl.BlockSpec((8, HD), lambda j: (0, 0))),
        scratch_shapes=[pltpu.VMEM((4, HD, HD), F32)],
        compiler_params=_params(dimension_semantics=("arbitrary",)),
    )(*[z] * 8, hgrn_lb, onorm, o_save, sprev, dymix, *dza)


XB = 512


def _xattn_fwd(q, k, v, t):
    tb = min(XB, t)

    def body(q_ref, k_ref, v_ref, o_ref):
        for h in range(XH):
            cols = slice(XD * h, XD * (h + 1))
            s = _dot(q_ref[:, cols], k_ref[:, cols], 1, 1) * (XD ** -0.5)
            p = jnp.exp(s - jnp.max(s, axis=-1, keepdims=True))
            l = jnp.sum(p, axis=-1, keepdims=True)
            o_ref[:, cols] = (_dot(p.astype(BF16), v_ref[:, cols], 1, 0) * (1.0 / l)).astype(BF16)

    row = pl.BlockSpec((tb, D), lambda i: (i, 0))
    mem = pl.BlockSpec(k.shape, lambda i: (0, 0))
    return pl.pallas_call(
        body, name="xattn_fwd", out_shape=jax.ShapeDtypeStruct((t, D), BF16), grid=(t // tb,),
        in_specs=[row, mem, mem], out_specs=row, compiler_params=_params(),
    )(q, k, v)


def _xattn_bwd(q, k, v, do, t):
    tb = min(XB, t)

    def body(q_ref, k_ref, v_ref, do_ref, dq_ref, dk_ref, dv_ref):
        @pl.when(pl.program_id(0) == 0)
        def _():
            dk_ref[...] = jnp.zeros_like(dk_ref)
            dv_ref[...] = jnp.zeros_like(dv_ref)

        for h in range(XH):
            cols = slice(XD * h, XD * (h + 1))
            qh, kh, vh, doh = q_ref[:, cols], k_ref[:, cols], v_ref[:, cols], do_ref[:, cols]
            s = _dot(qh, kh, 1, 1) * (XD ** -0.5)
            p = jnp.exp(s - jnp.max(s, axis=-1, keepdims=True))
            p = p * (1.0 / jnp.sum(p, axis=-1, keepdims=True))
            dp = _dot(doh, vh, 1, 1)
            ds = (p * (dp - jnp.sum(p * dp, axis=-1, keepdims=True)) * (XD ** -0.5)).astype(BF16)
            dq_ref[:, cols] = _dot(ds, kh, 1, 0).astype(BF16)
            dk_ref[:, cols] += _dot(ds, qh, 0, 0)
            dv_ref[:, cols] += _dot(p.astype(BF16), doh, 0, 0)

    row = pl.BlockSpec((tb, D), lambda i: (i, 0))
    mem = pl.BlockSpec(k.shape, lambda i: (0, 0))
    return pl.pallas_call(
        body, name="xattn_bwd",
        out_shape=(jax.ShapeDtypeStruct((t, D), BF16), jax.ShapeDtypeStruct(k.shape, F32),
                   jax.ShapeDtypeStruct(k.shape, F32)),
        grid=(t // tb,), in_specs=[row, mem, mem, row], out_specs=(row, mem, mem),
        compiler_params=_params(dimension_semantics=("arbitrary",)),
    )(q, k, v, do)


def _mem_gain_bwd(dm, mem, *, name):
    def body(dm_ref, m_ref, dg_ref):
        m_ = m_ref[...]
        dg_ref[...] = _rowsum8(dm_ref[...] * (m_ * _rstd(m_)))

    return pl.pallas_call(body, name=name, out_shape=jax.ShapeDtypeStruct((8, D), F32),
                          compiler_params=_params())(dm, mem)


FM, FN = 512, 1408


def _ffn_up(u, wgt, wut, t):
    tm = min(FM, t)

    def body(u_ref, wg_ref, wu_ref, g_ref, up_ref, a_ref):
        u_ = u_ref[...]
        g = _dot(u_, wg_ref[...], 1, 1)
        up = _dot(u_, wu_ref[...], 1, 1)
        g_ref[...] = g.astype(BF16)
        up_ref[...] = up.astype(BF16)
        a_ref[...] = (g * _sig(g) * up).astype(BF16)

    w = pl.BlockSpec((FN, D), lambda j, i: (j, 0))
    o = pl.BlockSpec((tm, FN), lambda j, i: (i, j))
    return pl.pallas_call(
        body, name="ffn_up", out_shape=(jax.ShapeDtypeStruct((t, D_FF), BF16),) * 3,
        grid=(D_FF // FN, t // tm), in_specs=[pl.BlockSpec((tm, D), lambda j, i: (i, 0)), w, w],
        out_specs=(o, o, o), compiler_params=_params(),
    )(u, wgt, wut)


def _ffn_down_bwd(dy, wd, gate, up, t, dep=None):
    tm = min(FM, t)
    deps = [] if dep is None else [dep]

    def body(dy_ref, w_ref, g_ref, up_ref, *rest):
        dg_ref, dup_ref = rest[len(deps):]
        da = _dot(dy_ref[...], w_ref[...], 1, 1)
        g = g_ref[...].astype(F32)
        sg = _sig(g)
        dup_ref[...] = (da * g * sg).astype(BF16)
        dg_ref[...] = (da * up_ref[...].astype(F32) * (sg * (1.0 + g * (1.0 - sg)))).astype(BF16)

    o = pl.BlockSpec((tm, FN), lambda j, i: (i, j))
    return pl.pallas_call(
        body, name="ffn_down_bwd", out_shape=(jax.ShapeDtypeStruct((t, D_FF), BF16),) * 2,
        grid=(D_FF // FN, t // tm),
        in_specs=[pl.BlockSpec((tm, D), lambda j, i: (i, 0)), pl.BlockSpec((FN, D), lambda j, i: (j, 0)), o, o]
        + [ANY_SPEC] * len(deps),
        out_specs=(o, o), compiler_params=_params(),
    )(dy, wd, gate, up, *deps)


def _local_step(x, mem, target, fetch, sm, emit=None):
    t = x.shape[0]
    w, gw = {}, {}

    def out(key, g):
        gw[key] = g
        return None if emit is None else emit(key, g)
    u1 = _prenorm(x, sm["g_mix_pre"], name="prenorm_mix")
    w["winT"] = fetch("winT", u1)
    z = _mm(u1, w["winT"], tb=True, out_dtype=F32, tm=1024, tn=1408, name="mm_z", n_outer=True)
    ymix, lse = _swa_fwd(z, sm["sinks"], t)
    ymix, o_h, sprev = _hgrn2_fwd(z, sm["hgrn_lb"], sm["hgrn_onorm"], ymix, t)
    w["wout"] = fetch("wout", ymix)
    y1 = _mm(ymix, w["wout"], out_dtype=BF16, tm=1024, tn=1024, name="mm_y1")
    h1, u2 = _post_pre(x, y1, sm["g_mix_post"], sm["g_x_pre"], name="post_mix")
    mn = _prenorm(mem, sm["g_mem"], name="prenorm_mem")
    for key in ("wq", "wk", "wv"):
        w[key] = fetch(key, u2)
    qx = _mm(u2, w["wq"], out_dtype=BF16, tm=1024, tn=1024, name="mm_qx")
    kx = _mm(mn, w["wk"], out_dtype=BF16, tm=1024, tn=1024, name="mm_kx")
    vx = _mm(mn, w["wv"], out_dtype=BF16, tm=1024, tn=1024, name="mm_vx")
    ox = _xattn_fwd(qx, kx, vx, t)
    w["wo"] = fetch("wo", ox)
    y2 = _mm(ox, w["wo"], out_dtype=BF16, tm=1024, tn=1024, name="mm_y2")
    h2, u3 = _post_pre(h1, y2, sm["g_x_post"], sm["g_ffn_pre"], name="post_x")
    w["wgT"], w["wuT"] = fetch("wgT", u3), fetch("wuT", u3)
    gate, up, act = _ffn_up(u3, w["wgT"], w["wuT"], t)
    w["wd"] = fetch("wd", act)
    y3 = _mm(act, w["wd"], out_dtype=BF16, tm=1024, tn=1024, name="mm_y3")
    sq, dh3, dy3, dg_ffn_post = _final_loss(h2, y3, sm["g_ffn_post"], target, name="final_loss")
    dep = out("wd", _mm(act, dy3, ta=True, out_dtype=BF16, tm=1408, tn=1024, name="mm_gwd"))
    dgate, dup = _ffn_down_bwd(dy3, w["wd"], gate, up, t, dep=dep)
    dep = out("wgT", _mm(dgate, u3, ta=True, out_dtype=BF16, tm=1408, tn=1024, name="mm_gwg"))
    dep = out("wuT", _mm(dup, u3, ta=True, out_dtype=BF16, tm=1408, tn=1024, name="mm_gwu", dep=dep))
    du3 = _mm2(dgate, w["wgT"], dup, w["wuT"], out_dtype=BF16, tm=512, name="mm_du3", dep=dep)
    dh2, dy2, dg_ffn_pre, dg_x_post = _post_pre_bwd(dh3, du3, h2, y2, sm["g_x_post"], sm["g_ffn_pre"], name="post_x_bwd")
    dep = out("wo", _mm(ox, dy2, ta=True, out_dtype=BF16, tm=512, tn=1024, name="mm_gwo"))
    dox = _mm(dy2, w["wo"], tb=True, out_dtype=BF16, tm=1024, tn=1024, name="mm_dox", dep=dep)
    dqx, dkx, dvx = _xattn_bwd(qx, kx, vx, dox, t)
    dep = out("wq", _mm(u2, dqx, ta=True, out_dtype=BF16, tm=512, tn=1024, name="mm_gwq"))
    dep = out("wk", _mm(mn, dkx, ta=True, out_dtype=BF16, tm=1024, tn=1024, name="mm_gwk", dep=dep))
    dep = out("wv", _mm(mn, dvx, ta=True, out_dtype=BF16, tm=1024, tn=1024, name="mm_gwv", dep=dep))
    du2 = _mm(dqx, w["wq"], tb=True, out_dtype=BF16, tm=1024, tn=1024, name="mm_du2", dep=dep)
    dmn = _mm2(dkx, w["wk"], dvx, w["wv"], tb=True, out_dtype=F32, tm=256, name="mm_dmn")
    dg_mem = _mem_gain_bwd(dmn, mem, name="mem_gain_bwd")
    dh1, dy1, dg_x_pre, dg_mix_post = _post_pre_bwd(dh2, du2, h1, y1, sm["g_mix_post"], sm["g_x_pre"], name="post_mix_bwd")
    dep = out("wout", _mm(ymix, dy1, ta=True, out_dtype=BF16, tm=512, tn=1024, name="mm_gwout"))
    dymix = _mm(dy1, w["wout"], tb=True, out_dtype=BF16, tm=1024, tn=1024, name="mm_dymix", dep=dep)
    *dza, dsinks = _swa_bwd(z, sm["sinks"], ymix, lse, dymix, t)
    dz, dlb, donorm = _hgrn2_bwd(z, sm["hgrn_lb"], sm["hgrn_onorm"], o_h, sprev, dymix, dza, t)
    dep = out("winT", _mm(dz, u1, ta=True, out_dtype=BF16, tm=1408, tn=1024, name="mm_gwin"))
    du1 = _mm(dz, w["winT"], out_dtype=BF16, tm=512, tn=1024, name="mm_du1", dep=dep)
    grad_x, dg_mix_pre = _pre_bwd(dh1, du1, x, sm["g_mix_pre"], name="pre_mix_bwd")
    parts = dict(g_mix_pre=dg_mix_pre, g_mix_post=dg_mix_post, g_mem=dg_mem, g_x_pre=dg_x_pre,
                 g_x_post=dg_x_post, g_ffn_pre=dg_ffn_pre, g_ffn_post=dg_ffn_post,
                 hgrn_onorm=donorm, hgrn_lb=dlb, sinks=dsinks, sq=sq)
    return grad_x, gw, parts


def _position():
    return lax.axis_index("x"), lax.axis_index("y"), lax.axis_index("c")


def _peer(pos, k):
    x, y, c = pos
    return (1 - x if k & 4 else x, 1 - y if k & 2 else y, 1 - c if k & 1 else c)


def _linear(pos):
    x, y, c = pos
    return 4 * x + 2 * y + c


HBM_SPEC = pl.BlockSpec(memory_space=pltpu.HBM)
SEM_SPEC = pl.BlockSpec(memory_space=pltpu.SEMAPHORE)
DATAFLOW = pltpu.SideEffectType.DATAFLOW_SIDE_EFFECTING
SEND_ORDER = (1, 2, 4, 3, 5, 6, 7)


def _in_hbm(a):
    return pltpu.with_memory_space_constraint(a, pltpu.HBM)


def _prepare_weights(shards):
    n = len(shards)

    def body(*refs):
        ins, outs, lands, sem = refs[:n], refs[n:2 * n], refs[2 * n:3 * n], refs[3 * n]
        me_lin = _linear(_position())
        copies = []
        for a in range(n):
            r = ins[a].shape[0]
            outs[a][...] = ins[a][...].astype(BF16)
            copies.append(pltpu.make_async_copy(outs[a], lands[a].at[pl.ds(me_lin * r, r), :], sem.at[a]))
            copies[-1].start()
        for cp in copies:
            cp.wait()

    vmem = pl.BlockSpec(memory_space=pltpu.VMEM)
    res = pl.pallas_call(
        body, name="prepare_weights",
        out_shape=tuple(jax.ShapeDtypeStruct(s.shape, BF16) for s in shards)
        + tuple(jax.ShapeDtypeStruct((N_DEV * s.shape[0], s.shape[1]), BF16) for s in shards),
        in_specs=[vmem] * n, out_specs=tuple([vmem] * n + [ANY_SPEC] * n),
        scratch_shapes=[pltpu.SemaphoreType.DMA((n,))], compiler_params=_params(),
    )(*shards)
    return res[:n], res[n:]


def _gather_start(shards, lands):
    n = len(shards)
    rows = [s.shape[0] for s in shards]

    def body(*refs):
        srcs, land = refs[:n], refs[n:2 * n]
        send_sems, recv_sems = refs[2 * n:3 * n], refs[3 * n:4 * n]
        me = _position()
        for a in range(n):
            mine = land[a].at[pl.ds(_linear(me) * rows[a], rows[a]), :]
            for k in SEND_ORDER:
                pltpu.make_async_remote_copy(
                    src_ref=srcs[a], dst_ref=mine, send_sem=send_sems[a].at[k - 1], recv_sem=recv_sems[a].at[k - 1],
                    device_id=_peer(me, k), device_id_type=MESH).start()

    sems = tuple(pltpu.SemaphoreType.DMA((N_DEV - 1,)) for _ in range(2 * n))
    res = pl.pallas_call(
        body, name="weights_send",
        out_shape=sems + tuple(pltpu.HBM(s.shape, s.dtype) for s in shards)
        + tuple(pltpu.HBM(l.shape, l.dtype) for l in lands),
        in_specs=(HBM_SPEC,) * (2 * n), out_specs=(SEM_SPEC,) * (2 * n) + (HBM_SPEC,) * (2 * n),
        input_output_aliases={i: 2 * n + i for i in range(2 * n)},
        compiler_params=pltpu.CompilerParams(has_side_effects=DATAFLOW),
    )(*[_in_hbm(s) for s in shards], *[_in_hbm(l) for l in lands])
    return [(res[a], res[n + a], res[2 * n + a], res[3 * n + a]) for a in range(n)]


def _gather_wait(send_sems, recv_sems, shard_thru, land_thru, after, *, name):
    r = shard_thru.shape[0]

    def body(src_ref, land_ref, send_sems, recv_sems, after_ref, src_dead, got_ref):
        del after_ref, src_dead, got_ref
        me = _position()
        for k in SEND_ORDER:
            peer = _peer(me, k)
            copy = pltpu.make_async_remote_copy(
                src_ref=src_ref, dst_ref=land_ref.at[pl.ds(_linear(peer) * r, r), :],
                send_sem=send_sems.at[k - 1], recv_sem=recv_sems.at[k - 1],
                device_id=peer, device_id_type=MESH)
            copy.wait_send()
            copy.wait_recv()

    return pl.pallas_call(
        body, name=name,
        out_shape=(pltpu.HBM(shard_thru.shape, shard_thru.dtype), pltpu.HBM(land_thru.shape, land_thru.dtype)),
        in_specs=(HBM_SPEC, HBM_SPEC, SEM_SPEC, SEM_SPEC, ANY_SPEC),
        out_specs=(HBM_SPEC, HBM_SPEC), input_output_aliases={0: 0, 1: 1},
        compiler_params=pltpu.CompilerParams(has_side_effects=DATAFLOW),
    )(shard_thru, land_thru, send_sems, recv_sems, after)[1]


def _exchange_start(gs, *, name):
    n = len(gs)
    rows = [g.shape[0] // N_DEV for g in gs]
    lands = [lax.empty((N_DEV - 1, r, g.shape[1]), g.dtype) for g, r in zip(gs, rows)]

    def body(*refs):
        g_refs, land_refs = refs[:n], refs[n:2 * n]
        send_sems, recv_sems = refs[2 * n:3 * n], refs[3 * n:4 * n]
        me = _position()
        for a in range(n):
            for k in SEND_ORDER:
                peer = _peer(me, k)
                pltpu.make_async_remote_copy(
                    src_ref=g_refs[a].at[pl.ds(_linear(peer) * rows[a], rows[a]), :],
                    dst_ref=land_refs[a].at[k - 1],
                    send_sem=send_sems[a].at[k - 1], recv_sem=recv_sems[a].at[k - 1],
                    device_id=peer, device_id_type=MESH).start()

    res = pl.pallas_call(
        body, name=name,
        out_shape=tuple(pltpu.SemaphoreType.DMA((N_DEV - 1,)) for _ in range(2 * n))
        + tuple(pltpu.HBM(a.shape, a.dtype) for a in gs + lands),
        in_specs=(HBM_SPEC,) * (2 * n), out_specs=(SEM_SPEC,) * (2 * n) + (HBM_SPEC,) * (2 * n),
        input_output_aliases={i: 2 * n + i for i in range(2 * n)},
        compiler_params=pltpu.CompilerParams(has_side_effects=DATAFLOW),
    )(*[_in_hbm(a) for a in gs + lands])
    return [(res[a], res[n + a], res[2 * n + a], res[3 * n + a]) for a in range(n)]


def _exchange_wait(send_sems, recv_sems, g_thru, land_thru, after, *, name):
    r = land_thru.shape[1]

    def body(g_ref, land_ref, send_sems, recv_sems, after_ref, g_dead, got_ref):
        del after_ref, g_dead, got_ref
        me = _position()
        for k in SEND_ORDER:
            peer = _peer(me, k)
            copy = pltpu.make_async_remote_copy(
                src_ref=g_ref.at[pl.ds(_linear(peer) * r, r), :], dst_ref=land_ref.at[k - 1],
                send_sem=send_sems.at[k - 1], recv_sem=recv_sems.at[k - 1],
                device_id=peer, device_id_type=MESH)
            copy.wait_send()
            copy.wait_recv()

    return pl.pallas_call(
        body, name=name,
        out_shape=(pltpu.HBM(g_thru.shape, g_thru.dtype), pltpu.HBM(land_thru.shape, land_thru.dtype)),
        in_specs=(HBM_SPEC, HBM_SPEC, SEM_SPEC, SEM_SPEC, pl.BlockSpec(memory_space=pl.ANY)),
        out_specs=(HBM_SPEC, HBM_SPEC), input_output_aliases={0: 0, 1: 1},
        compiler_params=pltpu.CompilerParams(has_side_effects=DATAFLOW),
    )(g_thru, land_thru, send_sems, recv_sems, after)


def _adamw_math(w, g, m, v):
    m = B1 * m + (1.0 - B1) * g
    v = B2 * v + (1.0 - B2) * (g * g)
    delta = -LR * ((m / C1) / (jnp.sqrt(v / C2) + AEPS) + WD * w)
    return delta, m, v


def _sum_adamw(g_all, land, w, m, v, *, name):
    r = w.shape[0]

    def body(all_ref, land_ref, w_ref, m_ref, v_ref, g_ref, d_ref, nm_ref, nv_ref, own_ref, sem):
        mine = pltpu.make_async_copy(all_ref.at[pl.ds(_linear(_position()) * r, r), :], own_ref, sem)
        mine.start()
        g = land_ref[0].astype(F32)
        for s in range(1, N_DEV - 1):
            g = g + land_ref[s].astype(F32)
        mine.wait()
        g = own_ref[...].astype(F32) + g
        g_ref[...] = g
        d_ref[...], nm_ref[...], nv_ref[...] = _adamw_math(w_ref[...], g, m_ref[...], v_ref[...])

    vmem = pl.BlockSpec(memory_space=pltpu.VMEM)
    return pl.pallas_call(
        body, name=name, out_shape=(jax.ShapeDtypeStruct(w.shape, F32),) * 4,
        in_specs=[ANY_SPEC, vmem, vmem, vmem, vmem], out_specs=(vmem,) * 4,
        scratch_shapes=[pltpu.VMEM((r, w.shape[1]), BF16), pltpu.SemaphoreType.DMA(())],
        compiler_params=_params(),
    )(g_all, land, w, m, v)


SMALL = ("g_mix_pre", "g_mix_post", "g_mem", "g_x_pre", "g_x_post", "g_ffn_pre", "g_ffn_post",
         "hgrn_onorm", "hgrn_lb", "sinks")
SMALL_W = dict(hgrn_onorm=HD, hgrn_lb=HG_W, sinks=8)
SQ_ROW = len(SMALL)
PACK_ROWS = 16


def _small_allreduce(parts):
    ns = len(SMALL)

    def body(*refs):
        part, tot_ref = refs[:ns + 1], refs[ns + 1]
        gath, send_sems, recv_sems = refs[ns + 2:]
        me = _position()
        mine = gath.at[_linear(me)]
        mine[...] = jnp.zeros((PACK_ROWS, D), F32)
        for r, name in enumerate(SMALL):
            wd = SMALL_W.get(name, D)
            mine[r:r + 1, 0:wd] = jnp.sum(part[r][...], axis=0, keepdims=True)[:, 0:wd]
        sq = jnp.sum(part[ns][...]) * (0.5 / D)
        mine[SQ_ROW:SQ_ROW + 1, :] = jnp.full((1, D), sq, F32)

        def copy(k):
            peer = _peer(me, k)
            return pltpu.make_async_remote_copy(
                src_ref=mine, dst_ref=mine, send_sem=send_sems.at[k - 1], recv_sem=recv_sems.at[k - 1],
                device_id=peer, device_id_type=MESH)

        def arrival(k):
            slot = gath.at[_linear(_peer(me, k))]
            return pltpu.make_async_remote_copy(
                src_ref=slot, dst_ref=slot, send_sem=send_sems.at[k - 1], recv_sem=recv_sems.at[k - 1],
                device_id=_peer(me, k), device_id_type=MESH)

        sent = [copy(k) for k in range(1, 8)]
        for cp in sent:
            cp.start()
        for k in range(1, 8):
            arrival(k).wait_recv()
        for cp in sent:
            cp.wait_send()
        tot = gath[0]
        for s in range(1, N_DEV):
            tot = tot + gath[s]
        tot_ref[...] = tot

    return pl.pallas_call(
        body, name="small_allreduce", out_shape=jax.ShapeDtypeStruct((PACK_ROWS, D), F32),
        scratch_shapes=[pltpu.VMEM((N_DEV, PACK_ROWS, D), F32), pltpu.SemaphoreType.DMA((7,)),
                        pltpu.SemaphoreType.DMA((7,))],
        compiler_params=_params(has_side_effects=True),
    )(*[parts[n] for n in SMALL], parts["sq"])


def _small_update(tot, sm, m_sm, v_sm):
    ns = len(SMALL)

    def body(*refs):
        tot = refs[0][...]
        w_refs, m_refs, v_refs = refs[1:ns + 1], refs[ns + 1:2 * ns + 1], refs[2 * ns + 1:3 * ns + 1]
        outs = refs[3 * ns + 1:]
        loss_ref = outs[0]
        g_out, d_out = outs[1:ns + 1], outs[ns + 1:2 * ns + 1]
        nm_out, nv_out = outs[2 * ns + 1:3 * ns + 1], outs[3 * ns + 1:4 * ns + 1]
        loss_ref[...] = tot[SQ_ROW:SQ_ROW + 1, 0:1]
        for r, name in enumerate(SMALL):
            wd = SMALL_W.get(name, D)
            g = tot[r:r + 1, 0:wd]
            w = w_refs[r][...]
            if name == "hgrn_lb":
                mx = jnp.maximum(w[0:1], w[1:2])
                e0, e1 = jnp.exp(w[0:1] - mx), jnp.exp(w[1:2] - mx)
                lb0 = e0 / (e0 + e1)
                g0 = g * lb0 * (1.0 - lb0)
                for i, gi in enumerate((g0, -g0)):
                    d, nm, nv = _adamw_math(w[i:i + 1], gi, m_refs[r][i:i + 1, :], v_refs[r][i:i + 1, :])
                    g_out[r][i:i + 1, :] = gi
                    d_out[r][i:i + 1, :], nm_out[r][i:i + 1, :], nv_out[r][i:i + 1, :] = d, nm, nv
            else:
                d, nm, nv = _adamw_math(w, g, m_refs[r][...], v_refs[r][...])
                g_out[r][...] = g
                d_out[r][...], nm_out[r][...], nv_out[r][...] = d, nm, nv

    shapes = [jax.ShapeDtypeStruct(sm[n].shape, F32) for n in SMALL]
    res = pl.pallas_call(
        body, name="small_update", out_shape=tuple([jax.ShapeDtypeStruct((1, 1), F32)] + shapes * 4),
        compiler_params=_params(),
    )(tot, *[sm[n] for n in SMALL], *[m_sm[n] for n in SMALL], *[v_sm[n] for n in SMALL])
    groups = [dict(zip(SMALL, res[1 + i * ns:1 + (i + 1) * ns])) for i in range(4)]
    return res[0], groups[0], groups[1], groups[2], groups[3]


BIG = ("w_in", "w_gate", "w_up", "w_down", "w_out", "wq_x", "wk_x", "wv_x", "wo_x")
BIG_KEY = dict(w_in="winT", w_gate="wgT", w_up="wuT", w_down="wd", w_out="wout", wq_x="wq", wk_x="wk",
               wv_x="wv", wo_x="wo")
TRANSPOSED = ("w_in", "w_gate", "w_up")
WEIGHTS = ("w_in", "sinks", "hgrn_lb", "hgrn_onorm", "w_out", "g_mix_pre", "g_mix_post", "g_mem", "g_x_pre",
           "g_x_post", "wq_x", "wk_x", "wv_x", "wo_x", "g_ffn_pre", "g_ffn_post", "w_gate", "w_up", "w_down")


def kernel(x, mem, w_in, sinks, hgrn_lb, hgrn_onorm, w_out, g_mix_pre, g_mix_post, g_mem, g_x_pre, g_x_post, wq_x, wk_x, wv_x, wo_x, g_ffn_pre, g_ffn_post, w_gate, w_up, w_down, loss_target, m_w_in, m_sinks, m_hgrn_lb, m_hgrn_onorm, m_w_out, m_g_mix_pre, m_g_mix_post, m_g_mem, m_g_x_pre, m_g_x_post, m_wq_x, m_wk_x, m_wv_x, m_wo_x, m_g_ffn_pre, m_g_ffn_post, m_w_gate, m_w_up, m_w_down, v_w_in, v_sinks, v_hgrn_lb, v_hgrn_onorm, v_w_out, v_g_mix_pre, v_g_mix_post, v_g_mem, v_g_x_pre, v_g_x_post, v_wq_x, v_wk_x, v_wv_x, v_wo_x, v_g_ffn_pre, v_g_ffn_post, v_w_gate, v_w_up, v_w_down):
    given = dict(locals())
    wts = {n: given[n] for n in WEIGHTS}
    ms = {n: given["m_" + n] for n in WEIGHTS}
    vs = {n: given["v_" + n] for n in WEIGHTS}

    def mat(a, name):
        a = a[0]
        return a.T if name in TRANSPOSED else a

    order = ("w_in", "w_out", "wq_x", "wk_x", "wv_x", "wo_x", "w_gate", "w_up", "w_down")
    flying = dict(zip(order, _gather_start(*_prepare_weights([mat(wts[n], n) for n in order]))))
    name_of = {k: n for n, k in BIG_KEY.items()}

    def fetch(key, after):
        return _gather_wait(*flying[name_of[key]], after, name="weights_recv_" + name_of[key])

    sm = {n: wts[n] for n in SMALL}
    started, held = {}, {}
    send_with = {"wgT": ("wgT", "wuT"), "wuT": ("wgT", "wuT"), "wq": ("wq", "wk", "wv"), "wk": ("wq", "wk", "wv"),
                 "wv": ("wq", "wk", "wv")}

    def emit(key, g):
        held[key] = g
        group = send_with.get(key, (key,))
        if key != group[-1]:
            return None
        flights = _exchange_start([held[k] for k in group], name="grad_send_" + name_of[group[0]])
        started.update({name_of[k]: f for k, f in zip(group, flights)})
        return flights[-1][2]

    grad_x, _, parts = _local_step(x[0], mem[0], loss_target[0], fetch, sm, emit)
    grads, deltas, new_m, new_v = {}, {}, {}, {}
    after = grad_x
    for n in ("w_down", "w_gate", "w_up", "wo_x", "wq_x", "wk_x", "wv_x", "w_out", "w_in"):
        g_all, land = _exchange_wait(*started[n], after, name="grad_recv_" + n)
        res = _sum_adamw(g_all, land, mat(wts[n], n), mat(ms[n], n), mat(vs[n], n), name="adamw_" + n)
        after = res[1]
        if n in TRANSPOSED:
            res = [a.T for a in res]
        grads[n], deltas[n], new_m[n], new_v[n] = [a[None] for a in res]
    loss, g_s, d_s, m_s, v_s = _small_update(_small_allreduce(parts), sm, {n: ms[n] for n in SMALL},
                                             {n: vs[n] for n in SMALL})
    grads.update(g_s), deltas.update(d_s), new_m.update(m_s), new_v.update(v_s)
    return (loss[0, 0], grad_x[None], *[grads[n] for n in WEIGHTS], *[deltas[n] for n in WEIGHTS],
            *[new_m[n] for n in WEIGHTS], *[new_v[n] for n in WEIGHTS])
```

```python
import functools

import jax
import jax.numpy as jnp
from jax import lax
from jax.experimental import pallas as pl
from jax.experimental.pallas import tpu as pltpu

F32 = jnp.float32
BF16 = jnp.bfloat16

D = 1024
D_IN = 2816
D_FF = 2816
CHUNK = 64
SWA_W = 512
KV_W = 128
HG_W = 512
HD = 128
ZQH, ZFH, ZIH, ZGH = 768, 1280, 1792, 2304
XH, XD = 4, 256
EPS = 1e-6
NEG = -1e30
N_DEV = 8
MESH = pl.DeviceIdType.MESH

LR, B1, B2, AEPS, WD, STEP = 0.001, 0.9, 0.999, 1e-08, 0.01, 10
C1 = 1.0 - B1 ** STEP
C2 = 1.0 - B2 ** STEP

VMEM_LIMIT = 56 * 1024 * 1024


def _params(**kw):
    return pltpu.CompilerParams(vmem_limit_bytes=VMEM_LIMIT, **kw)


def _sig(x):
    return 1.0 / (1.0 + jnp.exp(-x))


def _rowsum8(x):
    r, w = x.shape
    return jnp.sum(x.reshape(r // 8, 8, w), axis=0)


def _dot(a, b, ca, cb, precision=None):
    return lax.dot_general(a, b, (((ca,), (cb,)), ((), ())), preferred_element_type=F32,
                           precision=precision)


ANY_SPEC = pl.BlockSpec(memory_space=pl.ANY)


def _mm(a, b, *, ta=False, tb=False, out_dtype, tm, tn, tk=None, name, dep=None, n_outer=False):
    m = a.shape[1] if ta else a.shape[0]
    k = a.shape[0] if ta else a.shape[1]
    n = b.shape[0] if tb else b.shape[1]
    tm, tn = min(tm, m), min(tn, n)
    tk = k if tk is None else min(tk, k)
    nk = k // tk
    assert m % tm == 0 and n % tn == 0 and k % tk == 0, (name, m, n, k, tm, tn, tk)
    ij = (lambda g0, g1: (g1, g0)) if n_outer else (lambda g0, g1: (g0, g1))
    a_spec = (pl.BlockSpec((tk, tm), lambda g0, g1, kk: (kk, ij(g0, g1)[0])) if ta
              else pl.BlockSpec((tm, tk), lambda g0, g1, kk: (ij(g0, g1)[0], kk)))
    b_spec = (pl.BlockSpec((tn, tk), lambda g0, g1, kk: (ij(g0, g1)[1], kk)) if tb
              else pl.BlockSpec((tk, tn), lambda g0, g1, kk: (kk, ij(g0, g1)[1])))
    ca, cb = (0 if ta else 1), (1 if tb else 0)

    deps = [] if dep is None else [dep]

    def body(a_ref, b_ref, *rest):
        o_ref, acc = rest[len(deps)], rest[len(deps) + 1:]
        p = _dot(a_ref[...].astype(BF16), b_ref[...].astype(BF16), ca, cb)
        if nk == 1:
            o_ref[...] = p.astype(out_dtype)
        else:
            acc_ref, = acc
            kk = pl.program_id(2)

            @pl.when(kk == 0)
            def _():
                acc_ref[...] = p

            @pl.when(kk > 0)
            def _():
                acc_ref[...] += p

            @pl.when(kk == nk - 1)
            def _():
                o_ref[...] = acc_ref[...].astype(out_dtype)

    return pl.pallas_call(
        body, name=name, out_shape=jax.ShapeDtypeStruct((m, n), out_dtype),
        grid=(n // tn, m // tm, nk) if n_outer else (m // tm, n // tn, nk),
        in_specs=[a_spec, b_spec] + [ANY_SPEC] * len(deps),
        out_specs=pl.BlockSpec((tm, tn), lambda g0, g1, kk: ij(g0, g1)),
        scratch_shapes=[pltpu.VMEM((tm, tn), F32)] if nk > 1 else [],
        compiler_params=_params(dimension_semantics=("parallel", "parallel", "arbitrary")),
    )(a, b, *deps)


def _mm2(a1, b1, a2, b2, *, tb=False, out_dtype, tm, name, dep=None):
    m, k = a1.shape
    n = b1.shape[0] if tb else b1.shape[1]
    tm = min(tm, m)
    assert m % tm == 0
    cb = 1 if tb else 0
    deps = [] if dep is None else [dep]

    def body(a1_ref, b1_ref, a2_ref, b2_ref, *rest):
        o_ref = rest[len(deps)]
        o_ref[...] = (_dot(a1_ref[...].astype(BF16), b1_ref[...], 1, cb)
                      + _dot(a2_ref[...].astype(BF16), b2_ref[...], 1, cb)).astype(out_dtype)

    a_spec = pl.BlockSpec((tm, k), lambda i: (i, 0))
    b_spec = pl.BlockSpec(b1.shape, lambda i: (0, 0))
    return pl.pallas_call(
        body, name=name, out_shape=jax.ShapeDtypeStruct((m, n), out_dtype),
        grid=(m // tm,), in_specs=[a_spec, b_spec, a_spec, b_spec] + [ANY_SPEC] * len(deps),
        out_specs=pl.BlockSpec((tm, n), lambda i: (i, 0)),
        compiler_params=_params(dimension_semantics=("parallel",)),
    )(a1, b1, a2, b2, *deps)


def _rstd(x):
    return lax.rsqrt(jnp.mean(x * x, axis=-1, keepdims=True) + EPS)


def _norm_bwd(xh, r, t):
    return r * (t - xh * jnp.mean(xh * t, axis=-1, keepdims=True))


def _prenorm(x, g, *, name):
    t, d = x.shape
    tb = min(512, t)

    def body(x_ref, g_ref, o_ref):
        xf = x_ref[...]
        o_ref[...] = (xf * _rstd(xf) * g_ref[...]).astype(BF16)

    return pl.pallas_call(
        body, name=name, out_shape=jax.ShapeDtypeStruct((t, d), BF16), grid=(t // tb,),
        in_specs=[pl.BlockSpec((tb, d), lambda i: (i, 0)), pl.BlockSpec((1, d), lambda i: (0, 0))],
        out_specs=pl.BlockSpec((tb, d), lambda i: (i, 0)), compiler_params=_params(),
    )(x, g)


def _post_pre(h, y, g_post, g_pre, *, name):
    t, d = h.shape
    tb = min(512, t)

    def body(h_ref, y_ref, gp_ref, gn_ref, hn_ref, u_ref):
        y_ = y_ref[...].astype(F32)
        hn = h_ref[...] + y_ * _rstd(y_) * gp_ref[...]
        hn_ref[...] = hn
        u_ref[...] = (hn * _rstd(hn) * gn_ref[...]).astype(BF16)

    row = pl.BlockSpec((tb, d), lambda i: (i, 0))
    vec = pl.BlockSpec((1, d), lambda i: (0, 0))
    return pl.pallas_call(
        body, name=name, out_shape=(jax.ShapeDtypeStruct((t, d), F32), jax.ShapeDtypeStruct((t, d), BF16)),
        grid=(t // tb,), in_specs=[row, row, vec, vec], out_specs=(row, row), compiler_params=_params(),
    )(h, y, g_post, g_pre)


def _final_loss(h, y, g_post, target, *, name):
    t, d = h.shape
    tb = min(512, t)

    def body(h_ref, y_ref, g_ref, t_ref, sq_ref, dh_ref, dy_ref, dg_ref):
        @pl.when(pl.program_id(0) == 0)
        def _():
            sq_ref[...] = jnp.zeros_like(sq_ref)
            dg_ref[...] = jnp.zeros_like(dg_ref)

        y_ = y_ref[...].astype(F32)
        r = _rstd(y_)
        yh = y_ * r
        g = g_ref[...]
        err = h_ref[...] + yh * g - t_ref[...]
        sq_ref[...] += _rowsum8(err * err)
        dh = err * (1.0 / d)
        dh_ref[...] = dh
        dg_ref[...] += _rowsum8(dh * yh)
        dy_ref[...] = _norm_bwd(yh, r, dh * g).astype(BF16)

    row = pl.BlockSpec((tb, d), lambda i: (i, 0))
    vec = pl.BlockSpec((1, d), lambda i: (0, 0))
    acc = pl.BlockSpec((8, d), lambda i: (0, 0))
    return pl.pallas_call(
        body, name=name,
        out_shape=(jax.ShapeDtypeStruct((8, d), F32), jax.ShapeDtypeStruct((t, d), F32),
                   jax.ShapeDtypeStruct((t, d), BF16), jax.ShapeDtypeStruct((8, d), F32)),
        grid=(t // tb,), in_specs=[row, row, vec, row], out_specs=(acc, row, row, acc),
        compiler_params=_params(dimension_semantics=("arbitrary",)),
    )(h, y, g_post, target)


def _post_pre_bwd(dh_out, du, hn, y, g_post, g_pre, *, name):
    t, d = hn.shape
    tb = min(512, t)

    def body(dho_ref, du_ref, hn_ref, y_ref, gp_ref, gn_ref, dh_ref, dy_ref, dgn_ref, dgp_ref):
        @pl.when(pl.program_id(0) == 0)
        def _():
            dgn_ref[...] = jnp.zeros_like(dgn_ref)
            dgp_ref[...] = jnp.zeros_like(dgp_ref)

        hn_ = hn_ref[...]
        r2 = _rstd(hn_)
        xh = hn_ * r2
        du_ = du_ref[...].astype(F32)
        dgn_ref[...] += _rowsum8(du_ * xh)
        dh = dho_ref[...] + _norm_bwd(xh, r2, du_ * gn_ref[...])
        dh_ref[...] = dh
        y_ = y_ref[...].astype(F32)
        r1 = _rstd(y_)
        yh = y_ * r1
        dgp_ref[...] += _rowsum8(dh * yh)
        dy_ref[...] = _norm_bwd(yh, r1, dh * gp_ref[...]).astype(BF16)

    row = pl.BlockSpec((tb, d), lambda i: (i, 0))
    vec = pl.BlockSpec((1, d), lambda i: (0, 0))
    acc = pl.BlockSpec((8, d), lambda i: (0, 0))
    return pl.pallas_call(
        body, name=name,
        out_shape=(jax.ShapeDtypeStruct((t, d), F32), jax.ShapeDtypeStruct((t, d), BF16),
                   jax.ShapeDtypeStruct((8, d), F32), jax.ShapeDtypeStruct((8, d), F32)),
        grid=(t // tb,), in_specs=[row, row, row, row, vec, vec], out_specs=(row, row, acc, acc),
        compiler_params=_params(dimension_semantics=("arbitrary",)),
    )(dh_out, du, hn, y, g_post, g_pre)


def _pre_bwd(dh_out, du, x, g, *, name):
    t, d = x.shape
    tb = min(512, t)
    has_res = dh_out is not None

    def body(*refs):
        if has_res:
            dho_ref, du_ref, x_ref, g_ref, dx_ref, dg_ref = refs
        else:
            du_ref, x_ref, g_ref, dx_ref, dg_ref = refs

        @pl.when(pl.program_id(0) == 0)
        def _():
            dg_ref[...] = jnp.zeros_like(dg_ref)

        x_ = x_ref[...]
        r = _rstd(x_)
        xh = x_ * r
        du_ = du_ref[...].astype(F32)
        dg_ref[...] += _rowsum8(du_ * xh)
        dx = _norm_bwd(xh, r, du_ * g_ref[...])
        if has_res:
            dx = dx + dho_ref[...]
        dx_ref[...] = dx

    row = pl.BlockSpec((tb, d), lambda i: (i, 0))
    vec = pl.BlockSpec((1, d), lambda i: (0, 0))
    acc = pl.BlockSpec((8, d), lambda i: (0, 0))
    ins = ([dh_out] if has_res else []) + [du, x, g]
    return pl.pallas_call(
        body, name=name,
        out_shape=(jax.ShapeDtypeStruct((t, d), F32), jax.ShapeDtypeStruct((8, d), F32)),
        grid=(t // tb,), in_specs=[row] * (len(ins) - 1) + [vec], out_specs=(row, acc),
        compiler_params=_params(dimension_semantics=("arbitrary",)),
    )(*ins)


QB = 256


def _half_mask(shape, e):
    lane = lax.broadcasted_iota(jnp.int32, shape, len(shape) - 1)
    return (lane // 64) == e


def _place(kv):
    kv = kv.astype(F32)
    sw = pltpu.roll(kv, 64, 1)
    m0 = _half_mask(kv.shape, 0)
    return [[jnp.where(m0, kv, 0.0).astype(BF16), jnp.where(m0, 0.0, sw).astype(BF16)],
            [jnp.where(m0, sw, 0.0).astype(BF16), jnp.where(m0, 0.0, kv).astype(BF16)]]


def _swa_valid_q(i, nq, nk):
    qc = lax.broadcasted_iota(jnp.int32, (nq, nk), 0) // CHUNK
    kc = lax.broadcasted_iota(jnp.int32, (nq, nk), 1) // CHUNK - 2
    return (kc <= qc) & (qc <= kc + 2) & (4 * i + kc >= 0)


def _swa_fwd(z, sinks, t):
    nb = t // QB

    def body(s_ref, q_ref, kp_ref, kc_ref, vp_ref, vc_ref, o_ref, lse_ref):
        i = pl.program_id(0)
        kpl = _place(jnp.concatenate([kp_ref[...], kc_ref[...]], axis=0))
        vpl = _place(jnp.concatenate([vp_ref[...], vc_ref[...]], axis=0))
        valid = _swa_valid_q(i, QB, QB + 128)
        lane = lax.broadcasted_iota(jnp.int32, (QB, 128), 1)
        lse_out = jnp.zeros((QB, 128), F32)
        for j in range(4):
            qp = q_ref[:, 128 * j:128 * (j + 1)].astype(BF16)
            acc = jnp.zeros((QB, 128), F32)
            for e in range(2):
                h = 2 * j + e
                kvh = h // 4
                qm = jnp.where(_half_mask(qp.shape, e), qp, jnp.zeros_like(qp))
                s = _dot(qm, kpl[kvh][e], 1, 1) * 0.125
                s = jnp.where(valid, s, NEG)
                sink = s_ref[0, h]
                m = jnp.maximum(jnp.max(s, axis=-1, keepdims=True), sink)
                p = jnp.exp(s - m)
                l = jnp.sum(p, axis=-1, keepdims=True) + jnp.exp(sink - m)
                acc = acc + _dot(p.astype(BF16), vpl[kvh][e], 1, 0) * (1.0 / l)
                lse_out = jnp.where(lane == h, m + jnp.log(l), lse_out)
            o_ref[:, 128 * j:128 * (j + 1)] = acc.astype(BF16)
        lse_ref[...] = lse_out

    prev = lambda c: pl.BlockSpec((128, 128), lambda i: (jnp.maximum(2 * i - 1, 0), c))
    cur = lambda c: pl.BlockSpec((QB, 128), lambda i: (i, c))
    return pl.pallas_call(
        body, name="swa_fwd",
        out_shape=(jax.ShapeDtypeStruct((t, D), BF16), jax.ShapeDtypeStruct((t, 128), F32)),
        grid=(nb,),
        in_specs=[pl.BlockSpec(memory_space=pltpu.SMEM),
                  pl.BlockSpec((QB, SWA_W), lambda i: (i, 0)), prev(4), cur(4), prev(5), cur(5)],
        out_specs=(pl.BlockSpec((QB, SWA_W), lambda i: (i, 0)), pl.BlockSpec((QB, 128), lambda i: (i, 0))),
        compiler_params=_params(),
    )(sinks, z, z, z, z, z)


def _swa_bwd(z, sinks, ymix, lse, dymix, t):
    nb = t // QB
    nk = QB + 128

    def body(s_ref, q_ref, kp_ref, kc_ref, vp_ref, vc_ref, o_ref, do_ref, l_ref,
             dq_ref, first_ref, second_ref, ds_ref, carry_ref):
        i = pl.program_id(0)
        live = i < nb

        @pl.when(i == 0)
        def _():
            ds_ref[...] = jnp.zeros_like(ds_ref)
            carry_ref[...] = jnp.zeros_like(carry_ref)

        lane = lax.broadcasted_iota(jnp.int32, (8, 128), 1)
        kpl = _place(jnp.concatenate([kp_ref[...], kc_ref[...]], axis=0))
        vpl = _place(jnp.concatenate([vp_ref[...], vc_ref[...]], axis=0))
        valid = _swa_valid_q(i, QB, nk) & live
        lse_c = l_ref[...]
        dsink = jnp.zeros((8, 128), F32)
        dk_acc = [[jnp.zeros((nk, 128), F32) for _ in range(2)] for _ in range(2)]
        dv_acc = [[jnp.zeros((nk, 128), F32) for _ in range(2)] for _ in range(2)]
        dq = []
        for j in range(4):
            cols = slice(128 * j, 128 * (j + 1))
            qp = q_ref[:, cols].astype(BF16)
            dop = do_ref[:, cols]
            prod = dop.astype(F32) * o_ref[:, cols].astype(F32)
            acc = jnp.zeros((QB, 128), F32)
            for e in range(2):
                h = 2 * j + e
                kvh = h // 4
                hm = _half_mask(qp.shape, e)
                qm = jnp.where(hm, qp, jnp.zeros_like(qp))
                dom = jnp.where(hm, dop, jnp.zeros_like(dop))
                dd = jnp.sum(jnp.where(hm, prod, 0.0), axis=-1, keepdims=True)
                lse_h = lse_c[:, h:h + 1]
                s = _dot(qm, kpl[kvh][e], 1, 1) * 0.125
                p = jnp.where(valid, jnp.exp(s - lse_h), 0.0)
                dp = _dot(dom, vpl[kvh][e], 1, 1)
                ds = (p * (dp - dd) * 0.125).astype(BF16)
                acc = acc + _dot(ds, kpl[kvh][e], 1, 0)
                dk_acc[kvh][e] = dk_acc[kvh][e] + _dot(ds, qm, 0, 0)
                dv_acc[kvh][e] = dv_acc[kvh][e] + _dot(p.astype(BF16), dom, 0, 0)
                ps = jnp.where(live, jnp.exp(s_ref[0, h] - lse_h) * dd, 0.0)
                dsink = dsink - jnp.where(lane == h, _rowsum8(jnp.broadcast_to(ps, (QB, 128))), 0.0)
            dq.append(acc.astype(BF16))
        ds_ref[...] += dsink
        dk = dk_acc[0][0] + dk_acc[1][1] + pltpu.roll(dk_acc[0][1] + dk_acc[1][0], 64, 1)
        dv = dv_acc[0][0] + dv_acc[1][1] + pltpu.roll(dv_acc[0][1] + dv_acc[1][0], 64, 1)
        dkv = jnp.concatenate([dk, dv], axis=1)
        second_ref[...] = (carry_ref[...] + dkv[0:128]).astype(BF16)
        carry_ref[...] = dkv[256:384]

        @pl.when(live)
        def _():
            for j in range(4):
                dq_ref[:, 128 * j:128 * (j + 1)] = dq[j]
            first_ref[...] = dkv[128:256].astype(BF16)

    blk = lambda i: jnp.minimum(i, nb - 1)
    prev = lambda c: pl.BlockSpec((128, 128), lambda i: (jnp.maximum(2 * blk(i) - 1, 0), c))
    cur = lambda w, c: pl.BlockSpec((QB, w), lambda i: (blk(i), c))
    half = lambda index: pl.BlockSpec((128, 256), lambda i: (index(i), 0))
    return pl.pallas_call(
        body, name="swa_bwd",
        out_shape=(jax.ShapeDtypeStruct((t, SWA_W), BF16), jax.ShapeDtypeStruct((t // 2, 256), BF16),
                   jax.ShapeDtypeStruct((t // 2, 256), BF16), jax.ShapeDtypeStruct((8, 128), F32)),
        grid=(nb + 1,),
        in_specs=[pl.BlockSpec(memory_space=pltpu.SMEM),
                  cur(SWA_W, 0), prev(4), cur(128, 4), prev(5), cur(128, 5),
                  cur(SWA_W, 0), cur(SWA_W, 0), cur(128, 0)],
        out_specs=(cur(SWA_W, 0), half(blk), half(lambda i: jnp.maximum(i - 1, 0)),
                   pl.BlockSpec((8, 128), lambda i: (0, 0))),
        scratch_shapes=[pltpu.VMEM((128, 256), F32)],
        compiler_params=_params(dimension_semantics=("arbitrary",)),
    )(sinks, z, z, z, z, z, ymix, dymix, lse)


HB = 256


def _lower_bound(lb_ref):
    a = lb_ref[...]
    a0, a1 = a[0:1], a[1:2]
    mx = jnp.maximum(a0, a1)
    e0, e1 = jnp.exp(a0 - mx), jnp.exp(a1 - mx)
    return e0 / (e0 + e1)


def _hgrn_cols(row_block):
    return [pl.BlockSpec((HB, 2 * HD), lambda j, c=base // (2 * HD) + p: (row_block(j), c))
            for base in (ZQH, ZFH, ZIH, ZGH) for p in range(2)]


NCH = HB // CHUNK


def _split3(x):
    hi = x.astype(BF16)
    r1 = x - hi.astype(F32)
    mid = r1.astype(BF16)
    return hi, mid, (r1 - mid.astype(F32)).astype(BF16)


def _blockdiag(lower):
    r = lax.broadcasted_iota(jnp.int32, (HB, HB), 0)
    c = lax.broadcasted_iota(jnp.int32, (HB, HB), 1)
    return (r // CHUNK == c // CHUNK) & ((c <= r) if lower else (c >= r))


def _chunk_sums(mask_bf16, x):
    return sum(_dot(mask_bf16, part, 1, 0) for part in _split3(x))


def _per_chunk_rows(x, row):
    w = x.shape[1]
    picked = x.reshape(NCH, CHUNK, w)[:, row:row + 1, :]
    return jnp.broadcast_to(picked, (NCH, CHUNK, w)).reshape(HB, w)


def _chunk_stack(x, chunk_of_row):
    return jnp.concatenate([jnp.where(chunk_of_row == c, x, jnp.zeros_like(x)) for c in range(NCH)], axis=1)


def _chunk_pick(x, chunk_of_row):
    w = x.shape[1] // NCH
    out = jnp.zeros((HB, w), x.dtype)
    for c in range(NCH):
        out = jnp.where(chunk_of_row == c, x[:, c * w:(c + 1) * w], out)
    return out


def _hgrn_local(q, f, kf, b):
    sq = _sig(q)
    qf = q * sq * (HD ** -0.5)
    b_mid = _per_chunk_rows(b, CHUNK // 2 - 1)
    b_last = _per_chunk_rows(b, CHUNK - 1)
    qm = qf * jnp.exp(b - b_mid)
    km = kf * jnp.exp(b_mid - b)
    kl = kf * jnp.exp(b_last - b)
    qb = qf * jnp.exp(b)
    return dict(sq=sq, b_mid=b_mid, b_last=b_last, qm=qm, km=km, kl=kl, qb=qb)


def _hgrn2_fwd(z, hgrn_lb, onorm, ymix, t):
    nb = t // HB

    def body(*refs):
        zq, zf, zi, zg = refs[0:2], refs[2:4], refs[4:6], refs[6:8]
        lb_ref, on_ref, _, y_ref, o_ref, sp_ref, st_ref = refs[8:]

        @pl.when(pl.program_id(0) == 0)
        def _():
            st_ref[...] = jnp.zeros_like(st_ref)

        lb_all = _lower_bound(lb_ref)
        gn = on_ref[...]
        low = _blockdiag(True)
        low_b = low.astype(BF16)
        chunk_of_row = lax.broadcasted_iota(jnp.int32, (HB, HD), 0) // CHUNK
        for p in range(2):
            lbp = lb_all[:, 2 * HD * p:2 * HD * (p + 1)]
            fp = lbp + (1.0 - lbp) * _sig(zf[p][...].astype(F32))
            bp = _chunk_sums(low_b, jnp.log(fp))
            for e in range(2):
                h, ls = 2 * p + e, slice(e * HD, (e + 1) * HD)
                f = fp[:, ls]
                w = _hgrn_local(zq[p][:, ls].astype(F32), f, 1.0 - f, bp[:, ls])
                iv = zi[p][:, ls].astype(BF16)
                a = jnp.where(low, _dot(w["qm"].astype(BF16), w["km"].astype(BF16), 1, 1), 0.0)
                o = _dot(a.astype(BF16), iv, 1, 0)
                u = _dot(iv, _chunk_stack(w["kl"].astype(BF16), chunk_of_row), 0, 0)
                decay = jnp.exp(w["b_last"])
                st = st_ref[h]
                states = []
                for c in range(NCH):
                    sp_ref[h, c] = st
                    states.append(st.astype(BF16))
                    st = st * decay[c * CHUNK:c * CHUNK + 1] + u[:, c * HD:(c + 1) * HD]
                st_ref[h] = st
                inter = _dot(w["qb"].astype(BF16), jnp.concatenate(states, axis=0), 1, 1)
                o = o + _chunk_pick(inter, chunk_of_row)
                hs = slice(h * HD, (h + 1) * HD)
                o_ref[:, hs] = o
                gg = zg[p][:, ls].astype(F32)
                y_ref[:, hs] = (o * _rstd(o) * gn * (gg * _sig(gg))).astype(BF16)

    return pl.pallas_call(
        body, name="hgrn_fwd",
        out_shape=(jax.ShapeDtypeStruct((t, D), BF16), jax.ShapeDtypeStruct((t, HG_W), F32),
                   jax.ShapeDtypeStruct((4, t // CHUNK, HD, HD), F32)),
        grid=(nb,),
        in_specs=_hgrn_cols(lambda j: j) + [pl.BlockSpec((2, HG_W), lambda j: (0, 0)),
                                            pl.BlockSpec((1, HD), lambda j: (0, 0)), ANY_SPEC],
        out_specs=(pl.BlockSpec((HB, HG_W), lambda j: (j, 1)),
                   pl.BlockSpec((HB, HG_W), lambda j: (j, 0)),
                   pl.BlockSpec((4, NCH, HD, HD), lambda j: (0, j, 0, 0))),
        scratch_shapes=[pltpu.VMEM((4, HD, HD), F32)],
        input_output_aliases={10: 0},
        compiler_params=_params(dimension_semantics=("arbitrary",)),
    )(*[z] * 8, hgrn_lb, onorm, ymix)


def _hgrn2_bwd(z, hgrn_lb, onorm, o_save, sprev, dymix, dza, t):
    nb = t // HB

    def body(*refs):
        zq, zf, zi, zg = refs[0:2], refs[2:4], refs[4:6], refs[6:8]
        (lb_ref, on_ref, o_ref, sp_ref, dy_ref, dqa_ref, first_ref, second_ref,
         dz_ref, dlb_ref, don_ref, dst_ref) = refs[8:]

        @pl.when(pl.program_id(0) == 0)
        def _():
            dst_ref[...] = jnp.zeros_like(dst_ref)
            dlb_ref[...] = jnp.zeros_like(dlb_ref)
            don_ref[...] = jnp.zeros_like(don_ref)

        dz_ref[:, 0:SWA_W] = dqa_ref[...]
        dz_ref[0:HB // 2, SWA_W:ZQH] = first_ref[...]
        dz_ref[HB // 2:HB, SWA_W:ZQH] = second_ref[...]
        lb_all = _lower_bound(lb_ref)
        gn = on_ref[...]
        low, upp = _blockdiag(True), _blockdiag(False)
        upp_b = upp.astype(BF16)
        low_b = low.astype(BF16)
        row = lax.broadcasted_iota(jnp.int32, (HB, HD), 0)
        chunk_of_row = row // CHUNK
        in_chunk = row % CHUNK
        for p in range(2):
            lbp = lb_all[:, 2 * HD * p:2 * HD * (p + 1)]
            sgp = _sig(zf[p][...].astype(F32))
            fp = lbp + (1.0 - lbp) * sgp
            bp = _chunk_sums(low_b, jnp.log(fp))
            db_pair, dkf_pair = [], []
            for e in range(2):
                h, ls, hs = 2 * p + e, slice(e * HD, (e + 1) * HD), slice((2 * p + e) * HD, (2 * p + e + 1) * HD)
                f = fp[:, ls]
                q = zq[p][:, ls].astype(F32)
                w = _hgrn_local(q, f, 1.0 - f, bp[:, ls])
                iv = zi[p][:, ls].astype(BF16)
                gg = zg[p][:, ls].astype(F32)
                o = o_ref[:, hs]
                dout = dy_ref[:, hs].astype(F32)
                sgg = _sig(gg)
                r = _rstd(o)
                oh = o * r
                dyn = dout * (gg * sgg)
                dz_ref[:, ZGH + h * HD:ZGH + (h + 1) * HD] = (
                    dout * oh * gn * (sgg * (1.0 + gg * (1.0 - sgg)))).astype(BF16)
                don_ref[...] += _rowsum8(dyn * oh)
                do = _norm_bwd(oh, r, dyn * gn).astype(BF16)
                qm, km, kl, qb = (w[n].astype(BF16) for n in ("qm", "km", "kl", "qb"))
                decay = jnp.exp(w["b_last"])
                grads_in = _dot(do, _chunk_stack(qb, chunk_of_row), 0, 0)
                dst = dst_ref[h]
                dstn, dd_rows = [None] * NCH, [None] * NCH
                for c in reversed(range(NCH)):
                    dstn[c] = dst.astype(BF16)
                    dd_rows[c] = jnp.sum(dst * sp_ref[h, c], axis=0, keepdims=True)
                    dst = dst * decay[c * CHUNK:c * CHUNK + 1] + grads_in[:, c * HD:(c + 1) * HD]
                dst_ref[h] = dst
                states = jnp.concatenate([sp_ref[h, c].astype(BF16) for c in range(NCH)], axis=0)
                dstn_all = jnp.concatenate(dstn, axis=0)
                dqb = _dot(_chunk_stack(do, chunk_of_row), states, 1, 0)
                at = jnp.where(upp, _dot(km, qm, 1, 1), 0.0)
                di = _dot(at.astype(BF16), do, 1, 0) + _chunk_pick(_dot(kl, dstn_all, 1, 1), chunk_of_row)
                dz_ref[:, ZIH + h * HD:ZIH + (h + 1) * HD] = di.astype(BF16)
                dkl = _dot(_chunk_stack(iv, chunk_of_row), dstn_all, 1, 0)
                da = jnp.where(low, _dot(do, iv, 1, 1), 0.0).astype(BF16)
                dat = jnp.where(upp, _dot(iv, do, 1, 1), 0.0).astype(BF16)
                dqm = _dot(da, km, 1, 0)
                dkm = _dot(dat, qm, 1, 0)
                b = bp[:, ls]
                e1, e2 = jnp.exp(b - w["b_mid"]), jnp.exp(w["b_mid"] - b)
                e3, e4 = jnp.exp(w["b_last"] - b), jnp.exp(b)
                dqf = dqm * e1 + dqb * e4
                dkf_pair.append(dkm * e2 + dkl * e3)
                t_qm, t_km, t_kl = dqm * w["qm"], dkm * w["km"], dkl * w["kl"]
                db = t_qm - t_km - t_kl + dqb * w["qb"]
                db_mid = jnp.sum((t_km - t_qm).reshape(NCH, CHUNK, HD), axis=1, keepdims=True)
                db_last = jnp.sum(t_kl.reshape(NCH, CHUNK, HD), axis=1, keepdims=True)
                db_last = db_last + jnp.stack(dd_rows, axis=0) * jnp.exp(
                    bp[:, ls].reshape(NCH, CHUNK, HD)[:, CHUNK - 1:CHUNK, :])
                spread = lambda v: jnp.broadcast_to(v, (NCH, CHUNK, HD)).reshape(HB, HD)
                db = (db + jnp.where(in_chunk == CHUNK // 2 - 1, spread(db_mid), 0.0)
                      + jnp.where(in_chunk == CHUNK - 1, spread(db_last), 0.0))
                db_pair.append(db)
                sq = w["sq"]
                dz_ref[:, ZQH + h * HD:ZQH + (h + 1) * HD] = (
                    dqf * (HD ** -0.5) * (sq * (1.0 + q * (1.0 - sq)))).astype(BF16)
            dlogf = _chunk_sums(upp_b, jnp.concatenate(db_pair, axis=1))
            dfv = dlogf / fp - jnp.concatenate(dkf_pair, axis=1)
            dz_ref[:, ZFH + 2 * HD * p:ZFH + 2 * HD * (p + 1)] = (dfv * (1.0 - lbp) * sgp * (1.0 - sgp)).astype(BF16)
            dlb_ref[:, 2 * HD * p:2 * HD * (p + 1)] += _rowsum8(dfv * (1.0 - sgp))

    rev = lambda j: nb - 1 - j
    return pl.pallas_call(
        body, name="hgrn_bwd",
        out_shape=(jax.ShapeDtypeStruct((t, D_IN), BF16), jax.ShapeDtypeStruct((8, HG_W), F32),
                   jax.ShapeDtypeStruct((8, HD), F32)),
        grid=(nb,),
        in_specs=_hgrn_cols(rev) + [pl.BlockSpec((2, HG_W), lambda j: (0, 0)), pl.BlockSpec((1, HD), lambda j: (0, 0)),
                                    pl.BlockSpec((HB, HG_W), lambda j: (rev(j), 0)),
                                    pl.BlockSpec((4, NCH, HD, HD), lambda j: (0, rev(j), 0, 0)),
                                    pl.BlockSpec((HB, HG_W), lambda j: (rev(j), 1)),
                                    pl.BlockSpec((HB, SWA_W), lambda j: (rev(j), 0)),
                                    pl.BlockSpec((HB // 2, 2 * KV_W), lambda j: (rev(j), 0)),
                                    pl.BlockSpec((HB // 2, 2 * KV_W), lambda j: (rev(j), 0))],
        out_specs=(pl.BlockSpec((HB, D_IN), lambda j: (rev(j), 0)), pl.BlockSpec((8, HG_W), lambda j: (0, 0)),
                   pl.BlockSpec((8, HD), lambda j: (0, 0))),
        scratch_shapes=[pltpu.VMEM((4, HD, HD), F32)],
        compiler_params=_params(dimension_semantics=("arbitrary",)),
    )(*[z] * 8, hgrn_lb, onorm, o_save, sprev, dymix, *dza)


XB = 512


def _xattn_fwd(q, k, v, t):
    tb = min(XB, t)

    def body(q_ref, k_ref, v_ref, o_ref):
        for h in range(XH):
            cols = slice(XD * h, XD * (h + 1))
            s = _dot(q_ref[:, cols], k_ref[:, cols], 1, 1) * (XD ** -0.5)
            p = jnp.exp(s - jnp.max(s, axis=-1, keepdims=True))
            l = jnp.sum(p, axis=-1, keepdims=True)
            o_ref[:, cols] = (_dot(p.astype(BF16), v_ref[:, cols], 1, 0) * (1.0 / l)).astype(BF16)

    row = pl.BlockSpec((tb, D), lambda i: (i, 0))
    mem = pl.BlockSpec(k.shape, lambda i: (0, 0))
    return pl.pallas_call(
        body, name="xattn_fwd", out_shape=jax.ShapeDtypeStruct((t, D), BF16), grid=(t // tb,),
        in_specs=[row, mem, mem], out_specs=row, compiler_params=_params(),
    )(q, k, v)


def _xattn_bwd(q, k, v, do, t):
    tb = min(XB, t)

    def body(q_ref, k_ref, v_ref, do_ref, dq_ref, dk_ref, dv_ref):
        @pl.when(pl.program_id(0) == 0)
        def _():
            dk_ref[...] = jnp.zeros_like(dk_ref)
            dv_ref[...] = jnp.zeros_like(dv_ref)

        for h in range(XH):
            cols = slice(XD * h, XD * (h + 1))
            qh, kh, vh, doh = q_ref[:, cols], k_ref[:, cols], v_ref[:, cols], do_ref[:, cols]
            s = _dot(qh, kh, 1, 1) * (XD ** -0.5)
            p = jnp.exp(s - jnp.max(s, axis=-1, keepdims=True))
            p = p * (1.0 / jnp.sum(p, axis=-1, keepdims=True))
            dp = _dot(doh, vh, 1, 1)
            ds = (p * (dp - jnp.sum(p * dp, axis=-1, keepdims=True)) * (XD ** -0.5)).astype(BF16)
            dq_ref[:, cols] = _dot(ds, kh, 1, 0).astype(BF16)
            dk_ref[:, cols] += _dot(ds, qh, 0, 0)
            dv_ref[:, cols] += _dot(p.astype(BF16), doh, 0, 0)

    row = pl.BlockSpec((tb, D), lambda i: (i, 0))
    mem = pl.BlockSpec(k.shape, lambda i: (0, 0))
    return pl.pallas_call(
        body, name="xattn_bwd",
        out_shape=(jax.ShapeDtypeStruct((t, D), BF16), jax.ShapeDtypeStruct(k.shape, F32),
                   jax.ShapeDtypeStruct(k.shape, F32)),
        grid=(t // tb,), in_specs=[row, mem, mem, row], out_specs=(row, mem, mem),
        compiler_params=_params(dimension_semantics=("arbitrary",)),
    )(q, k, v, do)


def _mem_gain_bwd(dm, mem, *, name):
    def body(dm_ref, m_ref, dg_ref):
        m_ = m_ref[...]
        dg_ref[...] = _rowsum8(dm_ref[...] * (m_ * _rstd(m_)))

    return pl.pallas_call(body, name=name, out_shape=jax.ShapeDtypeStruct((8, D), F32),
                          compiler_params=_params())(dm, mem)


FM, FN = 512, 1408


def _ffn_up(u, wgt, wut, t):
    tm = min(FM, t)

    def body(u_ref, wg_ref, wu_ref, g_ref, up_ref, a_ref):
        u_ = u_ref[...]
        g = _dot(u_, wg_ref[...], 1, 1)
        up = _dot(u_, wu_ref[...], 1, 1)
        g_ref[...] = g.astype(BF16)
        up_ref[...] = up.astype(BF16)
        a_ref[...] = (g * _sig(g) * up).astype(BF16)

    w = pl.BlockSpec((FN, D), lambda j, i: (j, 0))
    o = pl.BlockSpec((tm, FN), lambda j, i: (i, j))
    return pl.pallas_call(
        body, name="ffn_up", out_shape=(jax.ShapeDtypeStruct((t, D_FF), BF16),) * 3,
        grid=(D_FF // FN, t // tm), in_specs=[pl.BlockSpec((tm, D), lambda j, i: (i, 0)), w, w],
        out_specs=(o, o, o), compiler_params=_params(),
    )(u, wgt, wut)


def _ffn_down_bwd(dy, wd, gate, up, t, dep=None):
    tm = min(FM, t)
    deps = [] if dep is None else [dep]

    def body(dy_ref, w_ref, g_ref, up_ref, *rest):
        dg_ref, dup_ref = rest[len(deps):]
        da = _dot(dy_ref[...], w_ref[...], 1, 1)
        g = g_ref[...].astype(F32)
        sg = _sig(g)
        dup_ref[...] = (da * g * sg).astype(BF16)
        dg_ref[...] = (da * up_ref[...].astype(F32) * (sg * (1.0 + g * (1.0 - sg)))).astype(BF16)

    o = pl.BlockSpec((tm, FN), lambda j, i: (i, j))
    return pl.pallas_call(
        body, name="ffn_down_bwd", out_shape=(jax.ShapeDtypeStruct((t, D_FF), BF16),) * 2,
        grid=(D_FF // FN, t // tm),
        in_specs=[pl.BlockSpec((tm, D), lambda j, i: (i, 0)), pl.BlockSpec((FN, D), lambda j, i: (j, 0)), o, o]
        + [ANY_SPEC] * len(deps),
        out_specs=(o, o), compiler_params=_params(),
    )(dy, wd, gate, up, *deps)


def _local_step(x, mem, target, fetch, sm, emit=None):
    t = x.shape[0]
    w, gw = {}, {}

    def out(key, g):
        gw[key] = g
        return None if emit is None else emit(key, g)
    u1 = _prenorm(x, sm["g_mix_pre"], name="prenorm_mix")
    w["winT"] = fetch("winT", u1)
    z = _mm(u1, w["winT"], tb=True, out_dtype=BF16, tm=1024, tn=1408, name="mm_z", n_outer=True)
    ymix, lse = _swa_fwd(z, sm["sinks"], t)
    ymix, o_h, sprev = _hgrn2_fwd(z, sm["hgrn_lb"], sm["hgrn_onorm"], ymix, t)
    w["wout"] = fetch("wout", ymix)
    y1 = _mm(ymix, w["wout"], out_dtype=BF16, tm=1024, tn=1024, name="mm_y1")
    h1, u2 = _post_pre(x, y1, sm["g_mix_post"], sm["g_x_pre"], name="post_mix")
    mn = _prenorm(mem, sm["g_mem"], name="prenorm_mem")
    for key in ("wq", "wk", "wv"):
        w[key] = fetch(key, u2)
    qx = _mm(u2, w["wq"], out_dtype=BF16, tm=1024, tn=1024, name="mm_qx")
    kx = _mm(mn, w["wk"], out_dtype=BF16, tm=1024, tn=1024, name="mm_kx")
    vx = _mm(mn, w["wv"], out_dtype=BF16, tm=1024, tn=1024, name="mm_vx")
    ox = _xattn_fwd(qx, kx, vx, t)
    w["wo"] = fetch("wo", ox)
    y2 = _mm(ox, w["wo"], out_dtype=BF16, tm=1024, tn=1024, name="mm_y2")
    h2, u3 = _post_pre(h1, y2, sm["g_x_post"], sm["g_ffn_pre"], name="post_x")
    w["wgT"], w["wuT"] = fetch("wgT", u3), fetch("wuT", u3)
    gate, up, act = _ffn_up(u3, w["wgT"], w["wuT"], t)
    w["wd"] = fetch("wd", act)
    y3 = _mm(act, w["wd"], out_dtype=BF16, tm=1024, tn=1024, name="mm_y3")
    sq, dh3, dy3, dg_ffn_post = _final_loss(h2, y3, sm["g_ffn_post"], target, name="final_loss")
    dep = out("wd", _mm(act, dy3, ta=True, out_dtype=BF16, tm=1408, tn=1024, name="mm_gwd"))
    dgate, dup = _ffn_down_bwd(dy3, w["wd"], gate, up, t, dep=dep)
    dep = out("wgT", _mm(dgate, u3, ta=True, out_dtype=BF16, tm=1408, tn=1024, name="mm_gwg"))
    dep = out("wuT", _mm(dup, u3, ta=True, out_dtype=BF16, tm=1408, tn=1024, name="mm_gwu", dep=dep))
    du3 = _mm2(dgate, w["wgT"], dup, w["wuT"], out_dtype=BF16, tm=512, name="mm_du3", dep=dep)
    dh2, dy2, dg_ffn_pre, dg_x_post = _post_pre_bwd(dh3, du3, h2, y2, sm["g_x_post"], sm["g_ffn_pre"], name="post_x_bwd")
    dep = out("wo", _mm(ox, dy2, ta=True, out_dtype=BF16, tm=512, tn=1024, name="mm_gwo"))
    dox = _mm(dy2, w["wo"], tb=True, out_dtype=BF16, tm=1024, tn=1024, name="mm_dox", dep=dep)
    dqx, dkx, dvx = _xattn_bwd(qx, kx, vx, dox, t)
    dep = out("wq", _mm(u2, dqx, ta=True, out_dtype=BF16, tm=512, tn=1024, name="mm_gwq"))
    dep = out("wk", _mm(mn, dkx, ta=True, out_dtype=BF16, tm=1024, tn=1024, name="mm_gwk", dep=dep))
    dep = out("wv", _mm(mn, dvx, ta=True, out_dtype=BF16, tm=1024, tn=1024, name="mm_gwv", dep=dep))
    du2 = _mm(dqx, w["wq"], tb=True, out_dtype=BF16, tm=1024, tn=1024, name="mm_du2", dep=dep)
    dmn = _mm2(dkx, w["wk"], dvx, w["wv"], tb=True, out_dtype=F32, tm=256, name="mm_dmn")
    dg_mem = _mem_gain_bwd(dmn, mem, name="mem_gain_bwd")
    dh1, dy1, dg_x_pre, dg_mix_post = _post_pre_bwd(dh2, du2, h1, y1, sm["g_mix_post"], sm["g_x_pre"], name="post_mix_bwd")
    dep = out("wout", _mm(ymix, dy1, ta=True, out_dtype=BF16, tm=512, tn=1024, name="mm_gwout"))
    dymix = _mm(dy1, w["wout"], tb=True, out_dtype=BF16, tm=1024, tn=1024, name="mm_dymix", dep=dep)
    *dza, dsinks = _swa_bwd(z, sm["sinks"], ymix, lse, dymix, t)
    dz, dlb, donorm = _hgrn2_bwd(z, sm["hgrn_lb"], sm["hgrn_onorm"], o_h, sprev, dymix, dza, t)
    dep = out("winT", _mm(dz, u1, ta=True, out_dtype=BF16, tm=1408, tn=1024, name="mm_gwin"))
    du1 = _mm(dz, w["winT"], out_dtype=BF16, tm=512, tn=1024, name="mm_du1", dep=dep)
    grad_x, dg_mix_pre = _pre_bwd(dh1, du1, x, sm["g_mix_pre"], name="pre_mix_bwd")
    parts = dict(g_mix_pre=dg_mix_pre, g_mix_post=dg_mix_post, g_mem=dg_mem, g_x_pre=dg_x_pre,
                 g_x_post=dg_x_post, g_ffn_pre=dg_ffn_pre, g_ffn_post=dg_ffn_post,
                 hgrn_onorm=donorm, hgrn_lb=dlb, sinks=dsinks, sq=sq)
    return grad_x, gw, parts


def _position():
    return lax.axis_index("x"), lax.axis_index("y"), lax.axis_index("c")


def _peer(pos, k):
    x, y, c = pos
    return (1 - x if k & 4 else x, 1 - y if k & 2 else y, 1 - c if k & 1 else c)


def _linear(pos):
    x, y, c = pos
    return 4 * x + 2 * y + c


HBM_SPEC = pl.BlockSpec(memory_space=pltpu.HBM)
SEM_SPEC = pl.BlockSpec(memory_space=pltpu.SEMAPHORE)
DATAFLOW = pltpu.SideEffectType.DATAFLOW_SIDE_EFFECTING
SEND_ORDER = (1, 2, 4, 3, 5, 6, 7)


def _in_hbm(a):
    return pltpu.with_memory_space_constraint(a, pltpu.HBM)


def _prepare_weights(shards):
    n = len(shards)

    def body(*refs):
        ins, outs, lands, sem = refs[:n], refs[n:2 * n], refs[2 * n:3 * n], refs[3 * n]
        me_lin = _linear(_position())
        copies = []
        for a in range(n):
            r = ins[a].shape[0]
            outs[a][...] = ins[a][...].astype(BF16)
            copies.append(pltpu.make_async_copy(outs[a], lands[a].at[pl.ds(me_lin * r, r), :], sem.at[a]))
            copies[-1].start()
        for cp in copies:
            cp.wait()

    vmem = pl.BlockSpec(memory_space=pltpu.VMEM)
    res = pl.pallas_call(
        body, name="prepare_weights",
        out_shape=tuple(jax.ShapeDtypeStruct(s.shape, BF16) for s in shards)
        + tuple(jax.ShapeDtypeStruct((N_DEV * s.shape[0], s.shape[1]), BF16) for s in shards),
        in_specs=[vmem] * n, out_specs=tuple([vmem] * n + [ANY_SPEC] * n),
        scratch_shapes=[pltpu.SemaphoreType.DMA((n,))], compiler_params=_params(),
    )(*shards)
    return res[:n], res[n:]


def _gather_start(shards, lands):
    n = len(shards)
    rows = [s.shape[0] for s in shards]

    def body(*refs):
        srcs, land = refs[:n], refs[n:2 * n]
        send_sems, recv_sems = refs[2 * n:3 * n], refs[3 * n:4 * n]
        me = _position()
        for a in range(n):
            mine = land[a].at[pl.ds(_linear(me) * rows[a], rows[a]), :]
            for k in SEND_ORDER:
                pltpu.make_async_remote_copy(
                    src_ref=srcs[a], dst_ref=mine, send_sem=send_sems[a].at[k - 1], recv_sem=recv_sems[a].at[k - 1],
                    device_id=_peer(me, k), device_id_type=MESH).start()

    sems = tuple(pltpu.SemaphoreType.DMA((N_DEV - 1,)) for _ in range(2 * n))
    res = pl.pallas_call(
        body, name="weights_send",
        out_shape=sems + tuple(pltpu.HBM(s.shape, s.dtype) for s in shards)
        + tuple(pltpu.HBM(l.shape, l.dtype) for l in lands),
        in_specs=(HBM_SPEC,) * (2 * n), out_specs=(SEM_SPEC,) * (2 * n) + (HBM_SPEC,) * (2 * n),
        input_output_aliases={i: 2 * n + i for i in range(2 * n)},
        compiler_params=pltpu.CompilerParams(has_side_effects=DATAFLOW),
    )(*[_in_hbm(s) for s in shards], *[_in_hbm(l) for l in lands])
    return [(res[a], res[n + a], res[2 * n + a], res[3 * n + a]) for a in range(n)]


def _gather_wait(send_sems, recv_sems, shard_thru, land_thru, after, *, name):
    r = shard_thru.shape[0]

    def body(src_ref, land_ref, send_sems, recv_sems, after_ref, src_dead, got_ref):
        del after_ref, src_dead, got_ref
        me = _position()
        for k in SEND_ORDER:
            peer = _peer(me, k)
            copy = pltpu.make_async_remote_copy(
                src_ref=src_ref, dst_ref=land_ref.at[pl.ds(_linear(peer) * r, r), :],
                send_sem=send_sems.at[k - 1], recv_sem=recv_sems.at[k - 1],
                device_id=peer, device_id_type=MESH)
            copy.wait_send()
            copy.wait_recv()

    return pl.pallas_call(
        body, name=name,
        out_shape=(pltpu.HBM(shard_thru.shape, shard_thru.dtype), pltpu.HBM(land_thru.shape, land_thru.dtype)),
        in_specs=(HBM_SPEC, HBM_SPEC, SEM_SPEC, SEM_SPEC, ANY_SPEC),
        out_specs=(HBM_SPEC, HBM_SPEC), input_output_aliases={0: 0, 1: 1},
        compiler_params=pltpu.CompilerParams(has_side_effects=DATAFLOW),
    )(shard_thru, land_thru, send_sems, recv_sems, after)[1]


def _exchange_start(gs, *, name):
    n = len(gs)
    rows = [g.shape[0] // N_DEV for g in gs]
    lands = [lax.empty((N_DEV - 1, r, g.shape[1]), g.dtype) for g, r in zip(gs, rows)]

    def body(*refs):
        g_refs, land_refs = refs[:n], refs[n:2 * n]
        send_sems, recv_sems = refs[2 * n:3 * n], refs[3 * n:4 * n]
        me = _position()
        for a in range(n):
            for k in SEND_ORDER:
                peer = _peer(me, k)
                pltpu.make_async_remote_copy(
                    src_ref=g_refs[a].at[pl.ds(_linear(peer) * rows[a], rows[a]), :],
                    dst_ref=land_refs[a].at[k - 1],
                    send_sem=send_sems[a].at[k - 1], recv_sem=recv_sems[a].at[k - 1],
                    device_id=peer, device_id_type=MESH).start()

    res = pl.pallas_call(
        body, name=name,
        out_shape=tuple(pltpu.SemaphoreType.DMA((N_DEV - 1,)) for _ in range(2 * n))
        + tuple(pltpu.HBM(a.shape, a.dtype) for a in gs + lands),
        in_specs=(HBM_SPEC,) * (2 * n), out_specs=(SEM_SPEC,) * (2 * n) + (HBM_SPEC,) * (2 * n),
        input_output_aliases={i: 2 * n + i for i in range(2 * n)},
        compiler_params=pltpu.CompilerParams(has_side_effects=DATAFLOW),
    )(*[_in_hbm(a) for a in gs + lands])
    return [(res[a], res[n + a], res[2 * n + a], res[3 * n + a]) for a in range(n)]


def _exchange_wait(send_sems, recv_sems, g_thru, land_thru, after, *, name):
    r = land_thru.shape[1]

    def body(g_ref, land_ref, send_sems, recv_sems, after_ref, g_dead, got_ref):
        del after_ref, g_dead, got_ref
        me = _position()
        for k in SEND_ORDER:
            peer = _peer(me, k)
            copy = pltpu.make_async_remote_copy(
                src_ref=g_ref.at[pl.ds(_linear(peer) * r, r), :], dst_ref=land_ref.at[k - 1],
                send_sem=send_sems.at[k - 1], recv_sem=recv_sems.at[k - 1],
                device_id=peer, device_id_type=MESH)
            copy.wait_send()
            copy.wait_recv()

    return pl.pallas_call(
        body, name=name,
        out_shape=(pltpu.HBM(g_thru.shape, g_thru.dtype), pltpu.HBM(land_thru.shape, land_thru.dtype)),
        in_specs=(HBM_SPEC, HBM_SPEC, SEM_SPEC, SEM_SPEC, pl.BlockSpec(memory_space=pl.ANY)),
        out_specs=(HBM_SPEC, HBM_SPEC), input_output_aliases={0: 0, 1: 1},
        compiler_params=pltpu.CompilerParams(has_side_effects=DATAFLOW),
    )(g_thru, land_thru, send_sems, recv_sems, after)


def _adamw_math(w, g, m, v):
    m = B1 * m + (1.0 - B1) * g
    v = B2 * v + (1.0 - B2) * (g * g)
    delta = -LR * ((m / C1) / (jnp.sqrt(v / C2) + AEPS) + WD * w)
    return delta, m, v


def _sum_adamw(g_all, land, w, m, v, *, name):
    r, width = w.shape
    rb = r // 2

    def body(all_ref, land_ref, w_ref, m_ref, v_ref, g_ref, d_ref, nm_ref, nv_ref, own_ref, sem):
        first = _linear(_position()) * r + pl.program_id(0) * rb
        mine = pltpu.make_async_copy(all_ref.at[pl.ds(pl.multiple_of(first, 16), rb), :], own_ref, sem)
        mine.start()
        g = land_ref[0].astype(F32)
        for s in range(1, N_DEV - 1):
            g = g + land_ref[s].astype(F32)
        mine.wait()
        g = own_ref[...].astype(F32) + g
        g_ref[...] = g
        d_ref[...], nm_ref[...], nv_ref[...] = _adamw_math(w_ref[...], g, m_ref[...], v_ref[...])

    rows = pl.BlockSpec((rb, width), lambda i: (i, 0))
    return pl.pallas_call(
        body, name=name, out_shape=(jax.ShapeDtypeStruct(w.shape, F32),) * 4, grid=(r // rb,),
        in_specs=[ANY_SPEC, pl.BlockSpec((N_DEV - 1, rb, width), lambda i: (0, i, 0)), rows, rows, rows],
        out_specs=(rows,) * 4,
        scratch_shapes=[pltpu.VMEM((rb, width), BF16), pltpu.SemaphoreType.DMA(())],
        compiler_params=_params(),
    )(g_all, land, w, m, v)


SMALL = ("g_mix_pre", "g_mix_post", "g_mem", "g_x_pre", "g_x_post", "g_ffn_pre", "g_ffn_post",
         "hgrn_onorm", "hgrn_lb", "sinks")
SMALL_W = dict(hgrn_onorm=HD, hgrn_lb=HG_W, sinks=8)
SQ_ROW = len(SMALL)
PACK_ROWS = 16


def _small_allreduce(parts):
    ns = len(SMALL)

    def body(*refs):
        part, tot_ref = refs[:ns + 1], refs[ns + 1]
        gath, send_sems, recv_sems = refs[ns + 2:]
        me = _position()
        mine = gath.at[_linear(me)]
        mine[...] = jnp.zeros((PACK_ROWS, D), F32)
        for r, name in enumerate(SMALL):
            wd = SMALL_W.get(name, D)
            mine[r:r + 1, 0:wd] = jnp.sum(part[r][...], axis=0, keepdims=True)[:, 0:wd]
        sq = jnp.sum(part[ns][...]) * (0.5 / D)
        mine[SQ_ROW:SQ_ROW + 1, :] = jnp.full((1, D), sq, F32)

        def copy(k):
            peer = _peer(me, k)
            return pltpu.make_async_remote_copy(
                src_ref=mine, dst_ref=mine, send_sem=send_sems.at[k - 1], recv_sem=recv_sems.at[k - 1],
                device_id=peer, device_id_type=MESH)

        def arrival(k):
            slot = gath.at[_linear(_peer(me, k))]
            return pltpu.make_async_remote_copy(
                src_ref=slot, dst_ref=slot, send_sem=send_sems.at[k - 1], recv_sem=recv_sems.at[k - 1],
                device_id=_peer(me, k), device_id_type=MESH)

        sent = [copy(k) for k in range(1, 8)]
        for cp in sent:
            cp.start()
        for k in range(1, 8):
            arrival(k).wait_recv()
        for cp in sent:
            cp.wait_send()
        tot = gath[0]
        for s in range(1, N_DEV):
            tot = tot + gath[s]
        tot_ref[...] = tot

    return pl.pallas_call(
        body, name="small_allreduce", out_shape=jax.ShapeDtypeStruct((PACK_ROWS, D), F32),
        scratch_shapes=[pltpu.VMEM((N_DEV, PACK_ROWS, D), F32), pltpu.SemaphoreType.DMA((7,)),
                        pltpu.SemaphoreType.DMA((7,))],
        compiler_params=_params(has_side_effects=True),
    )(*[parts[n] for n in SMALL], parts["sq"])


def _small_update(tot, sm, m_sm, v_sm):
    ns = len(SMALL)

    def body(*refs):
        tot = refs[0][...]
        w_refs, m_refs, v_refs = refs[1:ns + 1], refs[ns + 1:2 * ns + 1], refs[2 * ns + 1:3 * ns + 1]
        outs = refs[3 * ns + 1:]
        loss_ref = outs[0]
        g_out, d_out = outs[1:ns + 1], outs[ns + 1:2 * ns + 1]
        nm_out, nv_out = outs[2 * ns + 1:3 * ns + 1], outs[3 * ns + 1:4 * ns + 1]
        loss_ref[...] = tot[SQ_ROW:SQ_ROW + 1, 0:1]
        for r, name in enumerate(SMALL):
            wd = SMALL_W.get(name, D)
            g = tot[r:r + 1, 0:wd]
            w = w_refs[r][...]
            if name == "hgrn_lb":
                mx = jnp.maximum(w[0:1], w[1:2])
                e0, e1 = jnp.exp(w[0:1] - mx), jnp.exp(w[1:2] - mx)
                lb0 = e0 / (e0 + e1)
                g0 = g * lb0 * (1.0 - lb0)
                for i, gi in enumerate((g0, -g0)):
                    d, nm, nv = _adamw_math(w[i:i + 1], gi, m_refs[r][i:i + 1, :], v_refs[r][i:i + 1, :])
                    g_out[r][i:i + 1, :] = gi
                    d_out[r][i:i + 1, :], nm_out[r][i:i + 1, :], nv_out[r][i:i + 1, :] = d, nm, nv
            else:
                d, nm, nv = _adamw_math(w, g, m_refs[r][...], v_refs[r][...])
                g_out[r][...] = g
                d_out[r][...], nm_out[r][...], nv_out[r][...] = d, nm, nv

    shapes = [jax.ShapeDtypeStruct(sm[n].shape, F32) for n in SMALL]
    res = pl.pallas_call(
        body, name="small_update", out_shape=tuple([jax.ShapeDtypeStruct((1, 1), F32)] + shapes * 4),
        compiler_params=_params(),
    )(tot, *[sm[n] for n in SMALL], *[m_sm[n] for n in SMALL], *[v_sm[n] for n in SMALL])
    groups = [dict(zip(SMALL, res[1 + i * ns:1 + (i + 1) * ns])) for i in range(4)]
    return res[0], groups[0], groups[1], groups[2], groups[3]


BIG = ("w_in", "w_gate", "w_up", "w_down", "w_out", "wq_x", "wk_x", "wv_x", "wo_x")
BIG_KEY = dict(w_in="winT", w_gate="wgT", w_up="wuT", w_down="wd", w_out="wout", wq_x="wq", wk_x="wk",
               wv_x="wv", wo_x="wo")
TRANSPOSED = ("w_in", "w_gate", "w_up")
WEIGHTS = ("w_in", "sinks", "hgrn_lb", "hgrn_onorm", "w_out", "g_mix_pre", "g_mix_post", "g_mem", "g_x_pre",
           "g_x_post", "wq_x", "wk_x", "wv_x", "wo_x", "g_ffn_pre", "g_ffn_post", "w_gate", "w_up", "w_down")


def kernel(x, mem, w_in, sinks, hgrn_lb, hgrn_onorm, w_out, g_mix_pre, g_mix_post, g_mem, g_x_pre, g_x_post, wq_x, wk_x, wv_x, wo_x, g_ffn_pre, g_ffn_post, w_gate, w_up, w_down, loss_target, m_w_in, m_sinks, m_hgrn_lb, m_hgrn_onorm, m_w_out, m_g_mix_pre, m_g_mix_post, m_g_mem, m_g_x_pre, m_g_x_post, m_wq_x, m_wk_x, m_wv_x, m_wo_x, m_g_ffn_pre, m_g_ffn_post, m_w_gate, m_w_up, m_w_down, v_w_in, v_sinks, v_hgrn_lb, v_hgrn_onorm, v_w_out, v_g_mix_pre, v_g_mix_post, v_g_mem, v_g_x_pre, v_g_x_post, v_wq_x, v_wk_x, v_wv_x, v_wo_x, v_g_ffn_pre, v_g_ffn_post, v_w_gate, v_w_up, v_w_down):
    given = dict(locals())
    wts = {n: given[n] for n in WEIGHTS}
    ms = {n: given["m_" + n] for n in WEIGHTS}
    vs = {n: given["v_" + n] for n in WEIGHTS}

    def mat(a, name):
        a = a[0]
        return a.T if name in TRANSPOSED else a

    order = ("w_in", "w_out", "wq_x", "wk_x", "wv_x", "wo_x", "w_gate", "w_up", "w_down")
    flying = dict(zip(order, _gather_start(*_prepare_weights([mat(wts[n], n) for n in order]))))
    name_of = {k: n for n, k in BIG_KEY.items()}

    def fetch(key, after):
        return _gather_wait(*flying[name_of[key]], after, name="weights_recv_" + name_of[key])

    sm = {n: wts[n] for n in SMALL}
    started, held = {}, {}
    send_with = {"wgT": ("wgT", "wuT"), "wuT": ("wgT", "wuT"), "wq": ("wq", "wk", "wv"), "wk": ("wq", "wk", "wv"),
                 "wv": ("wq", "wk", "wv")}

    def emit(key, g):
        held[key] = g
        group = send_with.get(key, (key,))
        if key != group[-1]:
            return None
        flights = _exchange_start([held[k] for k in group], name="grad_send_" + name_of[group[0]])
        started.update({name_of[k]: f for k, f in zip(group, flights)})
        return flights[-1][2]

    grad_x, _, parts = _local_step(x[0], mem[0], loss_target[0], fetch, sm, emit)
    grads, deltas, new_m, new_v = {}, {}, {}, {}
    after = grad_x
    for n in ("w_down", "w_gate", "w_up", "wo_x", "wq_x", "wk_x", "wv_x", "w_out", "w_in"):
        g_all, land = _exchange_wait(*started[n], after, name="grad_recv_" + n)
        res = _sum_adamw(g_all, land, mat(wts[n], n), mat(ms[n], n), mat(vs[n], n), name="adamw_" + n)
        after = res[1]
        if n in TRANSPOSED:
            res = [a.T for a in res]
        grads[n], deltas[n], new_m[n], new_v[n] = [a[None] for a in res]
    loss, g_s, d_s, m_s, v_s = _small_update(_small_allreduce(parts), sm, {n: ms[n] for n in SMALL},
                                             {n: vs[n] for n in SMALL})
    grads.update(g_s), deltas.update(d_s), new_m.update(m_s), new_v.update(v_s)
    return (loss[0, 0], grad_x[None], *[grads[n] for n in WEIGHTS], *[deltas[n] for n in WEIGHTS],
            *[new_m[n] for n in WEIGHTS], *[new_v[n] for n in WEIGHTS])
```

```python
import functools

import jax
import jax.numpy as jnp
from jax import lax
from jax.experimental import pallas as pl
from jax.experimental.pallas import tpu as pltpu

F32 = jnp.float32
BF16 = jnp.bfloat16

D = 1024
D_IN = 2816
D_FF = 2816
CHUNK = 64
SWA_W = 512
KV_W = 128
HG_W = 512
HD = 128
ZQH, ZFH, ZIH, ZGH = 768, 1280, 1792, 2304
XH, XD = 4, 256
EPS = 1e-6
NEG = -1e30
N_DEV = 8
MESH = pl.DeviceIdType.MESH

LR, B1, B2, AEPS, WD, STEP = 0.001, 0.9, 0.999, 1e-08, 0.01, 10
C1 = 1.0 - B1 ** STEP
C2 = 1.0 - B2 ** STEP

VMEM_LIMIT = 56 * 1024 * 1024


def _params(**kw):
    return pltpu.CompilerParams(vmem_limit_bytes=VMEM_LIMIT, **kw)


def _sig(x):
    return 1.0 / (1.0 + jnp.exp(-x))


def _rowsum8(x):
    r, w = x.shape
    return jnp.sum(x.reshape(r // 8, 8, w), axis=0)


def _dot(a, b, ca, cb, precision=None):
    return lax.dot_general(a, b, (((ca,), (cb,)), ((), ())), preferred_element_type=F32,
                           precision=precision)


ANY_SPEC = pl.BlockSpec(memory_space=pl.ANY)


def _mm(a, b, *, ta=False, tb=False, out_dtype, tm, tn, tk=None, name, dep=None, n_outer=False):
    m = a.shape[1] if ta else a.shape[0]
    k = a.shape[0] if ta else a.shape[1]
    n = b.shape[0] if tb else b.shape[1]
    tm, tn = min(tm, m), min(tn, n)
    tk = k if tk is None else min(tk, k)
    nk = k // tk
    assert m % tm == 0 and n % tn == 0 and k % tk == 0, (name, m, n, k, tm, tn, tk)
    ij = (lambda g0, g1: (g1, g0)) if n_outer else (lambda g0, g1: (g0, g1))
    a_spec = (pl.BlockSpec((tk, tm), lambda g0, g1, kk: (kk, ij(g0, g1)[0])) if ta
              else pl.BlockSpec((tm, tk), lambda g0, g1, kk: (ij(g0, g1)[0], kk)))
    b_spec = (pl.BlockSpec((tn, tk), lambda g0, g1, kk: (ij(g0, g1)[1], kk)) if tb
              else pl.BlockSpec((tk, tn), lambda g0, g1, kk: (kk, ij(g0, g1)[1])))
    ca, cb = (0 if ta else 1), (1 if tb else 0)

    deps = [] if dep is None else [dep]

    def body(a_ref, b_ref, *rest):
        o_ref, acc = rest[len(deps)], rest[len(deps) + 1:]
        p = _dot(a_ref[...].astype(BF16), b_ref[...].astype(BF16), ca, cb)
        if nk == 1:
            o_ref[...] = p.astype(out_dtype)
        else:
            acc_ref, = acc
            kk = pl.program_id(2)

            @pl.when(kk == 0)
            def _():
                acc_ref[...] = p

            @pl.when(kk > 0)
            def _():
                acc_ref[...] += p

            @pl.when(kk == nk - 1)
            def _():
                o_ref[...] = acc_ref[...].astype(out_dtype)

    return pl.pallas_call(
        body, name=name, out_shape=jax.ShapeDtypeStruct((m, n), out_dtype),
        grid=(n // tn, m // tm, nk) if n_outer else (m // tm, n // tn, nk),
        in_specs=[a_spec, b_spec] + [ANY_SPEC] * len(deps),
        out_specs=pl.BlockSpec((tm, tn), lambda g0, g1, kk: ij(g0, g1)),
        scratch_shapes=[pltpu.VMEM((tm, tn), F32)] if nk > 1 else [],
        compiler_params=_params(dimension_semantics=("parallel", "parallel", "arbitrary")),
    )(a, b, *deps)


def _mm2(a1, b1, a2, b2, *, tb=False, out_dtype, tm, name, dep=None):
    m, k = a1.shape
    n = b1.shape[0] if tb else b1.shape[1]
    tm = min(tm, m)
    assert m % tm == 0
    cb = 1 if tb else 0
    deps = [] if dep is None else [dep]

    def body(a1_ref, b1_ref, a2_ref, b2_ref, *rest):
        o_ref = rest[len(deps)]
        o_ref[...] = (_dot(a1_ref[...].astype(BF16), b1_ref[...], 1, cb)
                      + _dot(a2_ref[...].astype(BF16), b2_ref[...], 1, cb)).astype(out_dtype)

    a_spec = pl.BlockSpec((tm, k), lambda i: (i, 0))
    b_spec = pl.BlockSpec(b1.shape, lambda i: (0, 0))
    return pl.pallas_call(
        body, name=name, out_shape=jax.ShapeDtypeStruct((m, n), out_dtype),
        grid=(m // tm,), in_specs=[a_spec, b_spec, a_spec, b_spec] + [ANY_SPEC] * len(deps),
        out_specs=pl.BlockSpec((tm, n), lambda i: (i, 0)),
        compiler_params=_params(dimension_semantics=("parallel",)),
    )(a1, b1, a2, b2, *deps)


def _mm_rows(prods, rows_in, vecs_in, epilogue, outs, *, tm, name, dep=None):
    m = prods[0][0].shape[0]
    n = prods[0][1].shape[0] if prods[0][2] else prods[0][1].shape[1]
    tm = min(tm, m)
    assert m % tm == 0
    deps = [] if dep is None else [dep]
    n_p, n_r, n_v = len(prods), len(rows_in), len(vecs_in)

    def body(*refs):
        ab = refs[:2 * n_p]
        row_refs = refs[2 * n_p:2 * n_p + n_r]
        vec_refs = refs[2 * n_p + n_r:2 * n_p + n_r + n_v]
        out_refs = refs[2 * n_p + n_r + n_v + len(deps):]
        p = None
        for j, (_, _, tb) in enumerate(prods):
            t = _dot(ab[2 * j][...].astype(BF16), ab[2 * j + 1][...], 1, 1 if tb else 0)
            p = t if p is None else p + t
        vals = epilogue(p, *[r[...] for r in row_refs], *[v[...] for v in vec_refs])
        for (dtype, kind), o_ref, val in zip(outs, out_refs, vals):
            if kind == "row":
                o_ref[...] = val.astype(dtype)
            else:
                @pl.when(pl.program_id(0) == 0)
                def _(o_ref=o_ref):
                    o_ref[...] = jnp.zeros_like(o_ref)

                o_ref[...] += val

    row = lambda w: pl.BlockSpec((tm, w), lambda i: (i, 0))
    whole = lambda a: pl.BlockSpec(a.shape, lambda i: (0,) * a.ndim, pipeline_mode=pl.Buffered(1))
    in_specs, args = [], []
    for a, b, _ in prods:
        in_specs += [row(a.shape[1]), whole(b)]
        args += [a, b]
    in_specs += [row(r.shape[1]) for r in rows_in] + [whole(v) for v in vecs_in] + [ANY_SPEC] * len(deps)
    return pl.pallas_call(
        body, name=name,
        out_shape=tuple(jax.ShapeDtypeStruct((m, n) if kind == "row" else (8, n), dtype) for dtype, kind in outs),
        grid=(m // tm,), in_specs=in_specs,
        out_specs=tuple(row(n) if kind == "row" else pl.BlockSpec((8, n), lambda i: (0, 0)) for _, kind in outs),
        compiler_params=_params(dimension_semantics=("arbitrary",)),
    )(*args, *rows_in, *vecs_in, *deps)


def _rstd(x):
    return lax.rsqrt(jnp.mean(x * x, axis=-1, keepdims=True) + EPS)


def _norm_bwd(xh, r, t):
    return r * (t - xh * jnp.mean(xh * t, axis=-1, keepdims=True))


ROW_F32, ROW_BF16, SUM_F32 = (F32, "row"), (BF16, "row"), (F32, "sum")


def _ep_post_pre(p, h, g_post, g_pre):
    y = p.astype(BF16)
    yf = y.astype(F32)
    hn = h + yf * _rstd(yf) * g_post
    return y, hn, hn * _rstd(hn) * g_pre


_EP_POST_PRE_OUTS = [ROW_BF16, ROW_F32, ROW_BF16]


def _ep_final_loss(y, h, target, g_post):
    r = _rstd(y)
    yh = y * r
    err = h + yh * g_post - target
    dh = err * (1.0 / D)
    return _rowsum8(err * err), dh, _norm_bwd(yh, r, dh * g_post), _rowsum8(dh * yh)


_EP_FINAL_LOSS_OUTS = [SUM_F32, ROW_F32, ROW_BF16, SUM_F32]


def _ep_post_pre_bwd(du, dh_out, hn, y, g_post, g_pre):
    r2 = _rstd(hn)
    xh = hn * r2
    dh = dh_out + _norm_bwd(xh, r2, du * g_pre)
    yf = y.astype(F32)
    r1 = _rstd(yf)
    yh = yf * r1
    return dh, _norm_bwd(yh, r1, dh * g_post), _rowsum8(du * xh), _rowsum8(dh * yh)


_EP_POST_PRE_BWD_OUTS = [ROW_F32, ROW_BF16, SUM_F32, SUM_F32]


def _ep_pre_bwd(du, dh_out, x, g):
    r = _rstd(x)
    xh = x * r
    return dh_out + _norm_bwd(xh, r, du * g), _rowsum8(du * xh)


_EP_PRE_BWD_OUTS = [ROW_F32, SUM_F32]


def _prenorm(x, g, *, name):
    t, d = x.shape
    tb = min(512, t)

    def body(x_ref, g_ref, o_ref):
        xf = x_ref[...]
        o_ref[...] = (xf * _rstd(xf) * g_ref[...]).astype(BF16)

    return pl.pallas_call(
        body, name=name, out_shape=jax.ShapeDtypeStruct((t, d), BF16), grid=(t // tb,),
        in_specs=[pl.BlockSpec((tb, d), lambda i: (i, 0)), pl.BlockSpec((1, d), lambda i: (0, 0))],
        out_specs=pl.BlockSpec((tb, d), lambda i: (i, 0)), compiler_params=_params(),
    )(x, g)


QB = 256


def _half_mask(shape, e):
    lane = lax.broadcasted_iota(jnp.int32, shape, len(shape) - 1)
    return (lane // 64) == e


def _place(kv):
    sw = pltpu.roll(kv, 64, 1)
    m0 = _half_mask(kv.shape, 0)
    return [[jnp.where(m0, kv, 0.0).astype(BF16), jnp.where(m0, 0.0, sw).astype(BF16)],
            [jnp.where(m0, sw, 0.0).astype(BF16), jnp.where(m0, 0.0, kv).astype(BF16)]]


def _swa_valid_q(i, nq, nk):
    qc = lax.broadcasted_iota(jnp.int32, (nq, nk), 0) // CHUNK
    kc = lax.broadcasted_iota(jnp.int32, (nq, nk), 1) // CHUNK - 2
    return (kc <= qc) & (qc <= kc + 2) & (4 * i + kc >= 0)


def _swa_fwd(z, sinks, t):
    nb = t // QB

    def body(s_ref, q_ref, kp_ref, kc_ref, vp_ref, vc_ref, o_ref, lse_ref):
        i = pl.program_id(0)
        kpl = _place(jnp.concatenate([kp_ref[...], kc_ref[...]], axis=0))
        vpl = _place(jnp.concatenate([vp_ref[...], vc_ref[...]], axis=0))
        valid = _swa_valid_q(i, QB, QB + 128)
        lane = lax.broadcasted_iota(jnp.int32, (QB, 128), 1)
        lse_out = jnp.zeros((QB, 128), F32)
        for j in range(4):
            qp = q_ref[:, 128 * j:128 * (j + 1)].astype(BF16)
            acc = jnp.zeros((QB, 128), F32)
            for e in range(2):
                h = 2 * j + e
                kvh = h // 4
                qm = jnp.where(_half_mask(qp.shape, e), qp, jnp.zeros_like(qp))
                s = _dot(qm, kpl[kvh][e], 1, 1) * 0.125
                s = jnp.where(valid, s, NEG)
                sink = s_ref[0, h]
                m = jnp.maximum(jnp.max(s, axis=-1, keepdims=True), sink)
                p = jnp.exp(s - m)
                l = jnp.sum(p, axis=-1, keepdims=True) + jnp.exp(sink - m)
                acc = acc + _dot(p.astype(BF16), vpl[kvh][e], 1, 0) * (1.0 / l)
                lse_out = jnp.where(lane == h, m + jnp.log(l), lse_out)
            o_ref[:, 128 * j:128 * (j + 1)] = acc.astype(BF16)
        lse_ref[...] = lse_out

    prev = lambda c: pl.BlockSpec((128, 128), lambda i: (jnp.maximum(2 * i - 1, 0), c))
    cur = lambda c: pl.BlockSpec((QB, 128), lambda i: (i, c))
    return pl.pallas_call(
        body, name="swa_fwd",
        out_shape=(jax.ShapeDtypeStruct((t, D), BF16), jax.ShapeDtypeStruct((t, 128), F32)),
        grid=(nb,),
        in_specs=[pl.BlockSpec(memory_space=pltpu.SMEM),
                  pl.BlockSpec((QB, SWA_W), lambda i: (i, 0)), prev(4), cur(4), prev(5), cur(5)],
        out_specs=(pl.BlockSpec((QB, SWA_W), lambda i: (i, 0)), pl.BlockSpec((QB, 128), lambda i: (i, 0))),
        compiler_params=_params(),
    )(sinks, z, z, z, z, z)


def _swa_bwd(z, sinks, ymix, lse, dymix, t):
    nb = t // QB
    nk = QB + 128

    def body(s_ref, q_ref, kp_ref, kc_ref, vp_ref, vc_ref, o_ref, do_ref, l_ref,
             dq_ref, first_ref, second_ref, ds_ref, carry_ref):
        i = pl.program_id(0)
        live = i < nb

        @pl.when(i == 0)
        def _():
            ds_ref[...] = jnp.zeros_like(ds_ref)
            carry_ref[...] = jnp.zeros_like(carry_ref)

        lane = lax.broadcasted_iota(jnp.int32, (8, 128), 1)
        kpl = _place(jnp.concatenate([kp_ref[...], kc_ref[...]], axis=0))
        vpl = _place(jnp.concatenate([vp_ref[...], vc_ref[...]], axis=0))
        valid = _swa_valid_q(i, QB, nk) & live
        lse_c = l_ref[...]
        dsink = jnp.zeros((8, 128), F32)
        dk_acc = [[jnp.zeros((nk, 128), F32) for _ in range(2)] for _ in range(2)]
        dv_acc = [[jnp.zeros((nk, 128), F32) for _ in range(2)] for _ in range(2)]
        dq = []
        for j in range(4):
            cols = slice(128 * j, 128 * (j + 1))
            qp = q_ref[:, cols].astype(BF16)
            dop = do_ref[:, cols]
            prod = dop.astype(F32) * o_ref[:, cols].astype(F32)
            acc = jnp.zeros((QB, 128), F32)
            for e in range(2):
                h = 2 * j + e
                kvh = h // 4
                hm = _half_mask(qp.shape, e)
                qm = jnp.where(hm, qp, jnp.zeros_like(qp))
                dom = jnp.where(hm, dop, jnp.zeros_like(dop))
                dd = jnp.sum(jnp.where(hm, prod, 0.0), axis=-1, keepdims=True)
                lse_h = lse_c[:, h:h + 1]
                s = _dot(qm, kpl[kvh][e], 1, 1) * 0.125
                p = jnp.where(valid, jnp.exp(s - lse_h), 0.0)
                dp = _dot(dom, vpl[kvh][e], 1, 1)
                ds = (p * (dp - dd) * 0.125).astype(BF16)
                acc = acc + _dot(ds, kpl[kvh][e], 1, 0)
                dk_acc[kvh][e] = dk_acc[kvh][e] + _dot(ds, qm, 0, 0)
                dv_acc[kvh][e] = dv_acc[kvh][e] + _dot(p.astype(BF16), dom, 0, 0)
                ps = jnp.where(live, jnp.exp(s_ref[0, h] - lse_h) * dd, 0.0)
                dsink = dsink - jnp.where(lane == h, _rowsum8(jnp.broadcast_to(ps, (QB, 128))), 0.0)
            dq.append(acc.astype(BF16))
        ds_ref[...] += dsink
        dk = dk_acc[0][0] + dk_acc[1][1] + pltpu.roll(dk_acc[0][1] + dk_acc[1][0], 64, 1)
        dv = dv_acc[0][0] + dv_acc[1][1] + pltpu.roll(dv_acc[0][1] + dv_acc[1][0], 64, 1)
        dkv = jnp.concatenate([dk, dv], axis=1)
        second_ref[...] = (carry_ref[...] + dkv[0:128]).astype(BF16)
        carry_ref[...] = dkv[256:384]

        @pl.when(live)
        def _():
            for j in range(4):
                dq_ref[:, 128 * j:128 * (j + 1)] = dq[j]
            first_ref[...] = dkv[128:256].astype(BF16)

    blk = lambda i: jnp.minimum(i, nb - 1)
    prev = lambda c: pl.BlockSpec((128, 128), lambda i: (jnp.maximum(2 * blk(i) - 1, 0), c))
    cur = lambda w, c: pl.BlockSpec((QB, w), lambda i: (blk(i), c))
    half = lambda index: pl.BlockSpec((128, 256), lambda i: (index(i), 0))
    return pl.pallas_call(
        body, name="swa_bwd",
        out_shape=(jax.ShapeDtypeStruct((t, SWA_W), BF16), jax.ShapeDtypeStruct((t // 2, 256), BF16),
                   jax.ShapeDtypeStruct((t // 2, 256), BF16), jax.ShapeDtypeStruct((8, 128), F32)),
        grid=(nb + 1,),
        in_specs=[pl.BlockSpec(memory_space=pltpu.SMEM),
                  cur(SWA_W, 0), prev(4), cur(128, 4), prev(5), cur(128, 5),
                  cur(SWA_W, 0), cur(SWA_W, 0), cur(128, 0)],
        out_specs=(cur(SWA_W, 0), half(blk), half(lambda i: jnp.maximum(i - 1, 0)),
                   pl.BlockSpec((8, 128), lambda i: (0, 0))),
        scratch_shapes=[pltpu.VMEM((128, 256), F32)],
        compiler_params=_params(dimension_semantics=("arbitrary",)),
    )(sinks, z, z, z, z, z, ymix, dymix, lse)


HB = 256


def _lower_bound(lb_ref):
    a = lb_ref[...]
    a0, a1 = a[0:1], a[1:2]
    mx = jnp.maximum(a0, a1)
    e0, e1 = jnp.exp(a0 - mx), jnp.exp(a1 - mx)
    return e0 / (e0 + e1)


def _hgrn_cols(row_block):
    return [pl.BlockSpec((HB, 2 * HD), lambda j, c=base // (2 * HD) + p: (row_block(j), c))
            for base in (ZQH, ZFH, ZIH, ZGH) for p in range(2)]


NCH = HB // CHUNK


def _split3(x):
    hi = x.astype(BF16)
    r1 = x - hi.astype(F32)
    mid = r1.astype(BF16)
    return hi, mid, (r1 - mid.astype(F32)).astype(BF16)


def _blockdiag(lower):
    r = lax.broadcasted_iota(jnp.int32, (HB, HB), 0)
    c = lax.broadcasted_iota(jnp.int32, (HB, HB), 1)
    return (r // CHUNK == c // CHUNK) & ((c <= r) if lower else (c >= r))


def _chunk_sums(mask_bf16, x):
    return sum(_dot(mask_bf16, part, 1, 0) for part in _split3(x))


def _per_chunk_rows(x, row):
    w = x.shape[1]
    picked = x.reshape(NCH, CHUNK, w)[:, row:row + 1, :]
    return jnp.broadcast_to(picked, (NCH, CHUNK, w)).reshape(HB, w)


def _chunk_stack(x, chunk_of_row):
    return jnp.concatenate([jnp.where(chunk_of_row == c, x, jnp.zeros_like(x)) for c in range(NCH)], axis=1)


def _chunk_pick(x, chunk_of_row):
    w = x.shape[1] // NCH
    out = jnp.zeros((HB, w), x.dtype)
    for c in range(NCH):
        out = jnp.where(chunk_of_row == c, x[:, c * w:(c + 1) * w], out)
    return out


def _hgrn_local(q, f, kf, b):
    sq = _sig(q)
    qf = q * sq * (HD ** -0.5)
    b_mid = _per_chunk_rows(b, CHUNK // 2 - 1)
    b_last = _per_chunk_rows(b, CHUNK - 1)
    qm = qf * jnp.exp(b - b_mid)
    km = kf * jnp.exp(b_mid - b)
    kl = kf * jnp.exp(b_last - b)
    qb = qf * jnp.exp(b)
    return dict(sq=sq, b_mid=b_mid, b_last=b_last, qm=qm, km=km, kl=kl, qb=qb)


def _hgrn2_fwd(z, hgrn_lb, onorm, ymix, t):
    nb = t // HB

    def body(*refs):
        zq, zf, zi, zg = refs[0:2], refs[2:4], refs[4:6], refs[6:8]
        lb_ref, on_ref, _, y_ref, o_ref, sp_ref, st_ref = refs[8:]

        @pl.when(pl.program_id(0) == 0)
        def _():
            st_ref[...] = jnp.zeros_like(st_ref)

        lb_all = _lower_bound(lb_ref)
        gn = on_ref[...]
        low = _blockdiag(True)
        low_b = low.astype(BF16)
        chunk_of_row = lax.broadcasted_iota(jnp.int32, (HB, HD), 0) // CHUNK
        for p in range(2):
            lbp = lb_all[:, 2 * HD * p:2 * HD * (p + 1)]
            fp = lbp + (1.0 - lbp) * _sig(zf[p][...])
            bp = _chunk_sums(low_b, jnp.log(fp))
            for e in range(2):
                h, ls = 2 * p + e, slice(e * HD, (e + 1) * HD)
                f = fp[:, ls]
                w = _hgrn_local(zq[p][:, ls], f, 1.0 - f, bp[:, ls])
                iv = zi[p][:, ls].astype(BF16)
                a = jnp.where(low, _dot(w["qm"].astype(BF16), w["km"].astype(BF16), 1, 1), 0.0)
                o = _dot(a.astype(BF16), iv, 1, 0)
                u = _dot(iv, _chunk_stack(w["kl"].astype(BF16), chunk_of_row), 0, 0)
                decay = jnp.exp(w["b_last"])
                st = st_ref[h]
                states = []
                for c in range(NCH):
                    sp_ref[h, c] = st
                    states.append(st.astype(BF16))
                    st = st * decay[c * CHUNK:c * CHUNK + 1] + u[:, c * HD:(c + 1) * HD]
                st_ref[h] = st
                inter = _dot(w["qb"].astype(BF16), jnp.concatenate(states, axis=0), 1, 1)
                o = o + _chunk_pick(inter, chunk_of_row)
                hs = slice(h * HD, (h + 1) * HD)
                o_ref[:, hs] = o
                gg = zg[p][:, ls]
                y_ref[:, hs] = (o * _rstd(o) * gn * (gg * _sig(gg))).astype(BF16)

    return pl.pallas_call(
        body, name="hgrn_fwd",
        out_shape=(jax.ShapeDtypeStruct((t, D), BF16), jax.ShapeDtypeStruct((t, HG_W), F32),
                   jax.ShapeDtypeStruct((4, t // CHUNK, HD, HD), F32)),
        grid=(nb,),
        in_specs=_hgrn_cols(lambda j: j) + [pl.BlockSpec((2, HG_W), lambda j: (0, 0)),
                                            pl.BlockSpec((1, HD), lambda j: (0, 0)), ANY_SPEC],
        out_specs=(pl.BlockSpec((HB, HG_W), lambda j: (j, 1)),
                   pl.BlockSpec((HB, HG_W), lambda j: (j, 0)),
                   pl.BlockSpec((4, NCH, HD, HD), lambda j: (0, j, 0, 0))),
        scratch_shapes=[pltpu.VMEM((4, HD, HD), F32)],
        input_output_aliases={10: 0},
        compiler_params=_params(dimension_semantics=("arbitrary",)),
    )(*[z] * 8, hgrn_lb, onorm, ymix)


def _hgrn2_bwd(z, hgrn_lb, onorm, o_save, sprev, dymix, dza, t):
    nb = t // HB

    def body(*refs):
        zq, zf, zi, zg = refs[0:2], refs[2:4], refs[4:6], refs[6:8]
        (lb_ref, on_ref, o_ref, sp_ref, dy_ref, dqa_ref, first_ref, second_ref,
         dz_ref, dlb_ref, don_ref, dst_ref) = refs[8:]

        @pl.when(pl.program_id(0) == 0)
        def _():
            dst_ref[...] = jnp.zeros_like(dst_ref)
            dlb_ref[...] = jnp.zeros_like(dlb_ref)
            don_ref[...] = jnp.zeros_like(don_ref)

        dz_ref[:, 0:SWA_W] = dqa_ref[...]
        dz_ref[0:HB // 2, SWA_W:ZQH] = first_ref[...]
        dz_ref[HB // 2:HB, SWA_W:ZQH] = second_ref[...]
        lb_all = _lower_bound(lb_ref)
        gn = on_ref[...]
        low, upp = _blockdiag(True), _blockdiag(False)
        upp_b = upp.astype(BF16)
        low_b = low.astype(BF16)
        row = lax.broadcasted_iota(jnp.int32, (HB, HD), 0)
        chunk_of_row = row // CHUNK
        in_chunk = row % CHUNK
        for p in range(2):
            lbp = lb_all[:, 2 * HD * p:2 * HD * (p + 1)]
            sgp = _sig(zf[p][...])
            fp = lbp + (1.0 - lbp) * sgp
            bp = _chunk_sums(low_b, jnp.log(fp))
            db_pair, dkf_pair = [], []
            for e in range(2):
                h, ls, hs = 2 * p + e, slice(e * HD, (e + 1) * HD), slice((2 * p + e) * HD, (2 * p + e + 1) * HD)
                f = fp[:, ls]
                q = zq[p][:, ls]
                w = _hgrn_local(q, f, 1.0 - f, bp[:, ls])
                iv = zi[p][:, ls].astype(BF16)
                gg = zg[p][:, ls]
                o = o_ref[:, hs]
                dout = dy_ref[:, hs].astype(F32)
                sgg = _sig(gg)
                r = _rstd(o)
                oh = o * r
                dyn = dout * (gg * sgg)
                dz_ref[:, ZGH + h * HD:ZGH + (h + 1) * HD] = (
                    dout * oh * gn * (sgg * (1.0 + gg * (1.0 - sgg)))).astype(BF16)
                don_ref[...] += _rowsum8(dyn * oh)
                do = _norm_bwd(oh, r, dyn * gn).astype(BF16)
                qm, km, kl, qb = (w[n].astype(BF16) for n in ("qm", "km", "kl", "qb"))
                decay = jnp.exp(w["b_last"])
                grads_in = _dot(do, _chunk_stack(qb, chunk_of_row), 0, 0)
                dst = dst_ref[h]
                dstn, dd_rows = [None] * NCH, [None] * NCH
                for c in reversed(range(NCH)):
                    dstn[c] = dst.astype(BF16)
                    dd_rows[c] = jnp.sum(dst * sp_ref[h, c], axis=0, keepdims=True)
                    dst = dst * decay[c * CHUNK:c * CHUNK + 1] + grads_in[:, c * HD:(c + 1) * HD]
                dst_ref[h] = dst
                states = jnp.concatenate([sp_ref[h, c].astype(BF16) for c in range(NCH)], axis=0)
                dstn_all = jnp.concatenate(dstn, axis=0)
                dqb = _dot(_chunk_stack(do, chunk_of_row), states, 1, 0)
                at = jnp.where(upp, _dot(km, qm, 1, 1), 0.0)
                di = _dot(at.astype(BF16), do, 1, 0) + _chunk_pick(_dot(kl, dstn_all, 1, 1), chunk_of_row)
                dz_ref[:, ZIH + h * HD:ZIH + (h + 1) * HD] = di.astype(BF16)
                dkl = _dot(_chunk_stack(iv, chunk_of_row), dstn_all, 1, 0)
                da = jnp.where(low, _dot(do, iv, 1, 1), 0.0).astype(BF16)
                dat = jnp.where(upp, _dot(iv, do, 1, 1), 0.0).astype(BF16)
                dqm = _dot(da, km, 1, 0)
                dkm = _dot(dat, qm, 1, 0)
                b = bp[:, ls]
                e1, e2 = jnp.exp(b - w["b_mid"]), jnp.exp(w["b_mid"] - b)
                e3, e4 = jnp.exp(w["b_last"] - b), jnp.exp(b)
                dqf = dqm * e1 + dqb * e4
                dkf_pair.append(dkm * e2 + dkl * e3)
                t_qm, t_km, t_kl = dqm * w["qm"], dkm * w["km"], dkl * w["kl"]
                db = t_qm - t_km - t_kl + dqb * w["qb"]
                db_mid = jnp.sum((t_km - t_qm).reshape(NCH, CHUNK, HD), axis=1, keepdims=True)
                db_last = jnp.sum(t_kl.reshape(NCH, CHUNK, HD), axis=1, keepdims=True)
                db_last = db_last + jnp.stack(dd_rows, axis=0) * jnp.exp(
                    bp[:, ls].reshape(NCH, CHUNK, HD)[:, CHUNK - 1:CHUNK, :])
                spread = lambda v: jnp.broadcast_to(v, (NCH, CHUNK, HD)).reshape(HB, HD)
                db = (db + jnp.where(in_chunk == CHUNK // 2 - 1, spread(db_mid), 0.0)
                      + jnp.where(in_chunk == CHUNK - 1, spread(db_last), 0.0))
                db_pair.append(db)
                sq = w["sq"]
                dz_ref[:, ZQH + h * HD:ZQH + (h + 1) * HD] = (
                    dqf * (HD ** -0.5) * (sq * (1.0 + q * (1.0 - sq)))).astype(BF16)
            dlogf = _chunk_sums(upp_b, jnp.concatenate(db_pair, axis=1))
            dfv = dlogf / fp - jnp.concatenate(dkf_pair, axis=1)
            dz_ref[:, ZFH + 2 * HD * p:ZFH + 2 * HD * (p + 1)] = (dfv * (1.0 - lbp) * sgp * (1.0 - sgp)).astype(BF16)
            dlb_ref[:, 2 * HD * p:2 * HD * (p + 1)] += _rowsum8(dfv * (1.0 - sgp))

    rev = lambda j: nb - 1 - j
    return pl.pallas_call(
        body, name="hgrn_bwd",
        out_shape=(jax.ShapeDtypeStruct((t, D_IN), BF16), jax.ShapeDtypeStruct((8, HG_W), F32),
                   jax.ShapeDtypeStruct((8, HD), F32)),
        grid=(nb,),
        in_specs=_hgrn_cols(rev) + [pl.BlockSpec((2, HG_W), lambda j: (0, 0)), pl.BlockSpec((1, HD), lambda j: (0, 0)),
                                    pl.BlockSpec((HB, HG_W), lambda j: (rev(j), 0)),
                                    pl.BlockSpec((4, NCH, HD, HD), lambda j: (0, rev(j), 0, 0)),
                                    pl.BlockSpec((HB, HG_W), lambda j: (rev(j), 1)),
                                    pl.BlockSpec((HB, SWA_W), lambda j: (rev(j), 0)),
                                    pl.BlockSpec((HB // 2, 2 * KV_W), lambda j: (rev(j), 0)),
                                    pl.BlockSpec((HB // 2, 2 * KV_W), lambda j: (rev(j), 0))],
        out_specs=(pl.BlockSpec((HB, D_IN), lambda j: (rev(j), 0)), pl.BlockSpec((8, HG_W), lambda j: (0, 0)),
                   pl.BlockSpec((8, HD), lambda j: (0, 0))),
        scratch_shapes=[pltpu.VMEM((4, HD, HD), F32)],
        compiler_params=_params(dimension_semantics=("arbitrary",)),
    )(*[z] * 8, hgrn_lb, onorm, o_save, sprev, dymix, *dza)


XB = 512


def _xattn_fwd(q, k, v, t):
    tb = min(XB, t)

    def body(q_ref, k_ref, v_ref, o_ref):
        for h in range(XH):
            cols = slice(XD * h, XD * (h + 1))
            s = _dot(q_ref[:, cols], k_ref[:, cols], 1, 1) * (XD ** -0.5)
            p = jnp.exp(s - jnp.max(s, axis=-1, keepdims=True))
            l = jnp.sum(p, axis=-1, keepdims=True)
            o_ref[:, cols] = (_dot(p.astype(BF16), v_ref[:, cols], 1, 0) * (1.0 / l)).astype(BF16)

    row = pl.BlockSpec((tb, D), lambda i: (i, 0))
    mem = pl.BlockSpec(k.shape, lambda i: (0, 0))
    return pl.pallas_call(
        body, name="xattn_fwd", out_shape=jax.ShapeDtypeStruct((t, D), BF16), grid=(t // tb,),
        in_specs=[row, mem, mem], out_specs=row, compiler_params=_params(),
    )(q, k, v)


def _xattn_bwd(q, k, v, do, t):
    tb = min(XB, t)

    def body(q_ref, k_ref, v_ref, do_ref, dq_ref, dk_ref, dv_ref):
        @pl.when(pl.program_id(0) == 0)
        def _():
            dk_ref[...] = jnp.zeros_like(dk_ref)
            dv_ref[...] = jnp.zeros_like(dv_ref)

        for h in range(XH):
            cols = slice(XD * h, XD * (h + 1))
            qh, kh, vh, doh = q_ref[:, cols], k_ref[:, cols], v_ref[:, cols], do_ref[:, cols]
            s = _dot(qh, kh, 1, 1) * (XD ** -0.5)
            p = jnp.exp(s - jnp.max(s, axis=-1, keepdims=True))
            p = p * (1.0 / jnp.sum(p, axis=-1, keepdims=True))
            dp = _dot(doh, vh, 1, 1)
            ds = (p * (dp - jnp.sum(p * dp, axis=-1, keepdims=True)) * (XD ** -0.5)).astype(BF16)
            dq_ref[:, cols] = _dot(ds, kh, 1, 0).astype(BF16)
            dk_ref[:, cols] += _dot(ds, qh, 0, 0)
            dv_ref[:, cols] += _dot(p.astype(BF16), doh, 0, 0)

    row = pl.BlockSpec((tb, D), lambda i: (i, 0))
    mem = pl.BlockSpec(k.shape, lambda i: (0, 0))
    return pl.pallas_call(
        body, name="xattn_bwd",
        out_shape=(jax.ShapeDtypeStruct((t, D), BF16), jax.ShapeDtypeStruct(k.shape, F32),
                   jax.ShapeDtypeStruct(k.shape, F32)),
        grid=(t // tb,), in_specs=[row, mem, mem, row], out_specs=(row, mem, mem),
        compiler_params=_params(dimension_semantics=("arbitrary",)),
    )(q, k, v, do)


def _mem_gain_bwd(dm, mem, *, name):
    def body(dm_ref, m_ref, dg_ref):
        m_ = m_ref[...]
        dg_ref[...] = _rowsum8(dm_ref[...] * (m_ * _rstd(m_)))

    return pl.pallas_call(body, name=name, out_shape=jax.ShapeDtypeStruct((8, D), F32),
                          compiler_params=_params())(dm, mem)


FM, FN = 512, 1408


def _ffn_up(u, wgt, wut, t):
    tm = min(FM, t)

    def body(u_ref, wg_ref, wu_ref, g_ref, up_ref, a_ref):
        u_ = u_ref[...]
        g = _dot(u_, wg_ref[...], 1, 1)
        up = _dot(u_, wu_ref[...], 1, 1)
        g_ref[...] = g.astype(BF16)
        up_ref[...] = up.astype(BF16)
        a_ref[...] = (g * _sig(g) * up).astype(BF16)

    w = pl.BlockSpec((FN, D), lambda j, i: (j, 0))
    o = pl.BlockSpec((tm, FN), lambda j, i: (i, j))
    return pl.pallas_call(
        body, name="ffn_up", out_shape=(jax.ShapeDtypeStruct((t, D_FF), BF16),) * 3,
        grid=(D_FF // FN, t // tm), in_specs=[pl.BlockSpec((tm, D), lambda j, i: (i, 0)), w, w],
        out_specs=(o, o, o), compiler_params=_params(),
    )(u, wgt, wut)


def _ffn_down_bwd(dy, wd, gate, up, t, dep=None):
    tm = min(FM, t)
    deps = [] if dep is None else [dep]

    def body(dy_ref, w_ref, g_ref, up_ref, *rest):
        dg_ref, dup_ref = rest[len(deps):]
        da = _dot(dy_ref[...], w_ref[...], 1, 1)
        g = g_ref[...].astype(F32)
        sg = _sig(g)
        dup_ref[...] = (da * g * sg).astype(BF16)
        dg_ref[...] = (da * up_ref[...].astype(F32) * (sg * (1.0 + g * (1.0 - sg)))).astype(BF16)

    o = pl.BlockSpec((tm, FN), lambda j, i: (i, j))
    return pl.pallas_call(
        body, name="ffn_down_bwd", out_shape=(jax.ShapeDtypeStruct((t, D_FF), BF16),) * 2,
        grid=(D_FF // FN, t // tm),
        in_specs=[pl.BlockSpec((tm, D), lambda j, i: (i, 0)), pl.BlockSpec((FN, D), lambda j, i: (j, 0)), o, o]
        + [ANY_SPEC] * len(deps),
        out_specs=(o, o), compiler_params=_params(),
    )(dy, wd, gate, up, *deps)


def _local_step(x, mem, target, fetch, sm, emit=None):
    t = x.shape[0]
    w, gw = {}, {}

    def out(key, g):
        gw[key] = g
        return None if emit is None else emit(key, g)
    u1 = _prenorm(x, sm["g_mix_pre"], name="prenorm_mix")
    w["winT"] = fetch("winT", u1)
    z = _mm(u1, w["winT"], tb=True, out_dtype=F32, tm=1024, tn=1408, name="mm_z", n_outer=True)
    ymix, lse = _swa_fwd(z, sm["sinks"], t)
    ymix, o_h, sprev = _hgrn2_fwd(z, sm["hgrn_lb"], sm["hgrn_onorm"], ymix, t)
    w["wout"] = fetch("wout", ymix)
    y1, h1, u2 = _mm_rows([(ymix, w["wout"], False)], [x], [sm["g_mix_post"], sm["g_x_pre"]], _ep_post_pre,
                          _EP_POST_PRE_OUTS, tm=512, name="mm_y1_post")
    mn = _prenorm(mem, sm["g_mem"], name="prenorm_mem")
    for key in ("wq", "wk", "wv"):
        w[key] = fetch(key, u2)
    qx = _mm(u2, w["wq"], out_dtype=BF16, tm=1024, tn=1024, name="mm_qx")
    kx = _mm(mn, w["wk"], out_dtype=BF16, tm=1024, tn=1024, name="mm_kx")
    vx = _mm(mn, w["wv"], out_dtype=BF16, tm=1024, tn=1024, name="mm_vx")
    ox = _xattn_fwd(qx, kx, vx, t)
    w["wo"] = fetch("wo", ox)
    y2, h2, u3 = _mm_rows([(ox, w["wo"], False)], [h1], [sm["g_x_post"], sm["g_ffn_pre"]], _ep_post_pre,
                          _EP_POST_PRE_OUTS, tm=512, name="mm_y2_post")
    w["wgT"], w["wuT"] = fetch("wgT", u3), fetch("wuT", u3)
    gate, up, act = _ffn_up(u3, w["wgT"], w["wuT"], t)
    w["wd"] = fetch("wd", act)
    sq, dh3, dy3, dg_ffn_post = _mm_rows([(act, w["wd"], False)], [h2, target], [sm["g_ffn_post"]], _ep_final_loss,
                                         _EP_FINAL_LOSS_OUTS, tm=512, name="mm_y3_loss")
    dep = out("wd", _mm(act, dy3, ta=True, out_dtype=BF16, tm=1408, tn=1024, name="mm_gwd"))
    dgate, dup = _ffn_down_bwd(dy3, w["wd"], gate, up, t, dep=dep)
    dep = out("wgT", _mm(dgate, u3, ta=True, out_dtype=BF16, tm=1408, tn=1024, name="mm_gwg"))
    dep = out("wuT", _mm(dup, u3, ta=True, out_dtype=BF16, tm=1408, tn=1024, name="mm_gwu", dep=dep))
    dh2, dy2, dg_ffn_pre, dg_x_post = _mm_rows(
        [(dgate, w["wgT"], False), (dup, w["wuT"], False)], [dh3, h2, y2], [sm["g_x_post"], sm["g_ffn_pre"]],
        _ep_post_pre_bwd, _EP_POST_PRE_BWD_OUTS, tm=512, name="mm_du3_post_bwd", dep=dep)
    dep = out("wo", _mm(ox, dy2, ta=True, out_dtype=BF16, tm=512, tn=1024, name="mm_gwo"))
    dox = _mm(dy2, w["wo"], tb=True, out_dtype=BF16, tm=1024, tn=1024, name="mm_dox", dep=dep)
    dqx, dkx, dvx = _xattn_bwd(qx, kx, vx, dox, t)
    dep = out("wq", _mm(u2, dqx, ta=True, out_dtype=BF16, tm=512, tn=1024, name="mm_gwq"))
    dep = out("wk", _mm(mn, dkx, ta=True, out_dtype=BF16, tm=1024, tn=1024, name="mm_gwk", dep=dep))
    dep = out("wv", _mm(mn, dvx, ta=True, out_dtype=BF16, tm=1024, tn=1024, name="mm_gwv", dep=dep))
    dh1, dy1, dg_x_pre, dg_mix_post = _mm_rows(
        [(dqx, w["wq"], True)], [dh2, h1, y1], [sm["g_mix_post"], sm["g_x_pre"]],
        _ep_post_pre_bwd, _EP_POST_PRE_BWD_OUTS, tm=512, name="mm_du2_post_bwd", dep=dep)
    dmn = _mm2(dkx, w["wk"], dvx, w["wv"], tb=True, out_dtype=F32, tm=256, name="mm_dmn")
    dg_mem = _mem_gain_bwd(dmn, mem, name="mem_gain_bwd")
    dep = out("wout", _mm(ymix, dy1, ta=True, out_dtype=BF16, tm=512, tn=1024, name="mm_gwout"))
    dymix = _mm(dy1, w["wout"], tb=True, out_dtype=BF16, tm=1024, tn=1024, name="mm_dymix", dep=dep)
    *dza, dsinks = _swa_bwd(z, sm["sinks"], ymix, lse, dymix, t)
    dz, dlb, donorm = _hgrn2_bwd(z, sm["hgrn_lb"], sm["hgrn_onorm"], o_h, sprev, dymix, dza, t)
    dep = out("winT", _mm(dz, u1, ta=True, out_dtype=BF16, tm=1408, tn=1024, name="mm_gwin"))
    grad_x, dg_mix_pre = _mm_rows([(dz, w["winT"], False)], [dh1, x], [sm["g_mix_pre"]], _ep_pre_bwd,
                                  _EP_PRE_BWD_OUTS, tm=512, name="mm_du1_pre_bwd", dep=dep)
    parts = dict(g_mix_pre=dg_mix_pre, g_mix_post=dg_mix_post, g_mem=dg_mem, g_x_pre=dg_x_pre,
                 g_x_post=dg_x_post, g_ffn_pre=dg_ffn_pre, g_ffn_post=dg_ffn_post,
                 hgrn_onorm=donorm, hgrn_lb=dlb, sinks=dsinks, sq=sq)
    return grad_x, gw, parts


def _position():
    return lax.axis_index("x"), lax.axis_index("y"), lax.axis_index("c")


def _peer(pos, k):
    x, y, c = pos
    return (1 - x if k & 4 else x, 1 - y if k & 2 else y, 1 - c if k & 1 else c)


def _linear(pos):
    x, y, c = pos
    return 4 * x + 2 * y + c


HBM_SPEC = pl.BlockSpec(memory_space=pltpu.HBM)
SEM_SPEC = pl.BlockSpec(memory_space=pltpu.SEMAPHORE)
DATAFLOW = pltpu.SideEffectType.DATAFLOW_SIDE_EFFECTING
SEND_ORDER = (1, 2, 4, 3, 5, 6, 7)


def _in_hbm(a):
    return pltpu.with_memory_space_constraint(a, pltpu.HBM)


def _prepare_weights(shards):
    n = len(shards)

    def body(*refs):
        ins, outs, lands, sem = refs[:n], refs[n:2 * n], refs[2 * n:3 * n], refs[3 * n]
        me_lin = _linear(_position())
        copies = []
        for a in range(n):
            r = ins[a].shape[0]
            outs[a][...] = ins[a][...].astype(BF16)
            copies.append(pltpu.make_async_copy(outs[a], lands[a].at[pl.ds(me_lin * r, r), :], sem.at[a]))
            copies[-1].start()
        for cp in copies:
            cp.wait()

    vmem = pl.BlockSpec(memory_space=pltpu.VMEM)
    res = pl.pallas_call(
        body, name="prepare_weights",
        out_shape=tuple(jax.ShapeDtypeStruct(s.shape, BF16) for s in shards)
        + tuple(jax.ShapeDtypeStruct((N_DEV * s.shape[0], s.shape[1]), BF16) for s in shards),
        in_specs=[vmem] * n, out_specs=tuple([vmem] * n + [ANY_SPEC] * n),
        scratch_shapes=[pltpu.SemaphoreType.DMA((n,))], compiler_params=_params(),
    )(*shards)
    return res[:n], res[n:]


def _gather_start(shards, lands):
    n = len(shards)
    rows = [s.shape[0] for s in shards]

    def body(*refs):
        srcs, land = refs[:n], refs[n:2 * n]
        send_sems, recv_sems = refs[2 * n:3 * n], refs[3 * n:4 * n]
        me = _position()
        for a in range(n):
            mine = land[a].at[pl.ds(_linear(me) * rows[a], rows[a]), :]
            for k in SEND_ORDER:
                pltpu.make_async_remote_copy(
                    src_ref=srcs[a], dst_ref=mine, send_sem=send_sems[a].at[k - 1], recv_sem=recv_sems[a].at[k - 1],
                    device_id=_peer(me, k), device_id_type=MESH).start()

    sems = tuple(pltpu.SemaphoreType.DMA((N_DEV - 1,)) for _ in range(2 * n))
    res = pl.pallas_call(
        body, name="weights_send",
        out_shape=sems + tuple(pltpu.HBM(s.shape, s.dtype) for s in shards)
        + tuple(pltpu.HBM(l.shape, l.dtype) for l in lands),
        in_specs=(HBM_SPEC,) * (2 * n), out_specs=(SEM_SPEC,) * (2 * n) + (HBM_SPEC,) * (2 * n),
        input_output_aliases={i: 2 * n + i for i in range(2 * n)},
        compiler_params=pltpu.CompilerParams(has_side_effects=DATAFLOW),
    )(*[_in_hbm(s) for s in shards], *[_in_hbm(l) for l in lands])
    return [(res[a], res[n + a], res[2 * n + a], res[3 * n + a]) for a in range(n)]


def _gather_wait(send_sems, recv_sems, shard_thru, land_thru, after, *, name):
    r = shard_thru.shape[0]

    def body(src_ref, land_ref, send_sems, recv_sems, after_ref, src_dead, got_ref):
        del after_ref, src_dead, got_ref
        me = _position()
        for k in SEND_ORDER:
            peer = _peer(me, k)
            copy = pltpu.make_async_remote_copy(
                src_ref=src_ref, dst_ref=land_ref.at[pl.ds(_linear(peer) * r, r), :],
                send_sem=send_sems.at[k - 1], recv_sem=recv_sems.at[k - 1],
                device_id=peer, device_id_type=MESH)
            copy.wait_send()
            copy.wait_recv()

    return pl.pallas_call(
        body, name=name,
        out_shape=(pltpu.HBM(shard_thru.shape, shard_thru.dtype), pltpu.HBM(land_thru.shape, land_thru.dtype)),
        in_specs=(HBM_SPEC, HBM_SPEC, SEM_SPEC, SEM_SPEC, ANY_SPEC),
        out_specs=(HBM_SPEC, HBM_SPEC), input_output_aliases={0: 0, 1: 1},
        compiler_params=pltpu.CompilerParams(has_side_effects=DATAFLOW),
    )(shard_thru, land_thru, send_sems, recv_sems, after)[1]


def _exchange_start(gs, *, name):
    n = len(gs)
    rows = [g.shape[0] // N_DEV for g in gs]
    lands = [lax.empty((N_DEV - 1, r, g.shape[1]), g.dtype) for g, r in zip(gs, rows)]

    def body(*refs):
        g_refs, land_refs = refs[:n], refs[n:2 * n]
        send_sems, recv_sems = refs[2 * n:3 * n], refs[3 * n:4 * n]
        me = _position()
        for a in range(n):
            for k in SEND_ORDER:
                peer = _peer(me, k)
                pltpu.make_async_remote_copy(
                    src_ref=g_refs[a].at[pl.ds(_linear(peer) * rows[a], rows[a]), :],
                    dst_ref=land_refs[a].at[k - 1],
                    send_sem=send_sems[a].at[k - 1], recv_sem=recv_sems[a].at[k - 1],
                    device_id=peer, device_id_type=MESH).start()

    res = pl.pallas_call(
        body, name=name,
        out_shape=tuple(pltpu.SemaphoreType.DMA((N_DEV - 1,)) for _ in range(2 * n))
        + tuple(pltpu.HBM(a.shape, a.dtype) for a in gs + lands),
        in_specs=(HBM_SPEC,) * (2 * n), out_specs=(SEM_SPEC,) * (2 * n) + (HBM_SPEC,) * (2 * n),
        input_output_aliases={i: 2 * n + i for i in range(2 * n)},
        compiler_params=pltpu.CompilerParams(has_side_effects=DATAFLOW),
    )(*[_in_hbm(a) for a in gs + lands])
    return [(res[a], res[n + a], res[2 * n + a], res[3 * n + a]) for a in range(n)]


def _exchange_wait(send_sems, recv_sems, g_thru, land_thru, after, *, name):
    r = land_thru.shape[1]

    def body(g_ref, land_ref, send_sems, recv_sems, after_ref, g_dead, got_ref):
        del after_ref, g_dead, got_ref
        me = _position()
        for k in SEND_ORDER:
            peer = _peer(me, k)
            copy = pltpu.make_async_remote_copy(
                src_ref=g_ref.at[pl.ds(_linear(peer) * r, r), :], dst_ref=land_ref.at[k - 1],
                send_sem=send_sems.at[k - 1], recv_sem=recv_sems.at[k - 1],
                device_id=peer, device_id_type=MESH)
            copy.wait_send()
            copy.wait_recv()

    return pl.pallas_call(
        body, name=name,
        out_shape=(pltpu.HBM(g_thru.shape, g_thru.dtype), pltpu.HBM(land_thru.shape, land_thru.dtype)),
        in_specs=(HBM_SPEC, HBM_SPEC, SEM_SPEC, SEM_SPEC, pl.BlockSpec(memory_space=pl.ANY)),
        out_specs=(HBM_SPEC, HBM_SPEC), input_output_aliases={0: 0, 1: 1},
        compiler_params=pltpu.CompilerParams(has_side_effects=DATAFLOW),
    )(g_thru, land_thru, send_sems, recv_sems, after)


def _adamw_math(w, g, m, v):
    m = B1 * m + (1.0 - B1) * g
    v = B2 * v + (1.0 - B2) * (g * g)
    delta = -LR * ((m / C1) / (jnp.sqrt(v / C2) + AEPS) + WD * w)
    return delta, m, v


def _sum_adamw(g_all, land, w, m, v, *, name):
    r = w.shape[0]

    def body(all_ref, land_ref, w_ref, m_ref, v_ref, g_ref, d_ref, nm_ref, nv_ref, own_ref, sem):
        mine = pltpu.make_async_copy(all_ref.at[pl.ds(_linear(_position()) * r, r), :], own_ref, sem)
        mine.start()
        g = land_ref[0].astype(F32)
        for s in range(1, N_DEV - 1):
            g = g + land_ref[s].astype(F32)
        mine.wait()
        g = own_ref[...].astype(F32) + g
        g_ref[...] = g
        d_ref[...], nm_ref[...], nv_ref[...] = _adamw_math(w_ref[...], g, m_ref[...], v_ref[...])

    vmem = pl.BlockSpec(memory_space=pltpu.VMEM)
    return pl.pallas_call(
        body, name=name, out_shape=(jax.ShapeDtypeStruct(w.shape, F32),) * 4,
        in_specs=[ANY_SPEC, vmem, vmem, vmem, vmem], out_specs=(vmem,) * 4,
        scratch_shapes=[pltpu.VMEM((r, w.shape[1]), BF16), pltpu.SemaphoreType.DMA(())],
        compiler_params=_params(),
    )(g_all, land, w, m, v)


SMALL = ("g_mix_pre", "g_mix_post", "g_mem", "g_x_pre", "g_x_post", "g_ffn_pre", "g_ffn_post",
         "hgrn_onorm", "hgrn_lb", "sinks")
SMALL_W = dict(hgrn_onorm=HD, hgrn_lb=HG_W, sinks=8)
SQ_ROW = len(SMALL)
PACK_ROWS = 16


def _small_allreduce(parts):
    ns = len(SMALL)

    def body(*refs):
        part, tot_ref = refs[:ns + 1], refs[ns + 1]
        gath, send_sems, recv_sems = refs[ns + 2:]
        me = _position()
        mine = gath.at[_linear(me)]
        mine[...] = jnp.zeros((PACK_ROWS, D), F32)
        for r, name in enumerate(SMALL):
            wd = SMALL_W.get(name, D)
            mine[r:r + 1, 0:wd] = jnp.sum(part[r][...], axis=0, keepdims=True)[:, 0:wd]
        sq = jnp.sum(part[ns][...]) * (0.5 / D)
        mine[SQ_ROW:SQ_ROW + 1, :] = jnp.full((1, D), sq, F32)

        def copy(k):
            peer = _peer(me, k)
            return pltpu.make_async_remote_copy(
                src_ref=mine, dst_ref=mine, send_sem=send_sems.at[k - 1], recv_sem=recv_sems.at[k - 1],
                device_id=peer, device_id_type=MESH)

        def arrival(k):
            slot = gath.at[_linear(_peer(me, k))]
            return pltpu.make_async_remote_copy(
                src_ref=slot, dst_ref=slot, send_sem=send_sems.at[k - 1], recv_sem=recv_sems.at[k - 1],
                device_id=_peer(me, k), device_id_type=MESH)

        sent = [copy(k) for k in range(1, 8)]
        for cp in sent:
            cp.start()
        for k in range(1, 8):
            arrival(k).wait_recv()
        for cp in sent:
            cp.wait_send()
        tot = gath[0]
        for s in range(1, N_DEV):
            tot = tot + gath[s]
        tot_ref[...] = tot

    return pl.pallas_call(
        body, name="small_allreduce", out_shape=jax.ShapeDtypeStruct((PACK_ROWS, D), F32),
        scratch_shapes=[pltpu.VMEM((N_DEV, PACK_ROWS, D), F32), pltpu.SemaphoreType.DMA((7,)),
                        pltpu.SemaphoreType.DMA((7,))],
        compiler_params=_params(has_side_effects=True),
    )(*[parts[n] for n in SMALL], parts["sq"])


def _small_update(tot, sm, m_sm, v_sm):
    ns = len(SMALL)

    def body(*refs):
        tot = refs[0][...]
        w_refs, m_refs, v_refs = refs[1:ns + 1], refs[ns + 1:2 * ns + 1], refs[2 * ns + 1:3 * ns + 1]
        outs = refs[3 * ns + 1:]
        loss_ref = outs[0]
        g_out, d_out = outs[1:ns + 1], outs[ns + 1:2 * ns + 1]
        nm_out, nv_out = outs[2 * ns + 1:3 * ns + 1], outs[3 * ns + 1:4 * ns + 1]
        loss_ref[...] = tot[SQ_ROW:SQ_ROW + 1, 0:1]
        for r, name in enumerate(SMALL):
            wd = SMALL_W.get(name, D)
            g = tot[r:r + 1, 0:wd]
            w = w_refs[r][...]
            if name == "hgrn_lb":
                mx = jnp.maximum(w[0:1], w[1:2])
                e0, e1 = jnp.exp(w[0:1] - mx), jnp.exp(w[1:2] - mx)
                lb0 = e0 / (e0 + e1)
                g0 = g * lb0 * (1.0 - lb0)
                for i, gi in enumerate((g0, -g0)):
                    d, nm, nv = _adamw_math(w[i:i + 1], gi, m_refs[r][i:i + 1, :], v_refs[r][i:i + 1, :])
                    g_out[r][i:i + 1, :] = gi
                    d_out[r][i:i + 1, :], nm_out[r][i:i + 1, :], nv_out[r][i:i + 1, :] = d, nm, nv
            else:
                d, nm, nv = _adamw_math(w, g, m_refs[r][...], v_refs[r][...])
                g_out[r][...] = g
                d_out[r][...], nm_out[r][...], nv_out[r][...] = d, nm, nv

    shapes = [jax.ShapeDtypeStruct(sm[n].shape, F32) for n in SMALL]
    res = pl.pallas_call(
        body, name="small_update", out_shape=tuple([jax.ShapeDtypeStruct((1, 1), F32)] + shapes * 4),
        compiler_params=_params(),
    )(tot, *[sm[n] for n in SMALL], *[m_sm[n] for n in SMALL], *[v_sm[n] for n in SMALL])
    groups = [dict(zip(SMALL, res[1 + i * ns:1 + (i + 1) * ns])) for i in range(4)]
    return res[0], groups[0], groups[1], groups[2], groups[3]


BIG = ("w_in", "w_gate", "w_up", "w_down", "w_out", "wq_x", "wk_x", "wv_x", "wo_x")
BIG_KEY = dict(w_in="winT", w_gate="wgT", w_up="wuT", w_down="wd", w_out="wout", wq_x="wq", wk_x="wk",
               wv_x="wv", wo_x="wo")
TRANSPOSED = ("w_in", "w_gate", "w_up")
WEIGHTS = ("w_in", "sinks", "hgrn_lb", "hgrn_onorm", "w_out", "g_mix_pre", "g_mix_post", "g_mem", "g_x_pre",
           "g_x_post", "wq_x", "wk_x", "wv_x", "wo_x", "g_ffn_pre", "g_ffn_post", "w_gate", "w_up", "w_down")


def kernel(x, mem, w_in, sinks, hgrn_lb, hgrn_onorm, w_out, g_mix_pre, g_mix_post, g_mem, g_x_pre, g_x_post, wq_x, wk_x, wv_x, wo_x, g_ffn_pre, g_ffn_post, w_gate, w_up, w_down, loss_target, m_w_in, m_sinks, m_hgrn_lb, m_hgrn_onorm, m_w_out, m_g_mix_pre, m_g_mix_post, m_g_mem, m_g_x_pre, m_g_x_post, m_wq_x, m_wk_x, m_wv_x, m_wo_x, m_g_ffn_pre, m_g_ffn_post, m_w_gate, m_w_up, m_w_down, v_w_in, v_sinks, v_hgrn_lb, v_hgrn_onorm, v_w_out, v_g_mix_pre, v_g_mix_post, v_g_mem, v_g_x_pre, v_g_x_post, v_wq_x, v_wk_x, v_wv_x, v_wo_x, v_g_ffn_pre, v_g_ffn_post, v_w_gate, v_w_up, v_w_down):
    given = dict(locals())
    wts = {n: given[n] for n in WEIGHTS}
    ms = {n: given["m_" + n] for n in WEIGHTS}
    vs = {n: given["v_" + n] for n in WEIGHTS}

    def mat(a, name):
        a = a[0]
        return a.T if name in TRANSPOSED else a

    order = ("w_in", "w_out", "wq_x", "wk_x", "wv_x", "wo_x", "w_gate", "w_up", "w_down")
    flying = dict(zip(order, _gather_start(*_prepare_weights([mat(wts[n], n) for n in order]))))
    name_of = {k: n for n, k in BIG_KEY.items()}

    def fetch(key, after):
        return _gather_wait(*flying[name_of[key]], after, name="weights_recv_" + name_of[key])

    sm = {n: wts[n] for n in SMALL}
    started, held = {}, {}
    send_with = {"wgT": ("wgT", "wuT"), "wuT": ("wgT", "wuT"), "wq": ("wq", "wk", "wv"), "wk": ("wq", "wk", "wv"),
                 "wv": ("wq", "wk", "wv")}

    def emit(key, g):
        held[key] = g
        group = send_with.get(key, (key,))
        if key != group[-1]:
            return None
        flights = _exchange_start([held[k] for k in group], name="grad_send_" + name_of[group[0]])
        started.update({name_of[k]: f for k, f in zip(group, flights)})
        return flights[-1][2]

    grad_x, _, parts = _local_step(x[0], mem[0], loss_target[0], fetch, sm, emit)
    grads, deltas, new_m, new_v = {}, {}, {}, {}
    after = grad_x
    for n in ("w_down", "w_gate", "w_up", "wo_x", "wq_x", "wk_x", "wv_x", "w_out", "w_in"):
        g_all, land = _exchange_wait(*started[n], after, name="grad_recv_" + n)
        res = _sum_adamw(g_all, land, mat(wts[n], n), mat(ms[n], n), mat(vs[n], n), name="adamw_" + n)
        after = res[1]
        if n in TRANSPOSED:
            res = [a.T for a in res]
        grads[n], deltas[n], new_m[n], new_v[n] = [a[None] for a in res]
    loss, g_s, d_s, m_s, v_s = _small_update(_small_allreduce(parts), sm, {n: ms[n] for n in SMALL},
                                             {n: vs[n] for n in SMALL})
    grads.update(g_s), deltas.update(d_s), new_m.update(m_s), new_v.update(v_s)
    return (loss[0, 0], grad_x[None], *[grads[n] for n in WEIGHTS], *[deltas[n] for n in WEIGHTS],
            *[new_m[n] for n in WEIGHTS], *[new_v[n] for n in WEIGHTS])
```

```python
import functools

import jax
import jax.numpy as jnp
from jax import lax
from jax.experimental import pallas as pl
from jax.experimental.pallas import tpu as pltpu

F32 = jnp.float32
BF16 = jnp.bfloat16

D = 1024
D_IN = 2816
D_FF = 2816
CHUNK = 64
SWA_W = 512
KV_W = 128
HG_W = 512
HD = 128
ZQH, ZFH, ZIH, ZGH = 768, 1280, 1792, 2304
XH, XD = 4, 256
EPS = 1e-6
NEG = -1e30
N_DEV = 8
MESH = pl.DeviceIdType.MESH

LR, B1, B2, AEPS, WD, STEP = 0.001, 0.9, 0.999, 1e-08, 0.01, 10
C1 = 1.0 - B1 ** STEP
C2 = 1.0 - B2 ** STEP

VMEM_LIMIT = 56 * 1024 * 1024


def _params(**kw):
    return pltpu.CompilerParams(vmem_limit_bytes=VMEM_LIMIT, **kw)


def _sig(x):
    return 1.0 / (1.0 + jnp.exp(-x))


def _rowsum8(x):
    r, w = x.shape
    return jnp.sum(x.reshape(r // 8, 8, w), axis=0)


def _dot(a, b, ca, cb, precision=None):
    return lax.dot_general(a, b, (((ca,), (cb,)), ((), ())), preferred_element_type=F32,
                           precision=precision)


ANY_SPEC = pl.BlockSpec(memory_space=pl.ANY)


def _mm(a, b, *, ta=False, tb=False, out_dtype, tm, tn, tk=None, name, dep=None, n_outer=False):
    m = a.shape[1] if ta else a.shape[0]
    k = a.shape[0] if ta else a.shape[1]
    n = b.shape[0] if tb else b.shape[1]
    tm, tn = min(tm, m), min(tn, n)
    tk = k if tk is None else min(tk, k)
    nk = k // tk
    assert m % tm == 0 and n % tn == 0 and k % tk == 0, (name, m, n, k, tm, tn, tk)
    ij = (lambda g0, g1: (g1, g0)) if n_outer else (lambda g0, g1: (g0, g1))
    a_spec = (pl.BlockSpec((tk, tm), lambda g0, g1, kk: (kk, ij(g0, g1)[0])) if ta
              else pl.BlockSpec((tm, tk), lambda g0, g1, kk: (ij(g0, g1)[0], kk)))
    b_spec = (pl.BlockSpec((tn, tk), lambda g0, g1, kk: (ij(g0, g1)[1], kk)) if tb
              else pl.BlockSpec((tk, tn), lambda g0, g1, kk: (kk, ij(g0, g1)[1])))
    ca, cb = (0 if ta else 1), (1 if tb else 0)

    deps = [] if dep is None else [dep]

    def body(a_ref, b_ref, *rest):
        o_ref, acc = rest[len(deps)], rest[len(deps) + 1:]
        p = _dot(a_ref[...].astype(BF16), b_ref[...].astype(BF16), ca, cb)
        if nk == 1:
            o_ref[...] = p.astype(out_dtype)
        else:
            acc_ref, = acc
            kk = pl.program_id(2)

            @pl.when(kk == 0)
            def _():
                acc_ref[...] = p

            @pl.when(kk > 0)
            def _():
                acc_ref[...] += p

            @pl.when(kk == nk - 1)
            def _():
                o_ref[...] = acc_ref[...].astype(out_dtype)

    return pl.pallas_call(
        body, name=name, out_shape=jax.ShapeDtypeStruct((m, n), out_dtype),
        grid=(n // tn, m // tm, nk) if n_outer else (m // tm, n // tn, nk),
        in_specs=[a_spec, b_spec] + [ANY_SPEC] * len(deps),
        out_specs=pl.BlockSpec((tm, tn), lambda g0, g1, kk: ij(g0, g1)),
        scratch_shapes=[pltpu.VMEM((tm, tn), F32)] if nk > 1 else [],
        compiler_params=_params(dimension_semantics=("parallel", "parallel", "arbitrary")),
    )(a, b, *deps)


def _mm2(a1, b1, a2, b2, *, tb=False, out_dtype, tm, name, dep=None):
    m, k = a1.shape
    n = b1.shape[0] if tb else b1.shape[1]
    tm = min(tm, m)
    assert m % tm == 0
    cb = 1 if tb else 0
    deps = [] if dep is None else [dep]

    def body(a1_ref, b1_ref, a2_ref, b2_ref, *rest):
        o_ref = rest[len(deps)]
        o_ref[...] = (_dot(a1_ref[...].astype(BF16), b1_ref[...], 1, cb)
                      + _dot(a2_ref[...].astype(BF16), b2_ref[...], 1, cb)).astype(out_dtype)

    a_spec = pl.BlockSpec((tm, k), lambda i: (i, 0))
    b_spec = pl.BlockSpec(b1.shape, lambda i: (0, 0))
    return pl.pallas_call(
        body, name=name, out_shape=jax.ShapeDtypeStruct((m, n), out_dtype),
        grid=(m // tm,), in_specs=[a_spec, b_spec, a_spec, b_spec] + [ANY_SPEC] * len(deps),
        out_specs=pl.BlockSpec((tm, n), lambda i: (i, 0)),
        compiler_params=_params(dimension_semantics=("parallel",)),
    )(a1, b1, a2, b2, *deps)


def _mm_rows(prods, rows_in, vecs_in, epilogue, outs, *, tm, name, dep=None):
    m = prods[0][0].shape[0]
    n = prods[0][1].shape[0] if prods[0][2] else prods[0][1].shape[1]
    tm = min(tm, m)
    assert m % tm == 0
    deps = [] if dep is None else [dep]
    n_p, n_r, n_v = len(prods), len(rows_in), len(vecs_in)

    def body(*refs):
        ab = refs[:2 * n_p]
        row_refs = refs[2 * n_p:2 * n_p + n_r]
        vec_refs = refs[2 * n_p + n_r:2 * n_p + n_r + n_v]
        out_refs = refs[2 * n_p + n_r + n_v + len(deps):]
        p = None
        for j, (_, _, tb) in enumerate(prods):
            t = _dot(ab[2 * j][...].astype(BF16), ab[2 * j + 1][...], 1, 1 if tb else 0)
            p = t if p is None else p + t
        vals = epilogue(p, *[r[...] for r in row_refs], *[v[...] for v in vec_refs])
        for (dtype, kind), o_ref, val in zip(outs, out_refs, vals):
            if kind == "row":
                o_ref[...] = val.astype(dtype)
            else:
                @pl.when(pl.program_id(0) == 0)
                def _(o_ref=o_ref):
                    o_ref[...] = jnp.zeros_like(o_ref)

                o_ref[...] += val

    row = lambda w: pl.BlockSpec((tm, w), lambda i: (i, 0))
    whole = lambda a: pl.BlockSpec(a.shape, lambda i: (0,) * a.ndim, pipeline_mode=pl.Buffered(1))
    in_specs, args = [], []
    for a, b, _ in prods:
        in_specs += [row(a.shape[1]), whole(b)]
        args += [a, b]
    in_specs += [row(r.shape[1]) for r in rows_in] + [whole(v) for v in vecs_in] + [ANY_SPEC] * len(deps)
    return pl.pallas_call(
        body, name=name,
        out_shape=tuple(jax.ShapeDtypeStruct((m, n) if kind == "row" else (8, n), dtype) for dtype, kind in outs),
        grid=(m // tm,), in_specs=in_specs,
        out_specs=tuple(row(n) if kind == "row" else pl.BlockSpec((8, n), lambda i: (0, 0)) for _, kind in outs),
        compiler_params=_params(dimension_semantics=("arbitrary",)),
    )(*args, *rows_in, *vecs_in, *deps)


def _rstd(x):
    return lax.rsqrt(jnp.mean(x * x, axis=-1, keepdims=True) + EPS)


def _norm_bwd(xh, r, t):
    return r * (t - xh * jnp.mean(xh * t, axis=-1, keepdims=True))


ROW_F32, ROW_BF16, SUM_F32 = (F32, "row"), (BF16, "row"), (F32, "sum")


def _ep_post_pre(p, h, g_post, g_pre):
    y = p.astype(BF16)
    yf = y.astype(F32)
    hn = h + yf * _rstd(yf) * g_post
    return y, hn, hn * _rstd(hn) * g_pre


_EP_POST_PRE_OUTS = [ROW_BF16, ROW_F32, ROW_BF16]


def _ep_final_loss(y, h, target, g_post):
    r = _rstd(y)
    yh = y * r
    err = h + yh * g_post - target
    dh = err * (1.0 / D)
    return _rowsum8(err * err), dh, _norm_bwd(yh, r, dh * g_post), _rowsum8(dh * yh)


_EP_FINAL_LOSS_OUTS = [SUM_F32, ROW_F32, ROW_BF16, SUM_F32]


def _ep_post_pre_bwd(du, dh_out, hn, y, g_post, g_pre):
    r2 = _rstd(hn)
    xh = hn * r2
    dh = dh_out + _norm_bwd(xh, r2, du * g_pre)
    yf = y.astype(F32)
    r1 = _rstd(yf)
    yh = yf * r1
    return dh, _norm_bwd(yh, r1, dh * g_post), _rowsum8(du * xh), _rowsum8(dh * yh)


_EP_POST_PRE_BWD_OUTS = [ROW_F32, ROW_BF16, SUM_F32, SUM_F32]


def _ep_pre_bwd(du, dh_out, x, g):
    r = _rstd(x)
    xh = x * r
    return dh_out + _norm_bwd(xh, r, du * g), _rowsum8(du * xh)


_EP_PRE_BWD_OUTS = [ROW_F32, SUM_F32]


def _prenorm(x, g, *, name, dep=None):
    t, d = x.shape
    tb = min(512, t)
    deps = [] if dep is None else [dep]

    def body(x_ref, g_ref, *rest):
        xf = x_ref[...]
        rest[-1][...] = (xf * _rstd(xf) * g_ref[...]).astype(BF16)

    return pl.pallas_call(
        body, name=name, out_shape=jax.ShapeDtypeStruct((t, d), BF16), grid=(t // tb,),
        in_specs=[pl.BlockSpec((tb, d), lambda i: (i, 0)), pl.BlockSpec((1, d), lambda i: (0, 0))]
        + [ANY_SPEC] * len(deps),
        out_specs=pl.BlockSpec((tb, d), lambda i: (i, 0)), compiler_params=_params(),
    )(x, g, *deps)


QB = 256


def _half_mask(shape, e):
    lane = lax.broadcasted_iota(jnp.int32, shape, len(shape) - 1)
    return (lane // 64) == e


def _place(kv):
    sw = pltpu.roll(kv, 64, 1)
    m0 = _half_mask(kv.shape, 0)
    return [[jnp.where(m0, kv, 0.0).astype(BF16), jnp.where(m0, 0.0, sw).astype(BF16)],
            [jnp.where(m0, sw, 0.0).astype(BF16), jnp.where(m0, 0.0, kv).astype(BF16)]]


def _swa_valid_q(i, nq, nk):
    qc = lax.broadcasted_iota(jnp.int32, (nq, nk), 0) // CHUNK
    kc = lax.broadcasted_iota(jnp.int32, (nq, nk), 1) // CHUNK - 2
    return (kc <= qc) & (qc <= kc + 2) & (4 * i + kc >= 0)


def _swa_fwd(z, sinks, t):
    nb = t // QB

    def body(s_ref, q_ref, kp_ref, kc_ref, vp_ref, vc_ref, o_ref, lse_ref):
        i = pl.program_id(0)
        kpl = _place(jnp.concatenate([kp_ref[...], kc_ref[...]], axis=0))
        vpl = _place(jnp.concatenate([vp_ref[...], vc_ref[...]], axis=0))
        valid = _swa_valid_q(i, QB, QB + 128)
        lane = lax.broadcasted_iota(jnp.int32, (QB, 128), 1)
        lse_out = jnp.zeros((QB, 128), F32)
        for j in range(4):
            qp = q_ref[:, 128 * j:128 * (j + 1)].astype(BF16)
            acc = jnp.zeros((QB, 128), F32)
            for e in range(2):
                h = 2 * j + e
                kvh = h // 4
                qm = jnp.where(_half_mask(qp.shape, e), qp, jnp.zeros_like(qp))
                s = _dot(qm, kpl[kvh][e], 1, 1) * 0.125
                s = jnp.where(valid, s, NEG)
                sink = s_ref[0, h]
                m = jnp.maximum(jnp.max(s, axis=-1, keepdims=True), sink)
                p = jnp.exp(s - m)
                l = jnp.sum(p, axis=-1, keepdims=True) + jnp.exp(sink - m)
                acc = acc + _dot(p.astype(BF16), vpl[kvh][e], 1, 0) * (1.0 / l)
                lse_out = jnp.where(lane == h, m + jnp.log(l), lse_out)
            o_ref[:, 128 * j:128 * (j + 1)] = acc.astype(BF16)
        lse_ref[...] = lse_out

    prev = lambda c: pl.BlockSpec((128, 128), lambda i: (jnp.maximum(2 * i - 1, 0), c))
    cur = lambda c: pl.BlockSpec((QB, 128), lambda i: (i, c))
    return pl.pallas_call(
        body, name="swa_fwd",
        out_shape=(jax.ShapeDtypeStruct((t, D), BF16), jax.ShapeDtypeStruct((t, 128), F32)),
        grid=(nb,),
        in_specs=[pl.BlockSpec(memory_space=pltpu.SMEM),
                  pl.BlockSpec((QB, SWA_W), lambda i: (i, 0)), prev(4), cur(4), prev(5), cur(5)],
        out_specs=(pl.BlockSpec((QB, SWA_W), lambda i: (i, 0)), pl.BlockSpec((QB, 128), lambda i: (i, 0))),
        compiler_params=_params(),
    )(sinks, z, z, z, z, z)


def _swa_bwd(z, sinks, ymix, lse, dymix, t):
    nb = t // QB
    nk = QB + 128

    def body(s_ref, q_ref, kp_ref, kc_ref, vp_ref, vc_ref, o_ref, do_ref, l_ref,
             dq_ref, first_ref, second_ref, ds_ref, carry_ref):
        i = pl.program_id(0)
        live = i < nb

        @pl.when(i == 0)
        def _():
            ds_ref[...] = jnp.zeros_like(ds_ref)
            carry_ref[...] = jnp.zeros_like(carry_ref)

        lane = lax.broadcasted_iota(jnp.int32, (8, 128), 1)
        kpl = _place(jnp.concatenate([kp_ref[...], kc_ref[...]], axis=0))
        vpl = _place(jnp.concatenate([vp_ref[...], vc_ref[...]], axis=0))
        valid = _swa_valid_q(i, QB, nk) & live
        lse_c = l_ref[...]
        dsink = jnp.zeros((8, 128), F32)
        dk_acc = [[jnp.zeros((nk, 128), F32) for _ in range(2)] for _ in range(2)]
        dv_acc = [[jnp.zeros((nk, 128), F32) for _ in range(2)] for _ in range(2)]
        dq = []
        for j in range(4):
            cols = slice(128 * j, 128 * (j + 1))
            qp = q_ref[:, cols].astype(BF16)
            dop = do_ref[:, cols]
            prod = dop.astype(F32) * o_ref[:, cols].astype(F32)
            acc = jnp.zeros((QB, 128), F32)
            for e in range(2):
                h = 2 * j + e
                kvh = h // 4
                hm = _half_mask(qp.shape, e)
                qm = jnp.where(hm, qp, jnp.zeros_like(qp))
                dom = jnp.where(hm, dop, jnp.zeros_like(dop))
                dd = jnp.sum(jnp.where(hm, prod, 0.0), axis=-1, keepdims=True)
                lse_h = lse_c[:, h:h + 1]
                s = _dot(qm, kpl[kvh][e], 1, 1) * 0.125
                p = jnp.where(valid, jnp.exp(s - lse_h), 0.0)
                dp = _dot(dom, vpl[kvh][e], 1, 1)
                ds = (p * (dp - dd) * 0.125).astype(BF16)
                acc = acc + _dot(ds, kpl[kvh][e], 1, 0)
                dk_acc[kvh][e] = dk_acc[kvh][e] + _dot(ds, qm, 0, 0)
                dv_acc[kvh][e] = dv_acc[kvh][e] + _dot(p.astype(BF16), dom, 0, 0)
                ps = jnp.where(live, jnp.exp(s_ref[0, h] - lse_h) * dd, 0.0)
                dsink = dsink - jnp.where(lane == h, _rowsum8(jnp.broadcast_to(ps, (QB, 128))), 0.0)
            dq.append(acc.astype(BF16))
        ds_ref[...] += dsink
        dk = dk_acc[0][0] + dk_acc[1][1] + pltpu.roll(dk_acc[0][1] + dk_acc[1][0], 64, 1)
        dv = dv_acc[0][0] + dv_acc[1][1] + pltpu.roll(dv_acc[0][1] + dv_acc[1][0], 64, 1)
        dkv = jnp.concatenate([dk, dv], axis=1)
        second_ref[...] = (carry_ref[...] + dkv[0:128]).astype(BF16)
        carry_ref[...] = dkv[256:384]

        @pl.when(live)
        def _():
            for j in range(4):
                dq_ref[:, 128 * j:128 * (j + 1)] = dq[j]
            first_ref[...] = dkv[128:256].astype(BF16)

    blk = lambda i: jnp.minimum(i, nb - 1)
    prev = lambda c: pl.BlockSpec((128, 128), lambda i: (jnp.maximum(2 * blk(i) - 1, 0), c))
    cur = lambda w, c: pl.BlockSpec((QB, w), lambda i: (blk(i), c))
    half = lambda index: pl.BlockSpec((128, 256), lambda i: (index(i), 0))
    return pl.pallas_call(
        body, name="swa_bwd",
        out_shape=(jax.ShapeDtypeStruct((t, SWA_W), BF16), jax.ShapeDtypeStruct((t // 2, 256), BF16),
                   jax.ShapeDtypeStruct((t // 2, 256), BF16), jax.ShapeDtypeStruct((8, 128), F32)),
        grid=(nb + 1,),
        in_specs=[pl.BlockSpec(memory_space=pltpu.SMEM),
                  cur(SWA_W, 0), prev(4), cur(128, 4), prev(5), cur(128, 5),
                  cur(SWA_W, 0), cur(SWA_W, 0), cur(128, 0)],
        out_specs=(cur(SWA_W, 0), half(blk), half(lambda i: jnp.maximum(i - 1, 0)),
                   pl.BlockSpec((8, 128), lambda i: (0, 0))),
        scratch_shapes=[pltpu.VMEM((128, 256), F32)],
        compiler_params=_params(dimension_semantics=("arbitrary",)),
    )(sinks, z, z, z, z, z, ymix, dymix, lse)


HB = 256


def _lower_bound(lb_ref):
    a = lb_ref[...]
    a0, a1 = a[0:1], a[1:2]
    mx = jnp.maximum(a0, a1)
    e0, e1 = jnp.exp(a0 - mx), jnp.exp(a1 - mx)
    return e0 / (e0 + e1)


def _hgrn_cols(row_block):
    return [pl.BlockSpec((HB, 2 * HD), lambda j, c=base // (2 * HD) + p: (row_block(j), c))
            for base in (ZQH, ZFH, ZIH, ZGH) for p in range(2)]


NCH = HB // CHUNK


def _split3(x):
    hi = x.astype(BF16)
    r1 = x - hi.astype(F32)
    mid = r1.astype(BF16)
    return hi, mid, (r1 - mid.astype(F32)).astype(BF16)


def _blockdiag(lower):
    r = lax.broadcasted_iota(jnp.int32, (HB, HB), 0)
    c = lax.broadcasted_iota(jnp.int32, (HB, HB), 1)
    return (r // CHUNK == c // CHUNK) & ((c <= r) if lower else (c >= r))


def _chunk_sums(mask_bf16, x):
    return sum(_dot(mask_bf16, part, 1, 0) for part in _split3(x))


def _per_chunk_rows(x, row):
    w = x.shape[1]
    picked = x.reshape(NCH, CHUNK, w)[:, row:row + 1, :]
    return jnp.broadcast_to(picked, (NCH, CHUNK, w)).reshape(HB, w)


def _chunk_stack(x, chunk_of_row):
    return jnp.concatenate([jnp.where(chunk_of_row == c, x, jnp.zeros_like(x)) for c in range(NCH)], axis=1)


def _chunk_pick(x, chunk_of_row):
    w = x.shape[1] // NCH
    out = jnp.zeros((HB, w), x.dtype)
    for c in range(NCH):
        out = jnp.where(chunk_of_row == c, x[:, c * w:(c + 1) * w], out)
    return out


def _hgrn_local(q, f, kf, b):
    sq = _sig(q)
    qf = q * sq * (HD ** -0.5)
    b_mid = _per_chunk_rows(b, CHUNK // 2 - 1)
    b_last = _per_chunk_rows(b, CHUNK - 1)
    qm = qf * jnp.exp(b - b_mid)
    km = kf * jnp.exp(b_mid - b)
    kl = kf * jnp.exp(b_last - b)
    qb = qf * jnp.exp(b)
    return dict(sq=sq, b_mid=b_mid, b_last=b_last, qm=qm, km=km, kl=kl, qb=qb)


def _hgrn2_fwd(z, hgrn_lb, onorm, ymix, t):
    nb = t // HB

    def body(*refs):
        zq, zf, zi, zg = refs[0:2], refs[2:4], refs[4:6], refs[6:8]
        lb_ref, on_ref, _, y_ref, o_ref, sp_ref, st_ref = refs[8:]

        @pl.when(pl.program_id(0) == 0)
        def _():
            st_ref[...] = jnp.zeros_like(st_ref)

        lb_all = _lower_bound(lb_ref)
        gn = on_ref[...]
        low = _blockdiag(True)
        low_b = low.astype(BF16)
        chunk_of_row = lax.broadcasted_iota(jnp.int32, (HB, HD), 0) // CHUNK
        for p in range(2):
            lbp = lb_all[:, 2 * HD * p:2 * HD * (p + 1)]
            fp = lbp + (1.0 - lbp) * _sig(zf[p][...])
            bp = _chunk_sums(low_b, jnp.log(fp))
            for e in range(2):
                h, ls = 2 * p + e, slice(e * HD, (e + 1) * HD)
                f = fp[:, ls]
                w = _hgrn_local(zq[p][:, ls], f, 1.0 - f, bp[:, ls])
                iv = zi[p][:, ls].astype(BF16)
                a = jnp.where(low, _dot(w["qm"].astype(BF16), w["km"].astype(BF16), 1, 1), 0.0)
                o = _dot(a.astype(BF16), iv, 1, 0)
                u = _dot(iv, _chunk_stack(w["kl"].astype(BF16), chunk_of_row), 0, 0)
                decay = jnp.exp(w["b_last"])
                st = st_ref[h]
                states = []
                for c in range(NCH):
                    sp_ref[h, c] = st
                    states.append(st.astype(BF16))
                    st = st * decay[c * CHUNK:c * CHUNK + 1] + u[:, c * HD:(c + 1) * HD]
                st_ref[h] = st
                inter = _dot(w["qb"].astype(BF16), jnp.concatenate(states, axis=0), 1, 1)
                o = o + _chunk_pick(inter, chunk_of_row)
                hs = slice(h * HD, (h + 1) * HD)
                o_ref[:, hs] = o
                gg = zg[p][:, ls]
                y_ref[:, hs] = (o * _rstd(o) * gn * (gg * _sig(gg))).astype(BF16)

    return pl.pallas_call(
        body, name="hgrn_fwd",
        out_shape=(jax.ShapeDtypeStruct((t, D), BF16), jax.ShapeDtypeStruct((t, HG_W), F32),
                   jax.ShapeDtypeStruct((4, t // CHUNK, HD, HD), F32)),
        grid=(nb,),
        in_specs=_hgrn_cols(lambda j: j) + [pl.BlockSpec((2, HG_W), lambda j: (0, 0)),
                                            pl.BlockSpec((1, HD), lambda j: (0, 0)), ANY_SPEC],
        out_specs=(pl.BlockSpec((HB, HG_W), lambda j: (j, 1)),
                   pl.BlockSpec((HB, HG_W), lambda j: (j, 0)),
                   pl.BlockSpec((4, NCH, HD, HD), lambda j: (0, j, 0, 0))),
        scratch_shapes=[pltpu.VMEM((4, HD, HD), F32)],
        input_output_aliases={10: 0},
        compiler_params=_params(dimension_semantics=("arbitrary",)),
    )(*[z] * 8, hgrn_lb, onorm, ymix)


def _hgrn2_bwd(z, hgrn_lb, onorm, o_save, sprev, dymix, dza, t):
    nb = t // HB

    def body(*refs):
        zq, zf, zi, zg = refs[0:2], refs[2:4], refs[4:6], refs[6:8]
        (lb_ref, on_ref, o_ref, sp_ref, dy_ref, dqa_ref, first_ref, second_ref,
         dz_ref, dlb_ref, don_ref, dst_ref) = refs[8:]

        @pl.when(pl.program_id(0) == 0)
        def _():
            dst_ref[...] = jnp.zeros_like(dst_ref)
            dlb_ref[...] = jnp.zeros_like(dlb_ref)
            don_ref[...] = jnp.zeros_like(don_ref)

        dz_ref[:, 0:SWA_W] = dqa_ref[...]
        dz_ref[0:HB // 2, SWA_W:ZQH] = first_ref[...]
        dz_ref[HB // 2:HB, SWA_W:ZQH] = second_ref[...]
        lb_all = _lower_bound(lb_ref)
        gn = on_ref[...]
        low, upp = _blockdiag(True), _blockdiag(False)
        upp_b = upp.astype(BF16)
        low_b = low.astype(BF16)
        row = lax.broadcasted_iota(jnp.int32, (HB, HD), 0)
        chunk_of_row = row // CHUNK
        in_chunk = row % CHUNK
        for p in range(2):
            lbp = lb_all[:, 2 * HD * p:2 * HD * (p + 1)]
            sgp = _sig(zf[p][...])
            fp = lbp + (1.0 - lbp) * sgp
            bp = _chunk_sums(low_b, jnp.log(fp))
            db_pair, dkf_pair = [], []
            for e in range(2):
                h, ls, hs = 2 * p + e, slice(e * HD, (e + 1) * HD), slice((2 * p + e) * HD, (2 * p + e + 1) * HD)
                f = fp[:, ls]
                q = zq[p][:, ls]
                w = _hgrn_local(q, f, 1.0 - f, bp[:, ls])
                iv = zi[p][:, ls].astype(BF16)
                gg = zg[p][:, ls]
                o = o_ref[:, hs]
                dout = dy_ref[:, hs].astype(F32)
                sgg = _sig(gg)
                r = _rstd(o)
                oh = o * r
                dyn = dout * (gg * sgg)
                dz_ref[:, ZGH + h * HD:ZGH + (h + 1) * HD] = (
                    dout * oh * gn * (sgg * (1.0 + gg * (1.0 - sgg)))).astype(BF16)
                don_ref[...] += _rowsum8(dyn * oh)
                do = _norm_bwd(oh, r, dyn * gn).astype(BF16)
                qm, km, kl, qb = (w[n].astype(BF16) for n in ("qm", "km", "kl", "qb"))
                decay = jnp.exp(w["b_last"])
                grads_in = _dot(do, _chunk_stack(qb, chunk_of_row), 0, 0)
                dst = dst_ref[h]
                dstn, dd_rows = [None] * NCH, [None] * NCH
                for c in reversed(range(NCH)):
                    dstn[c] = dst.astype(BF16)
                    dd_rows[c] = jnp.sum(dst * sp_ref[h, c], axis=0, keepdims=True)
                    dst = dst * decay[c * CHUNK:c * CHUNK + 1] + grads_in[:, c * HD:(c + 1) * HD]
                dst_ref[h] = dst
                states = jnp.concatenate([sp_ref[h, c].astype(BF16) for c in range(NCH)], axis=0)
                dstn_all = jnp.concatenate(dstn, axis=0)
                dqb = _dot(_chunk_stack(do, chunk_of_row), states, 1, 0)
                at = jnp.where(upp, _dot(km, qm, 1, 1), 0.0)
                di = _dot(at.astype(BF16), do, 1, 0) + _chunk_pick(_dot(kl, dstn_all, 1, 1), chunk_of_row)
                dz_ref[:, ZIH + h * HD:ZIH + (h + 1) * HD] = di.astype(BF16)
                dkl = _dot(_chunk_stack(iv, chunk_of_row), dstn_all, 1, 0)
                da = jnp.where(low, _dot(do, iv, 1, 1), 0.0).astype(BF16)
                dat = jnp.where(upp, _dot(iv, do, 1, 1), 0.0).astype(BF16)
                dqm = _dot(da, km, 1, 0)
                dkm = _dot(dat, qm, 1, 0)
                b = bp[:, ls]
                e1, e2 = jnp.exp(b - w["b_mid"]), jnp.exp(w["b_mid"] - b)
                e3, e4 = jnp.exp(w["b_last"] - b), jnp.exp(b)
                dqf = dqm * e1 + dqb * e4
                dkf_pair.append(dkm * e2 + dkl * e3)
                t_qm, t_km, t_kl = dqm * w["qm"], dkm * w["km"], dkl * w["kl"]
                db = t_qm - t_km - t_kl + dqb * w["qb"]
                db_mid = jnp.sum((t_km - t_qm).reshape(NCH, CHUNK, HD), axis=1, keepdims=True)
                db_last = jnp.sum(t_kl.reshape(NCH, CHUNK, HD), axis=1, keepdims=True)
                db_last = db_last + jnp.stack(dd_rows, axis=0) * jnp.exp(
                    bp[:, ls].reshape(NCH, CHUNK, HD)[:, CHUNK - 1:CHUNK, :])
                spread = lambda v: jnp.broadcast_to(v, (NCH, CHUNK, HD)).reshape(HB, HD)
                db = (db + jnp.where(in_chunk == CHUNK // 2 - 1, spread(db_mid), 0.0)
                      + jnp.where(in_chunk == CHUNK - 1, spread(db_last), 0.0))
                db_pair.append(db)
                sq = w["sq"]
                dz_ref[:, ZQH + h * HD:ZQH + (h + 1) * HD] = (
                    dqf * (HD ** -0.5) * (sq * (1.0 + q * (1.0 - sq)))).astype(BF16)
            dlogf = _chunk_sums(upp_b, jnp.concatenate(db_pair, axis=1))
            dfv = dlogf / fp - jnp.concatenate(dkf_pair, axis=1)
            dz_ref[:, ZFH + 2 * HD * p:ZFH + 2 * HD * (p + 1)] = (dfv * (1.0 - lbp) * sgp * (1.0 - sgp)).astype(BF16)
            dlb_ref[:, 2 * HD * p:2 * HD * (p + 1)] += _rowsum8(dfv * (1.0 - sgp))

    rev = lambda j: nb - 1 - j
    return pl.pallas_call(
        body, name="hgrn_bwd",
        out_shape=(jax.ShapeDtypeStruct((t, D_IN), BF16), jax.ShapeDtypeStruct((8, HG_W), F32),
                   jax.ShapeDtypeStruct((8, HD), F32)),
        grid=(nb,),
        in_specs=_hgrn_cols(rev) + [pl.BlockSpec((2, HG_W), lambda j: (0, 0)), pl.BlockSpec((1, HD), lambda j: (0, 0)),
                                    pl.BlockSpec((HB, HG_W), lambda j: (rev(j), 0)),
                                    pl.BlockSpec((4, NCH, HD, HD), lambda j: (0, rev(j), 0, 0)),
                                    pl.BlockSpec((HB, HG_W), lambda j: (rev(j), 1)),
                                    pl.BlockSpec((HB, SWA_W), lambda j: (rev(j), 0)),
                                    pl.BlockSpec((HB // 2, 2 * KV_W), lambda j: (rev(j), 0)),
                                    pl.BlockSpec((HB // 2, 2 * KV_W), lambda j: (rev(j), 0))],
        out_specs=(pl.BlockSpec((HB, D_IN), lambda j: (rev(j), 0)), pl.BlockSpec((8, HG_W), lambda j: (0, 0)),
                   pl.BlockSpec((8, HD), lambda j: (0, 0))),
        scratch_shapes=[pltpu.VMEM((4, HD, HD), F32)],
        compiler_params=_params(dimension_semantics=("arbitrary",)),
    )(*[z] * 8, hgrn_lb, onorm, o_save, sprev, dymix, *dza)


XB = 512


def _xattn_fwd(q, k, v, t):
    tb = min(XB, t)

    def body(q_ref, k_ref, v_ref, o_ref):
        for h in range(XH):
            cols = slice(XD * h, XD * (h + 1))
            s = _dot(q_ref[:, cols], k_ref[:, cols], 1, 1) * (XD ** -0.5)
            p = jnp.exp(s - jnp.max(s, axis=-1, keepdims=True))
            l = jnp.sum(p, axis=-1, keepdims=True)
            o_ref[:, cols] = (_dot(p.astype(BF16), v_ref[:, cols], 1, 0) * (1.0 / l)).astype(BF16)

    row = pl.BlockSpec((tb, D), lambda i: (i, 0))
    mem = pl.BlockSpec(k.shape, lambda i: (0, 0))
    return pl.pallas_call(
        body, name="xattn_fwd", out_shape=jax.ShapeDtypeStruct((t, D), BF16), grid=(t // tb,),
        in_specs=[row, mem, mem], out_specs=row, compiler_params=_params(),
    )(q, k, v)


def _xattn_bwd(q, k, v, do, t):
    tb = min(XB, t)

    def body(q_ref, k_ref, v_ref, do_ref, dq_ref, dk_ref, dv_ref):
        @pl.when(pl.program_id(0) == 0)
        def _():
            dk_ref[...] = jnp.zeros_like(dk_ref)
            dv_ref[...] = jnp.zeros_like(dv_ref)

        for h in range(XH):
            cols = slice(XD * h, XD * (h + 1))
            qh, kh, vh, doh = q_ref[:, cols], k_ref[:, cols], v_ref[:, cols], do_ref[:, cols]
            s = _dot(qh, kh, 1, 1) * (XD ** -0.5)
            p = jnp.exp(s - jnp.max(s, axis=-1, keepdims=True))
            p = p * (1.0 / jnp.sum(p, axis=-1, keepdims=True))
            dp = _dot(doh, vh, 1, 1)
            ds = (p * (dp - jnp.sum(p * dp, axis=-1, keepdims=True)) * (XD ** -0.5)).astype(BF16)
            dq_ref[:, cols] = _dot(ds, kh, 1, 0).astype(BF16)
            dk_ref[:, cols] += _dot(ds, qh, 0, 0)
            dv_ref[:, cols] += _dot(p.astype(BF16), doh, 0, 0)

    row = pl.BlockSpec((tb, D), lambda i: (i, 0))
    mem = pl.BlockSpec(k.shape, lambda i: (0, 0))
    return pl.pallas_call(
        body, name="xattn_bwd",
        out_shape=(jax.ShapeDtypeStruct((t, D), BF16), jax.ShapeDtypeStruct(k.shape, F32),
                   jax.ShapeDtypeStruct(k.shape, F32)),
        grid=(t // tb,), in_specs=[row, mem, mem, row], out_specs=(row, mem, mem),
        compiler_params=_params(dimension_semantics=("arbitrary",)),
    )(q, k, v, do)


def _mem_gain_bwd(dm, mem, *, name):
    def body(dm_ref, m_ref, dg_ref):
        m_ = m_ref[...]
        dg_ref[...] = _rowsum8(dm_ref[...] * (m_ * _rstd(m_)))

    return pl.pallas_call(body, name=name, out_shape=jax.ShapeDtypeStruct((8, D), F32),
                          compiler_params=_params())(dm, mem)


FM, FN = 512, 1408


def _ffn_up(u, wgt, wut, t):
    tm = min(FM, t)

    def body(u_ref, wg_ref, wu_ref, g_ref, up_ref, a_ref):
        u_ = u_ref[...]
        g = _dot(u_, wg_ref[...], 1, 1)
        up = _dot(u_, wu_ref[...], 1, 1)
        g_ref[...] = g.astype(BF16)
        up_ref[...] = up.astype(BF16)
        a_ref[...] = (g * _sig(g) * up).astype(BF16)

    w = pl.BlockSpec((FN, D), lambda j, i: (j, 0))
    o = pl.BlockSpec((tm, FN), lambda j, i: (i, j))
    return pl.pallas_call(
        body, name="ffn_up", out_shape=(jax.ShapeDtypeStruct((t, D_FF), BF16),) * 3,
        grid=(D_FF // FN, t // tm), in_specs=[pl.BlockSpec((tm, D), lambda j, i: (i, 0)), w, w],
        out_specs=(o, o, o), compiler_params=_params(),
    )(u, wgt, wut)


def _ffn_down_bwd(dy, wd, gate, up, t, dep=None):
    tm = min(FM, t)
    deps = [] if dep is None else [dep]

    def body(dy_ref, w_ref, g_ref, up_ref, *rest):
        dg_ref, dup_ref = rest[len(deps):]
        da = _dot(dy_ref[...], w_ref[...], 1, 1)
        g = g_ref[...].astype(F32)
        sg = _sig(g)
        dup_ref[...] = (da * g * sg).astype(BF16)
        dg_ref[...] = (da * up_ref[...].astype(F32) * (sg * (1.0 + g * (1.0 - sg)))).astype(BF16)

    o = pl.BlockSpec((tm, FN), lambda j, i: (i, j))
    return pl.pallas_call(
        body, name="ffn_down_bwd", out_shape=(jax.ShapeDtypeStruct((t, D_FF), BF16),) * 2,
        grid=(D_FF // FN, t // tm),
        in_specs=[pl.BlockSpec((tm, D), lambda j, i: (i, 0)), pl.BlockSpec((FN, D), lambda j, i: (j, 0)), o, o]
        + [ANY_SPEC] * len(deps),
        out_specs=(o, o), compiler_params=_params(),
    )(dy, wd, gate, up, *deps)


def _local_step(x, mem, target, fetch, sm, emit=None, first_dep=None):
    t = x.shape[0]
    w, gw = {}, {}

    def out(key, g):
        gw[key] = g
        return None if emit is None else emit(key, g)
    u1 = _prenorm(x, sm["g_mix_pre"], name="prenorm_mix", dep=first_dep)
    w["winT"] = fetch("winT", u1)
    z = _mm(u1, w["winT"], tb=True, out_dtype=F32, tm=1024, tn=1408, name="mm_z", n_outer=True)
    ymix, lse = _swa_fwd(z, sm["sinks"], t)
    ymix, o_h, sprev = _hgrn2_fwd(z, sm["hgrn_lb"], sm["hgrn_onorm"], ymix, t)
    w["wout"] = fetch("wout", ymix)
    y1, h1, u2 = _mm_rows([(ymix, w["wout"], False)], [x], [sm["g_mix_post"], sm["g_x_pre"]], _ep_post_pre,
                          _EP_POST_PRE_OUTS, tm=512, name="mm_y1_post")
    mn = _prenorm(mem, sm["g_mem"], name="prenorm_mem")
    for key in ("wq", "wk", "wv"):
        w[key] = fetch(key, u2)
    qx = _mm(u2, w["wq"], out_dtype=BF16, tm=1024, tn=1024, name="mm_qx")
    kx = _mm(mn, w["wk"], out_dtype=BF16, tm=1024, tn=1024, name="mm_kx")
    vx = _mm(mn, w["wv"], out_dtype=BF16, tm=1024, tn=1024, name="mm_vx")
    ox = _xattn_fwd(qx, kx, vx, t)
    w["wo"] = fetch("wo", ox)
    y2, h2, u3 = _mm_rows([(ox, w["wo"], False)], [h1], [sm["g_x_post"], sm["g_ffn_pre"]], _ep_post_pre,
                          _EP_POST_PRE_OUTS, tm=512, name="mm_y2_post")
    w["wgT"], w["wuT"] = fetch("wgT", u3), fetch("wuT", u3)
    gate, up, act = _ffn_up(u3, w["wgT"], w["wuT"], t)
    w["wd"] = fetch("wd", act)
    sq, dh3, dy3, dg_ffn_post = _mm_rows([(act, w["wd"], False)], [h2, target], [sm["g_ffn_post"]], _ep_final_loss,
                                         _EP_FINAL_LOSS_OUTS, tm=512, name="mm_y3_loss")
    dep = out("wd", _mm(act, dy3, ta=True, out_dtype=BF16, tm=1408, tn=1024, name="mm_gwd"))
    dgate, dup = _ffn_down_bwd(dy3, w["wd"], gate, up, t, dep=dep)
    dep = out("wgT", _mm(dgate, u3, ta=True, out_dtype=BF16, tm=1408, tn=1024, name="mm_gwg"))
    dep = out("wuT", _mm(dup, u3, ta=True, out_dtype=BF16, tm=1408, tn=1024, name="mm_gwu", dep=dep))
    dh2, dy2, dg_ffn_pre, dg_x_post = _mm_rows(
        [(dgate, w["wgT"], False), (dup, w["wuT"], False)], [dh3, h2, y2], [sm["g_x_post"], sm["g_ffn_pre"]],
        _ep_post_pre_bwd, _EP_POST_PRE_BWD_OUTS, tm=512, name="mm_du3_post_bwd", dep=dep)
    dep = out("wo", _mm(ox, dy2, ta=True, out_dtype=BF16, tm=512, tn=1024, name="mm_gwo"))
    dox = _mm(dy2, w["wo"], tb=True, out_dtype=BF16, tm=1024, tn=1024, name="mm_dox", dep=dep)
    dqx, dkx, dvx = _xattn_bwd(qx, kx, vx, dox, t)
    dep = out("wq", _mm(u2, dqx, ta=True, out_dtype=BF16, tm=512, tn=1024, name="mm_gwq"))
    dep = out("wk", _mm(mn, dkx, ta=True, out_dtype=BF16, tm=1024, tn=1024, name="mm_gwk", dep=dep))
    dep = out("wv", _mm(mn, dvx, ta=True, out_dtype=BF16, tm=1024, tn=1024, name="mm_gwv", dep=dep))
    dh1, dy1, dg_x_pre, dg_mix_post = _mm_rows(
        [(dqx, w["wq"], True)], [dh2, h1, y1], [sm["g_mix_post"], sm["g_x_pre"]],
        _ep_post_pre_bwd, _EP_POST_PRE_BWD_OUTS, tm=512, name="mm_du2_post_bwd", dep=dep)
    dmn = _mm2(dkx, w["wk"], dvx, w["wv"], tb=True, out_dtype=F32, tm=256, name="mm_dmn")
    dg_mem = _mem_gain_bwd(dmn, mem, name="mem_gain_bwd")
    dep = out("wout", _mm(ymix, dy1, ta=True, out_dtype=BF16, tm=512, tn=1024, name="mm_gwout"))
    dymix = _mm(dy1, w["wout"], tb=True, out_dtype=BF16, tm=1024, tn=1024, name="mm_dymix", dep=dep)
    *dza, dsinks = _swa_bwd(z, sm["sinks"], ymix, lse, dymix, t)
    dz, dlb, donorm = _hgrn2_bwd(z, sm["hgrn_lb"], sm["hgrn_onorm"], o_h, sprev, dymix, dza, t)
    dep = out("winT", _mm(dz, u1, ta=True, out_dtype=BF16, tm=1408, tn=1024, name="mm_gwin"))
    grad_x, dg_mix_pre = _mm_rows([(dz, w["winT"], False)], [dh1, x], [sm["g_mix_pre"]], _ep_pre_bwd,
                                  _EP_PRE_BWD_OUTS, tm=512, name="mm_du1_pre_bwd", dep=dep)
    parts = dict(g_mix_pre=dg_mix_pre, g_mix_post=dg_mix_post, g_mem=dg_mem, g_x_pre=dg_x_pre,
                 g_x_post=dg_x_post, g_ffn_pre=dg_ffn_pre, g_ffn_post=dg_ffn_post,
                 hgrn_onorm=donorm, hgrn_lb=dlb, sinks=dsinks, sq=sq)
    return grad_x, gw, parts


def _position():
    return lax.axis_index("x"), lax.axis_index("y"), lax.axis_index("c")


def _peer(pos, k):
    x, y, c = pos
    return (1 - x if k & 4 else x, 1 - y if k & 2 else y, 1 - c if k & 1 else c)


def _linear(pos):
    x, y, c = pos
    return 4 * x + 2 * y + c


HBM_SPEC = pl.BlockSpec(memory_space=pltpu.HBM)
SEM_SPEC = pl.BlockSpec(memory_space=pltpu.SEMAPHORE)
DATAFLOW = pltpu.SideEffectType.DATAFLOW_SIDE_EFFECTING
SEND_ORDER = (1, 2, 4, 3, 5, 6, 7)


def _in_hbm(a):
    return pltpu.with_memory_space_constraint(a, pltpu.HBM)


def _prepare_weights(shards, *, name, dep=None):
    n = len(shards)
    deps = [] if dep is None else [dep]

    def body(*refs):
        ins, (outs, lands, sem) = refs[:n], (refs[-2 * n - 1:-n - 1], refs[-n - 1:-1], refs[-1])
        me_lin = _linear(_position())
        copies = []
        for a in range(n):
            r = ins[a].shape[0]
            outs[a][...] = ins[a][...].astype(BF16)
            copies.append(pltpu.make_async_copy(outs[a], lands[a].at[pl.ds(me_lin * r, r), :], sem.at[a]))
            copies[-1].start()
        for cp in copies:
            cp.wait()

    vmem = pl.BlockSpec(memory_space=pltpu.VMEM)
    res = pl.pallas_call(
        body, name=name,
        out_shape=tuple(jax.ShapeDtypeStruct(s.shape, BF16) for s in shards)
        + tuple(jax.ShapeDtypeStruct((N_DEV * s.shape[0], s.shape[1]), BF16) for s in shards),
        in_specs=[vmem] * n + [ANY_SPEC] * len(deps), out_specs=tuple([vmem] * n + [ANY_SPEC] * n),
        scratch_shapes=[pltpu.SemaphoreType.DMA((n,))], compiler_params=_params(),
    )(*shards, *deps)
    return res[:n], res[n:]


def _copies_start(arrays, plan, n, *, name):
    na = len(arrays)

    def body(*refs):
        ins, send_sems, recv_sems = refs[:na], refs[na], refs[na + 1]
        me = _position()
        for j in range(n):
            src, dst, peer, _ = plan(ins, me, j)
            pltpu.make_async_remote_copy(src_ref=src, dst_ref=dst, send_sem=send_sems.at[j], recv_sem=recv_sems.at[j],
                                         device_id=peer, device_id_type=MESH).start()

    return pl.pallas_call(
        body, name=name,
        out_shape=(pltpu.SemaphoreType.DMA((n,)), pltpu.SemaphoreType.DMA((n,)))
        + tuple(pltpu.HBM(a.shape, a.dtype) for a in arrays),
        in_specs=(HBM_SPEC,) * na, out_specs=(SEM_SPEC, SEM_SPEC) + (HBM_SPEC,) * na,
        input_output_aliases={i: 2 + i for i in range(na)},
        compiler_params=pltpu.CompilerParams(has_side_effects=DATAFLOW),
    )(*[_in_hbm(a) for a in arrays])


def _copies_wait(send_sems, recv_sems, arrays, plan, n, after, *, name):
    na = len(arrays)

    def body(*refs):
        ins, send_sems, recv_sems = refs[:na], refs[na], refs[na + 1]
        me = _position()
        for j in range(n):
            src, _, peer, landed = plan(ins, me, j)
            copy = pltpu.make_async_remote_copy(src_ref=src, dst_ref=landed, send_sem=send_sems.at[j],
                                                recv_sem=recv_sems.at[j], device_id=peer, device_id_type=MESH)
            copy.wait_send()
            copy.wait_recv()

    return pl.pallas_call(
        body, name=name, out_shape=tuple(pltpu.HBM(a.shape, a.dtype) for a in arrays),
        in_specs=(HBM_SPEC,) * na + (SEM_SPEC, SEM_SPEC, ANY_SPEC), out_specs=(HBM_SPEC,) * na,
        input_output_aliases={i: i for i in range(na)},
        compiler_params=pltpu.CompilerParams(has_side_effects=DATAFLOW),
    )(*arrays, send_sems, recv_sems, after)


SAME_CORE = (2, 4, 6)


def _gather_two_level(shard, land):
    r = shard.shape[0]
    rows = lambda ref, pos: ref.at[pl.ds(_linear(pos) * r, r), :]
    first_peers = (1,) + SAME_CORE

    def first(refs, me, j):
        peer = _peer(me, first_peers[j])
        return refs[0], rows(refs[1], me), peer, rows(refs[1], peer)

    def second(refs, me, j):
        sibling = _peer(me, 1)
        mine = rows(refs[0], _peer(me, SAME_CORE[j]))
        return mine, mine, sibling, rows(refs[0], _peer(sibling, SAME_CORE[j]))

    send1, recv1, shard1, land1 = _copies_start([shard, land], first, 4, name="w_in_send")

    def finish(after):
        _, land2 = _copies_wait(send1, recv1, [shard1, land1], first, 4, after, name="w_in_recv")
        send2, recv2, land3 = _copies_start([land2], second, 3, name="w_in_pass")
        return _copies_wait(send2, recv2, [land3], second, 3, after, name="w_in_pass_recv")[0]

    return shard1, finish


def _gather_start(shards, lands):
    n = len(shards)
    rows = [s.shape[0] for s in shards]

    def body(*refs):
        srcs, land = refs[:n], refs[n:2 * n]
        send_sems, recv_sems = refs[2 * n:3 * n], refs[3 * n:4 * n]
        me = _position()
        for a in range(n):
            mine = land[a].at[pl.ds(_linear(me) * rows[a], rows[a]), :]
            for k in SEND_ORDER:
                pltpu.make_async_remote_copy(
                    src_ref=srcs[a], dst_ref=mine, send_sem=send_sems[a].at[k - 1], recv_sem=recv_sems[a].at[k - 1],
                    device_id=_peer(me, k), device_id_type=MESH).start()

    sems = tuple(pltpu.SemaphoreType.DMA((N_DEV - 1,)) for _ in range(2 * n))
    res = pl.pallas_call(
        body, name="weights_send",
        out_shape=sems + tuple(pltpu.HBM(s.shape, s.dtype) for s in shards)
        + tuple(pltpu.HBM(l.shape, l.dtype) for l in lands),
        in_specs=(HBM_SPEC,) * (2 * n), out_specs=(SEM_SPEC,) * (2 * n) + (HBM_SPEC,) * (2 * n),
        input_output_aliases={i: 2 * n + i for i in range(2 * n)},
        compiler_params=pltpu.CompilerParams(has_side_effects=DATAFLOW),
    )(*[_in_hbm(s) for s in shards], *[_in_hbm(l) for l in lands])
    return [(res[a], res[n + a], res[2 * n + a], res[3 * n + a]) for a in range(n)]


def _gather_wait(send_sems, recv_sems, shard_thru, land_thru, after, *, name):
    r = shard_thru.shape[0]

    def body(src_ref, land_ref, send_sems, recv_sems, after_ref, src_dead, got_ref):
        del after_ref, src_dead, got_ref
        me = _position()
        for k in SEND_ORDER:
            peer = _peer(me, k)
            copy = pltpu.make_async_remote_copy(
                src_ref=src_ref, dst_ref=land_ref.at[pl.ds(_linear(peer) * r, r), :],
                send_sem=send_sems.at[k - 1], recv_sem=recv_sems.at[k - 1],
                device_id=peer, device_id_type=MESH)
            copy.wait_send()
            copy.wait_recv()

    return pl.pallas_call(
        body, name=name,
        out_shape=(pltpu.HBM(shard_thru.shape, shard_thru.dtype), pltpu.HBM(land_thru.shape, land_thru.dtype)),
        in_specs=(HBM_SPEC, HBM_SPEC, SEM_SPEC, SEM_SPEC, ANY_SPEC),
        out_specs=(HBM_SPEC, HBM_SPEC), input_output_aliases={0: 0, 1: 1},
        compiler_params=pltpu.CompilerParams(has_side_effects=DATAFLOW),
    )(shard_thru, land_thru, send_sems, recv_sems, after)[1]


def _exchange_start(gs, *, name):
    n = len(gs)
    rows = [g.shape[0] // N_DEV for g in gs]
    lands = [lax.empty((N_DEV - 1, r, g.shape[1]), g.dtype) for g, r in zip(gs, rows)]

    def body(*refs):
        g_refs, land_refs = refs[:n], refs[n:2 * n]
        send_sems, recv_sems = refs[2 * n:3 * n], refs[3 * n:4 * n]
        me = _position()
        for a in range(n):
            for k in SEND_ORDER:
                peer = _peer(me, k)
                pltpu.make_async_remote_copy(
                    src_ref=g_refs[a].at[pl.ds(_linear(peer) * rows[a], rows[a]), :],
                    dst_ref=land_refs[a].at[k - 1],
                    send_sem=send_sems[a].at[k - 1], recv_sem=recv_sems[a].at[k - 1],
                    device_id=peer, device_id_type=MESH).start()

    res = pl.pallas_call(
        body, name=name,
        out_shape=tuple(pltpu.SemaphoreType.DMA((N_DEV - 1,)) for _ in range(2 * n))
        + tuple(pltpu.HBM(a.shape, a.dtype) for a in gs + lands),
        in_specs=(HBM_SPEC,) * (2 * n), out_specs=(SEM_SPEC,) * (2 * n) + (HBM_SPEC,) * (2 * n),
        input_output_aliases={i: 2 * n + i for i in range(2 * n)},
        compiler_params=pltpu.CompilerParams(has_side_effects=DATAFLOW),
    )(*[_in_hbm(a) for a in gs + lands])
    return [(res[a], res[n + a], res[2 * n + a], res[3 * n + a]) for a in range(n)]


def _exchange_wait(send_sems, recv_sems, g_thru, land_thru, after, *, name):
    r = land_thru.shape[1]

    def body(g_ref, land_ref, send_sems, recv_sems, after_ref, g_dead, got_ref):
        del after_ref, g_dead, got_ref
        me = _position()
        for k in SEND_ORDER:
            peer = _peer(me, k)
            copy = pltpu.make_async_remote_copy(
                src_ref=g_ref.at[pl.ds(_linear(peer) * r, r), :], dst_ref=land_ref.at[k - 1],
                send_sem=send_sems.at[k - 1], recv_sem=recv_sems.at[k - 1],
                device_id=peer, device_id_type=MESH)
            copy.wait_send()
            copy.wait_recv()

    return pl.pallas_call(
        body, name=name,
        out_shape=(pltpu.HBM(g_thru.shape, g_thru.dtype), pltpu.HBM(land_thru.shape, land_thru.dtype)),
        in_specs=(HBM_SPEC, HBM_SPEC, SEM_SPEC, SEM_SPEC, pl.BlockSpec(memory_space=pl.ANY)),
        out_specs=(HBM_SPEC, HBM_SPEC), input_output_aliases={0: 0, 1: 1},
        compiler_params=pltpu.CompilerParams(has_side_effects=DATAFLOW),
    )(g_thru, land_thru, send_sems, recv_sems, after)


def _adamw_math(w, g, m, v):
    m = B1 * m + (1.0 - B1) * g
    v = B2 * v + (1.0 - B2) * (g * g)
    delta = -LR * ((m / C1) / (jnp.sqrt(v / C2) + AEPS) + WD * w)
    return delta, m, v


def _sum_adamw(g_all, land, w, m, v, *, name):
    r = w.shape[0]

    def body(all_ref, land_ref, w_ref, m_ref, v_ref, g_ref, d_ref, nm_ref, nv_ref, own_ref, sem):
        mine = pltpu.make_async_copy(all_ref.at[pl.ds(_linear(_position()) * r, r), :], own_ref, sem)
        mine.start()
        g = land_ref[0].astype(F32)
        for s in range(1, N_DEV - 1):
            g = g + land_ref[s].astype(F32)
        mine.wait()
        g = own_ref[...].astype(F32) + g
        g_ref[...] = g
        d_ref[...], nm_ref[...], nv_ref[...] = _adamw_math(w_ref[...], g, m_ref[...], v_ref[...])

    vmem = pl.BlockSpec(memory_space=pltpu.VMEM)
    return pl.pallas_call(
        body, name=name, out_shape=(jax.ShapeDtypeStruct(w.shape, F32),) * 4,
        in_specs=[ANY_SPEC, vmem, vmem, vmem, vmem], out_specs=(vmem,) * 4,
        scratch_shapes=[pltpu.VMEM((r, w.shape[1]), BF16), pltpu.SemaphoreType.DMA(())],
        compiler_params=_params(),
    )(g_all, land, w, m, v)


SMALL = ("g_mix_pre", "g_mix_post", "g_mem", "g_x_pre", "g_x_post", "g_ffn_pre", "g_ffn_post",
         "hgrn_onorm", "hgrn_lb", "sinks")
SMALL_W = dict(hgrn_onorm=HD, hgrn_lb=HG_W, sinks=8)
SQ_ROW = len(SMALL)
PACK_ROWS = 16


def _small_allreduce(parts):
    ns = len(SMALL)

    def body(*refs):
        part, tot_ref = refs[:ns + 1], refs[ns + 1]
        gath, send_sems, recv_sems = refs[ns + 2:]
        me = _position()
        mine = gath.at[_linear(me)]
        mine[...] = jnp.zeros((PACK_ROWS, D), F32)
        for r, name in enumerate(SMALL):
            wd = SMALL_W.get(name, D)
            mine[r:r + 1, 0:wd] = jnp.sum(part[r][...], axis=0, keepdims=True)[:, 0:wd]
        sq = jnp.sum(part[ns][...]) * (0.5 / D)
        mine[SQ_ROW:SQ_ROW + 1, :] = jnp.full((1, D), sq, F32)

        def copy(k):
            peer = _peer(me, k)
            return pltpu.make_async_remote_copy(
                src_ref=mine, dst_ref=mine, send_sem=send_sems.at[k - 1], recv_sem=recv_sems.at[k - 1],
                device_id=peer, device_id_type=MESH)

        def arrival(k):
            slot = gath.at[_linear(_peer(me, k))]
            return pltpu.make_async_remote_copy(
                src_ref=slot, dst_ref=slot, send_sem=send_sems.at[k - 1], recv_sem=recv_sems.at[k - 1],
                device_id=_peer(me, k), device_id_type=MESH)

        sent = [copy(k) for k in range(1, 8)]
        for cp in sent:
            cp.start()
        for k in range(1, 8):
            arrival(k).wait_recv()
        for cp in sent:
            cp.wait_send()
        tot = gath[0]
        for s in range(1, N_DEV):
            tot = tot + gath[s]
        tot_ref[...] = tot

    return pl.pallas_call(
        body, name="small_allreduce", out_shape=jax.ShapeDtypeStruct((PACK_ROWS, D), F32),
        scratch_shapes=[pltpu.VMEM((N_DEV, PACK_ROWS, D), F32), pltpu.SemaphoreType.DMA((7,)),
                        pltpu.SemaphoreType.DMA((7,))],
        compiler_params=_params(has_side_effects=True),
    )(*[parts[n] for n in SMALL], parts["sq"])


def _small_update(tot, sm, m_sm, v_sm):
    ns = len(SMALL)

    def body(*refs):
        tot = refs[0][...]
        w_refs, m_refs, v_refs = refs[1:ns + 1], refs[ns + 1:2 * ns + 1], refs[2 * ns + 1:3 * ns + 1]
        outs = refs[3 * ns + 1:]
        loss_ref = outs[0]
        g_out, d_out = outs[1:ns + 1], outs[ns + 1:2 * ns + 1]
        nm_out, nv_out = outs[2 * ns + 1:3 * ns + 1], outs[3 * ns + 1:4 * ns + 1]
        loss_ref[...] = tot[SQ_ROW:SQ_ROW + 1, 0:1]
        for r, name in enumerate(SMALL):
            wd = SMALL_W.get(name, D)
            g = tot[r:r + 1, 0:wd]
            w = w_refs[r][...]
            if name == "hgrn_lb":
                mx = jnp.maximum(w[0:1], w[1:2])
                e0, e1 = jnp.exp(w[0:1] - mx), jnp.exp(w[1:2] - mx)
                lb0 = e0 / (e0 + e1)
                g0 = g * lb0 * (1.0 - lb0)
                for i, gi in enumerate((g0, -g0)):
                    d, nm, nv = _adamw_math(w[i:i + 1], gi, m_refs[r][i:i + 1, :], v_refs[r][i:i + 1, :])
                    g_out[r][i:i + 1, :] = gi
                    d_out[r][i:i + 1, :], nm_out[r][i:i + 1, :], nv_out[r][i:i + 1, :] = d, nm, nv
            else:
                d, nm, nv = _adamw_math(w, g, m_refs[r][...], v_refs[r][...])
                g_out[r][...] = g
                d_out[r][...], nm_out[r][...], nv_out[r][...] = d, nm, nv

    shapes = [jax.ShapeDtypeStruct(sm[n].shape, F32) for n in SMALL]
    res = pl.pallas_call(
        body, name="small_update", out_shape=tuple([jax.ShapeDtypeStruct((1, 1), F32)] + shapes * 4),
        compiler_params=_params(),
    )(tot, *[sm[n] for n in SMALL], *[m_sm[n] for n in SMALL], *[v_sm[n] for n in SMALL])
    groups = [dict(zip(SMALL, res[1 + i * ns:1 + (i + 1) * ns])) for i in range(4)]
    return res[0], groups[0], groups[1], groups[2], groups[3]


BIG = ("w_in", "w_gate", "w_up", "w_down", "w_out", "wq_x", "wk_x", "wv_x", "wo_x")
BIG_KEY = dict(w_in="winT", w_gate="wgT", w_up="wuT", w_down="wd", w_out="wout", wq_x="wq", wk_x="wk",
               wv_x="wv", wo_x="wo")
TRANSPOSED = ("w_in", "w_gate", "w_up")
WEIGHTS = ("w_in", "sinks", "hgrn_lb", "hgrn_onorm", "w_out", "g_mix_pre", "g_mix_post", "g_mem", "g_x_pre",
           "g_x_post", "wq_x", "wk_x", "wv_x", "wo_x", "g_ffn_pre", "g_ffn_post", "w_gate", "w_up", "w_down")


def kernel(x, mem, w_in, sinks, hgrn_lb, hgrn_onorm, w_out, g_mix_pre, g_mix_post, g_mem, g_x_pre, g_x_post, wq_x, wk_x, wv_x, wo_x, g_ffn_pre, g_ffn_post, w_gate, w_up, w_down, loss_target, m_w_in, m_sinks, m_hgrn_lb, m_hgrn_onorm, m_w_out, m_g_mix_pre, m_g_mix_post, m_g_mem, m_g_x_pre, m_g_x_post, m_wq_x, m_wk_x, m_wv_x, m_wo_x, m_g_ffn_pre, m_g_ffn_post, m_w_gate, m_w_up, m_w_down, v_w_in, v_sinks, v_hgrn_lb, v_hgrn_onorm, v_w_out, v_g_mix_pre, v_g_mix_post, v_g_mem, v_g_x_pre, v_g_x_post, v_wq_x, v_wk_x, v_wv_x, v_wo_x, v_g_ffn_pre, v_g_ffn_post, v_w_gate, v_w_up, v_w_down):
    given = dict(locals())
    wts = {n: given[n] for n in WEIGHTS}
    ms = {n: given["m_" + n] for n in WEIGHTS}
    vs = {n: given["v_" + n] for n in WEIGHTS}

    def mat(a, name):
        a = a[0]
        return a.T if name in TRANSPOSED else a

    shard_in, land_in = _prepare_weights([mat(w_in, "w_in")], name="prepare_w_in")
    sent_in, finish_w_in = _gather_two_level(shard_in[0], land_in[0])
    order = ("w_out", "wq_x", "wk_x", "wv_x", "wo_x", "w_gate", "w_up", "w_down")
    flying = dict(zip(order, _gather_start(*_prepare_weights([mat(wts[n], n) for n in order], name="prepare_weights",
                                                             dep=sent_in))))
    name_of = {k: n for n, k in BIG_KEY.items()}

    def fetch(key, after):
        if key == "winT":
            return finish_w_in(after)
        return _gather_wait(*flying[name_of[key]], after, name="weights_recv_" + name_of[key])

    sm = {n: wts[n] for n in SMALL}
    started, held = {}, {}
    send_with = {"wgT": ("wgT", "wuT"), "wuT": ("wgT", "wuT"), "wq": ("wq", "wk", "wv"), "wk": ("wq", "wk", "wv"),
                 "wv": ("wq", "wk", "wv")}

    def emit(key, g):
        held[key] = g
        group = send_with.get(key, (key,))
        if key != group[-1]:
            return None
        flights = _exchange_start([held[k] for k in group], name="grad_send_" + name_of[group[0]])
        started.update({name_of[k]: f for k, f in zip(group, flights)})
        return flights[-1][2]

    grad_x, _, parts = _local_step(x[0], mem[0], loss_target[0], fetch, sm, emit, first_dep=flying["w_out"][2])
    grads, deltas, new_m, new_v = {}, {}, {}, {}
    after = grad_x
    for n in ("w_down", "w_gate", "w_up", "wo_x", "wq_x", "wk_x", "wv_x", "w_out", "w_in"):
        g_all, land = _exchange_wait(*started[n], after, name="grad_recv_" + n)
        res = _sum_adamw(g_all, land, mat(wts[n], n), mat(ms[n], n), mat(vs[n], n), name="adamw_" + n)
        after = res[1]
        if n in TRANSPOSED:
            res = [a.T for a in res]
        grads[n], deltas[n], new_m[n], new_v[n] = [a[None] for a in res]
    loss, g_s, d_s, m_s, v_s = _small_update(_small_allreduce(parts), sm, {n: ms[n] for n in SMALL},
                                             {n: vs[n] for n in SMALL})
    grads.update(g_s), deltas.update(d_s), new_m.update(m_s), new_v.update(v_s)
    return (loss[0, 0], grad_x[None], *[grads[n] for n in WEIGHTS], *[deltas[n] for n in WEIGHTS],
            *[new_m[n] for n in WEIGHTS], *[new_v[n] for n in WEIGHTS])
```

```python
import functools

import jax
import jax.numpy as jnp
from jax import lax
from jax.experimental import pallas as pl
from jax.experimental.pallas import tpu as pltpu

F32 = jnp.float32
BF16 = jnp.bfloat16

D = 1024
D_IN = 2816
D_FF = 2816
CHUNK = 64
SWA_W = 512
KV_W = 128
HG_W = 512
HD = 128
ZQH, ZFH, ZIH, ZGH = 768, 1280, 1792, 2304
XH, XD = 4, 256
EPS = 1e-6
NEG = -1e30
N_DEV = 8
MESH = pl.DeviceIdType.MESH

LR, B1, B2, AEPS, WD, STEP = 0.001, 0.9, 0.999, 1e-08, 0.01, 10
C1 = 1.0 - B1 ** STEP
C2 = 1.0 - B2 ** STEP

VMEM_LIMIT = 56 * 1024 * 1024


def _params(**kw):
    return pltpu.CompilerParams(vmem_limit_bytes=VMEM_LIMIT, **kw)


def _sig(x):
    return 1.0 / (1.0 + jnp.exp(-x))


def _rowsum8(x):
    r, w = x.shape
    return jnp.sum(x.reshape(r // 8, 8, w), axis=0)


def _dot(a, b, ca, cb, precision=None):
    return lax.dot_general(a, b, (((ca,), (cb,)), ((), ())), preferred_element_type=F32,
                           precision=precision)


ANY_SPEC = pl.BlockSpec(memory_space=pl.ANY)


def _mm(a, b, *, ta=False, tb=False, out_dtype, tm, tn, tk=None, name, dep=None, n_outer=False):
    m = a.shape[1] if ta else a.shape[0]
    k = a.shape[0] if ta else a.shape[1]
    n = b.shape[0] if tb else b.shape[1]
    tm, tn = min(tm, m), min(tn, n)
    tk = k if tk is None else min(tk, k)
    nk = k // tk
    assert m % tm == 0 and n % tn == 0 and k % tk == 0, (name, m, n, k, tm, tn, tk)
    ij = (lambda g0, g1: (g1, g0)) if n_outer else (lambda g0, g1: (g0, g1))
    a_spec = (pl.BlockSpec((tk, tm), lambda g0, g1, kk: (kk, ij(g0, g1)[0])) if ta
              else pl.BlockSpec((tm, tk), lambda g0, g1, kk: (ij(g0, g1)[0], kk)))
    b_spec = (pl.BlockSpec((tn, tk), lambda g0, g1, kk: (ij(g0, g1)[1], kk)) if tb
              else pl.BlockSpec((tk, tn), lambda g0, g1, kk: (kk, ij(g0, g1)[1])))
    ca, cb = (0 if ta else 1), (1 if tb else 0)

    deps = [] if dep is None else [dep]

    def body(a_ref, b_ref, *rest):
        o_ref, acc = rest[len(deps)], rest[len(deps) + 1:]
        p = _dot(a_ref[...].astype(BF16), b_ref[...].astype(BF16), ca, cb)
        if nk == 1:
            o_ref[...] = p.astype(out_dtype)
        else:
            acc_ref, = acc
            kk = pl.program_id(2)

            @pl.when(kk == 0)
            def _():
                acc_ref[...] = p

            @pl.when(kk > 0)
            def _():
                acc_ref[...] += p

            @pl.when(kk == nk - 1)
            def _():
                o_ref[...] = acc_ref[...].astype(out_dtype)

    return pl.pallas_call(
        body, name=name, out_shape=jax.ShapeDtypeStruct((m, n), out_dtype),
        grid=(n // tn, m // tm, nk) if n_outer else (m // tm, n // tn, nk),
        in_specs=[a_spec, b_spec] + [ANY_SPEC] * len(deps),
        out_specs=pl.BlockSpec((tm, tn), lambda g0, g1, kk: ij(g0, g1)),
        scratch_shapes=[pltpu.VMEM((tm, tn), F32)] if nk > 1 else [],
        compiler_params=_params(dimension_semantics=("parallel", "parallel", "arbitrary")),
    )(a, b, *deps)


def _mm2(a1, b1, a2, b2, *, tb=False, out_dtype, tm, name, dep=None):
    m, k = a1.shape
    n = b1.shape[0] if tb else b1.shape[1]
    tm = min(tm, m)
    assert m % tm == 0
    cb = 1 if tb else 0
    deps = [] if dep is None else [dep]

    def body(a1_ref, b1_ref, a2_ref, b2_ref, *rest):
        o_ref = rest[len(deps)]
        o_ref[...] = (_dot(a1_ref[...].astype(BF16), b1_ref[...], 1, cb)
                      + _dot(a2_ref[...].astype(BF16), b2_ref[...], 1, cb)).astype(out_dtype)

    a_spec = pl.BlockSpec((tm, k), lambda i: (i, 0))
    b_spec = pl.BlockSpec(b1.shape, lambda i: (0, 0))
    return pl.pallas_call(
        body, name=name, out_shape=jax.ShapeDtypeStruct((m, n), out_dtype),
        grid=(m // tm,), in_specs=[a_spec, b_spec, a_spec, b_spec] + [ANY_SPEC] * len(deps),
        out_specs=pl.BlockSpec((tm, n), lambda i: (i, 0)),
        compiler_params=_params(dimension_semantics=("parallel",)),
    )(a1, b1, a2, b2, *deps)


def _mm_rows(prods, rows_in, vecs_in, epilogue, outs, *, tm, name, dep=None):
    m = prods[0][0].shape[0]
    n = prods[0][1].shape[0] if prods[0][2] else prods[0][1].shape[1]
    tm = min(tm, m)
    assert m % tm == 0
    deps = [] if dep is None else [dep]
    n_p, n_r, n_v = len(prods), len(rows_in), len(vecs_in)

    def body(*refs):
        ab = refs[:2 * n_p]
        row_refs = refs[2 * n_p:2 * n_p + n_r]
        vec_refs = refs[2 * n_p + n_r:2 * n_p + n_r + n_v]
        out_refs = refs[2 * n_p + n_r + n_v + len(deps):]
        p = None
        for j, (_, _, tb) in enumerate(prods):
            t = _dot(ab[2 * j][...].astype(BF16), ab[2 * j + 1][...], 1, 1 if tb else 0)
            p = t if p is None else p + t
        vals = epilogue(p, *[r[...] for r in row_refs], *[v[...] for v in vec_refs])
        for (dtype, kind), o_ref, val in zip(outs, out_refs, vals):
            if kind == "row":
                o_ref[...] = val.astype(dtype)
            else:
                @pl.when(pl.program_id(0) == 0)
                def _(o_ref=o_ref):
                    o_ref[...] = jnp.zeros_like(o_ref)

                o_ref[...] += val

    row = lambda w: pl.BlockSpec((tm, w), lambda i: (i, 0))
    whole = lambda a: pl.BlockSpec(a.shape, lambda i: (0,) * a.ndim, pipeline_mode=pl.Buffered(1))
    in_specs, args = [], []
    for a, b, _ in prods:
        in_specs += [row(a.shape[1]), whole(b)]
        args += [a, b]
    in_specs += [row(r.shape[1]) for r in rows_in] + [whole(v) for v in vecs_in] + [ANY_SPEC] * len(deps)
    return pl.pallas_call(
        body, name=name,
        out_shape=tuple(jax.ShapeDtypeStruct((m, n) if kind == "row" else (8, n), dtype) for dtype, kind in outs),
        grid=(m // tm,), in_specs=in_specs,
        out_specs=tuple(row(n) if kind == "row" else pl.BlockSpec((8, n), lambda i: (0, 0)) for _, kind in outs),
        compiler_params=_params(dimension_semantics=("arbitrary",)),
    )(*args, *rows_in, *vecs_in, *deps)


def _rstd(x):
    return lax.rsqrt(jnp.mean(x * x, axis=-1, keepdims=True) + EPS)


def _norm_bwd(xh, r, t):
    return r * (t - xh * jnp.mean(xh * t, axis=-1, keepdims=True))


ROW_F32, ROW_BF16, SUM_F32 = (F32, "row"), (BF16, "row"), (F32, "sum")


def _ep_post_pre(p, h, g_post, g_pre):
    y = p.astype(BF16)
    yf = y.astype(F32)
    hn = h + yf * _rstd(yf) * g_post
    return y, hn, hn * _rstd(hn) * g_pre


_EP_POST_PRE_OUTS = [ROW_BF16, ROW_F32, ROW_BF16]


def _ep_final_loss(y, h, target, g_post):
    r = _rstd(y)
    yh = y * r
    err = h + yh * g_post - target
    dh = err * (1.0 / D)
    return _rowsum8(err * err), dh, _norm_bwd(yh, r, dh * g_post), _rowsum8(dh * yh)


_EP_FINAL_LOSS_OUTS = [SUM_F32, ROW_F32, ROW_BF16, SUM_F32]


def _ep_post_pre_bwd(du, dh_out, hn, y, g_post, g_pre):
    r2 = _rstd(hn)
    xh = hn * r2
    dh = dh_out + _norm_bwd(xh, r2, du * g_pre)
    yf = y.astype(F32)
    r1 = _rstd(yf)
    yh = yf * r1
    return dh, _norm_bwd(yh, r1, dh * g_post), _rowsum8(du * xh), _rowsum8(dh * yh)


_EP_POST_PRE_BWD_OUTS = [ROW_F32, ROW_BF16, SUM_F32, SUM_F32]


def _ep_pre_bwd(du, dh_out, x, g):
    r = _rstd(x)
    xh = x * r
    return dh_out + _norm_bwd(xh, r, du * g), _rowsum8(du * xh)


_EP_PRE_BWD_OUTS = [ROW_F32, SUM_F32]


def _prenorm(x, g, *, name, dep=None):
    t, d = x.shape
    tb = min(512, t)
    deps = [] if dep is None else [dep]

    def body(x_ref, g_ref, *rest):
        xf = x_ref[...]
        rest[-1][...] = (xf * _rstd(xf) * g_ref[...]).astype(BF16)

    return pl.pallas_call(
        body, name=name, out_shape=jax.ShapeDtypeStruct((t, d), BF16), grid=(t // tb,),
        in_specs=[pl.BlockSpec((tb, d), lambda i: (i, 0)), pl.BlockSpec((1, d), lambda i: (0, 0))]
        + [ANY_SPEC] * len(deps),
        out_specs=pl.BlockSpec((tb, d), lambda i: (i, 0)), compiler_params=_params(),
    )(x, g, *deps)


QB = 256


def _half_mask(shape, e):
    lane = lax.broadcasted_iota(jnp.int32, shape, len(shape) - 1)
    return (lane // 64) == e


def _place(kv):
    sw = pltpu.roll(kv, 64, 1)
    m0 = _half_mask(kv.shape, 0)
    return [[jnp.where(m0, kv, 0.0).astype(BF16), jnp.where(m0, 0.0, sw).astype(BF16)],
            [jnp.where(m0, sw, 0.0).astype(BF16), jnp.where(m0, 0.0, kv).astype(BF16)]]


def _swa_valid_q(i, nq, nk):
    qc = lax.broadcasted_iota(jnp.int32, (nq, nk), 0) // CHUNK
    kc = lax.broadcasted_iota(jnp.int32, (nq, nk), 1) // CHUNK - 2
    return (kc <= qc) & (qc <= kc + 2) & (4 * i + kc >= 0)


def _swa_fwd(z, sinks, t):
    nb = t // QB

    def body(s_ref, q_ref, kp_ref, kc_ref, vp_ref, vc_ref, o_ref, lse_ref):
        i = pl.program_id(0)
        kpl = _place(jnp.concatenate([kp_ref[...], kc_ref[...]], axis=0))
        vpl = _place(jnp.concatenate([vp_ref[...], vc_ref[...]], axis=0))
        valid = _swa_valid_q(i, QB, QB + 128)
        lane = lax.broadcasted_iota(jnp.int32, (QB, 128), 1)
        lse_out = jnp.zeros((QB, 128), F32)
        for j in range(4):
            qp = q_ref[:, 128 * j:128 * (j + 1)].astype(BF16)
            acc = jnp.zeros((QB, 128), F32)
            for e in range(2):
                h = 2 * j + e
                kvh = h // 4
                qm = jnp.where(_half_mask(qp.shape, e), qp, jnp.zeros_like(qp))
                s = _dot(qm, kpl[kvh][e], 1, 1) * 0.125
                s = jnp.where(valid, s, NEG)
                sink = s_ref[0, h]
                m = jnp.maximum(jnp.max(s, axis=-1, keepdims=True), sink)
                p = jnp.exp(s - m)
                l = jnp.sum(p, axis=-1, keepdims=True) + jnp.exp(sink - m)
                acc = acc + _dot(p.astype(BF16), vpl[kvh][e], 1, 0) * (1.0 / l)
                lse_out = jnp.where(lane == h, m + jnp.log(l), lse_out)
            o_ref[:, 128 * j:128 * (j + 1)] = acc.astype(BF16)
        lse_ref[...] = lse_out

    prev = lambda c: pl.BlockSpec((128, 128), lambda i: (jnp.maximum(2 * i - 1, 0), c))
    cur = lambda c: pl.BlockSpec((QB, 128), lambda i: (i, c))
    return pl.pallas_call(
        body, name="swa_fwd",
        out_shape=(jax.ShapeDtypeStruct((t, D), BF16), jax.ShapeDtypeStruct((t, 128), F32)),
        grid=(nb,),
        in_specs=[pl.BlockSpec(memory_space=pltpu.SMEM),
                  pl.BlockSpec((QB, SWA_W), lambda i: (i, 0)), prev(4), cur(4), prev(5), cur(5)],
        out_specs=(pl.BlockSpec((QB, SWA_W), lambda i: (i, 0)), pl.BlockSpec((QB, 128), lambda i: (i, 0))),
        compiler_params=_params(),
    )(sinks, z, z, z, z, z)


def _swa_bwd(z, sinks, ymix, lse, dymix, t):
    nb = t // QB
    nk = QB + 128

    def body(s_ref, q_ref, kp_ref, kc_ref, vp_ref, vc_ref, o_ref, do_ref, l_ref,
             dq_ref, first_ref, second_ref, ds_ref, carry_ref):
        i = pl.program_id(0)
        live = i < nb

        @pl.when(i == 0)
        def _():
            ds_ref[...] = jnp.zeros_like(ds_ref)
            carry_ref[...] = jnp.zeros_like(carry_ref)

        lane = lax.broadcasted_iota(jnp.int32, (8, 128), 1)
        kpl = _place(jnp.concatenate([kp_ref[...], kc_ref[...]], axis=0))
        vpl = _place(jnp.concatenate([vp_ref[...], vc_ref[...]], axis=0))
        valid = _swa_valid_q(i, QB, nk) & live
        lse_c = l_ref[...]
        dsink = jnp.zeros((8, 128), F32)
        dk_acc = [[jnp.zeros((nk, 128), F32) for _ in range(2)] for _ in range(2)]
        dv_acc = [[jnp.zeros((nk, 128), F32) for _ in range(2)] for _ in range(2)]
        dq = []
        for j in range(4):
            cols = slice(128 * j, 128 * (j + 1))
            qp = q_ref[:, cols].astype(BF16)
            dop = do_ref[:, cols]
            prod = dop.astype(F32) * o_ref[:, cols].astype(F32)
            acc = jnp.zeros((QB, 128), F32)
            for e in range(2):
                h = 2 * j + e
                kvh = h // 4
                hm = _half_mask(qp.shape, e)
                qm = jnp.where(hm, qp, jnp.zeros_like(qp))
                dom = jnp.where(hm, dop, jnp.zeros_like(dop))
                dd = jnp.sum(jnp.where(hm, prod, 0.0), axis=-1, keepdims=True)
                lse_h = lse_c[:, h:h + 1]
                s = _dot(qm, kpl[kvh][e], 1, 1) * 0.125
                p = jnp.where(valid, jnp.exp(s - lse_h), 0.0)
                dp = _dot(dom, vpl[kvh][e], 1, 1)
                ds = (p * (dp - dd) * 0.125).astype(BF16)
                acc = acc + _dot(ds, kpl[kvh][e], 1, 0)
                dk_acc[kvh][e] = dk_acc[kvh][e] + _dot(ds, qm, 0, 0)
                dv_acc[kvh][e] = dv_acc[kvh][e] + _dot(p.astype(BF16), dom, 0, 0)
                ps = jnp.where(live, jnp.exp(s_ref[0, h] - lse_h) * dd, 0.0)
                dsink = dsink - jnp.where(lane == h, _rowsum8(jnp.broadcast_to(ps, (QB, 128))), 0.0)
            dq.append(acc.astype(BF16))
        ds_ref[...] += dsink
        dk = dk_acc[0][0] + dk_acc[1][1] + pltpu.roll(dk_acc[0][1] + dk_acc[1][0], 64, 1)
        dv = dv_acc[0][0] + dv_acc[1][1] + pltpu.roll(dv_acc[0][1] + dv_acc[1][0], 64, 1)
        dkv = jnp.concatenate([dk, dv], axis=1)
        second_ref[...] = (carry_ref[...] + dkv[0:128]).astype(BF16)
        carry_ref[...] = dkv[256:384]

        @pl.when(live)
        def _():
            for j in range(4):
                dq_ref[:, 128 * j:128 * (j + 1)] = dq[j]
            first_ref[...] = dkv[128:256].astype(BF16)

    blk = lambda i: jnp.minimum(i, nb - 1)
    prev = lambda c: pl.BlockSpec((128, 128), lambda i: (jnp.maximum(2 * blk(i) - 1, 0), c))
    cur = lambda w, c: pl.BlockSpec((QB, w), lambda i: (blk(i), c))
    half = lambda index: pl.BlockSpec((128, 256), lambda i: (index(i), 0))
    return pl.pallas_call(
        body, name="swa_bwd",
        out_shape=(jax.ShapeDtypeStruct((t, SWA_W), BF16), jax.ShapeDtypeStruct((t // 2, 256), BF16),
                   jax.ShapeDtypeStruct((t // 2, 256), BF16), jax.ShapeDtypeStruct((8, 128), F32)),
        grid=(nb + 1,),
        in_specs=[pl.BlockSpec(memory_space=pltpu.SMEM),
                  cur(SWA_W, 0), prev(4), cur(128, 4), prev(5), cur(128, 5),
                  cur(SWA_W, 0), cur(SWA_W, 0), cur(128, 0)],
        out_specs=(cur(SWA_W, 0), half(blk), half(lambda i: jnp.maximum(i - 1, 0)),
                   pl.BlockSpec((8, 128), lambda i: (0, 0))),
        scratch_shapes=[pltpu.VMEM((128, 256), F32)],
        compiler_params=_params(dimension_semantics=("arbitrary",)),
    )(sinks, z, z, z, z, z, ymix, dymix, lse)


HB = 256


def _lower_bound(lb_ref):
    a = lb_ref[...]
    a0, a1 = a[0:1], a[1:2]
    mx = jnp.maximum(a0, a1)
    e0, e1 = jnp.exp(a0 - mx), jnp.exp(a1 - mx)
    return e0 / (e0 + e1)


def _hgrn_cols(row_block):
    return [pl.BlockSpec((HB, 2 * HD), lambda j, c=base // (2 * HD) + p: (row_block(j), c))
            for base in (ZQH, ZFH, ZIH, ZGH) for p in range(2)]


NCH = HB // CHUNK


def _split3(x):
    hi = x.astype(BF16)
    r1 = x - hi.astype(F32)
    mid = r1.astype(BF16)
    return hi, mid, (r1 - mid.astype(F32)).astype(BF16)


def _blockdiag(lower):
    r = lax.broadcasted_iota(jnp.int32, (HB, HB), 0)
    c = lax.broadcasted_iota(jnp.int32, (HB, HB), 1)
    return (r // CHUNK == c // CHUNK) & ((c <= r) if lower else (c >= r))


def _chunk_sums(mask_bf16, x):
    return sum(_dot(mask_bf16, part, 1, 0) for part in _split3(x))


def _per_chunk_rows(x, row):
    w = x.shape[1]
    picked = x.reshape(NCH, CHUNK, w)[:, row:row + 1, :]
    return jnp.broadcast_to(picked, (NCH, CHUNK, w)).reshape(HB, w)


def _chunk_stack(x, chunk_of_row):
    return jnp.concatenate([jnp.where(chunk_of_row == c, x, jnp.zeros_like(x)) for c in range(NCH)], axis=1)


def _chunk_pick(x, chunk_of_row):
    w = x.shape[1] // NCH
    out = jnp.zeros((HB, w), x.dtype)
    for c in range(NCH):
        out = jnp.where(chunk_of_row == c, x[:, c * w:(c + 1) * w], out)
    return out


def _hgrn_local(q, f, kf, b):
    sq = _sig(q)
    qf = q * sq * (HD ** -0.5)
    b_mid = _per_chunk_rows(b, CHUNK // 2 - 1)
    b_last = _per_chunk_rows(b, CHUNK - 1)
    qm = qf * jnp.exp(b - b_mid)
    km = kf * jnp.exp(b_mid - b)
    kl = kf * jnp.exp(b_last - b)
    qb = qf * jnp.exp(b)
    return dict(sq=sq, b_mid=b_mid, b_last=b_last, qm=qm, km=km, kl=kl, qb=qb)


def _hgrn2_fwd(z, hgrn_lb, onorm, ymix, t):
    nb = t // HB

    def body(*refs):
        zq, zf, zi, zg = refs[0:2], refs[2:4], refs[4:6], refs[6:8]
        lb_ref, on_ref, _, y_ref, o_ref, sp_ref, st_ref = refs[8:]

        @pl.when(pl.program_id(0) == 0)
        def _():
            st_ref[...] = jnp.zeros_like(st_ref)

        lb_all = _lower_bound(lb_ref)
        gn = on_ref[...]
        low = _blockdiag(True)
        low_b = low.astype(BF16)
        chunk_of_row = lax.broadcasted_iota(jnp.int32, (HB, HD), 0) // CHUNK
        for p in range(2):
            lbp = lb_all[:, 2 * HD * p:2 * HD * (p + 1)]
            fp = lbp + (1.0 - lbp) * _sig(zf[p][...])
            bp = _chunk_sums(low_b, jnp.log(fp))
            for e in range(2):
                h, ls = 2 * p + e, slice(e * HD, (e + 1) * HD)
                f = fp[:, ls]
                w = _hgrn_local(zq[p][:, ls], f, 1.0 - f, bp[:, ls])
                iv = zi[p][:, ls].astype(BF16)
                a = jnp.where(low, _dot(w["qm"].astype(BF16), w["km"].astype(BF16), 1, 1), 0.0)
                o = _dot(a.astype(BF16), iv, 1, 0)
                u = _dot(iv, _chunk_stack(w["kl"].astype(BF16), chunk_of_row), 0, 0)
                decay = jnp.exp(w["b_last"])
                st = st_ref[h]
                states = []
                for c in range(NCH):
                    sp_ref[h, c] = st
                    states.append(st.astype(BF16))
                    st = st * decay[c * CHUNK:c * CHUNK + 1] + u[:, c * HD:(c + 1) * HD]
                st_ref[h] = st
                inter = _dot(w["qb"].astype(BF16), jnp.concatenate(states, axis=0), 1, 1)
                o = o + _chunk_pick(inter, chunk_of_row)
                hs = slice(h * HD, (h + 1) * HD)
                o_ref[:, hs] = o
                gg = zg[p][:, ls]
                y_ref[:, hs] = (o * _rstd(o) * gn * (gg * _sig(gg))).astype(BF16)

    return pl.pallas_call(
        body, name="hgrn_fwd",
        out_shape=(jax.ShapeDtypeStruct((t, D), BF16), jax.ShapeDtypeStruct((t, HG_W), F32),
                   jax.ShapeDtypeStruct((4, t // CHUNK, HD, HD), F32)),
        grid=(nb,),
        in_specs=_hgrn_cols(lambda j: j) + [pl.BlockSpec((2, HG_W), lambda j: (0, 0)),
                                            pl.BlockSpec((1, HD), lambda j: (0, 0)), ANY_SPEC],
        out_specs=(pl.BlockSpec((HB, HG_W), lambda j: (j, 1)),
                   pl.BlockSpec((HB, HG_W), lambda j: (j, 0)),
                   pl.BlockSpec((4, NCH, HD, HD), lambda j: (0, j, 0, 0))),
        scratch_shapes=[pltpu.VMEM((4, HD, HD), F32)],
        input_output_aliases={10: 0},
        compiler_params=_params(dimension_semantics=("arbitrary",)),
    )(*[z] * 8, hgrn_lb, onorm, ymix)


def _hgrn2_bwd(z, hgrn_lb, onorm, o_save, sprev, dymix, dza, t):
    nb = t // HB

    def body(*refs):
        zq, zf, zi, zg = refs[0:2], refs[2:4], refs[4:6], refs[6:8]
        (lb_ref, on_ref, o_ref, sp_ref, dy_ref, dqa_ref, first_ref, second_ref,
         dz_ref, dlb_ref, don_ref, dst_ref) = refs[8:]

        @pl.when(pl.program_id(0) == 0)
        def _():
            dst_ref[...] = jnp.zeros_like(dst_ref)
            dlb_ref[...] = jnp.zeros_like(dlb_ref)
            don_ref[...] = jnp.zeros_like(don_ref)

        dz_ref[:, 0:SWA_W] = dqa_ref[...]
        dz_ref[0:HB // 2, SWA_W:ZQH] = first_ref[...]
        dz_ref[HB // 2:HB, SWA_W:ZQH] = second_ref[...]
        lb_all = _lower_bound(lb_ref)
        gn = on_ref[...]
        low, upp = _blockdiag(True), _blockdiag(False)
        upp_b = upp.astype(BF16)
        low_b = low.astype(BF16)
        row = lax.broadcasted_iota(jnp.int32, (HB, HD), 0)
        chunk_of_row = row // CHUNK
        in_chunk = row % CHUNK
        for p in range(2):
            lbp = lb_all[:, 2 * HD * p:2 * HD * (p + 1)]
            sgp = _sig(zf[p][...])
            fp = lbp + (1.0 - lbp) * sgp
            bp = _chunk_sums(low_b, jnp.log(fp))
            db_pair, dkf_pair = [], []
            for e in range(2):
                h, ls, hs = 2 * p + e, slice(e * HD, (e + 1) * HD), slice((2 * p + e) * HD, (2 * p + e + 1) * HD)
                f = fp[:, ls]
                q = zq[p][:, ls]
                w = _hgrn_local(q, f, 1.0 - f, bp[:, ls])
                iv = zi[p][:, ls].astype(BF16)
                gg = zg[p][:, ls]
                o = o_ref[:, hs]
                dout = dy_ref[:, hs].astype(F32)
                sgg = _sig(gg)
                r = _rstd(o)
                oh = o * r
                dyn = dout * (gg * sgg)
                dz_ref[:, ZGH + h * HD:ZGH + (h + 1) * HD] = (
                    dout * oh * gn * (sgg * (1.0 + gg * (1.0 - sgg)))).astype(BF16)
                don_ref[...] += _rowsum8(dyn * oh)
                do = _norm_bwd(oh, r, dyn * gn).astype(BF16)
                qm, km, kl, qb = (w[n].astype(BF16) for n in ("qm", "km", "kl", "qb"))
                decay = jnp.exp(w["b_last"])
                grads_in = _dot(do, _chunk_stack(qb, chunk_of_row), 0, 0)
                dst = dst_ref[h]
                dstn, dd_rows = [None] * NCH, [None] * NCH
                for c in reversed(range(NCH)):
                    dstn[c] = dst.astype(BF16)
                    dd_rows[c] = jnp.sum(dst * sp_ref[h, c], axis=0, keepdims=True)
                    dst = dst * decay[c * CHUNK:c * CHUNK + 1] + grads_in[:, c * HD:(c + 1) * HD]
                dst_ref[h] = dst
                states = jnp.concatenate([sp_ref[h, c].astype(BF16) for c in range(NCH)], axis=0)
                dstn_all = jnp.concatenate(dstn, axis=0)
                dqb = _dot(_chunk_stack(do, chunk_of_row), states, 1, 0)
                at = jnp.where(upp, _dot(km, qm, 1, 1), 0.0)
                di = _dot(at.astype(BF16), do, 1, 0) + _chunk_pick(_dot(kl, dstn_all, 1, 1), chunk_of_row)
                dz_ref[:, ZIH + h * HD:ZIH + (h + 1) * HD] = di.astype(BF16)
                dkl = _dot(_chunk_stack(iv, chunk_of_row), dstn_all, 1, 0)
                da = jnp.where(low, _dot(do, iv, 1, 1), 0.0).astype(BF16)
                dat = jnp.where(upp, _dot(iv, do, 1, 1), 0.0).astype(BF16)
                dqm = _dot(da, km, 1, 0)
                dkm = _dot(dat, qm, 1, 0)
                b = bp[:, ls]
                e1, e2 = jnp.exp(b - w["b_mid"]), jnp.exp(w["b_mid"] - b)
                e3, e4 = jnp.exp(w["b_last"] - b), jnp.exp(b)
                dqf = dqm * e1 + dqb * e4
                dkf_pair.append(dkm * e2 + dkl * e3)
                t_qm, t_km, t_kl = dqm * w["qm"], dkm * w["km"], dkl * w["kl"]
                db = t_qm - t_km - t_kl + dqb * w["qb"]
                db_mid = jnp.sum((t_km - t_qm).reshape(NCH, CHUNK, HD), axis=1, keepdims=True)
                db_last = jnp.sum(t_kl.reshape(NCH, CHUNK, HD), axis=1, keepdims=True)
                db_last = db_last + jnp.stack(dd_rows, axis=0) * jnp.exp(
                    bp[:, ls].reshape(NCH, CHUNK, HD)[:, CHUNK - 1:CHUNK, :])
                spread = lambda v: jnp.broadcast_to(v, (NCH, CHUNK, HD)).reshape(HB, HD)
                db = (db + jnp.where(in_chunk == CHUNK // 2 - 1, spread(db_mid), 0.0)
                      + jnp.where(in_chunk == CHUNK - 1, spread(db_last), 0.0))
                db_pair.append(db)
                sq = w["sq"]
                dz_ref[:, ZQH + h * HD:ZQH + (h + 1) * HD] = (
                    dqf * (HD ** -0.5) * (sq * (1.0 + q * (1.0 - sq)))).astype(BF16)
            dlogf = _chunk_sums(upp_b, jnp.concatenate(db_pair, axis=1))
            dfv = dlogf / fp - jnp.concatenate(dkf_pair, axis=1)
            dz_ref[:, ZFH + 2 * HD * p:ZFH + 2 * HD * (p + 1)] = (dfv * (1.0 - lbp) * sgp * (1.0 - sgp)).astype(BF16)
            dlb_ref[:, 2 * HD * p:2 * HD * (p + 1)] += _rowsum8(dfv * (1.0 - sgp))

    rev = lambda j: nb - 1 - j
    return pl.pallas_call(
        body, name="hgrn_bwd",
        out_shape=(jax.ShapeDtypeStruct((t, D_IN), BF16), jax.ShapeDtypeStruct((8, HG_W), F32),
                   jax.ShapeDtypeStruct((8, HD), F32)),
        grid=(nb,),
        in_specs=_hgrn_cols(rev) + [pl.BlockSpec((2, HG_W), lambda j: (0, 0)), pl.BlockSpec((1, HD), lambda j: (0, 0)),
                                    pl.BlockSpec((HB, HG_W), lambda j: (rev(j), 0)),
                                    pl.BlockSpec((4, NCH, HD, HD), lambda j: (0, rev(j), 0, 0)),
                                    pl.BlockSpec((HB, HG_W), lambda j: (rev(j), 1)),
                                    pl.BlockSpec((HB, SWA_W), lambda j: (rev(j), 0)),
                                    pl.BlockSpec((HB // 2, 2 * KV_W), lambda j: (rev(j), 0)),
                                    pl.BlockSpec((HB // 2, 2 * KV_W), lambda j: (rev(j), 0))],
        out_specs=(pl.BlockSpec((HB, D_IN), lambda j: (rev(j), 0)), pl.BlockSpec((8, HG_W), lambda j: (0, 0)),
                   pl.BlockSpec((8, HD), lambda j: (0, 0))),
        scratch_shapes=[pltpu.VMEM((4, HD, HD), F32)],
        compiler_params=_params(dimension_semantics=("arbitrary",)),
    )(*[z] * 8, hgrn_lb, onorm, o_save, sprev, dymix, *dza)


XB = 512


def _xattn_fwd(q, k, v, t):
    tb = min(XB, t)

    def body(q_ref, k_ref, v_ref, o_ref):
        for h in range(XH):
            cols = slice(XD * h, XD * (h + 1))
            s = _dot(q_ref[:, cols], k_ref[:, cols], 1, 1) * (XD ** -0.5)
            p = jnp.exp(s - jnp.max(s, axis=-1, keepdims=True))
            l = jnp.sum(p, axis=-1, keepdims=True)
            o_ref[:, cols] = (_dot(p.astype(BF16), v_ref[:, cols], 1, 0) * (1.0 / l)).astype(BF16)

    row = pl.BlockSpec((tb, D), lambda i: (i, 0))
    mem = pl.BlockSpec(k.shape, lambda i: (0, 0))
    return pl.pallas_call(
        body, name="xattn_fwd", out_shape=jax.ShapeDtypeStruct((t, D), BF16), grid=(t // tb,),
        in_specs=[row, mem, mem], out_specs=row, compiler_params=_params(),
    )(q, k, v)


def _xattn_bwd(q, k, v, do, t):
    tb = min(XB, t)

    def body(q_ref, k_ref, v_ref, do_ref, dq_ref, dk_ref, dv_ref):
        @pl.when(pl.program_id(0) == 0)
        def _():
            dk_ref[...] = jnp.zeros_like(dk_ref)
            dv_ref[...] = jnp.zeros_like(dv_ref)

        for h in range(XH):
            cols = slice(XD * h, XD * (h + 1))
            qh, kh, vh, doh = q_ref[:, cols], k_ref[:, cols], v_ref[:, cols], do_ref[:, cols]
            s = _dot(qh, kh, 1, 1) * (XD ** -0.5)
            p = jnp.exp(s - jnp.max(s, axis=-1, keepdims=True))
            p = p * (1.0 / jnp.sum(p, axis=-1, keepdims=True))
            dp = _dot(doh, vh, 1, 1)
            ds = (p * (dp - jnp.sum(p * dp, axis=-1, keepdims=True)) * (XD ** -0.5)).astype(BF16)
            dq_ref[:, cols] = _dot(ds, kh, 1, 0).astype(BF16)
            dk_ref[:, cols] += _dot(ds, qh, 0, 0)
            dv_ref[:, cols] += _dot(p.astype(BF16), doh, 0, 0)

    row = pl.BlockSpec((tb, D), lambda i: (i, 0))
    mem = pl.BlockSpec(k.shape, lambda i: (0, 0))
    return pl.pallas_call(
        body, name="xattn_bwd",
        out_shape=(jax.ShapeDtypeStruct((t, D), BF16), jax.ShapeDtypeStruct(k.shape, F32),
                   jax.ShapeDtypeStruct(k.shape, F32)),
        grid=(t // tb,), in_specs=[row, mem, mem, row], out_specs=(row, mem, mem),
        compiler_params=_params(dimension_semantics=("arbitrary",)),
    )(q, k, v, do)


def _mem_gain_bwd(dm, mem, *, name):
    def body(dm_ref, m_ref, dg_ref):
        m_ = m_ref[...]
        dg_ref[...] = _rowsum8(dm_ref[...] * (m_ * _rstd(m_)))

    return pl.pallas_call(body, name=name, out_shape=jax.ShapeDtypeStruct((8, D), F32),
                          compiler_params=_params())(dm, mem)


FM, FN = 512, 1408


def _ffn_up(u, wgt, wut, t):
    tm = min(FM, t)

    def body(u_ref, wg_ref, wu_ref, g_ref, up_ref, a_ref):
        u_ = u_ref[...]
        g = _dot(u_, wg_ref[...], 1, 1)
        up = _dot(u_, wu_ref[...], 1, 1)
        g_ref[...] = g.astype(BF16)
        up_ref[...] = up.astype(BF16)
        a_ref[...] = (g * _sig(g) * up).astype(BF16)

    w = pl.BlockSpec((FN, D), lambda j, i: (j, 0))
    o = pl.BlockSpec((tm, FN), lambda j, i: (i, j))
    return pl.pallas_call(
        body, name="ffn_up", out_shape=(jax.ShapeDtypeStruct((t, D_FF), BF16),) * 3,
        grid=(D_FF // FN, t // tm), in_specs=[pl.BlockSpec((tm, D), lambda j, i: (i, 0)), w, w],
        out_specs=(o, o, o), compiler_params=_params(),
    )(u, wgt, wut)


def _ffn_down_bwd(dy, wd, gate, up, t, dep=None):
    tm = min(FM, t)
    deps = [] if dep is None else [dep]

    def body(dy_ref, w_ref, g_ref, up_ref, *rest):
        dg_ref, dup_ref = rest[len(deps):]
        da = _dot(dy_ref[...], w_ref[...], 1, 1)
        g = g_ref[...].astype(F32)
        sg = _sig(g)
        dup_ref[...] = (da * g * sg).astype(BF16)
        dg_ref[...] = (da * up_ref[...].astype(F32) * (sg * (1.0 + g * (1.0 - sg)))).astype(BF16)

    o = pl.BlockSpec((tm, FN), lambda j, i: (i, j))
    return pl.pallas_call(
        body, name="ffn_down_bwd", out_shape=(jax.ShapeDtypeStruct((t, D_FF), BF16),) * 2,
        grid=(D_FF // FN, t // tm),
        in_specs=[pl.BlockSpec((tm, D), lambda j, i: (i, 0)), pl.BlockSpec((FN, D), lambda j, i: (j, 0)), o, o]
        + [ANY_SPEC] * len(deps),
        out_specs=(o, o), compiler_params=_params(),
    )(dy, wd, gate, up, *deps)


def _local_step(x, mem, target, fetch, sm, emit=None, first_dep=None):
    t = x.shape[0]
    w, gw = {}, {}

    def out(key, g):
        gw[key] = g
        return None if emit is None else emit(key, g)
    u1 = _prenorm(x, sm["g_mix_pre"], name="prenorm_mix", dep=first_dep)
    w["winT"] = fetch("winT", u1)
    z = _mm(u1, w["winT"], tb=True, out_dtype=F32, tm=1024, tn=1408, name="mm_z", n_outer=True)
    ymix, lse = _swa_fwd(z, sm["sinks"], t)
    ymix, o_h, sprev = _hgrn2_fwd(z, sm["hgrn_lb"], sm["hgrn_onorm"], ymix, t)
    w["wout"] = fetch("wout", ymix)
    y1, h1, u2 = _mm_rows([(ymix, w["wout"], False)], [x], [sm["g_mix_post"], sm["g_x_pre"]], _ep_post_pre,
                          _EP_POST_PRE_OUTS, tm=512, name="mm_y1_post")
    mn = _prenorm(mem, sm["g_mem"], name="prenorm_mem")
    for key in ("wq", "wk", "wv"):
        w[key] = fetch(key, u2)
    qx = _mm(u2, w["wq"], out_dtype=BF16, tm=1024, tn=1024, name="mm_qx")
    kx = _mm(mn, w["wk"], out_dtype=BF16, tm=1024, tn=1024, name="mm_kx")
    vx = _mm(mn, w["wv"], out_dtype=BF16, tm=1024, tn=1024, name="mm_vx")
    ox = _xattn_fwd(qx, kx, vx, t)
    w["wo"] = fetch("wo", ox)
    y2, h2, u3 = _mm_rows([(ox, w["wo"], False)], [h1], [sm["g_x_post"], sm["g_ffn_pre"]], _ep_post_pre,
                          _EP_POST_PRE_OUTS, tm=512, name="mm_y2_post")
    w["wgT"], w["wuT"] = fetch("wgT", u3), fetch("wuT", u3)
    gate, up, act = _ffn_up(u3, w["wgT"], w["wuT"], t)
    w["wd"] = fetch("wd", act)
    sq, dh3, dy3, dg_ffn_post = _mm_rows([(act, w["wd"], False)], [h2, target], [sm["g_ffn_post"]], _ep_final_loss,
                                         _EP_FINAL_LOSS_OUTS, tm=512, name="mm_y3_loss")
    dep = out("wd", _mm(act, dy3, ta=True, out_dtype=BF16, tm=1408, tn=1024, name="mm_gwd"))
    dgate, dup = _ffn_down_bwd(dy3, w["wd"], gate, up, t, dep=dep)
    dep = out("wgT", _mm(dgate, u3, ta=True, out_dtype=BF16, tm=1408, tn=1024, name="mm_gwg"))
    dep = out("wuT", _mm(dup, u3, ta=True, out_dtype=BF16, tm=1408, tn=1024, name="mm_gwu", dep=dep))
    dh2, dy2, dg_ffn_pre, dg_x_post = _mm_rows(
        [(dgate, w["wgT"], False), (dup, w["wuT"], False)], [dh3, h2, y2], [sm["g_x_post"], sm["g_ffn_pre"]],
        _ep_post_pre_bwd, _EP_POST_PRE_BWD_OUTS, tm=512, name="mm_du3_post_bwd", dep=dep)
    dep = out("wo", _mm(ox, dy2, ta=True, out_dtype=BF16, tm=512, tn=1024, name="mm_gwo"))
    dox = _mm(dy2, w["wo"], tb=True, out_dtype=BF16, tm=1024, tn=1024, name="mm_dox", dep=dep)
    dqx, dkx, dvx = _xattn_bwd(qx, kx, vx, dox, t)
    dep = out("wq", _mm(u2, dqx, ta=True, out_dtype=BF16, tm=512, tn=1024, name="mm_gwq"))
    dep = out("wk", _mm(mn, dkx, ta=True, out_dtype=BF16, tm=1024, tn=1024, name="mm_gwk", dep=dep))
    dep = out("wv", _mm(mn, dvx, ta=True, out_dtype=BF16, tm=1024, tn=1024, name="mm_gwv", dep=dep))
    dh1, dy1, dg_x_pre, dg_mix_post = _mm_rows(
        [(dqx, w["wq"], True)], [dh2, h1, y1], [sm["g_mix_post"], sm["g_x_pre"]],
        _ep_post_pre_bwd, _EP_POST_PRE_BWD_OUTS, tm=512, name="mm_du2_post_bwd", dep=dep)
    dmn = _mm2(dkx, w["wk"], dvx, w["wv"], tb=True, out_dtype=F32, tm=256, name="mm_dmn")
    dg_mem = _mem_gain_bwd(dmn, mem, name="mem_gain_bwd")
    dep = out("wout", _mm(ymix, dy1, ta=True, out_dtype=BF16, tm=512, tn=1024, name="mm_gwout"))
    dymix = _mm(dy1, w["wout"], tb=True, out_dtype=BF16, tm=1024, tn=1024, name="mm_dymix", dep=dep)
    *dza, dsinks = _swa_bwd(z, sm["sinks"], ymix, lse, dymix, t)
    dz, dlb, donorm = _hgrn2_bwd(z, sm["hgrn_lb"], sm["hgrn_onorm"], o_h, sprev, dymix, dza, t)
    dep = out("winT", _mm(dz, u1, ta=True, out_dtype=BF16, tm=1408, tn=1024, name="mm_gwin"))
    grad_x, dg_mix_pre = _mm_rows([(dz, w["winT"], False)], [dh1, x], [sm["g_mix_pre"]], _ep_pre_bwd,
                                  _EP_PRE_BWD_OUTS, tm=512, name="mm_du1_pre_bwd", dep=dep)
    parts = dict(g_mix_pre=dg_mix_pre, g_mix_post=dg_mix_post, g_mem=dg_mem, g_x_pre=dg_x_pre,
                 g_x_post=dg_x_post, g_ffn_pre=dg_ffn_pre, g_ffn_post=dg_ffn_post,
                 hgrn_onorm=donorm, hgrn_lb=dlb, sinks=dsinks, sq=sq)
    return grad_x, gw, parts


def _position():
    return lax.axis_index("x"), lax.axis_index("y"), lax.axis_index("c")


def _peer(pos, k):
    x, y, c = pos
    return (1 - x if k & 4 else x, 1 - y if k & 2 else y, 1 - c if k & 1 else c)


def _linear(pos):
    x, y, c = pos
    return 4 * x + 2 * y + c


HBM_SPEC = pl.BlockSpec(memory_space=pltpu.HBM)
SEM_SPEC = pl.BlockSpec(memory_space=pltpu.SEMAPHORE)
DATAFLOW = pltpu.SideEffectType.DATAFLOW_SIDE_EFFECTING
SEND_ORDER = (1, 2, 4, 3, 5, 6, 7)


def _in_hbm(a):
    return pltpu.with_memory_space_constraint(a, pltpu.HBM)


def _prepare_weights(shards, *, name, dep=None):
    n = len(shards)
    deps = [] if dep is None else [dep]

    def body(*refs):
        ins, (outs, lands, sem) = refs[:n], (refs[-2 * n - 1:-n - 1], refs[-n - 1:-1], refs[-1])
        me_lin = _linear(_position())
        copies = []
        for a in range(n):
            r = ins[a].shape[0]
            outs[a][...] = ins[a][...].astype(BF16)
            copies.append(pltpu.make_async_copy(outs[a], lands[a].at[pl.ds(me_lin * r, r), :], sem.at[a]))
            copies[-1].start()
        for cp in copies:
            cp.wait()

    vmem = pl.BlockSpec(memory_space=pltpu.VMEM)
    res = pl.pallas_call(
        body, name=name,
        out_shape=tuple(jax.ShapeDtypeStruct(s.shape, BF16) for s in shards)
        + tuple(jax.ShapeDtypeStruct((N_DEV * s.shape[0], s.shape[1]), BF16) for s in shards),
        in_specs=[vmem] * n + [ANY_SPEC] * len(deps), out_specs=tuple([vmem] * n + [ANY_SPEC] * n),
        scratch_shapes=[pltpu.SemaphoreType.DMA((n,))], compiler_params=_params(),
    )(*shards, *deps)
    return res[:n], res[n:]


def _copies_start(arrays, plan, n, *, name):
    na = len(arrays)

    def body(*refs):
        ins, send_sems, recv_sems = refs[:na], refs[na], refs[na + 1]
        me = _position()
        for j in range(n):
            src, dst, peer, _ = plan(ins, me, j)
            pltpu.make_async_remote_copy(src_ref=src, dst_ref=dst, send_sem=send_sems.at[j], recv_sem=recv_sems.at[j],
                                         device_id=peer, device_id_type=MESH).start()

    return pl.pallas_call(
        body, name=name,
        out_shape=(pltpu.SemaphoreType.DMA((n,)), pltpu.SemaphoreType.DMA((n,)))
        + tuple(pltpu.HBM(a.shape, a.dtype) for a in arrays),
        in_specs=(HBM_SPEC,) * na, out_specs=(SEM_SPEC, SEM_SPEC) + (HBM_SPEC,) * na,
        input_output_aliases={i: 2 + i for i in range(na)},
        compiler_params=pltpu.CompilerParams(has_side_effects=DATAFLOW),
    )(*[_in_hbm(a) for a in arrays])


def _copies_wait(send_sems, recv_sems, arrays, plan, n, after, *, name):
    na = len(arrays)

    def body(*refs):
        ins, send_sems, recv_sems = refs[:na], refs[na], refs[na + 1]
        me = _position()
        for j in range(n):
            src, _, peer, landed = plan(ins, me, j)
            copy = pltpu.make_async_remote_copy(src_ref=src, dst_ref=landed, send_sem=send_sems.at[j],
                                                recv_sem=recv_sems.at[j], device_id=peer, device_id_type=MESH)
            copy.wait_send()
            copy.wait_recv()

    return pl.pallas_call(
        body, name=name, out_shape=tuple(pltpu.HBM(a.shape, a.dtype) for a in arrays),
        in_specs=(HBM_SPEC,) * na + (SEM_SPEC, SEM_SPEC, ANY_SPEC), out_specs=(HBM_SPEC,) * na,
        input_output_aliases={i: i for i in range(na)},
        compiler_params=pltpu.CompilerParams(has_side_effects=DATAFLOW),
    )(*arrays, send_sems, recv_sems, after)


SAME_CORE = (2, 4, 6)


def _gather_two_level(shard, land):
    r = shard.shape[0]
    rows = lambda ref, pos: ref.at[pl.ds(_linear(pos) * r, r), :]
    first_peers = (1,) + SAME_CORE

    def first(refs, me, j):
        peer = _peer(me, first_peers[j])
        return refs[0], rows(refs[1], me), peer, rows(refs[1], peer)

    def second(refs, me, j):
        sibling = _peer(me, 1)
        mine = rows(refs[0], _peer(me, SAME_CORE[j]))
        return mine, mine, sibling, rows(refs[0], _peer(sibling, SAME_CORE[j]))

    send1, recv1, shard1, land1 = _copies_start([shard, land], first, 4, name="w_in_send")

    def finish(after):
        _, land2 = _copies_wait(send1, recv1, [shard1, land1], first, 4, after, name="w_in_recv")
        send2, recv2, land3 = _copies_start([land2], second, 3, name="w_in_pass")
        return _copies_wait(send2, recv2, [land3], second, 3, after, name="w_in_pass_recv")[0]

    return shard1, finish


def _gather_start(shards, lands):
    n = len(shards)
    rows = [s.shape[0] for s in shards]

    def body(*refs):
        srcs, land = refs[:n], refs[n:2 * n]
        send_sems, recv_sems = refs[2 * n:3 * n], refs[3 * n:4 * n]
        me = _position()
        for a in range(n):
            mine = land[a].at[pl.ds(_linear(me) * rows[a], rows[a]), :]
            for k in SEND_ORDER:
                pltpu.make_async_remote_copy(
                    src_ref=srcs[a], dst_ref=mine, send_sem=send_sems[a].at[k - 1], recv_sem=recv_sems[a].at[k - 1],
                    device_id=_peer(me, k), device_id_type=MESH).start()

    sems = tuple(pltpu.SemaphoreType.DMA((N_DEV - 1,)) for _ in range(2 * n))
    res = pl.pallas_call(
        body, name="weights_send",
        out_shape=sems + tuple(pltpu.HBM(s.shape, s.dtype) for s in shards)
        + tuple(pltpu.HBM(l.shape, l.dtype) for l in lands),
        in_specs=(HBM_SPEC,) * (2 * n), out_specs=(SEM_SPEC,) * (2 * n) + (HBM_SPEC,) * (2 * n),
        input_output_aliases={i: 2 * n + i for i in range(2 * n)},
        compiler_params=pltpu.CompilerParams(has_side_effects=DATAFLOW),
    )(*[_in_hbm(s) for s in shards], *[_in_hbm(l) for l in lands])
    return [(res[a], res[n + a], res[2 * n + a], res[3 * n + a]) for a in range(n)]


def _gather_wait(send_sems, recv_sems, shard_thru, land_thru, after, *, name):
    r = shard_thru.shape[0]

    def body(src_ref, land_ref, send_sems, recv_sems, after_ref, src_dead, got_ref):
        del after_ref, src_dead, got_ref
        me = _position()
        for k in SEND_ORDER:
            peer = _peer(me, k)
            copy = pltpu.make_async_remote_copy(
                src_ref=src_ref, dst_ref=land_ref.at[pl.ds(_linear(peer) * r, r), :],
                send_sem=send_sems.at[k - 1], recv_sem=recv_sems.at[k - 1],
                device_id=peer, device_id_type=MESH)
            copy.wait_send()
            copy.wait_recv()

    return pl.pallas_call(
        body, name=name,
        out_shape=(pltpu.HBM(shard_thru.shape, shard_thru.dtype), pltpu.HBM(land_thru.shape, land_thru.dtype)),
        in_specs=(HBM_SPEC, HBM_SPEC, SEM_SPEC, SEM_SPEC, ANY_SPEC),
        out_specs=(HBM_SPEC, HBM_SPEC), input_output_aliases={0: 0, 1: 1},
        compiler_params=pltpu.CompilerParams(has_side_effects=DATAFLOW),
    )(shard_thru, land_thru, send_sems, recv_sems, after)[1]


def _exchange_start(gs, *, name):
    n = len(gs)
    rows = [g.shape[0] // N_DEV for g in gs]
    lands = [lax.empty((N_DEV - 1, r, g.shape[1]), g.dtype) for g, r in zip(gs, rows)]

    def body(*refs):
        g_refs, land_refs = refs[:n], refs[n:2 * n]
        send_sems, recv_sems = refs[2 * n:3 * n], refs[3 * n:4 * n]
        me = _position()
        for a in range(n):
            for k in SEND_ORDER:
                peer = _peer(me, k)
                pltpu.make_async_remote_copy(
                    src_ref=g_refs[a].at[pl.ds(_linear(peer) * rows[a], rows[a]), :],
                    dst_ref=land_refs[a].at[k - 1],
                    send_sem=send_sems[a].at[k - 1], recv_sem=recv_sems[a].at[k - 1],
                    device_id=peer, device_id_type=MESH).start()

    res = pl.pallas_call(
        body, name=name,
        out_shape=tuple(pltpu.SemaphoreType.DMA((N_DEV - 1,)) for _ in range(2 * n))
        + tuple(pltpu.HBM(a.shape, a.dtype) for a in gs + lands),
        in_specs=(HBM_SPEC,) * (2 * n), out_specs=(SEM_SPEC,) * (2 * n) + (HBM_SPEC,) * (2 * n),
        input_output_aliases={i: 2 * n + i for i in range(2 * n)},
        compiler_params=pltpu.CompilerParams(has_side_effects=DATAFLOW),
    )(*[_in_hbm(a) for a in gs + lands])
    return [(res[a], res[n + a], res[2 * n + a], res[3 * n + a]) for a in range(n)]


def _exchange_wait(send_sems, recv_sems, g_thru, land_thru, after, *, name):
    r = land_thru.shape[1]

    def body(g_ref, land_ref, send_sems, recv_sems, after_ref, g_dead, got_ref):
        del after_ref, g_dead, got_ref
        me = _position()
        for k in SEND_ORDER:
            peer = _peer(me, k)
            copy = pltpu.make_async_remote_copy(
                src_ref=g_ref.at[pl.ds(_linear(peer) * r, r), :], dst_ref=land_ref.at[k - 1],
                send_sem=send_sems.at[k - 1], recv_sem=recv_sems.at[k - 1],
                device_id=peer, device_id_type=MESH)
            copy.wait_send()
            copy.wait_recv()

    return pl.pallas_call(
        body, name=name,
        out_shape=(pltpu.HBM(g_thru.shape, g_thru.dtype), pltpu.HBM(land_thru.shape, land_thru.dtype)),
        in_specs=(HBM_SPEC, HBM_SPEC, SEM_SPEC, SEM_SPEC, pl.BlockSpec(memory_space=pl.ANY)),
        out_specs=(HBM_SPEC, HBM_SPEC), input_output_aliases={0: 0, 1: 1},
        compiler_params=pltpu.CompilerParams(has_side_effects=DATAFLOW),
    )(g_thru, land_thru, send_sems, recv_sems, after)


def _adamw_math(w, g, m, v):
    m = B1 * m + (1.0 - B1) * g
    v = B2 * v + (1.0 - B2) * (g * g)
    delta = -LR * ((m / C1) / (jnp.sqrt(v / C2) + AEPS) + WD * w)
    return delta, m, v


def _sum_adamw(items, *, name):
    n = len(items)

    def body(*refs):
        ins, outs, scratch = refs[:5 * n], refs[5 * n:9 * n], refs[9 * n:]
        me_lin = _linear(_position())
        mine = []
        for a in range(n):
            r = items[a][2].shape[0]
            mine.append(pltpu.make_async_copy(ins[5 * a].at[pl.ds(me_lin * r, r), :], scratch[a], scratch[n].at[a]))
            mine[-1].start()
        for a in range(n):
            _, land_ref, w_ref, m_ref, v_ref = ins[5 * a:5 * a + 5]
            g_ref, d_ref, nm_ref, nv_ref = outs[4 * a:4 * a + 4]
            g = land_ref[0].astype(F32)
            for s in range(1, N_DEV - 1):
                g = g + land_ref[s].astype(F32)
            mine[a].wait()
            g = scratch[a][...].astype(F32) + g
            g_ref[...] = g
            d_ref[...], nm_ref[...], nv_ref[...] = _adamw_math(w_ref[...], g, m_ref[...], v_ref[...])

    vmem = pl.BlockSpec(memory_space=pltpu.VMEM)
    res = pl.pallas_call(
        body, name=name,
        out_shape=tuple(jax.ShapeDtypeStruct(it[2].shape, F32) for it in items for _ in range(4)),
        in_specs=[ANY_SPEC, vmem, vmem, vmem, vmem] * n, out_specs=(vmem,) * (4 * n),
        scratch_shapes=[pltpu.VMEM(it[2].shape, BF16) for it in items] + [pltpu.SemaphoreType.DMA((n,))],
        compiler_params=_params(),
    )(*[a for it in items for a in it])
    return [res[4 * a:4 * a + 4] for a in range(n)]


SMALL = ("g_mix_pre", "g_mix_post", "g_mem", "g_x_pre", "g_x_post", "g_ffn_pre", "g_ffn_post",
         "hgrn_onorm", "hgrn_lb", "sinks")
SMALL_W = dict(hgrn_onorm=HD, hgrn_lb=HG_W, sinks=8)
SQ_ROW = len(SMALL)
PACK_ROWS = 16


def _small_allreduce(parts, dep):
    ns = len(SMALL)

    def body(*refs):
        part, tot_ref = refs[:ns + 1], refs[ns + 2]
        gath, send_sems, recv_sems = refs[ns + 3:]
        me = _position()
        mine = gath.at[_linear(me)]
        mine[...] = jnp.zeros((PACK_ROWS, D), F32)
        for r, name in enumerate(SMALL):
            wd = SMALL_W.get(name, D)
            mine[r:r + 1, 0:wd] = jnp.sum(part[r][...], axis=0, keepdims=True)[:, 0:wd]
        sq = jnp.sum(part[ns][...]) * (0.5 / D)
        mine[SQ_ROW:SQ_ROW + 1, :] = jnp.full((1, D), sq, F32)

        def copy(k):
            peer = _peer(me, k)
            return pltpu.make_async_remote_copy(
                src_ref=mine, dst_ref=mine, send_sem=send_sems.at[k - 1], recv_sem=recv_sems.at[k - 1],
                device_id=peer, device_id_type=MESH)

        def arrival(k):
            slot = gath.at[_linear(_peer(me, k))]
            return pltpu.make_async_remote_copy(
                src_ref=slot, dst_ref=slot, send_sem=send_sems.at[k - 1], recv_sem=recv_sems.at[k - 1],
                device_id=_peer(me, k), device_id_type=MESH)

        sent = [copy(k) for k in range(1, 8)]
        for cp in sent:
            cp.start()
        for k in range(1, 8):
            arrival(k).wait_recv()
        for cp in sent:
            cp.wait_send()
        tot = gath[0]
        for s in range(1, N_DEV):
            tot = tot + gath[s]
        tot_ref[...] = tot

    vmem = pl.BlockSpec(memory_space=pltpu.VMEM)
    return pl.pallas_call(
        body, name="small_allreduce", out_shape=jax.ShapeDtypeStruct((PACK_ROWS, D), F32),
        in_specs=[vmem] * (ns + 1) + [ANY_SPEC], out_specs=vmem,
        scratch_shapes=[pltpu.VMEM((N_DEV, PACK_ROWS, D), F32), pltpu.SemaphoreType.DMA((7,)),
                        pltpu.SemaphoreType.DMA((7,))],
        compiler_params=_params(has_side_effects=True),
    )(*[parts[n] for n in SMALL], parts["sq"], dep)


def _small_update(tot, sm, m_sm, v_sm):
    ns = len(SMALL)

    def body(*refs):
        tot = refs[0][...]
        w_refs, m_refs, v_refs = refs[1:ns + 1], refs[ns + 1:2 * ns + 1], refs[2 * ns + 1:3 * ns + 1]
        outs = refs[3 * ns + 1:]
        loss_ref = outs[0]
        g_out, d_out = outs[1:ns + 1], outs[ns + 1:2 * ns + 1]
        nm_out, nv_out = outs[2 * ns + 1:3 * ns + 1], outs[3 * ns + 1:4 * ns + 1]
        loss_ref[...] = tot[SQ_ROW:SQ_ROW + 1, 0:1]
        for r, name in enumerate(SMALL):
            wd = SMALL_W.get(name, D)
            g = tot[r:r + 1, 0:wd]
            w = w_refs[r][...]
            if name == "hgrn_lb":
                mx = jnp.maximum(w[0:1], w[1:2])
                e0, e1 = jnp.exp(w[0:1] - mx), jnp.exp(w[1:2] - mx)
                lb0 = e0 / (e0 + e1)
                g0 = g * lb0 * (1.0 - lb0)
                for i, gi in enumerate((g0, -g0)):
                    d, nm, nv = _adamw_math(w[i:i + 1], gi, m_refs[r][i:i + 1, :], v_refs[r][i:i + 1, :])
                    g_out[r][i:i + 1, :] = gi
                    d_out[r][i:i + 1, :], nm_out[r][i:i + 1, :], nv_out[r][i:i + 1, :] = d, nm, nv
            else:
                d, nm, nv = _adamw_math(w, g, m_refs[r][...], v_refs[r][...])
                g_out[r][...] = g
                d_out[r][...], nm_out[r][...], nv_out[r][...] = d, nm, nv

    shapes = [jax.ShapeDtypeStruct(sm[n].shape, F32) for n in SMALL]
    res = pl.pallas_call(
        body, name="small_update", out_shape=tuple([jax.ShapeDtypeStruct((1, 1), F32)] + shapes * 4),
        compiler_params=_params(),
    )(tot, *[sm[n] for n in SMALL], *[m_sm[n] for n in SMALL], *[v_sm[n] for n in SMALL])
    groups = [dict(zip(SMALL, res[1 + i * ns:1 + (i + 1) * ns])) for i in range(4)]
    return res[0], groups[0], groups[1], groups[2], groups[3]


BIG = ("w_in", "w_gate", "w_up", "w_down", "w_out", "wq_x", "wk_x", "wv_x", "wo_x")
BIG_KEY = dict(w_in="winT", w_gate="wgT", w_up="wuT", w_down="wd", w_out="wout", wq_x="wq", wk_x="wk",
               wv_x="wv", wo_x="wo")
TRANSPOSED = ("w_in", "w_gate", "w_up")
WEIGHTS = ("w_in", "sinks", "hgrn_lb", "hgrn_onorm", "w_out", "g_mix_pre", "g_mix_post", "g_mem", "g_x_pre",
           "g_x_post", "wq_x", "wk_x", "wv_x", "wo_x", "g_ffn_pre", "g_ffn_post", "w_gate", "w_up", "w_down")


def kernel(x, mem, w_in, sinks, hgrn_lb, hgrn_onorm, w_out, g_mix_pre, g_mix_post, g_mem, g_x_pre, g_x_post, wq_x, wk_x, wv_x, wo_x, g_ffn_pre, g_ffn_post, w_gate, w_up, w_down, loss_target, m_w_in, m_sinks, m_hgrn_lb, m_hgrn_onorm, m_w_out, m_g_mix_pre, m_g_mix_post, m_g_mem, m_g_x_pre, m_g_x_post, m_wq_x, m_wk_x, m_wv_x, m_wo_x, m_g_ffn_pre, m_g_ffn_post, m_w_gate, m_w_up, m_w_down, v_w_in, v_sinks, v_hgrn_lb, v_hgrn_onorm, v_w_out, v_g_mix_pre, v_g_mix_post, v_g_mem, v_g_x_pre, v_g_x_post, v_wq_x, v_wk_x, v_wv_x, v_wo_x, v_g_ffn_pre, v_g_ffn_post, v_w_gate, v_w_up, v_w_down):
    given = dict(locals())
    wts = {n: given[n] for n in WEIGHTS}
    ms = {n: given["m_" + n] for n in WEIGHTS}
    vs = {n: given["v_" + n] for n in WEIGHTS}

    def mat(a, name):
        a = a[0]
        return a.T if name in TRANSPOSED else a

    shard_in, land_in = _prepare_weights([mat(w_in, "w_in")], name="prepare_w_in")
    sent_in, finish_w_in = _gather_two_level(shard_in[0], land_in[0])
    order = ("w_out", "wq_x", "wk_x", "wv_x", "wo_x", "w_gate", "w_up", "w_down")
    flying = dict(zip(order, _gather_start(*_prepare_weights([mat(wts[n], n) for n in order], name="prepare_weights",
                                                             dep=sent_in))))
    name_of = {k: n for n, k in BIG_KEY.items()}

    def fetch(key, after):
        if key == "winT":
            return finish_w_in(after)
        return _gather_wait(*flying[name_of[key]], after, name="weights_recv_" + name_of[key])

    sm = {n: wts[n] for n in SMALL}
    started, held = {}, {}
    send_with = {k: group for group in (("wgT", "wuT"), ("wo", "wq", "wk", "wv")) for k in group}

    def emit(key, g):
        held[key] = g
        group = send_with.get(key, (key,))
        if key != group[-1]:
            return None
        flights = _exchange_start([held[k] for k in group], name="grad_send_" + name_of[group[0]])
        started.update({name_of[k]: f for k, f in zip(group, flights)})
        return flights[-1][2]

    grad_x, _, parts = _local_step(x[0], mem[0], loss_target[0], fetch, sm, emit, first_dep=flying["w_out"][2])
    grads, deltas, new_m, new_v = {}, {}, {}, {}
    after = grad_x
    for group in (("w_down",), ("w_gate",), ("w_up",), ("wo_x", "wq_x", "wk_x", "wv_x", "w_out"), ("w_in",)):
        items = []
        for n in group:
            g_all, land = _exchange_wait(*started[n], after, name="grad_recv_" + n)
            items.append((g_all, land, mat(wts[n], n), mat(ms[n], n), mat(vs[n], n)))
            after = land
        for n, res in zip(group, _sum_adamw(items, name="adamw_" + group[0])):
            after = res[1]
            if n in TRANSPOSED:
                res = [a.T for a in res]
            grads[n], deltas[n], new_m[n], new_v[n] = [a[None] for a in res]
    loss, g_s, d_s, m_s, v_s = _small_update(_small_allreduce(parts, after), sm, {n: ms[n] for n in SMALL},
                                             {n: vs[n] for n in SMALL})
    grads.update(g_s), deltas.update(d_s), new_m.update(m_s), new_v.update(v_s)
    return (loss[0, 0], grad_x[None], *[grads[n] for n in WEIGHTS], *[deltas[n] for n in WEIGHTS],
            *[new_m[n] for n in WEIGHTS], *[new_v[n] for n in WEIGHTS])
```

```python
import functools

import jax
import jax.numpy as jnp
from jax import lax
from jax.experimental import pallas as pl
from jax.experimental.pallas import tpu as pltpu

F32 = jnp.float32
BF16 = jnp.bfloat16

D = 1024
D_IN = 2816
D_FF = 2816
CHUNK = 64
SWA_W = 512
KV_W = 128
HG_W = 512
HD = 128
ZQH, ZFH, ZIH, ZGH = 768, 1280, 1792, 2304
XH, XD = 4, 256
EPS = 1e-6
NEG = -1e30
N_DEV = 8
MESH = pl.DeviceIdType.MESH

LR, B1, B2, AEPS, WD, STEP = 0.001, 0.9, 0.999, 1e-08, 0.01, 10
C1 = 1.0 - B1 ** STEP
C2 = 1.0 - B2 ** STEP

VMEM_LIMIT = 56 * 1024 * 1024


def _params(**kw):
    return pltpu.CompilerParams(vmem_limit_bytes=VMEM_LIMIT, **kw)


def _sig(x):
    return 1.0 / (1.0 + jnp.exp(-x))


def _rowsum8(x):
    r, w = x.shape
    return jnp.sum(x.reshape(r // 8, 8, w), axis=0)


def _dot(a, b, ca, cb, precision=None):
    return lax.dot_general(a, b, (((ca,), (cb,)), ((), ())), preferred_element_type=F32,
                           precision=precision)


ANY_SPEC = pl.BlockSpec(memory_space=pl.ANY)


def _mm(a, b, *, ta=False, tb=False, out_dtype, tm, tn, tk=None, name, dep=None, n_outer=False):
    m = a.shape[1] if ta else a.shape[0]
    k = a.shape[0] if ta else a.shape[1]
    n = b.shape[0] if tb else b.shape[1]
    tm, tn = min(tm, m), min(tn, n)
    tk = k if tk is None else min(tk, k)
    nk = k // tk
    assert m % tm == 0 and n % tn == 0 and k % tk == 0, (name, m, n, k, tm, tn, tk)
    ij = (lambda g0, g1: (g1, g0)) if n_outer else (lambda g0, g1: (g0, g1))
    a_spec = (pl.BlockSpec((tk, tm), lambda g0, g1, kk: (kk, ij(g0, g1)[0])) if ta
              else pl.BlockSpec((tm, tk), lambda g0, g1, kk: (ij(g0, g1)[0], kk)))
    b_spec = (pl.BlockSpec((tn, tk), lambda g0, g1, kk: (ij(g0, g1)[1], kk)) if tb
              else pl.BlockSpec((tk, tn), lambda g0, g1, kk: (kk, ij(g0, g1)[1])))
    ca, cb = (0 if ta else 1), (1 if tb else 0)

    deps = [] if dep is None else [dep]

    def body(a_ref, b_ref, *rest):
        o_ref, acc = rest[len(deps)], rest[len(deps) + 1:]
        p = _dot(a_ref[...].astype(BF16), b_ref[...].astype(BF16), ca, cb)
        if nk == 1:
            o_ref[...] = p.astype(out_dtype)
        else:
            acc_ref, = acc
            kk = pl.program_id(2)

            @pl.when(kk == 0)
            def _():
                acc_ref[...] = p

            @pl.when(kk > 0)
            def _():
                acc_ref[...] += p

            @pl.when(kk == nk - 1)
            def _():
                o_ref[...] = acc_ref[...].astype(out_dtype)

    return pl.pallas_call(
        body, name=name, out_shape=jax.ShapeDtypeStruct((m, n), out_dtype),
        grid=(n // tn, m // tm, nk) if n_outer else (m // tm, n // tn, nk),
        in_specs=[a_spec, b_spec] + [ANY_SPEC] * len(deps),
        out_specs=pl.BlockSpec((tm, tn), lambda g0, g1, kk: ij(g0, g1)),
        scratch_shapes=[pltpu.VMEM((tm, tn), F32)] if nk > 1 else [],
        compiler_params=_params(dimension_semantics=("parallel", "parallel", "arbitrary")),
    )(a, b, *deps)


def _mm2(a1, b1, a2, b2, *, tb=False, out_dtype, tm, name, dep=None):
    m, k = a1.shape
    n = b1.shape[0] if tb else b1.shape[1]
    tm = min(tm, m)
    assert m % tm == 0
    cb = 1 if tb else 0
    deps = [] if dep is None else [dep]

    def body(a1_ref, b1_ref, a2_ref, b2_ref, *rest):
        o_ref = rest[len(deps)]
        o_ref[...] = (_dot(a1_ref[...].astype(BF16), b1_ref[...], 1, cb)
                      + _dot(a2_ref[...].astype(BF16), b2_ref[...], 1, cb)).astype(out_dtype)

    a_spec = pl.BlockSpec((tm, k), lambda i: (i, 0))
    b_spec = pl.BlockSpec(b1.shape, lambda i: (0, 0))
    return pl.pallas_call(
        body, name=name, out_shape=jax.ShapeDtypeStruct((m, n), out_dtype),
        grid=(m // tm,), in_specs=[a_spec, b_spec, a_spec, b_spec] + [ANY_SPEC] * len(deps),
        out_specs=pl.BlockSpec((tm, n), lambda i: (i, 0)),
        compiler_params=_params(dimension_semantics=("parallel",)),
    )(a1, b1, a2, b2, *deps)


def _mm_rows(prods, rows_in, vecs_in, epilogue, outs, *, tm, name, dep=None):
    m = prods[0][0].shape[0]
    n = prods[0][1].shape[0] if prods[0][2] else prods[0][1].shape[1]
    tm = min(tm, m)
    assert m % tm == 0
    deps = [] if dep is None else [dep]
    n_p, n_r, n_v = len(prods), len(rows_in), len(vecs_in)

    def body(*refs):
        ab = refs[:2 * n_p]
        row_refs = refs[2 * n_p:2 * n_p + n_r]
        vec_refs = refs[2 * n_p + n_r:2 * n_p + n_r + n_v]
        out_refs = refs[2 * n_p + n_r + n_v + len(deps):]
        p = None
        for j, (_, _, tb) in enumerate(prods):
            t = _dot(ab[2 * j][...].astype(BF16), ab[2 * j + 1][...], 1, 1 if tb else 0)
            p = t if p is None else p + t
        vals = epilogue(p, *[r[...] for r in row_refs], *[v[...] for v in vec_refs])
        for (dtype, kind), o_ref, val in zip(outs, out_refs, vals):
            if kind == "row":
                o_ref[...] = val.astype(dtype)
            else:
                @pl.when(pl.program_id(0) == 0)
                def _(o_ref=o_ref):
                    o_ref[...] = jnp.zeros_like(o_ref)

                o_ref[...] += val

    row = lambda w: pl.BlockSpec((tm, w), lambda i: (i, 0))
    whole = lambda a: pl.BlockSpec(a.shape, lambda i: (0,) * a.ndim, pipeline_mode=pl.Buffered(1))
    in_specs, args = [], []
    for a, b, _ in prods:
        in_specs += [row(a.shape[1]), whole(b)]
        args += [a, b]
    in_specs += [row(r.shape[1]) for r in rows_in] + [whole(v) for v in vecs_in] + [ANY_SPEC] * len(deps)
    return pl.pallas_call(
        body, name=name,
        out_shape=tuple(jax.ShapeDtypeStruct((m, n) if kind == "row" else (8, n), dtype) for dtype, kind in outs),
        grid=(m // tm,), in_specs=in_specs,
        out_specs=tuple(row(n) if kind == "row" else pl.BlockSpec((8, n), lambda i: (0, 0)) for _, kind in outs),
        compiler_params=_params(dimension_semantics=("arbitrary",)),
    )(*args, *rows_in, *vecs_in, *deps)


def _rstd(x):
    return lax.rsqrt(jnp.mean(x * x, axis=-1, keepdims=True) + EPS)


def _norm_bwd(xh, r, t):
    return r * (t - xh * jnp.mean(xh * t, axis=-1, keepdims=True))


ROW_F32, ROW_BF16, SUM_F32 = (F32, "row"), (BF16, "row"), (F32, "sum")


def _ep_post_pre(p, h, g_post, g_pre):
    y = p.astype(BF16)
    yf = y.astype(F32)
    hn = h + yf * _rstd(yf) * g_post
    return y, hn, hn * _rstd(hn) * g_pre


_EP_POST_PRE_OUTS = [ROW_BF16, ROW_F32, ROW_BF16]


def _ep_final_loss(y, h, target, g_post):
    r = _rstd(y)
    yh = y * r
    err = h + yh * g_post - target
    dh = err * (1.0 / D)
    return _rowsum8(err * err), dh, _norm_bwd(yh, r, dh * g_post), _rowsum8(dh * yh)


_EP_FINAL_LOSS_OUTS = [SUM_F32, ROW_F32, ROW_BF16, SUM_F32]


def _ep_post_pre_bwd(du, dh_out, hn, y, g_post, g_pre):
    r2 = _rstd(hn)
    xh = hn * r2
    dh = dh_out + _norm_bwd(xh, r2, du * g_pre)
    yf = y.astype(F32)
    r1 = _rstd(yf)
    yh = yf * r1
    return dh, _norm_bwd(yh, r1, dh * g_post), _rowsum8(du * xh), _rowsum8(dh * yh)


_EP_POST_PRE_BWD_OUTS = [ROW_F32, ROW_BF16, SUM_F32, SUM_F32]


def _ep_pre_bwd(du, dh_out, x, g):
    r = _rstd(x)
    xh = x * r
    return dh_out + _norm_bwd(xh, r, du * g), _rowsum8(du * xh)


_EP_PRE_BWD_OUTS = [ROW_F32, SUM_F32]


def _prenorm(x, g, *, name, dep=None):
    t, d = x.shape
    tb = min(512, t)
    deps = [] if dep is None else [dep]

    def body(x_ref, g_ref, *rest):
        xf = x_ref[...]
        rest[-1][...] = (xf * _rstd(xf) * g_ref[...]).astype(BF16)

    return pl.pallas_call(
        body, name=name, out_shape=jax.ShapeDtypeStruct((t, d), BF16), grid=(t // tb,),
        in_specs=[pl.BlockSpec((tb, d), lambda i: (i, 0)), pl.BlockSpec((1, d), lambda i: (0, 0))]
        + [ANY_SPEC] * len(deps),
        out_specs=pl.BlockSpec((tb, d), lambda i: (i, 0)), compiler_params=_params(),
    )(x, g, *deps)


QB = 256


def _half_mask(shape, e):
    lane = lax.broadcasted_iota(jnp.int32, shape, len(shape) - 1)
    return (lane // 64) == e


def _place(kv):
    sw = pltpu.roll(kv, 64, 1)
    m0 = _half_mask(kv.shape, 0)
    return [[jnp.where(m0, kv, 0.0).astype(BF16), jnp.where(m0, 0.0, sw).astype(BF16)],
            [jnp.where(m0, sw, 0.0).astype(BF16), jnp.where(m0, 0.0, kv).astype(BF16)]]


SQ = 128
SK = 256


def _swa_valid(i, sb):
    qc = lax.broadcasted_iota(jnp.int32, (SQ, SK), 0) // CHUNK
    kc = lax.broadcasted_iota(jnp.int32, (SQ, SK), 1) // CHUNK - 2
    return (kc <= qc) & (qc <= kc + 2) & (4 * i + 2 * sb + kc >= 0)


def _swa_fwd(z, sinks, t):
    nb = t // QB

    def body(s_ref, q_ref, kp_ref, kc_ref, vp_ref, vc_ref, o_ref, lse_ref):
        i = pl.program_id(0)
        kpl = _place(jnp.concatenate([kp_ref[...], kc_ref[...]], axis=0))
        vpl = _place(jnp.concatenate([vp_ref[...], vc_ref[...]], axis=0))
        lane = lax.broadcasted_iota(jnp.int32, (SQ, 128), 1)
        for sb in range(QB // SQ):
            rows, keys = slice(SQ * sb, SQ * (sb + 1)), slice(SQ * sb, SQ * sb + SK)
            valid = _swa_valid(i, sb)
            lse_out = jnp.zeros((SQ, 128), F32)
            for j in range(4):
                qp = q_ref[rows, 128 * j:128 * (j + 1)].astype(BF16)
                acc = jnp.zeros((SQ, 128), F32)
                for e in range(2):
                    h = 2 * j + e
                    kvh = h // 4
                    qm = jnp.where(_half_mask(qp.shape, e), qp, jnp.zeros_like(qp))
                    s = _dot(qm, kpl[kvh][e][keys], 1, 1) * 0.125
                    s = jnp.where(valid, s, NEG)
                    sink = s_ref[0, h]
                    m = jnp.maximum(jnp.max(s, axis=-1, keepdims=True), sink)
                    p = jnp.exp(s - m)
                    l = jnp.sum(p, axis=-1, keepdims=True) + jnp.exp(sink - m)
                    acc = acc + _dot(p.astype(BF16), vpl[kvh][e][keys], 1, 0) * (1.0 / l)
                    lse_out = jnp.where(lane == h, m + jnp.log(l), lse_out)
                o_ref[rows, 128 * j:128 * (j + 1)] = acc.astype(BF16)
            lse_ref[rows, :] = lse_out

    prev = lambda c: pl.BlockSpec((128, 128), lambda i: (jnp.maximum(2 * i - 1, 0), c))
    cur = lambda c: pl.BlockSpec((QB, 128), lambda i: (i, c))
    return pl.pallas_call(
        body, name="swa_fwd",
        out_shape=(jax.ShapeDtypeStruct((t, D), BF16), jax.ShapeDtypeStruct((t, 128), F32)),
        grid=(nb,),
        in_specs=[pl.BlockSpec(memory_space=pltpu.SMEM),
                  pl.BlockSpec((QB, SWA_W), lambda i: (i, 0)), prev(4), cur(4), prev(5), cur(5)],
        out_specs=(pl.BlockSpec((QB, SWA_W), lambda i: (i, 0)), pl.BlockSpec((QB, 128), lambda i: (i, 0))),
        compiler_params=_params(),
    )(sinks, z, z, z, z, z)


def _swa_bwd(z, sinks, ymix, lse, dymix, t):
    nb = t // QB

    def body(s_ref, q_ref, kp_ref, kc_ref, vp_ref, vc_ref, o_ref, do_ref, l_ref,
             dq_ref, first_ref, second_ref, ds_ref, carry_ref):
        i = pl.program_id(0)
        live = i < nb

        @pl.when(i == 0)
        def _():
            ds_ref[...] = jnp.zeros_like(ds_ref)
            carry_ref[...] = jnp.zeros_like(carry_ref)

        lane = lax.broadcasted_iota(jnp.int32, (8, 128), 1)
        kpl = _place(jnp.concatenate([kp_ref[...], kc_ref[...]], axis=0))
        vpl = _place(jnp.concatenate([vp_ref[...], vc_ref[...]], axis=0))
        nk = QB + 128
        qc = lax.broadcasted_iota(jnp.int32, (QB, nk), 0) // CHUNK
        kc = lax.broadcasted_iota(jnp.int32, (QB, nk), 1) // CHUNK - 2
        valid = (kc <= qc) & (qc <= kc + 2) & (4 * i + kc >= 0) & live
        lse_c = l_ref[...]
        dsink = jnp.zeros((8, 128), F32)
        dk_acc = [[jnp.zeros((nk, 128), F32) for _ in range(2)] for _ in range(2)]
        dv_acc = [[jnp.zeros((nk, 128), F32) for _ in range(2)] for _ in range(2)]
        dq = []
        for j in range(4):
            cols = slice(128 * j, 128 * (j + 1))
            qp = q_ref[:, cols].astype(BF16)
            dop = do_ref[:, cols]
            prod = dop.astype(F32) * o_ref[:, cols].astype(F32)
            acc = jnp.zeros((QB, 128), F32)
            for e in range(2):
                h = 2 * j + e
                kvh = h // 4
                hm = _half_mask(qp.shape, e)
                qm = jnp.where(hm, qp, jnp.zeros_like(qp))
                dom = jnp.where(hm, dop, jnp.zeros_like(dop))
                dd = jnp.sum(jnp.where(hm, prod, 0.0), axis=-1, keepdims=True)
                lse_h = lse_c[:, h:h + 1]
                s = _dot(qm, kpl[kvh][e], 1, 1) * 0.125
                p = jnp.where(valid, jnp.exp(s - lse_h), 0.0)
                dp = _dot(dom, vpl[kvh][e], 1, 1)
                ds = (p * (dp - dd) * 0.125).astype(BF16)
                acc = acc + _dot(ds, kpl[kvh][e], 1, 0)
                dk_acc[kvh][e] = dk_acc[kvh][e] + _dot(ds, qm, 0, 0)
                dv_acc[kvh][e] = dv_acc[kvh][e] + _dot(p.astype(BF16), dom, 0, 0)
                ps = jnp.where(live, jnp.exp(s_ref[0, h] - lse_h) * dd, 0.0)
                dsink = dsink - jnp.where(lane == h, _rowsum8(jnp.broadcast_to(ps, (QB, 128))), 0.0)
            dq.append(acc.astype(BF16))
        ds_ref[...] += dsink
        dk = dk_acc[0][0] + dk_acc[1][1] + pltpu.roll(dk_acc[0][1] + dk_acc[1][0], 64, 1)
        dv = dv_acc[0][0] + dv_acc[1][1] + pltpu.roll(dv_acc[0][1] + dv_acc[1][0], 64, 1)
        dkv = jnp.concatenate([dk, dv], axis=1)
        second_ref[...] = (carry_ref[...] + dkv[0:128]).astype(BF16)
        carry_ref[...] = dkv[256:384]

        @pl.when(live)
        def _():
            for j in range(4):
                dq_ref[:, 128 * j:128 * (j + 1)] = dq[j]
            first_ref[...] = dkv[128:256].astype(BF16)

    blk = lambda i: jnp.minimum(i, nb - 1)
    prev = lambda c: pl.BlockSpec((128, 128), lambda i: (jnp.maximum(2 * blk(i) - 1, 0), c))
    cur = lambda w, c: pl.BlockSpec((QB, w), lambda i: (blk(i), c))
    half = lambda index: pl.BlockSpec((128, 256), lambda i: (index(i), 0))
    return pl.pallas_call(
        body, name="swa_bwd",
        out_shape=(jax.ShapeDtypeStruct((t, SWA_W), BF16), jax.ShapeDtypeStruct((t // 2, 256), BF16),
                   jax.ShapeDtypeStruct((t // 2, 256), BF16), jax.ShapeDtypeStruct((8, 128), F32)),
        grid=(nb + 1,),
        in_specs=[pl.BlockSpec(memory_space=pltpu.SMEM),
                  cur(SWA_W, 0), prev(4), cur(128, 4), prev(5), cur(128, 5),
                  cur(SWA_W, 0), cur(SWA_W, 0), cur(128, 0)],
        out_specs=(cur(SWA_W, 0), half(blk), half(lambda i: jnp.maximum(i - 1, 0)),
                   pl.BlockSpec((8, 128), lambda i: (0, 0))),
        scratch_shapes=[pltpu.VMEM((128, 256), F32)],
        compiler_params=_params(dimension_semantics=("arbitrary",)),
    )(sinks, z, z, z, z, z, ymix, dymix, lse)


HB = 256


def _lower_bound(lb_ref):
    a = lb_ref[...]
    a0, a1 = a[0:1], a[1:2]
    mx = jnp.maximum(a0, a1)
    e0, e1 = jnp.exp(a0 - mx), jnp.exp(a1 - mx)
    return e0 / (e0 + e1)


def _hgrn_cols(row_block):
    return [pl.BlockSpec((HB, 2 * HD), lambda j, c=base // (2 * HD) + p: (row_block(j), c))
            for base in (ZQH, ZFH, ZIH, ZGH) for p in range(2)]


NCH = HB // CHUNK


def _split3(x):
    hi = x.astype(BF16)
    r1 = x - hi.astype(F32)
    mid = r1.astype(BF16)
    return hi, mid, (r1 - mid.astype(F32)).astype(BF16)


def _blockdiag(lower):
    r = lax.broadcasted_iota(jnp.int32, (HB, HB), 0)
    c = lax.broadcasted_iota(jnp.int32, (HB, HB), 1)
    return (r // CHUNK == c // CHUNK) & ((c <= r) if lower else (c >= r))


def _chunk_sums(mask_bf16, x):
    return sum(_dot(mask_bf16, part, 1, 0) for part in _split3(x))


def _per_chunk_rows(x, row):
    w = x.shape[1]
    picked = x.reshape(NCH, CHUNK, w)[:, row:row + 1, :]
    return jnp.broadcast_to(picked, (NCH, CHUNK, w)).reshape(HB, w)


def _chunk_stack(x, chunk_of_row):
    return jnp.concatenate([jnp.where(chunk_of_row == c, x, jnp.zeros_like(x)) for c in range(NCH)], axis=1)


def _chunk_pick(x, chunk_of_row):
    w = x.shape[1] // NCH
    out = jnp.zeros((HB, w), x.dtype)
    for c in range(NCH):
        out = jnp.where(chunk_of_row == c, x[:, c * w:(c + 1) * w], out)
    return out


def _hgrn_local(q, f, kf, b):
    sq = _sig(q)
    qf = q * sq * (HD ** -0.5)
    b_mid = _per_chunk_rows(b, CHUNK // 2 - 1)
    b_last = _per_chunk_rows(b, CHUNK - 1)
    qm = qf * jnp.exp(b - b_mid)
    km = kf * jnp.exp(b_mid - b)
    kl = kf * jnp.exp(b_last - b)
    qb = qf * jnp.exp(b)
    return dict(sq=sq, b_mid=b_mid, b_last=b_last, qm=qm, km=km, kl=kl, qb=qb)


def _hgrn2_fwd(z, hgrn_lb, onorm, ymix, t):
    nb = t // HB

    def body(*refs):
        zq, zf, zi, zg = refs[0:2], refs[2:4], refs[4:6], refs[6:8]
        lb_ref, on_ref, _, y_ref, o_ref, sp_ref, st_ref = refs[8:]

        @pl.when(pl.program_id(0) == 0)
        def _():
            st_ref[...] = jnp.zeros_like(st_ref)

        lb_all = _lower_bound(lb_ref)
        gn = on_ref[...]
        low = _blockdiag(True)
        low_b = low.astype(BF16)
        chunk_of_row = lax.broadcasted_iota(jnp.int32, (HB, HD), 0) // CHUNK
        for p in range(2):
            lbp = lb_all[:, 2 * HD * p:2 * HD * (p + 1)]
            fp = lbp + (1.0 - lbp) * _sig(zf[p][...])
            bp = _chunk_sums(low_b, jnp.log(fp))
            for e in range(2):
                h, ls = 2 * p + e, slice(e * HD, (e + 1) * HD)
                f = fp[:, ls]
                w = _hgrn_local(zq[p][:, ls], f, 1.0 - f, bp[:, ls])
                iv = zi[p][:, ls].astype(BF16)
                a = jnp.where(low, _dot(w["qm"].astype(BF16), w["km"].astype(BF16), 1, 1), 0.0)
                o = _dot(a.astype(BF16), iv, 1, 0)
                u = _dot(iv, _chunk_stack(w["kl"].astype(BF16), chunk_of_row), 0, 0)
                decay = jnp.exp(w["b_last"])
                st = st_ref[h]
                states = []
                for c in range(NCH):
                    sp_ref[h, c] = st
                    states.append(st.astype(BF16))
                    st = st * decay[c * CHUNK:c * CHUNK + 1] + u[:, c * HD:(c + 1) * HD]
                st_ref[h] = st
                inter = _dot(w["qb"].astype(BF16), jnp.concatenate(states, axis=0), 1, 1)
                o = o + _chunk_pick(inter, chunk_of_row)
                hs = slice(h * HD, (h + 1) * HD)
                o_ref[:, hs] = o
                gg = zg[p][:, ls]
                y_ref[:, hs] = (o * _rstd(o) * gn * (gg * _sig(gg))).astype(BF16)

    return pl.pallas_call(
        body, name="hgrn_fwd",
        out_shape=(jax.ShapeDtypeStruct((t, D), BF16), jax.ShapeDtypeStruct((t, HG_W), F32),
                   jax.ShapeDtypeStruct((4, t // CHUNK, HD, HD), F32)),
        grid=(nb,),
        in_specs=_hgrn_cols(lambda j: j) + [pl.BlockSpec((2, HG_W), lambda j: (0, 0)),
                                            pl.BlockSpec((1, HD), lambda j: (0, 0)), ANY_SPEC],
        out_specs=(pl.BlockSpec((HB, HG_W), lambda j: (j, 1)),
                   pl.BlockSpec((HB, HG_W), lambda j: (j, 0)),
                   pl.BlockSpec((4, NCH, HD, HD), lambda j: (0, j, 0, 0))),
        scratch_shapes=[pltpu.VMEM((4, HD, HD), F32)],
        input_output_aliases={10: 0},
        compiler_params=_params(dimension_semantics=("arbitrary",)),
    )(*[z] * 8, hgrn_lb, onorm, ymix)


def _hgrn2_bwd(z, hgrn_lb, onorm, o_save, sprev, dymix, dza, t):
    nb = t // HB

    def body(*refs):
        zq, zf, zi, zg = refs[0:2], refs[2:4], refs[4:6], refs[6:8]
        (lb_ref, on_ref, o_ref, sp_ref, dy_ref, dqa_ref, first_ref, second_ref,
         dz_ref, dlb_ref, don_ref, dst_ref) = refs[8:]

        @pl.when(pl.program_id(0) == 0)
        def _():
            dst_ref[...] = jnp.zeros_like(dst_ref)
            dlb_ref[...] = jnp.zeros_like(dlb_ref)
            don_ref[...] = jnp.zeros_like(don_ref)

        dz_ref[:, 0:SWA_W] = dqa_ref[...]
        dz_ref[0:HB // 2, SWA_W:ZQH] = first_ref[...]
        dz_ref[HB // 2:HB, SWA_W:ZQH] = second_ref[...]
        lb_all = _lower_bound(lb_ref)
        gn = on_ref[...]
        low, upp = _blockdiag(True), _blockdiag(False)
        upp_b = upp.astype(BF16)
        low_b = low.astype(BF16)
        row = lax.broadcasted_iota(jnp.int32, (HB, HD), 0)
        chunk_of_row = row // CHUNK
        in_chunk = row % CHUNK
        for p in range(2):
            lbp = lb_all[:, 2 * HD * p:2 * HD * (p + 1)]
            sgp = _sig(zf[p][...])
            fp = lbp + (1.0 - lbp) * sgp
            bp = _chunk_sums(low_b, jnp.log(fp))
            db_pair, dkf_pair = [], []
            for e in range(2):
                h, ls, hs = 2 * p + e, slice(e * HD, (e + 1) * HD), slice((2 * p + e) * HD, (2 * p + e + 1) * HD)
                f = fp[:, ls]
                q = zq[p][:, ls]
                w = _hgrn_local(q, f, 1.0 - f, bp[:, ls])
                iv = zi[p][:, ls].astype(BF16)
                gg = zg[p][:, ls]
                o = o_ref[:, hs]
                dout = dy_ref[:, hs].astype(F32)
                sgg = _sig(gg)
                r = _rstd(o)
                oh = o * r
                dyn = dout * (gg * sgg)
                dz_ref[:, ZGH + h * HD:ZGH + (h + 1) * HD] = (
                    dout * oh * gn * (sgg * (1.0 + gg * (1.0 - sgg)))).astype(BF16)
                don_ref[...] += _rowsum8(dyn * oh)
                do = _norm_bwd(oh, r, dyn * gn).astype(BF16)
                qm, km, kl, qb = (w[n].astype(BF16) for n in ("qm", "km", "kl", "qb"))
                decay = jnp.exp(w["b_last"])
                grads_in = _dot(do, _chunk_stack(qb, chunk_of_row), 0, 0)
                dst = dst_ref[h]
                dstn, dd_rows = [None] * NCH, [None] * NCH
                for c in reversed(range(NCH)):
                    dstn[c] = dst.astype(BF16)
                    dd_rows[c] = jnp.sum(dst * sp_ref[h, c], axis=0, keepdims=True)
                    dst = dst * decay[c * CHUNK:c * CHUNK + 1] + grads_in[:, c * HD:(c + 1) * HD]
                dst_ref[h] = dst
                states = jnp.concatenate([sp_ref[h, c].astype(BF16) for c in range(NCH)], axis=0)
                dstn_all = jnp.concatenate(dstn, axis=0)
                dqb = _dot(_chunk_stack(do, chunk_of_row), states, 1, 0)
                at = jnp.where(upp, _dot(km, qm, 1, 1), 0.0)
                di = _dot(at.astype(BF16), do, 1, 0) + _chunk_pick(_dot(kl, dstn_all, 1, 1), chunk_of_row)
                dz_ref[:, ZIH + h * HD:ZIH + (h + 1) * HD] = di.astype(BF16)
                dkl = _dot(_chunk_stack(iv, chunk_of_row), dstn_all, 1, 0)
                da = jnp.where(low, _dot(do, iv, 1, 1), 0.0).astype(BF16)
                dat = jnp.where(upp, _dot(iv, do, 1, 1), 0.0).astype(BF16)
                dqm = _dot(da, km, 1, 0)
                dkm = _dot(dat, qm, 1, 0)
                b = bp[:, ls]
                e1, e2 = jnp.exp(b - w["b_mid"]), jnp.exp(w["b_mid"] - b)
                e3, e4 = jnp.exp(w["b_last"] - b), jnp.exp(b)
                dqf = dqm * e1 + dqb * e4
                dkf_pair.append(dkm * e2 + dkl * e3)
                t_qm, t_km, t_kl = dqm * w["qm"], dkm * w["km"], dkl * w["kl"]
                db = t_qm - t_km - t_kl + dqb * w["qb"]
                db_mid = jnp.sum((t_km - t_qm).reshape(NCH, CHUNK, HD), axis=1, keepdims=True)
                db_last = jnp.sum(t_kl.reshape(NCH, CHUNK, HD), axis=1, keepdims=True)
                db_last = db_last + jnp.stack(dd_rows, axis=0) * jnp.exp(
                    bp[:, ls].reshape(NCH, CHUNK, HD)[:, CHUNK - 1:CHUNK, :])
                spread = lambda v: jnp.broadcast_to(v, (NCH, CHUNK, HD)).reshape(HB, HD)
                db = (db + jnp.where(in_chunk == CHUNK // 2 - 1, spread(db_mid), 0.0)
                      + jnp.where(in_chunk == CHUNK - 1, spread(db_last), 0.0))
                db_pair.append(db)
                sq = w["sq"]
                dz_ref[:, ZQH + h * HD:ZQH + (h + 1) * HD] = (
                    dqf * (HD ** -0.5) * (sq * (1.0 + q * (1.0 - sq)))).astype(BF16)
            dlogf = _chunk_sums(upp_b, jnp.concatenate(db_pair, axis=1))
            dfv = dlogf / fp - jnp.concatenate(dkf_pair, axis=1)
            dz_ref[:, ZFH + 2 * HD * p:ZFH + 2 * HD * (p + 1)] = (dfv * (1.0 - lbp) * sgp * (1.0 - sgp)).astype(BF16)
            dlb_ref[:, 2 * HD * p:2 * HD * (p + 1)] += _rowsum8(dfv * (1.0 - sgp))

    rev = lambda j: nb - 1 - j
    return pl.pallas_call(
        body, name="hgrn_bwd",
        out_shape=(jax.ShapeDtypeStruct((t, D_IN), BF16), jax.ShapeDtypeStruct((8, HG_W), F32),
                   jax.ShapeDtypeStruct((8, HD), F32)),
        grid=(nb,),
        in_specs=_hgrn_cols(rev) + [pl.BlockSpec((2, HG_W), lambda j: (0, 0)), pl.BlockSpec((1, HD), lambda j: (0, 0)),
                                    pl.BlockSpec((HB, HG_W), lambda j: (rev(j), 0)),
                                    pl.BlockSpec((4, NCH, HD, HD), lambda j: (0, rev(j), 0, 0)),
                                    pl.BlockSpec((HB, HG_W), lambda j: (rev(j), 1)),
                                    pl.BlockSpec((HB, SWA_W), lambda j: (rev(j), 0)),
                                    pl.BlockSpec((HB // 2, 2 * KV_W), lambda j: (rev(j), 0)),
                                    pl.BlockSpec((HB // 2, 2 * KV_W), lambda j: (rev(j), 0))],
        out_specs=(pl.BlockSpec((HB, D_IN), lambda j: (rev(j), 0)), pl.BlockSpec((8, HG_W), lambda j: (0, 0)),
                   pl.BlockSpec((8, HD), lambda j: (0, 0))),
        scratch_shapes=[pltpu.VMEM((4, HD, HD), F32)],
        compiler_params=_params(dimension_semantics=("arbitrary",)),
    )(*[z] * 8, hgrn_lb, onorm, o_save, sprev, dymix, *dza)


XB = 512


def _xattn_fwd(q, k, v, t):
    tb = min(XB, t)

    def body(q_ref, k_ref, v_ref, o_ref):
        for h in range(XH):
            cols = slice(XD * h, XD * (h + 1))
            s = _dot(q_ref[:, cols], k_ref[:, cols], 1, 1) * (XD ** -0.5)
            p = jnp.exp(s - jnp.max(s, axis=-1, keepdims=True))
            l = jnp.sum(p, axis=-1, keepdims=True)
            o_ref[:, cols] = (_dot(p.astype(BF16), v_ref[:, cols], 1, 0) * (1.0 / l)).astype(BF16)

    row = pl.BlockSpec((tb, D), lambda i: (i, 0))
    mem = pl.BlockSpec(k.shape, lambda i: (0, 0))
    return pl.pallas_call(
        body, name="xattn_fwd", out_shape=jax.ShapeDtypeStruct((t, D), BF16), grid=(t // tb,),
        in_specs=[row, mem, mem], out_specs=row, compiler_params=_params(),
    )(q, k, v)


def _xattn_bwd(q, k, v, do, t):
    tb = min(XB, t)

    def body(q_ref, k_ref, v_ref, do_ref, dq_ref, dk_ref, dv_ref):
        @pl.when(pl.program_id(0) == 0)
        def _():
            dk_ref[...] = jnp.zeros_like(dk_ref)
            dv_ref[...] = jnp.zeros_like(dv_ref)

        for h in range(XH):
            cols = slice(XD * h, XD * (h + 1))
            qh, kh, vh, doh = q_ref[:, cols], k_ref[:, cols], v_ref[:, cols], do_ref[:, cols]
            s = _dot(qh, kh, 1, 1) * (XD ** -0.5)
            p = jnp.exp(s - jnp.max(s, axis=-1, keepdims=True))
            p = p * (1.0 / jnp.sum(p, axis=-1, keepdims=True))
            dp = _dot(doh, vh, 1, 1)
            ds = (p * (dp - jnp.sum(p * dp, axis=-1, keepdims=True)) * (XD ** -0.5)).astype(BF16)
            dq_ref[:, cols] = _dot(ds, kh, 1, 0).astype(BF16)
            dk_ref[:, cols] += _dot(ds, qh, 0, 0)
            dv_ref[:, cols] += _dot(p.astype(BF16), doh, 0, 0)

    row = pl.BlockSpec((tb, D), lambda i: (i, 0))
    mem = pl.BlockSpec(k.shape, lambda i: (0, 0))
    return pl.pallas_call(
        body, name="xattn_bwd",
        out_shape=(jax.ShapeDtypeStruct((t, D), BF16), jax.ShapeDtypeStruct(k.shape, F32),
                   jax.ShapeDtypeStruct(k.shape, F32)),
        grid=(t // tb,), in_specs=[row, mem, mem, row], out_specs=(row, mem, mem),
        compiler_params=_params(dimension_semantics=("arbitrary",)),
    )(q, k, v, do)


def _mem_gain_bwd(dm, mem, *, name):
    def body(dm_ref, m_ref, dg_ref):
        m_ = m_ref[...]
        dg_ref[...] = _rowsum8(dm_ref[...] * (m_ * _rstd(m_)))

    return pl.pallas_call(body, name=name, out_shape=jax.ShapeDtypeStruct((8, D), F32),
                          compiler_params=_params())(dm, mem)


FM, FN = 512, 1408


def _ffn_up(u, wgt, wut, t):
    tm = min(FM, t)

    def body(u_ref, wg_ref, wu_ref, g_ref, up_ref, a_ref):
        u_ = u_ref[...]
        g = _dot(u_, wg_ref[...], 1, 1)
        up = _dot(u_, wu_ref[...], 1, 1)
        g_ref[...] = g.astype(BF16)
        up_ref[...] = up.astype(BF16)
        a_ref[...] = (g * _sig(g) * up).astype(BF16)

    w = pl.BlockSpec((FN, D), lambda j, i: (j, 0))
    o = pl.BlockSpec((tm, FN), lambda j, i: (i, j))
    return pl.pallas_call(
        body, name="ffn_up", out_shape=(jax.ShapeDtypeStruct((t, D_FF), BF16),) * 3,
        grid=(D_FF // FN, t // tm), in_specs=[pl.BlockSpec((tm, D), lambda j, i: (i, 0)), w, w],
        out_specs=(o, o, o), compiler_params=_params(),
    )(u, wgt, wut)


def _ffn_down_bwd(dy, wd, gate, up, t, dep=None):
    tm = min(FM, t)
    deps = [] if dep is None else [dep]

    def body(dy_ref, w_ref, g_ref, up_ref, *rest):
        dg_ref, dup_ref = rest[len(deps):]
        da = _dot(dy_ref[...], w_ref[...], 1, 1)
        g = g_ref[...].astype(F32)
        sg = _sig(g)
        dup_ref[...] = (da * g * sg).astype(BF16)
        dg_ref[...] = (da * up_ref[...].astype(F32) * (sg * (1.0 + g * (1.0 - sg)))).astype(BF16)

    o = pl.BlockSpec((tm, FN), lambda j, i: (i, j))
    return pl.pallas_call(
        body, name="ffn_down_bwd", out_shape=(jax.ShapeDtypeStruct((t, D_FF), BF16),) * 2,
        grid=(D_FF // FN, t // tm),
        in_specs=[pl.BlockSpec((tm, D), lambda j, i: (i, 0)), pl.BlockSpec((FN, D), lambda j, i: (j, 0)), o, o]
        + [ANY_SPEC] * len(deps),
        out_specs=(o, o), compiler_params=_params(),
    )(dy, wd, gate, up, *deps)


def _local_step(x, mem, target, fetch, sm, emit=None, first_dep=None):
    t = x.shape[0]
    w, gw = {}, {}

    def out(key, g):
        gw[key] = g
        return None if emit is None else emit(key, g)
    u1 = _prenorm(x, sm["g_mix_pre"], name="prenorm_mix", dep=first_dep)
    w["winT"] = fetch("winT", u1)
    z = _mm(u1, w["winT"], tb=True, out_dtype=F32, tm=1024, tn=1408, name="mm_z", n_outer=True)
    ymix, lse = _swa_fwd(z, sm["sinks"], t)
    ymix, o_h, sprev = _hgrn2_fwd(z, sm["hgrn_lb"], sm["hgrn_onorm"], ymix, t)
    w["wout"] = fetch("wout", ymix)
    y1, h1, u2 = _mm_rows([(ymix, w["wout"], False)], [x], [sm["g_mix_post"], sm["g_x_pre"]], _ep_post_pre,
                          _EP_POST_PRE_OUTS, tm=512, name="mm_y1_post")
    mn = _prenorm(mem, sm["g_mem"], name="prenorm_mem")
    for key in ("wq", "wk", "wv"):
        w[key] = fetch(key, u2)
    qx = _mm(u2, w["wq"], out_dtype=BF16, tm=1024, tn=1024, name="mm_qx")
    kx = _mm(mn, w["wk"], out_dtype=BF16, tm=1024, tn=1024, name="mm_kx")
    vx = _mm(mn, w["wv"], out_dtype=BF16, tm=1024, tn=1024, name="mm_vx")
    ox = _xattn_fwd(qx, kx, vx, t)
    w["wo"] = fetch("wo", ox)
    y2, h2, u3 = _mm_rows([(ox, w["wo"], False)], [h1], [sm["g_x_post"], sm["g_ffn_pre"]], _ep_post_pre,
                          _EP_POST_PRE_OUTS, tm=512, name="mm_y2_post")
    w["wgT"], w["wuT"] = fetch("wgT", u3), fetch("wuT", u3)
    gate, up, act = _ffn_up(u3, w["wgT"], w["wuT"], t)
    w["wd"] = fetch("wd", act)
    sq, dh3, dy3, dg_ffn_post = _mm_rows([(act, w["wd"], False)], [h2, target], [sm["g_ffn_post"]], _ep_final_loss,
                                         _EP_FINAL_LOSS_OUTS, tm=512, name="mm_y3_loss")
    dep = out("wd", _mm(act, dy3, ta=True, out_dtype=BF16, tm=1408, tn=1024, name="mm_gwd"))
    dgate, dup = _ffn_down_bwd(dy3, w["wd"], gate, up, t, dep=dep)
    dep = out("wgT", _mm(dgate, u3, ta=True, out_dtype=BF16, tm=1408, tn=1024, name="mm_gwg"))
    dep = out("wuT", _mm(dup, u3, ta=True, out_dtype=BF16, tm=1408, tn=1024, name="mm_gwu", dep=dep))
    dh2, dy2, dg_ffn_pre, dg_x_post = _mm_rows(
        [(dgate, w["wgT"], False), (dup, w["wuT"], False)], [dh3, h2, y2], [sm["g_x_post"], sm["g_ffn_pre"]],
        _ep_post_pre_bwd, _EP_POST_PRE_BWD_OUTS, tm=512, name="mm_du3_post_bwd", dep=dep)
    dep = out("wo", _mm(ox, dy2, ta=True, out_dtype=BF16, tm=512, tn=1024, name="mm_gwo"))
    dox = _mm(dy2, w["wo"], tb=True, out_dtype=BF16, tm=1024, tn=1024, name="mm_dox", dep=dep)
    dqx, dkx, dvx = _xattn_bwd(qx, kx, vx, dox, t)
    dep = out("wq", _mm(u2, dqx, ta=True, out_dtype=BF16, tm=512, tn=1024, name="mm_gwq"))
    dep = out("wk", _mm(mn, dkx, ta=True, out_dtype=BF16, tm=1024, tn=1024, name="mm_gwk", dep=dep))
    dep = out("wv", _mm(mn, dvx, ta=True, out_dtype=BF16, tm=1024, tn=1024, name="mm_gwv", dep=dep))
    dh1, dy1, dg_x_pre, dg_mix_post = _mm_rows(
        [(dqx, w["wq"], True)], [dh2, h1, y1], [sm["g_mix_post"], sm["g_x_pre"]],
        _ep_post_pre_bwd, _EP_POST_PRE_BWD_OUTS, tm=512, name="mm_du2_post_bwd", dep=dep)
    dmn = _mm2(dkx, w["wk"], dvx, w["wv"], tb=True, out_dtype=F32, tm=256, name="mm_dmn")
    dg_mem = _mem_gain_bwd(dmn, mem, name="mem_gain_bwd")
    dep = out("wout", _mm(ymix, dy1, ta=True, out_dtype=BF16, tm=512, tn=1024, name="mm_gwout"))
    dymix = _mm(dy1, w["wout"], tb=True, out_dtype=BF16, tm=1024, tn=1024, name="mm_dymix", dep=dep)
    *dza, dsinks = _swa_bwd(z, sm["sinks"], ymix, lse, dymix, t)
    dz, dlb, donorm = _hgrn2_bwd(z, sm["hgrn_lb"], sm["hgrn_onorm"], o_h, sprev, dymix, dza, t)
    dep = out("winT", _mm(dz, u1, ta=True, out_dtype=BF16, tm=1408, tn=1024, name="mm_gwin"))
    grad_x, dg_mix_pre = _mm_rows([(dz, w["winT"], False)], [dh1, x], [sm["g_mix_pre"]], _ep_pre_bwd,
                                  _EP_PRE_BWD_OUTS, tm=512, name="mm_du1_pre_bwd", dep=dep)
    parts = dict(g_mix_pre=dg_mix_pre, g_mix_post=dg_mix_post, g_mem=dg_mem, g_x_pre=dg_x_pre,
                 g_x_post=dg_x_post, g_ffn_pre=dg_ffn_pre, g_ffn_post=dg_ffn_post,
                 hgrn_onorm=donorm, hgrn_lb=dlb, sinks=dsinks, sq=sq)
    return grad_x, gw, parts


def _position():
    return lax.axis_index("x"), lax.axis_index("y"), lax.axis_index("c")


def _peer(pos, k):
    x, y, c = pos
    return (1 - x if k & 4 else x, 1 - y if k & 2 else y, 1 - c if k & 1 else c)


def _linear(pos):
    x, y, c = pos
    return 4 * x + 2 * y + c


HBM_SPEC = pl.BlockSpec(memory_space=pltpu.HBM)
SEM_SPEC = pl.BlockSpec(memory_space=pltpu.SEMAPHORE)
DATAFLOW = pltpu.SideEffectType.DATAFLOW_SIDE_EFFECTING
SEND_ORDER = (1, 2, 4, 3, 5, 6, 7)


def _in_hbm(a):
    return pltpu.with_memory_space_constraint(a, pltpu.HBM)


def _prepare_weights(shards, *, name, dep=None):
    n = len(shards)
    deps = [] if dep is None else [dep]

    def body(*refs):
        ins, (outs, lands, sem) = refs[:n], (refs[-2 * n - 1:-n - 1], refs[-n - 1:-1], refs[-1])
        me_lin = _linear(_position())
        copies = []
        for a in range(n):
            r = ins[a].shape[0]
            outs[a][...] = ins[a][...].astype(BF16)
            copies.append(pltpu.make_async_copy(outs[a], lands[a].at[pl.ds(me_lin * r, r), :], sem.at[a]))
            copies[-1].start()
        for cp in copies:
            cp.wait()

    vmem = pl.BlockSpec(memory_space=pltpu.VMEM)
    res = pl.pallas_call(
        body, name=name,
        out_shape=tuple(jax.ShapeDtypeStruct(s.shape, BF16) for s in shards)
        + tuple(jax.ShapeDtypeStruct((N_DEV * s.shape[0], s.shape[1]), BF16) for s in shards),
        in_specs=[vmem] * n + [ANY_SPEC] * len(deps), out_specs=tuple([vmem] * n + [ANY_SPEC] * n),
        scratch_shapes=[pltpu.SemaphoreType.DMA((n,))], compiler_params=_params(),
    )(*shards, *deps)
    return res[:n], res[n:]


def _copies_start(arrays, plan, n, *, name):
    na = len(arrays)

    def body(*refs):
        ins, send_sems, recv_sems = refs[:na], refs[na], refs[na + 1]
        me = _position()
        for j in range(n):
            src, dst, peer, _ = plan(ins, me, j)
            pltpu.make_async_remote_copy(src_ref=src, dst_ref=dst, send_sem=send_sems.at[j], recv_sem=recv_sems.at[j],
                                         device_id=peer, device_id_type=MESH).start()

    return pl.pallas_call(
        body, name=name,
        out_shape=(pltpu.SemaphoreType.DMA((n,)), pltpu.SemaphoreType.DMA((n,)))
        + tuple(pltpu.HBM(a.shape, a.dtype) for a in arrays),
        in_specs=(HBM_SPEC,) * na, out_specs=(SEM_SPEC, SEM_SPEC) + (HBM_SPEC,) * na,
        input_output_aliases={i: 2 + i for i in range(na)},
        compiler_params=pltpu.CompilerParams(has_side_effects=DATAFLOW),
    )(*[_in_hbm(a) for a in arrays])


def _copies_wait(send_sems, recv_sems, arrays, plan, n, after, *, name):
    na = len(arrays)

    def body(*refs):
        ins, send_sems, recv_sems = refs[:na], refs[na], refs[na + 1]
        me = _position()
        for j in range(n):
            src, _, peer, landed = plan(ins, me, j)
            copy = pltpu.make_async_remote_copy(src_ref=src, dst_ref=landed, send_sem=send_sems.at[j],
                                                recv_sem=recv_sems.at[j], device_id=peer, device_id_type=MESH)
            copy.wait_send()
            copy.wait_recv()

    return pl.pallas_call(
        body, name=name, out_shape=tuple(pltpu.HBM(a.shape, a.dtype) for a in arrays),
        in_specs=(HBM_SPEC,) * na + (SEM_SPEC, SEM_SPEC, ANY_SPEC), out_specs=(HBM_SPEC,) * na,
        input_output_aliases={i: i for i in range(na)},
        compiler_params=pltpu.CompilerParams(has_side_effects=DATAFLOW),
    )(*arrays, send_sems, recv_sems, after)


SAME_CORE = (2, 4, 6)


def _gather_two_level(shard, land):
    r = shard.shape[0]
    rows = lambda ref, pos: ref.at[pl.ds(_linear(pos) * r, r), :]
    first_peers = (1,) + SAME_CORE

    def first(refs, me, j):
        peer = _peer(me, first_peers[j])
        return refs[0], rows(refs[1], me), peer, rows(refs[1], peer)

    def second(refs, me, j):
        sibling = _peer(me, 1)
        mine = rows(refs[0], _peer(me, SAME_CORE[j]))
        return mine, mine, sibling, rows(refs[0], _peer(sibling, SAME_CORE[j]))

    send1, recv1, shard1, land1 = _copies_start([shard, land], first, 4, name="w_in_send")

    def finish(after):
        _, land2 = _copies_wait(send1, recv1, [shard1, land1], first, 4, after, name="w_in_recv")
        send2, recv2, land3 = _copies_start([land2], second, 3, name="w_in_pass")
        return _copies_wait(send2, recv2, [land3], second, 3, after, name="w_in_pass_recv")[0]

    return shard1, finish


def _gather_start(shards, lands):
    n = len(shards)
    rows = [s.shape[0] for s in shards]

    def body(*refs):
        srcs, land = refs[:n], refs[n:2 * n]
        send_sems, recv_sems = refs[2 * n:3 * n], refs[3 * n:4 * n]
        me = _position()
        for a in range(n):
            mine = land[a].at[pl.ds(_linear(me) * rows[a], rows[a]), :]
            for k in SEND_ORDER:
                pltpu.make_async_remote_copy(
                    src_ref=srcs[a], dst_ref=mine, send_sem=send_sems[a].at[k - 1], recv_sem=recv_sems[a].at[k - 1],
                    device_id=_peer(me, k), device_id_type=MESH).start()

    sems = tuple(pltpu.SemaphoreType.DMA((N_DEV - 1,)) for _ in range(2 * n))
    res = pl.pallas_call(
        body, name="weights_send",
        out_shape=sems + tuple(pltpu.HBM(s.shape, s.dtype) for s in shards)
        + tuple(pltpu.HBM(l.shape, l.dtype) for l in lands),
        in_specs=(HBM_SPEC,) * (2 * n), out_specs=(SEM_SPEC,) * (2 * n) + (HBM_SPEC,) * (2 * n),
        input_output_aliases={i: 2 * n + i for i in range(2 * n)},
        compiler_params=pltpu.CompilerParams(has_side_effects=DATAFLOW),
    )(*[_in_hbm(s) for s in shards], *[_in_hbm(l) for l in lands])
    return [(res[a], res[n + a], res[2 * n + a], res[3 * n + a]) for a in range(n)]


def _gather_wait(send_sems, recv_sems, shard_thru, land_thru, after, *, name):
    r = shard_thru.shape[0]

    def body(src_ref, land_ref, send_sems, recv_sems, after_ref, src_dead, got_ref):
        del after_ref, src_dead, got_ref
        me = _position()
        for k in SEND_ORDER:
            peer = _peer(me, k)
            copy = pltpu.make_async_remote_copy(
                src_ref=src_ref, dst_ref=land_ref.at[pl.ds(_linear(peer) * r, r), :],
                send_sem=send_sems.at[k - 1], recv_sem=recv_sems.at[k - 1],
                device_id=peer, device_id_type=MESH)
            copy.wait_send()
            copy.wait_recv()

    return pl.pallas_call(
        body, name=name,
        out_shape=(pltpu.HBM(shard_thru.shape, shard_thru.dtype), pltpu.HBM(land_thru.shape, land_thru.dtype)),
        in_specs=(HBM_SPEC, HBM_SPEC, SEM_SPEC, SEM_SPEC, ANY_SPEC),
        out_specs=(HBM_SPEC, HBM_SPEC), input_output_aliases={0: 0, 1: 1},
        compiler_params=pltpu.CompilerParams(has_side_effects=DATAFLOW),
    )(shard_thru, land_thru, send_sems, recv_sems, after)[1]


def _exchange_start(gs, *, name):
    n = len(gs)
    rows = [g.shape[0] // N_DEV for g in gs]
    lands = [lax.empty((N_DEV - 1, r, g.shape[1]), g.dtype) for g, r in zip(gs, rows)]

    def body(*refs):
        g_refs, land_refs = refs[:n], refs[n:2 * n]
        send_sems, recv_sems = refs[2 * n:3 * n], refs[3 * n:4 * n]
        me = _position()
        for a in range(n):
            for k in SEND_ORDER:
                peer = _peer(me, k)
                pltpu.make_async_remote_copy(
                    src_ref=g_refs[a].at[pl.ds(_linear(peer) * rows[a], rows[a]), :],
                    dst_ref=land_refs[a].at[k - 1],
                    send_sem=send_sems[a].at[k - 1], recv_sem=recv_sems[a].at[k - 1],
                    device_id=peer, device_id_type=MESH).start()

    res = pl.pallas_call(
        body, name=name,
        out_shape=tuple(pltpu.SemaphoreType.DMA((N_DEV - 1,)) for _ in range(2 * n))
        + tuple(pltpu.HBM(a.shape, a.dtype) for a in gs + lands),
        in_specs=(HBM_SPEC,) * (2 * n), out_specs=(SEM_SPEC,) * (2 * n) + (HBM_SPEC,) * (2 * n),
        input_output_aliases={i: 2 * n + i for i in range(2 * n)},
        compiler_params=pltpu.CompilerParams(has_side_effects=DATAFLOW),
    )(*[_in_hbm(a) for a in gs + lands])
    return [(res[a], res[n + a], res[2 * n + a], res[3 * n + a]) for a in range(n)]


def _exchange_wait(send_sems, recv_sems, g_thru, land_thru, after, *, name):
    r = land_thru.shape[1]

    def body(g_ref, land_ref, send_sems, recv_sems, after_ref, g_dead, got_ref):
        del after_ref, g_dead, got_ref
        me = _position()
        for k in SEND_ORDER:
            peer = _peer(me, k)
            copy = pltpu.make_async_remote_copy(
                src_ref=g_ref.at[pl.ds(_linear(peer) * r, r), :], dst_ref=land_ref.at[k - 1],
                send_sem=send_sems.at[k - 1], recv_sem=recv_sems.at[k - 1],
                device_id=peer, device_id_type=MESH)
            copy.wait_send()
            copy.wait_recv()

    return pl.pallas_call(
        body, name=name,
        out_shape=(pltpu.HBM(g_thru.shape, g_thru.dtype), pltpu.HBM(land_thru.shape, land_thru.dtype)),
        in_specs=(HBM_SPEC, HBM_SPEC, SEM_SPEC, SEM_SPEC, pl.BlockSpec(memory_space=pl.ANY)),
        out_specs=(HBM_SPEC, HBM_SPEC), input_output_aliases={0: 0, 1: 1},
        compiler_params=pltpu.CompilerParams(has_side_effects=DATAFLOW),
    )(g_thru, land_thru, send_sems, recv_sems, after)


def _adamw_math(w, g, m, v):
    m = B1 * m + (1.0 - B1) * g
    v = B2 * v + (1.0 - B2) * (g * g)
    delta = -LR * ((m / C1) / (jnp.sqrt(v / C2) + AEPS) + WD * w)
    return delta, m, v


def _sum_adamw(items, *, name):
    n = len(items)

    def body(*refs):
        ins, outs, scratch = refs[:5 * n], refs[5 * n:9 * n], refs[9 * n:]
        me_lin = _linear(_position())
        mine = []
        for a in range(n):
            r = items[a][2].shape[0]
            mine.append(pltpu.make_async_copy(ins[5 * a].at[pl.ds(me_lin * r, r), :], scratch[a], scratch[n].at[a]))
            mine[-1].start()
        for a in range(n):
            _, land_ref, w_ref, m_ref, v_ref = ins[5 * a:5 * a + 5]
            g_ref, d_ref, nm_ref, nv_ref = outs[4 * a:4 * a + 4]
            g = land_ref[0].astype(F32)
            for s in range(1, N_DEV - 1):
                g = g + land_ref[s].astype(F32)
            mine[a].wait()
            g = scratch[a][...].astype(F32) + g
            g_ref[...] = g
            d_ref[...], nm_ref[...], nv_ref[...] = _adamw_math(w_ref[...], g, m_ref[...], v_ref[...])

    vmem = pl.BlockSpec(memory_space=pltpu.VMEM)
    res = pl.pallas_call(
        body, name=name,
        out_shape=tuple(jax.ShapeDtypeStruct(it[2].shape, F32) for it in items for _ in range(4)),
        in_specs=[ANY_SPEC, vmem, vmem, vmem, vmem] * n, out_specs=(vmem,) * (4 * n),
        scratch_shapes=[pltpu.VMEM(it[2].shape, BF16) for it in items] + [pltpu.SemaphoreType.DMA((n,))],
        compiler_params=_params(),
    )(*[a for it in items for a in it])
    return [res[4 * a:4 * a + 4] for a in range(n)]


SMALL = ("g_mix_pre", "g_mix_post", "g_mem", "g_x_pre", "g_x_post", "g_ffn_pre", "g_ffn_post",
         "hgrn_onorm", "hgrn_lb", "sinks")
SMALL_W = dict(hgrn_onorm=HD, hgrn_lb=HG_W, sinks=8)
SQ_ROW = len(SMALL)
PACK_ROWS = 16


def _small_allreduce(parts, dep):
    ns = len(SMALL)

    def body(*refs):
        part, tot_ref = refs[:ns + 1], refs[ns + 2]
        gath, send_sems, recv_sems = refs[ns + 3:]
        me = _position()
        mine = gath.at[_linear(me)]
        mine[...] = jnp.zeros((PACK_ROWS, D), F32)
        for r, name in enumerate(SMALL):
            wd = SMALL_W.get(name, D)
            mine[r:r + 1, 0:wd] = jnp.sum(part[r][...], axis=0, keepdims=True)[:, 0:wd]
        sq = jnp.sum(part[ns][...]) * (0.5 / D)
        mine[SQ_ROW:SQ_ROW + 1, :] = jnp.full((1, D), sq, F32)

        def copy(k):
            peer = _peer(me, k)
            return pltpu.make_async_remote_copy(
                src_ref=mine, dst_ref=mine, send_sem=send_sems.at[k - 1], recv_sem=recv_sems.at[k - 1],
                device_id=peer, device_id_type=MESH)

        def arrival(k):
            slot = gath.at[_linear(_peer(me, k))]
            return pltpu.make_async_remote_copy(
                src_ref=slot, dst_ref=slot, send_sem=send_sems.at[k - 1], recv_sem=recv_sems.at[k - 1],
                device_id=_peer(me, k), device_id_type=MESH)

        sent = [copy(k) for k in range(1, 8)]
        for cp in sent:
            cp.start()
        for k in range(1, 8):
            arrival(k).wait_recv()
        for cp in sent:
            cp.wait_send()
        tot = gath[0]
        for s in range(1, N_DEV):
            tot = tot + gath[s]
        tot_ref[...] = tot

    vmem = pl.BlockSpec(memory_space=pltpu.VMEM)
    return pl.pallas_call(
        body, name="small_allreduce", out_shape=jax.ShapeDtypeStruct((PACK_ROWS, D), F32),
        in_specs=[vmem] * (ns + 1) + [ANY_SPEC], out_specs=vmem,
        scratch_shapes=[pltpu.VMEM((N_DEV, PACK_ROWS, D), F32), pltpu.SemaphoreType.DMA((7,)),
                        pltpu.SemaphoreType.DMA((7,))],
        compiler_params=_params(has_side_effects=True),
    )(*[parts[n] for n in SMALL], parts["sq"], dep)


def _small_update(tot, sm, m_sm, v_sm):
    ns = len(SMALL)

    def body(*refs):
        tot = refs[0][...]
        w_refs, m_refs, v_refs = refs[1:ns + 1], refs[ns + 1:2 * ns + 1], refs[2 * ns + 1:3 * ns + 1]
        outs = refs[3 * ns + 1:]
        loss_ref = outs[0]
        g_out, d_out = outs[1:ns + 1], outs[ns + 1:2 * ns + 1]
        nm_out, nv_out = outs[2 * ns + 1:3 * ns + 1], outs[3 * ns + 1:4 * ns + 1]
        loss_ref[...] = tot[SQ_ROW:SQ_ROW + 1, 0:1]
        for r, name in enumerate(SMALL):
            wd = SMALL_W.get(name, D)
            g = tot[r:r + 1, 0:wd]
            w = w_refs[r][...]
            if name == "hgrn_lb":
                mx = jnp.maximum(w[0:1], w[1:2])
                e0, e1 = jnp.exp(w[0:1] - mx), jnp.exp(w[1:2] - mx)
                lb0 = e0 / (e0 + e1)
                g0 = g * lb0 * (1.0 - lb0)
                for i, gi in enumerate((g0, -g0)):
                    d, nm, nv = _adamw_math(w[i:i + 1], gi, m_refs[r][i:i + 1, :], v_refs[r][i:i + 1, :])
                    g_out[r][i:i + 1, :] = gi
                    d_out[r][i:i + 1, :], nm_out[r][i:i + 1, :], nv_out[r][i:i + 1, :] = d, nm, nv
            else:
                d, nm, nv = _adamw_math(w, g, m_refs[r][...], v_refs[r][...])
                g_out[r][...] = g
                d_out[r][...], nm_out[r][...], nv_out[r][...] = d, nm, nv

    shapes = [jax.ShapeDtypeStruct(sm[n].shape, F32) for n in SMALL]
    res = pl.pallas_call(
        body, name="small_update", out_shape=tuple([jax.ShapeDtypeStruct((1, 1), F32)] + shapes * 4),
        compiler_params=_params(),
    )(tot, *[sm[n] for n in SMALL], *[m_sm[n] for n in SMALL], *[v_sm[n] for n in SMALL])
    groups = [dict(zip(SMALL, res[1 + i * ns:1 + (i + 1) * ns])) for i in range(4)]
    return res[0], groups[0], groups[1], groups[2], groups[3]


BIG = ("w_in", "w_gate", "w_up", "w_down", "w_out", "wq_x", "wk_x", "wv_x", "wo_x")
BIG_KEY = dict(w_in="winT", w_gate="wgT", w_up="wuT", w_down="wd", w_out="wout", wq_x="wq", wk_x="wk",
               wv_x="wv", wo_x="wo")
TRANSPOSED = ("w_in", "w_gate", "w_up")
WEIGHTS = ("w_in", "sinks", "hgrn_lb", "hgrn_onorm", "w_out", "g_mix_pre", "g_mix_post", "g_mem", "g_x_pre",
           "g_x_post", "wq_x", "wk_x", "wv_x", "wo_x", "g_ffn_pre", "g_ffn_post", "w_gate", "w_up", "w_down")


def kernel(x, mem, w_in, sinks, hgrn_lb, hgrn_onorm, w_out, g_mix_pre, g_mix_post, g_mem, g_x_pre, g_x_post, wq_x, wk_x, wv_x, wo_x, g_ffn_pre, g_ffn_post, w_gate, w_up, w_down, loss_target, m_w_in, m_sinks, m_hgrn_lb, m_hgrn_onorm, m_w_out, m_g_mix_pre, m_g_mix_post, m_g_mem, m_g_x_pre, m_g_x_post, m_wq_x, m_wk_x, m_wv_x, m_wo_x, m_g_ffn_pre, m_g_ffn_post, m_w_gate, m_w_up, m_w_down, v_w_in, v_sinks, v_hgrn_lb, v_hgrn_onorm, v_w_out, v_g_mix_pre, v_g_mix_post, v_g_mem, v_g_x_pre, v_g_x_post, v_wq_x, v_wk_x, v_wv_x, v_wo_x, v_g_ffn_pre, v_g_ffn_post, v_w_gate, v_w_up, v_w_down):
    given = dict(locals())
    wts = {n: given[n] for n in WEIGHTS}
    ms = {n: given["m_" + n] for n in WEIGHTS}
    vs = {n: given["v_" + n] for n in WEIGHTS}

    def mat(a, name):
        a = a[0]
        return a.T if name in TRANSPOSED else a

    shard_in, land_in = _prepare_weights([mat(w_in, "w_in")], name="prepare_w_in")
    sent_in, finish_w_in = _gather_two_level(shard_in[0], land_in[0])
    order = ("w_out", "wq_x", "wk_x", "wv_x", "wo_x", "w_gate", "w_up", "w_down")
    flying = dict(zip(order, _gather_start(*_prepare_weights([mat(wts[n], n) for n in order], name="prepare_weights",
                                                             dep=sent_in))))
    name_of = {k: n for n, k in BIG_KEY.items()}

    def fetch(key, after):
        if key == "winT":
            return finish_w_in(after)
        return _gather_wait(*flying[name_of[key]], after, name="weights_recv_" + name_of[key])

    sm = {n: wts[n] for n in SMALL}
    started, held = {}, {}
    send_with = {k: group for group in (("wgT", "wuT"), ("wo", "wq", "wk", "wv")) for k in group}

    def emit(key, g):
        held[key] = g
        group = send_with.get(key, (key,))
        if key != group[-1]:
            return None
        flights = _exchange_start([held[k] for k in group], name="grad_send_" + name_of[group[0]])
        started.update({name_of[k]: f for k, f in zip(group, flights)})
        return flights[-1][2]

    grad_x, _, parts = _local_step(x[0], mem[0], loss_target[0], fetch, sm, emit, first_dep=flying["w_out"][2])
    grads, deltas, new_m, new_v = {}, {}, {}, {}
    after = grad_x
    for group in (("w_down",), ("w_gate",), ("w_up",), ("wo_x", "wq_x", "wk_x", "wv_x", "w_out"), ("w_in",)):
        items = []
        for n in group:
            g_all, land = _exchange_wait(*started[n], after, name="grad_recv_" + n)
            items.append((g_all, land, mat(wts[n], n), mat(ms[n], n), mat(vs[n], n)))
            after = land
        for n, res in zip(group, _sum_adamw(items, name="adamw_" + group[0])):
            after = res[1]
            if n in TRANSPOSED:
                res = [a.T for a in res]
            grads[n], deltas[n], new_m[n], new_v[n] = [a[None] for a in res]
    loss, g_s, d_s, m_s, v_s = _small_update(_small_allreduce(parts, after), sm, {n: ms[n] for n in SMALL},
                                             {n: vs[n] for n in SMALL})
    grads.update(g_s), deltas.update(d_s), new_m.update(m_s), new_v.update(v_s)
    return (loss[0, 0], grad_x[None], *[grads[n] for n in WEIGHTS], *[deltas[n] for n in WEIGHTS],
            *[new_m[n] for n in WEIGHTS], *[new_v[n] for n in WEIGHTS])
```

```python
import functools

import jax
import jax.numpy as jnp
from jax import lax
from jax.experimental import pallas as pl
from jax.experimental.pallas import tpu as pltpu

F32 = jnp.float32
BF16 = jnp.bfloat16

D = 1024
D_IN = 2816
D_FF = 2816
CHUNK = 64
SWA_W = 512
KV_W = 128
HG_W = 512
HD = 128
ZQH, ZFH, ZIH, ZGH = 768, 1280, 1792, 2304
XH, XD = 4, 256
EPS = 1e-6
NEG = -1e30
N_DEV = 8
MESH = pl.DeviceIdType.MESH

LR, B1, B2, AEPS, WD, STEP = 0.001, 0.9, 0.999, 1e-08, 0.01, 10
C1 = 1.0 - B1 ** STEP
C2 = 1.0 - B2 ** STEP

VMEM_LIMIT = 56 * 1024 * 1024


def _params(**kw):
    return pltpu.CompilerParams(vmem_limit_bytes=VMEM_LIMIT, **kw)


def _sig(x):
    return 1.0 / (1.0 + jnp.exp(-x))


def _rowsum8(x):
    r, w = x.shape
    return jnp.sum(x.reshape(r // 8, 8, w), axis=0)


def _dot(a, b, ca, cb, precision=None):
    return lax.dot_general(a, b, (((ca,), (cb,)), ((), ())), preferred_element_type=F32,
                           precision=precision)


ANY_SPEC = pl.BlockSpec(memory_space=pl.ANY)


def _mm(a, b, *, ta=False, tb=False, out_dtype, tm, tn, tk=None, name, dep=None, n_outer=False):
    m = a.shape[1] if ta else a.shape[0]
    k = a.shape[0] if ta else a.shape[1]
    n = b.shape[0] if tb else b.shape[1]
    tm, tn = min(tm, m), min(tn, n)
    tk = k if tk is None else min(tk, k)
    nk = k // tk
    assert m % tm == 0 and n % tn == 0 and k % tk == 0, (name, m, n, k, tm, tn, tk)
    ij = (lambda g0, g1: (g1, g0)) if n_outer else (lambda g0, g1: (g0, g1))
    a_spec = (pl.BlockSpec((tk, tm), lambda g0, g1, kk: (kk, ij(g0, g1)[0])) if ta
              else pl.BlockSpec((tm, tk), lambda g0, g1, kk: (ij(g0, g1)[0], kk)))
    b_spec = (pl.BlockSpec((tn, tk), lambda g0, g1, kk: (ij(g0, g1)[1], kk)) if tb
              else pl.BlockSpec((tk, tn), lambda g0, g1, kk: (kk, ij(g0, g1)[1])))
    ca, cb = (0 if ta else 1), (1 if tb else 0)

    deps = [] if dep is None else [dep]

    def body(a_ref, b_ref, *rest):
        o_ref, acc = rest[len(deps)], rest[len(deps) + 1:]
        p = _dot(a_ref[...].astype(BF16), b_ref[...].astype(BF16), ca, cb)
        if nk == 1:
            o_ref[...] = p.astype(out_dtype)
        else:
            acc_ref, = acc
            kk = pl.program_id(2)

            @pl.when(kk == 0)
            def _():
                acc_ref[...] = p

            @pl.when(kk > 0)
            def _():
                acc_ref[...] += p

            @pl.when(kk == nk - 1)
            def _():
                o_ref[...] = acc_ref[...].astype(out_dtype)

    return pl.pallas_call(
        body, name=name, out_shape=jax.ShapeDtypeStruct((m, n), out_dtype),
        grid=(n // tn, m // tm, nk) if n_outer else (m // tm, n // tn, nk),
        in_specs=[a_spec, b_spec] + [ANY_SPEC] * len(deps),
        out_specs=pl.BlockSpec((tm, tn), lambda g0, g1, kk: ij(g0, g1)),
        scratch_shapes=[pltpu.VMEM((tm, tn), F32)] if nk > 1 else [],
        compiler_params=_params(dimension_semantics=("parallel", "parallel", "arbitrary")),
    )(a, b, *deps)


def _mm2(a1, b1, a2, b2, *, tb=False, out_dtype, tm, name, dep=None):
    m, k = a1.shape
    n = b1.shape[0] if tb else b1.shape[1]
    tm = min(tm, m)
    assert m % tm == 0
    cb = 1 if tb else 0
    deps = [] if dep is None else [dep]

    def body(a1_ref, b1_ref, a2_ref, b2_ref, *rest):
        o_ref = rest[len(deps)]
        o_ref[...] = (_dot(a1_ref[...].astype(BF16), b1_ref[...], 1, cb)
                      + _dot(a2_ref[...].astype(BF16), b2_ref[...], 1, cb)).astype(out_dtype)

    a_spec = pl.BlockSpec((tm, k), lambda i: (i, 0))
    b_spec = pl.BlockSpec(b1.shape, lambda i: (0, 0))
    return pl.pallas_call(
        body, name=name, out_shape=jax.ShapeDtypeStruct((m, n), out_dtype),
        grid=(m // tm,), in_specs=[a_spec, b_spec, a_spec, b_spec] + [ANY_SPEC] * len(deps),
        out_specs=pl.BlockSpec((tm, n), lambda i: (i, 0)),
        compiler_params=_params(dimension_semantics=("parallel",)),
    )(a1, b1, a2, b2, *deps)


def _mm_rows(prods, rows_in, vecs_in, epilogue, outs, *, tm, name, dep=None):
    m = prods[0][0].shape[0]
    n = prods[0][1].shape[0] if prods[0][2] else prods[0][1].shape[1]
    tm = min(tm, m)
    assert m % tm == 0
    deps = [] if dep is None else [dep]
    n_p, n_r, n_v = len(prods), len(rows_in), len(vecs_in)

    def body(*refs):
        ab = refs[:2 * n_p]
        row_refs = refs[2 * n_p:2 * n_p + n_r]
        vec_refs = refs[2 * n_p + n_r:2 * n_p + n_r + n_v]
        out_refs = refs[2 * n_p + n_r + n_v + len(deps):]
        p = None
        for j, (_, _, tb) in enumerate(prods):
            t = _dot(ab[2 * j][...].astype(BF16), ab[2 * j + 1][...], 1, 1 if tb else 0)
            p = t if p is None else p + t
        vals = epilogue(p, *[r[...] for r in row_refs], *[v[...] for v in vec_refs])
        for (dtype, kind), o_ref, val in zip(outs, out_refs, vals):
            if kind == "row":
                o_ref[...] = val.astype(dtype)
            else:
                @pl.when(pl.program_id(0) == 0)
                def _(o_ref=o_ref):
                    o_ref[...] = jnp.zeros_like(o_ref)

                o_ref[...] += val

    row = lambda w: pl.BlockSpec((tm, w), lambda i: (i, 0))
    whole = lambda a: pl.BlockSpec(a.shape, lambda i: (0,) * a.ndim, pipeline_mode=pl.Buffered(1))
    in_specs, args = [], []
    for a, b, _ in prods:
        in_specs += [row(a.shape[1]), whole(b)]
        args += [a, b]
    in_specs += [row(r.shape[1]) for r in rows_in] + [whole(v) for v in vecs_in] + [ANY_SPEC] * len(deps)
    return pl.pallas_call(
        body, name=name,
        out_shape=tuple(jax.ShapeDtypeStruct((m, n) if kind == "row" else (8, n), dtype) for dtype, kind in outs),
        grid=(m // tm,), in_specs=in_specs,
        out_specs=tuple(row(n) if kind == "row" else pl.BlockSpec((8, n), lambda i: (0, 0)) for _, kind in outs),
        compiler_params=_params(dimension_semantics=("arbitrary",)),
    )(*args, *rows_in, *vecs_in, *deps)


def _rstd(x):
    return lax.rsqrt(jnp.mean(x * x, axis=-1, keepdims=True) + EPS)


def _norm_bwd(xh, r, t):
    return r * (t - xh * jnp.mean(xh * t, axis=-1, keepdims=True))


ROW_F32, ROW_BF16, SUM_F32 = (F32, "row"), (BF16, "row"), (F32, "sum")


def _ep_post_pre(p, h, g_post, g_pre):
    y = p.astype(BF16)
    yf = y.astype(F32)
    hn = h + yf * _rstd(yf) * g_post
    return y, hn, hn * _rstd(hn) * g_pre


_EP_POST_PRE_OUTS = [ROW_BF16, ROW_F32, ROW_BF16]


def _ep_final_loss(y, h, target, g_post):
    r = _rstd(y)
    yh = y * r
    err = h + yh * g_post - target
    dh = err * (1.0 / D)
    return _rowsum8(err * err), dh, _norm_bwd(yh, r, dh * g_post), _rowsum8(dh * yh)


_EP_FINAL_LOSS_OUTS = [SUM_F32, ROW_F32, ROW_BF16, SUM_F32]


def _ep_post_pre_bwd(du, dh_out, hn, y, g_post, g_pre):
    r2 = _rstd(hn)
    xh = hn * r2
    dh = dh_out + _norm_bwd(xh, r2, du * g_pre)
    yf = y.astype(F32)
    r1 = _rstd(yf)
    yh = yf * r1
    return dh, _norm_bwd(yh, r1, dh * g_post), _rowsum8(du * xh), _rowsum8(dh * yh)


_EP_POST_PRE_BWD_OUTS = [ROW_F32, ROW_BF16, SUM_F32, SUM_F32]


def _ep_pre_bwd(du, dh_out, x, g):
    r = _rstd(x)
    xh = x * r
    return dh_out + _norm_bwd(xh, r, du * g), _rowsum8(du * xh)


_EP_PRE_BWD_OUTS = [ROW_F32, SUM_F32]


def _prenorm(x, g, *, name, dep=None):
    t, d = x.shape
    tb = min(512, t)
    deps = [] if dep is None else [dep]

    def body(x_ref, g_ref, *rest):
        xf = x_ref[...]
        rest[-1][...] = (xf * _rstd(xf) * g_ref[...]).astype(BF16)

    return pl.pallas_call(
        body, name=name, out_shape=jax.ShapeDtypeStruct((t, d), BF16), grid=(t // tb,),
        in_specs=[pl.BlockSpec((tb, d), lambda i: (i, 0)), pl.BlockSpec((1, d), lambda i: (0, 0))]
        + [ANY_SPEC] * len(deps),
        out_specs=pl.BlockSpec((tb, d), lambda i: (i, 0)), compiler_params=_params(),
    )(x, g, *deps)


QB = 256


def _half_mask(shape, e):
    lane = lax.broadcasted_iota(jnp.int32, shape, len(shape) - 1)
    return (lane // 64) == e


def _place(kv):
    sw = pltpu.roll(kv, 64, 1)
    m0 = _half_mask(kv.shape, 0)
    return [[jnp.where(m0, kv, 0.0).astype(BF16), jnp.where(m0, 0.0, sw).astype(BF16)],
            [jnp.where(m0, sw, 0.0).astype(BF16), jnp.where(m0, 0.0, kv).astype(BF16)]]


SQ = 128
SK = 256


def _swa_valid(i, sb):
    qc = lax.broadcasted_iota(jnp.int32, (SQ, SK), 0) // CHUNK
    kc = lax.broadcasted_iota(jnp.int32, (SQ, SK), 1) // CHUNK - 2
    return (kc <= qc) & (qc <= kc + 2) & (4 * i + 2 * sb + kc >= 0)


def _swa_fwd(z, sinks, t, dep=None):
    nb = t // QB
    deps = [] if dep is None else [dep]

    def body(s_ref, q_ref, kp_ref, kc_ref, vp_ref, vc_ref, *rest):
        o_ref, lse_ref = rest[-2:]
        i = pl.program_id(0)
        kpl = _place(jnp.concatenate([kp_ref[...], kc_ref[...]], axis=0))
        vpl = _place(jnp.concatenate([vp_ref[...], vc_ref[...]], axis=0))
        lane = lax.broadcasted_iota(jnp.int32, (SQ, 128), 1)
        for sb in range(QB // SQ):
            rows, keys = slice(SQ * sb, SQ * (sb + 1)), slice(SQ * sb, SQ * sb + SK)
            valid = _swa_valid(i, sb)
            lse_out = jnp.zeros((SQ, 128), F32)
            for j in range(4):
                qp = q_ref[rows, 128 * j:128 * (j + 1)].astype(BF16)
                acc = jnp.zeros((SQ, 128), F32)
                for e in range(2):
                    h = 2 * j + e
                    kvh = h // 4
                    qm = jnp.where(_half_mask(qp.shape, e), qp, jnp.zeros_like(qp))
                    s = _dot(qm, kpl[kvh][e][keys], 1, 1) * 0.125
                    s = jnp.where(valid, s, NEG)
                    sink = s_ref[0, h]
                    m = jnp.maximum(jnp.max(s, axis=-1, keepdims=True), sink)
                    p = jnp.exp(s - m)
                    l = jnp.sum(p, axis=-1, keepdims=True) + jnp.exp(sink - m)
                    acc = acc + _dot(p.astype(BF16), vpl[kvh][e][keys], 1, 0) * (1.0 / l)
                    lse_out = jnp.where(lane == h, m + jnp.log(l), lse_out)
                o_ref[rows, 128 * j:128 * (j + 1)] = acc.astype(BF16)
            lse_ref[rows, :] = lse_out

    prev = lambda c: pl.BlockSpec((128, 128), lambda i: (jnp.maximum(2 * i - 1, 0), c))
    cur = lambda c: pl.BlockSpec((QB, 128), lambda i: (i, c))
    return pl.pallas_call(
        body, name="swa_fwd",
        out_shape=(jax.ShapeDtypeStruct((t, D), BF16), jax.ShapeDtypeStruct((t, 128), F32)),
        grid=(nb,),
        in_specs=[pl.BlockSpec(memory_space=pltpu.SMEM),
                  pl.BlockSpec((QB, SWA_W), lambda i: (i, 0)), prev(4), cur(4), prev(5), cur(5)]
        + [ANY_SPEC] * len(deps),
        out_specs=(pl.BlockSpec((QB, SWA_W), lambda i: (i, 0)), pl.BlockSpec((QB, 128), lambda i: (i, 0))),
        compiler_params=_params(),
    )(sinks, z, z, z, z, z, *deps)


def _swa_bwd(z, sinks, ymix, lse, dymix, t):
    nb = t // QB

    def body(s_ref, q_ref, kp_ref, kc_ref, vp_ref, vc_ref, o_ref, do_ref, l_ref,
             dq_ref, first_ref, second_ref, ds_ref, carry_ref):
        i = pl.program_id(0)
        live = i < nb

        @pl.when(i == 0)
        def _():
            ds_ref[...] = jnp.zeros_like(ds_ref)
            carry_ref[...] = jnp.zeros_like(carry_ref)

        lane = lax.broadcasted_iota(jnp.int32, (8, 128), 1)
        kpl = _place(jnp.concatenate([kp_ref[...], kc_ref[...]], axis=0))
        vpl = _place(jnp.concatenate([vp_ref[...], vc_ref[...]], axis=0))
        nk = QB + 128
        qc = lax.broadcasted_iota(jnp.int32, (QB, nk), 0) // CHUNK
        kc = lax.broadcasted_iota(jnp.int32, (QB, nk), 1) // CHUNK - 2
        valid = (kc <= qc) & (qc <= kc + 2) & (4 * i + kc >= 0) & live
        lse_c = l_ref[...]
        dsink = jnp.zeros((8, 128), F32)
        dk_acc = [[jnp.zeros((nk, 128), F32) for _ in range(2)] for _ in range(2)]
        dv_acc = [[jnp.zeros((nk, 128), F32) for _ in range(2)] for _ in range(2)]
        dq = []
        for j in range(4):
            cols = slice(128 * j, 128 * (j + 1))
            qp = q_ref[:, cols].astype(BF16)
            dop = do_ref[:, cols]
            prod = dop.astype(F32) * o_ref[:, cols].astype(F32)
            acc = jnp.zeros((QB, 128), F32)
            for e in range(2):
                h = 2 * j + e
                kvh = h // 4
                hm = _half_mask(qp.shape, e)
                qm = jnp.where(hm, qp, jnp.zeros_like(qp))
                dom = jnp.where(hm, dop, jnp.zeros_like(dop))
                dd = jnp.sum(jnp.where(hm, prod, 0.0), axis=-1, keepdims=True)
                lse_h = lse_c[:, h:h + 1]
                s = _dot(qm, kpl[kvh][e], 1, 1) * 0.125
                p = jnp.where(valid, jnp.exp(s - lse_h), 0.0)
                dp = _dot(dom, vpl[kvh][e], 1, 1)
                ds = (p * (dp - dd) * 0.125).astype(BF16)
                acc = acc + _dot(ds, kpl[kvh][e], 1, 0)
                dk_acc[kvh][e] = dk_acc[kvh][e] + _dot(ds, qm, 0, 0)
                dv_acc[kvh][e] = dv_acc[kvh][e] + _dot(p.astype(BF16), dom, 0, 0)
                ps = jnp.where(live, jnp.exp(s_ref[0, h] - lse_h) * dd, 0.0)
                dsink = dsink - jnp.where(lane == h, _rowsum8(jnp.broadcast_to(ps, (QB, 128))), 0.0)
            dq.append(acc.astype(BF16))
        ds_ref[...] += dsink
        dk = dk_acc[0][0] + dk_acc[1][1] + pltpu.roll(dk_acc[0][1] + dk_acc[1][0], 64, 1)
        dv = dv_acc[0][0] + dv_acc[1][1] + pltpu.roll(dv_acc[0][1] + dv_acc[1][0], 64, 1)
        dkv = jnp.concatenate([dk, dv], axis=1)
        second_ref[...] = (carry_ref[...] + dkv[0:128]).astype(BF16)
        carry_ref[...] = dkv[256:384]

        @pl.when(live)
        def _():
            for j in range(4):
                dq_ref[:, 128 * j:128 * (j + 1)] = dq[j]
            first_ref[...] = dkv[128:256].astype(BF16)

    blk = lambda i: jnp.minimum(i, nb - 1)
    prev = lambda c: pl.BlockSpec((128, 128), lambda i: (jnp.maximum(2 * blk(i) - 1, 0), c))
    cur = lambda w, c: pl.BlockSpec((QB, w), lambda i: (blk(i), c))
    half = lambda index: pl.BlockSpec((128, 256), lambda i: (index(i), 0))
    return pl.pallas_call(
        body, name="swa_bwd",
        out_shape=(jax.ShapeDtypeStruct((t, SWA_W), BF16), jax.ShapeDtypeStruct((t // 2, 256), BF16),
                   jax.ShapeDtypeStruct((t // 2, 256), BF16), jax.ShapeDtypeStruct((8, 128), F32)),
        grid=(nb + 1,),
        in_specs=[pl.BlockSpec(memory_space=pltpu.SMEM),
                  cur(SWA_W, 0), prev(4), cur(128, 4), prev(5), cur(128, 5),
                  cur(SWA_W, 0), cur(SWA_W, 0), cur(128, 0)],
        out_specs=(cur(SWA_W, 0), half(blk), half(lambda i: jnp.maximum(i - 1, 0)),
                   pl.BlockSpec((8, 128), lambda i: (0, 0))),
        scratch_shapes=[pltpu.VMEM((128, 256), F32)],
        compiler_params=_params(dimension_semantics=("arbitrary",)),
    )(sinks, z, z, z, z, z, ymix, dymix, lse)


HB = 256


def _lower_bound(lb_ref):
    a = lb_ref[...]
    a0, a1 = a[0:1], a[1:2]
    mx = jnp.maximum(a0, a1)
    e0, e1 = jnp.exp(a0 - mx), jnp.exp(a1 - mx)
    return e0 / (e0 + e1)


def _hgrn_cols(row_block):
    return [pl.BlockSpec((HB, 2 * HD), lambda j, c=base // (2 * HD) + p: (row_block(j), c))
            for base in (ZQH, ZFH, ZIH, ZGH) for p in range(2)]


NCH = HB // CHUNK


def _split3(x):
    hi = x.astype(BF16)
    r1 = x - hi.astype(F32)
    mid = r1.astype(BF16)
    return hi, mid, (r1 - mid.astype(F32)).astype(BF16)


def _blockdiag(lower):
    r = lax.broadcasted_iota(jnp.int32, (HB, HB), 0)
    c = lax.broadcasted_iota(jnp.int32, (HB, HB), 1)
    return (r // CHUNK == c // CHUNK) & ((c <= r) if lower else (c >= r))


def _chunk_sums(mask_bf16, x):
    return sum(_dot(mask_bf16, part, 1, 0) for part in _split3(x))


def _per_chunk_rows(x, row):
    w = x.shape[1]
    picked = x.reshape(NCH, CHUNK, w)[:, row:row + 1, :]
    return jnp.broadcast_to(picked, (NCH, CHUNK, w)).reshape(HB, w)


def _chunk_stack(x, chunk_of_row):
    return jnp.concatenate([jnp.where(chunk_of_row == c, x, jnp.zeros_like(x)) for c in range(NCH)], axis=1)


def _chunk_pick(x, chunk_of_row):
    w = x.shape[1] // NCH
    out = jnp.zeros((HB, w), x.dtype)
    for c in range(NCH):
        out = jnp.where(chunk_of_row == c, x[:, c * w:(c + 1) * w], out)
    return out


def _hgrn_local(q, f, kf, b):
    sq = _sig(q)
    qf = q * sq * (HD ** -0.5)
    b_mid = _per_chunk_rows(b, CHUNK // 2 - 1)
    b_last = _per_chunk_rows(b, CHUNK - 1)
    qm = qf * jnp.exp(b - b_mid)
    km = kf * jnp.exp(b_mid - b)
    kl = kf * jnp.exp(b_last - b)
    qb = qf * jnp.exp(b)
    return dict(sq=sq, b_mid=b_mid, b_last=b_last, qm=qm, km=km, kl=kl, qb=qb)


def _hgrn2_fwd(z, hgrn_lb, onorm, ymix, t):
    nb = t // HB

    def body(*refs):
        zq, zf, zi, zg = refs[0:2], refs[2:4], refs[4:6], refs[6:8]
        lb_ref, on_ref, _, y_ref, o_ref, sp_ref, st_ref = refs[8:]

        @pl.when(pl.program_id(0) == 0)
        def _():
            st_ref[...] = jnp.zeros_like(st_ref)

        lb_all = _lower_bound(lb_ref)
        gn = on_ref[...]
        low = _blockdiag(True)
        low_b = low.astype(BF16)
        chunk_of_row = lax.broadcasted_iota(jnp.int32, (HB, HD), 0) // CHUNK
        for p in range(2):
            lbp = lb_all[:, 2 * HD * p:2 * HD * (p + 1)]
            fp = lbp + (1.0 - lbp) * _sig(zf[p][...])
            bp = _chunk_sums(low_b, jnp.log(fp))
            for e in range(2):
                h, ls = 2 * p + e, slice(e * HD, (e + 1) * HD)
                f = fp[:, ls]
                w = _hgrn_local(zq[p][:, ls], f, 1.0 - f, bp[:, ls])
                iv = zi[p][:, ls].astype(BF16)
                a = jnp.where(low, _dot(w["qm"].astype(BF16), w["km"].astype(BF16), 1, 1), 0.0)
                o = _dot(a.astype(BF16), iv, 1, 0)
                u = _dot(iv, _chunk_stack(w["kl"].astype(BF16), chunk_of_row), 0, 0)
                decay = jnp.exp(w["b_last"])
                st = st_ref[h]
                states = []
                for c in range(NCH):
                    sp_ref[h, c] = st
                    states.append(st.astype(BF16))
                    st = st * decay[c * CHUNK:c * CHUNK + 1] + u[:, c * HD:(c + 1) * HD]
                st_ref[h] = st
                inter = _dot(w["qb"].astype(BF16), jnp.concatenate(states, axis=0), 1, 1)
                o = o + _chunk_pick(inter, chunk_of_row)
                hs = slice(h * HD, (h + 1) * HD)
                o_ref[:, hs] = o
                gg = zg[p][:, ls]
                y_ref[:, hs] = (o * _rstd(o) * gn * (gg * _sig(gg))).astype(BF16)

    return pl.pallas_call(
        body, name="hgrn_fwd",
        out_shape=(jax.ShapeDtypeStruct((t, D), BF16), jax.ShapeDtypeStruct((t, HG_W), F32),
                   jax.ShapeDtypeStruct((4, t // CHUNK, HD, HD), F32)),
        grid=(nb,),
        in_specs=_hgrn_cols(lambda j: j) + [pl.BlockSpec((2, HG_W), lambda j: (0, 0)),
                                            pl.BlockSpec((1, HD), lambda j: (0, 0)), ANY_SPEC],
        out_specs=(pl.BlockSpec((HB, HG_W), lambda j: (j, 1)),
                   pl.BlockSpec((HB, HG_W), lambda j: (j, 0)),
                   pl.BlockSpec((4, NCH, HD, HD), lambda j: (0, j, 0, 0))),
        scratch_shapes=[pltpu.VMEM((4, HD, HD), F32)],
        input_output_aliases={10: 0},
        compiler_params=_params(dimension_semantics=("arbitrary",)),
    )(*[z] * 8, hgrn_lb, onorm, ymix)


def _hgrn2_bwd(z, hgrn_lb, onorm, o_save, sprev, dymix, dza, t):
    nb = t // HB

    def body(*refs):
        zq, zf, zi, zg = refs[0:2], refs[2:4], refs[4:6], refs[6:8]
        (lb_ref, on_ref, o_ref, sp_ref, dy_ref, dqa_ref, first_ref, second_ref,
         dz_ref, dlb_ref, don_ref, dst_ref) = refs[8:]

        @pl.when(pl.program_id(0) == 0)
        def _():
            dst_ref[...] = jnp.zeros_like(dst_ref)
            dlb_ref[...] = jnp.zeros_like(dlb_ref)
            don_ref[...] = jnp.zeros_like(don_ref)

        dz_ref[:, 0:SWA_W] = dqa_ref[...]
        dz_ref[0:HB // 2, SWA_W:ZQH] = first_ref[...]
        dz_ref[HB // 2:HB, SWA_W:ZQH] = second_ref[...]
        lb_all = _lower_bound(lb_ref)
        gn = on_ref[...]
        low, upp = _blockdiag(True), _blockdiag(False)
        upp_b = upp.astype(BF16)
        low_b = low.astype(BF16)
        row = lax.broadcasted_iota(jnp.int32, (HB, HD), 0)
        chunk_of_row = row // CHUNK
        in_chunk = row % CHUNK
        for p in range(2):
            lbp = lb_all[:, 2 * HD * p:2 * HD * (p + 1)]
            sgp = _sig(zf[p][...])
            fp = lbp + (1.0 - lbp) * sgp
            bp = _chunk_sums(low_b, jnp.log(fp))
            db_pair, dkf_pair = [], []
            for e in range(2):
                h, ls, hs = 2 * p + e, slice(e * HD, (e + 1) * HD), slice((2 * p + e) * HD, (2 * p + e + 1) * HD)
                f = fp[:, ls]
                q = zq[p][:, ls]
                w = _hgrn_local(q, f, 1.0 - f, bp[:, ls])
                iv = zi[p][:, ls].astype(BF16)
                gg = zg[p][:, ls]
                o = o_ref[:, hs]
                dout = dy_ref[:, hs].astype(F32)
                sgg = _sig(gg)
                r = _rstd(o)
                oh = o * r
                dyn = dout * (gg * sgg)
                dz_ref[:, ZGH + h * HD:ZGH + (h + 1) * HD] = (
                    dout * oh * gn * (sgg * (1.0 + gg * (1.0 - sgg)))).astype(BF16)
                don_ref[...] += _rowsum8(dyn * oh)
                do = _norm_bwd(oh, r, dyn * gn).astype(BF16)
                qm, km, kl, qb = (w[n].astype(BF16) for n in ("qm", "km", "kl", "qb"))
                decay = jnp.exp(w["b_last"])
                grads_in = _dot(do, _chunk_stack(qb, chunk_of_row), 0, 0)
                dst = dst_ref[h]
                dstn, dd_rows = [None] * NCH, [None] * NCH
                for c in reversed(range(NCH)):
                    dstn[c] = dst.astype(BF16)
                    dd_rows[c] = jnp.sum(dst * sp_ref[h, c], axis=0, keepdims=True)
                    dst = dst * decay[c * CHUNK:c * CHUNK + 1] + grads_in[:, c * HD:(c + 1) * HD]
                dst_ref[h] = dst
                states = jnp.concatenate([sp_ref[h, c].astype(BF16) for c in range(NCH)], axis=0)
                dstn_all = jnp.concatenate(dstn, axis=0)
                dqb = _dot(_chunk_stack(do, chunk_of_row), states, 1, 0)
                at = jnp.where(upp, _dot(km, qm, 1, 1), 0.0)
                di = _dot(at.astype(BF16), do, 1, 0) + _chunk_pick(_dot(kl, dstn_all, 1, 1), chunk_of_row)
                dz_ref[:, ZIH + h * HD:ZIH + (h + 1) * HD] = di.astype(BF16)
                dkl = _dot(_chunk_stack(iv, chunk_of_row), dstn_all, 1, 0)
                da = jnp.where(low, _dot(do, iv, 1, 1), 0.0).astype(BF16)
                dat = jnp.where(upp, _dot(iv, do, 1, 1), 0.0).astype(BF16)
                dqm = _dot(da, km, 1, 0)
                dkm = _dot(dat, qm, 1, 0)
                b = bp[:, ls]
                e1, e2 = jnp.exp(b - w["b_mid"]), jnp.exp(w["b_mid"] - b)
                e3, e4 = jnp.exp(w["b_last"] - b), jnp.exp(b)
                dqf = dqm * e1 + dqb * e4
                dkf_pair.append(dkm * e2 + dkl * e3)
                t_qm, t_km, t_kl = dqm * w["qm"], dkm * w["km"], dkl * w["kl"]
                db = t_qm - t_km - t_kl + dqb * w["qb"]
                db_mid = jnp.sum((t_km - t_qm).reshape(NCH, CHUNK, HD), axis=1, keepdims=True)
                db_last = jnp.sum(t_kl.reshape(NCH, CHUNK, HD), axis=1, keepdims=True)
                db_last = db_last + jnp.stack(dd_rows, axis=0) * jnp.exp(
                    bp[:, ls].reshape(NCH, CHUNK, HD)[:, CHUNK - 1:CHUNK, :])
                spread = lambda v: jnp.broadcast_to(v, (NCH, CHUNK, HD)).reshape(HB, HD)
                db = (db + jnp.where(in_chunk == CHUNK // 2 - 1, spread(db_mid), 0.0)
                      + jnp.where(in_chunk == CHUNK - 1, spread(db_last), 0.0))
                db_pair.append(db)
                sq = w["sq"]
                dz_ref[:, ZQH + h * HD:ZQH + (h + 1) * HD] = (
                    dqf * (HD ** -0.5) * (sq * (1.0 + q * (1.0 - sq)))).astype(BF16)
            dlogf = _chunk_sums(upp_b, jnp.concatenate(db_pair, axis=1))
            dfv = dlogf / fp - jnp.concatenate(dkf_pair, axis=1)
            dz_ref[:, ZFH + 2 * HD * p:ZFH + 2 * HD * (p + 1)] = (dfv * (1.0 - lbp) * sgp * (1.0 - sgp)).astype(BF16)
            dlb_ref[:, 2 * HD * p:2 * HD * (p + 1)] += _rowsum8(dfv * (1.0 - sgp))

    rev = lambda j: nb - 1 - j
    return pl.pallas_call(
        body, name="hgrn_bwd",
        out_shape=(jax.ShapeDtypeStruct((t, D_IN), BF16), jax.ShapeDtypeStruct((8, HG_W), F32),
                   jax.ShapeDtypeStruct((8, HD), F32)),
        grid=(nb,),
        in_specs=_hgrn_cols(rev) + [pl.BlockSpec((2, HG_W), lambda j: (0, 0)), pl.BlockSpec((1, HD), lambda j: (0, 0)),
                                    pl.BlockSpec((HB, HG_W), lambda j: (rev(j), 0)),
                                    pl.BlockSpec((4, NCH, HD, HD), lambda j: (0, rev(j), 0, 0)),
                                    pl.BlockSpec((HB, HG_W), lambda j: (rev(j), 1)),
                                    pl.BlockSpec((HB, SWA_W), lambda j: (rev(j), 0)),
                                    pl.BlockSpec((HB // 2, 2 * KV_W), lambda j: (rev(j), 0)),
                                    pl.BlockSpec((HB // 2, 2 * KV_W), lambda j: (rev(j), 0))],
        out_specs=(pl.BlockSpec((HB, D_IN), lambda j: (rev(j), 0)), pl.BlockSpec((8, HG_W), lambda j: (0, 0)),
                   pl.BlockSpec((8, HD), lambda j: (0, 0))),
        scratch_shapes=[pltpu.VMEM((4, HD, HD), F32)],
        compiler_params=_params(dimension_semantics=("arbitrary",)),
    )(*[z] * 8, hgrn_lb, onorm, o_save, sprev, dymix, *dza)


XB = 512


def _xattn_fwd(q, k, v, t):
    tb = min(XB, t)

    def body(q_ref, k_ref, v_ref, o_ref):
        for h in range(XH):
            cols = slice(XD * h, XD * (h + 1))
            s = _dot(q_ref[:, cols], k_ref[:, cols], 1, 1) * (XD ** -0.5)
            p = jnp.exp(s - jnp.max(s, axis=-1, keepdims=True))
            l = jnp.sum(p, axis=-1, keepdims=True)
            o_ref[:, cols] = (_dot(p.astype(BF16), v_ref[:, cols], 1, 0) * (1.0 / l)).astype(BF16)

    row = pl.BlockSpec((tb, D), lambda i: (i, 0))
    mem = pl.BlockSpec(k.shape, lambda i: (0, 0))
    return pl.pallas_call(
        body, name="xattn_fwd", out_shape=jax.ShapeDtypeStruct((t, D), BF16), grid=(t // tb,),
        in_specs=[row, mem, mem], out_specs=row, compiler_params=_params(),
    )(q, k, v)


def _xattn_bwd(q, k, v, do, t):
    tb = min(XB, t)

    def body(q_ref, k_ref, v_ref, do_ref, dq_ref, dk_ref, dv_ref):
        @pl.when(pl.program_id(0) == 0)
        def _():
            dk_ref[...] = jnp.zeros_like(dk_ref)
            dv_ref[...] = jnp.zeros_like(dv_ref)

        for h in range(XH):
            cols = slice(XD * h, XD * (h + 1))
            qh, kh, vh, doh = q_ref[:, cols], k_ref[:, cols], v_ref[:, cols], do_ref[:, cols]
            s = _dot(qh, kh, 1, 1) * (XD ** -0.5)
            p = jnp.exp(s - jnp.max(s, axis=-1, keepdims=True))
            p = p * (1.0 / jnp.sum(p, axis=-1, keepdims=True))
            dp = _dot(doh, vh, 1, 1)
            ds = (p * (dp - jnp.sum(p * dp, axis=-1, keepdims=True)) * (XD ** -0.5)).astype(BF16)
            dq_ref[:, cols] = _dot(ds, kh, 1, 0).astype(BF16)
            dk_ref[:, cols] += _dot(ds, qh, 0, 0)
            dv_ref[:, cols] += _dot(p.astype(BF16), doh, 0, 0)

    row = pl.BlockSpec((tb, D), lambda i: (i, 0))
    mem = pl.BlockSpec(k.shape, lambda i: (0, 0))
    return pl.pallas_call(
        body, name="xattn_bwd",
        out_shape=(jax.ShapeDtypeStruct((t, D), BF16), jax.ShapeDtypeStruct(k.shape, F32),
                   jax.ShapeDtypeStruct(k.shape, F32)),
        grid=(t // tb,), in_specs=[row, mem, mem, row], out_specs=(row, mem, mem),
        compiler_params=_params(dimension_semantics=("arbitrary",)),
    )(q, k, v, do)


def _mem_gain_bwd(dm, mem, *, name):
    def body(dm_ref, m_ref, dg_ref):
        m_ = m_ref[...]
        dg_ref[...] = _rowsum8(dm_ref[...] * (m_ * _rstd(m_)))

    return pl.pallas_call(body, name=name, out_shape=jax.ShapeDtypeStruct((8, D), F32),
                          compiler_params=_params())(dm, mem)


FM, FN = 512, 1408


def _ffn_up(u, wgt, wut, t):
    tm = min(FM, t)

    def body(u_ref, wg_ref, wu_ref, g_ref, up_ref, a_ref):
        u_ = u_ref[...]
        g = _dot(u_, wg_ref[...], 1, 1)
        up = _dot(u_, wu_ref[...], 1, 1)
        g_ref[...] = g.astype(BF16)
        up_ref[...] = up.astype(BF16)
        a_ref[...] = (g * _sig(g) * up).astype(BF16)

    w = pl.BlockSpec((FN, D), lambda j, i: (j, 0))
    o = pl.BlockSpec((tm, FN), lambda j, i: (i, j))
    return pl.pallas_call(
        body, name="ffn_up", out_shape=(jax.ShapeDtypeStruct((t, D_FF), BF16),) * 3,
        grid=(D_FF // FN, t // tm), in_specs=[pl.BlockSpec((tm, D), lambda j, i: (i, 0)), w, w],
        out_specs=(o, o, o), compiler_params=_params(),
    )(u, wgt, wut)


def _ffn_down_bwd(dy, wd, gate, up, t, dep=None):
    tm = min(FM, t)
    deps = [] if dep is None else [dep]

    def body(dy_ref, w_ref, g_ref, up_ref, *rest):
        dg_ref, dup_ref = rest[len(deps):]
        da = _dot(dy_ref[...], w_ref[...], 1, 1)
        g = g_ref[...].astype(F32)
        sg = _sig(g)
        dup_ref[...] = (da * g * sg).astype(BF16)
        dg_ref[...] = (da * up_ref[...].astype(F32) * (sg * (1.0 + g * (1.0 - sg)))).astype(BF16)

    o = pl.BlockSpec((tm, FN), lambda j, i: (i, j))
    return pl.pallas_call(
        body, name="ffn_down_bwd", out_shape=(jax.ShapeDtypeStruct((t, D_FF), BF16),) * 2,
        grid=(D_FF // FN, t // tm),
        in_specs=[pl.BlockSpec((tm, D), lambda j, i: (i, 0)), pl.BlockSpec((FN, D), lambda j, i: (j, 0)), o, o]
        + [ANY_SPEC] * len(deps),
        out_specs=(o, o), compiler_params=_params(),
    )(dy, wd, gate, up, *deps)


def _local_step(x, mem, target, fetch, sm, emit=None, first_dep=None, milestone=None):
    t = x.shape[0]
    w, gw = {}, {}

    def out(key, g):
        gw[key] = g
        return None if emit is None else emit(key, g)

    def tell(tag, value):
        return None if milestone is None else milestone(tag, value)
    u1 = _prenorm(x, sm["g_mix_pre"], name="prenorm_mix", dep=first_dep)
    w["winT"] = fetch("winT", u1)
    z = _mm(u1, w["winT"], tb=True, out_dtype=F32, tm=1024, tn=1408, name="mm_z", n_outer=True)
    ymix, lse = _swa_fwd(z, sm["sinks"], t, dep=tell("z", z))
    ymix, o_h, sprev = _hgrn2_fwd(z, sm["hgrn_lb"], sm["hgrn_onorm"], ymix, t)
    w["wout"] = fetch("wout", ymix)
    y1, h1, u2 = _mm_rows([(ymix, w["wout"], False)], [x], [sm["g_mix_post"], sm["g_x_pre"]], _ep_post_pre,
                          _EP_POST_PRE_OUTS, tm=512, name="mm_y1_post")
    mn = _prenorm(mem, sm["g_mem"], name="prenorm_mem")
    for key in ("wq", "wk", "wv"):
        w[key] = fetch(key, u2)
    qx = _mm(u2, w["wq"], out_dtype=BF16, tm=1024, tn=1024, name="mm_qx", dep=tell("u2", u2))
    kx = _mm(mn, w["wk"], out_dtype=BF16, tm=1024, tn=1024, name="mm_kx")
    vx = _mm(mn, w["wv"], out_dtype=BF16, tm=1024, tn=1024, name="mm_vx")
    ox = _xattn_fwd(qx, kx, vx, t)
    w["wo"] = fetch("wo", ox)
    y2, h2, u3 = _mm_rows([(ox, w["wo"], False)], [h1], [sm["g_x_post"], sm["g_ffn_pre"]], _ep_post_pre,
                          _EP_POST_PRE_OUTS, tm=512, name="mm_y2_post")
    w["wgT"], w["wuT"] = fetch("wgT", u3), fetch("wuT", u3)
    gate, up, act = _ffn_up(u3, w["wgT"], w["wuT"], t)
    w["wd"] = fetch("wd", act)
    sq, dh3, dy3, dg_ffn_post = _mm_rows([(act, w["wd"], False)], [h2, target], [sm["g_ffn_post"]], _ep_final_loss,
                                         _EP_FINAL_LOSS_OUTS, tm=512, name="mm_y3_loss")
    dep = out("wd", _mm(act, dy3, ta=True, out_dtype=BF16, tm=1408, tn=1024, name="mm_gwd"))
    dgate, dup = _ffn_down_bwd(dy3, w["wd"], gate, up, t, dep=dep)
    dep = out("wgT", _mm(dgate, u3, ta=True, out_dtype=BF16, tm=1408, tn=1024, name="mm_gwg"))
    dep = out("wuT", _mm(dup, u3, ta=True, out_dtype=BF16, tm=1408, tn=1024, name="mm_gwu", dep=dep))
    dh2, dy2, dg_ffn_pre, dg_x_post = _mm_rows(
        [(dgate, w["wgT"], False), (dup, w["wuT"], False)], [dh3, h2, y2], [sm["g_x_post"], sm["g_ffn_pre"]],
        _ep_post_pre_bwd, _EP_POST_PRE_BWD_OUTS, tm=512, name="mm_du3_post_bwd", dep=dep)
    dep = out("wo", _mm(ox, dy2, ta=True, out_dtype=BF16, tm=512, tn=1024, name="mm_gwo"))
    dox = _mm(dy2, w["wo"], tb=True, out_dtype=BF16, tm=1024, tn=1024, name="mm_dox", dep=dep)
    dqx, dkx, dvx = _xattn_bwd(qx, kx, vx, dox, t)
    dep = out("wq", _mm(u2, dqx, ta=True, out_dtype=BF16, tm=512, tn=1024, name="mm_gwq"))
    dep = out("wk", _mm(mn, dkx, ta=True, out_dtype=BF16, tm=1024, tn=1024, name="mm_gwk", dep=dep))
    dep = out("wv", _mm(mn, dvx, ta=True, out_dtype=BF16, tm=1024, tn=1024, name="mm_gwv", dep=dep))
    dh1, dy1, dg_x_pre, dg_mix_post = _mm_rows(
        [(dqx, w["wq"], True)], [dh2, h1, y1], [sm["g_mix_post"], sm["g_x_pre"]],
        _ep_post_pre_bwd, _EP_POST_PRE_BWD_OUTS, tm=512, name="mm_du2_post_bwd", dep=dep)
    dmn = _mm2(dkx, w["wk"], dvx, w["wv"], tb=True, out_dtype=F32, tm=256, name="mm_dmn")
    dg_mem = _mem_gain_bwd(dmn, mem, name="mem_gain_bwd")
    dep = out("wout", _mm(ymix, dy1, ta=True, out_dtype=BF16, tm=512, tn=1024, name="mm_gwout"))
    dymix = _mm(dy1, w["wout"], tb=True, out_dtype=BF16, tm=1024, tn=1024, name="mm_dymix", dep=dep)
    *dza, dsinks = _swa_bwd(z, sm["sinks"], ymix, lse, dymix, t)
    dz, dlb, donorm = _hgrn2_bwd(z, sm["hgrn_lb"], sm["hgrn_onorm"], o_h, sprev, dymix, dza, t)
    dep = out("winT", _mm(dz, u1, ta=True, out_dtype=BF16, tm=1408, tn=1024, name="mm_gwin"))
    grad_x, dg_mix_pre = _mm_rows([(dz, w["winT"], False)], [dh1, x], [sm["g_mix_pre"]], _ep_pre_bwd,
                                  _EP_PRE_BWD_OUTS, tm=512, name="mm_du1_pre_bwd", dep=dep)
    parts = dict(g_mix_pre=dg_mix_pre, g_mix_post=dg_mix_post, g_mem=dg_mem, g_x_pre=dg_x_pre,
                 g_x_post=dg_x_post, g_ffn_pre=dg_ffn_pre, g_ffn_post=dg_ffn_post,
                 hgrn_onorm=donorm, hgrn_lb=dlb, sinks=dsinks, sq=sq)
    return grad_x, gw, parts


def _position():
    return lax.axis_index("x"), lax.axis_index("y"), lax.axis_index("c")


def _peer(pos, k):
    x, y, c = pos
    return (1 - x if k & 4 else x, 1 - y if k & 2 else y, 1 - c if k & 1 else c)


def _linear(pos):
    x, y, c = pos
    return 4 * x + 2 * y + c


HBM_SPEC = pl.BlockSpec(memory_space=pltpu.HBM)
SEM_SPEC = pl.BlockSpec(memory_space=pltpu.SEMAPHORE)
DATAFLOW = pltpu.SideEffectType.DATAFLOW_SIDE_EFFECTING
SEND_ORDER = (1, 2, 4, 3, 5, 6, 7)


def _in_hbm(a):
    return pltpu.with_memory_space_constraint(a, pltpu.HBM)


def _prepare_weights(shards, *, name, dep=None):
    n = len(shards)
    deps = [] if dep is None else [dep]

    def body(*refs):
        ins, (outs, lands, sem) = refs[:n], (refs[-2 * n - 1:-n - 1], refs[-n - 1:-1], refs[-1])
        me_lin = _linear(_position())
        copies = []
        for a in range(n):
            r = ins[a].shape[0]
            outs[a][...] = ins[a][...].astype(BF16)
            copies.append(pltpu.make_async_copy(outs[a], lands[a].at[pl.ds(me_lin * r, r), :], sem.at[a]))
            copies[-1].start()
        for cp in copies:
            cp.wait()

    vmem = pl.BlockSpec(memory_space=pltpu.VMEM)
    res = pl.pallas_call(
        body, name=name,
        out_shape=tuple(jax.ShapeDtypeStruct(s.shape, BF16) for s in shards)
        + tuple(jax.ShapeDtypeStruct((N_DEV * s.shape[0], s.shape[1]), BF16) for s in shards),
        in_specs=[vmem] * n + [ANY_SPEC] * len(deps), out_specs=tuple([vmem] * n + [ANY_SPEC] * n),
        scratch_shapes=[pltpu.SemaphoreType.DMA((n,))], compiler_params=_params(),
    )(*shards, *deps)
    return res[:n], res[n:]


def _copies_start(arrays, plan, n, *, name):
    na = len(arrays)

    def body(*refs):
        ins, send_sems, recv_sems = refs[:na], refs[na], refs[na + 1]
        me = _position()
        for j in range(n):
            src, dst, peer, _ = plan(ins, me, j)
            pltpu.make_async_remote_copy(src_ref=src, dst_ref=dst, send_sem=send_sems.at[j], recv_sem=recv_sems.at[j],
                                         device_id=peer, device_id_type=MESH).start()

    return pl.pallas_call(
        body, name=name,
        out_shape=(pltpu.SemaphoreType.DMA((n,)), pltpu.SemaphoreType.DMA((n,)))
        + tuple(pltpu.HBM(a.shape, a.dtype) for a in arrays),
        in_specs=(HBM_SPEC,) * na, out_specs=(SEM_SPEC, SEM_SPEC) + (HBM_SPEC,) * na,
        input_output_aliases={i: 2 + i for i in range(na)},
        compiler_params=pltpu.CompilerParams(has_side_effects=DATAFLOW),
    )(*[_in_hbm(a) for a in arrays])


def _copies_wait(send_sems, recv_sems, arrays, plan, n, after, *, name):
    na = len(arrays)

    def body(*refs):
        ins, send_sems, recv_sems = refs[:na], refs[na], refs[na + 1]
        me = _position()
        for j in range(n):
            src, _, peer, landed = plan(ins, me, j)
            copy = pltpu.make_async_remote_copy(src_ref=src, dst_ref=landed, send_sem=send_sems.at[j],
                                                recv_sem=recv_sems.at[j], device_id=peer, device_id_type=MESH)
            copy.wait_send()
            copy.wait_recv()

    return pl.pallas_call(
        body, name=name, out_shape=tuple(pltpu.HBM(a.shape, a.dtype) for a in arrays),
        in_specs=(HBM_SPEC,) * na + (SEM_SPEC, SEM_SPEC, ANY_SPEC), out_specs=(HBM_SPEC,) * na,
        input_output_aliases={i: i for i in range(na)},
        compiler_params=pltpu.CompilerParams(has_side_effects=DATAFLOW),
    )(*arrays, send_sems, recv_sems, after)


SAME_CORE = (2, 4, 6)


class _TwoLevelGather:
    def __init__(self, shards, lands, *, name):
        n = self.n = len(shards)
        self.name = name
        first_peers = (1,) + SAME_CORE

        def rows(ref, pos):
            r = ref.shape[0] // N_DEV
            return ref.at[pl.ds(_linear(pos) * r, r), :]

        def first(refs, me, j):
            a, peer = j // 4, _peer(me, first_peers[j % 4])
            return refs[a], rows(refs[n + a], me), peer, rows(refs[n + a], peer)

        def second(refs, me, j):
            a, sibling = j // 3, _peer(me, 1)
            mine = rows(refs[a], _peer(me, SAME_CORE[j % 3]))
            return mine, mine, sibling, rows(refs[a], _peer(sibling, SAME_CORE[j % 3]))

        self._first, self._second = first, second
        self._flight = _copies_start(list(shards) + list(lands), first, 4 * n, name=name + "_send")
        self.dep = self._flight[2]

    def pass_on(self, after):
        send1, recv1, *arrays = self._flight
        arrays = _copies_wait(send1, recv1, arrays, self._first, 4 * self.n, after, name=self.name + "_recv")
        self._flight = _copies_start(list(arrays[self.n:]), self._second, 3 * self.n, name=self.name + "_pass")
        return self._flight[2]

    def finish(self, after):
        send2, recv2, *lands = self._flight
        return _copies_wait(send2, recv2, lands, self._second, 3 * self.n, after, name=self.name + "_pass_recv")


def _exchange_start(gs, *, name):
    n = len(gs)
    rows = [g.shape[0] // N_DEV for g in gs]
    lands = [lax.empty((N_DEV - 1, r, g.shape[1]), g.dtype) for g, r in zip(gs, rows)]

    def body(*refs):
        g_refs, land_refs = refs[:n], refs[n:2 * n]
        send_sems, recv_sems = refs[2 * n:3 * n], refs[3 * n:4 * n]
        me = _position()
        for a in range(n):
            for k in SEND_ORDER:
                peer = _peer(me, k)
                pltpu.make_async_remote_copy(
                    src_ref=g_refs[a].at[pl.ds(_linear(peer) * rows[a], rows[a]), :],
                    dst_ref=land_refs[a].at[k - 1],
                    send_sem=send_sems[a].at[k - 1], recv_sem=recv_sems[a].at[k - 1],
                    device_id=peer, device_id_type=MESH).start()

    res = pl.pallas_call(
        body, name=name,
        out_shape=tuple(pltpu.SemaphoreType.DMA((N_DEV - 1,)) for _ in range(2 * n))
        + tuple(pltpu.HBM(a.shape, a.dtype) for a in gs + lands),
        in_specs=(HBM_SPEC,) * (2 * n), out_specs=(SEM_SPEC,) * (2 * n) + (HBM_SPEC,) * (2 * n),
        input_output_aliases={i: 2 * n + i for i in range(2 * n)},
        compiler_params=pltpu.CompilerParams(has_side_effects=DATAFLOW),
    )(*[_in_hbm(a) for a in gs + lands])
    return [(res[a], res[n + a], res[2 * n + a], res[3 * n + a]) for a in range(n)]


def _exchange_wait(send_sems, recv_sems, g_thru, land_thru, after, *, name):
    r = land_thru.shape[1]

    def body(g_ref, land_ref, send_sems, recv_sems, after_ref, g_dead, got_ref):
        del after_ref, g_dead, got_ref
        me = _position()
        for k in SEND_ORDER:
            peer = _peer(me, k)
            copy = pltpu.make_async_remote_copy(
                src_ref=g_ref.at[pl.ds(_linear(peer) * r, r), :], dst_ref=land_ref.at[k - 1],
                send_sem=send_sems.at[k - 1], recv_sem=recv_sems.at[k - 1],
                device_id=peer, device_id_type=MESH)
            copy.wait_send()
            copy.wait_recv()

    return pl.pallas_call(
        body, name=name,
        out_shape=(pltpu.HBM(g_thru.shape, g_thru.dtype), pltpu.HBM(land_thru.shape, land_thru.dtype)),
        in_specs=(HBM_SPEC, HBM_SPEC, SEM_SPEC, SEM_SPEC, pl.BlockSpec(memory_space=pl.ANY)),
        out_specs=(HBM_SPEC, HBM_SPEC), input_output_aliases={0: 0, 1: 1},
        compiler_params=pltpu.CompilerParams(has_side_effects=DATAFLOW),
    )(g_thru, land_thru, send_sems, recv_sems, after)


def _adamw_math(w, g, m, v):
    m = B1 * m + (1.0 - B1) * g
    v = B2 * v + (1.0 - B2) * (g * g)
    delta = -LR * ((m / C1) / (jnp.sqrt(v / C2) + AEPS) + WD * w)
    return delta, m, v


def _sum_adamw(items, *, name):
    n = len(items)

    def body(*refs):
        ins, outs, scratch = refs[:5 * n], refs[5 * n:9 * n], refs[9 * n:]
        me_lin = _linear(_position())
        mine = []
        for a in range(n):
            r = items[a][2].shape[0]
            mine.append(pltpu.make_async_copy(ins[5 * a].at[pl.ds(me_lin * r, r), :], scratch[a], scratch[n].at[a]))
            mine[-1].start()
        for a in range(n):
            _, land_ref, w_ref, m_ref, v_ref = ins[5 * a:5 * a + 5]
            g_ref, d_ref, nm_ref, nv_ref = outs[4 * a:4 * a + 4]
            g = land_ref[0].astype(F32)
            for s in range(1, N_DEV - 1):
                g = g + land_ref[s].astype(F32)
            mine[a].wait()
            g = scratch[a][...].astype(F32) + g
            g_ref[...] = g
            d_ref[...], nm_ref[...], nv_ref[...] = _adamw_math(w_ref[...], g, m_ref[...], v_ref[...])

    vmem = pl.BlockSpec(memory_space=pltpu.VMEM)
    res = pl.pallas_call(
        body, name=name,
        out_shape=tuple(jax.ShapeDtypeStruct(it[2].shape, F32) for it in items for _ in range(4)),
        in_specs=[ANY_SPEC, vmem, vmem, vmem, vmem] * n, out_specs=(vmem,) * (4 * n),
        scratch_shapes=[pltpu.VMEM(it[2].shape, BF16) for it in items] + [pltpu.SemaphoreType.DMA((n,))],
        compiler_params=_params(),
    )(*[a for it in items for a in it])
    return [res[4 * a:4 * a + 4] for a in range(n)]


SMALL = ("g_mix_pre", "g_mix_post", "g_mem", "g_x_pre", "g_x_post", "g_ffn_pre", "g_ffn_post",
         "hgrn_onorm", "hgrn_lb", "sinks")
SMALL_W = dict(hgrn_onorm=HD, hgrn_lb=HG_W, sinks=8)
SQ_ROW = len(SMALL)
PACK_ROWS = 16


def _small_allreduce(parts, dep):
    ns = len(SMALL)

    def body(*refs):
        part, tot_ref = refs[:ns + 1], refs[ns + 2]
        gath, send_sems, recv_sems = refs[ns + 3:]
        me = _position()
        mine = gath.at[_linear(me)]
        mine[...] = jnp.zeros((PACK_ROWS, D), F32)
        for r, name in enumerate(SMALL):
            wd = SMALL_W.get(name, D)
            mine[r:r + 1, 0:wd] = jnp.sum(part[r][...], axis=0, keepdims=True)[:, 0:wd]
        sq = jnp.sum(part[ns][...]) * (0.5 / D)
        mine[SQ_ROW:SQ_ROW + 1, :] = jnp.full((1, D), sq, F32)

        def copy(k):
            peer = _peer(me, k)
            return pltpu.make_async_remote_copy(
                src_ref=mine, dst_ref=mine, send_sem=send_sems.at[k - 1], recv_sem=recv_sems.at[k - 1],
                device_id=peer, device_id_type=MESH)

        def arrival(k):
            slot = gath.at[_linear(_peer(me, k))]
            return pltpu.make_async_remote_copy(
                src_ref=slot, dst_ref=slot, send_sem=send_sems.at[k - 1], recv_sem=recv_sems.at[k - 1],
                device_id=_peer(me, k), device_id_type=MESH)

        sent = [copy(k) for k in range(1, 8)]
        for cp in sent:
            cp.start()
        for k in range(1, 8):
            arrival(k).wait_recv()
        for cp in sent:
            cp.wait_send()
        tot = gath[0]
        for s in range(1, N_DEV):
            tot = tot + gath[s]
        tot_ref[...] = tot

    vmem = pl.BlockSpec(memory_space=pltpu.VMEM)
    return pl.pallas_call(
        body, name="small_allreduce", out_shape=jax.ShapeDtypeStruct((PACK_ROWS, D), F32),
        in_specs=[vmem] * (ns + 1) + [ANY_SPEC], out_specs=vmem,
        scratch_shapes=[pltpu.VMEM((N_DEV, PACK_ROWS, D), F32), pltpu.SemaphoreType.DMA((7,)),
                        pltpu.SemaphoreType.DMA((7,))],
        compiler_params=_params(has_side_effects=True),
    )(*[parts[n] for n in SMALL], parts["sq"], dep)


def _small_update(tot, sm, m_sm, v_sm):
    ns = len(SMALL)

    def body(*refs):
        tot = refs[0][...]
        w_refs, m_refs, v_refs = refs[1:ns + 1], refs[ns + 1:2 * ns + 1], refs[2 * ns + 1:3 * ns + 1]
        outs = refs[3 * ns + 1:]
        loss_ref = outs[0]
        g_out, d_out = outs[1:ns + 1], outs[ns + 1:2 * ns + 1]
        nm_out, nv_out = outs[2 * ns + 1:3 * ns + 1], outs[3 * ns + 1:4 * ns + 1]
        loss_ref[...] = tot[SQ_ROW:SQ_ROW + 1, 0:1]
        for r, name in enumerate(SMALL):
            wd = SMALL_W.get(name, D)
            g = tot[r:r + 1, 0:wd]
            w = w_refs[r][...]
            if name == "hgrn_lb":
                mx = jnp.maximum(w[0:1], w[1:2])
                e0, e1 = jnp.exp(w[0:1] - mx), jnp.exp(w[1:2] - mx)
                lb0 = e0 / (e0 + e1)
                g0 = g * lb0 * (1.0 - lb0)
                for i, gi in enumerate((g0, -g0)):
                    d, nm, nv = _adamw_math(w[i:i + 1], gi, m_refs[r][i:i + 1, :], v_refs[r][i:i + 1, :])
                    g_out[r][i:i + 1, :] = gi
                    d_out[r][i:i + 1, :], nm_out[r][i:i + 1, :], nv_out[r][i:i + 1, :] = d, nm, nv
            else:
                d, nm, nv = _adamw_math(w, g, m_refs[r][...], v_refs[r][...])
                g_out[r][...] = g
                d_out[r][...], nm_out[r][...], nv_out[r][...] = d, nm, nv

    shapes = [jax.ShapeDtypeStruct(sm[n].shape, F32) for n in SMALL]
    res = pl.pallas_call(
        body, name="small_update", out_shape=tuple([jax.ShapeDtypeStruct((1, 1), F32)] + shapes * 4),
        compiler_params=_params(),
    )(tot, *[sm[n] for n in SMALL], *[m_sm[n] for n in SMALL], *[v_sm[n] for n in SMALL])
    groups = [dict(zip(SMALL, res[1 + i * ns:1 + (i + 1) * ns])) for i in range(4)]
    return res[0], groups[0], groups[1], groups[2], groups[3]


BIG = ("w_in", "w_gate", "w_up", "w_down", "w_out", "wq_x", "wk_x", "wv_x", "wo_x")
BIG_KEY = dict(w_in="winT", w_gate="wgT", w_up="wuT", w_down="wd", w_out="wout", wq_x="wq", wk_x="wk",
               wv_x="wv", wo_x="wo")
TRANSPOSED = ("w_in", "w_gate", "w_up")
WEIGHTS = ("w_in", "sinks", "hgrn_lb", "hgrn_onorm", "w_out", "g_mix_pre", "g_mix_post", "g_mem", "g_x_pre",
           "g_x_post", "wq_x", "wk_x", "wv_x", "wo_x", "g_ffn_pre", "g_ffn_post", "w_gate", "w_up", "w_down")


def kernel(x, mem, w_in, sinks, hgrn_lb, hgrn_onorm, w_out, g_mix_pre, g_mix_post, g_mem, g_x_pre, g_x_post, wq_x, wk_x, wv_x, wo_x, g_ffn_pre, g_ffn_post, w_gate, w_up, w_down, loss_target, m_w_in, m_sinks, m_hgrn_lb, m_hgrn_onorm, m_w_out, m_g_mix_pre, m_g_mix_post, m_g_mem, m_g_x_pre, m_g_x_post, m_wq_x, m_wk_x, m_wv_x, m_wo_x, m_g_ffn_pre, m_g_ffn_post, m_w_gate, m_w_up, m_w_down, v_w_in, v_sinks, v_hgrn_lb, v_hgrn_onorm, v_w_out, v_g_mix_pre, v_g_mix_post, v_g_mem, v_g_x_pre, v_g_x_post, v_wq_x, v_wk_x, v_wv_x, v_wo_x, v_g_ffn_pre, v_g_ffn_post, v_w_gate, v_w_up, v_w_down):
    given = dict(locals())
    wts = {n: given[n] for n in WEIGHTS}
    ms = {n: given["m_" + n] for n in WEIGHTS}
    vs = {n: given["v_" + n] for n in WEIGHTS}

    def mat(a, name):
        a = a[0]
        return a.T if name in TRANSPOSED else a

    groups, gathers, dep = (("w_in",), ("w_out", "wq_x", "wk_x", "wv_x", "wo_x"), ("w_gate", "w_up", "w_down")), [], None
    for tag, group in zip(("w_in", "w_attn", "w_ffn"), groups):
        shards, lands = _prepare_weights([mat(wts[n], n) for n in group], name="prepare_" + tag, dep=dep)
        gathers.append(_TwoLevelGather(shards, lands, name=tag))
        dep = gathers[-1].dep
    pass_at = {"z": gathers[1], "u2": gathers[2]}
    name_of = {k: n for n, k in BIG_KEY.items()}
    gathered = {}

    def milestone(tag, value):
        return pass_at[tag].pass_on(value) if tag in pass_at else None

    def fetch(key, after):
        name = name_of[key]
        if name not in gathered:
            g = [i for i, group in enumerate(groups) if name in group][0]
            if g == 0:
                gathers[0].pass_on(after)
            gathered.update(zip(groups[g], gathers[g].finish(after)))
        return gathered[name]

    sm = {n: wts[n] for n in SMALL}
    started, held = {}, {}
    send_with = {k: group for group in (("wgT", "wuT"), ("wo", "wq", "wk", "wv")) for k in group}

    def emit(key, g):
        held[key] = g
        group = send_with.get(key, (key,))
        if key != group[-1]:
            return None
        flights = _exchange_start([held[k] for k in group], name="grad_send_" + name_of[group[0]])
        started.update({name_of[k]: f for k, f in zip(group, flights)})
        return flights[-1][2]

    grad_x, _, parts = _local_step(x[0], mem[0], loss_target[0], fetch, sm, emit, first_dep=dep, milestone=milestone)
    grads, deltas, new_m, new_v = {}, {}, {}, {}
    after = grad_x
    for group in (("w_down",), ("w_gate",), ("w_up",), ("wo_x", "wq_x", "wk_x", "wv_x", "w_out"), ("w_in",)):
        items = []
        for n in group:
            g_all, land = _exchange_wait(*started[n], after, name="grad_recv_" + n)
            items.append((g_all, land, mat(wts[n], n), mat(ms[n], n), mat(vs[n], n)))
            after = land
        for n, res in zip(group, _sum_adamw(items, name="adamw_" + group[0])):
            after = res[1]
            if n in TRANSPOSED:
                res = [a.T for a in res]
            grads[n], deltas[n], new_m[n], new_v[n] = [a[None] for a in res]
    loss, g_s, d_s, m_s, v_s = _small_update(_small_allreduce(parts, after), sm, {n: ms[n] for n in SMALL},
                                             {n: vs[n] for n in SMALL})
    grads.update(g_s), deltas.update(d_s), new_m.update(m_s), new_v.update(v_s)
    return (loss[0, 0], grad_x[None], *[grads[n] for n in WEIGHTS], *[deltas[n] for n in WEIGHTS],
            *[new_m[n] for n in WEIGHTS], *[new_v[n] for n in WEIGHTS])
```

```python
import functools

import jax
import jax.numpy as jnp
from jax import lax
from jax.experimental import pallas as pl
from jax.experimental.pallas import tpu as pltpu

F32 = jnp.float32
BF16 = jnp.bfloat16

D = 1024
D_IN = 2816
D_FF = 2816
CHUNK = 64
SWA_W = 512
KV_W = 128
HG_W = 512
HD = 128
ZQH, ZFH, ZIH, ZGH = 768, 1280, 1792, 2304
XH, XD = 4, 256
EPS = 1e-6
NEG = -1e30
N_DEV = 8
MESH = pl.DeviceIdType.MESH

LR, B1, B2, AEPS, WD, STEP = 0.001, 0.9, 0.999, 1e-08, 0.01, 10
C1 = 1.0 - B1 ** STEP
C2 = 1.0 - B2 ** STEP

VMEM_LIMIT = 56 * 1024 * 1024


def _params(**kw):
    return pltpu.CompilerParams(vmem_limit_bytes=VMEM_LIMIT, **kw)


def _sig(x):
    return 1.0 / (1.0 + jnp.exp(-x))


def _rowsum8(x):
    r, w = x.shape
    return jnp.sum(x.reshape(r // 8, 8, w), axis=0)


def _dot(a, b, ca, cb, precision=None):
    return lax.dot_general(a, b, (((ca,), (cb,)), ((), ())), preferred_element_type=F32,
                           precision=precision)


ANY_SPEC = pl.BlockSpec(memory_space=pl.ANY)


def _mm(a, b, *, ta=False, tb=False, out_dtype, tm, tn, tk=None, name, dep=None, n_outer=False):
    m = a.shape[1] if ta else a.shape[0]
    k = a.shape[0] if ta else a.shape[1]
    n = b.shape[0] if tb else b.shape[1]
    tm, tn = min(tm, m), min(tn, n)
    tk = k if tk is None else min(tk, k)
    nk = k // tk
    assert m % tm == 0 and n % tn == 0 and k % tk == 0, (name, m, n, k, tm, tn, tk)
    ij = (lambda g0, g1: (g1, g0)) if n_outer else (lambda g0, g1: (g0, g1))
    a_spec = (pl.BlockSpec((tk, tm), lambda g0, g1, kk: (kk, ij(g0, g1)[0])) if ta
              else pl.BlockSpec((tm, tk), lambda g0, g1, kk: (ij(g0, g1)[0], kk)))
    b_spec = (pl.BlockSpec((tn, tk), lambda g0, g1, kk: (ij(g0, g1)[1], kk)) if tb
              else pl.BlockSpec((tk, tn), lambda g0, g1, kk: (kk, ij(g0, g1)[1])))
    ca, cb = (0 if ta else 1), (1 if tb else 0)

    deps = [] if dep is None else [dep]

    def body(a_ref, b_ref, *rest):
        o_ref, acc = rest[len(deps)], rest[len(deps) + 1:]
        p = _dot(a_ref[...].astype(BF16), b_ref[...].astype(BF16), ca, cb)
        if nk == 1:
            o_ref[...] = p.astype(out_dtype)
        else:
            acc_ref, = acc
            kk = pl.program_id(2)

            @pl.when(kk == 0)
            def _():
                acc_ref[...] = p

            @pl.when(kk > 0)
            def _():
                acc_ref[...] += p

            @pl.when(kk == nk - 1)
            def _():
                o_ref[...] = acc_ref[...].astype(out_dtype)

    return pl.pallas_call(
        body, name=name, out_shape=jax.ShapeDtypeStruct((m, n), out_dtype),
        grid=(n // tn, m // tm, nk) if n_outer else (m // tm, n // tn, nk),
        in_specs=[a_spec, b_spec] + [ANY_SPEC] * len(deps),
        out_specs=pl.BlockSpec((tm, tn), lambda g0, g1, kk: ij(g0, g1)),
        scratch_shapes=[pltpu.VMEM((tm, tn), F32)] if nk > 1 else [],
        compiler_params=_params(dimension_semantics=("parallel", "parallel", "arbitrary")),
    )(a, b, *deps)


def _mm2(a1, b1, a2, b2, *, tb=False, out_dtype, tm, name, dep=None):
    m, k = a1.shape
    n = b1.shape[0] if tb else b1.shape[1]
    tm = min(tm, m)
    assert m % tm == 0
    cb = 1 if tb else 0
    deps = [] if dep is None else [dep]

    def body(a1_ref, b1_ref, a2_ref, b2_ref, *rest):
        o_ref = rest[len(deps)]
        o_ref[...] = (_dot(a1_ref[...].astype(BF16), b1_ref[...], 1, cb)
                      + _dot(a2_ref[...].astype(BF16), b2_ref[...], 1, cb)).astype(out_dtype)

    a_spec = pl.BlockSpec((tm, k), lambda i: (i, 0))
    b_spec = pl.BlockSpec(b1.shape, lambda i: (0, 0))
    return pl.pallas_call(
        body, name=name, out_shape=jax.ShapeDtypeStruct((m, n), out_dtype),
        grid=(m // tm,), in_specs=[a_spec, b_spec, a_spec, b_spec] + [ANY_SPEC] * len(deps),
        out_specs=pl.BlockSpec((tm, n), lambda i: (i, 0)),
        compiler_params=_params(dimension_semantics=("parallel",)),
    )(a1, b1, a2, b2, *deps)


def _mm_rows(prods, rows_in, vecs_in, epilogue, outs, *, tm, name, dep=None):
    m = prods[0][0].shape[0]
    n = prods[0][1].shape[0] if prods[0][2] else prods[0][1].shape[1]
    tm = min(tm, m)
    assert m % tm == 0
    deps = [] if dep is None else [dep]
    n_p, n_r, n_v = len(prods), len(rows_in), len(vecs_in)

    def body(*refs):
        ab = refs[:2 * n_p]
        row_refs = refs[2 * n_p:2 * n_p + n_r]
        vec_refs = refs[2 * n_p + n_r:2 * n_p + n_r + n_v]
        out_refs = refs[2 * n_p + n_r + n_v + len(deps):]
        p = None
        for j, (_, _, tb) in enumerate(prods):
            t = _dot(ab[2 * j][...].astype(BF16), ab[2 * j + 1][...], 1, 1 if tb else 0)
            p = t if p is None else p + t
        vals = epilogue(p, *[r[...] for r in row_refs], *[v[...] for v in vec_refs])
        for (dtype, kind), o_ref, val in zip(outs, out_refs, vals):
            if kind == "row":
                o_ref[...] = val.astype(dtype)
            else:
                @pl.when(pl.program_id(0) == 0)
                def _(o_ref=o_ref):
                    o_ref[...] = jnp.zeros_like(o_ref)

                o_ref[...] += val

    row = lambda w: pl.BlockSpec((tm, w), lambda i: (i, 0))
    whole = lambda a: pl.BlockSpec(a.shape, lambda i: (0,) * a.ndim, pipeline_mode=pl.Buffered(1))
    in_specs, args = [], []
    for a, b, _ in prods:
        in_specs += [row(a.shape[1]), whole(b)]
        args += [a, b]
    in_specs += [row(r.shape[1]) for r in rows_in] + [whole(v) for v in vecs_in] + [ANY_SPEC] * len(deps)
    return pl.pallas_call(
        body, name=name,
        out_shape=tuple(jax.ShapeDtypeStruct((m, n) if kind == "row" else (8, n), dtype) for dtype, kind in outs),
        grid=(m // tm,), in_specs=in_specs,
        out_specs=tuple(row(n) if kind == "row" else pl.BlockSpec((8, n), lambda i: (0, 0)) for _, kind in outs),
        compiler_params=_params(dimension_semantics=("arbitrary",)),
    )(*args, *rows_in, *vecs_in, *deps)


def _rstd(x):
    return lax.rsqrt(jnp.mean(x * x, axis=-1, keepdims=True) + EPS)


def _norm_bwd(xh, r, t):
    return r * (t - xh * jnp.mean(xh * t, axis=-1, keepdims=True))


ROW_F32, ROW_BF16, SUM_F32 = (F32, "row"), (BF16, "row"), (F32, "sum")


def _ep_post_pre(p, h, g_post, g_pre):
    y = p.astype(BF16)
    yf = y.astype(F32)
    hn = h + yf * _rstd(yf) * g_post
    return y, hn, hn * _rstd(hn) * g_pre


_EP_POST_PRE_OUTS = [ROW_BF16, ROW_F32, ROW_BF16]


def _ep_final_loss(y, h, target, g_post):
    r = _rstd(y)
    yh = y * r
    err = h + yh * g_post - target
    dh = err * (1.0 / D)
    return _rowsum8(err * err), dh, _norm_bwd(yh, r, dh * g_post), _rowsum8(dh * yh)


_EP_FINAL_LOSS_OUTS = [SUM_F32, ROW_F32, ROW_BF16, SUM_F32]


def _ep_post_pre_bwd(du, dh_out, hn, y, g_post, g_pre):
    r2 = _rstd(hn)
    xh = hn * r2
    dh = dh_out + _norm_bwd(xh, r2, du * g_pre)
    yf = y.astype(F32)
    r1 = _rstd(yf)
    yh = yf * r1
    return dh, _norm_bwd(yh, r1, dh * g_post), _rowsum8(du * xh), _rowsum8(dh * yh)


_EP_POST_PRE_BWD_OUTS = [ROW_F32, ROW_BF16, SUM_F32, SUM_F32]


def _ep_pre_bwd(du, dh_out, x, g):
    r = _rstd(x)
    xh = x * r
    return dh_out + _norm_bwd(xh, r, du * g), _rowsum8(du * xh)


_EP_PRE_BWD_OUTS = [ROW_F32, SUM_F32]


def _prenorm(x, g, *, name, dep=None):
    t, d = x.shape
    tb = min(512, t)
    deps = [] if dep is None else [dep]

    def body(x_ref, g_ref, *rest):
        xf = x_ref[...]
        rest[-1][...] = (xf * _rstd(xf) * g_ref[...]).astype(BF16)

    return pl.pallas_call(
        body, name=name, out_shape=jax.ShapeDtypeStruct((t, d), BF16), grid=(t // tb,),
        in_specs=[pl.BlockSpec((tb, d), lambda i: (i, 0)), pl.BlockSpec((1, d), lambda i: (0, 0))]
        + [ANY_SPEC] * len(deps),
        out_specs=pl.BlockSpec((tb, d), lambda i: (i, 0)), compiler_params=_params(),
    )(x, g, *deps)


QB = 256


def _half_mask(shape, e):
    lane = lax.broadcasted_iota(jnp.int32, shape, len(shape) - 1)
    return (lane // 64) == e


def _place(kv):
    sw = pltpu.roll(kv, 64, 1)
    m0 = _half_mask(kv.shape, 0)
    return [[jnp.where(m0, kv, 0.0).astype(BF16), jnp.where(m0, 0.0, sw).astype(BF16)],
            [jnp.where(m0, sw, 0.0).astype(BF16), jnp.where(m0, 0.0, kv).astype(BF16)]]


SQ = 128
SK = 256


def _swa_valid(i, sb):
    qc = lax.broadcasted_iota(jnp.int32, (SQ, SK), 0) // CHUNK
    kc = lax.broadcasted_iota(jnp.int32, (SQ, SK), 1) // CHUNK - 2
    return (kc <= qc) & (qc <= kc + 2) & (4 * i + 2 * sb + kc >= 0)


def _swa_fwd(z, sinks, t, dep=None):
    nb = t // QB
    deps = [] if dep is None else [dep]

    def body(s_ref, q_ref, kp_ref, kc_ref, vp_ref, vc_ref, *rest):
        o_ref, lse_ref = rest[-2:]
        i = pl.program_id(0)
        kpl = _place(jnp.concatenate([kp_ref[...], kc_ref[...]], axis=0))
        vpl = _place(jnp.concatenate([vp_ref[...], vc_ref[...]], axis=0))
        lane = lax.broadcasted_iota(jnp.int32, (SQ, 128), 1)
        for sb in range(QB // SQ):
            rows, keys = slice(SQ * sb, SQ * (sb + 1)), slice(SQ * sb, SQ * sb + SK)
            valid = _swa_valid(i, sb)
            lse_out = jnp.zeros((SQ, 128), F32)
            for j in range(4):
                qp = q_ref[rows, 128 * j:128 * (j + 1)].astype(BF16)
                acc = jnp.zeros((SQ, 128), F32)
                for e in range(2):
                    h = 2 * j + e
                    kvh = h // 4
                    qm = jnp.where(_half_mask(qp.shape, e), qp, jnp.zeros_like(qp))
                    s = _dot(qm, kpl[kvh][e][keys], 1, 1) * 0.125
                    s = jnp.where(valid, s, NEG)
                    sink = s_ref[0, h]
                    m = jnp.maximum(jnp.max(s, axis=-1, keepdims=True), sink)
                    p = jnp.exp(s - m)
                    l = jnp.sum(p, axis=-1, keepdims=True) + jnp.exp(sink - m)
                    acc = acc + _dot(p.astype(BF16), vpl[kvh][e][keys], 1, 0) * (1.0 / l)
                    lse_out = jnp.where(lane == h, m + jnp.log(l), lse_out)
                o_ref[rows, 128 * j:128 * (j + 1)] = acc.astype(BF16)
            lse_ref[rows, :] = lse_out

    prev = lambda c: pl.BlockSpec((128, 128), lambda i: (jnp.maximum(2 * i - 1, 0), c))
    cur = lambda c: pl.BlockSpec((QB, 128), lambda i: (i, c))
    return pl.pallas_call(
        body, name="swa_fwd",
        out_shape=(jax.ShapeDtypeStruct((t, D), BF16), jax.ShapeDtypeStruct((t, 128), F32)),
        grid=(nb,),
        in_specs=[pl.BlockSpec(memory_space=pltpu.SMEM),
                  pl.BlockSpec((QB, SWA_W), lambda i: (i, 0)), prev(4), cur(4), prev(5), cur(5)]
        + [ANY_SPEC] * len(deps),
        out_specs=(pl.BlockSpec((QB, SWA_W), lambda i: (i, 0)), pl.BlockSpec((QB, 128), lambda i: (i, 0))),
        compiler_params=_params(),
    )(sinks, z, z, z, z, z, *deps)


def _swa_bwd(z, sinks, ymix, lse, dymix, t):
    nb = t // QB

    def body(s_ref, q_ref, kp_ref, kc_ref, vp_ref, vc_ref, o_ref, do_ref, l_ref,
             dq_ref, first_ref, second_ref, ds_ref, carry_ref):
        i = pl.program_id(0)
        live = i < nb

        @pl.when(i == 0)
        def _():
            ds_ref[...] = jnp.zeros_like(ds_ref)
            carry_ref[...] = jnp.zeros_like(carry_ref)

        lane = lax.broadcasted_iota(jnp.int32, (8, 128), 1)
        kpl = _place(jnp.concatenate([kp_ref[...], kc_ref[...]], axis=0))
        vpl = _place(jnp.concatenate([vp_ref[...], vc_ref[...]], axis=0))
        nk = QB + 128
        qc = lax.broadcasted_iota(jnp.int32, (QB, nk), 0) // CHUNK
        kc = lax.broadcasted_iota(jnp.int32, (QB, nk), 1) // CHUNK - 2
        valid = (kc <= qc) & (qc <= kc + 2) & (4 * i + kc >= 0) & live
        lse_c = l_ref[...]
        dsink = jnp.zeros((8, 128), F32)
        dk_acc = [[jnp.zeros((nk, 128), F32) for _ in range(2)] for _ in range(2)]
        dv_acc = [[jnp.zeros((nk, 128), F32) for _ in range(2)] for _ in range(2)]
        dq = []
        for j in range(4):
            cols = slice(128 * j, 128 * (j + 1))
            qp = q_ref[:, cols].astype(BF16)
            dop = do_ref[:, cols]
            prod = dop.astype(F32) * o_ref[:, cols].astype(F32)
            acc = jnp.zeros((QB, 128), F32)
            for e in range(2):
                h = 2 * j + e
                kvh = h // 4
                hm = _half_mask(qp.shape, e)
                qm = jnp.where(hm, qp, jnp.zeros_like(qp))
                dom = jnp.where(hm, dop, jnp.zeros_like(dop))
                dd = jnp.sum(jnp.where(hm, prod, 0.0), axis=-1, keepdims=True)
                lse_h = lse_c[:, h:h + 1]
                s = _dot(qm, kpl[kvh][e], 1, 1) * 0.125
                p = jnp.where(valid, jnp.exp(s - lse_h), 0.0)
                dp = _dot(dom, vpl[kvh][e], 1, 1)
                ds = (p * (dp - dd) * 0.125).astype(BF16)
                acc = acc + _dot(ds, kpl[kvh][e], 1, 0)
                dk_acc[kvh][e] = dk_acc[kvh][e] + _dot(ds, qm, 0, 0)
                dv_acc[kvh][e] = dv_acc[kvh][e] + _dot(p.astype(BF16), dom, 0, 0)
                ps = jnp.where(live, jnp.exp(s_ref[0, h] - lse_h) * dd, 0.0)
                dsink = dsink - jnp.where(lane == h, _rowsum8(jnp.broadcast_to(ps, (QB, 128))), 0.0)
            dq.append(acc.astype(BF16))
        ds_ref[...] += dsink
        dk = dk_acc[0][0] + dk_acc[1][1] + pltpu.roll(dk_acc[0][1] + dk_acc[1][0], 64, 1)
        dv = dv_acc[0][0] + dv_acc[1][1] + pltpu.roll(dv_acc[0][1] + dv_acc[1][0], 64, 1)
        dkv = jnp.concatenate([dk, dv], axis=1)
        second_ref[...] = (carry_ref[...] + dkv[0:128]).astype(BF16)
        carry_ref[...] = dkv[256:384]

        @pl.when(live)
        def _():
            for j in range(4):
                dq_ref[:, 128 * j:128 * (j + 1)] = dq[j]
            first_ref[...] = dkv[128:256].astype(BF16)

    blk = lambda i: jnp.minimum(i, nb - 1)
    prev = lambda c: pl.BlockSpec((128, 128), lambda i: (jnp.maximum(2 * blk(i) - 1, 0), c))
    cur = lambda w, c: pl.BlockSpec((QB, w), lambda i: (blk(i), c))
    half = lambda index: pl.BlockSpec((128, 256), lambda i: (index(i), 0))
    return pl.pallas_call(
        body, name="swa_bwd",
        out_shape=(jax.ShapeDtypeStruct((t, SWA_W), BF16), jax.ShapeDtypeStruct((t // 2, 256), BF16),
                   jax.ShapeDtypeStruct((t // 2, 256), BF16), jax.ShapeDtypeStruct((8, 128), F32)),
        grid=(nb + 1,),
        in_specs=[pl.BlockSpec(memory_space=pltpu.SMEM),
                  cur(SWA_W, 0), prev(4), cur(128, 4), prev(5), cur(128, 5),
                  cur(SWA_W, 0), cur(SWA_W, 0), cur(128, 0)],
        out_specs=(cur(SWA_W, 0), half(blk), half(lambda i: jnp.maximum(i - 1, 0)),
                   pl.BlockSpec((8, 128), lambda i: (0, 0))),
        scratch_shapes=[pltpu.VMEM((128, 256), F32)],
        compiler_params=_params(dimension_semantics=("arbitrary",)),
    )(sinks, z, z, z, z, z, ymix, dymix, lse)


HB = 256


def _lower_bound(lb_ref):
    a = lb_ref[...]
    a0, a1 = a[0:1], a[1:2]
    mx = jnp.maximum(a0, a1)
    e0, e1 = jnp.exp(a0 - mx), jnp.exp(a1 - mx)
    return e0 / (e0 + e1)


def _hgrn_cols(row_block):
    return [pl.BlockSpec((HB, 2 * HD), lambda j, c=base // (2 * HD) + p: (row_block(j), c))
            for base in (ZQH, ZFH, ZIH, ZGH) for p in range(2)]


NCH = HB // CHUNK


def _split3(x):
    hi = x.astype(BF16)
    r1 = x - hi.astype(F32)
    mid = r1.astype(BF16)
    return hi, mid, (r1 - mid.astype(F32)).astype(BF16)


def _blockdiag(lower):
    r = lax.broadcasted_iota(jnp.int32, (HB, HB), 0)
    c = lax.broadcasted_iota(jnp.int32, (HB, HB), 1)
    return (r // CHUNK == c // CHUNK) & ((c <= r) if lower else (c >= r))


def _chunk_sums(mask_bf16, x):
    return sum(_dot(mask_bf16, part, 1, 0) for part in _split3(x))


def _per_chunk_rows(x, row):
    w = x.shape[1]
    picked = x.reshape(NCH, CHUNK, w)[:, row:row + 1, :]
    return jnp.broadcast_to(picked, (NCH, CHUNK, w)).reshape(HB, w)


def _chunk_stack(x, chunk_of_row):
    return jnp.concatenate([jnp.where(chunk_of_row == c, x, jnp.zeros_like(x)) for c in range(NCH)], axis=1)


def _chunk_pick(x, chunk_of_row):
    w = x.shape[1] // NCH
    out = jnp.zeros((HB, w), x.dtype)
    for c in range(NCH):
        out = jnp.where(chunk_of_row == c, x[:, c * w:(c + 1) * w], out)
    return out


def _hgrn_local(q, f, kf, b):
    sq = _sig(q)
    qf = q * sq * (HD ** -0.5)
    b_mid = _per_chunk_rows(b, CHUNK // 2 - 1)
    b_last = _per_chunk_rows(b, CHUNK - 1)
    qm = qf * jnp.exp(b - b_mid)
    km = kf * jnp.exp(b_mid - b)
    kl = kf * jnp.exp(b_last - b)
    qb = qf * jnp.exp(b)
    return dict(sq=sq, b_mid=b_mid, b_last=b_last, qm=qm, km=km, kl=kl, qb=qb)


def _hgrn2_fwd(z, hgrn_lb, onorm, ymix, t, dep=None):
    nb = t // HB
    deps = [] if dep is None else [dep]

    def body(*refs):
        zq, zf, zi, zg = refs[0:2], refs[2:4], refs[4:6], refs[6:8]
        (lb_ref, on_ref), (y_ref, o_ref, sp_ref, st_ref) = refs[8:10], refs[-4:]

        @pl.when(pl.program_id(0) == 0)
        def _():
            st_ref[...] = jnp.zeros_like(st_ref)

        lb_all = _lower_bound(lb_ref)
        gn = on_ref[...]
        low = _blockdiag(True)
        low_b = low.astype(BF16)
        chunk_of_row = lax.broadcasted_iota(jnp.int32, (HB, HD), 0) // CHUNK
        for p in range(2):
            lbp = lb_all[:, 2 * HD * p:2 * HD * (p + 1)]
            fp = lbp + (1.0 - lbp) * _sig(zf[p][...])
            bp = _chunk_sums(low_b, jnp.log(fp))
            for e in range(2):
                h, ls = 2 * p + e, slice(e * HD, (e + 1) * HD)
                f = fp[:, ls]
                w = _hgrn_local(zq[p][:, ls], f, 1.0 - f, bp[:, ls])
                iv = zi[p][:, ls].astype(BF16)
                a = jnp.where(low, _dot(w["qm"].astype(BF16), w["km"].astype(BF16), 1, 1), 0.0)
                o = _dot(a.astype(BF16), iv, 1, 0)
                u = _dot(iv, _chunk_stack(w["kl"].astype(BF16), chunk_of_row), 0, 0)
                decay = jnp.exp(w["b_last"])
                st = st_ref[h]
                states = []
                for c in range(NCH):
                    sp_ref[h, c] = st
                    states.append(st.astype(BF16))
                    st = st * decay[c * CHUNK:c * CHUNK + 1] + u[:, c * HD:(c + 1) * HD]
                st_ref[h] = st
                inter = _dot(w["qb"].astype(BF16), jnp.concatenate(states, axis=0), 1, 1)
                o = o + _chunk_pick(inter, chunk_of_row)
                hs = slice(h * HD, (h + 1) * HD)
                o_ref[:, hs] = o
                gg = zg[p][:, ls]
                y_ref[:, hs] = (o * _rstd(o) * gn * (gg * _sig(gg))).astype(BF16)

    return pl.pallas_call(
        body, name="hgrn_fwd",
        out_shape=(jax.ShapeDtypeStruct((t, D), BF16), jax.ShapeDtypeStruct((t, HG_W), F32),
                   jax.ShapeDtypeStruct((4, t // CHUNK, HD, HD), F32)),
        grid=(nb,),
        in_specs=_hgrn_cols(lambda j: j) + [pl.BlockSpec((2, HG_W), lambda j: (0, 0)),
                                            pl.BlockSpec((1, HD), lambda j: (0, 0)), ANY_SPEC]
        + [ANY_SPEC] * len(deps),
        out_specs=(pl.BlockSpec((HB, HG_W), lambda j: (j, 1)),
                   pl.BlockSpec((HB, HG_W), lambda j: (j, 0)),
                   pl.BlockSpec((4, NCH, HD, HD), lambda j: (0, j, 0, 0))),
        scratch_shapes=[pltpu.VMEM((4, HD, HD), F32)],
        input_output_aliases={10: 0},
        compiler_params=_params(dimension_semantics=("arbitrary",)),
    )(*[z] * 8, hgrn_lb, onorm, ymix, *deps)


def _hgrn2_bwd(z, hgrn_lb, onorm, o_save, sprev, dymix, dza, t):
    nb = t // HB

    def body(*refs):
        zq, zf, zi, zg = refs[0:2], refs[2:4], refs[4:6], refs[6:8]
        (lb_ref, on_ref, o_ref, sp_ref, dy_ref, dqa_ref, first_ref, second_ref,
         dz_ref, dlb_ref, don_ref, dst_ref) = refs[8:]

        @pl.when(pl.program_id(0) == 0)
        def _():
            dst_ref[...] = jnp.zeros_like(dst_ref)
            dlb_ref[...] = jnp.zeros_like(dlb_ref)
            don_ref[...] = jnp.zeros_like(don_ref)

        dz_ref[:, 0:SWA_W] = dqa_ref[...]
        dz_ref[0:HB // 2, SWA_W:ZQH] = first_ref[...]
        dz_ref[HB // 2:HB, SWA_W:ZQH] = second_ref[...]
        lb_all = _lower_bound(lb_ref)
        gn = on_ref[...]
        low, upp = _blockdiag(True), _blockdiag(False)
        upp_b = upp.astype(BF16)
        low_b = low.astype(BF16)
        row = lax.broadcasted_iota(jnp.int32, (HB, HD), 0)
        chunk_of_row = row // CHUNK
        in_chunk = row % CHUNK
        for p in range(2):
            lbp = lb_all[:, 2 * HD * p:2 * HD * (p + 1)]
            sgp = _sig(zf[p][...])
            fp = lbp + (1.0 - lbp) * sgp
            bp = _chunk_sums(low_b, jnp.log(fp))
            db_pair, dkf_pair = [], []
            for e in range(2):
                h, ls, hs = 2 * p + e, slice(e * HD, (e + 1) * HD), slice((2 * p + e) * HD, (2 * p + e + 1) * HD)
                f = fp[:, ls]
                q = zq[p][:, ls]
                w = _hgrn_local(q, f, 1.0 - f, bp[:, ls])
                iv = zi[p][:, ls].astype(BF16)
                gg = zg[p][:, ls]
                o = o_ref[:, hs]
                dout = dy_ref[:, hs].astype(F32)
                sgg = _sig(gg)
                r = _rstd(o)
                oh = o * r
                dyn = dout * (gg * sgg)
                dz_ref[:, ZGH + h * HD:ZGH + (h + 1) * HD] = (
                    dout * oh * gn * (sgg * (1.0 + gg * (1.0 - sgg)))).astype(BF16)
                don_ref[...] += _rowsum8(dyn * oh)
                do = _norm_bwd(oh, r, dyn * gn).astype(BF16)
                qm, km, kl, qb = (w[n].astype(BF16) for n in ("qm", "km", "kl", "qb"))
                decay = jnp.exp(w["b_last"])
                grads_in = _dot(do, _chunk_stack(qb, chunk_of_row), 0, 0)
                dst = dst_ref[h]
                dstn, dd_rows = [None] * NCH, [None] * NCH
                for c in reversed(range(NCH)):
                    dstn[c] = dst.astype(BF16)
                    dd_rows[c] = jnp.sum(dst * sp_ref[h, c], axis=0, keepdims=True)
                    dst = dst * decay[c * CHUNK:c * CHUNK + 1] + grads_in[:, c * HD:(c + 1) * HD]
                dst_ref[h] = dst
                states = jnp.concatenate([sp_ref[h, c].astype(BF16) for c in range(NCH)], axis=0)
                dstn_all = jnp.concatenate(dstn, axis=0)
                dqb = _dot(_chunk_stack(do, chunk_of_row), states, 1, 0)
                at = jnp.where(upp, _dot(km, qm, 1, 1), 0.0)
                di = _dot(at.astype(BF16), do, 1, 0) + _chunk_pick(_dot(kl, dstn_all, 1, 1), chunk_of_row)
                dz_ref[:, ZIH + h * HD:ZIH + (h + 1) * HD] = di.astype(BF16)
                dkl = _dot(_chunk_stack(iv, chunk_of_row), dstn_all, 1, 0)
                da = jnp.where(low, _dot(do, iv, 1, 1), 0.0).astype(BF16)
                dat = jnp.where(upp, _dot(iv, do, 1, 1), 0.0).astype(BF16)
                dqm = _dot(da, km, 1, 0)
                dkm = _dot(dat, qm, 1, 0)
                b = bp[:, ls]
                e1, e2 = jnp.exp(b - w["b_mid"]), jnp.exp(w["b_mid"] - b)
                e3, e4 = jnp.exp(w["b_last"] - b), jnp.exp(b)
                dqf = dqm * e1 + dqb * e4
                dkf_pair.append(dkm * e2 + dkl * e3)
                t_qm, t_km, t_kl = dqm * w["qm"], dkm * w["km"], dkl * w["kl"]
                db = t_qm - t_km - t_kl + dqb * w["qb"]
                db_mid = jnp.sum((t_km - t_qm).reshape(NCH, CHUNK, HD), axis=1, keepdims=True)
                db_last = jnp.sum(t_kl.reshape(NCH, CHUNK, HD), axis=1, keepdims=True)
                db_last = db_last + jnp.stack(dd_rows, axis=0) * jnp.exp(
                    bp[:, ls].reshape(NCH, CHUNK, HD)[:, CHUNK - 1:CHUNK, :])
                spread = lambda v: jnp.broadcast_to(v, (NCH, CHUNK, HD)).reshape(HB, HD)
                db = (db + jnp.where(in_chunk == CHUNK // 2 - 1, spread(db_mid), 0.0)
                      + jnp.where(in_chunk == CHUNK - 1, spread(db_last), 0.0))
                db_pair.append(db)
                sq = w["sq"]
                dz_ref[:, ZQH + h * HD:ZQH + (h + 1) * HD] = (
                    dqf * (HD ** -0.5) * (sq * (1.0 + q * (1.0 - sq)))).astype(BF16)
            dlogf = _chunk_sums(upp_b, jnp.concatenate(db_pair, axis=1))
            dfv = dlogf / fp - jnp.concatenate(dkf_pair, axis=1)
            dz_ref[:, ZFH + 2 * HD * p:ZFH + 2 * HD * (p + 1)] = (dfv * (1.0 - lbp) * sgp * (1.0 - sgp)).astype(BF16)
            dlb_ref[:, 2 * HD * p:2 * HD * (p + 1)] += _rowsum8(dfv * (1.0 - sgp))

    rev = lambda j: nb - 1 - j
    return pl.pallas_call(
        body, name="hgrn_bwd",
        out_shape=(jax.ShapeDtypeStruct((t, D_IN), BF16), jax.ShapeDtypeStruct((8, HG_W), F32),
                   jax.ShapeDtypeStruct((8, HD), F32)),
        grid=(nb,),
        in_specs=_hgrn_cols(rev) + [pl.BlockSpec((2, HG_W), lambda j: (0, 0)), pl.BlockSpec((1, HD), lambda j: (0, 0)),
                                    pl.BlockSpec((HB, HG_W), lambda j: (rev(j), 0)),
                                    pl.BlockSpec((4, NCH, HD, HD), lambda j: (0, rev(j), 0, 0)),
                                    pl.BlockSpec((HB, HG_W), lambda j: (rev(j), 1)),
                                    pl.BlockSpec((HB, SWA_W), lambda j: (rev(j), 0)),
                                    pl.BlockSpec((HB // 2, 2 * KV_W), lambda j: (rev(j), 0)),
                                    pl.BlockSpec((HB // 2, 2 * KV_W), lambda j: (rev(j), 0))],
        out_specs=(pl.BlockSpec((HB, D_IN), lambda j: (rev(j), 0)), pl.BlockSpec((8, HG_W), lambda j: (0, 0)),
                   pl.BlockSpec((8, HD), lambda j: (0, 0))),
        scratch_shapes=[pltpu.VMEM((4, HD, HD), F32)],
        compiler_params=_params(dimension_semantics=("arbitrary",)),
    )(*[z] * 8, hgrn_lb, onorm, o_save, sprev, dymix, *dza)


XB = 512


def _xattn_fwd(q, k, v, t):
    tb = min(XB, t)

    def body(q_ref, k_ref, v_ref, o_ref):
        for h in range(XH):
            cols = slice(XD * h, XD * (h + 1))
            s = _dot(q_ref[:, cols], k_ref[:, cols], 1, 1) * (XD ** -0.5)
            p = jnp.exp(s - jnp.max(s, axis=-1, keepdims=True))
            l = jnp.sum(p, axis=-1, keepdims=True)
            o_ref[:, cols] = (_dot(p.astype(BF16), v_ref[:, cols], 1, 0) * (1.0 / l)).astype(BF16)

    row = pl.BlockSpec((tb, D), lambda i: (i, 0))
    mem = pl.BlockSpec(k.shape, lambda i: (0, 0))
    return pl.pallas_call(
        body, name="xattn_fwd", out_shape=jax.ShapeDtypeStruct((t, D), BF16), grid=(t // tb,),
        in_specs=[row, mem, mem], out_specs=row, compiler_params=_params(),
    )(q, k, v)


def _xattn_bwd(q, k, v, do, t):
    tb = min(XB, t)

    def body(q_ref, k_ref, v_ref, do_ref, dq_ref, dk_ref, dv_ref):
        @pl.when(pl.program_id(0) == 0)
        def _():
            dk_ref[...] = jnp.zeros_like(dk_ref)
            dv_ref[...] = jnp.zeros_like(dv_ref)

        for h in range(XH):
            cols = slice(XD * h, XD * (h + 1))
            qh, kh, vh, doh = q_ref[:, cols], k_ref[:, cols], v_ref[:, cols], do_ref[:, cols]
            s = _dot(qh, kh, 1, 1) * (XD ** -0.5)
            p = jnp.exp(s - jnp.max(s, axis=-1, keepdims=True))
            p = p * (1.0 / jnp.sum(p, axis=-1, keepdims=True))
            dp = _dot(doh, vh, 1, 1)
            ds = (p * (dp - jnp.sum(p * dp, axis=-1, keepdims=True)) * (XD ** -0.5)).astype(BF16)
            dq_ref[:, cols] = _dot(ds, kh, 1, 0).astype(BF16)
            dk_ref[:, cols] += _dot(ds, qh, 0, 0)
            dv_ref[:, cols] += _dot(p.astype(BF16), doh, 0, 0)

    row = pl.BlockSpec((tb, D), lambda i: (i, 0))
    mem = pl.BlockSpec(k.shape, lambda i: (0, 0))
    return pl.pallas_call(
        body, name="xattn_bwd",
        out_shape=(jax.ShapeDtypeStruct((t, D), BF16), jax.ShapeDtypeStruct(k.shape, F32),
                   jax.ShapeDtypeStruct(k.shape, F32)),
        grid=(t // tb,), in_specs=[row, mem, mem, row], out_specs=(row, mem, mem),
        compiler_params=_params(dimension_semantics=("arbitrary",)),
    )(q, k, v, do)


def _mem_gain_bwd(dm, mem, *, name):
    def body(dm_ref, m_ref, dg_ref):
        m_ = m_ref[...]
        dg_ref[...] = _rowsum8(dm_ref[...] * (m_ * _rstd(m_)))

    return pl.pallas_call(body, name=name, out_shape=jax.ShapeDtypeStruct((8, D), F32),
                          compiler_params=_params())(dm, mem)


FM, FN = 512, 1408


def _ffn_up(u, wgt, wut, t):
    tm = min(FM, t)

    def body(u_ref, wg_ref, wu_ref, g_ref, up_ref, a_ref):
        u_ = u_ref[...]
        g = _dot(u_, wg_ref[...], 1, 1)
        up = _dot(u_, wu_ref[...], 1, 1)
        g_ref[...] = g.astype(BF16)
        up_ref[...] = up.astype(BF16)
        a_ref[...] = (g * _sig(g) * up).astype(BF16)

    w = pl.BlockSpec((FN, D), lambda j, i: (j, 0))
    o = pl.BlockSpec((tm, FN), lambda j, i: (i, j))
    return pl.pallas_call(
        body, name="ffn_up", out_shape=(jax.ShapeDtypeStruct((t, D_FF), BF16),) * 3,
        grid=(D_FF // FN, t // tm), in_specs=[pl.BlockSpec((tm, D), lambda j, i: (i, 0)), w, w],
        out_specs=(o, o, o), compiler_params=_params(),
    )(u, wgt, wut)


def _ffn_down_bwd(dy, wd, gate, up, t, dep=None):
    tm = min(FM, t)
    deps = [] if dep is None else [dep]

    def body(dy_ref, w_ref, g_ref, up_ref, *rest):
        dg_ref, dup_ref = rest[len(deps):]
        da = _dot(dy_ref[...], w_ref[...], 1, 1)
        g = g_ref[...].astype(F32)
        sg = _sig(g)
        dup_ref[...] = (da * g * sg).astype(BF16)
        dg_ref[...] = (da * up_ref[...].astype(F32) * (sg * (1.0 + g * (1.0 - sg)))).astype(BF16)

    o = pl.BlockSpec((tm, FN), lambda j, i: (i, j))
    return pl.pallas_call(
        body, name="ffn_down_bwd", out_shape=(jax.ShapeDtypeStruct((t, D_FF), BF16),) * 2,
        grid=(D_FF // FN, t // tm),
        in_specs=[pl.BlockSpec((tm, D), lambda j, i: (i, 0)), pl.BlockSpec((FN, D), lambda j, i: (j, 0)), o, o]
        + [ANY_SPEC] * len(deps),
        out_specs=(o, o), compiler_params=_params(),
    )(dy, wd, gate, up, *deps)


def _local_step(x, mem, target, fetch, sm, emit=None, first_dep=None, milestone=None):
    t = x.shape[0]
    w, gw = {}, {}

    def out(key, g):
        gw[key] = g
        return None if emit is None else emit(key, g)

    def tell(tag, value):
        return None if milestone is None else milestone(tag, value)
    u1 = _prenorm(x, sm["g_mix_pre"], name="prenorm_mix", dep=first_dep)
    w["winT"] = fetch("winT", u1)
    z = _mm(u1, w["winT"], tb=True, out_dtype=F32, tm=1024, tn=1408, name="mm_z", n_outer=True)
    ymix, lse = _swa_fwd(z, sm["sinks"], t, dep=tell("z", z))
    ymix, o_h, sprev = _hgrn2_fwd(z, sm["hgrn_lb"], sm["hgrn_onorm"], ymix, t, dep=tell("swa", lse))
    w["wout"] = fetch("wout", ymix)
    y1, h1, u2 = _mm_rows([(ymix, w["wout"], False)], [x], [sm["g_mix_post"], sm["g_x_pre"]], _ep_post_pre,
                          _EP_POST_PRE_OUTS, tm=512, name="mm_y1_post")
    mn = _prenorm(mem, sm["g_mem"], name="prenorm_mem")
    for key in ("wq", "wk", "wv"):
        w[key] = fetch(key, u2)
    qx = _mm(u2, w["wq"], out_dtype=BF16, tm=1024, tn=1024, name="mm_qx")
    kx = _mm(mn, w["wk"], out_dtype=BF16, tm=1024, tn=1024, name="mm_kx")
    vx = _mm(mn, w["wv"], out_dtype=BF16, tm=1024, tn=1024, name="mm_vx")
    ox = _xattn_fwd(qx, kx, vx, t)
    w["wo"] = fetch("wo", ox)
    y2, h2, u3 = _mm_rows([(ox, w["wo"], False)], [h1], [sm["g_x_post"], sm["g_ffn_pre"]], _ep_post_pre,
                          _EP_POST_PRE_OUTS, tm=512, name="mm_y2_post", dep=tell("ox", ox))
    w["wgT"], w["wuT"] = fetch("wgT", u3), fetch("wuT", u3)
    gate, up, act = _ffn_up(u3, w["wgT"], w["wuT"], t)
    w["wd"] = fetch("wd", act)
    sq, dh3, dy3, dg_ffn_post = _mm_rows([(act, w["wd"], False)], [h2, target], [sm["g_ffn_post"]], _ep_final_loss,
                                         _EP_FINAL_LOSS_OUTS, tm=512, name="mm_y3_loss")
    dep = out("wd", _mm(act, dy3, ta=True, out_dtype=BF16, tm=1408, tn=1024, name="mm_gwd"))
    dgate, dup = _ffn_down_bwd(dy3, w["wd"], gate, up, t, dep=dep)
    dep = out("wgT", _mm(dgate, u3, ta=True, out_dtype=BF16, tm=1408, tn=1024, name="mm_gwg"))
    dep = out("wuT", _mm(dup, u3, ta=True, out_dtype=BF16, tm=1408, tn=1024, name="mm_gwu", dep=dep))
    dh2, dy2, dg_ffn_pre, dg_x_post = _mm_rows(
        [(dgate, w["wgT"], False), (dup, w["wuT"], False)], [dh3, h2, y2], [sm["g_x_post"], sm["g_ffn_pre"]],
        _ep_post_pre_bwd, _EP_POST_PRE_BWD_OUTS, tm=512, name="mm_du3_post_bwd", dep=dep)
    dep = out("wo", _mm(ox, dy2, ta=True, out_dtype=BF16, tm=512, tn=1024, name="mm_gwo"))
    dox = _mm(dy2, w["wo"], tb=True, out_dtype=BF16, tm=1024, tn=1024, name="mm_dox", dep=dep)
    dqx, dkx, dvx = _xattn_bwd(qx, kx, vx, dox, t)
    dep = out("wq", _mm(u2, dqx, ta=True, out_dtype=BF16, tm=512, tn=1024, name="mm_gwq"))
    dep = out("wk", _mm(mn, dkx, ta=True, out_dtype=BF16, tm=1024, tn=1024, name="mm_gwk", dep=dep))
    dep = out("wv", _mm(mn, dvx, ta=True, out_dtype=BF16, tm=1024, tn=1024, name="mm_gwv", dep=dep))
    dh1, dy1, dg_x_pre, dg_mix_post = _mm_rows(
        [(dqx, w["wq"], True)], [dh2, h1, y1], [sm["g_mix_post"], sm["g_x_pre"]],
        _ep_post_pre_bwd, _EP_POST_PRE_BWD_OUTS, tm=512, name="mm_du2_post_bwd", dep=dep)
    dmn = _mm2(dkx, w["wk"], dvx, w["wv"], tb=True, out_dtype=F32, tm=256, name="mm_dmn")
    dg_mem = _mem_gain_bwd(dmn, mem, name="mem_gain_bwd")
    dep = out("wout", _mm(ymix, dy1, ta=True, out_dtype=BF16, tm=512, tn=1024, name="mm_gwout"))
    dymix = _mm(dy1, w["wout"], tb=True, out_dtype=BF16, tm=1024, tn=1024, name="mm_dymix", dep=dep)
    *dza, dsinks = _swa_bwd(z, sm["sinks"], ymix, lse, dymix, t)
    dz, dlb, donorm = _hgrn2_bwd(z, sm["hgrn_lb"], sm["hgrn_onorm"], o_h, sprev, dymix, dza, t)
    dep = out("winT", _mm(dz, u1, ta=True, out_dtype=BF16, tm=1408, tn=1024, name="mm_gwin"))
    grad_x, dg_mix_pre = _mm_rows([(dz, w["winT"], False)], [dh1, x], [sm["g_mix_pre"]], _ep_pre_bwd,
                                  _EP_PRE_BWD_OUTS, tm=512, name="mm_du1_pre_bwd", dep=dep)
    parts = dict(g_mix_pre=dg_mix_pre, g_mix_post=dg_mix_post, g_mem=dg_mem, g_x_pre=dg_x_pre,
                 g_x_post=dg_x_post, g_ffn_pre=dg_ffn_pre, g_ffn_post=dg_ffn_post,
                 hgrn_onorm=donorm, hgrn_lb=dlb, sinks=dsinks, sq=sq)
    return grad_x, gw, parts


def _position():
    return lax.axis_index("x"), lax.axis_index("y"), lax.axis_index("c")


def _peer(pos, k):
    x, y, c = pos
    return (1 - x if k & 4 else x, 1 - y if k & 2 else y, 1 - c if k & 1 else c)


def _linear(pos):
    x, y, c = pos
    return 4 * x + 2 * y + c


HBM_SPEC = pl.BlockSpec(memory_space=pltpu.HBM)
SEM_SPEC = pl.BlockSpec(memory_space=pltpu.SEMAPHORE)
DATAFLOW = pltpu.SideEffectType.DATAFLOW_SIDE_EFFECTING
SEND_ORDER = (1, 2, 4, 3, 5, 6, 7)


def _in_hbm(a):
    return pltpu.with_memory_space_constraint(a, pltpu.HBM)


def _prepare_weights(shards, *, name, dep=None):
    n = len(shards)
    deps = [] if dep is None else [dep]

    def body(*refs):
        ins, (outs, lands, sem) = refs[:n], (refs[-2 * n - 1:-n - 1], refs[-n - 1:-1], refs[-1])
        me_lin = _linear(_position())
        copies = []
        for a in range(n):
            r = ins[a].shape[0]
            outs[a][...] = ins[a][...].astype(BF16)
            copies.append(pltpu.make_async_copy(outs[a], lands[a].at[pl.ds(me_lin * r, r), :], sem.at[a]))
            copies[-1].start()
        for cp in copies:
            cp.wait()

    vmem = pl.BlockSpec(memory_space=pltpu.VMEM)
    res = pl.pallas_call(
        body, name=name,
        out_shape=tuple(jax.ShapeDtypeStruct(s.shape, BF16) for s in shards)
        + tuple(jax.ShapeDtypeStruct((N_DEV * s.shape[0], s.shape[1]), BF16) for s in shards),
        in_specs=[vmem] * n + [ANY_SPEC] * len(deps), out_specs=tuple([vmem] * n + [ANY_SPEC] * n),
        scratch_shapes=[pltpu.SemaphoreType.DMA((n,))], compiler_params=_params(),
    )(*shards, *deps)
    return res[:n], res[n:]


def _copies_start(arrays, plan, n, *, name):
    na = len(arrays)

    def body(*refs):
        ins, send_sems, recv_sems = refs[:na], refs[na], refs[na + 1]
        me = _position()
        for j in range(n):
            src, dst, peer, _ = plan(ins, me, j)
            pltpu.make_async_remote_copy(src_ref=src, dst_ref=dst, send_sem=send_sems.at[j], recv_sem=recv_sems.at[j],
                                         device_id=peer, device_id_type=MESH).start()

    return pl.pallas_call(
        body, name=name,
        out_shape=(pltpu.SemaphoreType.DMA((n,)), pltpu.SemaphoreType.DMA((n,)))
        + tuple(pltpu.HBM(a.shape, a.dtype) for a in arrays),
        in_specs=(HBM_SPEC,) * na, out_specs=(SEM_SPEC, SEM_SPEC) + (HBM_SPEC,) * na,
        input_output_aliases={i: 2 + i for i in range(na)},
        compiler_params=pltpu.CompilerParams(has_side_effects=DATAFLOW),
    )(*[_in_hbm(a) for a in arrays])


def _copies_wait(send_sems, recv_sems, arrays, plan, n, after, *, name):
    na = len(arrays)

    def body(*refs):
        ins, send_sems, recv_sems = refs[:na], refs[na], refs[na + 1]
        me = _position()
        for j in range(n):
            src, _, peer, landed = plan(ins, me, j)
            copy = pltpu.make_async_remote_copy(src_ref=src, dst_ref=landed, send_sem=send_sems.at[j],
                                                recv_sem=recv_sems.at[j], device_id=peer, device_id_type=MESH)
            copy.wait_send()
            copy.wait_recv()

    return pl.pallas_call(
        body, name=name, out_shape=tuple(pltpu.HBM(a.shape, a.dtype) for a in arrays),
        in_specs=(HBM_SPEC,) * na + (SEM_SPEC, SEM_SPEC, ANY_SPEC), out_specs=(HBM_SPEC,) * na,
        input_output_aliases={i: i for i in range(na)},
        compiler_params=pltpu.CompilerParams(has_side_effects=DATAFLOW),
    )(*arrays, send_sems, recv_sems, after)


SAME_CORE = (2, 4, 6)


class _TwoLevelGather:
    def __init__(self, shards, lands, *, name):
        n = self.n = len(shards)
        self.name = name
        first_peers = (1,) + SAME_CORE

        def rows(ref, pos):
            r = ref.shape[0] // N_DEV
            return ref.at[pl.ds(_linear(pos) * r, r), :]

        def first(refs, me, j):
            a, peer = j // 4, _peer(me, first_peers[j % 4])
            return refs[a], rows(refs[n + a], me), peer, rows(refs[n + a], peer)

        def second(refs, me, j):
            a, sibling = j // 3, _peer(me, 1)
            mine = rows(refs[a], _peer(me, SAME_CORE[j % 3]))
            return mine, mine, sibling, rows(refs[a], _peer(sibling, SAME_CORE[j % 3]))

        self._first, self._second = first, second
        self._flight = _copies_start(list(shards) + list(lands), first, 4 * n, name=name + "_send")
        self.dep = self._flight[2]

    def pass_on(self, after):
        send1, recv1, *arrays = self._flight
        arrays = _copies_wait(send1, recv1, arrays, self._first, 4 * self.n, after, name=self.name + "_recv")
        self._flight = _copies_start(list(arrays[self.n:]), self._second, 3 * self.n, name=self.name + "_pass")
        return self._flight[2]

    def finish(self, after):
        send2, recv2, *lands = self._flight
        return _copies_wait(send2, recv2, lands, self._second, 3 * self.n, after, name=self.name + "_pass_recv")


def _exchange_start(gs, *, name):
    n = len(gs)
    rows = [g.shape[0] // N_DEV for g in gs]
    lands = [lax.empty((N_DEV - 1, r, g.shape[1]), g.dtype) for g, r in zip(gs, rows)]

    def body(*refs):
        g_refs, land_refs = refs[:n], refs[n:2 * n]
        send_sems, recv_sems = refs[2 * n:3 * n], refs[3 * n:4 * n]
        me = _position()
        for a in range(n):
            for k in SEND_ORDER:
                peer = _peer(me, k)
                pltpu.make_async_remote_copy(
                    src_ref=g_refs[a].at[pl.ds(_linear(peer) * rows[a], rows[a]), :],
                    dst_ref=land_refs[a].at[k - 1],
                    send_sem=send_sems[a].at[k - 1], recv_sem=recv_sems[a].at[k - 1],
                    device_id=peer, device_id_type=MESH).start()

    res = pl.pallas_call(
        body, name=name,
        out_shape=tuple(pltpu.SemaphoreType.DMA((N_DEV - 1,)) for _ in range(2 * n))
        + tuple(pltpu.HBM(a.shape, a.dtype) for a in gs + lands),
        in_specs=(HBM_SPEC,) * (2 * n), out_specs=(SEM_SPEC,) * (2 * n) + (HBM_SPEC,) * (2 * n),
        input_output_aliases={i: 2 * n + i for i in range(2 * n)},
        compiler_params=pltpu.CompilerParams(has_side_effects=DATAFLOW),
    )(*[_in_hbm(a) for a in gs + lands])
    return [(res[a], res[n + a], res[2 * n + a], res[3 * n + a]) for a in range(n)]


def _exchange_wait(send_sems, recv_sems, g_thru, land_thru, after, *, name):
    r = land_thru.shape[1]

    def body(g_ref, land_ref, send_sems, recv_sems, after_ref, g_dead, got_ref):
        del after_ref, g_dead, got_ref
        me = _position()
        for k in SEND_ORDER:
            peer = _peer(me, k)
            copy = pltpu.make_async_remote_copy(
                src_ref=g_ref.at[pl.ds(_linear(peer) * r, r), :], dst_ref=land_ref.at[k - 1],
                send_sem=send_sems.at[k - 1], recv_sem=recv_sems.at[k - 1],
                device_id=peer, device_id_type=MESH)
            copy.wait_send()
            copy.wait_recv()

    return pl.pallas_call(
        body, name=name,
        out_shape=(pltpu.HBM(g_thru.shape, g_thru.dtype), pltpu.HBM(land_thru.shape, land_thru.dtype)),
        in_specs=(HBM_SPEC, HBM_SPEC, SEM_SPEC, SEM_SPEC, pl.BlockSpec(memory_space=pl.ANY)),
        out_specs=(HBM_SPEC, HBM_SPEC), input_output_aliases={0: 0, 1: 1},
        compiler_params=pltpu.CompilerParams(has_side_effects=DATAFLOW),
    )(g_thru, land_thru, send_sems, recv_sems, after)


def _adamw_math(w, g, m, v):
    m = B1 * m + (1.0 - B1) * g
    v = B2 * v + (1.0 - B2) * (g * g)
    delta = -LR * ((m / C1) / (jnp.sqrt(v / C2) + AEPS) + WD * w)
    return delta, m, v


def _sum_adamw(items, *, name):
    n = len(items)

    def body(*refs):
        ins, outs, scratch = refs[:5 * n], refs[5 * n:9 * n], refs[9 * n:]
        me_lin = _linear(_position())
        mine = []
        for a in range(n):
            r = items[a][2].shape[0]
            mine.append(pltpu.make_async_copy(ins[5 * a].at[pl.ds(me_lin * r, r), :], scratch[a], scratch[n].at[a]))
            mine[-1].start()
        for a in range(n):
            _, land_ref, w_ref, m_ref, v_ref = ins[5 * a:5 * a + 5]
            g_ref, d_ref, nm_ref, nv_ref = outs[4 * a:4 * a + 4]
            g = land_ref[0].astype(F32)
            for s in range(1, N_DEV - 1):
                g = g + land_ref[s].astype(F32)
            mine[a].wait()
            g = scratch[a][...].astype(F32) + g
            g_ref[...] = g
            d_ref[...], nm_ref[...], nv_ref[...] = _adamw_math(w_ref[...], g, m_ref[...], v_ref[...])

    vmem = pl.BlockSpec(memory_space=pltpu.VMEM)
    res = pl.pallas_call(
        body, name=name,
        out_shape=tuple(jax.ShapeDtypeStruct(it[2].shape, F32) for it in items for _ in range(4)),
        in_specs=[ANY_SPEC, vmem, vmem, vmem, vmem] * n, out_specs=(vmem,) * (4 * n),
        scratch_shapes=[pltpu.VMEM(it[2].shape, BF16) for it in items] + [pltpu.SemaphoreType.DMA((n,))],
        compiler_params=_params(),
    )(*[a for it in items for a in it])
    return [res[4 * a:4 * a + 4] for a in range(n)]


SMALL = ("g_mix_pre", "g_mix_post", "g_mem", "g_x_pre", "g_x_post", "g_ffn_pre", "g_ffn_post",
         "hgrn_onorm", "hgrn_lb", "sinks")
SMALL_W = dict(hgrn_onorm=HD, hgrn_lb=HG_W, sinks=8)
SQ_ROW = len(SMALL)
PACK_ROWS = 16


def _small_allreduce(parts, dep):
    ns = len(SMALL)

    def body(*refs):
        part, tot_ref = refs[:ns + 1], refs[ns + 2]
        gath, send_sems, recv_sems = refs[ns + 3:]
        me = _position()
        mine = gath.at[_linear(me)]
        mine[...] = jnp.zeros((PACK_ROWS, D), F32)
        for r, name in enumerate(SMALL):
            wd = SMALL_W.get(name, D)
            mine[r:r + 1, 0:wd] = jnp.sum(part[r][...], axis=0, keepdims=True)[:, 0:wd]
        sq = jnp.sum(part[ns][...]) * (0.5 / D)
        mine[SQ_ROW:SQ_ROW + 1, :] = jnp.full((1, D), sq, F32)

        def copy(k):
            peer = _peer(me, k)
            return pltpu.make_async_remote_copy(
                src_ref=mine, dst_ref=mine, send_sem=send_sems.at[k - 1], recv_sem=recv_sems.at[k - 1],
                device_id=peer, device_id_type=MESH)

        def arrival(k):
            slot = gath.at[_linear(_peer(me, k))]
            return pltpu.make_async_remote_copy(
                src_ref=slot, dst_ref=slot, send_sem=send_sems.at[k - 1], recv_sem=recv_sems.at[k - 1],
                device_id=_peer(me, k), device_id_type=MESH)

        sent = [copy(k) for k in range(1, 8)]
        for cp in sent:
            cp.start()
        for k in range(1, 8):
            arrival(k).wait_recv()
        for cp in sent:
            cp.wait_send()
        tot = gath[0]
        for s in range(1, N_DEV):
            tot = tot + gath[s]
        tot_ref[...] = tot

    vmem = pl.BlockSpec(memory_space=pltpu.VMEM)
    return pl.pallas_call(
        body, name="small_allreduce", out_shape=jax.ShapeDtypeStruct((PACK_ROWS, D), F32),
        in_specs=[vmem] * (ns + 1) + [ANY_SPEC], out_specs=vmem,
        scratch_shapes=[pltpu.VMEM((N_DEV, PACK_ROWS, D), F32), pltpu.SemaphoreType.DMA((7,)),
                        pltpu.SemaphoreType.DMA((7,))],
        compiler_params=_params(has_side_effects=True),
    )(*[parts[n] for n in SMALL], parts["sq"], dep)


def _small_update(tot, sm, m_sm, v_sm):
    ns = len(SMALL)

    def body(*refs):
        tot = refs[0][...]
        w_refs, m_refs, v_refs = refs[1:ns + 1], refs[ns + 1:2 * ns + 1], refs[2 * ns + 1:3 * ns + 1]
        outs = refs[3 * ns + 1:]
        loss_ref = outs[0]
        g_out, d_out = outs[1:ns + 1], outs[ns + 1:2 * ns + 1]
        nm_out, nv_out = outs[2 * ns + 1:3 * ns + 1], outs[3 * ns + 1:4 * ns + 1]
        loss_ref[...] = tot[SQ_ROW:SQ_ROW + 1, 0:1]
        for r, name in enumerate(SMALL):
            wd = SMALL_W.get(name, D)
            g = tot[r:r + 1, 0:wd]
            w = w_refs[r][...]
            if name == "hgrn_lb":
                mx = jnp.maximum(w[0:1], w[1:2])
                e0, e1 = jnp.exp(w[0:1] - mx), jnp.exp(w[1:2] - mx)
                lb0 = e0 / (e0 + e1)
                g0 = g * lb0 * (1.0 - lb0)
                for i, gi in enumerate((g0, -g0)):
                    d, nm, nv = _adamw_math(w[i:i + 1], gi, m_refs[r][i:i + 1, :], v_refs[r][i:i + 1, :])
                    g_out[r][i:i + 1, :] = gi
                    d_out[r][i:i + 1, :], nm_out[r][i:i + 1, :], nv_out[r][i:i + 1, :] = d, nm, nv
            else:
                d, nm, nv = _adamw_math(w, g, m_refs[r][...], v_refs[r][...])
                g_out[r][...] = g
                d_out[r][...], nm_out[r][...], nv_out[r][...] = d, nm, nv

    shapes = [jax.ShapeDtypeStruct(sm[n].shape, F32) for n in SMALL]
    res = pl.pallas_call(
        body, name="small_update", out_shape=tuple([jax.ShapeDtypeStruct((1, 1), F32)] + shapes * 4),
        compiler_params=_params(),
    )(tot, *[sm[n] for n in SMALL], *[m_sm[n] for n in SMALL], *[v_sm[n] for n in SMALL])
    groups = [dict(zip(SMALL, res[1 + i * ns:1 + (i + 1) * ns])) for i in range(4)]
    return res[0], groups[0], groups[1], groups[2], groups[3]


BIG = ("w_in", "w_gate", "w_up", "w_down", "w_out", "wq_x", "wk_x", "wv_x", "wo_x")
BIG_KEY = dict(w_in="winT", w_gate="wgT", w_up="wuT", w_down="wd", w_out="wout", wq_x="wq", wk_x="wk",
               wv_x="wv", wo_x="wo")
TRANSPOSED = ("w_in", "w_gate", "w_up")
WEIGHTS = ("w_in", "sinks", "hgrn_lb", "hgrn_onorm", "w_out", "g_mix_pre", "g_mix_post", "g_mem", "g_x_pre",
           "g_x_post", "wq_x", "wk_x", "wv_x", "wo_x", "g_ffn_pre", "g_ffn_post", "w_gate", "w_up", "w_down")


def kernel(x, mem, w_in, sinks, hgrn_lb, hgrn_onorm, w_out, g_mix_pre, g_mix_post, g_mem, g_x_pre, g_x_post, wq_x, wk_x, wv_x, wo_x, g_ffn_pre, g_ffn_post, w_gate, w_up, w_down, loss_target, m_w_in, m_sinks, m_hgrn_lb, m_hgrn_onorm, m_w_out, m_g_mix_pre, m_g_mix_post, m_g_mem, m_g_x_pre, m_g_x_post, m_wq_x, m_wk_x, m_wv_x, m_wo_x, m_g_ffn_pre, m_g_ffn_post, m_w_gate, m_w_up, m_w_down, v_w_in, v_sinks, v_hgrn_lb, v_hgrn_onorm, v_w_out, v_g_mix_pre, v_g_mix_post, v_g_mem, v_g_x_pre, v_g_x_post, v_wq_x, v_wk_x, v_wv_x, v_wo_x, v_g_ffn_pre, v_g_ffn_post, v_w_gate, v_w_up, v_w_down):
    given = dict(locals())
    wts = {n: given[n] for n in WEIGHTS}
    ms = {n: given["m_" + n] for n in WEIGHTS}
    vs = {n: given["v_" + n] for n in WEIGHTS}

    def mat(a, name):
        a = a[0]
        return a.T if name in TRANSPOSED else a

    groups = (("w_in",), ("w_out", "wq_x", "wk_x", "wv_x", "wo_x"), ("w_gate", "w_up", "w_down"))
    gathers = []

    def start_group(g, dep):
        tag = ("w_in", "w_attn", "w_ffn")[g]
        shards, lands = _prepare_weights([mat(wts[n], n) for n in groups[g]], name="prepare_" + tag, dep=dep)
        gathers.append(_TwoLevelGather(shards, lands, name=tag))
        return gathers[-1].dep

    first_dep = start_group(1, start_group(0, None))
    name_of = {k: n for n, k in BIG_KEY.items()}
    gathered = {}

    def milestone(tag, value):
        if tag == "z":
            return start_group(2, value)
        return gathers[{"swa": 1, "ox": 2}[tag]].pass_on(value)

    def fetch(key, after):
        name = name_of[key]
        if name not in gathered:
            g = [i for i, group in enumerate(groups) if name in group][0]
            if g == 0:
                gathers[0].pass_on(after)
            gathered.update(zip(groups[g], gathers[g].finish(after)))
        return gathered[name]

    sm = {n: wts[n] for n in SMALL}
    started, held = {}, {}
    send_with = {k: group for group in (("wgT", "wuT"), ("wo", "wq", "wk", "wv")) for k in group}

    def emit(key, g):
        held[key] = g
        group = send_with.get(key, (key,))
        if key != group[-1]:
            return None
        flights = _exchange_start([held[k] for k in group], name="grad_send_" + name_of[group[0]])
        started.update({name_of[k]: f for k, f in zip(group, flights)})
        return flights[-1][2]

    grad_x, _, parts = _local_step(x[0], mem[0], loss_target[0], fetch, sm, emit, first_dep=first_dep, milestone=milestone)
    grads, deltas, new_m, new_v = {}, {}, {}, {}
    after = grad_x
    for group in (("w_down",), ("w_gate",), ("w_up",), ("wo_x", "wq_x", "wk_x", "wv_x", "w_out"), ("w_in",)):
        items = []
        for n in group:
            g_all, land = _exchange_wait(*started[n], after, name="grad_recv_" + n)
            items.append((g_all, land, mat(wts[n], n), mat(ms[n], n), mat(vs[n], n)))
            after = land
        for n, res in zip(group, _sum_adamw(items, name="adamw_" + group[0])):
            after = res[1]
            if n in TRANSPOSED:
                res = [a.T for a in res]
            grads[n], deltas[n], new_m[n], new_v[n] = [a[None] for a in res]
    loss, g_s, d_s, m_s, v_s = _small_update(_small_allreduce(parts, after), sm, {n: ms[n] for n in SMALL},
                                             {n: vs[n] for n in SMALL})
    grads.update(g_s), deltas.update(d_s), new_m.update(m_s), new_v.update(v_s)
    return (loss[0, 0], grad_x[None], *[grads[n] for n in WEIGHTS], *[deltas[n] for n in WEIGHTS],
            *[new_m[n] for n in WEIGHTS], *[new_v[n] for n in WEIGHTS])
```

```python
import functools

import jax
import jax.numpy as jnp
from jax import lax
from jax.experimental import pallas as pl
from jax.experimental.pallas import tpu as pltpu

F32 = jnp.float32
BF16 = jnp.bfloat16

D = 1024
D_IN = 2816
D_FF = 2816
CHUNK = 64
SWA_W = 512
KV_W = 128
HG_W = 512
HD = 128
ZQH, ZFH, ZIH, ZGH = 768, 1280, 1792, 2304
XH, XD = 4, 256
EPS = 1e-6
NEG = -1e30
N_DEV = 8
MESH = pl.DeviceIdType.MESH

LR, B1, B2, AEPS, WD, STEP = 0.001, 0.9, 0.999, 1e-08, 0.01, 10
C1 = 1.0 - B1 ** STEP
C2 = 1.0 - B2 ** STEP

VMEM_LIMIT = 56 * 1024 * 1024


def _params(**kw):
    return pltpu.CompilerParams(vmem_limit_bytes=VMEM_LIMIT, **kw)


def _sig(x):
    return 1.0 / (1.0 + jnp.exp(-x))


def _rowsum8(x):
    r, w = x.shape
    return jnp.sum(x.reshape(r // 8, 8, w), axis=0)


def _dot(a, b, ca, cb, precision=None):
    return lax.dot_general(a, b, (((ca,), (cb,)), ((), ())), preferred_element_type=F32,
                           precision=precision)


ANY_SPEC = pl.BlockSpec(memory_space=pl.ANY)


def _mm(a, b, *, ta=False, tb=False, out_dtype, tm, tn, tk=None, name, dep=None, n_outer=False):
    m = a.shape[1] if ta else a.shape[0]
    k = a.shape[0] if ta else a.shape[1]
    n = b.shape[0] if tb else b.shape[1]
    tm, tn = min(tm, m), min(tn, n)
    tk = k if tk is None else min(tk, k)
    nk = k // tk
    assert m % tm == 0 and n % tn == 0 and k % tk == 0, (name, m, n, k, tm, tn, tk)
    ij = (lambda g0, g1: (g1, g0)) if n_outer else (lambda g0, g1: (g0, g1))
    a_spec = (pl.BlockSpec((tk, tm), lambda g0, g1, kk: (kk, ij(g0, g1)[0])) if ta
              else pl.BlockSpec((tm, tk), lambda g0, g1, kk: (ij(g0, g1)[0], kk)))
    b_spec = (pl.BlockSpec((tn, tk), lambda g0, g1, kk: (ij(g0, g1)[1], kk)) if tb
              else pl.BlockSpec((tk, tn), lambda g0, g1, kk: (kk, ij(g0, g1)[1])))
    ca, cb = (0 if ta else 1), (1 if tb else 0)

    deps = [] if dep is None else [dep]

    def body(a_ref, b_ref, *rest):
        o_ref, acc = rest[len(deps)], rest[len(deps) + 1:]
        p = _dot(a_ref[...].astype(BF16), b_ref[...].astype(BF16), ca, cb)
        if nk == 1:
            o_ref[...] = p.astype(out_dtype)
        else:
            acc_ref, = acc
            kk = pl.program_id(2)

            @pl.when(kk == 0)
            def _():
                acc_ref[...] = p

            @pl.when(kk > 0)
            def _():
                acc_ref[...] += p

            @pl.when(kk == nk - 1)
            def _():
                o_ref[...] = acc_ref[...].astype(out_dtype)

    return pl.pallas_call(
        body, name=name, out_shape=jax.ShapeDtypeStruct((m, n), out_dtype),
        grid=(n // tn, m // tm, nk) if n_outer else (m // tm, n // tn, nk),
        in_specs=[a_spec, b_spec] + [ANY_SPEC] * len(deps),
        out_specs=pl.BlockSpec((tm, tn), lambda g0, g1, kk: ij(g0, g1)),
        scratch_shapes=[pltpu.VMEM((tm, tn), F32)] if nk > 1 else [],
        compiler_params=_params(dimension_semantics=("parallel", "parallel", "arbitrary")),
    )(a, b, *deps)


def _mm_rows(prods, rows_in, vecs_in, epilogue, outs, *, tm, name, dep=None):
    m = prods[0][0].shape[0]
    n = prods[0][1].shape[0] if prods[0][2] else prods[0][1].shape[1]
    tm = min(tm, m)
    assert m % tm == 0
    deps = [] if dep is None else [dep]
    n_p, n_r, n_v = len(prods), len(rows_in), len(vecs_in)

    def body(*refs):
        ab = refs[:2 * n_p]
        row_refs = refs[2 * n_p:2 * n_p + n_r]
        vec_refs = refs[2 * n_p + n_r:2 * n_p + n_r + n_v]
        out_refs = refs[2 * n_p + n_r + n_v + len(deps):]
        p = None
        for j, (_, _, tb) in enumerate(prods):
            t = _dot(ab[2 * j][...].astype(BF16), ab[2 * j + 1][...], 1, 1 if tb else 0)
            p = t if p is None else p + t
        vals = epilogue(p, *[r[...] for r in row_refs], *[v[...] for v in vec_refs])
        for (dtype, kind), o_ref, val in zip(outs, out_refs, vals):
            if kind == "row":
                o_ref[...] = val.astype(dtype)
            else:
                @pl.when(pl.program_id(0) == 0)
                def _(o_ref=o_ref):
                    o_ref[...] = jnp.zeros_like(o_ref)

                o_ref[...] += val

    row = lambda w: pl.BlockSpec((tm, w), lambda i: (i, 0))
    whole = lambda a: pl.BlockSpec(a.shape, lambda i: (0,) * a.ndim, pipeline_mode=pl.Buffered(1))
    in_specs, args = [], []
    for a, b, _ in prods:
        in_specs += [row(a.shape[1]), whole(b)]
        args += [a, b]
    in_specs += [row(r.shape[1]) for r in rows_in] + [whole(v) for v in vecs_in] + [ANY_SPEC] * len(deps)
    return pl.pallas_call(
        body, name=name,
        out_shape=tuple(jax.ShapeDtypeStruct((m, n) if kind == "row" else (8, n), dtype) for dtype, kind in outs),
        grid=(m // tm,), in_specs=in_specs,
        out_specs=tuple(row(n) if kind == "row" else pl.BlockSpec((8, n), lambda i: (0, 0)) for _, kind in outs),
        compiler_params=_params(dimension_semantics=("arbitrary",)),
    )(*args, *rows_in, *vecs_in, *deps)


def _rstd(x):
    return lax.rsqrt(jnp.mean(x * x, axis=-1, keepdims=True) + EPS)


def _norm_bwd(xh, r, t):
    return r * (t - xh * jnp.mean(xh * t, axis=-1, keepdims=True))


ROW_F32, ROW_BF16, SUM_F32 = (F32, "row"), (BF16, "row"), (F32, "sum")


def _ep_post_pre(p, h, g_post, g_pre):
    y = p.astype(BF16)
    yf = y.astype(F32)
    hn = h + yf * _rstd(yf) * g_post
    return y, hn, hn * _rstd(hn) * g_pre


_EP_POST_PRE_OUTS = [ROW_BF16, ROW_F32, ROW_BF16]


def _ep_final_loss(y, h, target, g_post):
    r = _rstd(y)
    yh = y * r
    err = h + yh * g_post - target
    dh = err * (1.0 / D)
    return _rowsum8(err * err), dh, _norm_bwd(yh, r, dh * g_post), _rowsum8(dh * yh)


_EP_FINAL_LOSS_OUTS = [SUM_F32, ROW_F32, ROW_BF16, SUM_F32]


def _ep_post_pre_bwd(du, dh_out, hn, y, g_post, g_pre):
    r2 = _rstd(hn)
    xh = hn * r2
    dh = dh_out + _norm_bwd(xh, r2, du * g_pre)
    yf = y.astype(F32)
    r1 = _rstd(yf)
    yh = yf * r1
    return dh, _norm_bwd(yh, r1, dh * g_post), _rowsum8(du * xh), _rowsum8(dh * yh)


_EP_POST_PRE_BWD_OUTS = [ROW_F32, ROW_BF16, SUM_F32, SUM_F32]


def _ep_pre_bwd(du, dh_out, x, g):
    r = _rstd(x)
    xh = x * r
    return dh_out + _norm_bwd(xh, r, du * g), _rowsum8(du * xh)


_EP_PRE_BWD_OUTS = [ROW_F32, SUM_F32]


def _prenorm(x, g, *, name, dep=None):
    t, d = x.shape
    tb = min(512, t)
    deps = [] if dep is None else [dep]

    def body(x_ref, g_ref, *rest):
        xf = x_ref[...]
        rest[-1][...] = (xf * _rstd(xf) * g_ref[...]).astype(BF16)

    return pl.pallas_call(
        body, name=name, out_shape=jax.ShapeDtypeStruct((t, d), BF16), grid=(t // tb,),
        in_specs=[pl.BlockSpec((tb, d), lambda i: (i, 0)), pl.BlockSpec((1, d), lambda i: (0, 0))]
        + [ANY_SPEC] * len(deps),
        out_specs=pl.BlockSpec((tb, d), lambda i: (i, 0)), compiler_params=_params(),
    )(x, g, *deps)


QB = 256


def _half_mask(shape, e):
    lane = lax.broadcasted_iota(jnp.int32, shape, len(shape) - 1)
    return (lane // 64) == e


def _place(kv):
    sw = pltpu.roll(kv, 64, 1)
    m0 = _half_mask(kv.shape, 0)
    return [[jnp.where(m0, kv, 0.0).astype(BF16), jnp.where(m0, 0.0, sw).astype(BF16)],
            [jnp.where(m0, sw, 0.0).astype(BF16), jnp.where(m0, 0.0, kv).astype(BF16)]]


SQ = 128
SK = 256


def _swa_valid(i, sb):
    qc = lax.broadcasted_iota(jnp.int32, (SQ, SK), 0) // CHUNK
    kc = lax.broadcasted_iota(jnp.int32, (SQ, SK), 1) // CHUNK - 2
    return (kc <= qc) & (qc <= kc + 2) & (4 * i + 2 * sb + kc >= 0)


def _swa_fwd(z, sinks, t, dep=None):
    nb = t // QB
    deps = [] if dep is None else [dep]

    def body(s_ref, q_ref, kp_ref, kc_ref, vp_ref, vc_ref, *rest):
        o_ref, lse_ref = rest[-2:]
        i = pl.program_id(0)
        kpl = _place(jnp.concatenate([kp_ref[...], kc_ref[...]], axis=0))
        vpl = _place(jnp.concatenate([vp_ref[...], vc_ref[...]], axis=0))
        lane = lax.broadcasted_iota(jnp.int32, (SQ, 128), 1)
        for sb in range(QB // SQ):
            rows, keys = slice(SQ * sb, SQ * (sb + 1)), slice(SQ * sb, SQ * sb + SK)
            valid = _swa_valid(i, sb)
            lse_out = jnp.zeros((SQ, 128), F32)
            for j in range(4):
                qp = q_ref[rows, 128 * j:128 * (j + 1)].astype(BF16)
                acc = jnp.zeros((SQ, 128), F32)
                for e in range(2):
                    h = 2 * j + e
                    kvh = h // 4
                    qm = jnp.where(_half_mask(qp.shape, e), qp, jnp.zeros_like(qp))
                    s = _dot(qm, kpl[kvh][e][keys], 1, 1) * 0.125
                    s = jnp.where(valid, s, NEG)
                    sink = s_ref[0, h]
                    m = jnp.maximum(jnp.max(s, axis=-1, keepdims=True), sink)
                    p = jnp.exp(s - m)
                    l = jnp.sum(p, axis=-1, keepdims=True) + jnp.exp(sink - m)
                    acc = acc + _dot(p.astype(BF16), vpl[kvh][e][keys], 1, 0) * (1.0 / l)
                    lse_out = jnp.where(lane == h, m + jnp.log(l), lse_out)
                o_ref[rows, 128 * j:128 * (j + 1)] = acc.astype(BF16)
            lse_ref[rows, :] = lse_out

    prev = lambda c: pl.BlockSpec((128, 128), lambda i: (jnp.maximum(2 * i - 1, 0), c))
    cur = lambda c: pl.BlockSpec((QB, 128), lambda i: (i, c))
    return pl.pallas_call(
        body, name="swa_fwd",
        out_shape=(jax.ShapeDtypeStruct((t, D), BF16), jax.ShapeDtypeStruct((t, 128), F32)),
        grid=(nb,),
        in_specs=[pl.BlockSpec(memory_space=pltpu.SMEM),
                  pl.BlockSpec((QB, SWA_W), lambda i: (i, 0)), prev(4), cur(4), prev(5), cur(5)]
        + [ANY_SPEC] * len(deps),
        out_specs=(pl.BlockSpec((QB, SWA_W), lambda i: (i, 0)), pl.BlockSpec((QB, 128), lambda i: (i, 0))),
        compiler_params=_params(),
    )(sinks, z, z, z, z, z, *deps)


def _swa_bwd(z, sinks, ymix, lse, dymix, t):
    nb = t // QB

    def body(s_ref, q_ref, kp_ref, kc_ref, vp_ref, vc_ref, o_ref, do_ref, l_ref,
             dq_ref, first_ref, second_ref, ds_ref, carry_ref):
        i = pl.program_id(0)
        live = i < nb

        @pl.when(i == 0)
        def _():
            ds_ref[...] = jnp.zeros_like(ds_ref)
            carry_ref[...] = jnp.zeros_like(carry_ref)

        lane = lax.broadcasted_iota(jnp.int32, (8, 128), 1)
        kpl = _place(jnp.concatenate([kp_ref[...], kc_ref[...]], axis=0))
        vpl = _place(jnp.concatenate([vp_ref[...], vc_ref[...]], axis=0))
        nk = QB + 128
        qc = lax.broadcasted_iota(jnp.int32, (QB, nk), 0) // CHUNK
        kc = lax.broadcasted_iota(jnp.int32, (QB, nk), 1) // CHUNK - 2
        valid = (kc <= qc) & (qc <= kc + 2) & (4 * i + kc >= 0) & live
        lse_c = l_ref[...]
        dsink = jnp.zeros((8, 128), F32)
        dk_acc = [[jnp.zeros((nk, 128), F32) for _ in range(2)] for _ in range(2)]
        dv_acc = [[jnp.zeros((nk, 128), F32) for _ in range(2)] for _ in range(2)]
        dq = []
        for j in range(4):
            cols = slice(128 * j, 128 * (j + 1))
            qp = q_ref[:, cols].astype(BF16)
            dop = do_ref[:, cols]
            prod = dop.astype(F32) * o_ref[:, cols].astype(F32)
            acc = jnp.zeros((QB, 128), F32)
            for e in range(2):
                h = 2 * j + e
                kvh = h // 4
                hm = _half_mask(qp.shape, e)
                qm = jnp.where(hm, qp, jnp.zeros_like(qp))
                dom = jnp.where(hm, dop, jnp.zeros_like(dop))
                dd = jnp.sum(jnp.where(hm, prod, 0.0), axis=-1, keepdims=True)
                lse_h = lse_c[:, h:h + 1]
                s = _dot(qm, kpl[kvh][e], 1, 1) * 0.125
                p = jnp.where(valid, jnp.exp(s - lse_h), 0.0)
                dp = _dot(dom, vpl[kvh][e], 1, 1)
                ds = (p * (dp - dd) * 0.125).astype(BF16)
                acc = acc + _dot(ds, kpl[kvh][e], 1, 0)
                dk_acc[kvh][e] = dk_acc[kvh][e] + _dot(ds, qm, 0, 0)
                dv_acc[kvh][e] = dv_acc[kvh][e] + _dot(p.astype(BF16), dom, 0, 0)
                ps = jnp.where(live, jnp.exp(s_ref[0, h] - lse_h) * dd, 0.0)
                dsink = dsink - jnp.where(lane == h, _rowsum8(jnp.broadcast_to(ps, (QB, 128))), 0.0)
            dq.append(acc.astype(BF16))
        ds_ref[...] += dsink
        dk = dk_acc[0][0] + dk_acc[1][1] + pltpu.roll(dk_acc[0][1] + dk_acc[1][0], 64, 1)
        dv = dv_acc[0][0] + dv_acc[1][1] + pltpu.roll(dv_acc[0][1] + dv_acc[1][0], 64, 1)
        dkv = jnp.concatenate([dk, dv], axis=1)
        second_ref[...] = (carry_ref[...] + dkv[0:128]).astype(BF16)
        carry_ref[...] = dkv[256:384]

        @pl.when(live)
        def _():
            for j in range(4):
                dq_ref[:, 128 * j:128 * (j + 1)] = dq[j]
            first_ref[...] = dkv[128:256].astype(BF16)

    blk = lambda i: jnp.minimum(i, nb - 1)
    prev = lambda c: pl.BlockSpec((128, 128), lambda i: (jnp.maximum(2 * blk(i) - 1, 0), c))
    cur = lambda w, c: pl.BlockSpec((QB, w), lambda i: (blk(i), c))
    half = lambda index: pl.BlockSpec((128, 256), lambda i: (index(i), 0))
    return pl.pallas_call(
        body, name="swa_bwd",
        out_shape=(jax.ShapeDtypeStruct((t, SWA_W), BF16), jax.ShapeDtypeStruct((t // 2, 256), BF16),
                   jax.ShapeDtypeStruct((t // 2, 256), BF16), jax.ShapeDtypeStruct((8, 128), F32)),
        grid=(nb + 1,),
        in_specs=[pl.BlockSpec(memory_space=pltpu.SMEM),
                  cur(SWA_W, 0), prev(4), cur(128, 4), prev(5), cur(128, 5),
                  cur(SWA_W, 0), cur(SWA_W, 0), cur(128, 0)],
        out_specs=(cur(SWA_W, 0), half(blk), half(lambda i: jnp.maximum(i - 1, 0)),
                   pl.BlockSpec((8, 128), lambda i: (0, 0))),
        scratch_shapes=[pltpu.VMEM((128, 256), F32)],
        compiler_params=_params(dimension_semantics=("arbitrary",)),
    )(sinks, z, z, z, z, z, ymix, dymix, lse)


HB = 256


def _lower_bound(lb_ref):
    a = lb_ref[...]
    a0, a1 = a[0:1], a[1:2]
    mx = jnp.maximum(a0, a1)
    e0, e1 = jnp.exp(a0 - mx), jnp.exp(a1 - mx)
    return e0 / (e0 + e1)


def _hgrn_cols(row_block):
    return [pl.BlockSpec((HB, 2 * HD), lambda j, c=base // (2 * HD) + p: (row_block(j), c))
            for base in (ZQH, ZFH, ZIH, ZGH) for p in range(2)]


NCH = HB // CHUNK


def _split3(x):
    hi = x.astype(BF16)
    r1 = x - hi.astype(F32)
    mid = r1.astype(BF16)
    return hi, mid, (r1 - mid.astype(F32)).astype(BF16)


def _blockdiag(lower):
    r = lax.broadcasted_iota(jnp.int32, (HB, HB), 0)
    c = lax.broadcasted_iota(jnp.int32, (HB, HB), 1)
    return (r // CHUNK == c // CHUNK) & ((c <= r) if lower else (c >= r))


def _chunk_sums(mask_bf16, x):
    return sum(_dot(mask_bf16, part, 1, 0) for part in _split3(x))


def _per_chunk_rows(x, row):
    w = x.shape[1]
    picked = x.reshape(NCH, CHUNK, w)[:, row:row + 1, :]
    return jnp.broadcast_to(picked, (NCH, CHUNK, w)).reshape(HB, w)


def _chunk_stack(x, chunk_of_row):
    return jnp.concatenate([jnp.where(chunk_of_row == c, x, jnp.zeros_like(x)) for c in range(NCH)], axis=1)


def _chunk_pick(x, chunk_of_row):
    w = x.shape[1] // NCH
    out = jnp.zeros((HB, w), x.dtype)
    for c in range(NCH):
        out = jnp.where(chunk_of_row == c, x[:, c * w:(c + 1) * w], out)
    return out


def _hgrn_local(q, f, kf, b):
    sq = _sig(q)
    qf = q * sq * (HD ** -0.5)
    b_mid = _per_chunk_rows(b, CHUNK // 2 - 1)
    b_last = _per_chunk_rows(b, CHUNK - 1)
    qm = qf * jnp.exp(b - b_mid)
    km = kf * jnp.exp(b_mid - b)
    kl = kf * jnp.exp(b_last - b)
    qb = qf * jnp.exp(b)
    return dict(sq=sq, b_mid=b_mid, b_last=b_last, qm=qm, km=km, kl=kl, qb=qb)


def _hgrn2_fwd(z, hgrn_lb, onorm, ymix, t, dep=None):
    nb = t // HB
    deps = [] if dep is None else [dep]

    def body(*refs):
        zq, zf, zi, zg = refs[0:2], refs[2:4], refs[4:6], refs[6:8]
        (lb_ref, on_ref), (y_ref, o_ref, sp_ref, st_ref) = refs[8:10], refs[-4:]

        @pl.when(pl.program_id(0) == 0)
        def _():
            st_ref[...] = jnp.zeros_like(st_ref)

        lb_all = _lower_bound(lb_ref)
        gn = on_ref[...]
        low = _blockdiag(True)
        low_b = low.astype(BF16)
        chunk_of_row = lax.broadcasted_iota(jnp.int32, (HB, HD), 0) // CHUNK
        for p in range(2):
            lbp = lb_all[:, 2 * HD * p:2 * HD * (p + 1)]
            fp = lbp + (1.0 - lbp) * _sig(zf[p][...])
            bp = _chunk_sums(low_b, jnp.log(fp))
            for e in range(2):
                h, ls = 2 * p + e, slice(e * HD, (e + 1) * HD)
                f = fp[:, ls]
                w = _hgrn_local(zq[p][:, ls], f, 1.0 - f, bp[:, ls])
                iv = zi[p][:, ls].astype(BF16)
                a = jnp.where(low, _dot(w["qm"].astype(BF16), w["km"].astype(BF16), 1, 1), 0.0)
                o = _dot(a.astype(BF16), iv, 1, 0)
                u = _dot(iv, _chunk_stack(w["kl"].astype(BF16), chunk_of_row), 0, 0)
                decay = jnp.exp(w["b_last"])
                st = st_ref[h]
                states = []
                for c in range(NCH):
                    sp_ref[h, c] = st
                    states.append(st.astype(BF16))
                    st = st * decay[c * CHUNK:c * CHUNK + 1] + u[:, c * HD:(c + 1) * HD]
                st_ref[h] = st
                inter = _dot(w["qb"].astype(BF16), jnp.concatenate(states, axis=0), 1, 1)
                o = o + _chunk_pick(inter, chunk_of_row)
                hs = slice(h * HD, (h + 1) * HD)
                o_ref[:, hs] = o
                gg = zg[p][:, ls]
                y_ref[:, hs] = (o * _rstd(o) * gn * (gg * _sig(gg))).astype(BF16)

    return pl.pallas_call(
        body, name="hgrn_fwd",
        out_shape=(jax.ShapeDtypeStruct((t, D), BF16), jax.ShapeDtypeStruct((t, HG_W), F32),
                   jax.ShapeDtypeStruct((4, t // CHUNK, HD, HD), F32)),
        grid=(nb,),
        in_specs=_hgrn_cols(lambda j: j) + [pl.BlockSpec((2, HG_W), lambda j: (0, 0)),
                                            pl.BlockSpec((1, HD), lambda j: (0, 0)), ANY_SPEC]
        + [ANY_SPEC] * len(deps),
        out_specs=(pl.BlockSpec((HB, HG_W), lambda j: (j, 1)),
                   pl.BlockSpec((HB, HG_W), lambda j: (j, 0)),
                   pl.BlockSpec((4, NCH, HD, HD), lambda j: (0, j, 0, 0))),
        scratch_shapes=[pltpu.VMEM((4, HD, HD), F32)],
        input_output_aliases={10: 0},
        compiler_params=_params(dimension_semantics=("arbitrary",)),
    )(*[z] * 8, hgrn_lb, onorm, ymix, *deps)


def _hgrn2_bwd(z, hgrn_lb, onorm, o_save, sprev, dymix, dza, t):
    nb = t // HB

    def body(*refs):
        zq, zf, zi, zg = refs[0:2], refs[2:4], refs[4:6], refs[6:8]
        (lb_ref, on_ref, o_ref, sp_ref, dy_ref, dqa_ref, first_ref, second_ref,
         dz_ref, dlb_ref, don_ref, dst_ref) = refs[8:]

        @pl.when(pl.program_id(0) == 0)
        def _():
            dst_ref[...] = jnp.zeros_like(dst_ref)
            dlb_ref[...] = jnp.zeros_like(dlb_ref)
            don_ref[...] = jnp.zeros_like(don_ref)

        dz_ref[:, 0:SWA_W] = dqa_ref[...]
        dz_ref[0:HB // 2, SWA_W:ZQH] = first_ref[...]
        dz_ref[HB // 2:HB, SWA_W:ZQH] = second_ref[...]
        lb_all = _lower_bound(lb_ref)
        gn = on_ref[...]
        low, upp = _blockdiag(True), _blockdiag(False)
        upp_b = upp.astype(BF16)
        low_b = low.astype(BF16)
        row = lax.broadcasted_iota(jnp.int32, (HB, HD), 0)
        chunk_of_row = row // CHUNK
        in_chunk = row % CHUNK
        for p in range(2):
            lbp = lb_all[:, 2 * HD * p:2 * HD * (p + 1)]
            sgp = _sig(zf[p][...])
            fp = lbp + (1.0 - lbp) * sgp
            bp = _chunk_sums(low_b, jnp.log(fp))
            db_pair, dkf_pair = [], []
            for e in range(2):
                h, ls, hs = 2 * p + e, slice(e * HD, (e + 1) * HD), slice((2 * p + e) * HD, (2 * p + e + 1) * HD)
                f = fp[:, ls]
                q = zq[p][:, ls]
                w = _hgrn_local(q, f, 1.0 - f, bp[:, ls])
                iv = zi[p][:, ls].astype(BF16)
                gg = zg[p][:, ls]
                o = o_ref[:, hs]
                dout = dy_ref[:, hs].astype(F32)
                sgg = _sig(gg)
                r = _rstd(o)
                oh = o * r
                dyn = dout * (gg * sgg)
                dz_ref[:, ZGH + h * HD:ZGH + (h + 1) * HD] = (
                    dout * oh * gn * (sgg * (1.0 + gg * (1.0 - sgg)))).astype(BF16)
                don_ref[...] += _rowsum8(dyn * oh)
                do = _norm_bwd(oh, r, dyn * gn).astype(BF16)
                qm, km, kl, qb = (w[n].astype(BF16) for n in ("qm", "km", "kl", "qb"))
                decay = jnp.exp(w["b_last"])
                grads_in = _dot(do, _chunk_stack(qb, chunk_of_row), 0, 0)
                dst = dst_ref[h]
                dstn, dd_rows = [None] * NCH, [None] * NCH
                for c in reversed(range(NCH)):
                    dstn[c] = dst.astype(BF16)
                    dd_rows[c] = jnp.sum(dst * sp_ref[h, c], axis=0, keepdims=True)
                    dst = dst * decay[c * CHUNK:c * CHUNK + 1] + grads_in[:, c * HD:(c + 1) * HD]
                dst_ref[h] = dst
                states = jnp.concatenate([sp_ref[h, c].astype(BF16) for c in range(NCH)], axis=0)
                dstn_all = jnp.concatenate(dstn, axis=0)
                dqb = _dot(_chunk_stack(do, chunk_of_row), states, 1, 0)
                at = jnp.where(upp, _dot(km, qm, 1, 1), 0.0)
                di = _dot(at.astype(BF16), do, 1, 0) + _chunk_pick(_dot(kl, dstn_all, 1, 1), chunk_of_row)
                dz_ref[:, ZIH + h * HD:ZIH + (h + 1) * HD] = di.astype(BF16)
                dkl = _dot(_chunk_stack(iv, chunk_of_row), dstn_all, 1, 0)
                da = jnp.where(low, _dot(do, iv, 1, 1), 0.0).astype(BF16)
                dat = jnp.where(upp, _dot(iv, do, 1, 1), 0.0).astype(BF16)
                dqm = _dot(da, km, 1, 0)
                dkm = _dot(dat, qm, 1, 0)
                b = bp[:, ls]
                e1, e2 = jnp.exp(b - w["b_mid"]), jnp.exp(w["b_mid"] - b)
                e3, e4 = jnp.exp(w["b_last"] - b), jnp.exp(b)
                dqf = dqm * e1 + dqb * e4
                dkf_pair.append(dkm * e2 + dkl * e3)
                t_qm, t_km, t_kl = dqm * w["qm"], dkm * w["km"], dkl * w["kl"]
                db = t_qm - t_km - t_kl + dqb * w["qb"]
                db_mid = jnp.sum((t_km - t_qm).reshape(NCH, CHUNK, HD), axis=1, keepdims=True)
                db_last = jnp.sum(t_kl.reshape(NCH, CHUNK, HD), axis=1, keepdims=True)
                db_last = db_last + jnp.stack(dd_rows, axis=0) * jnp.exp(
                    bp[:, ls].reshape(NCH, CHUNK, HD)[:, CHUNK - 1:CHUNK, :])
                spread = lambda v: jnp.broadcast_to(v, (NCH, CHUNK, HD)).reshape(HB, HD)
                db = (db + jnp.where(in_chunk == CHUNK // 2 - 1, spread(db_mid), 0.0)
                      + jnp.where(in_chunk == CHUNK - 1, spread(db_last), 0.0))
                db_pair.append(db)
                sq = w["sq"]
                dz_ref[:, ZQH + h * HD:ZQH + (h + 1) * HD] = (
                    dqf * (HD ** -0.5) * (sq * (1.0 + q * (1.0 - sq)))).astype(BF16)
            dlogf = _chunk_sums(upp_b, jnp.concatenate(db_pair, axis=1))
            dfv = dlogf / fp - jnp.concatenate(dkf_pair, axis=1)
            dz_ref[:, ZFH + 2 * HD * p:ZFH + 2 * HD * (p + 1)] = (dfv * (1.0 - lbp) * sgp * (1.0 - sgp)).astype(BF16)
            dlb_ref[:, 2 * HD * p:2 * HD * (p + 1)] += _rowsum8(dfv * (1.0 - sgp))

    rev = lambda j: nb - 1 - j
    return pl.pallas_call(
        body, name="hgrn_bwd",
        out_shape=(jax.ShapeDtypeStruct((t, D_IN), BF16), jax.ShapeDtypeStruct((8, HG_W), F32),
                   jax.ShapeDtypeStruct((8, HD), F32)),
        grid=(nb,),
        in_specs=_hgrn_cols(rev) + [pl.BlockSpec((2, HG_W), lambda j: (0, 0)), pl.BlockSpec((1, HD), lambda j: (0, 0)),
                                    pl.BlockSpec((HB, HG_W), lambda j: (rev(j), 0)),
                                    pl.BlockSpec((4, NCH, HD, HD), lambda j: (0, rev(j), 0, 0)),
                                    pl.BlockSpec((HB, HG_W), lambda j: (rev(j), 1)),
                                    pl.BlockSpec((HB, SWA_W), lambda j: (rev(j), 0)),
                                    pl.BlockSpec((HB // 2, 2 * KV_W), lambda j: (rev(j), 0)),
                                    pl.BlockSpec((HB // 2, 2 * KV_W), lambda j: (rev(j), 0))],
        out_specs=(pl.BlockSpec((HB, D_IN), lambda j: (rev(j), 0)), pl.BlockSpec((8, HG_W), lambda j: (0, 0)),
                   pl.BlockSpec((8, HD), lambda j: (0, 0))),
        scratch_shapes=[pltpu.VMEM((4, HD, HD), F32)],
        compiler_params=_params(dimension_semantics=("arbitrary",)),
    )(*[z] * 8, hgrn_lb, onorm, o_save, sprev, dymix, *dza)


XB = 512


def _xattn_fwd(q, k, v, t):
    tb = min(XB, t)

    def body(q_ref, k_ref, v_ref, o_ref):
        for h in range(XH):
            cols = slice(XD * h, XD * (h + 1))
            s = _dot(q_ref[:, cols], k_ref[:, cols], 1, 1) * (XD ** -0.5)
            p = jnp.exp(s - jnp.max(s, axis=-1, keepdims=True))
            l = jnp.sum(p, axis=-1, keepdims=True)
            o_ref[:, cols] = (_dot(p.astype(BF16), v_ref[:, cols], 1, 0) * (1.0 / l)).astype(BF16)

    row = pl.BlockSpec((tb, D), lambda i: (i, 0))
    mem = pl.BlockSpec(k.shape, lambda i: (0, 0))
    return pl.pallas_call(
        body, name="xattn_fwd", out_shape=jax.ShapeDtypeStruct((t, D), BF16), grid=(t // tb,),
        in_specs=[row, mem, mem], out_specs=row, compiler_params=_params(),
    )(q, k, v)


def _xattn_bwd(q, k, v, do, t):
    tb = min(XB, t)

    def body(q_ref, k_ref, v_ref, do_ref, dq_ref, dk_ref, dv_ref):
        @pl.when(pl.program_id(0) == 0)
        def _():
            dk_ref[...] = jnp.zeros_like(dk_ref)
            dv_ref[...] = jnp.zeros_like(dv_ref)

        for h in range(XH):
            cols = slice(XD * h, XD * (h + 1))
            qh, kh, vh, doh = q_ref[:, cols], k_ref[:, cols], v_ref[:, cols], do_ref[:, cols]
            s = _dot(qh, kh, 1, 1) * (XD ** -0.5)
            p = jnp.exp(s - jnp.max(s, axis=-1, keepdims=True))
            p = p * (1.0 / jnp.sum(p, axis=-1, keepdims=True))
            dp = _dot(doh, vh, 1, 1)
            ds = (p * (dp - jnp.sum(p * dp, axis=-1, keepdims=True)) * (XD ** -0.5)).astype(BF16)
            dq_ref[:, cols] = _dot(ds, kh, 1, 0).astype(BF16)
            dk_ref[:, cols] += _dot(ds, qh, 0, 0)
            dv_ref[:, cols] += _dot(p.astype(BF16), doh, 0, 0)

    row = pl.BlockSpec((tb, D), lambda i: (i, 0))
    mem = pl.BlockSpec(k.shape, lambda i: (0, 0))
    return pl.pallas_call(
        body, name="xattn_bwd",
        out_shape=(jax.ShapeDtypeStruct((t, D), BF16), jax.ShapeDtypeStruct(k.shape, F32),
                   jax.ShapeDtypeStruct(k.shape, F32)),
        grid=(t // tb,), in_specs=[row, mem, mem, row], out_specs=(row, mem, mem),
        compiler_params=_params(dimension_semantics=("arbitrary",)),
    )(q, k, v, do)


def _mem_kv(mem, g_mem, wk, wv):
    def body(m_ref, g_ref, wk_ref, wv_ref, mn_ref, k_ref, v_ref):
        m_ = m_ref[...]
        mn = (m_ * _rstd(m_) * g_ref[...]).astype(BF16)
        mn_ref[...] = mn
        k_ref[...] = _dot(mn, wk_ref[...], 1, 0).astype(BF16)
        v_ref[...] = _dot(mn, wv_ref[...], 1, 0).astype(BF16)

    return pl.pallas_call(body, name="mem_kv", out_shape=(jax.ShapeDtypeStruct(mem.shape, BF16),) * 3,
                          compiler_params=_params())(mem, g_mem, wk, wv)


def _mem_kv_bwd(mn, mem, dk, dv, wk, wv, dep=None):
    deps = [] if dep is None else [dep]

    def body(mn_ref, m_ref, dk_ref, dv_ref, wk_ref, wv_ref, *rest):
        gk_ref, gv_ref, dg_ref = rest[len(deps):]
        mn = mn_ref[...]
        dkb, dvb = dk_ref[...].astype(BF16), dv_ref[...].astype(BF16)
        gk_ref[...] = _dot(mn, dkb, 0, 0).astype(BF16)
        gv_ref[...] = _dot(mn, dvb, 0, 0).astype(BF16)
        dmn = _dot(dkb, wk_ref[...], 1, 1) + _dot(dvb, wv_ref[...], 1, 1)
        m_ = m_ref[...]
        dg_ref[...] = _rowsum8(dmn * (m_ * _rstd(m_)))

    vmem = pl.BlockSpec(memory_space=pltpu.VMEM)
    return pl.pallas_call(
        body, name="mem_kv_bwd",
        out_shape=(jax.ShapeDtypeStruct(wk.shape, BF16), jax.ShapeDtypeStruct(wv.shape, BF16),
                   jax.ShapeDtypeStruct((8, D), F32)),
        in_specs=[vmem] * 6 + [ANY_SPEC] * len(deps), out_specs=(vmem,) * 3, compiler_params=_params(),
    )(mn, mem, dk, dv, wk, wv, *deps)


FM, FN = 512, 1408


def _ffn_up(u, wgt, wut, t):
    tm = min(FM, t)

    def body(u_ref, wg_ref, wu_ref, g_ref, up_ref, a_ref):
        u_ = u_ref[...]
        g = _dot(u_, wg_ref[...], 1, 1)
        up = _dot(u_, wu_ref[...], 1, 1)
        g_ref[...] = g.astype(BF16)
        up_ref[...] = up.astype(BF16)
        a_ref[...] = (g * _sig(g) * up).astype(BF16)

    w = pl.BlockSpec((FN, D), lambda j, i: (j, 0))
    o = pl.BlockSpec((tm, FN), lambda j, i: (i, j))
    return pl.pallas_call(
        body, name="ffn_up", out_shape=(jax.ShapeDtypeStruct((t, D_FF), BF16),) * 3,
        grid=(D_FF // FN, t // tm), in_specs=[pl.BlockSpec((tm, D), lambda j, i: (i, 0)), w, w],
        out_specs=(o, o, o), compiler_params=_params(),
    )(u, wgt, wut)


def _ffn_down_bwd(dy, wd, gate, up, t, dep=None):
    tm = min(FM, t)
    deps = [] if dep is None else [dep]

    def body(dy_ref, w_ref, g_ref, up_ref, *rest):
        dg_ref, dup_ref = rest[len(deps):]
        da = _dot(dy_ref[...], w_ref[...], 1, 1)
        g = g_ref[...].astype(F32)
        sg = _sig(g)
        dup_ref[...] = (da * g * sg).astype(BF16)
        dg_ref[...] = (da * up_ref[...].astype(F32) * (sg * (1.0 + g * (1.0 - sg)))).astype(BF16)

    o = pl.BlockSpec((tm, FN), lambda j, i: (i, j))
    return pl.pallas_call(
        body, name="ffn_down_bwd", out_shape=(jax.ShapeDtypeStruct((t, D_FF), BF16),) * 2,
        grid=(D_FF // FN, t // tm),
        in_specs=[pl.BlockSpec((tm, D), lambda j, i: (i, 0)), pl.BlockSpec((FN, D), lambda j, i: (j, 0)), o, o]
        + [ANY_SPEC] * len(deps),
        out_specs=(o, o), compiler_params=_params(),
    )(dy, wd, gate, up, *deps)


def _local_step(x, mem, target, fetch, sm, emit=None, first_dep=None, milestone=None):
    t = x.shape[0]
    w, gw = {}, {}

    def out(key, g):
        gw[key] = g
        return None if emit is None else emit(key, g)

    def tell(tag, value):
        return None if milestone is None else milestone(tag, value)
    u1 = _prenorm(x, sm["g_mix_pre"], name="prenorm_mix", dep=first_dep)
    w["winT"] = fetch("winT", u1)
    z = _mm(u1, w["winT"], tb=True, out_dtype=F32, tm=1024, tn=1408, name="mm_z", n_outer=True)
    ymix, lse = _swa_fwd(z, sm["sinks"], t, dep=tell("z", z))
    ymix, o_h, sprev = _hgrn2_fwd(z, sm["hgrn_lb"], sm["hgrn_onorm"], ymix, t, dep=tell("swa", lse))
    w["wout"] = fetch("wout", ymix)
    y1, h1, u2 = _mm_rows([(ymix, w["wout"], False)], [x], [sm["g_mix_post"], sm["g_x_pre"]], _ep_post_pre,
                          _EP_POST_PRE_OUTS, tm=512, name="mm_y1_post")
    for key in ("wq", "wk", "wv"):
        w[key] = fetch(key, u2)
    qx = _mm(u2, w["wq"], out_dtype=BF16, tm=1024, tn=1024, name="mm_qx")
    mn, kx, vx = _mem_kv(mem, sm["g_mem"], w["wk"], w["wv"])
    ox = _xattn_fwd(qx, kx, vx, t)
    w["wo"] = fetch("wo", ox)
    y2, h2, u3 = _mm_rows([(ox, w["wo"], False)], [h1], [sm["g_x_post"], sm["g_ffn_pre"]], _ep_post_pre,
                          _EP_POST_PRE_OUTS, tm=512, name="mm_y2_post", dep=tell("ox", ox))
    w["wgT"], w["wuT"] = fetch("wgT", u3), fetch("wuT", u3)
    gate, up, act = _ffn_up(u3, w["wgT"], w["wuT"], t)
    w["wd"] = fetch("wd", act)
    sq, dh3, dy3, dg_ffn_post = _mm_rows([(act, w["wd"], False)], [h2, target], [sm["g_ffn_post"]], _ep_final_loss,
                                         _EP_FINAL_LOSS_OUTS, tm=512, name="mm_y3_loss")
    dep = out("wd", _mm(act, dy3, ta=True, out_dtype=BF16, tm=1408, tn=1024, name="mm_gwd"))
    dgate, dup = _ffn_down_bwd(dy3, w["wd"], gate, up, t, dep=dep)
    dep = out("wgT", _mm(dgate, u3, ta=True, out_dtype=BF16, tm=1408, tn=1024, name="mm_gwg"))
    dep = out("wuT", _mm(dup, u3, ta=True, out_dtype=BF16, tm=1408, tn=1024, name="mm_gwu", dep=dep))
    dh2, dy2, dg_ffn_pre, dg_x_post = _mm_rows(
        [(dgate, w["wgT"], False), (dup, w["wuT"], False)], [dh3, h2, y2], [sm["g_x_post"], sm["g_ffn_pre"]],
        _ep_post_pre_bwd, _EP_POST_PRE_BWD_OUTS, tm=512, name="mm_du3_post_bwd", dep=dep)
    dep = out("wo", _mm(ox, dy2, ta=True, out_dtype=BF16, tm=512, tn=1024, name="mm_gwo"))
    dox = _mm(dy2, w["wo"], tb=True, out_dtype=BF16, tm=1024, tn=1024, name="mm_dox", dep=dep)
    dqx, dkx, dvx = _xattn_bwd(qx, kx, vx, dox, t)
    dep = out("wq", _mm(u2, dqx, ta=True, out_dtype=BF16, tm=512, tn=1024, name="mm_gwq"))
    gwk, gwv, dg_mem = _mem_kv_bwd(mn, mem, dkx, dvx, w["wk"], w["wv"], dep=dep)
    out("wk", gwk)
    dep = out("wv", gwv)
    dh1, dy1, dg_x_pre, dg_mix_post = _mm_rows(
        [(dqx, w["wq"], True)], [dh2, h1, y1], [sm["g_mix_post"], sm["g_x_pre"]],
        _ep_post_pre_bwd, _EP_POST_PRE_BWD_OUTS, tm=512, name="mm_du2_post_bwd", dep=dep)
    dep = out("wout", _mm(ymix, dy1, ta=True, out_dtype=BF16, tm=512, tn=1024, name="mm_gwout"))
    dymix = _mm(dy1, w["wout"], tb=True, out_dtype=BF16, tm=1024, tn=1024, name="mm_dymix", dep=dep)
    *dza, dsinks = _swa_bwd(z, sm["sinks"], ymix, lse, dymix, t)
    dz, dlb, donorm = _hgrn2_bwd(z, sm["hgrn_lb"], sm["hgrn_onorm"], o_h, sprev, dymix, dza, t)
    dep = out("winT", _mm(dz, u1, ta=True, out_dtype=BF16, tm=1408, tn=1024, name="mm_gwin"))
    grad_x, dg_mix_pre = _mm_rows([(dz, w["winT"], False)], [dh1, x], [sm["g_mix_pre"]], _ep_pre_bwd,
                                  _EP_PRE_BWD_OUTS, tm=512, name="mm_du1_pre_bwd", dep=dep)
    parts = dict(g_mix_pre=dg_mix_pre, g_mix_post=dg_mix_post, g_mem=dg_mem, g_x_pre=dg_x_pre,
                 g_x_post=dg_x_post, g_ffn_pre=dg_ffn_pre, g_ffn_post=dg_ffn_post,
                 hgrn_onorm=donorm, hgrn_lb=dlb, sinks=dsinks, sq=sq)
    return grad_x, gw, parts


def _position():
    return lax.axis_index("x"), lax.axis_index("y"), lax.axis_index("c")


def _peer(pos, k):
    x, y, c = pos
    return (1 - x if k & 4 else x, 1 - y if k & 2 else y, 1 - c if k & 1 else c)


def _linear(pos):
    x, y, c = pos
    return 4 * x + 2 * y + c


HBM_SPEC = pl.BlockSpec(memory_space=pltpu.HBM)
SEM_SPEC = pl.BlockSpec(memory_space=pltpu.SEMAPHORE)
DATAFLOW = pltpu.SideEffectType.DATAFLOW_SIDE_EFFECTING
SEND_ORDER = (1, 2, 4, 3, 5, 6, 7)


def _in_hbm(a):
    return pltpu.with_memory_space_constraint(a, pltpu.HBM)


def _prepare_weights(shards, *, name, dep=None):
    n = len(shards)
    deps = [] if dep is None else [dep]

    def body(*refs):
        ins, (outs, lands, sem) = refs[:n], (refs[-2 * n - 1:-n - 1], refs[-n - 1:-1], refs[-1])
        me_lin = _linear(_position())
        copies = []
        for a in range(n):
            r = ins[a].shape[0]
            outs[a][...] = ins[a][...].astype(BF16)
            copies.append(pltpu.make_async_copy(outs[a], lands[a].at[pl.ds(me_lin * r, r), :], sem.at[a]))
            copies[-1].start()
        for cp in copies:
            cp.wait()

    vmem = pl.BlockSpec(memory_space=pltpu.VMEM)
    res = pl.pallas_call(
        body, name=name,
        out_shape=tuple(jax.ShapeDtypeStruct(s.shape, BF16) for s in shards)
        + tuple(jax.ShapeDtypeStruct((N_DEV * s.shape[0], s.shape[1]), BF16) for s in shards),
        in_specs=[vmem] * n + [ANY_SPEC] * len(deps), out_specs=tuple([vmem] * n + [ANY_SPEC] * n),
        scratch_shapes=[pltpu.SemaphoreType.DMA((n,))], compiler_params=_params(),
    )(*shards, *deps)
    return res[:n], res[n:]


def _copies_start(arrays, plan, n, *, name):
    na = len(arrays)

    def body(*refs):
        ins, send_sems, recv_sems = refs[:na], refs[na], refs[na + 1]
        me = _position()
        for j in range(n):
            src, dst, peer, _ = plan(ins, me, j)
            pltpu.make_async_remote_copy(src_ref=src, dst_ref=dst, send_sem=send_sems.at[j], recv_sem=recv_sems.at[j],
                                         device_id=peer, device_id_type=MESH).start()

    return pl.pallas_call(
        body, name=name,
        out_shape=(pltpu.SemaphoreType.DMA((n,)), pltpu.SemaphoreType.DMA((n,)))
        + tuple(pltpu.HBM(a.shape, a.dtype) for a in arrays),
        in_specs=(HBM_SPEC,) * na, out_specs=(SEM_SPEC, SEM_SPEC) + (HBM_SPEC,) * na,
        input_output_aliases={i: 2 + i for i in range(na)},
        compiler_params=pltpu.CompilerParams(has_side_effects=DATAFLOW),
    )(*[_in_hbm(a) for a in arrays])


def _copies_wait(send_sems, recv_sems, arrays, plan, n, after, *, name):
    na = len(arrays)

    def body(*refs):
        ins, send_sems, recv_sems = refs[:na], refs[na], refs[na + 1]
        me = _position()
        for j in range(n):
            src, _, peer, landed = plan(ins, me, j)
            copy = pltpu.make_async_remote_copy(src_ref=src, dst_ref=landed, send_sem=send_sems.at[j],
                                                recv_sem=recv_sems.at[j], device_id=peer, device_id_type=MESH)
            copy.wait_send()
            copy.wait_recv()

    return pl.pallas_call(
        body, name=name, out_shape=tuple(pltpu.HBM(a.shape, a.dtype) for a in arrays),
        in_specs=(HBM_SPEC,) * na + (SEM_SPEC, SEM_SPEC, ANY_SPEC), out_specs=(HBM_SPEC,) * na,
        input_output_aliases={i: i for i in range(na)},
        compiler_params=pltpu.CompilerParams(has_side_effects=DATAFLOW),
    )(*arrays, send_sems, recv_sems, after)


SAME_CORE = (2, 4, 6)


class _TwoLevelGather:
    def __init__(self, shards, lands, *, name):
        n = self.n = len(shards)
        self.name = name
        first_peers = (1,) + SAME_CORE

        def rows(ref, pos):
            r = ref.shape[0] // N_DEV
            return ref.at[pl.ds(_linear(pos) * r, r), :]

        def first(refs, me, j):
            a, peer = j // 4, _peer(me, first_peers[j % 4])
            return refs[a], rows(refs[n + a], me), peer, rows(refs[n + a], peer)

        def second(refs, me, j):
            a, sibling = j // 3, _peer(me, 1)
            mine = rows(refs[a], _peer(me, SAME_CORE[j % 3]))
            return mine, mine, sibling, rows(refs[a], _peer(sibling, SAME_CORE[j % 3]))

        self._first, self._second = first, second
        self._flight = _copies_start(list(shards) + list(lands), first, 4 * n, name=name + "_send")
        self.dep = self._flight[2]

    def pass_on(self, after):
        send1, recv1, *arrays = self._flight
        arrays = _copies_wait(send1, recv1, arrays, self._first, 4 * self.n, after, name=self.name + "_recv")
        self._flight = _copies_start(list(arrays[self.n:]), self._second, 3 * self.n, name=self.name + "_pass")
        return self._flight[2]

    def finish(self, after):
        send2, recv2, *lands = self._flight
        return _copies_wait(send2, recv2, lands, self._second, 3 * self.n, after, name=self.name + "_pass_recv")


def _exchange_start(gs, *, name):
    n = len(gs)
    rows = [g.shape[0] // N_DEV for g in gs]
    lands = [lax.empty((N_DEV - 1, r, g.shape[1]), g.dtype) for g, r in zip(gs, rows)]

    def body(*refs):
        g_refs, land_refs = refs[:n], refs[n:2 * n]
        send_sems, recv_sems = refs[2 * n:3 * n], refs[3 * n:4 * n]
        me = _position()
        for a in range(n):
            for k in SEND_ORDER:
                peer = _peer(me, k)
                pltpu.make_async_remote_copy(
                    src_ref=g_refs[a].at[pl.ds(_linear(peer) * rows[a], rows[a]), :],
                    dst_ref=land_refs[a].at[k - 1],
                    send_sem=send_sems[a].at[k - 1], recv_sem=recv_sems[a].at[k - 1],
                    device_id=peer, device_id_type=MESH).start()

    res = pl.pallas_call(
        body, name=name,
        out_shape=tuple(pltpu.SemaphoreType.DMA((N_DEV - 1,)) for _ in range(2 * n))
        + tuple(pltpu.HBM(a.shape, a.dtype) for a in gs + lands),
        in_specs=(HBM_SPEC,) * (2 * n), out_specs=(SEM_SPEC,) * (2 * n) + (HBM_SPEC,) * (2 * n),
        input_output_aliases={i: 2 * n + i for i in range(2 * n)},
        compiler_params=pltpu.CompilerParams(has_side_effects=DATAFLOW),
    )(*[_in_hbm(a) for a in gs + lands])
    return [(res[a], res[n + a], res[2 * n + a], res[3 * n + a]) for a in range(n)]


def _exchange_wait(send_sems, recv_sems, g_thru, land_thru, after, *, name):
    r = land_thru.shape[1]

    def body(g_ref, land_ref, send_sems, recv_sems, after_ref, g_dead, got_ref):
        del after_ref, g_dead, got_ref
        me = _position()
        for k in SEND_ORDER:
            peer = _peer(me, k)
            copy = pltpu.make_async_remote_copy(
                src_ref=g_ref.at[pl.ds(_linear(peer) * r, r), :], dst_ref=land_ref.at[k - 1],
                send_sem=send_sems.at[k - 1], recv_sem=recv_sems.at[k - 1],
                device_id=peer, device_id_type=MESH)
            copy.wait_send()
            copy.wait_recv()

    return pl.pallas_call(
        body, name=name,
        out_shape=(pltpu.HBM(g_thru.shape, g_thru.dtype), pltpu.HBM(land_thru.shape, land_thru.dtype)),
        in_specs=(HBM_SPEC, HBM_SPEC, SEM_SPEC, SEM_SPEC, pl.BlockSpec(memory_space=pl.ANY)),
        out_specs=(HBM_SPEC, HBM_SPEC), input_output_aliases={0: 0, 1: 1},
        compiler_params=pltpu.CompilerParams(has_side_effects=DATAFLOW),
    )(g_thru, land_thru, send_sems, recv_sems, after)


def _adamw_math(w, g, m, v):
    m = B1 * m + (1.0 - B1) * g
    v = B2 * v + (1.0 - B2) * (g * g)
    delta = -LR * ((m / C1) / (jnp.sqrt(v / C2) + AEPS) + WD * w)
    return delta, m, v


def _sum_adamw(items, *, name):
    n = len(items)

    def body(*refs):
        ins, outs, scratch = refs[:5 * n], refs[5 * n:9 * n], refs[9 * n:]
        me_lin = _linear(_position())
        mine = []
        for a in range(n):
            r = items[a][2].shape[0]
            mine.append(pltpu.make_async_copy(ins[5 * a].at[pl.ds(me_lin * r, r), :], scratch[a], scratch[n].at[a]))
            mine[-1].start()
        for a in range(n):
            _, land_ref, w_ref, m_ref, v_ref = ins[5 * a:5 * a + 5]
            g_ref, d_ref, nm_ref, nv_ref = outs[4 * a:4 * a + 4]
            g = land_ref[0].astype(F32)
            for s in range(1, N_DEV - 1):
                g = g + land_ref[s].astype(F32)
            mine[a].wait()
            g = scratch[a][...].astype(F32) + g
            g_ref[...] = g
            d_ref[...], nm_ref[...], nv_ref[...] = _adamw_math(w_ref[...], g, m_ref[...], v_ref[...])

    vmem = pl.BlockSpec(memory_space=pltpu.VMEM)
    res = pl.pallas_call(
        body, name=name,
        out_shape=tuple(jax.ShapeDtypeStruct(it[2].shape, F32) for it in items for _ in range(4)),
        in_specs=[ANY_SPEC, vmem, vmem, vmem, vmem] * n, out_specs=(vmem,) * (4 * n),
        scratch_shapes=[pltpu.VMEM(it[2].shape, BF16) for it in items] + [pltpu.SemaphoreType.DMA((n,))],
        compiler_params=_params(),
    )(*[a for it in items for a in it])
    return [res[4 * a:4 * a + 4] for a in range(n)]


SMALL = ("g_mix_pre", "g_mix_post", "g_mem", "g_x_pre", "g_x_post", "g_ffn_pre", "g_ffn_post",
         "hgrn_onorm", "hgrn_lb", "sinks")
SMALL_W = dict(hgrn_onorm=HD, hgrn_lb=HG_W, sinks=8)
SQ_ROW = len(SMALL)
PACK_ROWS = 16


def _small_pack(parts):
    ns = len(SMALL)

    def body(*refs):
        part, mine, slots, sem = refs[:ns + 1], refs[ns + 1], refs[ns + 2], refs[ns + 3]
        mine[...] = jnp.zeros((PACK_ROWS, D), F32)
        for r, name in enumerate(SMALL):
            wd = SMALL_W.get(name, D)
            mine[r:r + 1, 0:wd] = jnp.sum(part[r][...], axis=0, keepdims=True)[:, 0:wd]
        sq = jnp.sum(part[ns][...]) * (0.5 / D)
        mine[SQ_ROW:SQ_ROW + 1, :] = jnp.full((1, D), sq, F32)
        own = pltpu.make_async_copy(mine, slots.at[_linear(_position())], sem)
        own.start()
        own.wait()

    vmem = pl.BlockSpec(memory_space=pltpu.VMEM)
    return pl.pallas_call(
        body, name="small_pack",
        out_shape=(jax.ShapeDtypeStruct((PACK_ROWS, D), F32), jax.ShapeDtypeStruct((N_DEV, PACK_ROWS, D), F32)),
        in_specs=[vmem] * (ns + 1), out_specs=(vmem, ANY_SPEC),
        scratch_shapes=[pltpu.SemaphoreType.DMA(())], compiler_params=_params(),
    )(*[parts[n] for n in SMALL], parts["sq"])


def _small_exchange(mine, slots):
    def plan(refs, me, j):
        peer = _peer(me, j + 1)
        return refs[0], refs[1].at[_linear(me)], peer, refs[1].at[_linear(peer)]

    send, recv, mine1, slots1 = _copies_start([mine, slots], plan, N_DEV - 1, name="small_send")
    return lambda after: _copies_wait(send, recv, [mine1, slots1], plan, N_DEV - 1, after, name="small_recv")[1]


def _small_update(slots, sm, m_sm, v_sm):
    ns = len(SMALL)

    def body(*refs):
        tot = refs[0][0]
        for s in range(1, N_DEV):
            tot = tot + refs[0][s]
        w_refs, m_refs, v_refs = refs[1:ns + 1], refs[ns + 1:2 * ns + 1], refs[2 * ns + 1:3 * ns + 1]
        outs = refs[3 * ns + 1:]
        loss_ref = outs[0]
        g_out, d_out = outs[1:ns + 1], outs[ns + 1:2 * ns + 1]
        nm_out, nv_out = outs[2 * ns + 1:3 * ns + 1], outs[3 * ns + 1:4 * ns + 1]
        loss_ref[...] = tot[SQ_ROW:SQ_ROW + 1, 0:1]
        for r, name in enumerate(SMALL):
            wd = SMALL_W.get(name, D)
            g = tot[r:r + 1, 0:wd]
            w = w_refs[r][...]
            if name == "hgrn_lb":
                mx = jnp.maximum(w[0:1], w[1:2])
                e0, e1 = jnp.exp(w[0:1] - mx), jnp.exp(w[1:2] - mx)
                lb0 = e0 / (e0 + e1)
                g0 = g * lb0 * (1.0 - lb0)
                for i, gi in enumerate((g0, -g0)):
                    d, nm, nv = _adamw_math(w[i:i + 1], gi, m_refs[r][i:i + 1, :], v_refs[r][i:i + 1, :])
                    g_out[r][i:i + 1, :] = gi
                    d_out[r][i:i + 1, :], nm_out[r][i:i + 1, :], nv_out[r][i:i + 1, :] = d, nm, nv
            else:
                d, nm, nv = _adamw_math(w, g, m_refs[r][...], v_refs[r][...])
                g_out[r][...] = g
                d_out[r][...], nm_out[r][...], nv_out[r][...] = d, nm, nv

    shapes = [jax.ShapeDtypeStruct(sm[n].shape, F32) for n in SMALL]
    res = pl.pallas_call(
        body, name="small_update", out_shape=tuple([jax.ShapeDtypeStruct((1, 1), F32)] + shapes * 4),
        compiler_params=_params(),
    )(slots, *[sm[n] for n in SMALL], *[m_sm[n] for n in SMALL], *[v_sm[n] for n in SMALL])
    groups = [dict(zip(SMALL, res[1 + i * ns:1 + (i + 1) * ns])) for i in range(4)]
    return res[0], groups[0], groups[1], groups[2], groups[3]


BIG = ("w_in", "w_gate", "w_up", "w_down", "w_out", "wq_x", "wk_x", "wv_x", "wo_x")
BIG_KEY = dict(w_in="winT", w_gate="wgT", w_up="wuT", w_down="wd", w_out="wout", wq_x="wq", wk_x="wk",
               wv_x="wv", wo_x="wo")
TRANSPOSED = ("w_in", "w_gate", "w_up")
WEIGHTS = ("w_in", "sinks", "hgrn_lb", "hgrn_onorm", "w_out", "g_mix_pre", "g_mix_post", "g_mem", "g_x_pre",
           "g_x_post", "wq_x", "wk_x", "wv_x", "wo_x", "g_ffn_pre", "g_ffn_post", "w_gate", "w_up", "w_down")


def kernel(x, mem, w_in, sinks, hgrn_lb, hgrn_onorm, w_out, g_mix_pre, g_mix_post, g_mem, g_x_pre, g_x_post, wq_x, wk_x, wv_x, wo_x, g_ffn_pre, g_ffn_post, w_gate, w_up, w_down, loss_target, m_w_in, m_sinks, m_hgrn_lb, m_hgrn_onorm, m_w_out, m_g_mix_pre, m_g_mix_post, m_g_mem, m_g_x_pre, m_g_x_post, m_wq_x, m_wk_x, m_wv_x, m_wo_x, m_g_ffn_pre, m_g_ffn_post, m_w_gate, m_w_up, m_w_down, v_w_in, v_sinks, v_hgrn_lb, v_hgrn_onorm, v_w_out, v_g_mix_pre, v_g_mix_post, v_g_mem, v_g_x_pre, v_g_x_post, v_wq_x, v_wk_x, v_wv_x, v_wo_x, v_g_ffn_pre, v_g_ffn_post, v_w_gate, v_w_up, v_w_down):
    given = dict(locals())
    wts = {n: given[n] for n in WEIGHTS}
    ms = {n: given["m_" + n] for n in WEIGHTS}
    vs = {n: given["v_" + n] for n in WEIGHTS}

    def mat(a, name):
        a = a[0]
        return a.T if name in TRANSPOSED else a

    groups = (("w_in",), ("w_out", "wq_x", "wk_x", "wv_x", "wo_x"), ("w_gate", "w_up", "w_down"))
    gathers = []

    def start_group(g, dep):
        tag = ("w_in", "w_attn", "w_ffn")[g]
        shards, lands = _prepare_weights([mat(wts[n], n) for n in groups[g]], name="prepare_" + tag, dep=dep)
        gathers.append(_TwoLevelGather(shards, lands, name=tag))
        return gathers[-1].dep

    first_dep = start_group(1, start_group(0, None))
    name_of = {k: n for n, k in BIG_KEY.items()}
    gathered = {}

    def milestone(tag, value):
        if tag == "z":
            return start_group(2, value)
        return gathers[{"swa": 1, "ox": 2}[tag]].pass_on(value)

    def fetch(key, after):
        name = name_of[key]
        if name not in gathered:
            g = [i for i, group in enumerate(groups) if name in group][0]
            if g == 0:
                gathers[0].pass_on(after)
            gathered.update(zip(groups[g], gathers[g].finish(after)))
        return gathered[name]

    sm = {n: wts[n] for n in SMALL}
    started, held = {}, {}
    send_with = {k: group for group in (("wgT", "wuT"), ("wo", "wq", "wk", "wv")) for k in group}

    def emit(key, g):
        held[key] = g
        group = send_with.get(key, (key,))
        if key != group[-1]:
            return None
        flights = _exchange_start([held[k] for k in group], name="grad_send_" + name_of[group[0]])
        started.update({name_of[k]: f for k, f in zip(group, flights)})
        return flights[-1][2]

    grad_x, _, parts = _local_step(x[0], mem[0], loss_target[0], fetch, sm, emit, first_dep=first_dep, milestone=milestone)
    small_finish = _small_exchange(*_small_pack(parts))
    grads, deltas, new_m, new_v = {}, {}, {}, {}
    after = grad_x
    for group in (("w_down",), ("w_gate",), ("w_up",), ("wo_x", "wq_x", "wk_x", "wv_x", "w_out"), ("w_in",)):
        items = []
        for n in group:
            g_all, land = _exchange_wait(*started[n], after, name="grad_recv_" + n)
            items.append((g_all, land, mat(wts[n], n), mat(ms[n], n), mat(vs[n], n)))
            after = land
        for n, res in zip(group, _sum_adamw(items, name="adamw_" + group[0])):
            after = res[1]
            if n in TRANSPOSED:
                res = [a.T for a in res]
            grads[n], deltas[n], new_m[n], new_v[n] = [a[None] for a in res]
    loss, g_s, d_s, m_s, v_s = _small_update(small_finish(after), sm, {n: ms[n] for n in SMALL},
                                             {n: vs[n] for n in SMALL})
    grads.update(g_s), deltas.update(d_s), new_m.update(m_s), new_v.update(v_s)
    return (loss[0, 0], grad_x[None], *[grads[n] for n in WEIGHTS], *[deltas[n] for n in WEIGHTS],
            *[new_m[n] for n in WEIGHTS], *[new_v[n] for n in WEIGHTS])
```

```python
import functools

import jax
import jax.numpy as jnp
from jax import lax
from jax.experimental import pallas as pl
from jax.experimental.pallas import tpu as pltpu

F32 = jnp.float32
BF16 = jnp.bfloat16

D = 1024
D_IN = 2816
D_FF = 2816
CHUNK = 64
SWA_W = 512
KV_W = 128
HG_W = 512
HD = 128
ZQH, ZFH, ZIH, ZGH = 768, 1280, 1792, 2304
XH, XD = 4, 256
EPS = 1e-6
NEG = -1e30
N_DEV = 8
MESH = pl.DeviceIdType.MESH

LR, B1, B2, AEPS, WD, STEP = 0.001, 0.9, 0.999, 1e-08, 0.01, 10
C1 = 1.0 - B1 ** STEP
C2 = 1.0 - B2 ** STEP

VMEM_LIMIT = 56 * 1024 * 1024


def _params(**kw):
    return pltpu.CompilerParams(vmem_limit_bytes=VMEM_LIMIT, **kw)


def _sig(x):
    return 1.0 / (1.0 + jnp.exp(-x))


def _rowsum8(x):
    r, w = x.shape
    return jnp.sum(x.reshape(r // 8, 8, w), axis=0)


def _dot(a, b, ca, cb, precision=None):
    return lax.dot_general(a, b, (((ca,), (cb,)), ((), ())), preferred_element_type=F32,
                           precision=precision)


ANY_SPEC = pl.BlockSpec(memory_space=pl.ANY)


def _mm(a, b, *, ta=False, tb=False, out_dtype, tm, tn, tk=None, name, dep=None, n_outer=False):
    m = a.shape[1] if ta else a.shape[0]
    k = a.shape[0] if ta else a.shape[1]
    n = b.shape[0] if tb else b.shape[1]
    tm, tn = min(tm, m), min(tn, n)
    tk = k if tk is None else min(tk, k)
    nk = k // tk
    assert m % tm == 0 and n % tn == 0 and k % tk == 0, (name, m, n, k, tm, tn, tk)
    ij = (lambda g0, g1: (g1, g0)) if n_outer else (lambda g0, g1: (g0, g1))
    a_spec = (pl.BlockSpec((tk, tm), lambda g0, g1, kk: (kk, ij(g0, g1)[0])) if ta
              else pl.BlockSpec((tm, tk), lambda g0, g1, kk: (ij(g0, g1)[0], kk)))
    b_spec = (pl.BlockSpec((tn, tk), lambda g0, g1, kk: (ij(g0, g1)[1], kk)) if tb
              else pl.BlockSpec((tk, tn), lambda g0, g1, kk: (kk, ij(g0, g1)[1])))
    ca, cb = (0 if ta else 1), (1 if tb else 0)

    deps = [] if dep is None else [dep]

    def body(a_ref, b_ref, *rest):
        o_ref, acc = rest[len(deps)], rest[len(deps) + 1:]
        p = _dot(a_ref[...].astype(BF16), b_ref[...].astype(BF16), ca, cb)
        if nk == 1:
            o_ref[...] = p.astype(out_dtype)
        else:
            acc_ref, = acc
            kk = pl.program_id(2)

            @pl.when(kk == 0)
            def _():
                acc_ref[...] = p

            @pl.when(kk > 0)
            def _():
                acc_ref[...] += p

            @pl.when(kk == nk - 1)
            def _():
                o_ref[...] = acc_ref[...].astype(out_dtype)

    return pl.pallas_call(
        body, name=name, out_shape=jax.ShapeDtypeStruct((m, n), out_dtype),
        grid=(n // tn, m // tm, nk) if n_outer else (m // tm, n // tn, nk),
        in_specs=[a_spec, b_spec] + [ANY_SPEC] * len(deps),
        out_specs=pl.BlockSpec((tm, tn), lambda g0, g1, kk: ij(g0, g1)),
        scratch_shapes=[pltpu.VMEM((tm, tn), F32)] if nk > 1 else [],
        compiler_params=_params(dimension_semantics=("parallel", "parallel", "arbitrary")),
    )(a, b, *deps)


def _mm_rows(prods, rows_in, vecs_in, epilogue, outs, *, tm, name, dep=None):
    m = prods[0][0].shape[0]
    n = prods[0][1].shape[0] if prods[0][2] else prods[0][1].shape[1]
    tm = min(tm, m)
    assert m % tm == 0
    deps = [] if dep is None else [dep]
    n_p, n_r, n_v = len(prods), len(rows_in), len(vecs_in)

    def body(*refs):
        ab = refs[:2 * n_p]
        row_refs = refs[2 * n_p:2 * n_p + n_r]
        vec_refs = refs[2 * n_p + n_r:2 * n_p + n_r + n_v]
        out_refs = refs[2 * n_p + n_r + n_v + len(deps):]
        p = None
        for j, (_, _, tb) in enumerate(prods):
            t = _dot(ab[2 * j][...].astype(BF16), ab[2 * j + 1][...], 1, 1 if tb else 0)
            p = t if p is None else p + t
        vals = epilogue(p, *[r[...] for r in row_refs], *[v[...] for v in vec_refs])
        for (dtype, kind), o_ref, val in zip(outs, out_refs, vals):
            if kind == "row":
                o_ref[...] = val.astype(dtype)
            else:
                @pl.when(pl.program_id(0) == 0)
                def _(o_ref=o_ref):
                    o_ref[...] = jnp.zeros_like(o_ref)

                o_ref[...] += val

    row = lambda w: pl.BlockSpec((tm, w), lambda i: (i, 0))
    whole = lambda a: pl.BlockSpec(a.shape, lambda i: (0,) * a.ndim, pipeline_mode=pl.Buffered(1))
    in_specs, args = [], []
    for a, b, _ in prods:
        in_specs += [row(a.shape[1]), whole(b)]
        args += [a, b]
    in_specs += [row(r.shape[1]) for r in rows_in] + [whole(v) for v in vecs_in] + [ANY_SPEC] * len(deps)
    return pl.pallas_call(
        body, name=name,
        out_shape=tuple(jax.ShapeDtypeStruct((m, n) if kind == "row" else (8, n), dtype) for dtype, kind in outs),
        grid=(m // tm,), in_specs=in_specs,
        out_specs=tuple(row(n) if kind == "row" else pl.BlockSpec((8, n), lambda i: (0, 0)) for _, kind in outs),
        compiler_params=_params(dimension_semantics=("arbitrary",)),
    )(*args, *rows_in, *vecs_in, *deps)


def _rstd(x):
    return lax.rsqrt(jnp.mean(x * x, axis=-1, keepdims=True) + EPS)


def _norm_bwd(xh, r, t):
    return r * (t - xh * jnp.mean(xh * t, axis=-1, keepdims=True))


ROW_F32, ROW_BF16, SUM_F32 = (F32, "row"), (BF16, "row"), (F32, "sum")


def _ep_post_pre(p, h, g_post, g_pre):
    y = p.astype(BF16)
    yf = y.astype(F32)
    hn = h + yf * _rstd(yf) * g_post
    return y, hn, hn * _rstd(hn) * g_pre


_EP_POST_PRE_OUTS = [ROW_BF16, ROW_F32, ROW_BF16]


def _ep_final_loss(y, h, target, g_post):
    r = _rstd(y)
    yh = y * r
    err = h + yh * g_post - target
    dh = err * (1.0 / D)
    return _rowsum8(err * err), dh, _norm_bwd(yh, r, dh * g_post), _rowsum8(dh * yh)


_EP_FINAL_LOSS_OUTS = [SUM_F32, ROW_F32, ROW_BF16, SUM_F32]


def _ep_post_pre_bwd(du, dh_out, hn, y, g_post, g_pre):
    r2 = _rstd(hn)
    xh = hn * r2
    dh = dh_out + _norm_bwd(xh, r2, du * g_pre)
    yf = y.astype(F32)
    r1 = _rstd(yf)
    yh = yf * r1
    return dh, _norm_bwd(yh, r1, dh * g_post), _rowsum8(du * xh), _rowsum8(dh * yh)


_EP_POST_PRE_BWD_OUTS = [ROW_F32, ROW_BF16, SUM_F32, SUM_F32]


def _ep_pre_bwd(du, dh_out, x, g):
    r = _rstd(x)
    xh = x * r
    return dh_out + _norm_bwd(xh, r, du * g), _rowsum8(du * xh)


_EP_PRE_BWD_OUTS = [ROW_F32, SUM_F32]


def _prenorm(x, g, *, name, dep=None):
    t, d = x.shape
    tb = min(512, t)
    deps = [] if dep is None else [dep]

    def body(x_ref, g_ref, *rest):
        xf = x_ref[...]
        rest[-1][...] = (xf * _rstd(xf) * g_ref[...]).astype(BF16)

    return pl.pallas_call(
        body, name=name, out_shape=jax.ShapeDtypeStruct((t, d), BF16), grid=(t // tb,),
        in_specs=[pl.BlockSpec((tb, d), lambda i: (i, 0)), pl.BlockSpec((1, d), lambda i: (0, 0))]
        + [ANY_SPEC] * len(deps),
        out_specs=pl.BlockSpec((tb, d), lambda i: (i, 0)), compiler_params=_params(),
    )(x, g, *deps)


QB = 256


def _half_mask(shape, e):
    lane = lax.broadcasted_iota(jnp.int32, shape, len(shape) - 1)
    return (lane // 64) == e


def _place(kv):
    sw = pltpu.roll(kv, 64, 1)
    m0 = _half_mask(kv.shape, 0)
    return [[jnp.where(m0, kv, 0.0).astype(BF16), jnp.where(m0, 0.0, sw).astype(BF16)],
            [jnp.where(m0, sw, 0.0).astype(BF16), jnp.where(m0, 0.0, kv).astype(BF16)]]


SQ = 128
SK = 256


def _swa_valid(i, sb):
    qc = lax.broadcasted_iota(jnp.int32, (SQ, SK), 0) // CHUNK
    kc = lax.broadcasted_iota(jnp.int32, (SQ, SK), 1) // CHUNK - 2
    return (kc <= qc) & (qc <= kc + 2) & (4 * i + 2 * sb + kc >= 0)


def _swa_fwd(z, sinks, t, dep=None):
    nb = t // QB
    deps = [] if dep is None else [dep]

    def body(s_ref, q_ref, kp_ref, kc_ref, vp_ref, vc_ref, *rest):
        o_ref, lse_ref = rest[-2:]
        i = pl.program_id(0)
        kpl = _place(jnp.concatenate([kp_ref[...], kc_ref[...]], axis=0))
        vpl = _place(jnp.concatenate([vp_ref[...], vc_ref[...]], axis=0))
        lane = lax.broadcasted_iota(jnp.int32, (SQ, 128), 1)
        for sb in range(QB // SQ):
            rows, keys = slice(SQ * sb, SQ * (sb + 1)), slice(SQ * sb, SQ * sb + SK)
            valid = _swa_valid(i, sb)
            lse_out = jnp.zeros((SQ, 128), F32)
            for j in range(4):
                qp = q_ref[rows, 128 * j:128 * (j + 1)].astype(BF16)
                acc = jnp.zeros((SQ, 128), F32)
                for e in range(2):
                    h = 2 * j + e
                    kvh = h // 4
                    qm = jnp.where(_half_mask(qp.shape, e), qp, jnp.zeros_like(qp))
                    s = _dot(qm, kpl[kvh][e][keys], 1, 1) * 0.125
                    s = jnp.where(valid, s, NEG)
                    sink = s_ref[0, h]
                    m = jnp.maximum(jnp.max(s, axis=-1, keepdims=True), sink)
                    p = jnp.exp(s - m)
                    l = jnp.sum(p, axis=-1, keepdims=True) + jnp.exp(sink - m)
                    acc = acc + _dot(p.astype(BF16), vpl[kvh][e][keys], 1, 0) * (1.0 / l)
                    lse_out = jnp.where(lane == h, m + jnp.log(l), lse_out)
                o_ref[rows, 128 * j:128 * (j + 1)] = acc.astype(BF16)
            lse_ref[rows, :] = lse_out

    prev = lambda c: pl.BlockSpec((128, 128), lambda i: (jnp.maximum(2 * i - 1, 0), c))
    cur = lambda c: pl.BlockSpec((QB, 128), lambda i: (i, c))
    return pl.pallas_call(
        body, name="swa_fwd",
        out_shape=(jax.ShapeDtypeStruct((t, D), BF16), jax.ShapeDtypeStruct((t, 128), F32)),
        grid=(nb,),
        in_specs=[pl.BlockSpec(memory_space=pltpu.SMEM),
                  pl.BlockSpec((QB, SWA_W), lambda i: (i, 0)), prev(4), cur(4), prev(5), cur(5)]
        + [ANY_SPEC] * len(deps),
        out_specs=(pl.BlockSpec((QB, SWA_W), lambda i: (i, 0)), pl.BlockSpec((QB, 128), lambda i: (i, 0))),
        compiler_params=_params(),
    )(sinks, z, z, z, z, z, *deps)


def _swa_bwd(z, sinks, ymix, lse, dymix, t):
    nb = t // QB

    def body(s_ref, q_ref, kp_ref, kc_ref, vp_ref, vc_ref, o_ref, do_ref, l_ref,
             dq_ref, first_ref, second_ref, ds_ref, carry_ref):
        i = pl.program_id(0)
        live = i < nb

        @pl.when(i == 0)
        def _():
            ds_ref[...] = jnp.zeros_like(ds_ref)
            carry_ref[...] = jnp.zeros_like(carry_ref)

        lane = lax.broadcasted_iota(jnp.int32, (8, 128), 1)
        kpl = _place(jnp.concatenate([kp_ref[...], kc_ref[...]], axis=0))
        vpl = _place(jnp.concatenate([vp_ref[...], vc_ref[...]], axis=0))
        nk = QB + 128
        qc = lax.broadcasted_iota(jnp.int32, (QB, nk), 0) // CHUNK
        kc = lax.broadcasted_iota(jnp.int32, (QB, nk), 1) // CHUNK - 2
        valid = (kc <= qc) & (qc <= kc + 2) & (4 * i + kc >= 0) & live
        lse_c = l_ref[...]
        dsink = jnp.zeros((8, 128), F32)
        dk_acc = [[jnp.zeros((nk, 128), F32) for _ in range(2)] for _ in range(2)]
        dv_acc = [[jnp.zeros((nk, 128), F32) for _ in range(2)] for _ in range(2)]
        dq = []
        for j in range(4):
            cols = slice(128 * j, 128 * (j + 1))
            qp = q_ref[:, cols].astype(BF16)
            dop = do_ref[:, cols]
            prod = dop.astype(F32) * o_ref[:, cols].astype(F32)
            acc = jnp.zeros((QB, 128), F32)
            for e in range(2):
                h = 2 * j + e
                kvh = h // 4
                hm = _half_mask(qp.shape, e)
                qm = jnp.where(hm, qp, jnp.zeros_like(qp))
                dom = jnp.where(hm, dop, jnp.zeros_like(dop))
                dd = jnp.sum(jnp.where(hm, prod, 0.0), axis=-1, keepdims=True)
                lse_h = lse_c[:, h:h + 1]
                s = _dot(qm, kpl[kvh][e], 1, 1) * 0.125
                p = jnp.where(valid, jnp.exp(s - lse_h), 0.0)
                dp = _dot(dom, vpl[kvh][e], 1, 1)
                ds = (p * (dp - dd) * 0.125).astype(BF16)
                acc = acc + _dot(ds, kpl[kvh][e], 1, 0)
                dk_acc[kvh][e] = dk_acc[kvh][e] + _dot(ds, qm, 0, 0)
                dv_acc[kvh][e] = dv_acc[kvh][e] + _dot(p.astype(BF16), dom, 0, 0)
                ps = jnp.where(live, jnp.exp(s_ref[0, h] - lse_h) * dd, 0.0)
                dsink = dsink - jnp.where(lane == h, _rowsum8(jnp.broadcast_to(ps, (QB, 128))), 0.0)
            dq.append(acc.astype(BF16))
        ds_ref[...] += dsink
        dk = dk_acc[0][0] + dk_acc[1][1] + pltpu.roll(dk_acc[0][1] + dk_acc[1][0], 64, 1)
        dv = dv_acc[0][0] + dv_acc[1][1] + pltpu.roll(dv_acc[0][1] + dv_acc[1][0], 64, 1)
        dkv = jnp.concatenate([dk, dv], axis=1)
        second_ref[...] = (carry_ref[...] + dkv[0:128]).astype(BF16)
        carry_ref[...] = dkv[256:384]

        @pl.when(live)
        def _():
            for j in range(4):
                dq_ref[:, 128 * j:128 * (j + 1)] = dq[j]
            first_ref[...] = dkv[128:256].astype(BF16)

    blk = lambda i: jnp.minimum(i, nb - 1)
    prev = lambda c: pl.BlockSpec((128, 128), lambda i: (jnp.maximum(2 * blk(i) - 1, 0), c))
    cur = lambda w, c: pl.BlockSpec((QB, w), lambda i: (blk(i), c))
    half = lambda index: pl.BlockSpec((128, 256), lambda i: (index(i), 0))
    return pl.pallas_call(
        body, name="swa_bwd",
        out_shape=(jax.ShapeDtypeStruct((t, SWA_W), BF16), jax.ShapeDtypeStruct((t // 2, 256), BF16),
                   jax.ShapeDtypeStruct((t // 2, 256), BF16), jax.ShapeDtypeStruct((8, 128), F32)),
        grid=(nb + 1,),
        in_specs=[pl.BlockSpec(memory_space=pltpu.SMEM),
                  cur(SWA_W, 0), prev(4), cur(128, 4), prev(5), cur(128, 5),
                  cur(SWA_W, 0), cur(SWA_W, 0), cur(128, 0)],
        out_specs=(cur(SWA_W, 0), half(blk), half(lambda i: jnp.maximum(i - 1, 0)),
                   pl.BlockSpec((8, 128), lambda i: (0, 0))),
        scratch_shapes=[pltpu.VMEM((128, 256), F32)],
        compiler_params=_params(dimension_semantics=("arbitrary",)),
    )(sinks, z, z, z, z, z, ymix, dymix, lse)


HB = 256


def _lower_bound(lb_ref):
    a = lb_ref[...]
    a0, a1 = a[0:1], a[1:2]
    mx = jnp.maximum(a0, a1)
    e0, e1 = jnp.exp(a0 - mx), jnp.exp(a1 - mx)
    return e0 / (e0 + e1)


def _hgrn_cols(row_block):
    return [pl.BlockSpec((HB, 2 * HD), lambda j, c=base // (2 * HD) + p: (row_block(j), c))
            for base in (ZQH, ZFH, ZIH, ZGH) for p in range(2)]


NCH = HB // CHUNK


def _split3(x):
    hi = x.astype(BF16)
    r1 = x - hi.astype(F32)
    mid = r1.astype(BF16)
    return hi, mid, (r1 - mid.astype(F32)).astype(BF16)


def _blockdiag(lower):
    r = lax.broadcasted_iota(jnp.int32, (HB, HB), 0)
    c = lax.broadcasted_iota(jnp.int32, (HB, HB), 1)
    return (r // CHUNK == c // CHUNK) & ((c <= r) if lower else (c >= r))


def _chunk_sums(mask_bf16, x):
    return sum(_dot(mask_bf16, part, 1, 0) for part in _split3(x))


def _per_chunk_rows(x, row):
    w = x.shape[1]
    picked = x.reshape(NCH, CHUNK, w)[:, row:row + 1, :]
    return jnp.broadcast_to(picked, (NCH, CHUNK, w)).reshape(HB, w)


def _chunk_stack(x, chunk_of_row):
    return jnp.concatenate([jnp.where(chunk_of_row == c, x, jnp.zeros_like(x)) for c in range(NCH)], axis=1)


def _chunk_pick(x, chunk_of_row):
    w = x.shape[1] // NCH
    out = jnp.zeros((HB, w), x.dtype)
    for c in range(NCH):
        out = jnp.where(chunk_of_row == c, x[:, c * w:(c + 1) * w], out)
    return out


def _hgrn_local(q, f, kf, b):
    sq = _sig(q)
    qf = q * sq * (HD ** -0.5)
    b_mid = _per_chunk_rows(b, CHUNK // 2 - 1)
    b_last = _per_chunk_rows(b, CHUNK - 1)
    qm = qf * jnp.exp(b - b_mid)
    km = kf * jnp.exp(b_mid - b)
    kl = kf * jnp.exp(b_last - b)
    qb = qf * jnp.exp(b)
    return dict(sq=sq, b_mid=b_mid, b_last=b_last, qm=qm, km=km, kl=kl, qb=qb)


def _hgrn2_fwd(z, hgrn_lb, onorm, ymix, t, dep=None):
    nb = t // HB
    deps = [] if dep is None else [dep]

    def body(*refs):
        zq, zf, zi, zg = refs[0:2], refs[2:4], refs[4:6], refs[6:8]
        (lb_ref, on_ref), (y_ref, o_ref, sp_ref, st_ref) = refs[8:10], refs[-4:]

        @pl.when(pl.program_id(0) == 0)
        def _():
            st_ref[...] = jnp.zeros_like(st_ref)

        lb_all = _lower_bound(lb_ref)
        gn = on_ref[...]
        low = _blockdiag(True)
        low_b = low.astype(BF16)
        chunk_of_row = lax.broadcasted_iota(jnp.int32, (HB, HD), 0) // CHUNK
        for p in range(2):
            lbp = lb_all[:, 2 * HD * p:2 * HD * (p + 1)]
            fp = lbp + (1.0 - lbp) * _sig(zf[p][...])
            bp = _chunk_sums(low_b, jnp.log(fp))
            for e in range(2):
                h, ls = 2 * p + e, slice(e * HD, (e + 1) * HD)
                f = fp[:, ls]
                w = _hgrn_local(zq[p][:, ls], f, 1.0 - f, bp[:, ls])
                iv = zi[p][:, ls].astype(BF16)
                a = jnp.where(low, _dot(w["qm"].astype(BF16), w["km"].astype(BF16), 1, 1), 0.0)
                o = _dot(a.astype(BF16), iv, 1, 0)
                u = _dot(iv, _chunk_stack(w["kl"].astype(BF16), chunk_of_row), 0, 0)
                decay = jnp.exp(w["b_last"])
                st = st_ref[h]
                states = []
                for c in range(NCH):
                    sp_ref[h, c] = st
                    states.append(st.astype(BF16))
                    st = st * decay[c * CHUNK:c * CHUNK + 1] + u[:, c * HD:(c + 1) * HD]
                st_ref[h] = st
                inter = _dot(w["qb"].astype(BF16), jnp.concatenate(states, axis=0), 1, 1)
                o = o + _chunk_pick(inter, chunk_of_row)
                hs = slice(h * HD, (h + 1) * HD)
                o_ref[:, hs] = o
                gg = zg[p][:, ls]
                y_ref[:, hs] = (o * _rstd(o) * gn * (gg * _sig(gg))).astype(BF16)

    return pl.pallas_call(
        body, name="hgrn_fwd",
        out_shape=(jax.ShapeDtypeStruct((t, D), BF16), jax.ShapeDtypeStruct((t, HG_W), F32),
                   jax.ShapeDtypeStruct((4, t // CHUNK, HD, HD), F32)),
        grid=(nb,),
        in_specs=_hgrn_cols(lambda j: j) + [pl.BlockSpec((2, HG_W), lambda j: (0, 0)),
                                            pl.BlockSpec((1, HD), lambda j: (0, 0)), ANY_SPEC]
        + [ANY_SPEC] * len(deps),
        out_specs=(pl.BlockSpec((HB, HG_W), lambda j: (j, 1)),
                   pl.BlockSpec((HB, HG_W), lambda j: (j, 0)),
                   pl.BlockSpec((4, NCH, HD, HD), lambda j: (0, j, 0, 0))),
        scratch_shapes=[pltpu.VMEM((4, HD, HD), F32)],
        input_output_aliases={10: 0},
        compiler_params=_params(dimension_semantics=("arbitrary",)),
    )(*[z] * 8, hgrn_lb, onorm, ymix, *deps)


def _hgrn2_bwd(z, hgrn_lb, onorm, o_save, sprev, dymix, dza, t):
    nb = t // HB

    def body(*refs):
        zq, zf, zi, zg = refs[0:2], refs[2:4], refs[4:6], refs[6:8]
        (lb_ref, on_ref, o_ref, sp_ref, dy_ref, dqa_ref, first_ref, second_ref,
         dz_ref, dlb_ref, don_ref, dst_ref) = refs[8:]

        @pl.when(pl.program_id(0) == 0)
        def _():
            dst_ref[...] = jnp.zeros_like(dst_ref)
            dlb_ref[...] = jnp.zeros_like(dlb_ref)
            don_ref[...] = jnp.zeros_like(don_ref)

        dz_ref[:, 0:SWA_W] = dqa_ref[...]
        dz_ref[0:HB // 2, SWA_W:ZQH] = first_ref[...]
        dz_ref[HB // 2:HB, SWA_W:ZQH] = second_ref[...]
        lb_all = _lower_bound(lb_ref)
        gn = on_ref[...]
        low, upp = _blockdiag(True), _blockdiag(False)
        upp_b = upp.astype(BF16)
        low_b = low.astype(BF16)
        row = lax.broadcasted_iota(jnp.int32, (HB, HD), 0)
        chunk_of_row = row // CHUNK
        in_chunk = row % CHUNK
        for p in range(2):
            lbp = lb_all[:, 2 * HD * p:2 * HD * (p + 1)]
            sgp = _sig(zf[p][...])
            fp = lbp + (1.0 - lbp) * sgp
            bp = _chunk_sums(low_b, jnp.log(fp))
            db_pair, dkf_pair = [], []
            for e in range(2):
                h, ls, hs = 2 * p + e, slice(e * HD, (e + 1) * HD), slice((2 * p + e) * HD, (2 * p + e + 1) * HD)
                f = fp[:, ls]
                q = zq[p][:, ls]
                w = _hgrn_local(q, f, 1.0 - f, bp[:, ls])
                iv = zi[p][:, ls].astype(BF16)
                gg = zg[p][:, ls]
                o = o_ref[:, hs]
                dout = dy_ref[:, hs].astype(F32)
                sgg = _sig(gg)
                r = _rstd(o)
                oh = o * r
                dyn = dout * (gg * sgg)
                dz_ref[:, ZGH + h * HD:ZGH + (h + 1) * HD] = (
                    dout * oh * gn * (sgg * (1.0 + gg * (1.0 - sgg)))).astype(BF16)
                don_ref[...] += _rowsum8(dyn * oh)
                do = _norm_bwd(oh, r, dyn * gn).astype(BF16)
                qm, km, kl, qb = (w[n].astype(BF16) for n in ("qm", "km", "kl", "qb"))
                decay = jnp.exp(w["b_last"])
                grads_in = _dot(do, _chunk_stack(qb, chunk_of_row), 0, 0)
                dst = dst_ref[h]
                dstn, dd_rows = [None] * NCH, [None] * NCH
                for c in reversed(range(NCH)):
                    dstn[c] = dst.astype(BF16)
                    dd_rows[c] = jnp.sum(dst * sp_ref[h, c], axis=0, keepdims=True)
                    dst = dst * decay[c * CHUNK:c * CHUNK + 1] + grads_in[:, c * HD:(c + 1) * HD]
                dst_ref[h] = dst
                states = jnp.concatenate([sp_ref[h, c].astype(BF16) for c in range(NCH)], axis=0)
                dstn_all = jnp.concatenate(dstn, axis=0)
                dqb = _dot(_chunk_stack(do, chunk_of_row), states, 1, 0)
                at = jnp.where(upp, _dot(km, qm, 1, 1), 0.0)
                di = _dot(at.astype(BF16), do, 1, 0) + _chunk_pick(_dot(kl, dstn_all, 1, 1), chunk_of_row)
                dz_ref[:, ZIH + h * HD:ZIH + (h + 1) * HD] = di.astype(BF16)
                dkl = _dot(_chunk_stack(iv, chunk_of_row), dstn_all, 1, 0)
                da = jnp.where(low, _dot(do, iv, 1, 1), 0.0).astype(BF16)
                dat = jnp.where(upp, _dot(iv, do, 1, 1), 0.0).astype(BF16)
                dqm = _dot(da, km, 1, 0)
                dkm = _dot(dat, qm, 1, 0)
                b = bp[:, ls]
                e1, e2 = jnp.exp(b - w["b_mid"]), jnp.exp(w["b_mid"] - b)
                e3, e4 = jnp.exp(w["b_last"] - b), jnp.exp(b)
                dqf = dqm * e1 + dqb * e4
                dkf_pair.append(dkm * e2 + dkl * e3)
                t_qm, t_km, t_kl = dqm * w["qm"], dkm * w["km"], dkl * w["kl"]
                db = t_qm - t_km - t_kl + dqb * w["qb"]
                db_mid = jnp.sum((t_km - t_qm).reshape(NCH, CHUNK, HD), axis=1, keepdims=True)
                db_last = jnp.sum(t_kl.reshape(NCH, CHUNK, HD), axis=1, keepdims=True)
                db_last = db_last + jnp.stack(dd_rows, axis=0) * jnp.exp(
                    bp[:, ls].reshape(NCH, CHUNK, HD)[:, CHUNK - 1:CHUNK, :])
                spread = lambda v: jnp.broadcast_to(v, (NCH, CHUNK, HD)).reshape(HB, HD)
                db = (db + jnp.where(in_chunk == CHUNK // 2 - 1, spread(db_mid), 0.0)
                      + jnp.where(in_chunk == CHUNK - 1, spread(db_last), 0.0))
                db_pair.append(db)
                sq = w["sq"]
                dz_ref[:, ZQH + h * HD:ZQH + (h + 1) * HD] = (
                    dqf * (HD ** -0.5) * (sq * (1.0 + q * (1.0 - sq)))).astype(BF16)
            dlogf = _chunk_sums(upp_b, jnp.concatenate(db_pair, axis=1))
            dfv = dlogf / fp - jnp.concatenate(dkf_pair, axis=1)
            dz_ref[:, ZFH + 2 * HD * p:ZFH + 2 * HD * (p + 1)] = (dfv * (1.0 - lbp) * sgp * (1.0 - sgp)).astype(BF16)
            dlb_ref[:, 2 * HD * p:2 * HD * (p + 1)] += _rowsum8(dfv * (1.0 - sgp))

    rev = lambda j: nb - 1 - j
    return pl.pallas_call(
        body, name="hgrn_bwd",
        out_shape=(jax.ShapeDtypeStruct((t, D_IN), BF16), jax.ShapeDtypeStruct((8, HG_W), F32),
                   jax.ShapeDtypeStruct((8, HD), F32)),
        grid=(nb,),
        in_specs=_hgrn_cols(rev) + [pl.BlockSpec((2, HG_W), lambda j: (0, 0)), pl.BlockSpec((1, HD), lambda j: (0, 0)),
                                    pl.BlockSpec((HB, HG_W), lambda j: (rev(j), 0)),
                                    pl.BlockSpec((4, NCH, HD, HD), lambda j: (0, rev(j), 0, 0)),
                                    pl.BlockSpec((HB, HG_W), lambda j: (rev(j), 1)),
                                    pl.BlockSpec((HB, SWA_W), lambda j: (rev(j), 0)),
                                    pl.BlockSpec((HB // 2, 2 * KV_W), lambda j: (rev(j), 0)),
                                    pl.BlockSpec((HB // 2, 2 * KV_W), lambda j: (rev(j), 0))],
        out_specs=(pl.BlockSpec((HB, D_IN), lambda j: (rev(j), 0)), pl.BlockSpec((8, HG_W), lambda j: (0, 0)),
                   pl.BlockSpec((8, HD), lambda j: (0, 0))),
        scratch_shapes=[pltpu.VMEM((4, HD, HD), F32)],
        compiler_params=_params(dimension_semantics=("arbitrary",)),
    )(*[z] * 8, hgrn_lb, onorm, o_save, sprev, dymix, *dza)


XB = 512


def _xattn_fwd(q, k, v, t):
    tb = min(XB, t)

    def body(q_ref, k_ref, v_ref, o_ref):
        for h in range(XH):
            cols = slice(XD * h, XD * (h + 1))
            s = _dot(q_ref[:, cols], k_ref[:, cols], 1, 1) * (XD ** -0.5)
            p = jnp.exp(s - jnp.max(s, axis=-1, keepdims=True))
            l = jnp.sum(p, axis=-1, keepdims=True)
            o_ref[:, cols] = (_dot(p.astype(BF16), v_ref[:, cols], 1, 0) * (1.0 / l)).astype(BF16)

    row = pl.BlockSpec((tb, D), lambda i: (i, 0))
    mem = pl.BlockSpec(k.shape, lambda i: (0, 0))
    return pl.pallas_call(
        body, name="xattn_fwd", out_shape=jax.ShapeDtypeStruct((t, D), BF16), grid=(t // tb,),
        in_specs=[row, mem, mem], out_specs=row, compiler_params=_params(),
    )(q, k, v)


def _xattn_bwd(q, k, v, do, t):
    tb = min(XB, t)

    def body(q_ref, k_ref, v_ref, do_ref, dq_ref, dk_ref, dv_ref):
        @pl.when(pl.program_id(0) == 0)
        def _():
            dk_ref[...] = jnp.zeros_like(dk_ref)
            dv_ref[...] = jnp.zeros_like(dv_ref)

        for h in range(XH):
            cols = slice(XD * h, XD * (h + 1))
            qh, kh, vh, doh = q_ref[:, cols], k_ref[:, cols], v_ref[:, cols], do_ref[:, cols]
            s = _dot(qh, kh, 1, 1) * (XD ** -0.5)
            p = jnp.exp(s - jnp.max(s, axis=-1, keepdims=True))
            p = p * (1.0 / jnp.sum(p, axis=-1, keepdims=True))
            dp = _dot(doh, vh, 1, 1)
            ds = (p * (dp - jnp.sum(p * dp, axis=-1, keepdims=True)) * (XD ** -0.5)).astype(BF16)
            dq_ref[:, cols] = _dot(ds, kh, 1, 0).astype(BF16)
            dk_ref[:, cols] += _dot(ds, qh, 0, 0)
            dv_ref[:, cols] += _dot(p.astype(BF16), doh, 0, 0)

    row = pl.BlockSpec((tb, D), lambda i: (i, 0))
    mem = pl.BlockSpec(k.shape, lambda i: (0, 0))
    return pl.pallas_call(
        body, name="xattn_bwd",
        out_shape=(jax.ShapeDtypeStruct((t, D), BF16), jax.ShapeDtypeStruct(k.shape, F32),
                   jax.ShapeDtypeStruct(k.shape, F32)),
        grid=(t // tb,), in_specs=[row, mem, mem, row], out_specs=(row, mem, mem),
        compiler_params=_params(dimension_semantics=("arbitrary",)),
    )(q, k, v, do)


def _mem_kv(mem, g_mem, wk, wv):
    def body(m_ref, g_ref, wk_ref, wv_ref, mn_ref, k_ref, v_ref):
        m_ = m_ref[...]
        mn = (m_ * _rstd(m_) * g_ref[...]).astype(BF16)
        mn_ref[...] = mn
        k_ref[...] = _dot(mn, wk_ref[...], 1, 0).astype(BF16)
        v_ref[...] = _dot(mn, wv_ref[...], 1, 0).astype(BF16)

    return pl.pallas_call(body, name="mem_kv", out_shape=(jax.ShapeDtypeStruct(mem.shape, BF16),) * 3,
                          compiler_params=_params())(mem, g_mem, wk, wv)


def _mem_kv_bwd(mn, mem, dk, dv, wk, wv, dep=None):
    deps = [] if dep is None else [dep]

    def body(mn_ref, m_ref, dk_ref, dv_ref, wk_ref, wv_ref, *rest):
        gk_ref, gv_ref, dg_ref = rest[len(deps):]
        mn = mn_ref[...]
        dkb, dvb = dk_ref[...].astype(BF16), dv_ref[...].astype(BF16)
        gk_ref[...] = _dot(mn, dkb, 0, 0).astype(BF16)
        gv_ref[...] = _dot(mn, dvb, 0, 0).astype(BF16)
        dmn = _dot(dkb, wk_ref[...], 1, 1) + _dot(dvb, wv_ref[...], 1, 1)
        m_ = m_ref[...]
        dg_ref[...] = _rowsum8(dmn * (m_ * _rstd(m_)))

    vmem = pl.BlockSpec(memory_space=pltpu.VMEM)
    return pl.pallas_call(
        body, name="mem_kv_bwd",
        out_shape=(jax.ShapeDtypeStruct(wk.shape, BF16), jax.ShapeDtypeStruct(wv.shape, BF16),
                   jax.ShapeDtypeStruct((8, D), F32)),
        in_specs=[vmem] * 6 + [ANY_SPEC] * len(deps), out_specs=(vmem,) * 3, compiler_params=_params(),
    )(mn, mem, dk, dv, wk, wv, *deps)


FM, FN = 1024, 1408


def _ffn_up(u, wgt, wut, t):
    tm = min(FM, t)

    def body(u_ref, wg_ref, wu_ref, g_ref, up_ref, a_ref):
        u_ = u_ref[...]
        g = _dot(u_, wg_ref[...], 1, 1)
        up = _dot(u_, wu_ref[...], 1, 1)
        g_ref[...] = g.astype(BF16)
        up_ref[...] = up.astype(BF16)
        a_ref[...] = (g * _sig(g) * up).astype(BF16)

    w = pl.BlockSpec((FN, D), lambda j, i: (j, 0))
    o = pl.BlockSpec((tm, FN), lambda j, i: (i, j))
    return pl.pallas_call(
        body, name="ffn_up", out_shape=(jax.ShapeDtypeStruct((t, D_FF), BF16),) * 3,
        grid=(D_FF // FN, t // tm), in_specs=[pl.BlockSpec((tm, D), lambda j, i: (i, 0)), w, w],
        out_specs=(o, o, o), compiler_params=_params(),
    )(u, wgt, wut)


def _ffn_down_bwd(dy, wd, gate, up, t, dep=None):
    tm = min(FM, t)
    deps = [] if dep is None else [dep]

    def body(dy_ref, w_ref, g_ref, up_ref, *rest):
        dg_ref, dup_ref = rest[len(deps):]
        da = _dot(dy_ref[...], w_ref[...], 1, 1)
        g = g_ref[...].astype(F32)
        sg = _sig(g)
        dup_ref[...] = (da * g * sg).astype(BF16)
        dg_ref[...] = (da * up_ref[...].astype(F32) * (sg * (1.0 + g * (1.0 - sg)))).astype(BF16)

    o = pl.BlockSpec((tm, FN), lambda j, i: (i, j))
    return pl.pallas_call(
        body, name="ffn_down_bwd", out_shape=(jax.ShapeDtypeStruct((t, D_FF), BF16),) * 2,
        grid=(D_FF // FN, t // tm),
        in_specs=[pl.BlockSpec((tm, D), lambda j, i: (i, 0)), pl.BlockSpec((FN, D), lambda j, i: (j, 0)), o, o]
        + [ANY_SPEC] * len(deps),
        out_specs=(o, o), compiler_params=_params(),
    )(dy, wd, gate, up, *deps)


def _local_step(x, mem, target, fetch, sm, emit=None, first_dep=None, milestone=None):
    t = x.shape[0]
    w, gw = {}, {}

    def out(key, g):
        gw[key] = g
        return None if emit is None else emit(key, g)

    def tell(tag, value):
        return None if milestone is None else milestone(tag, value)
    u1 = _prenorm(x, sm["g_mix_pre"], name="prenorm_mix", dep=first_dep)
    w["winT"] = fetch("winT", u1)
    z = _mm(u1, w["winT"], tb=True, out_dtype=F32, tm=1024, tn=1408, name="mm_z", n_outer=True)
    ymix, lse = _swa_fwd(z, sm["sinks"], t, dep=tell("z", z))
    ymix, o_h, sprev = _hgrn2_fwd(z, sm["hgrn_lb"], sm["hgrn_onorm"], ymix, t, dep=tell("swa", lse))
    w["wout"] = fetch("wout", ymix)
    y1, h1, u2 = _mm_rows([(ymix, w["wout"], False)], [x], [sm["g_mix_post"], sm["g_x_pre"]], _ep_post_pre,
                          _EP_POST_PRE_OUTS, tm=1024, name="mm_y1_post")
    for key in ("wq", "wk", "wv"):
        w[key] = fetch(key, u2)
    qx = _mm(u2, w["wq"], out_dtype=BF16, tm=1024, tn=1024, name="mm_qx")
    mn, kx, vx = _mem_kv(mem, sm["g_mem"], w["wk"], w["wv"])
    ox = _xattn_fwd(qx, kx, vx, t)
    w["wo"] = fetch("wo", ox)
    y2, h2, u3 = _mm_rows([(ox, w["wo"], False)], [h1], [sm["g_x_post"], sm["g_ffn_pre"]], _ep_post_pre,
                          _EP_POST_PRE_OUTS, tm=1024, name="mm_y2_post", dep=tell("ox", ox))
    w["wgT"], w["wuT"] = fetch("wgT", u3), fetch("wuT", u3)
    gate, up, act = _ffn_up(u3, w["wgT"], w["wuT"], t)
    w["wd"] = fetch("wd", act)
    sq, dh3, dy3, dg_ffn_post = _mm_rows([(act, w["wd"], False)], [h2, target], [sm["g_ffn_post"]], _ep_final_loss,
                                         _EP_FINAL_LOSS_OUTS, tm=512, name="mm_y3_loss")
    dep = out("wd", _mm(act, dy3, ta=True, out_dtype=BF16, tm=1408, tn=1024, name="mm_gwd"))
    dgate, dup = _ffn_down_bwd(dy3, w["wd"], gate, up, t, dep=dep)
    dep = out("wgT", _mm(dgate, u3, ta=True, out_dtype=BF16, tm=1408, tn=1024, name="mm_gwg"))
    dep = out("wuT", _mm(dup, u3, ta=True, out_dtype=BF16, tm=1408, tn=1024, name="mm_gwu", dep=dep))
    dh2, dy2, dg_ffn_pre, dg_x_post = _mm_rows(
        [(dgate, w["wgT"], False), (dup, w["wuT"], False)], [dh3, h2, y2], [sm["g_x_post"], sm["g_ffn_pre"]],
        _ep_post_pre_bwd, _EP_POST_PRE_BWD_OUTS, tm=512, name="mm_du3_post_bwd", dep=dep)
    dep = out("wo", _mm(ox, dy2, ta=True, out_dtype=BF16, tm=512, tn=1024, name="mm_gwo"))
    dox = _mm(dy2, w["wo"], tb=True, out_dtype=BF16, tm=1024, tn=1024, name="mm_dox", dep=dep)
    dqx, dkx, dvx = _xattn_bwd(qx, kx, vx, dox, t)
    dep = out("wq", _mm(u2, dqx, ta=True, out_dtype=BF16, tm=512, tn=1024, name="mm_gwq"))
    gwk, gwv, dg_mem = _mem_kv_bwd(mn, mem, dkx, dvx, w["wk"], w["wv"], dep=dep)
    out("wk", gwk)
    dep = out("wv", gwv)
    dh1, dy1, dg_x_pre, dg_mix_post = _mm_rows(
        [(dqx, w["wq"], True)], [dh2, h1, y1], [sm["g_mix_post"], sm["g_x_pre"]],
        _ep_post_pre_bwd, _EP_POST_PRE_BWD_OUTS, tm=1024, name="mm_du2_post_bwd", dep=dep)
    dep = out("wout", _mm(ymix, dy1, ta=True, out_dtype=BF16, tm=512, tn=1024, name="mm_gwout"))
    dymix = _mm(dy1, w["wout"], tb=True, out_dtype=BF16, tm=1024, tn=1024, name="mm_dymix", dep=dep)
    *dza, dsinks = _swa_bwd(z, sm["sinks"], ymix, lse, dymix, t)
    dz, dlb, donorm = _hgrn2_bwd(z, sm["hgrn_lb"], sm["hgrn_onorm"], o_h, sprev, dymix, dza, t)
    dep = out("winT", _mm(dz, u1, ta=True, out_dtype=BF16, tm=1408, tn=1024, name="mm_gwin"))
    grad_x, dg_mix_pre = _mm_rows([(dz, w["winT"], False)], [dh1, x], [sm["g_mix_pre"]], _ep_pre_bwd,
                                  _EP_PRE_BWD_OUTS, tm=512, name="mm_du1_pre_bwd", dep=dep)
    parts = dict(g_mix_pre=dg_mix_pre, g_mix_post=dg_mix_post, g_mem=dg_mem, g_x_pre=dg_x_pre,
                 g_x_post=dg_x_post, g_ffn_pre=dg_ffn_pre, g_ffn_post=dg_ffn_post,
                 hgrn_onorm=donorm, hgrn_lb=dlb, sinks=dsinks, sq=sq)
    return grad_x, gw, parts


def _position():
    return lax.axis_index("x"), lax.axis_index("y"), lax.axis_index("c")


def _peer(pos, k):
    x, y, c = pos
    return (1 - x if k & 4 else x, 1 - y if k & 2 else y, 1 - c if k & 1 else c)


def _linear(pos):
    x, y, c = pos
    return 4 * x + 2 * y + c


HBM_SPEC = pl.BlockSpec(memory_space=pltpu.HBM)
SEM_SPEC = pl.BlockSpec(memory_space=pltpu.SEMAPHORE)
DATAFLOW = pltpu.SideEffectType.DATAFLOW_SIDE_EFFECTING
SEND_ORDER = (1, 2, 4, 3, 5, 6, 7)


def _in_hbm(a):
    return pltpu.with_memory_space_constraint(a, pltpu.HBM)


def _prepare_weights(shards, *, name, dep=None):
    n = len(shards)
    deps = [] if dep is None else [dep]

    def body(*refs):
        ins, (outs, lands, sem) = refs[:n], (refs[-2 * n - 1:-n - 1], refs[-n - 1:-1], refs[-1])
        me_lin = _linear(_position())
        copies = []
        for a in range(n):
            r = ins[a].shape[0]
            outs[a][...] = ins[a][...].astype(BF16)
            copies.append(pltpu.make_async_copy(outs[a], lands[a].at[pl.ds(me_lin * r, r), :], sem.at[a]))
            copies[-1].start()
        for cp in copies:
            cp.wait()

    vmem = pl.BlockSpec(memory_space=pltpu.VMEM)
    res = pl.pallas_call(
        body, name=name,
        out_shape=tuple(jax.ShapeDtypeStruct(s.shape, BF16) for s in shards)
        + tuple(jax.ShapeDtypeStruct((N_DEV * s.shape[0], s.shape[1]), BF16) for s in shards),
        in_specs=[vmem] * n + [ANY_SPEC] * len(deps), out_specs=tuple([vmem] * n + [ANY_SPEC] * n),
        scratch_shapes=[pltpu.SemaphoreType.DMA((n,))], compiler_params=_params(),
    )(*shards, *deps)
    return res[:n], res[n:]


def _copies_start(arrays, plan, n, *, name):
    na = len(arrays)

    def body(*refs):
        ins, send_sems, recv_sems = refs[:na], refs[na], refs[na + 1]
        me = _position()
        for j in range(n):
            src, dst, peer, _ = plan(ins, me, j)
            pltpu.make_async_remote_copy(src_ref=src, dst_ref=dst, send_sem=send_sems.at[j], recv_sem=recv_sems.at[j],
                                         device_id=peer, device_id_type=MESH).start()

    return pl.pallas_call(
        body, name=name,
        out_shape=(pltpu.SemaphoreType.DMA((n,)), pltpu.SemaphoreType.DMA((n,)))
        + tuple(pltpu.HBM(a.shape, a.dtype) for a in arrays),
        in_specs=(HBM_SPEC,) * na, out_specs=(SEM_SPEC, SEM_SPEC) + (HBM_SPEC,) * na,
        input_output_aliases={i: 2 + i for i in range(na)},
        compiler_params=pltpu.CompilerParams(has_side_effects=DATAFLOW),
    )(*[_in_hbm(a) for a in arrays])


def _copies_wait(send_sems, recv_sems, arrays, plan, n, after, *, name):
    na = len(arrays)

    def body(*refs):
        ins, send_sems, recv_sems = refs[:na], refs[na], refs[na + 1]
        me = _position()
        for j in range(n):
            src, _, peer, landed = plan(ins, me, j)
            copy = pltpu.make_async_remote_copy(src_ref=src, dst_ref=landed, send_sem=send_sems.at[j],
                                                recv_sem=recv_sems.at[j], device_id=peer, device_id_type=MESH)
            copy.wait_send()
            copy.wait_recv()

    return pl.pallas_call(
        body, name=name, out_shape=tuple(pltpu.HBM(a.shape, a.dtype) for a in arrays),
        in_specs=(HBM_SPEC,) * na + (SEM_SPEC, SEM_SPEC, ANY_SPEC), out_specs=(HBM_SPEC,) * na,
        input_output_aliases={i: i for i in range(na)},
        compiler_params=pltpu.CompilerParams(has_side_effects=DATAFLOW),
    )(*arrays, send_sems, recv_sems, after)


SAME_CORE = (2, 4, 6)


class _TwoLevelGather:
    def __init__(self, shards, lands, *, name):
        n = self.n = len(shards)
        self.name = name
        first_peers = (1,) + SAME_CORE

        def rows(ref, pos):
            r = ref.shape[0] // N_DEV
            return ref.at[pl.ds(_linear(pos) * r, r), :]

        def first(refs, me, j):
            a, peer = j // 4, _peer(me, first_peers[j % 4])
            return refs[a], rows(refs[n + a], me), peer, rows(refs[n + a], peer)

        def second(refs, me, j):
            a, sibling = j // 3, _peer(me, 1)
            mine = rows(refs[a], _peer(me, SAME_CORE[j % 3]))
            return mine, mine, sibling, rows(refs[a], _peer(sibling, SAME_CORE[j % 3]))

        self._first, self._second = first, second
        self._flight = _copies_start(list(shards) + list(lands), first, 4 * n, name=name + "_send")
        self.dep = self._flight[2]

    def pass_on(self, after):
        send1, recv1, *arrays = self._flight
        arrays = _copies_wait(send1, recv1, arrays, self._first, 4 * self.n, after, name=self.name + "_recv")
        self._flight = _copies_start(list(arrays[self.n:]), self._second, 3 * self.n, name=self.name + "_pass")
        return self._flight[2]

    def finish(self, after):
        send2, recv2, *lands = self._flight
        return _copies_wait(send2, recv2, lands, self._second, 3 * self.n, after, name=self.name + "_pass_recv")


def _exchange_start(gs, *, name):
    n = len(gs)
    rows = [g.shape[0] // N_DEV for g in gs]
    lands = [lax.empty((N_DEV - 1, r, g.shape[1]), g.dtype) for g, r in zip(gs, rows)]

    def body(*refs):
        g_refs, land_refs = refs[:n], refs[n:2 * n]
        send_sems, recv_sems = refs[2 * n:3 * n], refs[3 * n:4 * n]
        me = _position()
        for a in range(n):
            for k in SEND_ORDER:
                peer = _peer(me, k)
                pltpu.make_async_remote_copy(
                    src_ref=g_refs[a].at[pl.ds(_linear(peer) * rows[a], rows[a]), :],
                    dst_ref=land_refs[a].at[k - 1],
                    send_sem=send_sems[a].at[k - 1], recv_sem=recv_sems[a].at[k - 1],
                    device_id=peer, device_id_type=MESH).start()

    res = pl.pallas_call(
        body, name=name,
        out_shape=tuple(pltpu.SemaphoreType.DMA((N_DEV - 1,)) for _ in range(2 * n))
        + tuple(pltpu.HBM(a.shape, a.dtype) for a in gs + lands),
        in_specs=(HBM_SPEC,) * (2 * n), out_specs=(SEM_SPEC,) * (2 * n) + (HBM_SPEC,) * (2 * n),
        input_output_aliases={i: 2 * n + i for i in range(2 * n)},
        compiler_params=pltpu.CompilerParams(has_side_effects=DATAFLOW),
    )(*[_in_hbm(a) for a in gs + lands])
    return [(res[a], res[n + a], res[2 * n + a], res[3 * n + a]) for a in range(n)]


def _exchange_wait(send_sems, recv_sems, g_thru, land_thru, after, *, name):
    r = land_thru.shape[1]

    def body(g_ref, land_ref, send_sems, recv_sems, after_ref, g_dead, got_ref):
        del after_ref, g_dead, got_ref
        me = _position()
        for k in SEND_ORDER:
            peer = _peer(me, k)
            copy = pltpu.make_async_remote_copy(
                src_ref=g_ref.at[pl.ds(_linear(peer) * r, r), :], dst_ref=land_ref.at[k - 1],
                send_sem=send_sems.at[k - 1], recv_sem=recv_sems.at[k - 1],
                device_id=peer, device_id_type=MESH)
            copy.wait_send()
            copy.wait_recv()

    return pl.pallas_call(
        body, name=name,
        out_shape=(pltpu.HBM(g_thru.shape, g_thru.dtype), pltpu.HBM(land_thru.shape, land_thru.dtype)),
        in_specs=(HBM_SPEC, HBM_SPEC, SEM_SPEC, SEM_SPEC, pl.BlockSpec(memory_space=pl.ANY)),
        out_specs=(HBM_SPEC, HBM_SPEC), input_output_aliases={0: 0, 1: 1},
        compiler_params=pltpu.CompilerParams(has_side_effects=DATAFLOW),
    )(g_thru, land_thru, send_sems, recv_sems, after)


def _adamw_math(w, g, m, v):
    m = B1 * m + (1.0 - B1) * g
    v = B2 * v + (1.0 - B2) * (g * g)
    delta = -LR * ((m / C1) / (jnp.sqrt(v / C2) + AEPS) + WD * w)
    return delta, m, v


def _sum_adamw(items, *, name):
    n = len(items)

    def body(*refs):
        ins, outs, scratch = refs[:5 * n], refs[5 * n:9 * n], refs[9 * n:]
        me_lin = _linear(_position())
        mine = []
        for a in range(n):
            r = items[a][2].shape[0]
            mine.append(pltpu.make_async_copy(ins[5 * a].at[pl.ds(me_lin * r, r), :], scratch[a], scratch[n].at[a]))
            mine[-1].start()
        for a in range(n):
            _, land_ref, w_ref, m_ref, v_ref = ins[5 * a:5 * a + 5]
            g_ref, d_ref, nm_ref, nv_ref = outs[4 * a:4 * a + 4]
            g = land_ref[0].astype(F32)
            for s in range(1, N_DEV - 1):
                g = g + land_ref[s].astype(F32)
            mine[a].wait()
            g = scratch[a][...].astype(F32) + g
            g_ref[...] = g
            d_ref[...], nm_ref[...], nv_ref[...] = _adamw_math(w_ref[...], g, m_ref[...], v_ref[...])

    vmem = pl.BlockSpec(memory_space=pltpu.VMEM)
    res = pl.pallas_call(
        body, name=name,
        out_shape=tuple(jax.ShapeDtypeStruct(it[2].shape, F32) for it in items for _ in range(4)),
        in_specs=[ANY_SPEC, vmem, vmem, vmem, vmem] * n, out_specs=(vmem,) * (4 * n),
        scratch_shapes=[pltpu.VMEM(it[2].shape, BF16) for it in items] + [pltpu.SemaphoreType.DMA((n,))],
        compiler_params=_params(),
    )(*[a for it in items for a in it])
    return [res[4 * a:4 * a + 4] for a in range(n)]


SMALL = ("g_mix_pre", "g_mix_post", "g_mem", "g_x_pre", "g_x_post", "g_ffn_pre", "g_ffn_post",
         "hgrn_onorm", "hgrn_lb", "sinks")
SMALL_W = dict(hgrn_onorm=HD, hgrn_lb=HG_W, sinks=8)
SQ_ROW = len(SMALL)
PACK_ROWS = 16


def _small_pack(parts):
    ns = len(SMALL)

    def body(*refs):
        part, mine, slots, sem = refs[:ns + 1], refs[ns + 1], refs[ns + 2], refs[ns + 3]
        mine[...] = jnp.zeros((PACK_ROWS, D), F32)
        for r, name in enumerate(SMALL):
            wd = SMALL_W.get(name, D)
            mine[r:r + 1, 0:wd] = jnp.sum(part[r][...], axis=0, keepdims=True)[:, 0:wd]
        sq = jnp.sum(part[ns][...]) * (0.5 / D)
        mine[SQ_ROW:SQ_ROW + 1, :] = jnp.full((1, D), sq, F32)
        own = pltpu.make_async_copy(mine, slots.at[_linear(_position())], sem)
        own.start()
        own.wait()

    vmem = pl.BlockSpec(memory_space=pltpu.VMEM)
    return pl.pallas_call(
        body, name="small_pack",
        out_shape=(jax.ShapeDtypeStruct((PACK_ROWS, D), F32), jax.ShapeDtypeStruct((N_DEV, PACK_ROWS, D), F32)),
        in_specs=[vmem] * (ns + 1), out_specs=(vmem, ANY_SPEC),
        scratch_shapes=[pltpu.SemaphoreType.DMA(())], compiler_params=_params(),
    )(*[parts[n] for n in SMALL], parts["sq"])


def _small_exchange(mine, slots):
    def plan(refs, me, j):
        peer = _peer(me, j + 1)
        return refs[0], refs[1].at[_linear(me)], peer, refs[1].at[_linear(peer)]

    send, recv, mine1, slots1 = _copies_start([mine, slots], plan, N_DEV - 1, name="small_send")
    return lambda after: _copies_wait(send, recv, [mine1, slots1], plan, N_DEV - 1, after, name="small_recv")[1]


def _small_update(slots, sm, m_sm, v_sm):
    ns = len(SMALL)

    def body(*refs):
        tot = refs[0][0]
        for s in range(1, N_DEV):
            tot = tot + refs[0][s]
        w_refs, m_refs, v_refs = refs[1:ns + 1], refs[ns + 1:2 * ns + 1], refs[2 * ns + 1:3 * ns + 1]
        outs = refs[3 * ns + 1:]
        loss_ref = outs[0]
        g_out, d_out = outs[1:ns + 1], outs[ns + 1:2 * ns + 1]
        nm_out, nv_out = outs[2 * ns + 1:3 * ns + 1], outs[3 * ns + 1:4 * ns + 1]
        loss_ref[...] = tot[SQ_ROW:SQ_ROW + 1, 0:1]
        for r, name in enumerate(SMALL):
            wd = SMALL_W.get(name, D)
            g = tot[r:r + 1, 0:wd]
            w = w_refs[r][...]
            if name == "hgrn_lb":
                mx = jnp.maximum(w[0:1], w[1:2])
                e0, e1 = jnp.exp(w[0:1] - mx), jnp.exp(w[1:2] - mx)
                lb0 = e0 / (e0 + e1)
                g0 = g * lb0 * (1.0 - lb0)
                for i, gi in enumerate((g0, -g0)):
                    d, nm, nv = _adamw_math(w[i:i + 1], gi, m_refs[r][i:i + 1, :], v_refs[r][i:i + 1, :])
                    g_out[r][i:i + 1, :] = gi
                    d_out[r][i:i + 1, :], nm_out[r][i:i + 1, :], nv_out[r][i:i + 1, :] = d, nm, nv
            else:
                d, nm, nv = _adamw_math(w, g, m_refs[r][...], v_refs[r][...])
                g_out[r][...] = g
                d_out[r][...], nm_out[r][...], nv_out[r][...] = d, nm, nv

    shapes = [jax.ShapeDtypeStruct(sm[n].shape, F32) for n in SMALL]
    res = pl.pallas_call(
        body, name="small_update", out_shape=tuple([jax.ShapeDtypeStruct((1, 1), F32)] + shapes * 4),
        compiler_params=_params(),
    )(slots, *[sm[n] for n in SMALL], *[m_sm[n] for n in SMALL], *[v_sm[n] for n in SMALL])
    groups = [dict(zip(SMALL, res[1 + i * ns:1 + (i + 1) * ns])) for i in range(4)]
    return res[0], groups[0], groups[1], groups[2], groups[3]


BIG = ("w_in", "w_gate", "w_up", "w_down", "w_out", "wq_x", "wk_x", "wv_x", "wo_x")
BIG_KEY = dict(w_in="winT", w_gate="wgT", w_up="wuT", w_down="wd", w_out="wout", wq_x="wq", wk_x="wk",
               wv_x="wv", wo_x="wo")
TRANSPOSED = ("w_in", "w_gate", "w_up")
WEIGHTS = ("w_in", "sinks", "hgrn_lb", "hgrn_onorm", "w_out", "g_mix_pre", "g_mix_post", "g_mem", "g_x_pre",
           "g_x_post", "wq_x", "wk_x", "wv_x", "wo_x", "g_ffn_pre", "g_ffn_post", "w_gate", "w_up", "w_down")


def kernel(x, mem, w_in, sinks, hgrn_lb, hgrn_onorm, w_out, g_mix_pre, g_mix_post, g_mem, g_x_pre, g_x_post, wq_x, wk_x, wv_x, wo_x, g_ffn_pre, g_ffn_post, w_gate, w_up, w_down, loss_target, m_w_in, m_sinks, m_hgrn_lb, m_hgrn_onorm, m_w_out, m_g_mix_pre, m_g_mix_post, m_g_mem, m_g_x_pre, m_g_x_post, m_wq_x, m_wk_x, m_wv_x, m_wo_x, m_g_ffn_pre, m_g_ffn_post, m_w_gate, m_w_up, m_w_down, v_w_in, v_sinks, v_hgrn_lb, v_hgrn_onorm, v_w_out, v_g_mix_pre, v_g_mix_post, v_g_mem, v_g_x_pre, v_g_x_post, v_wq_x, v_wk_x, v_wv_x, v_wo_x, v_g_ffn_pre, v_g_ffn_post, v_w_gate, v_w_up, v_w_down):
    given = dict(locals())
    wts = {n: given[n] for n in WEIGHTS}
    ms = {n: given["m_" + n] for n in WEIGHTS}
    vs = {n: given["v_" + n] for n in WEIGHTS}

    def mat(a, name):
        a = a[0]
        return a.T if name in TRANSPOSED else a

    groups = (("w_in",), ("w_out", "wq_x", "wk_x", "wv_x", "wo_x"), ("w_gate", "w_up", "w_down"))
    gathers = []

    def start_group(g, dep):
        tag = ("w_in", "w_attn", "w_ffn")[g]
        shards, lands = _prepare_weights([mat(wts[n], n) for n in groups[g]], name="prepare_" + tag, dep=dep)
        gathers.append(_TwoLevelGather(shards, lands, name=tag))
        return gathers[-1].dep

    first_dep = start_group(1, start_group(0, None))
    name_of = {k: n for n, k in BIG_KEY.items()}
    gathered = {}

    def milestone(tag, value):
        if tag == "z":
            return start_group(2, value)
        return gathers[{"swa": 1, "ox": 2}[tag]].pass_on(value)

    def fetch(key, after):
        name = name_of[key]
        if name not in gathered:
            g = [i for i, group in enumerate(groups) if name in group][0]
            if g == 0:
                gathers[0].pass_on(after)
            gathered.update(zip(groups[g], gathers[g].finish(after)))
        return gathered[name]

    sm = {n: wts[n] for n in SMALL}
    started, held = {}, {}
    send_with = {k: group for group in (("wgT", "wuT"), ("wo", "wq", "wk", "wv")) for k in group}

    def emit(key, g):
        held[key] = g
        group = send_with.get(key, (key,))
        if key != group[-1]:
            return None
        flights = _exchange_start([held[k] for k in group], name="grad_send_" + name_of[group[0]])
        started.update({name_of[k]: f for k, f in zip(group, flights)})
        return flights[-1][2]

    grad_x, _, parts = _local_step(x[0], mem[0], loss_target[0], fetch, sm, emit, first_dep=first_dep, milestone=milestone)
    small_finish = _small_exchange(*_small_pack(parts))
    grads, deltas, new_m, new_v = {}, {}, {}, {}
    after = grad_x
    for group in (("w_down",), ("w_gate", "w_up"), ("wo_x", "wq_x", "wk_x", "wv_x", "w_out"), ("w_in",)):
        items = []
        for n in group:
            g_all, land = _exchange_wait(*started[n], after, name="grad_recv_" + n)
            items.append((g_all, land, mat(wts[n], n), mat(ms[n], n), mat(vs[n], n)))
            after = land
        for n, res in zip(group, _sum_adamw(items, name="adamw_" + group[0])):
            after = res[1]
            if n in TRANSPOSED:
                res = [a.T for a in res]
            grads[n], deltas[n], new_m[n], new_v[n] = [a[None] for a in res]
    loss, g_s, d_s, m_s, v_s = _small_update(small_finish(after), sm, {n: ms[n] for n in SMALL},
                                             {n: vs[n] for n in SMALL})
    grads.update(g_s), deltas.update(d_s), new_m.update(m_s), new_v.update(v_s)
    return (loss[0, 0], grad_x[None], *[grads[n] for n in WEIGHTS], *[deltas[n] for n in WEIGHTS],
            *[new_m[n] for n in WEIGHTS], *[new_v[n] for n in WEIGHTS])
```

```python
import functools

import jax
import jax.numpy as jnp
from jax import lax
from jax.experimental import pallas as pl
from jax.experimental.pallas import tpu as pltpu

F32 = jnp.float32
BF16 = jnp.bfloat16

D = 1024
D_IN = 2816
D_FF = 2816
CHUNK = 64
SWA_W = 512
KV_W = 128
HG_W = 512
HD = 128
ZQH, ZFH, ZIH, ZGH = 768, 1280, 1792, 2304
XH, XD = 4, 256
EPS = 1e-6
NEG = -1e30
N_DEV = 8
MESH = pl.DeviceIdType.MESH

LR, B1, B2, AEPS, WD, STEP = 0.001, 0.9, 0.999, 1e-08, 0.01, 10
C1 = 1.0 - B1 ** STEP
C2 = 1.0 - B2 ** STEP

VMEM_LIMIT = 56 * 1024 * 1024


def _params(**kw):
    return pltpu.CompilerParams(vmem_limit_bytes=VMEM_LIMIT, **kw)


def _sig(x):
    return 1.0 / (1.0 + jnp.exp(-x))


def _rowsum8(x):
    r, w = x.shape
    return jnp.sum(x.reshape(r // 8, 8, w), axis=0)


def _dot(a, b, ca, cb, precision=None):
    return lax.dot_general(a, b, (((ca,), (cb,)), ((), ())), preferred_element_type=F32,
                           precision=precision)


ANY_SPEC = pl.BlockSpec(memory_space=pl.ANY)


def _mm(a, b, *, ta=False, tb=False, out_dtype, tm, tn, tk=None, name, dep=None, n_outer=False):
    m = a.shape[1] if ta else a.shape[0]
    k = a.shape[0] if ta else a.shape[1]
    n = b.shape[0] if tb else b.shape[1]
    tm, tn = min(tm, m), min(tn, n)
    tk = k if tk is None else min(tk, k)
    nk = k // tk
    assert m % tm == 0 and n % tn == 0 and k % tk == 0, (name, m, n, k, tm, tn, tk)
    ij = (lambda g0, g1: (g1, g0)) if n_outer else (lambda g0, g1: (g0, g1))
    a_spec = (pl.BlockSpec((tk, tm), lambda g0, g1, kk: (kk, ij(g0, g1)[0])) if ta
              else pl.BlockSpec((tm, tk), lambda g0, g1, kk: (ij(g0, g1)[0], kk)))
    b_spec = (pl.BlockSpec((tn, tk), lambda g0, g1, kk: (ij(g0, g1)[1], kk)) if tb
              else pl.BlockSpec((tk, tn), lambda g0, g1, kk: (kk, ij(g0, g1)[1])))
    ca, cb = (0 if ta else 1), (1 if tb else 0)

    deps = [] if dep is None else [dep]

    def body(a_ref, b_ref, *rest):
        o_ref, acc = rest[len(deps)], rest[len(deps) + 1:]
        p = _dot(a_ref[...].astype(BF16), b_ref[...].astype(BF16), ca, cb)
        if nk == 1:
            o_ref[...] = p.astype(out_dtype)
        else:
            acc_ref, = acc
            kk = pl.program_id(2)

            @pl.when(kk == 0)
            def _():
                acc_ref[...] = p

            @pl.when(kk > 0)
            def _():
                acc_ref[...] += p

            @pl.when(kk == nk - 1)
            def _():
                o_ref[...] = acc_ref[...].astype(out_dtype)

    return pl.pallas_call(
        body, name=name, out_shape=jax.ShapeDtypeStruct((m, n), out_dtype),
        grid=(n // tn, m // tm, nk) if n_outer else (m // tm, n // tn, nk),
        in_specs=[a_spec, b_spec] + [ANY_SPEC] * len(deps),
        out_specs=pl.BlockSpec((tm, tn), lambda g0, g1, kk: ij(g0, g1)),
        scratch_shapes=[pltpu.VMEM((tm, tn), F32)] if nk > 1 else [],
        compiler_params=_params(dimension_semantics=("parallel", "parallel", "arbitrary")),
    )(a, b, *deps)


def _mm_rows(prods, rows_in, vecs_in, epilogue, outs, *, tm, name, dep=None):
    m = prods[0][0].shape[0]
    n = prods[0][1].shape[0] if prods[0][2] else prods[0][1].shape[1]
    tm = min(tm, m)
    assert m % tm == 0
    deps = [] if dep is None else [dep]
    n_p, n_r, n_v = len(prods), len(rows_in), len(vecs_in)

    def body(*refs):
        ab = refs[:2 * n_p]
        row_refs = refs[2 * n_p:2 * n_p + n_r]
        vec_refs = refs[2 * n_p + n_r:2 * n_p + n_r + n_v]
        out_refs = refs[2 * n_p + n_r + n_v + len(deps):]
        p = None
        for j, (_, _, tb) in enumerate(prods):
            t = _dot(ab[2 * j][...].astype(BF16), ab[2 * j + 1][...], 1, 1 if tb else 0)
            p = t if p is None else p + t
        vals = epilogue(p, *[r[...] for r in row_refs], *[v[...] for v in vec_refs])
        for (dtype, kind), o_ref, val in zip(outs, out_refs, vals):
            if kind == "row":
                o_ref[...] = val.astype(dtype)
            else:
                @pl.when(pl.program_id(0) == 0)
                def _(o_ref=o_ref):
                    o_ref[...] = jnp.zeros_like(o_ref)

                o_ref[...] += val

    row = lambda w: pl.BlockSpec((tm, w), lambda i: (i, 0))
    whole = lambda a: pl.BlockSpec(a.shape, lambda i: (0,) * a.ndim, pipeline_mode=pl.Buffered(1))
    in_specs, args = [], []
    for a, b, _ in prods:
        in_specs += [row(a.shape[1]), whole(b)]
        args += [a, b]
    in_specs += [row(r.shape[1]) for r in rows_in] + [whole(v) for v in vecs_in] + [ANY_SPEC] * len(deps)
    return pl.pallas_call(
        body, name=name,
        out_shape=tuple(jax.ShapeDtypeStruct((m, n) if kind == "row" else (8, n), dtype) for dtype, kind in outs),
        grid=(m // tm,), in_specs=in_specs,
        out_specs=tuple(row(n) if kind == "row" else pl.BlockSpec((8, n), lambda i: (0, 0)) for _, kind in outs),
        compiler_params=_params(dimension_semantics=("arbitrary",)),
    )(*args, *rows_in, *vecs_in, *deps)


def _rstd(x):
    return lax.rsqrt(jnp.mean(x * x, axis=-1, keepdims=True) + EPS)


def _norm_bwd(xh, r, t):
    return r * (t - xh * jnp.mean(xh * t, axis=-1, keepdims=True))


ROW_F32, ROW_BF16, SUM_F32 = (F32, "row"), (BF16, "row"), (F32, "sum")


def _ep_post_pre(p, h, g_post, g_pre):
    y = p.astype(BF16)
    yf = y.astype(F32)
    hn = h + yf * _rstd(yf) * g_post
    return y, hn, hn * _rstd(hn) * g_pre


_EP_POST_PRE_OUTS = [ROW_BF16, ROW_F32, ROW_BF16]


def _ep_final_loss(y, h, target, g_post):
    r = _rstd(y)
    yh = y * r
    err = h + yh * g_post - target
    dh = err * (1.0 / D)
    return _rowsum8(err * err), dh, _norm_bwd(yh, r, dh * g_post), _rowsum8(dh * yh)


_EP_FINAL_LOSS_OUTS = [SUM_F32, ROW_F32, ROW_BF16, SUM_F32]


def _ep_post_pre_bwd(du, dh_out, hn, y, g_post, g_pre):
    r2 = _rstd(hn)
    xh = hn * r2
    dh = dh_out + _norm_bwd(xh, r2, du * g_pre)
    yf = y.astype(F32)
    r1 = _rstd(yf)
    yh = yf * r1
    return dh, _norm_bwd(yh, r1, dh * g_post), _rowsum8(du * xh), _rowsum8(dh * yh)


_EP_POST_PRE_BWD_OUTS = [ROW_F32, ROW_BF16, SUM_F32, SUM_F32]


def _ep_pre_bwd(du, dh_out, x, g):
    r = _rstd(x)
    xh = x * r
    return dh_out + _norm_bwd(xh, r, du * g), _rowsum8(du * xh)


_EP_PRE_BWD_OUTS = [ROW_F32, SUM_F32]


def _prenorm(x, g, *, name, dep=None):
    t, d = x.shape
    tb = min(512, t)
    deps = [] if dep is None else [dep]

    def body(x_ref, g_ref, *rest):
        xf = x_ref[...]
        rest[-1][...] = (xf * _rstd(xf) * g_ref[...]).astype(BF16)

    return pl.pallas_call(
        body, name=name, out_shape=jax.ShapeDtypeStruct((t, d), BF16), grid=(t // tb,),
        in_specs=[pl.BlockSpec((tb, d), lambda i: (i, 0)), pl.BlockSpec((1, d), lambda i: (0, 0))]
        + [ANY_SPEC] * len(deps),
        out_specs=pl.BlockSpec((tb, d), lambda i: (i, 0)), compiler_params=_params(),
    )(x, g, *deps)


QB = 256


def _half_mask(shape, e):
    lane = lax.broadcasted_iota(jnp.int32, shape, len(shape) - 1)
    return (lane // 64) == e


def _place(kv):
    sw = pltpu.roll(kv, 64, 1)
    m0 = _half_mask(kv.shape, 0)
    return [[jnp.where(m0, kv, 0.0).astype(BF16), jnp.where(m0, 0.0, sw).astype(BF16)],
            [jnp.where(m0, sw, 0.0).astype(BF16), jnp.where(m0, 0.0, kv).astype(BF16)]]


SQ = 128
SK = 256


def _swa_valid(i, sb):
    qc = lax.broadcasted_iota(jnp.int32, (SQ, SK), 0) // CHUNK
    kc = lax.broadcasted_iota(jnp.int32, (SQ, SK), 1) // CHUNK - 2
    return (kc <= qc) & (qc <= kc + 2) & (4 * i + 2 * sb + kc >= 0)


def _swa_fwd(z, sinks, t, dep=None):
    nb = t // QB
    deps = [] if dep is None else [dep]

    def body(s_ref, q_ref, kp_ref, kc_ref, vp_ref, vc_ref, *rest):
        o_ref, lse_ref = rest[-2:]
        i = pl.program_id(0)
        kpl = _place(jnp.concatenate([kp_ref[...], kc_ref[...]], axis=0))
        vpl = _place(jnp.concatenate([vp_ref[...], vc_ref[...]], axis=0))
        lane = lax.broadcasted_iota(jnp.int32, (SQ, 128), 1)
        for sb in range(QB // SQ):
            rows, keys = slice(SQ * sb, SQ * (sb + 1)), slice(SQ * sb, SQ * sb + SK)
            valid = _swa_valid(i, sb)
            lse_out = jnp.zeros((SQ, 128), F32)
            for j in range(4):
                qp = q_ref[rows, 128 * j:128 * (j + 1)].astype(BF16)
                acc = jnp.zeros((SQ, 128), F32)
                for e in range(2):
                    h = 2 * j + e
                    kvh = h // 4
                    qm = jnp.where(_half_mask(qp.shape, e), qp, jnp.zeros_like(qp))
                    s = _dot(qm, kpl[kvh][e][keys], 1, 1) * 0.125
                    s = jnp.where(valid, s, NEG)
                    sink = s_ref[0, h]
                    m = jnp.maximum(jnp.max(s, axis=-1, keepdims=True), sink)
                    p = jnp.exp(s - m)
                    l = jnp.sum(p, axis=-1, keepdims=True) + jnp.exp(sink - m)
                    acc = acc + _dot(p.astype(BF16), vpl[kvh][e][keys], 1, 0) * (1.0 / l)
                    lse_out = jnp.where(lane == h, m + jnp.log(l), lse_out)
                o_ref[rows, 128 * j:128 * (j + 1)] = acc.astype(BF16)
            lse_ref[rows, :] = lse_out

    prev = lambda c: pl.BlockSpec((128, 128), lambda i: (jnp.maximum(2 * i - 1, 0), c))
    cur = lambda c: pl.BlockSpec((QB, 128), lambda i: (i, c))
    return pl.pallas_call(
        body, name="swa_fwd",
        out_shape=(jax.ShapeDtypeStruct((t, D), BF16), jax.ShapeDtypeStruct((t, 128), F32)),
        grid=(nb,),
        in_specs=[pl.BlockSpec(memory_space=pltpu.SMEM),
                  pl.BlockSpec((QB, SWA_W), lambda i: (i, 0)), prev(4), cur(4), prev(5), cur(5)]
        + [ANY_SPEC] * len(deps),
        out_specs=(pl.BlockSpec((QB, SWA_W), lambda i: (i, 0)), pl.BlockSpec((QB, 128), lambda i: (i, 0))),
        compiler_params=_params(),
    )(sinks, z, z, z, z, z, *deps)


def _swa_bwd(z, sinks, ymix, lse, dymix, t):
    nb = t // QB

    def body(s_ref, q_ref, kp_ref, kc_ref, vp_ref, vc_ref, o_ref, do_ref, l_ref,
             dq_ref, first_ref, second_ref, ds_ref, carry_ref):
        i = pl.program_id(0)
        live = i < nb

        @pl.when(i == 0)
        def _():
            ds_ref[...] = jnp.zeros_like(ds_ref)
            carry_ref[...] = jnp.zeros_like(carry_ref)

        lane = lax.broadcasted_iota(jnp.int32, (8, 128), 1)
        kpl = _place(jnp.concatenate([kp_ref[...], kc_ref[...]], axis=0))
        vpl = _place(jnp.concatenate([vp_ref[...], vc_ref[...]], axis=0))
        nk = QB + 128
        qc = lax.broadcasted_iota(jnp.int32, (QB, nk), 0) // CHUNK
        kc = lax.broadcasted_iota(jnp.int32, (QB, nk), 1) // CHUNK - 2
        valid = (kc <= qc) & (qc <= kc + 2) & (4 * i + kc >= 0) & live
        lse_c = l_ref[...]
        dsink = jnp.zeros((8, 128), F32)
        dk_acc = [[jnp.zeros((128, nk), F32) for _ in range(2)] for _ in range(2)]
        dv_acc = [[jnp.zeros((128, nk), F32) for _ in range(2)] for _ in range(2)]
        dq = []
        for j in range(4):
            cols = slice(128 * j, 128 * (j + 1))
            qp = q_ref[:, cols].astype(BF16)
            dop = do_ref[:, cols]
            prod = dop.astype(F32) * o_ref[:, cols].astype(F32)
            acc = jnp.zeros((QB, 128), F32)
            for e in range(2):
                h = 2 * j + e
                kvh = h // 4
                hm = _half_mask(qp.shape, e)
                qm = jnp.where(hm, qp, jnp.zeros_like(qp))
                dom = jnp.where(hm, dop, jnp.zeros_like(dop))
                dd = jnp.sum(jnp.where(hm, prod, 0.0), axis=-1, keepdims=True)
                lse_h = lse_c[:, h:h + 1]
                s = _dot(qm, kpl[kvh][e], 1, 1) * 0.125
                p = jnp.where(valid, jnp.exp(s - lse_h), 0.0)
                dp = _dot(dom, vpl[kvh][e], 1, 1)
                ds = (p * (dp - dd) * 0.125).astype(BF16)
                acc = acc + _dot(ds, kpl[kvh][e], 1, 0)
                dk_acc[kvh][e] = dk_acc[kvh][e] + _dot(qm, ds, 0, 0)
                dv_acc[kvh][e] = dv_acc[kvh][e] + _dot(dom, p.astype(BF16), 0, 0)
                ps = jnp.where(live, jnp.exp(s_ref[0, h] - lse_h) * dd, 0.0)
                dsink = dsink - jnp.where(lane == h, _rowsum8(jnp.broadcast_to(ps, (QB, 128))), 0.0)
            dq.append(acc.astype(BF16))
        ds_ref[...] += dsink
        dk = (dk_acc[0][0] + dk_acc[1][1] + pltpu.roll(dk_acc[0][1] + dk_acc[1][0], 64, 0)).T
        dv = (dv_acc[0][0] + dv_acc[1][1] + pltpu.roll(dv_acc[0][1] + dv_acc[1][0], 64, 0)).T
        dkv = jnp.concatenate([dk, dv], axis=1)
        second_ref[...] = (carry_ref[...] + dkv[0:128]).astype(BF16)
        carry_ref[...] = dkv[256:384]

        @pl.when(live)
        def _():
            for j in range(4):
                dq_ref[:, 128 * j:128 * (j + 1)] = dq[j]
            first_ref[...] = dkv[128:256].astype(BF16)

    blk = lambda i: jnp.minimum(i, nb - 1)
    prev = lambda c: pl.BlockSpec((128, 128), lambda i: (jnp.maximum(2 * blk(i) - 1, 0), c))
    cur = lambda w, c: pl.BlockSpec((QB, w), lambda i: (blk(i), c))
    half = lambda index: pl.BlockSpec((128, 256), lambda i: (index(i), 0))
    return pl.pallas_call(
        body, name="swa_bwd",
        out_shape=(jax.ShapeDtypeStruct((t, SWA_W), BF16), jax.ShapeDtypeStruct((t // 2, 256), BF16),
                   jax.ShapeDtypeStruct((t // 2, 256), BF16), jax.ShapeDtypeStruct((8, 128), F32)),
        grid=(nb + 1,),
        in_specs=[pl.BlockSpec(memory_space=pltpu.SMEM),
                  cur(SWA_W, 0), prev(4), cur(128, 4), prev(5), cur(128, 5),
                  cur(SWA_W, 0), cur(SWA_W, 0), cur(128, 0)],
        out_specs=(cur(SWA_W, 0), half(blk), half(lambda i: jnp.maximum(i - 1, 0)),
                   pl.BlockSpec((8, 128), lambda i: (0, 0))),
        scratch_shapes=[pltpu.VMEM((128, 256), F32)],
        compiler_params=_params(dimension_semantics=("arbitrary",)),
    )(sinks, z, z, z, z, z, ymix, dymix, lse)


HB = 256


def _lower_bound(lb_ref):
    a = lb_ref[...]
    a0, a1 = a[0:1], a[1:2]
    mx = jnp.maximum(a0, a1)
    e0, e1 = jnp.exp(a0 - mx), jnp.exp(a1 - mx)
    return e0 / (e0 + e1)


def _hgrn_cols(row_block):
    return [pl.BlockSpec((HB, 2 * HD), lambda j, c=base // (2 * HD) + p: (row_block(j), c))
            for base in (ZQH, ZFH, ZIH, ZGH) for p in range(2)]


NCH = HB // CHUNK


def _split3(x):
    hi = x.astype(BF16)
    r1 = x - hi.astype(F32)
    mid = r1.astype(BF16)
    return hi, mid, (r1 - mid.astype(F32)).astype(BF16)


def _blockdiag(lower):
    r = lax.broadcasted_iota(jnp.int32, (HB, HB), 0)
    c = lax.broadcasted_iota(jnp.int32, (HB, HB), 1)
    return (r // CHUNK == c // CHUNK) & ((c <= r) if lower else (c >= r))


def _chunk_sums(mask_bf16, x):
    return sum(_dot(mask_bf16, part, 1, 0) for part in _split3(x))


def _per_chunk_rows(x, row):
    w = x.shape[1]
    picked = x.reshape(NCH, CHUNK, w)[:, row:row + 1, :]
    return jnp.broadcast_to(picked, (NCH, CHUNK, w)).reshape(HB, w)


def _chunk_stack(x, chunk_of_row):
    return jnp.concatenate([jnp.where(chunk_of_row == c, x, jnp.zeros_like(x)) for c in range(NCH)], axis=1)


def _chunk_pick(x, chunk_of_row):
    w = x.shape[1] // NCH
    out = jnp.zeros((HB, w), x.dtype)
    for c in range(NCH):
        out = jnp.where(chunk_of_row == c, x[:, c * w:(c + 1) * w], out)
    return out


def _hgrn_local(q, f, kf, b):
    sq = _sig(q)
    qf = q * sq * (HD ** -0.5)
    b_mid = _per_chunk_rows(b, CHUNK // 2 - 1)
    b_last = _per_chunk_rows(b, CHUNK - 1)
    qm = qf * jnp.exp(b - b_mid)
    km = kf * jnp.exp(b_mid - b)
    kl = kf * jnp.exp(b_last - b)
    qb = qf * jnp.exp(b)
    return dict(sq=sq, b_mid=b_mid, b_last=b_last, qm=qm, km=km, kl=kl, qb=qb)


def _hgrn2_fwd(z, hgrn_lb, onorm, ymix, t, dep=None):
    nb = t // HB
    deps = [] if dep is None else [dep]

    def body(*refs):
        zq, zf, zi, zg = refs[0:2], refs[2:4], refs[4:6], refs[6:8]
        (lb_ref, on_ref), (y_ref, o_ref, sp_ref, st_ref) = refs[8:10], refs[-4:]

        @pl.when(pl.program_id(0) == 0)
        def _():
            st_ref[...] = jnp.zeros_like(st_ref)

        lb_all = _lower_bound(lb_ref)
        gn = on_ref[...]
        low = _blockdiag(True)
        low_b = low.astype(BF16)
        chunk_of_row = lax.broadcasted_iota(jnp.int32, (HB, HD), 0) // CHUNK
        for p in range(2):
            lbp = lb_all[:, 2 * HD * p:2 * HD * (p + 1)]
            fp = lbp + (1.0 - lbp) * _sig(zf[p][...])
            bp = _chunk_sums(low_b, jnp.log(fp))
            for e in range(2):
                h, ls = 2 * p + e, slice(e * HD, (e + 1) * HD)
                f = fp[:, ls]
                w = _hgrn_local(zq[p][:, ls], f, 1.0 - f, bp[:, ls])
                iv = zi[p][:, ls].astype(BF16)
                a = jnp.where(low, _dot(w["qm"].astype(BF16), w["km"].astype(BF16), 1, 1), 0.0)
                o = _dot(a.astype(BF16), iv, 1, 0)
                u = _dot(iv, _chunk_stack(w["kl"].astype(BF16), chunk_of_row), 0, 0)
                decay = jnp.exp(w["b_last"])
                st = st_ref[h]
                states = []
                for c in range(NCH):
                    sp_ref[h, c] = st
                    states.append(st.astype(BF16))
                    st = st * decay[c * CHUNK:c * CHUNK + 1] + u[:, c * HD:(c + 1) * HD]
                st_ref[h] = st
                inter = _dot(w["qb"].astype(BF16), jnp.concatenate(states, axis=0), 1, 1)
                o = o + _chunk_pick(inter, chunk_of_row)
                hs = slice(h * HD, (h + 1) * HD)
                o_ref[:, hs] = o
                gg = zg[p][:, ls]
                y_ref[:, hs] = (o * _rstd(o) * gn * (gg * _sig(gg))).astype(BF16)

    return pl.pallas_call(
        body, name="hgrn_fwd",
        out_shape=(jax.ShapeDtypeStruct((t, D), BF16), jax.ShapeDtypeStruct((t, HG_W), F32),
                   jax.ShapeDtypeStruct((4, t // CHUNK, HD, HD), F32)),
        grid=(nb,),
        in_specs=_hgrn_cols(lambda j: j) + [pl.BlockSpec((2, HG_W), lambda j: (0, 0)),
                                            pl.BlockSpec((1, HD), lambda j: (0, 0)), ANY_SPEC]
        + [ANY_SPEC] * len(deps),
        out_specs=(pl.BlockSpec((HB, HG_W), lambda j: (j, 1)),
                   pl.BlockSpec((HB, HG_W), lambda j: (j, 0)),
                   pl.BlockSpec((4, NCH, HD, HD), lambda j: (0, j, 0, 0))),
        scratch_shapes=[pltpu.VMEM((4, HD, HD), F32)],
        input_output_aliases={10: 0},
        compiler_params=_params(dimension_semantics=("arbitrary",)),
    )(*[z] * 8, hgrn_lb, onorm, ymix, *deps)


def _hgrn2_bwd(z, hgrn_lb, onorm, o_save, sprev, dymix, dza, t):
    nb = t // HB

    def body(*refs):
        zq, zf, zi, zg = refs[0:2], refs[2:4], refs[4:6], refs[6:8]
        (lb_ref, on_ref, o_ref, sp_ref, dy_ref, dqa_ref, first_ref, second_ref,
         dz_ref, dlb_ref, don_ref, dst_ref) = refs[8:]

        @pl.when(pl.program_id(0) == 0)
        def _():
            dst_ref[...] = jnp.zeros_like(dst_ref)
            dlb_ref[...] = jnp.zeros_like(dlb_ref)
            don_ref[...] = jnp.zeros_like(don_ref)

        dz_ref[:, 0:SWA_W] = dqa_ref[...]
        dz_ref[0:HB // 2, SWA_W:ZQH] = first_ref[...]
        dz_ref[HB // 2:HB, SWA_W:ZQH] = second_ref[...]
        lb_all = _lower_bound(lb_ref)
        gn = on_ref[...]
        low, upp = _blockdiag(True), _blockdiag(False)
        upp_b = upp.astype(BF16)
        low_b = low.astype(BF16)
        row = lax.broadcasted_iota(jnp.int32, (HB, HD), 0)
        chunk_of_row = row // CHUNK
        in_chunk = row % CHUNK
        for p in range(2):
            lbp = lb_all[:, 2 * HD * p:2 * HD * (p + 1)]
            sgp = _sig(zf[p][...])
            fp = lbp + (1.0 - lbp) * sgp
            bp = _chunk_sums(low_b, jnp.log(fp))
            db_pair, dkf_pair = [], []
            for e in range(2):
                h, ls, hs = 2 * p + e, slice(e * HD, (e + 1) * HD), slice((2 * p + e) * HD, (2 * p + e + 1) * HD)
                f = fp[:, ls]
                q = zq[p][:, ls]
                w = _hgrn_local(q, f, 1.0 - f, bp[:, ls])
                iv = zi[p][:, ls].astype(BF16)
                gg = zg[p][:, ls]
                o = o_ref[:, hs]
                dout = dy_ref[:, hs].astype(F32)
                sgg = _sig(gg)
                r = _rstd(o)
                oh = o * r
                dyn = dout * (gg * sgg)
                dz_ref[:, ZGH + h * HD:ZGH + (h + 1) * HD] = (
                    dout * oh * gn * (sgg * (1.0 + gg * (1.0 - sgg)))).astype(BF16)
                don_ref[...] += _rowsum8(dyn * oh)
                do = _norm_bwd(oh, r, dyn * gn).astype(BF16)
                qm, km, kl, qb = (w[n].astype(BF16) for n in ("qm", "km", "kl", "qb"))
                decay = jnp.exp(w["b_last"])
                grads_in = _dot(do, _chunk_stack(qb, chunk_of_row), 0, 0)
                dst = dst_ref[h]
                dstn, dd_rows = [None] * NCH, [None] * NCH
                for c in reversed(range(NCH)):
                    dstn[c] = dst.astype(BF16)
                    dd_rows[c] = jnp.sum(dst * sp_ref[h, c], axis=0, keepdims=True)
                    dst = dst * decay[c * CHUNK:c * CHUNK + 1] + grads_in[:, c * HD:(c + 1) * HD]
                dst_ref[h] = dst
                states = jnp.concatenate([sp_ref[h, c].astype(BF16) for c in range(NCH)], axis=0)
                dstn_all = jnp.concatenate(dstn, axis=0)
                dqb = _dot(_chunk_stack(do, chunk_of_row), states, 1, 0)
                at = jnp.where(upp, _dot(km, qm, 1, 1), 0.0)
                di = _dot(at.astype(BF16), do, 1, 0) + _chunk_pick(_dot(kl, dstn_all, 1, 1), chunk_of_row)
                dz_ref[:, ZIH + h * HD:ZIH + (h + 1) * HD] = di.astype(BF16)
                dkl = _dot(_chunk_stack(iv, chunk_of_row), dstn_all, 1, 0)
                da = jnp.where(low, _dot(do, iv, 1, 1), 0.0).astype(BF16)
                dat = jnp.where(upp, _dot(iv, do, 1, 1), 0.0).astype(BF16)
                dqm = _dot(da, km, 1, 0)
                dkm = _dot(dat, qm, 1, 0)
                b = bp[:, ls]
                e1, e2 = jnp.exp(b - w["b_mid"]), jnp.exp(w["b_mid"] - b)
                e3, e4 = jnp.exp(w["b_last"] - b), jnp.exp(b)
                dqf = dqm * e1 + dqb * e4
                dkf_pair.append(dkm * e2 + dkl * e3)
                t_qm, t_km, t_kl = dqm * w["qm"], dkm * w["km"], dkl * w["kl"]
                db = t_qm - t_km - t_kl + dqb * w["qb"]
                db_mid = jnp.sum((t_km - t_qm).reshape(NCH, CHUNK, HD), axis=1, keepdims=True)
                db_last = jnp.sum(t_kl.reshape(NCH, CHUNK, HD), axis=1, keepdims=True)
                db_last = db_last + jnp.stack(dd_rows, axis=0) * jnp.exp(
                    bp[:, ls].reshape(NCH, CHUNK, HD)[:, CHUNK - 1:CHUNK, :])
                spread = lambda v: jnp.broadcast_to(v, (NCH, CHUNK, HD)).reshape(HB, HD)
                db = (db + jnp.where(in_chunk == CHUNK // 2 - 1, spread(db_mid), 0.0)
                      + jnp.where(in_chunk == CHUNK - 1, spread(db_last), 0.0))
                db_pair.append(db)
                sq = w["sq"]
                dz_ref[:, ZQH + h * HD:ZQH + (h + 1) * HD] = (
                    dqf * (HD ** -0.5) * (sq * (1.0 + q * (1.0 - sq)))).astype(BF16)
            dlogf = _chunk_sums(upp_b, jnp.concatenate(db_pair, axis=1))
            dfv = dlogf / fp - jnp.concatenate(dkf_pair, axis=1)
            dz_ref[:, ZFH + 2 * HD * p:ZFH + 2 * HD * (p + 1)] = (dfv * (1.0 - lbp) * sgp * (1.0 - sgp)).astype(BF16)
            dlb_ref[:, 2 * HD * p:2 * HD * (p + 1)] += _rowsum8(dfv * (1.0 - sgp))

    rev = lambda j: nb - 1 - j
    return pl.pallas_call(
        body, name="hgrn_bwd",
        out_shape=(jax.ShapeDtypeStruct((t, D_IN), BF16), jax.ShapeDtypeStruct((8, HG_W), F32),
                   jax.ShapeDtypeStruct((8, HD), F32)),
        grid=(nb,),
        in_specs=_hgrn_cols(rev) + [pl.BlockSpec((2, HG_W), lambda j: (0, 0)), pl.BlockSpec((1, HD), lambda j: (0, 0)),
                                    pl.BlockSpec((HB, HG_W), lambda j: (rev(j), 0)),
                                    pl.BlockSpec((4, NCH, HD, HD), lambda j: (0, rev(j), 0, 0)),
                                    pl.BlockSpec((HB, HG_W), lambda j: (rev(j), 1)),
                                    pl.BlockSpec((HB, SWA_W), lambda j: (rev(j), 0)),
                                    pl.BlockSpec((HB // 2, 2 * KV_W), lambda j: (rev(j), 0)),
                                    pl.BlockSpec((HB // 2, 2 * KV_W), lambda j: (rev(j), 0))],
        out_specs=(pl.BlockSpec((HB, D_IN), lambda j: (rev(j), 0)), pl.BlockSpec((8, HG_W), lambda j: (0, 0)),
                   pl.BlockSpec((8, HD), lambda j: (0, 0))),
        scratch_shapes=[pltpu.VMEM((4, HD, HD), F32)],
        compiler_params=_params(dimension_semantics=("arbitrary",)),
    )(*[z] * 8, hgrn_lb, onorm, o_save, sprev, dymix, *dza)


XB = 512


def _xattn_fwd(q, k, v, t):
    tb = min(XB, t)

    def body(q_ref, k_ref, v_ref, o_ref):
        for h in range(XH):
            cols = slice(XD * h, XD * (h + 1))
            s = _dot(q_ref[:, cols], k_ref[:, cols], 1, 1) * (XD ** -0.5)
            p = jnp.exp(s - jnp.max(s, axis=-1, keepdims=True))
            l = jnp.sum(p, axis=-1, keepdims=True)
            o_ref[:, cols] = (_dot(p.astype(BF16), v_ref[:, cols], 1, 0) * (1.0 / l)).astype(BF16)

    row = pl.BlockSpec((tb, D), lambda i: (i, 0))
    mem = pl.BlockSpec(k.shape, lambda i: (0, 0))
    return pl.pallas_call(
        body, name="xattn_fwd", out_shape=jax.ShapeDtypeStruct((t, D), BF16), grid=(t // tb,),
        in_specs=[row, mem, mem], out_specs=row, compiler_params=_params(),
    )(q, k, v)


def _xattn_bwd(q, k, v, do, t):
    tb = min(XB, t)

    def body(q_ref, k_ref, v_ref, do_ref, dq_ref, dk_ref, dv_ref):
        @pl.when(pl.program_id(0) == 0)
        def _():
            dk_ref[...] = jnp.zeros_like(dk_ref)
            dv_ref[...] = jnp.zeros_like(dv_ref)

        for h in range(XH):
            cols = slice(XD * h, XD * (h + 1))
            qh, kh, vh, doh = q_ref[:, cols], k_ref[:, cols], v_ref[:, cols], do_ref[:, cols]
            s = _dot(qh, kh, 1, 1) * (XD ** -0.5)
            p = jnp.exp(s - jnp.max(s, axis=-1, keepdims=True))
            p = p * (1.0 / jnp.sum(p, axis=-1, keepdims=True))
            dp = _dot(doh, vh, 1, 1)
            ds = (p * (dp - jnp.sum(p * dp, axis=-1, keepdims=True)) * (XD ** -0.5)).astype(BF16)
            dq_ref[:, cols] = _dot(ds, kh, 1, 0).astype(BF16)
            dk_ref[:, cols] += _dot(ds, qh, 0, 0)
            dv_ref[:, cols] += _dot(p.astype(BF16), doh, 0, 0)

    row = pl.BlockSpec((tb, D), lambda i: (i, 0))
    mem = pl.BlockSpec(k.shape, lambda i: (0, 0))
    return pl.pallas_call(
        body, name="xattn_bwd",
        out_shape=(jax.ShapeDtypeStruct((t, D), BF16), jax.ShapeDtypeStruct(k.shape, F32),
                   jax.ShapeDtypeStruct(k.shape, F32)),
        grid=(t // tb,), in_specs=[row, mem, mem, row], out_specs=(row, mem, mem),
        compiler_params=_params(dimension_semantics=("arbitrary",)),
    )(q, k, v, do)


def _mem_kv(mem, g_mem, wk, wv):
    def body(m_ref, g_ref, wk_ref, wv_ref, mn_ref, k_ref, v_ref):
        m_ = m_ref[...]
        mn = (m_ * _rstd(m_) * g_ref[...]).astype(BF16)
        mn_ref[...] = mn
        k_ref[...] = _dot(mn, wk_ref[...], 1, 0).astype(BF16)
        v_ref[...] = _dot(mn, wv_ref[...], 1, 0).astype(BF16)

    return pl.pallas_call(body, name="mem_kv", out_shape=(jax.ShapeDtypeStruct(mem.shape, BF16),) * 3,
                          compiler_params=_params())(mem, g_mem, wk, wv)


def _mem_kv_bwd(mn, mem, dk, dv, wk, wv, dep=None):
    deps = [] if dep is None else [dep]

    def body(mn_ref, m_ref, dk_ref, dv_ref, wk_ref, wv_ref, *rest):
        gk_ref, gv_ref, dg_ref = rest[len(deps):]
        mn = mn_ref[...]
        dkb, dvb = dk_ref[...].astype(BF16), dv_ref[...].astype(BF16)
        gk_ref[...] = _dot(mn, dkb, 0, 0).astype(BF16)
        gv_ref[...] = _dot(mn, dvb, 0, 0).astype(BF16)
        dmn = _dot(dkb, wk_ref[...], 1, 1) + _dot(dvb, wv_ref[...], 1, 1)
        m_ = m_ref[...]
        dg_ref[...] = _rowsum8(dmn * (m_ * _rstd(m_)))

    vmem = pl.BlockSpec(memory_space=pltpu.VMEM)
    return pl.pallas_call(
        body, name="mem_kv_bwd",
        out_shape=(jax.ShapeDtypeStruct(wk.shape, BF16), jax.ShapeDtypeStruct(wv.shape, BF16),
                   jax.ShapeDtypeStruct((8, D), F32)),
        in_specs=[vmem] * 6 + [ANY_SPEC] * len(deps), out_specs=(vmem,) * 3, compiler_params=_params(),
    )(mn, mem, dk, dv, wk, wv, *deps)


FM, FN = 1024, 1408


def _ffn_up(u, wgt, wut, t):
    tm = min(FM, t)

    def body(u_ref, wg_ref, wu_ref, g_ref, up_ref, a_ref):
        u_ = u_ref[...]
        g = _dot(u_, wg_ref[...], 1, 1)
        up = _dot(u_, wu_ref[...], 1, 1)
        g_ref[...] = g.astype(BF16)
        up_ref[...] = up.astype(BF16)
        a_ref[...] = (g * _sig(g) * up).astype(BF16)

    w = pl.BlockSpec((FN, D), lambda j, i: (j, 0))
    o = pl.BlockSpec((tm, FN), lambda j, i: (i, j))
    return pl.pallas_call(
        body, name="ffn_up", out_shape=(jax.ShapeDtypeStruct((t, D_FF), BF16),) * 3,
        grid=(D_FF // FN, t // tm), in_specs=[pl.BlockSpec((tm, D), lambda j, i: (i, 0)), w, w],
        out_specs=(o, o, o), compiler_params=_params(),
    )(u, wgt, wut)


def _ffn_down_bwd(dy, wd, gate, up, t, dep=None):
    tm = min(FM, t)
    deps = [] if dep is None else [dep]

    def body(dy_ref, w_ref, g_ref, up_ref, *rest):
        dg_ref, dup_ref = rest[len(deps):]
        da = _dot(dy_ref[...], w_ref[...], 1, 1)
        g = g_ref[...].astype(F32)
        sg = _sig(g)
        dup_ref[...] = (da * g * sg).astype(BF16)
        dg_ref[...] = (da * up_ref[...].astype(F32) * (sg * (1.0 + g * (1.0 - sg)))).astype(BF16)

    o = pl.BlockSpec((tm, FN), lambda j, i: (i, j))
    return pl.pallas_call(
        body, name="ffn_down_bwd", out_shape=(jax.ShapeDtypeStruct((t, D_FF), BF16),) * 2,
        grid=(D_FF // FN, t // tm),
        in_specs=[pl.BlockSpec((tm, D), lambda j, i: (i, 0)), pl.BlockSpec((FN, D), lambda j, i: (j, 0)), o, o]
        + [ANY_SPEC] * len(deps),
        out_specs=(o, o), compiler_params=_params(),
    )(dy, wd, gate, up, *deps)


def _local_step(x, mem, target, fetch, sm, emit=None, first_dep=None, milestone=None):
    t = x.shape[0]
    w, gw = {}, {}

    def out(key, g):
        gw[key] = g
        return None if emit is None else emit(key, g)

    def tell(tag, value):
        return None if milestone is None else milestone(tag, value)
    u1 = _prenorm(x, sm["g_mix_pre"], name="prenorm_mix", dep=first_dep)
    w["winT"] = fetch("winT", u1)
    z = _mm(u1, w["winT"], tb=True, out_dtype=F32, tm=1024, tn=1408, name="mm_z", n_outer=True)
    ymix, lse = _swa_fwd(z, sm["sinks"], t, dep=tell("z", z))
    ymix, o_h, sprev = _hgrn2_fwd(z, sm["hgrn_lb"], sm["hgrn_onorm"], ymix, t, dep=tell("swa", lse))
    w["wout"] = fetch("wout", ymix)
    y1, h1, u2 = _mm_rows([(ymix, w["wout"], False)], [x], [sm["g_mix_post"], sm["g_x_pre"]], _ep_post_pre,
                          _EP_POST_PRE_OUTS, tm=1024, name="mm_y1_post")
    for key in ("wq", "wk", "wv"):
        w[key] = fetch(key, u2)
    qx = _mm(u2, w["wq"], out_dtype=BF16, tm=1024, tn=1024, name="mm_qx")
    mn, kx, vx = _mem_kv(mem, sm["g_mem"], w["wk"], w["wv"])
    ox = _xattn_fwd(qx, kx, vx, t)
    w["wo"] = fetch("wo", ox)
    y2, h2, u3 = _mm_rows([(ox, w["wo"], False)], [h1], [sm["g_x_post"], sm["g_ffn_pre"]], _ep_post_pre,
                          _EP_POST_PRE_OUTS, tm=1024, name="mm_y2_post", dep=tell("ox", ox))
    w["wgT"], w["wuT"] = fetch("wgT", u3), fetch("wuT", u3)
    gate, up, act = _ffn_up(u3, w["wgT"], w["wuT"], t)
    w["wd"] = fetch("wd", act)
    sq, dh3, dy3, dg_ffn_post = _mm_rows([(act, w["wd"], False)], [h2, target], [sm["g_ffn_post"]], _ep_final_loss,
                                         _EP_FINAL_LOSS_OUTS, tm=512, name="mm_y3_loss")
    dep = out("wd", _mm(act, dy3, ta=True, out_dtype=BF16, tm=1408, tn=1024, name="mm_gwd"))
    dgate, dup = _ffn_down_bwd(dy3, w["wd"], gate, up, t, dep=dep)
    dep = out("wgT", _mm(dgate, u3, ta=True, out_dtype=BF16, tm=1408, tn=1024, name="mm_gwg"))
    dep = out("wuT", _mm(dup, u3, ta=True, out_dtype=BF16, tm=1408, tn=1024, name="mm_gwu", dep=dep))
    dh2, dy2, dg_ffn_pre, dg_x_post = _mm_rows(
        [(dgate, w["wgT"], False), (dup, w["wuT"], False)], [dh3, h2, y2], [sm["g_x_post"], sm["g_ffn_pre"]],
        _ep_post_pre_bwd, _EP_POST_PRE_BWD_OUTS, tm=512, name="mm_du3_post_bwd", dep=dep)
    dep = out("wo", _mm(ox, dy2, ta=True, out_dtype=BF16, tm=512, tn=1024, name="mm_gwo"))
    dox = _mm(dy2, w["wo"], tb=True, out_dtype=BF16, tm=1024, tn=1024, name="mm_dox", dep=dep)
    dqx, dkx, dvx = _xattn_bwd(qx, kx, vx, dox, t)
    dep = out("wq", _mm(u2, dqx, ta=True, out_dtype=BF16, tm=512, tn=1024, name="mm_gwq"))
    gwk, gwv, dg_mem = _mem_kv_bwd(mn, mem, dkx, dvx, w["wk"], w["wv"], dep=dep)
    out("wk", gwk)
    dep = out("wv", gwv)
    dh1, dy1, dg_x_pre, dg_mix_post = _mm_rows(
        [(dqx, w["wq"], True)], [dh2, h1, y1], [sm["g_mix_post"], sm["g_x_pre"]],
        _ep_post_pre_bwd, _EP_POST_PRE_BWD_OUTS, tm=512, name="mm_du2_post_bwd", dep=dep)
    dep = out("wout", _mm(ymix, dy1, ta=True, out_dtype=BF16, tm=512, tn=1024, name="mm_gwout"))
    dymix = _mm(dy1, w["wout"], tb=True, out_dtype=BF16, tm=1024, tn=1024, name="mm_dymix", dep=dep)
    *dza, dsinks = _swa_bwd(z, sm["sinks"], ymix, lse, dymix, t)
    dz, dlb, donorm = _hgrn2_bwd(z, sm["hgrn_lb"], sm["hgrn_onorm"], o_h, sprev, dymix, dza, t)
    dep = out("winT", _mm(dz, u1, ta=True, out_dtype=BF16, tm=1408, tn=1024, name="mm_gwin"))
    grad_x, dg_mix_pre = _mm_rows([(dz, w["winT"], False)], [dh1, x], [sm["g_mix_pre"]], _ep_pre_bwd,
                                  _EP_PRE_BWD_OUTS, tm=512, name="mm_du1_pre_bwd", dep=dep)
    parts = dict(g_mix_pre=dg_mix_pre, g_mix_post=dg_mix_post, g_mem=dg_mem, g_x_pre=dg_x_pre,
                 g_x_post=dg_x_post, g_ffn_pre=dg_ffn_pre, g_ffn_post=dg_ffn_post,
                 hgrn_onorm=donorm, hgrn_lb=dlb, sinks=dsinks, sq=sq)
    return grad_x, gw, parts


def _position():
    return lax.axis_index("x"), lax.axis_index("y"), lax.axis_index("c")


def _peer(pos, k):
    x, y, c = pos
    return (1 - x if k & 4 else x, 1 - y if k & 2 else y, 1 - c if k & 1 else c)


def _linear(pos):
    x, y, c = pos
    return 4 * x + 2 * y + c


HBM_SPEC = pl.BlockSpec(memory_space=pltpu.HBM)
SEM_SPEC = pl.BlockSpec(memory_space=pltpu.SEMAPHORE)
DATAFLOW = pltpu.SideEffectType.DATAFLOW_SIDE_EFFECTING
SEND_ORDER = (1, 2, 4, 3, 5, 6, 7)


def _in_hbm(a):
    return pltpu.with_memory_space_constraint(a, pltpu.HBM)


def _prepare_weights(shards, *, name, dep=None):
    n = len(shards)
    deps = [] if dep is None else [dep]

    def body(*refs):
        ins, (outs, lands, sem) = refs[:n], (refs[-2 * n - 1:-n - 1], refs[-n - 1:-1], refs[-1])
        me_lin = _linear(_position())
        copies = []
        for a in range(n):
            r = ins[a].shape[0]
            outs[a][...] = ins[a][...].astype(BF16)
            copies.append(pltpu.make_async_copy(outs[a], lands[a].at[pl.ds(me_lin * r, r), :], sem.at[a]))
            copies[-1].start()
        for cp in copies:
            cp.wait()

    vmem = pl.BlockSpec(memory_space=pltpu.VMEM)
    res = pl.pallas_call(
        body, name=name,
        out_shape=tuple(jax.ShapeDtypeStruct(s.shape, BF16) for s in shards)
        + tuple(jax.ShapeDtypeStruct((N_DEV * s.shape[0], s.shape[1]), BF16) for s in shards),
        in_specs=[vmem] * n + [ANY_SPEC] * len(deps), out_specs=tuple([vmem] * n + [ANY_SPEC] * n),
        scratch_shapes=[pltpu.SemaphoreType.DMA((n,))], compiler_params=_params(),
    )(*shards, *deps)
    return res[:n], res[n:]


def _copies_start(arrays, plan, n, *, name):
    na = len(arrays)

    def body(*refs):
        ins, send_sems, recv_sems = refs[:na], refs[na], refs[na + 1]
        me = _position()
        for j in range(n):
            src, dst, peer, _ = plan(ins, me, j)
            pltpu.make_async_remote_copy(src_ref=src, dst_ref=dst, send_sem=send_sems.at[j], recv_sem=recv_sems.at[j],
                                         device_id=peer, device_id_type=MESH).start()

    return pl.pallas_call(
        body, name=name,
        out_shape=(pltpu.SemaphoreType.DMA((n,)), pltpu.SemaphoreType.DMA((n,)))
        + tuple(pltpu.HBM(a.shape, a.dtype) for a in arrays),
        in_specs=(HBM_SPEC,) * na, out_specs=(SEM_SPEC, SEM_SPEC) + (HBM_SPEC,) * na,
        input_output_aliases={i: 2 + i for i in range(na)},
        compiler_params=pltpu.CompilerParams(has_side_effects=DATAFLOW),
    )(*[_in_hbm(a) for a in arrays])


def _copies_wait(send_sems, recv_sems, arrays, plan, n, after, *, name):
    na = len(arrays)

    def body(*refs):
        ins, send_sems, recv_sems = refs[:na], refs[na], refs[na + 1]
        me = _position()
        for j in range(n):
            src, _, peer, landed = plan(ins, me, j)
            copy = pltpu.make_async_remote_copy(src_ref=src, dst_ref=landed, send_sem=send_sems.at[j],
                                                recv_sem=recv_sems.at[j], device_id=peer, device_id_type=MESH)
            copy.wait_send()
            copy.wait_recv()

    return pl.pallas_call(
        body, name=name, out_shape=tuple(pltpu.HBM(a.shape, a.dtype) for a in arrays),
        in_specs=(HBM_SPEC,) * na + (SEM_SPEC, SEM_SPEC, ANY_SPEC), out_specs=(HBM_SPEC,) * na,
        input_output_aliases={i: i for i in range(na)},
        compiler_params=pltpu.CompilerParams(has_side_effects=DATAFLOW),
    )(*arrays, send_sems, recv_sems, after)


SAME_CORE = (2, 4, 6)


class _TwoLevelGather:
    def __init__(self, shards, lands, *, name):
        n = self.n = len(shards)
        self.name = name
        first_peers = (1,) + SAME_CORE

        def rows(ref, pos):
            r = ref.shape[0] // N_DEV
            return ref.at[pl.ds(_linear(pos) * r, r), :]

        def first(refs, me, j):
            a, peer = j // 4, _peer(me, first_peers[j % 4])
            return refs[a], rows(refs[n + a], me), peer, rows(refs[n + a], peer)

        def second(refs, me, j):
            a, sibling = j // 3, _peer(me, 1)
            mine = rows(refs[a], _peer(me, SAME_CORE[j % 3]))
            return mine, mine, sibling, rows(refs[a], _peer(sibling, SAME_CORE[j % 3]))

        self._first, self._second = first, second
        self._flight = _copies_start(list(shards) + list(lands), first, 4 * n, name=name + "_send")
        self.dep = self._flight[2]

    def pass_on(self, after):
        send1, recv1, *arrays = self._flight
        arrays = _copies_wait(send1, recv1, arrays, self._first, 4 * self.n, after, name=self.name + "_recv")
        self._flight = _copies_start(list(arrays[self.n:]), self._second, 3 * self.n, name=self.name + "_pass")
        return self._flight[2]

    def finish(self, after):
        send2, recv2, *lands = self._flight
        return _copies_wait(send2, recv2, lands, self._second, 3 * self.n, after, name=self.name + "_pass_recv")


def _exchange_start(gs, *, name):
    n = len(gs)
    rows = [g.shape[0] // N_DEV for g in gs]
    lands = [lax.empty((N_DEV - 1, r, g.shape[1]), g.dtype) for g, r in zip(gs, rows)]

    def body(*refs):
        g_refs, land_refs = refs[:n], refs[n:2 * n]
        send_sems, recv_sems = refs[2 * n:3 * n], refs[3 * n:4 * n]
        me = _position()
        for a in range(n):
            for k in SEND_ORDER:
                peer = _peer(me, k)
                pltpu.make_async_remote_copy(
                    src_ref=g_refs[a].at[pl.ds(_linear(peer) * rows[a], rows[a]), :],
                    dst_ref=land_refs[a].at[k - 1],
                    send_sem=send_sems[a].at[k - 1], recv_sem=recv_sems[a].at[k - 1],
                    device_id=peer, device_id_type=MESH).start()

    res = pl.pallas_call(
        body, name=name,
        out_shape=tuple(pltpu.SemaphoreType.DMA((N_DEV - 1,)) for _ in range(2 * n))
        + tuple(pltpu.HBM(a.shape, a.dtype) for a in gs + lands),
        in_specs=(HBM_SPEC,) * (2 * n), out_specs=(SEM_SPEC,) * (2 * n) + (HBM_SPEC,) * (2 * n),
        input_output_aliases={i: 2 * n + i for i in range(2 * n)},
        compiler_params=pltpu.CompilerParams(has_side_effects=DATAFLOW),
    )(*[_in_hbm(a) for a in gs + lands])
    return [(res[a], res[n + a], res[2 * n + a], res[3 * n + a]) for a in range(n)]


def _exchange_wait(send_sems, recv_sems, g_thru, land_thru, after, *, name):
    r = land_thru.shape[1]

    def body(g_ref, land_ref, send_sems, recv_sems, after_ref, g_dead, got_ref):
        del after_ref, g_dead, got_ref
        me = _position()
        for k in SEND_ORDER:
            peer = _peer(me, k)
            copy = pltpu.make_async_remote_copy(
                src_ref=g_ref.at[pl.ds(_linear(peer) * r, r), :], dst_ref=land_ref.at[k - 1],
                send_sem=send_sems.at[k - 1], recv_sem=recv_sems.at[k - 1],
                device_id=peer, device_id_type=MESH)
            copy.wait_send()
            copy.wait_recv()

    return pl.pallas_call(
        body, name=name,
        out_shape=(pltpu.HBM(g_thru.shape, g_thru.dtype), pltpu.HBM(land_thru.shape, land_thru.dtype)),
        in_specs=(HBM_SPEC, HBM_SPEC, SEM_SPEC, SEM_SPEC, pl.BlockSpec(memory_space=pl.ANY)),
        out_specs=(HBM_SPEC, HBM_SPEC), input_output_aliases={0: 0, 1: 1},
        compiler_params=pltpu.CompilerParams(has_side_effects=DATAFLOW),
    )(g_thru, land_thru, send_sems, recv_sems, after)


def _adamw_math(w, g, m, v):
    m = B1 * m + (1.0 - B1) * g
    v = B2 * v + (1.0 - B2) * (g * g)
    delta = -LR * ((m / C1) / (jnp.sqrt(v / C2) + AEPS) + WD * w)
    return delta, m, v


def _sum_adamw(items, *, name):
    n = len(items)

    def body(*refs):
        ins, outs, scratch = refs[:5 * n], refs[5 * n:9 * n], refs[9 * n:]
        me_lin = _linear(_position())
        mine = []
        for a in range(n):
            r = items[a][2].shape[0]
            mine.append(pltpu.make_async_copy(ins[5 * a].at[pl.ds(me_lin * r, r), :], scratch[a], scratch[n].at[a]))
            mine[-1].start()
        for a in range(n):
            _, land_ref, w_ref, m_ref, v_ref = ins[5 * a:5 * a + 5]
            g_ref, d_ref, nm_ref, nv_ref = outs[4 * a:4 * a + 4]
            g = land_ref[0].astype(F32)
            for s in range(1, N_DEV - 1):
                g = g + land_ref[s].astype(F32)
            mine[a].wait()
            g = scratch[a][...].astype(F32) + g
            g_ref[...] = g
            d_ref[...], nm_ref[...], nv_ref[...] = _adamw_math(w_ref[...], g, m_ref[...], v_ref[...])

    vmem = pl.BlockSpec(memory_space=pltpu.VMEM)
    res = pl.pallas_call(
        body, name=name,
        out_shape=tuple(jax.ShapeDtypeStruct(it[2].shape, F32) for it in items for _ in range(4)),
        in_specs=[ANY_SPEC, vmem, vmem, vmem, vmem] * n, out_specs=(vmem,) * (4 * n),
        scratch_shapes=[pltpu.VMEM(it[2].shape, BF16) for it in items] + [pltpu.SemaphoreType.DMA((n,))],
        compiler_params=_params(),
    )(*[a for it in items for a in it])
    return [res[4 * a:4 * a + 4] for a in range(n)]


SMALL = ("g_mix_pre", "g_mix_post", "g_mem", "g_x_pre", "g_x_post", "g_ffn_pre", "g_ffn_post",
         "hgrn_onorm", "hgrn_lb", "sinks")
SMALL_W = dict(hgrn_onorm=HD, hgrn_lb=HG_W, sinks=8)
SQ_ROW = len(SMALL)
PACK_ROWS = 16


def _small_pack(parts):
    ns = len(SMALL)

    def body(*refs):
        part, mine, slots, sem = refs[:ns + 1], refs[ns + 1], refs[ns + 2], refs[ns + 3]
        mine[...] = jnp.zeros((PACK_ROWS, D), F32)
        for r, name in enumerate(SMALL):
            wd = SMALL_W.get(name, D)
            mine[r:r + 1, 0:wd] = jnp.sum(part[r][...], axis=0, keepdims=True)[:, 0:wd]
        sq = jnp.sum(part[ns][...]) * (0.5 / D)
        mine[SQ_ROW:SQ_ROW + 1, :] = jnp.full((1, D), sq, F32)
        own = pltpu.make_async_copy(mine, slots.at[_linear(_position())], sem)
        own.start()
        own.wait()

    vmem = pl.BlockSpec(memory_space=pltpu.VMEM)
    return pl.pallas_call(
        body, name="small_pack",
        out_shape=(jax.ShapeDtypeStruct((PACK_ROWS, D), F32), jax.ShapeDtypeStruct((N_DEV, PACK_ROWS, D), F32)),
        in_specs=[vmem] * (ns + 1), out_specs=(vmem, ANY_SPEC),
        scratch_shapes=[pltpu.SemaphoreType.DMA(())], compiler_params=_params(),
    )(*[parts[n] for n in SMALL], parts["sq"])


def _small_exchange(mine, slots):
    def plan(refs, me, j):
        peer = _peer(me, j + 1)
        return refs[0], refs[1].at[_linear(me)], peer, refs[1].at[_linear(peer)]

    send, recv, mine1, slots1 = _copies_start([mine, slots], plan, N_DEV - 1, name="small_send")
    return lambda after: _copies_wait(send, recv, [mine1, slots1], plan, N_DEV - 1, after, name="small_recv")[1]


def _small_update(slots, sm, m_sm, v_sm):
    ns = len(SMALL)

    def body(*refs):
        tot = refs[0][0]
        for s in range(1, N_DEV):
            tot = tot + refs[0][s]
        w_refs, m_refs, v_refs = refs[1:ns + 1], refs[ns + 1:2 * ns + 1], refs[2 * ns + 1:3 * ns + 1]
        outs = refs[3 * ns + 1:]
        loss_ref = outs[0]
        g_out, d_out = outs[1:ns + 1], outs[ns + 1:2 * ns + 1]
        nm_out, nv_out = outs[2 * ns + 1:3 * ns + 1], outs[3 * ns + 1:4 * ns + 1]
        loss_ref[...] = tot[SQ_ROW:SQ_ROW + 1, 0:1]
        for r, name in enumerate(SMALL):
            wd = SMALL_W.get(name, D)
            g = tot[r:r + 1, 0:wd]
            w = w_refs[r][...]
            if name == "hgrn_lb":
                mx = jnp.maximum(w[0:1], w[1:2])
                e0, e1 = jnp.exp(w[0:1] - mx), jnp.exp(w[1:2] - mx)
                lb0 = e0 / (e0 + e1)
                g0 = g * lb0 * (1.0 - lb0)
                for i, gi in enumerate((g0, -g0)):
                    d, nm, nv = _adamw_math(w[i:i + 1], gi, m_refs[r][i:i + 1, :], v_refs[r][i:i + 1, :])
                    g_out[r][i:i + 1, :] = gi
                    d_out[r][i:i + 1, :], nm_out[r][i:i + 1, :], nv_out[r][i:i + 1, :] = d, nm, nv
            else:
                d, nm, nv = _adamw_math(w, g, m_refs[r][...], v_refs[r][...])
                g_out[r][...] = g
                d_out[r][...], nm_out[r][...], nv_out[r][...] = d, nm, nv

    shapes = [jax.ShapeDtypeStruct(sm[n].shape, F32) for n in SMALL]
    res = pl.pallas_call(
        body, name="small_update", out_shape=tuple([jax.ShapeDtypeStruct((1, 1), F32)] + shapes * 4),
        compiler_params=_params(),
    )(slots, *[sm[n] for n in SMALL], *[m_sm[n] for n in SMALL], *[v_sm[n] for n in SMALL])
    groups = [dict(zip(SMALL, res[1 + i * ns:1 + (i + 1) * ns])) for i in range(4)]
    return res[0], groups[0], groups[1], groups[2], groups[3]


BIG = ("w_in", "w_gate", "w_up", "w_down", "w_out", "wq_x", "wk_x", "wv_x", "wo_x")
BIG_KEY = dict(w_in="winT", w_gate="wgT", w_up="wuT", w_down="wd", w_out="wout", wq_x="wq", wk_x="wk",
               wv_x="wv", wo_x="wo")
TRANSPOSED = ("w_in", "w_gate", "w_up")
WEIGHTS = ("w_in", "sinks", "hgrn_lb", "hgrn_onorm", "w_out", "g_mix_pre", "g_mix_post", "g_mem", "g_x_pre",
           "g_x_post", "wq_x", "wk_x", "wv_x", "wo_x", "g_ffn_pre", "g_ffn_post", "w_gate", "w_up", "w_down")


def kernel(x, mem, w_in, sinks, hgrn_lb, hgrn_onorm, w_out, g_mix_pre, g_mix_post, g_mem, g_x_pre, g_x_post, wq_x, wk_x, wv_x, wo_x, g_ffn_pre, g_ffn_post, w_gate, w_up, w_down, loss_target, m_w_in, m_sinks, m_hgrn_lb, m_hgrn_onorm, m_w_out, m_g_mix_pre, m_g_mix_post, m_g_mem, m_g_x_pre, m_g_x_post, m_wq_x, m_wk_x, m_wv_x, m_wo_x, m_g_ffn_pre, m_g_ffn_post, m_w_gate, m_w_up, m_w_down, v_w_in, v_sinks, v_hgrn_lb, v_hgrn_onorm, v_w_out, v_g_mix_pre, v_g_mix_post, v_g_mem, v_g_x_pre, v_g_x_post, v_wq_x, v_wk_x, v_wv_x, v_wo_x, v_g_ffn_pre, v_g_ffn_post, v_w_gate, v_w_up, v_w_down):
    given = dict(locals())
    wts = {n: given[n] for n in WEIGHTS}
    ms = {n: given["m_" + n] for n in WEIGHTS}
    vs = {n: given["v_" + n] for n in WEIGHTS}

    def mat(a, name):
        a = a[0]
        return a.T if name in TRANSPOSED else a

    groups = (("w_in",), ("w_out", "wq_x", "wk_x", "wv_x", "wo_x"), ("w_gate", "w_up", "w_down"))
    gathers = []

    def start_group(g, dep):
        tag = ("w_in", "w_attn", "w_ffn")[g]
        shards, lands = _prepare_weights([mat(wts[n], n) for n in groups[g]], name="prepare_" + tag, dep=dep)
        gathers.append(_TwoLevelGather(shards, lands, name=tag))
        return gathers[-1].dep

    first_dep = start_group(1, start_group(0, None))
    name_of = {k: n for n, k in BIG_KEY.items()}
    gathered = {}

    def milestone(tag, value):
        if tag == "z":
            return start_group(2, value)
        return gathers[{"swa": 1, "ox": 2}[tag]].pass_on(value)

    def fetch(key, after):
        name = name_of[key]
        if name not in gathered:
            g = [i for i, group in enumerate(groups) if name in group][0]
            if g == 0:
                gathers[0].pass_on(after)
            gathered.update(zip(groups[g], gathers[g].finish(after)))
        return gathered[name]

    sm = {n: wts[n] for n in SMALL}
    started, held = {}, {}
    send_with = {k: group for group in (("wgT", "wuT"), ("wo", "wq", "wk", "wv")) for k in group}

    def emit(key, g):
        held[key] = g
        group = send_with.get(key, (key,))
        if key != group[-1]:
            return None
        flights = _exchange_start([held[k] for k in group], name="grad_send_" + name_of[group[0]])
        started.update({name_of[k]: f for k, f in zip(group, flights)})
        return flights[-1][2]

    grad_x, _, parts = _local_step(x[0], mem[0], loss_target[0], fetch, sm, emit, first_dep=first_dep, milestone=milestone)
    small_finish = _small_exchange(*_small_pack(parts))
    grads, deltas, new_m, new_v = {}, {}, {}, {}
    after = grad_x
    for group in (("w_down",), ("w_gate", "w_up"), ("wo_x", "wq_x", "wk_x", "wv_x", "w_out"), ("w_in",)):
        items = []
        for n in group:
            g_all, land = _exchange_wait(*started[n], after, name="grad_recv_" + n)
            items.append((g_all, land, mat(wts[n], n), mat(ms[n], n), mat(vs[n], n)))
            after = land
        for n, res in zip(group, _sum_adamw(items, name="adamw_" + group[0])):
            after = res[1]
            if n in TRANSPOSED:
                res = [a.T for a in res]
            grads[n], deltas[n], new_m[n], new_v[n] = [a[None] for a in res]
    loss, g_s, d_s, m_s, v_s = _small_update(small_finish(after), sm, {n: ms[n] for n in SMALL},
                                             {n: vs[n] for n in SMALL})
    grads.update(g_s), deltas.update(d_s), new_m.update(m_s), new_v.update(v_s)
    return (loss[0, 0], grad_x[None], *[grads[n] for n in WEIGHTS], *[deltas[n] for n in WEIGHTS],
            *[new_m[n] for n in WEIGHTS], *[new_v[n] for n in WEIGHTS])
```

```python
import functools

import jax
import jax.numpy as jnp
from jax import lax
from jax.experimental import pallas as pl
from jax.experimental.pallas import tpu as pltpu

F32 = jnp.float32
BF16 = jnp.bfloat16

D = 1024
D_IN = 2816
D_FF = 2816
CHUNK = 64
SWA_W = 512
KV_W = 128
HG_W = 512
HD = 128
ZQH, ZFH, ZIH, ZGH = 768, 1280, 1792, 2304
XH, XD = 4, 256
EPS = 1e-6
NEG = -1e30
N_DEV = 8
MESH = pl.DeviceIdType.MESH

LR, B1, B2, AEPS, WD, STEP = 0.001, 0.9, 0.999, 1e-08, 0.01, 10
C1 = 1.0 - B1 ** STEP
C2 = 1.0 - B2 ** STEP

VMEM_LIMIT = 56 * 1024 * 1024


def _params(**kw):
    return pltpu.CompilerParams(vmem_limit_bytes=VMEM_LIMIT, **kw)


def _sig(x):
    return 1.0 / (1.0 + jnp.exp(-x))


def _rowsum8(x):
    r, w = x.shape
    return jnp.sum(x.reshape(r // 8, 8, w), axis=0)


def _dot(a, b, ca, cb, precision=None):
    return lax.dot_general(a, b, (((ca,), (cb,)), ((), ())), preferred_element_type=F32,
                           precision=precision)


ANY_SPEC = pl.BlockSpec(memory_space=pl.ANY)


def _mm(a, b, *, ta=False, tb=False, out_dtype, tm, tn, tk=None, name, dep=None, n_outer=False):
    m = a.shape[1] if ta else a.shape[0]
    k = a.shape[0] if ta else a.shape[1]
    n = b.shape[0] if tb else b.shape[1]
    tm, tn = min(tm, m), min(tn, n)
    tk = k if tk is None else min(tk, k)
    nk = k // tk
    assert m % tm == 0 and n % tn == 0 and k % tk == 0, (name, m, n, k, tm, tn, tk)
    ij = (lambda g0, g1: (g1, g0)) if n_outer else (lambda g0, g1: (g0, g1))
    a_spec = (pl.BlockSpec((tk, tm), lambda g0, g1, kk: (kk, ij(g0, g1)[0])) if ta
              else pl.BlockSpec((tm, tk), lambda g0, g1, kk: (ij(g0, g1)[0], kk)))
    b_spec = (pl.BlockSpec((tn, tk), lambda g0, g1, kk: (ij(g0, g1)[1], kk)) if tb
              else pl.BlockSpec((tk, tn), lambda g0, g1, kk: (kk, ij(g0, g1)[1])))
    ca, cb = (0 if ta else 1), (1 if tb else 0)

    deps = [] if dep is None else [dep]

    def body(a_ref, b_ref, *rest):
        o_ref, acc = rest[len(deps)], rest[len(deps) + 1:]
        p = _dot(a_ref[...].astype(BF16), b_ref[...].astype(BF16), ca, cb)
        if nk == 1:
            o_ref[...] = p.astype(out_dtype)
        else:
            acc_ref, = acc
            kk = pl.program_id(2)

            @pl.when(kk == 0)
            def _():
                acc_ref[...] = p

            @pl.when(kk > 0)
            def _():
                acc_ref[...] += p

            @pl.when(kk == nk - 1)
            def _():
                o_ref[...] = acc_ref[...].astype(out_dtype)

    return pl.pallas_call(
        body, name=name, out_shape=jax.ShapeDtypeStruct((m, n), out_dtype),
        grid=(n // tn, m // tm, nk) if n_outer else (m // tm, n // tn, nk),
        in_specs=[a_spec, b_spec] + [ANY_SPEC] * len(deps),
        out_specs=pl.BlockSpec((tm, tn), lambda g0, g1, kk: ij(g0, g1)),
        scratch_shapes=[pltpu.VMEM((tm, tn), F32)] if nk > 1 else [],
        compiler_params=_params(dimension_semantics=("parallel", "parallel", "arbitrary")),
    )(a, b, *deps)


def _mm_rows(prods, rows_in, vecs_in, epilogue, outs, *, tm, name, dep=None):
    m = prods[0][0].shape[0]
    n = prods[0][1].shape[0] if prods[0][2] else prods[0][1].shape[1]
    tm = min(tm, m)
    assert m % tm == 0
    deps = [] if dep is None else [dep]
    n_p, n_r, n_v = len(prods), len(rows_in), len(vecs_in)

    def body(*refs):
        ab = refs[:2 * n_p]
        row_refs = refs[2 * n_p:2 * n_p + n_r]
        vec_refs = refs[2 * n_p + n_r:2 * n_p + n_r + n_v]
        out_refs = refs[2 * n_p + n_r + n_v + len(deps):]
        p = None
        for j, (_, _, tb) in enumerate(prods):
            t = _dot(ab[2 * j][...].astype(BF16), ab[2 * j + 1][...], 1, 1 if tb else 0)
            p = t if p is None else p + t
        vals = epilogue(p, *[r[...] for r in row_refs], *[v[...] for v in vec_refs])
        for (dtype, kind), o_ref, val in zip(outs, out_refs, vals):
            if kind == "row":
                o_ref[...] = val.astype(dtype)
            else:
                @pl.when(pl.program_id(0) == 0)
                def _(o_ref=o_ref):
                    o_ref[...] = jnp.zeros_like(o_ref)

                o_ref[...] += val

    row = lambda w: pl.BlockSpec((tm, w), lambda i: (i, 0))
    whole = lambda a: pl.BlockSpec(a.shape, lambda i: (0,) * a.ndim, pipeline_mode=pl.Buffered(1))
    in_specs, args = [], []
    for a, b, _ in prods:
        in_specs += [row(a.shape[1]), whole(b)]
        args += [a, b]
    in_specs += [row(r.shape[1]) for r in rows_in] + [whole(v) for v in vecs_in] + [ANY_SPEC] * len(deps)
    return pl.pallas_call(
        body, name=name,
        out_shape=tuple(jax.ShapeDtypeStruct((m, n) if kind == "row" else (8, n), dtype) for dtype, kind in outs),
        grid=(m // tm,), in_specs=in_specs,
        out_specs=tuple(row(n) if kind == "row" else pl.BlockSpec((8, n), lambda i: (0, 0)) for _, kind in outs),
        compiler_params=_params(dimension_semantics=("arbitrary",)),
    )(*args, *rows_in, *vecs_in, *deps)


def _rstd(x):
    return lax.rsqrt(jnp.mean(x * x, axis=-1, keepdims=True) + EPS)


def _norm_bwd(xh, r, t):
    return r * (t - xh * jnp.mean(xh * t, axis=-1, keepdims=True))


ROW_F32, ROW_BF16, SUM_F32 = (F32, "row"), (BF16, "row"), (F32, "sum")


def _ep_post_pre(p, h, g_post, g_pre):
    y = p.astype(BF16)
    yf = y.astype(F32)
    hn = h + yf * _rstd(yf) * g_post
    return y, hn, hn * _rstd(hn) * g_pre


_EP_POST_PRE_OUTS = [ROW_BF16, ROW_F32, ROW_BF16]


def _ep_final_loss(y, h, target, g_post):
    r = _rstd(y)
    yh = y * r
    err = h + yh * g_post - target
    dh = err * (1.0 / D)
    return _rowsum8(err * err), dh, _norm_bwd(yh, r, dh * g_post), _rowsum8(dh * yh)


_EP_FINAL_LOSS_OUTS = [SUM_F32, ROW_F32, ROW_BF16, SUM_F32]


def _ep_post_pre_bwd(du, dh_out, hn, y, g_post, g_pre):
    r2 = _rstd(hn)
    xh = hn * r2
    dh = dh_out + _norm_bwd(xh, r2, du * g_pre)
    yf = y.astype(F32)
    r1 = _rstd(yf)
    yh = yf * r1
    return dh, _norm_bwd(yh, r1, dh * g_post), _rowsum8(du * xh), _rowsum8(dh * yh)


_EP_POST_PRE_BWD_OUTS = [ROW_F32, ROW_BF16, SUM_F32, SUM_F32]


def _ep_pre_bwd(du, dh_out, x, g):
    r = _rstd(x)
    xh = x * r
    return dh_out + _norm_bwd(xh, r, du * g), _rowsum8(du * xh)


_EP_PRE_BWD_OUTS = [ROW_F32, SUM_F32]


def _prenorm(x, g, *, name, dep=None):
    t, d = x.shape
    tb = min(512, t)
    deps = [] if dep is None else [dep]

    def body(x_ref, g_ref, *rest):
        xf = x_ref[...]
        rest[-1][...] = (xf * _rstd(xf) * g_ref[...]).astype(BF16)

    return pl.pallas_call(
        body, name=name, out_shape=jax.ShapeDtypeStruct((t, d), BF16), grid=(t // tb,),
        in_specs=[pl.BlockSpec((tb, d), lambda i: (i, 0)), pl.BlockSpec((1, d), lambda i: (0, 0))]
        + [ANY_SPEC] * len(deps),
        out_specs=pl.BlockSpec((tb, d), lambda i: (i, 0)), compiler_params=_params(),
    )(x, g, *deps)


QB = 256


def _half_mask(shape, e):
    lane = lax.broadcasted_iota(jnp.int32, shape, len(shape) - 1)
    return (lane // 64) == e


def _place(kv):
    sw = pltpu.roll(kv, 64, 1)
    m0 = _half_mask(kv.shape, 0)
    return [[jnp.where(m0, kv, 0.0).astype(BF16), jnp.where(m0, 0.0, sw).astype(BF16)],
            [jnp.where(m0, sw, 0.0).astype(BF16), jnp.where(m0, 0.0, kv).astype(BF16)]]


SQ = 128
SK = 256


def _swa_valid(i, sb):
    qc = lax.broadcasted_iota(jnp.int32, (SQ, SK), 0) // CHUNK
    kc = lax.broadcasted_iota(jnp.int32, (SQ, SK), 1) // CHUNK - 2
    return (kc <= qc) & (qc <= kc + 2) & (4 * i + 2 * sb + kc >= 0)


def _swa_fwd(z, sinks, t, dep=None):
    nb = t // QB
    deps = [] if dep is None else [dep]

    def body(s_ref, q_ref, kp_ref, kc_ref, vp_ref, vc_ref, *rest):
        o_ref, lse_ref = rest[-2:]
        i = pl.program_id(0)
        kpl = _place(jnp.concatenate([kp_ref[...], kc_ref[...]], axis=0))
        vpl = _place(jnp.concatenate([vp_ref[...], vc_ref[...]], axis=0))
        lane = lax.broadcasted_iota(jnp.int32, (SQ, 128), 1)
        for sb in range(QB // SQ):
            rows, keys = slice(SQ * sb, SQ * (sb + 1)), slice(SQ * sb, SQ * sb + SK)
            valid = _swa_valid(i, sb)
            lse_out = jnp.zeros((SQ, 128), F32)
            for j in range(4):
                qp = q_ref[rows, 128 * j:128 * (j + 1)].astype(BF16)
                acc = jnp.zeros((SQ, 128), F32)
                for e in range(2):
                    h = 2 * j + e
                    kvh = h // 4
                    qm = jnp.where(_half_mask(qp.shape, e), qp, jnp.zeros_like(qp))
                    s = _dot(qm, kpl[kvh][e][keys], 1, 1) * 0.125
                    s = jnp.where(valid, s, NEG)
                    sink = s_ref[0, h]
                    m = jnp.maximum(jnp.max(s, axis=-1, keepdims=True), sink)
                    p = jnp.exp(s - m)
                    l = jnp.sum(p, axis=-1, keepdims=True) + jnp.exp(sink - m)
                    acc = acc + _dot(p.astype(BF16), vpl[kvh][e][keys], 1, 0) * (1.0 / l)
                    lse_out = jnp.where(lane == h, m + jnp.log(l), lse_out)
                o_ref[rows, 128 * j:128 * (j + 1)] = acc.astype(BF16)
            lse_ref[rows, :] = lse_out

    prev = lambda c: pl.BlockSpec((128, 128), lambda i: (jnp.maximum(2 * i - 1, 0), c))
    cur = lambda c: pl.BlockSpec((QB, 128), lambda i: (i, c))
    return pl.pallas_call(
        body, name="swa_fwd",
        out_shape=(jax.ShapeDtypeStruct((t, D), BF16), jax.ShapeDtypeStruct((t, 128), F32)),
        grid=(nb,),
        in_specs=[pl.BlockSpec(memory_space=pltpu.SMEM),
                  pl.BlockSpec((QB, SWA_W), lambda i: (i, 0)), prev(4), cur(4), prev(5), cur(5)]
        + [ANY_SPEC] * len(deps),
        out_specs=(pl.BlockSpec((QB, SWA_W), lambda i: (i, 0)), pl.BlockSpec((QB, 128), lambda i: (i, 0))),
        compiler_params=_params(),
    )(sinks, z, z, z, z, z, *deps)


def _swa_bwd(z, sinks, ymix, lse, dymix, t):
    nb = t // QB

    def body(s_ref, q_ref, kp_ref, kc_ref, vp_ref, vc_ref, o_ref, do_ref, l_ref,
             dq_ref, first_ref, second_ref, ds_ref, carry_ref):
        i = pl.program_id(0)
        live = i < nb

        @pl.when(i == 0)
        def _():
            ds_ref[...] = jnp.zeros_like(ds_ref)
            carry_ref[...] = jnp.zeros_like(carry_ref)

        lane = lax.broadcasted_iota(jnp.int32, (8, 128), 1)
        kpl = _place(jnp.concatenate([kp_ref[...], kc_ref[...]], axis=0))
        vpl = _place(jnp.concatenate([vp_ref[...], vc_ref[...]], axis=0))
        nk = QB + 128
        qc = lax.broadcasted_iota(jnp.int32, (QB, nk), 0) // CHUNK
        kc = lax.broadcasted_iota(jnp.int32, (QB, nk), 1) // CHUNK - 2
        valid = (kc <= qc) & (qc <= kc + 2) & (4 * i + kc >= 0) & live
        lse_c = l_ref[...]
        dsink = jnp.zeros((8, 128), F32)
        dk_acc = [[jnp.zeros((128, nk), F32) for _ in range(2)] for _ in range(2)]
        dv_acc = [[jnp.zeros((128, nk), F32) for _ in range(2)] for _ in range(2)]
        dq = []
        for j in range(4):
            cols = slice(128 * j, 128 * (j + 1))
            qp = q_ref[:, cols].astype(BF16)
            dop = do_ref[:, cols]
            prod = dop.astype(F32) * o_ref[:, cols].astype(F32)
            acc = jnp.zeros((QB, 128), F32)
            for e in range(2):
                h = 2 * j + e
                kvh = h // 4
                hm = _half_mask(qp.shape, e)
                qm = jnp.where(hm, qp, jnp.zeros_like(qp))
                dom = jnp.where(hm, dop, jnp.zeros_like(dop))
                dd = jnp.sum(jnp.where(hm, prod, 0.0), axis=-1, keepdims=True)
                lse_h = lse_c[:, h:h + 1]
                s = _dot(qm, kpl[kvh][e], 1, 1) * 0.125
                p = jnp.where(valid, jnp.exp(s - lse_h), 0.0)
                dp = _dot(dom, vpl[kvh][e], 1, 1)
                ds = (p * (dp - dd) * 0.125).astype(BF16)
                acc = acc + _dot(ds, kpl[kvh][e], 1, 0)
                dk_acc[kvh][e] = dk_acc[kvh][e] + _dot(qm, ds, 0, 0)
                dv_acc[kvh][e] = dv_acc[kvh][e] + _dot(dom, p.astype(BF16), 0, 0)
                ps = jnp.where(live, jnp.exp(s_ref[0, h] - lse_h) * dd, 0.0)
                dsink = dsink - jnp.where(lane == h, _rowsum8(jnp.broadcast_to(ps, (QB, 128))), 0.0)
            dq.append(acc.astype(BF16))
        ds_ref[...] += dsink
        dk = (dk_acc[0][0] + dk_acc[1][1] + pltpu.roll(dk_acc[0][1] + dk_acc[1][0], 64, 0)).T
        dv = (dv_acc[0][0] + dv_acc[1][1] + pltpu.roll(dv_acc[0][1] + dv_acc[1][0], 64, 0)).T
        dkv = jnp.concatenate([dk, dv], axis=1)
        second_ref[...] = (carry_ref[...] + dkv[0:128]).astype(BF16)
        carry_ref[...] = dkv[256:384]

        @pl.when(live)
        def _():
            for j in range(4):
                dq_ref[:, 128 * j:128 * (j + 1)] = dq[j]
            first_ref[...] = dkv[128:256].astype(BF16)

    blk = lambda i: jnp.minimum(i, nb - 1)
    prev = lambda c: pl.BlockSpec((128, 128), lambda i: (jnp.maximum(2 * blk(i) - 1, 0), c))
    cur = lambda w, c: pl.BlockSpec((QB, w), lambda i: (blk(i), c))
    half = lambda index: pl.BlockSpec((128, 256), lambda i: (index(i), 0))
    return pl.pallas_call(
        body, name="swa_bwd",
        out_shape=(jax.ShapeDtypeStruct((t, SWA_W), BF16), jax.ShapeDtypeStruct((t // 2, 256), BF16),
                   jax.ShapeDtypeStruct((t // 2, 256), BF16), jax.ShapeDtypeStruct((8, 128), F32)),
        grid=(nb + 1,),
        in_specs=[pl.BlockSpec(memory_space=pltpu.SMEM),
                  cur(SWA_W, 0), prev(4), cur(128, 4), prev(5), cur(128, 5),
                  cur(SWA_W, 0), cur(SWA_W, 0), cur(128, 0)],
        out_specs=(cur(SWA_W, 0), half(blk), half(lambda i: jnp.maximum(i - 1, 0)),
                   pl.BlockSpec((8, 128), lambda i: (0, 0))),
        scratch_shapes=[pltpu.VMEM((128, 256), F32)],
        compiler_params=_params(dimension_semantics=("arbitrary",)),
    )(sinks, z, z, z, z, z, ymix, dymix, lse)


HB = 256


def _lower_bound(lb_ref):
    a = lb_ref[...]
    a0, a1 = a[0:1], a[1:2]
    mx = jnp.maximum(a0, a1)
    e0, e1 = jnp.exp(a0 - mx), jnp.exp(a1 - mx)
    return e0 / (e0 + e1)


def _hgrn_cols(row_block):
    return [pl.BlockSpec((HB, 2 * HD), lambda j, c=base // (2 * HD) + p: (row_block(j), c))
            for base in (ZQH, ZFH, ZIH, ZGH) for p in range(2)]


NCH = HB // CHUNK


def _split3(x):
    hi = x.astype(BF16)
    r1 = x - hi.astype(F32)
    mid = r1.astype(BF16)
    return hi, mid, (r1 - mid.astype(F32)).astype(BF16)


def _blockdiag(lower):
    r = lax.broadcasted_iota(jnp.int32, (HB, HB), 0)
    c = lax.broadcasted_iota(jnp.int32, (HB, HB), 1)
    return (r // CHUNK == c // CHUNK) & ((c <= r) if lower else (c >= r))


def _chunk_sums(mask_bf16, x):
    return sum(_dot(mask_bf16, part, 1, 0) for part in _split3(x))


def _per_chunk_rows(x, row):
    w = x.shape[1]
    picked = x.reshape(NCH, CHUNK, w)[:, row:row + 1, :]
    return jnp.broadcast_to(picked, (NCH, CHUNK, w)).reshape(HB, w)


def _chunk_stack(x, chunk_of_row):
    return jnp.concatenate([jnp.where(chunk_of_row == c, x, jnp.zeros_like(x)) for c in range(NCH)], axis=1)


def _chunk_pick(x, chunk_of_row):
    w = x.shape[1] // NCH
    out = jnp.zeros((HB, w), x.dtype)
    for c in range(NCH):
        out = jnp.where(chunk_of_row == c, x[:, c * w:(c + 1) * w], out)
    return out


def _hgrn_local(q, f, kf, b):
    sq = _sig(q)
    qf = q * sq * (HD ** -0.5)
    b_mid = _per_chunk_rows(b, CHUNK // 2 - 1)
    b_last = _per_chunk_rows(b, CHUNK - 1)
    qm = qf * jnp.exp(b - b_mid)
    km = kf * jnp.exp(b_mid - b)
    kl = kf * jnp.exp(b_last - b)
    qb = qf * jnp.exp(b)
    return dict(sq=sq, b_mid=b_mid, b_last=b_last, qm=qm, km=km, kl=kl, qb=qb)


def _hgrn2_fwd(z, hgrn_lb, onorm, ymix, t, dep=None):
    nb = t // HB
    deps = [] if dep is None else [dep]

    def body(*refs):
        zq, zf, zi, zg = refs[0:2], refs[2:4], refs[4:6], refs[6:8]
        (lb_ref, on_ref), (y_ref, o_ref, sp_ref, st_ref) = refs[8:10], refs[-4:]

        @pl.when(pl.program_id(0) == 0)
        def _():
            st_ref[...] = jnp.zeros_like(st_ref)

        lb_all = _lower_bound(lb_ref)
        gn = on_ref[...]
        low = _blockdiag(True)
        low_b = low.astype(BF16)
        chunk_of_row = lax.broadcasted_iota(jnp.int32, (HB, HD), 0) // CHUNK
        for p in range(2):
            lbp = lb_all[:, 2 * HD * p:2 * HD * (p + 1)]
            fp = lbp + (1.0 - lbp) * _sig(zf[p][...])
            bp = _chunk_sums(low_b, jnp.log(fp))
            for e in range(2):
                h, ls = 2 * p + e, slice(e * HD, (e + 1) * HD)
                f = fp[:, ls]
                w = _hgrn_local(zq[p][:, ls], f, 1.0 - f, bp[:, ls])
                iv = zi[p][:, ls].astype(BF16)
                a = jnp.where(low, _dot(w["qm"].astype(BF16), w["km"].astype(BF16), 1, 1), 0.0)
                o = _dot(a.astype(BF16), iv, 1, 0)
                u = _dot(iv, _chunk_stack(w["kl"].astype(BF16), chunk_of_row), 0, 0)
                decay = jnp.exp(w["b_last"])
                st = st_ref[h]
                states = []
                for c in range(NCH):
                    sp_ref[h, c] = st
                    states.append(st.astype(BF16))
                    st = st * decay[c * CHUNK:c * CHUNK + 1] + u[:, c * HD:(c + 1) * HD]
                st_ref[h] = st
                inter = _dot(w["qb"].astype(BF16), jnp.concatenate(states, axis=0), 1, 1)
                o = o + _chunk_pick(inter, chunk_of_row)
                hs = slice(h * HD, (h + 1) * HD)
                o_ref[:, hs] = o
                gg = zg[p][:, ls]
                y_ref[:, hs] = (o * _rstd(o) * gn * (gg * _sig(gg))).astype(BF16)

    return pl.pallas_call(
        body, name="hgrn_fwd",
        out_shape=(jax.ShapeDtypeStruct((t, D), BF16), jax.ShapeDtypeStruct((t, HG_W), F32),
                   jax.ShapeDtypeStruct((4, t // CHUNK, HD, HD), F32)),
        grid=(nb,),
        in_specs=_hgrn_cols(lambda j: j) + [pl.BlockSpec((2, HG_W), lambda j: (0, 0)),
                                            pl.BlockSpec((1, HD), lambda j: (0, 0)), ANY_SPEC]
        + [ANY_SPEC] * len(deps),
        out_specs=(pl.BlockSpec((HB, HG_W), lambda j: (j, 1)),
                   pl.BlockSpec((HB, HG_W), lambda j: (j, 0)),
                   pl.BlockSpec((4, NCH, HD, HD), lambda j: (0, j, 0, 0))),
        scratch_shapes=[pltpu.VMEM((4, HD, HD), F32)],
        input_output_aliases={10: 0},
        compiler_params=_params(dimension_semantics=("arbitrary",)),
    )(*[z] * 8, hgrn_lb, onorm, ymix, *deps)


def _hgrn2_bwd(z, hgrn_lb, onorm, o_save, sprev, dymix, dza, t):
    nb = t // HB

    def body(*refs):
        zq, zf, zi, zg = refs[0:2], refs[2:4], refs[4:6], refs[6:8]
        (lb_ref, on_ref, o_ref, sp_ref, dy_ref, dqa_ref, first_ref, second_ref,
         dz_ref, dlb_ref, don_ref, dst_ref) = refs[8:]

        @pl.when(pl.program_id(0) == 0)
        def _():
            dst_ref[...] = jnp.zeros_like(dst_ref)
            dlb_ref[...] = jnp.zeros_like(dlb_ref)
            don_ref[...] = jnp.zeros_like(don_ref)

        dz_ref[:, 0:SWA_W] = dqa_ref[...]
        dz_ref[0:HB // 2, SWA_W:ZQH] = first_ref[...]
        dz_ref[HB // 2:HB, SWA_W:ZQH] = second_ref[...]
        lb_all = _lower_bound(lb_ref)
        gn = on_ref[...]
        low, upp = _blockdiag(True), _blockdiag(False)
        upp_b = upp.astype(BF16)
        low_b = low.astype(BF16)
        row = lax.broadcasted_iota(jnp.int32, (HB, HD), 0)
        chunk_of_row = row // CHUNK
        in_chunk = row % CHUNK
        for p in range(2):
            lbp = lb_all[:, 2 * HD * p:2 * HD * (p + 1)]
            sgp = _sig(zf[p][...])
            fp = lbp + (1.0 - lbp) * sgp
            bp = _chunk_sums(low_b, jnp.log(fp))
            db_pair, dkf_pair = [], []
            for e in range(2):
                h, ls, hs = 2 * p + e, slice(e * HD, (e + 1) * HD), slice((2 * p + e) * HD, (2 * p + e + 1) * HD)
                f = fp[:, ls]
                q = zq[p][:, ls]
                w = _hgrn_local(q, f, 1.0 - f, bp[:, ls])
                iv = zi[p][:, ls].astype(BF16)
                gg = zg[p][:, ls]
                o = o_ref[:, hs]
                dout = dy_ref[:, hs].astype(F32)
                sgg = _sig(gg)
                r = _rstd(o)
                oh = o * r
                dyn = dout * (gg * sgg)
                dz_ref[:, ZGH + h * HD:ZGH + (h + 1) * HD] = (
                    dout * oh * gn * (sgg * (1.0 + gg * (1.0 - sgg)))).astype(BF16)
                don_ref[...] += _rowsum8(dyn * oh)
                do = _norm_bwd(oh, r, dyn * gn).astype(BF16)
                qm, km, kl, qb = (w[n].astype(BF16) for n in ("qm", "km", "kl", "qb"))
                decay = jnp.exp(w["b_last"])
                grads_in = _dot(do, _chunk_stack(qb, chunk_of_row), 0, 0)
                dst = dst_ref[h]
                dstn, dd_rows = [None] * NCH, [None] * NCH
                for c in reversed(range(NCH)):
                    dstn[c] = dst.astype(BF16)
                    dd_rows[c] = jnp.sum(dst * sp_ref[h, c], axis=0, keepdims=True)
                    dst = dst * decay[c * CHUNK:c * CHUNK + 1] + grads_in[:, c * HD:(c + 1) * HD]
                dst_ref[h] = dst
                states = jnp.concatenate([sp_ref[h, c].astype(BF16) for c in range(NCH)], axis=0)
                dstn_all = jnp.concatenate(dstn, axis=0)
                dqb = _dot(_chunk_stack(do, chunk_of_row), states, 1, 0)
                at = jnp.where(upp, _dot(km, qm, 1, 1), 0.0)
                di = _dot(at.astype(BF16), do, 1, 0) + _chunk_pick(_dot(kl, dstn_all, 1, 1), chunk_of_row)
                dz_ref[:, ZIH + h * HD:ZIH + (h + 1) * HD] = di.astype(BF16)
                dkl = _dot(_chunk_stack(iv, chunk_of_row), dstn_all, 1, 0)
                da = jnp.where(low, _dot(do, iv, 1, 1), 0.0).astype(BF16)
                dat = jnp.where(upp, _dot(iv, do, 1, 1), 0.0).astype(BF16)
                dqm = _dot(da, km, 1, 0)
                dkm = _dot(dat, qm, 1, 0)
                b = bp[:, ls]
                e1, e2 = jnp.exp(b - w["b_mid"]), jnp.exp(w["b_mid"] - b)
                e3, e4 = jnp.exp(w["b_last"] - b), jnp.exp(b)
                dqf = dqm * e1 + dqb * e4
                dkf_pair.append(dkm * e2 + dkl * e3)
                t_qm, t_km, t_kl = dqm * w["qm"], dkm * w["km"], dkl * w["kl"]
                db = t_qm - t_km - t_kl + dqb * w["qb"]
                db_mid = jnp.sum((t_km - t_qm).reshape(NCH, CHUNK, HD), axis=1, keepdims=True)
                db_last = jnp.sum(t_kl.reshape(NCH, CHUNK, HD), axis=1, keepdims=True)
                db_last = db_last + jnp.stack(dd_rows, axis=0) * jnp.exp(
                    bp[:, ls].reshape(NCH, CHUNK, HD)[:, CHUNK - 1:CHUNK, :])
                spread = lambda v: jnp.broadcast_to(v, (NCH, CHUNK, HD)).reshape(HB, HD)
                db = (db + jnp.where(in_chunk == CHUNK // 2 - 1, spread(db_mid), 0.0)
                      + jnp.where(in_chunk == CHUNK - 1, spread(db_last), 0.0))
                db_pair.append(db)
                sq = w["sq"]
                dz_ref[:, ZQH + h * HD:ZQH + (h + 1) * HD] = (
                    dqf * (HD ** -0.5) * (sq * (1.0 + q * (1.0 - sq)))).astype(BF16)
            dlogf = _chunk_sums(upp_b, jnp.concatenate(db_pair, axis=1))
            dfv = dlogf / fp - jnp.concatenate(dkf_pair, axis=1)
            dz_ref[:, ZFH + 2 * HD * p:ZFH + 2 * HD * (p + 1)] = (dfv * (1.0 - lbp) * sgp * (1.0 - sgp)).astype(BF16)
            dlb_ref[:, 2 * HD * p:2 * HD * (p + 1)] += _rowsum8(dfv * (1.0 - sgp))

    rev = lambda j: nb - 1 - j
    return pl.pallas_call(
        body, name="hgrn_bwd",
        out_shape=(jax.ShapeDtypeStruct((t, D_IN), BF16), jax.ShapeDtypeStruct((8, HG_W), F32),
                   jax.ShapeDtypeStruct((8, HD), F32)),
        grid=(nb,),
        in_specs=_hgrn_cols(rev) + [pl.BlockSpec((2, HG_W), lambda j: (0, 0)), pl.BlockSpec((1, HD), lambda j: (0, 0)),
                                    pl.BlockSpec((HB, HG_W), lambda j: (rev(j), 0)),
                                    pl.BlockSpec((4, NCH, HD, HD), lambda j: (0, rev(j), 0, 0)),
                                    pl.BlockSpec((HB, HG_W), lambda j: (rev(j), 1)),
                                    pl.BlockSpec((HB, SWA_W), lambda j: (rev(j), 0)),
                                    pl.BlockSpec((HB // 2, 2 * KV_W), lambda j: (rev(j), 0)),
                                    pl.BlockSpec((HB // 2, 2 * KV_W), lambda j: (rev(j), 0))],
        out_specs=(pl.BlockSpec((HB, D_IN), lambda j: (rev(j), 0)), pl.BlockSpec((8, HG_W), lambda j: (0, 0)),
                   pl.BlockSpec((8, HD), lambda j: (0, 0))),
        scratch_shapes=[pltpu.VMEM((4, HD, HD), F32)],
        compiler_params=_params(dimension_semantics=("arbitrary",)),
    )(*[z] * 8, hgrn_lb, onorm, o_save, sprev, dymix, *dza)


XB = 512


def _xattn_fwd(q, k, v, t):
    tb = min(XB, t)

    def body(q_ref, k_ref, v_ref, o_ref):
        for h in range(XH):
            cols = slice(XD * h, XD * (h + 1))
            s = _dot(q_ref[:, cols], k_ref[:, cols], 1, 1) * (XD ** -0.5)
            p = jnp.exp(s - jnp.max(s, axis=-1, keepdims=True))
            l = jnp.sum(p, axis=-1, keepdims=True)
            o_ref[:, cols] = (_dot(p.astype(BF16), v_ref[:, cols], 1, 0) * (1.0 / l)).astype(BF16)

    row = pl.BlockSpec((tb, D), lambda i: (i, 0))
    mem = pl.BlockSpec(k.shape, lambda i: (0, 0))
    return pl.pallas_call(
        body, name="xattn_fwd", out_shape=jax.ShapeDtypeStruct((t, D), BF16), grid=(t // tb,),
        in_specs=[row, mem, mem], out_specs=row, compiler_params=_params(),
    )(q, k, v)


def _xattn_bwd(q, k, v, do, t):
    tb = min(XB, t)

    def body(q_ref, k_ref, v_ref, do_ref, dq_ref, dk_ref, dv_ref):
        @pl.when(pl.program_id(0) == 0)
        def _():
            dk_ref[...] = jnp.zeros_like(dk_ref)
            dv_ref[...] = jnp.zeros_like(dv_ref)

        for h in range(XH):
            cols = slice(XD * h, XD * (h + 1))
            qh, kh, vh, doh = q_ref[:, cols], k_ref[:, cols], v_ref[:, cols], do_ref[:, cols]
            s = _dot(qh, kh, 1, 1) * (XD ** -0.5)
            p = jnp.exp(s - jnp.max(s, axis=-1, keepdims=True))
            p = p * (1.0 / jnp.sum(p, axis=-1, keepdims=True))
            dp = _dot(doh, vh, 1, 1)
            ds = (p * (dp - jnp.sum(p * dp, axis=-1, keepdims=True)) * (XD ** -0.5)).astype(BF16)
            dq_ref[:, cols] = _dot(ds, kh, 1, 0).astype(BF16)
            dk_ref[:, cols] += _dot(ds, qh, 0, 0)
            dv_ref[:, cols] += _dot(p.astype(BF16), doh, 0, 0)

    row = pl.BlockSpec((tb, D), lambda i: (i, 0))
    mem = pl.BlockSpec(k.shape, lambda i: (0, 0))
    return pl.pallas_call(
        body, name="xattn_bwd",
        out_shape=(jax.ShapeDtypeStruct((t, D), BF16), jax.ShapeDtypeStruct(k.shape, F32),
                   jax.ShapeDtypeStruct(k.shape, F32)),
        grid=(t // tb,), in_specs=[row, mem, mem, row], out_specs=(row, mem, mem),
        compiler_params=_params(dimension_semantics=("arbitrary",)),
    )(q, k, v, do)


def _mem_kv(mem, g_mem, wk, wv):
    def body(m_ref, g_ref, wk_ref, wv_ref, mn_ref, k_ref, v_ref):
        m_ = m_ref[...]
        mn = (m_ * _rstd(m_) * g_ref[...]).astype(BF16)
        mn_ref[...] = mn
        k_ref[...] = _dot(mn, wk_ref[...], 1, 0).astype(BF16)
        v_ref[...] = _dot(mn, wv_ref[...], 1, 0).astype(BF16)

    return pl.pallas_call(body, name="mem_kv", out_shape=(jax.ShapeDtypeStruct(mem.shape, BF16),) * 3,
                          compiler_params=_params())(mem, g_mem, wk, wv)


def _mem_kv_bwd(mn, mem, dk, dv, wk, wv, dep=None):
    deps = [] if dep is None else [dep]

    def body(mn_ref, m_ref, dk_ref, dv_ref, wk_ref, wv_ref, *rest):
        gk_ref, gv_ref, dg_ref = rest[len(deps):]
        mn = mn_ref[...]
        dkb, dvb = dk_ref[...].astype(BF16), dv_ref[...].astype(BF16)
        gk_ref[...] = _dot(mn, dkb, 0, 0).astype(BF16)
        gv_ref[...] = _dot(mn, dvb, 0, 0).astype(BF16)
        dmn = _dot(dkb, wk_ref[...], 1, 1) + _dot(dvb, wv_ref[...], 1, 1)
        m_ = m_ref[...]
        dg_ref[...] = _rowsum8(dmn * (m_ * _rstd(m_)))

    vmem = pl.BlockSpec(memory_space=pltpu.VMEM)
    return pl.pallas_call(
        body, name="mem_kv_bwd",
        out_shape=(jax.ShapeDtypeStruct(wk.shape, BF16), jax.ShapeDtypeStruct(wv.shape, BF16),
                   jax.ShapeDtypeStruct((8, D), F32)),
        in_specs=[vmem] * 6 + [ANY_SPEC] * len(deps), out_specs=(vmem,) * 3, compiler_params=_params(),
    )(mn, mem, dk, dv, wk, wv, *deps)


FM, FN = 1024, 1408


def _ffn_up(u, wgt, wut, t):
    tm = min(FM, t)

    def body(u_ref, wg_ref, wu_ref, g_ref, up_ref, a_ref):
        u_ = u_ref[...]
        g = _dot(u_, wg_ref[...], 1, 1)
        up = _dot(u_, wu_ref[...], 1, 1)
        g_ref[...] = g.astype(BF16)
        up_ref[...] = up.astype(BF16)
        a_ref[...] = (g * _sig(g) * up).astype(BF16)

    w = pl.BlockSpec((FN, D), lambda j, i: (j, 0))
    o = pl.BlockSpec((tm, FN), lambda j, i: (i, j))
    return pl.pallas_call(
        body, name="ffn_up", out_shape=(jax.ShapeDtypeStruct((t, D_FF), BF16),) * 3,
        grid=(D_FF // FN, t // tm), in_specs=[pl.BlockSpec((tm, D), lambda j, i: (i, 0)), w, w],
        out_specs=(o, o, o), compiler_params=_params(),
    )(u, wgt, wut)


FB = 256


def _ffn_bwd(dy, wd, gate, up, wgt, wut, dh_out, hn, y, g_post, g_pre, t, dep=None):
    tb = min(FB, t)
    deps = [] if dep is None else [dep]

    def body(dy_ref, wd_ref, g_ref, up_ref, wg_ref, wu_ref, dho_ref, hn_ref, y_ref, gp_ref, gn_ref, *rest):
        dg_ref, dup_ref, dh_ref, dyp_ref, dgn_ref, dgp_ref = rest[len(deps):]

        @pl.when(pl.program_id(0) == 0)
        def _():
            dgn_ref[...] = jnp.zeros_like(dgn_ref)
            dgp_ref[...] = jnp.zeros_like(dgp_ref)

        da = _dot(dy_ref[...], wd_ref[...], 1, 1)
        g = g_ref[...].astype(F32)
        sg = _sig(g)
        dup = (da * g * sg).astype(BF16)
        dgate = (da * up_ref[...].astype(F32) * (sg * (1.0 + g * (1.0 - sg)))).astype(BF16)
        dup_ref[...] = dup
        dg_ref[...] = dgate
        du = _dot(dgate, wg_ref[...], 1, 0) + _dot(dup, wu_ref[...], 1, 0)
        dh, dyp, dgn, dgp = _ep_post_pre_bwd(du, dho_ref[...], hn_ref[...], y_ref[...], gp_ref[...], gn_ref[...])
        dh_ref[...] = dh
        dyp_ref[...] = dyp.astype(BF16)
        dgn_ref[...] += dgn
        dgp_ref[...] += dgp

    row = lambda w: pl.BlockSpec((tb, w), lambda i: (i, 0))
    whole = lambda a: pl.BlockSpec(a.shape, lambda i: (0,) * a.ndim, pipeline_mode=pl.Buffered(1))
    acc = pl.BlockSpec((8, D), lambda i: (0, 0))
    return pl.pallas_call(
        body, name="ffn_bwd",
        out_shape=(jax.ShapeDtypeStruct((t, D_FF), BF16), jax.ShapeDtypeStruct((t, D_FF), BF16),
                   jax.ShapeDtypeStruct((t, D), F32), jax.ShapeDtypeStruct((t, D), BF16),
                   jax.ShapeDtypeStruct((8, D), F32), jax.ShapeDtypeStruct((8, D), F32)),
        grid=(t // tb,),
        in_specs=[row(D), whole(wd), row(D_FF), row(D_FF), whole(wgt), whole(wut), row(D), row(D), row(D),
                  whole(g_post), whole(g_pre)] + [ANY_SPEC] * len(deps),
        out_specs=(row(D_FF), row(D_FF), row(D), row(D), acc, acc),
        compiler_params=_params(dimension_semantics=("arbitrary",)),
    )(dy, wd, gate, up, wgt, wut, dh_out, hn, y, g_post, g_pre, *deps)


def _local_step(x, mem, target, fetch, sm, emit=None, first_dep=None, milestone=None):
    t = x.shape[0]
    w, gw = {}, {}

    def out(key, g):
        gw[key] = g
        return None if emit is None else emit(key, g)

    def tell(tag, value):
        return None if milestone is None else milestone(tag, value)
    u1 = _prenorm(x, sm["g_mix_pre"], name="prenorm_mix", dep=first_dep)
    w["winT"] = fetch("winT", u1)
    z = _mm(u1, w["winT"], tb=True, out_dtype=F32, tm=1024, tn=1408, name="mm_z", n_outer=True)
    ymix, lse = _swa_fwd(z, sm["sinks"], t, dep=tell("z", z))
    ymix, o_h, sprev = _hgrn2_fwd(z, sm["hgrn_lb"], sm["hgrn_onorm"], ymix, t, dep=tell("swa", lse))
    w["wout"] = fetch("wout", ymix)
    y1, h1, u2 = _mm_rows([(ymix, w["wout"], False)], [x], [sm["g_mix_post"], sm["g_x_pre"]], _ep_post_pre,
                          _EP_POST_PRE_OUTS, tm=1024, name="mm_y1_post")
    for key in ("wq", "wk", "wv"):
        w[key] = fetch(key, u2)
    qx = _mm(u2, w["wq"], out_dtype=BF16, tm=1024, tn=1024, name="mm_qx")
    mn, kx, vx = _mem_kv(mem, sm["g_mem"], w["wk"], w["wv"])
    ox = _xattn_fwd(qx, kx, vx, t)
    w["wo"] = fetch("wo", ox)
    y2, h2, u3 = _mm_rows([(ox, w["wo"], False)], [h1], [sm["g_x_post"], sm["g_ffn_pre"]], _ep_post_pre,
                          _EP_POST_PRE_OUTS, tm=1024, name="mm_y2_post", dep=tell("ox", ox))
    w["wgT"], w["wuT"] = fetch("wgT", u3), fetch("wuT", u3)
    gate, up, act = _ffn_up(u3, w["wgT"], w["wuT"], t)
    w["wd"] = fetch("wd", act)
    sq, dh3, dy3, dg_ffn_post = _mm_rows([(act, w["wd"], False)], [h2, target], [sm["g_ffn_post"]], _ep_final_loss,
                                         _EP_FINAL_LOSS_OUTS, tm=512, name="mm_y3_loss")
    dep = out("wd", _mm(act, dy3, ta=True, out_dtype=BF16, tm=1408, tn=1024, name="mm_gwd"))
    dgate, dup, dh2, dy2, dg_ffn_pre, dg_x_post = _ffn_bwd(
        dy3, w["wd"], gate, up, w["wgT"], w["wuT"], dh3, h2, y2, sm["g_x_post"], sm["g_ffn_pre"], t, dep=dep)
    dep = out("wgT", _mm(dgate, u3, ta=True, out_dtype=BF16, tm=1408, tn=1024, name="mm_gwg"))
    dep = out("wuT", _mm(dup, u3, ta=True, out_dtype=BF16, tm=1408, tn=1024, name="mm_gwu", dep=dep))
    dep = out("wo", _mm(ox, dy2, ta=True, out_dtype=BF16, tm=512, tn=1024, name="mm_gwo", dep=dep))
    dox = _mm(dy2, w["wo"], tb=True, out_dtype=BF16, tm=1024, tn=1024, name="mm_dox", dep=dep)
    dqx, dkx, dvx = _xattn_bwd(qx, kx, vx, dox, t)
    dep = out("wq", _mm(u2, dqx, ta=True, out_dtype=BF16, tm=512, tn=1024, name="mm_gwq"))
    gwk, gwv, dg_mem = _mem_kv_bwd(mn, mem, dkx, dvx, w["wk"], w["wv"], dep=dep)
    out("wk", gwk)
    dep = out("wv", gwv)
    dh1, dy1, dg_x_pre, dg_mix_post = _mm_rows(
        [(dqx, w["wq"], True)], [dh2, h1, y1], [sm["g_mix_post"], sm["g_x_pre"]],
        _ep_post_pre_bwd, _EP_POST_PRE_BWD_OUTS, tm=512, name="mm_du2_post_bwd", dep=dep)
    dep = out("wout", _mm(ymix, dy1, ta=True, out_dtype=BF16, tm=512, tn=1024, name="mm_gwout"))
    dymix = _mm(dy1, w["wout"], tb=True, out_dtype=BF16, tm=1024, tn=1024, name="mm_dymix", dep=dep)
    *dza, dsinks = _swa_bwd(z, sm["sinks"], ymix, lse, dymix, t)
    dz, dlb, donorm = _hgrn2_bwd(z, sm["hgrn_lb"], sm["hgrn_onorm"], o_h, sprev, dymix, dza, t)
    dep = out("winT", _mm(dz, u1, ta=True, out_dtype=BF16, tm=1408, tn=1024, name="mm_gwin"))
    grad_x, dg_mix_pre = _mm_rows([(dz, w["winT"], False)], [dh1, x], [sm["g_mix_pre"]], _ep_pre_bwd,
                                  _EP_PRE_BWD_OUTS, tm=512, name="mm_du1_pre_bwd", dep=dep)
    parts = dict(g_mix_pre=dg_mix_pre, g_mix_post=dg_mix_post, g_mem=dg_mem, g_x_pre=dg_x_pre,
                 g_x_post=dg_x_post, g_ffn_pre=dg_ffn_pre, g_ffn_post=dg_ffn_post,
                 hgrn_onorm=donorm, hgrn_lb=dlb, sinks=dsinks, sq=sq)
    return grad_x, gw, parts


def _position():
    return lax.axis_index("x"), lax.axis_index("y"), lax.axis_index("c")


def _peer(pos, k):
    x, y, c = pos
    return (1 - x if k & 4 else x, 1 - y if k & 2 else y, 1 - c if k & 1 else c)


def _linear(pos):
    x, y, c = pos
    return 4 * x + 2 * y + c


HBM_SPEC = pl.BlockSpec(memory_space=pltpu.HBM)
SEM_SPEC = pl.BlockSpec(memory_space=pltpu.SEMAPHORE)
DATAFLOW = pltpu.SideEffectType.DATAFLOW_SIDE_EFFECTING
SEND_ORDER = (1, 2, 4, 3, 5, 6, 7)


def _in_hbm(a):
    return pltpu.with_memory_space_constraint(a, pltpu.HBM)


def _prepare_weights(shards, *, name, dep=None):
    n = len(shards)
    deps = [] if dep is None else [dep]

    def body(*refs):
        ins, (outs, lands, sem) = refs[:n], (refs[-2 * n - 1:-n - 1], refs[-n - 1:-1], refs[-1])
        me_lin = _linear(_position())
        copies = []
        for a in range(n):
            r = ins[a].shape[0]
            outs[a][...] = ins[a][...].astype(BF16)
            copies.append(pltpu.make_async_copy(outs[a], lands[a].at[pl.ds(me_lin * r, r), :], sem.at[a]))
            copies[-1].start()
        for cp in copies:
            cp.wait()

    vmem = pl.BlockSpec(memory_space=pltpu.VMEM)
    res = pl.pallas_call(
        body, name=name,
        out_shape=tuple(jax.ShapeDtypeStruct(s.shape, BF16) for s in shards)
        + tuple(jax.ShapeDtypeStruct((N_DEV * s.shape[0], s.shape[1]), BF16) for s in shards),
        in_specs=[vmem] * n + [ANY_SPEC] * len(deps), out_specs=tuple([vmem] * n + [ANY_SPEC] * n),
        scratch_shapes=[pltpu.SemaphoreType.DMA((n,))], compiler_params=_params(),
    )(*shards, *deps)
    return res[:n], res[n:]


def _copies_start(arrays, plan, n, *, name):
    na = len(arrays)

    def body(*refs):
        ins, send_sems, recv_sems = refs[:na], refs[na], refs[na + 1]
        me = _position()
        for j in range(n):
            src, dst, peer, _ = plan(ins, me, j)
            pltpu.make_async_remote_copy(src_ref=src, dst_ref=dst, send_sem=send_sems.at[j], recv_sem=recv_sems.at[j],
                                         device_id=peer, device_id_type=MESH).start()

    return pl.pallas_call(
        body, name=name,
        out_shape=(pltpu.SemaphoreType.DMA((n,)), pltpu.SemaphoreType.DMA((n,)))
        + tuple(pltpu.HBM(a.shape, a.dtype) for a in arrays),
        in_specs=(HBM_SPEC,) * na, out_specs=(SEM_SPEC, SEM_SPEC) + (HBM_SPEC,) * na,
        input_output_aliases={i: 2 + i for i in range(na)},
        compiler_params=pltpu.CompilerParams(has_side_effects=DATAFLOW),
    )(*[_in_hbm(a) for a in arrays])


def _copies_wait(send_sems, recv_sems, arrays, plan, n, after, *, name):
    na = len(arrays)

    def body(*refs):
        ins, send_sems, recv_sems = refs[:na], refs[na], refs[na + 1]
        me = _position()
        for j in range(n):
            src, _, peer, landed = plan(ins, me, j)
            copy = pltpu.make_async_remote_copy(src_ref=src, dst_ref=landed, send_sem=send_sems.at[j],
                                                recv_sem=recv_sems.at[j], device_id=peer, device_id_type=MESH)
            copy.wait_send()
            copy.wait_recv()

    return pl.pallas_call(
        body, name=name, out_shape=tuple(pltpu.HBM(a.shape, a.dtype) for a in arrays),
        in_specs=(HBM_SPEC,) * na + (SEM_SPEC, SEM_SPEC, ANY_SPEC), out_specs=(HBM_SPEC,) * na,
        input_output_aliases={i: i for i in range(na)},
        compiler_params=pltpu.CompilerParams(has_side_effects=DATAFLOW),
    )(*arrays, send_sems, recv_sems, after)


SAME_CORE = (2, 4, 6)


class _TwoLevelGather:
    def __init__(self, shards, lands, *, name):
        n = self.n = len(shards)
        self.name = name
        first_peers = (1,) + SAME_CORE

        def rows(ref, pos):
            r = ref.shape[0] // N_DEV
            return ref.at[pl.ds(_linear(pos) * r, r), :]

        def first(refs, me, j):
            a, peer = j // 4, _peer(me, first_peers[j % 4])
            return refs[a], rows(refs[n + a], me), peer, rows(refs[n + a], peer)

        def second(refs, me, j):
            a, sibling = j // 3, _peer(me, 1)
            mine = rows(refs[a], _peer(me, SAME_CORE[j % 3]))
            return mine, mine, sibling, rows(refs[a], _peer(sibling, SAME_CORE[j % 3]))

        self._first, self._second = first, second
        self._flight = _copies_start(list(shards) + list(lands), first, 4 * n, name=name + "_send")
        self.dep = self._flight[2]

    def pass_on(self, after):
        send1, recv1, *arrays = self._flight
        arrays = _copies_wait(send1, recv1, arrays, self._first, 4 * self.n, after, name=self.name + "_recv")
        self._flight = _copies_start(list(arrays[self.n:]), self._second, 3 * self.n, name=self.name + "_pass")
        return self._flight[2]

    def finish(self, after):
        send2, recv2, *lands = self._flight
        return _copies_wait(send2, recv2, lands, self._second, 3 * self.n, after, name=self.name + "_pass_recv")


def _exchange_start(gs, *, name):
    n = len(gs)
    rows = [g.shape[0] // N_DEV for g in gs]
    lands = [lax.empty((N_DEV - 1, r, g.shape[1]), g.dtype) for g, r in zip(gs, rows)]

    def body(*refs):
        g_refs, land_refs = refs[:n], refs[n:2 * n]
        send_sems, recv_sems = refs[2 * n:3 * n], refs[3 * n:4 * n]
        me = _position()
        for a in range(n):
            for k in SEND_ORDER:
                peer = _peer(me, k)
                pltpu.make_async_remote_copy(
                    src_ref=g_refs[a].at[pl.ds(_linear(peer) * rows[a], rows[a]), :],
                    dst_ref=land_refs[a].at[k - 1],
                    send_sem=send_sems[a].at[k - 1], recv_sem=recv_sems[a].at[k - 1],
                    device_id=peer, device_id_type=MESH).start()

    res = pl.pallas_call(
        body, name=name,
        out_shape=tuple(pltpu.SemaphoreType.DMA((N_DEV - 1,)) for _ in range(2 * n))
        + tuple(pltpu.HBM(a.shape, a.dtype) for a in gs + lands),
        in_specs=(HBM_SPEC,) * (2 * n), out_specs=(SEM_SPEC,) * (2 * n) + (HBM_SPEC,) * (2 * n),
        input_output_aliases={i: 2 * n + i for i in range(2 * n)},
        compiler_params=pltpu.CompilerParams(has_side_effects=DATAFLOW),
    )(*[_in_hbm(a) for a in gs + lands])
    return [(res[a], res[n + a], res[2 * n + a], res[3 * n + a]) for a in range(n)]


def _exchange_wait(send_sems, recv_sems, g_thru, land_thru, after, *, name):
    r = land_thru.shape[1]

    def body(g_ref, land_ref, send_sems, recv_sems, after_ref, g_dead, got_ref):
        del after_ref, g_dead, got_ref
        me = _position()
        for k in SEND_ORDER:
            peer = _peer(me, k)
            copy = pltpu.make_async_remote_copy(
                src_ref=g_ref.at[pl.ds(_linear(peer) * r, r), :], dst_ref=land_ref.at[k - 1],
                send_sem=send_sems.at[k - 1], recv_sem=recv_sems.at[k - 1],
                device_id=peer, device_id_type=MESH)
            copy.wait_send()
            copy.wait_recv()

    return pl.pallas_call(
        body, name=name,
        out_shape=(pltpu.HBM(g_thru.shape, g_thru.dtype), pltpu.HBM(land_thru.shape, land_thru.dtype)),
        in_specs=(HBM_SPEC, HBM_SPEC, SEM_SPEC, SEM_SPEC, pl.BlockSpec(memory_space=pl.ANY)),
        out_specs=(HBM_SPEC, HBM_SPEC), input_output_aliases={0: 0, 1: 1},
        compiler_params=pltpu.CompilerParams(has_side_effects=DATAFLOW),
    )(g_thru, land_thru, send_sems, recv_sems, after)


def _adamw_math(w, g, m, v):
    m = B1 * m + (1.0 - B1) * g
    v = B2 * v + (1.0 - B2) * (g * g)
    delta = -LR * ((m / C1) / (jnp.sqrt(v / C2) + AEPS) + WD * w)
    return delta, m, v


def _sum_adamw(items, *, name):
    n = len(items)

    def body(*refs):
        ins, outs, scratch = refs[:5 * n], refs[5 * n:9 * n], refs[9 * n:]
        me_lin = _linear(_position())
        mine = []
        for a in range(n):
            r = items[a][2].shape[0]
            mine.append(pltpu.make_async_copy(ins[5 * a].at[pl.ds(me_lin * r, r), :], scratch[a], scratch[n].at[a]))
            mine[-1].start()
        for a in range(n):
            _, land_ref, w_ref, m_ref, v_ref = ins[5 * a:5 * a + 5]
            g_ref, d_ref, nm_ref, nv_ref = outs[4 * a:4 * a + 4]
            g = land_ref[0].astype(F32)
            for s in range(1, N_DEV - 1):
                g = g + land_ref[s].astype(F32)
            mine[a].wait()
            g = scratch[a][...].astype(F32) + g
            g_ref[...] = g
            d_ref[...], nm_ref[...], nv_ref[...] = _adamw_math(w_ref[...], g, m_ref[...], v_ref[...])

    vmem = pl.BlockSpec(memory_space=pltpu.VMEM)
    res = pl.pallas_call(
        body, name=name,
        out_shape=tuple(jax.ShapeDtypeStruct(it[2].shape, F32) for it in items for _ in range(4)),
        in_specs=[ANY_SPEC, vmem, vmem, vmem, vmem] * n, out_specs=(vmem,) * (4 * n),
        scratch_shapes=[pltpu.VMEM(it[2].shape, BF16) for it in items] + [pltpu.SemaphoreType.DMA((n,))],
        compiler_params=_params(),
    )(*[a for it in items for a in it])
    return [res[4 * a:4 * a + 4] for a in range(n)]


SMALL = ("g_mix_pre", "g_mix_post", "g_mem", "g_x_pre", "g_x_post", "g_ffn_pre", "g_ffn_post",
         "hgrn_onorm", "hgrn_lb", "sinks")
SMALL_W = dict(hgrn_onorm=HD, hgrn_lb=HG_W, sinks=8)
SQ_ROW = len(SMALL)
PACK_ROWS = 16


def _small_pack(parts):
    ns = len(SMALL)

    def body(*refs):
        part, mine, slots, sem = refs[:ns + 1], refs[ns + 1], refs[ns + 2], refs[ns + 3]
        mine[...] = jnp.zeros((PACK_ROWS, D), F32)
        for r, name in enumerate(SMALL):
            wd = SMALL_W.get(name, D)
            mine[r:r + 1, 0:wd] = jnp.sum(part[r][...], axis=0, keepdims=True)[:, 0:wd]
        sq = jnp.sum(part[ns][...]) * (0.5 / D)
        mine[SQ_ROW:SQ_ROW + 1, :] = jnp.full((1, D), sq, F32)
        own = pltpu.make_async_copy(mine, slots.at[_linear(_position())], sem)
        own.start()
        own.wait()

    vmem = pl.BlockSpec(memory_space=pltpu.VMEM)
    return pl.pallas_call(
        body, name="small_pack",
        out_shape=(jax.ShapeDtypeStruct((PACK_ROWS, D), F32), jax.ShapeDtypeStruct((N_DEV, PACK_ROWS, D), F32)),
        in_specs=[vmem] * (ns + 1), out_specs=(vmem, ANY_SPEC),
        scratch_shapes=[pltpu.SemaphoreType.DMA(())], compiler_params=_params(),
    )(*[parts[n] for n in SMALL], parts["sq"])


def _small_exchange(mine, slots):
    def plan(refs, me, j):
        peer = _peer(me, j + 1)
        return refs[0], refs[1].at[_linear(me)], peer, refs[1].at[_linear(peer)]

    send, recv, mine1, slots1 = _copies_start([mine, slots], plan, N_DEV - 1, name="small_send")
    return lambda after: _copies_wait(send, recv, [mine1, slots1], plan, N_DEV - 1, after, name="small_recv")[1]


def _small_update(slots, sm, m_sm, v_sm):
    ns = len(SMALL)

    def body(*refs):
        tot = refs[0][0]
        for s in range(1, N_DEV):
            tot = tot + refs[0][s]
        w_refs, m_refs, v_refs = refs[1:ns + 1], refs[ns + 1:2 * ns + 1], refs[2 * ns + 1:3 * ns + 1]
        outs = refs[3 * ns + 1:]
        loss_ref = outs[0]
        g_out, d_out = outs[1:ns + 1], outs[ns + 1:2 * ns + 1]
        nm_out, nv_out = outs[2 * ns + 1:3 * ns + 1], outs[3 * ns + 1:4 * ns + 1]
        loss_ref[...] = tot[SQ_ROW:SQ_ROW + 1, 0:1]
        for r, name in enumerate(SMALL):
            wd = SMALL_W.get(name, D)
            g = tot[r:r + 1, 0:wd]
            w = w_refs[r][...]
            if name == "hgrn_lb":
                mx = jnp.maximum(w[0:1], w[1:2])
                e0, e1 = jnp.exp(w[0:1] - mx), jnp.exp(w[1:2] - mx)
                lb0 = e0 / (e0 + e1)
                g0 = g * lb0 * (1.0 - lb0)
                for i, gi in enumerate((g0, -g0)):
                    d, nm, nv = _adamw_math(w[i:i + 1], gi, m_refs[r][i:i + 1, :], v_refs[r][i:i + 1, :])
                    g_out[r][i:i + 1, :] = gi
                    d_out[r][i:i + 1, :], nm_out[r][i:i + 1, :], nv_out[r][i:i + 1, :] = d, nm, nv
            else:
                d, nm, nv = _adamw_math(w, g, m_refs[r][...], v_refs[r][...])
                g_out[r][...] = g
                d_out[r][...], nm_out[r][...], nv_out[r][...] = d, nm, nv

    shapes = [jax.ShapeDtypeStruct(sm[n].shape, F32) for n in SMALL]
    res = pl.pallas_call(
        body, name="small_update", out_shape=tuple([jax.ShapeDtypeStruct((1, 1), F32)] + shapes * 4),
        compiler_params=_params(),
    )(slots, *[sm[n] for n in SMALL], *[m_sm[n] for n in SMALL], *[v_sm[n] for n in SMALL])
    groups = [dict(zip(SMALL, res[1 + i * ns:1 + (i + 1) * ns])) for i in range(4)]
    return res[0], groups[0], groups[1], groups[2], groups[3]


BIG = ("w_in", "w_gate", "w_up", "w_down", "w_out", "wq_x", "wk_x", "wv_x", "wo_x")
BIG_KEY = dict(w_in="winT", w_gate="wgT", w_up="wuT", w_down="wd", w_out="wout", wq_x="wq", wk_x="wk",
               wv_x="wv", wo_x="wo")
TRANSPOSED = ("w_in", "w_gate", "w_up")
WEIGHTS = ("w_in", "sinks", "hgrn_lb", "hgrn_onorm", "w_out", "g_mix_pre", "g_mix_post", "g_mem", "g_x_pre",
           "g_x_post", "wq_x", "wk_x", "wv_x", "wo_x", "g_ffn_pre", "g_ffn_post", "w_gate", "w_up", "w_down")


def kernel(x, mem, w_in, sinks, hgrn_lb, hgrn_onorm, w_out, g_mix_pre, g_mix_post, g_mem, g_x_pre, g_x_post, wq_x, wk_x, wv_x, wo_x, g_ffn_pre, g_ffn_post, w_gate, w_up, w_down, loss_target, m_w_in, m_sinks, m_hgrn_lb, m_hgrn_onorm, m_w_out, m_g_mix_pre, m_g_mix_post, m_g_mem, m_g_x_pre, m_g_x_post, m_wq_x, m_wk_x, m_wv_x, m_wo_x, m_g_ffn_pre, m_g_ffn_post, m_w_gate, m_w_up, m_w_down, v_w_in, v_sinks, v_hgrn_lb, v_hgrn_onorm, v_w_out, v_g_mix_pre, v_g_mix_post, v_g_mem, v_g_x_pre, v_g_x_post, v_wq_x, v_wk_x, v_wv_x, v_wo_x, v_g_ffn_pre, v_g_ffn_post, v_w_gate, v_w_up, v_w_down):
    given = dict(locals())
    wts = {n: given[n] for n in WEIGHTS}
    ms = {n: given["m_" + n] for n in WEIGHTS}
    vs = {n: given["v_" + n] for n in WEIGHTS}

    def mat(a, name):
        a = a[0]
        return a.T if name in TRANSPOSED else a

    groups = (("w_in",), ("w_out", "wq_x", "wk_x", "wv_x", "wo_x"), ("w_gate", "w_up", "w_down"))
    gathers = []

    def start_group(g, dep):
        tag = ("w_in", "w_attn", "w_ffn")[g]
        shards, lands = _prepare_weights([mat(wts[n], n) for n in groups[g]], name="prepare_" + tag, dep=dep)
        gathers.append(_TwoLevelGather(shards, lands, name=tag))
        return gathers[-1].dep

    first_dep = start_group(1, start_group(0, None))
    name_of = {k: n for n, k in BIG_KEY.items()}
    gathered = {}

    def milestone(tag, value):
        if tag == "z":
            return start_group(2, value)
        return gathers[{"swa": 1, "ox": 2}[tag]].pass_on(value)

    def fetch(key, after):
        name = name_of[key]
        if name not in gathered:
            g = [i for i, group in enumerate(groups) if name in group][0]
            if g == 0:
                gathers[0].pass_on(after)
            gathered.update(zip(groups[g], gathers[g].finish(after)))
        return gathered[name]

    sm = {n: wts[n] for n in SMALL}
    started, held = {}, {}
    send_with = {k: group for group in (("wgT", "wuT"), ("wo", "wq", "wk", "wv")) for k in group}

    def emit(key, g):
        held[key] = g
        group = send_with.get(key, (key,))
        if key != group[-1]:
            return None
        flights = _exchange_start([held[k] for k in group], name="grad_send_" + name_of[group[0]])
        started.update({name_of[k]: f for k, f in zip(group, flights)})
        return flights[-1][2]

    grad_x, _, parts = _local_step(x[0], mem[0], loss_target[0], fetch, sm, emit, first_dep=first_dep, milestone=milestone)
    small_finish = _small_exchange(*_small_pack(parts))
    grads, deltas, new_m, new_v = {}, {}, {}, {}
    after = grad_x
    for group in (("w_down",), ("w_gate", "w_up"), ("wo_x", "wq_x", "wk_x", "wv_x", "w_out"), ("w_in",)):
        items = []
        for n in group:
            g_all, land = _exchange_wait(*started[n], after, name="grad_recv_" + n)
            items.append((g_all, land, mat(wts[n], n), mat(ms[n], n), mat(vs[n], n)))
            after = land
        for n, res in zip(group, _sum_adamw(items, name="adamw_" + group[0])):
            after = res[1]
            if n in TRANSPOSED:
                res = [a.T for a in res]
            grads[n], deltas[n], new_m[n], new_v[n] = [a[None] for a in res]
    loss, g_s, d_s, m_s, v_s = _small_update(small_finish(after), sm, {n: ms[n] for n in SMALL},
                                             {n: vs[n] for n in SMALL})
    grads.update(g_s), deltas.update(d_s), new_m.update(m_s), new_v.update(v_s)
    return (loss[0, 0], grad_x[None], *[grads[n] for n in WEIGHTS], *[deltas[n] for n in WEIGHTS],
            *[new_m[n] for n in WEIGHTS], *[new_v[n] for n in WEIGHTS])
```

```python
import functools

import jax
import jax.numpy as jnp
from jax import lax
from jax.experimental import pallas as pl
from jax.experimental.pallas import tpu as pltpu

F32 = jnp.float32
BF16 = jnp.bfloat16

D = 1024
D_IN = 2816
D_FF = 2816
CHUNK = 64
SWA_W = 512
KV_W = 128
HG_W = 512
HD = 128
ZQH, ZFH, ZIH, ZGH = 768, 1280, 1792, 2304
XH, XD = 4, 256
EPS = 1e-6
NEG = -1e30
N_DEV = 8
MESH = pl.DeviceIdType.MESH

LR, B1, B2, AEPS, WD, STEP = 0.001, 0.9, 0.999, 1e-08, 0.01, 10
C1 = 1.0 - B1 ** STEP
C2 = 1.0 - B2 ** STEP

VMEM_LIMIT = 56 * 1024 * 1024


def _params(**kw):
    return pltpu.CompilerParams(vmem_limit_bytes=VMEM_LIMIT, **kw)


def _sig(x):
    return 1.0 / (1.0 + jnp.exp(-x))


def _rowsum8(x):
    r, w = x.shape
    return jnp.sum(x.reshape(r // 8, 8, w), axis=0)


def _dot(a, b, ca, cb, precision=None):
    return lax.dot_general(a, b, (((ca,), (cb,)), ((), ())), preferred_element_type=F32,
                           precision=precision)


ANY_SPEC = pl.BlockSpec(memory_space=pl.ANY)


def _mm(a, b, *, ta=False, tb=False, out_dtype, tm, tn, tk=None, name, dep=None, n_outer=False):
    m = a.shape[1] if ta else a.shape[0]
    k = a.shape[0] if ta else a.shape[1]
    n = b.shape[0] if tb else b.shape[1]
    tm, tn = min(tm, m), min(tn, n)
    tk = k if tk is None else min(tk, k)
    nk = k // tk
    assert m % tm == 0 and n % tn == 0 and k % tk == 0, (name, m, n, k, tm, tn, tk)
    ij = (lambda g0, g1: (g1, g0)) if n_outer else (lambda g0, g1: (g0, g1))
    a_spec = (pl.BlockSpec((tk, tm), lambda g0, g1, kk: (kk, ij(g0, g1)[0])) if ta
              else pl.BlockSpec((tm, tk), lambda g0, g1, kk: (ij(g0, g1)[0], kk)))
    b_spec = (pl.BlockSpec((tn, tk), lambda g0, g1, kk: (ij(g0, g1)[1], kk)) if tb
              else pl.BlockSpec((tk, tn), lambda g0, g1, kk: (kk, ij(g0, g1)[1])))
    ca, cb = (0 if ta else 1), (1 if tb else 0)

    deps = [] if dep is None else [dep]

    def body(a_ref, b_ref, *rest):
        o_ref, acc = rest[len(deps)], rest[len(deps) + 1:]
        p = _dot(a_ref[...].astype(BF16), b_ref[...].astype(BF16), ca, cb)
        if nk == 1:
            o_ref[...] = p.astype(out_dtype)
        else:
            acc_ref, = acc
            kk = pl.program_id(2)

            @pl.when(kk == 0)
            def _():
                acc_ref[...] = p

            @pl.when(kk > 0)
            def _():
                acc_ref[...] += p

            @pl.when(kk == nk - 1)
            def _():
                o_ref[...] = acc_ref[...].astype(out_dtype)

    return pl.pallas_call(
        body, name=name, out_shape=jax.ShapeDtypeStruct((m, n), out_dtype),
        grid=(n // tn, m // tm, nk) if n_outer else (m // tm, n // tn, nk),
        in_specs=[a_spec, b_spec] + [ANY_SPEC] * len(deps),
        out_specs=pl.BlockSpec((tm, tn), lambda g0, g1, kk: ij(g0, g1)),
        scratch_shapes=[pltpu.VMEM((tm, tn), F32)] if nk > 1 else [],
        compiler_params=_params(dimension_semantics=("parallel", "parallel", "arbitrary")),
    )(a, b, *deps)


def _mm_rows(prods, rows_in, vecs_in, epilogue, outs, *, tm, name, dep=None):
    m = prods[0][0].shape[0]
    n = prods[0][1].shape[0] if prods[0][2] else prods[0][1].shape[1]
    tm = min(tm, m)
    assert m % tm == 0
    deps = [] if dep is None else [dep]
    n_p, n_r, n_v = len(prods), len(rows_in), len(vecs_in)

    def body(*refs):
        ab = refs[:2 * n_p]
        row_refs = refs[2 * n_p:2 * n_p + n_r]
        vec_refs = refs[2 * n_p + n_r:2 * n_p + n_r + n_v]
        out_refs = refs[2 * n_p + n_r + n_v + len(deps):]
        p = None
        for j, (_, _, tb) in enumerate(prods):
            t = _dot(ab[2 * j][...].astype(BF16), ab[2 * j + 1][...], 1, 1 if tb else 0)
            p = t if p is None else p + t
        vals = epilogue(p, *[r[...] for r in row_refs], *[v[...] for v in vec_refs])
        for (dtype, kind), o_ref, val in zip(outs, out_refs, vals):
            if kind == "row":
                o_ref[...] = val.astype(dtype)
            else:
                @pl.when(pl.program_id(0) == 0)
                def _(o_ref=o_ref):
                    o_ref[...] = jnp.zeros_like(o_ref)

                o_ref[...] += val

    row = lambda w: pl.BlockSpec((tm, w), lambda i: (i, 0))
    whole = lambda a: pl.BlockSpec(a.shape, lambda i: (0,) * a.ndim, pipeline_mode=pl.Buffered(1))
    in_specs, args = [], []
    for a, b, _ in prods:
        in_specs += [row(a.shape[1]), whole(b)]
        args += [a, b]
    in_specs += [row(r.shape[1]) for r in rows_in] + [whole(v) for v in vecs_in] + [ANY_SPEC] * len(deps)
    return pl.pallas_call(
        body, name=name,
        out_shape=tuple(jax.ShapeDtypeStruct((m, n) if kind == "row" else (8, n), dtype) for dtype, kind in outs),
        grid=(m // tm,), in_specs=in_specs,
        out_specs=tuple(row(n) if kind == "row" else pl.BlockSpec((8, n), lambda i: (0, 0)) for _, kind in outs),
        compiler_params=_params(dimension_semantics=("arbitrary",)),
    )(*args, *rows_in, *vecs_in, *deps)


def _rstd(x):
    return lax.rsqrt(jnp.mean(x * x, axis=-1, keepdims=True) + EPS)


def _norm_bwd(xh, r, t):
    return r * (t - xh * jnp.mean(xh * t, axis=-1, keepdims=True))


ROW_F32, ROW_BF16, SUM_F32 = (F32, "row"), (BF16, "row"), (F32, "sum")


def _ep_post_pre(p, h, g_post, g_pre):
    y = p.astype(BF16)
    yf = y.astype(F32)
    hn = h + yf * _rstd(yf) * g_post
    return y, hn, hn * _rstd(hn) * g_pre


_EP_POST_PRE_OUTS = [ROW_BF16, ROW_F32, ROW_BF16]


def _ep_final_loss(y, h, target, g_post):
    r = _rstd(y)
    yh = y * r
    err = h + yh * g_post - target
    dh = err * (1.0 / D)
    return _rowsum8(err * err), dh, _norm_bwd(yh, r, dh * g_post), _rowsum8(dh * yh)


def _ep_post_pre_bwd(du, dh_out, hn, y, g_post, g_pre):
    r2 = _rstd(hn)
    xh = hn * r2
    dh = dh_out + _norm_bwd(xh, r2, du * g_pre)
    yf = y.astype(F32)
    r1 = _rstd(yf)
    yh = yf * r1
    return dh, _norm_bwd(yh, r1, dh * g_post), _rowsum8(du * xh), _rowsum8(dh * yh)


_EP_POST_PRE_BWD_OUTS = [ROW_F32, ROW_BF16, SUM_F32, SUM_F32]


def _ep_pre_bwd(du, dh_out, x, g):
    r = _rstd(x)
    xh = x * r
    return dh_out + _norm_bwd(xh, r, du * g), _rowsum8(du * xh)


_EP_PRE_BWD_OUTS = [ROW_F32, SUM_F32]


def _prenorm(x, g, *, name, dep=None):
    t, d = x.shape
    tb = min(512, t)
    deps = [] if dep is None else [dep]

    def body(x_ref, g_ref, *rest):
        xf = x_ref[...]
        rest[-1][...] = (xf * _rstd(xf) * g_ref[...]).astype(BF16)

    return pl.pallas_call(
        body, name=name, out_shape=jax.ShapeDtypeStruct((t, d), BF16), grid=(t // tb,),
        in_specs=[pl.BlockSpec((tb, d), lambda i: (i, 0)), pl.BlockSpec((1, d), lambda i: (0, 0))]
        + [ANY_SPEC] * len(deps),
        out_specs=pl.BlockSpec((tb, d), lambda i: (i, 0)), compiler_params=_params(),
    )(x, g, *deps)


QB = 256


def _half_mask(shape, e):
    lane = lax.broadcasted_iota(jnp.int32, shape, len(shape) - 1)
    return (lane // 64) == e


def _place(kv):
    sw = pltpu.roll(kv, 64, 1)
    m0 = _half_mask(kv.shape, 0)
    return [[jnp.where(m0, kv, 0.0).astype(BF16), jnp.where(m0, 0.0, sw).astype(BF16)],
            [jnp.where(m0, sw, 0.0).astype(BF16), jnp.where(m0, 0.0, kv).astype(BF16)]]


SQ = 128
SK = 256


def _swa_valid(i, sb):
    qc = lax.broadcasted_iota(jnp.int32, (SQ, SK), 0) // CHUNK
    kc = lax.broadcasted_iota(jnp.int32, (SQ, SK), 1) // CHUNK - 2
    return (kc <= qc) & (qc <= kc + 2) & (4 * i + 2 * sb + kc >= 0)


def _swa_fwd(z, sinks, t, dep=None):
    nb = t // QB
    deps = [] if dep is None else [dep]

    def body(s_ref, q_ref, kp_ref, kc_ref, vp_ref, vc_ref, *rest):
        o_ref, lse_ref = rest[-2:]
        i = pl.program_id(0)
        kpl = _place(jnp.concatenate([kp_ref[...], kc_ref[...]], axis=0))
        vpl = _place(jnp.concatenate([vp_ref[...], vc_ref[...]], axis=0))
        lane = lax.broadcasted_iota(jnp.int32, (SQ, 128), 1)
        for sb in range(QB // SQ):
            rows, keys = slice(SQ * sb, SQ * (sb + 1)), slice(SQ * sb, SQ * sb + SK)
            valid = _swa_valid(i, sb)
            lse_out = jnp.zeros((SQ, 128), F32)
            for j in range(4):
                qp = q_ref[rows, 128 * j:128 * (j + 1)].astype(BF16)
                acc = jnp.zeros((SQ, 128), F32)
                for e in range(2):
                    h = 2 * j + e
                    kvh = h // 4
                    qm = jnp.where(_half_mask(qp.shape, e), qp, jnp.zeros_like(qp))
                    s = _dot(qm, kpl[kvh][e][keys], 1, 1) * 0.125
                    s = jnp.where(valid, s, NEG)
                    sink = s_ref[0, h]
                    m = jnp.maximum(jnp.max(s, axis=-1, keepdims=True), sink)
                    p = jnp.exp(s - m)
                    l = jnp.sum(p, axis=-1, keepdims=True) + jnp.exp(sink - m)
                    acc = acc + _dot(p.astype(BF16), vpl[kvh][e][keys], 1, 0) * (1.0 / l)
                    lse_out = jnp.where(lane == h, m + jnp.log(l), lse_out)
                o_ref[rows, 128 * j:128 * (j + 1)] = acc.astype(BF16)
            lse_ref[rows, :] = lse_out

    prev = lambda c: pl.BlockSpec((128, 128), lambda i: (jnp.maximum(2 * i - 1, 0), c))
    cur = lambda c: pl.BlockSpec((QB, 128), lambda i: (i, c))
    return pl.pallas_call(
        body, name="swa_fwd",
        out_shape=(jax.ShapeDtypeStruct((t, D), BF16), jax.ShapeDtypeStruct((t, 128), F32)),
        grid=(nb,),
        in_specs=[pl.BlockSpec(memory_space=pltpu.SMEM),
                  pl.BlockSpec((QB, SWA_W), lambda i: (i, 0)), prev(4), cur(4), prev(5), cur(5)]
        + [ANY_SPEC] * len(deps),
        out_specs=(pl.BlockSpec((QB, SWA_W), lambda i: (i, 0)), pl.BlockSpec((QB, 128), lambda i: (i, 0))),
        compiler_params=_params(),
    )(sinks, z, z, z, z, z, *deps)


def _swa_bwd(z, sinks, ymix, lse, dymix, t):
    nb = t // QB

    def body(s_ref, q_ref, kp_ref, kc_ref, vp_ref, vc_ref, o_ref, do_ref, l_ref,
             dq_ref, first_ref, second_ref, ds_ref, carry_ref):
        i = pl.program_id(0)
        live = i < nb

        @pl.when(i == 0)
        def _():
            ds_ref[...] = jnp.zeros_like(ds_ref)
            carry_ref[...] = jnp.zeros_like(carry_ref)

        lane = lax.broadcasted_iota(jnp.int32, (8, 128), 1)
        kpl = _place(jnp.concatenate([kp_ref[...], kc_ref[...]], axis=0))
        vpl = _place(jnp.concatenate([vp_ref[...], vc_ref[...]], axis=0))
        nk = QB + 128
        qc = lax.broadcasted_iota(jnp.int32, (QB, nk), 0) // CHUNK
        kc = lax.broadcasted_iota(jnp.int32, (QB, nk), 1) // CHUNK - 2
        valid = (kc <= qc) & (qc <= kc + 2) & (4 * i + kc >= 0) & live
        lse_c = l_ref[...]
        dsink = jnp.zeros((8, 128), F32)
        dk_acc = [[jnp.zeros((128, nk), F32) for _ in range(2)] for _ in range(2)]
        dv_acc = [[jnp.zeros((128, nk), F32) for _ in range(2)] for _ in range(2)]
        dq = []
        for j in range(4):
            cols = slice(128 * j, 128 * (j + 1))
            qp = q_ref[:, cols].astype(BF16)
            dop = do_ref[:, cols]
            prod = dop.astype(F32) * o_ref[:, cols].astype(F32)
            acc = jnp.zeros((QB, 128), F32)
            for e in range(2):
                h = 2 * j + e
                kvh = h // 4
                hm = _half_mask(qp.shape, e)
                qm = jnp.where(hm, qp, jnp.zeros_like(qp))
                dom = jnp.where(hm, dop, jnp.zeros_like(dop))
                dd = jnp.sum(jnp.where(hm, prod, 0.0), axis=-1, keepdims=True)
                lse_h = lse_c[:, h:h + 1]
                s = _dot(qm, kpl[kvh][e], 1, 1) * 0.125
                p = jnp.where(valid, jnp.exp(s - lse_h), 0.0)
                dp = _dot(dom, vpl[kvh][e], 1, 1)
                ds = (p * (dp - dd) * 0.125).astype(BF16)
                acc = acc + _dot(ds, kpl[kvh][e], 1, 0)
                dk_acc[kvh][e] = dk_acc[kvh][e] + _dot(qm, ds, 0, 0)
                dv_acc[kvh][e] = dv_acc[kvh][e] + _dot(dom, p.astype(BF16), 0, 0)
                ps = jnp.where(live, jnp.exp(s_ref[0, h] - lse_h) * dd, 0.0)
                dsink = dsink - jnp.where(lane == h, _rowsum8(jnp.broadcast_to(ps, (QB, 128))), 0.0)
            dq.append(acc.astype(BF16))
        ds_ref[...] += dsink
        dk = (dk_acc[0][0] + dk_acc[1][1] + pltpu.roll(dk_acc[0][1] + dk_acc[1][0], 64, 0)).T
        dv = (dv_acc[0][0] + dv_acc[1][1] + pltpu.roll(dv_acc[0][1] + dv_acc[1][0], 64, 0)).T
        dkv = jnp.concatenate([dk, dv], axis=1)
        second_ref[...] = (carry_ref[...] + dkv[0:128]).astype(BF16)
        carry_ref[...] = dkv[256:384]

        @pl.when(live)
        def _():
            for j in range(4):
                dq_ref[:, 128 * j:128 * (j + 1)] = dq[j]
            first_ref[...] = dkv[128:256].astype(BF16)

    blk = lambda i: jnp.minimum(i, nb - 1)
    prev = lambda c: pl.BlockSpec((128, 128), lambda i: (jnp.maximum(2 * blk(i) - 1, 0), c))
    cur = lambda w, c: pl.BlockSpec((QB, w), lambda i: (blk(i), c))
    half = lambda index: pl.BlockSpec((128, 256), lambda i: (index(i), 0))
    return pl.pallas_call(
        body, name="swa_bwd",
        out_shape=(jax.ShapeDtypeStruct((t, SWA_W), BF16), jax.ShapeDtypeStruct((t // 2, 256), BF16),
                   jax.ShapeDtypeStruct((t // 2, 256), BF16), jax.ShapeDtypeStruct((8, 128), F32)),
        grid=(nb + 1,),
        in_specs=[pl.BlockSpec(memory_space=pltpu.SMEM),
                  cur(SWA_W, 0), prev(4), cur(128, 4), prev(5), cur(128, 5),
                  cur(SWA_W, 0), cur(SWA_W, 0), cur(128, 0)],
        out_specs=(cur(SWA_W, 0), half(blk), half(lambda i: jnp.maximum(i - 1, 0)),
                   pl.BlockSpec((8, 128), lambda i: (0, 0))),
        scratch_shapes=[pltpu.VMEM((128, 256), F32)],
        compiler_params=_params(dimension_semantics=("arbitrary",)),
    )(sinks, z, z, z, z, z, ymix, dymix, lse)


HB = 256


def _lower_bound(lb_ref):
    a = lb_ref[...]
    a0, a1 = a[0:1], a[1:2]
    mx = jnp.maximum(a0, a1)
    e0, e1 = jnp.exp(a0 - mx), jnp.exp(a1 - mx)
    return e0 / (e0 + e1)


def _hgrn_cols(row_block):
    return [pl.BlockSpec((HB, 2 * HD), lambda j, c=base // (2 * HD) + p: (row_block(j), c))
            for base in (ZQH, ZFH, ZIH, ZGH) for p in range(2)]


NCH = HB // CHUNK


def _split3(x):
    hi = x.astype(BF16)
    r1 = x - hi.astype(F32)
    mid = r1.astype(BF16)
    return hi, mid, (r1 - mid.astype(F32)).astype(BF16)


def _blockdiag(lower):
    r = lax.broadcasted_iota(jnp.int32, (HB, HB), 0)
    c = lax.broadcasted_iota(jnp.int32, (HB, HB), 1)
    return (r // CHUNK == c // CHUNK) & ((c <= r) if lower else (c >= r))


def _chunk_sums(mask_bf16, x):
    return sum(_dot(mask_bf16, part, 1, 0) for part in _split3(x))


def _per_chunk_rows(x, row):
    w = x.shape[1]
    picked = x.reshape(NCH, CHUNK, w)[:, row:row + 1, :]
    return jnp.broadcast_to(picked, (NCH, CHUNK, w)).reshape(HB, w)


def _chunk_stack(x, chunk_of_row):
    return jnp.concatenate([jnp.where(chunk_of_row == c, x, jnp.zeros_like(x)) for c in range(NCH)], axis=1)


def _chunk_pick(x, chunk_of_row):
    w = x.shape[1] // NCH
    out = jnp.zeros((HB, w), x.dtype)
    for c in range(NCH):
        out = jnp.where(chunk_of_row == c, x[:, c * w:(c + 1) * w], out)
    return out


def _hgrn_local(q, f, kf, b):
    sq = _sig(q)
    qf = q * sq * (HD ** -0.5)
    b_mid = _per_chunk_rows(b, CHUNK // 2 - 1)
    b_last = _per_chunk_rows(b, CHUNK - 1)
    qm = qf * jnp.exp(b - b_mid)
    km = kf * jnp.exp(b_mid - b)
    kl = kf * jnp.exp(b_last - b)
    qb = qf * jnp.exp(b)
    return dict(sq=sq, b_mid=b_mid, b_last=b_last, qm=qm, km=km, kl=kl, qb=qb)


def _hgrn2_fwd(z, hgrn_lb, onorm, ymix, t, dep=None):
    nb = t // HB
    deps = [] if dep is None else [dep]

    def body(*refs):
        zq, zf, zi, zg = refs[0:2], refs[2:4], refs[4:6], refs[6:8]
        (lb_ref, on_ref), (y_ref, o_ref, sp_ref, st_ref) = refs[8:10], refs[-4:]

        @pl.when(pl.program_id(0) == 0)
        def _():
            st_ref[...] = jnp.zeros_like(st_ref)

        lb_all = _lower_bound(lb_ref)
        gn = on_ref[...]
        low = _blockdiag(True)
        low_b = low.astype(BF16)
        chunk_of_row = lax.broadcasted_iota(jnp.int32, (HB, HD), 0) // CHUNK
        for p in range(2):
            lbp = lb_all[:, 2 * HD * p:2 * HD * (p + 1)]
            fp = lbp + (1.0 - lbp) * _sig(zf[p][...])
            bp = _chunk_sums(low_b, jnp.log(fp))
            for e in range(2):
                h, ls = 2 * p + e, slice(e * HD, (e + 1) * HD)
                f = fp[:, ls]
                w = _hgrn_local(zq[p][:, ls], f, 1.0 - f, bp[:, ls])
                iv = zi[p][:, ls].astype(BF16)
                a = jnp.where(low, _dot(w["qm"].astype(BF16), w["km"].astype(BF16), 1, 1), 0.0)
                o = _dot(a.astype(BF16), iv, 1, 0)
                u = _dot(iv, _chunk_stack(w["kl"].astype(BF16), chunk_of_row), 0, 0)
                decay = jnp.exp(w["b_last"])
                st = st_ref[h]
                states = []
                for c in range(NCH):
                    sp_ref[h, c] = st
                    states.append(st.astype(BF16))
                    st = st * decay[c * CHUNK:c * CHUNK + 1] + u[:, c * HD:(c + 1) * HD]
                st_ref[h] = st
                inter = _dot(w["qb"].astype(BF16), jnp.concatenate(states, axis=0), 1, 1)
                o = o + _chunk_pick(inter, chunk_of_row)
                hs = slice(h * HD, (h + 1) * HD)
                o_ref[:, hs] = o
                gg = zg[p][:, ls]
                y_ref[:, hs] = (o * _rstd(o) * gn * (gg * _sig(gg))).astype(BF16)

    return pl.pallas_call(
        body, name="hgrn_fwd",
        out_shape=(jax.ShapeDtypeStruct((t, D), BF16), jax.ShapeDtypeStruct((t, HG_W), F32),
                   jax.ShapeDtypeStruct((4, t // CHUNK, HD, HD), F32)),
        grid=(nb,),
        in_specs=_hgrn_cols(lambda j: j) + [pl.BlockSpec((2, HG_W), lambda j: (0, 0)),
                                            pl.BlockSpec((1, HD), lambda j: (0, 0)), ANY_SPEC]
        + [ANY_SPEC] * len(deps),
        out_specs=(pl.BlockSpec((HB, HG_W), lambda j: (j, 1)),
                   pl.BlockSpec((HB, HG_W), lambda j: (j, 0)),
                   pl.BlockSpec((4, NCH, HD, HD), lambda j: (0, j, 0, 0))),
        scratch_shapes=[pltpu.VMEM((4, HD, HD), F32)],
        input_output_aliases={10: 0},
        compiler_params=_params(dimension_semantics=("arbitrary",)),
    )(*[z] * 8, hgrn_lb, onorm, ymix, *deps)


def _hgrn2_bwd(z, hgrn_lb, onorm, o_save, sprev, dymix, dza, t):
    nb = t // HB

    def body(*refs):
        zq, zf, zi, zg = refs[0:2], refs[2:4], refs[4:6], refs[6:8]
        (lb_ref, on_ref, o_ref, sp_ref, dy_ref, dqa_ref, first_ref, second_ref,
         dz_ref, dlb_ref, don_ref, dst_ref) = refs[8:]

        @pl.when(pl.program_id(0) == 0)
        def _():
            dst_ref[...] = jnp.zeros_like(dst_ref)
            dlb_ref[...] = jnp.zeros_like(dlb_ref)
            don_ref[...] = jnp.zeros_like(don_ref)

        dz_ref[:, 0:SWA_W] = dqa_ref[...]
        dz_ref[0:HB // 2, SWA_W:ZQH] = first_ref[...]
        dz_ref[HB // 2:HB, SWA_W:ZQH] = second_ref[...]
        lb_all = _lower_bound(lb_ref)
        gn = on_ref[...]
        low, upp = _blockdiag(True), _blockdiag(False)
        upp_b = upp.astype(BF16)
        low_b = low.astype(BF16)
        row = lax.broadcasted_iota(jnp.int32, (HB, HD), 0)
        chunk_of_row = row // CHUNK
        in_chunk = row % CHUNK
        for p in range(2):
            lbp = lb_all[:, 2 * HD * p:2 * HD * (p + 1)]
            sgp = _sig(zf[p][...])
            fp = lbp + (1.0 - lbp) * sgp
            bp = _chunk_sums(low_b, jnp.log(fp))
            db_pair, dkf_pair = [], []
            for e in range(2):
                h, ls, hs = 2 * p + e, slice(e * HD, (e + 1) * HD), slice((2 * p + e) * HD, (2 * p + e + 1) * HD)
                f = fp[:, ls]
                q = zq[p][:, ls]
                w = _hgrn_local(q, f, 1.0 - f, bp[:, ls])
                iv = zi[p][:, ls].astype(BF16)
                gg = zg[p][:, ls]
                o = o_ref[:, hs]
                dout = dy_ref[:, hs].astype(F32)
                sgg = _sig(gg)
                r = _rstd(o)
                oh = o * r
                dyn = dout * (gg * sgg)
                dz_ref[:, ZGH + h * HD:ZGH + (h + 1) * HD] = (
                    dout * oh * gn * (sgg * (1.0 + gg * (1.0 - sgg)))).astype(BF16)
                don_ref[...] += _rowsum8(dyn * oh)
                do = _norm_bwd(oh, r, dyn * gn).astype(BF16)
                qm, km, kl, qb = (w[n].astype(BF16) for n in ("qm", "km", "kl", "qb"))
                decay = jnp.exp(w["b_last"])
                grads_in = _dot(do, _chunk_stack(qb, chunk_of_row), 0, 0)
                dst = dst_ref[h]
                dstn, dd_rows = [None] * NCH, [None] * NCH
                for c in reversed(range(NCH)):
                    dstn[c] = dst.astype(BF16)
                    dd_rows[c] = jnp.sum(dst * sp_ref[h, c], axis=0, keepdims=True)
                    dst = dst * decay[c * CHUNK:c * CHUNK + 1] + grads_in[:, c * HD:(c + 1) * HD]
                dst_ref[h] = dst
                states = jnp.concatenate([sp_ref[h, c].astype(BF16) for c in range(NCH)], axis=0)
                dstn_all = jnp.concatenate(dstn, axis=0)
                dqb = _dot(_chunk_stack(do, chunk_of_row), states, 1, 0)
                at = jnp.where(upp, _dot(km, qm, 1, 1), 0.0)
                di = _dot(at.astype(BF16), do, 1, 0) + _chunk_pick(_dot(kl, dstn_all, 1, 1), chunk_of_row)
                dz_ref[:, ZIH + h * HD:ZIH + (h + 1) * HD] = di.astype(BF16)
                dkl = _dot(_chunk_stack(iv, chunk_of_row), dstn_all, 1, 0)
                da = jnp.where(low, _dot(do, iv, 1, 1), 0.0).astype(BF16)
                dat = jnp.where(upp, _dot(iv, do, 1, 1), 0.0).astype(BF16)
                dqm = _dot(da, km, 1, 0)
                dkm = _dot(dat, qm, 1, 0)
                b = bp[:, ls]
                e1, e2 = jnp.exp(b - w["b_mid"]), jnp.exp(w["b_mid"] - b)
                e3, e4 = jnp.exp(w["b_last"] - b), jnp.exp(b)
                dqf = dqm * e1 + dqb * e4
                dkf_pair.append(dkm * e2 + dkl * e3)
                t_qm, t_km, t_kl = dqm * w["qm"], dkm * w["km"], dkl * w["kl"]
                db = t_qm - t_km - t_kl + dqb * w["qb"]
                db_mid = jnp.sum((t_km - t_qm).reshape(NCH, CHUNK, HD), axis=1, keepdims=True)
                db_last = jnp.sum(t_kl.reshape(NCH, CHUNK, HD), axis=1, keepdims=True)
                db_last = db_last + jnp.stack(dd_rows, axis=0) * jnp.exp(
                    bp[:, ls].reshape(NCH, CHUNK, HD)[:, CHUNK - 1:CHUNK, :])
                spread = lambda v: jnp.broadcast_to(v, (NCH, CHUNK, HD)).reshape(HB, HD)
                db = (db + jnp.where(in_chunk == CHUNK // 2 - 1, spread(db_mid), 0.0)
                      + jnp.where(in_chunk == CHUNK - 1, spread(db_last), 0.0))
                db_pair.append(db)
                sq = w["sq"]
                dz_ref[:, ZQH + h * HD:ZQH + (h + 1) * HD] = (
                    dqf * (HD ** -0.5) * (sq * (1.0 + q * (1.0 - sq)))).astype(BF16)
            dlogf = _chunk_sums(upp_b, jnp.concatenate(db_pair, axis=1))
            dfv = dlogf / fp - jnp.concatenate(dkf_pair, axis=1)
            dz_ref[:, ZFH + 2 * HD * p:ZFH + 2 * HD * (p + 1)] = (dfv * (1.0 - lbp) * sgp * (1.0 - sgp)).astype(BF16)
            dlb_ref[:, 2 * HD * p:2 * HD * (p + 1)] += _rowsum8(dfv * (1.0 - sgp))

    rev = lambda j: nb - 1 - j
    return pl.pallas_call(
        body, name="hgrn_bwd",
        out_shape=(jax.ShapeDtypeStruct((t, D_IN), BF16), jax.ShapeDtypeStruct((8, HG_W), F32),
                   jax.ShapeDtypeStruct((8, HD), F32)),
        grid=(nb,),
        in_specs=_hgrn_cols(rev) + [pl.BlockSpec((2, HG_W), lambda j: (0, 0)), pl.BlockSpec((1, HD), lambda j: (0, 0)),
                                    pl.BlockSpec((HB, HG_W), lambda j: (rev(j), 0)),
                                    pl.BlockSpec((4, NCH, HD, HD), lambda j: (0, rev(j), 0, 0)),
                                    pl.BlockSpec((HB, HG_W), lambda j: (rev(j), 1)),
                                    pl.BlockSpec((HB, SWA_W), lambda j: (rev(j), 0)),
                                    pl.BlockSpec((HB // 2, 2 * KV_W), lambda j: (rev(j), 0)),
                                    pl.BlockSpec((HB // 2, 2 * KV_W), lambda j: (rev(j), 0))],
        out_specs=(pl.BlockSpec((HB, D_IN), lambda j: (rev(j), 0)), pl.BlockSpec((8, HG_W), lambda j: (0, 0)),
                   pl.BlockSpec((8, HD), lambda j: (0, 0))),
        scratch_shapes=[pltpu.VMEM((4, HD, HD), F32)],
        compiler_params=_params(dimension_semantics=("arbitrary",)),
    )(*[z] * 8, hgrn_lb, onorm, o_save, sprev, dymix, *dza)


XB = 512


def _xattn_fwd(q, k, v, t):
    tb = min(XB, t)

    def body(q_ref, k_ref, v_ref, o_ref):
        for h in range(XH):
            cols = slice(XD * h, XD * (h + 1))
            s = _dot(q_ref[:, cols], k_ref[:, cols], 1, 1) * (XD ** -0.5)
            p = jnp.exp(s - jnp.max(s, axis=-1, keepdims=True))
            l = jnp.sum(p, axis=-1, keepdims=True)
            o_ref[:, cols] = (_dot(p.astype(BF16), v_ref[:, cols], 1, 0) * (1.0 / l)).astype(BF16)

    row = pl.BlockSpec((tb, D), lambda i: (i, 0))
    mem = pl.BlockSpec(k.shape, lambda i: (0, 0))
    return pl.pallas_call(
        body, name="xattn_fwd", out_shape=jax.ShapeDtypeStruct((t, D), BF16), grid=(t // tb,),
        in_specs=[row, mem, mem], out_specs=row, compiler_params=_params(),
    )(q, k, v)


def _xattn_bwd(q, k, v, do, t):
    tb = min(XB, t)

    def body(q_ref, k_ref, v_ref, do_ref, dq_ref, dk_ref, dv_ref):
        @pl.when(pl.program_id(0) == 0)
        def _():
            dk_ref[...] = jnp.zeros_like(dk_ref)
            dv_ref[...] = jnp.zeros_like(dv_ref)

        for h in range(XH):
            cols = slice(XD * h, XD * (h + 1))
            qh, kh, vh, doh = q_ref[:, cols], k_ref[:, cols], v_ref[:, cols], do_ref[:, cols]
            s = _dot(qh, kh, 1, 1) * (XD ** -0.5)
            p = jnp.exp(s - jnp.max(s, axis=-1, keepdims=True))
            p = p * (1.0 / jnp.sum(p, axis=-1, keepdims=True))
            dp = _dot(doh, vh, 1, 1)
            ds = (p * (dp - jnp.sum(p * dp, axis=-1, keepdims=True)) * (XD ** -0.5)).astype(BF16)
            dq_ref[:, cols] = _dot(ds, kh, 1, 0).astype(BF16)
            dk_ref[:, cols] += _dot(ds, qh, 0, 0)
            dv_ref[:, cols] += _dot(p.astype(BF16), doh, 0, 0)

    row = pl.BlockSpec((tb, D), lambda i: (i, 0))
    mem = pl.BlockSpec(k.shape, lambda i: (0, 0))
    return pl.pallas_call(
        body, name="xattn_bwd",
        out_shape=(jax.ShapeDtypeStruct((t, D), BF16), jax.ShapeDtypeStruct(k.shape, F32),
                   jax.ShapeDtypeStruct(k.shape, F32)),
        grid=(t // tb,), in_specs=[row, mem, mem, row], out_specs=(row, mem, mem),
        compiler_params=_params(dimension_semantics=("arbitrary",)),
    )(q, k, v, do)


def _mem_kv(mem, g_mem, wk, wv):
    def body(m_ref, g_ref, wk_ref, wv_ref, mn_ref, k_ref, v_ref):
        m_ = m_ref[...]
        mn = (m_ * _rstd(m_) * g_ref[...]).astype(BF16)
        mn_ref[...] = mn
        k_ref[...] = _dot(mn, wk_ref[...], 1, 0).astype(BF16)
        v_ref[...] = _dot(mn, wv_ref[...], 1, 0).astype(BF16)

    return pl.pallas_call(body, name="mem_kv", out_shape=(jax.ShapeDtypeStruct(mem.shape, BF16),) * 3,
                          compiler_params=_params())(mem, g_mem, wk, wv)


def _mem_kv_bwd(mn, mem, dk, dv, wk, wv, dep=None):
    deps = [] if dep is None else [dep]

    def body(mn_ref, m_ref, dk_ref, dv_ref, wk_ref, wv_ref, *rest):
        gk_ref, gv_ref, dg_ref = rest[len(deps):]
        mn = mn_ref[...]
        dkb, dvb = dk_ref[...].astype(BF16), dv_ref[...].astype(BF16)
        gk_ref[...] = _dot(mn, dkb, 0, 0).astype(BF16)
        gv_ref[...] = _dot(mn, dvb, 0, 0).astype(BF16)
        dmn = _dot(dkb, wk_ref[...], 1, 1) + _dot(dvb, wv_ref[...], 1, 1)
        m_ = m_ref[...]
        dg_ref[...] = _rowsum8(dmn * (m_ * _rstd(m_)))

    vmem = pl.BlockSpec(memory_space=pltpu.VMEM)
    return pl.pallas_call(
        body, name="mem_kv_bwd",
        out_shape=(jax.ShapeDtypeStruct(wk.shape, BF16), jax.ShapeDtypeStruct(wv.shape, BF16),
                   jax.ShapeDtypeStruct((8, D), F32)),
        in_specs=[vmem] * 6 + [ANY_SPEC] * len(deps), out_specs=(vmem,) * 3, compiler_params=_params(),
    )(mn, mem, dk, dv, wk, wv, *deps)


FB = 256


def _ffn_fwd_loss(u, wgt, wut, wd, h, target, g_post, t):
    tb = min(FB, t)

    def body(u_ref, wg_ref, wu_ref, wd_ref, h_ref, t_ref, gp_ref, g_ref, up_ref, a_ref, sq_ref, dh_ref, dy_ref, dg_ref):
        @pl.when(pl.program_id(0) == 0)
        def _():
            sq_ref[...] = jnp.zeros_like(sq_ref)
            dg_ref[...] = jnp.zeros_like(dg_ref)

        u_ = u_ref[...]
        g = _dot(u_, wg_ref[...], 1, 1)
        up = _dot(u_, wu_ref[...], 1, 1)
        a = (g * _sig(g) * up).astype(BF16)
        g_ref[...] = g.astype(BF16)
        up_ref[...] = up.astype(BF16)
        a_ref[...] = a
        sq, dh, dy, dg = _ep_final_loss(_dot(a, wd_ref[...], 1, 0), h_ref[...], t_ref[...], gp_ref[...])
        sq_ref[...] += sq
        dh_ref[...] = dh
        dy_ref[...] = dy.astype(BF16)
        dg_ref[...] += dg

    row = lambda w: pl.BlockSpec((tb, w), lambda i: (i, 0))
    whole = lambda a: pl.BlockSpec(a.shape, lambda i: (0,) * a.ndim, pipeline_mode=pl.Buffered(1))
    acc = pl.BlockSpec((8, D), lambda i: (0, 0))
    wide = jax.ShapeDtypeStruct((t, D_FF), BF16)
    return pl.pallas_call(
        body, name="ffn_fwd_loss",
        out_shape=(wide, wide, wide, jax.ShapeDtypeStruct((8, D), F32), jax.ShapeDtypeStruct((t, D), F32),
                   jax.ShapeDtypeStruct((t, D), BF16), jax.ShapeDtypeStruct((8, D), F32)),
        grid=(t // tb,),
        in_specs=[row(D), whole(wgt), whole(wut), whole(wd), row(D), row(D), whole(g_post)],
        out_specs=(row(D_FF), row(D_FF), row(D_FF), acc, row(D), row(D), acc),
        compiler_params=_params(dimension_semantics=("arbitrary",)),
    )(u, wgt, wut, wd, h, target, g_post)


def _ffn_bwd(dy, wd, gate, up, wgt, wut, dh_out, hn, y, g_post, g_pre, t, dep=None):
    tb = min(FB, t)
    deps = [] if dep is None else [dep]

    def body(dy_ref, wd_ref, g_ref, up_ref, wg_ref, wu_ref, dho_ref, hn_ref, y_ref, gp_ref, gn_ref, *rest):
        dg_ref, dup_ref, dh_ref, dyp_ref, dgn_ref, dgp_ref = rest[len(deps):]

        @pl.when(pl.program_id(0) == 0)
        def _():
            dgn_ref[...] = jnp.zeros_like(dgn_ref)
            dgp_ref[...] = jnp.zeros_like(dgp_ref)

        da = _dot(dy_ref[...], wd_ref[...], 1, 1)
        g = g_ref[...].astype(F32)
        sg = _sig(g)
        dup = (da * g * sg).astype(BF16)
        dgate = (da * up_ref[...].astype(F32) * (sg * (1.0 + g * (1.0 - sg)))).astype(BF16)
        dup_ref[...] = dup
        dg_ref[...] = dgate
        du = _dot(dgate, wg_ref[...], 1, 0) + _dot(dup, wu_ref[...], 1, 0)
        dh, dyp, dgn, dgp = _ep_post_pre_bwd(du, dho_ref[...], hn_ref[...], y_ref[...], gp_ref[...], gn_ref[...])
        dh_ref[...] = dh
        dyp_ref[...] = dyp.astype(BF16)
        dgn_ref[...] += dgn
        dgp_ref[...] += dgp

    row = lambda w: pl.BlockSpec((tb, w), lambda i: (i, 0))
    whole = lambda a: pl.BlockSpec(a.shape, lambda i: (0,) * a.ndim, pipeline_mode=pl.Buffered(1))
    acc = pl.BlockSpec((8, D), lambda i: (0, 0))
    return pl.pallas_call(
        body, name="ffn_bwd",
        out_shape=(jax.ShapeDtypeStruct((t, D_FF), BF16), jax.ShapeDtypeStruct((t, D_FF), BF16),
                   jax.ShapeDtypeStruct((t, D), F32), jax.ShapeDtypeStruct((t, D), BF16),
                   jax.ShapeDtypeStruct((8, D), F32), jax.ShapeDtypeStruct((8, D), F32)),
        grid=(t // tb,),
        in_specs=[row(D), whole(wd), row(D_FF), row(D_FF), whole(wgt), whole(wut), row(D), row(D), row(D),
                  whole(g_post), whole(g_pre)] + [ANY_SPEC] * len(deps),
        out_specs=(row(D_FF), row(D_FF), row(D), row(D), acc, acc),
        compiler_params=_params(dimension_semantics=("arbitrary",)),
    )(dy, wd, gate, up, wgt, wut, dh_out, hn, y, g_post, g_pre, *deps)


def _local_step(x, mem, target, fetch, sm, emit=None, first_dep=None, milestone=None):
    t = x.shape[0]
    w, gw = {}, {}

    def out(key, g):
        gw[key] = g
        return None if emit is None else emit(key, g)

    def tell(tag, value):
        return None if milestone is None else milestone(tag, value)
    u1 = _prenorm(x, sm["g_mix_pre"], name="prenorm_mix", dep=first_dep)
    w["winT"] = fetch("winT", u1)
    z = _mm(u1, w["winT"], tb=True, out_dtype=F32, tm=1024, tn=1408, name="mm_z", n_outer=True)
    ymix, lse = _swa_fwd(z, sm["sinks"], t, dep=tell("z", z))
    ymix, o_h, sprev = _hgrn2_fwd(z, sm["hgrn_lb"], sm["hgrn_onorm"], ymix, t, dep=tell("swa", lse))
    w["wout"] = fetch("wout", ymix)
    y1, h1, u2 = _mm_rows([(ymix, w["wout"], False)], [x], [sm["g_mix_post"], sm["g_x_pre"]], _ep_post_pre,
                          _EP_POST_PRE_OUTS, tm=1024, name="mm_y1_post")
    for key in ("wq", "wk", "wv"):
        w[key] = fetch(key, u2)
    qx = _mm(u2, w["wq"], out_dtype=BF16, tm=1024, tn=1024, name="mm_qx")
    mn, kx, vx = _mem_kv(mem, sm["g_mem"], w["wk"], w["wv"])
    ox = _xattn_fwd(qx, kx, vx, t)
    w["wo"] = fetch("wo", ox)
    y2, h2, u3 = _mm_rows([(ox, w["wo"], False)], [h1], [sm["g_x_post"], sm["g_ffn_pre"]], _ep_post_pre,
                          _EP_POST_PRE_OUTS, tm=1024, name="mm_y2_post", dep=tell("ox", ox))
    for key in ("wgT", "wuT", "wd"):
        w[key] = fetch(key, u3)
    gate, up, act, sq, dh3, dy3, dg_ffn_post = _ffn_fwd_loss(u3, w["wgT"], w["wuT"], w["wd"], h2, target,
                                                             sm["g_ffn_post"], t)
    dep = out("wd", _mm(act, dy3, ta=True, out_dtype=BF16, tm=1408, tn=1024, name="mm_gwd"))
    dgate, dup, dh2, dy2, dg_ffn_pre, dg_x_post = _ffn_bwd(
        dy3, w["wd"], gate, up, w["wgT"], w["wuT"], dh3, h2, y2, sm["g_x_post"], sm["g_ffn_pre"], t, dep=dep)
    dep = out("wgT", _mm(dgate, u3, ta=True, out_dtype=BF16, tm=1408, tn=1024, name="mm_gwg"))
    dep = out("wuT", _mm(dup, u3, ta=True, out_dtype=BF16, tm=1408, tn=1024, name="mm_gwu", dep=dep))
    dep = out("wo", _mm(ox, dy2, ta=True, out_dtype=BF16, tm=512, tn=1024, name="mm_gwo", dep=dep))
    dox = _mm(dy2, w["wo"], tb=True, out_dtype=BF16, tm=1024, tn=1024, name="mm_dox", dep=dep)
    dqx, dkx, dvx = _xattn_bwd(qx, kx, vx, dox, t)
    dep = out("wq", _mm(u2, dqx, ta=True, out_dtype=BF16, tm=512, tn=1024, name="mm_gwq"))
    gwk, gwv, dg_mem = _mem_kv_bwd(mn, mem, dkx, dvx, w["wk"], w["wv"], dep=dep)
    out("wk", gwk)
    dep = out("wv", gwv)
    dh1, dy1, dg_x_pre, dg_mix_post = _mm_rows(
        [(dqx, w["wq"], True)], [dh2, h1, y1], [sm["g_mix_post"], sm["g_x_pre"]],
        _ep_post_pre_bwd, _EP_POST_PRE_BWD_OUTS, tm=512, name="mm_du2_post_bwd", dep=dep)
    dep = out("wout", _mm(ymix, dy1, ta=True, out_dtype=BF16, tm=512, tn=1024, name="mm_gwout"))
    dymix = _mm(dy1, w["wout"], tb=True, out_dtype=BF16, tm=1024, tn=1024, name="mm_dymix", dep=dep)
    *dza, dsinks = _swa_bwd(z, sm["sinks"], ymix, lse, dymix, t)
    dz, dlb, donorm = _hgrn2_bwd(z, sm["hgrn_lb"], sm["hgrn_onorm"], o_h, sprev, dymix, dza, t)
    dep = out("winT", _mm(dz, u1, ta=True, out_dtype=BF16, tm=1408, tn=1024, name="mm_gwin"))
    grad_x, dg_mix_pre = _mm_rows([(dz, w["winT"], False)], [dh1, x], [sm["g_mix_pre"]], _ep_pre_bwd,
                                  _EP_PRE_BWD_OUTS, tm=512, name="mm_du1_pre_bwd", dep=dep)
    parts = dict(g_mix_pre=dg_mix_pre, g_mix_post=dg_mix_post, g_mem=dg_mem, g_x_pre=dg_x_pre,
                 g_x_post=dg_x_post, g_ffn_pre=dg_ffn_pre, g_ffn_post=dg_ffn_post,
                 hgrn_onorm=donorm, hgrn_lb=dlb, sinks=dsinks, sq=sq)
    return grad_x, gw, parts


def _position():
    return lax.axis_index("x"), lax.axis_index("y"), lax.axis_index("c")


def _peer(pos, k):
    x, y, c = pos
    return (1 - x if k & 4 else x, 1 - y if k & 2 else y, 1 - c if k & 1 else c)


def _linear(pos):
    x, y, c = pos
    return 4 * x + 2 * y + c


HBM_SPEC = pl.BlockSpec(memory_space=pltpu.HBM)
SEM_SPEC = pl.BlockSpec(memory_space=pltpu.SEMAPHORE)
DATAFLOW = pltpu.SideEffectType.DATAFLOW_SIDE_EFFECTING
SEND_ORDER = (1, 2, 4, 3, 5, 6, 7)


def _in_hbm(a):
    return pltpu.with_memory_space_constraint(a, pltpu.HBM)


def _prepare_weights(shards, *, name, dep=None):
    n = len(shards)
    deps = [] if dep is None else [dep]

    def body(*refs):
        ins, (outs, lands, sem) = refs[:n], (refs[-2 * n - 1:-n - 1], refs[-n - 1:-1], refs[-1])
        me_lin = _linear(_position())
        copies = []
        for a in range(n):
            r = ins[a].shape[0]
            outs[a][...] = ins[a][...].astype(BF16)
            copies.append(pltpu.make_async_copy(outs[a], lands[a].at[pl.ds(me_lin * r, r), :], sem.at[a]))
            copies[-1].start()
        for cp in copies:
            cp.wait()

    vmem = pl.BlockSpec(memory_space=pltpu.VMEM)
    res = pl.pallas_call(
        body, name=name,
        out_shape=tuple(jax.ShapeDtypeStruct(s.shape, BF16) for s in shards)
        + tuple(jax.ShapeDtypeStruct((N_DEV * s.shape[0], s.shape[1]), BF16) for s in shards),
        in_specs=[vmem] * n + [ANY_SPEC] * len(deps), out_specs=tuple([vmem] * n + [ANY_SPEC] * n),
        scratch_shapes=[pltpu.SemaphoreType.DMA((n,))], compiler_params=_params(),
    )(*shards, *deps)
    return res[:n], res[n:]


def _copies_start(arrays, plan, n, *, name):
    na = len(arrays)

    def body(*refs):
        ins, send_sems, recv_sems = refs[:na], refs[na], refs[na + 1]
        me = _position()
        for j in range(n):
            src, dst, peer, _ = plan(ins, me, j)
            pltpu.make_async_remote_copy(src_ref=src, dst_ref=dst, send_sem=send_sems.at[j], recv_sem=recv_sems.at[j],
                                         device_id=peer, device_id_type=MESH).start()

    return pl.pallas_call(
        body, name=name,
        out_shape=(pltpu.SemaphoreType.DMA((n,)), pltpu.SemaphoreType.DMA((n,)))
        + tuple(pltpu.HBM(a.shape, a.dtype) for a in arrays),
        in_specs=(HBM_SPEC,) * na, out_specs=(SEM_SPEC, SEM_SPEC) + (HBM_SPEC,) * na,
        input_output_aliases={i: 2 + i for i in range(na)},
        compiler_params=pltpu.CompilerParams(has_side_effects=DATAFLOW),
    )(*[_in_hbm(a) for a in arrays])


def _copies_wait(send_sems, recv_sems, arrays, plan, n, after, *, name):
    na = len(arrays)

    def body(*refs):
        ins, send_sems, recv_sems = refs[:na], refs[na], refs[na + 1]
        me = _position()
        for j in range(n):
            src, _, peer, landed = plan(ins, me, j)
            copy = pltpu.make_async_remote_copy(src_ref=src, dst_ref=landed, send_sem=send_sems.at[j],
                                                recv_sem=recv_sems.at[j], device_id=peer, device_id_type=MESH)
            copy.wait_send()
            copy.wait_recv()

    return pl.pallas_call(
        body, name=name, out_shape=tuple(pltpu.HBM(a.shape, a.dtype) for a in arrays),
        in_specs=(HBM_SPEC,) * na + (SEM_SPEC, SEM_SPEC, ANY_SPEC), out_specs=(HBM_SPEC,) * na,
        input_output_aliases={i: i for i in range(na)},
        compiler_params=pltpu.CompilerParams(has_side_effects=DATAFLOW),
    )(*arrays, send_sems, recv_sems, after)


SAME_CORE = (2, 4, 6)


class _TwoLevelGather:
    def __init__(self, shards, lands, *, name):
        n = self.n = len(shards)
        self.name = name
        first_peers = (1,) + SAME_CORE

        def rows(ref, pos):
            r = ref.shape[0] // N_DEV
            return ref.at[pl.ds(_linear(pos) * r, r), :]

        def first(refs, me, j):
            a, peer = j // 4, _peer(me, first_peers[j % 4])
            return refs[a], rows(refs[n + a], me), peer, rows(refs[n + a], peer)

        def second(refs, me, j):
            a, sibling = j // 3, _peer(me, 1)
            mine = rows(refs[a], _peer(me, SAME_CORE[j % 3]))
            return mine, mine, sibling, rows(refs[a], _peer(sibling, SAME_CORE[j % 3]))

        self._first, self._second = first, second
        self._flight = _copies_start(list(shards) + list(lands), first, 4 * n, name=name + "_send")
        self.dep = self._flight[2]

    def pass_on(self, after):
        send1, recv1, *arrays = self._flight
        arrays = _copies_wait(send1, recv1, arrays, self._first, 4 * self.n, after, name=self.name + "_recv")
        self._flight = _copies_start(list(arrays[self.n:]), self._second, 3 * self.n, name=self.name + "_pass")
        return self._flight[2]

    def finish(self, after):
        send2, recv2, *lands = self._flight
        return _copies_wait(send2, recv2, lands, self._second, 3 * self.n, after, name=self.name + "_pass_recv")


def _exchange_start(gs, *, name):
    n = len(gs)
    rows = [g.shape[0] // N_DEV for g in gs]
    lands = [lax.empty((N_DEV - 1, r, g.shape[1]), g.dtype) for g, r in zip(gs, rows)]

    def body(*refs):
        g_refs, land_refs = refs[:n], refs[n:2 * n]
        send_sems, recv_sems = refs[2 * n:3 * n], refs[3 * n:4 * n]
        me = _position()
        for a in range(n):
            for k in SEND_ORDER:
                peer = _peer(me, k)
                pltpu.make_async_remote_copy(
                    src_ref=g_refs[a].at[pl.ds(_linear(peer) * rows[a], rows[a]), :],
                    dst_ref=land_refs[a].at[k - 1],
                    send_sem=send_sems[a].at[k - 1], recv_sem=recv_sems[a].at[k - 1],
                    device_id=peer, device_id_type=MESH).start()

    res = pl.pallas_call(
        body, name=name,
        out_shape=tuple(pltpu.SemaphoreType.DMA((N_DEV - 1,)) for _ in range(2 * n))
        + tuple(pltpu.HBM(a.shape, a.dtype) for a in gs + lands),
        in_specs=(HBM_SPEC,) * (2 * n), out_specs=(SEM_SPEC,) * (2 * n) + (HBM_SPEC,) * (2 * n),
        input_output_aliases={i: 2 * n + i for i in range(2 * n)},
        compiler_params=pltpu.CompilerParams(has_side_effects=DATAFLOW),
    )(*[_in_hbm(a) for a in gs + lands])
    return [(res[a], res[n + a], res[2 * n + a], res[3 * n + a]) for a in range(n)]


def _exchange_wait(send_sems, recv_sems, g_thru, land_thru, after, *, name):
    r = land_thru.shape[1]

    def body(g_ref, land_ref, send_sems, recv_sems, after_ref, g_dead, got_ref):
        del after_ref, g_dead, got_ref
        me = _position()
        for k in SEND_ORDER:
            peer = _peer(me, k)
            copy = pltpu.make_async_remote_copy(
                src_ref=g_ref.at[pl.ds(_linear(peer) * r, r), :], dst_ref=land_ref.at[k - 1],
                send_sem=send_sems.at[k - 1], recv_sem=recv_sems.at[k - 1],
                device_id=peer, device_id_type=MESH)
            copy.wait_send()
            copy.wait_recv()

    return pl.pallas_call(
        body, name=name,
        out_shape=(pltpu.HBM(g_thru.shape, g_thru.dtype), pltpu.HBM(land_thru.shape, land_thru.dtype)),
        in_specs=(HBM_SPEC, HBM_SPEC, SEM_SPEC, SEM_SPEC, pl.BlockSpec(memory_space=pl.ANY)),
        out_specs=(HBM_SPEC, HBM_SPEC), input_output_aliases={0: 0, 1: 1},
        compiler_params=pltpu.CompilerParams(has_side_effects=DATAFLOW),
    )(g_thru, land_thru, send_sems, recv_sems, after)


def _adamw_math(w, g, m, v):
    m = B1 * m + (1.0 - B1) * g
    v = B2 * v + (1.0 - B2) * (g * g)
    delta = -LR * ((m / C1) / (jnp.sqrt(v / C2) + AEPS) + WD * w)
    return delta, m, v


def _sum_adamw(items, *, name):
    n = len(items)

    def body(*refs):
        ins, outs, scratch = refs[:5 * n], refs[5 * n:9 * n], refs[9 * n:]
        me_lin = _linear(_position())
        mine = []
        for a in range(n):
            r = items[a][2].shape[0]
            mine.append(pltpu.make_async_copy(ins[5 * a].at[pl.ds(me_lin * r, r), :], scratch[a], scratch[n].at[a]))
            mine[-1].start()
        for a in range(n):
            _, land_ref, w_ref, m_ref, v_ref = ins[5 * a:5 * a + 5]
            g_ref, d_ref, nm_ref, nv_ref = outs[4 * a:4 * a + 4]
            g = land_ref[0].astype(F32)
            for s in range(1, N_DEV - 1):
                g = g + land_ref[s].astype(F32)
            mine[a].wait()
            g = scratch[a][...].astype(F32) + g
            g_ref[...] = g
            d_ref[...], nm_ref[...], nv_ref[...] = _adamw_math(w_ref[...], g, m_ref[...], v_ref[...])

    vmem = pl.BlockSpec(memory_space=pltpu.VMEM)
    res = pl.pallas_call(
        body, name=name,
        out_shape=tuple(jax.ShapeDtypeStruct(it[2].shape, F32) for it in items for _ in range(4)),
        in_specs=[ANY_SPEC, vmem, vmem, vmem, vmem] * n, out_specs=(vmem,) * (4 * n),
        scratch_shapes=[pltpu.VMEM(it[2].shape, BF16) for it in items] + [pltpu.SemaphoreType.DMA((n,))],
        compiler_params=_params(),
    )(*[a for it in items for a in it])
    return [res[4 * a:4 * a + 4] for a in range(n)]


SMALL = ("g_mix_pre", "g_mix_post", "g_mem", "g_x_pre", "g_x_post", "g_ffn_pre", "g_ffn_post",
         "hgrn_onorm", "hgrn_lb", "sinks")
SMALL_W = dict(hgrn_onorm=HD, hgrn_lb=HG_W, sinks=8)
SQ_ROW = len(SMALL)
PACK_ROWS = 16


def _small_pack(parts):
    ns = len(SMALL)

    def body(*refs):
        part, mine, slots, sem = refs[:ns + 1], refs[ns + 1], refs[ns + 2], refs[ns + 3]
        mine[...] = jnp.zeros((PACK_ROWS, D), F32)
        for r, name in enumerate(SMALL):
            wd = SMALL_W.get(name, D)
            mine[r:r + 1, 0:wd] = jnp.sum(part[r][...], axis=0, keepdims=True)[:, 0:wd]
        sq = jnp.sum(part[ns][...]) * (0.5 / D)
        mine[SQ_ROW:SQ_ROW + 1, :] = jnp.full((1, D), sq, F32)
        own = pltpu.make_async_copy(mine, slots.at[_linear(_position())], sem)
        own.start()
        own.wait()

    vmem = pl.BlockSpec(memory_space=pltpu.VMEM)
    return pl.pallas_call(
        body, name="small_pack",
        out_shape=(jax.ShapeDtypeStruct((PACK_ROWS, D), F32), jax.ShapeDtypeStruct((N_DEV, PACK_ROWS, D), F32)),
        in_specs=[vmem] * (ns + 1), out_specs=(vmem, ANY_SPEC),
        scratch_shapes=[pltpu.SemaphoreType.DMA(())], compiler_params=_params(),
    )(*[parts[n] for n in SMALL], parts["sq"])


def _small_exchange(mine, slots):
    def plan(refs, me, j):
        peer = _peer(me, j + 1)
        return refs[0], refs[1].at[_linear(me)], peer, refs[1].at[_linear(peer)]

    send, recv, mine1, slots1 = _copies_start([mine, slots], plan, N_DEV - 1, name="small_send")
    return lambda after: _copies_wait(send, recv, [mine1, slots1], plan, N_DEV - 1, after, name="small_recv")[1]


def _small_update(slots, sm, m_sm, v_sm):
    ns = len(SMALL)

    def body(*refs):
        tot = refs[0][0]
        for s in range(1, N_DEV):
            tot = tot + refs[0][s]
        w_refs, m_refs, v_refs = refs[1:ns + 1], refs[ns + 1:2 * ns + 1], refs[2 * ns + 1:3 * ns + 1]
        outs = refs[3 * ns + 1:]
        loss_ref = outs[0]
        g_out, d_out = outs[1:ns + 1], outs[ns + 1:2 * ns + 1]
        nm_out, nv_out = outs[2 * ns + 1:3 * ns + 1], outs[3 * ns + 1:4 * ns + 1]
        loss_ref[...] = tot[SQ_ROW:SQ_ROW + 1, 0:1]
        for r, name in enumerate(SMALL):
            wd = SMALL_W.get(name, D)
            g = tot[r:r + 1, 0:wd]
            w = w_refs[r][...]
            if name == "hgrn_lb":
                mx = jnp.maximum(w[0:1], w[1:2])
                e0, e1 = jnp.exp(w[0:1] - mx), jnp.exp(w[1:2] - mx)
                lb0 = e0 / (e0 + e1)
                g0 = g * lb0 * (1.0 - lb0)
                for i, gi in enumerate((g0, -g0)):
                    d, nm, nv = _adamw_math(w[i:i + 1], gi, m_refs[r][i:i + 1, :], v_refs[r][i:i + 1, :])
                    g_out[r][i:i + 1, :] = gi
                    d_out[r][i:i + 1, :], nm_out[r][i:i + 1, :], nv_out[r][i:i + 1, :] = d, nm, nv
            else:
                d, nm, nv = _adamw_math(w, g, m_refs[r][...], v_refs[r][...])
                g_out[r][...] = g
                d_out[r][...], nm_out[r][...], nv_out[r][...] = d, nm, nv

    shapes = [jax.ShapeDtypeStruct(sm[n].shape, F32) for n in SMALL]
    res = pl.pallas_call(
        body, name="small_update", out_shape=tuple([jax.ShapeDtypeStruct((1, 1), F32)] + shapes * 4),
        compiler_params=_params(),
    )(slots, *[sm[n] for n in SMALL], *[m_sm[n] for n in SMALL], *[v_sm[n] for n in SMALL])
    groups = [dict(zip(SMALL, res[1 + i * ns:1 + (i + 1) * ns])) for i in range(4)]
    return res[0], groups[0], groups[1], groups[2], groups[3]


BIG = ("w_in", "w_gate", "w_up", "w_down", "w_out", "wq_x", "wk_x", "wv_x", "wo_x")
BIG_KEY = dict(w_in="winT", w_gate="wgT", w_up="wuT", w_down="wd", w_out="wout", wq_x="wq", wk_x="wk",
               wv_x="wv", wo_x="wo")
TRANSPOSED = ("w_in", "w_gate", "w_up")
WEIGHTS = ("w_in", "sinks", "hgrn_lb", "hgrn_onorm", "w_out", "g_mix_pre", "g_mix_post", "g_mem", "g_x_pre",
           "g_x_post", "wq_x", "wk_x", "wv_x", "wo_x", "g_ffn_pre", "g_ffn_post", "w_gate", "w_up", "w_down")


def kernel(x, mem, w_in, sinks, hgrn_lb, hgrn_onorm, w_out, g_mix_pre, g_mix_post, g_mem, g_x_pre, g_x_post, wq_x, wk_x, wv_x, wo_x, g_ffn_pre, g_ffn_post, w_gate, w_up, w_down, loss_target, m_w_in, m_sinks, m_hgrn_lb, m_hgrn_onorm, m_w_out, m_g_mix_pre, m_g_mix_post, m_g_mem, m_g_x_pre, m_g_x_post, m_wq_x, m_wk_x, m_wv_x, m_wo_x, m_g_ffn_pre, m_g_ffn_post, m_w_gate, m_w_up, m_w_down, v_w_in, v_sinks, v_hgrn_lb, v_hgrn_onorm, v_w_out, v_g_mix_pre, v_g_mix_post, v_g_mem, v_g_x_pre, v_g_x_post, v_wq_x, v_wk_x, v_wv_x, v_wo_x, v_g_ffn_pre, v_g_ffn_post, v_w_gate, v_w_up, v_w_down):
    given = dict(locals())
    wts = {n: given[n] for n in WEIGHTS}
    ms = {n: given["m_" + n] for n in WEIGHTS}
    vs = {n: given["v_" + n] for n in WEIGHTS}

    def mat(a, name):
        a = a[0]
        return a.T if name in TRANSPOSED else a

    groups = (("w_in",), ("w_out", "wq_x", "wk_x", "wv_x", "wo_x"), ("w_gate", "w_up", "w_down"))
    gathers = []

    def start_group(g, dep):
        tag = ("w_in", "w_attn", "w_ffn")[g]
        shards, lands = _prepare_weights([mat(wts[n], n) for n in groups[g]], name="prepare_" + tag, dep=dep)
        gathers.append(_TwoLevelGather(shards, lands, name=tag))
        return gathers[-1].dep

    first_dep = start_group(1, start_group(0, None))
    name_of = {k: n for n, k in BIG_KEY.items()}
    gathered = {}

    def milestone(tag, value):
        if tag == "z":
            return start_group(2, value)
        return gathers[{"swa": 1, "ox": 2}[tag]].pass_on(value)

    def fetch(key, after):
        name = name_of[key]
        if name not in gathered:
            g = [i for i, group in enumerate(groups) if name in group][0]
            if g == 0:
                gathers[0].pass_on(after)
            gathered.update(zip(groups[g], gathers[g].finish(after)))
        return gathered[name]

    sm = {n: wts[n] for n in SMALL}
    started, held = {}, {}
    send_with = {k: group for group in (("wgT", "wuT"), ("wo", "wq", "wk", "wv")) for k in group}

    def emit(key, g):
        held[key] = g
        group = send_with.get(key, (key,))
        if key != group[-1]:
            return None
        flights = _exchange_start([held[k] for k in group], name="grad_send_" + name_of[group[0]])
        started.update({name_of[k]: f for k, f in zip(group, flights)})
        return flights[-1][2]

    grad_x, _, parts = _local_step(x[0], mem[0], loss_target[0], fetch, sm, emit, first_dep=first_dep, milestone=milestone)
    small_finish = _small_exchange(*_small_pack(parts))
    grads, deltas, new_m, new_v = {}, {}, {}, {}
    after = grad_x
    for group in (("w_down",), ("w_gate", "w_up"), ("wo_x", "wq_x", "wk_x", "wv_x", "w_out"), ("w_in",)):
        items = []
        for n in group:
            g_all, land = _exchange_wait(*started[n], after, name="grad_recv_" + n)
            items.append((g_all, land, mat(wts[n], n), mat(ms[n], n), mat(vs[n], n)))
            after = land
        for n, res in zip(group, _sum_adamw(items, name="adamw_" + group[0])):
            after = res[1]
            if n in TRANSPOSED:
                res = [a.T for a in res]
            grads[n], deltas[n], new_m[n], new_v[n] = [a[None] for a in res]
    loss, g_s, d_s, m_s, v_s = _small_update(small_finish(after), sm, {n: ms[n] for n in SMALL},
                                             {n: vs[n] for n in SMALL})
    grads.update(g_s), deltas.update(d_s), new_m.update(m_s), new_v.update(v_s)
    return (loss[0, 0], grad_x[None], *[grads[n] for n in WEIGHTS], *[deltas[n] for n in WEIGHTS],
            *[new_m[n] for n in WEIGHTS], *[new_v[n] for n in WEIGHTS])
```

```python
import functools

import jax
import jax.numpy as jnp
from jax import lax
from jax.experimental import pallas as pl
from jax.experimental.pallas import tpu as pltpu

F32 = jnp.float32
BF16 = jnp.bfloat16

D = 1024
D_IN = 2816
D_FF = 2816
CHUNK = 64
SWA_W = 512
KV_W = 128
HG_W = 512
HD = 128
ZQH, ZFH, ZIH, ZGH = 768, 1280, 1792, 2304
XH, XD = 4, 256
EPS = 1e-6
NEG = -1e30
N_DEV = 8
MESH = pl.DeviceIdType.MESH

LR, B1, B2, AEPS, WD, STEP = 0.001, 0.9, 0.999, 1e-08, 0.01, 10
C1 = 1.0 - B1 ** STEP
C2 = 1.0 - B2 ** STEP

VMEM_LIMIT = 56 * 1024 * 1024


def _params(**kw):
    return pltpu.CompilerParams(vmem_limit_bytes=VMEM_LIMIT, **kw)


def _sig(x):
    return 1.0 / (1.0 + jnp.exp(-x))


def _rowsum8(x):
    r, w = x.shape
    return jnp.sum(x.reshape(r // 8, 8, w), axis=0)


def _dot(a, b, ca, cb, precision=None):
    return lax.dot_general(a, b, (((ca,), (cb,)), ((), ())), preferred_element_type=F32,
                           precision=precision)


ANY_SPEC = pl.BlockSpec(memory_space=pl.ANY)


def _mm(a, b, *, ta=False, tb=False, out_dtype, tm, tn, tk=None, name, dep=None, n_outer=False):
    m = a.shape[1] if ta else a.shape[0]
    k = a.shape[0] if ta else a.shape[1]
    n = b.shape[0] if tb else b.shape[1]
    tm, tn = min(tm, m), min(tn, n)
    tk = k if tk is None else min(tk, k)
    nk = k // tk
    assert m % tm == 0 and n % tn == 0 and k % tk == 0, (name, m, n, k, tm, tn, tk)
    ij = (lambda g0, g1: (g1, g0)) if n_outer else (lambda g0, g1: (g0, g1))
    a_spec = (pl.BlockSpec((tk, tm), lambda g0, g1, kk: (kk, ij(g0, g1)[0])) if ta
              else pl.BlockSpec((tm, tk), lambda g0, g1, kk: (ij(g0, g1)[0], kk)))
    b_spec = (pl.BlockSpec((tn, tk), lambda g0, g1, kk: (ij(g0, g1)[1], kk)) if tb
              else pl.BlockSpec((tk, tn), lambda g0, g1, kk: (kk, ij(g0, g1)[1])))
    ca, cb = (0 if ta else 1), (1 if tb else 0)

    deps = [] if dep is None else [dep]

    def body(a_ref, b_ref, *rest):
        o_ref, acc = rest[len(deps)], rest[len(deps) + 1:]
        p = _dot(a_ref[...].astype(BF16), b_ref[...].astype(BF16), ca, cb)
        if nk == 1:
            o_ref[...] = p.astype(out_dtype)
        else:
            acc_ref, = acc
            kk = pl.program_id(2)

            @pl.when(kk == 0)
            def _():
                acc_ref[...] = p

            @pl.when(kk > 0)
            def _():
                acc_ref[...] += p

            @pl.when(kk == nk - 1)
            def _():
                o_ref[...] = acc_ref[...].astype(out_dtype)

    return pl.pallas_call(
        body, name=name, out_shape=jax.ShapeDtypeStruct((m, n), out_dtype),
        grid=(n // tn, m // tm, nk) if n_outer else (m // tm, n // tn, nk),
        in_specs=[a_spec, b_spec] + [ANY_SPEC] * len(deps),
        out_specs=pl.BlockSpec((tm, tn), lambda g0, g1, kk: ij(g0, g1)),
        scratch_shapes=[pltpu.VMEM((tm, tn), F32)] if nk > 1 else [],
        compiler_params=_params(dimension_semantics=("parallel", "parallel", "arbitrary")),
    )(a, b, *deps)


def _mm_rows(prods, rows_in, vecs_in, epilogue, outs, *, tm, name, dep=None):
    m = prods[0][0].shape[0]
    n = prods[0][1].shape[0] if prods[0][2] else prods[0][1].shape[1]
    tm = min(tm, m)
    assert m % tm == 0
    deps = [] if dep is None else [dep]
    n_p, n_r, n_v = len(prods), len(rows_in), len(vecs_in)

    def body(*refs):
        ab = refs[:2 * n_p]
        row_refs = refs[2 * n_p:2 * n_p + n_r]
        vec_refs = refs[2 * n_p + n_r:2 * n_p + n_r + n_v]
        out_refs = refs[2 * n_p + n_r + n_v + len(deps):]
        p = None
        for j, (_, _, tb) in enumerate(prods):
            t = _dot(ab[2 * j][...].astype(BF16), ab[2 * j + 1][...], 1, 1 if tb else 0)
            p = t if p is None else p + t
        vals = epilogue(p, *[r[...] for r in row_refs], *[v[...] for v in vec_refs])
        for (dtype, kind), o_ref, val in zip(outs, out_refs, vals):
            if kind == "row":
                o_ref[...] = val.astype(dtype)
            else:
                @pl.when(pl.program_id(0) == 0)
                def _(o_ref=o_ref):
                    o_ref[...] = jnp.zeros_like(o_ref)

                o_ref[...] += val

    row = lambda w: pl.BlockSpec((tm, w), lambda i: (i, 0))
    whole = lambda a: pl.BlockSpec(a.shape, lambda i: (0,) * a.ndim, pipeline_mode=pl.Buffered(1))
    in_specs, args = [], []
    for a, b, _ in prods:
        in_specs += [row(a.shape[1]), whole(b)]
        args += [a, b]
    in_specs += [row(r.shape[1]) for r in rows_in] + [whole(v) for v in vecs_in] + [ANY_SPEC] * len(deps)
    return pl.pallas_call(
        body, name=name,
        out_shape=tuple(jax.ShapeDtypeStruct((m, n) if kind == "row" else (8, n), dtype) for dtype, kind in outs),
        grid=(m // tm,), in_specs=in_specs,
        out_specs=tuple(row(n) if kind == "row" else pl.BlockSpec((8, n), lambda i: (0, 0)) for _, kind in outs),
        compiler_params=_params(dimension_semantics=("arbitrary",)),
    )(*args, *rows_in, *vecs_in, *deps)


def _rstd(x):
    return lax.rsqrt(jnp.mean(x * x, axis=-1, keepdims=True) + EPS)


def _norm_bwd(xh, r, t):
    return r * (t - xh * jnp.mean(xh * t, axis=-1, keepdims=True))


ROW_F32, ROW_BF16, SUM_F32 = (F32, "row"), (BF16, "row"), (F32, "sum")


def _then(epilogue, index, tb):
    def run(p, *args):
        vals = epilogue(p, *args[:-1])
        return (*vals, _dot(vals[index].astype(BF16), args[-1], 1, 1 if tb else 0))

    return run


def _ep_post_pre(p, h, g_post, g_pre):
    y = p.astype(BF16)
    yf = y.astype(F32)
    hn = h + yf * _rstd(yf) * g_post
    return y, hn, hn * _rstd(hn) * g_pre


_EP_POST_PRE_OUTS = [ROW_BF16, ROW_F32, ROW_BF16]


def _ep_final_loss(y, h, target, g_post):
    r = _rstd(y)
    yh = y * r
    err = h + yh * g_post - target
    dh = err * (1.0 / D)
    return _rowsum8(err * err), dh, _norm_bwd(yh, r, dh * g_post), _rowsum8(dh * yh)


def _ep_post_pre_bwd(du, dh_out, hn, y, g_post, g_pre):
    r2 = _rstd(hn)
    xh = hn * r2
    dh = dh_out + _norm_bwd(xh, r2, du * g_pre)
    yf = y.astype(F32)
    r1 = _rstd(yf)
    yh = yf * r1
    return dh, _norm_bwd(yh, r1, dh * g_post), _rowsum8(du * xh), _rowsum8(dh * yh)


_EP_POST_PRE_BWD_OUTS = [ROW_F32, ROW_BF16, SUM_F32, SUM_F32]


def _ep_pre_bwd(du, dh_out, x, g):
    r = _rstd(x)
    xh = x * r
    return dh_out + _norm_bwd(xh, r, du * g), _rowsum8(du * xh)


_EP_PRE_BWD_OUTS = [ROW_F32, SUM_F32]


def _prenorm(x, g, *, name, dep=None):
    t, d = x.shape
    tb = min(512, t)
    deps = [] if dep is None else [dep]

    def body(x_ref, g_ref, *rest):
        xf = x_ref[...]
        rest[-1][...] = (xf * _rstd(xf) * g_ref[...]).astype(BF16)

    return pl.pallas_call(
        body, name=name, out_shape=jax.ShapeDtypeStruct((t, d), BF16), grid=(t // tb,),
        in_specs=[pl.BlockSpec((tb, d), lambda i: (i, 0)), pl.BlockSpec((1, d), lambda i: (0, 0))]
        + [ANY_SPEC] * len(deps),
        out_specs=pl.BlockSpec((tb, d), lambda i: (i, 0)), compiler_params=_params(),
    )(x, g, *deps)


QB = 256


def _half_mask(shape, e):
    lane = lax.broadcasted_iota(jnp.int32, shape, len(shape) - 1)
    return (lane // 64) == e


def _place(kv):
    sw = pltpu.roll(kv, 64, 1)
    m0 = _half_mask(kv.shape, 0)
    return [[jnp.where(m0, kv, 0.0).astype(BF16), jnp.where(m0, 0.0, sw).astype(BF16)],
            [jnp.where(m0, sw, 0.0).astype(BF16), jnp.where(m0, 0.0, kv).astype(BF16)]]


SQ = 128
SK = 256


def _swa_valid(i, sb):
    qc = lax.broadcasted_iota(jnp.int32, (SQ, SK), 0) // CHUNK
    kc = lax.broadcasted_iota(jnp.int32, (SQ, SK), 1) // CHUNK - 2
    return (kc <= qc) & (qc <= kc + 2) & (4 * i + 2 * sb + kc >= 0)


def _swa_fwd(z, sinks, t, dep=None):
    nb = t // QB
    deps = [] if dep is None else [dep]

    def body(s_ref, q_ref, kp_ref, kc_ref, vp_ref, vc_ref, *rest):
        o_ref, lse_ref = rest[-2:]
        i = pl.program_id(0)
        kpl = _place(jnp.concatenate([kp_ref[...], kc_ref[...]], axis=0))
        vpl = _place(jnp.concatenate([vp_ref[...], vc_ref[...]], axis=0))
        lane = lax.broadcasted_iota(jnp.int32, (SQ, 128), 1)
        for sb in range(QB // SQ):
            rows, keys = slice(SQ * sb, SQ * (sb + 1)), slice(SQ * sb, SQ * sb + SK)
            valid = _swa_valid(i, sb)
            lse_out = jnp.zeros((SQ, 128), F32)
            for j in range(4):
                qp = q_ref[rows, 128 * j:128 * (j + 1)].astype(BF16)
                acc = jnp.zeros((SQ, 128), F32)
                for e in range(2):
                    h = 2 * j + e
                    kvh = h // 4
                    qm = jnp.where(_half_mask(qp.shape, e), qp, jnp.zeros_like(qp))
                    s = _dot(qm, kpl[kvh][e][keys], 1, 1) * 0.125
                    s = jnp.where(valid, s, NEG)
                    sink = s_ref[0, h]
                    m = jnp.maximum(jnp.max(s, axis=-1, keepdims=True), sink)
                    p = jnp.exp(s - m)
                    l = jnp.sum(p, axis=-1, keepdims=True) + jnp.exp(sink - m)
                    acc = acc + _dot(p.astype(BF16), vpl[kvh][e][keys], 1, 0) * (1.0 / l)
                    lse_out = jnp.where(lane == h, m + jnp.log(l), lse_out)
                o_ref[rows, 128 * j:128 * (j + 1)] = acc.astype(BF16)
            lse_ref[rows, :] = lse_out

    prev = lambda c: pl.BlockSpec((128, 128), lambda i: (jnp.maximum(2 * i - 1, 0), c))
    cur = lambda c: pl.BlockSpec((QB, 128), lambda i: (i, c))
    return pl.pallas_call(
        body, name="swa_fwd",
        out_shape=(jax.ShapeDtypeStruct((t, D), BF16), jax.ShapeDtypeStruct((t, 128), F32)),
        grid=(nb,),
        in_specs=[pl.BlockSpec(memory_space=pltpu.SMEM),
                  pl.BlockSpec((QB, SWA_W), lambda i: (i, 0)), prev(4), cur(4), prev(5), cur(5)]
        + [ANY_SPEC] * len(deps),
        out_specs=(pl.BlockSpec((QB, SWA_W), lambda i: (i, 0)), pl.BlockSpec((QB, 128), lambda i: (i, 0))),
        compiler_params=_params(),
    )(sinks, z, z, z, z, z, *deps)


def _swa_bwd(z, sinks, ymix, lse, dymix, t, dep=None):
    nb = t // QB
    deps = [] if dep is None else [dep]

    def body(s_ref, q_ref, kp_ref, kc_ref, vp_ref, vc_ref, o_ref, do_ref, l_ref, *rest):
        dq_ref, first_ref, second_ref, ds_ref, carry_ref = rest[len(deps):]
        i = pl.program_id(0)
        live = i < nb

        @pl.when(i == 0)
        def _():
            ds_ref[...] = jnp.zeros_like(ds_ref)
            carry_ref[...] = jnp.zeros_like(carry_ref)

        lane = lax.broadcasted_iota(jnp.int32, (8, 128), 1)
        kpl = _place(jnp.concatenate([kp_ref[...], kc_ref[...]], axis=0))
        vpl = _place(jnp.concatenate([vp_ref[...], vc_ref[...]], axis=0))
        nk = QB + 128
        qc = lax.broadcasted_iota(jnp.int32, (QB, nk), 0) // CHUNK
        kc = lax.broadcasted_iota(jnp.int32, (QB, nk), 1) // CHUNK - 2
        valid = (kc <= qc) & (qc <= kc + 2) & (4 * i + kc >= 0) & live
        lse_c = l_ref[...]
        dsink = jnp.zeros((8, 128), F32)
        dk_acc = [[jnp.zeros((128, nk), F32) for _ in range(2)] for _ in range(2)]
        dv_acc = [[jnp.zeros((128, nk), F32) for _ in range(2)] for _ in range(2)]
        dq = []
        for j in range(4):
            cols = slice(128 * j, 128 * (j + 1))
            qp = q_ref[:, cols].astype(BF16)
            dop = do_ref[:, cols]
            prod = dop.astype(F32) * o_ref[:, cols].astype(F32)
            acc = jnp.zeros((QB, 128), F32)
            for e in range(2):
                h = 2 * j + e
                kvh = h // 4
                hm = _half_mask(qp.shape, e)
                qm = jnp.where(hm, qp, jnp.zeros_like(qp))
                dom = jnp.where(hm, dop, jnp.zeros_like(dop))
                dd = jnp.sum(jnp.where(hm, prod, 0.0), axis=-1, keepdims=True)
                lse_h = lse_c[:, h:h + 1]
                s = _dot(qm, kpl[kvh][e], 1, 1) * 0.125
                p = jnp.where(valid, jnp.exp(s - lse_h), 0.0)
                dp = _dot(dom, vpl[kvh][e], 1, 1)
                ds = (p * (dp - dd) * 0.125).astype(BF16)
                acc = acc + _dot(ds, kpl[kvh][e], 1, 0)
                dk_acc[kvh][e] = dk_acc[kvh][e] + _dot(qm, ds, 0, 0)
                dv_acc[kvh][e] = dv_acc[kvh][e] + _dot(dom, p.astype(BF16), 0, 0)
                ps = jnp.where(live, jnp.exp(s_ref[0, h] - lse_h) * dd, 0.0)
                dsink = dsink - jnp.where(lane == h, _rowsum8(jnp.broadcast_to(ps, (QB, 128))), 0.0)
            dq.append(acc.astype(BF16))
        ds_ref[...] += dsink
        dk = (dk_acc[0][0] + dk_acc[1][1] + pltpu.roll(dk_acc[0][1] + dk_acc[1][0], 64, 0)).T
        dv = (dv_acc[0][0] + dv_acc[1][1] + pltpu.roll(dv_acc[0][1] + dv_acc[1][0], 64, 0)).T
        dkv = jnp.concatenate([dk, dv], axis=1)
        second_ref[...] = (carry_ref[...] + dkv[0:128]).astype(BF16)
        carry_ref[...] = dkv[256:384]

        @pl.when(live)
        def _():
            for j in range(4):
                dq_ref[:, 128 * j:128 * (j + 1)] = dq[j]
            first_ref[...] = dkv[128:256].astype(BF16)

    blk = lambda i: jnp.minimum(i, nb - 1)
    prev = lambda c: pl.BlockSpec((128, 128), lambda i: (jnp.maximum(2 * blk(i) - 1, 0), c))
    cur = lambda w, c: pl.BlockSpec((QB, w), lambda i: (blk(i), c))
    half = lambda index: pl.BlockSpec((128, 256), lambda i: (index(i), 0))
    return pl.pallas_call(
        body, name="swa_bwd",
        out_shape=(jax.ShapeDtypeStruct((t, SWA_W), BF16), jax.ShapeDtypeStruct((t // 2, 256), BF16),
                   jax.ShapeDtypeStruct((t // 2, 256), BF16), jax.ShapeDtypeStruct((8, 128), F32)),
        grid=(nb + 1,),
        in_specs=[pl.BlockSpec(memory_space=pltpu.SMEM),
                  cur(SWA_W, 0), prev(4), cur(128, 4), prev(5), cur(128, 5),
                  cur(SWA_W, 0), cur(SWA_W, 0), cur(128, 0)] + [ANY_SPEC] * len(deps),
        out_specs=(cur(SWA_W, 0), half(blk), half(lambda i: jnp.maximum(i - 1, 0)),
                   pl.BlockSpec((8, 128), lambda i: (0, 0))),
        scratch_shapes=[pltpu.VMEM((128, 256), F32)],
        compiler_params=_params(dimension_semantics=("arbitrary",)),
    )(sinks, z, z, z, z, z, ymix, dymix, lse, *deps)


HB = 256


def _lower_bound(lb_ref):
    a = lb_ref[...]
    a0, a1 = a[0:1], a[1:2]
    mx = jnp.maximum(a0, a1)
    e0, e1 = jnp.exp(a0 - mx), jnp.exp(a1 - mx)
    return e0 / (e0 + e1)


def _hgrn_cols(row_block):
    return [pl.BlockSpec((HB, 2 * HD), lambda j, c=base // (2 * HD) + p: (row_block(j), c))
            for base in (ZQH, ZFH, ZIH, ZGH) for p in range(2)]


NCH = HB // CHUNK


def _split3(x):
    hi = x.astype(BF16)
    r1 = x - hi.astype(F32)
    mid = r1.astype(BF16)
    return hi, mid, (r1 - mid.astype(F32)).astype(BF16)


def _blockdiag(lower):
    r = lax.broadcasted_iota(jnp.int32, (HB, HB), 0)
    c = lax.broadcasted_iota(jnp.int32, (HB, HB), 1)
    return (r // CHUNK == c // CHUNK) & ((c <= r) if lower else (c >= r))


def _chunk_sums(mask_bf16, x):
    return sum(_dot(mask_bf16, part, 1, 0) for part in _split3(x))


def _per_chunk_rows(x, row):
    w = x.shape[1]
    picked = x.reshape(NCH, CHUNK, w)[:, row:row + 1, :]
    return jnp.broadcast_to(picked, (NCH, CHUNK, w)).reshape(HB, w)


def _chunk_stack(x, chunk_of_row):
    return jnp.concatenate([jnp.where(chunk_of_row == c, x, jnp.zeros_like(x)) for c in range(NCH)], axis=1)


def _chunk_pick(x, chunk_of_row):
    w = x.shape[1] // NCH
    out = jnp.zeros((HB, w), x.dtype)
    for c in range(NCH):
        out = jnp.where(chunk_of_row == c, x[:, c * w:(c + 1) * w], out)
    return out


def _hgrn_local(q, f, kf, b):
    sq = _sig(q)
    qf = q * sq * (HD ** -0.5)
    b_mid = _per_chunk_rows(b, CHUNK // 2 - 1)
    b_last = _per_chunk_rows(b, CHUNK - 1)
    qm = qf * jnp.exp(b - b_mid)
    km = kf * jnp.exp(b_mid - b)
    kl = kf * jnp.exp(b_last - b)
    qb = qf * jnp.exp(b)
    return dict(sq=sq, b_mid=b_mid, b_last=b_last, qm=qm, km=km, kl=kl, qb=qb)


def _hgrn2_fwd(z, hgrn_lb, onorm, ymix, t, dep=None):
    nb = t // HB
    deps = [] if dep is None else [dep]

    def body(*refs):
        zq, zf, zi, zg = refs[0:2], refs[2:4], refs[4:6], refs[6:8]
        (lb_ref, on_ref), (y_ref, o_ref, sp_ref, st_ref) = refs[8:10], refs[-4:]

        @pl.when(pl.program_id(0) == 0)
        def _():
            st_ref[...] = jnp.zeros_like(st_ref)

        lb_all = _lower_bound(lb_ref)
        gn = on_ref[...]
        low = _blockdiag(True)
        low_b = low.astype(BF16)
        chunk_of_row = lax.broadcasted_iota(jnp.int32, (HB, HD), 0) // CHUNK
        for p in range(2):
            lbp = lb_all[:, 2 * HD * p:2 * HD * (p + 1)]
            fp = lbp + (1.0 - lbp) * _sig(zf[p][...])
            bp = _chunk_sums(low_b, jnp.log(fp))
            for e in range(2):
                h, ls = 2 * p + e, slice(e * HD, (e + 1) * HD)
                f = fp[:, ls]
                w = _hgrn_local(zq[p][:, ls], f, 1.0 - f, bp[:, ls])
                iv = zi[p][:, ls].astype(BF16)
                a = jnp.where(low, _dot(w["qm"].astype(BF16), w["km"].astype(BF16), 1, 1), 0.0)
                o = _dot(a.astype(BF16), iv, 1, 0)
                u = _dot(iv, _chunk_stack(w["kl"].astype(BF16), chunk_of_row), 0, 0)
                decay = jnp.exp(w["b_last"])
                st = st_ref[h]
                states = []
                for c in range(NCH):
                    sp_ref[h, c] = st
                    states.append(st.astype(BF16))
                    st = st * decay[c * CHUNK:c * CHUNK + 1] + u[:, c * HD:(c + 1) * HD]
                st_ref[h] = st
                inter = _dot(w["qb"].astype(BF16), jnp.concatenate(states, axis=0), 1, 1)
                o = o + _chunk_pick(inter, chunk_of_row)
                hs = slice(h * HD, (h + 1) * HD)
                o_ref[:, hs] = o
                gg = zg[p][:, ls]
                y_ref[:, hs] = (o * _rstd(o) * gn * (gg * _sig(gg))).astype(BF16)

    return pl.pallas_call(
        body, name="hgrn_fwd",
        out_shape=(jax.ShapeDtypeStruct((t, D), BF16), jax.ShapeDtypeStruct((t, HG_W), F32),
                   jax.ShapeDtypeStruct((4, t // CHUNK, HD, HD), F32)),
        grid=(nb,),
        in_specs=_hgrn_cols(lambda j: j) + [pl.BlockSpec((2, HG_W), lambda j: (0, 0)),
                                            pl.BlockSpec((1, HD), lambda j: (0, 0)), ANY_SPEC]
        + [ANY_SPEC] * len(deps),
        out_specs=(pl.BlockSpec((HB, HG_W), lambda j: (j, 1)),
                   pl.BlockSpec((HB, HG_W), lambda j: (j, 0)),
                   pl.BlockSpec((4, NCH, HD, HD), lambda j: (0, j, 0, 0))),
        scratch_shapes=[pltpu.VMEM((4, HD, HD), F32)],
        input_output_aliases={10: 0},
        compiler_params=_params(dimension_semantics=("arbitrary",)),
    )(*[z] * 8, hgrn_lb, onorm, ymix, *deps)


def _hgrn2_bwd(z, hgrn_lb, onorm, o_save, sprev, dymix, dza, t):
    nb = t // HB

    def body(*refs):
        zq, zf, zi, zg = refs[0:2], refs[2:4], refs[4:6], refs[6:8]
        (lb_ref, on_ref, o_ref, sp_ref, dy_ref, dqa_ref, first_ref, second_ref,
         dz_ref, dlb_ref, don_ref, dst_ref) = refs[8:]

        @pl.when(pl.program_id(0) == 0)
        def _():
            dst_ref[...] = jnp.zeros_like(dst_ref)
            dlb_ref[...] = jnp.zeros_like(dlb_ref)
            don_ref[...] = jnp.zeros_like(don_ref)

        dz_ref[:, 0:SWA_W] = dqa_ref[...]
        dz_ref[0:HB // 2, SWA_W:ZQH] = first_ref[...]
        dz_ref[HB // 2:HB, SWA_W:ZQH] = second_ref[...]
        lb_all = _lower_bound(lb_ref)
        gn = on_ref[...]
        low, upp = _blockdiag(True), _blockdiag(False)
        upp_b = upp.astype(BF16)
        low_b = low.astype(BF16)
        row = lax.broadcasted_iota(jnp.int32, (HB, HD), 0)
        chunk_of_row = row // CHUNK
        in_chunk = row % CHUNK
        for p in range(2):
            lbp = lb_all[:, 2 * HD * p:2 * HD * (p + 1)]
            sgp = _sig(zf[p][...])
            fp = lbp + (1.0 - lbp) * sgp
            bp = _chunk_sums(low_b, jnp.log(fp))
            db_pair, dkf_pair = [], []
            for e in range(2):
                h, ls, hs = 2 * p + e, slice(e * HD, (e + 1) * HD), slice((2 * p + e) * HD, (2 * p + e + 1) * HD)
                f = fp[:, ls]
                q = zq[p][:, ls]
                w = _hgrn_local(q, f, 1.0 - f, bp[:, ls])
                iv = zi[p][:, ls].astype(BF16)
                gg = zg[p][:, ls]
                o = o_ref[:, hs]
                dout = dy_ref[:, hs].astype(F32)
                sgg = _sig(gg)
                r = _rstd(o)
                oh = o * r
                dyn = dout * (gg * sgg)
                dz_ref[:, ZGH + h * HD:ZGH + (h + 1) * HD] = (
                    dout * oh * gn * (sgg * (1.0 + gg * (1.0 - sgg)))).astype(BF16)
                don_ref[...] += _rowsum8(dyn * oh)
                do = _norm_bwd(oh, r, dyn * gn).astype(BF16)
                qm, km, kl, qb = (w[n].astype(BF16) for n in ("qm", "km", "kl", "qb"))
                decay = jnp.exp(w["b_last"])
                grads_in = _dot(do, _chunk_stack(qb, chunk_of_row), 0, 0)
                dst = dst_ref[h]
                dstn, dd_rows = [None] * NCH, [None] * NCH
                for c in reversed(range(NCH)):
                    dstn[c] = dst.astype(BF16)
                    dd_rows[c] = jnp.sum(dst * sp_ref[h, c], axis=0, keepdims=True)
                    dst = dst * decay[c * CHUNK:c * CHUNK + 1] + grads_in[:, c * HD:(c + 1) * HD]
                dst_ref[h] = dst
                states = jnp.concatenate([sp_ref[h, c].astype(BF16) for c in range(NCH)], axis=0)
                dstn_all = jnp.concatenate(dstn, axis=0)
                dqb = _dot(_chunk_stack(do, chunk_of_row), states, 1, 0)
                at = jnp.where(upp, _dot(km, qm, 1, 1), 0.0)
                di = _dot(at.astype(BF16), do, 1, 0) + _chunk_pick(_dot(kl, dstn_all, 1, 1), chunk_of_row)
                dz_ref[:, ZIH + h * HD:ZIH + (h + 1) * HD] = di.astype(BF16)
                dkl = _dot(_chunk_stack(iv, chunk_of_row), dstn_all, 1, 0)
                da = jnp.where(low, _dot(do, iv, 1, 1), 0.0).astype(BF16)
                dat = jnp.where(upp, _dot(iv, do, 1, 1), 0.0).astype(BF16)
                dqm = _dot(da, km, 1, 0)
                dkm = _dot(dat, qm, 1, 0)
                b = bp[:, ls]
                e1, e2 = jnp.exp(b - w["b_mid"]), jnp.exp(w["b_mid"] - b)
                e3, e4 = jnp.exp(w["b_last"] - b), jnp.exp(b)
                dqf = dqm * e1 + dqb * e4
                dkf_pair.append(dkm * e2 + dkl * e3)
                t_qm, t_km, t_kl = dqm * w["qm"], dkm * w["km"], dkl * w["kl"]
                db = t_qm - t_km - t_kl + dqb * w["qb"]
                db_mid = jnp.sum((t_km - t_qm).reshape(NCH, CHUNK, HD), axis=1, keepdims=True)
                db_last = jnp.sum(t_kl.reshape(NCH, CHUNK, HD), axis=1, keepdims=True)
                db_last = db_last + jnp.stack(dd_rows, axis=0) * jnp.exp(
                    bp[:, ls].reshape(NCH, CHUNK, HD)[:, CHUNK - 1:CHUNK, :])
                spread = lambda v: jnp.broadcast_to(v, (NCH, CHUNK, HD)).reshape(HB, HD)
                db = (db + jnp.where(in_chunk == CHUNK // 2 - 1, spread(db_mid), 0.0)
                      + jnp.where(in_chunk == CHUNK - 1, spread(db_last), 0.0))
                db_pair.append(db)
                sq = w["sq"]
                dz_ref[:, ZQH + h * HD:ZQH + (h + 1) * HD] = (
                    dqf * (HD ** -0.5) * (sq * (1.0 + q * (1.0 - sq)))).astype(BF16)
            dlogf = _chunk_sums(upp_b, jnp.concatenate(db_pair, axis=1))
            dfv = dlogf / fp - jnp.concatenate(dkf_pair, axis=1)
            dz_ref[:, ZFH + 2 * HD * p:ZFH + 2 * HD * (p + 1)] = (dfv * (1.0 - lbp) * sgp * (1.0 - sgp)).astype(BF16)
            dlb_ref[:, 2 * HD * p:2 * HD * (p + 1)] += _rowsum8(dfv * (1.0 - sgp))

    rev = lambda j: nb - 1 - j
    return pl.pallas_call(
        body, name="hgrn_bwd",
        out_shape=(jax.ShapeDtypeStruct((t, D_IN), BF16), jax.ShapeDtypeStruct((8, HG_W), F32),
                   jax.ShapeDtypeStruct((8, HD), F32)),
        grid=(nb,),
        in_specs=_hgrn_cols(rev) + [pl.BlockSpec((2, HG_W), lambda j: (0, 0)), pl.BlockSpec((1, HD), lambda j: (0, 0)),
                                    pl.BlockSpec((HB, HG_W), lambda j: (rev(j), 0)),
                                    pl.BlockSpec((4, NCH, HD, HD), lambda j: (0, rev(j), 0, 0)),
                                    pl.BlockSpec((HB, HG_W), lambda j: (rev(j), 1)),
                                    pl.BlockSpec((HB, SWA_W), lambda j: (rev(j), 0)),
                                    pl.BlockSpec((HB // 2, 2 * KV_W), lambda j: (rev(j), 0)),
                                    pl.BlockSpec((HB // 2, 2 * KV_W), lambda j: (rev(j), 0))],
        out_specs=(pl.BlockSpec((HB, D_IN), lambda j: (rev(j), 0)), pl.BlockSpec((8, HG_W), lambda j: (0, 0)),
                   pl.BlockSpec((8, HD), lambda j: (0, 0))),
        scratch_shapes=[pltpu.VMEM((4, HD, HD), F32)],
        compiler_params=_params(dimension_semantics=("arbitrary",)),
    )(*[z] * 8, hgrn_lb, onorm, o_save, sprev, dymix, *dza)


XB = 512


def _xattn_fwd(q, k, v, t):
    tb = min(XB, t)

    def body(q_ref, k_ref, v_ref, o_ref):
        for h in range(XH):
            cols = slice(XD * h, XD * (h + 1))
            s = _dot(q_ref[:, cols], k_ref[:, cols], 1, 1) * (XD ** -0.5)
            p = jnp.exp(s - jnp.max(s, axis=-1, keepdims=True))
            l = jnp.sum(p, axis=-1, keepdims=True)
            o_ref[:, cols] = (_dot(p.astype(BF16), v_ref[:, cols], 1, 0) * (1.0 / l)).astype(BF16)

    row = pl.BlockSpec((tb, D), lambda i: (i, 0))
    mem = pl.BlockSpec(k.shape, lambda i: (0, 0))
    return pl.pallas_call(
        body, name="xattn_fwd", out_shape=jax.ShapeDtypeStruct((t, D), BF16), grid=(t // tb,),
        in_specs=[row, mem, mem], out_specs=row, compiler_params=_params(),
    )(q, k, v)


def _xattn_bwd(q, k, v, do, t):
    tb = min(XB, t)

    def body(q_ref, k_ref, v_ref, do_ref, dq_ref, dk_ref, dv_ref):
        @pl.when(pl.program_id(0) == 0)
        def _():
            dk_ref[...] = jnp.zeros_like(dk_ref)
            dv_ref[...] = jnp.zeros_like(dv_ref)

        for h in range(XH):
            cols = slice(XD * h, XD * (h + 1))
            qh, kh, vh, doh = q_ref[:, cols], k_ref[:, cols], v_ref[:, cols], do_ref[:, cols]
            s = _dot(qh, kh, 1, 1) * (XD ** -0.5)
            p = jnp.exp(s - jnp.max(s, axis=-1, keepdims=True))
            p = p * (1.0 / jnp.sum(p, axis=-1, keepdims=True))
            dp = _dot(doh, vh, 1, 1)
            ds = (p * (dp - jnp.sum(p * dp, axis=-1, keepdims=True)) * (XD ** -0.5)).astype(BF16)
            dq_ref[:, cols] = _dot(ds, kh, 1, 0).astype(BF16)
            dk_ref[:, cols] += _dot(ds, qh, 0, 0)
            dv_ref[:, cols] += _dot(p.astype(BF16), doh, 0, 0)

    row = pl.BlockSpec((tb, D), lambda i: (i, 0))
    mem = pl.BlockSpec(k.shape, lambda i: (0, 0))
    return pl.pallas_call(
        body, name="xattn_bwd",
        out_shape=(jax.ShapeDtypeStruct((t, D), BF16), jax.ShapeDtypeStruct(k.shape, F32),
                   jax.ShapeDtypeStruct(k.shape, F32)),
        grid=(t // tb,), in_specs=[row, mem, mem, row], out_specs=(row, mem, mem),
        compiler_params=_params(dimension_semantics=("arbitrary",)),
    )(q, k, v, do)


def _mem_kv(mem, g_mem, wk, wv):
    def body(m_ref, g_ref, wk_ref, wv_ref, mn_ref, k_ref, v_ref):
        m_ = m_ref[...]
        mn = (m_ * _rstd(m_) * g_ref[...]).astype(BF16)
        mn_ref[...] = mn
        k_ref[...] = _dot(mn, wk_ref[...], 1, 0).astype(BF16)
        v_ref[...] = _dot(mn, wv_ref[...], 1, 0).astype(BF16)

    return pl.pallas_call(body, name="mem_kv", out_shape=(jax.ShapeDtypeStruct(mem.shape, BF16),) * 3,
                          compiler_params=_params())(mem, g_mem, wk, wv)


def _mem_kv_bwd(mn, mem, dk, dv, wk, wv, dep=None):
    deps = [] if dep is None else [dep]

    def body(mn_ref, m_ref, dk_ref, dv_ref, wk_ref, wv_ref, *rest):
        gk_ref, gv_ref, dg_ref = rest[len(deps):]
        mn = mn_ref[...]
        dkb, dvb = dk_ref[...].astype(BF16), dv_ref[...].astype(BF16)
        gk_ref[...] = _dot(mn, dkb, 0, 0).astype(BF16)
        gv_ref[...] = _dot(mn, dvb, 0, 0).astype(BF16)
        dmn = _dot(dkb, wk_ref[...], 1, 1) + _dot(dvb, wv_ref[...], 1, 1)
        m_ = m_ref[...]
        dg_ref[...] = _rowsum8(dmn * (m_ * _rstd(m_)))

    vmem = pl.BlockSpec(memory_space=pltpu.VMEM)
    return pl.pallas_call(
        body, name="mem_kv_bwd",
        out_shape=(jax.ShapeDtypeStruct(wk.shape, BF16), jax.ShapeDtypeStruct(wv.shape, BF16),
                   jax.ShapeDtypeStruct((8, D), F32)),
        in_specs=[vmem] * 6 + [ANY_SPEC] * len(deps), out_specs=(vmem,) * 3, compiler_params=_params(),
    )(mn, mem, dk, dv, wk, wv, *deps)


FB = 256


def _ffn_fwd_loss(u, wgt, wut, wd, h, target, g_post, t):
    tb = min(FB, t)

    def body(u_ref, wg_ref, wu_ref, wd_ref, h_ref, t_ref, gp_ref, g_ref, up_ref, a_ref, sq_ref, dh_ref, dy_ref, dg_ref):
        @pl.when(pl.program_id(0) == 0)
        def _():
            sq_ref[...] = jnp.zeros_like(sq_ref)
            dg_ref[...] = jnp.zeros_like(dg_ref)

        u_ = u_ref[...]
        g = _dot(u_, wg_ref[...], 1, 1)
        up = _dot(u_, wu_ref[...], 1, 1)
        a = (g * _sig(g) * up).astype(BF16)
        g_ref[...] = g.astype(BF16)
        up_ref[...] = up.astype(BF16)
        a_ref[...] = a
        sq, dh, dy, dg = _ep_final_loss(_dot(a, wd_ref[...], 1, 0), h_ref[...], t_ref[...], gp_ref[...])
        sq_ref[...] += sq
        dh_ref[...] = dh
        dy_ref[...] = dy.astype(BF16)
        dg_ref[...] += dg

    row = lambda w: pl.BlockSpec((tb, w), lambda i: (i, 0))
    whole = lambda a: pl.BlockSpec(a.shape, lambda i: (0,) * a.ndim, pipeline_mode=pl.Buffered(1))
    acc = pl.BlockSpec((8, D), lambda i: (0, 0))
    wide = jax.ShapeDtypeStruct((t, D_FF), BF16)
    return pl.pallas_call(
        body, name="ffn_fwd_loss",
        out_shape=(wide, wide, wide, jax.ShapeDtypeStruct((8, D), F32), jax.ShapeDtypeStruct((t, D), F32),
                   jax.ShapeDtypeStruct((t, D), BF16), jax.ShapeDtypeStruct((8, D), F32)),
        grid=(t // tb,),
        in_specs=[row(D), whole(wgt), whole(wut), whole(wd), row(D), row(D), whole(g_post)],
        out_specs=(row(D_FF), row(D_FF), row(D_FF), acc, row(D), row(D), acc),
        compiler_params=_params(dimension_semantics=("arbitrary",)),
    )(u, wgt, wut, wd, h, target, g_post)


def _ffn_bwd(dy, wd, gate, up, wgt, wut, dh_out, hn, y, g_post, g_pre, wo, t, dep=None):
    tb = min(FB, t)
    deps = [] if dep is None else [dep]

    def body(dy_ref, wd_ref, g_ref, up_ref, wg_ref, wu_ref, dho_ref, hn_ref, y_ref, gp_ref, gn_ref, wo_ref, *rest):
        dg_ref, dup_ref, dh_ref, dyp_ref, do_ref, dgn_ref, dgp_ref = rest[len(deps):]

        @pl.when(pl.program_id(0) == 0)
        def _():
            dgn_ref[...] = jnp.zeros_like(dgn_ref)
            dgp_ref[...] = jnp.zeros_like(dgp_ref)

        da = _dot(dy_ref[...], wd_ref[...], 1, 1)
        g = g_ref[...].astype(F32)
        sg = _sig(g)
        dup = (da * g * sg).astype(BF16)
        dgate = (da * up_ref[...].astype(F32) * (sg * (1.0 + g * (1.0 - sg)))).astype(BF16)
        dup_ref[...] = dup
        dg_ref[...] = dgate
        du = _dot(dgate, wg_ref[...], 1, 0) + _dot(dup, wu_ref[...], 1, 0)
        dh, dyp, dgn, dgp = _ep_post_pre_bwd(du, dho_ref[...], hn_ref[...], y_ref[...], gp_ref[...], gn_ref[...])
        dh_ref[...] = dh
        dyp = dyp.astype(BF16)
        dyp_ref[...] = dyp
        do_ref[...] = _dot(dyp, wo_ref[...], 1, 1).astype(BF16)
        dgn_ref[...] += dgn
        dgp_ref[...] += dgp

    row = lambda w: pl.BlockSpec((tb, w), lambda i: (i, 0))
    whole = lambda a: pl.BlockSpec(a.shape, lambda i: (0,) * a.ndim, pipeline_mode=pl.Buffered(1))
    acc = pl.BlockSpec((8, D), lambda i: (0, 0))
    return pl.pallas_call(
        body, name="ffn_bwd",
        out_shape=(jax.ShapeDtypeStruct((t, D_FF), BF16), jax.ShapeDtypeStruct((t, D_FF), BF16),
                   jax.ShapeDtypeStruct((t, D), F32), jax.ShapeDtypeStruct((t, D), BF16),
                   jax.ShapeDtypeStruct((t, D), BF16), jax.ShapeDtypeStruct((8, D), F32),
                   jax.ShapeDtypeStruct((8, D), F32)),
        grid=(t // tb,),
        in_specs=[row(D), whole(wd), row(D_FF), row(D_FF), whole(wgt), whole(wut), row(D), row(D), row(D),
                  whole(g_post), whole(g_pre), whole(wo)] + [ANY_SPEC] * len(deps),
        out_specs=(row(D_FF), row(D_FF), row(D), row(D), row(D), acc, acc),
        compiler_params=_params(dimension_semantics=("arbitrary",)),
    )(dy, wd, gate, up, wgt, wut, dh_out, hn, y, g_post, g_pre, wo, *deps)


def _local_step(x, mem, target, fetch, sm, emit=None, first_dep=None, milestone=None):
    t = x.shape[0]
    w, gw = {}, {}

    def out(key, g):
        gw[key] = g
        return None if emit is None else emit(key, g)

    def tell(tag, value):
        return None if milestone is None else milestone(tag, value)
    u1 = _prenorm(x, sm["g_mix_pre"], name="prenorm_mix", dep=first_dep)
    w["winT"] = fetch("winT", u1)
    z = _mm(u1, w["winT"], tb=True, out_dtype=F32, tm=1024, tn=1408, name="mm_z", n_outer=True)
    ymix, lse = _swa_fwd(z, sm["sinks"], t, dep=tell("z", z))
    ymix, o_h, sprev = _hgrn2_fwd(z, sm["hgrn_lb"], sm["hgrn_onorm"], ymix, t, dep=tell("swa", lse))
    for key in ("wout", "wq", "wk", "wv"):
        w[key] = fetch(key, ymix)
    y1, h1, u2, qx = _mm_rows([(ymix, w["wout"], False)], [x], [sm["g_mix_post"], sm["g_x_pre"], w["wq"]],
                              _then(_ep_post_pre, 2, False), _EP_POST_PRE_OUTS + [ROW_BF16], tm=1024,
                              name="mm_y1_post_qx")
    mn, kx, vx = _mem_kv(mem, sm["g_mem"], w["wk"], w["wv"])
    ox = _xattn_fwd(qx, kx, vx, t)
    w["wo"] = fetch("wo", ox)
    y2, h2, u3 = _mm_rows([(ox, w["wo"], False)], [h1], [sm["g_x_post"], sm["g_ffn_pre"]], _ep_post_pre,
                          _EP_POST_PRE_OUTS, tm=1024, name="mm_y2_post", dep=tell("ox", ox))
    for key in ("wgT", "wuT", "wd"):
        w[key] = fetch(key, u3)
    gate, up, act, sq, dh3, dy3, dg_ffn_post = _ffn_fwd_loss(u3, w["wgT"], w["wuT"], w["wd"], h2, target,
                                                             sm["g_ffn_post"], t)
    dep = out("wd", _mm(act, dy3, ta=True, out_dtype=BF16, tm=1408, tn=1024, name="mm_gwd"))
    dgate, dup, dh2, dy2, dox, dg_ffn_pre, dg_x_post = _ffn_bwd(
        dy3, w["wd"], gate, up, w["wgT"], w["wuT"], dh3, h2, y2, sm["g_x_post"], sm["g_ffn_pre"], w["wo"], t, dep=dep)
    dep = out("wgT", _mm(dgate, u3, ta=True, out_dtype=BF16, tm=1408, tn=1024, name="mm_gwg"))
    dep = out("wuT", _mm(dup, u3, ta=True, out_dtype=BF16, tm=1408, tn=1024, name="mm_gwu", dep=dep))
    out("wo", _mm(ox, dy2, ta=True, out_dtype=BF16, tm=512, tn=1024, name="mm_gwo", dep=dep))
    dqx, dkx, dvx = _xattn_bwd(qx, kx, vx, dox, t)
    dep = out("wq", _mm(u2, dqx, ta=True, out_dtype=BF16, tm=512, tn=1024, name="mm_gwq"))
    gwk, gwv, dg_mem = _mem_kv_bwd(mn, mem, dkx, dvx, w["wk"], w["wv"], dep=dep)
    out("wk", gwk)
    dep = out("wv", gwv)
    dh1, dy1, dg_x_pre, dg_mix_post, dymix = _mm_rows(
        [(dqx, w["wq"], True)], [dh2, h1, y1], [sm["g_mix_post"], sm["g_x_pre"], w["wout"]],
        _then(_ep_post_pre_bwd, 1, True), _EP_POST_PRE_BWD_OUTS + [ROW_BF16], tm=512, name="mm_du2_post_bwd_dymix",
        dep=dep)
    dep = out("wout", _mm(ymix, dy1, ta=True, out_dtype=BF16, tm=512, tn=1024, name="mm_gwout"))
    *dza, dsinks = _swa_bwd(z, sm["sinks"], ymix, lse, dymix, t, dep=dep)
    dz, dlb, donorm = _hgrn2_bwd(z, sm["hgrn_lb"], sm["hgrn_onorm"], o_h, sprev, dymix, dza, t)
    dep = out("winT", _mm(dz, u1, ta=True, out_dtype=BF16, tm=1408, tn=1024, name="mm_gwin"))
    grad_x, dg_mix_pre = _mm_rows([(dz, w["winT"], False)], [dh1, x], [sm["g_mix_pre"]], _ep_pre_bwd,
                                  _EP_PRE_BWD_OUTS, tm=512, name="mm_du1_pre_bwd", dep=dep)
    parts = dict(g_mix_pre=dg_mix_pre, g_mix_post=dg_mix_post, g_mem=dg_mem, g_x_pre=dg_x_pre,
                 g_x_post=dg_x_post, g_ffn_pre=dg_ffn_pre, g_ffn_post=dg_ffn_post,
                 hgrn_onorm=donorm, hgrn_lb=dlb, sinks=dsinks, sq=sq)
    return grad_x, gw, parts


def _position():
    return lax.axis_index("x"), lax.axis_index("y"), lax.axis_index("c")


def _peer(pos, k):
    x, y, c = pos
    return (1 - x if k & 4 else x, 1 - y if k & 2 else y, 1 - c if k & 1 else c)


def _linear(pos):
    x, y, c = pos
    return 4 * x + 2 * y + c


HBM_SPEC = pl.BlockSpec(memory_space=pltpu.HBM)
SEM_SPEC = pl.BlockSpec(memory_space=pltpu.SEMAPHORE)
DATAFLOW = pltpu.SideEffectType.DATAFLOW_SIDE_EFFECTING
SEND_ORDER = (1, 2, 4, 3, 5, 6, 7)


def _in_hbm(a):
    return pltpu.with_memory_space_constraint(a, pltpu.HBM)


def _prepare_weights(shards, *, name, dep=None):
    n = len(shards)
    deps = [] if dep is None else [dep]

    def body(*refs):
        ins, (outs, lands, sem) = refs[:n], (refs[-2 * n - 1:-n - 1], refs[-n - 1:-1], refs[-1])
        me_lin = _linear(_position())
        copies = []
        for a in range(n):
            r = ins[a].shape[0]
            outs[a][...] = ins[a][...].astype(BF16)
            copies.append(pltpu.make_async_copy(outs[a], lands[a].at[pl.ds(me_lin * r, r), :], sem.at[a]))
            copies[-1].start()
        for cp in copies:
            cp.wait()

    vmem = pl.BlockSpec(memory_space=pltpu.VMEM)
    res = pl.pallas_call(
        body, name=name,
        out_shape=tuple(jax.ShapeDtypeStruct(s.shape, BF16) for s in shards)
        + tuple(jax.ShapeDtypeStruct((N_DEV * s.shape[0], s.shape[1]), BF16) for s in shards),
        in_specs=[vmem] * n + [ANY_SPEC] * len(deps), out_specs=tuple([vmem] * n + [ANY_SPEC] * n),
        scratch_shapes=[pltpu.SemaphoreType.DMA((n,))], compiler_params=_params(),
    )(*shards, *deps)
    return res[:n], res[n:]


def _copies_start(arrays, plan, n, *, name):
    na = len(arrays)

    def body(*refs):
        ins, send_sems, recv_sems = refs[:na], refs[na], refs[na + 1]
        me = _position()
        for j in range(n):
            src, dst, peer, _ = plan(ins, me, j)
            pltpu.make_async_remote_copy(src_ref=src, dst_ref=dst, send_sem=send_sems.at[j], recv_sem=recv_sems.at[j],
                                         device_id=peer, device_id_type=MESH).start()

    return pl.pallas_call(
        body, name=name,
        out_shape=(pltpu.SemaphoreType.DMA((n,)), pltpu.SemaphoreType.DMA((n,)))
        + tuple(pltpu.HBM(a.shape, a.dtype) for a in arrays),
        in_specs=(HBM_SPEC,) * na, out_specs=(SEM_SPEC, SEM_SPEC) + (HBM_SPEC,) * na,
        input_output_aliases={i: 2 + i for i in range(na)},
        compiler_params=pltpu.CompilerParams(has_side_effects=DATAFLOW),
    )(*[_in_hbm(a) for a in arrays])


def _copies_wait(send_sems, recv_sems, arrays, plan, n, after, *, name):
    na = len(arrays)

    def body(*refs):
        ins, send_sems, recv_sems = refs[:na], refs[na], refs[na + 1]
        me = _position()
        for j in range(n):
            src, _, peer, landed = plan(ins, me, j)
            copy = pltpu.make_async_remote_copy(src_ref=src, dst_ref=landed, send_sem=send_sems.at[j],
                                                recv_sem=recv_sems.at[j], device_id=peer, device_id_type=MESH)
            copy.wait_send()
            copy.wait_recv()

    return pl.pallas_call(
        body, name=name, out_shape=tuple(pltpu.HBM(a.shape, a.dtype) for a in arrays),
        in_specs=(HBM_SPEC,) * na + (SEM_SPEC, SEM_SPEC, ANY_SPEC), out_specs=(HBM_SPEC,) * na,
        input_output_aliases={i: i for i in range(na)},
        compiler_params=pltpu.CompilerParams(has_side_effects=DATAFLOW),
    )(*arrays, send_sems, recv_sems, after)


SAME_CORE = (2, 4, 6)


class _TwoLevelGather:
    def __init__(self, shards, lands, *, name):
        n = self.n = len(shards)
        self.name = name
        first_peers = (1,) + SAME_CORE

        def rows(ref, pos):
            r = ref.shape[0] // N_DEV
            return ref.at[pl.ds(_linear(pos) * r, r), :]

        def first(refs, me, j):
            a, peer = j // 4, _peer(me, first_peers[j % 4])
            return refs[a], rows(refs[n + a], me), peer, rows(refs[n + a], peer)

        def second(refs, me, j):
            a, sibling = j // 3, _peer(me, 1)
            mine = rows(refs[a], _peer(me, SAME_CORE[j % 3]))
            return mine, mine, sibling, rows(refs[a], _peer(sibling, SAME_CORE[j % 3]))

        self._first, self._second = first, second
        self._flight = _copies_start(list(shards) + list(lands), first, 4 * n, name=name + "_send")
        self.dep = self._flight[2]

    def pass_on(self, after):
        send1, recv1, *arrays = self._flight
        arrays = _copies_wait(send1, recv1, arrays, self._first, 4 * self.n, after, name=self.name + "_recv")
        self._flight = _copies_start(list(arrays[self.n:]), self._second, 3 * self.n, name=self.name + "_pass")
        return self._flight[2]

    def finish(self, after):
        send2, recv2, *lands = self._flight
        return _copies_wait(send2, recv2, lands, self._second, 3 * self.n, after, name=self.name + "_pass_recv")


def _exchange_start(gs, *, name):
    n = len(gs)
    rows = [g.shape[0] // N_DEV for g in gs]
    lands = [lax.empty((N_DEV - 1, r, g.shape[1]), g.dtype) for g, r in zip(gs, rows)]

    def body(*refs):
        g_refs, land_refs = refs[:n], refs[n:2 * n]
        send_sems, recv_sems = refs[2 * n:3 * n], refs[3 * n:4 * n]
        me = _position()
        for a in range(n):
            for k in SEND_ORDER:
                peer = _peer(me, k)
                pltpu.make_async_remote_copy(
                    src_ref=g_refs[a].at[pl.ds(_linear(peer) * rows[a], rows[a]), :],
                    dst_ref=land_refs[a].at[k - 1],
                    send_sem=send_sems[a].at[k - 1], recv_sem=recv_sems[a].at[k - 1],
                    device_id=peer, device_id_type=MESH).start()

    res = pl.pallas_call(
        body, name=name,
        out_shape=tuple(pltpu.SemaphoreType.DMA((N_DEV - 1,)) for _ in range(2 * n))
        + tuple(pltpu.HBM(a.shape, a.dtype) for a in gs + lands),
        in_specs=(HBM_SPEC,) * (2 * n), out_specs=(SEM_SPEC,) * (2 * n) + (HBM_SPEC,) * (2 * n),
        input_output_aliases={i: 2 * n + i for i in range(2 * n)},
        compiler_params=pltpu.CompilerParams(has_side_effects=DATAFLOW),
    )(*[_in_hbm(a) for a in gs + lands])
    return [(res[a], res[n + a], res[2 * n + a], res[3 * n + a]) for a in range(n)]


def _exchange_wait(send_sems, recv_sems, g_thru, land_thru, after, *, name):
    r = land_thru.shape[1]

    def body(g_ref, land_ref, send_sems, recv_sems, after_ref, g_dead, got_ref):
        del after_ref, g_dead, got_ref
        me = _position()
        for k in SEND_ORDER:
            peer = _peer(me, k)
            copy = pltpu.make_async_remote_copy(
                src_ref=g_ref.at[pl.ds(_linear(peer) * r, r), :], dst_ref=land_ref.at[k - 1],
                send_sem=send_sems.at[k - 1], recv_sem=recv_sems.at[k - 1],
                device_id=peer, device_id_type=MESH)
            copy.wait_send()
            copy.wait_recv()

    return pl.pallas_call(
        body, name=name,
        out_shape=(pltpu.HBM(g_thru.shape, g_thru.dtype), pltpu.HBM(land_thru.shape, land_thru.dtype)),
        in_specs=(HBM_SPEC, HBM_SPEC, SEM_SPEC, SEM_SPEC, pl.BlockSpec(memory_space=pl.ANY)),
        out_specs=(HBM_SPEC, HBM_SPEC), input_output_aliases={0: 0, 1: 1},
        compiler_params=pltpu.CompilerParams(has_side_effects=DATAFLOW),
    )(g_thru, land_thru, send_sems, recv_sems, after)


def _adamw_math(w, g, m, v):
    m = B1 * m + (1.0 - B1) * g
    v = B2 * v + (1.0 - B2) * (g * g)
    delta = -LR * ((m / C1) / (jnp.sqrt(v / C2) + AEPS) + WD * w)
    return delta, m, v


def _sum_adamw(items, *, name):
    n = len(items)

    def body(*refs):
        ins, outs, scratch = refs[:5 * n], refs[5 * n:9 * n], refs[9 * n:]
        me_lin = _linear(_position())
        mine = []
        for a in range(n):
            r = items[a][2].shape[0]
            mine.append(pltpu.make_async_copy(ins[5 * a].at[pl.ds(me_lin * r, r), :], scratch[a], scratch[n].at[a]))
            mine[-1].start()
        for a in range(n):
            _, land_ref, w_ref, m_ref, v_ref = ins[5 * a:5 * a + 5]
            g_ref, d_ref, nm_ref, nv_ref = outs[4 * a:4 * a + 4]
            g = land_ref[0].astype(F32)
            for s in range(1, N_DEV - 1):
                g = g + land_ref[s].astype(F32)
            mine[a].wait()
            g = scratch[a][...].astype(F32) + g
            g_ref[...] = g
            d_ref[...], nm_ref[...], nv_ref[...] = _adamw_math(w_ref[...], g, m_ref[...], v_ref[...])

    vmem = pl.BlockSpec(memory_space=pltpu.VMEM)
    res = pl.pallas_call(
        body, name=name,
        out_shape=tuple(jax.ShapeDtypeStruct(it[2].shape, F32) for it in items for _ in range(4)),
        in_specs=[ANY_SPEC, vmem, vmem, vmem, vmem] * n, out_specs=(vmem,) * (4 * n),
        scratch_shapes=[pltpu.VMEM(it[2].shape, BF16) for it in items] + [pltpu.SemaphoreType.DMA((n,))],
        compiler_params=_params(),
    )(*[a for it in items for a in it])
    return [res[4 * a:4 * a + 4] for a in range(n)]


SMALL = ("g_mix_pre", "g_mix_post", "g_mem", "g_x_pre", "g_x_post", "g_ffn_pre", "g_ffn_post",
         "hgrn_onorm", "hgrn_lb", "sinks")
SMALL_W = dict(hgrn_onorm=HD, hgrn_lb=HG_W, sinks=8)
SQ_ROW = len(SMALL)
PACK_ROWS = 16


def _small_pack(parts):
    ns = len(SMALL)

    def body(*refs):
        part, mine, slots, sem = refs[:ns + 1], refs[ns + 1], refs[ns + 2], refs[ns + 3]
        mine[...] = jnp.zeros((PACK_ROWS, D), F32)
        for r, name in enumerate(SMALL):
            wd = SMALL_W.get(name, D)
            mine[r:r + 1, 0:wd] = jnp.sum(part[r][...], axis=0, keepdims=True)[:, 0:wd]
        sq = jnp.sum(part[ns][...]) * (0.5 / D)
        mine[SQ_ROW:SQ_ROW + 1, :] = jnp.full((1, D), sq, F32)
        own = pltpu.make_async_copy(mine, slots.at[_linear(_position())], sem)
        own.start()
        own.wait()

    vmem = pl.BlockSpec(memory_space=pltpu.VMEM)
    return pl.pallas_call(
        body, name="small_pack",
        out_shape=(jax.ShapeDtypeStruct((PACK_ROWS, D), F32), jax.ShapeDtypeStruct((N_DEV, PACK_ROWS, D), F32)),
        in_specs=[vmem] * (ns + 1), out_specs=(vmem, ANY_SPEC),
        scratch_shapes=[pltpu.SemaphoreType.DMA(())], compiler_params=_params(),
    )(*[parts[n] for n in SMALL], parts["sq"])


def _small_exchange(mine, slots):
    def plan(refs, me, j):
        peer = _peer(me, j + 1)
        return refs[0], refs[1].at[_linear(me)], peer, refs[1].at[_linear(peer)]

    send, recv, mine1, slots1 = _copies_start([mine, slots], plan, N_DEV - 1, name="small_send")
    return lambda after: _copies_wait(send, recv, [mine1, slots1], plan, N_DEV - 1, after, name="small_recv")[1]


def _small_update(slots, sm, m_sm, v_sm):
    ns = len(SMALL)

    def body(*refs):
        tot = refs[0][0]
        for s in range(1, N_DEV):
            tot = tot + refs[0][s]
        w_refs, m_refs, v_refs = refs[1:ns + 1], refs[ns + 1:2 * ns + 1], refs[2 * ns + 1:3 * ns + 1]
        outs = refs[3 * ns + 1:]
        loss_ref = outs[0]
        g_out, d_out = outs[1:ns + 1], outs[ns + 1:2 * ns + 1]
        nm_out, nv_out = outs[2 * ns + 1:3 * ns + 1], outs[3 * ns + 1:4 * ns + 1]
        loss_ref[...] = tot[SQ_ROW:SQ_ROW + 1, 0:1]
        for r, name in enumerate(SMALL):
            wd = SMALL_W.get(name, D)
            g = tot[r:r + 1, 0:wd]
            w = w_refs[r][...]
            if name == "hgrn_lb":
                mx = jnp.maximum(w[0:1], w[1:2])
                e0, e1 = jnp.exp(w[0:1] - mx), jnp.exp(w[1:2] - mx)
                lb0 = e0 / (e0 + e1)
                g0 = g * lb0 * (1.0 - lb0)
                for i, gi in enumerate((g0, -g0)):
                    d, nm, nv = _adamw_math(w[i:i + 1], gi, m_refs[r][i:i + 1, :], v_refs[r][i:i + 1, :])
                    g_out[r][i:i + 1, :] = gi
                    d_out[r][i:i + 1, :], nm_out[r][i:i + 1, :], nv_out[r][i:i + 1, :] = d, nm, nv
            else:
                d, nm, nv = _adamw_math(w, g, m_refs[r][...], v_refs[r][...])
                g_out[r][...] = g
                d_out[r][...], nm_out[r][...], nv_out[r][...] = d, nm, nv

    shapes = [jax.ShapeDtypeStruct(sm[n].shape, F32) for n in SMALL]
    res = pl.pallas_call(
        body, name="small_update", out_shape=tuple([jax.ShapeDtypeStruct((1, 1), F32)] + shapes * 4),
        compiler_params=_params(),
    )(slots, *[sm[n] for n in SMALL], *[m_sm[n] for n in SMALL], *[v_sm[n] for n in SMALL])
    groups = [dict(zip(SMALL, res[1 + i * ns:1 + (i + 1) * ns])) for i in range(4)]
    return res[0], groups[0], groups[1], groups[2], groups[3]


BIG = ("w_in", "w_gate", "w_up", "w_down", "w_out", "wq_x", "wk_x", "wv_x", "wo_x")
BIG_KEY = dict(w_in="winT", w_gate="wgT", w_up="wuT", w_down="wd", w_out="wout", wq_x="wq", wk_x="wk",
               wv_x="wv", wo_x="wo")
TRANSPOSED = ("w_in", "w_gate", "w_up")
WEIGHTS = ("w_in", "sinks", "hgrn_lb", "hgrn_onorm", "w_out", "g_mix_pre", "g_mix_post", "g_mem", "g_x_pre",
           "g_x_post", "wq_x", "wk_x", "wv_x", "wo_x", "g_ffn_pre", "g_ffn_post", "w_gate", "w_up", "w_down")


def kernel(x, mem, w_in, sinks, hgrn_lb, hgrn_onorm, w_out, g_mix_pre, g_mix_post, g_mem, g_x_pre, g_x_post, wq_x, wk_x, wv_x, wo_x, g_ffn_pre, g_ffn_post, w_gate, w_up, w_down, loss_target, m_w_in, m_sinks, m_hgrn_lb, m_hgrn_onorm, m_w_out, m_g_mix_pre, m_g_mix_post, m_g_mem, m_g_x_pre, m_g_x_post, m_wq_x, m_wk_x, m_wv_x, m_wo_x, m_g_ffn_pre, m_g_ffn_post, m_w_gate, m_w_up, m_w_down, v_w_in, v_sinks, v_hgrn_lb, v_hgrn_onorm, v_w_out, v_g_mix_pre, v_g_mix_post, v_g_mem, v_g_x_pre, v_g_x_post, v_wq_x, v_wk_x, v_wv_x, v_wo_x, v_g_ffn_pre, v_g_ffn_post, v_w_gate, v_w_up, v_w_down):
    given = dict(locals())
    wts = {n: given[n] for n in WEIGHTS}
    ms = {n: given["m_" + n] for n in WEIGHTS}
    vs = {n: given["v_" + n] for n in WEIGHTS}

    def mat(a, name):
        a = a[0]
        return a.T if name in TRANSPOSED else a

    groups = (("w_in",), ("w_out", "wq_x", "wk_x", "wv_x", "wo_x"), ("w_gate", "w_up", "w_down"))
    gathers = []

    def start_group(g, dep):
        tag = ("w_in", "w_attn", "w_ffn")[g]
        shards, lands = _prepare_weights([mat(wts[n], n) for n in groups[g]], name="prepare_" + tag, dep=dep)
        gathers.append(_TwoLevelGather(shards, lands, name=tag))
        return gathers[-1].dep

    first_dep = start_group(1, start_group(0, None))
    name_of = {k: n for n, k in BIG_KEY.items()}
    gathered = {}

    def milestone(tag, value):
        if tag == "z":
            return start_group(2, value)
        return gathers[{"swa": 1, "ox": 2}[tag]].pass_on(value)

    def fetch(key, after):
        name = name_of[key]
        if name not in gathered:
            g = [i for i, group in enumerate(groups) if name in group][0]
            if g == 0:
                gathers[0].pass_on(after)
            gathered.update(zip(groups[g], gathers[g].finish(after)))
        return gathered[name]

    sm = {n: wts[n] for n in SMALL}
    started, held = {}, {}
    send_with = {k: group for group in (("wgT", "wuT"), ("wo", "wq", "wk", "wv")) for k in group}

    def emit(key, g):
        held[key] = g
        group = send_with.get(key, (key,))
        if key != group[-1]:
            return None
        flights = _exchange_start([held[k] for k in group], name="grad_send_" + name_of[group[0]])
        started.update({name_of[k]: f for k, f in zip(group, flights)})
        return flights[-1][2]

    grad_x, _, parts = _local_step(x[0], mem[0], loss_target[0], fetch, sm, emit, first_dep=first_dep, milestone=milestone)
    small_finish = _small_exchange(*_small_pack(parts))
    grads, deltas, new_m, new_v = {}, {}, {}, {}
    after = grad_x
    for group in (("w_down",), ("w_gate", "w_up"), ("wo_x", "wq_x", "wk_x", "wv_x", "w_out"), ("w_in",)):
        items = []
        for n in group:
            g_all, land = _exchange_wait(*started[n], after, name="grad_recv_" + n)
            items.append((g_all, land, mat(wts[n], n), mat(ms[n], n), mat(vs[n], n)))
            after = land
        for n, res in zip(group, _sum_adamw(items, name="adamw_" + group[0])):
            after = res[1]
            if n in TRANSPOSED:
                res = [a.T for a in res]
            grads[n], deltas[n], new_m[n], new_v[n] = [a[None] for a in res]
    loss, g_s, d_s, m_s, v_s = _small_update(small_finish(after), sm, {n: ms[n] for n in SMALL},
                                             {n: vs[n] for n in SMALL})
    grads.update(g_s), deltas.update(d_s), new_m.update(m_s), new_v.update(v_s)
    return (loss[0, 0], grad_x[None], *[grads[n] for n in WEIGHTS], *[deltas[n] for n in WEIGHTS],
            *[new_m[n] for n in WEIGHTS], *[new_v[n] for n in WEIGHTS])
```

```python
import functools

import jax
import jax.numpy as jnp
from jax import lax
from jax.experimental import pallas as pl
from jax.experimental.pallas import tpu as pltpu

F32 = jnp.float32
BF16 = jnp.bfloat16

D = 1024
D_IN = 2816
D_FF = 2816
CHUNK = 64
SWA_W = 512
KV_W = 128
HG_W = 512
HD = 128
ZQH, ZFH, ZIH, ZGH = 768, 1280, 1792, 2304
XH, XD = 4, 256
EPS = 1e-6
NEG = -1e30
N_DEV = 8
MESH = pl.DeviceIdType.MESH

LR, B1, B2, AEPS, WD, STEP = 0.001, 0.9, 0.999, 1e-08, 0.01, 10
C1 = 1.0 - B1 ** STEP
C2 = 1.0 - B2 ** STEP

VMEM_LIMIT = 56 * 1024 * 1024


def _params(**kw):
    return pltpu.CompilerParams(vmem_limit_bytes=VMEM_LIMIT, **kw)


def _sig(x):
    return 1.0 / (1.0 + jnp.exp(-x))


def _rowsum8(x):
    r, w = x.shape
    return jnp.sum(x.reshape(r // 8, 8, w), axis=0)


def _dot(a, b, ca, cb, precision=None):
    return lax.dot_general(a, b, (((ca,), (cb,)), ((), ())), preferred_element_type=F32,
                           precision=precision)


ANY_SPEC = pl.BlockSpec(memory_space=pl.ANY)


def _mm(a, b, *, ta=False, tb=False, out_dtype, tm, tn, tk=None, name, dep=None, n_outer=False):
    m = a.shape[1] if ta else a.shape[0]
    k = a.shape[0] if ta else a.shape[1]
    n = b.shape[0] if tb else b.shape[1]
    tm, tn = min(tm, m), min(tn, n)
    tk = k if tk is None else min(tk, k)
    nk = k // tk
    assert m % tm == 0 and n % tn == 0 and k % tk == 0, (name, m, n, k, tm, tn, tk)
    ij = (lambda g0, g1: (g1, g0)) if n_outer else (lambda g0, g1: (g0, g1))
    a_spec = (pl.BlockSpec((tk, tm), lambda g0, g1, kk: (kk, ij(g0, g1)[0])) if ta
              else pl.BlockSpec((tm, tk), lambda g0, g1, kk: (ij(g0, g1)[0], kk)))
    b_spec = (pl.BlockSpec((tn, tk), lambda g0, g1, kk: (ij(g0, g1)[1], kk)) if tb
              else pl.BlockSpec((tk, tn), lambda g0, g1, kk: (kk, ij(g0, g1)[1])))
    ca, cb = (0 if ta else 1), (1 if tb else 0)

    deps = [] if dep is None else [dep]

    def body(a_ref, b_ref, *rest):
        o_ref, acc = rest[len(deps)], rest[len(deps) + 1:]
        p = _dot(a_ref[...].astype(BF16), b_ref[...].astype(BF16), ca, cb)
        if nk == 1:
            o_ref[...] = p.astype(out_dtype)
        else:
            acc_ref, = acc
            kk = pl.program_id(2)

            @pl.when(kk == 0)
            def _():
                acc_ref[...] = p

            @pl.when(kk > 0)
            def _():
                acc_ref[...] += p

            @pl.when(kk == nk - 1)
            def _():
                o_ref[...] = acc_ref[...].astype(out_dtype)

    return pl.pallas_call(
        body, name=name, out_shape=jax.ShapeDtypeStruct((m, n), out_dtype),
        grid=(n // tn, m // tm, nk) if n_outer else (m // tm, n // tn, nk),
        in_specs=[a_spec, b_spec] + [ANY_SPEC] * len(deps),
        out_specs=pl.BlockSpec((tm, tn), lambda g0, g1, kk: ij(g0, g1)),
        scratch_shapes=[pltpu.VMEM((tm, tn), F32)] if nk > 1 else [],
        compiler_params=_params(dimension_semantics=("parallel", "parallel", "arbitrary")),
    )(a, b, *deps)


def _mm_rows(prods, rows_in, vecs_in, epilogue, outs, *, tm, name, dep=None):
    m = prods[0][0].shape[0]
    n = prods[0][1].shape[0] if prods[0][2] else prods[0][1].shape[1]
    tm = min(tm, m)
    assert m % tm == 0
    deps = [] if dep is None else [dep]
    n_p, n_r, n_v = len(prods), len(rows_in), len(vecs_in)

    def body(*refs):
        ab = refs[:2 * n_p]
        row_refs = refs[2 * n_p:2 * n_p + n_r]
        vec_refs = refs[2 * n_p + n_r:2 * n_p + n_r + n_v]
        out_refs = refs[2 * n_p + n_r + n_v + len(deps):]
        p = None
        for j, (_, _, tb) in enumerate(prods):
            t = _dot(ab[2 * j][...].astype(BF16), ab[2 * j + 1][...], 1, 1 if tb else 0)
            p = t if p is None else p + t
        vals = epilogue(p, *[r[...] for r in row_refs], *[v[...] for v in vec_refs])
        for (dtype, kind), o_ref, val in zip(outs, out_refs, vals):
            if kind == "row":
                o_ref[...] = val.astype(dtype)
            else:
                @pl.when(pl.program_id(0) == 0)
                def _(o_ref=o_ref):
                    o_ref[...] = jnp.zeros_like(o_ref)

                o_ref[...] += val

    row = lambda w: pl.BlockSpec((tm, w), lambda i: (i, 0))
    whole = lambda a: pl.BlockSpec(a.shape, lambda i: (0,) * a.ndim, pipeline_mode=pl.Buffered(1))
    in_specs, args = [], []
    for a, b, _ in prods:
        in_specs += [row(a.shape[1]), whole(b)]
        args += [a, b]
    in_specs += [row(r.shape[1]) for r in rows_in] + [whole(v) for v in vecs_in] + [ANY_SPEC] * len(deps)
    return pl.pallas_call(
        body, name=name,
        out_shape=tuple(jax.ShapeDtypeStruct((m, n) if kind == "row" else (8, n), dtype) for dtype, kind in outs),
        grid=(m // tm,), in_specs=in_specs,
        out_specs=tuple(row(n) if kind == "row" else pl.BlockSpec((8, n), lambda i: (0, 0)) for _, kind in outs),
        compiler_params=_params(dimension_semantics=("arbitrary",)),
    )(*args, *rows_in, *vecs_in, *deps)


def _rstd(x):
    return lax.rsqrt(jnp.mean(x * x, axis=-1, keepdims=True) + EPS)


def _norm_bwd(xh, r, t):
    return r * (t - xh * jnp.mean(xh * t, axis=-1, keepdims=True))


ROW_F32, ROW_BF16, SUM_F32 = (F32, "row"), (BF16, "row"), (F32, "sum")


def _then(epilogue, index, tb):
    def run(p, *args):
        vals = epilogue(p, *args[:-1])
        return (*vals, _dot(vals[index].astype(BF16), args[-1], 1, 1 if tb else 0))

    return run


def _ep_post_pre(p, h, g_post, g_pre):
    y = p.astype(BF16)
    yf = y.astype(F32)
    hn = h + yf * _rstd(yf) * g_post
    return y, hn, hn * _rstd(hn) * g_pre


_EP_POST_PRE_OUTS = [ROW_BF16, ROW_F32, ROW_BF16]


def _ep_final_loss(y, h, target, g_post):
    r = _rstd(y)
    yh = y * r
    err = h + yh * g_post - target
    dh = err * (1.0 / D)
    return _rowsum8(err * err), dh, _norm_bwd(yh, r, dh * g_post), _rowsum8(dh * yh)


def _ep_post_pre_bwd(du, dh_out, hn, y, g_post, g_pre):
    r2 = _rstd(hn)
    xh = hn * r2
    dh = dh_out + _norm_bwd(xh, r2, du * g_pre)
    yf = y.astype(F32)
    r1 = _rstd(yf)
    yh = yf * r1
    return dh, _norm_bwd(yh, r1, dh * g_post), _rowsum8(du * xh), _rowsum8(dh * yh)


_EP_POST_PRE_BWD_OUTS = [ROW_F32, ROW_BF16, SUM_F32, SUM_F32]


def _ep_pre_bwd(du, dh_out, x, g):
    r = _rstd(x)
    xh = x * r
    return dh_out + _norm_bwd(xh, r, du * g), _rowsum8(du * xh)


_EP_PRE_BWD_OUTS = [ROW_F32, SUM_F32]


def _prenorm(x, g, *, name, dep=None):
    t, d = x.shape
    tb = min(512, t)
    deps = [] if dep is None else [dep]

    def body(x_ref, g_ref, *rest):
        xf = x_ref[...]
        rest[-1][...] = (xf * _rstd(xf) * g_ref[...]).astype(BF16)

    return pl.pallas_call(
        body, name=name, out_shape=jax.ShapeDtypeStruct((t, d), BF16), grid=(t // tb,),
        in_specs=[pl.BlockSpec((tb, d), lambda i: (i, 0)), pl.BlockSpec((1, d), lambda i: (0, 0))]
        + [ANY_SPEC] * len(deps),
        out_specs=pl.BlockSpec((tb, d), lambda i: (i, 0)), compiler_params=_params(),
    )(x, g, *deps)


QB = 256


def _half_mask(shape, e):
    lane = lax.broadcasted_iota(jnp.int32, shape, len(shape) - 1)
    return (lane // 64) == e


def _place(kv):
    sw = pltpu.roll(kv, 64, 1)
    m0 = _half_mask(kv.shape, 0)
    return [[jnp.where(m0, kv, 0.0).astype(BF16), jnp.where(m0, 0.0, sw).astype(BF16)],
            [jnp.where(m0, sw, 0.0).astype(BF16), jnp.where(m0, 0.0, kv).astype(BF16)]]


SQ = 128
SK = 256


def _swa_valid(i, sb):
    qc = lax.broadcasted_iota(jnp.int32, (SQ, SK), 0) // CHUNK
    kc = lax.broadcasted_iota(jnp.int32, (SQ, SK), 1) // CHUNK - 2
    return (kc <= qc) & (qc <= kc + 2) & (4 * i + 2 * sb + kc >= 0)


def _swa_fwd(z, sinks, t, dep=None):
    nb = t // QB
    deps = [] if dep is None else [dep]

    def body(s_ref, q_ref, kp_ref, kc_ref, vp_ref, vc_ref, *rest):
        o_ref, lse_ref = rest[-2:]
        i = pl.program_id(0)
        kpl = _place(jnp.concatenate([kp_ref[...], kc_ref[...]], axis=0))
        vpl = _place(jnp.concatenate([vp_ref[...], vc_ref[...]], axis=0))
        lane = lax.broadcasted_iota(jnp.int32, (SQ, 128), 1)
        for sb in range(QB // SQ):
            rows, keys = slice(SQ * sb, SQ * (sb + 1)), slice(SQ * sb, SQ * sb + SK)
            valid = _swa_valid(i, sb)
            lse_out = jnp.zeros((SQ, 128), F32)
            for j in range(4):
                qp = q_ref[rows, 128 * j:128 * (j + 1)].astype(BF16)
                acc = jnp.zeros((SQ, 128), F32)
                for e in range(2):
                    h = 2 * j + e
                    kvh = h // 4
                    qm = jnp.where(_half_mask(qp.shape, e), qp, jnp.zeros_like(qp))
                    s = _dot(qm, kpl[kvh][e][keys], 1, 1) * 0.125
                    s = jnp.where(valid, s, NEG)
                    sink = s_ref[0, h]
                    m = jnp.maximum(jnp.max(s, axis=-1, keepdims=True), sink)
                    p = jnp.exp(s - m)
                    l = jnp.sum(p, axis=-1, keepdims=True) + jnp.exp(sink - m)
                    acc = acc + _dot(p.astype(BF16), vpl[kvh][e][keys], 1, 0) * (1.0 / l)
                    lse_out = jnp.where(lane == h, m + jnp.log(l), lse_out)
                o_ref[rows, 128 * j:128 * (j + 1)] = acc.astype(BF16)
            lse_ref[rows, :] = lse_out

    prev = lambda c: pl.BlockSpec((128, 128), lambda i: (jnp.maximum(2 * i - 1, 0), c))
    cur = lambda c: pl.BlockSpec((QB, 128), lambda i: (i, c))
    return pl.pallas_call(
        body, name="swa_fwd",
        out_shape=(jax.ShapeDtypeStruct((t, D), BF16), jax.ShapeDtypeStruct((t, 128), F32)),
        grid=(nb,),
        in_specs=[pl.BlockSpec(memory_space=pltpu.SMEM),
                  pl.BlockSpec((QB, SWA_W), lambda i: (i, 0)), prev(4), cur(4), prev(5), cur(5)]
        + [ANY_SPEC] * len(deps),
        out_specs=(pl.BlockSpec((QB, SWA_W), lambda i: (i, 0)), pl.BlockSpec((QB, 128), lambda i: (i, 0))),
        compiler_params=_params(),
    )(sinks, z, z, z, z, z, *deps)


def _swa_bwd(z, sinks, ymix, lse, dymix, t, dep=None):
    nb = t // QB
    deps = [] if dep is None else [dep]

    def body(s_ref, q_ref, kp_ref, kc_ref, vp_ref, vc_ref, o_ref, do_ref, l_ref, *rest):
        dq_ref, first_ref, second_ref, ds_ref, carry_ref = rest[len(deps):]
        i = pl.program_id(0)
        live = i < nb

        @pl.when(i == 0)
        def _():
            ds_ref[...] = jnp.zeros_like(ds_ref)
            carry_ref[...] = jnp.zeros_like(carry_ref)

        lane = lax.broadcasted_iota(jnp.int32, (8, 128), 1)
        kpl = _place(jnp.concatenate([kp_ref[...], kc_ref[...]], axis=0))
        vpl = _place(jnp.concatenate([vp_ref[...], vc_ref[...]], axis=0))
        nk = QB + 128
        qc = lax.broadcasted_iota(jnp.int32, (QB, nk), 0) // CHUNK
        kc = lax.broadcasted_iota(jnp.int32, (QB, nk), 1) // CHUNK - 2
        valid = (kc <= qc) & (qc <= kc + 2) & (4 * i + kc >= 0) & live
        lse_c = l_ref[...]
        dsink = jnp.zeros((8, 128), F32)
        dk_acc = [[jnp.zeros((128, nk), F32) for _ in range(2)] for _ in range(2)]
        dv_acc = [[jnp.zeros((128, nk), F32) for _ in range(2)] for _ in range(2)]
        dq = []
        for j in range(4):
            cols = slice(128 * j, 128 * (j + 1))
            qp = q_ref[:, cols].astype(BF16)
            dop = do_ref[:, cols]
            prod = dop.astype(F32) * o_ref[:, cols].astype(F32)
            acc = jnp.zeros((QB, 128), F32)
            for e in range(2):
                h = 2 * j + e
                kvh = h // 4
                hm = _half_mask(qp.shape, e)
                qm = jnp.where(hm, qp, jnp.zeros_like(qp))
                dom = jnp.where(hm, dop, jnp.zeros_like(dop))
                dd = jnp.sum(jnp.where(hm, prod, 0.0), axis=-1, keepdims=True)
                lse_h = lse_c[:, h:h + 1]
                s = _dot(qm, kpl[kvh][e], 1, 1) * 0.125
                p = jnp.where(valid, jnp.exp(s - lse_h), 0.0)
                dp = _dot(dom, vpl[kvh][e], 1, 1)
                ds = (p * (dp - dd) * 0.125).astype(BF16)
                acc = acc + _dot(ds, kpl[kvh][e], 1, 0)
                dk_acc[kvh][e] = dk_acc[kvh][e] + _dot(qm, ds, 0, 0)
                dv_acc[kvh][e] = dv_acc[kvh][e] + _dot(dom, p.astype(BF16), 0, 0)
                ps = jnp.where(live, jnp.exp(s_ref[0, h] - lse_h) * dd, 0.0)
                dsink = dsink - jnp.where(lane == h, _rowsum8(jnp.broadcast_to(ps, (QB, 128))), 0.0)
            dq.append(acc.astype(BF16))
        ds_ref[...] += dsink
        dk = (dk_acc[0][0] + dk_acc[1][1] + pltpu.roll(dk_acc[0][1] + dk_acc[1][0], 64, 0)).T
        dv = (dv_acc[0][0] + dv_acc[1][1] + pltpu.roll(dv_acc[0][1] + dv_acc[1][0], 64, 0)).T
        dkv = jnp.concatenate([dk, dv], axis=1)
        second_ref[...] = (carry_ref[...] + dkv[0:128]).astype(BF16)
        carry_ref[...] = dkv[256:384]

        @pl.when(live)
        def _():
            for j in range(4):
                dq_ref[:, 128 * j:128 * (j + 1)] = dq[j]
            first_ref[...] = dkv[128:256].astype(BF16)

    blk = lambda i: jnp.minimum(i, nb - 1)
    prev = lambda c: pl.BlockSpec((128, 128), lambda i: (jnp.maximum(2 * blk(i) - 1, 0), c))
    cur = lambda w, c: pl.BlockSpec((QB, w), lambda i: (blk(i), c))
    half = lambda index: pl.BlockSpec((128, 256), lambda i: (index(i), 0))
    return pl.pallas_call(
        body, name="swa_bwd",
        out_shape=(jax.ShapeDtypeStruct((t, SWA_W), BF16), jax.ShapeDtypeStruct((t // 2, 256), BF16),
                   jax.ShapeDtypeStruct((t // 2, 256), BF16), jax.ShapeDtypeStruct((8, 128), F32)),
        grid=(nb + 1,),
        in_specs=[pl.BlockSpec(memory_space=pltpu.SMEM),
                  cur(SWA_W, 0), prev(4), cur(128, 4), prev(5), cur(128, 5),
                  cur(SWA_W, 0), cur(SWA_W, 0), cur(128, 0)] + [ANY_SPEC] * len(deps),
        out_specs=(cur(SWA_W, 0), half(blk), half(lambda i: jnp.maximum(i - 1, 0)),
                   pl.BlockSpec((8, 128), lambda i: (0, 0))),
        scratch_shapes=[pltpu.VMEM((128, 256), F32)],
        compiler_params=_params(dimension_semantics=("arbitrary",)),
    )(sinks, z, z, z, z, z, ymix, dymix, lse, *deps)


HB = 256


def _lower_bound(lb_ref):
    a = lb_ref[...]
    a0, a1 = a[0:1], a[1:2]
    mx = jnp.maximum(a0, a1)
    e0, e1 = jnp.exp(a0 - mx), jnp.exp(a1 - mx)
    return e0 / (e0 + e1)


def _hgrn_cols(row_block):
    return [pl.BlockSpec((HB, 2 * HD), lambda j, c=base // (2 * HD) + p: (row_block(j), c))
            for base in (ZQH, ZFH, ZIH, ZGH) for p in range(2)]


NCH = HB // CHUNK


def _split3(x):
    hi = x.astype(BF16)
    r1 = x - hi.astype(F32)
    mid = r1.astype(BF16)
    return hi, mid, (r1 - mid.astype(F32)).astype(BF16)


def _blockdiag(lower):
    r = lax.broadcasted_iota(jnp.int32, (HB, HB), 0)
    c = lax.broadcasted_iota(jnp.int32, (HB, HB), 1)
    return (r // CHUNK == c // CHUNK) & ((c <= r) if lower else (c >= r))


def _chunk_sums(mask_bf16, x):
    return sum(_dot(mask_bf16, part, 1, 0) for part in _split3(x))


def _per_chunk_rows(x, row):
    w = x.shape[1]
    picked = x.reshape(NCH, CHUNK, w)[:, row:row + 1, :]
    return jnp.broadcast_to(picked, (NCH, CHUNK, w)).reshape(HB, w)


def _chunk_stack(x, chunk_of_row):
    return jnp.concatenate([jnp.where(chunk_of_row == c, x, jnp.zeros_like(x)) for c in range(NCH)], axis=1)


def _chunk_pick(x, chunk_of_row):
    w = x.shape[1] // NCH
    out = jnp.zeros((HB, w), x.dtype)
    for c in range(NCH):
        out = jnp.where(chunk_of_row == c, x[:, c * w:(c + 1) * w], out)
    return out


def _hgrn_local(q, f, kf, b):
    sq = _sig(q)
    qf = q * sq * (HD ** -0.5)
    b_mid = _per_chunk_rows(b, CHUNK // 2 - 1)
    b_last = _per_chunk_rows(b, CHUNK - 1)
    qm = qf * jnp.exp(b - b_mid)
    km = kf * jnp.exp(b_mid - b)
    kl = kf * jnp.exp(b_last - b)
    qb = qf * jnp.exp(b)
    return dict(sq=sq, b_mid=b_mid, b_last=b_last, qm=qm, km=km, kl=kl, qb=qb)


def _hgrn2_fwd(z, hgrn_lb, onorm, ymix, t, dep=None):
    nb = t // HB
    deps = [] if dep is None else [dep]

    def body(*refs):
        zq, zf, zi, zg = refs[0:2], refs[2:4], refs[4:6], refs[6:8]
        (lb_ref, on_ref), (y_ref, o_ref, sp_ref, st_ref) = refs[8:10], refs[-4:]

        @pl.when(pl.program_id(0) == 0)
        def _():
            st_ref[...] = jnp.zeros_like(st_ref)

        lb_all = _lower_bound(lb_ref)
        gn = on_ref[...]
        low = _blockdiag(True)
        low_b = low.astype(BF16)
        chunk_of_row = lax.broadcasted_iota(jnp.int32, (HB, HD), 0) // CHUNK
        for p in range(2):
            lbp = lb_all[:, 2 * HD * p:2 * HD * (p + 1)]
            fp = lbp + (1.0 - lbp) * _sig(zf[p][...])
            bp = _chunk_sums(low_b, jnp.log(fp))
            for e in range(2):
                h, ls = 2 * p + e, slice(e * HD, (e + 1) * HD)
                f = fp[:, ls]
                w = _hgrn_local(zq[p][:, ls], f, 1.0 - f, bp[:, ls])
                iv = zi[p][:, ls].astype(BF16)
                a = jnp.where(low, _dot(w["qm"].astype(BF16), w["km"].astype(BF16), 1, 1), 0.0)
                o = _dot(a.astype(BF16), iv, 1, 0)
                u = _dot(iv, _chunk_stack(w["kl"].astype(BF16), chunk_of_row), 0, 0)
                decay = jnp.exp(w["b_last"])
                st = st_ref[h]
                states = []
                for c in range(NCH):
                    sp_ref[h, c] = st
                    states.append(st.astype(BF16))
                    st = st * decay[c * CHUNK:c * CHUNK + 1] + u[:, c * HD:(c + 1) * HD]
                st_ref[h] = st
                inter = _dot(w["qb"].astype(BF16), jnp.concatenate(states, axis=0), 1, 1)
                o = o + _chunk_pick(inter, chunk_of_row)
                hs = slice(h * HD, (h + 1) * HD)
                o_ref[:, hs] = o
                gg = zg[p][:, ls]
                y_ref[:, hs] = (o * _rstd(o) * gn * (gg * _sig(gg))).astype(BF16)

    return pl.pallas_call(
        body, name="hgrn_fwd",
        out_shape=(jax.ShapeDtypeStruct((t, D), BF16), jax.ShapeDtypeStruct((t, HG_W), F32),
                   jax.ShapeDtypeStruct((4, t // CHUNK, HD, HD), F32)),
        grid=(nb,),
        in_specs=_hgrn_cols(lambda j: j) + [pl.BlockSpec((2, HG_W), lambda j: (0, 0)),
                                            pl.BlockSpec((1, HD), lambda j: (0, 0)), ANY_SPEC]
        + [ANY_SPEC] * len(deps),
        out_specs=(pl.BlockSpec((HB, HG_W), lambda j: (j, 1)),
                   pl.BlockSpec((HB, HG_W), lambda j: (j, 0)),
                   pl.BlockSpec((4, NCH, HD, HD), lambda j: (0, j, 0, 0))),
        scratch_shapes=[pltpu.VMEM((4, HD, HD), F32)],
        input_output_aliases={10: 0},
        compiler_params=_params(dimension_semantics=("arbitrary",)),
    )(*[z] * 8, hgrn_lb, onorm, ymix, *deps)


def _hgrn2_bwd(z, hgrn_lb, onorm, o_save, sprev, dymix, dza, t):
    nb = t // HB

    def body(*refs):
        zq, zf, zi, zg = refs[0:2], refs[2:4], refs[4:6], refs[6:8]
        (lb_ref, on_ref, o_ref, sp_ref, dy_ref, dqa_ref, first_ref, second_ref,
         dz_ref, dlb_ref, don_ref, dst_ref) = refs[8:]

        @pl.when(pl.program_id(0) == 0)
        def _():
            dst_ref[...] = jnp.zeros_like(dst_ref)
            dlb_ref[...] = jnp.zeros_like(dlb_ref)
            don_ref[...] = jnp.zeros_like(don_ref)

        dz_ref[:, 0:SWA_W] = dqa_ref[...]
        dz_ref[0:HB // 2, SWA_W:ZQH] = first_ref[...]
        dz_ref[HB // 2:HB, SWA_W:ZQH] = second_ref[...]
        lb_all = _lower_bound(lb_ref)
        gn = on_ref[...]
        low, upp = _blockdiag(True), _blockdiag(False)
        upp_b = upp.astype(BF16)
        low_b = low.astype(BF16)
        row = lax.broadcasted_iota(jnp.int32, (HB, HD), 0)
        chunk_of_row = row // CHUNK
        in_chunk = row % CHUNK
        for p in range(2):
            lbp = lb_all[:, 2 * HD * p:2 * HD * (p + 1)]
            sgp = _sig(zf[p][...])
            fp = lbp + (1.0 - lbp) * sgp
            bp = _chunk_sums(low_b, jnp.log(fp))
            db_pair, dkf_pair = [], []
            for e in range(2):
                h, ls, hs = 2 * p + e, slice(e * HD, (e + 1) * HD), slice((2 * p + e) * HD, (2 * p + e + 1) * HD)
                f = fp[:, ls]
                q = zq[p][:, ls]
                w = _hgrn_local(q, f, 1.0 - f, bp[:, ls])
                iv = zi[p][:, ls].astype(BF16)
                gg = zg[p][:, ls]
                o = o_ref[:, hs]
                dout = dy_ref[:, hs].astype(F32)
                sgg = _sig(gg)
                r = _rstd(o)
                oh = o * r
                dyn = dout * (gg * sgg)
                dz_ref[:, ZGH + h * HD:ZGH + (h + 1) * HD] = (
                    dout * oh * gn * (sgg * (1.0 + gg * (1.0 - sgg)))).astype(BF16)
                don_ref[...] += _rowsum8(dyn * oh)
                do = _norm_bwd(oh, r, dyn * gn).astype(BF16)
                qm, km, kl, qb = (w[n].astype(BF16) for n in ("qm", "km", "kl", "qb"))
                decay = jnp.exp(w["b_last"])
                grads_in = _dot(do, _chunk_stack(qb, chunk_of_row), 0, 0)
                dst = dst_ref[h]
                dstn, dd_rows = [None] * NCH, [None] * NCH
                for c in reversed(range(NCH)):
                    dstn[c] = dst.astype(BF16)
                    dd_rows[c] = jnp.sum(dst * sp_ref[h, c], axis=0, keepdims=True)
                    dst = dst * decay[c * CHUNK:c * CHUNK + 1] + grads_in[:, c * HD:(c + 1) * HD]
                dst_ref[h] = dst
                states = jnp.concatenate([sp_ref[h, c].astype(BF16) for c in range(NCH)], axis=0)
                dstn_all = jnp.concatenate(dstn, axis=0)
                dqb = _dot(_chunk_stack(do, chunk_of_row), states, 1, 0)
                at = jnp.where(upp, _dot(km, qm, 1, 1), 0.0)
                di = _dot(at.astype(BF16), do, 1, 0) + _chunk_pick(_dot(kl, dstn_all, 1, 1), chunk_of_row)
                dz_ref[:, ZIH + h * HD:ZIH + (h + 1) * HD] = di.astype(BF16)
                dkl = _dot(_chunk_stack(iv, chunk_of_row), dstn_all, 1, 0)
                da = jnp.where(low, _dot(do, iv, 1, 1), 0.0).astype(BF16)
                dat = jnp.where(upp, _dot(iv, do, 1, 1), 0.0).astype(BF16)
                dqm = _dot(da, km, 1, 0)
                dkm = _dot(dat, qm, 1, 0)
                b = bp[:, ls]
                e1, e2 = jnp.exp(b - w["b_mid"]), jnp.exp(w["b_mid"] - b)
                e3, e4 = jnp.exp(w["b_last"] - b), jnp.exp(b)
                dqf = dqm * e1 + dqb * e4
                dkf_pair.append(dkm * e2 + dkl * e3)
                t_qm, t_km, t_kl = dqm * w["qm"], dkm * w["km"], dkl * w["kl"]
                db = t_qm - t_km - t_kl + dqb * w["qb"]
                db_mid = jnp.sum((t_km - t_qm).reshape(NCH, CHUNK, HD), axis=1, keepdims=True)
                db_last = jnp.sum(t_kl.reshape(NCH, CHUNK, HD), axis=1, keepdims=True)
                db_last = db_last + jnp.stack(dd_rows, axis=0) * jnp.exp(
                    bp[:, ls].reshape(NCH, CHUNK, HD)[:, CHUNK - 1:CHUNK, :])
                spread = lambda v: jnp.broadcast_to(v, (NCH, CHUNK, HD)).reshape(HB, HD)
                db = (db + jnp.where(in_chunk == CHUNK // 2 - 1, spread(db_mid), 0.0)
                      + jnp.where(in_chunk == CHUNK - 1, spread(db_last), 0.0))
                db_pair.append(db)
                sq = w["sq"]
                dz_ref[:, ZQH + h * HD:ZQH + (h + 1) * HD] = (
                    dqf * (HD ** -0.5) * (sq * (1.0 + q * (1.0 - sq)))).astype(BF16)
            dlogf = _chunk_sums(upp_b, jnp.concatenate(db_pair, axis=1))
            dfv = dlogf / fp - jnp.concatenate(dkf_pair, axis=1)
            dz_ref[:, ZFH + 2 * HD * p:ZFH + 2 * HD * (p + 1)] = (dfv * (1.0 - lbp) * sgp * (1.0 - sgp)).astype(BF16)
            dlb_ref[:, 2 * HD * p:2 * HD * (p + 1)] += _rowsum8(dfv * (1.0 - sgp))

    rev = lambda j: nb - 1 - j
    return pl.pallas_call(
        body, name="hgrn_bwd",
        out_shape=(jax.ShapeDtypeStruct((t, D_IN), BF16), jax.ShapeDtypeStruct((8, HG_W), F32),
                   jax.ShapeDtypeStruct((8, HD), F32)),
        grid=(nb,),
        in_specs=_hgrn_cols(rev) + [pl.BlockSpec((2, HG_W), lambda j: (0, 0)), pl.BlockSpec((1, HD), lambda j: (0, 0)),
                                    pl.BlockSpec((HB, HG_W), lambda j: (rev(j), 0)),
                                    pl.BlockSpec((4, NCH, HD, HD), lambda j: (0, rev(j), 0, 0)),
                                    pl.BlockSpec((HB, HG_W), lambda j: (rev(j), 1)),
                                    pl.BlockSpec((HB, SWA_W), lambda j: (rev(j), 0)),
                                    pl.BlockSpec((HB // 2, 2 * KV_W), lambda j: (rev(j), 0)),
                                    pl.BlockSpec((HB // 2, 2 * KV_W), lambda j: (rev(j), 0))],
        out_specs=(pl.BlockSpec((HB, D_IN), lambda j: (rev(j), 0)), pl.BlockSpec((8, HG_W), lambda j: (0, 0)),
                   pl.BlockSpec((8, HD), lambda j: (0, 0))),
        scratch_shapes=[pltpu.VMEM((4, HD, HD), F32)],
        compiler_params=_params(dimension_semantics=("arbitrary",)),
    )(*[z] * 8, hgrn_lb, onorm, o_save, sprev, dymix, *dza)


XB = 512


def _xattn_fwd(q, k, v, wo, h, g_post, g_pre, t, dep=None):
    tb = min(XB, t)
    deps = [] if dep is None else [dep]

    def body(q_ref, k_ref, v_ref, wo_ref, h_ref, gp_ref, gn_ref, *rest):
        o_ref, y_ref, hn_ref, u_ref = rest[len(deps):]
        for hd in range(XH):
            cols = slice(XD * hd, XD * (hd + 1))
            s = _dot(q_ref[:, cols], k_ref[:, cols], 1, 1) * (XD ** -0.5)
            p = jnp.exp(s - jnp.max(s, axis=-1, keepdims=True))
            l = jnp.sum(p, axis=-1, keepdims=True)
            o_ref[:, cols] = (_dot(p.astype(BF16), v_ref[:, cols], 1, 0) * (1.0 / l)).astype(BF16)
        y, hn, u = _ep_post_pre(_dot(o_ref[...], wo_ref[...], 1, 0), h_ref[...], gp_ref[...], gn_ref[...])
        y_ref[...] = y
        hn_ref[...] = hn
        u_ref[...] = u.astype(BF16)

    row = pl.BlockSpec((tb, D), lambda i: (i, 0))
    whole = lambda a: pl.BlockSpec(a.shape, lambda i: (0,) * a.ndim, pipeline_mode=pl.Buffered(1))
    half = jax.ShapeDtypeStruct((t, D), BF16)
    return pl.pallas_call(
        body, name="xattn_fwd", out_shape=(half, half, jax.ShapeDtypeStruct((t, D), F32), half), grid=(t // tb,),
        in_specs=[row, whole(k), whole(v), whole(wo), row, whole(g_post), whole(g_pre)] + [ANY_SPEC] * len(deps),
        out_specs=(row, row, row, row), compiler_params=_params(),
    )(q, k, v, wo, h, g_post, g_pre, *deps)


def _xattn_bwd(q, k, v, do, wq, wout, dh_out, hn, y, g_post, g_pre, t):
    tb = min(XB, t)

    def body(q_ref, k_ref, v_ref, do_ref, wq_ref, wout_ref, dho_ref, hn_ref, y_ref, gp_ref, gn_ref,
             dq_ref, dk_ref, dv_ref, dh_ref, dyp_ref, dym_ref, dgn_ref, dgp_ref):
        @pl.when(pl.program_id(0) == 0)
        def _():
            dk_ref[...] = jnp.zeros_like(dk_ref)
            dv_ref[...] = jnp.zeros_like(dv_ref)
            dgn_ref[...] = jnp.zeros_like(dgn_ref)
            dgp_ref[...] = jnp.zeros_like(dgp_ref)

        for h in range(XH):
            cols = slice(XD * h, XD * (h + 1))
            qh, kh, vh, doh = q_ref[:, cols], k_ref[:, cols], v_ref[:, cols], do_ref[:, cols]
            s = _dot(qh, kh, 1, 1) * (XD ** -0.5)
            p = jnp.exp(s - jnp.max(s, axis=-1, keepdims=True))
            p = p * (1.0 / jnp.sum(p, axis=-1, keepdims=True))
            dp = _dot(doh, vh, 1, 1)
            ds = (p * (dp - jnp.sum(p * dp, axis=-1, keepdims=True)) * (XD ** -0.5)).astype(BF16)
            dq_ref[:, cols] = _dot(ds, kh, 1, 0).astype(BF16)
            dk_ref[:, cols] += _dot(ds, qh, 0, 0)
            dv_ref[:, cols] += _dot(p.astype(BF16), doh, 0, 0)
        du = _dot(dq_ref[...], wq_ref[...], 1, 1)
        dh, dyp, dgn, dgp = _ep_post_pre_bwd(du, dho_ref[...], hn_ref[...], y_ref[...], gp_ref[...], gn_ref[...])
        dh_ref[...] = dh
        dyp = dyp.astype(BF16)
        dyp_ref[...] = dyp
        dym_ref[...] = _dot(dyp, wout_ref[...], 1, 1).astype(BF16)
        dgn_ref[...] += dgn
        dgp_ref[...] += dgp

    row = pl.BlockSpec((tb, D), lambda i: (i, 0))
    mem = pl.BlockSpec(k.shape, lambda i: (0, 0))
    whole = lambda a: pl.BlockSpec(a.shape, lambda i: (0,) * a.ndim, pipeline_mode=pl.Buffered(1))
    acc = pl.BlockSpec((8, D), lambda i: (0, 0))
    half = jax.ShapeDtypeStruct((t, D), BF16)
    return pl.pallas_call(
        body, name="xattn_bwd",
        out_shape=(half, jax.ShapeDtypeStruct(k.shape, F32), jax.ShapeDtypeStruct(k.shape, F32),
                   jax.ShapeDtypeStruct((t, D), F32), half, half,
                   jax.ShapeDtypeStruct((8, D), F32), jax.ShapeDtypeStruct((8, D), F32)),
        grid=(t // tb,),
        in_specs=[row, whole(k), whole(v), row, whole(wq), whole(wout), row, row, row, whole(g_post), whole(g_pre)],
        out_specs=(row, mem, mem, row, row, row, acc, acc),
        compiler_params=_params(dimension_semantics=("arbitrary",)),
    )(q, k, v, do, wq, wout, dh_out, hn, y, g_post, g_pre)


def _mem_kv(mem, g_mem, wk, wv):
    def body(m_ref, g_ref, wk_ref, wv_ref, mn_ref, k_ref, v_ref):
        m_ = m_ref[...]
        mn = (m_ * _rstd(m_) * g_ref[...]).astype(BF16)
        mn_ref[...] = mn
        k_ref[...] = _dot(mn, wk_ref[...], 1, 0).astype(BF16)
        v_ref[...] = _dot(mn, wv_ref[...], 1, 0).astype(BF16)

    return pl.pallas_call(body, name="mem_kv", out_shape=(jax.ShapeDtypeStruct(mem.shape, BF16),) * 3,
                          compiler_params=_params())(mem, g_mem, wk, wv)


def _mem_kv_bwd(mn, mem, dk, dv, wk, wv, dep=None):
    deps = [] if dep is None else [dep]

    def body(mn_ref, m_ref, dk_ref, dv_ref, wk_ref, wv_ref, *rest):
        gk_ref, gv_ref, dg_ref = rest[len(deps):]
        mn = mn_ref[...]
        dkb, dvb = dk_ref[...].astype(BF16), dv_ref[...].astype(BF16)
        gk_ref[...] = _dot(mn, dkb, 0, 0).astype(BF16)
        gv_ref[...] = _dot(mn, dvb, 0, 0).astype(BF16)
        dmn = _dot(dkb, wk_ref[...], 1, 1) + _dot(dvb, wv_ref[...], 1, 1)
        m_ = m_ref[...]
        dg_ref[...] = _rowsum8(dmn * (m_ * _rstd(m_)))

    vmem = pl.BlockSpec(memory_space=pltpu.VMEM)
    return pl.pallas_call(
        body, name="mem_kv_bwd",
        out_shape=(jax.ShapeDtypeStruct(wk.shape, BF16), jax.ShapeDtypeStruct(wv.shape, BF16),
                   jax.ShapeDtypeStruct((8, D), F32)),
        in_specs=[vmem] * 6 + [ANY_SPEC] * len(deps), out_specs=(vmem,) * 3, compiler_params=_params(),
    )(mn, mem, dk, dv, wk, wv, *deps)


FB = 256


def _ffn_fwd_loss(u, wgt, wut, wd, h, target, g_post, t):
    tb = min(FB, t)

    def body(u_ref, wg_ref, wu_ref, wd_ref, h_ref, t_ref, gp_ref, g_ref, up_ref, a_ref, sq_ref, dh_ref, dy_ref, dg_ref):
        @pl.when(pl.program_id(0) == 0)
        def _():
            sq_ref[...] = jnp.zeros_like(sq_ref)
            dg_ref[...] = jnp.zeros_like(dg_ref)

        u_ = u_ref[...]
        g = _dot(u_, wg_ref[...], 1, 1)
        up = _dot(u_, wu_ref[...], 1, 1)
        a = (g * _sig(g) * up).astype(BF16)
        g_ref[...] = g.astype(BF16)
        up_ref[...] = up.astype(BF16)
        a_ref[...] = a
        sq, dh, dy, dg = _ep_final_loss(_dot(a, wd_ref[...], 1, 0), h_ref[...], t_ref[...], gp_ref[...])
        sq_ref[...] += sq
        dh_ref[...] = dh
        dy_ref[...] = dy.astype(BF16)
        dg_ref[...] += dg

    row = lambda w: pl.BlockSpec((tb, w), lambda i: (i, 0))
    whole = lambda a: pl.BlockSpec(a.shape, lambda i: (0,) * a.ndim, pipeline_mode=pl.Buffered(1))
    acc = pl.BlockSpec((8, D), lambda i: (0, 0))
    wide = jax.ShapeDtypeStruct((t, D_FF), BF16)
    return pl.pallas_call(
        body, name="ffn_fwd_loss",
        out_shape=(wide, wide, wide, jax.ShapeDtypeStruct((8, D), F32), jax.ShapeDtypeStruct((t, D), F32),
                   jax.ShapeDtypeStruct((t, D), BF16), jax.ShapeDtypeStruct((8, D), F32)),
        grid=(t // tb,),
        in_specs=[row(D), whole(wgt), whole(wut), whole(wd), row(D), row(D), whole(g_post)],
        out_specs=(row(D_FF), row(D_FF), row(D_FF), acc, row(D), row(D), acc),
        compiler_params=_params(dimension_semantics=("arbitrary",)),
    )(u, wgt, wut, wd, h, target, g_post)


def _ffn_bwd(dy, wd, gate, up, wgt, wut, dh_out, hn, y, g_post, g_pre, wo, t, dep=None):
    tb = min(FB, t)
    deps = [] if dep is None else [dep]

    def body(dy_ref, wd_ref, g_ref, up_ref, wg_ref, wu_ref, dho_ref, hn_ref, y_ref, gp_ref, gn_ref, wo_ref, *rest):
        dg_ref, dup_ref, dh_ref, dyp_ref, do_ref, dgn_ref, dgp_ref = rest[len(deps):]

        @pl.when(pl.program_id(0) == 0)
        def _():
            dgn_ref[...] = jnp.zeros_like(dgn_ref)
            dgp_ref[...] = jnp.zeros_like(dgp_ref)

        da = _dot(dy_ref[...], wd_ref[...], 1, 1)
        g = g_ref[...].astype(F32)
        sg = _sig(g)
        dup = (da * g * sg).astype(BF16)
        dgate = (da * up_ref[...].astype(F32) * (sg * (1.0 + g * (1.0 - sg)))).astype(BF16)
        dup_ref[...] = dup
        dg_ref[...] = dgate
        du = _dot(dgate, wg_ref[...], 1, 0) + _dot(dup, wu_ref[...], 1, 0)
        dh, dyp, dgn, dgp = _ep_post_pre_bwd(du, dho_ref[...], hn_ref[...], y_ref[...], gp_ref[...], gn_ref[...])
        dh_ref[...] = dh
        dyp = dyp.astype(BF16)
        dyp_ref[...] = dyp
        do_ref[...] = _dot(dyp, wo_ref[...], 1, 1).astype(BF16)
        dgn_ref[...] += dgn
        dgp_ref[...] += dgp

    row = lambda w: pl.BlockSpec((tb, w), lambda i: (i, 0))
    whole = lambda a: pl.BlockSpec(a.shape, lambda i: (0,) * a.ndim, pipeline_mode=pl.Buffered(1))
    acc = pl.BlockSpec((8, D), lambda i: (0, 0))
    return pl.pallas_call(
        body, name="ffn_bwd",
        out_shape=(jax.ShapeDtypeStruct((t, D_FF), BF16), jax.ShapeDtypeStruct((t, D_FF), BF16),
                   jax.ShapeDtypeStruct((t, D), F32), jax.ShapeDtypeStruct((t, D), BF16),
                   jax.ShapeDtypeStruct((t, D), BF16), jax.ShapeDtypeStruct((8, D), F32),
                   jax.ShapeDtypeStruct((8, D), F32)),
        grid=(t // tb,),
        in_specs=[row(D), whole(wd), row(D_FF), row(D_FF), whole(wgt), whole(wut), row(D), row(D), row(D),
                  whole(g_post), whole(g_pre), whole(wo)] + [ANY_SPEC] * len(deps),
        out_specs=(row(D_FF), row(D_FF), row(D), row(D), row(D), acc, acc),
        compiler_params=_params(dimension_semantics=("arbitrary",)),
    )(dy, wd, gate, up, wgt, wut, dh_out, hn, y, g_post, g_pre, wo, *deps)


def _local_step(x, mem, target, fetch, sm, emit=None, first_dep=None, milestone=None):
    t = x.shape[0]
    w, gw = {}, {}

    def out(key, g):
        gw[key] = g
        return None if emit is None else emit(key, g)

    def tell(tag, value):
        return None if milestone is None else milestone(tag, value)
    u1 = _prenorm(x, sm["g_mix_pre"], name="prenorm_mix", dep=first_dep)
    w["winT"] = fetch("winT", u1)
    z = _mm(u1, w["winT"], tb=True, out_dtype=F32, tm=1024, tn=1408, name="mm_z", n_outer=True)
    ymix, lse = _swa_fwd(z, sm["sinks"], t, dep=tell("z", z))
    ymix, o_h, sprev = _hgrn2_fwd(z, sm["hgrn_lb"], sm["hgrn_onorm"], ymix, t, dep=tell("swa", lse))
    for key in ("wout", "wq", "wk", "wv", "wo"):
        w[key] = fetch(key, ymix)
    y1, h1, u2, qx = _mm_rows([(ymix, w["wout"], False)], [x], [sm["g_mix_post"], sm["g_x_pre"], w["wq"]],
                              _then(_ep_post_pre, 2, False), _EP_POST_PRE_OUTS + [ROW_BF16], tm=1024,
                              name="mm_y1_post_qx")
    mn, kx, vx = _mem_kv(mem, sm["g_mem"], w["wk"], w["wv"])
    ox, y2, h2, u3 = _xattn_fwd(qx, kx, vx, w["wo"], h1, sm["g_x_post"], sm["g_ffn_pre"], t, dep=tell("kv", kx))
    for key in ("wgT", "wuT", "wd"):
        w[key] = fetch(key, u3)
    gate, up, act, sq, dh3, dy3, dg_ffn_post = _ffn_fwd_loss(u3, w["wgT"], w["wuT"], w["wd"], h2, target,
                                                             sm["g_ffn_post"], t)
    dep = out("wd", _mm(act, dy3, ta=True, out_dtype=BF16, tm=1408, tn=1024, name="mm_gwd"))
    dgate, dup, dh2, dy2, dox, dg_ffn_pre, dg_x_post = _ffn_bwd(
        dy3, w["wd"], gate, up, w["wgT"], w["wuT"], dh3, h2, y2, sm["g_x_post"], sm["g_ffn_pre"], w["wo"], t, dep=dep)
    dep = out("wgT", _mm(dgate, u3, ta=True, out_dtype=BF16, tm=1408, tn=1024, name="mm_gwg"))
    dep = out("wuT", _mm(dup, u3, ta=True, out_dtype=BF16, tm=1408, tn=1024, name="mm_gwu", dep=dep))
    out("wo", _mm(ox, dy2, ta=True, out_dtype=BF16, tm=512, tn=1024, name="mm_gwo", dep=dep))
    dqx, dkx, dvx, dh1, dy1, dymix, dg_x_pre, dg_mix_post = _xattn_bwd(
        qx, kx, vx, dox, w["wq"], w["wout"], dh2, h1, y1, sm["g_mix_post"], sm["g_x_pre"], t)
    out("wq", _mm(u2, dqx, ta=True, out_dtype=BF16, tm=512, tn=1024, name="mm_gwq"))
    gwk, gwv, dg_mem = _mem_kv_bwd(mn, mem, dkx, dvx, w["wk"], w["wv"])
    out("wk", gwk)
    dep = out("wv", gwv)
    dep = out("wout", _mm(ymix, dy1, ta=True, out_dtype=BF16, tm=512, tn=1024, name="mm_gwout", dep=dep))
    *dza, dsinks = _swa_bwd(z, sm["sinks"], ymix, lse, dymix, t, dep=dep)
    dz, dlb, donorm = _hgrn2_bwd(z, sm["hgrn_lb"], sm["hgrn_onorm"], o_h, sprev, dymix, dza, t)
    dep = out("winT", _mm(dz, u1, ta=True, out_dtype=BF16, tm=1408, tn=1024, name="mm_gwin"))
    grad_x, dg_mix_pre = _mm_rows([(dz, w["winT"], False)], [dh1, x], [sm["g_mix_pre"]], _ep_pre_bwd,
                                  _EP_PRE_BWD_OUTS, tm=512, name="mm_du1_pre_bwd", dep=dep)
    parts = dict(g_mix_pre=dg_mix_pre, g_mix_post=dg_mix_post, g_mem=dg_mem, g_x_pre=dg_x_pre,
                 g_x_post=dg_x_post, g_ffn_pre=dg_ffn_pre, g_ffn_post=dg_ffn_post,
                 hgrn_onorm=donorm, hgrn_lb=dlb, sinks=dsinks, sq=sq)
    return grad_x, gw, parts


def _position():
    return lax.axis_index("x"), lax.axis_index("y"), lax.axis_index("c")


def _peer(pos, k):
    x, y, c = pos
    return (1 - x if k & 4 else x, 1 - y if k & 2 else y, 1 - c if k & 1 else c)


def _linear(pos):
    x, y, c = pos
    return 4 * x + 2 * y + c


HBM_SPEC = pl.BlockSpec(memory_space=pltpu.HBM)
SEM_SPEC = pl.BlockSpec(memory_space=pltpu.SEMAPHORE)
DATAFLOW = pltpu.SideEffectType.DATAFLOW_SIDE_EFFECTING
SEND_ORDER = (1, 2, 4, 3, 5, 6, 7)


def _in_hbm(a):
    return pltpu.with_memory_space_constraint(a, pltpu.HBM)


def _prepare_weights(shards, *, name, dep=None):
    n = len(shards)
    deps = [] if dep is None else [dep]

    def body(*refs):
        ins, (outs, lands, sem) = refs[:n], (refs[-2 * n - 1:-n - 1], refs[-n - 1:-1], refs[-1])
        me_lin = _linear(_position())
        copies = []
        for a in range(n):
            r = ins[a].shape[0]
            outs[a][...] = ins[a][...].astype(BF16)
            copies.append(pltpu.make_async_copy(outs[a], lands[a].at[pl.ds(me_lin * r, r), :], sem.at[a]))
            copies[-1].start()
        for cp in copies:
            cp.wait()

    vmem = pl.BlockSpec(memory_space=pltpu.VMEM)
    res = pl.pallas_call(
        body, name=name,
        out_shape=tuple(jax.ShapeDtypeStruct(s.shape, BF16) for s in shards)
        + tuple(jax.ShapeDtypeStruct((N_DEV * s.shape[0], s.shape[1]), BF16) for s in shards),
        in_specs=[vmem] * n + [ANY_SPEC] * len(deps), out_specs=tuple([vmem] * n + [ANY_SPEC] * n),
        scratch_shapes=[pltpu.SemaphoreType.DMA((n,))], compiler_params=_params(),
    )(*shards, *deps)
    return res[:n], res[n:]


def _copies_start(arrays, plan, n, *, name):
    na = len(arrays)

    def body(*refs):
        ins, send_sems, recv_sems = refs[:na], refs[na], refs[na + 1]
        me = _position()
        for j in range(n):
            src, dst, peer, _ = plan(ins, me, j)
            pltpu.make_async_remote_copy(src_ref=src, dst_ref=dst, send_sem=send_sems.at[j], recv_sem=recv_sems.at[j],
                                         device_id=peer, device_id_type=MESH).start()

    return pl.pallas_call(
        body, name=name,
        out_shape=(pltpu.SemaphoreType.DMA((n,)), pltpu.SemaphoreType.DMA((n,)))
        + tuple(pltpu.HBM(a.shape, a.dtype) for a in arrays),
        in_specs=(HBM_SPEC,) * na, out_specs=(SEM_SPEC, SEM_SPEC) + (HBM_SPEC,) * na,
        input_output_aliases={i: 2 + i for i in range(na)},
        compiler_params=pltpu.CompilerParams(has_side_effects=DATAFLOW),
    )(*[_in_hbm(a) for a in arrays])


def _copies_wait(send_sems, recv_sems, arrays, plan, n, after, *, name):
    na = len(arrays)

    def body(*refs):
        ins, send_sems, recv_sems = refs[:na], refs[na], refs[na + 1]
        me = _position()
        for j in range(n):
            src, _, peer, landed = plan(ins, me, j)
            copy = pltpu.make_async_remote_copy(src_ref=src, dst_ref=landed, send_sem=send_sems.at[j],
                                                recv_sem=recv_sems.at[j], device_id=peer, device_id_type=MESH)
            copy.wait_send()
            copy.wait_recv()

    return pl.pallas_call(
        body, name=name, out_shape=tuple(pltpu.HBM(a.shape, a.dtype) for a in arrays),
        in_specs=(HBM_SPEC,) * na + (SEM_SPEC, SEM_SPEC, ANY_SPEC), out_specs=(HBM_SPEC,) * na,
        input_output_aliases={i: i for i in range(na)},
        compiler_params=pltpu.CompilerParams(has_side_effects=DATAFLOW),
    )(*arrays, send_sems, recv_sems, after)


SAME_CORE = (2, 4, 6)


class _TwoLevelGather:
    def __init__(self, shards, lands, *, name):
        n = self.n = len(shards)
        self.name = name
        first_peers = (1,) + SAME_CORE

        def rows(ref, pos):
            r = ref.shape[0] // N_DEV
            return ref.at[pl.ds(_linear(pos) * r, r), :]

        def first(refs, me, j):
            a, peer = j // 4, _peer(me, first_peers[j % 4])
            return refs[a], rows(refs[n + a], me), peer, rows(refs[n + a], peer)

        def second(refs, me, j):
            a, sibling = j // 3, _peer(me, 1)
            mine = rows(refs[a], _peer(me, SAME_CORE[j % 3]))
            return mine, mine, sibling, rows(refs[a], _peer(sibling, SAME_CORE[j % 3]))

        self._first, self._second = first, second
        self._flight = _copies_start(list(shards) + list(lands), first, 4 * n, name=name + "_send")
        self.dep = self._flight[2]

    def pass_on(self, after):
        send1, recv1, *arrays = self._flight
        arrays = _copies_wait(send1, recv1, arrays, self._first, 4 * self.n, after, name=self.name + "_recv")
        self._flight = _copies_start(list(arrays[self.n:]), self._second, 3 * self.n, name=self.name + "_pass")
        return self._flight[2]

    def finish(self, after):
        send2, recv2, *lands = self._flight
        return _copies_wait(send2, recv2, lands, self._second, 3 * self.n, after, name=self.name + "_pass_recv")


def _exchange_start(gs, *, name):
    n = len(gs)
    rows = [g.shape[0] // N_DEV for g in gs]
    lands = [lax.empty((N_DEV - 1, r, g.shape[1]), g.dtype) for g, r in zip(gs, rows)]

    def body(*refs):
        g_refs, land_refs = refs[:n], refs[n:2 * n]
        send_sems, recv_sems = refs[2 * n:3 * n], refs[3 * n:4 * n]
        me = _position()
        for a in range(n):
            for k in SEND_ORDER:
                peer = _peer(me, k)
                pltpu.make_async_remote_copy(
                    src_ref=g_refs[a].at[pl.ds(_linear(peer) * rows[a], rows[a]), :],
                    dst_ref=land_refs[a].at[k - 1],
                    send_sem=send_sems[a].at[k - 1], recv_sem=recv_sems[a].at[k - 1],
                    device_id=peer, device_id_type=MESH).start()

    res = pl.pallas_call(
        body, name=name,
        out_shape=tuple(pltpu.SemaphoreType.DMA((N_DEV - 1,)) for _ in range(2 * n))
        + tuple(pltpu.HBM(a.shape, a.dtype) for a in gs + lands),
        in_specs=(HBM_SPEC,) * (2 * n), out_specs=(SEM_SPEC,) * (2 * n) + (HBM_SPEC,) * (2 * n),
        input_output_aliases={i: 2 * n + i for i in range(2 * n)},
        compiler_params=pltpu.CompilerParams(has_side_effects=DATAFLOW),
    )(*[_in_hbm(a) for a in gs + lands])
    return [(res[a], res[n + a], res[2 * n + a], res[3 * n + a]) for a in range(n)]


def _exchange_wait(send_sems, recv_sems, g_thru, land_thru, after, *, name):
    r = land_thru.shape[1]

    def body(g_ref, land_ref, send_sems, recv_sems, after_ref, g_dead, got_ref):
        del after_ref, g_dead, got_ref
        me = _position()
        for k in SEND_ORDER:
            peer = _peer(me, k)
            copy = pltpu.make_async_remote_copy(
                src_ref=g_ref.at[pl.ds(_linear(peer) * r, r), :], dst_ref=land_ref.at[k - 1],
                send_sem=send_sems.at[k - 1], recv_sem=recv_sems.at[k - 1],
                device_id=peer, device_id_type=MESH)
            copy.wait_send()
            copy.wait_recv()

    return pl.pallas_call(
        body, name=name,
        out_shape=(pltpu.HBM(g_thru.shape, g_thru.dtype), pltpu.HBM(land_thru.shape, land_thru.dtype)),
        in_specs=(HBM_SPEC, HBM_SPEC, SEM_SPEC, SEM_SPEC, pl.BlockSpec(memory_space=pl.ANY)),
        out_specs=(HBM_SPEC, HBM_SPEC), input_output_aliases={0: 0, 1: 1},
        compiler_params=pltpu.CompilerParams(has_side_effects=DATAFLOW),
    )(g_thru, land_thru, send_sems, recv_sems, after)


def _adamw_math(w, g, m, v):
    m = B1 * m + (1.0 - B1) * g
    v = B2 * v + (1.0 - B2) * (g * g)
    delta = -LR * ((m / C1) / (jnp.sqrt(v / C2) + AEPS) + WD * w)
    return delta, m, v


def _sum_adamw(items, *, name):
    n = len(items)

    def body(*refs):
        ins, outs, scratch = refs[:5 * n], refs[5 * n:9 * n], refs[9 * n:]
        me_lin = _linear(_position())
        mine = []
        for a in range(n):
            r = items[a][2].shape[0]
            mine.append(pltpu.make_async_copy(ins[5 * a].at[pl.ds(me_lin * r, r), :], scratch[a], scratch[n].at[a]))
            mine[-1].start()
        for a in range(n):
            _, land_ref, w_ref, m_ref, v_ref = ins[5 * a:5 * a + 5]
            g_ref, d_ref, nm_ref, nv_ref = outs[4 * a:4 * a + 4]
            g = land_ref[0].astype(F32)
            for s in range(1, N_DEV - 1):
                g = g + land_ref[s].astype(F32)
            mine[a].wait()
            g = scratch[a][...].astype(F32) + g
            g_ref[...] = g
            d_ref[...], nm_ref[...], nv_ref[...] = _adamw_math(w_ref[...], g, m_ref[...], v_ref[...])

    vmem = pl.BlockSpec(memory_space=pltpu.VMEM)
    res = pl.pallas_call(
        body, name=name,
        out_shape=tuple(jax.ShapeDtypeStruct(it[2].shape, F32) for it in items for _ in range(4)),
        in_specs=[ANY_SPEC, vmem, vmem, vmem, vmem] * n, out_specs=(vmem,) * (4 * n),
        scratch_shapes=[pltpu.VMEM(it[2].shape, BF16) for it in items] + [pltpu.SemaphoreType.DMA((n,))],
        compiler_params=_params(),
    )(*[a for it in items for a in it])
    return [res[4 * a:4 * a + 4] for a in range(n)]


SMALL = ("g_mix_pre", "g_mix_post", "g_mem", "g_x_pre", "g_x_post", "g_ffn_pre", "g_ffn_post",
         "hgrn_onorm", "hgrn_lb", "sinks")
SMALL_W = dict(hgrn_onorm=HD, hgrn_lb=HG_W, sinks=8)
SQ_ROW = len(SMALL)
PACK_ROWS = 16


def _small_pack(parts):
    ns = len(SMALL)

    def body(*refs):
        part, mine, slots, sem = refs[:ns + 1], refs[ns + 1], refs[ns + 2], refs[ns + 3]
        mine[...] = jnp.zeros((PACK_ROWS, D), F32)
        for r, name in enumerate(SMALL):
            wd = SMALL_W.get(name, D)
            mine[r:r + 1, 0:wd] = jnp.sum(part[r][...], axis=0, keepdims=True)[:, 0:wd]
        sq = jnp.sum(part[ns][...]) * (0.5 / D)
        mine[SQ_ROW:SQ_ROW + 1, :] = jnp.full((1, D), sq, F32)
        own = pltpu.make_async_copy(mine, slots.at[_linear(_position())], sem)
        own.start()
        own.wait()

    vmem = pl.BlockSpec(memory_space=pltpu.VMEM)
    return pl.pallas_call(
        body, name="small_pack",
        out_shape=(jax.ShapeDtypeStruct((PACK_ROWS, D), F32), jax.ShapeDtypeStruct((N_DEV, PACK_ROWS, D), F32)),
        in_specs=[vmem] * (ns + 1), out_specs=(vmem, ANY_SPEC),
        scratch_shapes=[pltpu.SemaphoreType.DMA(())], compiler_params=_params(),
    )(*[parts[n] for n in SMALL], parts["sq"])


def _small_exchange(mine, slots):
    def plan(refs, me, j):
        peer = _peer(me, j + 1)
        return refs[0], refs[1].at[_linear(me)], peer, refs[1].at[_linear(peer)]

    send, recv, mine1, slots1 = _copies_start([mine, slots], plan, N_DEV - 1, name="small_send")
    return lambda after: _copies_wait(send, recv, [mine1, slots1], plan, N_DEV - 1, after, name="small_recv")[1]


def _small_update(slots, sm, m_sm, v_sm):
    ns = len(SMALL)

    def body(*refs):
        tot = refs[0][0]
        for s in range(1, N_DEV):
            tot = tot + refs[0][s]
        w_refs, m_refs, v_refs = refs[1:ns + 1], refs[ns + 1:2 * ns + 1], refs[2 * ns + 1:3 * ns + 1]
        outs = refs[3 * ns + 1:]
        loss_ref = outs[0]
        g_out, d_out = outs[1:ns + 1], outs[ns + 1:2 * ns + 1]
        nm_out, nv_out = outs[2 * ns + 1:3 * ns + 1], outs[3 * ns + 1:4 * ns + 1]
        loss_ref[...] = tot[SQ_ROW:SQ_ROW + 1, 0:1]
        for r, name in enumerate(SMALL):
            wd = SMALL_W.get(name, D)
            g = tot[r:r + 1, 0:wd]
            w = w_refs[r][...]
            if name == "hgrn_lb":
                mx = jnp.maximum(w[0:1], w[1:2])
                e0, e1 = jnp.exp(w[0:1] - mx), jnp.exp(w[1:2] - mx)
                lb0 = e0 / (e0 + e1)
                g0 = g * lb0 * (1.0 - lb0)
                for i, gi in enumerate((g0, -g0)):
                    d, nm, nv = _adamw_math(w[i:i + 1], gi, m_refs[r][i:i + 1, :], v_refs[r][i:i + 1, :])
                    g_out[r][i:i + 1, :] = gi
                    d_out[r][i:i + 1, :], nm_out[r][i:i + 1, :], nv_out[r][i:i + 1, :] = d, nm, nv
            else:
                d, nm, nv = _adamw_math(w, g, m_refs[r][...], v_refs[r][...])
                g_out[r][...] = g
                d_out[r][...], nm_out[r][...], nv_out[r][...] = d, nm, nv

    shapes = [jax.ShapeDtypeStruct(sm[n].shape, F32) for n in SMALL]
    res = pl.pallas_call(
        body, name="small_update", out_shape=tuple([jax.ShapeDtypeStruct((1, 1), F32)] + shapes * 4),
        compiler_params=_params(),
    )(slots, *[sm[n] for n in SMALL], *[m_sm[n] for n in SMALL], *[v_sm[n] for n in SMALL])
    groups = [dict(zip(SMALL, res[1 + i * ns:1 + (i + 1) * ns])) for i in range(4)]
    return res[0], groups[0], groups[1], groups[2], groups[3]


BIG = ("w_in", "w_gate", "w_up", "w_down", "w_out", "wq_x", "wk_x", "wv_x", "wo_x")
BIG_KEY = dict(w_in="winT", w_gate="wgT", w_up="wuT", w_down="wd", w_out="wout", wq_x="wq", wk_x="wk",
               wv_x="wv", wo_x="wo")
TRANSPOSED = ("w_in", "w_gate", "w_up")
WEIGHTS = ("w_in", "sinks", "hgrn_lb", "hgrn_onorm", "w_out", "g_mix_pre", "g_mix_post", "g_mem", "g_x_pre",
           "g_x_post", "wq_x", "wk_x", "wv_x", "wo_x", "g_ffn_pre", "g_ffn_post", "w_gate", "w_up", "w_down")


def kernel(x, mem, w_in, sinks, hgrn_lb, hgrn_onorm, w_out, g_mix_pre, g_mix_post, g_mem, g_x_pre, g_x_post, wq_x, wk_x, wv_x, wo_x, g_ffn_pre, g_ffn_post, w_gate, w_up, w_down, loss_target, m_w_in, m_sinks, m_hgrn_lb, m_hgrn_onorm, m_w_out, m_g_mix_pre, m_g_mix_post, m_g_mem, m_g_x_pre, m_g_x_post, m_wq_x, m_wk_x, m_wv_x, m_wo_x, m_g_ffn_pre, m_g_ffn_post, m_w_gate, m_w_up, m_w_down, v_w_in, v_sinks, v_hgrn_lb, v_hgrn_onorm, v_w_out, v_g_mix_pre, v_g_mix_post, v_g_mem, v_g_x_pre, v_g_x_post, v_wq_x, v_wk_x, v_wv_x, v_wo_x, v_g_ffn_pre, v_g_ffn_post, v_w_gate, v_w_up, v_w_down):
    given = dict(locals())
    wts = {n: given[n] for n in WEIGHTS}
    ms = {n: given["m_" + n] for n in WEIGHTS}
    vs = {n: given["v_" + n] for n in WEIGHTS}

    def mat(a, name):
        a = a[0]
        return a.T if name in TRANSPOSED else a

    groups = (("w_in",), ("w_out", "wq_x", "wk_x", "wv_x", "wo_x"), ("w_gate", "w_up", "w_down"))
    gathers = []

    def start_group(g, dep):
        tag = ("w_in", "w_attn", "w_ffn")[g]
        shards, lands = _prepare_weights([mat(wts[n], n) for n in groups[g]], name="prepare_" + tag, dep=dep)
        gathers.append(_TwoLevelGather(shards, lands, name=tag))
        return gathers[-1].dep

    first_dep = start_group(1, start_group(0, None))
    name_of = {k: n for n, k in BIG_KEY.items()}
    gathered = {}

    def milestone(tag, value):
        if tag == "z":
            return start_group(2, value)
        return gathers[{"swa": 1, "kv": 2}[tag]].pass_on(value)

    def fetch(key, after):
        name = name_of[key]
        if name not in gathered:
            g = [i for i, group in enumerate(groups) if name in group][0]
            if g == 0:
                gathers[0].pass_on(after)
            gathered.update(zip(groups[g], gathers[g].finish(after)))
        return gathered[name]

    sm = {n: wts[n] for n in SMALL}
    started, held = {}, {}
    send_with = {k: group for group in (("wgT", "wuT"), ("wo", "wq", "wk", "wv")) for k in group}

    def emit(key, g):
        held[key] = g
        group = send_with.get(key, (key,))
        if key != group[-1]:
            return None
        flights = _exchange_start([held[k] for k in group], name="grad_send_" + name_of[group[0]])
        started.update({name_of[k]: f for k, f in zip(group, flights)})
        return flights[-1][2]

    grad_x, _, parts = _local_step(x[0], mem[0], loss_target[0], fetch, sm, emit, first_dep=first_dep, milestone=milestone)
    small_finish = _small_exchange(*_small_pack(parts))
    grads, deltas, new_m, new_v = {}, {}, {}, {}
    after = grad_x
    for group in (("w_down",), ("w_gate", "w_up"), ("wo_x", "wq_x", "wk_x", "wv_x", "w_out"), ("w_in",)):
        items = []
        for n in group:
            g_all, land = _exchange_wait(*started[n], after, name="grad_recv_" + n)
            items.append((g_all, land, mat(wts[n], n), mat(ms[n], n), mat(vs[n], n)))
            after = land
        for n, res in zip(group, _sum_adamw(items, name="adamw_" + group[0])):
            after = res[1]
            if n in TRANSPOSED:
                res = [a.T for a in res]
            grads[n], deltas[n], new_m[n], new_v[n] = [a[None] for a in res]
    loss, g_s, d_s, m_s, v_s = _small_update(small_finish(after), sm, {n: ms[n] for n in SMALL},
                                             {n: vs[n] for n in SMALL})
    grads.update(g_s), deltas.update(d_s), new_m.update(m_s), new_v.update(v_s)
    return (loss[0, 0], grad_x[None], *[grads[n] for n in WEIGHTS], *[deltas[n] for n in WEIGHTS],
            *[new_m[n] for n in WEIGHTS], *[new_v[n] for n in WEIGHTS])
```

```python
import functools

import jax
import jax.numpy as jnp
from jax import lax
from jax.experimental import pallas as pl
from jax.experimental.pallas import tpu as pltpu

F32 = jnp.float32
BF16 = jnp.bfloat16

D = 1024
D_IN = 2816
D_FF = 2816
CHUNK = 64
SWA_W = 512
KV_W = 128
HG_W = 512
HD = 128
ZQH, ZFH, ZIH, ZGH = 768, 1280, 1792, 2304
XH, XD = 4, 256
EPS = 1e-6
NEG = -1e30
N_DEV = 8
MESH = pl.DeviceIdType.MESH

LR, B1, B2, AEPS, WD, STEP = 0.001, 0.9, 0.999, 1e-08, 0.01, 10
C1 = 1.0 - B1 ** STEP
C2 = 1.0 - B2 ** STEP

VMEM_LIMIT = 56 * 1024 * 1024


def _params(**kw):
    return pltpu.CompilerParams(vmem_limit_bytes=VMEM_LIMIT, **kw)


def _sig(x):
    return 1.0 / (1.0 + jnp.exp(-x))


def _rowsum8(x):
    r, w = x.shape
    return jnp.sum(x.reshape(r // 8, 8, w), axis=0)


def _dot(a, b, ca, cb, precision=None):
    return lax.dot_general(a, b, (((ca,), (cb,)), ((), ())), preferred_element_type=F32,
                           precision=precision)


ANY_SPEC = pl.BlockSpec(memory_space=pl.ANY)


def _mm(a, b, *, ta=False, tb=False, out_dtype, tm, tn, tk=None, name, dep=None, n_outer=False):
    m = a.shape[1] if ta else a.shape[0]
    k = a.shape[0] if ta else a.shape[1]
    n = b.shape[0] if tb else b.shape[1]
    tm, tn = min(tm, m), min(tn, n)
    tk = k if tk is None else min(tk, k)
    nk = k // tk
    assert m % tm == 0 and n % tn == 0 and k % tk == 0, (name, m, n, k, tm, tn, tk)
    ij = (lambda g0, g1: (g1, g0)) if n_outer else (lambda g0, g1: (g0, g1))
    a_spec = (pl.BlockSpec((tk, tm), lambda g0, g1, kk: (kk, ij(g0, g1)[0])) if ta
              else pl.BlockSpec((tm, tk), lambda g0, g1, kk: (ij(g0, g1)[0], kk)))
    b_spec = (pl.BlockSpec((tn, tk), lambda g0, g1, kk: (ij(g0, g1)[1], kk)) if tb
              else pl.BlockSpec((tk, tn), lambda g0, g1, kk: (kk, ij(g0, g1)[1])))
    ca, cb = (0 if ta else 1), (1 if tb else 0)

    deps = [] if dep is None else [dep]

    def body(a_ref, b_ref, *rest):
        o_ref, acc = rest[len(deps)], rest[len(deps) + 1:]
        p = _dot(a_ref[...].astype(BF16), b_ref[...].astype(BF16), ca, cb)
        if nk == 1:
            o_ref[...] = p.astype(out_dtype)
        else:
            acc_ref, = acc
            kk = pl.program_id(2)

            @pl.when(kk == 0)
            def _():
                acc_ref[...] = p

            @pl.when(kk > 0)
            def _():
                acc_ref[...] += p

            @pl.when(kk == nk - 1)
            def _():
                o_ref[...] = acc_ref[...].astype(out_dtype)

    return pl.pallas_call(
        body, name=name, out_shape=jax.ShapeDtypeStruct((m, n), out_dtype),
        grid=(n // tn, m // tm, nk) if n_outer else (m // tm, n // tn, nk),
        in_specs=[a_spec, b_spec] + [ANY_SPEC] * len(deps),
        out_specs=pl.BlockSpec((tm, tn), lambda g0, g1, kk: ij(g0, g1)),
        scratch_shapes=[pltpu.VMEM((tm, tn), F32)] if nk > 1 else [],
        compiler_params=_params(dimension_semantics=("parallel", "parallel", "arbitrary")),
    )(a, b, *deps)


def _mm_rows(prods, rows_in, vecs_in, epilogue, outs, *, tm, name, dep=None):
    m = prods[0][0].shape[0]
    n = prods[0][1].shape[0] if prods[0][2] else prods[0][1].shape[1]
    tm = min(tm, m)
    assert m % tm == 0
    deps = [] if dep is None else [dep]
    n_p, n_r, n_v = len(prods), len(rows_in), len(vecs_in)

    def body(*refs):
        ab = refs[:2 * n_p]
        row_refs = refs[2 * n_p:2 * n_p + n_r]
        vec_refs = refs[2 * n_p + n_r:2 * n_p + n_r + n_v]
        out_refs = refs[2 * n_p + n_r + n_v + len(deps):]
        p = None
        for j, (_, _, tb) in enumerate(prods):
            t = _dot(ab[2 * j][...].astype(BF16), ab[2 * j + 1][...], 1, 1 if tb else 0)
            p = t if p is None else p + t
        vals = epilogue(p, *[r[...] for r in row_refs], *[v[...] for v in vec_refs])
        for (dtype, kind), o_ref, val in zip(outs, out_refs, vals):
            if kind == "row":
                o_ref[...] = val.astype(dtype)
            else:
                @pl.when(pl.program_id(0) == 0)
                def _(o_ref=o_ref):
                    o_ref[...] = jnp.zeros_like(o_ref)

                o_ref[...] += val

    row = lambda w: pl.BlockSpec((tm, w), lambda i: (i, 0))
    whole = lambda a: pl.BlockSpec(a.shape, lambda i: (0,) * a.ndim, pipeline_mode=pl.Buffered(1))
    in_specs, args = [], []
    for a, b, _ in prods:
        in_specs += [row(a.shape[1]), whole(b)]
        args += [a, b]
    in_specs += [row(r.shape[1]) for r in rows_in] + [whole(v) for v in vecs_in] + [ANY_SPEC] * len(deps)
    return pl.pallas_call(
        body, name=name,
        out_shape=tuple(jax.ShapeDtypeStruct((m, n) if kind == "row" else (8, n), dtype) for dtype, kind in outs),
        grid=(m // tm,), in_specs=in_specs,
        out_specs=tuple(row(n) if kind == "row" else pl.BlockSpec((8, n), lambda i: (0, 0)) for _, kind in outs),
        compiler_params=_params(dimension_semantics=("arbitrary",)),
    )(*args, *rows_in, *vecs_in, *deps)


def _rstd(x):
    return lax.rsqrt(jnp.mean(x * x, axis=-1, keepdims=True) + EPS)


def _norm_bwd(xh, r, t):
    return r * (t - xh * jnp.mean(xh * t, axis=-1, keepdims=True))


ROW_F32, ROW_BF16, SUM_F32 = (F32, "row"), (BF16, "row"), (F32, "sum")


def _then(epilogue, index, tb):
    def run(p, *args):
        vals = epilogue(p, *args[:-1])
        return (*vals, _dot(vals[index].astype(BF16), args[-1], 1, 1 if tb else 0))

    return run


def _ep_post_pre(p, h, g_post, g_pre):
    y = p.astype(BF16)
    yf = y.astype(F32)
    hn = h + yf * _rstd(yf) * g_post
    return y, hn, hn * _rstd(hn) * g_pre


_EP_POST_PRE_OUTS = [ROW_BF16, ROW_F32, ROW_BF16]


def _ep_final_loss(y, h, target, g_post):
    r = _rstd(y)
    yh = y * r
    err = h + yh * g_post - target
    dh = err * (1.0 / D)
    return _rowsum8(err * err), dh, _norm_bwd(yh, r, dh * g_post), _rowsum8(dh * yh)


def _ep_post_pre_bwd(du, dh_out, hn, y, g_post, g_pre):
    r2 = _rstd(hn)
    xh = hn * r2
    dh = dh_out + _norm_bwd(xh, r2, du * g_pre)
    yf = y.astype(F32)
    r1 = _rstd(yf)
    yh = yf * r1
    return dh, _norm_bwd(yh, r1, dh * g_post), _rowsum8(du * xh), _rowsum8(dh * yh)


_EP_POST_PRE_BWD_OUTS = [ROW_F32, ROW_BF16, SUM_F32, SUM_F32]


def _ep_pre_bwd(du, dh_out, x, g):
    r = _rstd(x)
    xh = x * r
    return dh_out + _norm_bwd(xh, r, du * g), _rowsum8(du * xh)


_EP_PRE_BWD_OUTS = [ROW_F32, SUM_F32]


def _prenorm(x, g, *, name, dep=None):
    t, d = x.shape
    tb = min(512, t)
    deps = [] if dep is None else [dep]

    def body(x_ref, g_ref, *rest):
        xf = x_ref[...]
        rest[-1][...] = (xf * _rstd(xf) * g_ref[...]).astype(BF16)

    return pl.pallas_call(
        body, name=name, out_shape=jax.ShapeDtypeStruct((t, d), BF16), grid=(t // tb,),
        in_specs=[pl.BlockSpec((tb, d), lambda i: (i, 0)), pl.BlockSpec((1, d), lambda i: (0, 0))]
        + [ANY_SPEC] * len(deps),
        out_specs=pl.BlockSpec((tb, d), lambda i: (i, 0)), compiler_params=_params(),
    )(x, g, *deps)


QB = 256


def _half_mask(shape, e):
    lane = lax.broadcasted_iota(jnp.int32, shape, len(shape) - 1)
    return (lane // 64) == e


def _place(kv):
    sw = pltpu.roll(kv, 64, 1)
    m0 = _half_mask(kv.shape, 0)
    return [[jnp.where(m0, kv, 0.0).astype(BF16), jnp.where(m0, 0.0, sw).astype(BF16)],
            [jnp.where(m0, sw, 0.0).astype(BF16), jnp.where(m0, 0.0, kv).astype(BF16)]]


SQ = 128
SK = 256


def _swa_valid(i, sb):
    qc = lax.broadcasted_iota(jnp.int32, (SQ, SK), 0) // CHUNK
    kc = lax.broadcasted_iota(jnp.int32, (SQ, SK), 1) // CHUNK - 2
    return (kc <= qc) & (qc <= kc + 2) & (4 * i + 2 * sb + kc >= 0)


def _swa_fwd(z, sinks, t, dep=None):
    nb = t // QB
    deps = [] if dep is None else [dep]

    def body(s_ref, q_ref, kp_ref, kc_ref, vp_ref, vc_ref, *rest):
        o_ref, lse_ref = rest[-2:]
        i = pl.program_id(0)
        kpl = _place(jnp.concatenate([kp_ref[...], kc_ref[...]], axis=0))
        vpl = _place(jnp.concatenate([vp_ref[...], vc_ref[...]], axis=0))
        lane = lax.broadcasted_iota(jnp.int32, (SQ, 128), 1)
        for sb in range(QB // SQ):
            rows, keys = slice(SQ * sb, SQ * (sb + 1)), slice(SQ * sb, SQ * sb + SK)
            valid = _swa_valid(i, sb)
            lse_out = jnp.zeros((SQ, 128), F32)
            for j in range(4):
                qp = q_ref[rows, 128 * j:128 * (j + 1)].astype(BF16)
                acc = jnp.zeros((SQ, 128), F32)
                for e in range(2):
                    h = 2 * j + e
                    kvh = h // 4
                    qm = jnp.where(_half_mask(qp.shape, e), qp, jnp.zeros_like(qp))
                    s = _dot(qm, kpl[kvh][e][keys], 1, 1) * 0.125
                    s = jnp.where(valid, s, NEG)
                    sink = s_ref[0, h]
                    m = jnp.maximum(jnp.max(s, axis=-1, keepdims=True), sink)
                    p = jnp.exp(s - m)
                    l = jnp.sum(p, axis=-1, keepdims=True) + jnp.exp(sink - m)
                    acc = acc + _dot(p.astype(BF16), vpl[kvh][e][keys], 1, 0) * (1.0 / l)
                    lse_out = jnp.where(lane == h, m + jnp.log(l), lse_out)
                o_ref[rows, 128 * j:128 * (j + 1)] = acc.astype(BF16)
            lse_ref[rows, :] = lse_out

    prev = lambda c: pl.BlockSpec((128, 128), lambda i: (jnp.maximum(2 * i - 1, 0), c))
    cur = lambda c: pl.BlockSpec((QB, 128), lambda i: (i, c))
    return pl.pallas_call(
        body, name="swa_fwd",
        out_shape=(jax.ShapeDtypeStruct((t, SWA_W), BF16), jax.ShapeDtypeStruct((t, 128), F32)),
        grid=(nb,),
        in_specs=[pl.BlockSpec(memory_space=pltpu.SMEM),
                  pl.BlockSpec((QB, SWA_W), lambda i: (i, 0)), prev(4), cur(4), prev(5), cur(5)]
        + [ANY_SPEC] * len(deps),
        out_specs=(pl.BlockSpec((QB, SWA_W), lambda i: (i, 0)), pl.BlockSpec((QB, 128), lambda i: (i, 0))),
        compiler_params=_params(),
    )(sinks, z, z, z, z, z, *deps)


def _swa_bwd(z, sinks, ymix, lse, dymix, t, dep=None):
    nb = t // QB
    deps = [] if dep is None else [dep]

    def body(s_ref, q_ref, kp_ref, kc_ref, vp_ref, vc_ref, o_ref, do_ref, l_ref, *rest):
        dq_ref, first_ref, second_ref, ds_ref, carry_ref = rest[len(deps):]
        i = pl.program_id(0)
        live = i < nb

        @pl.when(i == 0)
        def _():
            ds_ref[...] = jnp.zeros_like(ds_ref)
            carry_ref[...] = jnp.zeros_like(carry_ref)

        lane = lax.broadcasted_iota(jnp.int32, (8, 128), 1)
        kpl = _place(jnp.concatenate([kp_ref[...], kc_ref[...]], axis=0))
        vpl = _place(jnp.concatenate([vp_ref[...], vc_ref[...]], axis=0))
        nk = QB + 128
        qc = lax.broadcasted_iota(jnp.int32, (QB, nk), 0) // CHUNK
        kc = lax.broadcasted_iota(jnp.int32, (QB, nk), 1) // CHUNK - 2
        valid = (kc <= qc) & (qc <= kc + 2) & (4 * i + kc >= 0) & live
        lse_c = l_ref[...]
        dsink = jnp.zeros((8, 128), F32)
        dk_acc = [[jnp.zeros((128, nk), F32) for _ in range(2)] for _ in range(2)]
        dv_acc = [[jnp.zeros((128, nk), F32) for _ in range(2)] for _ in range(2)]
        dq = []
        for j in range(4):
            cols = slice(128 * j, 128 * (j + 1))
            qp = q_ref[:, cols].astype(BF16)
            dop = do_ref[:, cols]
            prod = dop.astype(F32) * o_ref[:, cols].astype(F32)
            acc = jnp.zeros((QB, 128), F32)
            for e in range(2):
                h = 2 * j + e
                kvh = h // 4
                hm = _half_mask(qp.shape, e)
                qm = jnp.where(hm, qp, jnp.zeros_like(qp))
                dom = jnp.where(hm, dop, jnp.zeros_like(dop))
                dd = jnp.sum(jnp.where(hm, prod, 0.0), axis=-1, keepdims=True)
                lse_h = lse_c[:, h:h + 1]
                s = _dot(qm, kpl[kvh][e], 1, 1) * 0.125
                p = jnp.where(valid, jnp.exp(s - lse_h), 0.0)
                dp = _dot(dom, vpl[kvh][e], 1, 1)
                ds = (p * (dp - dd) * 0.125).astype(BF16)
                acc = acc + _dot(ds, kpl[kvh][e], 1, 0)
                dk_acc[kvh][e] = dk_acc[kvh][e] + _dot(qm, ds, 0, 0)
                dv_acc[kvh][e] = dv_acc[kvh][e] + _dot(dom, p.astype(BF16), 0, 0)
                ps = jnp.where(live, jnp.exp(s_ref[0, h] - lse_h) * dd, 0.0)
                dsink = dsink - jnp.where(lane == h, _rowsum8(jnp.broadcast_to(ps, (QB, 128))), 0.0)
            dq.append(acc.astype(BF16))
        ds_ref[...] += dsink
        dk = (dk_acc[0][0] + dk_acc[1][1] + pltpu.roll(dk_acc[0][1] + dk_acc[1][0], 64, 0)).T
        dv = (dv_acc[0][0] + dv_acc[1][1] + pltpu.roll(dv_acc[0][1] + dv_acc[1][0], 64, 0)).T
        dkv = jnp.concatenate([dk, dv], axis=1)
        second_ref[...] = (carry_ref[...] + dkv[0:128]).astype(BF16)
        carry_ref[...] = dkv[256:384]

        @pl.when(live)
        def _():
            for j in range(4):
                dq_ref[:, 128 * j:128 * (j + 1)] = dq[j]
            first_ref[...] = dkv[128:256].astype(BF16)

    blk = lambda i: jnp.minimum(i, nb - 1)
    prev = lambda c: pl.BlockSpec((128, 128), lambda i: (jnp.maximum(2 * blk(i) - 1, 0), c))
    cur = lambda w, c: pl.BlockSpec((QB, w), lambda i: (blk(i), c))
    half = lambda index: pl.BlockSpec((128, 256), lambda i: (index(i), 0))
    return pl.pallas_call(
        body, name="swa_bwd",
        out_shape=(jax.ShapeDtypeStruct((t, SWA_W), BF16), jax.ShapeDtypeStruct((t // 2, 256), BF16),
                   jax.ShapeDtypeStruct((t // 2, 256), BF16), jax.ShapeDtypeStruct((8, 128), F32)),
        grid=(nb + 1,),
        in_specs=[pl.BlockSpec(memory_space=pltpu.SMEM),
                  cur(SWA_W, 0), prev(4), cur(128, 4), prev(5), cur(128, 5),
                  cur(SWA_W, 0), cur(SWA_W, 0), cur(128, 0)] + [ANY_SPEC] * len(deps),
        out_specs=(cur(SWA_W, 0), half(blk), half(lambda i: jnp.maximum(i - 1, 0)),
                   pl.BlockSpec((8, 128), lambda i: (0, 0))),
        scratch_shapes=[pltpu.VMEM((128, 256), F32)],
        compiler_params=_params(dimension_semantics=("arbitrary",)),
    )(sinks, z, z, z, z, z, ymix, dymix, lse, *deps)


HB = 256


def _lower_bound(lb_ref):
    a = lb_ref[...]
    a0, a1 = a[0:1], a[1:2]
    mx = jnp.maximum(a0, a1)
    e0, e1 = jnp.exp(a0 - mx), jnp.exp(a1 - mx)
    return e0 / (e0 + e1)


def _hgrn_cols(row_block):
    return [pl.BlockSpec((HB, 2 * HD), lambda j, c=base // (2 * HD) + p: (row_block(j), c))
            for base in (ZQH, ZFH, ZIH, ZGH) for p in range(2)]


NCH = HB // CHUNK


def _split3(x):
    hi = x.astype(BF16)
    r1 = x - hi.astype(F32)
    mid = r1.astype(BF16)
    return hi, mid, (r1 - mid.astype(F32)).astype(BF16)


def _blockdiag(lower):
    r = lax.broadcasted_iota(jnp.int32, (HB, HB), 0)
    c = lax.broadcasted_iota(jnp.int32, (HB, HB), 1)
    return (r // CHUNK == c // CHUNK) & ((c <= r) if lower else (c >= r))


def _chunk_sums(mask_bf16, x):
    return sum(_dot(mask_bf16, part, 1, 0) for part in _split3(x))


def _per_chunk_rows(x, row):
    w = x.shape[1]
    picked = x.reshape(NCH, CHUNK, w)[:, row:row + 1, :]
    return jnp.broadcast_to(picked, (NCH, CHUNK, w)).reshape(HB, w)


def _chunk_stack(x, chunk_of_row):
    return jnp.concatenate([jnp.where(chunk_of_row == c, x, jnp.zeros_like(x)) for c in range(NCH)], axis=1)


def _chunk_pick(x, chunk_of_row):
    w = x.shape[1] // NCH
    out = jnp.zeros((HB, w), x.dtype)
    for c in range(NCH):
        out = jnp.where(chunk_of_row == c, x[:, c * w:(c + 1) * w], out)
    return out


def _hgrn_local(q, f, kf, b):
    sq = _sig(q)
    qf = q * sq * (HD ** -0.5)
    b_mid = _per_chunk_rows(b, CHUNK // 2 - 1)
    b_last = _per_chunk_rows(b, CHUNK - 1)
    qm = qf * jnp.exp(b - b_mid)
    km = kf * jnp.exp(b_mid - b)
    kl = kf * jnp.exp(b_last - b)
    qb = qf * jnp.exp(b)
    return dict(sq=sq, b_mid=b_mid, b_last=b_last, qm=qm, km=km, kl=kl, qb=qb)


def _hgrn2_fwd(z, hgrn_lb, onorm, ya, x, wout, wq, g_post, g_pre, t, dep=None):
    nb = t // HB
    deps = [] if dep is None else [dep]

    def body(*refs):
        zq, zf, zi, zg = refs[0:2], refs[2:4], refs[4:6], refs[6:8]
        lb_ref, on_ref, ya_ref, x_ref, wout_ref, wq_ref, gp_ref, gnext_ref = refs[8:16]
        y_ref, o_ref, sp_ref, y1_ref, hn_ref, u_ref, q_ref, st_ref = refs[-8:]

        @pl.when(pl.program_id(0) == 0)
        def _():
            st_ref[...] = jnp.zeros_like(st_ref)

        y_ref[:, 0:SWA_W] = ya_ref[...]

        lb_all = _lower_bound(lb_ref)
        gn = on_ref[...]
        low = _blockdiag(True)
        low_b = low.astype(BF16)
        chunk_of_row = lax.broadcasted_iota(jnp.int32, (HB, HD), 0) // CHUNK
        for p in range(2):
            lbp = lb_all[:, 2 * HD * p:2 * HD * (p + 1)]
            fp = lbp + (1.0 - lbp) * _sig(zf[p][...])
            bp = _chunk_sums(low_b, jnp.log(fp))
            for e in range(2):
                h, ls = 2 * p + e, slice(e * HD, (e + 1) * HD)
                f = fp[:, ls]
                w = _hgrn_local(zq[p][:, ls], f, 1.0 - f, bp[:, ls])
                iv = zi[p][:, ls].astype(BF16)
                a = jnp.where(low, _dot(w["qm"].astype(BF16), w["km"].astype(BF16), 1, 1), 0.0)
                o = _dot(a.astype(BF16), iv, 1, 0)
                u = _dot(iv, _chunk_stack(w["kl"].astype(BF16), chunk_of_row), 0, 0)
                decay = jnp.exp(w["b_last"])
                st = st_ref[h]
                states = []
                for c in range(NCH):
                    sp_ref[h, c] = st
                    states.append(st.astype(BF16))
                    st = st * decay[c * CHUNK:c * CHUNK + 1] + u[:, c * HD:(c + 1) * HD]
                st_ref[h] = st
                inter = _dot(w["qb"].astype(BF16), jnp.concatenate(states, axis=0), 1, 1)
                o = o + _chunk_pick(inter, chunk_of_row)
                o_ref[:, h * HD:(h + 1) * HD] = o
                gg = zg[p][:, ls]
                y_ref[:, SWA_W + h * HD:SWA_W + (h + 1) * HD] = (o * _rstd(o) * gn * (gg * _sig(gg))).astype(BF16)
        y1, hn, u = _ep_post_pre(_dot(y_ref[...], wout_ref[...], 1, 0), x_ref[...], gp_ref[...], gnext_ref[...])
        y1_ref[...] = y1
        hn_ref[...] = hn
        u = u.astype(BF16)
        u_ref[...] = u
        q_ref[...] = _dot(u, wq_ref[...], 1, 0).astype(BF16)

    row = lambda w: pl.BlockSpec((HB, w), lambda j: (j, 0))
    whole = lambda a: pl.BlockSpec(a.shape, lambda j: (0,) * a.ndim, pipeline_mode=pl.Buffered(1))
    half = jax.ShapeDtypeStruct((t, D), BF16)
    return pl.pallas_call(
        body, name="hgrn_fwd",
        out_shape=(half, jax.ShapeDtypeStruct((t, HG_W), F32), jax.ShapeDtypeStruct((4, t // CHUNK, HD, HD), F32),
                   half, jax.ShapeDtypeStruct((t, D), F32), half, half),
        grid=(nb,),
        in_specs=_hgrn_cols(lambda j: j) + [whole(hgrn_lb), whole(onorm), row(SWA_W), row(D), whole(wout), whole(wq),
                                            whole(g_post), whole(g_pre)] + [ANY_SPEC] * len(deps),
        out_specs=(row(D), row(HG_W), pl.BlockSpec((4, NCH, HD, HD), lambda j: (0, j, 0, 0)),
                   row(D), row(D), row(D), row(D)),
        scratch_shapes=[pltpu.VMEM((4, HD, HD), F32)],
        compiler_params=_params(dimension_semantics=("arbitrary",)),
    )(*[z] * 8, hgrn_lb, onorm, ya, x, wout, wq, g_post, g_pre, *deps)


def _hgrn2_bwd(z, hgrn_lb, onorm, o_save, sprev, dymix, dza, t):
    nb = t // HB

    def body(*refs):
        zq, zf, zi, zg = refs[0:2], refs[2:4], refs[4:6], refs[6:8]
        (lb_ref, on_ref, o_ref, sp_ref, dy_ref, dqa_ref, first_ref, second_ref,
         dz_ref, dlb_ref, don_ref, dst_ref) = refs[8:]

        @pl.when(pl.program_id(0) == 0)
        def _():
            dst_ref[...] = jnp.zeros_like(dst_ref)
            dlb_ref[...] = jnp.zeros_like(dlb_ref)
            don_ref[...] = jnp.zeros_like(don_ref)

        dz_ref[:, 0:SWA_W] = dqa_ref[...]
        dz_ref[0:HB // 2, SWA_W:ZQH] = first_ref[...]
        dz_ref[HB // 2:HB, SWA_W:ZQH] = second_ref[...]
        lb_all = _lower_bound(lb_ref)
        gn = on_ref[...]
        low, upp = _blockdiag(True), _blockdiag(False)
        upp_b = upp.astype(BF16)
        low_b = low.astype(BF16)
        row = lax.broadcasted_iota(jnp.int32, (HB, HD), 0)
        chunk_of_row = row // CHUNK
        in_chunk = row % CHUNK
        for p in range(2):
            lbp = lb_all[:, 2 * HD * p:2 * HD * (p + 1)]
            sgp = _sig(zf[p][...])
            fp = lbp + (1.0 - lbp) * sgp
            bp = _chunk_sums(low_b, jnp.log(fp))
            db_pair, dkf_pair = [], []
            for e in range(2):
                h, ls, hs = 2 * p + e, slice(e * HD, (e + 1) * HD), slice((2 * p + e) * HD, (2 * p + e + 1) * HD)
                f = fp[:, ls]
                q = zq[p][:, ls]
                w = _hgrn_local(q, f, 1.0 - f, bp[:, ls])
                iv = zi[p][:, ls].astype(BF16)
                gg = zg[p][:, ls]
                o = o_ref[:, hs]
                dout = dy_ref[:, hs].astype(F32)
                sgg = _sig(gg)
                r = _rstd(o)
                oh = o * r
                dyn = dout * (gg * sgg)
                dz_ref[:, ZGH + h * HD:ZGH + (h + 1) * HD] = (
                    dout * oh * gn * (sgg * (1.0 + gg * (1.0 - sgg)))).astype(BF16)
                don_ref[...] += _rowsum8(dyn * oh)
                do = _norm_bwd(oh, r, dyn * gn).astype(BF16)
                qm, km, kl, qb = (w[n].astype(BF16) for n in ("qm", "km", "kl", "qb"))
                decay = jnp.exp(w["b_last"])
                grads_in = _dot(do, _chunk_stack(qb, chunk_of_row), 0, 0)
                dst = dst_ref[h]
                dstn, dd_rows = [None] * NCH, [None] * NCH
                for c in reversed(range(NCH)):
                    dstn[c] = dst.astype(BF16)
                    dd_rows[c] = jnp.sum(dst * sp_ref[h, c], axis=0, keepdims=True)
                    dst = dst * decay[c * CHUNK:c * CHUNK + 1] + grads_in[:, c * HD:(c + 1) * HD]
                dst_ref[h] = dst
                states = jnp.concatenate([sp_ref[h, c].astype(BF16) for c in range(NCH)], axis=0)
                dstn_all = jnp.concatenate(dstn, axis=0)
                dqb = _dot(_chunk_stack(do, chunk_of_row), states, 1, 0)
                at = jnp.where(upp, _dot(km, qm, 1, 1), 0.0)
                di = _dot(at.astype(BF16), do, 1, 0) + _chunk_pick(_dot(kl, dstn_all, 1, 1), chunk_of_row)
                dz_ref[:, ZIH + h * HD:ZIH + (h + 1) * HD] = di.astype(BF16)
                dkl = _dot(_chunk_stack(iv, chunk_of_row), dstn_all, 1, 0)
                da = jnp.where(low, _dot(do, iv, 1, 1), 0.0).astype(BF16)
                dat = jnp.where(upp, _dot(iv, do, 1, 1), 0.0).astype(BF16)
                dqm = _dot(da, km, 1, 0)
                dkm = _dot(dat, qm, 1, 0)
                b = bp[:, ls]
                e1, e2 = jnp.exp(b - w["b_mid"]), jnp.exp(w["b_mid"] - b)
                e3, e4 = jnp.exp(w["b_last"] - b), jnp.exp(b)
                dqf = dqm * e1 + dqb * e4
                dkf_pair.append(dkm * e2 + dkl * e3)
                t_qm, t_km, t_kl = dqm * w["qm"], dkm * w["km"], dkl * w["kl"]
                db = t_qm - t_km - t_kl + dqb * w["qb"]
                db_mid = jnp.sum((t_km - t_qm).reshape(NCH, CHUNK, HD), axis=1, keepdims=True)
                db_last = jnp.sum(t_kl.reshape(NCH, CHUNK, HD), axis=1, keepdims=True)
                db_last = db_last + jnp.stack(dd_rows, axis=0) * jnp.exp(
                    bp[:, ls].reshape(NCH, CHUNK, HD)[:, CHUNK - 1:CHUNK, :])
                spread = lambda v: jnp.broadcast_to(v, (NCH, CHUNK, HD)).reshape(HB, HD)
                db = (db + jnp.where(in_chunk == CHUNK // 2 - 1, spread(db_mid), 0.0)
                      + jnp.where(in_chunk == CHUNK - 1, spread(db_last), 0.0))
                db_pair.append(db)
                sq = w["sq"]
                dz_ref[:, ZQH + h * HD:ZQH + (h + 1) * HD] = (
                    dqf * (HD ** -0.5) * (sq * (1.0 + q * (1.0 - sq)))).astype(BF16)
            dlogf = _chunk_sums(upp_b, jnp.concatenate(db_pair, axis=1))
            dfv = dlogf / fp - jnp.concatenate(dkf_pair, axis=1)
            dz_ref[:, ZFH + 2 * HD * p:ZFH + 2 * HD * (p + 1)] = (dfv * (1.0 - lbp) * sgp * (1.0 - sgp)).astype(BF16)
            dlb_ref[:, 2 * HD * p:2 * HD * (p + 1)] += _rowsum8(dfv * (1.0 - sgp))

    rev = lambda j: nb - 1 - j
    return pl.pallas_call(
        body, name="hgrn_bwd",
        out_shape=(jax.ShapeDtypeStruct((t, D_IN), BF16), jax.ShapeDtypeStruct((8, HG_W), F32),
                   jax.ShapeDtypeStruct((8, HD), F32)),
        grid=(nb,),
        in_specs=_hgrn_cols(rev) + [pl.BlockSpec((2, HG_W), lambda j: (0, 0)), pl.BlockSpec((1, HD), lambda j: (0, 0)),
                                    pl.BlockSpec((HB, HG_W), lambda j: (rev(j), 0)),
                                    pl.BlockSpec((4, NCH, HD, HD), lambda j: (0, rev(j), 0, 0)),
                                    pl.BlockSpec((HB, HG_W), lambda j: (rev(j), 1)),
                                    pl.BlockSpec((HB, SWA_W), lambda j: (rev(j), 0)),
                                    pl.BlockSpec((HB // 2, 2 * KV_W), lambda j: (rev(j), 0)),
                                    pl.BlockSpec((HB // 2, 2 * KV_W), lambda j: (rev(j), 0))],
        out_specs=(pl.BlockSpec((HB, D_IN), lambda j: (rev(j), 0)), pl.BlockSpec((8, HG_W), lambda j: (0, 0)),
                   pl.BlockSpec((8, HD), lambda j: (0, 0))),
        scratch_shapes=[pltpu.VMEM((4, HD, HD), F32)],
        compiler_params=_params(dimension_semantics=("arbitrary",)),
    )(*[z] * 8, hgrn_lb, onorm, o_save, sprev, dymix, *dza)


XB = 512


def _xattn_fwd(q, k, v, wo, h, g_post, g_pre, t, dep=None):
    tb = min(XB, t)
    deps = [] if dep is None else [dep]

    def body(q_ref, k_ref, v_ref, wo_ref, h_ref, gp_ref, gn_ref, *rest):
        o_ref, y_ref, hn_ref, u_ref = rest[len(deps):]
        for hd in range(XH):
            cols = slice(XD * hd, XD * (hd + 1))
            s = _dot(q_ref[:, cols], k_ref[:, cols], 1, 1) * (XD ** -0.5)
            p = jnp.exp(s - jnp.max(s, axis=-1, keepdims=True))
            l = jnp.sum(p, axis=-1, keepdims=True)
            o_ref[:, cols] = (_dot(p.astype(BF16), v_ref[:, cols], 1, 0) * (1.0 / l)).astype(BF16)
        y, hn, u = _ep_post_pre(_dot(o_ref[...], wo_ref[...], 1, 0), h_ref[...], gp_ref[...], gn_ref[...])
        y_ref[...] = y
        hn_ref[...] = hn
        u_ref[...] = u.astype(BF16)

    row = pl.BlockSpec((tb, D), lambda i: (i, 0))
    whole = lambda a: pl.BlockSpec(a.shape, lambda i: (0,) * a.ndim, pipeline_mode=pl.Buffered(1))
    half = jax.ShapeDtypeStruct((t, D), BF16)
    return pl.pallas_call(
        body, name="xattn_fwd", out_shape=(half, half, jax.ShapeDtypeStruct((t, D), F32), half), grid=(t // tb,),
        in_specs=[row, whole(k), whole(v), whole(wo), row, whole(g_post), whole(g_pre)] + [ANY_SPEC] * len(deps),
        out_specs=(row, row, row, row), compiler_params=_params(),
    )(q, k, v, wo, h, g_post, g_pre, *deps)


def _xattn_bwd(q, k, v, do, wq, wout, dh_out, hn, y, g_post, g_pre, t):
    tb = min(XB, t)

    def body(q_ref, k_ref, v_ref, do_ref, wq_ref, wout_ref, dho_ref, hn_ref, y_ref, gp_ref, gn_ref,
             dq_ref, dk_ref, dv_ref, dh_ref, dyp_ref, dym_ref, dgn_ref, dgp_ref):
        @pl.when(pl.program_id(0) == 0)
        def _():
            dk_ref[...] = jnp.zeros_like(dk_ref)
            dv_ref[...] = jnp.zeros_like(dv_ref)
            dgn_ref[...] = jnp.zeros_like(dgn_ref)
            dgp_ref[...] = jnp.zeros_like(dgp_ref)

        for h in range(XH):
            cols = slice(XD * h, XD * (h + 1))
            qh, kh, vh, doh = q_ref[:, cols], k_ref[:, cols], v_ref[:, cols], do_ref[:, cols]
            s = _dot(qh, kh, 1, 1) * (XD ** -0.5)
            p = jnp.exp(s - jnp.max(s, axis=-1, keepdims=True))
            p = p * (1.0 / jnp.sum(p, axis=-1, keepdims=True))
            dp = _dot(doh, vh, 1, 1)
            ds = (p * (dp - jnp.sum(p * dp, axis=-1, keepdims=True)) * (XD ** -0.5)).astype(BF16)
            dq_ref[:, cols] = _dot(ds, kh, 1, 0).astype(BF16)
            dk_ref[:, cols] += _dot(ds, qh, 0, 0)
            dv_ref[:, cols] += _dot(p.astype(BF16), doh, 0, 0)
        du = _dot(dq_ref[...], wq_ref[...], 1, 1)
        dh, dyp, dgn, dgp = _ep_post_pre_bwd(du, dho_ref[...], hn_ref[...], y_ref[...], gp_ref[...], gn_ref[...])
        dh_ref[...] = dh
        dyp = dyp.astype(BF16)
        dyp_ref[...] = dyp
        dym_ref[...] = _dot(dyp, wout_ref[...], 1, 1).astype(BF16)
        dgn_ref[...] += dgn
        dgp_ref[...] += dgp

    row = pl.BlockSpec((tb, D), lambda i: (i, 0))
    mem = pl.BlockSpec(k.shape, lambda i: (0, 0))
    whole = lambda a: pl.BlockSpec(a.shape, lambda i: (0,) * a.ndim, pipeline_mode=pl.Buffered(1))
    acc = pl.BlockSpec((8, D), lambda i: (0, 0))
    half = jax.ShapeDtypeStruct((t, D), BF16)
    return pl.pallas_call(
        body, name="xattn_bwd",
        out_shape=(half, jax.ShapeDtypeStruct(k.shape, F32), jax.ShapeDtypeStruct(k.shape, F32),
                   jax.ShapeDtypeStruct((t, D), F32), half, half,
                   jax.ShapeDtypeStruct((8, D), F32), jax.ShapeDtypeStruct((8, D), F32)),
        grid=(t // tb,),
        in_specs=[row, whole(k), whole(v), row, whole(wq), whole(wout), row, row, row, whole(g_post), whole(g_pre)],
        out_specs=(row, mem, mem, row, row, row, acc, acc),
        compiler_params=_params(dimension_semantics=("arbitrary",)),
    )(q, k, v, do, wq, wout, dh_out, hn, y, g_post, g_pre)


def _mem_kv(mem, g_mem, wk, wv):
    def body(m_ref, g_ref, wk_ref, wv_ref, mn_ref, k_ref, v_ref):
        m_ = m_ref[...]
        mn = (m_ * _rstd(m_) * g_ref[...]).astype(BF16)
        mn_ref[...] = mn
        k_ref[...] = _dot(mn, wk_ref[...], 1, 0).astype(BF16)
        v_ref[...] = _dot(mn, wv_ref[...], 1, 0).astype(BF16)

    return pl.pallas_call(body, name="mem_kv", out_shape=(jax.ShapeDtypeStruct(mem.shape, BF16),) * 3,
                          compiler_params=_params())(mem, g_mem, wk, wv)


def _mem_kv_bwd(mn, mem, dk, dv, wk, wv, dep=None):
    deps = [] if dep is None else [dep]

    def body(mn_ref, m_ref, dk_ref, dv_ref, wk_ref, wv_ref, *rest):
        gk_ref, gv_ref, dg_ref = rest[len(deps):]
        mn = mn_ref[...]
        dkb, dvb = dk_ref[...].astype(BF16), dv_ref[...].astype(BF16)
        gk_ref[...] = _dot(mn, dkb, 0, 0).astype(BF16)
        gv_ref[...] = _dot(mn, dvb, 0, 0).astype(BF16)
        dmn = _dot(dkb, wk_ref[...], 1, 1) + _dot(dvb, wv_ref[...], 1, 1)
        m_ = m_ref[...]
        dg_ref[...] = _rowsum8(dmn * (m_ * _rstd(m_)))

    vmem = pl.BlockSpec(memory_space=pltpu.VMEM)
    return pl.pallas_call(
        body, name="mem_kv_bwd",
        out_shape=(jax.ShapeDtypeStruct(wk.shape, BF16), jax.ShapeDtypeStruct(wv.shape, BF16),
                   jax.ShapeDtypeStruct((8, D), F32)),
        in_specs=[vmem] * 6 + [ANY_SPEC] * len(deps), out_specs=(vmem,) * 3, compiler_params=_params(),
    )(mn, mem, dk, dv, wk, wv, *deps)


FB = 256


def _ffn_fwd_loss(u, wgt, wut, wd, h, target, g_post, t):
    tb = min(FB, t)

    def body(u_ref, wg_ref, wu_ref, wd_ref, h_ref, t_ref, gp_ref, g_ref, up_ref, a_ref, sq_ref, dh_ref, dy_ref, dg_ref):
        @pl.when(pl.program_id(0) == 0)
        def _():
            sq_ref[...] = jnp.zeros_like(sq_ref)
            dg_ref[...] = jnp.zeros_like(dg_ref)

        u_ = u_ref[...]
        g = _dot(u_, wg_ref[...], 1, 1)
        up = _dot(u_, wu_ref[...], 1, 1)
        a = (g * _sig(g) * up).astype(BF16)
        g_ref[...] = g.astype(BF16)
        up_ref[...] = up.astype(BF16)
        a_ref[...] = a
        sq, dh, dy, dg = _ep_final_loss(_dot(a, wd_ref[...], 1, 0), h_ref[...], t_ref[...], gp_ref[...])
        sq_ref[...] += sq
        dh_ref[...] = dh
        dy_ref[...] = dy.astype(BF16)
        dg_ref[...] += dg

    row = lambda w: pl.BlockSpec((tb, w), lambda i: (i, 0))
    whole = lambda a: pl.BlockSpec(a.shape, lambda i: (0,) * a.ndim, pipeline_mode=pl.Buffered(1))
    acc = pl.BlockSpec((8, D), lambda i: (0, 0))
    wide = jax.ShapeDtypeStruct((t, D_FF), BF16)
    return pl.pallas_call(
        body, name="ffn_fwd_loss",
        out_shape=(wide, wide, wide, jax.ShapeDtypeStruct((8, D), F32), jax.ShapeDtypeStruct((t, D), F32),
                   jax.ShapeDtypeStruct((t, D), BF16), jax.ShapeDtypeStruct((8, D), F32)),
        grid=(t // tb,),
        in_specs=[row(D), whole(wgt), whole(wut), whole(wd), row(D), row(D), whole(g_post)],
        out_specs=(row(D_FF), row(D_FF), row(D_FF), acc, row(D), row(D), acc),
        compiler_params=_params(dimension_semantics=("arbitrary",)),
    )(u, wgt, wut, wd, h, target, g_post)


def _ffn_bwd(dy, wd, gate, up, wgt, wut, dh_out, hn, y, g_post, g_pre, wo, t, dep=None):
    tb = min(FB, t)
    deps = [] if dep is None else [dep]

    def body(dy_ref, wd_ref, g_ref, up_ref, wg_ref, wu_ref, dho_ref, hn_ref, y_ref, gp_ref, gn_ref, wo_ref, *rest):
        dg_ref, dup_ref, dh_ref, dyp_ref, do_ref, dgn_ref, dgp_ref = rest[len(deps):]

        @pl.when(pl.program_id(0) == 0)
        def _():
            dgn_ref[...] = jnp.zeros_like(dgn_ref)
            dgp_ref[...] = jnp.zeros_like(dgp_ref)

        da = _dot(dy_ref[...], wd_ref[...], 1, 1)
        g = g_ref[...].astype(F32)
        sg = _sig(g)
        dup = (da * g * sg).astype(BF16)
        dgate = (da * up_ref[...].astype(F32) * (sg * (1.0 + g * (1.0 - sg)))).astype(BF16)
        dup_ref[...] = dup
        dg_ref[...] = dgate
        du = _dot(dgate, wg_ref[...], 1, 0) + _dot(dup, wu_ref[...], 1, 0)
        dh, dyp, dgn, dgp = _ep_post_pre_bwd(du, dho_ref[...], hn_ref[...], y_ref[...], gp_ref[...], gn_ref[...])
        dh_ref[...] = dh
        dyp = dyp.astype(BF16)
        dyp_ref[...] = dyp
        do_ref[...] = _dot(dyp, wo_ref[...], 1, 1).astype(BF16)
        dgn_ref[...] += dgn
        dgp_ref[...] += dgp

    row = lambda w: pl.BlockSpec((tb, w), lambda i: (i, 0))
    whole = lambda a: pl.BlockSpec(a.shape, lambda i: (0,) * a.ndim, pipeline_mode=pl.Buffered(1))
    acc = pl.BlockSpec((8, D), lambda i: (0, 0))
    return pl.pallas_call(
        body, name="ffn_bwd",
        out_shape=(jax.ShapeDtypeStruct((t, D_FF), BF16), jax.ShapeDtypeStruct((t, D_FF), BF16),
                   jax.ShapeDtypeStruct((t, D), F32), jax.ShapeDtypeStruct((t, D), BF16),
                   jax.ShapeDtypeStruct((t, D), BF16), jax.ShapeDtypeStruct((8, D), F32),
                   jax.ShapeDtypeStruct((8, D), F32)),
        grid=(t // tb,),
        in_specs=[row(D), whole(wd), row(D_FF), row(D_FF), whole(wgt), whole(wut), row(D), row(D), row(D),
                  whole(g_post), whole(g_pre), whole(wo)] + [ANY_SPEC] * len(deps),
        out_specs=(row(D_FF), row(D_FF), row(D), row(D), row(D), acc, acc),
        compiler_params=_params(dimension_semantics=("arbitrary",)),
    )(dy, wd, gate, up, wgt, wut, dh_out, hn, y, g_post, g_pre, wo, *deps)


def _local_step(x, mem, target, fetch, sm, emit=None, first_dep=None, milestone=None):
    t = x.shape[0]
    w, gw = {}, {}

    def out(key, g):
        gw[key] = g
        return None if emit is None else emit(key, g)

    def tell(tag, value):
        return None if milestone is None else milestone(tag, value)
    u1 = _prenorm(x, sm["g_mix_pre"], name="prenorm_mix", dep=first_dep)
    w["winT"] = fetch("winT", u1)
    z = _mm(u1, w["winT"], tb=True, out_dtype=F32, tm=1024, tn=1408, name="mm_z", n_outer=True)
    ya, lse = _swa_fwd(z, sm["sinks"], t, dep=tell("z", z))
    for key in ("wout", "wq"):
        w[key] = fetch(key, lse)
    ymix, o_h, sprev, y1, h1, u2, qx = _hgrn2_fwd(
        z, sm["hgrn_lb"], sm["hgrn_onorm"], ya, x, w["wout"], w["wq"], sm["g_mix_post"], sm["g_x_pre"], t,
        dep=tell("swa", lse))
    for key in ("wk", "wv", "wo"):
        w[key] = fetch(key, qx)
    mn, kx, vx = _mem_kv(mem, sm["g_mem"], w["wk"], w["wv"])
    ox, y2, h2, u3 = _xattn_fwd(qx, kx, vx, w["wo"], h1, sm["g_x_post"], sm["g_ffn_pre"], t, dep=tell("kv", kx))
    for key in ("wgT", "wuT", "wd"):
        w[key] = fetch(key, u3)
    gate, up, act, sq, dh3, dy3, dg_ffn_post = _ffn_fwd_loss(u3, w["wgT"], w["wuT"], w["wd"], h2, target,
                                                             sm["g_ffn_post"], t)
    dep = out("wd", _mm(act, dy3, ta=True, out_dtype=BF16, tm=1408, tn=1024, name="mm_gwd"))
    dgate, dup, dh2, dy2, dox, dg_ffn_pre, dg_x_post = _ffn_bwd(
        dy3, w["wd"], gate, up, w["wgT"], w["wuT"], dh3, h2, y2, sm["g_x_post"], sm["g_ffn_pre"], w["wo"], t, dep=dep)
    dep = out("wgT", _mm(dgate, u3, ta=True, out_dtype=BF16, tm=1408, tn=1024, name="mm_gwg"))
    dep = out("wuT", _mm(dup, u3, ta=True, out_dtype=BF16, tm=1408, tn=1024, name="mm_gwu", dep=dep))
    out("wo", _mm(ox, dy2, ta=True, out_dtype=BF16, tm=512, tn=1024, name="mm_gwo", dep=dep))
    dqx, dkx, dvx, dh1, dy1, dymix, dg_x_pre, dg_mix_post = _xattn_bwd(
        qx, kx, vx, dox, w["wq"], w["wout"], dh2, h1, y1, sm["g_mix_post"], sm["g_x_pre"], t)
    out("wq", _mm(u2, dqx, ta=True, out_dtype=BF16, tm=512, tn=1024, name="mm_gwq"))
    gwk, gwv, dg_mem = _mem_kv_bwd(mn, mem, dkx, dvx, w["wk"], w["wv"])
    out("wk", gwk)
    dep = out("wv", gwv)
    dep = out("wout", _mm(ymix, dy1, ta=True, out_dtype=BF16, tm=512, tn=1024, name="mm_gwout", dep=dep))
    *dza, dsinks = _swa_bwd(z, sm["sinks"], ymix, lse, dymix, t, dep=dep)
    dz, dlb, donorm = _hgrn2_bwd(z, sm["hgrn_lb"], sm["hgrn_onorm"], o_h, sprev, dymix, dza, t)
    dep = out("winT", _mm(dz, u1, ta=True, out_dtype=BF16, tm=1408, tn=1024, name="mm_gwin"))
    grad_x, dg_mix_pre = _mm_rows([(dz, w["winT"], False)], [dh1, x], [sm["g_mix_pre"]], _ep_pre_bwd,
                                  _EP_PRE_BWD_OUTS, tm=512, name="mm_du1_pre_bwd", dep=dep)
    parts = dict(g_mix_pre=dg_mix_pre, g_mix_post=dg_mix_post, g_mem=dg_mem, g_x_pre=dg_x_pre,
                 g_x_post=dg_x_post, g_ffn_pre=dg_ffn_pre, g_ffn_post=dg_ffn_post,
                 hgrn_onorm=donorm, hgrn_lb=dlb, sinks=dsinks, sq=sq)
    return grad_x, gw, parts


def _position():
    return lax.axis_index("x"), lax.axis_index("y"), lax.axis_index("c")


def _peer(pos, k):
    x, y, c = pos
    return (1 - x if k & 4 else x, 1 - y if k & 2 else y, 1 - c if k & 1 else c)


def _linear(pos):
    x, y, c = pos
    return 4 * x + 2 * y + c


HBM_SPEC = pl.BlockSpec(memory_space=pltpu.HBM)
SEM_SPEC = pl.BlockSpec(memory_space=pltpu.SEMAPHORE)
DATAFLOW = pltpu.SideEffectType.DATAFLOW_SIDE_EFFECTING
SEND_ORDER = (1, 2, 4, 3, 5, 6, 7)


def _in_hbm(a):
    return pltpu.with_memory_space_constraint(a, pltpu.HBM)


def _prepare_weights(shards, *, name, dep=None):
    n = len(shards)
    deps = [] if dep is None else [dep]

    def body(*refs):
        ins, (outs, lands, sem) = refs[:n], (refs[-2 * n - 1:-n - 1], refs[-n - 1:-1], refs[-1])
        me_lin = _linear(_position())
        copies = []
        for a in range(n):
            r = ins[a].shape[0]
            outs[a][...] = ins[a][...].astype(BF16)
            copies.append(pltpu.make_async_copy(outs[a], lands[a].at[pl.ds(me_lin * r, r), :], sem.at[a]))
            copies[-1].start()
        for cp in copies:
            cp.wait()

    vmem = pl.BlockSpec(memory_space=pltpu.VMEM)
    res = pl.pallas_call(
        body, name=name,
        out_shape=tuple(jax.ShapeDtypeStruct(s.shape, BF16) for s in shards)
        + tuple(jax.ShapeDtypeStruct((N_DEV * s.shape[0], s.shape[1]), BF16) for s in shards),
        in_specs=[vmem] * n + [ANY_SPEC] * len(deps), out_specs=tuple([vmem] * n + [ANY_SPEC] * n),
        scratch_shapes=[pltpu.SemaphoreType.DMA((n,))], compiler_params=_params(),
    )(*shards, *deps)
    return res[:n], res[n:]


def _copies_start(arrays, plan, n, *, name):
    na = len(arrays)

    def body(*refs):
        ins, send_sems, recv_sems = refs[:na], refs[na], refs[na + 1]
        me = _position()
        for j in range(n):
            src, dst, peer, _ = plan(ins, me, j)
            pltpu.make_async_remote_copy(src_ref=src, dst_ref=dst, send_sem=send_sems.at[j], recv_sem=recv_sems.at[j],
                                         device_id=peer, device_id_type=MESH).start()

    return pl.pallas_call(
        body, name=name,
        out_shape=(pltpu.SemaphoreType.DMA((n,)), pltpu.SemaphoreType.DMA((n,)))
        + tuple(pltpu.HBM(a.shape, a.dtype) for a in arrays),
        in_specs=(HBM_SPEC,) * na, out_specs=(SEM_SPEC, SEM_SPEC) + (HBM_SPEC,) * na,
        input_output_aliases={i: 2 + i for i in range(na)},
        compiler_params=pltpu.CompilerParams(has_side_effects=DATAFLOW),
    )(*[_in_hbm(a) for a in arrays])


def _copies_wait(send_sems, recv_sems, arrays, plan, n, after, *, name):
    na = len(arrays)

    def body(*refs):
        ins, send_sems, recv_sems = refs[:na], refs[na], refs[na + 1]
        me = _position()
        for j in range(n):
            src, _, peer, landed = plan(ins, me, j)
            copy = pltpu.make_async_remote_copy(src_ref=src, dst_ref=landed, send_sem=send_sems.at[j],
                                                recv_sem=recv_sems.at[j], device_id=peer, device_id_type=MESH)
            copy.wait_send()
            copy.wait_recv()

    return pl.pallas_call(
        body, name=name, out_shape=tuple(pltpu.HBM(a.shape, a.dtype) for a in arrays),
        in_specs=(HBM_SPEC,) * na + (SEM_SPEC, SEM_SPEC, ANY_SPEC), out_specs=(HBM_SPEC,) * na,
        input_output_aliases={i: i for i in range(na)},
        compiler_params=pltpu.CompilerParams(has_side_effects=DATAFLOW),
    )(*arrays, send_sems, recv_sems, after)


SAME_CORE = (2, 4, 6)


class _TwoLevelGather:
    def __init__(self, shards, lands, *, name):
        n = self.n = len(shards)
        self.name = name
        first_peers = (1,) + SAME_CORE

        def rows(ref, pos):
            r = ref.shape[0] // N_DEV
            return ref.at[pl.ds(_linear(pos) * r, r), :]

        def first(refs, me, j):
            a, peer = j // 4, _peer(me, first_peers[j % 4])
            return refs[a], rows(refs[n + a], me), peer, rows(refs[n + a], peer)

        def second(refs, me, j):
            a, sibling = j // 3, _peer(me, 1)
            mine = rows(refs[a], _peer(me, SAME_CORE[j % 3]))
            return mine, mine, sibling, rows(refs[a], _peer(sibling, SAME_CORE[j % 3]))

        self._first, self._second = first, second
        self._flight = _copies_start(list(shards) + list(lands), first, 4 * n, name=name + "_send")
        self.dep = self._flight[2]

    def pass_on(self, after):
        send1, recv1, *arrays = self._flight
        arrays = _copies_wait(send1, recv1, arrays, self._first, 4 * self.n, after, name=self.name + "_recv")
        self._flight = _copies_start(list(arrays[self.n:]), self._second, 3 * self.n, name=self.name + "_pass")
        return self._flight[2]

    def finish(self, after):
        send2, recv2, *lands = self._flight
        return _copies_wait(send2, recv2, lands, self._second, 3 * self.n, after, name=self.name + "_pass_recv")


def _exchange_start(gs, *, name):
    n = len(gs)
    rows = [g.shape[0] // N_DEV for g in gs]
    lands = [lax.empty((N_DEV - 1, r, g.shape[1]), g.dtype) for g, r in zip(gs, rows)]

    def body(*refs):
        g_refs, land_refs = refs[:n], refs[n:2 * n]
        send_sems, recv_sems = refs[2 * n:3 * n], refs[3 * n:4 * n]
        me = _position()
        for a in range(n):
            for k in SEND_ORDER:
                peer = _peer(me, k)
                pltpu.make_async_remote_copy(
                    src_ref=g_refs[a].at[pl.ds(_linear(peer) * rows[a], rows[a]), :],
                    dst_ref=land_refs[a].at[k - 1],
                    send_sem=send_sems[a].at[k - 1], recv_sem=recv_sems[a].at[k - 1],
                    device_id=peer, device_id_type=MESH).start()

    res = pl.pallas_call(
        body, name=name,
        out_shape=tuple(pltpu.SemaphoreType.DMA((N_DEV - 1,)) for _ in range(2 * n))
        + tuple(pltpu.HBM(a.shape, a.dtype) for a in gs + lands),
        in_specs=(HBM_SPEC,) * (2 * n), out_specs=(SEM_SPEC,) * (2 * n) + (HBM_SPEC,) * (2 * n),
        input_output_aliases={i: 2 * n + i for i in range(2 * n)},
        compiler_params=pltpu.CompilerParams(has_side_effects=DATAFLOW),
    )(*[_in_hbm(a) for a in gs + lands])
    return [(res[a], res[n + a], res[2 * n + a], res[3 * n + a]) for a in range(n)]


def _exchange_wait(send_sems, recv_sems, g_thru, land_thru, after, *, name):
    r = land_thru.shape[1]

    def body(g_ref, land_ref, send_sems, recv_sems, after_ref, g_dead, got_ref):
        del after_ref, g_dead, got_ref
        me = _position()
        for k in SEND_ORDER:
            peer = _peer(me, k)
            copy = pltpu.make_async_remote_copy(
                src_ref=g_ref.at[pl.ds(_linear(peer) * r, r), :], dst_ref=land_ref.at[k - 1],
                send_sem=send_sems.at[k - 1], recv_sem=recv_sems.at[k - 1],
                device_id=peer, device_id_type=MESH)
            copy.wait_send()
            copy.wait_recv()

    return pl.pallas_call(
        body, name=name,
        out_shape=(pltpu.HBM(g_thru.shape, g_thru.dtype), pltpu.HBM(land_thru.shape, land_thru.dtype)),
        in_specs=(HBM_SPEC, HBM_SPEC, SEM_SPEC, SEM_SPEC, pl.BlockSpec(memory_space=pl.ANY)),
        out_specs=(HBM_SPEC, HBM_SPEC), input_output_aliases={0: 0, 1: 1},
        compiler_params=pltpu.CompilerParams(has_side_effects=DATAFLOW),
    )(g_thru, land_thru, send_sems, recv_sems, after)


def _adamw_math(w, g, m, v):
    m = B1 * m + (1.0 - B1) * g
    v = B2 * v + (1.0 - B2) * (g * g)
    delta = -LR * ((m / C1) / (jnp.sqrt(v / C2) + AEPS) + WD * w)
    return delta, m, v


def _sum_adamw(items, *, name):
    n = len(items)

    def body(*refs):
        ins, outs, scratch = refs[:5 * n], refs[5 * n:9 * n], refs[9 * n:]
        me_lin = _linear(_position())
        mine = []
        for a in range(n):
            r = items[a][2].shape[0]
            mine.append(pltpu.make_async_copy(ins[5 * a].at[pl.ds(me_lin * r, r), :], scratch[a], scratch[n].at[a]))
            mine[-1].start()
        for a in range(n):
            _, land_ref, w_ref, m_ref, v_ref = ins[5 * a:5 * a + 5]
            g_ref, d_ref, nm_ref, nv_ref = outs[4 * a:4 * a + 4]
            g = land_ref[0].astype(F32)
            for s in range(1, N_DEV - 1):
                g = g + land_ref[s].astype(F32)
            mine[a].wait()
            g = scratch[a][...].astype(F32) + g
            g_ref[...] = g
            d_ref[...], nm_ref[...], nv_ref[...] = _adamw_math(w_ref[...], g, m_ref[...], v_ref[...])

    vmem = pl.BlockSpec(memory_space=pltpu.VMEM)
    res = pl.pallas_call(
        body, name=name,
        out_shape=tuple(jax.ShapeDtypeStruct(it[2].shape, F32) for it in items for _ in range(4)),
        in_specs=[ANY_SPEC, vmem, vmem, vmem, vmem] * n, out_specs=(vmem,) * (4 * n),
        scratch_shapes=[pltpu.VMEM(it[2].shape, BF16) for it in items] + [pltpu.SemaphoreType.DMA((n,))],
        compiler_params=_params(),
    )(*[a for it in items for a in it])
    return [res[4 * a:4 * a + 4] for a in range(n)]


SMALL = ("g_mix_pre", "g_mix_post", "g_mem", "g_x_pre", "g_x_post", "g_ffn_pre", "g_ffn_post",
         "hgrn_onorm", "hgrn_lb", "sinks")
SMALL_W = dict(hgrn_onorm=HD, hgrn_lb=HG_W, sinks=8)
SQ_ROW = len(SMALL)
PACK_ROWS = 16


def _small_pack(parts):
    ns = len(SMALL)

    def body(*refs):
        part, mine, slots, sem = refs[:ns + 1], refs[ns + 1], refs[ns + 2], refs[ns + 3]
        mine[...] = jnp.zeros((PACK_ROWS, D), F32)
        for r, name in enumerate(SMALL):
            wd = SMALL_W.get(name, D)
            mine[r:r + 1, 0:wd] = jnp.sum(part[r][...], axis=0, keepdims=True)[:, 0:wd]
        sq = jnp.sum(part[ns][...]) * (0.5 / D)
        mine[SQ_ROW:SQ_ROW + 1, :] = jnp.full((1, D), sq, F32)
        own = pltpu.make_async_copy(mine, slots.at[_linear(_position())], sem)
        own.start()
        own.wait()

    vmem = pl.BlockSpec(memory_space=pltpu.VMEM)
    return pl.pallas_call(
        body, name="small_pack",
        out_shape=(jax.ShapeDtypeStruct((PACK_ROWS, D), F32), jax.ShapeDtypeStruct((N_DEV, PACK_ROWS, D), F32)),
        in_specs=[vmem] * (ns + 1), out_specs=(vmem, ANY_SPEC),
        scratch_shapes=[pltpu.SemaphoreType.DMA(())], compiler_params=_params(),
    )(*[parts[n] for n in SMALL], parts["sq"])


def _small_exchange(mine, slots):
    def plan(refs, me, j):
        peer = _peer(me, j + 1)
        return refs[0], refs[1].at[_linear(me)], peer, refs[1].at[_linear(peer)]

    send, recv, mine1, slots1 = _copies_start([mine, slots], plan, N_DEV - 1, name="small_send")
    return lambda after: _copies_wait(send, recv, [mine1, slots1], plan, N_DEV - 1, after, name="small_recv")[1]


def _small_update(slots, sm, m_sm, v_sm):
    ns = len(SMALL)

    def body(*refs):
        tot = refs[0][0]
        for s in range(1, N_DEV):
            tot = tot + refs[0][s]
        w_refs, m_refs, v_refs = refs[1:ns + 1], refs[ns + 1:2 * ns + 1], refs[2 * ns + 1:3 * ns + 1]
        outs = refs[3 * ns + 1:]
        loss_ref = outs[0]
        g_out, d_out = outs[1:ns + 1], outs[ns + 1:2 * ns + 1]
        nm_out, nv_out = outs[2 * ns + 1:3 * ns + 1], outs[3 * ns + 1:4 * ns + 1]
        loss_ref[...] = tot[SQ_ROW:SQ_ROW + 1, 0:1]
        for r, name in enumerate(SMALL):
            wd = SMALL_W.get(name, D)
            g = tot[r:r + 1, 0:wd]
            w = w_refs[r][...]
            if name == "hgrn_lb":
                mx = jnp.maximum(w[0:1], w[1:2])
                e0, e1 = jnp.exp(w[0:1] - mx), jnp.exp(w[1:2] - mx)
                lb0 = e0 / (e0 + e1)
                g0 = g * lb0 * (1.0 - lb0)
                for i, gi in enumerate((g0, -g0)):
                    d, nm, nv = _adamw_math(w[i:i + 1], gi, m_refs[r][i:i + 1, :], v_refs[r][i:i + 1, :])
                    g_out[r][i:i + 1, :] = gi
                    d_out[r][i:i + 1, :], nm_out[r][i:i + 1, :], nv_out[r][i:i + 1, :] = d, nm, nv
            else:
                d, nm, nv = _adamw_math(w, g, m_refs[r][...], v_refs[r][...])
                g_out[r][...] = g
                d_out[r][...], nm_out[r][...], nv_out[r][...] = d, nm, nv

    shapes = [jax.ShapeDtypeStruct(sm[n].shape, F32) for n in SMALL]
    res = pl.pallas_call(
        body, name="small_update", out_shape=tuple([jax.ShapeDtypeStruct((1, 1), F32)] + shapes * 4),
        compiler_params=_params(),
    )(slots, *[sm[n] for n in SMALL], *[m_sm[n] for n in SMALL], *[v_sm[n] for n in SMALL])
    groups = [dict(zip(SMALL, res[1 + i * ns:1 + (i + 1) * ns])) for i in range(4)]
    return res[0], groups[0], groups[1], groups[2], groups[3]


BIG = ("w_in", "w_gate", "w_up", "w_down", "w_out", "wq_x", "wk_x", "wv_x", "wo_x")
BIG_KEY = dict(w_in="winT", w_gate="wgT", w_up="wuT", w_down="wd", w_out="wout", wq_x="wq", wk_x="wk",
               wv_x="wv", wo_x="wo")
TRANSPOSED = ("w_in", "w_gate", "w_up")
WEIGHTS = ("w_in", "sinks", "hgrn_lb", "hgrn_onorm", "w_out", "g_mix_pre", "g_mix_post", "g_mem", "g_x_pre",
           "g_x_post", "wq_x", "wk_x", "wv_x", "wo_x", "g_ffn_pre", "g_ffn_post", "w_gate", "w_up", "w_down")


def kernel(x, mem, w_in, sinks, hgrn_lb, hgrn_onorm, w_out, g_mix_pre, g_mix_post, g_mem, g_x_pre, g_x_post, wq_x, wk_x, wv_x, wo_x, g_ffn_pre, g_ffn_post, w_gate, w_up, w_down, loss_target, m_w_in, m_sinks, m_hgrn_lb, m_hgrn_onorm, m_w_out, m_g_mix_pre, m_g_mix_post, m_g_mem, m_g_x_pre, m_g_x_post, m_wq_x, m_wk_x, m_wv_x, m_wo_x, m_g_ffn_pre, m_g_ffn_post, m_w_gate, m_w_up, m_w_down, v_w_in, v_sinks, v_hgrn_lb, v_hgrn_onorm, v_w_out, v_g_mix_pre, v_g_mix_post, v_g_mem, v_g_x_pre, v_g_x_post, v_wq_x, v_wk_x, v_wv_x, v_wo_x, v_g_ffn_pre, v_g_ffn_post, v_w_gate, v_w_up, v_w_down):
    given = dict(locals())
    wts = {n: given[n] for n in WEIGHTS}
    ms = {n: given["m_" + n] for n in WEIGHTS}
    vs = {n: given["v_" + n] for n in WEIGHTS}

    def mat(a, name):
        a = a[0]
        return a.T if name in TRANSPOSED else a

    groups = (("w_in",), ("w_out", "wq_x"), ("wk_x", "wv_x", "wo_x"), ("w_gate", "w_up", "w_down"))
    gathers = []

    def start_group(g, dep):
        tag = ("w_in", "w_mix", "w_attn", "w_ffn")[g]
        shards, lands = _prepare_weights([mat(wts[n], n) for n in groups[g]], name="prepare_" + tag, dep=dep)
        gathers.append(_TwoLevelGather(shards, lands, name=tag))
        return gathers[-1].dep

    first_dep = start_group(2, start_group(1, start_group(0, None)))
    name_of = {k: n for n, k in BIG_KEY.items()}
    gathered = {}

    def milestone(tag, value):
        if tag == "z":
            return gathers[1].pass_on(start_group(3, value))
        return gathers[{"swa": 2, "kv": 3}[tag]].pass_on(value)

    def fetch(key, after):
        name = name_of[key]
        if name not in gathered:
            g = [i for i, group in enumerate(groups) if name in group][0]
            if g == 0:
                gathers[0].pass_on(after)
            gathered.update(zip(groups[g], gathers[g].finish(after)))
        return gathered[name]

    sm = {n: wts[n] for n in SMALL}
    started, held = {}, {}
    send_with = {k: group for group in (("wgT", "wuT"), ("wo", "wq", "wk", "wv")) for k in group}

    def emit(key, g):
        held[key] = g
        group = send_with.get(key, (key,))
        if key != group[-1]:
            return None
        flights = _exchange_start([held[k] for k in group], name="grad_send_" + name_of[group[0]])
        started.update({name_of[k]: f for k, f in zip(group, flights)})
        return flights[-1][2]

    grad_x, _, parts = _local_step(x[0], mem[0], loss_target[0], fetch, sm, emit, first_dep=first_dep, milestone=milestone)
    small_finish = _small_exchange(*_small_pack(parts))
    grads, deltas, new_m, new_v = {}, {}, {}, {}
    after = grad_x
    for group in (("w_down",), ("w_gate", "w_up"), ("wo_x", "wq_x", "wk_x", "wv_x", "w_out"), ("w_in",)):
        items = []
        for n in group:
            g_all, land = _exchange_wait(*started[n], after, name="grad_recv_" + n)
            items.append((g_all, land, mat(wts[n], n), mat(ms[n], n), mat(vs[n], n)))
            after = land
        for n, res in zip(group, _sum_adamw(items, name="adamw_" + group[0])):
            after = res[1]
            if n in TRANSPOSED:
                res = [a.T for a in res]
            grads[n], deltas[n], new_m[n], new_v[n] = [a[None] for a in res]
    loss, g_s, d_s, m_s, v_s = _small_update(small_finish(after), sm, {n: ms[n] for n in SMALL},
                                             {n: vs[n] for n in SMALL})
    grads.update(g_s), deltas.update(d_s), new_m.update(m_s), new_v.update(v_s)
    return (loss[0, 0], grad_x[None], *[grads[n] for n in WEIGHTS], *[deltas[n] for n in WEIGHTS],
            *[new_m[n] for n in WEIGHTS], *[new_v[n] for n in WEIGHTS])
```

```python
import functools

import jax
import jax.numpy as jnp
from jax import lax
from jax.experimental import pallas as pl
from jax.experimental.pallas import tpu as pltpu

F32 = jnp.float32
BF16 = jnp.bfloat16

D = 1024
D_IN = 2816
D_FF = 2816
CHUNK = 64
SWA_W = 512
KV_W = 128
HG_W = 512
HD = 128
ZQH, ZFH, ZIH, ZGH = 768, 1280, 1792, 2304
XH, XD = 4, 256
EPS = 1e-6
NEG = -1e30
N_DEV = 8
MESH = pl.DeviceIdType.MESH

LR, B1, B2, AEPS, WD, STEP = 0.001, 0.9, 0.999, 1e-08, 0.01, 10
C1 = 1.0 - B1 ** STEP
C2 = 1.0 - B2 ** STEP

VMEM_LIMIT = 56 * 1024 * 1024


def _params(**kw):
    return pltpu.CompilerParams(vmem_limit_bytes=VMEM_LIMIT, **kw)


def _sig(x):
    return 1.0 / (1.0 + jnp.exp(-x))


def _rowsum8(x):
    r, w = x.shape
    return jnp.sum(x.reshape(r // 8, 8, w), axis=0)


def _dot(a, b, ca, cb, precision=None):
    return lax.dot_general(a, b, (((ca,), (cb,)), ((), ())), preferred_element_type=F32,
                           precision=precision)


ANY_SPEC = pl.BlockSpec(memory_space=pl.ANY)


def _mm(a, b, *, ta=False, tb=False, out_dtype, tm, tn, tk=None, name, dep=None, n_outer=False):
    m = a.shape[1] if ta else a.shape[0]
    k = a.shape[0] if ta else a.shape[1]
    n = b.shape[0] if tb else b.shape[1]
    tm, tn = min(tm, m), min(tn, n)
    tk = k if tk is None else min(tk, k)
    nk = k // tk
    assert m % tm == 0 and n % tn == 0 and k % tk == 0, (name, m, n, k, tm, tn, tk)
    ij = (lambda g0, g1: (g1, g0)) if n_outer else (lambda g0, g1: (g0, g1))
    a_spec = (pl.BlockSpec((tk, tm), lambda g0, g1, kk: (kk, ij(g0, g1)[0])) if ta
              else pl.BlockSpec((tm, tk), lambda g0, g1, kk: (ij(g0, g1)[0], kk)))
    b_spec = (pl.BlockSpec((tn, tk), lambda g0, g1, kk: (ij(g0, g1)[1], kk)) if tb
              else pl.BlockSpec((tk, tn), lambda g0, g1, kk: (kk, ij(g0, g1)[1])))
    ca, cb = (0 if ta else 1), (1 if tb else 0)

    deps = [] if dep is None else [dep]

    def body(a_ref, b_ref, *rest):
        o_ref, acc = rest[len(deps)], rest[len(deps) + 1:]
        p = _dot(a_ref[...].astype(BF16), b_ref[...].astype(BF16), ca, cb)
        if nk == 1:
            o_ref[...] = p.astype(out_dtype)
        else:
            acc_ref, = acc
            kk = pl.program_id(2)

            @pl.when(kk == 0)
            def _():
                acc_ref[...] = p

            @pl.when(kk > 0)
            def _():
                acc_ref[...] += p

            @pl.when(kk == nk - 1)
            def _():
                o_ref[...] = acc_ref[...].astype(out_dtype)

    return pl.pallas_call(
        body, name=name, out_shape=jax.ShapeDtypeStruct((m, n), out_dtype),
        grid=(n // tn, m // tm, nk) if n_outer else (m // tm, n // tn, nk),
        in_specs=[a_spec, b_spec] + [ANY_SPEC] * len(deps),
        out_specs=pl.BlockSpec((tm, tn), lambda g0, g1, kk: ij(g0, g1)),
        scratch_shapes=[pltpu.VMEM((tm, tn), F32)] if nk > 1 else [],
        compiler_params=_params(dimension_semantics=("parallel", "parallel", "arbitrary")),
    )(a, b, *deps)


def _mm_rows(prods, rows_in, vecs_in, epilogue, outs, *, tm, name, dep=None):
    m = prods[0][0].shape[0]
    n = prods[0][1].shape[0] if prods[0][2] else prods[0][1].shape[1]
    tm = min(tm, m)
    assert m % tm == 0
    deps = [] if dep is None else [dep]
    n_p, n_r, n_v = len(prods), len(rows_in), len(vecs_in)

    def body(*refs):
        ab = refs[:2 * n_p]
        row_refs = refs[2 * n_p:2 * n_p + n_r]
        vec_refs = refs[2 * n_p + n_r:2 * n_p + n_r + n_v]
        out_refs = refs[2 * n_p + n_r + n_v + len(deps):]
        p = None
        for j, (_, _, tb) in enumerate(prods):
            t = _dot(ab[2 * j][...].astype(BF16), ab[2 * j + 1][...], 1, 1 if tb else 0)
            p = t if p is None else p + t
        vals = epilogue(p, *[r[...] for r in row_refs], *[v[...] for v in vec_refs])
        for (dtype, kind), o_ref, val in zip(outs, out_refs, vals):
            if kind == "row":
                o_ref[...] = val.astype(dtype)
            else:
                @pl.when(pl.program_id(0) == 0)
                def _(o_ref=o_ref):
                    o_ref[...] = jnp.zeros_like(o_ref)

                o_ref[...] += val

    row = lambda w: pl.BlockSpec((tm, w), lambda i: (i, 0))
    whole = lambda a: pl.BlockSpec(a.shape, lambda i: (0,) * a.ndim, pipeline_mode=pl.Buffered(1))
    in_specs, args = [], []
    for a, b, _ in prods:
        in_specs += [row(a.shape[1]), whole(b)]
        args += [a, b]
    in_specs += [row(r.shape[1]) for r in rows_in] + [whole(v) for v in vecs_in] + [ANY_SPEC] * len(deps)
    return pl.pallas_call(
        body, name=name,
        out_shape=tuple(jax.ShapeDtypeStruct((m, n) if kind == "row" else (8, n), dtype) for dtype, kind in outs),
        grid=(m // tm,), in_specs=in_specs,
        out_specs=tuple(row(n) if kind == "row" else pl.BlockSpec((8, n), lambda i: (0, 0)) for _, kind in outs),
        compiler_params=_params(dimension_semantics=("arbitrary",)),
    )(*args, *rows_in, *vecs_in, *deps)


def _rstd(x):
    return lax.rsqrt(jnp.mean(x * x, axis=-1, keepdims=True) + EPS)


def _norm_bwd(xh, r, t):
    return r * (t - xh * jnp.mean(xh * t, axis=-1, keepdims=True))


ROW_F32, ROW_BF16, SUM_F32 = (F32, "row"), (BF16, "row"), (F32, "sum")


def _then(epilogue, index, tb):
    def run(p, *args):
        vals = epilogue(p, *args[:-1])
        return (*vals, _dot(vals[index].astype(BF16), args[-1], 1, 1 if tb else 0))

    return run


def _ep_post_pre(p, h, g_post, g_pre):
    y = p.astype(BF16)
    yf = y.astype(F32)
    hn = h + yf * _rstd(yf) * g_post
    return y, hn, hn * _rstd(hn) * g_pre


_EP_POST_PRE_OUTS = [ROW_BF16, ROW_F32, ROW_BF16]


def _ep_final_loss(y, h, target, g_post):
    r = _rstd(y)
    yh = y * r
    err = h + yh * g_post - target
    dh = err * (1.0 / D)
    return _rowsum8(err * err), dh, _norm_bwd(yh, r, dh * g_post), _rowsum8(dh * yh)


def _ep_post_pre_bwd(du, dh_out, hn, y, g_post, g_pre):
    r2 = _rstd(hn)
    xh = hn * r2
    dh = dh_out + _norm_bwd(xh, r2, du * g_pre)
    yf = y.astype(F32)
    r1 = _rstd(yf)
    yh = yf * r1
    return dh, _norm_bwd(yh, r1, dh * g_post), _rowsum8(du * xh), _rowsum8(dh * yh)


_EP_POST_PRE_BWD_OUTS = [ROW_F32, ROW_BF16, SUM_F32, SUM_F32]


def _ep_pre_bwd(du, dh_out, x, g):
    r = _rstd(x)
    xh = x * r
    return dh_out + _norm_bwd(xh, r, du * g), _rowsum8(du * xh)


_EP_PRE_BWD_OUTS = [ROW_F32, SUM_F32]


def _prenorm(x, g, *, name, dep=None):
    t, d = x.shape
    tb = min(512, t)
    deps = [] if dep is None else [dep]

    def body(x_ref, g_ref, *rest):
        xf = x_ref[...]
        rest[-1][...] = (xf * _rstd(xf) * g_ref[...]).astype(BF16)

    return pl.pallas_call(
        body, name=name, out_shape=jax.ShapeDtypeStruct((t, d), BF16), grid=(t // tb,),
        in_specs=[pl.BlockSpec((tb, d), lambda i: (i, 0)), pl.BlockSpec((1, d), lambda i: (0, 0))]
        + [ANY_SPEC] * len(deps),
        out_specs=pl.BlockSpec((tb, d), lambda i: (i, 0)), compiler_params=_params(),
    )(x, g, *deps)


QB = 256


def _half_mask(shape, e):
    lane = lax.broadcasted_iota(jnp.int32, shape, len(shape) - 1)
    return (lane // 64) == e


def _place(kv):
    sw = pltpu.roll(kv, 64, 1)
    m0 = _half_mask(kv.shape, 0)
    return [[jnp.where(m0, kv, 0.0).astype(BF16), jnp.where(m0, 0.0, sw).astype(BF16)],
            [jnp.where(m0, sw, 0.0).astype(BF16), jnp.where(m0, 0.0, kv).astype(BF16)]]


SQ = 128
SK = 256


def _swa_valid(i, sb):
    qc = lax.broadcasted_iota(jnp.int32, (SQ, SK), 0) // CHUNK
    kc = lax.broadcasted_iota(jnp.int32, (SQ, SK), 1) // CHUNK - 2
    return (kc <= qc) & (qc <= kc + 2) & (4 * i + 2 * sb + kc >= 0)


def _swa_fwd(z, sinks, t, dep=None):
    nb = t // QB
    deps = [] if dep is None else [dep]

    def body(s_ref, q_ref, kp_ref, kc_ref, vp_ref, vc_ref, *rest):
        o_ref, lse_ref = rest[-2:]
        i = pl.program_id(0)
        kpl = _place(jnp.concatenate([kp_ref[...], kc_ref[...]], axis=0))
        vpl = _place(jnp.concatenate([vp_ref[...], vc_ref[...]], axis=0))
        lane = lax.broadcasted_iota(jnp.int32, (SQ, 128), 1)
        for sb in range(QB // SQ):
            rows, keys = slice(SQ * sb, SQ * (sb + 1)), slice(SQ * sb, SQ * sb + SK)
            valid = _swa_valid(i, sb)
            lse_out = jnp.zeros((SQ, 128), F32)
            for j in range(4):
                qp = q_ref[rows, 128 * j:128 * (j + 1)].astype(BF16)
                acc = jnp.zeros((SQ, 128), F32)
                for e in range(2):
                    h = 2 * j + e
                    kvh = h // 4
                    qm = jnp.where(_half_mask(qp.shape, e), qp, jnp.zeros_like(qp))
                    s = _dot(qm, kpl[kvh][e][keys], 1, 1) * 0.125
                    s = jnp.where(valid, s, NEG)
                    sink = s_ref[0, h]
                    m = jnp.maximum(jnp.max(s, axis=-1, keepdims=True), sink)
                    p = jnp.exp(s - m)
                    l = jnp.sum(p, axis=-1, keepdims=True) + jnp.exp(sink - m)
                    acc = acc + _dot(p.astype(BF16), vpl[kvh][e][keys], 1, 0) * (1.0 / l)
                    lse_out = jnp.where(lane == h, m + jnp.log(l), lse_out)
                o_ref[rows, 128 * j:128 * (j + 1)] = acc.astype(BF16)
            lse_ref[rows, :] = lse_out

    prev = lambda c: pl.BlockSpec((128, 128), lambda i: (jnp.maximum(2 * i - 1, 0), c))
    cur = lambda c: pl.BlockSpec((QB, 128), lambda i: (i, c))
    return pl.pallas_call(
        body, name="swa_fwd",
        out_shape=(jax.ShapeDtypeStruct((t, D), BF16), jax.ShapeDtypeStruct((t, 128), F32)),
        grid=(nb,),
        in_specs=[pl.BlockSpec(memory_space=pltpu.SMEM),
                  pl.BlockSpec((QB, SWA_W), lambda i: (i, 0)), prev(4), cur(4), prev(5), cur(5)]
        + [ANY_SPEC] * len(deps),
        out_specs=(pl.BlockSpec((QB, SWA_W), lambda i: (i, 0)), pl.BlockSpec((QB, 128), lambda i: (i, 0))),
        compiler_params=_params(),
    )(sinks, z, z, z, z, z, *deps)


def _swa_bwd(z, sinks, ymix, lse, dymix, t, dep=None):
    nb = t // QB
    deps = [] if dep is None else [dep]

    def body(s_ref, q_ref, kp_ref, kc_ref, vp_ref, vc_ref, o_ref, do_ref, l_ref, *rest):
        dq_ref, first_ref, second_ref, ds_ref, carry_ref = rest[len(deps):]
        i = pl.program_id(0)
        live = i < nb

        @pl.when(i == 0)
        def _():
            ds_ref[...] = jnp.zeros_like(ds_ref)
            carry_ref[...] = jnp.zeros_like(carry_ref)

        lane = lax.broadcasted_iota(jnp.int32, (8, 128), 1)
        kpl = _place(jnp.concatenate([kp_ref[...], kc_ref[...]], axis=0))
        vpl = _place(jnp.concatenate([vp_ref[...], vc_ref[...]], axis=0))
        nk = QB + 128
        qc = lax.broadcasted_iota(jnp.int32, (QB, nk), 0) // CHUNK
        kc = lax.broadcasted_iota(jnp.int32, (QB, nk), 1) // CHUNK - 2
        valid = (kc <= qc) & (qc <= kc + 2) & (4 * i + kc >= 0) & live
        lse_c = l_ref[...]
        dsink = jnp.zeros((8, 128), F32)
        dk_acc = [[jnp.zeros((128, nk), F32) for _ in range(2)] for _ in range(2)]
        dv_acc = [[jnp.zeros((128, nk), F32) for _ in range(2)] for _ in range(2)]
        dq = []
        for j in range(4):
            cols = slice(128 * j, 128 * (j + 1))
            qp = q_ref[:, cols].astype(BF16)
            dop = do_ref[:, cols]
            prod = dop.astype(F32) * o_ref[:, cols].astype(F32)
            acc = jnp.zeros((QB, 128), F32)
            for e in range(2):
                h = 2 * j + e
                kvh = h // 4
                hm = _half_mask(qp.shape, e)
                qm = jnp.where(hm, qp, jnp.zeros_like(qp))
                dom = jnp.where(hm, dop, jnp.zeros_like(dop))
                dd = jnp.sum(jnp.where(hm, prod, 0.0), axis=-1, keepdims=True)
                lse_h = lse_c[:, h:h + 1]
                s = _dot(qm, kpl[kvh][e], 1, 1) * 0.125
                p = jnp.where(valid, jnp.exp(s - lse_h), 0.0)
                dp = _dot(dom, vpl[kvh][e], 1, 1)
                ds = (p * (dp - dd) * 0.125).astype(BF16)
                acc = acc + _dot(ds, kpl[kvh][e], 1, 0)
                dk_acc[kvh][e] = dk_acc[kvh][e] + _dot(qm, ds, 0, 0)
                dv_acc[kvh][e] = dv_acc[kvh][e] + _dot(dom, p.astype(BF16), 0, 0)
                ps = jnp.where(live, jnp.exp(s_ref[0, h] - lse_h) * dd, 0.0)
                dsink = dsink - jnp.where(lane == h, _rowsum8(jnp.broadcast_to(ps, (QB, 128))), 0.0)
            dq.append(acc.astype(BF16))
        ds_ref[...] += dsink
        dk = (dk_acc[0][0] + dk_acc[1][1] + pltpu.roll(dk_acc[0][1] + dk_acc[1][0], 64, 0)).T
        dv = (dv_acc[0][0] + dv_acc[1][1] + pltpu.roll(dv_acc[0][1] + dv_acc[1][0], 64, 0)).T
        dkv = jnp.concatenate([dk, dv], axis=1)
        second_ref[...] = (carry_ref[...] + dkv[0:128]).astype(BF16)
        carry_ref[...] = dkv[256:384]

        @pl.when(live)
        def _():
            for j in range(4):
                dq_ref[:, 128 * j:128 * (j + 1)] = dq[j]
            first_ref[...] = dkv[128:256].astype(BF16)

    blk = lambda i: jnp.minimum(i, nb - 1)
    prev = lambda c: pl.BlockSpec((128, 128), lambda i: (jnp.maximum(2 * blk(i) - 1, 0), c))
    cur = lambda w, c: pl.BlockSpec((QB, w), lambda i: (blk(i), c))
    half = lambda index: pl.BlockSpec((128, 256), lambda i: (index(i), 0))
    return pl.pallas_call(
        body, name="swa_bwd",
        out_shape=(jax.ShapeDtypeStruct((t, SWA_W), BF16), jax.ShapeDtypeStruct((t // 2, 256), BF16),
                   jax.ShapeDtypeStruct((t // 2, 256), BF16), jax.ShapeDtypeStruct((8, 128), F32)),
        grid=(nb + 1,),
        in_specs=[pl.BlockSpec(memory_space=pltpu.SMEM),
                  cur(SWA_W, 0), prev(4), cur(128, 4), prev(5), cur(128, 5),
                  cur(SWA_W, 0), cur(SWA_W, 0), cur(128, 0)] + [ANY_SPEC] * len(deps),
        out_specs=(cur(SWA_W, 0), half(blk), half(lambda i: jnp.maximum(i - 1, 0)),
                   pl.BlockSpec((8, 128), lambda i: (0, 0))),
        scratch_shapes=[pltpu.VMEM((128, 256), F32)],
        compiler_params=_params(dimension_semantics=("arbitrary",)),
    )(sinks, z, z, z, z, z, ymix, dymix, lse, *deps)


HB = 256


def _lower_bound(lb_ref):
    a = lb_ref[...]
    a0, a1 = a[0:1], a[1:2]
    mx = jnp.maximum(a0, a1)
    e0, e1 = jnp.exp(a0 - mx), jnp.exp(a1 - mx)
    return e0 / (e0 + e1)


def _hgrn_cols(row_block):
    return [pl.BlockSpec((HB, 2 * HD), lambda j, c=base // (2 * HD) + p: (row_block(j), c))
            for base in (ZQH, ZFH, ZIH, ZGH) for p in range(2)]


NCH = HB // CHUNK


def _split3(x):
    hi = x.astype(BF16)
    r1 = x - hi.astype(F32)
    mid = r1.astype(BF16)
    return hi, mid, (r1 - mid.astype(F32)).astype(BF16)


def _blockdiag(lower):
    r = lax.broadcasted_iota(jnp.int32, (HB, HB), 0)
    c = lax.broadcasted_iota(jnp.int32, (HB, HB), 1)
    return (r // CHUNK == c // CHUNK) & ((c <= r) if lower else (c >= r))


def _chunk_sums(mask_bf16, x):
    return sum(_dot(mask_bf16, part, 1, 0) for part in _split3(x))


def _per_chunk_rows(x, row):
    w = x.shape[1]
    picked = x.reshape(NCH, CHUNK, w)[:, row:row + 1, :]
    return jnp.broadcast_to(picked, (NCH, CHUNK, w)).reshape(HB, w)


def _chunk_stack(x, chunk_of_row):
    return jnp.concatenate([jnp.where(chunk_of_row == c, x, jnp.zeros_like(x)) for c in range(NCH)], axis=1)


def _chunk_pick(x, chunk_of_row):
    w = x.shape[1] // NCH
    out = jnp.zeros((HB, w), x.dtype)
    for c in range(NCH):
        out = jnp.where(chunk_of_row == c, x[:, c * w:(c + 1) * w], out)
    return out


def _hgrn_local(q, f, kf, b):
    sq = _sig(q)
    qf = q * sq * (HD ** -0.5)
    b_mid = _per_chunk_rows(b, CHUNK // 2 - 1)
    b_last = _per_chunk_rows(b, CHUNK - 1)
    qm = qf * jnp.exp(b - b_mid)
    km = kf * jnp.exp(b_mid - b)
    kl = kf * jnp.exp(b_last - b)
    qb = qf * jnp.exp(b)
    return dict(sq=sq, b_mid=b_mid, b_last=b_last, qm=qm, km=km, kl=kl, qb=qb)


def _hgrn2_fwd(z, hgrn_lb, onorm, ymix, t, dep=None):
    nb = t // HB
    deps = [] if dep is None else [dep]

    def body(*refs):
        zq, zf, zi, zg = refs[0:2], refs[2:4], refs[4:6], refs[6:8]
        (lb_ref, on_ref), (y_ref, o_ref, sp_ref, st_ref) = refs[8:10], refs[-4:]

        @pl.when(pl.program_id(0) == 0)
        def _():
            st_ref[...] = jnp.zeros_like(st_ref)

        lb_all = _lower_bound(lb_ref)
        gn = on_ref[...]
        low = _blockdiag(True)
        low_b = low.astype(BF16)
        chunk_of_row = lax.broadcasted_iota(jnp.int32, (HB, HD), 0) // CHUNK
        for p in range(2):
            lbp = lb_all[:, 2 * HD * p:2 * HD * (p + 1)]
            fp = lbp + (1.0 - lbp) * _sig(zf[p][...])
            bp = _chunk_sums(low_b, jnp.log(fp))
            for e in range(2):
                h, ls = 2 * p + e, slice(e * HD, (e + 1) * HD)
                f = fp[:, ls]
                w = _hgrn_local(zq[p][:, ls], f, 1.0 - f, bp[:, ls])
                iv = zi[p][:, ls].astype(BF16)
                a = jnp.where(low, _dot(w["qm"].astype(BF16), w["km"].astype(BF16), 1, 1), 0.0)
                o = _dot(a.astype(BF16), iv, 1, 0)
                u = _dot(iv, _chunk_stack(w["kl"].astype(BF16), chunk_of_row), 0, 0)
                decay = jnp.exp(w["b_last"])
                st = st_ref[h]
                states = []
                for c in range(NCH):
                    sp_ref[h, c] = st
                    states.append(st.astype(BF16))
                    st = st * decay[c * CHUNK:c * CHUNK + 1] + u[:, c * HD:(c + 1) * HD]
                st_ref[h] = st
                inter = _dot(w["qb"].astype(BF16), jnp.concatenate(states, axis=0), 1, 1)
                o = o + _chunk_pick(inter, chunk_of_row)
                hs = slice(h * HD, (h + 1) * HD)
                o_ref[:, hs] = o
                gg = zg[p][:, ls]
                y_ref[:, hs] = (o * _rstd(o) * gn * (gg * _sig(gg))).astype(BF16)

    return pl.pallas_call(
        body, name="hgrn_fwd",
        out_shape=(jax.ShapeDtypeStruct((t, D), BF16), jax.ShapeDtypeStruct((t, HG_W), F32),
                   jax.ShapeDtypeStruct((4, t // CHUNK, HD, HD), F32)),
        grid=(nb,),
        in_specs=_hgrn_cols(lambda j: j) + [pl.BlockSpec((2, HG_W), lambda j: (0, 0)),
                                            pl.BlockSpec((1, HD), lambda j: (0, 0)), ANY_SPEC]
        + [ANY_SPEC] * len(deps),
        out_specs=(pl.BlockSpec((HB, HG_W), lambda j: (j, 1)),
                   pl.BlockSpec((HB, HG_W), lambda j: (j, 0)),
                   pl.BlockSpec((4, NCH, HD, HD), lambda j: (0, j, 0, 0))),
        scratch_shapes=[pltpu.VMEM((4, HD, HD), F32)],
        input_output_aliases={10: 0},
        compiler_params=_params(dimension_semantics=("arbitrary",)),
    )(*[z] * 8, hgrn_lb, onorm, ymix, *deps)


def _hgrn2_bwd(z, hgrn_lb, onorm, o_save, sprev, dymix, dza, t):
    nb = t // HB

    def body(*refs):
        zq, zf, zi, zg = refs[0:2], refs[2:4], refs[4:6], refs[6:8]
        (lb_ref, on_ref, o_ref, sp_ref, dy_ref, dqa_ref, first_ref, second_ref,
         dz_ref, dlb_ref, don_ref, dst_ref) = refs[8:]

        @pl.when(pl.program_id(0) == 0)
        def _():
            dst_ref[...] = jnp.zeros_like(dst_ref)
            dlb_ref[...] = jnp.zeros_like(dlb_ref)
            don_ref[...] = jnp.zeros_like(don_ref)

        dz_ref[:, 0:SWA_W] = dqa_ref[...]
        dz_ref[0:HB // 2, SWA_W:ZQH] = first_ref[...]
        dz_ref[HB // 2:HB, SWA_W:ZQH] = second_ref[...]
        lb_all = _lower_bound(lb_ref)
        gn = on_ref[...]
        low, upp = _blockdiag(True), _blockdiag(False)
        upp_b = upp.astype(BF16)
        low_b = low.astype(BF16)
        row = lax.broadcasted_iota(jnp.int32, (HB, HD), 0)
        chunk_of_row = row // CHUNK
        in_chunk = row % CHUNK
        for p in range(2):
            lbp = lb_all[:, 2 * HD * p:2 * HD * (p + 1)]
            sgp = _sig(zf[p][...])
            fp = lbp + (1.0 - lbp) * sgp
            bp = _chunk_sums(low_b, jnp.log(fp))
            db_pair, dkf_pair = [], []
            for e in range(2):
                h, ls, hs = 2 * p + e, slice(e * HD, (e + 1) * HD), slice((2 * p + e) * HD, (2 * p + e + 1) * HD)
                f = fp[:, ls]
                q = zq[p][:, ls]
                w = _hgrn_local(q, f, 1.0 - f, bp[:, ls])
                iv = zi[p][:, ls].astype(BF16)
                gg = zg[p][:, ls]
                o = o_ref[:, hs]
                dout = dy_ref[:, hs].astype(F32)
                sgg = _sig(gg)
                r = _rstd(o)
                oh = o * r
                dyn = dout * (gg * sgg)
                dz_ref[:, ZGH + h * HD:ZGH + (h + 1) * HD] = (
                    dout * oh * gn * (sgg * (1.0 + gg * (1.0 - sgg)))).astype(BF16)
                don_ref[...] += _rowsum8(dyn * oh)
                do = _norm_bwd(oh, r, dyn * gn).astype(BF16)
                qm, km, kl, qb = (w[n].astype(BF16) for n in ("qm", "km", "kl", "qb"))
                decay = jnp.exp(w["b_last"])
                grads_in = _dot(do, _chunk_stack(qb, chunk_of_row), 0, 0)
                dst = dst_ref[h]
                dstn, dd_rows = [None] * NCH, [None] * NCH
                for c in reversed(range(NCH)):
                    dstn[c] = dst.astype(BF16)
                    dd_rows[c] = jnp.sum(dst * sp_ref[h, c], axis=0, keepdims=True)
                    dst = dst * decay[c * CHUNK:c * CHUNK + 1] + grads_in[:, c * HD:(c + 1) * HD]
                dst_ref[h] = dst
                states = jnp.concatenate([sp_ref[h, c].astype(BF16) for c in range(NCH)], axis=0)
                dstn_all = jnp.concatenate(dstn, axis=0)
                dqb = _dot(_chunk_stack(do, chunk_of_row), states, 1, 0)
                at = jnp.where(upp, _dot(km, qm, 1, 1), 0.0)
                di = _dot(at.astype(BF16), do, 1, 0) + _chunk_pick(_dot(kl, dstn_all, 1, 1), chunk_of_row)
                dz_ref[:, ZIH + h * HD:ZIH + (h + 1) * HD] = di.astype(BF16)
                dkl = _dot(_chunk_stack(iv, chunk_of_row), dstn_all, 1, 0)
                da = jnp.where(low, _dot(do, iv, 1, 1), 0.0).astype(BF16)
                dat = jnp.where(upp, _dot(iv, do, 1, 1), 0.0).astype(BF16)
                dqm = _dot(da, km, 1, 0)
                dkm = _dot(dat, qm, 1, 0)
                b = bp[:, ls]
                e1, e2 = jnp.exp(b - w["b_mid"]), jnp.exp(w["b_mid"] - b)
                e3, e4 = jnp.exp(w["b_last"] - b), jnp.exp(b)
                dqf = dqm * e1 + dqb * e4
                dkf_pair.append(dkm * e2 + dkl * e3)
                t_qm, t_km, t_kl = dqm * w["qm"], dkm * w["km"], dkl * w["kl"]
                db = t_qm - t_km - t_kl + dqb * w["qb"]
                db_mid = jnp.sum((t_km - t_qm).reshape(NCH, CHUNK, HD), axis=1, keepdims=True)
                db_last = jnp.sum(t_kl.reshape(NCH, CHUNK, HD), axis=1, keepdims=True)
                db_last = db_last + jnp.stack(dd_rows, axis=0) * jnp.exp(
                    bp[:, ls].reshape(NCH, CHUNK, HD)[:, CHUNK - 1:CHUNK, :])
                spread = lambda v: jnp.broadcast_to(v, (NCH, CHUNK, HD)).reshape(HB, HD)
                db = (db + jnp.where(in_chunk == CHUNK // 2 - 1, spread(db_mid), 0.0)
                      + jnp.where(in_chunk == CHUNK - 1, spread(db_last), 0.0))
                db_pair.append(db)
                sq = w["sq"]
                dz_ref[:, ZQH + h * HD:ZQH + (h + 1) * HD] = (
                    dqf * (HD ** -0.5) * (sq * (1.0 + q * (1.0 - sq)))).astype(BF16)
            dlogf = _chunk_sums(upp_b, jnp.concatenate(db_pair, axis=1))
            dfv = dlogf / fp - jnp.concatenate(dkf_pair, axis=1)
            dz_ref[:, ZFH + 2 * HD * p:ZFH + 2 * HD * (p + 1)] = (dfv * (1.0 - lbp) * sgp * (1.0 - sgp)).astype(BF16)
            dlb_ref[:, 2 * HD * p:2 * HD * (p + 1)] += _rowsum8(dfv * (1.0 - sgp))

    rev = lambda j: nb - 1 - j
    return pl.pallas_call(
        body, name="hgrn_bwd",
        out_shape=(jax.ShapeDtypeStruct((t, D_IN), BF16), jax.ShapeDtypeStruct((8, HG_W), F32),
                   jax.ShapeDtypeStruct((8, HD), F32)),
        grid=(nb,),
        in_specs=_hgrn_cols(rev) + [pl.BlockSpec((2, HG_W), lambda j: (0, 0)), pl.BlockSpec((1, HD), lambda j: (0, 0)),
                                    pl.BlockSpec((HB, HG_W), lambda j: (rev(j), 0)),
                                    pl.BlockSpec((4, NCH, HD, HD), lambda j: (0, rev(j), 0, 0)),
                                    pl.BlockSpec((HB, HG_W), lambda j: (rev(j), 1)),
                                    pl.BlockSpec((HB, SWA_W), lambda j: (rev(j), 0)),
                                    pl.BlockSpec((HB // 2, 2 * KV_W), lambda j: (rev(j), 0)),
                                    pl.BlockSpec((HB // 2, 2 * KV_W), lambda j: (rev(j), 0))],
        out_specs=(pl.BlockSpec((HB, D_IN), lambda j: (rev(j), 0)), pl.BlockSpec((8, HG_W), lambda j: (0, 0)),
                   pl.BlockSpec((8, HD), lambda j: (0, 0))),
        scratch_shapes=[pltpu.VMEM((4, HD, HD), F32)],
        compiler_params=_params(dimension_semantics=("arbitrary",)),
    )(*[z] * 8, hgrn_lb, onorm, o_save, sprev, dymix, *dza)


XB = 512


def _xattn_fwd(q, k, v, wo, h, g_post, g_pre, t, dep=None):
    tb = min(XB, t)
    deps = [] if dep is None else [dep]

    def body(q_ref, k_ref, v_ref, wo_ref, h_ref, gp_ref, gn_ref, *rest):
        o_ref, y_ref, hn_ref, u_ref = rest[len(deps):]
        for hd in range(XH):
            cols = slice(XD * hd, XD * (hd + 1))
            s = _dot(q_ref[:, cols], k_ref[:, cols], 1, 1) * (XD ** -0.5)
            p = jnp.exp(s - jnp.max(s, axis=-1, keepdims=True))
            l = jnp.sum(p, axis=-1, keepdims=True)
            o_ref[:, cols] = (_dot(p.astype(BF16), v_ref[:, cols], 1, 0) * (1.0 / l)).astype(BF16)
        y, hn, u = _ep_post_pre(_dot(o_ref[...], wo_ref[...], 1, 0), h_ref[...], gp_ref[...], gn_ref[...])
        y_ref[...] = y
        hn_ref[...] = hn
        u_ref[...] = u.astype(BF16)

    row = pl.BlockSpec((tb, D), lambda i: (i, 0))
    whole = lambda a: pl.BlockSpec(a.shape, lambda i: (0,) * a.ndim, pipeline_mode=pl.Buffered(1))
    half = jax.ShapeDtypeStruct((t, D), BF16)
    return pl.pallas_call(
        body, name="xattn_fwd", out_shape=(half, half, jax.ShapeDtypeStruct((t, D), F32), half), grid=(t // tb,),
        in_specs=[row, whole(k), whole(v), whole(wo), row, whole(g_post), whole(g_pre)] + [ANY_SPEC] * len(deps),
        out_specs=(row, row, row, row), compiler_params=_params(),
    )(q, k, v, wo, h, g_post, g_pre, *deps)


def _xattn_bwd(q, k, v, do, wq, wout, dh_out, hn, y, g_post, g_pre, t):
    tb = min(XB, t)

    def body(q_ref, k_ref, v_ref, do_ref, wq_ref, wout_ref, dho_ref, hn_ref, y_ref, gp_ref, gn_ref,
             dq_ref, dk_ref, dv_ref, dh_ref, dyp_ref, dym_ref, dgn_ref, dgp_ref):
        @pl.when(pl.program_id(0) == 0)
        def _():
            dk_ref[...] = jnp.zeros_like(dk_ref)
            dv_ref[...] = jnp.zeros_like(dv_ref)
            dgn_ref[...] = jnp.zeros_like(dgn_ref)
            dgp_ref[...] = jnp.zeros_like(dgp_ref)

        for h in range(XH):
            cols = slice(XD * h, XD * (h + 1))
            qh, kh, vh, doh = q_ref[:, cols], k_ref[:, cols], v_ref[:, cols], do_ref[:, cols]
            s = _dot(qh, kh, 1, 1) * (XD ** -0.5)
            p = jnp.exp(s - jnp.max(s, axis=-1, keepdims=True))
            p = p * (1.0 / jnp.sum(p, axis=-1, keepdims=True))
            dp = _dot(doh, vh, 1, 1)
            ds = (p * (dp - jnp.sum(p * dp, axis=-1, keepdims=True)) * (XD ** -0.5)).astype(BF16)
            dq_ref[:, cols] = _dot(ds, kh, 1, 0).astype(BF16)
            dk_ref[:, cols] += _dot(ds, qh, 0, 0)
            dv_ref[:, cols] += _dot(p.astype(BF16), doh, 0, 0)
        du = _dot(dq_ref[...], wq_ref[...], 1, 1)
        dh, dyp, dgn, dgp = _ep_post_pre_bwd(du, dho_ref[...], hn_ref[...], y_ref[...], gp_ref[...], gn_ref[...])
        dh_ref[...] = dh
        dyp = dyp.astype(BF16)
        dyp_ref[...] = dyp
        dym_ref[...] = _dot(dyp, wout_ref[...], 1, 1).astype(BF16)
        dgn_ref[...] += dgn
        dgp_ref[...] += dgp

    row = pl.BlockSpec((tb, D), lambda i: (i, 0))
    mem = pl.BlockSpec(k.shape, lambda i: (0, 0))
    whole = lambda a: pl.BlockSpec(a.shape, lambda i: (0,) * a.ndim, pipeline_mode=pl.Buffered(1))
    acc = pl.BlockSpec((8, D), lambda i: (0, 0))
    half = jax.ShapeDtypeStruct((t, D), BF16)
    return pl.pallas_call(
        body, name="xattn_bwd",
        out_shape=(half, jax.ShapeDtypeStruct(k.shape, F32), jax.ShapeDtypeStruct(k.shape, F32),
                   jax.ShapeDtypeStruct((t, D), F32), half, half,
                   jax.ShapeDtypeStruct((8, D), F32), jax.ShapeDtypeStruct((8, D), F32)),
        grid=(t // tb,),
        in_specs=[row, whole(k), whole(v), row, whole(wq), whole(wout), row, row, row, whole(g_post), whole(g_pre)],
        out_specs=(row, mem, mem, row, row, row, acc, acc),
        compiler_params=_params(dimension_semantics=("arbitrary",)),
    )(q, k, v, do, wq, wout, dh_out, hn, y, g_post, g_pre)


def _mem_kv(mem, g_mem, wk, wv):
    def body(m_ref, g_ref, wk_ref, wv_ref, mn_ref, k_ref, v_ref):
        m_ = m_ref[...]
        mn = (m_ * _rstd(m_) * g_ref[...]).astype(BF16)
        mn_ref[...] = mn
        k_ref[...] = _dot(mn, wk_ref[...], 1, 0).astype(BF16)
        v_ref[...] = _dot(mn, wv_ref[...], 1, 0).astype(BF16)

    return pl.pallas_call(body, name="mem_kv", out_shape=(jax.ShapeDtypeStruct(mem.shape, BF16),) * 3,
                          compiler_params=_params())(mem, g_mem, wk, wv)


def _mem_kv_bwd(mn, mem, dk, dv, wk, wv, dep=None):
    deps = [] if dep is None else [dep]

    def body(mn_ref, m_ref, dk_ref, dv_ref, wk_ref, wv_ref, *rest):
        gk_ref, gv_ref, dg_ref = rest[len(deps):]
        mn = mn_ref[...]
        dkb, dvb = dk_ref[...].astype(BF16), dv_ref[...].astype(BF16)
        gk_ref[...] = _dot(mn, dkb, 0, 0).astype(BF16)
        gv_ref[...] = _dot(mn, dvb, 0, 0).astype(BF16)
        dmn = _dot(dkb, wk_ref[...], 1, 1) + _dot(dvb, wv_ref[...], 1, 1)
        m_ = m_ref[...]
        dg_ref[...] = _rowsum8(dmn * (m_ * _rstd(m_)))

    vmem = pl.BlockSpec(memory_space=pltpu.VMEM)
    return pl.pallas_call(
        body, name="mem_kv_bwd",
        out_shape=(jax.ShapeDtypeStruct(wk.shape, BF16), jax.ShapeDtypeStruct(wv.shape, BF16),
                   jax.ShapeDtypeStruct((8, D), F32)),
        in_specs=[vmem] * 6 + [ANY_SPEC] * len(deps), out_specs=(vmem,) * 3, compiler_params=_params(),
    )(mn, mem, dk, dv, wk, wv, *deps)


FB = 256


def _ffn_fwd_bwd(u, wgt, wut, wd, h, target, g_last, y_prev, g_post, g_pre, wo, t):
    tb = min(FB, t)

    def body(u_ref, wg_ref, wu_ref, wd_ref, h_ref, t_ref, gl_ref, yp_ref, gp_ref, gn_ref, wo_ref,
             a_ref, dy_ref, dg_ref, dup_ref, dh_ref, dyp_ref, do_ref, sq_ref, dgl_ref, dgn_ref, dgp_ref):
        @pl.when(pl.program_id(0) == 0)
        def _():
            for ref in (sq_ref, dgl_ref, dgn_ref, dgp_ref):
                ref[...] = jnp.zeros_like(ref)

        u_ = u_ref[...]
        g = _dot(u_, wg_ref[...], 1, 1)
        up = _dot(u_, wu_ref[...], 1, 1)
        sg = _sig(g)
        a = (g * sg * up).astype(BF16)
        a_ref[...] = a
        h_ = h_ref[...]
        sq, dh3, dy, dgl = _ep_final_loss(_dot(a, wd_ref[...], 1, 0), h_, t_ref[...], gl_ref[...])
        sq_ref[...] += sq
        dgl_ref[...] += dgl
        dy = dy.astype(BF16)
        dy_ref[...] = dy
        da = _dot(dy, wd_ref[...], 1, 1)
        dup = (da * g * sg).astype(BF16)
        dgate = (da * up * (sg * (1.0 + g * (1.0 - sg)))).astype(BF16)
        dup_ref[...] = dup
        dg_ref[...] = dgate
        du = _dot(dgate, wg_ref[...], 1, 0) + _dot(dup, wu_ref[...], 1, 0)
        dh, dyp, dgn, dgp = _ep_post_pre_bwd(du, dh3, h_, yp_ref[...], gp_ref[...], gn_ref[...])
        dh_ref[...] = dh
        dyp = dyp.astype(BF16)
        dyp_ref[...] = dyp
        do_ref[...] = _dot(dyp, wo_ref[...], 1, 1).astype(BF16)
        dgn_ref[...] += dgn
        dgp_ref[...] += dgp

    row = lambda w: pl.BlockSpec((tb, w), lambda i: (i, 0))
    whole = lambda a: pl.BlockSpec(a.shape, lambda i: (0,) * a.ndim, pipeline_mode=pl.Buffered(1))
    acc = pl.BlockSpec((8, D), lambda i: (0, 0))
    wide, half, sums = (jax.ShapeDtypeStruct((t, D_FF), BF16), jax.ShapeDtypeStruct((t, D), BF16),
                        jax.ShapeDtypeStruct((8, D), F32))
    return pl.pallas_call(
        body, name="ffn_fwd_bwd",
        out_shape=(wide, half, wide, wide, jax.ShapeDtypeStruct((t, D), F32), half, half, sums, sums, sums, sums),
        grid=(t // tb,),
        in_specs=[row(D), whole(wgt), whole(wut), whole(wd), row(D), row(D), whole(g_last), row(D), whole(g_post),
                  whole(g_pre), whole(wo)],
        out_specs=(row(D_FF), row(D), row(D_FF), row(D_FF), row(D), row(D), row(D), acc, acc, acc, acc),
        compiler_params=_params(dimension_semantics=("arbitrary",)),
    )(u, wgt, wut, wd, h, target, g_last, y_prev, g_post, g_pre, wo)


def _local_step(x, mem, target, fetch, sm, emit=None, first_dep=None, milestone=None):
    t = x.shape[0]
    w, gw = {}, {}

    def out(key, g):
        gw[key] = g
        return None if emit is None else emit(key, g)

    def tell(tag, value):
        return None if milestone is None else milestone(tag, value)
    u1 = _prenorm(x, sm["g_mix_pre"], name="prenorm_mix", dep=first_dep)
    w["winT"] = fetch("winT", u1)
    z = _mm(u1, w["winT"], tb=True, out_dtype=F32, tm=1024, tn=1408, name="mm_z", n_outer=True)
    ymix, lse = _swa_fwd(z, sm["sinks"], t, dep=tell("z", z))
    ymix, o_h, sprev = _hgrn2_fwd(z, sm["hgrn_lb"], sm["hgrn_onorm"], ymix, t, dep=tell("swa", lse))
    for key in ("wout", "wq", "wk", "wv", "wo"):
        w[key] = fetch(key, ymix)
    y1, h1, u2, qx = _mm_rows([(ymix, w["wout"], False)], [x], [sm["g_mix_post"], sm["g_x_pre"], w["wq"]],
                              _then(_ep_post_pre, 2, False), _EP_POST_PRE_OUTS + [ROW_BF16], tm=1024,
                              name="mm_y1_post_qx")
    mn, kx, vx = _mem_kv(mem, sm["g_mem"], w["wk"], w["wv"])
    ox, y2, h2, u3 = _xattn_fwd(qx, kx, vx, w["wo"], h1, sm["g_x_post"], sm["g_ffn_pre"], t, dep=tell("kv", kx))
    for key in ("wgT", "wuT", "wd"):
        w[key] = fetch(key, u3)
    act, dy3, dgate, dup, dh2, dy2, dox, sq, dg_ffn_post, dg_ffn_pre, dg_x_post = _ffn_fwd_bwd(
        u3, w["wgT"], w["wuT"], w["wd"], h2, target, sm["g_ffn_post"], y2, sm["g_x_post"], sm["g_ffn_pre"], w["wo"], t)
    dep = out("wd", _mm(act, dy3, ta=True, out_dtype=BF16, tm=1408, tn=1024, name="mm_gwd"))
    dep = out("wgT", _mm(dgate, u3, ta=True, out_dtype=BF16, tm=1408, tn=1024, name="mm_gwg", dep=dep))
    dep = out("wuT", _mm(dup, u3, ta=True, out_dtype=BF16, tm=1408, tn=1024, name="mm_gwu", dep=dep))
    out("wo", _mm(ox, dy2, ta=True, out_dtype=BF16, tm=512, tn=1024, name="mm_gwo", dep=dep))
    dqx, dkx, dvx, dh1, dy1, dymix, dg_x_pre, dg_mix_post = _xattn_bwd(
        qx, kx, vx, dox, w["wq"], w["wout"], dh2, h1, y1, sm["g_mix_post"], sm["g_x_pre"], t)
    out("wq", _mm(u2, dqx, ta=True, out_dtype=BF16, tm=512, tn=1024, name="mm_gwq"))
    gwk, gwv, dg_mem = _mem_kv_bwd(mn, mem, dkx, dvx, w["wk"], w["wv"])
    out("wk", gwk)
    dep = out("wv", gwv)
    dep = out("wout", _mm(ymix, dy1, ta=True, out_dtype=BF16, tm=512, tn=1024, name="mm_gwout", dep=dep))
    *dza, dsinks = _swa_bwd(z, sm["sinks"], ymix, lse, dymix, t, dep=dep)
    dz, dlb, donorm = _hgrn2_bwd(z, sm["hgrn_lb"], sm["hgrn_onorm"], o_h, sprev, dymix, dza, t)
    dep = out("winT", _mm(dz, u1, ta=True, out_dtype=BF16, tm=1408, tn=1024, name="mm_gwin"))
    grad_x, dg_mix_pre = _mm_rows([(dz, w["winT"], False)], [dh1, x], [sm["g_mix_pre"]], _ep_pre_bwd,
                                  _EP_PRE_BWD_OUTS, tm=512, name="mm_du1_pre_bwd", dep=dep)
    parts = dict(g_mix_pre=dg_mix_pre, g_mix_post=dg_mix_post, g_mem=dg_mem, g_x_pre=dg_x_pre,
                 g_x_post=dg_x_post, g_ffn_pre=dg_ffn_pre, g_ffn_post=dg_ffn_post,
                 hgrn_onorm=donorm, hgrn_lb=dlb, sinks=dsinks, sq=sq)
    return grad_x, gw, parts


def _position():
    return lax.axis_index("x"), lax.axis_index("y"), lax.axis_index("c")


def _peer(pos, k):
    x, y, c = pos
    return (1 - x if k & 4 else x, 1 - y if k & 2 else y, 1 - c if k & 1 else c)


def _linear(pos):
    x, y, c = pos
    return 4 * x + 2 * y + c


HBM_SPEC = pl.BlockSpec(memory_space=pltpu.HBM)
SEM_SPEC = pl.BlockSpec(memory_space=pltpu.SEMAPHORE)
DATAFLOW = pltpu.SideEffectType.DATAFLOW_SIDE_EFFECTING
SEND_ORDER = (1, 2, 4, 3, 5, 6, 7)


def _in_hbm(a):
    return pltpu.with_memory_space_constraint(a, pltpu.HBM)


def _prepare_weights(shards, *, name, dep=None):
    n = len(shards)
    deps = [] if dep is None else [dep]

    def body(*refs):
        ins, (outs, lands, sem) = refs[:n], (refs[-2 * n - 1:-n - 1], refs[-n - 1:-1], refs[-1])
        me_lin = _linear(_position())
        copies = []
        for a in range(n):
            r = ins[a].shape[0]
            outs[a][...] = ins[a][...].astype(BF16)
            copies.append(pltpu.make_async_copy(outs[a], lands[a].at[pl.ds(me_lin * r, r), :], sem.at[a]))
            copies[-1].start()
        for cp in copies:
            cp.wait()

    vmem = pl.BlockSpec(memory_space=pltpu.VMEM)
    res = pl.pallas_call(
        body, name=name,
        out_shape=tuple(jax.ShapeDtypeStruct(s.shape, BF16) for s in shards)
        + tuple(jax.ShapeDtypeStruct((N_DEV * s.shape[0], s.shape[1]), BF16) for s in shards),
        in_specs=[vmem] * n + [ANY_SPEC] * len(deps), out_specs=tuple([vmem] * n + [ANY_SPEC] * n),
        scratch_shapes=[pltpu.SemaphoreType.DMA((n,))], compiler_params=_params(),
    )(*shards, *deps)
    return res[:n], res[n:]


def _copies_start(arrays, plan, n, *, name):
    na = len(arrays)

    def body(*refs):
        ins, send_sems, recv_sems = refs[:na], refs[na], refs[na + 1]
        me = _position()
        for j in range(n):
            src, dst, peer, _ = plan(ins, me, j)
            pltpu.make_async_remote_copy(src_ref=src, dst_ref=dst, send_sem=send_sems.at[j], recv_sem=recv_sems.at[j],
                                         device_id=peer, device_id_type=MESH).start()

    return pl.pallas_call(
        body, name=name,
        out_shape=(pltpu.SemaphoreType.DMA((n,)), pltpu.SemaphoreType.DMA((n,)))
        + tuple(pltpu.HBM(a.shape, a.dtype) for a in arrays),
        in_specs=(HBM_SPEC,) * na, out_specs=(SEM_SPEC, SEM_SPEC) + (HBM_SPEC,) * na,
        input_output_aliases={i: 2 + i for i in range(na)},
        compiler_params=pltpu.CompilerParams(has_side_effects=DATAFLOW),
    )(*[_in_hbm(a) for a in arrays])


def _copies_wait(send_sems, recv_sems, arrays, plan, n, after, *, name):
    na = len(arrays)

    def body(*refs):
        ins, send_sems, recv_sems = refs[:na], refs[na], refs[na + 1]
        me = _position()
        for j in range(n):
            src, _, peer, landed = plan(ins, me, j)
            copy = pltpu.make_async_remote_copy(src_ref=src, dst_ref=landed, send_sem=send_sems.at[j],
                                                recv_sem=recv_sems.at[j], device_id=peer, device_id_type=MESH)
            copy.wait_send()
            copy.wait_recv()

    return pl.pallas_call(
        body, name=name, out_shape=tuple(pltpu.HBM(a.shape, a.dtype) for a in arrays),
        in_specs=(HBM_SPEC,) * na + (SEM_SPEC, SEM_SPEC, ANY_SPEC), out_specs=(HBM_SPEC,) * na,
        input_output_aliases={i: i for i in range(na)},
        compiler_params=pltpu.CompilerParams(has_side_effects=DATAFLOW),
    )(*arrays, send_sems, recv_sems, after)


SAME_CORE = (2, 4, 6)


class _TwoLevelGather:
    def __init__(self, shards, lands, *, name):
        n = self.n = len(shards)
        self.name = name
        first_peers = (1,) + SAME_CORE

        def rows(ref, pos):
            r = ref.shape[0] // N_DEV
            return ref.at[pl.ds(_linear(pos) * r, r), :]

        def first(refs, me, j):
            a, peer = j // 4, _peer(me, first_peers[j % 4])
            return refs[a], rows(refs[n + a], me), peer, rows(refs[n + a], peer)

        def second(refs, me, j):
            a, sibling = j // 3, _peer(me, 1)
            mine = rows(refs[a], _peer(me, SAME_CORE[j % 3]))
            return mine, mine, sibling, rows(refs[a], _peer(sibling, SAME_CORE[j % 3]))

        self._first, self._second = first, second
        self._flight = _copies_start(list(shards) + list(lands), first, 4 * n, name=name + "_send")
        self.dep = self._flight[2]

    def pass_on(self, after):
        send1, recv1, *arrays = self._flight
        arrays = _copies_wait(send1, recv1, arrays, self._first, 4 * self.n, after, name=self.name + "_recv")
        self._flight = _copies_start(list(arrays[self.n:]), self._second, 3 * self.n, name=self.name + "_pass")
        return self._flight[2]

    def finish(self, after):
        send2, recv2, *lands = self._flight
        return _copies_wait(send2, recv2, lands, self._second, 3 * self.n, after, name=self.name + "_pass_recv")


def _exchange_start(gs, *, name):
    n = len(gs)
    rows = [g.shape[0] // N_DEV for g in gs]
    lands = [lax.empty((N_DEV - 1, r, g.shape[1]), g.dtype) for g, r in zip(gs, rows)]

    def body(*refs):
        g_refs, land_refs = refs[:n], refs[n:2 * n]
        send_sems, recv_sems = refs[2 * n:3 * n], refs[3 * n:4 * n]
        me = _position()
        for a in range(n):
            for k in SEND_ORDER:
                peer = _peer(me, k)
                pltpu.make_async_remote_copy(
                    src_ref=g_refs[a].at[pl.ds(_linear(peer) * rows[a], rows[a]), :],
                    dst_ref=land_refs[a].at[k - 1],
                    send_sem=send_sems[a].at[k - 1], recv_sem=recv_sems[a].at[k - 1],
                    device_id=peer, device_id_type=MESH).start()

    res = pl.pallas_call(
        body, name=name,
        out_shape=tuple(pltpu.SemaphoreType.DMA((N_DEV - 1,)) for _ in range(2 * n))
        + tuple(pltpu.HBM(a.shape, a.dtype) for a in gs + lands),
        in_specs=(HBM_SPEC,) * (2 * n), out_specs=(SEM_SPEC,) * (2 * n) + (HBM_SPEC,) * (2 * n),
        input_output_aliases={i: 2 * n + i for i in range(2 * n)},
        compiler_params=pltpu.CompilerParams(has_side_effects=DATAFLOW),
    )(*[_in_hbm(a) for a in gs + lands])
    return [(res[a], res[n + a], res[2 * n + a], res[3 * n + a]) for a in range(n)]


def _exchange_wait(send_sems, recv_sems, g_thru, land_thru, after, *, name):
    r = land_thru.shape[1]

    def body(g_ref, land_ref, send_sems, recv_sems, after_ref, g_dead, got_ref):
        del after_ref, g_dead, got_ref
        me = _position()
        for k in SEND_ORDER:
            peer = _peer(me, k)
            copy = pltpu.make_async_remote_copy(
                src_ref=g_ref.at[pl.ds(_linear(peer) * r, r), :], dst_ref=land_ref.at[k - 1],
                send_sem=send_sems.at[k - 1], recv_sem=recv_sems.at[k - 1],
                device_id=peer, device_id_type=MESH)
            copy.wait_send()
            copy.wait_recv()

    return pl.pallas_call(
        body, name=name,
        out_shape=(pltpu.HBM(g_thru.shape, g_thru.dtype), pltpu.HBM(land_thru.shape, land_thru.dtype)),
        in_specs=(HBM_SPEC, HBM_SPEC, SEM_SPEC, SEM_SPEC, pl.BlockSpec(memory_space=pl.ANY)),
        out_specs=(HBM_SPEC, HBM_SPEC), input_output_aliases={0: 0, 1: 1},
        compiler_params=pltpu.CompilerParams(has_side_effects=DATAFLOW),
    )(g_thru, land_thru, send_sems, recv_sems, after)


def _adamw_math(w, g, m, v):
    m = B1 * m + (1.0 - B1) * g
    v = B2 * v + (1.0 - B2) * (g * g)
    delta = -LR * ((m / C1) / (jnp.sqrt(v / C2) + AEPS) + WD * w)
    return delta, m, v


def _sum_adamw(items, *, name):
    n = len(items)

    def body(*refs):
        ins, outs, scratch = refs[:5 * n], refs[5 * n:9 * n], refs[9 * n:]
        me_lin = _linear(_position())
        mine = []
        for a in range(n):
            r = items[a][2].shape[0]
            mine.append(pltpu.make_async_copy(ins[5 * a].at[pl.ds(me_lin * r, r), :], scratch[a], scratch[n].at[a]))
            mine[-1].start()
        for a in range(n):
            _, land_ref, w_ref, m_ref, v_ref = ins[5 * a:5 * a + 5]
            g_ref, d_ref, nm_ref, nv_ref = outs[4 * a:4 * a + 4]
            g = land_ref[0].astype(F32)
            for s in range(1, N_DEV - 1):
                g = g + land_ref[s].astype(F32)
            mine[a].wait()
            g = scratch[a][...].astype(F32) + g
            g_ref[...] = g
            d_ref[...], nm_ref[...], nv_ref[...] = _adamw_math(w_ref[...], g, m_ref[...], v_ref[...])

    vmem = pl.BlockSpec(memory_space=pltpu.VMEM)
    res = pl.pallas_call(
        body, name=name,
        out_shape=tuple(jax.ShapeDtypeStruct(it[2].shape, F32) for it in items for _ in range(4)),
        in_specs=[ANY_SPEC, vmem, vmem, vmem, vmem] * n, out_specs=(vmem,) * (4 * n),
        scratch_shapes=[pltpu.VMEM(it[2].shape, BF16) for it in items] + [pltpu.SemaphoreType.DMA((n,))],
        compiler_params=_params(),
    )(*[a for it in items for a in it])
    return [res[4 * a:4 * a + 4] for a in range(n)]


SMALL = ("g_mix_pre", "g_mix_post", "g_mem", "g_x_pre", "g_x_post", "g_ffn_pre", "g_ffn_post",
         "hgrn_onorm", "hgrn_lb", "sinks")
SMALL_W = dict(hgrn_onorm=HD, hgrn_lb=HG_W, sinks=8)
SQ_ROW = len(SMALL)
PACK_ROWS = 16


def _small_pack(parts):
    ns = len(SMALL)

    def body(*refs):
        part, mine, slots, sem = refs[:ns + 1], refs[ns + 1], refs[ns + 2], refs[ns + 3]
        mine[...] = jnp.zeros((PACK_ROWS, D), F32)
        for r, name in enumerate(SMALL):
            wd = SMALL_W.get(name, D)
            mine[r:r + 1, 0:wd] = jnp.sum(part[r][...], axis=0, keepdims=True)[:, 0:wd]
        sq = jnp.sum(part[ns][...]) * (0.5 / D)
        mine[SQ_ROW:SQ_ROW + 1, :] = jnp.full((1, D), sq, F32)
        own = pltpu.make_async_copy(mine, slots.at[_linear(_position())], sem)
        own.start()
        own.wait()

    vmem = pl.BlockSpec(memory_space=pltpu.VMEM)
    return pl.pallas_call(
        body, name="small_pack",
        out_shape=(jax.ShapeDtypeStruct((PACK_ROWS, D), F32), jax.ShapeDtypeStruct((N_DEV, PACK_ROWS, D), F32)),
        in_specs=[vmem] * (ns + 1), out_specs=(vmem, ANY_SPEC),
        scratch_shapes=[pltpu.SemaphoreType.DMA(())], compiler_params=_params(),
    )(*[parts[n] for n in SMALL], parts["sq"])


def _small_exchange(mine, slots):
    def plan(refs, me, j):
        peer = _peer(me, j + 1)
        return refs[0], refs[1].at[_linear(me)], peer, refs[1].at[_linear(peer)]

    send, recv, mine1, slots1 = _copies_start([mine, slots], plan, N_DEV - 1, name="small_send")
    return lambda after: _copies_wait(send, recv, [mine1, slots1], plan, N_DEV - 1, after, name="small_recv")[1]


def _small_update(slots, sm, m_sm, v_sm):
    ns = len(SMALL)

    def body(*refs):
        tot = refs[0][0]
        for s in range(1, N_DEV):
            tot = tot + refs[0][s]
        w_refs, m_refs, v_refs = refs[1:ns + 1], refs[ns + 1:2 * ns + 1], refs[2 * ns + 1:3 * ns + 1]
        outs = refs[3 * ns + 1:]
        loss_ref = outs[0]
        g_out, d_out = outs[1:ns + 1], outs[ns + 1:2 * ns + 1]
        nm_out, nv_out = outs[2 * ns + 1:3 * ns + 1], outs[3 * ns + 1:4 * ns + 1]
        loss_ref[...] = tot[SQ_ROW:SQ_ROW + 1, 0:1]
        for r, name in enumerate(SMALL):
            wd = SMALL_W.get(name, D)
            g = tot[r:r + 1, 0:wd]
            w = w_refs[r][...]
            if name == "hgrn_lb":
                mx = jnp.maximum(w[0:1], w[1:2])
                e0, e1 = jnp.exp(w[0:1] - mx), jnp.exp(w[1:2] - mx)
                lb0 = e0 / (e0 + e1)
                g0 = g * lb0 * (1.0 - lb0)
                for i, gi in enumerate((g0, -g0)):
                    d, nm, nv = _adamw_math(w[i:i + 1], gi, m_refs[r][i:i + 1, :], v_refs[r][i:i + 1, :])
                    g_out[r][i:i + 1, :] = gi
                    d_out[r][i:i + 1, :], nm_out[r][i:i + 1, :], nv_out[r][i:i + 1, :] = d, nm, nv
            else:
                d, nm, nv = _adamw_math(w, g, m_refs[r][...], v_refs[r][...])
                g_out[r][...] = g
                d_out[r][...], nm_out[r][...], nv_out[r][...] = d, nm, nv

    shapes = [jax.ShapeDtypeStruct(sm[n].shape, F32) for n in SMALL]
    res = pl.pallas_call(
        body, name="small_update", out_shape=tuple([jax.ShapeDtypeStruct((1, 1), F32)] + shapes * 4),
        compiler_params=_params(),
    )(slots, *[sm[n] for n in SMALL], *[m_sm[n] for n in SMALL], *[v_sm[n] for n in SMALL])
    groups = [dict(zip(SMALL, res[1 + i * ns:1 + (i + 1) * ns])) for i in range(4)]
    return res[0], groups[0], groups[1], groups[2], groups[3]


BIG = ("w_in", "w_gate", "w_up", "w_down", "w_out", "wq_x", "wk_x", "wv_x", "wo_x")
BIG_KEY = dict(w_in="winT", w_gate="wgT", w_up="wuT", w_down="wd", w_out="wout", wq_x="wq", wk_x="wk",
               wv_x="wv", wo_x="wo")
TRANSPOSED = ("w_in", "w_gate", "w_up")
WEIGHTS = ("w_in", "sinks", "hgrn_lb", "hgrn_onorm", "w_out", "g_mix_pre", "g_mix_post", "g_mem", "g_x_pre",
           "g_x_post", "wq_x", "wk_x", "wv_x", "wo_x", "g_ffn_pre", "g_ffn_post", "w_gate", "w_up", "w_down")


def kernel(x, mem, w_in, sinks, hgrn_lb, hgrn_onorm, w_out, g_mix_pre, g_mix_post, g_mem, g_x_pre, g_x_post, wq_x, wk_x, wv_x, wo_x, g_ffn_pre, g_ffn_post, w_gate, w_up, w_down, loss_target, m_w_in, m_sinks, m_hgrn_lb, m_hgrn_onorm, m_w_out, m_g_mix_pre, m_g_mix_post, m_g_mem, m_g_x_pre, m_g_x_post, m_wq_x, m_wk_x, m_wv_x, m_wo_x, m_g_ffn_pre, m_g_ffn_post, m_w_gate, m_w_up, m_w_down, v_w_in, v_sinks, v_hgrn_lb, v_hgrn_onorm, v_w_out, v_g_mix_pre, v_g_mix_post, v_g_mem, v_g_x_pre, v_g_x_post, v_wq_x, v_wk_x, v_wv_x, v_wo_x, v_g_ffn_pre, v_g_ffn_post, v_w_gate, v_w_up, v_w_down):
    given = dict(locals())
    wts = {n: given[n] for n in WEIGHTS}
    ms = {n: given["m_" + n] for n in WEIGHTS}
    vs = {n: given["v_" + n] for n in WEIGHTS}

    def mat(a, name):
        a = a[0]
        return a.T if name in TRANSPOSED else a

    groups = (("w_in",), ("w_out", "wq_x", "wk_x", "wv_x", "wo_x"), ("w_gate", "w_up", "w_down"))
    gathers = []

    def start_group(g, dep):
        tag = ("w_in", "w_attn", "w_ffn")[g]
        shards, lands = _prepare_weights([mat(wts[n], n) for n in groups[g]], name="prepare_" + tag, dep=dep)
        gathers.append(_TwoLevelGather(shards, lands, name=tag))
        return gathers[-1].dep

    first_dep = start_group(1, start_group(0, None))
    name_of = {k: n for n, k in BIG_KEY.items()}
    gathered = {}

    def milestone(tag, value):
        if tag == "z":
            return start_group(2, value)
        return gathers[{"swa": 1, "kv": 2}[tag]].pass_on(value)

    def fetch(key, after):
        name = name_of[key]
        if name not in gathered:
            g = [i for i, group in enumerate(groups) if name in group][0]
            if g == 0:
                gathers[0].pass_on(after)
            gathered.update(zip(groups[g], gathers[g].finish(after)))
        return gathered[name]

    sm = {n: wts[n] for n in SMALL}
    started, held = {}, {}
    send_with = {k: group for group in (("wgT", "wuT"), ("wo", "wq", "wk", "wv")) for k in group}

    def emit(key, g):
        held[key] = g
        group = send_with.get(key, (key,))
        if key != group[-1]:
            return None
        flights = _exchange_start([held[k] for k in group], name="grad_send_" + name_of[group[0]])
        started.update({name_of[k]: f for k, f in zip(group, flights)})
        return flights[-1][2]

    grad_x, _, parts = _local_step(x[0], mem[0], loss_target[0], fetch, sm, emit, first_dep=first_dep, milestone=milestone)
    small_finish = _small_exchange(*_small_pack(parts))
    grads, deltas, new_m, new_v = {}, {}, {}, {}
    after = grad_x
    for group in (("w_down",), ("w_gate", "w_up"), ("wo_x", "wq_x", "wk_x", "wv_x", "w_out"), ("w_in",)):
        items = []
        for n in group:
            g_all, land = _exchange_wait(*started[n], after, name="grad_recv_" + n)
            items.append((g_all, land, mat(wts[n], n), mat(ms[n], n), mat(vs[n], n)))
            after = land
        for n, res in zip(group, _sum_adamw(items, name="adamw_" + group[0])):
            after = res[1]
            if n in TRANSPOSED:
                res = [a.T for a in res]
            grads[n], deltas[n], new_m[n], new_v[n] = [a[None] for a in res]
    loss, g_s, d_s, m_s, v_s = _small_update(small_finish(after), sm, {n: ms[n] for n in SMALL},
                                             {n: vs[n] for n in SMALL})
    grads.update(g_s), deltas.update(d_s), new_m.update(m_s), new_v.update(v_s)
    return (loss[0, 0], grad_x[None], *[grads[n] for n in WEIGHTS], *[deltas[n] for n in WEIGHTS],
            *[new_m[n] for n in WEIGHTS], *[new_v[n] for n in WEIGHTS])
```

```python
import functools

import jax
import jax.numpy as jnp
from jax import lax
from jax.experimental import pallas as pl
from jax.experimental.pallas import tpu as pltpu

F32 = jnp.float32
BF16 = jnp.bfloat16

D = 1024
D_IN = 2816
D_FF = 2816
CHUNK = 64
SWA_W = 512
KV_W = 128
HG_W = 512
HD = 128
ZQH, ZFH, ZIH, ZGH = 768, 1280, 1792, 2304
XH, XD = 4, 256
EPS = 1e-6
NEG = -1e30
N_DEV = 8
MESH = pl.DeviceIdType.MESH

LR, B1, B2, AEPS, WD, STEP = 0.001, 0.9, 0.999, 1e-08, 0.01, 10
C1 = 1.0 - B1 ** STEP
C2 = 1.0 - B2 ** STEP

VMEM_LIMIT = 56 * 1024 * 1024


def _params(**kw):
    return pltpu.CompilerParams(vmem_limit_bytes=VMEM_LIMIT, **kw)


def _sig(x):
    return 1.0 / (1.0 + jnp.exp(-x))


def _rowsum8(x):
    r, w = x.shape
    return jnp.sum(x.reshape(r // 8, 8, w), axis=0)


def _dot(a, b, ca, cb, precision=None):
    return lax.dot_general(a, b, (((ca,), (cb,)), ((), ())), preferred_element_type=F32,
                           precision=precision)


ANY_SPEC = pl.BlockSpec(memory_space=pl.ANY)


def _mm(a, b, *, ta=False, tb=False, out_dtype, tm, tn, tk=None, name, dep=None, n_outer=False):
    m = a.shape[1] if ta else a.shape[0]
    k = a.shape[0] if ta else a.shape[1]
    n = b.shape[0] if tb else b.shape[1]
    tm, tn = min(tm, m), min(tn, n)
    tk = k if tk is None else min(tk, k)
    nk = k // tk
    assert m % tm == 0 and n % tn == 0 and k % tk == 0, (name, m, n, k, tm, tn, tk)
    ij = (lambda g0, g1: (g1, g0)) if n_outer else (lambda g0, g1: (g0, g1))
    a_spec = (pl.BlockSpec((tk, tm), lambda g0, g1, kk: (kk, ij(g0, g1)[0])) if ta
              else pl.BlockSpec((tm, tk), lambda g0, g1, kk: (ij(g0, g1)[0], kk)))
    b_spec = (pl.BlockSpec((tn, tk), lambda g0, g1, kk: (ij(g0, g1)[1], kk)) if tb
              else pl.BlockSpec((tk, tn), lambda g0, g1, kk: (kk, ij(g0, g1)[1])))
    ca, cb = (0 if ta else 1), (1 if tb else 0)

    deps = [] if dep is None else [dep]

    def body(a_ref, b_ref, *rest):
        o_ref, acc = rest[len(deps)], rest[len(deps) + 1:]
        p = _dot(a_ref[...].astype(BF16), b_ref[...].astype(BF16), ca, cb)
        if nk == 1:
            o_ref[...] = p.astype(out_dtype)
        else:
            acc_ref, = acc
            kk = pl.program_id(2)

            @pl.when(kk == 0)
            def _():
                acc_ref[...] = p

            @pl.when(kk > 0)
            def _():
                acc_ref[...] += p

            @pl.when(kk == nk - 1)
            def _():
                o_ref[...] = acc_ref[...].astype(out_dtype)

    return pl.pallas_call(
        body, name=name, out_shape=jax.ShapeDtypeStruct((m, n), out_dtype),
        grid=(n // tn, m // tm, nk) if n_outer else (m // tm, n // tn, nk),
        in_specs=[a_spec, b_spec] + [ANY_SPEC] * len(deps),
        out_specs=pl.BlockSpec((tm, tn), lambda g0, g1, kk: ij(g0, g1)),
        scratch_shapes=[pltpu.VMEM((tm, tn), F32)] if nk > 1 else [],
        compiler_params=_params(dimension_semantics=("parallel", "parallel", "arbitrary")),
    )(a, b, *deps)


def _mm_rows(prods, rows_in, vecs_in, epilogue, outs, *, tm, name, dep=None):
    m = prods[0][0].shape[0]
    n = prods[0][1].shape[0] if prods[0][2] else prods[0][1].shape[1]
    tm = min(tm, m)
    assert m % tm == 0
    deps = [] if dep is None else [dep]
    n_p, n_r, n_v = len(prods), len(rows_in), len(vecs_in)

    def body(*refs):
        ab = refs[:2 * n_p]
        row_refs = refs[2 * n_p:2 * n_p + n_r]
        vec_refs = refs[2 * n_p + n_r:2 * n_p + n_r + n_v]
        out_refs = refs[2 * n_p + n_r + n_v + len(deps):]
        p = None
        for j, (_, _, tb) in enumerate(prods):
            t = _dot(ab[2 * j][...].astype(BF16), ab[2 * j + 1][...], 1, 1 if tb else 0)
            p = t if p is None else p + t
        vals = epilogue(p, *[r[...] for r in row_refs], *[v[...] for v in vec_refs])
        for (dtype, kind), o_ref, val in zip(outs, out_refs, vals):
            if kind == "row":
                o_ref[...] = val.astype(dtype)
            else:
                @pl.when(pl.program_id(0) == 0)
                def _(o_ref=o_ref):
                    o_ref[...] = jnp.zeros_like(o_ref)

                o_ref[...] += val

    row = lambda w: pl.BlockSpec((tm, w), lambda i: (i, 0))
    whole = lambda a: pl.BlockSpec(a.shape, lambda i: (0,) * a.ndim, pipeline_mode=pl.Buffered(1))
    in_specs, args = [], []
    for a, b, _ in prods:
        in_specs += [row(a.shape[1]), whole(b)]
        args += [a, b]
    in_specs += [row(r.shape[1]) for r in rows_in] + [whole(v) for v in vecs_in] + [ANY_SPEC] * len(deps)
    return pl.pallas_call(
        body, name=name,
        out_shape=tuple(jax.ShapeDtypeStruct((m, n) if kind == "row" else (8, n), dtype) for dtype, kind in outs),
        grid=(m // tm,), in_specs=in_specs,
        out_specs=tuple(row(n) if kind == "row" else pl.BlockSpec((8, n), lambda i: (0, 0)) for _, kind in outs),
        compiler_params=_params(dimension_semantics=("arbitrary",)),
    )(*args, *rows_in, *vecs_in, *deps)


def _rstd(x):
    return lax.rsqrt(jnp.mean(x * x, axis=-1, keepdims=True) + EPS)


def _norm_bwd(xh, r, t):
    return r * (t - xh * jnp.mean(xh * t, axis=-1, keepdims=True))


ROW_F32, ROW_BF16, SUM_F32 = (F32, "row"), (BF16, "row"), (F32, "sum")


def _then(epilogue, index, tb):
    def run(p, *args):
        vals = epilogue(p, *args[:-1])
        return (*vals, _dot(vals[index].astype(BF16), args[-1], 1, 1 if tb else 0))

    return run


def _ep_post_pre(p, h, g_post, g_pre):
    y = p.astype(BF16)
    yf = y.astype(F32)
    hn = h + yf * _rstd(yf) * g_post
    return y, hn, hn * _rstd(hn) * g_pre


_EP_POST_PRE_OUTS = [ROW_BF16, ROW_F32, ROW_BF16]


def _ep_final_loss(y, h, target, g_post):
    r = _rstd(y)
    yh = y * r
    err = h + yh * g_post - target
    dh = err * (1.0 / D)
    return _rowsum8(err * err), dh, _norm_bwd(yh, r, dh * g_post), _rowsum8(dh * yh)


def _ep_post_pre_bwd(du, dh_out, hn, y, g_post, g_pre):
    r2 = _rstd(hn)
    xh = hn * r2
    dh = dh_out + _norm_bwd(xh, r2, du * g_pre)
    yf = y.astype(F32)
    r1 = _rstd(yf)
    yh = yf * r1
    return dh, _norm_bwd(yh, r1, dh * g_post), _rowsum8(du * xh), _rowsum8(dh * yh)


_EP_POST_PRE_BWD_OUTS = [ROW_F32, ROW_BF16, SUM_F32, SUM_F32]


def _ep_pre_bwd(du, dh_out, x, g):
    r = _rstd(x)
    xh = x * r
    return dh_out + _norm_bwd(xh, r, du * g), _rowsum8(du * xh)


_EP_PRE_BWD_OUTS = [ROW_F32, SUM_F32]


def _prenorm(x, g, *, name, dep=None):
    t, d = x.shape
    tb = min(512, t)
    deps = [] if dep is None else [dep]

    def body(x_ref, g_ref, *rest):
        xf = x_ref[...]
        rest[-1][...] = (xf * _rstd(xf) * g_ref[...]).astype(BF16)

    return pl.pallas_call(
        body, name=name, out_shape=jax.ShapeDtypeStruct((t, d), BF16), grid=(t // tb,),
        in_specs=[pl.BlockSpec((tb, d), lambda i: (i, 0)), pl.BlockSpec((1, d), lambda i: (0, 0))]
        + [ANY_SPEC] * len(deps),
        out_specs=pl.BlockSpec((tb, d), lambda i: (i, 0)), compiler_params=_params(),
    )(x, g, *deps)


QB = 256


def _half_mask(shape, e):
    lane = lax.broadcasted_iota(jnp.int32, shape, len(shape) - 1)
    return (lane // 64) == e


def _place(kv):
    sw = pltpu.roll(kv, 64, 1)
    m0 = _half_mask(kv.shape, 0)
    return [[jnp.where(m0, kv, 0.0).astype(BF16), jnp.where(m0, 0.0, sw).astype(BF16)],
            [jnp.where(m0, sw, 0.0).astype(BF16), jnp.where(m0, 0.0, kv).astype(BF16)]]


SQ = 128
SK = 256


def _swa_valid(i, sb):
    qc = lax.broadcasted_iota(jnp.int32, (SQ, SK), 0) // CHUNK
    kc = lax.broadcasted_iota(jnp.int32, (SQ, SK), 1) // CHUNK - 2
    return (kc <= qc) & (qc <= kc + 2) & (4 * i + 2 * sb + kc >= 0)


def _swa_fwd(z, sinks, t, dep=None):
    nb = t // QB
    deps = [] if dep is None else [dep]

    def body(s_ref, q_ref, kp_ref, kc_ref, vp_ref, vc_ref, *rest):
        o_ref, lse_ref = rest[-2:]
        i = pl.program_id(0)
        kpl = _place(jnp.concatenate([kp_ref[...], kc_ref[...]], axis=0))
        vpl = _place(jnp.concatenate([vp_ref[...], vc_ref[...]], axis=0))
        lane = lax.broadcasted_iota(jnp.int32, (SQ, 128), 1)
        for sb in range(QB // SQ):
            rows, keys = slice(SQ * sb, SQ * (sb + 1)), slice(SQ * sb, SQ * sb + SK)
            valid = _swa_valid(i, sb)
            lse_out = jnp.zeros((SQ, 128), F32)
            for j in range(4):
                qp = q_ref[rows, 128 * j:128 * (j + 1)].astype(BF16)
                acc = jnp.zeros((SQ, 128), F32)
                for e in range(2):
                    h = 2 * j + e
                    kvh = h // 4
                    qm = jnp.where(_half_mask(qp.shape, e), qp, jnp.zeros_like(qp))
                    s = _dot(qm, kpl[kvh][e][keys], 1, 1) * 0.125
                    s = jnp.where(valid, s, NEG)
                    sink = s_ref[0, h]
                    m = jnp.maximum(jnp.max(s, axis=-1, keepdims=True), sink)
                    p = jnp.exp(s - m)
                    l = jnp.sum(p, axis=-1, keepdims=True) + jnp.exp(sink - m)
                    acc = acc + _dot(p.astype(BF16), vpl[kvh][e][keys], 1, 0) * (1.0 / l)
                    lse_out = jnp.where(lane == h, m + jnp.log(l), lse_out)
                o_ref[rows, 128 * j:128 * (j + 1)] = acc.astype(BF16)
            lse_ref[rows, :] = lse_out

    prev = lambda c: pl.BlockSpec((128, 128), lambda i: (jnp.maximum(2 * i - 1, 0), c))
    cur = lambda c: pl.BlockSpec((QB, 128), lambda i: (i, c))
    return pl.pallas_call(
        body, name="swa_fwd",
        out_shape=(jax.ShapeDtypeStruct((t, D), BF16), jax.ShapeDtypeStruct((t, 128), F32)),
        grid=(nb,),
        in_specs=[pl.BlockSpec(memory_space=pltpu.SMEM),
                  pl.BlockSpec((QB, SWA_W), lambda i: (i, 0)), prev(4), cur(4), prev(5), cur(5)]
        + [ANY_SPEC] * len(deps),
        out_specs=(pl.BlockSpec((QB, SWA_W), lambda i: (i, 0)), pl.BlockSpec((QB, 128), lambda i: (i, 0))),
        compiler_params=_params(),
    )(sinks, z, z, z, z, z, *deps)


def _swa_bwd(z, sinks, ymix, lse, dymix, t, dep=None):
    nb = t // QB
    deps = [] if dep is None else [dep]

    def body(s_ref, q_ref, kp_ref, kc_ref, vp_ref, vc_ref, o_ref, do_ref, l_ref, *rest):
        dq_ref, first_ref, second_ref, ds_ref, carry_ref = rest[len(deps):]
        i = pl.program_id(0)
        live = i < nb

        @pl.when(i == 0)
        def _():
            ds_ref[...] = jnp.zeros_like(ds_ref)
            carry_ref[...] = jnp.zeros_like(carry_ref)

        lane = lax.broadcasted_iota(jnp.int32, (8, 128), 1)
        kpl = _place(jnp.concatenate([kp_ref[...], kc_ref[...]], axis=0))
        vpl = _place(jnp.concatenate([vp_ref[...], vc_ref[...]], axis=0))
        nk = QB + 128
        qc = lax.broadcasted_iota(jnp.int32, (QB, nk), 0) // CHUNK
        kc = lax.broadcasted_iota(jnp.int32, (QB, nk), 1) // CHUNK - 2
        valid = (kc <= qc) & (qc <= kc + 2) & (4 * i + kc >= 0) & live
        lse_c = l_ref[...]
        dsink = jnp.zeros((8, 128), F32)
        dk_acc = [[jnp.zeros((128, nk), F32) for _ in range(2)] for _ in range(2)]
        dv_acc = [[jnp.zeros((128, nk), F32) for _ in range(2)] for _ in range(2)]
        dq = []
        for j in range(4):
            cols = slice(128 * j, 128 * (j + 1))
            qp = q_ref[:, cols].astype(BF16)
            dop = do_ref[:, cols]
            prod = dop.astype(F32) * o_ref[:, cols].astype(F32)
            acc = jnp.zeros((QB, 128), F32)
            for e in range(2):
                h = 2 * j + e
                kvh = h // 4
                hm = _half_mask(qp.shape, e)
                qm = jnp.where(hm, qp, jnp.zeros_like(qp))
                dom = jnp.where(hm, dop, jnp.zeros_like(dop))
                dd = jnp.sum(jnp.where(hm, prod, 0.0), axis=-1, keepdims=True)
                lse_h = lse_c[:, h:h + 1]
                s = _dot(qm, kpl[kvh][e], 1, 1) * 0.125
                p = jnp.where(valid, jnp.exp(s - lse_h), 0.0)
                dp = _dot(dom, vpl[kvh][e], 1, 1)
                ds = (p * (dp - dd) * 0.125).astype(BF16)
                acc = acc + _dot(ds, kpl[kvh][e], 1, 0)
                dk_acc[kvh][e] = dk_acc[kvh][e] + _dot(qm, ds, 0, 0)
                dv_acc[kvh][e] = dv_acc[kvh][e] + _dot(dom, p.astype(BF16), 0, 0)
                ps = jnp.where(live, jnp.exp(s_ref[0, h] - lse_h) * dd, 0.0)
                dsink = dsink - jnp.where(lane == h, _rowsum8(jnp.broadcast_to(ps, (QB, 128))), 0.0)
            dq.append(acc.astype(BF16))
        ds_ref[...] += dsink
        dk = (dk_acc[0][0] + dk_acc[1][1] + pltpu.roll(dk_acc[0][1] + dk_acc[1][0], 64, 0)).T
        dv = (dv_acc[0][0] + dv_acc[1][1] + pltpu.roll(dv_acc[0][1] + dv_acc[1][0], 64, 0)).T
        dkv = jnp.concatenate([dk, dv], axis=1)
        second_ref[...] = (carry_ref[...] + dkv[0:128]).astype(BF16)
        carry_ref[...] = dkv[256:384]

        @pl.when(live)
        def _():
            for j in range(4):
                dq_ref[:, 128 * j:128 * (j + 1)] = dq[j]
            first_ref[...] = dkv[128:256].astype(BF16)

    blk = lambda i: jnp.minimum(i, nb - 1)
    prev = lambda c: pl.BlockSpec((128, 128), lambda i: (jnp.maximum(2 * blk(i) - 1, 0), c))
    cur = lambda w, c: pl.BlockSpec((QB, w), lambda i: (blk(i), c))
    half = lambda index: pl.BlockSpec((128, 256), lambda i: (index(i), 0))
    return pl.pallas_call(
        body, name="swa_bwd",
        out_shape=(jax.ShapeDtypeStruct((t, SWA_W), BF16), jax.ShapeDtypeStruct((t // 2, 256), BF16),
                   jax.ShapeDtypeStruct((t // 2, 256), BF16), jax.ShapeDtypeStruct((8, 128), F32)),
        grid=(nb + 1,),
        in_specs=[pl.BlockSpec(memory_space=pltpu.SMEM),
                  cur(SWA_W, 0), prev(4), cur(128, 4), prev(5), cur(128, 5),
                  cur(SWA_W, 0), cur(SWA_W, 0), cur(128, 0)] + [ANY_SPEC] * len(deps),
        out_specs=(cur(SWA_W, 0), half(blk), half(lambda i: jnp.maximum(i - 1, 0)),
                   pl.BlockSpec((8, 128), lambda i: (0, 0))),
        scratch_shapes=[pltpu.VMEM((128, 256), F32)],
        compiler_params=_params(dimension_semantics=("arbitrary",)),
    )(sinks, z, z, z, z, z, ymix, dymix, lse, *deps)


HB = 256


def _lower_bound(lb_ref):
    a = lb_ref[...]
    a0, a1 = a[0:1], a[1:2]
    mx = jnp.maximum(a0, a1)
    e0, e1 = jnp.exp(a0 - mx), jnp.exp(a1 - mx)
    return e0 / (e0 + e1)


def _hgrn_cols(row_block):
    return [pl.BlockSpec((HB, 2 * HD), lambda j, c=base // (2 * HD) + p: (row_block(j), c))
            for base in (ZQH, ZFH, ZIH, ZGH) for p in range(2)]


NCH = HB // CHUNK


def _split3(x):
    hi = x.astype(BF16)
    r1 = x - hi.astype(F32)
    mid = r1.astype(BF16)
    return hi, mid, (r1 - mid.astype(F32)).astype(BF16)


def _blockdiag(lower):
    r = lax.broadcasted_iota(jnp.int32, (HB, HB), 0)
    c = lax.broadcasted_iota(jnp.int32, (HB, HB), 1)
    return (r // CHUNK == c // CHUNK) & ((c <= r) if lower else (c >= r))


def _chunk_sums(mask_bf16, x):
    return sum(_dot(mask_bf16, part, 1, 0) for part in _split3(x))


def _per_chunk_rows(x, row):
    w = x.shape[1]
    picked = x.reshape(NCH, CHUNK, w)[:, row:row + 1, :]
    return jnp.broadcast_to(picked, (NCH, CHUNK, w)).reshape(HB, w)


def _chunk_stack(x, chunk_of_row):
    return jnp.concatenate([jnp.where(chunk_of_row == c, x, jnp.zeros_like(x)) for c in range(NCH)], axis=1)


def _chunk_pick(x, chunk_of_row):
    w = x.shape[1] // NCH
    out = jnp.zeros((HB, w), x.dtype)
    for c in range(NCH):
        out = jnp.where(chunk_of_row == c, x[:, c * w:(c + 1) * w], out)
    return out


def _hgrn_local(q, f, kf, b):
    sq = _sig(q)
    qf = q * sq * (HD ** -0.5)
    b_mid = _per_chunk_rows(b, CHUNK // 2 - 1)
    b_last = _per_chunk_rows(b, CHUNK - 1)
    qm = qf * jnp.exp(b - b_mid)
    km = kf * jnp.exp(b_mid - b)
    kl = kf * jnp.exp(b_last - b)
    qb = qf * jnp.exp(b)
    return dict(sq=sq, b_mid=b_mid, b_last=b_last, qm=qm, km=km, kl=kl, qb=qb)


def _hgrn2_fwd(z, hgrn_lb, onorm, ymix, t, dep=None):
    nb = t // HB
    deps = [] if dep is None else [dep]

    def body(*refs):
        zq, zf, zi, zg = refs[0:2], refs[2:4], refs[4:6], refs[6:8]
        (lb_ref, on_ref), (y_ref, o_ref, sp_ref, st_ref) = refs[8:10], refs[-4:]

        @pl.when(pl.program_id(0) == 0)
        def _():
            st_ref[...] = jnp.zeros_like(st_ref)

        lb_all = _lower_bound(lb_ref)
        gn = on_ref[...]
        low = _blockdiag(True)
        low_b = low.astype(BF16)
        chunk_of_row = lax.broadcasted_iota(jnp.int32, (HB, HD), 0) // CHUNK
        for p in range(2):
            lbp = lb_all[:, 2 * HD * p:2 * HD * (p + 1)]
            fp = lbp + (1.0 - lbp) * _sig(zf[p][...])
            bp = _chunk_sums(low_b, jnp.log(fp))
            for e in range(2):
                h, ls = 2 * p + e, slice(e * HD, (e + 1) * HD)
                f = fp[:, ls]
                w = _hgrn_local(zq[p][:, ls], f, 1.0 - f, bp[:, ls])
                iv = zi[p][:, ls].astype(BF16)
                a = jnp.where(low, _dot(w["qm"].astype(BF16), w["km"].astype(BF16), 1, 1), 0.0)
                o = _dot(a.astype(BF16), iv, 1, 0)
                u = _dot(iv, _chunk_stack(w["kl"].astype(BF16), chunk_of_row), 0, 0)
                decay = jnp.exp(w["b_last"])
                st = st_ref[h]
                states = []
                for c in range(NCH):
                    sp_ref[h, c] = st
                    states.append(st.astype(BF16))
                    st = st * decay[c * CHUNK:c * CHUNK + 1] + u[:, c * HD:(c + 1) * HD]
                st_ref[h] = st
                inter = _dot(w["qb"].astype(BF16), jnp.concatenate(states, axis=0), 1, 1)
                o = o + _chunk_pick(inter, chunk_of_row)
                hs = slice(h * HD, (h + 1) * HD)
                o_ref[:, hs] = o
                gg = zg[p][:, ls]
                y_ref[:, hs] = (o * _rstd(o) * gn * (gg * _sig(gg))).astype(BF16)

    return pl.pallas_call(
        body, name="hgrn_fwd",
        out_shape=(jax.ShapeDtypeStruct((t, D), BF16), jax.ShapeDtypeStruct((t, HG_W), F32),
                   jax.ShapeDtypeStruct((4, t // CHUNK, HD, HD), F32)),
        grid=(nb,),
        in_specs=_hgrn_cols(lambda j: j) + [pl.BlockSpec((2, HG_W), lambda j: (0, 0)),
                                            pl.BlockSpec((1, HD), lambda j: (0, 0)), ANY_SPEC]
        + [ANY_SPEC] * len(deps),
        out_specs=(pl.BlockSpec((HB, HG_W), lambda j: (j, 1)),
                   pl.BlockSpec((HB, HG_W), lambda j: (j, 0)),
                   pl.BlockSpec((4, NCH, HD, HD), lambda j: (0, j, 0, 0))),
        scratch_shapes=[pltpu.VMEM((4, HD, HD), F32)],
        input_output_aliases={10: 0},
        compiler_params=_params(dimension_semantics=("arbitrary",)),
    )(*[z] * 8, hgrn_lb, onorm, ymix, *deps)


def _hgrn2_bwd(z, hgrn_lb, onorm, o_save, sprev, dymix, dza, t):
    nb = t // HB

    def body(*refs):
        zq, zf, zi, zg = refs[0:2], refs[2:4], refs[4:6], refs[6:8]
        (lb_ref, on_ref, o_ref, sp_ref, dy_ref, dqa_ref, first_ref, second_ref,
         dz_ref, dlb_ref, don_ref, dst_ref) = refs[8:]

        @pl.when(pl.program_id(0) == 0)
        def _():
            dst_ref[...] = jnp.zeros_like(dst_ref)
            dlb_ref[...] = jnp.zeros_like(dlb_ref)
            don_ref[...] = jnp.zeros_like(don_ref)

        dz_ref[:, 0:SWA_W] = dqa_ref[...]
        dz_ref[0:HB // 2, SWA_W:ZQH] = first_ref[...]
        dz_ref[HB // 2:HB, SWA_W:ZQH] = second_ref[...]
        lb_all = _lower_bound(lb_ref)
        gn = on_ref[...]
        low, upp = _blockdiag(True), _blockdiag(False)
        upp_b = upp.astype(BF16)
        low_b = low.astype(BF16)
        row = lax.broadcasted_iota(jnp.int32, (HB, HD), 0)
        chunk_of_row = row // CHUNK
        in_chunk = row % CHUNK
        for p in range(2):
            lbp = lb_all[:, 2 * HD * p:2 * HD * (p + 1)]
            sgp = _sig(zf[p][...])
            fp = lbp + (1.0 - lbp) * sgp
            bp = _chunk_sums(low_b, jnp.log(fp))
            db_pair, dkf_pair = [], []
            for e in range(2):
                h, ls, hs = 2 * p + e, slice(e * HD, (e + 1) * HD), slice((2 * p + e) * HD, (2 * p + e + 1) * HD)
                f = fp[:, ls]
                q = zq[p][:, ls]
                w = _hgrn_local(q, f, 1.0 - f, bp[:, ls])
                iv = zi[p][:, ls].astype(BF16)
                gg = zg[p][:, ls]
                o = o_ref[:, hs]
                dout = dy_ref[:, hs].astype(F32)
                sgg = _sig(gg)
                r = _rstd(o)
                oh = o * r
                dyn = dout * (gg * sgg)
                dz_ref[:, ZGH + h * HD:ZGH + (h + 1) * HD] = (
                    dout * oh * gn * (sgg * (1.0 + gg * (1.0 - sgg)))).astype(BF16)
                don_ref[...] += _rowsum8(dyn * oh)
                do = _norm_bwd(oh, r, dyn * gn).astype(BF16)
                qm, km, kl, qb = (w[n].astype(BF16) for n in ("qm", "km", "kl", "qb"))
                decay = jnp.exp(w["b_last"])
                grads_in = _dot(do, _chunk_stack(qb, chunk_of_row), 0, 0)
                dst = dst_ref[h]
                dstn, dd_rows = [None] * NCH, [None] * NCH
                for c in reversed(range(NCH)):
                    dstn[c] = dst.astype(BF16)
                    dd_rows[c] = jnp.sum(dst * sp_ref[h, c], axis=0, keepdims=True)
                    dst = dst * decay[c * CHUNK:c * CHUNK + 1] + grads_in[:, c * HD:(c + 1) * HD]
                dst_ref[h] = dst
                states = jnp.concatenate([sp_ref[h, c].astype(BF16) for c in range(NCH)], axis=0)
                dstn_all = jnp.concatenate(dstn, axis=0)
                dqb = _dot(_chunk_stack(do, chunk_of_row), states, 1, 0)
                at = jnp.where(upp, _dot(km, qm, 1, 1), 0.0)
                di = _dot(at.astype(BF16), do, 1, 0) + _chunk_pick(_dot(kl, dstn_all, 1, 1), chunk_of_row)
                dz_ref[:, ZIH + h * HD:ZIH + (h + 1) * HD] = di.astype(BF16)
                dkl = _dot(_chunk_stack(iv, chunk_of_row), dstn_all, 1, 0)
                da = jnp.where(low, _dot(do, iv, 1, 1), 0.0).astype(BF16)
                dat = jnp.where(upp, _dot(iv, do, 1, 1), 0.0).astype(BF16)
                dqm = _dot(da, km, 1, 0)
                dkm = _dot(dat, qm, 1, 0)
                b = bp[:, ls]
                e1, e2 = jnp.exp(b - w["b_mid"]), jnp.exp(w["b_mid"] - b)
                e3, e4 = jnp.exp(w["b_last"] - b), jnp.exp(b)
                dqf = dqm * e1 + dqb * e4
                dkf_pair.append(dkm * e2 + dkl * e3)
                t_qm, t_km, t_kl = dqm * w["qm"], dkm * w["km"], dkl * w["kl"]
                db = t_qm - t_km - t_kl + dqb * w["qb"]
                db_mid = jnp.sum((t_km - t_qm).reshape(NCH, CHUNK, HD), axis=1, keepdims=True)
                db_last = jnp.sum(t_kl.reshape(NCH, CHUNK, HD), axis=1, keepdims=True)
                db_last = db_last + jnp.stack(dd_rows, axis=0) * jnp.exp(
                    bp[:, ls].reshape(NCH, CHUNK, HD)[:, CHUNK - 1:CHUNK, :])
                spread = lambda v: jnp.broadcast_to(v, (NCH, CHUNK, HD)).reshape(HB, HD)
                db = (db + jnp.where(in_chunk == CHUNK // 2 - 1, spread(db_mid), 0.0)
                      + jnp.where(in_chunk == CHUNK - 1, spread(db_last), 0.0))
                db_pair.append(db)
                sq = w["sq"]
                dz_ref[:, ZQH + h * HD:ZQH + (h + 1) * HD] = (
                    dqf * (HD ** -0.5) * (sq * (1.0 + q * (1.0 - sq)))).astype(BF16)
            dlogf = _chunk_sums(upp_b, jnp.concatenate(db_pair, axis=1))
            dfv = dlogf / fp - jnp.concatenate(dkf_pair, axis=1)
            dz_ref[:, ZFH + 2 * HD * p:ZFH + 2 * HD * (p + 1)] = (dfv * (1.0 - lbp) * sgp * (1.0 - sgp)).astype(BF16)
            dlb_ref[:, 2 * HD * p:2 * HD * (p + 1)] += _rowsum8(dfv * (1.0 - sgp))

    rev = lambda j: nb - 1 - j
    return pl.pallas_call(
        body, name="hgrn_bwd",
        out_shape=(jax.ShapeDtypeStruct((t, D_IN), BF16), jax.ShapeDtypeStruct((8, HG_W), F32),
                   jax.ShapeDtypeStruct((8, HD), F32)),
        grid=(nb,),
        in_specs=_hgrn_cols(rev) + [pl.BlockSpec((2, HG_W), lambda j: (0, 0)), pl.BlockSpec((1, HD), lambda j: (0, 0)),
                                    pl.BlockSpec((HB, HG_W), lambda j: (rev(j), 0)),
                                    pl.BlockSpec((4, NCH, HD, HD), lambda j: (0, rev(j), 0, 0)),
                                    pl.BlockSpec((HB, HG_W), lambda j: (rev(j), 1)),
                                    pl.BlockSpec((HB, SWA_W), lambda j: (rev(j), 0)),
                                    pl.BlockSpec((HB // 2, 2 * KV_W), lambda j: (rev(j), 0)),
                                    pl.BlockSpec((HB // 2, 2 * KV_W), lambda j: (rev(j), 0))],
        out_specs=(pl.BlockSpec((HB, D_IN), lambda j: (rev(j), 0)), pl.BlockSpec((8, HG_W), lambda j: (0, 0)),
                   pl.BlockSpec((8, HD), lambda j: (0, 0))),
        scratch_shapes=[pltpu.VMEM((4, HD, HD), F32)],
        compiler_params=_params(dimension_semantics=("arbitrary",)),
    )(*[z] * 8, hgrn_lb, onorm, o_save, sprev, dymix, *dza)


XB = 512


def _xattn_fwd(q, k, v, wo, h, g_post, g_pre, t, dep=None):
    tb = min(XB, t)
    deps = [] if dep is None else [dep]

    def body(q_ref, k_ref, v_ref, wo_ref, h_ref, gp_ref, gn_ref, *rest):
        o_ref, y_ref, hn_ref, u_ref = rest[len(deps):]
        for hd in range(XH):
            cols = slice(XD * hd, XD * (hd + 1))
            s = _dot(q_ref[:, cols], k_ref[:, cols], 1, 1) * (XD ** -0.5)
            p = jnp.exp(s - jnp.max(s, axis=-1, keepdims=True))
            l = jnp.sum(p, axis=-1, keepdims=True)
            o_ref[:, cols] = (_dot(p.astype(BF16), v_ref[:, cols], 1, 0) * (1.0 / l)).astype(BF16)
        y, hn, u = _ep_post_pre(_dot(o_ref[...], wo_ref[...], 1, 0), h_ref[...], gp_ref[...], gn_ref[...])
        y_ref[...] = y
        hn_ref[...] = hn
        u_ref[...] = u.astype(BF16)

    row = pl.BlockSpec((tb, D), lambda i: (i, 0))
    whole = lambda a: pl.BlockSpec(a.shape, lambda i: (0,) * a.ndim, pipeline_mode=pl.Buffered(1))
    half = jax.ShapeDtypeStruct((t, D), BF16)
    return pl.pallas_call(
        body, name="xattn_fwd", out_shape=(half, half, jax.ShapeDtypeStruct((t, D), F32), half), grid=(t // tb,),
        in_specs=[row, whole(k), whole(v), whole(wo), row, whole(g_post), whole(g_pre)] + [ANY_SPEC] * len(deps),
        out_specs=(row, row, row, row), compiler_params=_params(),
    )(q, k, v, wo, h, g_post, g_pre, *deps)


def _xattn_bwd(q, k, v, do, wq, wout, dh_out, hn, y, g_post, g_pre, t):
    tb = min(XB, t)

    def body(q_ref, k_ref, v_ref, do_ref, wq_ref, wout_ref, dho_ref, hn_ref, y_ref, gp_ref, gn_ref,
             dq_ref, dk_ref, dv_ref, dh_ref, dyp_ref, dym_ref, dgn_ref, dgp_ref):
        @pl.when(pl.program_id(0) == 0)
        def _():
            dk_ref[...] = jnp.zeros_like(dk_ref)
            dv_ref[...] = jnp.zeros_like(dv_ref)
            dgn_ref[...] = jnp.zeros_like(dgn_ref)
            dgp_ref[...] = jnp.zeros_like(dgp_ref)

        for h in range(XH):
            cols = slice(XD * h, XD * (h + 1))
            qh, kh, vh, doh = q_ref[:, cols], k_ref[:, cols], v_ref[:, cols], do_ref[:, cols]
            s = _dot(qh, kh, 1, 1) * (XD ** -0.5)
            p = jnp.exp(s - jnp.max(s, axis=-1, keepdims=True))
            p = p * (1.0 / jnp.sum(p, axis=-1, keepdims=True))
            dp = _dot(doh, vh, 1, 1)
            ds = (p * (dp - jnp.sum(p * dp, axis=-1, keepdims=True)) * (XD ** -0.5)).astype(BF16)
            dq_ref[:, cols] = _dot(ds, kh, 1, 0).astype(BF16)
            dk_ref[:, cols] += _dot(ds, qh, 0, 0)
            dv_ref[:, cols] += _dot(p.astype(BF16), doh, 0, 0)
        du = _dot(dq_ref[...], wq_ref[...], 1, 1)
        dh, dyp, dgn, dgp = _ep_post_pre_bwd(du, dho_ref[...], hn_ref[...], y_ref[...], gp_ref[...], gn_ref[...])
        dh_ref[...] = dh
        dyp = dyp.astype(BF16)
        dyp_ref[...] = dyp
        dym_ref[...] = _dot(dyp, wout_ref[...], 1, 1).astype(BF16)
        dgn_ref[...] += dgn
        dgp_ref[...] += dgp

    row = pl.BlockSpec((tb, D), lambda i: (i, 0))
    mem = pl.BlockSpec(k.shape, lambda i: (0, 0))
    whole = lambda a: pl.BlockSpec(a.shape, lambda i: (0,) * a.ndim, pipeline_mode=pl.Buffered(1))
    acc = pl.BlockSpec((8, D), lambda i: (0, 0))
    half = jax.ShapeDtypeStruct((t, D), BF16)
    return pl.pallas_call(
        body, name="xattn_bwd",
        out_shape=(half, jax.ShapeDtypeStruct(k.shape, F32), jax.ShapeDtypeStruct(k.shape, F32),
                   jax.ShapeDtypeStruct((t, D), F32), half, half,
                   jax.ShapeDtypeStruct((8, D), F32), jax.ShapeDtypeStruct((8, D), F32)),
        grid=(t // tb,),
        in_specs=[row, whole(k), whole(v), row, whole(wq), whole(wout), row, row, row, whole(g_post), whole(g_pre)],
        out_specs=(row, mem, mem, row, row, row, acc, acc),
        compiler_params=_params(dimension_semantics=("arbitrary",)),
    )(q, k, v, do, wq, wout, dh_out, hn, y, g_post, g_pre)


def _mem_kv(mem, g_mem, wk, wv):
    def body(m_ref, g_ref, wk_ref, wv_ref, mn_ref, k_ref, v_ref):
        m_ = m_ref[...]
        mn = (m_ * _rstd(m_) * g_ref[...]).astype(BF16)
        mn_ref[...] = mn
        k_ref[...] = _dot(mn, wk_ref[...], 1, 0).astype(BF16)
        v_ref[...] = _dot(mn, wv_ref[...], 1, 0).astype(BF16)

    return pl.pallas_call(body, name="mem_kv", out_shape=(jax.ShapeDtypeStruct(mem.shape, BF16),) * 3,
                          compiler_params=_params())(mem, g_mem, wk, wv)


def _mem_kv_bwd(mn, mem, dk, dv, wk, wv, dep=None):
    deps = [] if dep is None else [dep]

    def body(mn_ref, m_ref, dk_ref, dv_ref, wk_ref, wv_ref, *rest):
        gk_ref, gv_ref, dg_ref = rest[len(deps):]
        mn = mn_ref[...]
        dkb, dvb = dk_ref[...].astype(BF16), dv_ref[...].astype(BF16)
        gk_ref[...] = _dot(mn, dkb, 0, 0).astype(BF16)
        gv_ref[...] = _dot(mn, dvb, 0, 0).astype(BF16)
        dmn = _dot(dkb, wk_ref[...], 1, 1) + _dot(dvb, wv_ref[...], 1, 1)
        m_ = m_ref[...]
        dg_ref[...] = _rowsum8(dmn * (m_ * _rstd(m_)))

    vmem = pl.BlockSpec(memory_space=pltpu.VMEM)
    return pl.pallas_call(
        body, name="mem_kv_bwd",
        out_shape=(jax.ShapeDtypeStruct(wk.shape, BF16), jax.ShapeDtypeStruct(wv.shape, BF16),
                   jax.ShapeDtypeStruct((8, D), F32)),
        in_specs=[vmem] * 6 + [ANY_SPEC] * len(deps), out_specs=(vmem,) * 3, compiler_params=_params(),
    )(mn, mem, dk, dv, wk, wv, *deps)


FB = 256


def _ffn_fwd_bwd(u, wgt, wut, wd, h, target, g_last, y_prev, g_post, g_pre, wo, t):
    tb = min(FB, t)

    def body(u_ref, wg_ref, wu_ref, wd_ref, h_ref, t_ref, gl_ref, yp_ref, gp_ref, gn_ref, wo_ref,
             a_ref, dy_ref, dg_ref, dup_ref, dh_ref, dyp_ref, do_ref, sq_ref, dgl_ref, dgn_ref, dgp_ref):
        @pl.when(pl.program_id(0) == 0)
        def _():
            for ref in (sq_ref, dgl_ref, dgn_ref, dgp_ref):
                ref[...] = jnp.zeros_like(ref)

        u_ = u_ref[...]
        g = _dot(u_, wg_ref[...], 1, 1)
        up = _dot(u_, wu_ref[...], 1, 1)
        sg = _sig(g)
        a = (g * sg * up).astype(BF16)
        a_ref[...] = a
        h_ = h_ref[...]
        sq, dh3, dy, dgl = _ep_final_loss(_dot(a, wd_ref[...], 1, 0), h_, t_ref[...], gl_ref[...])
        sq_ref[...] += sq
        dgl_ref[...] += dgl
        dy = dy.astype(BF16)
        dy_ref[...] = dy
        da = _dot(dy, wd_ref[...], 1, 1)
        dup = (da * g * sg).astype(BF16)
        dgate = (da * up * (sg * (1.0 + g * (1.0 - sg)))).astype(BF16)
        dup_ref[...] = dup
        dg_ref[...] = dgate
        du = _dot(dgate, wg_ref[...], 1, 0) + _dot(dup, wu_ref[...], 1, 0)
        dh, dyp, dgn, dgp = _ep_post_pre_bwd(du, dh3, h_, yp_ref[...], gp_ref[...], gn_ref[...])
        dh_ref[...] = dh
        dyp = dyp.astype(BF16)
        dyp_ref[...] = dyp
        do_ref[...] = _dot(dyp, wo_ref[...], 1, 1).astype(BF16)
        dgn_ref[...] += dgn
        dgp_ref[...] += dgp

    row = lambda w: pl.BlockSpec((tb, w), lambda i: (i, 0))
    whole = lambda a: pl.BlockSpec(a.shape, lambda i: (0,) * a.ndim, pipeline_mode=pl.Buffered(1))
    acc = pl.BlockSpec((8, D), lambda i: (0, 0))
    wide, half, sums = (jax.ShapeDtypeStruct((t, D_FF), BF16), jax.ShapeDtypeStruct((t, D), BF16),
                        jax.ShapeDtypeStruct((8, D), F32))
    return pl.pallas_call(
        body, name="ffn_fwd_bwd",
        out_shape=(wide, half, wide, wide, jax.ShapeDtypeStruct((t, D), F32), half, half, sums, sums, sums, sums),
        grid=(t // tb,),
        in_specs=[row(D), whole(wgt), whole(wut), whole(wd), row(D), row(D), whole(g_last), row(D), whole(g_post),
                  whole(g_pre), whole(wo)],
        out_specs=(row(D_FF), row(D), row(D_FF), row(D_FF), row(D), row(D), row(D), acc, acc, acc, acc),
        compiler_params=_params(dimension_semantics=("arbitrary",)),
    )(u, wgt, wut, wd, h, target, g_last, y_prev, g_post, g_pre, wo)


def _local_step(x, mem, target, fetch, sm, emit=None, first_dep=None, milestone=None):
    t = x.shape[0]
    w, gw = {}, {}

    def out(key, g):
        gw[key] = g
        return None if emit is None else emit(key, g)

    def tell(tag, value):
        return None if milestone is None else milestone(tag, value)
    u1 = _prenorm(x, sm["g_mix_pre"], name="prenorm_mix", dep=first_dep)
    w["winT"] = fetch("winT", u1)
    z = _mm(u1, w["winT"], tb=True, out_dtype=F32, tm=1024, tn=1408, name="mm_z", n_outer=True)
    ymix, lse = _swa_fwd(z, sm["sinks"], t, dep=tell("z", z))
    ymix, o_h, sprev = _hgrn2_fwd(z, sm["hgrn_lb"], sm["hgrn_onorm"], ymix, t, dep=tell("swa", lse))
    for key in ("wout", "wq", "wk", "wv", "wo"):
        w[key] = fetch(key, ymix)
    y1, h1, u2, qx = _mm_rows([(ymix, w["wout"], False)], [x], [sm["g_mix_post"], sm["g_x_pre"], w["wq"]],
                              _then(_ep_post_pre, 2, False), _EP_POST_PRE_OUTS + [ROW_BF16], tm=1024,
                              name="mm_y1_post_qx")
    mn, kx, vx = _mem_kv(mem, sm["g_mem"], w["wk"], w["wv"])
    ox, y2, h2, u3 = _xattn_fwd(qx, kx, vx, w["wo"], h1, sm["g_x_post"], sm["g_ffn_pre"], t, dep=tell("kv", kx))
    for key in ("wgT", "wuT", "wd"):
        w[key] = fetch(key, u3)
    act, dy3, dgate, dup, dh2, dy2, dox, sq, dg_ffn_post, dg_ffn_pre, dg_x_post = _ffn_fwd_bwd(
        u3, w["wgT"], w["wuT"], w["wd"], h2, target, sm["g_ffn_post"], y2, sm["g_x_post"], sm["g_ffn_pre"], w["wo"], t)
    dep = out("wd", _mm(act, dy3, ta=True, out_dtype=BF16, tm=1408, tn=1024, name="mm_gwd"))
    dep = out("wgT", _mm(dgate, u3, ta=True, out_dtype=BF16, tm=1408, tn=1024, name="mm_gwg", dep=dep))
    dep = out("wuT", _mm(dup, u3, ta=True, out_dtype=BF16, tm=1408, tn=1024, name="mm_gwu", dep=dep))
    out("wo", _mm(ox, dy2, ta=True, out_dtype=BF16, tm=512, tn=1024, name="mm_gwo", dep=dep))
    dqx, dkx, dvx, dh1, dy1, dymix, dg_x_pre, dg_mix_post = _xattn_bwd(
        qx, kx, vx, dox, w["wq"], w["wout"], dh2, h1, y1, sm["g_mix_post"], sm["g_x_pre"], t)
    out("wq", _mm(u2, dqx, ta=True, out_dtype=BF16, tm=512, tn=1024, name="mm_gwq"))
    gwk, gwv, dg_mem = _mem_kv_bwd(mn, mem, dkx, dvx, w["wk"], w["wv"])
    out("wk", gwk)
    dep = out("wv", gwv)
    dep = out("wout", _mm(ymix, dy1, ta=True, out_dtype=BF16, tm=512, tn=1024, name="mm_gwout", dep=dep))
    *dza, dsinks = _swa_bwd(z, sm["sinks"], ymix, lse, dymix, t, dep=dep)
    dz, dlb, donorm = _hgrn2_bwd(z, sm["hgrn_lb"], sm["hgrn_onorm"], o_h, sprev, dymix, dza, t)
    dep = out("winT", _mm(dz, u1, ta=True, out_dtype=BF16, tm=1408, tn=1024, name="mm_gwin"))
    grad_x, dg_mix_pre = _mm_rows([(dz, w["winT"], False)], [dh1, x], [sm["g_mix_pre"]], _ep_pre_bwd,
                                  _EP_PRE_BWD_OUTS, tm=512, name="mm_du1_pre_bwd", dep=dep)
    parts = dict(g_mix_pre=dg_mix_pre, g_mix_post=dg_mix_post, g_mem=dg_mem, g_x_pre=dg_x_pre,
                 g_x_post=dg_x_post, g_ffn_pre=dg_ffn_pre, g_ffn_post=dg_ffn_post,
                 hgrn_onorm=donorm, hgrn_lb=dlb, sinks=dsinks, sq=sq)
    return grad_x, gw, parts


def _position():
    return lax.axis_index("x"), lax.axis_index("y"), lax.axis_index("c")


def _peer(pos, k):
    x, y, c = pos
    return (1 - x if k & 4 else x, 1 - y if k & 2 else y, 1 - c if k & 1 else c)


def _linear(pos):
    x, y, c = pos
    return 4 * x + 2 * y + c


HBM_SPEC = pl.BlockSpec(memory_space=pltpu.HBM)
SEM_SPEC = pl.BlockSpec(memory_space=pltpu.SEMAPHORE)
DATAFLOW = pltpu.SideEffectType.DATAFLOW_SIDE_EFFECTING
SEND_ORDER = (1, 2, 4, 3, 5, 6, 7)


def _in_hbm(a):
    return pltpu.with_memory_space_constraint(a, pltpu.HBM)


def _prepare_weights(shards, *, name, dep=None):
    n = len(shards)
    deps = [] if dep is None else [dep]

    def body(*refs):
        ins, (outs, lands, sem) = refs[:n], (refs[-2 * n - 1:-n - 1], refs[-n - 1:-1], refs[-1])
        me_lin = _linear(_position())
        copies = []
        for a in range(n):
            r = ins[a].shape[0]
            outs[a][...] = ins[a][...].astype(BF16)
            copies.append(pltpu.make_async_copy(outs[a], lands[a].at[pl.ds(me_lin * r, r), :], sem.at[a]))
            copies[-1].start()
        for cp in copies:
            cp.wait()

    vmem = pl.BlockSpec(memory_space=pltpu.VMEM)
    res = pl.pallas_call(
        body, name=name,
        out_shape=tuple(jax.ShapeDtypeStruct(s.shape, BF16) for s in shards)
        + tuple(jax.ShapeDtypeStruct((N_DEV * s.shape[0], s.shape[1]), BF16) for s in shards),
        in_specs=[vmem] * n + [ANY_SPEC] * len(deps), out_specs=tuple([vmem] * n + [ANY_SPEC] * n),
        scratch_shapes=[pltpu.SemaphoreType.DMA((n,))], compiler_params=_params(),
    )(*shards, *deps)
    return res[:n], res[n:]


def _copies_start(arrays, plan, n, *, name):
    na = len(arrays)

    def body(*refs):
        ins, send_sems, recv_sems = refs[:na], refs[na], refs[na + 1]
        me = _position()
        for j in range(n):
            src, dst, peer, _ = plan(ins, me, j)
            pltpu.make_async_remote_copy(src_ref=src, dst_ref=dst, send_sem=send_sems.at[j], recv_sem=recv_sems.at[j],
                                         device_id=peer, device_id_type=MESH).start()

    return pl.pallas_call(
        body, name=name,
        out_shape=(pltpu.SemaphoreType.DMA((n,)), pltpu.SemaphoreType.DMA((n,)))
        + tuple(pltpu.HBM(a.shape, a.dtype) for a in arrays),
        in_specs=(HBM_SPEC,) * na, out_specs=(SEM_SPEC, SEM_SPEC) + (HBM_SPEC,) * na,
        input_output_aliases={i: 2 + i for i in range(na)},
        compiler_params=pltpu.CompilerParams(has_side_effects=DATAFLOW),
    )(*[_in_hbm(a) for a in arrays])


def _copies_wait(send_sems, recv_sems, arrays, plan, n, after, *, name):
    na = len(arrays)

    def body(*refs):
        ins, send_sems, recv_sems = refs[:na], refs[na], refs[na + 1]
        me = _position()
        for j in range(n):
            src, _, peer, landed = plan(ins, me, j)
            copy = pltpu.make_async_remote_copy(src_ref=src, dst_ref=landed, send_sem=send_sems.at[j],
                                                recv_sem=recv_sems.at[j], device_id=peer, device_id_type=MESH)
            copy.wait_send()
            copy.wait_recv()

    return pl.pallas_call(
        body, name=name, out_shape=tuple(pltpu.HBM(a.shape, a.dtype) for a in arrays),
        in_specs=(HBM_SPEC,) * na + (SEM_SPEC, SEM_SPEC, ANY_SPEC), out_specs=(HBM_SPEC,) * na,
        input_output_aliases={i: i for i in range(na)},
        compiler_params=pltpu.CompilerParams(has_side_effects=DATAFLOW),
    )(*arrays, send_sems, recv_sems, after)


SAME_CORE = (2, 4, 6)


class _TwoLevelGather:
    def __init__(self, shards, lands, *, name):
        n = self.n = len(shards)
        self.name = name
        first_peers = (1,) + SAME_CORE

        def rows(ref, pos):
            r = ref.shape[0] // N_DEV
            return ref.at[pl.ds(_linear(pos) * r, r), :]

        def first(refs, me, j):
            a, peer = j // 4, _peer(me, first_peers[j % 4])
            return refs[a], rows(refs[n + a], me), peer, rows(refs[n + a], peer)

        def second(refs, me, j):
            a, sibling = j // 3, _peer(me, 1)
            mine = rows(refs[a], _peer(me, SAME_CORE[j % 3]))
            return mine, mine, sibling, rows(refs[a], _peer(sibling, SAME_CORE[j % 3]))

        self._first, self._second = first, second
        self._flight = _copies_start(list(shards) + list(lands), first, 4 * n, name=name + "_send")
        self.dep = self._flight[2]

    def pass_on(self, after):
        send1, recv1, *arrays = self._flight
        arrays = _copies_wait(send1, recv1, arrays, self._first, 4 * self.n, after, name=self.name + "_recv")
        self._flight = _copies_start(list(arrays[self.n:]), self._second, 3 * self.n, name=self.name + "_pass")
        return self._flight[2]

    def finish(self, after):
        send2, recv2, *lands = self._flight
        return _copies_wait(send2, recv2, lands, self._second, 3 * self.n, after, name=self.name + "_pass_recv")


def _exchange_start(gs, *, name):
    n = len(gs)
    rows = [g.shape[0] // N_DEV for g in gs]
    lands = [lax.empty((N_DEV - 1, r, g.shape[1]), g.dtype) for g, r in zip(gs, rows)]

    def body(*refs):
        g_refs, land_refs = refs[:n], refs[n:2 * n]
        send_sems, recv_sems = refs[2 * n:3 * n], refs[3 * n:4 * n]
        me = _position()
        for a in range(n):
            for k in SEND_ORDER:
                peer = _peer(me, k)
                pltpu.make_async_remote_copy(
                    src_ref=g_refs[a].at[pl.ds(_linear(peer) * rows[a], rows[a]), :],
                    dst_ref=land_refs[a].at[k - 1],
                    send_sem=send_sems[a].at[k - 1], recv_sem=recv_sems[a].at[k - 1],
                    device_id=peer, device_id_type=MESH).start()

    res = pl.pallas_call(
        body, name=name,
        out_shape=tuple(pltpu.SemaphoreType.DMA((N_DEV - 1,)) for _ in range(2 * n))
        + tuple(pltpu.HBM(a.shape, a.dtype) for a in gs + lands),
        in_specs=(HBM_SPEC,) * (2 * n), out_specs=(SEM_SPEC,) * (2 * n) + (HBM_SPEC,) * (2 * n),
        input_output_aliases={i: 2 * n + i for i in range(2 * n)},
        compiler_params=pltpu.CompilerParams(has_side_effects=DATAFLOW),
    )(*[_in_hbm(a) for a in gs + lands])
    return [(res[a], res[n + a], res[2 * n + a], res[3 * n + a]) for a in range(n)]


def _exchange_wait(send_sems, recv_sems, g_thru, land_thru, after, *, name):
    r = land_thru.shape[1]

    def body(g_ref, land_ref, send_sems, recv_sems, after_ref, g_dead, got_ref):
        del after_ref, g_dead, got_ref
        me = _position()
        for k in SEND_ORDER:
            peer = _peer(me, k)
            copy = pltpu.make_async_remote_copy(
                src_ref=g_ref.at[pl.ds(_linear(peer) * r, r), :], dst_ref=land_ref.at[k - 1],
                send_sem=send_sems.at[k - 1], recv_sem=recv_sems.at[k - 1],
                device_id=peer, device_id_type=MESH)
            copy.wait_send()
            copy.wait_recv()

    return pl.pallas_call(
        body, name=name,
        out_shape=(pltpu.HBM(g_thru.shape, g_thru.dtype), pltpu.HBM(land_thru.shape, land_thru.dtype)),
        in_specs=(HBM_SPEC, HBM_SPEC, SEM_SPEC, SEM_SPEC, pl.BlockSpec(memory_space=pl.ANY)),
        out_specs=(HBM_SPEC, HBM_SPEC), input_output_aliases={0: 0, 1: 1},
        compiler_params=pltpu.CompilerParams(has_side_effects=DATAFLOW),
    )(g_thru, land_thru, send_sems, recv_sems, after)


def _adamw_math(w, g, m, v):
    m = B1 * m + (1.0 - B1) * g
    v = B2 * v + (1.0 - B2) * (g * g)
    delta = -LR * ((m / C1) / (jnp.sqrt(v / C2) + AEPS) + WD * w)
    return delta, m, v


def _sum_adamw(items, *, name):
    n = len(items)

    def body(*refs):
        ins, outs, scratch = refs[:5 * n], refs[5 * n:9 * n], refs[9 * n:]
        me_lin = _linear(_position())
        mine = []
        for a in range(n):
            r = items[a][2].shape[0]
            mine.append(pltpu.make_async_copy(ins[5 * a].at[pl.ds(me_lin * r, r), :], scratch[a], scratch[n].at[a]))
            mine[-1].start()
        for a in range(n):
            _, land_ref, w_ref, m_ref, v_ref = ins[5 * a:5 * a + 5]
            g_ref, d_ref, nm_ref, nv_ref = outs[4 * a:4 * a + 4]
            g = land_ref[0].astype(F32)
            for s in range(1, N_DEV - 1):
                g = g + land_ref[s].astype(F32)
            mine[a].wait()
            g = scratch[a][...].astype(F32) + g
            g_ref[...] = g
            d_ref[...], nm_ref[...], nv_ref[...] = _adamw_math(w_ref[...], g, m_ref[...], v_ref[...])

    vmem = pl.BlockSpec(memory_space=pltpu.VMEM)
    res = pl.pallas_call(
        body, name=name,
        out_shape=tuple(jax.ShapeDtypeStruct(it[2].shape, F32) for it in items for _ in range(4)),
        in_specs=[ANY_SPEC, vmem, vmem, vmem, vmem] * n, out_specs=(vmem,) * (4 * n),
        scratch_shapes=[pltpu.VMEM(it[2].shape, BF16) for it in items] + [pltpu.SemaphoreType.DMA((n,))],
        compiler_params=_params(),
    )(*[a for it in items for a in it])
    return [res[4 * a:4 * a + 4] for a in range(n)]


SMALL = ("g_mix_pre", "g_mix_post", "g_mem", "g_x_pre", "g_x_post", "g_ffn_pre", "g_ffn_post",
         "hgrn_onorm", "hgrn_lb", "sinks")
SMALL_W = dict(hgrn_onorm=HD, hgrn_lb=HG_W, sinks=8)
SQ_ROW = len(SMALL)
PACK_ROWS = 16


def _small_pack(parts):
    ns = len(SMALL)

    def body(*refs):
        part, mine, slots, sem = refs[:ns + 1], refs[ns + 1], refs[ns + 2], refs[ns + 3]
        mine[...] = jnp.zeros((PACK_ROWS, D), F32)
        for r, name in enumerate(SMALL):
            wd = SMALL_W.get(name, D)
            mine[r:r + 1, 0:wd] = jnp.sum(part[r][...], axis=0, keepdims=True)[:, 0:wd]
        sq = jnp.sum(part[ns][...]) * (0.5 / D)
        mine[SQ_ROW:SQ_ROW + 1, :] = jnp.full((1, D), sq, F32)
        own = pltpu.make_async_copy(mine, slots.at[_linear(_position())], sem)
        own.start()
        own.wait()

    vmem = pl.BlockSpec(memory_space=pltpu.VMEM)
    return pl.pallas_call(
        body, name="small_pack",
        out_shape=(jax.ShapeDtypeStruct((PACK_ROWS, D), F32), jax.ShapeDtypeStruct((N_DEV, PACK_ROWS, D), F32)),
        in_specs=[vmem] * (ns + 1), out_specs=(vmem, ANY_SPEC),
        scratch_shapes=[pltpu.SemaphoreType.DMA(())], compiler_params=_params(),
    )(*[parts[n] for n in SMALL], parts["sq"])


def _small_exchange(mine, slots):
    def plan(refs, me, j):
        peer = _peer(me, j + 1)
        return refs[0], refs[1].at[_linear(me)], peer, refs[1].at[_linear(peer)]

    send, recv, mine1, slots1 = _copies_start([mine, slots], plan, N_DEV - 1, name="small_send")
    return lambda after: _copies_wait(send, recv, [mine1, slots1], plan, N_DEV - 1, after, name="small_recv")[1]


def _small_update(slots, sm, m_sm, v_sm):
    ns = len(SMALL)

    def body(*refs):
        tot = refs[0][0]
        for s in range(1, N_DEV):
            tot = tot + refs[0][s]
        w_refs, m_refs, v_refs = refs[1:ns + 1], refs[ns + 1:2 * ns + 1], refs[2 * ns + 1:3 * ns + 1]
        outs = refs[3 * ns + 1:]
        loss_ref = outs[0]
        g_out, d_out = outs[1:ns + 1], outs[ns + 1:2 * ns + 1]
        nm_out, nv_out = outs[2 * ns + 1:3 * ns + 1], outs[3 * ns + 1:4 * ns + 1]
        loss_ref[...] = tot[SQ_ROW:SQ_ROW + 1, 0:1]
        for r, name in enumerate(SMALL):
            wd = SMALL_W.get(name, D)
            g = tot[r:r + 1, 0:wd]
            w = w_refs[r][...]
            if name == "hgrn_lb":
                mx = jnp.maximum(w[0:1], w[1:2])
                e0, e1 = jnp.exp(w[0:1] - mx), jnp.exp(w[1:2] - mx)
                lb0 = e0 / (e0 + e1)
                g0 = g * lb0 * (1.0 - lb0)
                for i, gi in enumerate((g0, -g0)):
                    d, nm, nv = _adamw_math(w[i:i + 1], gi, m_refs[r][i:i + 1, :], v_refs[r][i:i + 1, :])
                    g_out[r][i:i + 1, :] = gi
                    d_out[r][i:i + 1, :], nm_out[r][i:i + 1, :], nv_out[r][i:i + 1, :] = d, nm, nv
            else:
                d, nm, nv = _adamw_math(w, g, m_refs[r][...], v_refs[r][...])
                g_out[r][...] = g
                d_out[r][...], nm_out[r][...], nv_out[r][...] = d, nm, nv

    shapes = [jax.ShapeDtypeStruct(sm[n].shape, F32) for n in SMALL]
    res = pl.pallas_call(
        body, name="small_update", out_shape=tuple([jax.ShapeDtypeStruct((1, 1), F32)] + shapes * 4),
        compiler_params=_params(),
    )(slots, *[sm[n] for n in SMALL], *[m_sm[n] for n in SMALL], *[v_sm[n] for n in SMALL])
    groups = [dict(zip(SMALL, res[1 + i * ns:1 + (i + 1) * ns])) for i in range(4)]
    return res[0], groups[0], groups[1], groups[2], groups[3]


BIG = ("w_in", "w_gate", "w_up", "w_down", "w_out", "wq_x", "wk_x", "wv_x", "wo_x")
BIG_KEY = dict(w_in="winT", w_gate="wgT", w_up="wuT", w_down="wd", w_out="wout", wq_x="wq", wk_x="wk",
               wv_x="wv", wo_x="wo")
TRANSPOSED = ("w_in", "w_gate", "w_up")
WEIGHTS = ("w_in", "sinks", "hgrn_lb", "hgrn_onorm", "w_out", "g_mix_pre", "g_mix_post", "g_mem", "g_x_pre",
           "g_x_post", "wq_x", "wk_x", "wv_x", "wo_x", "g_ffn_pre", "g_ffn_post", "w_gate", "w_up", "w_down")


def kernel(x, mem, w_in, sinks, hgrn_lb, hgrn_onorm, w_out, g_mix_pre, g_mix_post, g_mem, g_x_pre, g_x_post, wq_x, wk_x, wv_x, wo_x, g_ffn_pre, g_ffn_post, w_gate, w_up, w_down, loss_target, m_w_in, m_sinks, m_hgrn_lb, m_hgrn_onorm, m_w_out, m_g_mix_pre, m_g_mix_post, m_g_mem, m_g_x_pre, m_g_x_post, m_wq_x, m_wk_x, m_wv_x, m_wo_x, m_g_ffn_pre, m_g_ffn_post, m_w_gate, m_w_up, m_w_down, v_w_in, v_sinks, v_hgrn_lb, v_hgrn_onorm, v_w_out, v_g_mix_pre, v_g_mix_post, v_g_mem, v_g_x_pre, v_g_x_post, v_wq_x, v_wk_x, v_wv_x, v_wo_x, v_g_ffn_pre, v_g_ffn_post, v_w_gate, v_w_up, v_w_down):
    given = dict(locals())
    wts = {n: given[n] for n in WEIGHTS}
    ms = {n: given["m_" + n] for n in WEIGHTS}
    vs = {n: given["v_" + n] for n in WEIGHTS}

    def mat(a, name):
        a = a[0]
        return a.T if name in TRANSPOSED else a

    groups = (("w_in",), ("w_out", "wq_x", "wk_x", "wv_x", "wo_x"), ("w_gate", "w_up", "w_down"))
    gathers = []

    def start_group(g, dep):
        tag = ("w_in", "w_attn", "w_ffn")[g]
        shards, lands = _prepare_weights([mat(wts[n], n) for n in groups[g]], name="prepare_" + tag, dep=dep)
        gathers.append(_TwoLevelGather(shards, lands, name=tag))
        return gathers[-1].dep

    first_dep = start_group(0, None)
    name_of = {k: n for n, k in BIG_KEY.items()}
    gathered = {}

    def milestone(tag, value):
        if tag == "z":
            return start_group(2, value)
        return gathers[{"swa": 1, "kv": 2}[tag]].pass_on(value)

    def fetch(key, after):
        name = name_of[key]
        if name not in gathered:
            g = [i for i, group in enumerate(groups) if name in group][0]
            if g == 0:
                after = start_group(1, gathers[0].pass_on(after))
            gathered.update(zip(groups[g], gathers[g].finish(after)))
        return gathered[name]

    sm = {n: wts[n] for n in SMALL}
    started, held = {}, {}
    send_with = {k: group for group in (("wgT", "wuT"), ("wo", "wq", "wk", "wv")) for k in group}

    def emit(key, g):
        held[key] = g
        group = send_with.get(key, (key,))
        if key != group[-1]:
            return None
        flights = _exchange_start([held[k] for k in group], name="grad_send_" + name_of[group[0]])
        started.update({name_of[k]: f for k, f in zip(group, flights)})
        return flights[-1][2]

    grad_x, _, parts = _local_step(x[0], mem[0], loss_target[0], fetch, sm, emit, first_dep=first_dep, milestone=milestone)
    small_finish = _small_exchange(*_small_pack(parts))
    grads, deltas, new_m, new_v = {}, {}, {}, {}
    after = grad_x
    for group in (("w_down",), ("w_gate", "w_up"), ("wo_x", "wq_x", "wk_x", "wv_x", "w_out"), ("w_in",)):
        items = []
        for n in group:
            g_all, land = _exchange_wait(*started[n], after, name="grad_recv_" + n)
            items.append((g_all, land, mat(wts[n], n), mat(ms[n], n), mat(vs[n], n)))
            after = land
        for n, res in zip(group, _sum_adamw(items, name="adamw_" + group[0])):
            after = res[1]
            if n in TRANSPOSED:
                res = [a.T for a in res]
            grads[n], deltas[n], new_m[n], new_v[n] = [a[None] for a in res]
    loss, g_s, d_s, m_s, v_s = _small_update(small_finish(after), sm, {n: ms[n] for n in SMALL},
                                             {n: vs[n] for n in SMALL})
    grads.update(g_s), deltas.update(d_s), new_m.update(m_s), new_v.update(v_s)
    return (loss[0, 0], grad_x[None], *[grads[n] for n in WEIGHTS], *[deltas[n] for n in WEIGHTS],
            *[new_m[n] for n in WEIGHTS], *[new_v[n] for n in WEIGHTS])
```

```python
import functools

import jax
import jax.numpy as jnp
from jax import lax
from jax.experimental import pallas as pl
from jax.experimental.pallas import tpu as pltpu

F32 = jnp.float32
BF16 = jnp.bfloat16

D = 1024
D_IN = 2816
D_FF = 2816
CHUNK = 64
SWA_W = 512
KV_W = 128
HG_W = 512
HD = 128
ZQH, ZFH, ZIH, ZGH = 768, 1280, 1792, 2304
XH, XD = 4, 256
EPS = 1e-6
NEG = -1e30
N_DEV = 8
MESH = pl.DeviceIdType.MESH

LR, B1, B2, AEPS, WD, STEP = 0.001, 0.9, 0.999, 1e-08, 0.01, 10
C1 = 1.0 - B1 ** STEP
C2 = 1.0 - B2 ** STEP

VMEM_LIMIT = 56 * 1024 * 1024


def _params(**kw):
    return pltpu.CompilerParams(vmem_limit_bytes=VMEM_LIMIT, **kw)


def _sig(x):
    return 1.0 / (1.0 + jnp.exp(-x))


def _rowsum8(x):
    r, w = x.shape
    return jnp.sum(x.reshape(r // 8, 8, w), axis=0)


def _dot(a, b, ca, cb, precision=None):
    return lax.dot_general(a, b, (((ca,), (cb,)), ((), ())), preferred_element_type=F32,
                           precision=precision)


ANY_SPEC = pl.BlockSpec(memory_space=pl.ANY)


def _mm(a, b, *, ta=False, tb=False, out_dtype, tm, tn, tk=None, name, dep=None, n_outer=False):
    m = a.shape[1] if ta else a.shape[0]
    k = a.shape[0] if ta else a.shape[1]
    n = b.shape[0] if tb else b.shape[1]
    tm, tn = min(tm, m), min(tn, n)
    tk = k if tk is None else min(tk, k)
    nk = k // tk
    assert m % tm == 0 and n % tn == 0 and k % tk == 0, (name, m, n, k, tm, tn, tk)
    ij = (lambda g0, g1: (g1, g0)) if n_outer else (lambda g0, g1: (g0, g1))
    a_spec = (pl.BlockSpec((tk, tm), lambda g0, g1, kk: (kk, ij(g0, g1)[0])) if ta
              else pl.BlockSpec((tm, tk), lambda g0, g1, kk: (ij(g0, g1)[0], kk)))
    b_spec = (pl.BlockSpec((tn, tk), lambda g0, g1, kk: (ij(g0, g1)[1], kk)) if tb
              else pl.BlockSpec((tk, tn), lambda g0, g1, kk: (kk, ij(g0, g1)[1])))
    ca, cb = (0 if ta else 1), (1 if tb else 0)

    deps = [] if dep is None else [dep]

    def body(a_ref, b_ref, *rest):
        o_ref, acc = rest[len(deps)], rest[len(deps) + 1:]
        p = _dot(a_ref[...].astype(BF16), b_ref[...].astype(BF16), ca, cb)
        if nk == 1:
            o_ref[...] = p.astype(out_dtype)
        else:
            acc_ref, = acc
            kk = pl.program_id(2)

            @pl.when(kk == 0)
            def _():
                acc_ref[...] = p

            @pl.when(kk > 0)
            def _():
                acc_ref[...] += p

            @pl.when(kk == nk - 1)
            def _():
                o_ref[...] = acc_ref[...].astype(out_dtype)

    return pl.pallas_call(
        body, name=name, out_shape=jax.ShapeDtypeStruct((m, n), out_dtype),
        grid=(n // tn, m // tm, nk) if n_outer else (m // tm, n // tn, nk),
        in_specs=[a_spec, b_spec] + [ANY_SPEC] * len(deps),
        out_specs=pl.BlockSpec((tm, tn), lambda g0, g1, kk: ij(g0, g1)),
        scratch_shapes=[pltpu.VMEM((tm, tn), F32)] if nk > 1 else [],
        compiler_params=_params(dimension_semantics=("parallel", "parallel", "arbitrary")),
    )(a, b, *deps)


def _mm_rows(prods, rows_in, vecs_in, epilogue, outs, *, tm, name, dep=None):
    m = prods[0][0].shape[0]
    n = prods[0][1].shape[0] if prods[0][2] else prods[0][1].shape[1]
    tm = min(tm, m)
    assert m % tm == 0
    deps = [] if dep is None else [dep]
    n_p, n_r, n_v = len(prods), len(rows_in), len(vecs_in)

    def body(*refs):
        ab = refs[:2 * n_p]
        row_refs = refs[2 * n_p:2 * n_p + n_r]
        vec_refs = refs[2 * n_p + n_r:2 * n_p + n_r + n_v]
        out_refs = refs[2 * n_p + n_r + n_v + len(deps):]
        p = None
        for j, (_, _, tb) in enumerate(prods):
            t = _dot(ab[2 * j][...].astype(BF16), ab[2 * j + 1][...], 1, 1 if tb else 0)
            p = t if p is None else p + t
        vals = epilogue(p, *[r[...] for r in row_refs], *[v[...] for v in vec_refs])
        for (dtype, kind), o_ref, val in zip(outs, out_refs, vals):
            if kind == "row":
                o_ref[...] = val.astype(dtype)
            else:
                @pl.when(pl.program_id(0) == 0)
                def _(o_ref=o_ref):
                    o_ref[...] = jnp.zeros_like(o_ref)

                o_ref[...] += val

    row = lambda w: pl.BlockSpec((tm, w), lambda i: (i, 0))
    whole = lambda a: pl.BlockSpec(a.shape, lambda i: (0,) * a.ndim, pipeline_mode=pl.Buffered(1))
    in_specs, args = [], []
    for a, b, _ in prods:
        in_specs += [row(a.shape[1]), whole(b)]
        args += [a, b]
    in_specs += [row(r.shape[1]) for r in rows_in] + [whole(v) for v in vecs_in] + [ANY_SPEC] * len(deps)
    return pl.pallas_call(
        body, name=name,
        out_shape=tuple(jax.ShapeDtypeStruct((m, n) if kind == "row" else (8, n), dtype) for dtype, kind in outs),
        grid=(m // tm,), in_specs=in_specs,
        out_specs=tuple(row(n) if kind == "row" else pl.BlockSpec((8, n), lambda i: (0, 0)) for _, kind in outs),
        compiler_params=_params(dimension_semantics=("arbitrary",)),
    )(*args, *rows_in, *vecs_in, *deps)


def _rstd(x):
    return lax.rsqrt(jnp.mean(x * x, axis=-1, keepdims=True) + EPS)


def _norm_bwd(xh, r, t):
    return r * (t - xh * jnp.mean(xh * t, axis=-1, keepdims=True))


ROW_F32, ROW_BF16, SUM_F32 = (F32, "row"), (BF16, "row"), (F32, "sum")


def _then(epilogue, index, tb):
    def run(p, *args):
        vals = epilogue(p, *args[:-1])
        return (*vals, _dot(vals[index].astype(BF16), args[-1], 1, 1 if tb else 0))

    return run


def _ep_post_pre(p, h, g_post, g_pre):
    y = p.astype(BF16)
    yf = y.astype(F32)
    hn = h + yf * _rstd(yf) * g_post
    return y, hn, hn * _rstd(hn) * g_pre


_EP_POST_PRE_OUTS = [ROW_BF16, ROW_F32, ROW_BF16]


def _ep_final_loss(y, h, target, g_post):
    r = _rstd(y)
    yh = y * r
    err = h + yh * g_post - target
    dh = err * (1.0 / D)
    return _rowsum8(err * err), dh, _norm_bwd(yh, r, dh * g_post), _rowsum8(dh * yh)


def _ep_post_pre_bwd(du, dh_out, hn, y, g_post, g_pre):
    r2 = _rstd(hn)
    xh = hn * r2
    dh = dh_out + _norm_bwd(xh, r2, du * g_pre)
    yf = y.astype(F32)
    r1 = _rstd(yf)
    yh = yf * r1
    return dh, _norm_bwd(yh, r1, dh * g_post), _rowsum8(du * xh), _rowsum8(dh * yh)


_EP_POST_PRE_BWD_OUTS = [ROW_F32, ROW_BF16, SUM_F32, SUM_F32]


def _ep_pre_bwd(du, dh_out, x, g):
    r = _rstd(x)
    xh = x * r
    return dh_out + _norm_bwd(xh, r, du * g), _rowsum8(du * xh)


_EP_PRE_BWD_OUTS = [ROW_F32, SUM_F32]


def _prenorm(x, g, *, name, dep=None):
    t, d = x.shape
    tb = min(512, t)
    deps = [] if dep is None else [dep]

    def body(x_ref, g_ref, *rest):
        xf = x_ref[...]
        rest[-1][...] = (xf * _rstd(xf) * g_ref[...]).astype(BF16)

    return pl.pallas_call(
        body, name=name, out_shape=jax.ShapeDtypeStruct((t, d), BF16), grid=(t // tb,),
        in_specs=[pl.BlockSpec((tb, d), lambda i: (i, 0)), pl.BlockSpec((1, d), lambda i: (0, 0))]
        + [ANY_SPEC] * len(deps),
        out_specs=pl.BlockSpec((tb, d), lambda i: (i, 0)), compiler_params=_params(),
    )(x, g, *deps)


QB = 256


def _half_mask(shape, e):
    lane = lax.broadcasted_iota(jnp.int32, shape, len(shape) - 1)
    return (lane // 64) == e


def _place(kv):
    sw = pltpu.roll(kv, 64, 1)
    m0 = _half_mask(kv.shape, 0)
    return [[jnp.where(m0, kv, 0.0).astype(BF16), jnp.where(m0, 0.0, sw).astype(BF16)],
            [jnp.where(m0, sw, 0.0).astype(BF16), jnp.where(m0, 0.0, kv).astype(BF16)]]


SQ = 128
SK = 256


def _swa_valid(i, sb):
    qc = lax.broadcasted_iota(jnp.int32, (SQ, SK), 0) // CHUNK
    kc = lax.broadcasted_iota(jnp.int32, (SQ, SK), 1) // CHUNK - 2
    return (kc <= qc) & (qc <= kc + 2) & (4 * i + 2 * sb + kc >= 0)


def _swa_fwd(z, sinks, t, dep=None):
    nb = t // QB
    deps = [] if dep is None else [dep]

    def body(s_ref, q_ref, kp_ref, kc_ref, vp_ref, vc_ref, *rest):
        o_ref, lse_ref = rest[-2:]
        i = pl.program_id(0)
        kpl = _place(jnp.concatenate([kp_ref[...], kc_ref[...]], axis=0))
        vpl = _place(jnp.concatenate([vp_ref[...], vc_ref[...]], axis=0))
        lane = lax.broadcasted_iota(jnp.int32, (SQ, 128), 1)
        for sb in range(QB // SQ):
            rows, keys = slice(SQ * sb, SQ * (sb + 1)), slice(SQ * sb, SQ * sb + SK)
            valid = _swa_valid(i, sb)
            lse_out = jnp.zeros((SQ, 128), F32)
            for j in range(4):
                qp = q_ref[rows, 128 * j:128 * (j + 1)].astype(BF16)
                acc = jnp.zeros((SQ, 128), F32)
                for e in range(2):
                    h = 2 * j + e
                    kvh = h // 4
                    qm = jnp.where(_half_mask(qp.shape, e), qp, jnp.zeros_like(qp))
                    s = _dot(qm, kpl[kvh][e][keys], 1, 1) * 0.125
                    s = jnp.where(valid, s, NEG)
                    sink = s_ref[0, h]
                    m = jnp.maximum(jnp.max(s, axis=-1, keepdims=True), sink)
                    p = jnp.exp(s - m)
                    l = jnp.sum(p, axis=-1, keepdims=True) + jnp.exp(sink - m)
                    acc = acc + _dot(p.astype(BF16), vpl[kvh][e][keys], 1, 0) * (1.0 / l)
                    lse_out = jnp.where(lane == h, m + jnp.log(l), lse_out)
                o_ref[rows, 128 * j:128 * (j + 1)] = acc.astype(BF16)
            lse_ref[rows, :] = lse_out

    prev = lambda c: pl.BlockSpec((128, 128), lambda i: (jnp.maximum(2 * i - 1, 0), c))
    cur = lambda c: pl.BlockSpec((QB, 128), lambda i: (i, c))
    return pl.pallas_call(
        body, name="swa_fwd",
        out_shape=(jax.ShapeDtypeStruct((t, D), BF16), jax.ShapeDtypeStruct((t, 128), F32)),
        grid=(nb,),
        in_specs=[pl.BlockSpec(memory_space=pltpu.SMEM),
                  pl.BlockSpec((QB, SWA_W), lambda i: (i, 0)), prev(4), cur(4), prev(5), cur(5)]
        + [ANY_SPEC] * len(deps),
        out_specs=(pl.BlockSpec((QB, SWA_W), lambda i: (i, 0)), pl.BlockSpec((QB, 128), lambda i: (i, 0))),
        compiler_params=_params(),
    )(sinks, z, z, z, z, z, *deps)


def _swa_bwd(z, sinks, ymix, lse, dymix, t, dep=None):
    nb = t // QB
    deps = [] if dep is None else [dep]

    def body(s_ref, q_ref, kp_ref, kc_ref, vp_ref, vc_ref, o_ref, do_ref, l_ref, *rest):
        dq_ref, first_ref, second_ref, ds_ref, carry_ref = rest[len(deps):]
        i = pl.program_id(0)
        live = i < nb

        @pl.when(i == 0)
        def _():
            ds_ref[...] = jnp.zeros_like(ds_ref)
            carry_ref[...] = jnp.zeros_like(carry_ref)

        lane = lax.broadcasted_iota(jnp.int32, (8, 128), 1)
        kpl = _place(jnp.concatenate([kp_ref[...], kc_ref[...]], axis=0))
        vpl = _place(jnp.concatenate([vp_ref[...], vc_ref[...]], axis=0))
        nk = QB + 128
        qc = lax.broadcasted_iota(jnp.int32, (QB, nk), 0) // CHUNK
        kc = lax.broadcasted_iota(jnp.int32, (QB, nk), 1) // CHUNK - 2
        valid = (kc <= qc) & (qc <= kc + 2) & (4 * i + kc >= 0) & live
        lse_c = l_ref[...]
        dsink = jnp.zeros((8, 128), F32)
        dk_acc = [[jnp.zeros((128, nk), F32) for _ in range(2)] for _ in range(2)]
        dv_acc = [[jnp.zeros((128, nk), F32) for _ in range(2)] for _ in range(2)]
        dq = []
        for j in range(4):
            cols = slice(128 * j, 128 * (j + 1))
            qp = q_ref[:, cols].astype(BF16)
            dop = do_ref[:, cols]
            prod = dop.astype(F32) * o_ref[:, cols].astype(F32)
            acc = jnp.zeros((QB, 128), F32)
            for e in range(2):
                h = 2 * j + e
                kvh = h // 4
                hm = _half_mask(qp.shape, e)
                qm = jnp.where(hm, qp, jnp.zeros_like(qp))
                dom = jnp.where(hm, dop, jnp.zeros_like(dop))
                dd = jnp.sum(jnp.where(hm, prod, 0.0), axis=-1, keepdims=True)
                lse_h = lse_c[:, h:h + 1]
                s = _dot(qm, kpl[kvh][e], 1, 1) * 0.125
                p = jnp.where(valid, jnp.exp(s - lse_h), 0.0)
                dp = _dot(dom, vpl[kvh][e], 1, 1)
                ds = (p * (dp - dd) * 0.125).astype(BF16)
                acc = acc + _dot(ds, kpl[kvh][e], 1, 0)
                dk_acc[kvh][e] = dk_acc[kvh][e] + _dot(qm, ds, 0, 0)
                dv_acc[kvh][e] = dv_acc[kvh][e] + _dot(dom, p.astype(BF16), 0, 0)
                ps = jnp.where(live, jnp.exp(s_ref[0, h] - lse_h) * dd, 0.0)
                dsink = dsink - jnp.where(lane == h, _rowsum8(jnp.broadcast_to(ps, (QB, 128))), 0.0)
            dq.append(acc.astype(BF16))
        ds_ref[...] += dsink
        dk = (dk_acc[0][0] + dk_acc[1][1] + pltpu.roll(dk_acc[0][1] + dk_acc[1][0], 64, 0)).T
        dv = (dv_acc[0][0] + dv_acc[1][1] + pltpu.roll(dv_acc[0][1] + dv_acc[1][0], 64, 0)).T
        dkv = jnp.concatenate([dk, dv], axis=1)
        second_ref[...] = (carry_ref[...] + dkv[0:128]).astype(BF16)
        carry_ref[...] = dkv[256:384]

        @pl.when(live)
        def _():
            for j in range(4):
                dq_ref[:, 128 * j:128 * (j + 1)] = dq[j]
            first_ref[...] = dkv[128:256].astype(BF16)

    blk = lambda i: jnp.minimum(i, nb - 1)
    prev = lambda c: pl.BlockSpec((128, 128), lambda i: (jnp.maximum(2 * blk(i) - 1, 0), c))
    cur = lambda w, c: pl.BlockSpec((QB, w), lambda i: (blk(i), c))
    half = lambda index: pl.BlockSpec((128, 256), lambda i: (index(i), 0))
    return pl.pallas_call(
        body, name="swa_bwd",
        out_shape=(jax.ShapeDtypeStruct((t, SWA_W), BF16), jax.ShapeDtypeStruct((t // 2, 256), BF16),
                   jax.ShapeDtypeStruct((t // 2, 256), BF16), jax.ShapeDtypeStruct((8, 128), F32)),
        grid=(nb + 1,),
        in_specs=[pl.BlockSpec(memory_space=pltpu.SMEM),
                  cur(SWA_W, 0), prev(4), cur(128, 4), prev(5), cur(128, 5),
                  cur(SWA_W, 0), cur(SWA_W, 0), cur(128, 0)] + [ANY_SPEC] * len(deps),
        out_specs=(cur(SWA_W, 0), half(blk), half(lambda i: jnp.maximum(i - 1, 0)),
                   pl.BlockSpec((8, 128), lambda i: (0, 0))),
        scratch_shapes=[pltpu.VMEM((128, 256), F32)],
        compiler_params=_params(dimension_semantics=("arbitrary",)),
    )(sinks, z, z, z, z, z, ymix, dymix, lse, *deps)


HB = 256


def _lower_bound(lb_ref):
    a = lb_ref[...]
    a0, a1 = a[0:1], a[1:2]
    mx = jnp.maximum(a0, a1)
    e0, e1 = jnp.exp(a0 - mx), jnp.exp(a1 - mx)
    return e0 / (e0 + e1)


def _hgrn_cols(row_block):
    return [pl.BlockSpec((HB, 2 * HD), lambda j, c=base // (2 * HD) + p: (row_block(j), c))
            for base in (ZQH, ZFH, ZIH, ZGH) for p in range(2)]


NCH = HB // CHUNK


def _split3(x):
    hi = x.astype(BF16)
    r1 = x - hi.astype(F32)
    mid = r1.astype(BF16)
    return hi, mid, (r1 - mid.astype(F32)).astype(BF16)


def _blockdiag(lower):
    r = lax.broadcasted_iota(jnp.int32, (HB, HB), 0)
    c = lax.broadcasted_iota(jnp.int32, (HB, HB), 1)
    return (r // CHUNK == c // CHUNK) & ((c <= r) if lower else (c >= r))


def _chunk_sums(mask_bf16, x):
    return sum(_dot(mask_bf16, part, 1, 0) for part in _split3(x))


def _per_chunk_rows(x, row):
    w = x.shape[1]
    picked = x.reshape(NCH, CHUNK, w)[:, row:row + 1, :]
    return jnp.broadcast_to(picked, (NCH, CHUNK, w)).reshape(HB, w)


def _chunk_stack(x, chunk_of_row):
    return jnp.concatenate([jnp.where(chunk_of_row == c, x, jnp.zeros_like(x)) for c in range(NCH)], axis=1)


def _chunk_pick(x, chunk_of_row):
    w = x.shape[1] // NCH
    out = jnp.zeros((HB, w), x.dtype)
    for c in range(NCH):
        out = jnp.where(chunk_of_row == c, x[:, c * w:(c + 1) * w], out)
    return out


def _hgrn_local(q, f, kf, b):
    sq = _sig(q)
    qf = q * sq * (HD ** -0.5)
    b_mid = _per_chunk_rows(b, CHUNK // 2 - 1)
    b_last = _per_chunk_rows(b, CHUNK - 1)
    qm = qf * jnp.exp(b - b_mid)
    km = kf * jnp.exp(b_mid - b)
    kl = kf * jnp.exp(b_last - b)
    qb = qf * jnp.exp(b)
    return dict(sq=sq, b_mid=b_mid, b_last=b_last, qm=qm, km=km, kl=kl, qb=qb)


def _hgrn2_fwd(z, hgrn_lb, onorm, ymix, t, dep=None):
    nb = t // HB
    deps = [] if dep is None else [dep]

    def body(*refs):
        zq, zf, zi, zg = refs[0:2], refs[2:4], refs[4:6], refs[6:8]
        (lb_ref, on_ref), (y_ref, o_ref, sp_ref, st_ref) = refs[8:10], refs[-4:]

        @pl.when(pl.program_id(0) == 0)
        def _():
            st_ref[...] = jnp.zeros_like(st_ref)

        lb_all = _lower_bound(lb_ref)
        gn = on_ref[...]
        low = _blockdiag(True)
        low_b = low.astype(BF16)
        chunk_of_row = lax.broadcasted_iota(jnp.int32, (HB, HD), 0) // CHUNK
        for p in range(2):
            lbp = lb_all[:, 2 * HD * p:2 * HD * (p + 1)]
            fp = lbp + (1.0 - lbp) * _sig(zf[p][...])
            bp = _chunk_sums(low_b, jnp.log(fp))
            for e in range(2):
                h, ls = 2 * p + e, slice(e * HD, (e + 1) * HD)
                f = fp[:, ls]
                w = _hgrn_local(zq[p][:, ls], f, 1.0 - f, bp[:, ls])
                iv = zi[p][:, ls].astype(BF16)
                a = jnp.where(low, _dot(w["qm"].astype(BF16), w["km"].astype(BF16), 1, 1), 0.0)
                o = _dot(a.astype(BF16), iv, 1, 0)
                u = _dot(iv, _chunk_stack(w["kl"].astype(BF16), chunk_of_row), 0, 0)
                decay = jnp.exp(w["b_last"])
                st = st_ref[h]
                states = []
                for c in range(NCH):
                    sp_ref[h, c] = st
                    states.append(st.astype(BF16))
                    st = st * decay[c * CHUNK:c * CHUNK + 1] + u[:, c * HD:(c + 1) * HD]
                st_ref[h] = st
                inter = _dot(w["qb"].astype(BF16), jnp.concatenate(states, axis=0), 1, 1)
                o = o + _chunk_pick(inter, chunk_of_row)
                hs = slice(h * HD, (h + 1) * HD)
                o_ref[:, hs] = o
                gg = zg[p][:, ls]
                y_ref[:, hs] = (o * _rstd(o) * gn * (gg * _sig(gg))).astype(BF16)

    return pl.pallas_call(
        body, name="hgrn_fwd",
        out_shape=(jax.ShapeDtypeStruct((t, D), BF16), jax.ShapeDtypeStruct((t, HG_W), F32),
                   jax.ShapeDtypeStruct((4, t // CHUNK, HD, HD), F32)),
        grid=(nb,),
        in_specs=_hgrn_cols(lambda j: j) + [pl.BlockSpec((2, HG_W), lambda j: (0, 0)),
                                            pl.BlockSpec((1, HD), lambda j: (0, 0)), ANY_SPEC]
        + [ANY_SPEC] * len(deps),
        out_specs=(pl.BlockSpec((HB, HG_W), lambda j: (j, 1)),
                   pl.BlockSpec((HB, HG_W), lambda j: (j, 0)),
                   pl.BlockSpec((4, NCH, HD, HD), lambda j: (0, j, 0, 0))),
        scratch_shapes=[pltpu.VMEM((4, HD, HD), F32)],
        input_output_aliases={10: 0},
        compiler_params=_params(dimension_semantics=("arbitrary",)),
    )(*[z] * 8, hgrn_lb, onorm, ymix, *deps)


def _hgrn2_bwd(z, hgrn_lb, onorm, o_save, sprev, dymix, dza, t):
    nb = t // HB

    def body(*refs):
        zq, zf, zi, zg = refs[0:2], refs[2:4], refs[4:6], refs[6:8]
        (lb_ref, on_ref, o_ref, sp_ref, dy_ref, dqa_ref, first_ref, second_ref,
         dz_ref, dlb_ref, don_ref, dst_ref) = refs[8:]

        @pl.when(pl.program_id(0) == 0)
        def _():
            dst_ref[...] = jnp.zeros_like(dst_ref)
            dlb_ref[...] = jnp.zeros_like(dlb_ref)
            don_ref[...] = jnp.zeros_like(don_ref)

        dz_ref[:, 0:SWA_W] = dqa_ref[...]
        dz_ref[0:HB // 2, SWA_W:ZQH] = first_ref[...]
        dz_ref[HB // 2:HB, SWA_W:ZQH] = second_ref[...]
        lb_all = _lower_bound(lb_ref)
        gn = on_ref[...]
        low, upp = _blockdiag(True), _blockdiag(False)
        upp_b = upp.astype(BF16)
        low_b = low.astype(BF16)
        row = lax.broadcasted_iota(jnp.int32, (HB, HD), 0)
        chunk_of_row = row // CHUNK
        in_chunk = row % CHUNK
        for p in range(2):
            lbp = lb_all[:, 2 * HD * p:2 * HD * (p + 1)]
            sgp = _sig(zf[p][...])
            fp = lbp + (1.0 - lbp) * sgp
            bp = _chunk_sums(low_b, jnp.log(fp))
            db_pair, dkf_pair = [], []
            for e in range(2):
                h, ls, hs = 2 * p + e, slice(e * HD, (e + 1) * HD), slice((2 * p + e) * HD, (2 * p + e + 1) * HD)
                f = fp[:, ls]
                q = zq[p][:, ls]
                w = _hgrn_local(q, f, 1.0 - f, bp[:, ls])
                iv = zi[p][:, ls].astype(BF16)
                gg = zg[p][:, ls]
                o = o_ref[:, hs]
                dout = dy_ref[:, hs].astype(F32)
                sgg = _sig(gg)
                r = _rstd(o)
                oh = o * r
                dyn = dout * (gg * sgg)
                dz_ref[:, ZGH + h * HD:ZGH + (h + 1) * HD] = (
                    dout * oh * gn * (sgg * (1.0 + gg * (1.0 - sgg)))).astype(BF16)
                don_ref[...] += _rowsum8(dyn * oh)
                do = _norm_bwd(oh, r, dyn * gn).astype(BF16)
                qm, km, kl, qb = (w[n].astype(BF16) for n in ("qm", "km", "kl", "qb"))
                decay = jnp.exp(w["b_last"])
                grads_in = _dot(do, _chunk_stack(qb, chunk_of_row), 0, 0)
                dst = dst_ref[h]
                dstn, dd_rows = [None] * NCH, [None] * NCH
                for c in reversed(range(NCH)):
                    dstn[c] = dst.astype(BF16)
                    dd_rows[c] = jnp.sum(dst * sp_ref[h, c], axis=0, keepdims=True)
                    dst = dst * decay[c * CHUNK:c * CHUNK + 1] + grads_in[:, c * HD:(c + 1) * HD]
                dst_ref[h] = dst
                states = jnp.concatenate([sp_ref[h, c].astype(BF16) for c in range(NCH)], axis=0)
                dstn_all = jnp.concatenate(dstn, axis=0)
                dqb = _dot(_chunk_stack(do, chunk_of_row), states, 1, 0)
                at = jnp.where(upp, _dot(km, qm, 1, 1), 0.0)
                di = _dot(at.astype(BF16), do, 1, 0) + _chunk_pick(_dot(kl, dstn_all, 1, 1), chunk_of_row)
                dz_ref[:, ZIH + h * HD:ZIH + (h + 1) * HD] = di.astype(BF16)
                dkl = _dot(_chunk_stack(iv, chunk_of_row), dstn_all, 1, 0)
                da = jnp.where(low, _dot(do, iv, 1, 1), 0.0).astype(BF16)
                dat = jnp.where(upp, _dot(iv, do, 1, 1), 0.0).astype(BF16)
                dqm = _dot(da, km, 1, 0)
                dkm = _dot(dat, qm, 1, 0)
                b = bp[:, ls]
                e1, e2 = jnp.exp(b - w["b_mid"]), jnp.exp(w["b_mid"] - b)
                e3, e4 = jnp.exp(w["b_last"] - b), jnp.exp(b)
                dqf = dqm * e1 + dqb * e4
                dkf_pair.append(dkm * e2 + dkl * e3)
                t_qm, t_km, t_kl = dqm * w["qm"], dkm * w["km"], dkl * w["kl"]
                db = t_qm - t_km - t_kl + dqb * w["qb"]
                db_mid = jnp.sum((t_km - t_qm).reshape(NCH, CHUNK, HD), axis=1, keepdims=True)
                db_last = jnp.sum(t_kl.reshape(NCH, CHUNK, HD), axis=1, keepdims=True)
                db_last = db_last + jnp.stack(dd_rows, axis=0) * jnp.exp(
                    bp[:, ls].reshape(NCH, CHUNK, HD)[:, CHUNK - 1:CHUNK, :])
                spread = lambda v: jnp.broadcast_to(v, (NCH, CHUNK, HD)).reshape(HB, HD)
                db = (db + jnp.where(in_chunk == CHUNK // 2 - 1, spread(db_mid), 0.0)
                      + jnp.where(in_chunk == CHUNK - 1, spread(db_last), 0.0))
                db_pair.append(db)
                sq = w["sq"]
                dz_ref[:, ZQH + h * HD:ZQH + (h + 1) * HD] = (
                    dqf * (HD ** -0.5) * (sq * (1.0 + q * (1.0 - sq)))).astype(BF16)
            dlogf = _chunk_sums(upp_b, jnp.concatenate(db_pair, axis=1))
            dfv = dlogf / fp - jnp.concatenate(dkf_pair, axis=1)
            dz_ref[:, ZFH + 2 * HD * p:ZFH + 2 * HD * (p + 1)] = (dfv * (1.0 - lbp) * sgp * (1.0 - sgp)).astype(BF16)
            dlb_ref[:, 2 * HD * p:2 * HD * (p + 1)] += _rowsum8(dfv * (1.0 - sgp))

    rev = lambda j: nb - 1 - j
    return pl.pallas_call(
        body, name="hgrn_bwd",
        out_shape=(jax.ShapeDtypeStruct((t, D_IN), BF16), jax.ShapeDtypeStruct((8, HG_W), F32),
                   jax.ShapeDtypeStruct((8, HD), F32)),
        grid=(nb,),
        in_specs=_hgrn_cols(rev) + [pl.BlockSpec((2, HG_W), lambda j: (0, 0)), pl.BlockSpec((1, HD), lambda j: (0, 0)),
                                    pl.BlockSpec((HB, HG_W), lambda j: (rev(j), 0)),
                                    pl.BlockSpec((4, NCH, HD, HD), lambda j: (0, rev(j), 0, 0)),
                                    pl.BlockSpec((HB, HG_W), lambda j: (rev(j), 1)),
                                    pl.BlockSpec((HB, SWA_W), lambda j: (rev(j), 0)),
                                    pl.BlockSpec((HB // 2, 2 * KV_W), lambda j: (rev(j), 0)),
                                    pl.BlockSpec((HB // 2, 2 * KV_W), lambda j: (rev(j), 0))],
        out_specs=(pl.BlockSpec((HB, D_IN), lambda j: (rev(j), 0)), pl.BlockSpec((8, HG_W), lambda j: (0, 0)),
                   pl.BlockSpec((8, HD), lambda j: (0, 0))),
        scratch_shapes=[pltpu.VMEM((4, HD, HD), F32)],
        compiler_params=_params(dimension_semantics=("arbitrary",)),
    )(*[z] * 8, hgrn_lb, onorm, o_save, sprev, dymix, *dza)


XB = 512


def _xattn_fwd(q, k, v, wo, h, g_post, g_pre, t, dep=None):
    tb = min(XB, t)
    deps = [] if dep is None else [dep]

    def body(q_ref, k_ref, v_ref, wo_ref, h_ref, gp_ref, gn_ref, *rest):
        o_ref, y_ref, hn_ref, u_ref = rest[len(deps):]
        for hd in range(XH):
            cols = slice(XD * hd, XD * (hd + 1))
            s = _dot(q_ref[:, cols], k_ref[:, cols], 1, 1) * (XD ** -0.5)
            p = jnp.exp(s - jnp.max(s, axis=-1, keepdims=True))
            l = jnp.sum(p, axis=-1, keepdims=True)
            o_ref[:, cols] = (_dot(p.astype(BF16), v_ref[:, cols], 1, 0) * (1.0 / l)).astype(BF16)
        y, hn, u = _ep_post_pre(_dot(o_ref[...], wo_ref[...], 1, 0), h_ref[...], gp_ref[...], gn_ref[...])
        y_ref[...] = y
        hn_ref[...] = hn
        u_ref[...] = u.astype(BF16)

    row = pl.BlockSpec((tb, D), lambda i: (i, 0))
    whole = lambda a: pl.BlockSpec(a.shape, lambda i: (0,) * a.ndim, pipeline_mode=pl.Buffered(1))
    half = jax.ShapeDtypeStruct((t, D), BF16)
    return pl.pallas_call(
        body, name="xattn_fwd", out_shape=(half, half, jax.ShapeDtypeStruct((t, D), F32), half), grid=(t // tb,),
        in_specs=[row, whole(k), whole(v), whole(wo), row, whole(g_post), whole(g_pre)] + [ANY_SPEC] * len(deps),
        out_specs=(row, row, row, row), compiler_params=_params(),
    )(q, k, v, wo, h, g_post, g_pre, *deps)


def _xattn_bwd(q, k, v, do, wq, wout, dh_out, hn, y, g_post, g_pre, t):
    tb = min(XB, t)

    def body(q_ref, k_ref, v_ref, do_ref, wq_ref, wout_ref, dho_ref, hn_ref, y_ref, gp_ref, gn_ref,
             dq_ref, dk_ref, dv_ref, dh_ref, dyp_ref, dym_ref, dgn_ref, dgp_ref):
        @pl.when(pl.program_id(0) == 0)
        def _():
            dk_ref[...] = jnp.zeros_like(dk_ref)
            dv_ref[...] = jnp.zeros_like(dv_ref)
            dgn_ref[...] = jnp.zeros_like(dgn_ref)
            dgp_ref[...] = jnp.zeros_like(dgp_ref)

        for h in range(XH):
            cols = slice(XD * h, XD * (h + 1))
            qh, kh, vh, doh = q_ref[:, cols], k_ref[:, cols], v_ref[:, cols], do_ref[:, cols]
            s = _dot(qh, kh, 1, 1) * (XD ** -0.5)
            p = jnp.exp(s - jnp.max(s, axis=-1, keepdims=True))
            p = p * (1.0 / jnp.sum(p, axis=-1, keepdims=True))
            dp = _dot(doh, vh, 1, 1)
            ds = (p * (dp - jnp.sum(p * dp, axis=-1, keepdims=True)) * (XD ** -0.5)).astype(BF16)
            dq_ref[:, cols] = _dot(ds, kh, 1, 0).astype(BF16)
            dk_ref[:, cols] += _dot(ds, qh, 0, 0)
            dv_ref[:, cols] += _dot(p.astype(BF16), doh, 0, 0)
        du = _dot(dq_ref[...], wq_ref[...], 1, 1)
        dh, dyp, dgn, dgp = _ep_post_pre_bwd(du, dho_ref[...], hn_ref[...], y_ref[...], gp_ref[...], gn_ref[...])
        dh_ref[...] = dh
        dyp = dyp.astype(BF16)
        dyp_ref[...] = dyp
        dym_ref[...] = _dot(dyp, wout_ref[...], 1, 1).astype(BF16)
        dgn_ref[...] += dgn
        dgp_ref[...] += dgp

    row = pl.BlockSpec((tb, D), lambda i: (i, 0))
    mem = pl.BlockSpec(k.shape, lambda i: (0, 0))
    whole = lambda a: pl.BlockSpec(a.shape, lambda i: (0,) * a.ndim, pipeline_mode=pl.Buffered(1))
    acc = pl.BlockSpec((8, D), lambda i: (0, 0))
    half = jax.ShapeDtypeStruct((t, D), BF16)
    return pl.pallas_call(
        body, name="xattn_bwd",
        out_shape=(half, jax.ShapeDtypeStruct(k.shape, F32), jax.ShapeDtypeStruct(k.shape, F32),
                   jax.ShapeDtypeStruct((t, D), F32), half, half,
                   jax.ShapeDtypeStruct((8, D), F32), jax.ShapeDtypeStruct((8, D), F32)),
        grid=(t // tb,),
        in_specs=[row, whole(k), whole(v), row, whole(wq), whole(wout), row, row, row, whole(g_post), whole(g_pre)],
        out_specs=(row, mem, mem, row, row, row, acc, acc),
        compiler_params=_params(dimension_semantics=("arbitrary",)),
    )(q, k, v, do, wq, wout, dh_out, hn, y, g_post, g_pre)


def _mem_kv(mem, g_mem, wk, wv):
    def body(m_ref, g_ref, wk_ref, wv_ref, mn_ref, k_ref, v_ref):
        m_ = m_ref[...]
        mn = (m_ * _rstd(m_) * g_ref[...]).astype(BF16)
        mn_ref[...] = mn
        k_ref[...] = _dot(mn, wk_ref[...], 1, 0).astype(BF16)
        v_ref[...] = _dot(mn, wv_ref[...], 1, 0).astype(BF16)

    return pl.pallas_call(body, name="mem_kv", out_shape=(jax.ShapeDtypeStruct(mem.shape, BF16),) * 3,
                          compiler_params=_params())(mem, g_mem, wk, wv)


def _mem_kv_bwd(mn, mem, dk, dv, wk, wv, dep=None):
    deps = [] if dep is None else [dep]

    def body(mn_ref, m_ref, dk_ref, dv_ref, wk_ref, wv_ref, *rest):
        gk_ref, gv_ref, dg_ref = rest[len(deps):]
        mn = mn_ref[...]
        dkb, dvb = dk_ref[...].astype(BF16), dv_ref[...].astype(BF16)
        gk_ref[...] = _dot(mn, dkb, 0, 0).astype(BF16)
        gv_ref[...] = _dot(mn, dvb, 0, 0).astype(BF16)
        dmn = _dot(dkb, wk_ref[...], 1, 1) + _dot(dvb, wv_ref[...], 1, 1)
        m_ = m_ref[...]
        dg_ref[...] = _rowsum8(dmn * (m_ * _rstd(m_)))

    vmem = pl.BlockSpec(memory_space=pltpu.VMEM)
    return pl.pallas_call(
        body, name="mem_kv_bwd",
        out_shape=(jax.ShapeDtypeStruct(wk.shape, BF16), jax.ShapeDtypeStruct(wv.shape, BF16),
                   jax.ShapeDtypeStruct((8, D), F32)),
        in_specs=[vmem] * 6 + [ANY_SPEC] * len(deps), out_specs=(vmem,) * 3, compiler_params=_params(),
    )(mn, mem, dk, dv, wk, wv, *deps)


FB = 256


def _ffn_fwd_bwd(u, wgt, wut, wd, h, target, g_last, y_prev, g_post, g_pre, wo, t):
    tb = min(FB, t)

    def body(u_ref, wg_ref, wu_ref, wd_ref, h_ref, t_ref, gl_ref, yp_ref, gp_ref, gn_ref, wo_ref,
             a_ref, dy_ref, dg_ref, dup_ref, dh_ref, dyp_ref, do_ref, sq_ref, dgl_ref, dgn_ref, dgp_ref):
        @pl.when(pl.program_id(0) == 0)
        def _():
            for ref in (sq_ref, dgl_ref, dgn_ref, dgp_ref):
                ref[...] = jnp.zeros_like(ref)

        u_ = u_ref[...]
        g = _dot(u_, wg_ref[...], 1, 1)
        up = _dot(u_, wu_ref[...], 1, 1)
        sg = _sig(g)
        a = (g * sg * up).astype(BF16)
        a_ref[...] = a
        h_ = h_ref[...]
        sq, dh3, dy, dgl = _ep_final_loss(_dot(a, wd_ref[...], 1, 0), h_, t_ref[...], gl_ref[...])
        sq_ref[...] += sq
        dgl_ref[...] += dgl
        dy = dy.astype(BF16)
        dy_ref[...] = dy
        da = _dot(dy, wd_ref[...], 1, 1)
        dup = (da * g * sg).astype(BF16)
        dgate = (da * up * (sg * (1.0 + g * (1.0 - sg)))).astype(BF16)
        dup_ref[...] = dup
        dg_ref[...] = dgate
        du = _dot(dgate, wg_ref[...], 1, 0) + _dot(dup, wu_ref[...], 1, 0)
        dh, dyp, dgn, dgp = _ep_post_pre_bwd(du, dh3, h_, yp_ref[...], gp_ref[...], gn_ref[...])
        dh_ref[...] = dh
        dyp = dyp.astype(BF16)
        dyp_ref[...] = dyp
        do_ref[...] = _dot(dyp, wo_ref[...], 1, 1).astype(BF16)
        dgn_ref[...] += dgn
        dgp_ref[...] += dgp

    row = lambda w: pl.BlockSpec((tb, w), lambda i: (i, 0))
    whole = lambda a: pl.BlockSpec(a.shape, lambda i: (0,) * a.ndim, pipeline_mode=pl.Buffered(1))
    acc = pl.BlockSpec((8, D), lambda i: (0, 0))
    wide, half, sums = (jax.ShapeDtypeStruct((t, D_FF), BF16), jax.ShapeDtypeStruct((t, D), BF16),
                        jax.ShapeDtypeStruct((8, D), F32))
    return pl.pallas_call(
        body, name="ffn_fwd_bwd",
        out_shape=(wide, half, wide, wide, jax.ShapeDtypeStruct((t, D), F32), half, half, sums, sums, sums, sums),
        grid=(t // tb,),
        in_specs=[row(D), whole(wgt), whole(wut), whole(wd), row(D), row(D), whole(g_last), row(D), whole(g_post),
                  whole(g_pre), whole(wo)],
        out_specs=(row(D_FF), row(D), row(D_FF), row(D_FF), row(D), row(D), row(D), acc, acc, acc, acc),
        compiler_params=_params(dimension_semantics=("arbitrary",)),
    )(u, wgt, wut, wd, h, target, g_last, y_prev, g_post, g_pre, wo)


def _local_step(x, mem, target, fetch, sm, emit=None, first_dep=None, milestone=None):
    t = x.shape[0]
    w, gw = {}, {}

    def out(key, g):
        gw[key] = g
        return None if emit is None else emit(key, g)

    def tell(tag, value):
        return None if milestone is None else milestone(tag, value)
    u1 = _prenorm(x, sm["g_mix_pre"], name="prenorm_mix", dep=first_dep)
    w["winT"] = fetch("winT", u1)
    z = _mm(u1, w["winT"], tb=True, out_dtype=F32, tm=1024, tn=1408, name="mm_z", n_outer=True)
    ymix, lse = _swa_fwd(z, sm["sinks"], t, dep=tell("z", z))
    ymix, o_h, sprev = _hgrn2_fwd(z, sm["hgrn_lb"], sm["hgrn_onorm"], ymix, t, dep=tell("swa", lse))
    for key in ("wout", "wq", "wk", "wv", "wo"):
        w[key] = fetch(key, ymix)
    y1, h1, u2, qx = _mm_rows([(ymix, w["wout"], False)], [x], [sm["g_mix_post"], sm["g_x_pre"], w["wq"]],
                              _then(_ep_post_pre, 2, False), _EP_POST_PRE_OUTS + [ROW_BF16], tm=1024,
                              name="mm_y1_post_qx")
    mn, kx, vx = _mem_kv(mem, sm["g_mem"], w["wk"], w["wv"])
    ox, y2, h2, u3 = _xattn_fwd(qx, kx, vx, w["wo"], h1, sm["g_x_post"], sm["g_ffn_pre"], t, dep=tell("kv", kx))
    for key in ("wgT", "wuT", "wd"):
        w[key] = fetch(key, u3)
    act, dy3, dgate, dup, dh2, dy2, dox, sq, dg_ffn_post, dg_ffn_pre, dg_x_post = _ffn_fwd_bwd(
        u3, w["wgT"], w["wuT"], w["wd"], h2, target, sm["g_ffn_post"], y2, sm["g_x_post"], sm["g_ffn_pre"], w["wo"], t)
    dep = out("wd", _mm(act, dy3, ta=True, out_dtype=BF16, tm=1408, tn=1024, name="mm_gwd"))
    dep = out("wgT", _mm(dgate, u3, ta=True, out_dtype=BF16, tm=1408, tn=1024, name="mm_gwg", dep=dep))
    dep = out("wuT", _mm(dup, u3, ta=True, out_dtype=BF16, tm=1408, tn=1024, name="mm_gwu", dep=dep))
    out("wo", _mm(ox, dy2, ta=True, out_dtype=BF16, tm=512, tn=1024, name="mm_gwo", dep=dep))
    dqx, dkx, dvx, dh1, dy1, dymix, dg_x_pre, dg_mix_post = _xattn_bwd(
        qx, kx, vx, dox, w["wq"], w["wout"], dh2, h1, y1, sm["g_mix_post"], sm["g_x_pre"], t)
    out("wq", _mm(u2, dqx, ta=True, out_dtype=BF16, tm=512, tn=1024, name="mm_gwq"))
    gwk, gwv, dg_mem = _mem_kv_bwd(mn, mem, dkx, dvx, w["wk"], w["wv"])
    out("wk", gwk)
    dep = out("wv", gwv)
    dep = out("wout", _mm(ymix, dy1, ta=True, out_dtype=BF16, tm=512, tn=1024, name="mm_gwout", dep=dep))
    *dza, dsinks = _swa_bwd(z, sm["sinks"], ymix, lse, dymix, t, dep=dep)
    dz, dlb, donorm = _hgrn2_bwd(z, sm["hgrn_lb"], sm["hgrn_onorm"], o_h, sprev, dymix, dza, t)
    dep = out("winT", _mm(dz, u1, ta=True, out_dtype=BF16, tm=1408, tn=1024, name="mm_gwin"))
    grad_x, dg_mix_pre = _mm_rows([(dz, w["winT"], False)], [dh1, x], [sm["g_mix_pre"]], _ep_pre_bwd,
                                  _EP_PRE_BWD_OUTS, tm=512, name="mm_du1_pre_bwd", dep=dep)
    parts = dict(g_mix_pre=dg_mix_pre, g_mix_post=dg_mix_post, g_mem=dg_mem, g_x_pre=dg_x_pre,
                 g_x_post=dg_x_post, g_ffn_pre=dg_ffn_pre, g_ffn_post=dg_ffn_post,
                 hgrn_onorm=donorm, hgrn_lb=dlb, sinks=dsinks, sq=sq)
    return grad_x, gw, parts


def _position():
    return lax.axis_index("x"), lax.axis_index("y"), lax.axis_index("c")


def _peer(pos, k):
    x, y, c = pos
    return (1 - x if k & 4 else x, 1 - y if k & 2 else y, 1 - c if k & 1 else c)


def _linear(pos):
    x, y, c = pos
    return 4 * x + 2 * y + c


HBM_SPEC = pl.BlockSpec(memory_space=pltpu.HBM)
SEM_SPEC = pl.BlockSpec(memory_space=pltpu.SEMAPHORE)
DATAFLOW = pltpu.SideEffectType.DATAFLOW_SIDE_EFFECTING
SEND_ORDER = (1, 2, 4, 3, 5, 6, 7)


def _in_hbm(a):
    return pltpu.with_memory_space_constraint(a, pltpu.HBM)


def _prepare_weights(shards, *, name, dep=None):
    n = len(shards)
    deps = [] if dep is None else [dep]

    def body(*refs):
        ins, (outs, lands, sem) = refs[:n], (refs[-2 * n - 1:-n - 1], refs[-n - 1:-1], refs[-1])
        me_lin = _linear(_position())
        copies = []
        for a in range(n):
            r = ins[a].shape[0]
            outs[a][...] = ins[a][...].astype(BF16)
            copies.append(pltpu.make_async_copy(outs[a], lands[a].at[pl.ds(me_lin * r, r), :], sem.at[a]))
            copies[-1].start()
        for cp in copies:
            cp.wait()

    vmem = pl.BlockSpec(memory_space=pltpu.VMEM)
    res = pl.pallas_call(
        body, name=name,
        out_shape=tuple(jax.ShapeDtypeStruct(s.shape, BF16) for s in shards)
        + tuple(jax.ShapeDtypeStruct((N_DEV * s.shape[0], s.shape[1]), BF16) for s in shards),
        in_specs=[vmem] * n + [ANY_SPEC] * len(deps), out_specs=tuple([vmem] * n + [ANY_SPEC] * n),
        scratch_shapes=[pltpu.SemaphoreType.DMA((n,))], compiler_params=_params(),
    )(*shards, *deps)
    return res[:n], res[n:]


def _copies_start(arrays, plan, n, *, name):
    na = len(arrays)

    def body(*refs):
        ins, send_sems, recv_sems = refs[:na], refs[na], refs[na + 1]
        me = _position()
        for j in range(n):
            src, dst, peer, _ = plan(ins, me, j)
            pltpu.make_async_remote_copy(src_ref=src, dst_ref=dst, send_sem=send_sems.at[j], recv_sem=recv_sems.at[j],
                                         device_id=peer, device_id_type=MESH).start()

    return pl.pallas_call(
        body, name=name,
        out_shape=(pltpu.SemaphoreType.DMA((n,)), pltpu.SemaphoreType.DMA((n,)))
        + tuple(pltpu.HBM(a.shape, a.dtype) for a in arrays),
        in_specs=(HBM_SPEC,) * na, out_specs=(SEM_SPEC, SEM_SPEC) + (HBM_SPEC,) * na,
        input_output_aliases={i: 2 + i for i in range(na)},
        compiler_params=pltpu.CompilerParams(has_side_effects=DATAFLOW),
    )(*[_in_hbm(a) for a in arrays])


def _copies_wait(send_sems, recv_sems, arrays, plan, n, after, *, name):
    na = len(arrays)

    def body(*refs):
        ins, send_sems, recv_sems = refs[:na], refs[na], refs[na + 1]
        me = _position()
        for j in range(n):
            src, _, peer, landed = plan(ins, me, j)
            copy = pltpu.make_async_remote_copy(src_ref=src, dst_ref=landed, send_sem=send_sems.at[j],
                                                recv_sem=recv_sems.at[j], device_id=peer, device_id_type=MESH)
            copy.wait_send()
            copy.wait_recv()

    return pl.pallas_call(
        body, name=name, out_shape=tuple(pltpu.HBM(a.shape, a.dtype) for a in arrays),
        in_specs=(HBM_SPEC,) * na + (SEM_SPEC, SEM_SPEC, ANY_SPEC), out_specs=(HBM_SPEC,) * na,
        input_output_aliases={i: i for i in range(na)},
        compiler_params=pltpu.CompilerParams(has_side_effects=DATAFLOW),
    )(*arrays, send_sems, recv_sems, after)


SAME_CORE = (2, 4, 6)


class _TwoLevelGather:
    def __init__(self, shards, lands, *, name):
        n = self.n = len(shards)
        self.name = name
        first_peers = (1,) + SAME_CORE

        def rows(ref, pos):
            r = ref.shape[0] // N_DEV
            return ref.at[pl.ds(_linear(pos) * r, r), :]

        def first(refs, me, j):
            a, peer = j // 4, _peer(me, first_peers[j % 4])
            return refs[a], rows(refs[n + a], me), peer, rows(refs[n + a], peer)

        def second(refs, me, j):
            a, sibling = j // 3, _peer(me, 1)
            mine = rows(refs[a], _peer(me, SAME_CORE[j % 3]))
            return mine, mine, sibling, rows(refs[a], _peer(sibling, SAME_CORE[j % 3]))

        self._first, self._second = first, second
        self._flight = _copies_start(list(shards) + list(lands), first, 4 * n, name=name + "_send")
        self.dep = self._flight[2]

    def pass_on(self, after):
        send1, recv1, *arrays = self._flight
        arrays = _copies_wait(send1, recv1, arrays, self._first, 4 * self.n, after, name=self.name + "_recv")
        self._flight = _copies_start(list(arrays[self.n:]), self._second, 3 * self.n, name=self.name + "_pass")
        return self._flight[2]

    def finish(self, after):
        send2, recv2, *lands = self._flight
        return _copies_wait(send2, recv2, lands, self._second, 3 * self.n, after, name=self.name + "_pass_recv")


def _exchange_start(gs, *, name):
    n = len(gs)
    rows = [g.shape[0] // N_DEV for g in gs]
    lands = [lax.empty((N_DEV - 1, r, g.shape[1]), g.dtype) for g, r in zip(gs, rows)]

    def body(*refs):
        g_refs, land_refs = refs[:n], refs[n:2 * n]
        send_sems, recv_sems = refs[2 * n:3 * n], refs[3 * n:4 * n]
        me = _position()
        for a in range(n):
            for k in SEND_ORDER:
                peer = _peer(me, k)
                pltpu.make_async_remote_copy(
                    src_ref=g_refs[a].at[pl.ds(_linear(peer) * rows[a], rows[a]), :],
                    dst_ref=land_refs[a].at[k - 1],
                    send_sem=send_sems[a].at[k - 1], recv_sem=recv_sems[a].at[k - 1],
                    device_id=peer, device_id_type=MESH).start()

    res = pl.pallas_call(
        body, name=name,
        out_shape=tuple(pltpu.SemaphoreType.DMA((N_DEV - 1,)) for _ in range(2 * n))
        + tuple(pltpu.HBM(a.shape, a.dtype) for a in gs + lands),
        in_specs=(HBM_SPEC,) * (2 * n), out_specs=(SEM_SPEC,) * (2 * n) + (HBM_SPEC,) * (2 * n),
        input_output_aliases={i: 2 * n + i for i in range(2 * n)},
        compiler_params=pltpu.CompilerParams(has_side_effects=DATAFLOW),
    )(*[_in_hbm(a) for a in gs + lands])
    return [(res[a], res[n + a], res[2 * n + a], res[3 * n + a]) for a in range(n)]


def _exchange_wait(send_sems, recv_sems, g_thru, land_thru, after, *, name):
    r = land_thru.shape[1]

    def body(g_ref, land_ref, send_sems, recv_sems, after_ref, g_dead, got_ref):
        del after_ref, g_dead, got_ref
        me = _position()
        for k in SEND_ORDER:
            peer = _peer(me, k)
            copy = pltpu.make_async_remote_copy(
                src_ref=g_ref.at[pl.ds(_linear(peer) * r, r), :], dst_ref=land_ref.at[k - 1],
                send_sem=send_sems.at[k - 1], recv_sem=recv_sems.at[k - 1],
                device_id=peer, device_id_type=MESH)
            copy.wait_send()
            copy.wait_recv()

    return pl.pallas_call(
        body, name=name,
        out_shape=(pltpu.HBM(g_thru.shape, g_thru.dtype), pltpu.HBM(land_thru.shape, land_thru.dtype)),
        in_specs=(HBM_SPEC, HBM_SPEC, SEM_SPEC, SEM_SPEC, pl.BlockSpec(memory_space=pl.ANY)),
        out_specs=(HBM_SPEC, HBM_SPEC), input_output_aliases={0: 0, 1: 1},
        compiler_params=pltpu.CompilerParams(has_side_effects=DATAFLOW),
    )(g_thru, land_thru, send_sems, recv_sems, after)


ADAMW_TILE_ROWS = 256


def _adamw_math(w, g, m, v):
    m = B1 * m + (1.0 - B1) * g
    v = B2 * v + (1.0 - B2) * (g * g)
    delta = -LR * ((m / C1) / (jnp.sqrt(v / C2) + AEPS) + WD * w)
    return delta, m, v


def _sum_adamw(items, *, name):
    n = len(items)
    r, d = items[0][2].shape
    assert all(it[2].shape == (r, d) for it in items)
    rc = r // 2 if r > ADAMW_TILE_ROWS else r
    tiles = [(a, r0) for a in range(n) for r0 in range(0, r, rc)]
    n_in, n_out = 5, 4

    def body(*refs):
        ins, outs = refs[:n_in * n], refs[n_in * n:(n_in + n_out) * n]
        land_v, own_v, f32_v, sems = refs[(n_in + n_out) * n:]
        me_lin = _linear(_position())

        def loads(j):
            a, r0 = tiles[j]
            g_all, land, w, m, v = ins[n_in * a:n_in * a + n_in]
            rows = pl.ds(r0, rc)
            pairs = [(land.at[:, rows, :], land_v.at[j]), (g_all.at[pl.ds(me_lin * r + r0, rc), :], own_v.at[j]),
                     (w.at[rows, :], f32_v.at[j, 0]), (m.at[rows, :], f32_v.at[j, 1]), (v.at[rows, :], f32_v.at[j, 2])]
            return [pltpu.make_async_copy(src, dst, sems.at[j, i]) for i, (src, dst) in enumerate(pairs)]

        def stores(j):
            a, r0 = tiles[j]
            return [pltpu.make_async_copy(f32_v.at[j, 3 + i], outs[n_out * a + i].at[pl.ds(r0, rc), :],
                                          sems.at[j, n_in + i]) for i in range(n_out)]

        for j in range(len(tiles)):
            for cp in loads(j):
                cp.start()
        for j in range(len(tiles)):
            for cp in loads(j):
                cp.wait()
            g = land_v[j, 0].astype(F32)
            for s in range(1, N_DEV - 1):
                g = g + land_v[j, s].astype(F32)
            g = own_v[j].astype(F32) + g
            f32_v[j, 3] = g
            f32_v[j, 4], f32_v[j, 5], f32_v[j, 6] = _adamw_math(f32_v[j, 0], g, f32_v[j, 1], f32_v[j, 2])
            for cp in stores(j):
                cp.start()
        for j in range(len(tiles)):
            for cp in stores(j):
                cp.wait()

    nt = len(tiles)
    res = pl.pallas_call(
        body, name=name,
        out_shape=tuple(jax.ShapeDtypeStruct((r, d), F32) for _ in range(n_out * n)),
        in_specs=[ANY_SPEC] * (n_in * n), out_specs=(ANY_SPEC,) * (n_out * n),
        scratch_shapes=[pltpu.VMEM((nt, N_DEV - 1, rc, d), BF16), pltpu.VMEM((nt, rc, d), BF16),
                        pltpu.VMEM((nt, 3 + n_out, rc, d), F32), pltpu.SemaphoreType.DMA((nt, n_in + n_out))],
        compiler_params=_params(),
    )(*[a for it in items for a in it])
    return [res[n_out * a:n_out * a + n_out] for a in range(n)]


SMALL = ("g_mix_pre", "g_mix_post", "g_mem", "g_x_pre", "g_x_post", "g_ffn_pre", "g_ffn_post",
         "hgrn_onorm", "hgrn_lb", "sinks")
SMALL_W = dict(hgrn_onorm=HD, hgrn_lb=HG_W, sinks=8)
SQ_ROW = len(SMALL)
PACK_ROWS = 16


def _small_pack(parts):
    ns = len(SMALL)

    def body(*refs):
        part, mine, slots, sem = refs[:ns + 1], refs[ns + 1], refs[ns + 2], refs[ns + 3]
        mine[...] = jnp.zeros((PACK_ROWS, D), F32)
        for r, name in enumerate(SMALL):
            wd = SMALL_W.get(name, D)
            mine[r:r + 1, 0:wd] = jnp.sum(part[r][...], axis=0, keepdims=True)[:, 0:wd]
        sq = jnp.sum(part[ns][...]) * (0.5 / D)
        mine[SQ_ROW:SQ_ROW + 1, :] = jnp.full((1, D), sq, F32)
        own = pltpu.make_async_copy(mine, slots.at[_linear(_position())], sem)
        own.start()
        own.wait()

    vmem = pl.BlockSpec(memory_space=pltpu.VMEM)
    return pl.pallas_call(
        body, name="small_pack",
        out_shape=(jax.ShapeDtypeStruct((PACK_ROWS, D), F32), jax.ShapeDtypeStruct((N_DEV, PACK_ROWS, D), F32)),
        in_specs=[vmem] * (ns + 1), out_specs=(vmem, ANY_SPEC),
        scratch_shapes=[pltpu.SemaphoreType.DMA(())], compiler_params=_params(),
    )(*[parts[n] for n in SMALL], parts["sq"])


def _small_exchange(mine, slots):
    def plan(refs, me, j):
        peer = _peer(me, j + 1)
        return refs[0], refs[1].at[_linear(me)], peer, refs[1].at[_linear(peer)]

    send, recv, mine1, slots1 = _copies_start([mine, slots], plan, N_DEV - 1, name="small_send")
    return lambda after: _copies_wait(send, recv, [mine1, slots1], plan, N_DEV - 1, after, name="small_recv")[1]


def _small_update(slots, sm, m_sm, v_sm):
    ns = len(SMALL)

    def body(*refs):
        tot = refs[0][0]
        for s in range(1, N_DEV):
            tot = tot + refs[0][s]
        w_refs, m_refs, v_refs = refs[1:ns + 1], refs[ns + 1:2 * ns + 1], refs[2 * ns + 1:3 * ns + 1]
        outs = refs[3 * ns + 1:]
        loss_ref = outs[0]
        g_out, d_out = outs[1:ns + 1], outs[ns + 1:2 * ns + 1]
        nm_out, nv_out = outs[2 * ns + 1:3 * ns + 1], outs[3 * ns + 1:4 * ns + 1]
        loss_ref[...] = tot[SQ_ROW:SQ_ROW + 1, 0:1]
        for r, name in enumerate(SMALL):
            wd = SMALL_W.get(name, D)
            g = tot[r:r + 1, 0:wd]
            w = w_refs[r][...]
            if name == "hgrn_lb":
                mx = jnp.maximum(w[0:1], w[1:2])
                e0, e1 = jnp.exp(w[0:1] - mx), jnp.exp(w[1:2] - mx)
                lb0 = e0 / (e0 + e1)
                g0 = g * lb0 * (1.0 - lb0)
                for i, gi in enumerate((g0, -g0)):
                    d, nm, nv = _adamw_math(w[i:i + 1], gi, m_refs[r][i:i + 1, :], v_refs[r][i:i + 1, :])
                    g_out[r][i:i + 1, :] = gi
                    d_out[r][i:i + 1, :], nm_out[r][i:i + 1, :], nv_out[r][i:i + 1, :] = d, nm, nv
            else:
                d, nm, nv = _adamw_math(w, g, m_refs[r][...], v_refs[r][...])
                g_out[r][...] = g
                d_out[r][...], nm_out[r][...], nv_out[r][...] = d, nm, nv

    shapes = [jax.ShapeDtypeStruct(sm[n].shape, F32) for n in SMALL]
    res = pl.pallas_call(
        body, name="small_update", out_shape=tuple([jax.ShapeDtypeStruct((1, 1), F32)] + shapes * 4),
        compiler_params=_params(),
    )(slots, *[sm[n] for n in SMALL], *[m_sm[n] for n in SMALL], *[v_sm[n] for n in SMALL])
    groups = [dict(zip(SMALL, res[1 + i * ns:1 + (i + 1) * ns])) for i in range(4)]
    return res[0], groups[0], groups[1], groups[2], groups[3]


BIG = ("w_in", "w_gate", "w_up", "w_down", "w_out", "wq_x", "wk_x", "wv_x", "wo_x")
BIG_KEY = dict(w_in="winT", w_gate="wgT", w_up="wuT", w_down="wd", w_out="wout", wq_x="wq", wk_x="wk",
               wv_x="wv", wo_x="wo")
TRANSPOSED = ("w_in", "w_gate", "w_up")
WEIGHTS = ("w_in", "sinks", "hgrn_lb", "hgrn_onorm", "w_out", "g_mix_pre", "g_mix_post", "g_mem", "g_x_pre",
           "g_x_post", "wq_x", "wk_x", "wv_x", "wo_x", "g_ffn_pre", "g_ffn_post", "w_gate", "w_up", "w_down")


def kernel(x, mem, w_in, sinks, hgrn_lb, hgrn_onorm, w_out, g_mix_pre, g_mix_post, g_mem, g_x_pre, g_x_post, wq_x, wk_x, wv_x, wo_x, g_ffn_pre, g_ffn_post, w_gate, w_up, w_down, loss_target, m_w_in, m_sinks, m_hgrn_lb, m_hgrn_onorm, m_w_out, m_g_mix_pre, m_g_mix_post, m_g_mem, m_g_x_pre, m_g_x_post, m_wq_x, m_wk_x, m_wv_x, m_wo_x, m_g_ffn_pre, m_g_ffn_post, m_w_gate, m_w_up, m_w_down, v_w_in, v_sinks, v_hgrn_lb, v_hgrn_onorm, v_w_out, v_g_mix_pre, v_g_mix_post, v_g_mem, v_g_x_pre, v_g_x_post, v_wq_x, v_wk_x, v_wv_x, v_wo_x, v_g_ffn_pre, v_g_ffn_post, v_w_gate, v_w_up, v_w_down):
    given = dict(locals())
    wts = {n: given[n] for n in WEIGHTS}
    ms = {n: given["m_" + n] for n in WEIGHTS}
    vs = {n: given["v_" + n] for n in WEIGHTS}

    def mat(a, name):
        a = a[0]
        return a.T if name in TRANSPOSED else a

    groups = (("w_in",), ("w_out", "wq_x", "wk_x", "wv_x", "wo_x"), ("w_gate", "w_up", "w_down"))
    gathers = []

    def start_group(g, dep):
        tag = ("w_in", "w_attn", "w_ffn")[g]
        shards, lands = _prepare_weights([mat(wts[n], n) for n in groups[g]], name="prepare_" + tag, dep=dep)
        gathers.append(_TwoLevelGather(shards, lands, name=tag))
        return gathers[-1].dep

    first_dep = start_group(1, start_group(0, None))
    name_of = {k: n for n, k in BIG_KEY.items()}
    gathered = {}

    def milestone(tag, value):
        if tag == "z":
            return start_group(2, value)
        return gathers[{"swa": 1, "kv": 2}[tag]].pass_on(value)

    def fetch(key, after):
        name = name_of[key]
        if name not in gathered:
            g = [i for i, group in enumerate(groups) if name in group][0]
            if g == 0:
                gathers[0].pass_on(after)
            gathered.update(zip(groups[g], gathers[g].finish(after)))
        return gathered[name]

    sm = {n: wts[n] for n in SMALL}
    started, held = {}, {}
    send_with = {k: group for group in (("wgT", "wuT"), ("wo", "wq", "wk", "wv")) for k in group}

    def emit(key, g):
        held[key] = g
        group = send_with.get(key, (key,))
        if key != group[-1]:
            return None
        flights = _exchange_start([held[k] for k in group], name="grad_send_" + name_of[group[0]])
        started.update({name_of[k]: f for k, f in zip(group, flights)})
        return flights[-1][2]

    grad_x, _, parts = _local_step(x[0], mem[0], loss_target[0], fetch, sm, emit, first_dep=first_dep, milestone=milestone)
    small_finish = _small_exchange(*_small_pack(parts))
    grads, deltas, new_m, new_v = {}, {}, {}, {}
    after = grad_x
    for group in (("w_down",), ("w_gate", "w_up"), ("wo_x", "wq_x", "wk_x", "wv_x", "w_out"), ("w_in",)):
        items = []
        for n in group:
            g_all, land = _exchange_wait(*started[n], after, name="grad_recv_" + n)
            items.append((g_all, land, mat(wts[n], n), mat(ms[n], n), mat(vs[n], n)))
            after = land
        for n, res in zip(group, _sum_adamw(items, name="adamw_" + group[0])):
            after = res[1]
            if n in TRANSPOSED:
                res = [a.T for a in res]
            grads[n], deltas[n], new_m[n], new_v[n] = [a[None] for a in res]
    loss, g_s, d_s, m_s, v_s = _small_update(small_finish(after), sm, {n: ms[n] for n in SMALL},
                                             {n: vs[n] for n in SMALL})
    grads.update(g_s), deltas.update(d_s), new_m.update(m_s), new_v.update(v_s)
    return (loss[0, 0], grad_x[None], *[grads[n] for n in WEIGHTS], *[deltas[n] for n in WEIGHTS],
            *[new_m[n] for n in WEIGHTS], *[new_v[n] for n in WEIGHTS])
```

```python
import functools

import jax
import jax.numpy as jnp
from jax import lax
from jax.experimental import pallas as pl
from jax.experimental.pallas import tpu as pltpu

F32 = jnp.float32
BF16 = jnp.bfloat16

D = 1024
D_IN = 2816
D_FF = 2816
CHUNK = 64
SWA_W = 512
KV_W = 128
HG_W = 512
HD = 128
ZQH, ZFH, ZIH, ZGH = 768, 1280, 1792, 2304
XH, XD = 4, 256
EPS = 1e-6
NEG = -1e30
N_DEV = 8
MESH = pl.DeviceIdType.MESH

LR, B1, B2, AEPS, WD, STEP = 0.001, 0.9, 0.999, 1e-08, 0.01, 10
C1 = 1.0 - B1 ** STEP
C2 = 1.0 - B2 ** STEP

VMEM_LIMIT = 56 * 1024 * 1024


def _params(**kw):
    return pltpu.CompilerParams(vmem_limit_bytes=VMEM_LIMIT, **kw)


def _sig(x):
    return 1.0 / (1.0 + jnp.exp(-x))


def _rowsum8(x):
    r, w = x.shape
    return jnp.sum(x.reshape(r // 8, 8, w), axis=0)


def _dot(a, b, ca, cb, precision=None):
    return lax.dot_general(a, b, (((ca,), (cb,)), ((), ())), preferred_element_type=F32,
                           precision=precision)


ANY_SPEC = pl.BlockSpec(memory_space=pl.ANY)


def _mm(a, b, *, ta=False, tb=False, out_dtype, tm, tn, tk=None, name, dep=None, n_outer=False):
    m = a.shape[1] if ta else a.shape[0]
    k = a.shape[0] if ta else a.shape[1]
    n = b.shape[0] if tb else b.shape[1]
    tm, tn = min(tm, m), min(tn, n)
    tk = k if tk is None else min(tk, k)
    nk = k // tk
    assert m % tm == 0 and n % tn == 0 and k % tk == 0, (name, m, n, k, tm, tn, tk)
    ij = (lambda g0, g1: (g1, g0)) if n_outer else (lambda g0, g1: (g0, g1))
    a_spec = (pl.BlockSpec((tk, tm), lambda g0, g1, kk: (kk, ij(g0, g1)[0])) if ta
              else pl.BlockSpec((tm, tk), lambda g0, g1, kk: (ij(g0, g1)[0], kk)))
    b_spec = (pl.BlockSpec((tn, tk), lambda g0, g1, kk: (ij(g0, g1)[1], kk)) if tb
              else pl.BlockSpec((tk, tn), lambda g0, g1, kk: (kk, ij(g0, g1)[1])))
    ca, cb = (0 if ta else 1), (1 if tb else 0)

    deps = [] if dep is None else [dep]

    def body(a_ref, b_ref, *rest):
        o_ref, acc = rest[len(deps)], rest[len(deps) + 1:]
        p = _dot(a_ref[...].astype(BF16), b_ref[...].astype(BF16), ca, cb)
        if nk == 1:
            o_ref[...] = p.astype(out_dtype)
        else:
            acc_ref, = acc
            kk = pl.program_id(2)

            @pl.when(kk == 0)
            def _():
                acc_ref[...] = p

            @pl.when(kk > 0)
            def _():
                acc_ref[...] += p

            @pl.when(kk == nk - 1)
            def _():
                o_ref[...] = acc_ref[...].astype(out_dtype)

    return pl.pallas_call(
        body, name=name, out_shape=jax.ShapeDtypeStruct((m, n), out_dtype),
        grid=(n // tn, m // tm, nk) if n_outer else (m // tm, n // tn, nk),
        in_specs=[a_spec, b_spec] + [ANY_SPEC] * len(deps),
        out_specs=pl.BlockSpec((tm, tn), lambda g0, g1, kk: ij(g0, g1)),
        scratch_shapes=[pltpu.VMEM((tm, tn), F32)] if nk > 1 else [],
        compiler_params=_params(dimension_semantics=("parallel", "parallel", "arbitrary")),
    )(a, b, *deps)


TN_FIRST = 256
TN_REST = 1152


def _mm_tn(a, b, *, name, dep=None):
    (k, m), n = a.shape, b.shape[1]
    assert a.dtype == BF16 and b.dtype == BF16 and b.shape[0] == k
    widths = [TN_FIRST, TN_FIRST]
    while sum(widths) < m:
        widths.append(min(TN_REST, m - sum(widths)))
    starts = [sum(widths[:i]) for i in range(len(widths))]
    assert sum(widths) == m
    nb = len(widths)
    ahead = 2
    deps = [] if dep is None else [dep]

    def body(a_hbm, b_hbm, *rest):
        o_hbm, b_v, sems = rest[len(deps)], rest[len(deps) + 1], rest[-1]
        a_v, o_v = rest[len(deps) + 2:len(deps) + 2 + nb], rest[len(deps) + 2 + nb:-1]
        load_b = pltpu.make_async_copy(b_hbm, b_v, sems.at[0])
        loads = [pltpu.make_async_copy(a_hbm.at[:, pl.ds(c0, cw)], a_v[i], sems.at[1 + i])
                 for i, (c0, cw) in enumerate(zip(starts, widths))]
        stores = [pltpu.make_async_copy(o_v[i], o_hbm.at[pl.ds(c0, cw), :], sems.at[1 + nb + i])
                  for i, (c0, cw) in enumerate(zip(starts, widths))]
        loads[0].start()
        load_b.start()
        for i in range(1, min(ahead, nb)):
            loads[i].start()
        loads[0].wait()
        load_b.wait()
        for i in range(nb):
            if i > 0:
                loads[i].wait()
            if i + ahead < nb:
                loads[i + ahead].start()
            o_v[i][...] = _dot(a_v[i][...], b_v[...], 0, 0).astype(BF16)
            stores[i].start()
        for cp in stores:
            cp.wait()

    return pl.pallas_call(
        body, name=name, out_shape=jax.ShapeDtypeStruct((m, n), BF16),
        in_specs=[ANY_SPEC] * (2 + len(deps)), out_specs=ANY_SPEC,
        scratch_shapes=[pltpu.VMEM((k, n), BF16)] + [pltpu.VMEM((k, cw), BF16) for cw in widths]
        + [pltpu.VMEM((cw, n), BF16) for cw in widths] + [pltpu.SemaphoreType.DMA((1 + 2 * nb,))],
        compiler_params=_params(),
    )(a, b, *deps)


def _mm_rows(prods, rows_in, vecs_in, epilogue, outs, *, tm, name, dep=None):
    m = prods[0][0].shape[0]
    n = prods[0][1].shape[0] if prods[0][2] else prods[0][1].shape[1]
    tm = min(tm, m)
    assert m % tm == 0
    deps = [] if dep is None else [dep]
    n_p, n_r, n_v = len(prods), len(rows_in), len(vecs_in)

    def body(*refs):
        ab = refs[:2 * n_p]
        row_refs = refs[2 * n_p:2 * n_p + n_r]
        vec_refs = refs[2 * n_p + n_r:2 * n_p + n_r + n_v]
        out_refs = refs[2 * n_p + n_r + n_v + len(deps):]
        p = None
        for j, (_, _, tb) in enumerate(prods):
            t = _dot(ab[2 * j][...].astype(BF16), ab[2 * j + 1][...], 1, 1 if tb else 0)
            p = t if p is None else p + t
        vals = epilogue(p, *[r[...] for r in row_refs], *[v[...] for v in vec_refs])
        for (dtype, kind), o_ref, val in zip(outs, out_refs, vals):
            if kind == "row":
                o_ref[...] = val.astype(dtype)
            else:
                @pl.when(pl.program_id(0) == 0)
                def _(o_ref=o_ref):
                    o_ref[...] = jnp.zeros_like(o_ref)

                o_ref[...] += val

    row = lambda w: pl.BlockSpec((tm, w), lambda i: (i, 0))
    whole = lambda a: pl.BlockSpec(a.shape, lambda i: (0,) * a.ndim, pipeline_mode=pl.Buffered(1))
    in_specs, args = [], []
    for a, b, _ in prods:
        in_specs += [row(a.shape[1]), whole(b)]
        args += [a, b]
    in_specs += [row(r.shape[1]) for r in rows_in] + [whole(v) for v in vecs_in] + [ANY_SPEC] * len(deps)
    return pl.pallas_call(
        body, name=name,
        out_shape=tuple(jax.ShapeDtypeStruct((m, n) if kind == "row" else (8, n), dtype) for dtype, kind in outs),
        grid=(m // tm,), in_specs=in_specs,
        out_specs=tuple(row(n) if kind == "row" else pl.BlockSpec((8, n), lambda i: (0, 0)) for _, kind in outs),
        compiler_params=_params(dimension_semantics=("arbitrary",)),
    )(*args, *rows_in, *vecs_in, *deps)


def _rstd(x):
    return lax.rsqrt(jnp.mean(x * x, axis=-1, keepdims=True) + EPS)


def _norm_bwd(xh, r, t):
    return r * (t - xh * jnp.mean(xh * t, axis=-1, keepdims=True))


ROW_F32, ROW_BF16, SUM_F32 = (F32, "row"), (BF16, "row"), (F32, "sum")


def _then(epilogue, index, tb):
    def run(p, *args):
        vals = epilogue(p, *args[:-1])
        return (*vals, _dot(vals[index].astype(BF16), args[-1], 1, 1 if tb else 0))

    return run


def _ep_post_pre(p, h, g_post, g_pre):
    y = p.astype(BF16)
    yf = y.astype(F32)
    hn = h + yf * _rstd(yf) * g_post
    return y, hn, hn * _rstd(hn) * g_pre


_EP_POST_PRE_OUTS = [ROW_BF16, ROW_F32, ROW_BF16]


def _ep_final_loss(y, h, target, g_post):
    r = _rstd(y)
    yh = y * r
    err = h + yh * g_post - target
    dh = err * (1.0 / D)
    return _rowsum8(err * err), dh, _norm_bwd(yh, r, dh * g_post), _rowsum8(dh * yh)


def _ep_post_pre_bwd(du, dh_out, hn, y, g_post, g_pre):
    r2 = _rstd(hn)
    xh = hn * r2
    dh = dh_out + _norm_bwd(xh, r2, du * g_pre)
    yf = y.astype(F32)
    r1 = _rstd(yf)
    yh = yf * r1
    return dh, _norm_bwd(yh, r1, dh * g_post), _rowsum8(du * xh), _rowsum8(dh * yh)


_EP_POST_PRE_BWD_OUTS = [ROW_F32, ROW_BF16, SUM_F32, SUM_F32]


def _ep_pre_bwd(du, dh_out, x, g):
    r = _rstd(x)
    xh = x * r
    return dh_out + _norm_bwd(xh, r, du * g), _rowsum8(du * xh)


_EP_PRE_BWD_OUTS = [ROW_F32, SUM_F32]


def _prenorm(x, g, *, name, dep=None):
    t, d = x.shape
    tb = min(512, t)
    deps = [] if dep is None else [dep]

    def body(x_ref, g_ref, *rest):
        xf = x_ref[...]
        rest[-1][...] = (xf * _rstd(xf) * g_ref[...]).astype(BF16)

    return pl.pallas_call(
        body, name=name, out_shape=jax.ShapeDtypeStruct((t, d), BF16), grid=(t // tb,),
        in_specs=[pl.BlockSpec((tb, d), lambda i: (i, 0)), pl.BlockSpec((1, d), lambda i: (0, 0))]
        + [ANY_SPEC] * len(deps),
        out_specs=pl.BlockSpec((tb, d), lambda i: (i, 0)), compiler_params=_params(),
    )(x, g, *deps)


QB = 256


def _half_mask(shape, e):
    lane = lax.broadcasted_iota(jnp.int32, shape, len(shape) - 1)
    return (lane // 64) == e


def _place(kv):
    sw = pltpu.roll(kv, 64, 1)
    m0 = _half_mask(kv.shape, 0)
    return [[jnp.where(m0, kv, 0.0).astype(BF16), jnp.where(m0, 0.0, sw).astype(BF16)],
            [jnp.where(m0, sw, 0.0).astype(BF16), jnp.where(m0, 0.0, kv).astype(BF16)]]


SQ = 128
SK = 256


def _swa_valid(i, sb):
    qc = lax.broadcasted_iota(jnp.int32, (SQ, SK), 0) // CHUNK
    kc = lax.broadcasted_iota(jnp.int32, (SQ, SK), 1) // CHUNK - 2
    return (kc <= qc) & (qc <= kc + 2) & (4 * i + 2 * sb + kc >= 0)


def _swa_fwd(z, sinks, t, dep=None):
    nb = t // QB
    deps = [] if dep is None else [dep]

    def body(s_ref, q_ref, kp_ref, kc_ref, vp_ref, vc_ref, *rest):
        o_ref, lse_ref = rest[-2:]
        i = pl.program_id(0)
        kpl = _place(jnp.concatenate([kp_ref[...], kc_ref[...]], axis=0))
        vpl = _place(jnp.concatenate([vp_ref[...], vc_ref[...]], axis=0))
        lane = lax.broadcasted_iota(jnp.int32, (SQ, 128), 1)
        for sb in range(QB // SQ):
            rows, keys = slice(SQ * sb, SQ * (sb + 1)), slice(SQ * sb, SQ * sb + SK)
            valid = _swa_valid(i, sb)
            lse_out = jnp.zeros((SQ, 128), F32)
            for j in range(4):
                qp = q_ref[rows, 128 * j:128 * (j + 1)].astype(BF16)
                acc = jnp.zeros((SQ, 128), F32)
                for e in range(2):
                    h = 2 * j + e
                    kvh = h // 4
                    qm = jnp.where(_half_mask(qp.shape, e), qp, jnp.zeros_like(qp))
                    s = _dot(qm, kpl[kvh][e][keys], 1, 1) * 0.125
                    s = jnp.where(valid, s, NEG)
                    sink = s_ref[0, h]
                    m = jnp.maximum(jnp.max(s, axis=-1, keepdims=True), sink)
                    p = jnp.exp(s - m)
                    l = jnp.sum(p, axis=-1, keepdims=True) + jnp.exp(sink - m)
                    acc = acc + _dot(p.astype(BF16), vpl[kvh][e][keys], 1, 0) * (1.0 / l)
                    lse_out = jnp.where(lane == h, m + jnp.log(l), lse_out)
                o_ref[rows, 128 * j:128 * (j + 1)] = acc.astype(BF16)
            lse_ref[rows, :] = lse_out

    prev = lambda c: pl.BlockSpec((128, 128), lambda i: (jnp.maximum(2 * i - 1, 0), c))
    cur = lambda c: pl.BlockSpec((QB, 128), lambda i: (i, c))
    return pl.pallas_call(
        body, name="swa_fwd",
        out_shape=(jax.ShapeDtypeStruct((t, D), BF16), jax.ShapeDtypeStruct((t, 128), F32)),
        grid=(nb,),
        in_specs=[pl.BlockSpec(memory_space=pltpu.SMEM),
                  pl.BlockSpec((QB, SWA_W), lambda i: (i, 0)), prev(4), cur(4), prev(5), cur(5)]
        + [ANY_SPEC] * len(deps),
        out_specs=(pl.BlockSpec((QB, SWA_W), lambda i: (i, 0)), pl.BlockSpec((QB, 128), lambda i: (i, 0))),
        compiler_params=_params(),
    )(sinks, z, z, z, z, z, *deps)


def _swa_bwd(z, sinks, ymix, lse, dymix, t, dep=None):
    nb = t // QB
    deps = [] if dep is None else [dep]

    def body(s_ref, q_ref, kp_ref, kc_ref, vp_ref, vc_ref, o_ref, do_ref, l_ref, *rest):
        dq_ref, first_ref, second_ref, ds_ref, carry_ref = rest[len(deps):]
        i = pl.program_id(0)
        live = i < nb

        @pl.when(i == 0)
        def _():
            ds_ref[...] = jnp.zeros_like(ds_ref)
            carry_ref[...] = jnp.zeros_like(carry_ref)

        lane = lax.broadcasted_iota(jnp.int32, (8, 128), 1)
        kpl = _place(jnp.concatenate([kp_ref[...], kc_ref[...]], axis=0))
        vpl = _place(jnp.concatenate([vp_ref[...], vc_ref[...]], axis=0))
        nk = QB + 128
        qc = lax.broadcasted_iota(jnp.int32, (QB, nk), 0) // CHUNK
        kc = lax.broadcasted_iota(jnp.int32, (QB, nk), 1) // CHUNK - 2
        valid = (kc <= qc) & (qc <= kc + 2) & (4 * i + kc >= 0) & live
        lse_c = l_ref[...]
        dsink = jnp.zeros((8, 128), F32)
        dk_acc = [[jnp.zeros((128, nk), F32) for _ in range(2)] for _ in range(2)]
        dv_acc = [[jnp.zeros((128, nk), F32) for _ in range(2)] for _ in range(2)]
        dq = []
        for j in range(4):
            cols = slice(128 * j, 128 * (j + 1))
            qp = q_ref[:, cols].astype(BF16)
            dop = do_ref[:, cols]
            prod = dop.astype(F32) * o_ref[:, cols].astype(F32)
            acc = jnp.zeros((QB, 128), F32)
            for e in range(2):
                h = 2 * j + e
                kvh = h // 4
                hm = _half_mask(qp.shape, e)
                qm = jnp.where(hm, qp, jnp.zeros_like(qp))
                dom = jnp.where(hm, dop, jnp.zeros_like(dop))
                dd = jnp.sum(jnp.where(hm, prod, 0.0), axis=-1, keepdims=True)
                lse_h = lse_c[:, h:h + 1]
                s = _dot(qm, kpl[kvh][e], 1, 1) * 0.125
                p = jnp.where(valid, jnp.exp(s - lse_h), 0.0)
                dp = _dot(dom, vpl[kvh][e], 1, 1)
                ds = (p * (dp - dd) * 0.125).astype(BF16)
                acc = acc + _dot(ds, kpl[kvh][e], 1, 0)
                dk_acc[kvh][e] = dk_acc[kvh][e] + _dot(qm, ds, 0, 0)
                dv_acc[kvh][e] = dv_acc[kvh][e] + _dot(dom, p.astype(BF16), 0, 0)
                ps = jnp.where(live, jnp.exp(s_ref[0, h] - lse_h) * dd, 0.0)
                dsink = dsink - jnp.where(lane == h, _rowsum8(jnp.broadcast_to(ps, (QB, 128))), 0.0)
            dq.append(acc.astype(BF16))
        ds_ref[...] += dsink
        dk = (dk_acc[0][0] + dk_acc[1][1] + pltpu.roll(dk_acc[0][1] + dk_acc[1][0], 64, 0)).T
        dv = (dv_acc[0][0] + dv_acc[1][1] + pltpu.roll(dv_acc[0][1] + dv_acc[1][0], 64, 0)).T
        dkv = jnp.concatenate([dk, dv], axis=1)
        second_ref[...] = (carry_ref[...] + dkv[0:128]).astype(BF16)
        carry_ref[...] = dkv[256:384]

        @pl.when(live)
        def _():
            for j in range(4):
                dq_ref[:, 128 * j:128 * (j + 1)] = dq[j]
            first_ref[...] = dkv[128:256].astype(BF16)

    blk = lambda i: jnp.minimum(i, nb - 1)
    prev = lambda c: pl.BlockSpec((128, 128), lambda i: (jnp.maximum(2 * blk(i) - 1, 0), c))
    cur = lambda w, c: pl.BlockSpec((QB, w), lambda i: (blk(i), c))
    half = lambda index: pl.BlockSpec((128, 256), lambda i: (index(i), 0))
    return pl.pallas_call(
        body, name="swa_bwd",
        out_shape=(jax.ShapeDtypeStruct((t, SWA_W), BF16), jax.ShapeDtypeStruct((t // 2, 256), BF16),
                   jax.ShapeDtypeStruct((t // 2, 256), BF16), jax.ShapeDtypeStruct((8, 128), F32)),
        grid=(nb + 1,),
        in_specs=[pl.BlockSpec(memory_space=pltpu.SMEM),
                  cur(SWA_W, 0), prev(4), cur(128, 4), prev(5), cur(128, 5),
                  cur(SWA_W, 0), cur(SWA_W, 0), cur(128, 0)] + [ANY_SPEC] * len(deps),
        out_specs=(cur(SWA_W, 0), half(blk), half(lambda i: jnp.maximum(i - 1, 0)),
                   pl.BlockSpec((8, 128), lambda i: (0, 0))),
        scratch_shapes=[pltpu.VMEM((128, 256), F32)],
        compiler_params=_params(dimension_semantics=("arbitrary",)),
    )(sinks, z, z, z, z, z, ymix, dymix, lse, *deps)


HB = 256


def _lower_bound(lb_ref):
    a = lb_ref[...]
    a0, a1 = a[0:1], a[1:2]
    mx = jnp.maximum(a0, a1)
    e0, e1 = jnp.exp(a0 - mx), jnp.exp(a1 - mx)
    return e0 / (e0 + e1)


def _hgrn_cols(row_block):
    return [pl.BlockSpec((HB, 2 * HD), lambda j, c=base // (2 * HD) + p: (row_block(j), c))
            for base in (ZQH, ZFH, ZIH, ZGH) for p in range(2)]


NCH = HB // CHUNK


def _split3(x):
    hi = x.astype(BF16)
    r1 = x - hi.astype(F32)
    mid = r1.astype(BF16)
    return hi, mid, (r1 - mid.astype(F32)).astype(BF16)


def _blockdiag(lower):
    r = lax.broadcasted_iota(jnp.int32, (HB, HB), 0)
    c = lax.broadcasted_iota(jnp.int32, (HB, HB), 1)
    return (r // CHUNK == c // CHUNK) & ((c <= r) if lower else (c >= r))


def _chunk_sums(mask_bf16, x):
    return sum(_dot(mask_bf16, part, 1, 0) for part in _split3(x))


def _per_chunk_rows(x, row):
    w = x.shape[1]
    picked = x.reshape(NCH, CHUNK, w)[:, row:row + 1, :]
    return jnp.broadcast_to(picked, (NCH, CHUNK, w)).reshape(HB, w)


def _chunk_stack(x, chunk_of_row):
    return jnp.concatenate([jnp.where(chunk_of_row == c, x, jnp.zeros_like(x)) for c in range(NCH)], axis=1)


def _chunk_pick(x, chunk_of_row):
    w = x.shape[1] // NCH
    out = jnp.zeros((HB, w), x.dtype)
    for c in range(NCH):
        out = jnp.where(chunk_of_row == c, x[:, c * w:(c + 1) * w], out)
    return out


def _hgrn_local(q, f, kf, b):
    sq = _sig(q)
    qf = q * sq * (HD ** -0.5)
    b_mid = _per_chunk_rows(b, CHUNK // 2 - 1)
    b_last = _per_chunk_rows(b, CHUNK - 1)
    qm = qf * jnp.exp(b - b_mid)
    km = kf * jnp.exp(b_mid - b)
    kl = kf * jnp.exp(b_last - b)
    qb = qf * jnp.exp(b)
    return dict(sq=sq, b_mid=b_mid, b_last=b_last, qm=qm, km=km, kl=kl, qb=qb)


def _hgrn2_fwd(z, hgrn_lb, onorm, ymix, t, dep=None):
    nb = t // HB
    deps = [] if dep is None else [dep]

    def body(*refs):
        zq, zf, zi, zg = refs[0:2], refs[2:4], refs[4:6], refs[6:8]
        (lb_ref, on_ref), (y_ref, o_ref, sp_ref, st_ref) = refs[8:10], refs[-4:]

        @pl.when(pl.program_id(0) == 0)
        def _():
            st_ref[...] = jnp.zeros_like(st_ref)

        lb_all = _lower_bound(lb_ref)
        gn = on_ref[...]
        low = _blockdiag(True)
        low_b = low.astype(BF16)
        chunk_of_row = lax.broadcasted_iota(jnp.int32, (HB, HD), 0) // CHUNK
        for p in range(2):
            lbp = lb_all[:, 2 * HD * p:2 * HD * (p + 1)]
            fp = lbp + (1.0 - lbp) * _sig(zf[p][...])
            bp = _chunk_sums(low_b, jnp.log(fp))
            for e in range(2):
                h, ls = 2 * p + e, slice(e * HD, (e + 1) * HD)
                f = fp[:, ls]
                w = _hgrn_local(zq[p][:, ls], f, 1.0 - f, bp[:, ls])
                iv = zi[p][:, ls].astype(BF16)
                a = jnp.where(low, _dot(w["qm"].astype(BF16), w["km"].astype(BF16), 1, 1), 0.0)
                o = _dot(a.astype(BF16), iv, 1, 0)
                u = _dot(iv, _chunk_stack(w["kl"].astype(BF16), chunk_of_row), 0, 0)
                decay = jnp.exp(w["b_last"])
                st = st_ref[h]
                states = []
                for c in range(NCH):
                    sp_ref[h, c] = st
                    states.append(st.astype(BF16))
                    st = st * decay[c * CHUNK:c * CHUNK + 1] + u[:, c * HD:(c + 1) * HD]
                st_ref[h] = st
                inter = _dot(w["qb"].astype(BF16), jnp.concatenate(states, axis=0), 1, 1)
                o = o + _chunk_pick(inter, chunk_of_row)
                hs = slice(h * HD, (h + 1) * HD)
                o_ref[:, hs] = o
                gg = zg[p][:, ls]
                y_ref[:, hs] = (o * _rstd(o) * gn * (gg * _sig(gg))).astype(BF16)

    return pl.pallas_call(
        body, name="hgrn_fwd",
        out_shape=(jax.ShapeDtypeStruct((t, D), BF16), jax.ShapeDtypeStruct((t, HG_W), F32),
                   jax.ShapeDtypeStruct((4, t // CHUNK, HD, HD), F32)),
        grid=(nb,),
        in_specs=_hgrn_cols(lambda j: j) + [pl.BlockSpec((2, HG_W), lambda j: (0, 0)),
                                            pl.BlockSpec((1, HD), lambda j: (0, 0)), ANY_SPEC]
        + [ANY_SPEC] * len(deps),
        out_specs=(pl.BlockSpec((HB, HG_W), lambda j: (j, 1)),
                   pl.BlockSpec((HB, HG_W), lambda j: (j, 0)),
                   pl.BlockSpec((4, NCH, HD, HD), lambda j: (0, j, 0, 0))),
        scratch_shapes=[pltpu.VMEM((4, HD, HD), F32)],
        input_output_aliases={10: 0},
        compiler_params=_params(dimension_semantics=("arbitrary",)),
    )(*[z] * 8, hgrn_lb, onorm, ymix, *deps)


def _hgrn2_bwd(z, hgrn_lb, onorm, o_save, sprev, dymix, dza, t):
    nb = t // HB

    def body(*refs):
        zq, zf, zi, zg = refs[0:2], refs[2:4], refs[4:6], refs[6:8]
        (lb_ref, on_ref, o_ref, sp_ref, dy_ref, dqa_ref, first_ref, second_ref,
         dz_ref, dlb_ref, don_ref, dst_ref) = refs[8:]

        @pl.when(pl.program_id(0) == 0)
        def _():
            dst_ref[...] = jnp.zeros_like(dst_ref)
            dlb_ref[...] = jnp.zeros_like(dlb_ref)
            don_ref[...] = jnp.zeros_like(don_ref)

        dz_ref[:, 0:SWA_W] = dqa_ref[...]
        dz_ref[0:HB // 2, SWA_W:ZQH] = first_ref[...]
        dz_ref[HB // 2:HB, SWA_W:ZQH] = second_ref[...]
        lb_all = _lower_bound(lb_ref)
        gn = on_ref[...]
        low, upp = _blockdiag(True), _blockdiag(False)
        upp_b = upp.astype(BF16)
        low_b = low.astype(BF16)
        row = lax.broadcasted_iota(jnp.int32, (HB, HD), 0)
        chunk_of_row = row // CHUNK
        in_chunk = row % CHUNK
        for p in range(2):
            lbp = lb_all[:, 2 * HD * p:2 * HD * (p + 1)]
            sgp = _sig(zf[p][...])
            fp = lbp + (1.0 - lbp) * sgp
            bp = _chunk_sums(low_b, jnp.log(fp))
            db_pair, dkf_pair = [], []
            for e in range(2):
                h, ls, hs = 2 * p + e, slice(e * HD, (e + 1) * HD), slice((2 * p + e) * HD, (2 * p + e + 1) * HD)
                f = fp[:, ls]
                q = zq[p][:, ls]
                w = _hgrn_local(q, f, 1.0 - f, bp[:, ls])
                iv = zi[p][:, ls].astype(BF16)
                gg = zg[p][:, ls]
                o = o_ref[:, hs]
                dout = dy_ref[:, hs].astype(F32)
                sgg = _sig(gg)
                r = _rstd(o)
                oh = o * r
                dyn = dout * (gg * sgg)
                dz_ref[:, ZGH + h * HD:ZGH + (h + 1) * HD] = (
                    dout * oh * gn * (sgg * (1.0 + gg * (1.0 - sgg)))).astype(BF16)
                don_ref[...] += _rowsum8(dyn * oh)
                do = _norm_bwd(oh, r, dyn * gn).astype(BF16)
                qm, km, kl, qb = (w[n].astype(BF16) for n in ("qm", "km", "kl", "qb"))
                decay = jnp.exp(w["b_last"])
                grads_in = _dot(do, _chunk_stack(qb, chunk_of_row), 0, 0)
                dst = dst_ref[h]
                dstn, dd_rows = [None] * NCH, [None] * NCH
                for c in reversed(range(NCH)):
                    dstn[c] = dst.astype(BF16)
                    dd_rows[c] = jnp.sum(dst * sp_ref[h, c], axis=0, keepdims=True)
                    dst = dst * decay[c * CHUNK:c * CHUNK + 1] + grads_in[:, c * HD:(c + 1) * HD]
                dst_ref[h] = dst
                states = jnp.concatenate([sp_ref[h, c].astype(BF16) for c in range(NCH)], axis=0)
                dstn_all = jnp.concatenate(dstn, axis=0)
                dqb = _dot(_chunk_stack(do, chunk_of_row), states, 1, 0)
                at = jnp.where(upp, _dot(km, qm, 1, 1), 0.0)
                di = _dot(at.astype(BF16), do, 1, 0) + _chunk_pick(_dot(kl, dstn_all, 1, 1), chunk_of_row)
                dz_ref[:, ZIH + h * HD:ZIH + (h + 1) * HD] = di.astype(BF16)
                dkl = _dot(_chunk_stack(iv, chunk_of_row), dstn_all, 1, 0)
                da = jnp.where(low, _dot(do, iv, 1, 1), 0.0).astype(BF16)
                dat = jnp.where(upp, _dot(iv, do, 1, 1), 0.0).astype(BF16)
                dqm = _dot(da, km, 1, 0)
                dkm = _dot(dat, qm, 1, 0)
                b = bp[:, ls]
                e1, e2 = jnp.exp(b - w["b_mid"]), jnp.exp(w["b_mid"] - b)
                e3, e4 = jnp.exp(w["b_last"] - b), jnp.exp(b)
                dqf = dqm * e1 + dqb * e4
                dkf_pair.append(dkm * e2 + dkl * e3)
                t_qm, t_km, t_kl = dqm * w["qm"], dkm * w["km"], dkl * w["kl"]
                db = t_qm - t_km - t_kl + dqb * w["qb"]
                db_mid = jnp.sum((t_km - t_qm).reshape(NCH, CHUNK, HD), axis=1, keepdims=True)
                db_last = jnp.sum(t_kl.reshape(NCH, CHUNK, HD), axis=1, keepdims=True)
                db_last = db_last + jnp.stack(dd_rows, axis=0) * jnp.exp(
                    bp[:, ls].reshape(NCH, CHUNK, HD)[:, CHUNK - 1:CHUNK, :])
                spread = lambda v: jnp.broadcast_to(v, (NCH, CHUNK, HD)).reshape(HB, HD)
                db = (db + jnp.where(in_chunk == CHUNK // 2 - 1, spread(db_mid), 0.0)
                      + jnp.where(in_chunk == CHUNK - 1, spread(db_last), 0.0))
                db_pair.append(db)
                sq = w["sq"]
                dz_ref[:, ZQH + h * HD:ZQH + (h + 1) * HD] = (
                    dqf * (HD ** -0.5) * (sq * (1.0 + q * (1.0 - sq)))).astype(BF16)
            dlogf = _chunk_sums(upp_b, jnp.concatenate(db_pair, axis=1))
            dfv = dlogf / fp - jnp.concatenate(dkf_pair, axis=1)
            dz_ref[:, ZFH + 2 * HD * p:ZFH + 2 * HD * (p + 1)] = (dfv * (1.0 - lbp) * sgp * (1.0 - sgp)).astype(BF16)
            dlb_ref[:, 2 * HD * p:2 * HD * (p + 1)] += _rowsum8(dfv * (1.0 - sgp))

    rev = lambda j: nb - 1 - j
    return pl.pallas_call(
        body, name="hgrn_bwd",
        out_shape=(jax.ShapeDtypeStruct((t, D_IN), BF16), jax.ShapeDtypeStruct((8, HG_W), F32),
                   jax.ShapeDtypeStruct((8, HD), F32)),
        grid=(nb,),
        in_specs=_hgrn_cols(rev) + [pl.BlockSpec((2, HG_W), lambda j: (0, 0)), pl.BlockSpec((1, HD), lambda j: (0, 0)),
                                    pl.BlockSpec((HB, HG_W), lambda j: (rev(j), 0)),
                                    pl.BlockSpec((4, NCH, HD, HD), lambda j: (0, rev(j), 0, 0)),
                                    pl.BlockSpec((HB, HG_W), lambda j: (rev(j), 1)),
                                    pl.BlockSpec((HB, SWA_W), lambda j: (rev(j), 0)),
                                    pl.BlockSpec((HB // 2, 2 * KV_W), lambda j: (rev(j), 0)),
                                    pl.BlockSpec((HB // 2, 2 * KV_W), lambda j: (rev(j), 0))],
        out_specs=(pl.BlockSpec((HB, D_IN), lambda j: (rev(j), 0)), pl.BlockSpec((8, HG_W), lambda j: (0, 0)),
                   pl.BlockSpec((8, HD), lambda j: (0, 0))),
        scratch_shapes=[pltpu.VMEM((4, HD, HD), F32)],
        compiler_params=_params(dimension_semantics=("arbitrary",)),
    )(*[z] * 8, hgrn_lb, onorm, o_save, sprev, dymix, *dza)


XB = 512


def _xattn_fwd(q, k, v, wo, h, g_post, g_pre, t, dep=None):
    tb = min(XB, t)
    deps = [] if dep is None else [dep]

    def body(q_ref, k_ref, v_ref, wo_ref, h_ref, gp_ref, gn_ref, *rest):
        o_ref, y_ref, hn_ref, u_ref = rest[len(deps):]
        for hd in range(XH):
            cols = slice(XD * hd, XD * (hd + 1))
            s = _dot(q_ref[:, cols], k_ref[:, cols], 1, 1) * (XD ** -0.5)
            p = jnp.exp(s - jnp.max(s, axis=-1, keepdims=True))
            l = jnp.sum(p, axis=-1, keepdims=True)
            o_ref[:, cols] = (_dot(p.astype(BF16), v_ref[:, cols], 1, 0) * (1.0 / l)).astype(BF16)
        y, hn, u = _ep_post_pre(_dot(o_ref[...], wo_ref[...], 1, 0), h_ref[...], gp_ref[...], gn_ref[...])
        y_ref[...] = y
        hn_ref[...] = hn
        u_ref[...] = u.astype(BF16)

    row = pl.BlockSpec((tb, D), lambda i: (i, 0))
    whole = lambda a: pl.BlockSpec(a.shape, lambda i: (0,) * a.ndim, pipeline_mode=pl.Buffered(1))
    half = jax.ShapeDtypeStruct((t, D), BF16)
    return pl.pallas_call(
        body, name="xattn_fwd", out_shape=(half, half, jax.ShapeDtypeStruct((t, D), F32), half), grid=(t // tb,),
        in_specs=[row, whole(k), whole(v), whole(wo), row, whole(g_post), whole(g_pre)] + [ANY_SPEC] * len(deps),
        out_specs=(row, row, row, row), compiler_params=_params(),
    )(q, k, v, wo, h, g_post, g_pre, *deps)


def _xattn_bwd(q, k, v, do, wq, wout, dh_out, hn, y, g_post, g_pre, t):
    tb = min(XB, t)

    def body(q_ref, k_ref, v_ref, do_ref, wq_ref, wout_ref, dho_ref, hn_ref, y_ref, gp_ref, gn_ref,
             dq_ref, dk_ref, dv_ref, dh_ref, dyp_ref, dym_ref, dgn_ref, dgp_ref):
        @pl.when(pl.program_id(0) == 0)
        def _():
            dk_ref[...] = jnp.zeros_like(dk_ref)
            dv_ref[...] = jnp.zeros_like(dv_ref)
            dgn_ref[...] = jnp.zeros_like(dgn_ref)
            dgp_ref[...] = jnp.zeros_like(dgp_ref)

        for h in range(XH):
            cols = slice(XD * h, XD * (h + 1))
            qh, kh, vh, doh = q_ref[:, cols], k_ref[:, cols], v_ref[:, cols], do_ref[:, cols]
            s = _dot(qh, kh, 1, 1) * (XD ** -0.5)
            p = jnp.exp(s - jnp.max(s, axis=-1, keepdims=True))
            p = p * (1.0 / jnp.sum(p, axis=-1, keepdims=True))
            dp = _dot(doh, vh, 1, 1)
            ds = (p * (dp - jnp.sum(p * dp, axis=-1, keepdims=True)) * (XD ** -0.5)).astype(BF16)
            dq_ref[:, cols] = _dot(ds, kh, 1, 0).astype(BF16)
            dk_ref[:, cols] += _dot(ds, qh, 0, 0)
            dv_ref[:, cols] += _dot(p.astype(BF16), doh, 0, 0)
        du = _dot(dq_ref[...], wq_ref[...], 1, 1)
        dh, dyp, dgn, dgp = _ep_post_pre_bwd(du, dho_ref[...], hn_ref[...], y_ref[...], gp_ref[...], gn_ref[...])
        dh_ref[...] = dh
        dyp = dyp.astype(BF16)
        dyp_ref[...] = dyp
        dym_ref[...] = _dot(dyp, wout_ref[...], 1, 1).astype(BF16)
        dgn_ref[...] += dgn
        dgp_ref[...] += dgp

    row = pl.BlockSpec((tb, D), lambda i: (i, 0))
    mem = pl.BlockSpec(k.shape, lambda i: (0, 0))
    whole = lambda a: pl.BlockSpec(a.shape, lambda i: (0,) * a.ndim, pipeline_mode=pl.Buffered(1))
    acc = pl.BlockSpec((8, D), lambda i: (0, 0))
    half = jax.ShapeDtypeStruct((t, D), BF16)
    return pl.pallas_call(
        body, name="xattn_bwd",
        out_shape=(half, jax.ShapeDtypeStruct(k.shape, F32), jax.ShapeDtypeStruct(k.shape, F32),
                   jax.ShapeDtypeStruct((t, D), F32), half, half,
                   jax.ShapeDtypeStruct((8, D), F32), jax.ShapeDtypeStruct((8, D), F32)),
        grid=(t // tb,),
        in_specs=[row, whole(k), whole(v), row, whole(wq), whole(wout), row, row, row, whole(g_post), whole(g_pre)],
        out_specs=(row, mem, mem, row, row, row, acc, acc),
        compiler_params=_params(dimension_semantics=("arbitrary",)),
    )(q, k, v, do, wq, wout, dh_out, hn, y, g_post, g_pre)


def _mem_kv(mem, g_mem, wk, wv):
    def body(m_ref, g_ref, wk_ref, wv_ref, mn_ref, k_ref, v_ref):
        m_ = m_ref[...]
        mn = (m_ * _rstd(m_) * g_ref[...]).astype(BF16)
        mn_ref[...] = mn
        k_ref[...] = _dot(mn, wk_ref[...], 1, 0).astype(BF16)
        v_ref[...] = _dot(mn, wv_ref[...], 1, 0).astype(BF16)

    return pl.pallas_call(body, name="mem_kv", out_shape=(jax.ShapeDtypeStruct(mem.shape, BF16),) * 3,
                          compiler_params=_params())(mem, g_mem, wk, wv)


def _mem_kv_bwd(mn, mem, dk, dv, wk, wv, dep=None):
    deps = [] if dep is None else [dep]

    def body(mn_ref, m_ref, dk_ref, dv_ref, wk_ref, wv_ref, *rest):
        gk_ref, gv_ref, dg_ref = rest[len(deps):]
        mn = mn_ref[...]
        dkb, dvb = dk_ref[...].astype(BF16), dv_ref[...].astype(BF16)
        gk_ref[...] = _dot(mn, dkb, 0, 0).astype(BF16)
        gv_ref[...] = _dot(mn, dvb, 0, 0).astype(BF16)
        dmn = _dot(dkb, wk_ref[...], 1, 1) + _dot(dvb, wv_ref[...], 1, 1)
        m_ = m_ref[...]
        dg_ref[...] = _rowsum8(dmn * (m_ * _rstd(m_)))

    vmem = pl.BlockSpec(memory_space=pltpu.VMEM)
    return pl.pallas_call(
        body, name="mem_kv_bwd",
        out_shape=(jax.ShapeDtypeStruct(wk.shape, BF16), jax.ShapeDtypeStruct(wv.shape, BF16),
                   jax.ShapeDtypeStruct((8, D), F32)),
        in_specs=[vmem] * 6 + [ANY_SPEC] * len(deps), out_specs=(vmem,) * 3, compiler_params=_params(),
    )(mn, mem, dk, dv, wk, wv, *deps)


FB = 256


def _ffn_fwd_bwd(u, wgt, wut, wd, h, target, g_last, y_prev, g_post, g_pre, wo, t):
    tb = min(FB, t)

    def body(u_ref, wg_ref, wu_ref, wd_ref, h_ref, t_ref, gl_ref, yp_ref, gp_ref, gn_ref, wo_ref,
             a_ref, dy_ref, dg_ref, dup_ref, dh_ref, dyp_ref, do_ref, sq_ref, dgl_ref, dgn_ref, dgp_ref):
        @pl.when(pl.program_id(0) == 0)
        def _():
            for ref in (sq_ref, dgl_ref, dgn_ref, dgp_ref):
                ref[...] = jnp.zeros_like(ref)

        u_ = u_ref[...]
        g = _dot(u_, wg_ref[...], 1, 1)
        up = _dot(u_, wu_ref[...], 1, 1)
        sg = _sig(g)
        a = (g * sg * up).astype(BF16)
        a_ref[...] = a
        h_ = h_ref[...]
        sq, dh3, dy, dgl = _ep_final_loss(_dot(a, wd_ref[...], 1, 0), h_, t_ref[...], gl_ref[...])
        sq_ref[...] += sq
        dgl_ref[...] += dgl
        dy = dy.astype(BF16)
        dy_ref[...] = dy
        da = _dot(dy, wd_ref[...], 1, 1)
        dup = (da * g * sg).astype(BF16)
        dgate = (da * up * (sg * (1.0 + g * (1.0 - sg)))).astype(BF16)
        dup_ref[...] = dup
        dg_ref[...] = dgate
        du = _dot(dgate, wg_ref[...], 1, 0) + _dot(dup, wu_ref[...], 1, 0)
        dh, dyp, dgn, dgp = _ep_post_pre_bwd(du, dh3, h_, yp_ref[...], gp_ref[...], gn_ref[...])
        dh_ref[...] = dh
        dyp = dyp.astype(BF16)
        dyp_ref[...] = dyp
        do_ref[...] = _dot(dyp, wo_ref[...], 1, 1).astype(BF16)
        dgn_ref[...] += dgn
        dgp_ref[...] += dgp

    row = lambda w: pl.BlockSpec((tb, w), lambda i: (i, 0))
    whole = lambda a: pl.BlockSpec(a.shape, lambda i: (0,) * a.ndim, pipeline_mode=pl.Buffered(1))
    acc = pl.BlockSpec((8, D), lambda i: (0, 0))
    wide, half, sums = (jax.ShapeDtypeStruct((t, D_FF), BF16), jax.ShapeDtypeStruct((t, D), BF16),
                        jax.ShapeDtypeStruct((8, D), F32))
    return pl.pallas_call(
        body, name="ffn_fwd_bwd",
        out_shape=(wide, half, wide, wide, jax.ShapeDtypeStruct((t, D), F32), half, half, sums, sums, sums, sums),
        grid=(t // tb,),
        in_specs=[row(D), whole(wgt), whole(wut), whole(wd), row(D), row(D), whole(g_last), row(D), whole(g_post),
                  whole(g_pre), whole(wo)],
        out_specs=(row(D_FF), row(D), row(D_FF), row(D_FF), row(D), row(D), row(D), acc, acc, acc, acc),
        compiler_params=_params(dimension_semantics=("arbitrary",)),
    )(u, wgt, wut, wd, h, target, g_last, y_prev, g_post, g_pre, wo)


def _local_step(x, mem, target, fetch, sm, emit=None, first_dep=None, milestone=None):
    t = x.shape[0]
    w, gw = {}, {}

    def out(key, g):
        gw[key] = g
        return None if emit is None else emit(key, g)

    def tell(tag, value):
        return None if milestone is None else milestone(tag, value)
    u1 = _prenorm(x, sm["g_mix_pre"], name="prenorm_mix", dep=first_dep)
    w["winT"] = fetch("winT", u1)
    z = _mm(u1, w["winT"], tb=True, out_dtype=F32, tm=1024, tn=1408, name="mm_z", n_outer=True)
    ymix, lse = _swa_fwd(z, sm["sinks"], t, dep=tell("z", z))
    ymix, o_h, sprev = _hgrn2_fwd(z, sm["hgrn_lb"], sm["hgrn_onorm"], ymix, t, dep=tell("swa", lse))
    for key in ("wout", "wq", "wk", "wv", "wo"):
        w[key] = fetch(key, ymix)
    y1, h1, u2, qx = _mm_rows([(ymix, w["wout"], False)], [x], [sm["g_mix_post"], sm["g_x_pre"], w["wq"]],
                              _then(_ep_post_pre, 2, False), _EP_POST_PRE_OUTS + [ROW_BF16], tm=1024,
                              name="mm_y1_post_qx")
    mn, kx, vx = _mem_kv(mem, sm["g_mem"], w["wk"], w["wv"])
    ox, y2, h2, u3 = _xattn_fwd(qx, kx, vx, w["wo"], h1, sm["g_x_post"], sm["g_ffn_pre"], t, dep=tell("kv", kx))
    for key in ("wgT", "wuT", "wd"):
        w[key] = fetch(key, u3)
    act, dy3, dgate, dup, dh2, dy2, dox, sq, dg_ffn_post, dg_ffn_pre, dg_x_post = _ffn_fwd_bwd(
        u3, w["wgT"], w["wuT"], w["wd"], h2, target, sm["g_ffn_post"], y2, sm["g_x_post"], sm["g_ffn_pre"], w["wo"], t)
    dep = out("wd", _mm_tn(act, dy3, name="mm_gwd"))
    dep = out("wgT", _mm_tn(dgate, u3, name="mm_gwg", dep=dep))
    dep = out("wuT", _mm_tn(dup, u3, name="mm_gwu", dep=dep))
    out("wo", _mm_tn(ox, dy2, name="mm_gwo", dep=dep))
    dqx, dkx, dvx, dh1, dy1, dymix, dg_x_pre, dg_mix_post = _xattn_bwd(
        qx, kx, vx, dox, w["wq"], w["wout"], dh2, h1, y1, sm["g_mix_post"], sm["g_x_pre"], t)
    out("wq", _mm_tn(u2, dqx, name="mm_gwq"))
    gwk, gwv, dg_mem = _mem_kv_bwd(mn, mem, dkx, dvx, w["wk"], w["wv"])
    out("wk", gwk)
    dep = out("wv", gwv)
    dep = out("wout", _mm_tn(ymix, dy1, name="mm_gwout", dep=dep))
    *dza, dsinks = _swa_bwd(z, sm["sinks"], ymix, lse, dymix, t, dep=dep)
    dz, dlb, donorm = _hgrn2_bwd(z, sm["hgrn_lb"], sm["hgrn_onorm"], o_h, sprev, dymix, dza, t)
    dep = out("winT", _mm_tn(dz, u1, name="mm_gwin"))
    grad_x, dg_mix_pre = _mm_rows([(dz, w["winT"], False)], [dh1, x], [sm["g_mix_pre"]], _ep_pre_bwd,
                                  _EP_PRE_BWD_OUTS, tm=512, name="mm_du1_pre_bwd", dep=dep)
    parts = dict(g_mix_pre=dg_mix_pre, g_mix_post=dg_mix_post, g_mem=dg_mem, g_x_pre=dg_x_pre,
                 g_x_post=dg_x_post, g_ffn_pre=dg_ffn_pre, g_ffn_post=dg_ffn_post,
                 hgrn_onorm=donorm, hgrn_lb=dlb, sinks=dsinks, sq=sq)
    return grad_x, gw, parts


def _position():
    return lax.axis_index("x"), lax.axis_index("y"), lax.axis_index("c")


def _peer(pos, k):
    x, y, c = pos
    return (1 - x if k & 4 else x, 1 - y if k & 2 else y, 1 - c if k & 1 else c)


def _linear(pos):
    x, y, c = pos
    return 4 * x + 2 * y + c


HBM_SPEC = pl.BlockSpec(memory_space=pltpu.HBM)
SEM_SPEC = pl.BlockSpec(memory_space=pltpu.SEMAPHORE)
DATAFLOW = pltpu.SideEffectType.DATAFLOW_SIDE_EFFECTING
SEND_ORDER = (1, 2, 4, 3, 5, 6, 7)


def _in_hbm(a):
    return pltpu.with_memory_space_constraint(a, pltpu.HBM)


def _prepare_weights(shards, *, name, dep=None):
    n = len(shards)
    deps = [] if dep is None else [dep]

    def body(*refs):
        ins, (outs, lands, sem) = refs[:n], (refs[-2 * n - 1:-n - 1], refs[-n - 1:-1], refs[-1])
        me_lin = _linear(_position())
        copies = []
        for a in range(n):
            r = ins[a].shape[0]
            outs[a][...] = ins[a][...].astype(BF16)
            copies.append(pltpu.make_async_copy(outs[a], lands[a].at[pl.ds(me_lin * r, r), :], sem.at[a]))
            copies[-1].start()
        for cp in copies:
            cp.wait()

    vmem = pl.BlockSpec(memory_space=pltpu.VMEM)
    res = pl.pallas_call(
        body, name=name,
        out_shape=tuple(jax.ShapeDtypeStruct(s.shape, BF16) for s in shards)
        + tuple(jax.ShapeDtypeStruct((N_DEV * s.shape[0], s.shape[1]), BF16) for s in shards),
        in_specs=[vmem] * n + [ANY_SPEC] * len(deps), out_specs=tuple([vmem] * n + [ANY_SPEC] * n),
        scratch_shapes=[pltpu.SemaphoreType.DMA((n,))], compiler_params=_params(),
    )(*shards, *deps)
    return res[:n], res[n:]


def _copies_start(arrays, plan, n, *, name):
    na = len(arrays)

    def body(*refs):
        ins, send_sems, recv_sems = refs[:na], refs[na], refs[na + 1]
        me = _position()
        for j in range(n):
            src, dst, peer, _ = plan(ins, me, j)
            pltpu.make_async_remote_copy(src_ref=src, dst_ref=dst, send_sem=send_sems.at[j], recv_sem=recv_sems.at[j],
                                         device_id=peer, device_id_type=MESH).start()

    return pl.pallas_call(
        body, name=name,
        out_shape=(pltpu.SemaphoreType.DMA((n,)), pltpu.SemaphoreType.DMA((n,)))
        + tuple(pltpu.HBM(a.shape, a.dtype) for a in arrays),
        in_specs=(HBM_SPEC,) * na, out_specs=(SEM_SPEC, SEM_SPEC) + (HBM_SPEC,) * na,
        input_output_aliases={i: 2 + i for i in range(na)},
        compiler_params=pltpu.CompilerParams(has_side_effects=DATAFLOW),
    )(*[_in_hbm(a) for a in arrays])


def _copies_wait(send_sems, recv_sems, arrays, plan, n, after, *, name):
    na = len(arrays)

    def body(*refs):
        ins, send_sems, recv_sems = refs[:na], refs[na], refs[na + 1]
        me = _position()
        for j in range(n):
            src, _, peer, landed = plan(ins, me, j)
            copy = pltpu.make_async_remote_copy(src_ref=src, dst_ref=landed, send_sem=send_sems.at[j],
                                                recv_sem=recv_sems.at[j], device_id=peer, device_id_type=MESH)
            copy.wait_send()
            copy.wait_recv()

    return pl.pallas_call(
        body, name=name, out_shape=tuple(pltpu.HBM(a.shape, a.dtype) for a in arrays),
        in_specs=(HBM_SPEC,) * na + (SEM_SPEC, SEM_SPEC, ANY_SPEC), out_specs=(HBM_SPEC,) * na,
        input_output_aliases={i: i for i in range(na)},
        compiler_params=pltpu.CompilerParams(has_side_effects=DATAFLOW),
    )(*arrays, send_sems, recv_sems, after)


SAME_CORE = (2, 4, 6)


class _TwoLevelGather:
    def __init__(self, shards, lands, *, name):
        n = self.n = len(shards)
        self.name = name
        first_peers = (1,) + SAME_CORE

        def rows(ref, pos):
            r = ref.shape[0] // N_DEV
            return ref.at[pl.ds(_linear(pos) * r, r), :]

        def first(refs, me, j):
            a, peer = j // 4, _peer(me, first_peers[j % 4])
            return refs[a], rows(refs[n + a], me), peer, rows(refs[n + a], peer)

        def second(refs, me, j):
            a, sibling = j // 3, _peer(me, 1)
            mine = rows(refs[a], _peer(me, SAME_CORE[j % 3]))
            return mine, mine, sibling, rows(refs[a], _peer(sibling, SAME_CORE[j % 3]))

        self._first, self._second = first, second
        self._flight = _copies_start(list(shards) + list(lands), first, 4 * n, name=name + "_send")
        self.dep = self._flight[2]

    def pass_on(self, after):
        send1, recv1, *arrays = self._flight
        arrays = _copies_wait(send1, recv1, arrays, self._first, 4 * self.n, after, name=self.name + "_recv")
        self._flight = _copies_start(list(arrays[self.n:]), self._second, 3 * self.n, name=self.name + "_pass")
        return self._flight[2]

    def finish(self, after):
        send2, recv2, *lands = self._flight
        return _copies_wait(send2, recv2, lands, self._second, 3 * self.n, after, name=self.name + "_pass_recv")


def _exchange_start(gs, *, name):
    n = len(gs)
    rows = [g.shape[0] // N_DEV for g in gs]
    lands = [lax.empty((N_DEV - 1, r, g.shape[1]), g.dtype) for g, r in zip(gs, rows)]

    def body(*refs):
        g_refs, land_refs = refs[:n], refs[n:2 * n]
        send_sems, recv_sems = refs[2 * n:3 * n], refs[3 * n:4 * n]
        me = _position()
        for a in range(n):
            for k in SEND_ORDER:
                peer = _peer(me, k)
                pltpu.make_async_remote_copy(
                    src_ref=g_refs[a].at[pl.ds(_linear(peer) * rows[a], rows[a]), :],
                    dst_ref=land_refs[a].at[k - 1],
                    send_sem=send_sems[a].at[k - 1], recv_sem=recv_sems[a].at[k - 1],
                    device_id=peer, device_id_type=MESH).start()

    res = pl.pallas_call(
        body, name=name,
        out_shape=tuple(pltpu.SemaphoreType.DMA((N_DEV - 1,)) for _ in range(2 * n))
        + tuple(pltpu.HBM(a.shape, a.dtype) for a in gs + lands),
        in_specs=(HBM_SPEC,) * (2 * n), out_specs=(SEM_SPEC,) * (2 * n) + (HBM_SPEC,) * (2 * n),
        input_output_aliases={i: 2 * n + i for i in range(2 * n)},
        compiler_params=pltpu.CompilerParams(has_side_effects=DATAFLOW),
    )(*[_in_hbm(a) for a in gs + lands])
    return [(res[a], res[n + a], res[2 * n + a], res[3 * n + a]) for a in range(n)]


def _exchange_wait(send_sems, recv_sems, g_thru, land_thru, after, *, name):
    r = land_thru.shape[1]

    def body(g_ref, land_ref, send_sems, recv_sems, after_ref, g_dead, got_ref):
        del after_ref, g_dead, got_ref
        me = _position()
        for k in SEND_ORDER:
            peer = _peer(me, k)
            copy = pltpu.make_async_remote_copy(
                src_ref=g_ref.at[pl.ds(_linear(peer) * r, r), :], dst_ref=land_ref.at[k - 1],
                send_sem=send_sems.at[k - 1], recv_sem=recv_sems.at[k - 1],
                device_id=peer, device_id_type=MESH)
            copy.wait_send()
            copy.wait_recv()

    return pl.pallas_call(
        body, name=name,
        out_shape=(pltpu.HBM(g_thru.shape, g_thru.dtype), pltpu.HBM(land_thru.shape, land_thru.dtype)),
        in_specs=(HBM_SPEC, HBM_SPEC, SEM_SPEC, SEM_SPEC, pl.BlockSpec(memory_space=pl.ANY)),
        out_specs=(HBM_SPEC, HBM_SPEC), input_output_aliases={0: 0, 1: 1},
        compiler_params=pltpu.CompilerParams(has_side_effects=DATAFLOW),
    )(g_thru, land_thru, send_sems, recv_sems, after)


ADAMW_TILE_ROWS = 256


def _adamw_math(w, g, m, v):
    m = B1 * m + (1.0 - B1) * g
    v = B2 * v + (1.0 - B2) * (g * g)
    delta = -LR * ((m / C1) / (jnp.sqrt(v / C2) + AEPS) + WD * w)
    return delta, m, v


def _sum_adamw(items, *, name):
    n = len(items)
    r, d = items[0][2].shape
    assert all(it[2].shape == (r, d) for it in items)
    rc = r // 2 if r > ADAMW_TILE_ROWS else r
    tiles = [(a, r0) for a in range(n) for r0 in range(0, r, rc)]
    n_in, n_out = 5, 4

    def body(*refs):
        ins, outs = refs[:n_in * n], refs[n_in * n:(n_in + n_out) * n]
        land_v, own_v, f32_v, sems = refs[(n_in + n_out) * n:]
        me_lin = _linear(_position())

        def loads(j):
            a, r0 = tiles[j]
            g_all, land, w, m, v = ins[n_in * a:n_in * a + n_in]
            rows = pl.ds(r0, rc)
            pairs = [(land.at[:, rows, :], land_v.at[j]), (g_all.at[pl.ds(me_lin * r + r0, rc), :], own_v.at[j]),
                     (w.at[rows, :], f32_v.at[j, 0]), (m.at[rows, :], f32_v.at[j, 1]), (v.at[rows, :], f32_v.at[j, 2])]
            return [pltpu.make_async_copy(src, dst, sems.at[j, i]) for i, (src, dst) in enumerate(pairs)]

        def stores(j):
            a, r0 = tiles[j]
            return [pltpu.make_async_copy(f32_v.at[j, 3 + i], outs[n_out * a + i].at[pl.ds(r0, rc), :],
                                          sems.at[j, n_in + i]) for i in range(n_out)]

        for j in range(len(tiles)):
            for cp in loads(j):
                cp.start()
        for j in range(len(tiles)):
            for cp in loads(j):
                cp.wait()
            g = land_v[j, 0].astype(F32)
            for s in range(1, N_DEV - 1):
                g = g + land_v[j, s].astype(F32)
            g = own_v[j].astype(F32) + g
            f32_v[j, 3] = g
            f32_v[j, 4], f32_v[j, 5], f32_v[j, 6] = _adamw_math(f32_v[j, 0], g, f32_v[j, 1], f32_v[j, 2])
            for cp in stores(j):
                cp.start()
        for j in range(len(tiles)):
            for cp in stores(j):
                cp.wait()

    nt = len(tiles)
    res = pl.pallas_call(
        body, name=name,
        out_shape=tuple(jax.ShapeDtypeStruct((r, d), F32) for _ in range(n_out * n)),
        in_specs=[ANY_SPEC] * (n_in * n), out_specs=(ANY_SPEC,) * (n_out * n),
        scratch_shapes=[pltpu.VMEM((nt, N_DEV - 1, rc, d), BF16), pltpu.VMEM((nt, rc, d), BF16),
                        pltpu.VMEM((nt, 3 + n_out, rc, d), F32), pltpu.SemaphoreType.DMA((nt, n_in + n_out))],
        compiler_params=_params(),
    )(*[a for it in items for a in it])
    return [res[n_out * a:n_out * a + n_out] for a in range(n)]


SMALL = ("g_mix_pre", "g_mix_post", "g_mem", "g_x_pre", "g_x_post", "g_ffn_pre", "g_ffn_post",
         "hgrn_onorm", "hgrn_lb", "sinks")
SMALL_W = dict(hgrn_onorm=HD, hgrn_lb=HG_W, sinks=8)
SQ_ROW = len(SMALL)
PACK_ROWS = 16


def _small_pack(parts):
    ns = len(SMALL)

    def body(*refs):
        part, mine, slots, sem = refs[:ns + 1], refs[ns + 1], refs[ns + 2], refs[ns + 3]
        mine[...] = jnp.zeros((PACK_ROWS, D), F32)
        for r, name in enumerate(SMALL):
            wd = SMALL_W.get(name, D)
            mine[r:r + 1, 0:wd] = jnp.sum(part[r][...], axis=0, keepdims=True)[:, 0:wd]
        sq = jnp.sum(part[ns][...]) * (0.5 / D)
        mine[SQ_ROW:SQ_ROW + 1, :] = jnp.full((1, D), sq, F32)
        own = pltpu.make_async_copy(mine, slots.at[_linear(_position())], sem)
        own.start()
        own.wait()

    vmem = pl.BlockSpec(memory_space=pltpu.VMEM)
    return pl.pallas_call(
        body, name="small_pack",
        out_shape=(jax.ShapeDtypeStruct((PACK_ROWS, D), F32), jax.ShapeDtypeStruct((N_DEV, PACK_ROWS, D), F32)),
        in_specs=[vmem] * (ns + 1), out_specs=(vmem, ANY_SPEC),
        scratch_shapes=[pltpu.SemaphoreType.DMA(())], compiler_params=_params(),
    )(*[parts[n] for n in SMALL], parts["sq"])


def _small_exchange(mine, slots):
    def plan(refs, me, j):
        peer = _peer(me, j + 1)
        return refs[0], refs[1].at[_linear(me)], peer, refs[1].at[_linear(peer)]

    send, recv, mine1, slots1 = _copies_start([mine, slots], plan, N_DEV - 1, name="small_send")
    return lambda after: _copies_wait(send, recv, [mine1, slots1], plan, N_DEV - 1, after, name="small_recv")[1]


def _small_update(slots, sm, m_sm, v_sm):
    ns = len(SMALL)

    def body(*refs):
        tot = refs[0][0]
        for s in range(1, N_DEV):
            tot = tot + refs[0][s]
        w_refs, m_refs, v_refs = refs[1:ns + 1], refs[ns + 1:2 * ns + 1], refs[2 * ns + 1:3 * ns + 1]
        outs = refs[3 * ns + 1:]
        loss_ref = outs[0]
        g_out, d_out = outs[1:ns + 1], outs[ns + 1:2 * ns + 1]
        nm_out, nv_out = outs[2 * ns + 1:3 * ns + 1], outs[3 * ns + 1:4 * ns + 1]
        loss_ref[...] = tot[SQ_ROW:SQ_ROW + 1, 0:1]
        for r, name in enumerate(SMALL):
            wd = SMALL_W.get(name, D)
            g = tot[r:r + 1, 0:wd]
            w = w_refs[r][...]
            if name == "hgrn_lb":
                mx = jnp.maximum(w[0:1], w[1:2])
                e0, e1 = jnp.exp(w[0:1] - mx), jnp.exp(w[1:2] - mx)
                lb0 = e0 / (e0 + e1)
                g0 = g * lb0 * (1.0 - lb0)
                for i, gi in enumerate((g0, -g0)):
                    d, nm, nv = _adamw_math(w[i:i + 1], gi, m_refs[r][i:i + 1, :], v_refs[r][i:i + 1, :])
                    g_out[r][i:i + 1, :] = gi
                    d_out[r][i:i + 1, :], nm_out[r][i:i + 1, :], nv_out[r][i:i + 1, :] = d, nm, nv
            else:
                d, nm, nv = _adamw_math(w, g, m_refs[r][...], v_refs[r][...])
                g_out[r][...] = g
                d_out[r][...], nm_out[r][...], nv_out[r][...] = d, nm, nv

    shapes = [jax.ShapeDtypeStruct(sm[n].shape, F32) for n in SMALL]
    res = pl.pallas_call(
        body, name="small_update", out_shape=tuple([jax.ShapeDtypeStruct((1, 1), F32)] + shapes * 4),
        compiler_params=_params(),
    )(slots, *[sm[n] for n in SMALL], *[m_sm[n] for n in SMALL], *[v_sm[n] for n in SMALL])
    groups = [dict(zip(SMALL, res[1 + i * ns:1 + (i + 1) * ns])) for i in range(4)]
    return res[0], groups[0], groups[1], groups[2], groups[3]


BIG = ("w_in", "w_gate", "w_up", "w_down", "w_out", "wq_x", "wk_x", "wv_x", "wo_x")
BIG_KEY = dict(w_in="winT", w_gate="wgT", w_up="wuT", w_down="wd", w_out="wout", wq_x="wq", wk_x="wk",
               wv_x="wv", wo_x="wo")
TRANSPOSED = ("w_in", "w_gate", "w_up")
WEIGHTS = ("w_in", "sinks", "hgrn_lb", "hgrn_onorm", "w_out", "g_mix_pre", "g_mix_post", "g_mem", "g_x_pre",
           "g_x_post", "wq_x", "wk_x", "wv_x", "wo_x", "g_ffn_pre", "g_ffn_post", "w_gate", "w_up", "w_down")


def kernel(x, mem, w_in, sinks, hgrn_lb, hgrn_onorm, w_out, g_mix_pre, g_mix_post, g_mem, g_x_pre, g_x_post, wq_x, wk_x, wv_x, wo_x, g_ffn_pre, g_ffn_post, w_gate, w_up, w_down, loss_target, m_w_in, m_sinks, m_hgrn_lb, m_hgrn_onorm, m_w_out, m_g_mix_pre, m_g_mix_post, m_g_mem, m_g_x_pre, m_g_x_post, m_wq_x, m_wk_x, m_wv_x, m_wo_x, m_g_ffn_pre, m_g_ffn_post, m_w_gate, m_w_up, m_w_down, v_w_in, v_sinks, v_hgrn_lb, v_hgrn_onorm, v_w_out, v_g_mix_pre, v_g_mix_post, v_g_mem, v_g_x_pre, v_g_x_post, v_wq_x, v_wk_x, v_wv_x, v_wo_x, v_g_ffn_pre, v_g_ffn_post, v_w_gate, v_w_up, v_w_down):
    given = dict(locals())
    wts = {n: given[n] for n in WEIGHTS}
    ms = {n: given["m_" + n] for n in WEIGHTS}
    vs = {n: given["v_" + n] for n in WEIGHTS}

    def mat(a, name):
        a = a[0]
        return a.T if name in TRANSPOSED else a

    groups = (("w_in",), ("w_out", "wq_x", "wk_x", "wv_x", "wo_x"), ("w_gate", "w_up", "w_down"))
    gathers = []

    def start_group(g, dep):
        tag = ("w_in", "w_attn", "w_ffn")[g]
        shards, lands = _prepare_weights([mat(wts[n], n) for n in groups[g]], name="prepare_" + tag, dep=dep)
        gathers.append(_TwoLevelGather(shards, lands, name=tag))
        return gathers[-1].dep

    first_dep = start_group(1, start_group(0, None))
    name_of = {k: n for n, k in BIG_KEY.items()}
    gathered = {}

    def milestone(tag, value):
        if tag == "z":
            return start_group(2, value)
        return gathers[{"swa": 1, "kv": 2}[tag]].pass_on(value)

    def fetch(key, after):
        name = name_of[key]
        if name not in gathered:
            g = [i for i, group in enumerate(groups) if name in group][0]
            if g == 0:
                gathers[0].pass_on(after)
            gathered.update(zip(groups[g], gathers[g].finish(after)))
        return gathered[name]

    sm = {n: wts[n] for n in SMALL}
    started, held = {}, {}
    send_with = {k: group for group in (("wgT", "wuT"), ("wo", "wq", "wk", "wv")) for k in group}

    def emit(key, g):
        held[key] = g
        group = send_with.get(key, (key,))
        if key != group[-1]:
            return None
        flights = _exchange_start([held[k] for k in group], name="grad_send_" + name_of[group[0]])
        started.update({name_of[k]: f for k, f in zip(group, flights)})
        return flights[-1][2]

    grad_x, _, parts = _local_step(x[0], mem[0], loss_target[0], fetch, sm, emit, first_dep=first_dep, milestone=milestone)
    small_finish = _small_exchange(*_small_pack(parts))
    grads, deltas, new_m, new_v = {}, {}, {}, {}
    after = grad_x
    for group in (("w_down",), ("w_gate", "w_up"), ("wo_x", "wq_x", "wk_x", "wv_x", "w_out"), ("w_in",)):
        items = []
        for n in group:
            g_all, land = _exchange_wait(*started[n], after, name="grad_recv_" + n)
            items.append((g_all, land, mat(wts[n], n), mat(ms[n], n), mat(vs[n], n)))
            after = land
        for n, res in zip(group, _sum_adamw(items, name="adamw_" + group[0])):
            after = res[1]
            if n in TRANSPOSED:
                res = [a.T for a in res]
            grads[n], deltas[n], new_m[n], new_v[n] = [a[None] for a in res]
    loss, g_s, d_s, m_s, v_s = _small_update(small_finish(after), sm, {n: ms[n] for n in SMALL},
                                             {n: vs[n] for n in SMALL})
    grads.update(g_s), deltas.update(d_s), new_m.update(m_s), new_v.update(v_s)
    return (loss[0, 0], grad_x[None], *[grads[n] for n in WEIGHTS], *[deltas[n] for n in WEIGHTS],
            *[new_m[n] for n in WEIGHTS], *[new_v[n] for n in WEIGHTS])
```

```python
import functools

import jax
import jax.numpy as jnp
from jax import lax
from jax.experimental import pallas as pl
from jax.experimental.pallas import tpu as pltpu

F32 = jnp.float32
BF16 = jnp.bfloat16

D = 1024
D_IN = 2816
D_FF = 2816
CHUNK = 64
SWA_W = 512
KV_W = 128
HG_W = 512
HD = 128
ZQH, ZFH, ZIH, ZGH = 768, 1280, 1792, 2304
XH, XD = 4, 256
EPS = 1e-6
NEG = -1e30
N_DEV = 8
MESH = pl.DeviceIdType.MESH

LR, B1, B2, AEPS, WD, STEP = 0.001, 0.9, 0.999, 1e-08, 0.01, 10
C1 = 1.0 - B1 ** STEP
C2 = 1.0 - B2 ** STEP

VMEM_LIMIT = 56 * 1024 * 1024


def _params(**kw):
    return pltpu.CompilerParams(vmem_limit_bytes=VMEM_LIMIT, **kw)


def _sig(x):
    return 1.0 / (1.0 + jnp.exp(-x))


def _rowsum8(x):
    r, w = x.shape
    return jnp.sum(x.reshape(r // 8, 8, w), axis=0)


def _dot(a, b, ca, cb, precision=None):
    return lax.dot_general(a, b, (((ca,), (cb,)), ((), ())), preferred_element_type=F32,
                           precision=precision)


ANY_SPEC = pl.BlockSpec(memory_space=pl.ANY)


def _mm(a, b, *, ta=False, tb=False, out_dtype, tm, tn, tk=None, name, dep=None, n_outer=False):
    m = a.shape[1] if ta else a.shape[0]
    k = a.shape[0] if ta else a.shape[1]
    n = b.shape[0] if tb else b.shape[1]
    tm, tn = min(tm, m), min(tn, n)
    tk = k if tk is None else min(tk, k)
    nk = k // tk
    assert m % tm == 0 and n % tn == 0 and k % tk == 0, (name, m, n, k, tm, tn, tk)
    ij = (lambda g0, g1: (g1, g0)) if n_outer else (lambda g0, g1: (g0, g1))
    a_spec = (pl.BlockSpec((tk, tm), lambda g0, g1, kk: (kk, ij(g0, g1)[0])) if ta
              else pl.BlockSpec((tm, tk), lambda g0, g1, kk: (ij(g0, g1)[0], kk)))
    b_spec = (pl.BlockSpec((tn, tk), lambda g0, g1, kk: (ij(g0, g1)[1], kk)) if tb
              else pl.BlockSpec((tk, tn), lambda g0, g1, kk: (kk, ij(g0, g1)[1])))
    ca, cb = (0 if ta else 1), (1 if tb else 0)

    deps = [] if dep is None else [dep]

    def body(a_ref, b_ref, *rest):
        o_ref, acc = rest[len(deps)], rest[len(deps) + 1:]
        p = _dot(a_ref[...].astype(BF16), b_ref[...].astype(BF16), ca, cb)
        if nk == 1:
            o_ref[...] = p.astype(out_dtype)
        else:
            acc_ref, = acc
            kk = pl.program_id(2)

            @pl.when(kk == 0)
            def _():
                acc_ref[...] = p

            @pl.when(kk > 0)
            def _():
                acc_ref[...] += p

            @pl.when(kk == nk - 1)
            def _():
                o_ref[...] = acc_ref[...].astype(out_dtype)

    return pl.pallas_call(
        body, name=name, out_shape=jax.ShapeDtypeStruct((m, n), out_dtype),
        grid=(n // tn, m // tm, nk) if n_outer else (m // tm, n // tn, nk),
        in_specs=[a_spec, b_spec] + [ANY_SPEC] * len(deps),
        out_specs=pl.BlockSpec((tm, tn), lambda g0, g1, kk: ij(g0, g1)),
        scratch_shapes=[pltpu.VMEM((tm, tn), F32)] if nk > 1 else [],
        compiler_params=_params(dimension_semantics=("parallel", "parallel", "arbitrary")),
    )(a, b, *deps)


TN_FIRST = 256
TN_REST = 1152
TN_SLICES = 4


def _mm_tn(a, b, *, name, dep=None):
    (k, m), n = a.shape, b.shape[1]
    assert a.dtype == BF16 and b.dtype == BF16 and b.shape[0] == k
    widths = [TN_FIRST, TN_FIRST]
    while sum(widths) < m:
        widths.append(min(TN_REST, m - sum(widths)))
    starts = [sum(widths[:i]) for i in range(len(widths))]
    assert sum(widths) == m
    nb = len(widths)
    ks = k // TN_SLICES
    ahead = 2
    deps = [] if dep is None else [dep]

    def body(a_hbm, b_hbm, *rest):
        o_hbm, b_v, sems = rest[len(deps)], rest[len(deps) + 1], rest[-1]
        a_v, o_v = rest[len(deps) + 2:len(deps) + 2 + nb], rest[len(deps) + 2 + nb:-1]
        sliced = []
        for c in range(TN_SLICES):
            rows = pl.ds(c * ks, ks)
            sliced.append((pltpu.make_async_copy(b_hbm.at[rows, :], b_v.at[rows, :], sems.at[2 * c]),
                           pltpu.make_async_copy(a_hbm.at[rows, pl.ds(0, widths[0])], a_v[0].at[rows, :],
                                                 sems.at[2 * c + 1])))
        base = 2 * TN_SLICES - 1
        loads = [None] + [pltpu.make_async_copy(a_hbm.at[:, pl.ds(c0, cw)], a_v[i], sems.at[base + i])
                          for i, (c0, cw) in enumerate(zip(starts, widths)) if i > 0]
        stores = [pltpu.make_async_copy(o_v[i], o_hbm.at[pl.ds(c0, cw), :], sems.at[base + nb + i])
                  for i, (c0, cw) in enumerate(zip(starts, widths))]
        for pair in sliced:
            for cp in pair:
                cp.start()
        for i in range(1, 1 + ahead):
            loads[i].start()
        acc = None
        for c, pair in enumerate(sliced):
            for cp in pair:
                cp.wait()
            p = _dot(a_v[0][c * ks:(c + 1) * ks, :], b_v[c * ks:(c + 1) * ks, :], 0, 0)
            acc = p if acc is None else acc + p
        o_v[0][...] = acc.astype(BF16)
        stores[0].start()
        for i in range(1, nb):
            loads[i].wait()
            if i + ahead < nb:
                loads[i + ahead].start()
            o_v[i][...] = _dot(a_v[i][...], b_v[...], 0, 0).astype(BF16)
            stores[i].start()
        for cp in stores:
            cp.wait()

    return pl.pallas_call(
        body, name=name, out_shape=jax.ShapeDtypeStruct((m, n), BF16),
        in_specs=[ANY_SPEC] * (2 + len(deps)), out_specs=ANY_SPEC,
        scratch_shapes=[pltpu.VMEM((k, n), BF16)] + [pltpu.VMEM((k, cw), BF16) for cw in widths]
        + [pltpu.VMEM((cw, n), BF16) for cw in widths] + [pltpu.SemaphoreType.DMA((2 * TN_SLICES - 1 + 2 * nb,))],
        compiler_params=_params(),
    )(a, b, *deps)


def _mm_rows(prods, rows_in, vecs_in, epilogue, outs, *, tm, name, dep=None):
    m = prods[0][0].shape[0]
    n = prods[0][1].shape[0] if prods[0][2] else prods[0][1].shape[1]
    tm = min(tm, m)
    assert m % tm == 0
    deps = [] if dep is None else [dep]
    n_p, n_r, n_v = len(prods), len(rows_in), len(vecs_in)

    def body(*refs):
        ab = refs[:2 * n_p]
        row_refs = refs[2 * n_p:2 * n_p + n_r]
        vec_refs = refs[2 * n_p + n_r:2 * n_p + n_r + n_v]
        out_refs = refs[2 * n_p + n_r + n_v + len(deps):]
        p = None
        for j, (_, _, tb) in enumerate(prods):
            t = _dot(ab[2 * j][...].astype(BF16), ab[2 * j + 1][...], 1, 1 if tb else 0)
            p = t if p is None else p + t
        vals = epilogue(p, *[r[...] for r in row_refs], *[v[...] for v in vec_refs])
        for (dtype, kind), o_ref, val in zip(outs, out_refs, vals):
            if kind == "row":
                o_ref[...] = val.astype(dtype)
            else:
                @pl.when(pl.program_id(0) == 0)
                def _(o_ref=o_ref):
                    o_ref[...] = jnp.zeros_like(o_ref)

                o_ref[...] += val

    row = lambda w: pl.BlockSpec((tm, w), lambda i: (i, 0))
    whole = lambda a: pl.BlockSpec(a.shape, lambda i: (0,) * a.ndim, pipeline_mode=pl.Buffered(1))
    in_specs, args = [], []
    for a, b, _ in prods:
        in_specs += [row(a.shape[1]), whole(b)]
        args += [a, b]
    in_specs += [row(r.shape[1]) for r in rows_in] + [whole(v) for v in vecs_in] + [ANY_SPEC] * len(deps)
    return pl.pallas_call(
        body, name=name,
        out_shape=tuple(jax.ShapeDtypeStruct((m, n) if kind == "row" else (8, n), dtype) for dtype, kind in outs),
        grid=(m // tm,), in_specs=in_specs,
        out_specs=tuple(row(n) if kind == "row" else pl.BlockSpec((8, n), lambda i: (0, 0)) for _, kind in outs),
        compiler_params=_params(dimension_semantics=("arbitrary",)),
    )(*args, *rows_in, *vecs_in, *deps)


def _rstd(x):
    return lax.rsqrt(jnp.mean(x * x, axis=-1, keepdims=True) + EPS)


def _norm_bwd(xh, r, t):
    return r * (t - xh * jnp.mean(xh * t, axis=-1, keepdims=True))


ROW_F32, ROW_BF16, SUM_F32 = (F32, "row"), (BF16, "row"), (F32, "sum")


def _then(epilogue, index, tb):
    def run(p, *args):
        vals = epilogue(p, *args[:-1])
        return (*vals, _dot(vals[index].astype(BF16), args[-1], 1, 1 if tb else 0))

    return run


def _ep_post_pre(p, h, g_post, g_pre):
    y = p.astype(BF16)
    yf = y.astype(F32)
    hn = h + yf * _rstd(yf) * g_post
    return y, hn, hn * _rstd(hn) * g_pre


_EP_POST_PRE_OUTS = [ROW_BF16, ROW_F32, ROW_BF16]


def _ep_final_loss(y, h, target, g_post):
    r = _rstd(y)
    yh = y * r
    err = h + yh * g_post - target
    dh = err * (1.0 / D)
    return _rowsum8(err * err), dh, _norm_bwd(yh, r, dh * g_post), _rowsum8(dh * yh)


def _ep_post_pre_bwd(du, dh_out, hn, y, g_post, g_pre):
    r2 = _rstd(hn)
    xh = hn * r2
    dh = dh_out + _norm_bwd(xh, r2, du * g_pre)
    yf = y.astype(F32)
    r1 = _rstd(yf)
    yh = yf * r1
    return dh, _norm_bwd(yh, r1, dh * g_post), _rowsum8(du * xh), _rowsum8(dh * yh)


_EP_POST_PRE_BWD_OUTS = [ROW_F32, ROW_BF16, SUM_F32, SUM_F32]


def _ep_pre_bwd(du, dh_out, x, g):
    r = _rstd(x)
    xh = x * r
    return dh_out + _norm_bwd(xh, r, du * g), _rowsum8(du * xh)


_EP_PRE_BWD_OUTS = [ROW_F32, SUM_F32]


def _prenorm(x, g, *, name, dep=None):
    t, d = x.shape
    tb = min(512, t)
    deps = [] if dep is None else [dep]

    def body(x_ref, g_ref, *rest):
        xf = x_ref[...]
        rest[-1][...] = (xf * _rstd(xf) * g_ref[...]).astype(BF16)

    return pl.pallas_call(
        body, name=name, out_shape=jax.ShapeDtypeStruct((t, d), BF16), grid=(t // tb,),
        in_specs=[pl.BlockSpec((tb, d), lambda i: (i, 0)), pl.BlockSpec((1, d), lambda i: (0, 0))]
        + [ANY_SPEC] * len(deps),
        out_specs=pl.BlockSpec((tb, d), lambda i: (i, 0)), compiler_params=_params(),
    )(x, g, *deps)


QB = 256


def _half_mask(shape, e):
    lane = lax.broadcasted_iota(jnp.int32, shape, len(shape) - 1)
    return (lane // 64) == e


def _place(kv):
    sw = pltpu.roll(kv, 64, 1)
    m0 = _half_mask(kv.shape, 0)
    return [[jnp.where(m0, kv, 0.0).astype(BF16), jnp.where(m0, 0.0, sw).astype(BF16)],
            [jnp.where(m0, sw, 0.0).astype(BF16), jnp.where(m0, 0.0, kv).astype(BF16)]]


SQ = 128
SK = 256


def _swa_valid(i, sb):
    qc = lax.broadcasted_iota(jnp.int32, (SQ, SK), 0) // CHUNK
    kc = lax.broadcasted_iota(jnp.int32, (SQ, SK), 1) // CHUNK - 2
    return (kc <= qc) & (qc <= kc + 2) & (4 * i + 2 * sb + kc >= 0)


def _swa_fwd(z, sinks, t, dep=None):
    nb = t // QB
    deps = [] if dep is None else [dep]

    def body(s_ref, q_ref, kp_ref, kc_ref, vp_ref, vc_ref, *rest):
        o_ref, lse_ref = rest[-2:]
        i = pl.program_id(0)
        kpl = _place(jnp.concatenate([kp_ref[...], kc_ref[...]], axis=0))
        vpl = _place(jnp.concatenate([vp_ref[...], vc_ref[...]], axis=0))
        lane = lax.broadcasted_iota(jnp.int32, (SQ, 128), 1)
        for sb in range(QB // SQ):
            rows, keys = slice(SQ * sb, SQ * (sb + 1)), slice(SQ * sb, SQ * sb + SK)
            valid = _swa_valid(i, sb)
            lse_out = jnp.zeros((SQ, 128), F32)
            for j in range(4):
                qp = q_ref[rows, 128 * j:128 * (j + 1)].astype(BF16)
                acc = jnp.zeros((SQ, 128), F32)
                for e in range(2):
                    h = 2 * j + e
                    kvh = h // 4
                    qm = jnp.where(_half_mask(qp.shape, e), qp, jnp.zeros_like(qp))
                    s = _dot(qm, kpl[kvh][e][keys], 1, 1) * 0.125
                    s = jnp.where(valid, s, NEG)
                    sink = s_ref[0, h]
                    m = jnp.maximum(jnp.max(s, axis=-1, keepdims=True), sink)
                    p = jnp.exp(s - m)
                    l = jnp.sum(p, axis=-1, keepdims=True) + jnp.exp(sink - m)
                    acc = acc + _dot(p.astype(BF16), vpl[kvh][e][keys], 1, 0) * (1.0 / l)
                    lse_out = jnp.where(lane == h, m + jnp.log(l), lse_out)
                o_ref[rows, 128 * j:128 * (j + 1)] = acc.astype(BF16)
            lse_ref[rows, :] = lse_out

    prev = lambda c: pl.BlockSpec((128, 128), lambda i: (jnp.maximum(2 * i - 1, 0), c))
    cur = lambda c: pl.BlockSpec((QB, 128), lambda i: (i, c))
    return pl.pallas_call(
        body, name="swa_fwd",
        out_shape=(jax.ShapeDtypeStruct((t, D), BF16), jax.ShapeDtypeStruct((t, 128), F32)),
        grid=(nb,),
        in_specs=[pl.BlockSpec(memory_space=pltpu.SMEM),
                  pl.BlockSpec((QB, SWA_W), lambda i: (i, 0)), prev(4), cur(4), prev(5), cur(5)]
        + [ANY_SPEC] * len(deps),
        out_specs=(pl.BlockSpec((QB, SWA_W), lambda i: (i, 0)), pl.BlockSpec((QB, 128), lambda i: (i, 0))),
        compiler_params=_params(),
    )(sinks, z, z, z, z, z, *deps)


def _swa_bwd(z, sinks, ymix, lse, dymix, t, dep=None):
    nb = t // QB
    deps = [] if dep is None else [dep]

    def body(s_ref, q_ref, kp_ref, kc_ref, vp_ref, vc_ref, o_ref, do_ref, l_ref, *rest):
        dq_ref, first_ref, second_ref, ds_ref, carry_ref = rest[len(deps):]
        i = pl.program_id(0)
        live = i < nb

        @pl.when(i == 0)
        def _():
            ds_ref[...] = jnp.zeros_like(ds_ref)
            carry_ref[...] = jnp.zeros_like(carry_ref)

        lane = lax.broadcasted_iota(jnp.int32, (8, 128), 1)
        kpl = _place(jnp.concatenate([kp_ref[...], kc_ref[...]], axis=0))
        vpl = _place(jnp.concatenate([vp_ref[...], vc_ref[...]], axis=0))
        nk = QB + 128
        qc = lax.broadcasted_iota(jnp.int32, (QB, nk), 0) // CHUNK
        kc = lax.broadcasted_iota(jnp.int32, (QB, nk), 1) // CHUNK - 2
        valid = (kc <= qc) & (qc <= kc + 2) & (4 * i + kc >= 0) & live
        lse_c = l_ref[...]
        dsink = jnp.zeros((8, 128), F32)
        dk_acc = [[jnp.zeros((128, nk), F32) for _ in range(2)] for _ in range(2)]
        dv_acc = [[jnp.zeros((128, nk), F32) for _ in range(2)] for _ in range(2)]
        dq = []
        for j in range(4):
            cols = slice(128 * j, 128 * (j + 1))
            qp = q_ref[:, cols].astype(BF16)
            dop = do_ref[:, cols]
            prod = dop.astype(F32) * o_ref[:, cols].astype(F32)
            acc = jnp.zeros((QB, 128), F32)
            for e in range(2):
                h = 2 * j + e
                kvh = h // 4
                hm = _half_mask(qp.shape, e)
                qm = jnp.where(hm, qp, jnp.zeros_like(qp))
                dom = jnp.where(hm, dop, jnp.zeros_like(dop))
                dd = jnp.sum(jnp.where(hm, prod, 0.0), axis=-1, keepdims=True)
                lse_h = lse_c[:, h:h + 1]
                s = _dot(qm, kpl[kvh][e], 1, 1) * 0.125
                p = jnp.where(valid, jnp.exp(s - lse_h), 0.0)
                dp = _dot(dom, vpl[kvh][e], 1, 1)
                ds = (p * (dp - dd) * 0.125).astype(BF16)
                acc = acc + _dot(ds, kpl[kvh][e], 1, 0)
                dk_acc[kvh][e] = dk_acc[kvh][e] + _dot(qm, ds, 0, 0)
                dv_acc[kvh][e] = dv_acc[kvh][e] + _dot(dom, p.astype(BF16), 0, 0)
                ps = jnp.where(live, jnp.exp(s_ref[0, h] - lse_h) * dd, 0.0)
                dsink = dsink - jnp.where(lane == h, _rowsum8(jnp.broadcast_to(ps, (QB, 128))), 0.0)
            dq.append(acc.astype(BF16))
        ds_ref[...] += dsink
        dk = (dk_acc[0][0] + dk_acc[1][1] + pltpu.roll(dk_acc[0][1] + dk_acc[1][0], 64, 0)).T
        dv = (dv_acc[0][0] + dv_acc[1][1] + pltpu.roll(dv_acc[0][1] + dv_acc[1][0], 64, 0)).T
        dkv = jnp.concatenate([dk, dv], axis=1)
        second_ref[...] = (carry_ref[...] + dkv[0:128]).astype(BF16)
        carry_ref[...] = dkv[256:384]

        @pl.when(live)
        def _():
            for j in range(4):
                dq_ref[:, 128 * j:128 * (j + 1)] = dq[j]
            first_ref[...] = dkv[128:256].astype(BF16)

    blk = lambda i: jnp.minimum(i, nb - 1)
    prev = lambda c: pl.BlockSpec((128, 128), lambda i: (jnp.maximum(2 * blk(i) - 1, 0), c))
    cur = lambda w, c: pl.BlockSpec((QB, w), lambda i: (blk(i), c))
    half = lambda index: pl.BlockSpec((128, 256), lambda i: (index(i), 0))
    return pl.pallas_call(
        body, name="swa_bwd",
        out_shape=(jax.ShapeDtypeStruct((t, SWA_W), BF16), jax.ShapeDtypeStruct((t // 2, 256), BF16),
                   jax.ShapeDtypeStruct((t // 2, 256), BF16), jax.ShapeDtypeStruct((8, 128), F32)),
        grid=(nb + 1,),
        in_specs=[pl.BlockSpec(memory_space=pltpu.SMEM),
                  cur(SWA_W, 0), prev(4), cur(128, 4), prev(5), cur(128, 5),
                  cur(SWA_W, 0), cur(SWA_W, 0), cur(128, 0)] + [ANY_SPEC] * len(deps),
        out_specs=(cur(SWA_W, 0), half(blk), half(lambda i: jnp.maximum(i - 1, 0)),
                   pl.BlockSpec((8, 128), lambda i: (0, 0))),
        scratch_shapes=[pltpu.VMEM((128, 256), F32)],
        compiler_params=_params(dimension_semantics=("arbitrary",)),
    )(sinks, z, z, z, z, z, ymix, dymix, lse, *deps)


HB = 256


def _lower_bound(lb_ref):
    a = lb_ref[...]
    a0, a1 = a[0:1], a[1:2]
    mx = jnp.maximum(a0, a1)
    e0, e1 = jnp.exp(a0 - mx), jnp.exp(a1 - mx)
    return e0 / (e0 + e1)


def _hgrn_cols(row_block):
    return [pl.BlockSpec((HB, 2 * HD), lambda j, c=base // (2 * HD) + p: (row_block(j), c))
            for base in (ZQH, ZFH, ZIH, ZGH) for p in range(2)]


NCH = HB // CHUNK


def _split3(x):
    hi = x.astype(BF16)
    r1 = x - hi.astype(F32)
    mid = r1.astype(BF16)
    return hi, mid, (r1 - mid.astype(F32)).astype(BF16)


def _blockdiag(lower):
    r = lax.broadcasted_iota(jnp.int32, (HB, HB), 0)
    c = lax.broadcasted_iota(jnp.int32, (HB, HB), 1)
    return (r // CHUNK == c // CHUNK) & ((c <= r) if lower else (c >= r))


def _chunk_sums(mask_bf16, x):
    return sum(_dot(mask_bf16, part, 1, 0) for part in _split3(x))


def _per_chunk_rows(x, row):
    w = x.shape[1]
    picked = x.reshape(NCH, CHUNK, w)[:, row:row + 1, :]
    return jnp.broadcast_to(picked, (NCH, CHUNK, w)).reshape(HB, w)


def _chunk_stack(x, chunk_of_row):
    return jnp.concatenate([jnp.where(chunk_of_row == c, x, jnp.zeros_like(x)) for c in range(NCH)], axis=1)


def _chunk_pick(x, chunk_of_row):
    w = x.shape[1] // NCH
    out = jnp.zeros((HB, w), x.dtype)
    for c in range(NCH):
        out = jnp.where(chunk_of_row == c, x[:, c * w:(c + 1) * w], out)
    return out


def _hgrn_local(q, f, kf, b):
    sq = _sig(q)
    qf = q * sq * (HD ** -0.5)
    b_mid = _per_chunk_rows(b, CHUNK // 2 - 1)
    b_last = _per_chunk_rows(b, CHUNK - 1)
    qm = qf * jnp.exp(b - b_mid)
    km = kf * jnp.exp(b_mid - b)
    kl = kf * jnp.exp(b_last - b)
    qb = qf * jnp.exp(b)
    return dict(sq=sq, b_mid=b_mid, b_last=b_last, qm=qm, km=km, kl=kl, qb=qb)


def _hgrn2_fwd(z, hgrn_lb, onorm, ymix, t, dep=None):
    nb = t // HB
    deps = [] if dep is None else [dep]

    def body(*refs):
        zq, zf, zi, zg = refs[0:2], refs[2:4], refs[4:6], refs[6:8]
        (lb_ref, on_ref), (y_ref, o_ref, sp_ref, st_ref) = refs[8:10], refs[-4:]

        @pl.when(pl.program_id(0) == 0)
        def _():
            st_ref[...] = jnp.zeros_like(st_ref)

        lb_all = _lower_bound(lb_ref)
        gn = on_ref[...]
        low = _blockdiag(True)
        low_b = low.astype(BF16)
        chunk_of_row = lax.broadcasted_iota(jnp.int32, (HB, HD), 0) // CHUNK
        for p in range(2):
            lbp = lb_all[:, 2 * HD * p:2 * HD * (p + 1)]
            fp = lbp + (1.0 - lbp) * _sig(zf[p][...])
            bp = _chunk_sums(low_b, jnp.log(fp))
            for e in range(2):
                h, ls = 2 * p + e, slice(e * HD, (e + 1) * HD)
                f = fp[:, ls]
                w = _hgrn_local(zq[p][:, ls], f, 1.0 - f, bp[:, ls])
                iv = zi[p][:, ls].astype(BF16)
                a = jnp.where(low, _dot(w["qm"].astype(BF16), w["km"].astype(BF16), 1, 1), 0.0)
                o = _dot(a.astype(BF16), iv, 1, 0)
                u = _dot(iv, _chunk_stack(w["kl"].astype(BF16), chunk_of_row), 0, 0)
                decay = jnp.exp(w["b_last"])
                st = st_ref[h]
                states = []
                for c in range(NCH):
                    sp_ref[h, c] = st
                    states.append(st.astype(BF16))
                    st = st * decay[c * CHUNK:c * CHUNK + 1] + u[:, c * HD:(c + 1) * HD]
                st_ref[h] = st
                inter = _dot(w["qb"].astype(BF16), jnp.concatenate(states, axis=0), 1, 1)
                o = o + _chunk_pick(inter, chunk_of_row)
                hs = slice(h * HD, (h + 1) * HD)
                o_ref[:, hs] = o
                gg = zg[p][:, ls]
                y_ref[:, hs] = (o * _rstd(o) * gn * (gg * _sig(gg))).astype(BF16)

    return pl.pallas_call(
        body, name="hgrn_fwd",
        out_shape=(jax.ShapeDtypeStruct((t, D), BF16), jax.ShapeDtypeStruct((t, HG_W), F32),
                   jax.ShapeDtypeStruct((4, t // CHUNK, HD, HD), F32)),
        grid=(nb,),
        in_specs=_hgrn_cols(lambda j: j) + [pl.BlockSpec((2, HG_W), lambda j: (0, 0)),
                                            pl.BlockSpec((1, HD), lambda j: (0, 0)), ANY_SPEC]
        + [ANY_SPEC] * len(deps),
        out_specs=(pl.BlockSpec((HB, HG_W), lambda j: (j, 1)),
                   pl.BlockSpec((HB, HG_W), lambda j: (j, 0)),
                   pl.BlockSpec((4, NCH, HD, HD), lambda j: (0, j, 0, 0))),
        scratch_shapes=[pltpu.VMEM((4, HD, HD), F32)],
        input_output_aliases={10: 0},
        compiler_params=_params(dimension_semantics=("arbitrary",)),
    )(*[z] * 8, hgrn_lb, onorm, ymix, *deps)


def _hgrn2_bwd(z, hgrn_lb, onorm, o_save, sprev, dymix, dza, t):
    nb = t // HB

    def body(*refs):
        zq, zf, zi, zg = refs[0:2], refs[2:4], refs[4:6], refs[6:8]
        (lb_ref, on_ref, o_ref, sp_ref, dy_ref, dqa_ref, first_ref, second_ref,
         dz_ref, dlb_ref, don_ref, dst_ref) = refs[8:]

        @pl.when(pl.program_id(0) == 0)
        def _():
            dst_ref[...] = jnp.zeros_like(dst_ref)
            dlb_ref[...] = jnp.zeros_like(dlb_ref)
            don_ref[...] = jnp.zeros_like(don_ref)

        dz_ref[:, 0:SWA_W] = dqa_ref[...]
        dz_ref[0:HB // 2, SWA_W:ZQH] = first_ref[...]
        dz_ref[HB // 2:HB, SWA_W:ZQH] = second_ref[...]
        lb_all = _lower_bound(lb_ref)
        gn = on_ref[...]
        low, upp = _blockdiag(True), _blockdiag(False)
        upp_b = upp.astype(BF16)
        low_b = low.astype(BF16)
        row = lax.broadcasted_iota(jnp.int32, (HB, HD), 0)
        chunk_of_row = row // CHUNK
        in_chunk = row % CHUNK
        for p in range(2):
            lbp = lb_all[:, 2 * HD * p:2 * HD * (p + 1)]
            sgp = _sig(zf[p][...])
            fp = lbp + (1.0 - lbp) * sgp
            bp = _chunk_sums(low_b, jnp.log(fp))
            db_pair, dkf_pair = [], []
            for e in range(2):
                h, ls, hs = 2 * p + e, slice(e * HD, (e + 1) * HD), slice((2 * p + e) * HD, (2 * p + e + 1) * HD)
                f = fp[:, ls]
                q = zq[p][:, ls]
                w = _hgrn_local(q, f, 1.0 - f, bp[:, ls])
                iv = zi[p][:, ls].astype(BF16)
                gg = zg[p][:, ls]
                o = o_ref[:, hs]
                dout = dy_ref[:, hs].astype(F32)
                sgg = _sig(gg)
                r = _rstd(o)
                oh = o * r
                dyn = dout * (gg * sgg)
                dz_ref[:, ZGH + h * HD:ZGH + (h + 1) * HD] = (
                    dout * oh * gn * (sgg * (1.0 + gg * (1.0 - sgg)))).astype(BF16)
                don_ref[...] += _rowsum8(dyn * oh)
                do = _norm_bwd(oh, r, dyn * gn).astype(BF16)
                qm, km, kl, qb = (w[n].astype(BF16) for n in ("qm", "km", "kl", "qb"))
                decay = jnp.exp(w["b_last"])
                grads_in = _dot(do, _chunk_stack(qb, chunk_of_row), 0, 0)
                dst = dst_ref[h]
                dstn, dd_rows = [None] * NCH, [None] * NCH
                for c in reversed(range(NCH)):
                    dstn[c] = dst.astype(BF16)
                    dd_rows[c] = jnp.sum(dst * sp_ref[h, c], axis=0, keepdims=True)
                    dst = dst * decay[c * CHUNK:c * CHUNK + 1] + grads_in[:, c * HD:(c + 1) * HD]
                dst_ref[h] = dst
                states = jnp.concatenate([sp_ref[h, c].astype(BF16) for c in range(NCH)], axis=0)
                dstn_all = jnp.concatenate(dstn, axis=0)
                dqb = _dot(_chunk_stack(do, chunk_of_row), states, 1, 0)
                at = jnp.where(upp, _dot(km, qm, 1, 1), 0.0)
                di = _dot(at.astype(BF16), do, 1, 0) + _chunk_pick(_dot(kl, dstn_all, 1, 1), chunk_of_row)
                dz_ref[:, ZIH + h * HD:ZIH + (h + 1) * HD] = di.astype(BF16)
                dkl = _dot(_chunk_stack(iv, chunk_of_row), dstn_all, 1, 0)
                da = jnp.where(low, _dot(do, iv, 1, 1), 0.0).astype(BF16)
                dat = jnp.where(upp, _dot(iv, do, 1, 1), 0.0).astype(BF16)
                dqm = _dot(da, km, 1, 0)
                dkm = _dot(dat, qm, 1, 0)
                b = bp[:, ls]
                e1, e2 = jnp.exp(b - w["b_mid"]), jnp.exp(w["b_mid"] - b)
                e3, e4 = jnp.exp(w["b_last"] - b), jnp.exp(b)
                dqf = dqm * e1 + dqb * e4
                dkf_pair.append(dkm * e2 + dkl * e3)
                t_qm, t_km, t_kl = dqm * w["qm"], dkm * w["km"], dkl * w["kl"]
                db = t_qm - t_km - t_kl + dqb * w["qb"]
                db_mid = jnp.sum((t_km - t_qm).reshape(NCH, CHUNK, HD), axis=1, keepdims=True)
                db_last = jnp.sum(t_kl.reshape(NCH, CHUNK, HD), axis=1, keepdims=True)
                db_last = db_last + jnp.stack(dd_rows, axis=0) * jnp.exp(
                    bp[:, ls].reshape(NCH, CHUNK, HD)[:, CHUNK - 1:CHUNK, :])
                spread = lambda v: jnp.broadcast_to(v, (NCH, CHUNK, HD)).reshape(HB, HD)
                db = (db + jnp.where(in_chunk == CHUNK // 2 - 1, spread(db_mid), 0.0)
                      + jnp.where(in_chunk == CHUNK - 1, spread(db_last), 0.0))
                db_pair.append(db)
                sq = w["sq"]
                dz_ref[:, ZQH + h * HD:ZQH + (h + 1) * HD] = (
                    dqf * (HD ** -0.5) * (sq * (1.0 + q * (1.0 - sq)))).astype(BF16)
            dlogf = _chunk_sums(upp_b, jnp.concatenate(db_pair, axis=1))
            dfv = dlogf / fp - jnp.concatenate(dkf_pair, axis=1)
            dz_ref[:, ZFH + 2 * HD * p:ZFH + 2 * HD * (p + 1)] = (dfv * (1.0 - lbp) * sgp * (1.0 - sgp)).astype(BF16)
            dlb_ref[:, 2 * HD * p:2 * HD * (p + 1)] += _rowsum8(dfv * (1.0 - sgp))

    rev = lambda j: nb - 1 - j
    return pl.pallas_call(
        body, name="hgrn_bwd",
        out_shape=(jax.ShapeDtypeStruct((t, D_IN), BF16), jax.ShapeDtypeStruct((8, HG_W), F32),
                   jax.ShapeDtypeStruct((8, HD), F32)),
        grid=(nb,),
        in_specs=_hgrn_cols(rev) + [pl.BlockSpec((2, HG_W), lambda j: (0, 0)), pl.BlockSpec((1, HD), lambda j: (0, 0)),
                                    pl.BlockSpec((HB, HG_W), lambda j: (rev(j), 0)),
                                    pl.BlockSpec((4, NCH, HD, HD), lambda j: (0, rev(j), 0, 0)),
                                    pl.BlockSpec((HB, HG_W), lambda j: (rev(j), 1)),
                                    pl.BlockSpec((HB, SWA_W), lambda j: (rev(j), 0)),
                                    pl.BlockSpec((HB // 2, 2 * KV_W), lambda j: (rev(j), 0)),
                                    pl.BlockSpec((HB // 2, 2 * KV_W), lambda j: (rev(j), 0))],
        out_specs=(pl.BlockSpec((HB, D_IN), lambda j: (rev(j), 0)), pl.BlockSpec((8, HG_W), lambda j: (0, 0)),
                   pl.BlockSpec((8, HD), lambda j: (0, 0))),
        scratch_shapes=[pltpu.VMEM((4, HD, HD), F32)],
        compiler_params=_params(dimension_semantics=("arbitrary",)),
    )(*[z] * 8, hgrn_lb, onorm, o_save, sprev, dymix, *dza)


XB = 512


def _xattn_fwd(q, k, v, wo, h, g_post, g_pre, t, dep=None):
    tb = min(XB, t)
    deps = [] if dep is None else [dep]

    def body(q_ref, k_ref, v_ref, wo_ref, h_ref, gp_ref, gn_ref, *rest):
        o_ref, y_ref, hn_ref, u_ref = rest[len(deps):]
        for hd in range(XH):
            cols = slice(XD * hd, XD * (hd + 1))
            s = _dot(q_ref[:, cols], k_ref[:, cols], 1, 1) * (XD ** -0.5)
            p = jnp.exp(s - jnp.max(s, axis=-1, keepdims=True))
            l = jnp.sum(p, axis=-1, keepdims=True)
            o_ref[:, cols] = (_dot(p.astype(BF16), v_ref[:, cols], 1, 0) * (1.0 / l)).astype(BF16)
        y, hn, u = _ep_post_pre(_dot(o_ref[...], wo_ref[...], 1, 0), h_ref[...], gp_ref[...], gn_ref[...])
        y_ref[...] = y
        hn_ref[...] = hn
        u_ref[...] = u.astype(BF16)

    row = pl.BlockSpec((tb, D), lambda i: (i, 0))
    whole = lambda a: pl.BlockSpec(a.shape, lambda i: (0,) * a.ndim, pipeline_mode=pl.Buffered(1))
    half = jax.ShapeDtypeStruct((t, D), BF16)
    return pl.pallas_call(
        body, name="xattn_fwd", out_shape=(half, half, jax.ShapeDtypeStruct((t, D), F32), half), grid=(t // tb,),
        in_specs=[row, whole(k), whole(v), whole(wo), row, whole(g_post), whole(g_pre)] + [ANY_SPEC] * len(deps),
        out_specs=(row, row, row, row), compiler_params=_params(),
    )(q, k, v, wo, h, g_post, g_pre, *deps)


def _xattn_bwd(q, k, v, do, wq, wout, dh_out, hn, y, g_post, g_pre, t):
    tb = min(XB, t)

    def body(q_ref, k_ref, v_ref, do_ref, wq_ref, wout_ref, dho_ref, hn_ref, y_ref, gp_ref, gn_ref,
             dq_ref, dk_ref, dv_ref, dh_ref, dyp_ref, dym_ref, dgn_ref, dgp_ref):
        @pl.when(pl.program_id(0) == 0)
        def _():
            dk_ref[...] = jnp.zeros_like(dk_ref)
            dv_ref[...] = jnp.zeros_like(dv_ref)
            dgn_ref[...] = jnp.zeros_like(dgn_ref)
            dgp_ref[...] = jnp.zeros_like(dgp_ref)

        for h in range(XH):
            cols = slice(XD * h, XD * (h + 1))
            qh, kh, vh, doh = q_ref[:, cols], k_ref[:, cols], v_ref[:, cols], do_ref[:, cols]
            s = _dot(qh, kh, 1, 1) * (XD ** -0.5)
            p = jnp.exp(s - jnp.max(s, axis=-1, keepdims=True))
            p = p * (1.0 / jnp.sum(p, axis=-1, keepdims=True))
            dp = _dot(doh, vh, 1, 1)
            ds = (p * (dp - jnp.sum(p * dp, axis=-1, keepdims=True)) * (XD ** -0.5)).astype(BF16)
            dq_ref[:, cols] = _dot(ds, kh, 1, 0).astype(BF16)
            dk_ref[:, cols] += _dot(ds, qh, 0, 0)
            dv_ref[:, cols] += _dot(p.astype(BF16), doh, 0, 0)
        du = _dot(dq_ref[...], wq_ref[...], 1, 1)
        dh, dyp, dgn, dgp = _ep_post_pre_bwd(du, dho_ref[...], hn_ref[...], y_ref[...], gp_ref[...], gn_ref[...])
        dh_ref[...] = dh
        dyp = dyp.astype(BF16)
        dyp_ref[...] = dyp
        dym_ref[...] = _dot(dyp, wout_ref[...], 1, 1).astype(BF16)
        dgn_ref[...] += dgn
        dgp_ref[...] += dgp

    row = pl.BlockSpec((tb, D), lambda i: (i, 0))
    mem = pl.BlockSpec(k.shape, lambda i: (0, 0))
    whole = lambda a: pl.BlockSpec(a.shape, lambda i: (0,) * a.ndim, pipeline_mode=pl.Buffered(1))
    acc = pl.BlockSpec((8, D), lambda i: (0, 0))
    half = jax.ShapeDtypeStruct((t, D), BF16)
    return pl.pallas_call(
        body, name="xattn_bwd",
        out_shape=(half, jax.ShapeDtypeStruct(k.shape, F32), jax.ShapeDtypeStruct(k.shape, F32),
                   jax.ShapeDtypeStruct((t, D), F32), half, half,
                   jax.ShapeDtypeStruct((8, D), F32), jax.ShapeDtypeStruct((8, D), F32)),
        grid=(t // tb,),
        in_specs=[row, whole(k), whole(v), row, whole(wq), whole(wout), row, row, row, whole(g_post), whole(g_pre)],
        out_specs=(row, mem, mem, row, row, row, acc, acc),
        compiler_params=_params(dimension_semantics=("arbitrary",)),
    )(q, k, v, do, wq, wout, dh_out, hn, y, g_post, g_pre)


def _mem_kv(mem, g_mem, wk, wv):
    def body(m_ref, g_ref, wk_ref, wv_ref, mn_ref, k_ref, v_ref):
        m_ = m_ref[...]
        mn = (m_ * _rstd(m_) * g_ref[...]).astype(BF16)
        mn_ref[...] = mn
        k_ref[...] = _dot(mn, wk_ref[...], 1, 0).astype(BF16)
        v_ref[...] = _dot(mn, wv_ref[...], 1, 0).astype(BF16)

    return pl.pallas_call(body, name="mem_kv", out_shape=(jax.ShapeDtypeStruct(mem.shape, BF16),) * 3,
                          compiler_params=_params())(mem, g_mem, wk, wv)


def _mem_kv_bwd(mn, mem, dk, dv, wk, wv, dep=None):
    deps = [] if dep is None else [dep]

    def body(mn_ref, m_ref, dk_ref, dv_ref, wk_ref, wv_ref, *rest):
        gk_ref, gv_ref, dg_ref = rest[len(deps):]
        mn = mn_ref[...]
        dkb, dvb = dk_ref[...].astype(BF16), dv_ref[...].astype(BF16)
        gk_ref[...] = _dot(mn, dkb, 0, 0).astype(BF16)
        gv_ref[...] = _dot(mn, dvb, 0, 0).astype(BF16)
        dmn = _dot(dkb, wk_ref[...], 1, 1) + _dot(dvb, wv_ref[...], 1, 1)
        m_ = m_ref[...]
        dg_ref[...] = _rowsum8(dmn * (m_ * _rstd(m_)))

    vmem = pl.BlockSpec(memory_space=pltpu.VMEM)
    return pl.pallas_call(
        body, name="mem_kv_bwd",
        out_shape=(jax.ShapeDtypeStruct(wk.shape, BF16), jax.ShapeDtypeStruct(wv.shape, BF16),
                   jax.ShapeDtypeStruct((8, D), F32)),
        in_specs=[vmem] * 6 + [ANY_SPEC] * len(deps), out_specs=(vmem,) * 3, compiler_params=_params(),
    )(mn, mem, dk, dv, wk, wv, *deps)


FB = 256


def _ffn_fwd_bwd(u, wgt, wut, wd, h, target, g_last, y_prev, g_post, g_pre, wo, t):
    tb = min(FB, t)

    def body(u_ref, wg_ref, wu_ref, wd_ref, h_ref, t_ref, gl_ref, yp_ref, gp_ref, gn_ref, wo_ref,
             a_ref, dy_ref, dg_ref, dup_ref, dh_ref, dyp_ref, do_ref, sq_ref, dgl_ref, dgn_ref, dgp_ref):
        @pl.when(pl.program_id(0) == 0)
        def _():
            for ref in (sq_ref, dgl_ref, dgn_ref, dgp_ref):
                ref[...] = jnp.zeros_like(ref)

        u_ = u_ref[...]
        g = _dot(u_, wg_ref[...], 1, 1)
        up = _dot(u_, wu_ref[...], 1, 1)
        sg = _sig(g)
        a = (g * sg * up).astype(BF16)
        a_ref[...] = a
        h_ = h_ref[...]
        sq, dh3, dy, dgl = _ep_final_loss(_dot(a, wd_ref[...], 1, 0), h_, t_ref[...], gl_ref[...])
        sq_ref[...] += sq
        dgl_ref[...] += dgl
        dy = dy.astype(BF16)
        dy_ref[...] = dy
        da = _dot(dy, wd_ref[...], 1, 1)
        dup = (da * g * sg).astype(BF16)
        dgate = (da * up * (sg * (1.0 + g * (1.0 - sg)))).astype(BF16)
        dup_ref[...] = dup
        dg_ref[...] = dgate
        du = _dot(dgate, wg_ref[...], 1, 0) + _dot(dup, wu_ref[...], 1, 0)
        dh, dyp, dgn, dgp = _ep_post_pre_bwd(du, dh3, h_, yp_ref[...], gp_ref[...], gn_ref[...])
        dh_ref[...] = dh
        dyp = dyp.astype(BF16)
        dyp_ref[...] = dyp
        do_ref[...] = _dot(dyp, wo_ref[...], 1, 1).astype(BF16)
        dgn_ref[...] += dgn
        dgp_ref[...] += dgp

    row = lambda w: pl.BlockSpec((tb, w), lambda i: (i, 0))
    whole = lambda a: pl.BlockSpec(a.shape, lambda i: (0,) * a.ndim, pipeline_mode=pl.Buffered(1))
    acc = pl.BlockSpec((8, D), lambda i: (0, 0))
    wide, half, sums = (jax.ShapeDtypeStruct((t, D_FF), BF16), jax.ShapeDtypeStruct((t, D), BF16),
                        jax.ShapeDtypeStruct((8, D), F32))
    return pl.pallas_call(
        body, name="ffn_fwd_bwd",
        out_shape=(wide, half, wide, wide, jax.ShapeDtypeStruct((t, D), F32), half, half, sums, sums, sums, sums),
        grid=(t // tb,),
        in_specs=[row(D), whole(wgt), whole(wut), whole(wd), row(D), row(D), whole(g_last), row(D), whole(g_post),
                  whole(g_pre), whole(wo)],
        out_specs=(row(D_FF), row(D), row(D_FF), row(D_FF), row(D), row(D), row(D), acc, acc, acc, acc),
        compiler_params=_params(dimension_semantics=("arbitrary",)),
    )(u, wgt, wut, wd, h, target, g_last, y_prev, g_post, g_pre, wo)


def _local_step(x, mem, target, fetch, sm, emit=None, first_dep=None, milestone=None):
    t = x.shape[0]
    w, gw = {}, {}

    def out(key, g):
        gw[key] = g
        return None if emit is None else emit(key, g)

    def tell(tag, value):
        return None if milestone is None else milestone(tag, value)
    u1 = _prenorm(x, sm["g_mix_pre"], name="prenorm_mix", dep=first_dep)
    w["winT"] = fetch("winT", u1)
    z = _mm(u1, w["winT"], tb=True, out_dtype=F32, tm=1024, tn=1408, name="mm_z", n_outer=True)
    ymix, lse = _swa_fwd(z, sm["sinks"], t, dep=tell("z", z))
    ymix, o_h, sprev = _hgrn2_fwd(z, sm["hgrn_lb"], sm["hgrn_onorm"], ymix, t, dep=tell("swa", lse))
    for key in ("wout", "wq", "wk", "wv", "wo"):
        w[key] = fetch(key, ymix)
    y1, h1, u2, qx = _mm_rows([(ymix, w["wout"], False)], [x], [sm["g_mix_post"], sm["g_x_pre"], w["wq"]],
                              _then(_ep_post_pre, 2, False), _EP_POST_PRE_OUTS + [ROW_BF16], tm=1024,
                              name="mm_y1_post_qx")
    mn, kx, vx = _mem_kv(mem, sm["g_mem"], w["wk"], w["wv"])
    ox, y2, h2, u3 = _xattn_fwd(qx, kx, vx, w["wo"], h1, sm["g_x_post"], sm["g_ffn_pre"], t, dep=tell("kv", kx))
    for key in ("wgT", "wuT", "wd"):
        w[key] = fetch(key, u3)
    act, dy3, dgate, dup, dh2, dy2, dox, sq, dg_ffn_post, dg_ffn_pre, dg_x_post = _ffn_fwd_bwd(
        u3, w["wgT"], w["wuT"], w["wd"], h2, target, sm["g_ffn_post"], y2, sm["g_x_post"], sm["g_ffn_pre"], w["wo"], t)
    dep = out("wd", _mm_tn(act, dy3, name="mm_gwd"))
    dep = out("wgT", _mm_tn(dgate, u3, name="mm_gwg", dep=dep))
    dep = out("wuT", _mm_tn(dup, u3, name="mm_gwu", dep=dep))
    out("wo", _mm_tn(ox, dy2, name="mm_gwo", dep=dep))
    dqx, dkx, dvx, dh1, dy1, dymix, dg_x_pre, dg_mix_post = _xattn_bwd(
        qx, kx, vx, dox, w["wq"], w["wout"], dh2, h1, y1, sm["g_mix_post"], sm["g_x_pre"], t)
    out("wq", _mm_tn(u2, dqx, name="mm_gwq"))
    gwk, gwv, dg_mem = _mem_kv_bwd(mn, mem, dkx, dvx, w["wk"], w["wv"])
    out("wk", gwk)
    dep = out("wv", gwv)
    dep = out("wout", _mm_tn(ymix, dy1, name="mm_gwout", dep=dep))
    *dza, dsinks = _swa_bwd(z, sm["sinks"], ymix, lse, dymix, t, dep=dep)
    dz, dlb, donorm = _hgrn2_bwd(z, sm["hgrn_lb"], sm["hgrn_onorm"], o_h, sprev, dymix, dza, t)
    dep = out("winT", _mm_tn(dz, u1, name="mm_gwin"))
    grad_x, dg_mix_pre = _mm_rows([(dz, w["winT"], False)], [dh1, x], [sm["g_mix_pre"]], _ep_pre_bwd,
                                  _EP_PRE_BWD_OUTS, tm=512, name="mm_du1_pre_bwd", dep=dep)
    parts = dict(g_mix_pre=dg_mix_pre, g_mix_post=dg_mix_post, g_mem=dg_mem, g_x_pre=dg_x_pre,
                 g_x_post=dg_x_post, g_ffn_pre=dg_ffn_pre, g_ffn_post=dg_ffn_post,
                 hgrn_onorm=donorm, hgrn_lb=dlb, sinks=dsinks, sq=sq)
    return grad_x, gw, parts


def _position():
    return lax.axis_index("x"), lax.axis_index("y"), lax.axis_index("c")


def _peer(pos, k):
    x, y, c = pos
    return (1 - x if k & 4 else x, 1 - y if k & 2 else y, 1 - c if k & 1 else c)


def _linear(pos):
    x, y, c = pos
    return 4 * x + 2 * y + c


HBM_SPEC = pl.BlockSpec(memory_space=pltpu.HBM)
SEM_SPEC = pl.BlockSpec(memory_space=pltpu.SEMAPHORE)
DATAFLOW = pltpu.SideEffectType.DATAFLOW_SIDE_EFFECTING
SEND_ORDER = (1, 2, 4, 3, 5, 6, 7)


def _in_hbm(a):
    return pltpu.with_memory_space_constraint(a, pltpu.HBM)


def _prepare_weights(shards, *, name, dep=None):
    n = len(shards)
    deps = [] if dep is None else [dep]

    def body(*refs):
        ins, (outs, lands, sem) = refs[:n], (refs[-2 * n - 1:-n - 1], refs[-n - 1:-1], refs[-1])
        me_lin = _linear(_position())
        copies = []
        for a in range(n):
            r = ins[a].shape[0]
            outs[a][...] = ins[a][...].astype(BF16)
            copies.append(pltpu.make_async_copy(outs[a], lands[a].at[pl.ds(me_lin * r, r), :], sem.at[a]))
            copies[-1].start()
        for cp in copies:
            cp.wait()

    vmem = pl.BlockSpec(memory_space=pltpu.VMEM)
    res = pl.pallas_call(
        body, name=name,
        out_shape=tuple(jax.ShapeDtypeStruct(s.shape, BF16) for s in shards)
        + tuple(jax.ShapeDtypeStruct((N_DEV * s.shape[0], s.shape[1]), BF16) for s in shards),
        in_specs=[vmem] * n + [ANY_SPEC] * len(deps), out_specs=tuple([vmem] * n + [ANY_SPEC] * n),
        scratch_shapes=[pltpu.SemaphoreType.DMA((n,))], compiler_params=_params(),
    )(*shards, *deps)
    return res[:n], res[n:]


def _copies_start(arrays, plan, n, *, name):
    na = len(arrays)

    def body(*refs):
        ins, send_sems, recv_sems = refs[:na], refs[na], refs[na + 1]
        me = _position()
        for j in range(n):
            src, dst, peer, _ = plan(ins, me, j)
            pltpu.make_async_remote_copy(src_ref=src, dst_ref=dst, send_sem=send_sems.at[j], recv_sem=recv_sems.at[j],
                                         device_id=peer, device_id_type=MESH).start()

    return pl.pallas_call(
        body, name=name,
        out_shape=(pltpu.SemaphoreType.DMA((n,)), pltpu.SemaphoreType.DMA((n,)))
        + tuple(pltpu.HBM(a.shape, a.dtype) for a in arrays),
        in_specs=(HBM_SPEC,) * na, out_specs=(SEM_SPEC, SEM_SPEC) + (HBM_SPEC,) * na,
        input_output_aliases={i: 2 + i for i in range(na)},
        compiler_params=pltpu.CompilerParams(has_side_effects=DATAFLOW),
    )(*[_in_hbm(a) for a in arrays])


def _copies_wait(send_sems, recv_sems, arrays, plan, n, after, *, name):
    na = len(arrays)

    def body(*refs):
        ins, send_sems, recv_sems = refs[:na], refs[na], refs[na + 1]
        me = _position()
        for j in range(n):
            src, _, peer, landed = plan(ins, me, j)
            copy = pltpu.make_async_remote_copy(src_ref=src, dst_ref=landed, send_sem=send_sems.at[j],
                                                recv_sem=recv_sems.at[j], device_id=peer, device_id_type=MESH)
            copy.wait_send()
            copy.wait_recv()

    return pl.pallas_call(
        body, name=name, out_shape=tuple(pltpu.HBM(a.shape, a.dtype) for a in arrays),
        in_specs=(HBM_SPEC,) * na + (SEM_SPEC, SEM_SPEC, ANY_SPEC), out_specs=(HBM_SPEC,) * na,
        input_output_aliases={i: i for i in range(na)},
        compiler_params=pltpu.CompilerParams(has_side_effects=DATAFLOW),
    )(*arrays, send_sems, recv_sems, after)


SAME_CORE = (2, 4, 6)


class _TwoLevelGather:
    def __init__(self, shards, lands, *, name):
        n = self.n = len(shards)
        self.name = name
        first_peers = (1,) + SAME_CORE

        def rows(ref, pos):
            r = ref.shape[0] // N_DEV
            return ref.at[pl.ds(_linear(pos) * r, r), :]

        def first(refs, me, j):
            a, peer = j // 4, _peer(me, first_peers[j % 4])
            return refs[a], rows(refs[n + a], me), peer, rows(refs[n + a], peer)

        def second(refs, me, j):
            a, sibling = j // 3, _peer(me, 1)
            mine = rows(refs[a], _peer(me, SAME_CORE[j % 3]))
            return mine, mine, sibling, rows(refs[a], _peer(sibling, SAME_CORE[j % 3]))

        self._first, self._second = first, second
        self._flight = _copies_start(list(shards) + list(lands), first, 4 * n, name=name + "_send")
        self.dep = self._flight[2]

    def pass_on(self, after):
        send1, recv1, *arrays = self._flight
        arrays = _copies_wait(send1, recv1, arrays, self._first, 4 * self.n, after, name=self.name + "_recv")
        self._flight = _copies_start(list(arrays[self.n:]), self._second, 3 * self.n, name=self.name + "_pass")
        return self._flight[2]

    def finish(self, after):
        send2, recv2, *lands = self._flight
        return _copies_wait(send2, recv2, lands, self._second, 3 * self.n, after, name=self.name + "_pass_recv")


def _exchange_start(gs, *, name):
    n = len(gs)
    rows = [g.shape[0] // N_DEV for g in gs]
    lands = [lax.empty((N_DEV - 1, r, g.shape[1]), g.dtype) for g, r in zip(gs, rows)]

    def body(*refs):
        g_refs, land_refs = refs[:n], refs[n:2 * n]
        send_sems, recv_sems = refs[2 * n:3 * n], refs[3 * n:4 * n]
        me = _position()
        for a in range(n):
            for k in SEND_ORDER:
                peer = _peer(me, k)
                pltpu.make_async_remote_copy(
                    src_ref=g_refs[a].at[pl.ds(_linear(peer) * rows[a], rows[a]), :],
                    dst_ref=land_refs[a].at[k - 1],
                    send_sem=send_sems[a].at[k - 1], recv_sem=recv_sems[a].at[k - 1],
                    device_id=peer, device_id_type=MESH).start()

    res = pl.pallas_call(
        body, name=name,
        out_shape=tuple(pltpu.SemaphoreType.DMA((N_DEV - 1,)) for _ in range(2 * n))
        + tuple(pltpu.HBM(a.shape, a.dtype) for a in gs + lands),
        in_specs=(HBM_SPEC,) * (2 * n), out_specs=(SEM_SPEC,) * (2 * n) + (HBM_SPEC,) * (2 * n),
        input_output_aliases={i: 2 * n + i for i in range(2 * n)},
        compiler_params=pltpu.CompilerParams(has_side_effects=DATAFLOW),
    )(*[_in_hbm(a) for a in gs + lands])
    return [(res[a], res[n + a], res[2 * n + a], res[3 * n + a]) for a in range(n)]


def _exchange_wait(send_sems, recv_sems, g_thru, land_thru, after, *, name):
    r = land_thru.shape[1]

    def body(g_ref, land_ref, send_sems, recv_sems, after_ref, g_dead, got_ref):
        del after_ref, g_dead, got_ref
        me = _position()
        for k in SEND_ORDER:
            peer = _peer(me, k)
            copy = pltpu.make_async_remote_copy(
                src_ref=g_ref.at[pl.ds(_linear(peer) * r, r), :], dst_ref=land_ref.at[k - 1],
                send_sem=send_sems.at[k - 1], recv_sem=recv_sems.at[k - 1],
                device_id=peer, device_id_type=MESH)
            copy.wait_send()
            copy.wait_recv()

    return pl.pallas_call(
        body, name=name,
        out_shape=(pltpu.HBM(g_thru.shape, g_thru.dtype), pltpu.HBM(land_thru.shape, land_thru.dtype)),
        in_specs=(HBM_SPEC, HBM_SPEC, SEM_SPEC, SEM_SPEC, pl.BlockSpec(memory_space=pl.ANY)),
        out_specs=(HBM_SPEC, HBM_SPEC), input_output_aliases={0: 0, 1: 1},
        compiler_params=pltpu.CompilerParams(has_side_effects=DATAFLOW),
    )(g_thru, land_thru, send_sems, recv_sems, after)


ADAMW_TILE_ROWS = 256


def _adamw_math(w, g, m, v):
    m = B1 * m + (1.0 - B1) * g
    v = B2 * v + (1.0 - B2) * (g * g)
    delta = -LR * ((m / C1) / (jnp.sqrt(v / C2) + AEPS) + WD * w)
    return delta, m, v


def _sum_adamw(items, *, name):
    n = len(items)
    r, d = items[0][2].shape
    assert all(it[2].shape == (r, d) for it in items)
    rc = r // 2 if r > ADAMW_TILE_ROWS else r
    tiles = [(a, r0) for a in range(n) for r0 in range(0, r, rc)]
    n_in, n_out = 5, 4

    def body(*refs):
        ins, outs = refs[:n_in * n], refs[n_in * n:(n_in + n_out) * n]
        land_v, own_v, f32_v, sems = refs[(n_in + n_out) * n:]
        me_lin = _linear(_position())

        def loads(j):
            a, r0 = tiles[j]
            g_all, land, w, m, v = ins[n_in * a:n_in * a + n_in]
            rows = pl.ds(r0, rc)
            pairs = [(land.at[:, rows, :], land_v.at[j]), (g_all.at[pl.ds(me_lin * r + r0, rc), :], own_v.at[j]),
                     (w.at[rows, :], f32_v.at[j, 0]), (m.at[rows, :], f32_v.at[j, 1]), (v.at[rows, :], f32_v.at[j, 2])]
            return [pltpu.make_async_copy(src, dst, sems.at[j, i]) for i, (src, dst) in enumerate(pairs)]

        def stores(j):
            a, r0 = tiles[j]
            return [pltpu.make_async_copy(f32_v.at[j, 3 + i], outs[n_out * a + i].at[pl.ds(r0, rc), :],
                                          sems.at[j, n_in + i]) for i in range(n_out)]

        for j in range(len(tiles)):
            for cp in loads(j):
                cp.start()
        for j in range(len(tiles)):
            for cp in loads(j):
                cp.wait()
            g = land_v[j, 0].astype(F32)
            for s in range(1, N_DEV - 1):
                g = g + land_v[j, s].astype(F32)
            g = own_v[j].astype(F32) + g
            f32_v[j, 3] = g
            f32_v[j, 4], f32_v[j, 5], f32_v[j, 6] = _adamw_math(f32_v[j, 0], g, f32_v[j, 1], f32_v[j, 2])
            for cp in stores(j):
                cp.start()
        for j in range(len(tiles)):
            for cp in stores(j):
                cp.wait()

    nt = len(tiles)
    res = pl.pallas_call(
        body, name=name,
        out_shape=tuple(jax.ShapeDtypeStruct((r, d), F32) for _ in range(n_out * n)),
        in_specs=[ANY_SPEC] * (n_in * n), out_specs=(ANY_SPEC,) * (n_out * n),
        scratch_shapes=[pltpu.VMEM((nt, N_DEV - 1, rc, d), BF16), pltpu.VMEM((nt, rc, d), BF16),
                        pltpu.VMEM((nt, 3 + n_out, rc, d), F32), pltpu.SemaphoreType.DMA((nt, n_in + n_out))],
        compiler_params=_params(),
    )(*[a for it in items for a in it])
    return [res[n_out * a:n_out * a + n_out] for a in range(n)]


SMALL = ("g_mix_pre", "g_mix_post", "g_mem", "g_x_pre", "g_x_post", "g_ffn_pre", "g_ffn_post",
         "hgrn_onorm", "hgrn_lb", "sinks")
SMALL_W = dict(hgrn_onorm=HD, hgrn_lb=HG_W, sinks=8)
SQ_ROW = len(SMALL)
PACK_ROWS = 16


def _small_pack(parts):
    ns = len(SMALL)

    def body(*refs):
        part, mine, slots, sem = refs[:ns + 1], refs[ns + 1], refs[ns + 2], refs[ns + 3]
        mine[...] = jnp.zeros((PACK_ROWS, D), F32)
        for r, name in enumerate(SMALL):
            wd = SMALL_W.get(name, D)
            mine[r:r + 1, 0:wd] = jnp.sum(part[r][...], axis=0, keepdims=True)[:, 0:wd]
        sq = jnp.sum(part[ns][...]) * (0.5 / D)
        mine[SQ_ROW:SQ_ROW + 1, :] = jnp.full((1, D), sq, F32)
        own = pltpu.make_async_copy(mine, slots.at[_linear(_position())], sem)
        own.start()
        own.wait()

    vmem = pl.BlockSpec(memory_space=pltpu.VMEM)
    return pl.pallas_call(
        body, name="small_pack",
        out_shape=(jax.ShapeDtypeStruct((PACK_ROWS, D), F32), jax.ShapeDtypeStruct((N_DEV, PACK_ROWS, D), F32)),
        in_specs=[vmem] * (ns + 1), out_specs=(vmem, ANY_SPEC),
        scratch_shapes=[pltpu.SemaphoreType.DMA(())], compiler_params=_params(),
    )(*[parts[n] for n in SMALL], parts["sq"])


def _small_exchange(mine, slots):
    def plan(refs, me, j):
        peer = _peer(me, j + 1)
        return refs[0], refs[1].at[_linear(me)], peer, refs[1].at[_linear(peer)]

    send, recv, mine1, slots1 = _copies_start([mine, slots], plan, N_DEV - 1, name="small_send")
    return lambda after: _copies_wait(send, recv, [mine1, slots1], plan, N_DEV - 1, after, name="small_recv")[1]


def _small_update(slots, sm, m_sm, v_sm):
    ns = len(SMALL)

    def body(*refs):
        tot = refs[0][0]
        for s in range(1, N_DEV):
            tot = tot + refs[0][s]
        w_refs, m_refs, v_refs = refs[1:ns + 1], refs[ns + 1:2 * ns + 1], refs[2 * ns + 1:3 * ns + 1]
        outs = refs[3 * ns + 1:]
        loss_ref = outs[0]
        g_out, d_out = outs[1:ns + 1], outs[ns + 1:2 * ns + 1]
        nm_out, nv_out = outs[2 * ns + 1:3 * ns + 1], outs[3 * ns + 1:4 * ns + 1]
        loss_ref[...] = tot[SQ_ROW:SQ_ROW + 1, 0:1]
        for r, name in enumerate(SMALL):
            wd = SMALL_W.get(name, D)
            g = tot[r:r + 1, 0:wd]
            w = w_refs[r][...]
            if name == "hgrn_lb":
                mx = jnp.maximum(w[0:1], w[1:2])
                e0, e1 = jnp.exp(w[0:1] - mx), jnp.exp(w[1:2] - mx)
                lb0 = e0 / (e0 + e1)
                g0 = g * lb0 * (1.0 - lb0)
                for i, gi in enumerate((g0, -g0)):
                    d, nm, nv = _adamw_math(w[i:i + 1], gi, m_refs[r][i:i + 1, :], v_refs[r][i:i + 1, :])
                    g_out[r][i:i + 1, :] = gi
                    d_out[r][i:i + 1, :], nm_out[r][i:i + 1, :], nv_out[r][i:i + 1, :] = d, nm, nv
            else:
                d, nm, nv = _adamw_math(w, g, m_refs[r][...], v_refs[r][...])
                g_out[r][...] = g
                d_out[r][...], nm_out[r][...], nv_out[r][...] = d, nm, nv

    shapes = [jax.ShapeDtypeStruct(sm[n].shape, F32) for n in SMALL]
    res = pl.pallas_call(
        body, name="small_update", out_shape=tuple([jax.ShapeDtypeStruct((1, 1), F32)] + shapes * 4),
        compiler_params=_params(),
    )(slots, *[sm[n] for n in SMALL], *[m_sm[n] for n in SMALL], *[v_sm[n] for n in SMALL])
    groups = [dict(zip(SMALL, res[1 + i * ns:1 + (i + 1) * ns])) for i in range(4)]
    return res[0], groups[0], groups[1], groups[2], groups[3]


BIG = ("w_in", "w_gate", "w_up", "w_down", "w_out", "wq_x", "wk_x", "wv_x", "wo_x")
BIG_KEY = dict(w_in="winT", w_gate="wgT", w_up="wuT", w_down="wd", w_out="wout", wq_x="wq", wk_x="wk",
               wv_x="wv", wo_x="wo")
TRANSPOSED = ("w_in", "w_gate", "w_up")
WEIGHTS = ("w_in", "sinks", "hgrn_lb", "hgrn_onorm", "w_out", "g_mix_pre", "g_mix_post", "g_mem", "g_x_pre",
           "g_x_post", "wq_x", "wk_x", "wv_x", "wo_x", "g_ffn_pre", "g_ffn_post", "w_gate", "w_up", "w_down")


def kernel(x, mem, w_in, sinks, hgrn_lb, hgrn_onorm, w_out, g_mix_pre, g_mix_post, g_mem, g_x_pre, g_x_post, wq_x, wk_x, wv_x, wo_x, g_ffn_pre, g_ffn_post, w_gate, w_up, w_down, loss_target, m_w_in, m_sinks, m_hgrn_lb, m_hgrn_onorm, m_w_out, m_g_mix_pre, m_g_mix_post, m_g_mem, m_g_x_pre, m_g_x_post, m_wq_x, m_wk_x, m_wv_x, m_wo_x, m_g_ffn_pre, m_g_ffn_post, m_w_gate, m_w_up, m_w_down, v_w_in, v_sinks, v_hgrn_lb, v_hgrn_onorm, v_w_out, v_g_mix_pre, v_g_mix_post, v_g_mem, v_g_x_pre, v_g_x_post, v_wq_x, v_wk_x, v_wv_x, v_wo_x, v_g_ffn_pre, v_g_ffn_post, v_w_gate, v_w_up, v_w_down):
    given = dict(locals())
    wts = {n: given[n] for n in WEIGHTS}
    ms = {n: given["m_" + n] for n in WEIGHTS}
    vs = {n: given["v_" + n] for n in WEIGHTS}

    def mat(a, name):
        a = a[0]
        return a.T if name in TRANSPOSED else a

    groups = (("w_in",), ("w_out", "wq_x", "wk_x", "wv_x", "wo_x"), ("w_gate", "w_up", "w_down"))
    gathers = []

    def start_group(g, dep):
        tag = ("w_in", "w_attn", "w_ffn")[g]
        shards, lands = _prepare_weights([mat(wts[n], n) for n in groups[g]], name="prepare_" + tag, dep=dep)
        gathers.append(_TwoLevelGather(shards, lands, name=tag))
        return gathers[-1].dep

    first_dep = start_group(1, start_group(0, None))
    name_of = {k: n for n, k in BIG_KEY.items()}
    gathered = {}

    def milestone(tag, value):
        if tag == "z":
            return start_group(2, value)
        return gathers[{"swa": 1, "kv": 2}[tag]].pass_on(value)

    def fetch(key, after):
        name = name_of[key]
        if name not in gathered:
            g = [i for i, group in enumerate(groups) if name in group][0]
            if g == 0:
                gathers[0].pass_on(after)
            gathered.update(zip(groups[g], gathers[g].finish(after)))
        return gathered[name]

    sm = {n: wts[n] for n in SMALL}
    started, held = {}, {}
    send_with = {k: group for group in (("wgT", "wuT"), ("wo", "wq", "wk", "wv")) for k in group}

    def emit(key, g):
        held[key] = g
        group = send_with.get(key, (key,))
        if key != group[-1]:
            return None
        flights = _exchange_start([held[k] for k in group], name="grad_send_" + name_of[group[0]])
        started.update({name_of[k]: f for k, f in zip(group, flights)})
        return flights[-1][2]

    grad_x, _, parts = _local_step(x[0], mem[0], loss_target[0], fetch, sm, emit, first_dep=first_dep, milestone=milestone)
    small_finish = _small_exchange(*_small_pack(parts))
    grads, deltas, new_m, new_v = {}, {}, {}, {}
    after = grad_x
    for group in (("w_down",), ("w_gate", "w_up"), ("wo_x", "wq_x", "wk_x", "wv_x", "w_out"), ("w_in",)):
        items = []
        for n in group:
            g_all, land = _exchange_wait(*started[n], after, name="grad_recv_" + n)
            items.append((g_all, land, mat(wts[n], n), mat(ms[n], n), mat(vs[n], n)))
            after = land
        for n, res in zip(group, _sum_adamw(items, name="adamw_" + group[0])):
            after = res[1]
            if n in TRANSPOSED:
                res = [a.T for a in res]
            grads[n], deltas[n], new_m[n], new_v[n] = [a[None] for a in res]
    loss, g_s, d_s, m_s, v_s = _small_update(small_finish(after), sm, {n: ms[n] for n in SMALL},
                                             {n: vs[n] for n in SMALL})
    grads.update(g_s), deltas.update(d_s), new_m.update(m_s), new_v.update(v_s)
    return (loss[0, 0], grad_x[None], *[grads[n] for n in WEIGHTS], *[deltas[n] for n in WEIGHTS],
            *[new_m[n] for n in WEIGHTS], *[new_v[n] for n in WEIGHTS])
```

```python
import functools

import jax
import jax.numpy as jnp
from jax import lax
from jax.experimental import pallas as pl
from jax.experimental.pallas import tpu as pltpu

F32 = jnp.float32
BF16 = jnp.bfloat16

D = 1024
D_IN = 2816
D_FF = 2816
CHUNK = 64
SWA_W = 512
KV_W = 128
HG_W = 512
HD = 128
ZQH, ZFH, ZIH, ZGH = 768, 1280, 1792, 2304
XH, XD = 4, 256
EPS = 1e-6
NEG = -1e30
N_DEV = 8
MESH = pl.DeviceIdType.MESH

LR, B1, B2, AEPS, WD, STEP = 0.001, 0.9, 0.999, 1e-08, 0.01, 10
C1 = 1.0 - B1 ** STEP
C2 = 1.0 - B2 ** STEP

VMEM_LIMIT = 56 * 1024 * 1024


def _params(**kw):
    return pltpu.CompilerParams(vmem_limit_bytes=VMEM_LIMIT, **kw)


def _sig(x):
    return 1.0 / (1.0 + jnp.exp(-x))


def _rowsum8(x):
    r, w = x.shape
    return jnp.sum(x.reshape(r // 8, 8, w), axis=0)


def _dot(a, b, ca, cb, precision=None):
    return lax.dot_general(a, b, (((ca,), (cb,)), ((), ())), preferred_element_type=F32,
                           precision=precision)


ANY_SPEC = pl.BlockSpec(memory_space=pl.ANY)


def _mm(a, b, *, ta=False, tb=False, out_dtype, tm, tn, tk=None, name, dep=None, n_outer=False):
    m = a.shape[1] if ta else a.shape[0]
    k = a.shape[0] if ta else a.shape[1]
    n = b.shape[0] if tb else b.shape[1]
    tm, tn = min(tm, m), min(tn, n)
    tk = k if tk is None else min(tk, k)
    nk = k // tk
    assert m % tm == 0 and n % tn == 0 and k % tk == 0, (name, m, n, k, tm, tn, tk)
    ij = (lambda g0, g1: (g1, g0)) if n_outer else (lambda g0, g1: (g0, g1))
    a_spec = (pl.BlockSpec((tk, tm), lambda g0, g1, kk: (kk, ij(g0, g1)[0])) if ta
              else pl.BlockSpec((tm, tk), lambda g0, g1, kk: (ij(g0, g1)[0], kk)))
    b_spec = (pl.BlockSpec((tn, tk), lambda g0, g1, kk: (ij(g0, g1)[1], kk)) if tb
              else pl.BlockSpec((tk, tn), lambda g0, g1, kk: (kk, ij(g0, g1)[1])))
    ca, cb = (0 if ta else 1), (1 if tb else 0)

    deps = [] if dep is None else [dep]

    def body(a_ref, b_ref, *rest):
        o_ref, acc = rest[len(deps)], rest[len(deps) + 1:]
        p = _dot(a_ref[...].astype(BF16), b_ref[...].astype(BF16), ca, cb)
        if nk == 1:
            o_ref[...] = p.astype(out_dtype)
        else:
            acc_ref, = acc
            kk = pl.program_id(2)

            @pl.when(kk == 0)
            def _():
                acc_ref[...] = p

            @pl.when(kk > 0)
            def _():
                acc_ref[...] += p

            @pl.when(kk == nk - 1)
            def _():
                o_ref[...] = acc_ref[...].astype(out_dtype)

    return pl.pallas_call(
        body, name=name, out_shape=jax.ShapeDtypeStruct((m, n), out_dtype),
        grid=(n // tn, m // tm, nk) if n_outer else (m // tm, n // tn, nk),
        in_specs=[a_spec, b_spec] + [ANY_SPEC] * len(deps),
        out_specs=pl.BlockSpec((tm, tn), lambda g0, g1, kk: ij(g0, g1)),
        scratch_shapes=[pltpu.VMEM((tm, tn), F32)] if nk > 1 else [],
        compiler_params=_params(dimension_semantics=("parallel", "parallel", "arbitrary")),
    )(a, b, *deps)


TN_FIRST = 256
TN_REST = 1152
TN_SLICES = 4


def _mm_tn(a, b, *, name, dep=None):
    (k, m), n = a.shape, b.shape[1]
    assert a.dtype == BF16 and b.dtype == BF16 and b.shape[0] == k
    widths = [TN_FIRST, TN_FIRST]
    while sum(widths) < m:
        widths.append(min(TN_REST, m - sum(widths)))
    starts = [sum(widths[:i]) for i in range(len(widths))]
    assert sum(widths) == m
    nb = len(widths)
    ks = k // TN_SLICES
    ahead = 2
    deps = [] if dep is None else [dep]

    def body(a_hbm, b_hbm, *rest):
        o_hbm, b_v, sems = rest[len(deps)], rest[len(deps) + 1], rest[-1]
        a_v, o_v = rest[len(deps) + 2:len(deps) + 2 + nb], rest[len(deps) + 2 + nb:-1]
        sliced = []
        for c in range(TN_SLICES):
            rows = pl.ds(c * ks, ks)
            sliced.append((pltpu.make_async_copy(b_hbm.at[rows, :], b_v.at[rows, :], sems.at[2 * c]),
                           pltpu.make_async_copy(a_hbm.at[rows, pl.ds(0, widths[0])], a_v[0].at[rows, :],
                                                 sems.at[2 * c + 1])))
        base = 2 * TN_SLICES - 1
        loads = [None] + [pltpu.make_async_copy(a_hbm.at[:, pl.ds(c0, cw)], a_v[i], sems.at[base + i])
                          for i, (c0, cw) in enumerate(zip(starts, widths)) if i > 0]
        stores = [pltpu.make_async_copy(o_v[i], o_hbm.at[pl.ds(c0, cw), :], sems.at[base + nb + i])
                  for i, (c0, cw) in enumerate(zip(starts, widths))]
        for pair in sliced:
            for cp in pair:
                cp.start()
        for i in range(1, 1 + ahead):
            loads[i].start()
        acc = None
        for c, pair in enumerate(sliced):
            for cp in pair:
                cp.wait()
            p = _dot(a_v[0][c * ks:(c + 1) * ks, :], b_v[c * ks:(c + 1) * ks, :], 0, 0)
            acc = p if acc is None else acc + p
        o_v[0][...] = acc.astype(BF16)
        stores[0].start()
        for i in range(1, nb):
            loads[i].wait()
            if i + ahead < nb:
                loads[i + ahead].start()
            o_v[i][...] = _dot(a_v[i][...], b_v[...], 0, 0).astype(BF16)
            stores[i].start()
        for cp in stores:
            cp.wait()

    return pl.pallas_call(
        body, name=name, out_shape=jax.ShapeDtypeStruct((m, n), BF16),
        in_specs=[ANY_SPEC] * (2 + len(deps)), out_specs=ANY_SPEC,
        scratch_shapes=[pltpu.VMEM((k, n), BF16)] + [pltpu.VMEM((k, cw), BF16) for cw in widths]
        + [pltpu.VMEM((cw, n), BF16) for cw in widths] + [pltpu.SemaphoreType.DMA((2 * TN_SLICES - 1 + 2 * nb,))],
        compiler_params=_params(),
    )(a, b, *deps)


def _mm_rows(prods, rows_in, vecs_in, epilogue, outs, *, tm, name, dep=None):
    m = prods[0][0].shape[0]
    n = prods[0][1].shape[0] if prods[0][2] else prods[0][1].shape[1]
    tm = min(tm, m)
    assert m % tm == 0
    deps = [] if dep is None else [dep]
    n_p, n_r, n_v = len(prods), len(rows_in), len(vecs_in)

    def body(*refs):
        ab = refs[:2 * n_p]
        row_refs = refs[2 * n_p:2 * n_p + n_r]
        vec_refs = refs[2 * n_p + n_r:2 * n_p + n_r + n_v]
        out_refs = refs[2 * n_p + n_r + n_v + len(deps):]
        p = None
        for j, (_, _, tb) in enumerate(prods):
            t = _dot(ab[2 * j][...].astype(BF16), ab[2 * j + 1][...], 1, 1 if tb else 0)
            p = t if p is None else p + t
        vals = epilogue(p, *[r[...] for r in row_refs], *[v[...] for v in vec_refs])
        for (dtype, kind), o_ref, val in zip(outs, out_refs, vals):
            if kind == "row":
                o_ref[...] = val.astype(dtype)
            else:
                @pl.when(pl.program_id(0) == 0)
                def _(o_ref=o_ref):
                    o_ref[...] = jnp.zeros_like(o_ref)

                o_ref[...] += val

    row = lambda w: pl.BlockSpec((tm, w), lambda i: (i, 0))
    whole = lambda a: pl.BlockSpec(a.shape, lambda i: (0,) * a.ndim, pipeline_mode=pl.Buffered(1))
    in_specs, args = [], []
    for a, b, _ in prods:
        in_specs += [row(a.shape[1]), whole(b)]
        args += [a, b]
    in_specs += [row(r.shape[1]) for r in rows_in] + [whole(v) for v in vecs_in] + [ANY_SPEC] * len(deps)
    return pl.pallas_call(
        body, name=name,
        out_shape=tuple(jax.ShapeDtypeStruct((m, n) if kind == "row" else (8, n), dtype) for dtype, kind in outs),
        grid=(m // tm,), in_specs=in_specs,
        out_specs=tuple(row(n) if kind == "row" else pl.BlockSpec((8, n), lambda i: (0, 0)) for _, kind in outs),
        compiler_params=_params(dimension_semantics=("arbitrary",)),
    )(*args, *rows_in, *vecs_in, *deps)


def _rstd(x):
    return lax.rsqrt(jnp.mean(x * x, axis=-1, keepdims=True) + EPS)


def _norm_bwd(xh, r, t):
    return r * (t - xh * jnp.mean(xh * t, axis=-1, keepdims=True))


ROW_F32, ROW_BF16, SUM_F32 = (F32, "row"), (BF16, "row"), (F32, "sum")


def _then(epilogue, index, tb):
    def run(p, *args):
        vals = epilogue(p, *args[:-1])
        return (*vals, _dot(vals[index].astype(BF16), args[-1], 1, 1 if tb else 0))

    return run


def _ep_post_pre(p, h, g_post, g_pre):
    y = p.astype(BF16)
    yf = y.astype(F32)
    hn = h + yf * _rstd(yf) * g_post
    return y, hn, hn * _rstd(hn) * g_pre


_EP_POST_PRE_OUTS = [ROW_BF16, ROW_F32, ROW_BF16]


def _ep_final_loss(y, h, target, g_post):
    r = _rstd(y)
    yh = y * r
    err = h + yh * g_post - target
    dh = err * (1.0 / D)
    return _rowsum8(err * err), dh, _norm_bwd(yh, r, dh * g_post), _rowsum8(dh * yh)


def _ep_post_pre_bwd(du, dh_out, hn, y, g_post, g_pre):
    r2 = _rstd(hn)
    xh = hn * r2
    dh = dh_out + _norm_bwd(xh, r2, du * g_pre)
    yf = y.astype(F32)
    r1 = _rstd(yf)
    yh = yf * r1
    return dh, _norm_bwd(yh, r1, dh * g_post), _rowsum8(du * xh), _rowsum8(dh * yh)


_EP_POST_PRE_BWD_OUTS = [ROW_F32, ROW_BF16, SUM_F32, SUM_F32]


def _ep_pre_bwd(du, dh_out, x, g):
    r = _rstd(x)
    xh = x * r
    return dh_out + _norm_bwd(xh, r, du * g), _rowsum8(du * xh)


_EP_PRE_BWD_OUTS = [ROW_F32, SUM_F32]


def _prenorm(x, g, *, name, dep=None):
    t, d = x.shape
    tb = min(512, t)
    deps = [] if dep is None else [dep]

    def body(x_ref, g_ref, *rest):
        xf = x_ref[...]
        rest[-1][...] = (xf * _rstd(xf) * g_ref[...]).astype(BF16)

    return pl.pallas_call(
        body, name=name, out_shape=jax.ShapeDtypeStruct((t, d), BF16), grid=(t // tb,),
        in_specs=[pl.BlockSpec((tb, d), lambda i: (i, 0)), pl.BlockSpec((1, d), lambda i: (0, 0))]
        + [ANY_SPEC] * len(deps),
        out_specs=pl.BlockSpec((tb, d), lambda i: (i, 0)), compiler_params=_params(),
    )(x, g, *deps)


QB = 256


def _half_mask(shape, e):
    lane = lax.broadcasted_iota(jnp.int32, shape, len(shape) - 1)
    return (lane // 64) == e


def _place(kv):
    sw = pltpu.roll(kv, 64, 1)
    m0 = _half_mask(kv.shape, 0)
    return [[jnp.where(m0, kv, 0.0).astype(BF16), jnp.where(m0, 0.0, sw).astype(BF16)],
            [jnp.where(m0, sw, 0.0).astype(BF16), jnp.where(m0, 0.0, kv).astype(BF16)]]


SQ = 128
SK = 256


def _swa_valid(i, sb):
    qc = lax.broadcasted_iota(jnp.int32, (SQ, SK), 0) // CHUNK
    kc = lax.broadcasted_iota(jnp.int32, (SQ, SK), 1) // CHUNK - 2
    return (kc <= qc) & (qc <= kc + 2) & (4 * i + 2 * sb + kc >= 0)


def _swa_fwd(z, sinks, t, dep=None):
    nb = t // QB
    deps = [] if dep is None else [dep]

    def body(s_ref, q_ref, kp_ref, kc_ref, vp_ref, vc_ref, *rest):
        o_ref, lse_ref = rest[-2:]
        i = pl.program_id(0)
        kpl = _place(jnp.concatenate([kp_ref[...], kc_ref[...]], axis=0))
        vpl = _place(jnp.concatenate([vp_ref[...], vc_ref[...]], axis=0))
        lane = lax.broadcasted_iota(jnp.int32, (SQ, 128), 1)
        for sb in range(QB // SQ):
            rows, keys = slice(SQ * sb, SQ * (sb + 1)), slice(SQ * sb, SQ * sb + SK)
            valid = _swa_valid(i, sb)
            lse_out = jnp.zeros((SQ, 128), F32)
            for j in range(4):
                qp = q_ref[rows, 128 * j:128 * (j + 1)].astype(BF16)
                acc = jnp.zeros((SQ, 128), F32)
                for e in range(2):
                    h = 2 * j + e
                    kvh = h // 4
                    qm = jnp.where(_half_mask(qp.shape, e), qp, jnp.zeros_like(qp))
                    s = _dot(qm, kpl[kvh][e][keys], 1, 1) * 0.125
                    s = jnp.where(valid, s, NEG)
                    sink = s_ref[0, h]
                    m = jnp.maximum(jnp.max(s, axis=-1, keepdims=True), sink)
                    p = jnp.exp(s - m)
                    l = jnp.sum(p, axis=-1, keepdims=True) + jnp.exp(sink - m)
                    acc = acc + _dot(p.astype(BF16), vpl[kvh][e][keys], 1, 0) * (1.0 / l)
                    lse_out = jnp.where(lane == h, m + jnp.log(l), lse_out)
                o_ref[rows, 128 * j:128 * (j + 1)] = acc.astype(BF16)
            lse_ref[rows, :] = lse_out

    prev = lambda c: pl.BlockSpec((128, 128), lambda i: (jnp.maximum(2 * i - 1, 0), c))
    cur = lambda c: pl.BlockSpec((QB, 128), lambda i: (i, c))
    return pl.pallas_call(
        body, name="swa_fwd",
        out_shape=(jax.ShapeDtypeStruct((t, D), BF16), jax.ShapeDtypeStruct((t, 128), F32)),
        grid=(nb,),
        in_specs=[pl.BlockSpec(memory_space=pltpu.SMEM),
                  pl.BlockSpec((QB, SWA_W), lambda i: (i, 0)), prev(4), cur(4), prev(5), cur(5)]
        + [ANY_SPEC] * len(deps),
        out_specs=(pl.BlockSpec((QB, SWA_W), lambda i: (i, 0)), pl.BlockSpec((QB, 128), lambda i: (i, 0))),
        compiler_params=_params(),
    )(sinks, z, z, z, z, z, *deps)


def _swa_bwd(z, sinks, ymix, lse, dymix, t, dep=None):
    nb = t // QB
    deps = [] if dep is None else [dep]

    def body(s_ref, q_ref, kp_ref, kc_ref, vp_ref, vc_ref, o_ref, do_ref, l_ref, *rest):
        dq_ref, first_ref, second_ref, ds_ref, carry_ref = rest[len(deps):]
        i = pl.program_id(0)
        live = i < nb

        @pl.when(i == 0)
        def _():
            ds_ref[...] = jnp.zeros_like(ds_ref)
            carry_ref[...] = jnp.zeros_like(carry_ref)

        lane = lax.broadcasted_iota(jnp.int32, (8, 128), 1)
        kpl = _place(jnp.concatenate([kp_ref[...], kc_ref[...]], axis=0))
        vpl = _place(jnp.concatenate([vp_ref[...], vc_ref[...]], axis=0))
        nk = QB + 128
        qc = lax.broadcasted_iota(jnp.int32, (QB, nk), 0) // CHUNK
        kc = lax.broadcasted_iota(jnp.int32, (QB, nk), 1) // CHUNK - 2
        valid = (kc <= qc) & (qc <= kc + 2) & (4 * i + kc >= 0) & live
        lse_c = l_ref[...]
        dsink = jnp.zeros((8, 128), F32)
        dk_acc = [[jnp.zeros((128, nk), F32) for _ in range(2)] for _ in range(2)]
        dv_acc = [[jnp.zeros((128, nk), F32) for _ in range(2)] for _ in range(2)]
        dq = []
        for j in range(4):
            cols = slice(128 * j, 128 * (j + 1))
            qp = q_ref[:, cols].astype(BF16)
            dop = do_ref[:, cols]
            prod = dop.astype(F32) * o_ref[:, cols].astype(F32)
            acc = jnp.zeros((QB, 128), F32)
            for e in range(2):
                h = 2 * j + e
                kvh = h // 4
                hm = _half_mask(qp.shape, e)
                qm = jnp.where(hm, qp, jnp.zeros_like(qp))
                dom = jnp.where(hm, dop, jnp.zeros_like(dop))
                dd = jnp.sum(jnp.where(hm, prod, 0.0), axis=-1, keepdims=True)
                lse_h = lse_c[:, h:h + 1]
                s = _dot(qm, kpl[kvh][e], 1, 1) * 0.125
                p = jnp.where(valid, jnp.exp(s - lse_h), 0.0)
                dp = _dot(dom, vpl[kvh][e], 1, 1)
                ds = (p * (dp - dd) * 0.125).astype(BF16)
                acc = acc + _dot(ds, kpl[kvh][e], 1, 0)
                dk_acc[kvh][e] = dk_acc[kvh][e] + _dot(qm, ds, 0, 0)
                dv_acc[kvh][e] = dv_acc[kvh][e] + _dot(dom, p.astype(BF16), 0, 0)
                ps = jnp.where(live, jnp.exp(s_ref[0, h] - lse_h) * dd, 0.0)
                dsink = dsink - jnp.where(lane == h, _rowsum8(jnp.broadcast_to(ps, (QB, 128))), 0.0)
            dq.append(acc.astype(BF16))
        ds_ref[...] += dsink
        dk = (dk_acc[0][0] + dk_acc[1][1] + pltpu.roll(dk_acc[0][1] + dk_acc[1][0], 64, 0)).T
        dv = (dv_acc[0][0] + dv_acc[1][1] + pltpu.roll(dv_acc[0][1] + dv_acc[1][0], 64, 0)).T
        dkv = jnp.concatenate([dk, dv], axis=1)
        second_ref[...] = (carry_ref[...] + dkv[0:128]).astype(BF16)
        carry_ref[...] = dkv[256:384]

        @pl.when(live)
        def _():
            for j in range(4):
                dq_ref[:, 128 * j:128 * (j + 1)] = dq[j]
            first_ref[...] = dkv[128:256].astype(BF16)

    blk = lambda i: jnp.minimum(i, nb - 1)
    prev = lambda c: pl.BlockSpec((128, 128), lambda i: (jnp.maximum(2 * blk(i) - 1, 0), c))
    cur = lambda w, c: pl.BlockSpec((QB, w), lambda i: (blk(i), c))
    half = lambda index: pl.BlockSpec((128, 256), lambda i: (index(i), 0))
    return pl.pallas_call(
        body, name="swa_bwd",
        out_shape=(jax.ShapeDtypeStruct((t, SWA_W), BF16), jax.ShapeDtypeStruct((t // 2, 256), BF16),
                   jax.ShapeDtypeStruct((t // 2, 256), BF16), jax.ShapeDtypeStruct((8, 128), F32)),
        grid=(nb + 1,),
        in_specs=[pl.BlockSpec(memory_space=pltpu.SMEM),
                  cur(SWA_W, 0), prev(4), cur(128, 4), prev(5), cur(128, 5),
                  cur(SWA_W, 0), cur(SWA_W, 0), cur(128, 0)] + [ANY_SPEC] * len(deps),
        out_specs=(cur(SWA_W, 0), half(blk), half(lambda i: jnp.maximum(i - 1, 0)),
                   pl.BlockSpec((8, 128), lambda i: (0, 0))),
        scratch_shapes=[pltpu.VMEM((128, 256), F32)],
        compiler_params=_params(dimension_semantics=("arbitrary",)),
    )(sinks, z, z, z, z, z, ymix, dymix, lse, *deps)


HB = 256


def _lower_bound(lb_ref):
    a = lb_ref[...]
    a0, a1 = a[0:1], a[1:2]
    mx = jnp.maximum(a0, a1)
    e0, e1 = jnp.exp(a0 - mx), jnp.exp(a1 - mx)
    return e0 / (e0 + e1)


def _hgrn_cols(row_block):
    return [pl.BlockSpec((HB, 2 * HD), lambda j, c=base // (2 * HD) + p: (row_block(j), c))
            for base in (ZQH, ZFH, ZIH, ZGH) for p in range(2)]


NCH = HB // CHUNK


def _split3(x):
    hi = x.astype(BF16)
    r1 = x - hi.astype(F32)
    mid = r1.astype(BF16)
    return hi, mid, (r1 - mid.astype(F32)).astype(BF16)


def _blockdiag(lower):
    r = lax.broadcasted_iota(jnp.int32, (HB, HB), 0)
    c = lax.broadcasted_iota(jnp.int32, (HB, HB), 1)
    return (r // CHUNK == c // CHUNK) & ((c <= r) if lower else (c >= r))


def _chunk_sums(mask_bf16, x):
    return sum(_dot(mask_bf16, part, 1, 0) for part in _split3(x))


def _per_chunk_rows(x, row):
    w = x.shape[1]
    picked = x.reshape(NCH, CHUNK, w)[:, row:row + 1, :]
    return jnp.broadcast_to(picked, (NCH, CHUNK, w)).reshape(HB, w)


def _chunk_stack(x, chunk_of_row):
    return jnp.concatenate([jnp.where(chunk_of_row == c, x, jnp.zeros_like(x)) for c in range(NCH)], axis=1)


def _chunk_pick(x, chunk_of_row):
    w = x.shape[1] // NCH
    out = jnp.zeros((HB, w), x.dtype)
    for c in range(NCH):
        out = jnp.where(chunk_of_row == c, x[:, c * w:(c + 1) * w], out)
    return out


def _hgrn_local(q, f, kf, b):
    sq = _sig(q)
    qf = q * sq * (HD ** -0.5)
    b_mid = _per_chunk_rows(b, CHUNK // 2 - 1)
    b_last = _per_chunk_rows(b, CHUNK - 1)
    qm = qf * jnp.exp(b - b_mid)
    km = kf * jnp.exp(b_mid - b)
    kl = kf * jnp.exp(b_last - b)
    qb = qf * jnp.exp(b)
    return dict(sq=sq, b_mid=b_mid, b_last=b_last, qm=qm, km=km, kl=kl, qb=qb)


def _hgrn2_fwd(z, hgrn_lb, onorm, ymix, t, dep=None):
    nb = t // HB
    deps = [] if dep is None else [dep]

    def body(*refs):
        zq, zf, zi, zg = refs[0:2], refs[2:4], refs[4:6], refs[6:8]
        (lb_ref, on_ref), (y_ref, o_ref, sp_ref, st_ref) = refs[8:10], refs[-4:]

        @pl.when(pl.program_id(0) == 0)
        def _():
            st_ref[...] = jnp.zeros_like(st_ref)

        lb_all = _lower_bound(lb_ref)
        gn = on_ref[...]
        low = _blockdiag(True)
        low_b = low.astype(BF16)
        chunk_of_row = lax.broadcasted_iota(jnp.int32, (HB, HD), 0) // CHUNK
        for p in range(2):
            lbp = lb_all[:, 2 * HD * p:2 * HD * (p + 1)]
            fp = lbp + (1.0 - lbp) * _sig(zf[p][...])
            bp = _chunk_sums(low_b, jnp.log(fp))
            for e in range(2):
                h, ls = 2 * p + e, slice(e * HD, (e + 1) * HD)
                f = fp[:, ls]
                w = _hgrn_local(zq[p][:, ls], f, 1.0 - f, bp[:, ls])
                iv = zi[p][:, ls].astype(BF16)
                a = jnp.where(low, _dot(w["qm"].astype(BF16), w["km"].astype(BF16), 1, 1), 0.0)
                o = _dot(a.astype(BF16), iv, 1, 0)
                u = _dot(iv, _chunk_stack(w["kl"].astype(BF16), chunk_of_row), 0, 0)
                decay = jnp.exp(w["b_last"])
                st = st_ref[h]
                states = []
                for c in range(NCH):
                    sp_ref[h, c] = st
                    states.append(st.astype(BF16))
                    st = st * decay[c * CHUNK:c * CHUNK + 1] + u[:, c * HD:(c + 1) * HD]
                st_ref[h] = st
                inter = _dot(w["qb"].astype(BF16), jnp.concatenate(states, axis=0), 1, 1)
                o = o + _chunk_pick(inter, chunk_of_row)
                hs = slice(h * HD, (h + 1) * HD)
                o_ref[:, hs] = o
                gg = zg[p][:, ls]
                y_ref[:, hs] = (o * _rstd(o) * gn * (gg * _sig(gg))).astype(BF16)

    return pl.pallas_call(
        body, name="hgrn_fwd",
        out_shape=(jax.ShapeDtypeStruct((t, D), BF16), jax.ShapeDtypeStruct((t, HG_W), F32),
                   jax.ShapeDtypeStruct((4, t // CHUNK, HD, HD), F32)),
        grid=(nb,),
        in_specs=_hgrn_cols(lambda j: j) + [pl.BlockSpec((2, HG_W), lambda j: (0, 0)),
                                            pl.BlockSpec((1, HD), lambda j: (0, 0)), ANY_SPEC]
        + [ANY_SPEC] * len(deps),
        out_specs=(pl.BlockSpec((HB, HG_W), lambda j: (j, 1)),
                   pl.BlockSpec((HB, HG_W), lambda j: (j, 0)),
                   pl.BlockSpec((4, NCH, HD, HD), lambda j: (0, j, 0, 0))),
        scratch_shapes=[pltpu.VMEM((4, HD, HD), F32)],
        input_output_aliases={10: 0},
        compiler_params=_params(dimension_semantics=("arbitrary",)),
    )(*[z] * 8, hgrn_lb, onorm, ymix, *deps)


def _hgrn2_bwd(z, hgrn_lb, onorm, o_save, sprev, dymix, dza, t):
    nb = t // HB

    def body(*refs):
        zq, zf, zi, zg = refs[0:2], refs[2:4], refs[4:6], refs[6:8]
        (lb_ref, on_ref, o_ref, sp_ref, dy_ref, dqa_ref, first_ref, second_ref,
         dz_ref, dlb_ref, don_ref, dst_ref) = refs[8:]

        @pl.when(pl.program_id(0) == 0)
        def _():
            dst_ref[...] = jnp.zeros_like(dst_ref)
            dlb_ref[...] = jnp.zeros_like(dlb_ref)
            don_ref[...] = jnp.zeros_like(don_ref)

        dz_ref[:, 0:SWA_W] = dqa_ref[...]
        dz_ref[0:HB // 2, SWA_W:ZQH] = first_ref[...]
        dz_ref[HB // 2:HB, SWA_W:ZQH] = second_ref[...]
        lb_all = _lower_bound(lb_ref)
        gn = on_ref[...]
        low, upp = _blockdiag(True), _blockdiag(False)
        upp_b = upp.astype(BF16)
        low_b = low.astype(BF16)
        row = lax.broadcasted_iota(jnp.int32, (HB, HD), 0)
        chunk_of_row = row // CHUNK
        in_chunk = row % CHUNK
        for p in range(2):
            lbp = lb_all[:, 2 * HD * p:2 * HD * (p + 1)]
            sgp = _sig(zf[p][...])
            fp = lbp + (1.0 - lbp) * sgp
            bp = _chunk_sums(low_b, jnp.log(fp))
            db_pair, dkf_pair = [], []
            for e in range(2):
                h, ls, hs = 2 * p + e, slice(e * HD, (e + 1) * HD), slice((2 * p + e) * HD, (2 * p + e + 1) * HD)
                f = fp[:, ls]
                q = zq[p][:, ls]
                w = _hgrn_local(q, f, 1.0 - f, bp[:, ls])
                iv = zi[p][:, ls].astype(BF16)
                gg = zg[p][:, ls]
                o = o_ref[:, hs]
                dout = dy_ref[:, hs].astype(F32)
                sgg = _sig(gg)
                r = _rstd(o)
                oh = o * r
                dyn = dout * (gg * sgg)
                dz_ref[:, ZGH + h * HD:ZGH + (h + 1) * HD] = (
                    dout * oh * gn * (sgg * (1.0 + gg * (1.0 - sgg)))).astype(BF16)
                don_ref[...] += _rowsum8(dyn * oh)
                do = _norm_bwd(oh, r, dyn * gn).astype(BF16)
                qm, km, kl, qb = (w[n].astype(BF16) for n in ("qm", "km", "kl", "qb"))
                decay = jnp.exp(w["b_last"])
                grads_in = _dot(do, _chunk_stack(qb, chunk_of_row), 0, 0)
                dst = dst_ref[h]
                dstn, dd_rows = [None] * NCH, [None] * NCH
                for c in reversed(range(NCH)):
                    dstn[c] = dst.astype(BF16)
                    dd_rows[c] = jnp.sum(dst * sp_ref[h, c], axis=0, keepdims=True)
                    dst = dst * decay[c * CHUNK:c * CHUNK + 1] + grads_in[:, c * HD:(c + 1) * HD]
                dst_ref[h] = dst
                states = jnp.concatenate([sp_ref[h, c].astype(BF16) for c in range(NCH)], axis=0)
                dstn_all = jnp.concatenate(dstn, axis=0)
                dqb = _dot(_chunk_stack(do, chunk_of_row), states, 1, 0)
                at = jnp.where(upp, _dot(km, qm, 1, 1), 0.0)
                di = _dot(at.astype(BF16), do, 1, 0) + _chunk_pick(_dot(kl, dstn_all, 1, 1), chunk_of_row)
                dz_ref[:, ZIH + h * HD:ZIH + (h + 1) * HD] = di.astype(BF16)
                dkl = _dot(_chunk_stack(iv, chunk_of_row), dstn_all, 1, 0)
                da = jnp.where(low, _dot(do, iv, 1, 1), 0.0).astype(BF16)
                dat = jnp.where(upp, _dot(iv, do, 1, 1), 0.0).astype(BF16)
                dqm = _dot(da, km, 1, 0)
                dkm = _dot(dat, qm, 1, 0)
                b = bp[:, ls]
                e1, e2 = jnp.exp(b - w["b_mid"]), jnp.exp(w["b_mid"] - b)
                e3, e4 = jnp.exp(w["b_last"] - b), jnp.exp(b)
                dqf = dqm * e1 + dqb * e4
                dkf_pair.append(dkm * e2 + dkl * e3)
                t_qm, t_km, t_kl = dqm * w["qm"], dkm * w["km"], dkl * w["kl"]
                db = t_qm - t_km - t_kl + dqb * w["qb"]
                db_mid = jnp.sum((t_km - t_qm).reshape(NCH, CHUNK, HD), axis=1, keepdims=True)
                db_last = jnp.sum(t_kl.reshape(NCH, CHUNK, HD), axis=1, keepdims=True)
                db_last = db_last + jnp.stack(dd_rows, axis=0) * jnp.exp(
                    bp[:, ls].reshape(NCH, CHUNK, HD)[:, CHUNK - 1:CHUNK, :])
                spread = lambda v: jnp.broadcast_to(v, (NCH, CHUNK, HD)).reshape(HB, HD)
                db = (db + jnp.where(in_chunk == CHUNK // 2 - 1, spread(db_mid), 0.0)
                      + jnp.where(in_chunk == CHUNK - 1, spread(db_last), 0.0))
                db_pair.append(db)
                sq = w["sq"]
                dz_ref[:, ZQH + h * HD:ZQH + (h + 1) * HD] = (
                    dqf * (HD ** -0.5) * (sq * (1.0 + q * (1.0 - sq)))).astype(BF16)
            dlogf = _chunk_sums(upp_b, jnp.concatenate(db_pair, axis=1))
            dfv = dlogf / fp - jnp.concatenate(dkf_pair, axis=1)
            dz_ref[:, ZFH + 2 * HD * p:ZFH + 2 * HD * (p + 1)] = (dfv * (1.0 - lbp) * sgp * (1.0 - sgp)).astype(BF16)
            dlb_ref[:, 2 * HD * p:2 * HD * (p + 1)] += _rowsum8(dfv * (1.0 - sgp))

    rev = lambda j: nb - 1 - j
    return pl.pallas_call(
        body, name="hgrn_bwd",
        out_shape=(jax.ShapeDtypeStruct((t, D_IN), BF16), jax.ShapeDtypeStruct((8, HG_W), F32),
                   jax.ShapeDtypeStruct((8, HD), F32)),
        grid=(nb,),
        in_specs=_hgrn_cols(rev) + [pl.BlockSpec((2, HG_W), lambda j: (0, 0)), pl.BlockSpec((1, HD), lambda j: (0, 0)),
                                    pl.BlockSpec((HB, HG_W), lambda j: (rev(j), 0)),
                                    pl.BlockSpec((4, NCH, HD, HD), lambda j: (0, rev(j), 0, 0)),
                                    pl.BlockSpec((HB, HG_W), lambda j: (rev(j), 1)),
                                    pl.BlockSpec((HB, SWA_W), lambda j: (rev(j), 0)),
                                    pl.BlockSpec((HB // 2, 2 * KV_W), lambda j: (rev(j), 0)),
                                    pl.BlockSpec((HB // 2, 2 * KV_W), lambda j: (rev(j), 0))],
        out_specs=(pl.BlockSpec((HB, D_IN), lambda j: (rev(j), 0)), pl.BlockSpec((8, HG_W), lambda j: (0, 0)),
                   pl.BlockSpec((8, HD), lambda j: (0, 0))),
        scratch_shapes=[pltpu.VMEM((4, HD, HD), F32)],
        compiler_params=_params(dimension_semantics=("arbitrary",)),
    )(*[z] * 8, hgrn_lb, onorm, o_save, sprev, dymix, *dza)


XB = 512


def _xattn_fwd(q, k, v, wo, h, g_post, g_pre, t, dep=None):
    tb = min(XB, t)
    deps = [] if dep is None else [dep]

    def body(q_ref, k_ref, v_ref, wo_ref, h_ref, gp_ref, gn_ref, *rest):
        o_ref, y_ref, hn_ref, u_ref = rest[len(deps):]
        for hd in range(XH):
            cols = slice(XD * hd, XD * (hd + 1))
            s = _dot(q_ref[:, cols], k_ref[:, cols], 1, 1) * (XD ** -0.5)
            p = jnp.exp(s - jnp.max(s, axis=-1, keepdims=True))
            l = jnp.sum(p, axis=-1, keepdims=True)
            o_ref[:, cols] = (_dot(p.astype(BF16), v_ref[:, cols], 1, 0) * (1.0 / l)).astype(BF16)
        y, hn, u = _ep_post_pre(_dot(o_ref[...], wo_ref[...], 1, 0), h_ref[...], gp_ref[...], gn_ref[...])
        y_ref[...] = y
        hn_ref[...] = hn
        u_ref[...] = u.astype(BF16)

    row = pl.BlockSpec((tb, D), lambda i: (i, 0))
    whole = lambda a: pl.BlockSpec(a.shape, lambda i: (0,) * a.ndim, pipeline_mode=pl.Buffered(1))
    half = jax.ShapeDtypeStruct((t, D), BF16)
    return pl.pallas_call(
        body, name="xattn_fwd", out_shape=(half, half, jax.ShapeDtypeStruct((t, D), F32), half), grid=(t // tb,),
        in_specs=[row, whole(k), whole(v), whole(wo), row, whole(g_post), whole(g_pre)] + [ANY_SPEC] * len(deps),
        out_specs=(row, row, row, row), compiler_params=_params(),
    )(q, k, v, wo, h, g_post, g_pre, *deps)


def _xattn_bwd(q, k, v, do, wq, wout, dh_out, hn, y, g_post, g_pre, t):
    tb = min(XB, t)

    def body(q_ref, k_ref, v_ref, do_ref, wq_ref, wout_ref, dho_ref, hn_ref, y_ref, gp_ref, gn_ref,
             dq_ref, dk_ref, dv_ref, dh_ref, dyp_ref, dym_ref, dgn_ref, dgp_ref):
        @pl.when(pl.program_id(0) == 0)
        def _():
            dk_ref[...] = jnp.zeros_like(dk_ref)
            dv_ref[...] = jnp.zeros_like(dv_ref)
            dgn_ref[...] = jnp.zeros_like(dgn_ref)
            dgp_ref[...] = jnp.zeros_like(dgp_ref)

        for h in range(XH):
            cols = slice(XD * h, XD * (h + 1))
            qh, kh, vh, doh = q_ref[:, cols], k_ref[:, cols], v_ref[:, cols], do_ref[:, cols]
            s = _dot(qh, kh, 1, 1) * (XD ** -0.5)
            p = jnp.exp(s - jnp.max(s, axis=-1, keepdims=True))
            p = p * (1.0 / jnp.sum(p, axis=-1, keepdims=True))
            dp = _dot(doh, vh, 1, 1)
            ds = (p * (dp - jnp.sum(p * dp, axis=-1, keepdims=True)) * (XD ** -0.5)).astype(BF16)
            dq_ref[:, cols] = _dot(ds, kh, 1, 0).astype(BF16)
            dk_ref[:, cols] += _dot(ds, qh, 0, 0)
            dv_ref[:, cols] += _dot(p.astype(BF16), doh, 0, 0)
        du = _dot(dq_ref[...], wq_ref[...], 1, 1)
        dh, dyp, dgn, dgp = _ep_post_pre_bwd(du, dho_ref[...], hn_ref[...], y_ref[...], gp_ref[...], gn_ref[...])
        dh_ref[...] = dh
        dyp = dyp.astype(BF16)
        dyp_ref[...] = dyp
        dym_ref[...] = _dot(dyp, wout_ref[...], 1, 1).astype(BF16)
        dgn_ref[...] += dgn
        dgp_ref[...] += dgp

    row = pl.BlockSpec((tb, D), lambda i: (i, 0))
    mem = pl.BlockSpec(k.shape, lambda i: (0, 0))
    whole = lambda a: pl.BlockSpec(a.shape, lambda i: (0,) * a.ndim, pipeline_mode=pl.Buffered(1))
    acc = pl.BlockSpec((8, D), lambda i: (0, 0))
    half = jax.ShapeDtypeStruct((t, D), BF16)
    return pl.pallas_call(
        body, name="xattn_bwd",
        out_shape=(half, jax.ShapeDtypeStruct(k.shape, F32), jax.ShapeDtypeStruct(k.shape, F32),
                   jax.ShapeDtypeStruct((t, D), F32), half, half,
                   jax.ShapeDtypeStruct((8, D), F32), jax.ShapeDtypeStruct((8, D), F32)),
        grid=(t // tb,),
        in_specs=[row, whole(k), whole(v), row, whole(wq), whole(wout), row, row, row, whole(g_post), whole(g_pre)],
        out_specs=(row, mem, mem, row, row, row, acc, acc),
        compiler_params=_params(dimension_semantics=("arbitrary",)),
    )(q, k, v, do, wq, wout, dh_out, hn, y, g_post, g_pre)


def _mem_kv(mem, g_mem, wk, wv):
    def body(m_ref, g_ref, wk_ref, wv_ref, mn_ref, k_ref, v_ref):
        m_ = m_ref[...]
        mn = (m_ * _rstd(m_) * g_ref[...]).astype(BF16)
        mn_ref[...] = mn
        k_ref[...] = _dot(mn, wk_ref[...], 1, 0).astype(BF16)
        v_ref[...] = _dot(mn, wv_ref[...], 1, 0).astype(BF16)

    return pl.pallas_call(body, name="mem_kv", out_shape=(jax.ShapeDtypeStruct(mem.shape, BF16),) * 3,
                          compiler_params=_params())(mem, g_mem, wk, wv)


def _mem_kv_bwd(mn, mem, dk, dv, wk, wv, dep=None):
    deps = [] if dep is None else [dep]

    def body(mn_ref, m_ref, dk_ref, dv_ref, wk_ref, wv_ref, *rest):
        gk_ref, gv_ref, dg_ref = rest[len(deps):]
        mn = mn_ref[...]
        dkb, dvb = dk_ref[...].astype(BF16), dv_ref[...].astype(BF16)
        gk_ref[...] = _dot(mn, dkb, 0, 0).astype(BF16)
        gv_ref[...] = _dot(mn, dvb, 0, 0).astype(BF16)
        dmn = _dot(dkb, wk_ref[...], 1, 1) + _dot(dvb, wv_ref[...], 1, 1)
        m_ = m_ref[...]
        dg_ref[...] = _rowsum8(dmn * (m_ * _rstd(m_)))

    vmem = pl.BlockSpec(memory_space=pltpu.VMEM)
    return pl.pallas_call(
        body, name="mem_kv_bwd",
        out_shape=(jax.ShapeDtypeStruct(wk.shape, BF16), jax.ShapeDtypeStruct(wv.shape, BF16),
                   jax.ShapeDtypeStruct((8, D), F32)),
        in_specs=[vmem] * 6 + [ANY_SPEC] * len(deps), out_specs=(vmem,) * 3, compiler_params=_params(),
    )(mn, mem, dk, dv, wk, wv, *deps)


FB = 256


def _ffn_fwd_bwd(u, wgt, wut, wd, h, target, g_last, y_prev, g_post, g_pre, wo, t):
    tb = min(FB, t)

    def body(u_ref, wg_ref, wu_ref, wd_ref, h_ref, t_ref, gl_ref, yp_ref, gp_ref, gn_ref, wo_ref,
             a_ref, dy_ref, dg_ref, dup_ref, dh_ref, dyp_ref, do_ref, sq_ref, dgl_ref, dgn_ref, dgp_ref):
        @pl.when(pl.program_id(0) == 0)
        def _():
            for ref in (sq_ref, dgl_ref, dgn_ref, dgp_ref):
                ref[...] = jnp.zeros_like(ref)

        u_ = u_ref[...]
        g = _dot(u_, wg_ref[...], 1, 1)
        up = _dot(u_, wu_ref[...], 1, 1)
        sg = _sig(g)
        a = (g * sg * up).astype(BF16)
        a_ref[...] = a
        h_ = h_ref[...]
        sq, dh3, dy, dgl = _ep_final_loss(_dot(a, wd_ref[...], 1, 0), h_, t_ref[...], gl_ref[...])
        sq_ref[...] += sq
        dgl_ref[...] += dgl
        dy = dy.astype(BF16)
        dy_ref[...] = dy
        da = _dot(dy, wd_ref[...], 1, 1)
        dup = (da * g * sg).astype(BF16)
        dgate = (da * up * (sg * (1.0 + g * (1.0 - sg)))).astype(BF16)
        dup_ref[...] = dup
        dg_ref[...] = dgate
        du = _dot(dgate, wg_ref[...], 1, 0) + _dot(dup, wu_ref[...], 1, 0)
        dh, dyp, dgn, dgp = _ep_post_pre_bwd(du, dh3, h_, yp_ref[...], gp_ref[...], gn_ref[...])
        dh_ref[...] = dh
        dyp = dyp.astype(BF16)
        dyp_ref[...] = dyp
        do_ref[...] = _dot(dyp, wo_ref[...], 1, 1).astype(BF16)
        dgn_ref[...] += dgn
        dgp_ref[...] += dgp

    row = lambda w: pl.BlockSpec((tb, w), lambda i: (i, 0))
    whole = lambda a: pl.BlockSpec(a.shape, lambda i: (0,) * a.ndim, pipeline_mode=pl.Buffered(1))
    acc = pl.BlockSpec((8, D), lambda i: (0, 0))
    wide, half, sums = (jax.ShapeDtypeStruct((t, D_FF), BF16), jax.ShapeDtypeStruct((t, D), BF16),
                        jax.ShapeDtypeStruct((8, D), F32))
    return pl.pallas_call(
        body, name="ffn_fwd_bwd",
        out_shape=(wide, half, wide, wide, jax.ShapeDtypeStruct((t, D), F32), half, half, sums, sums, sums, sums),
        grid=(t // tb,),
        in_specs=[row(D), whole(wgt), whole(wut), whole(wd), row(D), row(D), whole(g_last), row(D), whole(g_post),
                  whole(g_pre), whole(wo)],
        out_specs=(row(D_FF), row(D), row(D_FF), row(D_FF), row(D), row(D), row(D), acc, acc, acc, acc),
        compiler_params=_params(dimension_semantics=("arbitrary",)),
    )(u, wgt, wut, wd, h, target, g_last, y_prev, g_post, g_pre, wo)


def _local_step(x, mem, target, fetch, sm, emit=None, first_dep=None, milestone=None):
    t = x.shape[0]
    w, gw = {}, {}

    def out(key, g):
        gw[key] = g
        return None if emit is None else emit(key, g)

    def tell(tag, value):
        return None if milestone is None else milestone(tag, value)
    u1 = _prenorm(x, sm["g_mix_pre"], name="prenorm_mix", dep=first_dep)
    w["winT"] = fetch("winT", u1)
    z = _mm(u1, w["winT"], tb=True, out_dtype=F32, tm=1024, tn=1408, name="mm_z", n_outer=True)
    ymix, lse = _swa_fwd(z, sm["sinks"], t)
    ymix, o_h, sprev = _hgrn2_fwd(z, sm["hgrn_lb"], sm["hgrn_onorm"], ymix, t, dep=tell("swa", lse))
    for key in ("wout", "wq", "wk", "wv", "wo"):
        w[key] = fetch(key, ymix)
    y1, h1, u2, qx = _mm_rows([(ymix, w["wout"], False)], [x], [sm["g_mix_post"], sm["g_x_pre"], w["wq"]],
                              _then(_ep_post_pre, 2, False), _EP_POST_PRE_OUTS + [ROW_BF16], tm=1024,
                              name="mm_y1_post_qx")
    mn, kx, vx = _mem_kv(mem, sm["g_mem"], w["wk"], w["wv"])
    ox, y2, h2, u3 = _xattn_fwd(qx, kx, vx, w["wo"], h1, sm["g_x_post"], sm["g_ffn_pre"], t, dep=tell("kv", kx))
    for key in ("wgT", "wuT", "wd"):
        w[key] = fetch(key, u3)
    act, dy3, dgate, dup, dh2, dy2, dox, sq, dg_ffn_post, dg_ffn_pre, dg_x_post = _ffn_fwd_bwd(
        u3, w["wgT"], w["wuT"], w["wd"], h2, target, sm["g_ffn_post"], y2, sm["g_x_post"], sm["g_ffn_pre"], w["wo"], t)
    dep = out("wd", _mm_tn(act, dy3, name="mm_gwd"))
    dep = out("wgT", _mm_tn(dgate, u3, name="mm_gwg", dep=dep))
    dep = out("wuT", _mm_tn(dup, u3, name="mm_gwu", dep=dep))
    out("wo", _mm_tn(ox, dy2, name="mm_gwo", dep=dep))
    dqx, dkx, dvx, dh1, dy1, dymix, dg_x_pre, dg_mix_post = _xattn_bwd(
        qx, kx, vx, dox, w["wq"], w["wout"], dh2, h1, y1, sm["g_mix_post"], sm["g_x_pre"], t)
    out("wq", _mm_tn(u2, dqx, name="mm_gwq"))
    gwk, gwv, dg_mem = _mem_kv_bwd(mn, mem, dkx, dvx, w["wk"], w["wv"])
    out("wk", gwk)
    dep = out("wv", gwv)
    dep = out("wout", _mm_tn(ymix, dy1, name="mm_gwout", dep=dep))
    *dza, dsinks = _swa_bwd(z, sm["sinks"], ymix, lse, dymix, t, dep=dep)
    dz, dlb, donorm = _hgrn2_bwd(z, sm["hgrn_lb"], sm["hgrn_onorm"], o_h, sprev, dymix, dza, t)
    dep = out("winT", _mm_tn(dz, u1, name="mm_gwin"))
    grad_x, dg_mix_pre = _mm_rows([(dz, w["winT"], False)], [dh1, x], [sm["g_mix_pre"]], _ep_pre_bwd,
                                  _EP_PRE_BWD_OUTS, tm=512, name="mm_du1_pre_bwd", dep=dep)
    parts = dict(g_mix_pre=dg_mix_pre, g_mix_post=dg_mix_post, g_mem=dg_mem, g_x_pre=dg_x_pre,
                 g_x_post=dg_x_post, g_ffn_pre=dg_ffn_pre, g_ffn_post=dg_ffn_post,
                 hgrn_onorm=donorm, hgrn_lb=dlb, sinks=dsinks, sq=sq)
    return grad_x, gw, parts


def _position():
    return lax.axis_index("x"), lax.axis_index("y"), lax.axis_index("c")


def _peer(pos, k):
    x, y, c = pos
    return (1 - x if k & 4 else x, 1 - y if k & 2 else y, 1 - c if k & 1 else c)


def _linear(pos):
    x, y, c = pos
    return 4 * x + 2 * y + c


HBM_SPEC = pl.BlockSpec(memory_space=pltpu.HBM)
SEM_SPEC = pl.BlockSpec(memory_space=pltpu.SEMAPHORE)
DATAFLOW = pltpu.SideEffectType.DATAFLOW_SIDE_EFFECTING
SEND_ORDER = (1, 2, 4, 3, 5, 6, 7)


def _in_hbm(a):
    return pltpu.with_memory_space_constraint(a, pltpu.HBM)


def _prepare_weights(shards, *, name, dep=None):
    n = len(shards)
    deps = [] if dep is None else [dep]

    def body(*refs):
        ins, (outs, lands, sem) = refs[:n], (refs[-2 * n - 1:-n - 1], refs[-n - 1:-1], refs[-1])
        me_lin = _linear(_position())
        copies = []
        for a in range(n):
            r = ins[a].shape[0]
            outs[a][...] = ins[a][...].astype(BF16)
            copies.append(pltpu.make_async_copy(outs[a], lands[a].at[pl.ds(me_lin * r, r), :], sem.at[a]))
            copies[-1].start()
        for cp in copies:
            cp.wait()

    vmem = pl.BlockSpec(memory_space=pltpu.VMEM)
    res = pl.pallas_call(
        body, name=name,
        out_shape=tuple(jax.ShapeDtypeStruct(s.shape, BF16) for s in shards)
        + tuple(jax.ShapeDtypeStruct((N_DEV * s.shape[0], s.shape[1]), BF16) for s in shards),
        in_specs=[vmem] * n + [ANY_SPEC] * len(deps), out_specs=tuple([vmem] * n + [ANY_SPEC] * n),
        scratch_shapes=[pltpu.SemaphoreType.DMA((n,))], compiler_params=_params(),
    )(*shards, *deps)
    return res[:n], res[n:]


def _copies_start(arrays, plan, n, *, name):
    na = len(arrays)

    def body(*refs):
        ins, send_sems, recv_sems = refs[:na], refs[na], refs[na + 1]
        me = _position()
        for j in range(n):
            src, dst, peer, _ = plan(ins, me, j)
            pltpu.make_async_remote_copy(src_ref=src, dst_ref=dst, send_sem=send_sems.at[j], recv_sem=recv_sems.at[j],
                                         device_id=peer, device_id_type=MESH).start()

    return pl.pallas_call(
        body, name=name,
        out_shape=(pltpu.SemaphoreType.DMA((n,)), pltpu.SemaphoreType.DMA((n,)))
        + tuple(pltpu.HBM(a.shape, a.dtype) for a in arrays),
        in_specs=(HBM_SPEC,) * na, out_specs=(SEM_SPEC, SEM_SPEC) + (HBM_SPEC,) * na,
        input_output_aliases={i: 2 + i for i in range(na)},
        compiler_params=pltpu.CompilerParams(has_side_effects=DATAFLOW),
    )(*[_in_hbm(a) for a in arrays])


def _copies_wait(send_sems, recv_sems, arrays, plan, n, after, *, name):
    na = len(arrays)

    def body(*refs):
        ins, send_sems, recv_sems = refs[:na], refs[na], refs[na + 1]
        me = _position()
        for j in range(n):
            src, _, peer, landed = plan(ins, me, j)
            copy = pltpu.make_async_remote_copy(src_ref=src, dst_ref=landed, send_sem=send_sems.at[j],
                                                recv_sem=recv_sems.at[j], device_id=peer, device_id_type=MESH)
            copy.wait_send()
            copy.wait_recv()

    return pl.pallas_call(
        body, name=name, out_shape=tuple(pltpu.HBM(a.shape, a.dtype) for a in arrays),
        in_specs=(HBM_SPEC,) * na + (SEM_SPEC, SEM_SPEC, ANY_SPEC), out_specs=(HBM_SPEC,) * na,
        input_output_aliases={i: i for i in range(na)},
        compiler_params=pltpu.CompilerParams(has_side_effects=DATAFLOW),
    )(*arrays, send_sems, recv_sems, after)


SAME_CORE = (2, 4, 6)


class _TwoLevelGather:
    def __init__(self, shards, lands, *, name):
        n = self.n = len(shards)
        self.name = name
        first_peers = (1,) + SAME_CORE

        def rows(ref, pos):
            r = ref.shape[0] // N_DEV
            return ref.at[pl.ds(_linear(pos) * r, r), :]

        def first(refs, me, j):
            a, peer = j // 4, _peer(me, first_peers[j % 4])
            return refs[a], rows(refs[n + a], me), peer, rows(refs[n + a], peer)

        def second(refs, me, j):
            a, sibling = j // 3, _peer(me, 1)
            mine = rows(refs[a], _peer(me, SAME_CORE[j % 3]))
            return mine, mine, sibling, rows(refs[a], _peer(sibling, SAME_CORE[j % 3]))

        self._first, self._second = first, second
        self._flight = _copies_start(list(shards) + list(lands), first, 4 * n, name=name + "_send")
        self.dep = self._flight[2]

    def pass_on(self, after):
        send1, recv1, *arrays = self._flight
        arrays = _copies_wait(send1, recv1, arrays, self._first, 4 * self.n, after, name=self.name + "_recv")
        self._flight = _copies_start(list(arrays[self.n:]), self._second, 3 * self.n, name=self.name + "_pass")
        return self._flight[2]

    def finish(self, after):
        send2, recv2, *lands = self._flight
        return _copies_wait(send2, recv2, lands, self._second, 3 * self.n, after, name=self.name + "_pass_recv")


def _exchange_start(gs, *, name):
    n = len(gs)
    rows = [g.shape[0] // N_DEV for g in gs]
    lands = [lax.empty((N_DEV - 1, r, g.shape[1]), g.dtype) for g, r in zip(gs, rows)]

    def body(*refs):
        g_refs, land_refs = refs[:n], refs[n:2 * n]
        send_sems, recv_sems = refs[2 * n:3 * n], refs[3 * n:4 * n]
        me = _position()
        for a in range(n):
            for k in SEND_ORDER:
                peer = _peer(me, k)
                pltpu.make_async_remote_copy(
                    src_ref=g_refs[a].at[pl.ds(_linear(peer) * rows[a], rows[a]), :],
                    dst_ref=land_refs[a].at[k - 1],
                    send_sem=send_sems[a].at[k - 1], recv_sem=recv_sems[a].at[k - 1],
                    device_id=peer, device_id_type=MESH).start()

    res = pl.pallas_call(
        body, name=name,
        out_shape=tuple(pltpu.SemaphoreType.DMA((N_DEV - 1,)) for _ in range(2 * n))
        + tuple(pltpu.HBM(a.shape, a.dtype) for a in gs + lands),
        in_specs=(HBM_SPEC,) * (2 * n), out_specs=(SEM_SPEC,) * (2 * n) + (HBM_SPEC,) * (2 * n),
        input_output_aliases={i: 2 * n + i for i in range(2 * n)},
        compiler_params=pltpu.CompilerParams(has_side_effects=DATAFLOW),
    )(*[_in_hbm(a) for a in gs + lands])
    return [(res[a], res[n + a], res[2 * n + a], res[3 * n + a]) for a in range(n)]


def _exchange_wait(send_sems, recv_sems, g_thru, land_thru, after, *, name):
    r = land_thru.shape[1]

    def body(g_ref, land_ref, send_sems, recv_sems, after_ref, g_dead, got_ref):
        del after_ref, g_dead, got_ref
        me = _position()
        for k in SEND_ORDER:
            peer = _peer(me, k)
            copy = pltpu.make_async_remote_copy(
                src_ref=g_ref.at[pl.ds(_linear(peer) * r, r), :], dst_ref=land_ref.at[k - 1],
                send_sem=send_sems.at[k - 1], recv_sem=recv_sems.at[k - 1],
                device_id=peer, device_id_type=MESH)
            copy.wait_send()
            copy.wait_recv()

    return pl.pallas_call(
        body, name=name,
        out_shape=(pltpu.HBM(g_thru.shape, g_thru.dtype), pltpu.HBM(land_thru.shape, land_thru.dtype)),
        in_specs=(HBM_SPEC, HBM_SPEC, SEM_SPEC, SEM_SPEC, pl.BlockSpec(memory_space=pl.ANY)),
        out_specs=(HBM_SPEC, HBM_SPEC), input_output_aliases={0: 0, 1: 1},
        compiler_params=pltpu.CompilerParams(has_side_effects=DATAFLOW),
    )(g_thru, land_thru, send_sems, recv_sems, after)


ADAMW_TILE_ROWS = 256


def _adamw_math(w, g, m, v):
    m = B1 * m + (1.0 - B1) * g
    v = B2 * v + (1.0 - B2) * (g * g)
    delta = -LR * ((m / C1) / (jnp.sqrt(v / C2) + AEPS) + WD * w)
    return delta, m, v


def _sum_adamw(items, *, name):
    n = len(items)
    r, d = items[0][2].shape
    assert all(it[2].shape == (r, d) for it in items)
    rc = r // 2 if r > ADAMW_TILE_ROWS else r
    tiles = [(a, r0) for a in range(n) for r0 in range(0, r, rc)]
    n_in, n_out = 5, 4

    def body(*refs):
        ins, outs = refs[:n_in * n], refs[n_in * n:(n_in + n_out) * n]
        land_v, own_v, f32_v, sems = refs[(n_in + n_out) * n:]
        me_lin = _linear(_position())

        def loads(j):
            a, r0 = tiles[j]
            g_all, land, w, m, v = ins[n_in * a:n_in * a + n_in]
            rows = pl.ds(r0, rc)
            pairs = [(land.at[:, rows, :], land_v.at[j]), (g_all.at[pl.ds(me_lin * r + r0, rc), :], own_v.at[j]),
                     (w.at[rows, :], f32_v.at[j, 0]), (m.at[rows, :], f32_v.at[j, 1]), (v.at[rows, :], f32_v.at[j, 2])]
            return [pltpu.make_async_copy(src, dst, sems.at[j, i]) for i, (src, dst) in enumerate(pairs)]

        def stores(j):
            a, r0 = tiles[j]
            return [pltpu.make_async_copy(f32_v.at[j, 3 + i], outs[n_out * a + i].at[pl.ds(r0, rc), :],
                                          sems.at[j, n_in + i]) for i in range(n_out)]

        for j in range(len(tiles)):
            for cp in loads(j):
                cp.start()
        for j in range(len(tiles)):
            for cp in loads(j):
                cp.wait()
            g = land_v[j, 0].astype(F32)
            for s in range(1, N_DEV - 1):
                g = g + land_v[j, s].astype(F32)
            g = own_v[j].astype(F32) + g
            f32_v[j, 3] = g
            f32_v[j, 4], f32_v[j, 5], f32_v[j, 6] = _adamw_math(f32_v[j, 0], g, f32_v[j, 1], f32_v[j, 2])
            for cp in stores(j):
                cp.start()
        for j in range(len(tiles)):
            for cp in stores(j):
                cp.wait()

    nt = len(tiles)
    res = pl.pallas_call(
        body, name=name,
        out_shape=tuple(jax.ShapeDtypeStruct((r, d), F32) for _ in range(n_out * n)),
        in_specs=[ANY_SPEC] * (n_in * n), out_specs=(ANY_SPEC,) * (n_out * n),
        scratch_shapes=[pltpu.VMEM((nt, N_DEV - 1, rc, d), BF16), pltpu.VMEM((nt, rc, d), BF16),
                        pltpu.VMEM((nt, 3 + n_out, rc, d), F32), pltpu.SemaphoreType.DMA((nt, n_in + n_out))],
        compiler_params=_params(),
    )(*[a for it in items for a in it])
    return [res[n_out * a:n_out * a + n_out] for a in range(n)]


SMALL = ("g_mix_pre", "g_mix_post", "g_mem", "g_x_pre", "g_x_post", "g_ffn_pre", "g_ffn_post",
         "hgrn_onorm", "hgrn_lb", "sinks")
SMALL_W = dict(hgrn_onorm=HD, hgrn_lb=HG_W, sinks=8)
SQ_ROW = len(SMALL)
PACK_ROWS = 16


def _small_pack(parts):
    ns = len(SMALL)

    def body(*refs):
        part, mine, slots, sem = refs[:ns + 1], refs[ns + 1], refs[ns + 2], refs[ns + 3]
        mine[...] = jnp.zeros((PACK_ROWS, D), F32)
        for r, name in enumerate(SMALL):
            wd = SMALL_W.get(name, D)
            mine[r:r + 1, 0:wd] = jnp.sum(part[r][...], axis=0, keepdims=True)[:, 0:wd]
        sq = jnp.sum(part[ns][...]) * (0.5 / D)
        mine[SQ_ROW:SQ_ROW + 1, :] = jnp.full((1, D), sq, F32)
        own = pltpu.make_async_copy(mine, slots.at[_linear(_position())], sem)
        own.start()
        own.wait()

    vmem = pl.BlockSpec(memory_space=pltpu.VMEM)
    return pl.pallas_call(
        body, name="small_pack",
        out_shape=(jax.ShapeDtypeStruct((PACK_ROWS, D), F32), jax.ShapeDtypeStruct((N_DEV, PACK_ROWS, D), F32)),
        in_specs=[vmem] * (ns + 1), out_specs=(vmem, ANY_SPEC),
        scratch_shapes=[pltpu.SemaphoreType.DMA(())], compiler_params=_params(),
    )(*[parts[n] for n in SMALL], parts["sq"])


def _small_exchange(mine, slots):
    def plan(refs, me, j):
        peer = _peer(me, j + 1)
        return refs[0], refs[1].at[_linear(me)], peer, refs[1].at[_linear(peer)]

    send, recv, mine1, slots1 = _copies_start([mine, slots], plan, N_DEV - 1, name="small_send")
    return lambda after: _copies_wait(send, recv, [mine1, slots1], plan, N_DEV - 1, after, name="small_recv")[1]


def _small_update(slots, sm, m_sm, v_sm):
    ns = len(SMALL)

    def body(*refs):
        tot = refs[0][0]
        for s in range(1, N_DEV):
            tot = tot + refs[0][s]
        w_refs, m_refs, v_refs = refs[1:ns + 1], refs[ns + 1:2 * ns + 1], refs[2 * ns + 1:3 * ns + 1]
        outs = refs[3 * ns + 1:]
        loss_ref = outs[0]
        g_out, d_out = outs[1:ns + 1], outs[ns + 1:2 * ns + 1]
        nm_out, nv_out = outs[2 * ns + 1:3 * ns + 1], outs[3 * ns + 1:4 * ns + 1]
        loss_ref[...] = tot[SQ_ROW:SQ_ROW + 1, 0:1]
        for r, name in enumerate(SMALL):
            wd = SMALL_W.get(name, D)
            g = tot[r:r + 1, 0:wd]
            w = w_refs[r][...]
            if name == "hgrn_lb":
                mx = jnp.maximum(w[0:1], w[1:2])
                e0, e1 = jnp.exp(w[0:1] - mx), jnp.exp(w[1:2] - mx)
                lb0 = e0 / (e0 + e1)
                g0 = g * lb0 * (1.0 - lb0)
                for i, gi in enumerate((g0, -g0)):
                    d, nm, nv = _adamw_math(w[i:i + 1], gi, m_refs[r][i:i + 1, :], v_refs[r][i:i + 1, :])
                    g_out[r][i:i + 1, :] = gi
                    d_out[r][i:i + 1, :], nm_out[r][i:i + 1, :], nv_out[r][i:i + 1, :] = d, nm, nv
            else:
                d, nm, nv = _adamw_math(w, g, m_refs[r][...], v_refs[r][...])
                g_out[r][...] = g
                d_out[r][...], nm_out[r][...], nv_out[r][...] = d, nm, nv

    shapes = [jax.ShapeDtypeStruct(sm[n].shape, F32) for n in SMALL]
    res = pl.pallas_call(
        body, name="small_update", out_shape=tuple([jax.ShapeDtypeStruct((1, 1), F32)] + shapes * 4),
        compiler_params=_params(),
    )(slots, *[sm[n] for n in SMALL], *[m_sm[n] for n in SMALL], *[v_sm[n] for n in SMALL])
    groups = [dict(zip(SMALL, res[1 + i * ns:1 + (i + 1) * ns])) for i in range(4)]
    return res[0], groups[0], groups[1], groups[2], groups[3]


BIG = ("w_in", "w_gate", "w_up", "w_down", "w_out", "wq_x", "wk_x", "wv_x", "wo_x")
BIG_KEY = dict(w_in="winT", w_gate="wgT", w_up="wuT", w_down="wd", w_out="wout", wq_x="wq", wk_x="wk",
               wv_x="wv", wo_x="wo")
TRANSPOSED = ("w_in", "w_gate", "w_up")
WEIGHTS = ("w_in", "sinks", "hgrn_lb", "hgrn_onorm", "w_out", "g_mix_pre", "g_mix_post", "g_mem", "g_x_pre",
           "g_x_post", "wq_x", "wk_x", "wv_x", "wo_x", "g_ffn_pre", "g_ffn_post", "w_gate", "w_up", "w_down")


def kernel(x, mem, w_in, sinks, hgrn_lb, hgrn_onorm, w_out, g_mix_pre, g_mix_post, g_mem, g_x_pre, g_x_post, wq_x, wk_x, wv_x, wo_x, g_ffn_pre, g_ffn_post, w_gate, w_up, w_down, loss_target, m_w_in, m_sinks, m_hgrn_lb, m_hgrn_onorm, m_w_out, m_g_mix_pre, m_g_mix_post, m_g_mem, m_g_x_pre, m_g_x_post, m_wq_x, m_wk_x, m_wv_x, m_wo_x, m_g_ffn_pre, m_g_ffn_post, m_w_gate, m_w_up, m_w_down, v_w_in, v_sinks, v_hgrn_lb, v_hgrn_onorm, v_w_out, v_g_mix_pre, v_g_mix_post, v_g_mem, v_g_x_pre, v_g_x_post, v_wq_x, v_wk_x, v_wv_x, v_wo_x, v_g_ffn_pre, v_g_ffn_post, v_w_gate, v_w_up, v_w_down):
    given = dict(locals())
    wts = {n: given[n] for n in WEIGHTS}
    ms = {n: given["m_" + n] for n in WEIGHTS}
    vs = {n: given["v_" + n] for n in WEIGHTS}

    def mat(a, name):
        a = a[0]
        return a.T if name in TRANSPOSED else a

    groups = (("w_in",), ("w_out", "wq_x", "wk_x", "wv_x", "wo_x"), ("w_gate", "w_up", "w_down"))
    gathers = []
    first_dep = None
    for tag, group in zip(("w_in", "w_attn", "w_ffn"), groups):
        shards, lands = _prepare_weights([mat(wts[n], n) for n in group], name="prepare_" + tag, dep=first_dep)
        gathers.append(_TwoLevelGather(shards, lands, name=tag))
        first_dep = gathers[-1].dep
    name_of = {k: n for n, k in BIG_KEY.items()}
    gathered = {}

    def milestone(tag, value):
        return gathers[{"swa": 1, "kv": 2}[tag]].pass_on(value)

    def fetch(key, after):
        name = name_of[key]
        if name not in gathered:
            g = [i for i, group in enumerate(groups) if name in group][0]
            if g == 0:
                gathers[0].pass_on(after)
            gathered.update(zip(groups[g], gathers[g].finish(after)))
        return gathered[name]

    sm = {n: wts[n] for n in SMALL}
    started, held = {}, {}
    send_with = {k: group for group in (("wgT", "wuT"), ("wo", "wq", "wk", "wv")) for k in group}

    def emit(key, g):
        held[key] = g
        group = send_with.get(key, (key,))
        if key != group[-1]:
            return None
        flights = _exchange_start([held[k] for k in group], name="grad_send_" + name_of[group[0]])
        started.update({name_of[k]: f for k, f in zip(group, flights)})
        return flights[-1][2]

    grad_x, _, parts = _local_step(x[0], mem[0], loss_target[0], fetch, sm, emit, first_dep=first_dep, milestone=milestone)
    small_finish = _small_exchange(*_small_pack(parts))
    grads, deltas, new_m, new_v = {}, {}, {}, {}
    after = grad_x
    for group in (("w_down",), ("w_gate", "w_up"), ("wo_x", "wq_x", "wk_x", "wv_x", "w_out"), ("w_in",)):
        items = []
        for n in group:
            g_all, land = _exchange_wait(*started[n], after, name="grad_recv_" + n)
            items.append((g_all, land, mat(wts[n], n), mat(ms[n], n), mat(vs[n], n)))
            after = land
        for n, res in zip(group, _sum_adamw(items, name="adamw_" + group[0])):
            after = res[1]
            if n in TRANSPOSED:
                res = [a.T for a in res]
            grads[n], deltas[n], new_m[n], new_v[n] = [a[None] for a in res]
    loss, g_s, d_s, m_s, v_s = _small_update(small_finish(after), sm, {n: ms[n] for n in SMALL},
                                             {n: vs[n] for n in SMALL})
    grads.update(g_s), deltas.update(d_s), new_m.update(m_s), new_v.update(v_s)
    return (loss[0, 0], grad_x[None], *[grads[n] for n in WEIGHTS], *[deltas[n] for n in WEIGHTS],
            *[new_m[n] for n in WEIGHTS], *[new_v[n] for n in WEIGHTS])
```

```python
import functools

import jax
import jax.numpy as jnp
from jax import lax
from jax.experimental import pallas as pl
from jax.experimental.pallas import tpu as pltpu

F32 = jnp.float32
BF16 = jnp.bfloat16

D = 1024
D_IN = 2816
D_FF = 2816
CHUNK = 64
SWA_W = 512
KV_W = 128
HG_W = 512
HD = 128
ZQH, ZFH, ZIH, ZGH = 768, 1280, 1792, 2304
XH, XD = 4, 256
EPS = 1e-6
NEG = -1e30
N_DEV = 8
MESH = pl.DeviceIdType.MESH

LR, B1, B2, AEPS, WD, STEP = 0.001, 0.9, 0.999, 1e-08, 0.01, 10
C1 = 1.0 - B1 ** STEP
C2 = 1.0 - B2 ** STEP

VMEM_LIMIT = 56 * 1024 * 1024


def _params(**kw):
    return pltpu.CompilerParams(vmem_limit_bytes=VMEM_LIMIT, **kw)


def _sig(x):
    return 1.0 / (1.0 + jnp.exp(-x))


def _rowsum8(x):
    r, w = x.shape
    return jnp.sum(x.reshape(r // 8, 8, w), axis=0)


def _dot(a, b, ca, cb, precision=None):
    return lax.dot_general(a, b, (((ca,), (cb,)), ((), ())), preferred_element_type=F32,
                           precision=precision)


ANY_SPEC = pl.BlockSpec(memory_space=pl.ANY)


def _mm(a, b, *, ta=False, tb=False, out_dtype, tm, tn, tk=None, name, dep=None, n_outer=False):
    m = a.shape[1] if ta else a.shape[0]
    k = a.shape[0] if ta else a.shape[1]
    n = b.shape[0] if tb else b.shape[1]
    tm, tn = min(tm, m), min(tn, n)
    tk = k if tk is None else min(tk, k)
    nk = k // tk
    assert m % tm == 0 and n % tn == 0 and k % tk == 0, (name, m, n, k, tm, tn, tk)
    ij = (lambda g0, g1: (g1, g0)) if n_outer else (lambda g0, g1: (g0, g1))
    a_spec = (pl.BlockSpec((tk, tm), lambda g0, g1, kk: (kk, ij(g0, g1)[0])) if ta
              else pl.BlockSpec((tm, tk), lambda g0, g1, kk: (ij(g0, g1)[0], kk)))
    b_spec = (pl.BlockSpec((tn, tk), lambda g0, g1, kk: (ij(g0, g1)[1], kk)) if tb
              else pl.BlockSpec((tk, tn), lambda g0, g1, kk: (kk, ij(g0, g1)[1])))
    ca, cb = (0 if ta else 1), (1 if tb else 0)

    deps = [] if dep is None else [dep]

    def body(a_ref, b_ref, *rest):
        o_ref, acc = rest[len(deps)], rest[len(deps) + 1:]
        p = _dot(a_ref[...].astype(BF16), b_ref[...].astype(BF16), ca, cb)
        if nk == 1:
            o_ref[...] = p.astype(out_dtype)
        else:
            acc_ref, = acc
            kk = pl.program_id(2)

            @pl.when(kk == 0)
            def _():
                acc_ref[...] = p

            @pl.when(kk > 0)
            def _():
                acc_ref[...] += p

            @pl.when(kk == nk - 1)
            def _():
                o_ref[...] = acc_ref[...].astype(out_dtype)

    return pl.pallas_call(
        body, name=name, out_shape=jax.ShapeDtypeStruct((m, n), out_dtype),
        grid=(n // tn, m // tm, nk) if n_outer else (m // tm, n // tn, nk),
        in_specs=[a_spec, b_spec] + [ANY_SPEC] * len(deps),
        out_specs=pl.BlockSpec((tm, tn), lambda g0, g1, kk: ij(g0, g1)),
        scratch_shapes=[pltpu.VMEM((tm, tn), F32)] if nk > 1 else [],
        compiler_params=_params(dimension_semantics=("parallel", "parallel", "arbitrary")),
    )(a, b, *deps)


TN_FIRST = 256
TN_REST = 1152
TN_SLICES = 4


def _mm_tn(a, b, *, name, dep=None):
    (k, m), n = a.shape, b.shape[1]
    assert a.dtype == BF16 and b.dtype == BF16 and b.shape[0] == k
    widths = [TN_FIRST, TN_FIRST]
    while sum(widths) < m:
        widths.append(min(TN_REST, m - sum(widths)))
    starts = [sum(widths[:i]) for i in range(len(widths))]
    assert sum(widths) == m
    nb = len(widths)
    ks = k // TN_SLICES
    ahead = 2
    deps = [] if dep is None else [dep]

    def body(a_hbm, b_hbm, *rest):
        o_hbm, b_v, sems = rest[len(deps)], rest[len(deps) + 1], rest[-1]
        a_v, o_v = rest[len(deps) + 2:len(deps) + 2 + nb], rest[len(deps) + 2 + nb:-1]
        sliced = []
        for c in range(TN_SLICES):
            rows = pl.ds(c * ks, ks)
            sliced.append((pltpu.make_async_copy(b_hbm.at[rows, :], b_v.at[rows, :], sems.at[2 * c]),
                           pltpu.make_async_copy(a_hbm.at[rows, pl.ds(0, widths[0])], a_v[0].at[rows, :],
                                                 sems.at[2 * c + 1])))
        base = 2 * TN_SLICES - 1
        loads = [None] + [pltpu.make_async_copy(a_hbm.at[:, pl.ds(c0, cw)], a_v[i], sems.at[base + i])
                          for i, (c0, cw) in enumerate(zip(starts, widths)) if i > 0]
        stores = [pltpu.make_async_copy(o_v[i], o_hbm.at[pl.ds(c0, cw), :], sems.at[base + nb + i])
                  for i, (c0, cw) in enumerate(zip(starts, widths))]
        for pair in sliced:
            for cp in pair:
                cp.start()
        for i in range(1, 1 + ahead):
            loads[i].start()
        acc = None
        for c, pair in enumerate(sliced):
            for cp in pair:
                cp.wait()
            p = _dot(a_v[0][c * ks:(c + 1) * ks, :], b_v[c * ks:(c + 1) * ks, :], 0, 0)
            acc = p if acc is None else acc + p
        o_v[0][...] = acc.astype(BF16)
        stores[0].start()
        for i in range(1, nb):
            loads[i].wait()
            if i + ahead < nb:
                loads[i + ahead].start()
            o_v[i][...] = _dot(a_v[i][...], b_v[...], 0, 0).astype(BF16)
            stores[i].start()
        for cp in stores:
            cp.wait()

    return pl.pallas_call(
        body, name=name, out_shape=jax.ShapeDtypeStruct((m, n), BF16),
        in_specs=[ANY_SPEC] * (2 + len(deps)), out_specs=ANY_SPEC,
        scratch_shapes=[pltpu.VMEM((k, n), BF16)] + [pltpu.VMEM((k, cw), BF16) for cw in widths]
        + [pltpu.VMEM((cw, n), BF16) for cw in widths] + [pltpu.SemaphoreType.DMA((2 * TN_SLICES - 1 + 2 * nb,))],
        compiler_params=_params(),
    )(a, b, *deps)


def _mm_rows(prods, rows_in, vecs_in, epilogue, outs, *, tm, name, dep=None):
    m = prods[0][0].shape[0]
    n = prods[0][1].shape[0] if prods[0][2] else prods[0][1].shape[1]
    tm = min(tm, m)
    assert m % tm == 0
    deps = [] if dep is None else [dep]
    n_p, n_r, n_v = len(prods), len(rows_in), len(vecs_in)

    def body(*refs):
        ab = refs[:2 * n_p]
        row_refs = refs[2 * n_p:2 * n_p + n_r]
        vec_refs = refs[2 * n_p + n_r:2 * n_p + n_r + n_v]
        out_refs = refs[2 * n_p + n_r + n_v + len(deps):]
        p = None
        for j, (_, _, tb) in enumerate(prods):
            t = _dot(ab[2 * j][...].astype(BF16), ab[2 * j + 1][...], 1, 1 if tb else 0)
            p = t if p is None else p + t
        vals = epilogue(p, *[r[...] for r in row_refs], *[v[...] for v in vec_refs])
        for (dtype, kind), o_ref, val in zip(outs, out_refs, vals):
            if kind == "row":
                o_ref[...] = val.astype(dtype)
            else:
                @pl.when(pl.program_id(0) == 0)
                def _(o_ref=o_ref):
                    o_ref[...] = jnp.zeros_like(o_ref)

                o_ref[...] += val

    row = lambda w: pl.BlockSpec((tm, w), lambda i: (i, 0))
    whole = lambda a: pl.BlockSpec(a.shape, lambda i: (0,) * a.ndim, pipeline_mode=pl.Buffered(1))
    in_specs, args = [], []
    for a, b, _ in prods:
        in_specs += [row(a.shape[1]), whole(b)]
        args += [a, b]
    in_specs += [row(r.shape[1]) for r in rows_in] + [whole(v) for v in vecs_in] + [ANY_SPEC] * len(deps)
    return pl.pallas_call(
        body, name=name,
        out_shape=tuple(jax.ShapeDtypeStruct((m, n) if kind == "row" else (8, n), dtype) for dtype, kind in outs),
        grid=(m // tm,), in_specs=in_specs,
        out_specs=tuple(row(n) if kind == "row" else pl.BlockSpec((8, n), lambda i: (0, 0)) for _, kind in outs),
        compiler_params=_params(dimension_semantics=("arbitrary",)),
    )(*args, *rows_in, *vecs_in, *deps)


def _rstd(x):
    return lax.rsqrt(jnp.mean(x * x, axis=-1, keepdims=True) + EPS)


def _norm_bwd(xh, r, t):
    return r * (t - xh * jnp.mean(xh * t, axis=-1, keepdims=True))


ROW_F32, ROW_BF16, SUM_F32 = (F32, "row"), (BF16, "row"), (F32, "sum")


def _then(epilogue, index, tb):
    def run(p, *args):
        vals = epilogue(p, *args[:-1])
        return (*vals, _dot(vals[index].astype(BF16), args[-1], 1, 1 if tb else 0))

    return run


def _ep_post_pre(p, h, g_post, g_pre):
    y = p.astype(BF16)
    yf = y.astype(F32)
    hn = h + yf * _rstd(yf) * g_post
    return y, hn, hn * _rstd(hn) * g_pre


_EP_POST_PRE_OUTS = [ROW_BF16, ROW_F32, ROW_BF16]


def _ep_final_loss(y, h, target, g_post):
    r = _rstd(y)
    yh = y * r
    err = h + yh * g_post - target
    dh = err * (1.0 / D)
    return _rowsum8(err * err), dh, _norm_bwd(yh, r, dh * g_post), _rowsum8(dh * yh)


def _ep_post_pre_bwd(du, dh_out, hn, y, g_post, g_pre):
    r2 = _rstd(hn)
    xh = hn * r2
    dh = dh_out + _norm_bwd(xh, r2, du * g_pre)
    yf = y.astype(F32)
    r1 = _rstd(yf)
    yh = yf * r1
    return dh, _norm_bwd(yh, r1, dh * g_post), _rowsum8(du * xh), _rowsum8(dh * yh)


_EP_POST_PRE_BWD_OUTS = [ROW_F32, ROW_BF16, SUM_F32, SUM_F32]


def _ep_pre_bwd(du, dh_out, x, g):
    r = _rstd(x)
    xh = x * r
    return dh_out + _norm_bwd(xh, r, du * g), _rowsum8(du * xh)


_EP_PRE_BWD_OUTS = [ROW_F32, SUM_F32]


def _prenorm(x, g, *, name, dep=None):
    t, d = x.shape
    tb = min(512, t)
    deps = [] if dep is None else [dep]

    def body(x_ref, g_ref, *rest):
        xf = x_ref[...]
        rest[-1][...] = (xf * _rstd(xf) * g_ref[...]).astype(BF16)

    return pl.pallas_call(
        body, name=name, out_shape=jax.ShapeDtypeStruct((t, d), BF16), grid=(t // tb,),
        in_specs=[pl.BlockSpec((tb, d), lambda i: (i, 0)), pl.BlockSpec((1, d), lambda i: (0, 0))]
        + [ANY_SPEC] * len(deps),
        out_specs=pl.BlockSpec((tb, d), lambda i: (i, 0)), compiler_params=_params(),
    )(x, g, *deps)


QB = 256


def _half_mask(shape, e):
    lane = lax.broadcasted_iota(jnp.int32, shape, len(shape) - 1)
    return (lane // 64) == e


def _place(kv):
    sw = pltpu.roll(kv, 64, 1)
    m0 = _half_mask(kv.shape, 0)
    return [[jnp.where(m0, kv, 0.0).astype(BF16), jnp.where(m0, 0.0, sw).astype(BF16)],
            [jnp.where(m0, sw, 0.0).astype(BF16), jnp.where(m0, 0.0, kv).astype(BF16)]]


SQ = 128
SK = 256


def _swa_valid(i, sb):
    qc = lax.broadcasted_iota(jnp.int32, (SQ, SK), 0) // CHUNK
    kc = lax.broadcasted_iota(jnp.int32, (SQ, SK), 1) // CHUNK - 2
    return (kc <= qc) & (qc <= kc + 2) & (4 * i + 2 * sb + kc >= 0)


def _swa_fwd(z, sinks, t, dep=None):
    nb = t // QB
    deps = [] if dep is None else [dep]

    def body(s_ref, q_ref, kp_ref, kc_ref, vp_ref, vc_ref, *rest):
        o_ref, lse_ref = rest[-2:]
        i = pl.program_id(0)
        kpl = _place(jnp.concatenate([kp_ref[...], kc_ref[...]], axis=0))
        vpl = _place(jnp.concatenate([vp_ref[...], vc_ref[...]], axis=0))
        lane = lax.broadcasted_iota(jnp.int32, (SQ, 128), 1)
        for sb in range(QB // SQ):
            rows, keys = slice(SQ * sb, SQ * (sb + 1)), slice(SQ * sb, SQ * sb + SK)
            valid = _swa_valid(i, sb)
            lse_out = jnp.zeros((SQ, 128), F32)
            for j in range(4):
                qp = q_ref[rows, 128 * j:128 * (j + 1)].astype(BF16)
                acc = jnp.zeros((SQ, 128), F32)
                for e in range(2):
                    h = 2 * j + e
                    kvh = h // 4
                    qm = jnp.where(_half_mask(qp.shape, e), qp, jnp.zeros_like(qp))
                    s = _dot(qm, kpl[kvh][e][keys], 1, 1) * 0.125
                    s = jnp.where(valid, s, NEG)
                    sink = s_ref[0, h]
                    m = jnp.maximum(jnp.max(s, axis=-1, keepdims=True), sink)
                    p = jnp.exp(s - m)
                    l = jnp.sum(p, axis=-1, keepdims=True) + jnp.exp(sink - m)
                    acc = acc + _dot(p.astype(BF16), vpl[kvh][e][keys], 1, 0) * (1.0 / l)
                    lse_out = jnp.where(lane == h, m + jnp.log(l), lse_out)
                o_ref[rows, 128 * j:128 * (j + 1)] = acc.astype(BF16)
            lse_ref[rows, :] = lse_out

    prev = lambda c: pl.BlockSpec((128, 128), lambda i: (jnp.maximum(2 * i - 1, 0), c))
    cur = lambda c: pl.BlockSpec((QB, 128), lambda i: (i, c))
    return pl.pallas_call(
        body, name="swa_fwd",
        out_shape=(jax.ShapeDtypeStruct((t, D), BF16), jax.ShapeDtypeStruct((t, 128), F32)),
        grid=(nb,),
        in_specs=[pl.BlockSpec(memory_space=pltpu.SMEM),
                  pl.BlockSpec((QB, SWA_W), lambda i: (i, 0)), prev(4), cur(4), prev(5), cur(5)]
        + [ANY_SPEC] * len(deps),
        out_specs=(pl.BlockSpec((QB, SWA_W), lambda i: (i, 0)), pl.BlockSpec((QB, 128), lambda i: (i, 0))),
        compiler_params=_params(),
    )(sinks, z, z, z, z, z, *deps)


def _swa_bwd(z, sinks, ymix, lse, dymix, t, dep=None):
    nb = t // QB
    deps = [] if dep is None else [dep]

    def body(s_ref, q_ref, kp_ref, kc_ref, vp_ref, vc_ref, o_ref, do_ref, l_ref, *rest):
        dq_ref, first_ref, second_ref, ds_ref, carry_ref = rest[len(deps):]
        i = pl.program_id(0)
        live = i < nb

        @pl.when(i == 0)
        def _():
            ds_ref[...] = jnp.zeros_like(ds_ref)
            carry_ref[...] = jnp.zeros_like(carry_ref)

        lane = lax.broadcasted_iota(jnp.int32, (8, 128), 1)
        kpl = _place(jnp.concatenate([kp_ref[...], kc_ref[...]], axis=0))
        vpl = _place(jnp.concatenate([vp_ref[...], vc_ref[...]], axis=0))
        nk = QB + 128
        qc = lax.broadcasted_iota(jnp.int32, (QB, nk), 0) // CHUNK
        kc = lax.broadcasted_iota(jnp.int32, (QB, nk), 1) // CHUNK - 2
        valid = (kc <= qc) & (qc <= kc + 2) & (4 * i + kc >= 0) & live
        lse_c = l_ref[...]
        dsink = jnp.zeros((8, 128), F32)
        dk_acc = [[jnp.zeros((128, nk), F32) for _ in range(2)] for _ in range(2)]
        dv_acc = [[jnp.zeros((128, nk), F32) for _ in range(2)] for _ in range(2)]
        dq = []
        for j in range(4):
            cols = slice(128 * j, 128 * (j + 1))
            qp = q_ref[:, cols].astype(BF16)
            dop = do_ref[:, cols]
            prod = dop.astype(F32) * o_ref[:, cols].astype(F32)
            acc = jnp.zeros((QB, 128), F32)
            for e in range(2):
                h = 2 * j + e
                kvh = h // 4
                hm = _half_mask(qp.shape, e)
                qm = jnp.where(hm, qp, jnp.zeros_like(qp))
                dom = jnp.where(hm, dop, jnp.zeros_like(dop))
                dd = jnp.sum(jnp.where(hm, prod, 0.0), axis=-1, keepdims=True)
                lse_h = lse_c[:, h:h + 1]
                s = _dot(qm, kpl[kvh][e], 1, 1) * 0.125
                p = jnp.where(valid, jnp.exp(s - lse_h), 0.0)
                dp = _dot(dom, vpl[kvh][e], 1, 1)
                ds = (p * (dp - dd) * 0.125).astype(BF16)
                acc = acc + _dot(ds, kpl[kvh][e], 1, 0)
                dk_acc[kvh][e] = dk_acc[kvh][e] + _dot(qm, ds, 0, 0)
                dv_acc[kvh][e] = dv_acc[kvh][e] + _dot(dom, p.astype(BF16), 0, 0)
                ps = jnp.where(live, jnp.exp(s_ref[0, h] - lse_h) * dd, 0.0)
                dsink = dsink - jnp.where(lane == h, _rowsum8(jnp.broadcast_to(ps, (QB, 128))), 0.0)
            dq.append(acc.astype(BF16))
        ds_ref[...] += dsink
        dk = (dk_acc[0][0] + dk_acc[1][1] + pltpu.roll(dk_acc[0][1] + dk_acc[1][0], 64, 0)).T
        dv = (dv_acc[0][0] + dv_acc[1][1] + pltpu.roll(dv_acc[0][1] + dv_acc[1][0], 64, 0)).T
        dkv = jnp.concatenate([dk, dv], axis=1)
        second_ref[...] = (carry_ref[...] + dkv[0:128]).astype(BF16)
        carry_ref[...] = dkv[256:384]

        @pl.when(live)
        def _():
            for j in range(4):
                dq_ref[:, 128 * j:128 * (j + 1)] = dq[j]
            first_ref[...] = dkv[128:256].astype(BF16)

    blk = lambda i: jnp.minimum(i, nb - 1)
    prev = lambda c: pl.BlockSpec((128, 128), lambda i: (jnp.maximum(2 * blk(i) - 1, 0), c))
    cur = lambda w, c: pl.BlockSpec((QB, w), lambda i: (blk(i), c))
    half = lambda index: pl.BlockSpec((128, 256), lambda i: (index(i), 0))
    return pl.pallas_call(
        body, name="swa_bwd",
        out_shape=(jax.ShapeDtypeStruct((t, SWA_W), BF16), jax.ShapeDtypeStruct((t // 2, 256), BF16),
                   jax.ShapeDtypeStruct((t // 2, 256), BF16), jax.ShapeDtypeStruct((8, 128), F32)),
        grid=(nb + 1,),
        in_specs=[pl.BlockSpec(memory_space=pltpu.SMEM),
                  cur(SWA_W, 0), prev(4), cur(128, 4), prev(5), cur(128, 5),
                  cur(SWA_W, 0), cur(SWA_W, 0), cur(128, 0)] + [ANY_SPEC] * len(deps),
        out_specs=(cur(SWA_W, 0), half(blk), half(lambda i: jnp.maximum(i - 1, 0)),
                   pl.BlockSpec((8, 128), lambda i: (0, 0))),
        scratch_shapes=[pltpu.VMEM((128, 256), F32)],
        compiler_params=_params(dimension_semantics=("arbitrary",)),
    )(sinks, z, z, z, z, z, ymix, dymix, lse, *deps)


HB = 256


def _lower_bound(lb_ref):
    a = lb_ref[...]
    a0, a1 = a[0:1], a[1:2]
    mx = jnp.maximum(a0, a1)
    e0, e1 = jnp.exp(a0 - mx), jnp.exp(a1 - mx)
    return e0 / (e0 + e1)


def _hgrn_cols(row_block):
    return [pl.BlockSpec((HB, 2 * HD), lambda j, c=base // (2 * HD) + p: (row_block(j), c))
            for base in (ZQH, ZFH, ZIH, ZGH) for p in range(2)]


NCH = HB // CHUNK


def _split3(x):
    hi = x.astype(BF16)
    r1 = x - hi.astype(F32)
    mid = r1.astype(BF16)
    return hi, mid, (r1 - mid.astype(F32)).astype(BF16)


def _blockdiag(lower):
    r = lax.broadcasted_iota(jnp.int32, (HB, HB), 0)
    c = lax.broadcasted_iota(jnp.int32, (HB, HB), 1)
    return (r // CHUNK == c // CHUNK) & ((c <= r) if lower else (c >= r))


def _chunk_sums(mask_bf16, x):
    return sum(_dot(mask_bf16, part, 1, 0) for part in _split3(x))


def _per_chunk_rows(x, row):
    w = x.shape[1]
    picked = x.reshape(NCH, CHUNK, w)[:, row:row + 1, :]
    return jnp.broadcast_to(picked, (NCH, CHUNK, w)).reshape(HB, w)


def _chunk_stack(x, chunk_of_row):
    return jnp.concatenate([jnp.where(chunk_of_row == c, x, jnp.zeros_like(x)) for c in range(NCH)], axis=1)


def _chunk_pick(x, chunk_of_row):
    w = x.shape[1] // NCH
    out = jnp.zeros((HB, w), x.dtype)
    for c in range(NCH):
        out = jnp.where(chunk_of_row == c, x[:, c * w:(c + 1) * w], out)
    return out


def _hgrn_local(q, f, kf, b):
    sq = _sig(q)
    qf = q * sq * (HD ** -0.5)
    b_mid = _per_chunk_rows(b, CHUNK // 2 - 1)
    b_last = _per_chunk_rows(b, CHUNK - 1)
    qm = qf * jnp.exp(b - b_mid)
    km = kf * jnp.exp(b_mid - b)
    kl = kf * jnp.exp(b_last - b)
    qb = qf * jnp.exp(b)
    return dict(sq=sq, b_mid=b_mid, b_last=b_last, qm=qm, km=km, kl=kl, qb=qb)


def _hgrn2_fwd(z, hgrn_lb, onorm, ymix, t, dep=None):
    nb = t // HB
    deps = [] if dep is None else [dep]

    def body(*refs):
        zq, zf, zi, zg = refs[0:2], refs[2:4], refs[4:6], refs[6:8]
        (lb_ref, on_ref), (y_ref, o_ref, sp_ref, st_ref) = refs[8:10], refs[-4:]

        @pl.when(pl.program_id(0) == 0)
        def _():
            st_ref[...] = jnp.zeros_like(st_ref)

        lb_all = _lower_bound(lb_ref)
        gn = on_ref[...]
        low = _blockdiag(True)
        low_b = low.astype(BF16)
        chunk_of_row = lax.broadcasted_iota(jnp.int32, (HB, HD), 0) // CHUNK
        for p in range(2):
            lbp = lb_all[:, 2 * HD * p:2 * HD * (p + 1)]
            fp = lbp + (1.0 - lbp) * _sig(zf[p][...])
            bp = _chunk_sums(low_b, jnp.log(fp))
            for e in range(2):
                h, ls = 2 * p + e, slice(e * HD, (e + 1) * HD)
                f = fp[:, ls]
                w = _hgrn_local(zq[p][:, ls], f, 1.0 - f, bp[:, ls])
                iv = zi[p][:, ls].astype(BF16)
                a = jnp.where(low, _dot(w["qm"].astype(BF16), w["km"].astype(BF16), 1, 1), 0.0)
                o = _dot(a.astype(BF16), iv, 1, 0)
                u = _dot(iv, _chunk_stack(w["kl"].astype(BF16), chunk_of_row), 0, 0)
                decay = jnp.exp(w["b_last"])
                st = st_ref[h]
                states = []
                for c in range(NCH):
                    sp_ref[h, c] = st
                    states.append(st.astype(BF16))
                    st = st * decay[c * CHUNK:c * CHUNK + 1] + u[:, c * HD:(c + 1) * HD]
                st_ref[h] = st
                inter = _dot(w["qb"].astype(BF16), jnp.concatenate(states, axis=0), 1, 1)
                o = o + _chunk_pick(inter, chunk_of_row)
                hs = slice(h * HD, (h + 1) * HD)
                o_ref[:, hs] = o
                gg = zg[p][:, ls]
                y_ref[:, hs] = (o * _rstd(o) * gn * (gg * _sig(gg))).astype(BF16)

    return pl.pallas_call(
        body, name="hgrn_fwd",
        out_shape=(jax.ShapeDtypeStruct((t, D), BF16), jax.ShapeDtypeStruct((t, HG_W), F32),
                   jax.ShapeDtypeStruct((4, t // CHUNK, HD, HD), F32)),
        grid=(nb,),
        in_specs=_hgrn_cols(lambda j: j) + [pl.BlockSpec((2, HG_W), lambda j: (0, 0)),
                                            pl.BlockSpec((1, HD), lambda j: (0, 0)), ANY_SPEC]
        + [ANY_SPEC] * len(deps),
        out_specs=(pl.BlockSpec((HB, HG_W), lambda j: (j, 1)),
                   pl.BlockSpec((HB, HG_W), lambda j: (j, 0)),
                   pl.BlockSpec((4, NCH, HD, HD), lambda j: (0, j, 0, 0))),
        scratch_shapes=[pltpu.VMEM((4, HD, HD), F32)],
        input_output_aliases={10: 0},
        compiler_params=_params(dimension_semantics=("arbitrary",)),
    )(*[z] * 8, hgrn_lb, onorm, ymix, *deps)


def _hgrn2_bwd(z, hgrn_lb, onorm, o_save, sprev, dymix, dza, t):
    nb = t // HB

    def body(*refs):
        zq, zf, zi, zg = refs[0:2], refs[2:4], refs[4:6], refs[6:8]
        (lb_ref, on_ref, o_ref, sp_ref, dy_ref, dqa_ref, first_ref, second_ref,
         dz_ref, dlb_ref, don_ref, dst_ref) = refs[8:]

        @pl.when(pl.program_id(0) == 0)
        def _():
            dst_ref[...] = jnp.zeros_like(dst_ref)
            dlb_ref[...] = jnp.zeros_like(dlb_ref)
            don_ref[...] = jnp.zeros_like(don_ref)

        dz_ref[:, 0:SWA_W] = dqa_ref[...]
        dz_ref[0:HB // 2, SWA_W:ZQH] = first_ref[...]
        dz_ref[HB // 2:HB, SWA_W:ZQH] = second_ref[...]
        lb_all = _lower_bound(lb_ref)
        gn = on_ref[...]
        low, upp = _blockdiag(True), _blockdiag(False)
        upp_b = upp.astype(BF16)
        low_b = low.astype(BF16)
        row = lax.broadcasted_iota(jnp.int32, (HB, HD), 0)
        chunk_of_row = row // CHUNK
        in_chunk = row % CHUNK
        for p in range(2):
            lbp = lb_all[:, 2 * HD * p:2 * HD * (p + 1)]
            sgp = _sig(zf[p][...])
            fp = lbp + (1.0 - lbp) * sgp
            bp = _chunk_sums(low_b, jnp.log(fp))
            db_pair, dkf_pair = [], []
            for e in range(2):
                h, ls, hs = 2 * p + e, slice(e * HD, (e + 1) * HD), slice((2 * p + e) * HD, (2 * p + e + 1) * HD)
                f = fp[:, ls]
                q = zq[p][:, ls]
                w = _hgrn_local(q, f, 1.0 - f, bp[:, ls])
                iv = zi[p][:, ls].astype(BF16)
                gg = zg[p][:, ls]
                o = o_ref[:, hs]
                dout = dy_ref[:, hs].astype(F32)
                sgg = _sig(gg)
                r = _rstd(o)
                oh = o * r
                dyn = dout * (gg * sgg)
                dz_ref[:, ZGH + h * HD:ZGH + (h + 1) * HD] = (
                    dout * oh * gn * (sgg * (1.0 + gg * (1.0 - sgg)))).astype(BF16)
                don_ref[...] += _rowsum8(dyn * oh)
                do = _norm_bwd(oh, r, dyn * gn).astype(BF16)
                qm, km, kl, qb = (w[n].astype(BF16) for n in ("qm", "km", "kl", "qb"))
                decay = jnp.exp(w["b_last"])
                grads_in = _dot(do, _chunk_stack(qb, chunk_of_row), 0, 0)
                dst = dst_ref[h]
                dstn, dd_rows = [None] * NCH, [None] * NCH
                for c in reversed(range(NCH)):
                    dstn[c] = dst.astype(BF16)
                    dd_rows[c] = jnp.sum(dst * sp_ref[h, c], axis=0, keepdims=True)
                    dst = dst * decay[c * CHUNK:c * CHUNK + 1] + grads_in[:, c * HD:(c + 1) * HD]
                dst_ref[h] = dst
                states = jnp.concatenate([sp_ref[h, c].astype(BF16) for c in range(NCH)], axis=0)
                dstn_all = jnp.concatenate(dstn, axis=0)
                dqb = _dot(_chunk_stack(do, chunk_of_row), states, 1, 0)
                at = jnp.where(upp, _dot(km, qm, 1, 1), 0.0)
                di = _dot(at.astype(BF16), do, 1, 0) + _chunk_pick(_dot(kl, dstn_all, 1, 1), chunk_of_row)
                dz_ref[:, ZIH + h * HD:ZIH + (h + 1) * HD] = di.astype(BF16)
                dkl = _dot(_chunk_stack(iv, chunk_of_row), dstn_all, 1, 0)
                da = jnp.where(low, _dot(do, iv, 1, 1), 0.0).astype(BF16)
                dat = jnp.where(upp, _dot(iv, do, 1, 1), 0.0).astype(BF16)
                dqm = _dot(da, km, 1, 0)
                dkm = _dot(dat, qm, 1, 0)
                b = bp[:, ls]
                e1, e2 = jnp.exp(b - w["b_mid"]), jnp.exp(w["b_mid"] - b)
                e3, e4 = jnp.exp(w["b_last"] - b), jnp.exp(b)
                dqf = dqm * e1 + dqb * e4
                dkf_pair.append(dkm * e2 + dkl * e3)
                t_qm, t_km, t_kl = dqm * w["qm"], dkm * w["km"], dkl * w["kl"]
                db = t_qm - t_km - t_kl + dqb * w["qb"]
                db_mid = jnp.sum((t_km - t_qm).reshape(NCH, CHUNK, HD), axis=1, keepdims=True)
                db_last = jnp.sum(t_kl.reshape(NCH, CHUNK, HD), axis=1, keepdims=True)
                db_last = db_last + jnp.stack(dd_rows, axis=0) * jnp.exp(
                    bp[:, ls].reshape(NCH, CHUNK, HD)[:, CHUNK - 1:CHUNK, :])
                spread = lambda v: jnp.broadcast_to(v, (NCH, CHUNK, HD)).reshape(HB, HD)
                db = (db + jnp.where(in_chunk == CHUNK // 2 - 1, spread(db_mid), 0.0)
                      + jnp.where(in_chunk == CHUNK - 1, spread(db_last), 0.0))
                db_pair.append(db)
                sq = w["sq"]
                dz_ref[:, ZQH + h * HD:ZQH + (h + 1) * HD] = (
                    dqf * (HD ** -0.5) * (sq * (1.0 + q * (1.0 - sq)))).astype(BF16)
            dlogf = _chunk_sums(upp_b, jnp.concatenate(db_pair, axis=1))
            dfv = dlogf / fp - jnp.concatenate(dkf_pair, axis=1)
            dz_ref[:, ZFH + 2 * HD * p:ZFH + 2 * HD * (p + 1)] = (dfv * (1.0 - lbp) * sgp * (1.0 - sgp)).astype(BF16)
            dlb_ref[:, 2 * HD * p:2 * HD * (p + 1)] += _rowsum8(dfv * (1.0 - sgp))

    rev = lambda j: nb - 1 - j
    return pl.pallas_call(
        body, name="hgrn_bwd",
        out_shape=(jax.ShapeDtypeStruct((t, D_IN), BF16), jax.ShapeDtypeStruct((8, HG_W), F32),
                   jax.ShapeDtypeStruct((8, HD), F32)),
        grid=(nb,),
        in_specs=_hgrn_cols(rev) + [pl.BlockSpec((2, HG_W), lambda j: (0, 0)), pl.BlockSpec((1, HD), lambda j: (0, 0)),
                                    pl.BlockSpec((HB, HG_W), lambda j: (rev(j), 0)),
                                    pl.BlockSpec((4, NCH, HD, HD), lambda j: (0, rev(j), 0, 0)),
                                    pl.BlockSpec((HB, HG_W), lambda j: (rev(j), 1)),
                                    pl.BlockSpec((HB, SWA_W), lambda j: (rev(j), 0)),
                                    pl.BlockSpec((HB // 2, 2 * KV_W), lambda j: (rev(j), 0)),
                                    pl.BlockSpec((HB // 2, 2 * KV_W), lambda j: (rev(j), 0))],
        out_specs=(pl.BlockSpec((HB, D_IN), lambda j: (rev(j), 0)), pl.BlockSpec((8, HG_W), lambda j: (0, 0)),
                   pl.BlockSpec((8, HD), lambda j: (0, 0))),
        scratch_shapes=[pltpu.VMEM((4, HD, HD), F32)],
        compiler_params=_params(dimension_semantics=("arbitrary",)),
    )(*[z] * 8, hgrn_lb, onorm, o_save, sprev, dymix, *dza)


XB = 512


def _xattn_fwd(q, k, v, wo, h, g_post, g_pre, t, dep=None):
    tb = min(XB, t)
    deps = [] if dep is None else [dep]

    def body(q_ref, k_ref, v_ref, wo_ref, h_ref, gp_ref, gn_ref, *rest):
        o_ref, y_ref, hn_ref, u_ref = rest[len(deps):]
        for hd in range(XH):
            cols = slice(XD * hd, XD * (hd + 1))
            s = _dot(q_ref[:, cols], k_ref[:, cols], 1, 1) * (XD ** -0.5)
            p = jnp.exp(s - jnp.max(s, axis=-1, keepdims=True))
            l = jnp.sum(p, axis=-1, keepdims=True)
            o_ref[:, cols] = (_dot(p.astype(BF16), v_ref[:, cols], 1, 0) * (1.0 / l)).astype(BF16)
        y, hn, u = _ep_post_pre(_dot(o_ref[...], wo_ref[...], 1, 0), h_ref[...], gp_ref[...], gn_ref[...])
        y_ref[...] = y
        hn_ref[...] = hn
        u_ref[...] = u.astype(BF16)

    row = pl.BlockSpec((tb, D), lambda i: (i, 0))
    whole = lambda a: pl.BlockSpec(a.shape, lambda i: (0,) * a.ndim, pipeline_mode=pl.Buffered(1))
    half = jax.ShapeDtypeStruct((t, D), BF16)
    return pl.pallas_call(
        body, name="xattn_fwd", out_shape=(half, half, jax.ShapeDtypeStruct((t, D), F32), half), grid=(t // tb,),
        in_specs=[row, whole(k), whole(v), whole(wo), row, whole(g_post), whole(g_pre)] + [ANY_SPEC] * len(deps),
        out_specs=(row, row, row, row), compiler_params=_params(),
    )(q, k, v, wo, h, g_post, g_pre, *deps)


def _xattn_bwd(q, k, v, do, wq, wout, dh_out, hn, y, g_post, g_pre, t):
    tb = min(XB, t)

    def body(q_ref, k_ref, v_ref, do_ref, wq_ref, wout_ref, dho_ref, hn_ref, y_ref, gp_ref, gn_ref,
             dq_ref, dk_ref, dv_ref, dh_ref, dyp_ref, dym_ref, dgn_ref, dgp_ref):
        @pl.when(pl.program_id(0) == 0)
        def _():
            dk_ref[...] = jnp.zeros_like(dk_ref)
            dv_ref[...] = jnp.zeros_like(dv_ref)
            dgn_ref[...] = jnp.zeros_like(dgn_ref)
            dgp_ref[...] = jnp.zeros_like(dgp_ref)

        for h in range(XH):
            cols = slice(XD * h, XD * (h + 1))
            qh, kh, vh, doh = q_ref[:, cols], k_ref[:, cols], v_ref[:, cols], do_ref[:, cols]
            s = _dot(qh, kh, 1, 1) * (XD ** -0.5)
            p = jnp.exp(s - jnp.max(s, axis=-1, keepdims=True))
            p = p * (1.0 / jnp.sum(p, axis=-1, keepdims=True))
            dp = _dot(doh, vh, 1, 1)
            ds = (p * (dp - jnp.sum(p * dp, axis=-1, keepdims=True)) * (XD ** -0.5)).astype(BF16)
            dq_ref[:, cols] = _dot(ds, kh, 1, 0).astype(BF16)
            dk_ref[:, cols] += _dot(ds, qh, 0, 0)
            dv_ref[:, cols] += _dot(p.astype(BF16), doh, 0, 0)
        du = _dot(dq_ref[...], wq_ref[...], 1, 1)
        dh, dyp, dgn, dgp = _ep_post_pre_bwd(du, dho_ref[...], hn_ref[...], y_ref[...], gp_ref[...], gn_ref[...])
        dh_ref[...] = dh
        dyp = dyp.astype(BF16)
        dyp_ref[...] = dyp
        dym_ref[...] = _dot(dyp, wout_ref[...], 1, 1).astype(BF16)
        dgn_ref[...] += dgn
        dgp_ref[...] += dgp

    row = pl.BlockSpec((tb, D), lambda i: (i, 0))
    mem = pl.BlockSpec(k.shape, lambda i: (0, 0))
    whole = lambda a: pl.BlockSpec(a.shape, lambda i: (0,) * a.ndim, pipeline_mode=pl.Buffered(1))
    acc = pl.BlockSpec((8, D), lambda i: (0, 0))
    half = jax.ShapeDtypeStruct((t, D), BF16)
    return pl.pallas_call(
        body, name="xattn_bwd",
        out_shape=(half, jax.ShapeDtypeStruct(k.shape, F32), jax.ShapeDtypeStruct(k.shape, F32),
                   jax.ShapeDtypeStruct((t, D), F32), half, half,
                   jax.ShapeDtypeStruct((8, D), F32), jax.ShapeDtypeStruct((8, D), F32)),
        grid=(t // tb,),
        in_specs=[row, whole(k), whole(v), row, whole(wq), whole(wout), row, row, row, whole(g_post), whole(g_pre)],
        out_specs=(row, mem, mem, row, row, row, acc, acc),
        compiler_params=_params(dimension_semantics=("arbitrary",)),
    )(q, k, v, do, wq, wout, dh_out, hn, y, g_post, g_pre)


def _mem_kv(mem, g_mem, wk, wv):
    def body(m_ref, g_ref, wk_ref, wv_ref, mn_ref, k_ref, v_ref):
        m_ = m_ref[...]
        mn = (m_ * _rstd(m_) * g_ref[...]).astype(BF16)
        mn_ref[...] = mn
        k_ref[...] = _dot(mn, wk_ref[...], 1, 0).astype(BF16)
        v_ref[...] = _dot(mn, wv_ref[...], 1, 0).astype(BF16)

    return pl.pallas_call(body, name="mem_kv", out_shape=(jax.ShapeDtypeStruct(mem.shape, BF16),) * 3,
                          compiler_params=_params())(mem, g_mem, wk, wv)


def _mem_kv_bwd(mn, mem, dk, dv, wk, wv, dep=None):
    deps = [] if dep is None else [dep]

    def body(mn_ref, m_ref, dk_ref, dv_ref, wk_ref, wv_ref, *rest):
        gk_ref, gv_ref, dg_ref = rest[len(deps):]
        mn = mn_ref[...]
        dkb, dvb = dk_ref[...].astype(BF16), dv_ref[...].astype(BF16)
        gk_ref[...] = _dot(mn, dkb, 0, 0).astype(BF16)
        gv_ref[...] = _dot(mn, dvb, 0, 0).astype(BF16)
        dmn = _dot(dkb, wk_ref[...], 1, 1) + _dot(dvb, wv_ref[...], 1, 1)
        m_ = m_ref[...]
        dg_ref[...] = _rowsum8(dmn * (m_ * _rstd(m_)))

    vmem = pl.BlockSpec(memory_space=pltpu.VMEM)
    return pl.pallas_call(
        body, name="mem_kv_bwd",
        out_shape=(jax.ShapeDtypeStruct(wk.shape, BF16), jax.ShapeDtypeStruct(wv.shape, BF16),
                   jax.ShapeDtypeStruct((8, D), F32)),
        in_specs=[vmem] * 6 + [ANY_SPEC] * len(deps), out_specs=(vmem,) * 3, compiler_params=_params(),
    )(mn, mem, dk, dv, wk, wv, *deps)


FB = 256


def _ffn_fwd_bwd(u, wgt, wut, wd, h, target, g_last, y_prev, g_post, g_pre, wo, t):
    tb = min(FB, t)

    def body(u_ref, wg_ref, wu_ref, wd_ref, h_ref, t_ref, gl_ref, yp_ref, gp_ref, gn_ref, wo_ref,
             a_ref, dy_ref, dg_ref, dup_ref, dh_ref, dyp_ref, do_ref, sq_ref, dgl_ref, dgn_ref, dgp_ref):
        @pl.when(pl.program_id(0) == 0)
        def _():
            for ref in (sq_ref, dgl_ref, dgn_ref, dgp_ref):
                ref[...] = jnp.zeros_like(ref)

        u_ = u_ref[...]
        g = _dot(u_, wg_ref[...], 1, 1)
        up = _dot(u_, wu_ref[...], 1, 1)
        sg = _sig(g)
        a = (g * sg * up).astype(BF16)
        a_ref[...] = a
        h_ = h_ref[...]
        sq, dh3, dy, dgl = _ep_final_loss(_dot(a, wd_ref[...], 1, 0), h_, t_ref[...], gl_ref[...])
        sq_ref[...] += sq
        dgl_ref[...] += dgl
        dy = dy.astype(BF16)
        dy_ref[...] = dy
        da = _dot(dy, wd_ref[...], 1, 1)
        dup = (da * g * sg).astype(BF16)
        dgate = (da * up * (sg * (1.0 + g * (1.0 - sg)))).astype(BF16)
        dup_ref[...] = dup
        dg_ref[...] = dgate
        du = _dot(dgate, wg_ref[...], 1, 0) + _dot(dup, wu_ref[...], 1, 0)
        dh, dyp, dgn, dgp = _ep_post_pre_bwd(du, dh3, h_, yp_ref[...], gp_ref[...], gn_ref[...])
        dh_ref[...] = dh
        dyp = dyp.astype(BF16)
        dyp_ref[...] = dyp
        do_ref[...] = _dot(dyp, wo_ref[...], 1, 1).astype(BF16)
        dgn_ref[...] += dgn
        dgp_ref[...] += dgp

    row = lambda w: pl.BlockSpec((tb, w), lambda i: (i, 0))
    whole = lambda a: pl.BlockSpec(a.shape, lambda i: (0,) * a.ndim, pipeline_mode=pl.Buffered(1))
    acc = pl.BlockSpec((8, D), lambda i: (0, 0))
    wide, half, sums = (jax.ShapeDtypeStruct((t, D_FF), BF16), jax.ShapeDtypeStruct((t, D), BF16),
                        jax.ShapeDtypeStruct((8, D), F32))
    return pl.pallas_call(
        body, name="ffn_fwd_bwd",
        out_shape=(wide, half, wide, wide, jax.ShapeDtypeStruct((t, D), F32), half, half, sums, sums, sums, sums),
        grid=(t // tb,),
        in_specs=[row(D), whole(wgt), whole(wut), whole(wd), row(D), row(D), whole(g_last), row(D), whole(g_post),
                  whole(g_pre), whole(wo)],
        out_specs=(row(D_FF), row(D), row(D_FF), row(D_FF), row(D), row(D), row(D), acc, acc, acc, acc),
        compiler_params=_params(dimension_semantics=("arbitrary",)),
    )(u, wgt, wut, wd, h, target, g_last, y_prev, g_post, g_pre, wo)


def _local_step(x, mem, target, fetch, sm, emit=None, first_dep=None, milestone=None):
    t = x.shape[0]
    w, gw = {}, {}

    def out(key, g):
        gw[key] = g
        return None if emit is None else emit(key, g)

    def tell(tag, value):
        return None if milestone is None else milestone(tag, value)
    u1 = _prenorm(x, sm["g_mix_pre"], name="prenorm_mix", dep=first_dep)
    w["winT"] = fetch("winT", u1)
    z = _mm(u1, w["winT"], tb=True, out_dtype=F32, tm=1024, tn=1408, name="mm_z", n_outer=True)
    ymix, lse = _swa_fwd(z, sm["sinks"], t)
    ymix, o_h, sprev = _hgrn2_fwd(z, sm["hgrn_lb"], sm["hgrn_onorm"], ymix, t, dep=tell("swa", lse))
    for key in ("wout", "wq", "wk", "wv", "wo"):
        w[key] = fetch(key, ymix)
    y1, h1, u2, qx = _mm_rows([(ymix, w["wout"], False)], [x], [sm["g_mix_post"], sm["g_x_pre"], w["wq"]],
                              _then(_ep_post_pre, 2, False), _EP_POST_PRE_OUTS + [ROW_BF16], tm=1024,
                              name="mm_y1_post_qx")
    mn, kx, vx = _mem_kv(mem, sm["g_mem"], w["wk"], w["wv"])
    ox, y2, h2, u3 = _xattn_fwd(qx, kx, vx, w["wo"], h1, sm["g_x_post"], sm["g_ffn_pre"], t, dep=tell("kv", kx))
    for key in ("wgT", "wuT", "wd"):
        w[key] = fetch(key, u3)
    act, dy3, dgate, dup, dh2, dy2, dox, sq, dg_ffn_post, dg_ffn_pre, dg_x_post = _ffn_fwd_bwd(
        u3, w["wgT"], w["wuT"], w["wd"], h2, target, sm["g_ffn_post"], y2, sm["g_x_post"], sm["g_ffn_pre"], w["wo"], t)
    dep = out("wd", _mm_tn(act, dy3, name="mm_gwd"))
    dep = out("wgT", _mm_tn(dgate, u3, name="mm_gwg", dep=dep))
    dep = out("wuT", _mm_tn(dup, u3, name="mm_gwu", dep=dep))
    out("wo", _mm_tn(ox, dy2, name="mm_gwo", dep=dep))
    dqx, dkx, dvx, dh1, dy1, dymix, dg_x_pre, dg_mix_post = _xattn_bwd(
        qx, kx, vx, dox, w["wq"], w["wout"], dh2, h1, y1, sm["g_mix_post"], sm["g_x_pre"], t)
    out("wq", _mm_tn(u2, dqx, name="mm_gwq"))
    gwk, gwv, dg_mem = _mem_kv_bwd(mn, mem, dkx, dvx, w["wk"], w["wv"])
    out("wk", gwk)
    dep = out("wv", gwv)
    dep = out("wout", _mm_tn(ymix, dy1, name="mm_gwout", dep=dep))
    *dza, dsinks = _swa_bwd(z, sm["sinks"], ymix, lse, dymix, t, dep=dep)
    dz, dlb, donorm = _hgrn2_bwd(z, sm["hgrn_lb"], sm["hgrn_onorm"], o_h, sprev, dymix, dza, t)
    dep = out("winT", _mm_tn(dz, u1, name="mm_gwin"))
    grad_x, dg_mix_pre = _mm_rows([(dz, w["winT"], False)], [dh1, x], [sm["g_mix_pre"]], _ep_pre_bwd,
                                  _EP_PRE_BWD_OUTS, tm=512, name="mm_du1_pre_bwd", dep=dep)
    parts = dict(g_mix_pre=dg_mix_pre, g_mix_post=dg_mix_post, g_mem=dg_mem, g_x_pre=dg_x_pre,
                 g_x_post=dg_x_post, g_ffn_pre=dg_ffn_pre, g_ffn_post=dg_ffn_post,
                 hgrn_onorm=donorm, hgrn_lb=dlb, sinks=dsinks, sq=sq)
    return grad_x, gw, parts


def _position():
    return lax.axis_index("x"), lax.axis_index("y"), lax.axis_index("c")


def _peer(pos, k):
    x, y, c = pos
    return (1 - x if k & 4 else x, 1 - y if k & 2 else y, 1 - c if k & 1 else c)


def _linear(pos):
    x, y, c = pos
    return 4 * x + 2 * y + c


HBM_SPEC = pl.BlockSpec(memory_space=pltpu.HBM)
SEM_SPEC = pl.BlockSpec(memory_space=pltpu.SEMAPHORE)
DATAFLOW = pltpu.SideEffectType.DATAFLOW_SIDE_EFFECTING
SEND_ORDER = (1, 2, 4, 3, 5, 6, 7)


def _in_hbm(a):
    return pltpu.with_memory_space_constraint(a, pltpu.HBM)


def _prepare_weights(shards, *, name, dep=None):
    n = len(shards)
    deps = [] if dep is None else [dep]

    def body(*refs):
        ins, (outs, lands, sem) = refs[:n], (refs[-2 * n - 1:-n - 1], refs[-n - 1:-1], refs[-1])
        me_lin = _linear(_position())
        copies = []
        for a in range(n):
            r = ins[a].shape[0]
            outs[a][...] = ins[a][...].astype(BF16)
            copies.append(pltpu.make_async_copy(outs[a], lands[a].at[pl.ds(me_lin * r, r), :], sem.at[a]))
            copies[-1].start()
        for cp in copies:
            cp.wait()

    vmem = pl.BlockSpec(memory_space=pltpu.VMEM)
    res = pl.pallas_call(
        body, name=name,
        out_shape=tuple(jax.ShapeDtypeStruct(s.shape, BF16) for s in shards)
        + tuple(jax.ShapeDtypeStruct((N_DEV * s.shape[0], s.shape[1]), BF16) for s in shards),
        in_specs=[vmem] * n + [ANY_SPEC] * len(deps), out_specs=tuple([vmem] * n + [ANY_SPEC] * n),
        scratch_shapes=[pltpu.SemaphoreType.DMA((n,))], compiler_params=_params(),
    )(*shards, *deps)
    return res[:n], res[n:]


def _copies_start(arrays, plan, n, *, name):
    na = len(arrays)

    def body(*refs):
        ins, send_sems, recv_sems = refs[:na], refs[na], refs[na + 1]
        me = _position()
        for j in range(n):
            src, dst, peer, _ = plan(ins, me, j)
            pltpu.make_async_remote_copy(src_ref=src, dst_ref=dst, send_sem=send_sems.at[j], recv_sem=recv_sems.at[j],
                                         device_id=peer, device_id_type=MESH).start()

    return pl.pallas_call(
        body, name=name,
        out_shape=(pltpu.SemaphoreType.DMA((n,)), pltpu.SemaphoreType.DMA((n,)))
        + tuple(pltpu.HBM(a.shape, a.dtype) for a in arrays),
        in_specs=(HBM_SPEC,) * na, out_specs=(SEM_SPEC, SEM_SPEC) + (HBM_SPEC,) * na,
        input_output_aliases={i: 2 + i for i in range(na)},
        compiler_params=pltpu.CompilerParams(has_side_effects=DATAFLOW),
    )(*[_in_hbm(a) for a in arrays])


def _copies_wait(send_sems, recv_sems, arrays, plan, n, after, *, name):
    na = len(arrays)

    def body(*refs):
        ins, send_sems, recv_sems = refs[:na], refs[na], refs[na + 1]
        me = _position()
        for j in range(n):
            src, _, peer, landed = plan(ins, me, j)
            copy = pltpu.make_async_remote_copy(src_ref=src, dst_ref=landed, send_sem=send_sems.at[j],
                                                recv_sem=recv_sems.at[j], device_id=peer, device_id_type=MESH)
            copy.wait_send()
            copy.wait_recv()

    return pl.pallas_call(
        body, name=name, out_shape=tuple(pltpu.HBM(a.shape, a.dtype) for a in arrays),
        in_specs=(HBM_SPEC,) * na + (SEM_SPEC, SEM_SPEC, ANY_SPEC), out_specs=(HBM_SPEC,) * na,
        input_output_aliases={i: i for i in range(na)},
        compiler_params=pltpu.CompilerParams(has_side_effects=DATAFLOW),
    )(*arrays, send_sems, recv_sems, after)


SAME_CORE = (2, 4, 6)


class _TwoLevelGather:
    def __init__(self, shards, lands, *, name):
        n = self.n = len(shards)
        self.name = name
        first_peers = (1,) + SAME_CORE

        def rows(ref, pos):
            r = ref.shape[0] // N_DEV
            return ref.at[pl.ds(_linear(pos) * r, r), :]

        def first(refs, me, j):
            a, peer = j // 4, _peer(me, first_peers[j % 4])
            return refs[a], rows(refs[n + a], me), peer, rows(refs[n + a], peer)

        def second(refs, me, j):
            a, sibling = j // 3, _peer(me, 1)
            mine = rows(refs[a], _peer(me, SAME_CORE[j % 3]))
            return mine, mine, sibling, rows(refs[a], _peer(sibling, SAME_CORE[j % 3]))

        self._first, self._second = first, second
        self._flight = _copies_start(list(shards) + list(lands), first, 4 * n, name=name + "_send")
        self.dep = self._flight[2]

    def pass_on(self, after):
        send1, recv1, *arrays = self._flight
        arrays = _copies_wait(send1, recv1, arrays, self._first, 4 * self.n, after, name=self.name + "_recv")
        self._flight = _copies_start(list(arrays[self.n:]), self._second, 3 * self.n, name=self.name + "_pass")
        return self._flight[2]

    def finish(self, after):
        send2, recv2, *lands = self._flight
        return _copies_wait(send2, recv2, lands, self._second, 3 * self.n, after, name=self.name + "_pass_recv")


def _exchange_start(gs, *, name):
    n = len(gs)
    rows = [g.shape[0] // N_DEV for g in gs]
    lands = [lax.empty((N_DEV - 1, r, g.shape[1]), g.dtype) for g, r in zip(gs, rows)]

    def body(*refs):
        g_refs, land_refs = refs[:n], refs[n:2 * n]
        send_sems, recv_sems = refs[2 * n:3 * n], refs[3 * n:4 * n]
        me = _position()
        for a in range(n):
            for k in SEND_ORDER:
                peer = _peer(me, k)
                pltpu.make_async_remote_copy(
                    src_ref=g_refs[a].at[pl.ds(_linear(peer) * rows[a], rows[a]), :],
                    dst_ref=land_refs[a].at[k - 1],
                    send_sem=send_sems[a].at[k - 1], recv_sem=recv_sems[a].at[k - 1],
                    device_id=peer, device_id_type=MESH).start()

    res = pl.pallas_call(
        body, name=name,
        out_shape=tuple(pltpu.SemaphoreType.DMA((N_DEV - 1,)) for _ in range(2 * n))
        + tuple(pltpu.HBM(a.shape, a.dtype) for a in gs + lands),
        in_specs=(HBM_SPEC,) * (2 * n), out_specs=(SEM_SPEC,) * (2 * n) + (HBM_SPEC,) * (2 * n),
        input_output_aliases={i: 2 * n + i for i in range(2 * n)},
        compiler_params=pltpu.CompilerParams(has_side_effects=DATAFLOW),
    )(*[_in_hbm(a) for a in gs + lands])
    return [(res[a], res[n + a], res[2 * n + a], res[3 * n + a]) for a in range(n)]


def _exchange_wait(send_sems, recv_sems, g_thru, land_thru, after, *, name):
    r = land_thru.shape[1]

    def body(g_ref, land_ref, send_sems, recv_sems, after_ref, g_dead, got_ref):
        del after_ref, g_dead, got_ref
        me = _position()
        for k in SEND_ORDER:
            peer = _peer(me, k)
            copy = pltpu.make_async_remote_copy(
                src_ref=g_ref.at[pl.ds(_linear(peer) * r, r), :], dst_ref=land_ref.at[k - 1],
                send_sem=send_sems.at[k - 1], recv_sem=recv_sems.at[k - 1],
                device_id=peer, device_id_type=MESH)
            copy.wait_send()
            copy.wait_recv()

    return pl.pallas_call(
        body, name=name,
        out_shape=(pltpu.HBM(g_thru.shape, g_thru.dtype), pltpu.HBM(land_thru.shape, land_thru.dtype)),
        in_specs=(HBM_SPEC, HBM_SPEC, SEM_SPEC, SEM_SPEC, pl.BlockSpec(memory_space=pl.ANY)),
        out_specs=(HBM_SPEC, HBM_SPEC), input_output_aliases={0: 0, 1: 1},
        compiler_params=pltpu.CompilerParams(has_side_effects=DATAFLOW),
    )(g_thru, land_thru, send_sems, recv_sems, after)


ADAMW_TILE_ROWS = 256


def _adamw_math(w, g, m, v):
    m = B1 * m + (1.0 - B1) * g
    v = B2 * v + (1.0 - B2) * (g * g)
    delta = -LR * ((m / C1) / (jnp.sqrt(v / C2) + AEPS) + WD * w)
    return delta, m, v


def _sum_adamw(items, *, name):
    n = len(items)
    r, d = items[0][2].shape
    assert all(it[2].shape == (r, d) for it in items)
    rc = r // 2 if r > ADAMW_TILE_ROWS else r
    tiles = [(a, r0) for a in range(n) for r0 in range(0, r, rc)]
    n_in, n_out = 5, 4

    def body(*refs):
        ins, outs = refs[:n_in * n], refs[n_in * n:(n_in + n_out) * n]
        land_v, own_v, f32_v, sems = refs[(n_in + n_out) * n:]
        me_lin = _linear(_position())

        def loads(j):
            a, r0 = tiles[j]
            g_all, land, w, m, v = ins[n_in * a:n_in * a + n_in]
            rows = pl.ds(r0, rc)
            pairs = [(land.at[:, rows, :], land_v.at[j]), (g_all.at[pl.ds(me_lin * r + r0, rc), :], own_v.at[j]),
                     (w.at[rows, :], f32_v.at[j, 0]), (m.at[rows, :], f32_v.at[j, 1]), (v.at[rows, :], f32_v.at[j, 2])]
            return [pltpu.make_async_copy(src, dst, sems.at[j, i]) for i, (src, dst) in enumerate(pairs)]

        def stores(j):
            a, r0 = tiles[j]
            return [pltpu.make_async_copy(f32_v.at[j, 3 + i], outs[n_out * a + i].at[pl.ds(r0, rc), :],
                                          sems.at[j, n_in + i]) for i in range(n_out)]

        for j in range(len(tiles)):
            for cp in loads(j):
                cp.start()
        for j in range(len(tiles)):
            for cp in loads(j):
                cp.wait()
            g = land_v[j, 0].astype(F32)
            for s in range(1, N_DEV - 1):
                g = g + land_v[j, s].astype(F32)
            g = own_v[j].astype(F32) + g
            f32_v[j, 3] = g
            f32_v[j, 4], f32_v[j, 5], f32_v[j, 6] = _adamw_math(f32_v[j, 0], g, f32_v[j, 1], f32_v[j, 2])
            for cp in stores(j):
                cp.start()
        for j in range(len(tiles)):
            for cp in stores(j):
                cp.wait()

    nt = len(tiles)
    res = pl.pallas_call(
        body, name=name,
        out_shape=tuple(jax.ShapeDtypeStruct((r, d), F32) for _ in range(n_out * n)),
        in_specs=[ANY_SPEC] * (n_in * n), out_specs=(ANY_SPEC,) * (n_out * n),
        scratch_shapes=[pltpu.VMEM((nt, N_DEV - 1, rc, d), BF16), pltpu.VMEM((nt, rc, d), BF16),
                        pltpu.VMEM((nt, 3 + n_out, rc, d), F32), pltpu.SemaphoreType.DMA((nt, n_in + n_out))],
        compiler_params=_params(),
    )(*[a for it in items for a in it])
    return [res[n_out * a:n_out * a + n_out] for a in range(n)]


SMALL = ("g_mix_pre", "g_mix_post", "g_mem", "g_x_pre", "g_x_post", "g_ffn_pre", "g_ffn_post",
         "hgrn_onorm", "hgrn_lb", "sinks")
SMALL_W = dict(hgrn_onorm=HD, hgrn_lb=HG_W, sinks=8)
SQ_ROW = len(SMALL)
PACK_ROWS = 16


def _small_pack(parts):
    ns = len(SMALL)

    def body(*refs):
        part, mine, slots, sem = refs[:ns + 1], refs[ns + 1], refs[ns + 2], refs[ns + 3]
        mine[...] = jnp.zeros((PACK_ROWS, D), F32)
        for r, name in enumerate(SMALL):
            wd = SMALL_W.get(name, D)
            mine[r:r + 1, 0:wd] = jnp.sum(part[r][...], axis=0, keepdims=True)[:, 0:wd]
        sq = jnp.sum(part[ns][...]) * (0.5 / D)
        mine[SQ_ROW:SQ_ROW + 1, :] = jnp.full((1, D), sq, F32)
        own = pltpu.make_async_copy(mine, slots.at[_linear(_position())], sem)
        own.start()
        own.wait()

    vmem = pl.BlockSpec(memory_space=pltpu.VMEM)
    return pl.pallas_call(
        body, name="small_pack",
        out_shape=(jax.ShapeDtypeStruct((PACK_ROWS, D), F32), jax.ShapeDtypeStruct((N_DEV, PACK_ROWS, D), F32)),
        in_specs=[vmem] * (ns + 1), out_specs=(vmem, ANY_SPEC),
        scratch_shapes=[pltpu.SemaphoreType.DMA(())], compiler_params=_params(),
    )(*[parts[n] for n in SMALL], parts["sq"])


def _small_exchange(mine, slots):
    def plan(refs, me, j):
        peer = _peer(me, j + 1)
        return refs[0], refs[1].at[_linear(me)], peer, refs[1].at[_linear(peer)]

    send, recv, mine1, slots1 = _copies_start([mine, slots], plan, N_DEV - 1, name="small_send")
    return lambda after: _copies_wait(send, recv, [mine1, slots1], plan, N_DEV - 1, after, name="small_recv")[1]


def _small_update(slots, sm, m_sm, v_sm):
    ns = len(SMALL)

    def body(*refs):
        tot = refs[0][0]
        for s in range(1, N_DEV):
            tot = tot + refs[0][s]
        w_refs, m_refs, v_refs = refs[1:ns + 1], refs[ns + 1:2 * ns + 1], refs[2 * ns + 1:3 * ns + 1]
        outs = refs[3 * ns + 1:]
        loss_ref = outs[0]
        g_out, d_out = outs[1:ns + 1], outs[ns + 1:2 * ns + 1]
        nm_out, nv_out = outs[2 * ns + 1:3 * ns + 1], outs[3 * ns + 1:4 * ns + 1]
        loss_ref[...] = tot[SQ_ROW:SQ_ROW + 1, 0:1]
        for r, name in enumerate(SMALL):
            wd = SMALL_W.get(name, D)
            g = tot[r:r + 1, 0:wd]
            w = w_refs[r][...]
            if name == "hgrn_lb":
                mx = jnp.maximum(w[0:1], w[1:2])
                e0, e1 = jnp.exp(w[0:1] - mx), jnp.exp(w[1:2] - mx)
                lb0 = e0 / (e0 + e1)
                g0 = g * lb0 * (1.0 - lb0)
                for i, gi in enumerate((g0, -g0)):
                    d, nm, nv = _adamw_math(w[i:i + 1], gi, m_refs[r][i:i + 1, :], v_refs[r][i:i + 1, :])
                    g_out[r][i:i + 1, :] = gi
                    d_out[r][i:i + 1, :], nm_out[r][i:i + 1, :], nv_out[r][i:i + 1, :] = d, nm, nv
            else:
                d, nm, nv = _adamw_math(w, g, m_refs[r][...], v_refs[r][...])
                g_out[r][...] = g
                d_out[r][...], nm_out[r][...], nv_out[r][...] = d, nm, nv

    shapes = [jax.ShapeDtypeStruct(sm[n].shape, F32) for n in SMALL]
    res = pl.pallas_call(
        body, name="small_update", out_shape=tuple([jax.ShapeDtypeStruct((1, 1), F32)] + shapes * 4),
        compiler_params=_params(),
    )(slots, *[sm[n] for n in SMALL], *[m_sm[n] for n in SMALL], *[v_sm[n] for n in SMALL])
    groups = [dict(zip(SMALL, res[1 + i * ns:1 + (i + 1) * ns])) for i in range(4)]
    return res[0], groups[0], groups[1], groups[2], groups[3]


BIG = ("w_in", "w_gate", "w_up", "w_down", "w_out", "wq_x", "wk_x", "wv_x", "wo_x")
BIG_KEY = dict(w_in="winT", w_gate="wgT", w_up="wuT", w_down="wd", w_out="wout", wq_x="wq", wk_x="wk",
               wv_x="wv", wo_x="wo")
TRANSPOSED = ("w_in", "w_gate", "w_up")
WEIGHTS = ("w_in", "sinks", "hgrn_lb", "hgrn_onorm", "w_out", "g_mix_pre", "g_mix_post", "g_mem", "g_x_pre",
           "g_x_post", "wq_x", "wk_x", "wv_x", "wo_x", "g_ffn_pre", "g_ffn_post", "w_gate", "w_up", "w_down")


def kernel(x, mem, w_in, sinks, hgrn_lb, hgrn_onorm, w_out, g_mix_pre, g_mix_post, g_mem, g_x_pre, g_x_post, wq_x, wk_x, wv_x, wo_x, g_ffn_pre, g_ffn_post, w_gate, w_up, w_down, loss_target, m_w_in, m_sinks, m_hgrn_lb, m_hgrn_onorm, m_w_out, m_g_mix_pre, m_g_mix_post, m_g_mem, m_g_x_pre, m_g_x_post, m_wq_x, m_wk_x, m_wv_x, m_wo_x, m_g_ffn_pre, m_g_ffn_post, m_w_gate, m_w_up, m_w_down, v_w_in, v_sinks, v_hgrn_lb, v_hgrn_onorm, v_w_out, v_g_mix_pre, v_g_mix_post, v_g_mem, v_g_x_pre, v_g_x_post, v_wq_x, v_wk_x, v_wv_x, v_wo_x, v_g_ffn_pre, v_g_ffn_post, v_w_gate, v_w_up, v_w_down):
    given = dict(locals())
    wts = {n: given[n] for n in WEIGHTS}
    ms = {n: given["m_" + n] for n in WEIGHTS}
    vs = {n: given["v_" + n] for n in WEIGHTS}

    def mat(a, name):
        a = a[0]
        return a.T if name in TRANSPOSED else a

    groups = (("w_in",), ("w_out", "wq_x", "wk_x", "wv_x", "wo_x"), ("w_gate", "w_up", "w_down"))
    gathers = []
    first_dep = None
    for tag, group in zip(("w_in", "w_attn", "w_ffn"), groups):
        shards, lands = _prepare_weights([mat(wts[n], n) for n in group], name="prepare_" + tag, dep=first_dep)
        gathers.append(_TwoLevelGather(shards, lands, name=tag))
        first_dep = gathers[-1].dep
    name_of = {k: n for n, k in BIG_KEY.items()}
    gathered = {}

    def milestone(tag, value):
        return gathers[{"swa": 1, "kv": 2}[tag]].pass_on(value)

    def fetch(key, after):
        name = name_of[key]
        if name not in gathered:
            g = [i for i, group in enumerate(groups) if name in group][0]
            if g == 0:
                gathers[0].pass_on(after)
            gathered.update(zip(groups[g], gathers[g].finish(after)))
        return gathered[name]

    sm = {n: wts[n] for n in SMALL}
    started, held = {}, {}
    send_with = {k: group for group in (("wd", "wgT", "wuT"), ("wo", "wq", "wk", "wv", "wout")) for k in group}

    def emit(key, g):
        held[key] = g
        group = send_with.get(key, (key,))
        if key != group[-1]:
            return None
        flights = _exchange_start([held[k] for k in group], name="grad_send_" + name_of[group[0]])
        started.update({name_of[k]: f for k, f in zip(group, flights)})
        return flights[-1][2]

    grad_x, _, parts = _local_step(x[0], mem[0], loss_target[0], fetch, sm, emit, first_dep=first_dep, milestone=milestone)
    small_finish = _small_exchange(*_small_pack(parts))
    grads, deltas, new_m, new_v = {}, {}, {}, {}
    after = grad_x
    for group in (("w_down",), ("w_gate", "w_up"), ("wo_x", "wq_x", "wk_x", "wv_x", "w_out"), ("w_in",)):
        items = []
        for n in group:
            g_all, land = _exchange_wait(*started[n], after, name="grad_recv_" + n)
            items.append((g_all, land, mat(wts[n], n), mat(ms[n], n), mat(vs[n], n)))
            after = land
        for n, res in zip(group, _sum_adamw(items, name="adamw_" + group[0])):
            after = res[1]
            if n in TRANSPOSED:
                res = [a.T for a in res]
            grads[n], deltas[n], new_m[n], new_v[n] = [a[None] for a in res]
    loss, g_s, d_s, m_s, v_s = _small_update(small_finish(after), sm, {n: ms[n] for n in SMALL},
                                             {n: vs[n] for n in SMALL})
    grads.update(g_s), deltas.update(d_s), new_m.update(m_s), new_v.update(v_s)
    return (loss[0, 0], grad_x[None], *[grads[n] for n in WEIGHTS], *[deltas[n] for n in WEIGHTS],
            *[new_m[n] for n in WEIGHTS], *[new_v[n] for n in WEIGHTS])
```

```python
import functools

import jax
import jax.numpy as jnp
from jax import lax
from jax.experimental import pallas as pl
from jax.experimental.pallas import tpu as pltpu

F32 = jnp.float32
BF16 = jnp.bfloat16

D = 1024
D_IN = 2816
D_FF = 2816
CHUNK = 64
SWA_W = 512
KV_W = 128
HG_W = 512
HD = 128
ZQH, ZFH, ZIH, ZGH = 768, 1280, 1792, 2304
XH, XD = 4, 256
EPS = 1e-6
NEG = -1e30
N_DEV = 8
MESH = pl.DeviceIdType.MESH

LR, B1, B2, AEPS, WD, STEP = 0.001, 0.9, 0.999, 1e-08, 0.01, 10
C1 = 1.0 - B1 ** STEP
C2 = 1.0 - B2 ** STEP

VMEM_LIMIT = 56 * 1024 * 1024


def _params(**kw):
    return pltpu.CompilerParams(vmem_limit_bytes=VMEM_LIMIT, **kw)


def _sig(x):
    return 1.0 / (1.0 + jnp.exp(-x))


def _rowsum8(x):
    r, w = x.shape
    return jnp.sum(x.reshape(r // 8, 8, w), axis=0)


def _dot(a, b, ca, cb, precision=None):
    return lax.dot_general(a, b, (((ca,), (cb,)), ((), ())), preferred_element_type=F32,
                           precision=precision)


ANY_SPEC = pl.BlockSpec(memory_space=pl.ANY)


def _mm_nt(a, b, *, out_dtype, tm, tn, name):
    (m, k), n = a.shape, b.shape[0]
    tm, tn = min(tm, m), min(tn, n)
    assert a.dtype == BF16 and b.dtype == BF16 and m % tm == 0 and n % tn == 0, (name, m, n, tm, tn)

    def body(a_ref, b_ref, o_ref):
        o_ref[...] = _dot(a_ref[...], b_ref[...], 1, 1).astype(out_dtype)

    return pl.pallas_call(
        body, name=name, out_shape=jax.ShapeDtypeStruct((m, n), out_dtype), grid=(n // tn, m // tm),
        in_specs=[pl.BlockSpec((tm, k), lambda j, i: (i, 0)), pl.BlockSpec((tn, k), lambda j, i: (j, 0))],
        out_specs=pl.BlockSpec((tm, tn), lambda j, i: (i, j)),
        compiler_params=_params(dimension_semantics=("parallel", "parallel")),
    )(a, b)


TN_FIRST = 256
TN_REST = 1152
TN_SLICES = 4


def _mm_tn(a, b, *, name, dep=None):
    (k, m), n = a.shape, b.shape[1]
    assert a.dtype == BF16 and b.dtype == BF16 and b.shape[0] == k
    widths = [TN_FIRST, TN_FIRST]
    while sum(widths) < m:
        widths.append(min(TN_REST, m - sum(widths)))
    starts = [sum(widths[:i]) for i in range(len(widths))]
    assert sum(widths) == m
    nb = len(widths)
    ks = k // TN_SLICES
    ahead = 2
    deps = [] if dep is None else [dep]

    def body(a_hbm, b_hbm, *rest):
        o_hbm, b_v, sems = rest[len(deps)], rest[len(deps) + 1], rest[-1]
        a_v, o_v = rest[len(deps) + 2:len(deps) + 2 + nb], rest[len(deps) + 2 + nb:-1]
        sliced = []
        for c in range(TN_SLICES):
            rows = pl.ds(c * ks, ks)
            sliced.append((pltpu.make_async_copy(b_hbm.at[rows, :], b_v.at[rows, :], sems.at[2 * c]),
                           pltpu.make_async_copy(a_hbm.at[rows, pl.ds(0, widths[0])], a_v[0].at[rows, :],
                                                 sems.at[2 * c + 1])))
        base = 2 * TN_SLICES - 1
        loads = [None] + [pltpu.make_async_copy(a_hbm.at[:, pl.ds(c0, cw)], a_v[i], sems.at[base + i])
                          for i, (c0, cw) in enumerate(zip(starts, widths)) if i > 0]
        stores = [pltpu.make_async_copy(o_v[i], o_hbm.at[pl.ds(c0, cw), :], sems.at[base + nb + i])
                  for i, (c0, cw) in enumerate(zip(starts, widths))]
        for pair in sliced:
            for cp in pair:
                cp.start()
        for i in range(1, 1 + ahead):
            loads[i].start()
        acc = None
        for c, pair in enumerate(sliced):
            for cp in pair:
                cp.wait()
            p = _dot(a_v[0][c * ks:(c + 1) * ks, :], b_v[c * ks:(c + 1) * ks, :], 0, 0)
            acc = p if acc is None else acc + p
        o_v[0][...] = acc.astype(BF16)
        stores[0].start()
        for i in range(1, nb):
            loads[i].wait()
            if i + ahead < nb:
                loads[i + ahead].start()
            o_v[i][...] = _dot(a_v[i][...], b_v[...], 0, 0).astype(BF16)
            stores[i].start()
        for cp in stores:
            cp.wait()

    return pl.pallas_call(
        body, name=name, out_shape=jax.ShapeDtypeStruct((m, n), BF16),
        in_specs=[ANY_SPEC] * (2 + len(deps)), out_specs=ANY_SPEC,
        scratch_shapes=[pltpu.VMEM((k, n), BF16)] + [pltpu.VMEM((k, cw), BF16) for cw in widths]
        + [pltpu.VMEM((cw, n), BF16) for cw in widths] + [pltpu.SemaphoreType.DMA((2 * TN_SLICES - 1 + 2 * nb,))],
        compiler_params=_params(),
    )(a, b, *deps)


def _mm_rows(prods, rows_in, vecs_in, epilogue, outs, *, tm, name, dep=None):
    m = prods[0][0].shape[0]
    n = prods[0][1].shape[0] if prods[0][2] else prods[0][1].shape[1]
    tm = min(tm, m)
    assert m % tm == 0
    deps = [] if dep is None else [dep]
    n_p, n_r, n_v = len(prods), len(rows_in), len(vecs_in)

    def body(*refs):
        ab = refs[:2 * n_p]
        row_refs = refs[2 * n_p:2 * n_p + n_r]
        vec_refs = refs[2 * n_p + n_r:2 * n_p + n_r + n_v]
        out_refs = refs[2 * n_p + n_r + n_v + len(deps):]
        p = None
        for j, (_, _, tb) in enumerate(prods):
            t = _dot(ab[2 * j][...].astype(BF16), ab[2 * j + 1][...], 1, 1 if tb else 0)
            p = t if p is None else p + t
        vals = epilogue(p, *[r[...] for r in row_refs], *[v[...] for v in vec_refs])
        for (dtype, kind), o_ref, val in zip(outs, out_refs, vals):
            if kind == "row":
                o_ref[...] = val.astype(dtype)
            else:
                @pl.when(pl.program_id(0) == 0)
                def _(o_ref=o_ref):
                    o_ref[...] = jnp.zeros_like(o_ref)

                o_ref[...] += val

    row = lambda w: pl.BlockSpec((tm, w), lambda i: (i, 0))
    whole = lambda a: pl.BlockSpec(a.shape, lambda i: (0,) * a.ndim, pipeline_mode=pl.Buffered(1))
    in_specs, args = [], []
    for a, b, _ in prods:
        in_specs += [row(a.shape[1]), whole(b)]
        args += [a, b]
    in_specs += [row(r.shape[1]) for r in rows_in] + [whole(v) for v in vecs_in] + [ANY_SPEC] * len(deps)
    return pl.pallas_call(
        body, name=name,
        out_shape=tuple(jax.ShapeDtypeStruct((m, n) if kind == "row" else (8, n), dtype) for dtype, kind in outs),
        grid=(m // tm,), in_specs=in_specs,
        out_specs=tuple(row(n) if kind == "row" else pl.BlockSpec((8, n), lambda i: (0, 0)) for _, kind in outs),
        compiler_params=_params(dimension_semantics=("arbitrary",)),
    )(*args, *rows_in, *vecs_in, *deps)


def _rstd(x):
    return lax.rsqrt(jnp.mean(x * x, axis=-1, keepdims=True) + EPS)


def _norm_bwd(xh, r, t):
    return r * (t - xh * jnp.mean(xh * t, axis=-1, keepdims=True))


ROW_F32, ROW_BF16, SUM_F32 = (F32, "row"), (BF16, "row"), (F32, "sum")


def _then(epilogue, index, tb):
    def run(p, *args):
        vals = epilogue(p, *args[:-1])
        return (*vals, _dot(vals[index].astype(BF16), args[-1], 1, 1 if tb else 0))

    return run


def _ep_post_pre(p, h, g_post, g_pre):
    y = p.astype(BF16)
    yf = y.astype(F32)
    hn = h + yf * _rstd(yf) * g_post
    return y, hn, hn * _rstd(hn) * g_pre


_EP_POST_PRE_OUTS = [ROW_BF16, ROW_F32, ROW_BF16]


def _ep_final_loss(y, h, target, g_post):
    r = _rstd(y)
    yh = y * r
    err = h + yh * g_post - target
    dh = err * (1.0 / D)
    return _rowsum8(err * err), dh, _norm_bwd(yh, r, dh * g_post), _rowsum8(dh * yh)


def _ep_post_pre_bwd(du, dh_out, hn, y, g_post, g_pre):
    r2 = _rstd(hn)
    xh = hn * r2
    dh = dh_out + _norm_bwd(xh, r2, du * g_pre)
    yf = y.astype(F32)
    r1 = _rstd(yf)
    yh = yf * r1
    return dh, _norm_bwd(yh, r1, dh * g_post), _rowsum8(du * xh), _rowsum8(dh * yh)


_EP_POST_PRE_BWD_OUTS = [ROW_F32, ROW_BF16, SUM_F32, SUM_F32]


def _ep_pre_bwd(du, dh_out, x, g):
    r = _rstd(x)
    xh = x * r
    return dh_out + _norm_bwd(xh, r, du * g), _rowsum8(du * xh)


_EP_PRE_BWD_OUTS = [ROW_F32, SUM_F32]


def _prenorm(x, g, *, name, dep=None):
    t, d = x.shape
    tb = min(512, t)
    deps = [] if dep is None else [dep]

    def body(x_ref, g_ref, *rest):
        xf = x_ref[...]
        rest[-1][...] = (xf * _rstd(xf) * g_ref[...]).astype(BF16)

    return pl.pallas_call(
        body, name=name, out_shape=jax.ShapeDtypeStruct((t, d), BF16), grid=(t // tb,),
        in_specs=[pl.BlockSpec((tb, d), lambda i: (i, 0)), pl.BlockSpec((1, d), lambda i: (0, 0))]
        + [ANY_SPEC] * len(deps),
        out_specs=pl.BlockSpec((tb, d), lambda i: (i, 0)), compiler_params=_params(),
    )(x, g, *deps)


QB = 256


def _half_mask(shape, e):
    lane = lax.broadcasted_iota(jnp.int32, shape, len(shape) - 1)
    return (lane // 64) == e


def _place(kv):
    sw = pltpu.roll(kv, 64, 1)
    m0 = _half_mask(kv.shape, 0)
    return [[jnp.where(m0, kv, 0.0).astype(BF16), jnp.where(m0, 0.0, sw).astype(BF16)],
            [jnp.where(m0, sw, 0.0).astype(BF16), jnp.where(m0, 0.0, kv).astype(BF16)]]


SQ = 128
SK = 256


def _swa_valid(i, sb):
    qc = lax.broadcasted_iota(jnp.int32, (SQ, SK), 0) // CHUNK
    kc = lax.broadcasted_iota(jnp.int32, (SQ, SK), 1) // CHUNK - 2
    return (kc <= qc) & (qc <= kc + 2) & (4 * i + 2 * sb + kc >= 0)


def _swa_fwd(z, sinks, t, dep=None):
    nb = t // QB
    deps = [] if dep is None else [dep]

    def body(s_ref, q_ref, kp_ref, kc_ref, vp_ref, vc_ref, *rest):
        o_ref, lse_ref = rest[-2:]
        i = pl.program_id(0)
        kpl = _place(jnp.concatenate([kp_ref[...], kc_ref[...]], axis=0))
        vpl = _place(jnp.concatenate([vp_ref[...], vc_ref[...]], axis=0))
        lane = lax.broadcasted_iota(jnp.int32, (SQ, 128), 1)
        for sb in range(QB // SQ):
            rows, keys = slice(SQ * sb, SQ * (sb + 1)), slice(SQ * sb, SQ * sb + SK)
            valid = _swa_valid(i, sb)
            lse_out = jnp.zeros((SQ, 128), F32)
            for j in range(4):
                qp = q_ref[rows, 128 * j:128 * (j + 1)].astype(BF16)
                acc = jnp.zeros((SQ, 128), F32)
                for e in range(2):
                    h = 2 * j + e
                    kvh = h // 4
                    qm = jnp.where(_half_mask(qp.shape, e), qp, jnp.zeros_like(qp))
                    s = _dot(qm, kpl[kvh][e][keys], 1, 1) * 0.125
                    s = jnp.where(valid, s, NEG)
                    sink = s_ref[0, h]
                    m = jnp.maximum(jnp.max(s, axis=-1, keepdims=True), sink)
                    p = jnp.exp(s - m)
                    l = jnp.sum(p, axis=-1, keepdims=True) + jnp.exp(sink - m)
                    acc = acc + _dot(p.astype(BF16), vpl[kvh][e][keys], 1, 0) * (1.0 / l)
                    lse_out = jnp.where(lane == h, m + jnp.log(l), lse_out)
                o_ref[rows, 128 * j:128 * (j + 1)] = acc.astype(BF16)
            lse_ref[rows, :] = lse_out

    prev = lambda c: pl.BlockSpec((128, 128), lambda i: (jnp.maximum(2 * i - 1, 0), c))
    cur = lambda c: pl.BlockSpec((QB, 128), lambda i: (i, c))
    return pl.pallas_call(
        body, name="swa_fwd",
        out_shape=(jax.ShapeDtypeStruct((t, D), BF16), jax.ShapeDtypeStruct((t, 128), F32)),
        grid=(nb,),
        in_specs=[pl.BlockSpec(memory_space=pltpu.SMEM),
                  pl.BlockSpec((QB, SWA_W), lambda i: (i, 0)), prev(4), cur(4), prev(5), cur(5)]
        + [ANY_SPEC] * len(deps),
        out_specs=(pl.BlockSpec((QB, SWA_W), lambda i: (i, 0)), pl.BlockSpec((QB, 128), lambda i: (i, 0))),
        compiler_params=_params(),
    )(sinks, z, z, z, z, z, *deps)


def _swa_bwd(z, sinks, ymix, lse, dymix, t, dep=None):
    nb = t // QB
    deps = [] if dep is None else [dep]

    def body(s_ref, q_ref, kp_ref, kc_ref, vp_ref, vc_ref, o_ref, do_ref, l_ref, *rest):
        dq_ref, first_ref, second_ref, ds_ref, carry_ref = rest[len(deps):]
        i = pl.program_id(0)
        live = i < nb

        @pl.when(i == 0)
        def _():
            ds_ref[...] = jnp.zeros_like(ds_ref)
            carry_ref[...] = jnp.zeros_like(carry_ref)

        lane = lax.broadcasted_iota(jnp.int32, (8, 128), 1)
        kpl = _place(jnp.concatenate([kp_ref[...], kc_ref[...]], axis=0))
        vpl = _place(jnp.concatenate([vp_ref[...], vc_ref[...]], axis=0))
        nk = QB + 128
        qc = lax.broadcasted_iota(jnp.int32, (QB, nk), 0) // CHUNK
        kc = lax.broadcasted_iota(jnp.int32, (QB, nk), 1) // CHUNK - 2
        valid = (kc <= qc) & (qc <= kc + 2) & (4 * i + kc >= 0) & live
        lse_c = l_ref[...]
        dsink = jnp.zeros((8, 128), F32)
        dk_acc = [[jnp.zeros((128, nk), F32) for _ in range(2)] for _ in range(2)]
        dv_acc = [[jnp.zeros((128, nk), F32) for _ in range(2)] for _ in range(2)]
        dq = []
        for j in range(4):
            cols = slice(128 * j, 128 * (j + 1))
            qp = q_ref[:, cols].astype(BF16)
            dop = do_ref[:, cols]
            prod = dop.astype(F32) * o_ref[:, cols].astype(F32)
            acc = jnp.zeros((QB, 128), F32)
            for e in range(2):
                h = 2 * j + e
                kvh = h // 4
                hm = _half_mask(qp.shape, e)
                qm = jnp.where(hm, qp, jnp.zeros_like(qp))
                dom = jnp.where(hm, dop, jnp.zeros_like(dop))
                dd = jnp.sum(jnp.where(hm, prod, 0.0), axis=-1, keepdims=True)
                lse_h = lse_c[:, h:h + 1]
                s = _dot(qm, kpl[kvh][e], 1, 1) * 0.125
                p = jnp.where(valid, jnp.exp(s - lse_h), 0.0)
                dp = _dot(dom, vpl[kvh][e], 1, 1)
                ds = (p * (dp - dd) * 0.125).astype(BF16)
                acc = acc + _dot(ds, kpl[kvh][e], 1, 0)
                dk_acc[kvh][e] = dk_acc[kvh][e] + _dot(qm, ds, 0, 0)
                dv_acc[kvh][e] = dv_acc[kvh][e] + _dot(dom, p.astype(BF16), 0, 0)
                ps = jnp.where(live, jnp.exp(s_ref[0, h] - lse_h) * dd, 0.0)
                dsink = dsink - jnp.where(lane == h, _rowsum8(jnp.broadcast_to(ps, (QB, 128))), 0.0)
            dq.append(acc.astype(BF16))
        ds_ref[...] += dsink
        dk = (dk_acc[0][0] + dk_acc[1][1] + pltpu.roll(dk_acc[0][1] + dk_acc[1][0], 64, 0)).T
        dv = (dv_acc[0][0] + dv_acc[1][1] + pltpu.roll(dv_acc[0][1] + dv_acc[1][0], 64, 0)).T
        dkv = jnp.concatenate([dk, dv], axis=1)
        second_ref[...] = (carry_ref[...] + dkv[0:128]).astype(BF16)
        carry_ref[...] = dkv[256:384]

        @pl.when(live)
        def _():
            for j in range(4):
                dq_ref[:, 128 * j:128 * (j + 1)] = dq[j]
            first_ref[...] = dkv[128:256].astype(BF16)

    blk = lambda i: jnp.minimum(i, nb - 1)
    prev = lambda c: pl.BlockSpec((128, 128), lambda i: (jnp.maximum(2 * blk(i) - 1, 0), c))
    cur = lambda w, c: pl.BlockSpec((QB, w), lambda i: (blk(i), c))
    half = lambda index: pl.BlockSpec((128, 256), lambda i: (index(i), 0))
    return pl.pallas_call(
        body, name="swa_bwd",
        out_shape=(jax.ShapeDtypeStruct((t, SWA_W), BF16), jax.ShapeDtypeStruct((t // 2, 256), BF16),
                   jax.ShapeDtypeStruct((t // 2, 256), BF16), jax.ShapeDtypeStruct((8, 128), F32)),
        grid=(nb + 1,),
        in_specs=[pl.BlockSpec(memory_space=pltpu.SMEM),
                  cur(SWA_W, 0), prev(4), cur(128, 4), prev(5), cur(128, 5),
                  cur(SWA_W, 0), cur(SWA_W, 0), cur(128, 0)] + [ANY_SPEC] * len(deps),
        out_specs=(cur(SWA_W, 0), half(blk), half(lambda i: jnp.maximum(i - 1, 0)),
                   pl.BlockSpec((8, 128), lambda i: (0, 0))),
        scratch_shapes=[pltpu.VMEM((128, 256), F32)],
        compiler_params=_params(dimension_semantics=("arbitrary",)),
    )(sinks, z, z, z, z, z, ymix, dymix, lse, *deps)


HB = 256


def _lower_bound(lb_ref):
    a = lb_ref[...]
    a0, a1 = a[0:1], a[1:2]
    mx = jnp.maximum(a0, a1)
    e0, e1 = jnp.exp(a0 - mx), jnp.exp(a1 - mx)
    return e0 / (e0 + e1)


def _hgrn_cols(row_block):
    return [pl.BlockSpec((HB, 2 * HD), lambda j, c=base // (2 * HD) + p: (row_block(j), c))
            for base in (ZQH, ZFH, ZIH, ZGH) for p in range(2)]


NCH = HB // CHUNK


def _split3(x):
    hi = x.astype(BF16)
    r1 = x - hi.astype(F32)
    mid = r1.astype(BF16)
    return hi, mid, (r1 - mid.astype(F32)).astype(BF16)


def _blockdiag(lower):
    r = lax.broadcasted_iota(jnp.int32, (HB, HB), 0)
    c = lax.broadcasted_iota(jnp.int32, (HB, HB), 1)
    return (r // CHUNK == c // CHUNK) & ((c <= r) if lower else (c >= r))


def _chunk_sums(mask_bf16, x):
    return sum(_dot(mask_bf16, part, 1, 0) for part in _split3(x))


def _per_chunk_rows(x, row):
    w = x.shape[1]
    picked = x.reshape(NCH, CHUNK, w)[:, row:row + 1, :]
    return jnp.broadcast_to(picked, (NCH, CHUNK, w)).reshape(HB, w)


def _chunk_stack(x, chunk_of_row):
    return jnp.concatenate([jnp.where(chunk_of_row == c, x, jnp.zeros_like(x)) for c in range(NCH)], axis=1)


def _chunk_pick(x, chunk_of_row):
    w = x.shape[1] // NCH
    out = jnp.zeros((HB, w), x.dtype)
    for c in range(NCH):
        out = jnp.where(chunk_of_row == c, x[:, c * w:(c + 1) * w], out)
    return out


def _hgrn_local(q, f, kf, b):
    sq = _sig(q)
    qf = q * sq * (HD ** -0.5)
    b_mid = _per_chunk_rows(b, CHUNK // 2 - 1)
    b_last = _per_chunk_rows(b, CHUNK - 1)
    qm = qf * jnp.exp(b - b_mid)
    km = kf * jnp.exp(b_mid - b)
    kl = kf * jnp.exp(b_last - b)
    qb = qf * jnp.exp(b)
    return dict(sq=sq, b_mid=b_mid, b_last=b_last, qm=qm, km=km, kl=kl, qb=qb)


def _hgrn2_fwd(z, hgrn_lb, onorm, ymix, t, dep=None):
    nb = t // HB
    deps = [] if dep is None else [dep]

    def body(*refs):
        zq, zf, zi, zg = refs[0:2], refs[2:4], refs[4:6], refs[6:8]
        (lb_ref, on_ref), (y_ref, o_ref, sp_ref, st_ref) = refs[8:10], refs[-4:]

        @pl.when(pl.program_id(0) == 0)
        def _():
            st_ref[...] = jnp.zeros_like(st_ref)

        lb_all = _lower_bound(lb_ref)
        gn = on_ref[...]
        low = _blockdiag(True)
        low_b = low.astype(BF16)
        chunk_of_row = lax.broadcasted_iota(jnp.int32, (HB, HD), 0) // CHUNK
        for p in range(2):
            lbp = lb_all[:, 2 * HD * p:2 * HD * (p + 1)]
            fp = lbp + (1.0 - lbp) * _sig(zf[p][...])
            bp = _chunk_sums(low_b, jnp.log(fp))
            for e in range(2):
                h, ls = 2 * p + e, slice(e * HD, (e + 1) * HD)
                f = fp[:, ls]
                w = _hgrn_local(zq[p][:, ls], f, 1.0 - f, bp[:, ls])
                iv = zi[p][:, ls].astype(BF16)
                a = jnp.where(low, _dot(w["qm"].astype(BF16), w["km"].astype(BF16), 1, 1), 0.0)
                o = _dot(a.astype(BF16), iv, 1, 0)
                u = _dot(iv, _chunk_stack(w["kl"].astype(BF16), chunk_of_row), 0, 0)
                decay = jnp.exp(w["b_last"])
                st = st_ref[h]
                states = []
                for c in range(NCH):
                    sp_ref[h, c] = st
                    states.append(st.astype(BF16))
                    st = st * decay[c * CHUNK:c * CHUNK + 1] + u[:, c * HD:(c + 1) * HD]
                st_ref[h] = st
                inter = _dot(w["qb"].astype(BF16), jnp.concatenate(states, axis=0), 1, 1)
                o = o + _chunk_pick(inter, chunk_of_row)
                hs = slice(h * HD, (h + 1) * HD)
                o_ref[:, hs] = o
                gg = zg[p][:, ls]
                y_ref[:, hs] = (o * _rstd(o) * gn * (gg * _sig(gg))).astype(BF16)

    return pl.pallas_call(
        body, name="hgrn_fwd",
        out_shape=(jax.ShapeDtypeStruct((t, D), BF16), jax.ShapeDtypeStruct((t, HG_W), F32),
                   jax.ShapeDtypeStruct((4, t // CHUNK, HD, HD), F32)),
        grid=(nb,),
        in_specs=_hgrn_cols(lambda j: j) + [pl.BlockSpec((2, HG_W), lambda j: (0, 0)),
                                            pl.BlockSpec((1, HD), lambda j: (0, 0)), ANY_SPEC]
        + [ANY_SPEC] * len(deps),
        out_specs=(pl.BlockSpec((HB, HG_W), lambda j: (j, 1)),
                   pl.BlockSpec((HB, HG_W), lambda j: (j, 0)),
                   pl.BlockSpec((4, NCH, HD, HD), lambda j: (0, j, 0, 0))),
        scratch_shapes=[pltpu.VMEM((4, HD, HD), F32)],
        input_output_aliases={10: 0},
        compiler_params=_params(dimension_semantics=("arbitrary",)),
    )(*[z] * 8, hgrn_lb, onorm, ymix, *deps)


def _hgrn2_bwd(z, hgrn_lb, onorm, o_save, sprev, dymix, dza, t):
    nb = t // HB

    def body(*refs):
        zq, zf, zi, zg = refs[0:2], refs[2:4], refs[4:6], refs[6:8]
        (lb_ref, on_ref, o_ref, sp_ref, dy_ref, dqa_ref, first_ref, second_ref,
         dz_ref, dlb_ref, don_ref, dst_ref) = refs[8:]

        @pl.when(pl.program_id(0) == 0)
        def _():
            dst_ref[...] = jnp.zeros_like(dst_ref)
            dlb_ref[...] = jnp.zeros_like(dlb_ref)
            don_ref[...] = jnp.zeros_like(don_ref)

        dz_ref[:, 0:SWA_W] = dqa_ref[...]
        dz_ref[0:HB // 2, SWA_W:ZQH] = first_ref[...]
        dz_ref[HB // 2:HB, SWA_W:ZQH] = second_ref[...]
        lb_all = _lower_bound(lb_ref)
        gn = on_ref[...]
        low, upp = _blockdiag(True), _blockdiag(False)
        upp_b = upp.astype(BF16)
        low_b = low.astype(BF16)
        row = lax.broadcasted_iota(jnp.int32, (HB, HD), 0)
        chunk_of_row = row // CHUNK
        in_chunk = row % CHUNK
        for p in range(2):
            lbp = lb_all[:, 2 * HD * p:2 * HD * (p + 1)]
            sgp = _sig(zf[p][...])
            fp = lbp + (1.0 - lbp) * sgp
            bp = _chunk_sums(low_b, jnp.log(fp))
            db_pair, dkf_pair = [], []
            for e in range(2):
                h, ls, hs = 2 * p + e, slice(e * HD, (e + 1) * HD), slice((2 * p + e) * HD, (2 * p + e + 1) * HD)
                f = fp[:, ls]
                q = zq[p][:, ls]
                w = _hgrn_local(q, f, 1.0 - f, bp[:, ls])
                iv = zi[p][:, ls].astype(BF16)
                gg = zg[p][:, ls]
                o = o_ref[:, hs]
                dout = dy_ref[:, hs].astype(F32)
                sgg = _sig(gg)
                r = _rstd(o)
                oh = o * r
                dyn = dout * (gg * sgg)
                dz_ref[:, ZGH + h * HD:ZGH + (h + 1) * HD] = (
                    dout * oh * gn * (sgg * (1.0 + gg * (1.0 - sgg)))).astype(BF16)
                don_ref[...] += _rowsum8(dyn * oh)
                do = _norm_bwd(oh, r, dyn * gn).astype(BF16)
                qm, km, kl, qb = (w[n].astype(BF16) for n in ("qm", "km", "kl", "qb"))
                decay = jnp.exp(w["b_last"])
                grads_in = _dot(do, _chunk_stack(qb, chunk_of_row), 0, 0)
                dst = dst_ref[h]
                dstn, dd_rows = [None] * NCH, [None] * NCH
                for c in reversed(range(NCH)):
                    dstn[c] = dst.astype(BF16)
                    dd_rows[c] = jnp.sum(dst * sp_ref[h, c], axis=0, keepdims=True)
                    dst = dst * decay[c * CHUNK:c * CHUNK + 1] + grads_in[:, c * HD:(c + 1) * HD]
                dst_ref[h] = dst
                states = jnp.concatenate([sp_ref[h, c].astype(BF16) for c in range(NCH)], axis=0)
                dstn_all = jnp.concatenate(dstn, axis=0)
                dqb = _dot(_chunk_stack(do, chunk_of_row), states, 1, 0)
                at = jnp.where(upp, _dot(km, qm, 1, 1), 0.0)
                di = _dot(at.astype(BF16), do, 1, 0) + _chunk_pick(_dot(kl, dstn_all, 1, 1), chunk_of_row)
                dz_ref[:, ZIH + h * HD:ZIH + (h + 1) * HD] = di.astype(BF16)
                dkl = _dot(_chunk_stack(iv, chunk_of_row), dstn_all, 1, 0)
                da = jnp.where(low, _dot(do, iv, 1, 1), 0.0).astype(BF16)
                dat = jnp.where(upp, _dot(iv, do, 1, 1), 0.0).astype(BF16)
                dqm = _dot(da, km, 1, 0)
                dkm = _dot(dat, qm, 1, 0)
                b = bp[:, ls]
                e1, e2 = jnp.exp(b - w["b_mid"]), jnp.exp(w["b_mid"] - b)
                e3, e4 = jnp.exp(w["b_last"] - b), jnp.exp(b)
                dqf = dqm * e1 + dqb * e4
                dkf_pair.append(dkm * e2 + dkl * e3)
                t_qm, t_km, t_kl = dqm * w["qm"], dkm * w["km"], dkl * w["kl"]
                db = t_qm - t_km - t_kl + dqb * w["qb"]
                db_mid = jnp.sum((t_km - t_qm).reshape(NCH, CHUNK, HD), axis=1, keepdims=True)
                db_last = jnp.sum(t_kl.reshape(NCH, CHUNK, HD), axis=1, keepdims=True)
                db_last = db_last + jnp.stack(dd_rows, axis=0) * jnp.exp(
                    bp[:, ls].reshape(NCH, CHUNK, HD)[:, CHUNK - 1:CHUNK, :])
                spread = lambda v: jnp.broadcast_to(v, (NCH, CHUNK, HD)).reshape(HB, HD)
                db = (db + jnp.where(in_chunk == CHUNK // 2 - 1, spread(db_mid), 0.0)
                      + jnp.where(in_chunk == CHUNK - 1, spread(db_last), 0.0))
                db_pair.append(db)
                sq = w["sq"]
                dz_ref[:, ZQH + h * HD:ZQH + (h + 1) * HD] = (
                    dqf * (HD ** -0.5) * (sq * (1.0 + q * (1.0 - sq)))).astype(BF16)
            dlogf = _chunk_sums(upp_b, jnp.concatenate(db_pair, axis=1))
            dfv = dlogf / fp - jnp.concatenate(dkf_pair, axis=1)
            dz_ref[:, ZFH + 2 * HD * p:ZFH + 2 * HD * (p + 1)] = (dfv * (1.0 - lbp) * sgp * (1.0 - sgp)).astype(BF16)
            dlb_ref[:, 2 * HD * p:2 * HD * (p + 1)] += _rowsum8(dfv * (1.0 - sgp))

    rev = lambda j: nb - 1 - j
    return pl.pallas_call(
        body, name="hgrn_bwd",
        out_shape=(jax.ShapeDtypeStruct((t, D_IN), BF16), jax.ShapeDtypeStruct((8, HG_W), F32),
                   jax.ShapeDtypeStruct((8, HD), F32)),
        grid=(nb,),
        in_specs=_hgrn_cols(rev) + [pl.BlockSpec((2, HG_W), lambda j: (0, 0)), pl.BlockSpec((1, HD), lambda j: (0, 0)),
                                    pl.BlockSpec((HB, HG_W), lambda j: (rev(j), 0)),
                                    pl.BlockSpec((4, NCH, HD, HD), lambda j: (0, rev(j), 0, 0)),
                                    pl.BlockSpec((HB, HG_W), lambda j: (rev(j), 1)),
                                    pl.BlockSpec((HB, SWA_W), lambda j: (rev(j), 0)),
                                    pl.BlockSpec((HB // 2, 2 * KV_W), lambda j: (rev(j), 0)),
                                    pl.BlockSpec((HB // 2, 2 * KV_W), lambda j: (rev(j), 0))],
        out_specs=(pl.BlockSpec((HB, D_IN), lambda j: (rev(j), 0)), pl.BlockSpec((8, HG_W), lambda j: (0, 0)),
                   pl.BlockSpec((8, HD), lambda j: (0, 0))),
        scratch_shapes=[pltpu.VMEM((4, HD, HD), F32)],
        compiler_params=_params(dimension_semantics=("arbitrary",)),
    )(*[z] * 8, hgrn_lb, onorm, o_save, sprev, dymix, *dza)


XB = 512


def _xattn_fwd(q, k, v, wo, h, g_post, g_pre, t, dep=None):
    tb = min(XB, t)
    deps = [] if dep is None else [dep]

    def body(q_ref, k_ref, v_ref, wo_ref, h_ref, gp_ref, gn_ref, *rest):
        o_ref, y_ref, hn_ref, u_ref = rest[len(deps):]
        for hd in range(XH):
            cols = slice(XD * hd, XD * (hd + 1))
            s = _dot(q_ref[:, cols], k_ref[:, cols], 1, 1) * (XD ** -0.5)
            p = jnp.exp(s - jnp.max(s, axis=-1, keepdims=True))
            l = jnp.sum(p, axis=-1, keepdims=True)
            o_ref[:, cols] = (_dot(p.astype(BF16), v_ref[:, cols], 1, 0) * (1.0 / l)).astype(BF16)
        y, hn, u = _ep_post_pre(_dot(o_ref[...], wo_ref[...], 1, 0), h_ref[...], gp_ref[...], gn_ref[...])
        y_ref[...] = y
        hn_ref[...] = hn
        u_ref[...] = u.astype(BF16)

    row = pl.BlockSpec((tb, D), lambda i: (i, 0))
    whole = lambda a: pl.BlockSpec(a.shape, lambda i: (0,) * a.ndim, pipeline_mode=pl.Buffered(1))
    half = jax.ShapeDtypeStruct((t, D), BF16)
    return pl.pallas_call(
        body, name="xattn_fwd", out_shape=(half, half, jax.ShapeDtypeStruct((t, D), F32), half), grid=(t // tb,),
        in_specs=[row, whole(k), whole(v), whole(wo), row, whole(g_post), whole(g_pre)] + [ANY_SPEC] * len(deps),
        out_specs=(row, row, row, row), compiler_params=_params(),
    )(q, k, v, wo, h, g_post, g_pre, *deps)


def _xattn_bwd(q, k, v, do, wq, wout, dh_out, hn, y, g_post, g_pre, t):
    tb = min(XB, t)

    def body(q_ref, k_ref, v_ref, do_ref, wq_ref, wout_ref, dho_ref, hn_ref, y_ref, gp_ref, gn_ref,
             dq_ref, dk_ref, dv_ref, dh_ref, dyp_ref, dym_ref, dgn_ref, dgp_ref):
        @pl.when(pl.program_id(0) == 0)
        def _():
            dk_ref[...] = jnp.zeros_like(dk_ref)
            dv_ref[...] = jnp.zeros_like(dv_ref)
            dgn_ref[...] = jnp.zeros_like(dgn_ref)
            dgp_ref[...] = jnp.zeros_like(dgp_ref)

        for h in range(XH):
            cols = slice(XD * h, XD * (h + 1))
            qh, kh, vh, doh = q_ref[:, cols], k_ref[:, cols], v_ref[:, cols], do_ref[:, cols]
            s = _dot(qh, kh, 1, 1) * (XD ** -0.5)
            p = jnp.exp(s - jnp.max(s, axis=-1, keepdims=True))
            p = p * (1.0 / jnp.sum(p, axis=-1, keepdims=True))
            dp = _dot(doh, vh, 1, 1)
            ds = (p * (dp - jnp.sum(p * dp, axis=-1, keepdims=True)) * (XD ** -0.5)).astype(BF16)
            dq_ref[:, cols] = _dot(ds, kh, 1, 0).astype(BF16)
            dk_ref[:, cols] += _dot(ds, qh, 0, 0)
            dv_ref[:, cols] += _dot(p.astype(BF16), doh, 0, 0)
        du = _dot(dq_ref[...], wq_ref[...], 1, 1)
        dh, dyp, dgn, dgp = _ep_post_pre_bwd(du, dho_ref[...], hn_ref[...], y_ref[...], gp_ref[...], gn_ref[...])
        dh_ref[...] = dh
        dyp = dyp.astype(BF16)
        dyp_ref[...] = dyp
        dym_ref[...] = _dot(dyp, wout_ref[...], 1, 1).astype(BF16)
        dgn_ref[...] += dgn
        dgp_ref[...] += dgp

    row = pl.BlockSpec((tb, D), lambda i: (i, 0))
    mem = pl.BlockSpec(k.shape, lambda i: (0, 0))
    whole = lambda a: pl.BlockSpec(a.shape, lambda i: (0,) * a.ndim, pipeline_mode=pl.Buffered(1))
    acc = pl.BlockSpec((8, D), lambda i: (0, 0))
    half = jax.ShapeDtypeStruct((t, D), BF16)
    return pl.pallas_call(
        body, name="xattn_bwd",
        out_shape=(half, jax.ShapeDtypeStruct(k.shape, F32), jax.ShapeDtypeStruct(k.shape, F32),
                   jax.ShapeDtypeStruct((t, D), F32), half, half,
                   jax.ShapeDtypeStruct((8, D), F32), jax.ShapeDtypeStruct((8, D), F32)),
        grid=(t // tb,),
        in_specs=[row, whole(k), whole(v), row, whole(wq), whole(wout), row, row, row, whole(g_post), whole(g_pre)],
        out_specs=(row, mem, mem, row, row, row, acc, acc),
        compiler_params=_params(dimension_semantics=("arbitrary",)),
    )(q, k, v, do, wq, wout, dh_out, hn, y, g_post, g_pre)


def _mem_kv(mem, g_mem, wk, wv):
    def body(m_ref, g_ref, wk_ref, wv_ref, mn_ref, k_ref, v_ref):
        m_ = m_ref[...]
        mn = (m_ * _rstd(m_) * g_ref[...]).astype(BF16)
        mn_ref[...] = mn
        k_ref[...] = _dot(mn, wk_ref[...], 1, 0).astype(BF16)
        v_ref[...] = _dot(mn, wv_ref[...], 1, 0).astype(BF16)

    return pl.pallas_call(body, name="mem_kv", out_shape=(jax.ShapeDtypeStruct(mem.shape, BF16),) * 3,
                          compiler_params=_params())(mem, g_mem, wk, wv)


def _mem_kv_bwd(mn, mem, dk, dv, wk, wv, dep=None):
    deps = [] if dep is None else [dep]

    def body(mn_ref, m_ref, dk_ref, dv_ref, wk_ref, wv_ref, *rest):
        gk_ref, gv_ref, dg_ref = rest[len(deps):]
        mn = mn_ref[...]
        dkb, dvb = dk_ref[...].astype(BF16), dv_ref[...].astype(BF16)
        gk_ref[...] = _dot(mn, dkb, 0, 0).astype(BF16)
        gv_ref[...] = _dot(mn, dvb, 0, 0).astype(BF16)
        dmn = _dot(dkb, wk_ref[...], 1, 1) + _dot(dvb, wv_ref[...], 1, 1)
        m_ = m_ref[...]
        dg_ref[...] = _rowsum8(dmn * (m_ * _rstd(m_)))

    vmem = pl.BlockSpec(memory_space=pltpu.VMEM)
    return pl.pallas_call(
        body, name="mem_kv_bwd",
        out_shape=(jax.ShapeDtypeStruct(wk.shape, BF16), jax.ShapeDtypeStruct(wv.shape, BF16),
                   jax.ShapeDtypeStruct((8, D), F32)),
        in_specs=[vmem] * 6 + [ANY_SPEC] * len(deps), out_specs=(vmem,) * 3, compiler_params=_params(),
    )(mn, mem, dk, dv, wk, wv, *deps)


FB = 256


def _ffn_fwd_bwd(u, wgt, wut, wd, h, target, g_last, y_prev, g_post, g_pre, wo, t):
    tb = min(FB, t)

    def body(u_ref, wg_ref, wu_ref, wd_ref, h_ref, t_ref, gl_ref, yp_ref, gp_ref, gn_ref, wo_ref,
             a_ref, dy_ref, dg_ref, dup_ref, dh_ref, dyp_ref, do_ref, sq_ref, dgl_ref, dgn_ref, dgp_ref):
        @pl.when(pl.program_id(0) == 0)
        def _():
            for ref in (sq_ref, dgl_ref, dgn_ref, dgp_ref):
                ref[...] = jnp.zeros_like(ref)

        u_ = u_ref[...]
        g = _dot(u_, wg_ref[...], 1, 1)
        up = _dot(u_, wu_ref[...], 1, 1)
        sg = _sig(g)
        a = (g * sg * up).astype(BF16)
        a_ref[...] = a
        h_ = h_ref[...]
        sq, dh3, dy, dgl = _ep_final_loss(_dot(a, wd_ref[...], 1, 0), h_, t_ref[...], gl_ref[...])
        sq_ref[...] += sq
        dgl_ref[...] += dgl
        dy = dy.astype(BF16)
        dy_ref[...] = dy
        da = _dot(dy, wd_ref[...], 1, 1)
        dup = (da * g * sg).astype(BF16)
        dgate = (da * up * (sg * (1.0 + g * (1.0 - sg)))).astype(BF16)
        dup_ref[...] = dup
        dg_ref[...] = dgate
        du = _dot(dgate, wg_ref[...], 1, 0) + _dot(dup, wu_ref[...], 1, 0)
        dh, dyp, dgn, dgp = _ep_post_pre_bwd(du, dh3, h_, yp_ref[...], gp_ref[...], gn_ref[...])
        dh_ref[...] = dh
        dyp = dyp.astype(BF16)
        dyp_ref[...] = dyp
        do_ref[...] = _dot(dyp, wo_ref[...], 1, 1).astype(BF16)
        dgn_ref[...] += dgn
        dgp_ref[...] += dgp

    row = lambda w: pl.BlockSpec((tb, w), lambda i: (i, 0))
    whole = lambda a: pl.BlockSpec(a.shape, lambda i: (0,) * a.ndim, pipeline_mode=pl.Buffered(1))
    acc = pl.BlockSpec((8, D), lambda i: (0, 0))
    wide, half, sums = (jax.ShapeDtypeStruct((t, D_FF), BF16), jax.ShapeDtypeStruct((t, D), BF16),
                        jax.ShapeDtypeStruct((8, D), F32))
    return pl.pallas_call(
        body, name="ffn_fwd_bwd",
        out_shape=(wide, half, wide, wide, jax.ShapeDtypeStruct((t, D), F32), half, half, sums, sums, sums, sums),
        grid=(t // tb,),
        in_specs=[row(D), whole(wgt), whole(wut), whole(wd), row(D), row(D), whole(g_last), row(D), whole(g_post),
                  whole(g_pre), whole(wo)],
        out_specs=(row(D_FF), row(D), row(D_FF), row(D_FF), row(D), row(D), row(D), acc, acc, acc, acc),
        compiler_params=_params(dimension_semantics=("arbitrary",)),
    )(u, wgt, wut, wd, h, target, g_last, y_prev, g_post, g_pre, wo)


def _local_step(x, mem, target, fetch, sm, emit=None, first_dep=None, milestone=None):
    t = x.shape[0]
    w, gw = {}, {}

    def out(key, g):
        gw[key] = g
        return None if emit is None else emit(key, g)

    def tell(tag, value):
        return None if milestone is None else milestone(tag, value)
    u1 = _prenorm(x, sm["g_mix_pre"], name="prenorm_mix", dep=first_dep)
    w["winT"] = fetch("winT", u1)
    z = _mm_nt(u1, w["winT"], out_dtype=F32, tm=1024, tn=1408, name="mm_z")
    ymix, lse = _swa_fwd(z, sm["sinks"], t)
    ymix, o_h, sprev = _hgrn2_fwd(z, sm["hgrn_lb"], sm["hgrn_onorm"], ymix, t, dep=tell("swa", lse))
    for key in ("wout", "wq", "wk", "wv", "wo"):
        w[key] = fetch(key, ymix)
    y1, h1, u2, qx = _mm_rows([(ymix, w["wout"], False)], [x], [sm["g_mix_post"], sm["g_x_pre"], w["wq"]],
                              _then(_ep_post_pre, 2, False), _EP_POST_PRE_OUTS + [ROW_BF16], tm=512,
                              name="mm_y1_post_qx")
    mn, kx, vx = _mem_kv(mem, sm["g_mem"], w["wk"], w["wv"])
    ox, y2, h2, u3 = _xattn_fwd(qx, kx, vx, w["wo"], h1, sm["g_x_post"], sm["g_ffn_pre"], t, dep=tell("kv", kx))
    for key in ("wgT", "wuT", "wd"):
        w[key] = fetch(key, u3)
    act, dy3, dgate, dup, dh2, dy2, dox, sq, dg_ffn_post, dg_ffn_pre, dg_x_post = _ffn_fwd_bwd(
        u3, w["wgT"], w["wuT"], w["wd"], h2, target, sm["g_ffn_post"], y2, sm["g_x_post"], sm["g_ffn_pre"], w["wo"], t)
    dep = out("wd", _mm_tn(act, dy3, name="mm_gwd"))
    dep = out("wgT", _mm_tn(dgate, u3, name="mm_gwg", dep=dep))
    dep = out("wuT", _mm_tn(dup, u3, name="mm_gwu", dep=dep))
    out("wo", _mm_tn(ox, dy2, name="mm_gwo", dep=dep))
    dqx, dkx, dvx, dh1, dy1, dymix, dg_x_pre, dg_mix_post = _xattn_bwd(
        qx, kx, vx, dox, w["wq"], w["wout"], dh2, h1, y1, sm["g_mix_post"], sm["g_x_pre"], t)
    out("wq", _mm_tn(u2, dqx, name="mm_gwq"))
    gwk, gwv, dg_mem = _mem_kv_bwd(mn, mem, dkx, dvx, w["wk"], w["wv"])
    out("wk", gwk)
    dep = out("wv", gwv)
    dep = out("wout", _mm_tn(ymix, dy1, name="mm_gwout", dep=dep))
    *dza, dsinks = _swa_bwd(z, sm["sinks"], ymix, lse, dymix, t, dep=dep)
    dz, dlb, donorm = _hgrn2_bwd(z, sm["hgrn_lb"], sm["hgrn_onorm"], o_h, sprev, dymix, dza, t)
    dep = out("winT", _mm_tn(dz, u1, name="mm_gwin"))
    grad_x, dg_mix_pre = _mm_rows([(dz, w["winT"], False)], [dh1, x], [sm["g_mix_pre"]], _ep_pre_bwd,
                                  _EP_PRE_BWD_OUTS, tm=512, name="mm_du1_pre_bwd", dep=dep)
    parts = dict(g_mix_pre=dg_mix_pre, g_mix_post=dg_mix_post, g_mem=dg_mem, g_x_pre=dg_x_pre,
                 g_x_post=dg_x_post, g_ffn_pre=dg_ffn_pre, g_ffn_post=dg_ffn_post,
                 hgrn_onorm=donorm, hgrn_lb=dlb, sinks=dsinks, sq=sq)
    return grad_x, gw, parts


def _position():
    return lax.axis_index("x"), lax.axis_index("y"), lax.axis_index("c")


def _peer(pos, k):
    x, y, c = pos
    return (1 - x if k & 4 else x, 1 - y if k & 2 else y, 1 - c if k & 1 else c)


def _linear(pos):
    x, y, c = pos
    return 4 * x + 2 * y + c


HBM_SPEC = pl.BlockSpec(memory_space=pltpu.HBM)
SEM_SPEC = pl.BlockSpec(memory_space=pltpu.SEMAPHORE)
DATAFLOW = pltpu.SideEffectType.DATAFLOW_SIDE_EFFECTING
SEND_ORDER = (1, 2, 4, 3, 5, 6, 7)


def _in_hbm(a):
    return pltpu.with_memory_space_constraint(a, pltpu.HBM)


def _prepare_weights(shards, *, name, dep=None):
    n = len(shards)
    deps = [] if dep is None else [dep]

    def body(*refs):
        ins, (outs, lands, sem) = refs[:n], (refs[-2 * n - 1:-n - 1], refs[-n - 1:-1], refs[-1])
        me_lin = _linear(_position())
        copies = []
        for a in range(n):
            r = ins[a].shape[0]
            outs[a][...] = ins[a][...].astype(BF16)
            copies.append(pltpu.make_async_copy(outs[a], lands[a].at[pl.ds(me_lin * r, r), :], sem.at[a]))
            copies[-1].start()
        for cp in copies:
            cp.wait()

    vmem = pl.BlockSpec(memory_space=pltpu.VMEM)
    res = pl.pallas_call(
        body, name=name,
        out_shape=tuple(jax.ShapeDtypeStruct(s.shape, BF16) for s in shards)
        + tuple(jax.ShapeDtypeStruct((N_DEV * s.shape[0], s.shape[1]), BF16) for s in shards),
        in_specs=[vmem] * n + [ANY_SPEC] * len(deps), out_specs=tuple([vmem] * n + [ANY_SPEC] * n),
        scratch_shapes=[pltpu.SemaphoreType.DMA((n,))], compiler_params=_params(),
    )(*shards, *deps)
    return res[:n], res[n:]


def _copies_start(arrays, plan, n, *, name):
    na = len(arrays)

    def body(*refs):
        ins, send_sems, recv_sems = refs[:na], refs[na], refs[na + 1]
        me = _position()
        for j in range(n):
            src, dst, peer, _ = plan(ins, me, j)
            pltpu.make_async_remote_copy(src_ref=src, dst_ref=dst, send_sem=send_sems.at[j], recv_sem=recv_sems.at[j],
                                         device_id=peer, device_id_type=MESH).start()

    return pl.pallas_call(
        body, name=name,
        out_shape=(pltpu.SemaphoreType.DMA((n,)), pltpu.SemaphoreType.DMA((n,)))
        + tuple(pltpu.HBM(a.shape, a.dtype) for a in arrays),
        in_specs=(HBM_SPEC,) * na, out_specs=(SEM_SPEC, SEM_SPEC) + (HBM_SPEC,) * na,
        input_output_aliases={i: 2 + i for i in range(na)},
        compiler_params=pltpu.CompilerParams(has_side_effects=DATAFLOW),
    )(*[_in_hbm(a) for a in arrays])


def _copies_wait(send_sems, recv_sems, arrays, plan, n, after, *, name):
    na = len(arrays)

    def body(*refs):
        ins, send_sems, recv_sems = refs[:na], refs[na], refs[na + 1]
        me = _position()
        for j in range(n):
            src, _, peer, landed = plan(ins, me, j)
            copy = pltpu.make_async_remote_copy(src_ref=src, dst_ref=landed, send_sem=send_sems.at[j],
                                                recv_sem=recv_sems.at[j], device_id=peer, device_id_type=MESH)
            copy.wait_send()
            copy.wait_recv()

    return pl.pallas_call(
        body, name=name, out_shape=tuple(pltpu.HBM(a.shape, a.dtype) for a in arrays),
        in_specs=(HBM_SPEC,) * na + (SEM_SPEC, SEM_SPEC, ANY_SPEC), out_specs=(HBM_SPEC,) * na,
        input_output_aliases={i: i for i in range(na)},
        compiler_params=pltpu.CompilerParams(has_side_effects=DATAFLOW),
    )(*arrays, send_sems, recv_sems, after)


SAME_CORE = (2, 4, 6)


class _TwoLevelGather:
    def __init__(self, shards, lands, *, name):
        n = self.n = len(shards)
        self.name = name
        first_peers = (1,) + SAME_CORE

        def rows(ref, pos):
            r = ref.shape[0] // N_DEV
            return ref.at[pl.ds(_linear(pos) * r, r), :]

        def first(refs, me, j):
            a, peer = j // 4, _peer(me, first_peers[j % 4])
            return refs[a], rows(refs[n + a], me), peer, rows(refs[n + a], peer)

        def second(refs, me, j):
            a, sibling = j // 3, _peer(me, 1)
            mine = rows(refs[a], _peer(me, SAME_CORE[j % 3]))
            return mine, mine, sibling, rows(refs[a], _peer(sibling, SAME_CORE[j % 3]))

        self._first, self._second = first, second
        self._flight = _copies_start(list(shards) + list(lands), first, 4 * n, name=name + "_send")
        self.dep = self._flight[2]

    def pass_on(self, after):
        send1, recv1, *arrays = self._flight
        arrays = _copies_wait(send1, recv1, arrays, self._first, 4 * self.n, after, name=self.name + "_recv")
        self._flight = _copies_start(list(arrays[self.n:]), self._second, 3 * self.n, name=self.name + "_pass")
        return self._flight[2]

    def finish(self, after):
        send2, recv2, *lands = self._flight
        return _copies_wait(send2, recv2, lands, self._second, 3 * self.n, after, name=self.name + "_pass_recv")


def _exchange_start(gs, *, name):
    n = len(gs)
    rows = [g.shape[0] // N_DEV for g in gs]
    lands = [lax.empty((N_DEV - 1, r, g.shape[1]), g.dtype) for g, r in zip(gs, rows)]

    def body(*refs):
        g_refs, land_refs = refs[:n], refs[n:2 * n]
        send_sems, recv_sems = refs[2 * n:3 * n], refs[3 * n:4 * n]
        me = _position()
        for a in range(n):
            for k in SEND_ORDER:
                peer = _peer(me, k)
                pltpu.make_async_remote_copy(
                    src_ref=g_refs[a].at[pl.ds(_linear(peer) * rows[a], rows[a]), :],
                    dst_ref=land_refs[a].at[k - 1],
                    send_sem=send_sems[a].at[k - 1], recv_sem=recv_sems[a].at[k - 1],
                    device_id=peer, device_id_type=MESH).start()

    res = pl.pallas_call(
        body, name=name,
        out_shape=tuple(pltpu.SemaphoreType.DMA((N_DEV - 1,)) for _ in range(2 * n))
        + tuple(pltpu.HBM(a.shape, a.dtype) for a in gs + lands),
        in_specs=(HBM_SPEC,) * (2 * n), out_specs=(SEM_SPEC,) * (2 * n) + (HBM_SPEC,) * (2 * n),
        input_output_aliases={i: 2 * n + i for i in range(2 * n)},
        compiler_params=pltpu.CompilerParams(has_side_effects=DATAFLOW),
    )(*[_in_hbm(a) for a in gs + lands])
    return [(res[a], res[n + a], res[2 * n + a], res[3 * n + a]) for a in range(n)]


def _exchange_wait(send_sems, recv_sems, g_thru, land_thru, after, *, name):
    r = land_thru.shape[1]

    def body(g_ref, land_ref, send_sems, recv_sems, after_ref, g_dead, got_ref):
        del after_ref, g_dead, got_ref
        me = _position()
        for k in SEND_ORDER:
            peer = _peer(me, k)
            copy = pltpu.make_async_remote_copy(
                src_ref=g_ref.at[pl.ds(_linear(peer) * r, r), :], dst_ref=land_ref.at[k - 1],
                send_sem=send_sems.at[k - 1], recv_sem=recv_sems.at[k - 1],
                device_id=peer, device_id_type=MESH)
            copy.wait_send()
            copy.wait_recv()

    return pl.pallas_call(
        body, name=name,
        out_shape=(pltpu.HBM(g_thru.shape, g_thru.dtype), pltpu.HBM(land_thru.shape, land_thru.dtype)),
        in_specs=(HBM_SPEC, HBM_SPEC, SEM_SPEC, SEM_SPEC, pl.BlockSpec(memory_space=pl.ANY)),
        out_specs=(HBM_SPEC, HBM_SPEC), input_output_aliases={0: 0, 1: 1},
        compiler_params=pltpu.CompilerParams(has_side_effects=DATAFLOW),
    )(g_thru, land_thru, send_sems, recv_sems, after)


ADAMW_TILE_ROWS = 256


def _adamw_math(w, g, m, v):
    m = B1 * m + (1.0 - B1) * g
    v = B2 * v + (1.0 - B2) * (g * g)
    delta = -LR * ((m / C1) / (jnp.sqrt(v / C2) + AEPS) + WD * w)
    return delta, m, v


def _sum_adamw(items, *, name):
    n = len(items)
    r, d = items[0][2].shape
    assert all(it[2].shape == (r, d) for it in items)
    rc = r // 2 if r > ADAMW_TILE_ROWS else r
    tiles = [(a, r0) for a in range(n) for r0 in range(0, r, rc)]
    n_in, n_out = 5, 4

    def body(*refs):
        ins, outs = refs[:n_in * n], refs[n_in * n:(n_in + n_out) * n]
        land_v, own_v, f32_v, sems = refs[(n_in + n_out) * n:]
        me_lin = _linear(_position())

        def loads(j):
            a, r0 = tiles[j]
            g_all, land, w, m, v = ins[n_in * a:n_in * a + n_in]
            rows = pl.ds(r0, rc)
            pairs = [(land.at[:, rows, :], land_v.at[j]), (g_all.at[pl.ds(me_lin * r + r0, rc), :], own_v.at[j]),
                     (w.at[rows, :], f32_v.at[j, 0]), (m.at[rows, :], f32_v.at[j, 1]), (v.at[rows, :], f32_v.at[j, 2])]
            return [pltpu.make_async_copy(src, dst, sems.at[j, i]) for i, (src, dst) in enumerate(pairs)]

        def stores(j):
            a, r0 = tiles[j]
            return [pltpu.make_async_copy(f32_v.at[j, 3 + i], outs[n_out * a + i].at[pl.ds(r0, rc), :],
                                          sems.at[j, n_in + i]) for i in range(n_out)]

        for j in range(len(tiles)):
            for cp in loads(j):
                cp.start()
        for j in range(len(tiles)):
            for cp in loads(j):
                cp.wait()
            g = land_v[j, 0].astype(F32)
            for s in range(1, N_DEV - 1):
                g = g + land_v[j, s].astype(F32)
            g = own_v[j].astype(F32) + g
            f32_v[j, 3] = g
            f32_v[j, 4], f32_v[j, 5], f32_v[j, 6] = _adamw_math(f32_v[j, 0], g, f32_v[j, 1], f32_v[j, 2])
            for cp in stores(j):
                cp.start()
        for j in range(len(tiles)):
            for cp in stores(j):
                cp.wait()

    nt = len(tiles)
    res = pl.pallas_call(
        body, name=name,
        out_shape=tuple(jax.ShapeDtypeStruct((r, d), F32) for _ in range(n_out * n)),
        in_specs=[ANY_SPEC] * (n_in * n), out_specs=(ANY_SPEC,) * (n_out * n),
        scratch_shapes=[pltpu.VMEM((nt, N_DEV - 1, rc, d), BF16), pltpu.VMEM((nt, rc, d), BF16),
                        pltpu.VMEM((nt, 3 + n_out, rc, d), F32), pltpu.SemaphoreType.DMA((nt, n_in + n_out))],
        compiler_params=_params(),
    )(*[a for it in items for a in it])
    return [res[n_out * a:n_out * a + n_out] for a in range(n)]


SMALL = ("g_mix_pre", "g_mix_post", "g_mem", "g_x_pre", "g_x_post", "g_ffn_pre", "g_ffn_post",
         "hgrn_onorm", "hgrn_lb", "sinks")
SMALL_W = dict(hgrn_onorm=HD, hgrn_lb=HG_W, sinks=8)
SQ_ROW = len(SMALL)
PACK_ROWS = 16


def _small_pack(parts):
    ns = len(SMALL)

    def body(*refs):
        part, mine, slots, sem = refs[:ns + 1], refs[ns + 1], refs[ns + 2], refs[ns + 3]
        mine[...] = jnp.zeros((PACK_ROWS, D), F32)
        for r, name in enumerate(SMALL):
            wd = SMALL_W.get(name, D)
            mine[r:r + 1, 0:wd] = jnp.sum(part[r][...], axis=0, keepdims=True)[:, 0:wd]
        sq = jnp.sum(part[ns][...]) * (0.5 / D)
        mine[SQ_ROW:SQ_ROW + 1, :] = jnp.full((1, D), sq, F32)
        own = pltpu.make_async_copy(mine, slots.at[_linear(_position())], sem)
        own.start()
        own.wait()

    vmem = pl.BlockSpec(memory_space=pltpu.VMEM)
    return pl.pallas_call(
        body, name="small_pack",
        out_shape=(jax.ShapeDtypeStruct((PACK_ROWS, D), F32), jax.ShapeDtypeStruct((N_DEV, PACK_ROWS, D), F32)),
        in_specs=[vmem] * (ns + 1), out_specs=(vmem, ANY_SPEC),
        scratch_shapes=[pltpu.SemaphoreType.DMA(())], compiler_params=_params(),
    )(*[parts[n] for n in SMALL], parts["sq"])


def _small_exchange(mine, slots):
    def plan(refs, me, j):
        peer = _peer(me, j + 1)
        return refs[0], refs[1].at[_linear(me)], peer, refs[1].at[_linear(peer)]

    send, recv, mine1, slots1 = _copies_start([mine, slots], plan, N_DEV - 1, name="small_send")
    return lambda after: _copies_wait(send, recv, [mine1, slots1], plan, N_DEV - 1, after, name="small_recv")[1]


def _small_update(slots, sm, m_sm, v_sm):
    ns = len(SMALL)

    def body(*refs):
        tot = refs[0][0]
        for s in range(1, N_DEV):
            tot = tot + refs[0][s]
        w_refs, m_refs, v_refs = refs[1:ns + 1], refs[ns + 1:2 * ns + 1], refs[2 * ns + 1:3 * ns + 1]
        outs = refs[3 * ns + 1:]
        loss_ref = outs[0]
        g_out, d_out = outs[1:ns + 1], outs[ns + 1:2 * ns + 1]
        nm_out, nv_out = outs[2 * ns + 1:3 * ns + 1], outs[3 * ns + 1:4 * ns + 1]
        loss_ref[...] = tot[SQ_ROW:SQ_ROW + 1, 0:1]
        for r, name in enumerate(SMALL):
            wd = SMALL_W.get(name, D)
            g = tot[r:r + 1, 0:wd]
            w = w_refs[r][...]
            if name == "hgrn_lb":
                mx = jnp.maximum(w[0:1], w[1:2])
                e0, e1 = jnp.exp(w[0:1] - mx), jnp.exp(w[1:2] - mx)
                lb0 = e0 / (e0 + e1)
                g0 = g * lb0 * (1.0 - lb0)
                for i, gi in enumerate((g0, -g0)):
                    d, nm, nv = _adamw_math(w[i:i + 1], gi, m_refs[r][i:i + 1, :], v_refs[r][i:i + 1, :])
                    g_out[r][i:i + 1, :] = gi
                    d_out[r][i:i + 1, :], nm_out[r][i:i + 1, :], nv_out[r][i:i + 1, :] = d, nm, nv
            else:
                d, nm, nv = _adamw_math(w, g, m_refs[r][...], v_refs[r][...])
                g_out[r][...] = g
                d_out[r][...], nm_out[r][...], nv_out[r][...] = d, nm, nv

    shapes = [jax.ShapeDtypeStruct(sm[n].shape, F32) for n in SMALL]
    res = pl.pallas_call(
        body, name="small_update", out_shape=tuple([jax.ShapeDtypeStruct((1, 1), F32)] + shapes * 4),
        compiler_params=_params(),
    )(slots, *[sm[n] for n in SMALL], *[m_sm[n] for n in SMALL], *[v_sm[n] for n in SMALL])
    groups = [dict(zip(SMALL, res[1 + i * ns:1 + (i + 1) * ns])) for i in range(4)]
    return res[0], groups[0], groups[1], groups[2], groups[3]


BIG = ("w_in", "w_gate", "w_up", "w_down", "w_out", "wq_x", "wk_x", "wv_x", "wo_x")
BIG_KEY = dict(w_in="winT", w_gate="wgT", w_up="wuT", w_down="wd", w_out="wout", wq_x="wq", wk_x="wk",
               wv_x="wv", wo_x="wo")
TRANSPOSED = ("w_in", "w_gate", "w_up")
WEIGHTS = ("w_in", "sinks", "hgrn_lb", "hgrn_onorm", "w_out", "g_mix_pre", "g_mix_post", "g_mem", "g_x_pre",
           "g_x_post", "wq_x", "wk_x", "wv_x", "wo_x", "g_ffn_pre", "g_ffn_post", "w_gate", "w_up", "w_down")


def kernel(x, mem, w_in, sinks, hgrn_lb, hgrn_onorm, w_out, g_mix_pre, g_mix_post, g_mem, g_x_pre, g_x_post, wq_x, wk_x, wv_x, wo_x, g_ffn_pre, g_ffn_post, w_gate, w_up, w_down, loss_target, m_w_in, m_sinks, m_hgrn_lb, m_hgrn_onorm, m_w_out, m_g_mix_pre, m_g_mix_post, m_g_mem, m_g_x_pre, m_g_x_post, m_wq_x, m_wk_x, m_wv_x, m_wo_x, m_g_ffn_pre, m_g_ffn_post, m_w_gate, m_w_up, m_w_down, v_w_in, v_sinks, v_hgrn_lb, v_hgrn_onorm, v_w_out, v_g_mix_pre, v_g_mix_post, v_g_mem, v_g_x_pre, v_g_x_post, v_wq_x, v_wk_x, v_wv_x, v_wo_x, v_g_ffn_pre, v_g_ffn_post, v_w_gate, v_w_up, v_w_down):
    given = dict(locals())
    wts = {n: given[n] for n in WEIGHTS}
    ms = {n: given["m_" + n] for n in WEIGHTS}
    vs = {n: given["v_" + n] for n in WEIGHTS}

    def mat(a, name):
        a = a[0]
        return a.T if name in TRANSPOSED else a

    groups = (("w_in",), ("w_out", "wq_x", "wk_x", "wv_x", "wo_x"), ("w_gate", "w_up", "w_down"))
    gathers = []
    first_dep = None
    for tag, group in zip(("w_in", "w_attn", "w_ffn"), groups):
        shards, lands = _prepare_weights([mat(wts[n], n) for n in group], name="prepare_" + tag, dep=first_dep)
        gathers.append(_TwoLevelGather(shards, lands, name=tag))
        first_dep = gathers[-1].dep
    name_of = {k: n for n, k in BIG_KEY.items()}
    gathered = {}

    def milestone(tag, value):
        return gathers[{"swa": 1, "kv": 2}[tag]].pass_on(value)

    def fetch(key, after):
        name = name_of[key]
        if name not in gathered:
            g = [i for i, group in enumerate(groups) if name in group][0]
            if g == 0:
                gathers[0].pass_on(after)
            gathered.update(zip(groups[g], gathers[g].finish(after)))
        return gathered[name]

    sm = {n: wts[n] for n in SMALL}
    started, held = {}, {}
    send_with = {k: group for group in (("wgT", "wuT"), ("wo", "wq", "wk", "wv")) for k in group}

    def emit(key, g):
        held[key] = g
        group = send_with.get(key, (key,))
        if key != group[-1]:
            return None
        flights = _exchange_start([held[k] for k in group], name="grad_send_" + name_of[group[0]])
        started.update({name_of[k]: f for k, f in zip(group, flights)})
        return flights[-1][2]

    grad_x, _, parts = _local_step(x[0], mem[0], loss_target[0], fetch, sm, emit, first_dep=first_dep, milestone=milestone)
    small_finish = _small_exchange(*_small_pack(parts))
    grads, deltas, new_m, new_v = {}, {}, {}, {}
    after = grad_x
    for group in (("w_down",), ("w_gate", "w_up"), ("wo_x", "wq_x", "wk_x", "wv_x", "w_out"), ("w_in",)):
        items = []
        for n in group:
            g_all, land = _exchange_wait(*started[n], after, name="grad_recv_" + n)
            items.append((g_all, land, mat(wts[n], n), mat(ms[n], n), mat(vs[n], n)))
            after = land
        for n, res in zip(group, _sum_adamw(items, name="adamw_" + group[0])):
            after = res[1]
            if n in TRANSPOSED:
                res = [a.T for a in res]
            grads[n], deltas[n], new_m[n], new_v[n] = [a[None] for a in res]
    loss, g_s, d_s, m_s, v_s = _small_update(small_finish(after), sm, {n: ms[n] for n in SMALL},
                                             {n: vs[n] for n in SMALL})
    grads.update(g_s), deltas.update(d_s), new_m.update(m_s), new_v.update(v_s)
    return (loss[0, 0], grad_x[None], *[grads[n] for n in WEIGHTS], *[deltas[n] for n in WEIGHTS],
            *[new_m[n] for n in WEIGHTS], *[new_v[n] for n in WEIGHTS])
```

```python
import functools

import jax
import jax.numpy as jnp
from jax import lax
from jax.experimental import pallas as pl
from jax.experimental.pallas import tpu as pltpu

F32 = jnp.float32
BF16 = jnp.bfloat16

D = 1024
D_IN = 2816
D_FF = 2816
CHUNK = 64
SWA_W = 512
KV_W = 128
HG_W = 512
HD = 128
ZQH, ZFH, ZIH, ZGH = 768, 1280, 1792, 2304
XH, XD = 4, 256
EPS = 1e-6
NEG = -1e30
N_DEV = 8
MESH = pl.DeviceIdType.MESH

LR, B1, B2, AEPS, WD, STEP = 0.001, 0.9, 0.999, 1e-08, 0.01, 10
C1 = 1.0 - B1 ** STEP
C2 = 1.0 - B2 ** STEP

VMEM_LIMIT = 56 * 1024 * 1024


def _params(**kw):
    return pltpu.CompilerParams(vmem_limit_bytes=VMEM_LIMIT, **kw)


def _sig(x):
    return 1.0 / (1.0 + jnp.exp(-x))


def _rowsum8(x):
    r, w = x.shape
    return jnp.sum(x.reshape(r // 8, 8, w), axis=0)


def _dot(a, b, ca, cb, precision=None):
    return lax.dot_general(a, b, (((ca,), (cb,)), ((), ())), preferred_element_type=F32,
                           precision=precision)


ANY_SPEC = pl.BlockSpec(memory_space=pl.ANY)


def _mm_nt(a, b, *, out_dtype, tm, tn, name):
    (m, k), n = a.shape, b.shape[0]
    tm, tn = min(tm, m), min(tn, n)
    assert a.dtype == BF16 and b.dtype == BF16 and m % tm == 0 and n % tn == 0, (name, m, n, tm, tn)

    def body(a_ref, b_ref, o_ref):
        o_ref[...] = _dot(a_ref[...], b_ref[...], 1, 1).astype(out_dtype)

    return pl.pallas_call(
        body, name=name, out_shape=jax.ShapeDtypeStruct((m, n), out_dtype), grid=(n // tn, m // tm),
        in_specs=[pl.BlockSpec((tm, k), lambda j, i: (i, 0)), pl.BlockSpec((tn, k), lambda j, i: (j, 0))],
        out_specs=pl.BlockSpec((tm, tn), lambda j, i: (i, j)),
        compiler_params=_params(dimension_semantics=("parallel", "parallel")),
    )(a, b)


TN_FIRST = 256
TN_REST = 1152
TN_SLICES = 4


def _mm_tn(a_list, b, *, name, dep=None):
    na, (k, m), n = len(a_list), a_list[0].shape, b.shape[1]
    assert b.dtype == BF16 and b.shape[0] == k and all(a.dtype == BF16 and a.shape == (k, m) for a in a_list)
    widths = [TN_FIRST, TN_FIRST]
    while sum(widths) < m:
        widths.append(min(TN_REST, m - sum(widths)))
    starts = [sum(widths[:i]) for i in range(len(widths))]
    assert sum(widths) == m
    per = len(widths)
    nb = na * per
    ks = k // TN_SLICES
    ahead = 2
    assert ahead < per
    deps = [] if dep is None else [dep]

    def body(*refs):
        a_hbm, b_hbm, rest = refs[:na], refs[na], refs[na + 1 + len(deps):]
        o_hbm, b_v, a_v, o_v, sems = rest[:na], rest[na], rest[na + 1:na + 1 + per], rest[na + 1 + per:-1], rest[-1]
        sliced = []
        for c in range(TN_SLICES):
            rows = pl.ds(c * ks, ks)
            sliced.append((pltpu.make_async_copy(b_hbm.at[rows, :], b_v.at[rows, :], sems.at[2 * c]),
                           pltpu.make_async_copy(a_hbm[0].at[rows, pl.ds(0, widths[0])], a_v[0].at[rows, :],
                                                 sems.at[2 * c + 1])))
        base = 2 * TN_SLICES - 1
        cols = [pl.ds(starts[i % per], widths[i % per]) for i in range(nb)]
        loads = [None] + [pltpu.make_async_copy(a_hbm[i // per].at[:, cols[i]], a_v[i % per], sems.at[base + i])
                          for i in range(1, nb)]
        stores = [pltpu.make_async_copy(o_v[i % per], o_hbm[i // per].at[cols[i], :], sems.at[base + nb + i])
                  for i in range(nb)]
        for pair in sliced:
            for cp in pair:
                cp.start()
        for i in range(1, 1 + ahead):
            loads[i].start()
        acc = None
        for c, pair in enumerate(sliced):
            for cp in pair:
                cp.wait()
            p = _dot(a_v[0][c * ks:(c + 1) * ks, :], b_v[c * ks:(c + 1) * ks, :], 0, 0)
            acc = p if acc is None else acc + p
        o_v[0][...] = acc.astype(BF16)
        stores[0].start()
        for i in range(1, nb):
            loads[i].wait()
            if i + ahead < nb:
                loads[i + ahead].start()
            if i >= per:
                stores[i - per].wait()
            o_v[i % per][...] = _dot(a_v[i % per][...], b_v[...], 0, 0).astype(BF16)
            stores[i].start()
        for cp in stores[nb - per:]:
            cp.wait()

    return pl.pallas_call(
        body, name=name, out_shape=tuple(jax.ShapeDtypeStruct((m, n), BF16) for _ in a_list),
        in_specs=[ANY_SPEC] * (na + 1 + len(deps)), out_specs=(ANY_SPEC,) * na,
        scratch_shapes=[pltpu.VMEM((k, n), BF16)] + [pltpu.VMEM((k, cw), BF16) for cw in widths]
        + [pltpu.VMEM((cw, n), BF16) for cw in widths] + [pltpu.SemaphoreType.DMA((2 * TN_SLICES - 1 + 2 * nb,))],
        compiler_params=_params(),
    )(*a_list, b, *deps)


def _mm_rows(prods, rows_in, vecs_in, epilogue, outs, *, tm, name, dep=None):
    m = prods[0][0].shape[0]
    n = prods[0][1].shape[0] if prods[0][2] else prods[0][1].shape[1]
    tm = min(tm, m)
    assert m % tm == 0
    deps = [] if dep is None else [dep]
    n_p, n_r, n_v = len(prods), len(rows_in), len(vecs_in)

    def body(*refs):
        ab = refs[:2 * n_p]
        row_refs = refs[2 * n_p:2 * n_p + n_r]
        vec_refs = refs[2 * n_p + n_r:2 * n_p + n_r + n_v]
        out_refs = refs[2 * n_p + n_r + n_v + len(deps):]
        p = None
        for j, (_, _, tb) in enumerate(prods):
            t = _dot(ab[2 * j][...].astype(BF16), ab[2 * j + 1][...], 1, 1 if tb else 0)
            p = t if p is None else p + t
        vals = epilogue(p, *[r[...] for r in row_refs], *[v[...] for v in vec_refs])
        for (dtype, kind), o_ref, val in zip(outs, out_refs, vals):
            if kind == "row":
                o_ref[...] = val.astype(dtype)
            else:
                @pl.when(pl.program_id(0) == 0)
                def _(o_ref=o_ref):
                    o_ref[...] = jnp.zeros_like(o_ref)

                o_ref[...] += val

    row = lambda w: pl.BlockSpec((tm, w), lambda i: (i, 0))
    whole = lambda a: pl.BlockSpec(a.shape, lambda i: (0,) * a.ndim, pipeline_mode=pl.Buffered(1))
    in_specs, args = [], []
    for a, b, _ in prods:
        in_specs += [row(a.shape[1]), whole(b)]
        args += [a, b]
    in_specs += [row(r.shape[1]) for r in rows_in] + [whole(v) for v in vecs_in] + [ANY_SPEC] * len(deps)
    return pl.pallas_call(
        body, name=name,
        out_shape=tuple(jax.ShapeDtypeStruct((m, n) if kind == "row" else (8, n), dtype) for dtype, kind in outs),
        grid=(m // tm,), in_specs=in_specs,
        out_specs=tuple(row(n) if kind == "row" else pl.BlockSpec((8, n), lambda i: (0, 0)) for _, kind in outs),
        compiler_params=_params(dimension_semantics=("arbitrary",)),
    )(*args, *rows_in, *vecs_in, *deps)


def _rstd(x):
    return lax.rsqrt(jnp.mean(x * x, axis=-1, keepdims=True) + EPS)


def _norm_bwd(xh, r, t):
    return r * (t - xh * jnp.mean(xh * t, axis=-1, keepdims=True))


ROW_F32, ROW_BF16, SUM_F32 = (F32, "row"), (BF16, "row"), (F32, "sum")


def _then(epilogue, index, tb):
    def run(p, *args):
        vals = epilogue(p, *args[:-1])
        return (*vals, _dot(vals[index].astype(BF16), args[-1], 1, 1 if tb else 0))

    return run


def _ep_post_pre(p, h, g_post, g_pre):
    y = p.astype(BF16)
    yf = y.astype(F32)
    hn = h + yf * _rstd(yf) * g_post
    return y, hn, hn * _rstd(hn) * g_pre


_EP_POST_PRE_OUTS = [ROW_BF16, ROW_F32, ROW_BF16]


def _ep_final_loss(y, h, target, g_post):
    r = _rstd(y)
    yh = y * r
    err = h + yh * g_post - target
    dh = err * (1.0 / D)
    return _rowsum8(err * err), dh, _norm_bwd(yh, r, dh * g_post), _rowsum8(dh * yh)


def _ep_post_pre_bwd(du, dh_out, hn, y, g_post, g_pre):
    r2 = _rstd(hn)
    xh = hn * r2
    dh = dh_out + _norm_bwd(xh, r2, du * g_pre)
    yf = y.astype(F32)
    r1 = _rstd(yf)
    yh = yf * r1
    return dh, _norm_bwd(yh, r1, dh * g_post), _rowsum8(du * xh), _rowsum8(dh * yh)


_EP_POST_PRE_BWD_OUTS = [ROW_F32, ROW_BF16, SUM_F32, SUM_F32]


def _ep_pre_bwd(du, dh_out, x, g):
    r = _rstd(x)
    xh = x * r
    return dh_out + _norm_bwd(xh, r, du * g), _rowsum8(du * xh)


_EP_PRE_BWD_OUTS = [ROW_F32, SUM_F32]


def _prenorm(x, g, *, name, dep=None):
    t, d = x.shape
    tb = min(512, t)
    deps = [] if dep is None else [dep]

    def body(x_ref, g_ref, *rest):
        xf = x_ref[...]
        rest[-1][...] = (xf * _rstd(xf) * g_ref[...]).astype(BF16)

    return pl.pallas_call(
        body, name=name, out_shape=jax.ShapeDtypeStruct((t, d), BF16), grid=(t // tb,),
        in_specs=[pl.BlockSpec((tb, d), lambda i: (i, 0)), pl.BlockSpec((1, d), lambda i: (0, 0))]
        + [ANY_SPEC] * len(deps),
        out_specs=pl.BlockSpec((tb, d), lambda i: (i, 0)), compiler_params=_params(),
    )(x, g, *deps)


QB = 256


def _half_mask(shape, e):
    lane = lax.broadcasted_iota(jnp.int32, shape, len(shape) - 1)
    return (lane // 64) == e


def _place(kv):
    sw = pltpu.roll(kv, 64, 1)
    m0 = _half_mask(kv.shape, 0)
    return [[jnp.where(m0, kv, 0.0).astype(BF16), jnp.where(m0, 0.0, sw).astype(BF16)],
            [jnp.where(m0, sw, 0.0).astype(BF16), jnp.where(m0, 0.0, kv).astype(BF16)]]


SQ = 128
SK = 256


def _swa_valid(i, sb):
    qc = lax.broadcasted_iota(jnp.int32, (SQ, SK), 0) // CHUNK
    kc = lax.broadcasted_iota(jnp.int32, (SQ, SK), 1) // CHUNK - 2
    return (kc <= qc) & (qc <= kc + 2) & (4 * i + 2 * sb + kc >= 0)


def _swa_fwd(z, sinks, t, dep=None):
    nb = t // QB
    deps = [] if dep is None else [dep]

    def body(s_ref, q_ref, kp_ref, kc_ref, vp_ref, vc_ref, *rest):
        o_ref, lse_ref = rest[-2:]
        i = pl.program_id(0)
        kpl = _place(jnp.concatenate([kp_ref[...], kc_ref[...]], axis=0))
        vpl = _place(jnp.concatenate([vp_ref[...], vc_ref[...]], axis=0))
        lane = lax.broadcasted_iota(jnp.int32, (SQ, 128), 1)
        for sb in range(QB // SQ):
            rows, keys = slice(SQ * sb, SQ * (sb + 1)), slice(SQ * sb, SQ * sb + SK)
            valid = _swa_valid(i, sb)
            lse_out = jnp.zeros((SQ, 128), F32)
            for j in range(4):
                qp = q_ref[rows, 128 * j:128 * (j + 1)].astype(BF16)
                acc = jnp.zeros((SQ, 128), F32)
                for e in range(2):
                    h = 2 * j + e
                    kvh = h // 4
                    qm = jnp.where(_half_mask(qp.shape, e), qp, jnp.zeros_like(qp))
                    s = _dot(qm, kpl[kvh][e][keys], 1, 1) * 0.125
                    s = jnp.where(valid, s, NEG)
                    sink = s_ref[0, h]
                    m = jnp.maximum(jnp.max(s, axis=-1, keepdims=True), sink)
                    p = jnp.exp(s - m)
                    l = jnp.sum(p, axis=-1, keepdims=True) + jnp.exp(sink - m)
                    acc = acc + _dot(p.astype(BF16), vpl[kvh][e][keys], 1, 0) * (1.0 / l)
                    lse_out = jnp.where(lane == h, m + jnp.log(l), lse_out)
                o_ref[rows, 128 * j:128 * (j + 1)] = acc.astype(BF16)
            lse_ref[rows, :] = lse_out

    prev = lambda c: pl.BlockSpec((128, 128), lambda i: (jnp.maximum(2 * i - 1, 0), c))
    cur = lambda c: pl.BlockSpec((QB, 128), lambda i: (i, c))
    return pl.pallas_call(
        body, name="swa_fwd",
        out_shape=(jax.ShapeDtypeStruct((t, D), BF16), jax.ShapeDtypeStruct((t, 128), F32)),
        grid=(nb,),
        in_specs=[pl.BlockSpec(memory_space=pltpu.SMEM),
                  pl.BlockSpec((QB, SWA_W), lambda i: (i, 0)), prev(4), cur(4), prev(5), cur(5)]
        + [ANY_SPEC] * len(deps),
        out_specs=(pl.BlockSpec((QB, SWA_W), lambda i: (i, 0)), pl.BlockSpec((QB, 128), lambda i: (i, 0))),
        compiler_params=_params(),
    )(sinks, z, z, z, z, z, *deps)


def _swa_bwd(z, sinks, ymix, lse, dymix, t, dep=None):
    nb = t // QB
    deps = [] if dep is None else [dep]

    def body(s_ref, q_ref, kp_ref, kc_ref, vp_ref, vc_ref, o_ref, do_ref, l_ref, *rest):
        dq_ref, first_ref, second_ref, ds_ref, carry_ref = rest[len(deps):]
        i = pl.program_id(0)
        live = i < nb

        @pl.when(i == 0)
        def _():
            ds_ref[...] = jnp.zeros_like(ds_ref)
            carry_ref[...] = jnp.zeros_like(carry_ref)

        lane = lax.broadcasted_iota(jnp.int32, (8, 128), 1)
        kpl = _place(jnp.concatenate([kp_ref[...], kc_ref[...]], axis=0))
        vpl = _place(jnp.concatenate([vp_ref[...], vc_ref[...]], axis=0))
        nk = QB + 128
        qc = lax.broadcasted_iota(jnp.int32, (QB, nk), 0) // CHUNK
        kc = lax.broadcasted_iota(jnp.int32, (QB, nk), 1) // CHUNK - 2
        valid = (kc <= qc) & (qc <= kc + 2) & (4 * i + kc >= 0) & live
        lse_c = l_ref[...]
        dsink = jnp.zeros((8, 128), F32)
        dk_acc = [[jnp.zeros((128, nk), F32) for _ in range(2)] for _ in range(2)]
        dv_acc = [[jnp.zeros((128, nk), F32) for _ in range(2)] for _ in range(2)]
        dq = []
        for j in range(4):
            cols = slice(128 * j, 128 * (j + 1))
            qp = q_ref[:, cols].astype(BF16)
            dop = do_ref[:, cols]
            prod = dop.astype(F32) * o_ref[:, cols].astype(F32)
            acc = jnp.zeros((QB, 128), F32)
            for e in range(2):
                h = 2 * j + e
                kvh = h // 4
                hm = _half_mask(qp.shape, e)
                qm = jnp.where(hm, qp, jnp.zeros_like(qp))
                dom = jnp.where(hm, dop, jnp.zeros_like(dop))
                dd = jnp.sum(jnp.where(hm, prod, 0.0), axis=-1, keepdims=True)
                lse_h = lse_c[:, h:h + 1]
                s = _dot(qm, kpl[kvh][e], 1, 1) * 0.125
                p = jnp.where(valid, jnp.exp(s - lse_h), 0.0)
                dp = _dot(dom, vpl[kvh][e], 1, 1)
                ds = (p * (dp - dd) * 0.125).astype(BF16)
                acc = acc + _dot(ds, kpl[kvh][e], 1, 0)
                dk_acc[kvh][e] = dk_acc[kvh][e] + _dot(qm, ds, 0, 0)
                dv_acc[kvh][e] = dv_acc[kvh][e] + _dot(dom, p.astype(BF16), 0, 0)
                ps = jnp.where(live, jnp.exp(s_ref[0, h] - lse_h) * dd, 0.0)
                dsink = dsink - jnp.where(lane == h, _rowsum8(jnp.broadcast_to(ps, (QB, 128))), 0.0)
            dq.append(acc.astype(BF16))
        ds_ref[...] += dsink
        dk = (dk_acc[0][0] + dk_acc[1][1] + pltpu.roll(dk_acc[0][1] + dk_acc[1][0], 64, 0)).T
        dv = (dv_acc[0][0] + dv_acc[1][1] + pltpu.roll(dv_acc[0][1] + dv_acc[1][0], 64, 0)).T
        dkv = jnp.concatenate([dk, dv], axis=1)
        second_ref[...] = (carry_ref[...] + dkv[0:128]).astype(BF16)
        carry_ref[...] = dkv[256:384]

        @pl.when(live)
        def _():
            for j in range(4):
                dq_ref[:, 128 * j:128 * (j + 1)] = dq[j]
            first_ref[...] = dkv[128:256].astype(BF16)

    blk = lambda i: jnp.minimum(i, nb - 1)
    prev = lambda c: pl.BlockSpec((128, 128), lambda i: (jnp.maximum(2 * blk(i) - 1, 0), c))
    cur = lambda w, c: pl.BlockSpec((QB, w), lambda i: (blk(i), c))
    half = lambda index: pl.BlockSpec((128, 256), lambda i: (index(i), 0))
    return pl.pallas_call(
        body, name="swa_bwd",
        out_shape=(jax.ShapeDtypeStruct((t, SWA_W), BF16), jax.ShapeDtypeStruct((t // 2, 256), BF16),
                   jax.ShapeDtypeStruct((t // 2, 256), BF16), jax.ShapeDtypeStruct((8, 128), F32)),
        grid=(nb + 1,),
        in_specs=[pl.BlockSpec(memory_space=pltpu.SMEM),
                  cur(SWA_W, 0), prev(4), cur(128, 4), prev(5), cur(128, 5),
                  cur(SWA_W, 0), cur(SWA_W, 0), cur(128, 0)] + [ANY_SPEC] * len(deps),
        out_specs=(cur(SWA_W, 0), half(blk), half(lambda i: jnp.maximum(i - 1, 0)),
                   pl.BlockSpec((8, 128), lambda i: (0, 0))),
        scratch_shapes=[pltpu.VMEM((128, 256), F32)],
        compiler_params=_params(dimension_semantics=("arbitrary",)),
    )(sinks, z, z, z, z, z, ymix, dymix, lse, *deps)


HB = 256


def _lower_bound(lb_ref):
    a = lb_ref[...]
    a0, a1 = a[0:1], a[1:2]
    mx = jnp.maximum(a0, a1)
    e0, e1 = jnp.exp(a0 - mx), jnp.exp(a1 - mx)
    return e0 / (e0 + e1)


def _hgrn_cols(row_block):
    return [pl.BlockSpec((HB, 2 * HD), lambda j, c=base // (2 * HD) + p: (row_block(j), c))
            for base in (ZQH, ZFH, ZIH, ZGH) for p in range(2)]


NCH = HB // CHUNK


def _split3(x):
    hi = x.astype(BF16)
    r1 = x - hi.astype(F32)
    mid = r1.astype(BF16)
    return hi, mid, (r1 - mid.astype(F32)).astype(BF16)


def _blockdiag(lower):
    r = lax.broadcasted_iota(jnp.int32, (HB, HB), 0)
    c = lax.broadcasted_iota(jnp.int32, (HB, HB), 1)
    return (r // CHUNK == c // CHUNK) & ((c <= r) if lower else (c >= r))


def _chunk_sums(mask_bf16, x):
    return sum(_dot(mask_bf16, part, 1, 0) for part in _split3(x))


def _per_chunk_rows(x, row):
    w = x.shape[1]
    picked = x.reshape(NCH, CHUNK, w)[:, row:row + 1, :]
    return jnp.broadcast_to(picked, (NCH, CHUNK, w)).reshape(HB, w)


def _chunk_stack(x, chunk_of_row):
    return jnp.concatenate([jnp.where(chunk_of_row == c, x, jnp.zeros_like(x)) for c in range(NCH)], axis=1)


def _chunk_pick(x, chunk_of_row):
    w = x.shape[1] // NCH
    out = jnp.zeros((HB, w), x.dtype)
    for c in range(NCH):
        out = jnp.where(chunk_of_row == c, x[:, c * w:(c + 1) * w], out)
    return out


def _hgrn_local(q, f, kf, b):
    sq = _sig(q)
    qf = q * sq * (HD ** -0.5)
    b_mid = _per_chunk_rows(b, CHUNK // 2 - 1)
    b_last = _per_chunk_rows(b, CHUNK - 1)
    qm = qf * jnp.exp(b - b_mid)
    km = kf * jnp.exp(b_mid - b)
    kl = kf * jnp.exp(b_last - b)
    qb = qf * jnp.exp(b)
    return dict(sq=sq, b_mid=b_mid, b_last=b_last, qm=qm, km=km, kl=kl, qb=qb)


def _hgrn2_fwd(z, hgrn_lb, onorm, ymix, t, dep=None):
    nb = t // HB
    deps = [] if dep is None else [dep]

    def body(*refs):
        zq, zf, zi, zg = refs[0:2], refs[2:4], refs[4:6], refs[6:8]
        (lb_ref, on_ref), (y_ref, o_ref, sp_ref, st_ref) = refs[8:10], refs[-4:]

        @pl.when(pl.program_id(0) == 0)
        def _():
            st_ref[...] = jnp.zeros_like(st_ref)

        lb_all = _lower_bound(lb_ref)
        gn = on_ref[...]
        low = _blockdiag(True)
        low_b = low.astype(BF16)
        chunk_of_row = lax.broadcasted_iota(jnp.int32, (HB, HD), 0) // CHUNK
        for p in range(2):
            lbp = lb_all[:, 2 * HD * p:2 * HD * (p + 1)]
            fp = lbp + (1.0 - lbp) * _sig(zf[p][...])
            bp = _chunk_sums(low_b, jnp.log(fp))
            for e in range(2):
                h, ls = 2 * p + e, slice(e * HD, (e + 1) * HD)
                f = fp[:, ls]
                w = _hgrn_local(zq[p][:, ls], f, 1.0 - f, bp[:, ls])
                iv = zi[p][:, ls].astype(BF16)
                a = jnp.where(low, _dot(w["qm"].astype(BF16), w["km"].astype(BF16), 1, 1), 0.0)
                o = _dot(a.astype(BF16), iv, 1, 0)
                u = _dot(iv, _chunk_stack(w["kl"].astype(BF16), chunk_of_row), 0, 0)
                decay = jnp.exp(w["b_last"])
                st = st_ref[h]
                states = []
                for c in range(NCH):
                    sp_ref[h, c] = st
                    states.append(st.astype(BF16))
                    st = st * decay[c * CHUNK:c * CHUNK + 1] + u[:, c * HD:(c + 1) * HD]
                st_ref[h] = st
                inter = _dot(w["qb"].astype(BF16), jnp.concatenate(states, axis=0), 1, 1)
                o = o + _chunk_pick(inter, chunk_of_row)
                hs = slice(h * HD, (h + 1) * HD)
                o_ref[:, hs] = o
                gg = zg[p][:, ls]
                y_ref[:, hs] = (o * _rstd(o) * gn * (gg * _sig(gg))).astype(BF16)

    return pl.pallas_call(
        body, name="hgrn_fwd",
        out_shape=(jax.ShapeDtypeStruct((t, D), BF16), jax.ShapeDtypeStruct((t, HG_W), F32),
                   jax.ShapeDtypeStruct((4, t // CHUNK, HD, HD), F32)),
        grid=(nb,),
        in_specs=_hgrn_cols(lambda j: j) + [pl.BlockSpec((2, HG_W), lambda j: (0, 0)),
                                            pl.BlockSpec((1, HD), lambda j: (0, 0)), ANY_SPEC]
        + [ANY_SPEC] * len(deps),
        out_specs=(pl.BlockSpec((HB, HG_W), lambda j: (j, 1)),
                   pl.BlockSpec((HB, HG_W), lambda j: (j, 0)),
                   pl.BlockSpec((4, NCH, HD, HD), lambda j: (0, j, 0, 0))),
        scratch_shapes=[pltpu.VMEM((4, HD, HD), F32)],
        input_output_aliases={10: 0},
        compiler_params=_params(dimension_semantics=("arbitrary",)),
    )(*[z] * 8, hgrn_lb, onorm, ymix, *deps)


def _hgrn2_bwd(z, hgrn_lb, onorm, o_save, sprev, dymix, dza, t):
    nb = t // HB

    def body(*refs):
        zq, zf, zi, zg = refs[0:2], refs[2:4], refs[4:6], refs[6:8]
        (lb_ref, on_ref, o_ref, sp_ref, dy_ref, dqa_ref, first_ref, second_ref,
         dz_ref, dlb_ref, don_ref, dst_ref) = refs[8:]

        @pl.when(pl.program_id(0) == 0)
        def _():
            dst_ref[...] = jnp.zeros_like(dst_ref)
            dlb_ref[...] = jnp.zeros_like(dlb_ref)
            don_ref[...] = jnp.zeros_like(don_ref)

        dz_ref[:, 0:SWA_W] = dqa_ref[...]
        dz_ref[0:HB // 2, SWA_W:ZQH] = first_ref[...]
        dz_ref[HB // 2:HB, SWA_W:ZQH] = second_ref[...]
        lb_all = _lower_bound(lb_ref)
        gn = on_ref[...]
        low, upp = _blockdiag(True), _blockdiag(False)
        upp_b = upp.astype(BF16)
        low_b = low.astype(BF16)
        row = lax.broadcasted_iota(jnp.int32, (HB, HD), 0)
        chunk_of_row = row // CHUNK
        in_chunk = row % CHUNK
        for p in range(2):
            lbp = lb_all[:, 2 * HD * p:2 * HD * (p + 1)]
            sgp = _sig(zf[p][...])
            fp = lbp + (1.0 - lbp) * sgp
            bp = _chunk_sums(low_b, jnp.log(fp))
            db_pair, dkf_pair = [], []
            for e in range(2):
                h, ls, hs = 2 * p + e, slice(e * HD, (e + 1) * HD), slice((2 * p + e) * HD, (2 * p + e + 1) * HD)
                f = fp[:, ls]
                q = zq[p][:, ls]
                w = _hgrn_local(q, f, 1.0 - f, bp[:, ls])
                iv = zi[p][:, ls].astype(BF16)
                gg = zg[p][:, ls]
                o = o_ref[:, hs]
                dout = dy_ref[:, hs].astype(F32)
                sgg = _sig(gg)
                r = _rstd(o)
                oh = o * r
                dyn = dout * (gg * sgg)
                dz_ref[:, ZGH + h * HD:ZGH + (h + 1) * HD] = (
                    dout * oh * gn * (sgg * (1.0 + gg * (1.0 - sgg)))).astype(BF16)
                don_ref[...] += _rowsum8(dyn * oh)
                do = _norm_bwd(oh, r, dyn * gn).astype(BF16)
                qm, km, kl, qb = (w[n].astype(BF16) for n in ("qm", "km", "kl", "qb"))
                decay = jnp.exp(w["b_last"])
                grads_in = _dot(do, _chunk_stack(qb, chunk_of_row), 0, 0)
                dst = dst_ref[h]
                dstn, dd_rows = [None] * NCH, [None] * NCH
                for c in reversed(range(NCH)):
                    dstn[c] = dst.astype(BF16)
                    dd_rows[c] = jnp.sum(dst * sp_ref[h, c], axis=0, keepdims=True)
                    dst = dst * decay[c * CHUNK:c * CHUNK + 1] + grads_in[:, c * HD:(c + 1) * HD]
                dst_ref[h] = dst
                states = jnp.concatenate([sp_ref[h, c].astype(BF16) for c in range(NCH)], axis=0)
                dstn_all = jnp.concatenate(dstn, axis=0)
                dqb = _dot(_chunk_stack(do, chunk_of_row), states, 1, 0)
                at = jnp.where(upp, _dot(km, qm, 1, 1), 0.0)
                di = _dot(at.astype(BF16), do, 1, 0) + _chunk_pick(_dot(kl, dstn_all, 1, 1), chunk_of_row)
                dz_ref[:, ZIH + h * HD:ZIH + (h + 1) * HD] = di.astype(BF16)
                dkl = _dot(_chunk_stack(iv, chunk_of_row), dstn_all, 1, 0)
                da = jnp.where(low, _dot(do, iv, 1, 1), 0.0).astype(BF16)
                dat = jnp.where(upp, _dot(iv, do, 1, 1), 0.0).astype(BF16)
                dqm = _dot(da, km, 1, 0)
                dkm = _dot(dat, qm, 1, 0)
                b = bp[:, ls]
                e1, e2 = jnp.exp(b - w["b_mid"]), jnp.exp(w["b_mid"] - b)
                e3, e4 = jnp.exp(w["b_last"] - b), jnp.exp(b)
                dqf = dqm * e1 + dqb * e4
                dkf_pair.append(dkm * e2 + dkl * e3)
                t_qm, t_km, t_kl = dqm * w["qm"], dkm * w["km"], dkl * w["kl"]
                db = t_qm - t_km - t_kl + dqb * w["qb"]
                db_mid = jnp.sum((t_km - t_qm).reshape(NCH, CHUNK, HD), axis=1, keepdims=True)
                db_last = jnp.sum(t_kl.reshape(NCH, CHUNK, HD), axis=1, keepdims=True)
                db_last = db_last + jnp.stack(dd_rows, axis=0) * jnp.exp(
                    bp[:, ls].reshape(NCH, CHUNK, HD)[:, CHUNK - 1:CHUNK, :])
                spread = lambda v: jnp.broadcast_to(v, (NCH, CHUNK, HD)).reshape(HB, HD)
                db = (db + jnp.where(in_chunk == CHUNK // 2 - 1, spread(db_mid), 0.0)
                      + jnp.where(in_chunk == CHUNK - 1, spread(db_last), 0.0))
                db_pair.append(db)
                sq = w["sq"]
                dz_ref[:, ZQH + h * HD:ZQH + (h + 1) * HD] = (
                    dqf * (HD ** -0.5) * (sq * (1.0 + q * (1.0 - sq)))).astype(BF16)
            dlogf = _chunk_sums(upp_b, jnp.concatenate(db_pair, axis=1))
            dfv = dlogf / fp - jnp.concatenate(dkf_pair, axis=1)
            dz_ref[:, ZFH + 2 * HD * p:ZFH + 2 * HD * (p + 1)] = (dfv * (1.0 - lbp) * sgp * (1.0 - sgp)).astype(BF16)
            dlb_ref[:, 2 * HD * p:2 * HD * (p + 1)] += _rowsum8(dfv * (1.0 - sgp))

    rev = lambda j: nb - 1 - j
    return pl.pallas_call(
        body, name="hgrn_bwd",
        out_shape=(jax.ShapeDtypeStruct((t, D_IN), BF16), jax.ShapeDtypeStruct((8, HG_W), F32),
                   jax.ShapeDtypeStruct((8, HD), F32)),
        grid=(nb,),
        in_specs=_hgrn_cols(rev) + [pl.BlockSpec((2, HG_W), lambda j: (0, 0)), pl.BlockSpec((1, HD), lambda j: (0, 0)),
                                    pl.BlockSpec((HB, HG_W), lambda j: (rev(j), 0)),
                                    pl.BlockSpec((4, NCH, HD, HD), lambda j: (0, rev(j), 0, 0)),
                                    pl.BlockSpec((HB, HG_W), lambda j: (rev(j), 1)),
                                    pl.BlockSpec((HB, SWA_W), lambda j: (rev(j), 0)),
                                    pl.BlockSpec((HB // 2, 2 * KV_W), lambda j: (rev(j), 0)),
                                    pl.BlockSpec((HB // 2, 2 * KV_W), lambda j: (rev(j), 0))],
        out_specs=(pl.BlockSpec((HB, D_IN), lambda j: (rev(j), 0)), pl.BlockSpec((8, HG_W), lambda j: (0, 0)),
                   pl.BlockSpec((8, HD), lambda j: (0, 0))),
        scratch_shapes=[pltpu.VMEM((4, HD, HD), F32)],
        compiler_params=_params(dimension_semantics=("arbitrary",)),
    )(*[z] * 8, hgrn_lb, onorm, o_save, sprev, dymix, *dza)


XB = 512


def _xattn_fwd(q, k, v, wo, h, g_post, g_pre, t, dep=None):
    tb = min(XB, t)
    deps = [] if dep is None else [dep]

    def body(q_ref, k_ref, v_ref, wo_ref, h_ref, gp_ref, gn_ref, *rest):
        o_ref, y_ref, hn_ref, u_ref = rest[len(deps):]
        for hd in range(XH):
            cols = slice(XD * hd, XD * (hd + 1))
            s = _dot(q_ref[:, cols], k_ref[:, cols], 1, 1) * (XD ** -0.5)
            p = jnp.exp(s - jnp.max(s, axis=-1, keepdims=True))
            l = jnp.sum(p, axis=-1, keepdims=True)
            o_ref[:, cols] = (_dot(p.astype(BF16), v_ref[:, cols], 1, 0) * (1.0 / l)).astype(BF16)
        y, hn, u = _ep_post_pre(_dot(o_ref[...], wo_ref[...], 1, 0), h_ref[...], gp_ref[...], gn_ref[...])
        y_ref[...] = y
        hn_ref[...] = hn
        u_ref[...] = u.astype(BF16)

    row = pl.BlockSpec((tb, D), lambda i: (i, 0))
    whole = lambda a: pl.BlockSpec(a.shape, lambda i: (0,) * a.ndim, pipeline_mode=pl.Buffered(1))
    half = jax.ShapeDtypeStruct((t, D), BF16)
    return pl.pallas_call(
        body, name="xattn_fwd", out_shape=(half, half, jax.ShapeDtypeStruct((t, D), F32), half), grid=(t // tb,),
        in_specs=[row, whole(k), whole(v), whole(wo), row, whole(g_post), whole(g_pre)] + [ANY_SPEC] * len(deps),
        out_specs=(row, row, row, row), compiler_params=_params(),
    )(q, k, v, wo, h, g_post, g_pre, *deps)


def _xattn_bwd(q, k, v, do, wq, wout, dh_out, hn, y, g_post, g_pre, t):
    tb = min(XB, t)

    def body(q_ref, k_ref, v_ref, do_ref, wq_ref, wout_ref, dho_ref, hn_ref, y_ref, gp_ref, gn_ref,
             dq_ref, dk_ref, dv_ref, dh_ref, dyp_ref, dym_ref, dgn_ref, dgp_ref):
        @pl.when(pl.program_id(0) == 0)
        def _():
            dk_ref[...] = jnp.zeros_like(dk_ref)
            dv_ref[...] = jnp.zeros_like(dv_ref)
            dgn_ref[...] = jnp.zeros_like(dgn_ref)
            dgp_ref[...] = jnp.zeros_like(dgp_ref)

        for h in range(XH):
            cols = slice(XD * h, XD * (h + 1))
            qh, kh, vh, doh = q_ref[:, cols], k_ref[:, cols], v_ref[:, cols], do_ref[:, cols]
            s = _dot(qh, kh, 1, 1) * (XD ** -0.5)
            p = jnp.exp(s - jnp.max(s, axis=-1, keepdims=True))
            p = p * (1.0 / jnp.sum(p, axis=-1, keepdims=True))
            dp = _dot(doh, vh, 1, 1)
            ds = (p * (dp - jnp.sum(p * dp, axis=-1, keepdims=True)) * (XD ** -0.5)).astype(BF16)
            dq_ref[:, cols] = _dot(ds, kh, 1, 0).astype(BF16)
            dk_ref[:, cols] += _dot(ds, qh, 0, 0)
            dv_ref[:, cols] += _dot(p.astype(BF16), doh, 0, 0)
        du = _dot(dq_ref[...], wq_ref[...], 1, 1)
        dh, dyp, dgn, dgp = _ep_post_pre_bwd(du, dho_ref[...], hn_ref[...], y_ref[...], gp_ref[...], gn_ref[...])
        dh_ref[...] = dh
        dyp = dyp.astype(BF16)
        dyp_ref[...] = dyp
        dym_ref[...] = _dot(dyp, wout_ref[...], 1, 1).astype(BF16)
        dgn_ref[...] += dgn
        dgp_ref[...] += dgp

    row = pl.BlockSpec((tb, D), lambda i: (i, 0))
    mem = pl.BlockSpec(k.shape, lambda i: (0, 0))
    whole = lambda a: pl.BlockSpec(a.shape, lambda i: (0,) * a.ndim, pipeline_mode=pl.Buffered(1))
    acc = pl.BlockSpec((8, D), lambda i: (0, 0))
    half = jax.ShapeDtypeStruct((t, D), BF16)
    return pl.pallas_call(
        body, name="xattn_bwd",
        out_shape=(half, jax.ShapeDtypeStruct(k.shape, F32), jax.ShapeDtypeStruct(k.shape, F32),
                   jax.ShapeDtypeStruct((t, D), F32), half, half,
                   jax.ShapeDtypeStruct((8, D), F32), jax.ShapeDtypeStruct((8, D), F32)),
        grid=(t // tb,),
        in_specs=[row, whole(k), whole(v), row, whole(wq), whole(wout), row, row, row, whole(g_post), whole(g_pre)],
        out_specs=(row, mem, mem, row, row, row, acc, acc),
        compiler_params=_params(dimension_semantics=("arbitrary",)),
    )(q, k, v, do, wq, wout, dh_out, hn, y, g_post, g_pre)


def _mem_kv(mem, g_mem, wk, wv):
    def body(m_ref, g_ref, wk_ref, wv_ref, mn_ref, k_ref, v_ref):
        m_ = m_ref[...]
        mn = (m_ * _rstd(m_) * g_ref[...]).astype(BF16)
        mn_ref[...] = mn
        k_ref[...] = _dot(mn, wk_ref[...], 1, 0).astype(BF16)
        v_ref[...] = _dot(mn, wv_ref[...], 1, 0).astype(BF16)

    return pl.pallas_call(body, name="mem_kv", out_shape=(jax.ShapeDtypeStruct(mem.shape, BF16),) * 3,
                          compiler_params=_params())(mem, g_mem, wk, wv)


def _mem_kv_bwd(mn, mem, dk, dv, wk, wv, dep=None):
    deps = [] if dep is None else [dep]

    def body(mn_ref, m_ref, dk_ref, dv_ref, wk_ref, wv_ref, *rest):
        gk_ref, gv_ref, dg_ref = rest[len(deps):]
        mn = mn_ref[...]
        dkb, dvb = dk_ref[...].astype(BF16), dv_ref[...].astype(BF16)
        gk_ref[...] = _dot(mn, dkb, 0, 0).astype(BF16)
        gv_ref[...] = _dot(mn, dvb, 0, 0).astype(BF16)
        dmn = _dot(dkb, wk_ref[...], 1, 1) + _dot(dvb, wv_ref[...], 1, 1)
        m_ = m_ref[...]
        dg_ref[...] = _rowsum8(dmn * (m_ * _rstd(m_)))

    vmem = pl.BlockSpec(memory_space=pltpu.VMEM)
    return pl.pallas_call(
        body, name="mem_kv_bwd",
        out_shape=(jax.ShapeDtypeStruct(wk.shape, BF16), jax.ShapeDtypeStruct(wv.shape, BF16),
                   jax.ShapeDtypeStruct((8, D), F32)),
        in_specs=[vmem] * 6 + [ANY_SPEC] * len(deps), out_specs=(vmem,) * 3, compiler_params=_params(),
    )(mn, mem, dk, dv, wk, wv, *deps)


FB = 256


def _ffn_fwd_bwd(u, wgt, wut, wd, h, target, g_last, y_prev, g_post, g_pre, wo, t):
    tb = min(FB, t)

    def body(u_ref, wg_ref, wu_ref, wd_ref, h_ref, t_ref, gl_ref, yp_ref, gp_ref, gn_ref, wo_ref,
             a_ref, dy_ref, dg_ref, dup_ref, dh_ref, dyp_ref, do_ref, sq_ref, dgl_ref, dgn_ref, dgp_ref):
        @pl.when(pl.program_id(0) == 0)
        def _():
            for ref in (sq_ref, dgl_ref, dgn_ref, dgp_ref):
                ref[...] = jnp.zeros_like(ref)

        u_ = u_ref[...]
        g = _dot(u_, wg_ref[...], 1, 1)
        up = _dot(u_, wu_ref[...], 1, 1)
        sg = _sig(g)
        a = (g * sg * up).astype(BF16)
        a_ref[...] = a
        h_ = h_ref[...]
        sq, dh3, dy, dgl = _ep_final_loss(_dot(a, wd_ref[...], 1, 0), h_, t_ref[...], gl_ref[...])
        sq_ref[...] += sq
        dgl_ref[...] += dgl
        dy = dy.astype(BF16)
        dy_ref[...] = dy
        da = _dot(dy, wd_ref[...], 1, 1)
        dup = (da * g * sg).astype(BF16)
        dgate = (da * up * (sg * (1.0 + g * (1.0 - sg)))).astype(BF16)
        dup_ref[...] = dup
        dg_ref[...] = dgate
        du = _dot(dgate, wg_ref[...], 1, 0) + _dot(dup, wu_ref[...], 1, 0)
        dh, dyp, dgn, dgp = _ep_post_pre_bwd(du, dh3, h_, yp_ref[...], gp_ref[...], gn_ref[...])
        dh_ref[...] = dh
        dyp = dyp.astype(BF16)
        dyp_ref[...] = dyp
        do_ref[...] = _dot(dyp, wo_ref[...], 1, 1).astype(BF16)
        dgn_ref[...] += dgn
        dgp_ref[...] += dgp

    row = lambda w: pl.BlockSpec((tb, w), lambda i: (i, 0))
    whole = lambda a: pl.BlockSpec(a.shape, lambda i: (0,) * a.ndim, pipeline_mode=pl.Buffered(1))
    acc = pl.BlockSpec((8, D), lambda i: (0, 0))
    wide, half, sums = (jax.ShapeDtypeStruct((t, D_FF), BF16), jax.ShapeDtypeStruct((t, D), BF16),
                        jax.ShapeDtypeStruct((8, D), F32))
    return pl.pallas_call(
        body, name="ffn_fwd_bwd",
        out_shape=(wide, half, wide, wide, jax.ShapeDtypeStruct((t, D), F32), half, half, sums, sums, sums, sums),
        grid=(t // tb,),
        in_specs=[row(D), whole(wgt), whole(wut), whole(wd), row(D), row(D), whole(g_last), row(D), whole(g_post),
                  whole(g_pre), whole(wo)],
        out_specs=(row(D_FF), row(D), row(D_FF), row(D_FF), row(D), row(D), row(D), acc, acc, acc, acc),
        compiler_params=_params(dimension_semantics=("arbitrary",)),
    )(u, wgt, wut, wd, h, target, g_last, y_prev, g_post, g_pre, wo)


def _local_step(x, mem, target, fetch, sm, emit=None, first_dep=None, milestone=None):
    t = x.shape[0]
    w, gw = {}, {}

    def out(key, g):
        gw[key] = g
        return None if emit is None else emit(key, g)

    def tell(tag, value):
        return None if milestone is None else milestone(tag, value)
    u1 = _prenorm(x, sm["g_mix_pre"], name="prenorm_mix", dep=first_dep)
    w["winT"] = fetch("winT", u1)
    z = _mm_nt(u1, w["winT"], out_dtype=F32, tm=1024, tn=1408, name="mm_z")
    ymix, lse = _swa_fwd(z, sm["sinks"], t)
    ymix, o_h, sprev = _hgrn2_fwd(z, sm["hgrn_lb"], sm["hgrn_onorm"], ymix, t, dep=tell("swa", lse))
    for key in ("wout", "wq", "wk", "wv", "wo"):
        w[key] = fetch(key, ymix)
    y1, h1, u2, qx = _mm_rows([(ymix, w["wout"], False)], [x], [sm["g_mix_post"], sm["g_x_pre"], w["wq"]],
                              _then(_ep_post_pre, 2, False), _EP_POST_PRE_OUTS + [ROW_BF16], tm=512,
                              name="mm_y1_post_qx")
    mn, kx, vx = _mem_kv(mem, sm["g_mem"], w["wk"], w["wv"])
    ox, y2, h2, u3 = _xattn_fwd(qx, kx, vx, w["wo"], h1, sm["g_x_post"], sm["g_ffn_pre"], t, dep=tell("kv", kx))
    for key in ("wgT", "wuT", "wd"):
        w[key] = fetch(key, u3)
    act, dy3, dgate, dup, dh2, dy2, dox, sq, dg_ffn_post, dg_ffn_pre, dg_x_post = _ffn_fwd_bwd(
        u3, w["wgT"], w["wuT"], w["wd"], h2, target, sm["g_ffn_post"], y2, sm["g_x_post"], sm["g_ffn_pre"], w["wo"], t)
    dep = out("wd", *_mm_tn([act], dy3, name="mm_gwd"))
    gwg, gwu = _mm_tn([dgate, dup], u3, name="mm_gwg_gwu", dep=dep)
    out("wgT", gwg)
    dep = out("wuT", gwu)
    out("wo", *_mm_tn([ox], dy2, name="mm_gwo", dep=dep))
    dqx, dkx, dvx, dh1, dy1, dymix, dg_x_pre, dg_mix_post = _xattn_bwd(
        qx, kx, vx, dox, w["wq"], w["wout"], dh2, h1, y1, sm["g_mix_post"], sm["g_x_pre"], t)
    out("wq", *_mm_tn([u2], dqx, name="mm_gwq"))
    gwk, gwv, dg_mem = _mem_kv_bwd(mn, mem, dkx, dvx, w["wk"], w["wv"])
    out("wk", gwk)
    dep = out("wv", gwv)
    dep = out("wout", *_mm_tn([ymix], dy1, name="mm_gwout", dep=dep))
    *dza, dsinks = _swa_bwd(z, sm["sinks"], ymix, lse, dymix, t, dep=dep)
    dz, dlb, donorm = _hgrn2_bwd(z, sm["hgrn_lb"], sm["hgrn_onorm"], o_h, sprev, dymix, dza, t)
    dep = out("winT", *_mm_tn([dz], u1, name="mm_gwin"))
    grad_x, dg_mix_pre = _mm_rows([(dz, w["winT"], False)], [dh1, x], [sm["g_mix_pre"]], _ep_pre_bwd,
                                  _EP_PRE_BWD_OUTS, tm=512, name="mm_du1_pre_bwd", dep=dep)
    parts = dict(g_mix_pre=dg_mix_pre, g_mix_post=dg_mix_post, g_mem=dg_mem, g_x_pre=dg_x_pre,
                 g_x_post=dg_x_post, g_ffn_pre=dg_ffn_pre, g_ffn_post=dg_ffn_post,
                 hgrn_onorm=donorm, hgrn_lb=dlb, sinks=dsinks, sq=sq)
    return grad_x, gw, parts


def _position():
    return lax.axis_index("x"), lax.axis_index("y"), lax.axis_index("c")


def _peer(pos, k):
    x, y, c = pos
    return (1 - x if k & 4 else x, 1 - y if k & 2 else y, 1 - c if k & 1 else c)


def _linear(pos):
    x, y, c = pos
    return 4 * x + 2 * y + c


HBM_SPEC = pl.BlockSpec(memory_space=pltpu.HBM)
SEM_SPEC = pl.BlockSpec(memory_space=pltpu.SEMAPHORE)
DATAFLOW = pltpu.SideEffectType.DATAFLOW_SIDE_EFFECTING
SEND_ORDER = (1, 2, 4, 3, 5, 6, 7)


def _in_hbm(a):
    return pltpu.with_memory_space_constraint(a, pltpu.HBM)


def _prepare_weights(shards, *, name, dep=None):
    n = len(shards)
    deps = [] if dep is None else [dep]

    def body(*refs):
        ins, (outs, lands, sem) = refs[:n], (refs[-2 * n - 1:-n - 1], refs[-n - 1:-1], refs[-1])
        me_lin = _linear(_position())
        copies = []
        for a in range(n):
            r = ins[a].shape[0]
            outs[a][...] = ins[a][...].astype(BF16)
            copies.append(pltpu.make_async_copy(outs[a], lands[a].at[pl.ds(me_lin * r, r), :], sem.at[a]))
            copies[-1].start()
        for cp in copies:
            cp.wait()

    vmem = pl.BlockSpec(memory_space=pltpu.VMEM)
    res = pl.pallas_call(
        body, name=name,
        out_shape=tuple(jax.ShapeDtypeStruct(s.shape, BF16) for s in shards)
        + tuple(jax.ShapeDtypeStruct((N_DEV * s.shape[0], s.shape[1]), BF16) for s in shards),
        in_specs=[vmem] * n + [ANY_SPEC] * len(deps), out_specs=tuple([vmem] * n + [ANY_SPEC] * n),
        scratch_shapes=[pltpu.SemaphoreType.DMA((n,))], compiler_params=_params(),
    )(*shards, *deps)
    return res[:n], res[n:]


def _copies_start(arrays, plan, n, *, name):
    na = len(arrays)

    def body(*refs):
        ins, send_sems, recv_sems = refs[:na], refs[na], refs[na + 1]
        me = _position()
        for j in range(n):
            src, dst, peer, _ = plan(ins, me, j)
            pltpu.make_async_remote_copy(src_ref=src, dst_ref=dst, send_sem=send_sems.at[j], recv_sem=recv_sems.at[j],
                                         device_id=peer, device_id_type=MESH).start()

    return pl.pallas_call(
        body, name=name,
        out_shape=(pltpu.SemaphoreType.DMA((n,)), pltpu.SemaphoreType.DMA((n,)))
        + tuple(pltpu.HBM(a.shape, a.dtype) for a in arrays),
        in_specs=(HBM_SPEC,) * na, out_specs=(SEM_SPEC, SEM_SPEC) + (HBM_SPEC,) * na,
        input_output_aliases={i: 2 + i for i in range(na)},
        compiler_params=pltpu.CompilerParams(has_side_effects=DATAFLOW),
    )(*[_in_hbm(a) for a in arrays])


def _copies_wait(send_sems, recv_sems, arrays, plan, n, after, *, name):
    na = len(arrays)

    def body(*refs):
        ins, send_sems, recv_sems = refs[:na], refs[na], refs[na + 1]
        me = _position()
        for j in range(n):
            src, _, peer, landed = plan(ins, me, j)
            copy = pltpu.make_async_remote_copy(src_ref=src, dst_ref=landed, send_sem=send_sems.at[j],
                                                recv_sem=recv_sems.at[j], device_id=peer, device_id_type=MESH)
            copy.wait_send()
            copy.wait_recv()

    return pl.pallas_call(
        body, name=name, out_shape=tuple(pltpu.HBM(a.shape, a.dtype) for a in arrays),
        in_specs=(HBM_SPEC,) * na + (SEM_SPEC, SEM_SPEC, ANY_SPEC), out_specs=(HBM_SPEC,) * na,
        input_output_aliases={i: i for i in range(na)},
        compiler_params=pltpu.CompilerParams(has_side_effects=DATAFLOW),
    )(*arrays, send_sems, recv_sems, after)


SAME_CORE = (2, 4, 6)


class _TwoLevelGather:
    def __init__(self, shards, lands, *, name):
        n = self.n = len(shards)
        self.name = name
        first_peers = (1,) + SAME_CORE

        def rows(ref, pos):
            r = ref.shape[0] // N_DEV
            return ref.at[pl.ds(_linear(pos) * r, r), :]

        def first(refs, me, j):
            a, peer = j // 4, _peer(me, first_peers[j % 4])
            return refs[a], rows(refs[n + a], me), peer, rows(refs[n + a], peer)

        def second(refs, me, j):
            a, sibling = j // 3, _peer(me, 1)
            mine = rows(refs[a], _peer(me, SAME_CORE[j % 3]))
            return mine, mine, sibling, rows(refs[a], _peer(sibling, SAME_CORE[j % 3]))

        self._first, self._second = first, second
        self._flight = _copies_start(list(shards) + list(lands), first, 4 * n, name=name + "_send")
        self.dep = self._flight[2]

    def pass_on(self, after):
        send1, recv1, *arrays = self._flight
        arrays = _copies_wait(send1, recv1, arrays, self._first, 4 * self.n, after, name=self.name + "_recv")
        self._flight = _copies_start(list(arrays[self.n:]), self._second, 3 * self.n, name=self.name + "_pass")
        return self._flight[2]

    def finish(self, after):
        send2, recv2, *lands = self._flight
        return _copies_wait(send2, recv2, lands, self._second, 3 * self.n, after, name=self.name + "_pass_recv")


def _exchange_start(gs, *, name):
    n = len(gs)
    rows = [g.shape[0] // N_DEV for g in gs]
    lands = [lax.empty((N_DEV - 1, r, g.shape[1]), g.dtype) for g, r in zip(gs, rows)]

    def body(*refs):
        g_refs, land_refs = refs[:n], refs[n:2 * n]
        send_sems, recv_sems = refs[2 * n:3 * n], refs[3 * n:4 * n]
        me = _position()
        for a in range(n):
            for k in SEND_ORDER:
                peer = _peer(me, k)
                pltpu.make_async_remote_copy(
                    src_ref=g_refs[a].at[pl.ds(_linear(peer) * rows[a], rows[a]), :],
                    dst_ref=land_refs[a].at[k - 1],
                    send_sem=send_sems[a].at[k - 1], recv_sem=recv_sems[a].at[k - 1],
                    device_id=peer, device_id_type=MESH).start()

    res = pl.pallas_call(
        body, name=name,
        out_shape=tuple(pltpu.SemaphoreType.DMA((N_DEV - 1,)) for _ in range(2 * n))
        + tuple(pltpu.HBM(a.shape, a.dtype) for a in gs + lands),
        in_specs=(HBM_SPEC,) * (2 * n), out_specs=(SEM_SPEC,) * (2 * n) + (HBM_SPEC,) * (2 * n),
        input_output_aliases={i: 2 * n + i for i in range(2 * n)},
        compiler_params=pltpu.CompilerParams(has_side_effects=DATAFLOW),
    )(*[_in_hbm(a) for a in gs + lands])
    return [(res[a], res[n + a], res[2 * n + a], res[3 * n + a]) for a in range(n)]


def _exchange_wait(send_sems, recv_sems, g_thru, land_thru, after, *, name):
    r = land_thru.shape[1]

    def body(g_ref, land_ref, send_sems, recv_sems, after_ref, g_dead, got_ref):
        del after_ref, g_dead, got_ref
        me = _position()
        for k in SEND_ORDER:
            peer = _peer(me, k)
            copy = pltpu.make_async_remote_copy(
                src_ref=g_ref.at[pl.ds(_linear(peer) * r, r), :], dst_ref=land_ref.at[k - 1],
                send_sem=send_sems.at[k - 1], recv_sem=recv_sems.at[k - 1],
                device_id=peer, device_id_type=MESH)
            copy.wait_send()
            copy.wait_recv()

    return pl.pallas_call(
        body, name=name,
        out_shape=(pltpu.HBM(g_thru.shape, g_thru.dtype), pltpu.HBM(land_thru.shape, land_thru.dtype)),
        in_specs=(HBM_SPEC, HBM_SPEC, SEM_SPEC, SEM_SPEC, pl.BlockSpec(memory_space=pl.ANY)),
        out_specs=(HBM_SPEC, HBM_SPEC), input_output_aliases={0: 0, 1: 1},
        compiler_params=pltpu.CompilerParams(has_side_effects=DATAFLOW),
    )(g_thru, land_thru, send_sems, recv_sems, after)


ADAMW_TILE_ROWS = 256


def _adamw_math(w, g, m, v):
    m = B1 * m + (1.0 - B1) * g
    v = B2 * v + (1.0 - B2) * (g * g)
    delta = -LR * ((m / C1) / (jnp.sqrt(v / C2) + AEPS) + WD * w)
    return delta, m, v


def _sum_adamw(items, *, name):
    n = len(items)
    r, d = items[0][2].shape
    assert all(it[2].shape == (r, d) for it in items)
    rc = r // 2 if r > ADAMW_TILE_ROWS else r
    tiles = [(a, r0) for a in range(n) for r0 in range(0, r, rc)]
    n_in, n_out = 5, 4

    def body(*refs):
        ins, outs = refs[:n_in * n], refs[n_in * n:(n_in + n_out) * n]
        land_v, own_v, f32_v, sems = refs[(n_in + n_out) * n:]
        me_lin = _linear(_position())

        def loads(j):
            a, r0 = tiles[j]
            g_all, land, w, m, v = ins[n_in * a:n_in * a + n_in]
            rows = pl.ds(r0, rc)
            pairs = [(land.at[:, rows, :], land_v.at[j]), (g_all.at[pl.ds(me_lin * r + r0, rc), :], own_v.at[j]),
                     (w.at[rows, :], f32_v.at[j, 0]), (m.at[rows, :], f32_v.at[j, 1]), (v.at[rows, :], f32_v.at[j, 2])]
            return [pltpu.make_async_copy(src, dst, sems.at[j, i]) for i, (src, dst) in enumerate(pairs)]

        def stores(j):
            a, r0 = tiles[j]
            return [pltpu.make_async_copy(f32_v.at[j, 3 + i], outs[n_out * a + i].at[pl.ds(r0, rc), :],
                                          sems.at[j, n_in + i]) for i in range(n_out)]

        for j in range(len(tiles)):
            for cp in loads(j):
                cp.start()
        for j in range(len(tiles)):
            for cp in loads(j):
                cp.wait()
            g = land_v[j, 0].astype(F32)
            for s in range(1, N_DEV - 1):
                g = g + land_v[j, s].astype(F32)
            g = own_v[j].astype(F32) + g
            f32_v[j, 3] = g
            f32_v[j, 4], f32_v[j, 5], f32_v[j, 6] = _adamw_math(f32_v[j, 0], g, f32_v[j, 1], f32_v[j, 2])
            for cp in stores(j):
                cp.start()
        for j in range(len(tiles)):
            for cp in stores(j):
                cp.wait()

    nt = len(tiles)
    res = pl.pallas_call(
        body, name=name,
        out_shape=tuple(jax.ShapeDtypeStruct((r, d), F32) for _ in range(n_out * n)),
        in_specs=[ANY_SPEC] * (n_in * n), out_specs=(ANY_SPEC,) * (n_out * n),
        scratch_shapes=[pltpu.VMEM((nt, N_DEV - 1, rc, d), BF16), pltpu.VMEM((nt, rc, d), BF16),
                        pltpu.VMEM((nt, 3 + n_out, rc, d), F32), pltpu.SemaphoreType.DMA((nt, n_in + n_out))],
        compiler_params=_params(),
    )(*[a for it in items for a in it])
    return [res[n_out * a:n_out * a + n_out] for a in range(n)]


SMALL = ("g_mix_pre", "g_mix_post", "g_mem", "g_x_pre", "g_x_post", "g_ffn_pre", "g_ffn_post",
         "hgrn_onorm", "hgrn_lb", "sinks")
SMALL_W = dict(hgrn_onorm=HD, hgrn_lb=HG_W, sinks=8)
SQ_ROW = len(SMALL)
PACK_ROWS = 16


def _small_pack(parts):
    ns = len(SMALL)

    def body(*refs):
        part, mine, slots, sem = refs[:ns + 1], refs[ns + 1], refs[ns + 2], refs[ns + 3]
        mine[...] = jnp.zeros((PACK_ROWS, D), F32)
        for r, name in enumerate(SMALL):
            wd = SMALL_W.get(name, D)
            mine[r:r + 1, 0:wd] = jnp.sum(part[r][...], axis=0, keepdims=True)[:, 0:wd]
        sq = jnp.sum(part[ns][...]) * (0.5 / D)
        mine[SQ_ROW:SQ_ROW + 1, :] = jnp.full((1, D), sq, F32)
        own = pltpu.make_async_copy(mine, slots.at[_linear(_position())], sem)
        own.start()
        own.wait()

    vmem = pl.BlockSpec(memory_space=pltpu.VMEM)
    return pl.pallas_call(
        body, name="small_pack",
        out_shape=(jax.ShapeDtypeStruct((PACK_ROWS, D), F32), jax.ShapeDtypeStruct((N_DEV, PACK_ROWS, D), F32)),
        in_specs=[vmem] * (ns + 1), out_specs=(vmem, ANY_SPEC),
        scratch_shapes=[pltpu.SemaphoreType.DMA(())], compiler_params=_params(),
    )(*[parts[n] for n in SMALL], parts["sq"])


def _small_exchange(mine, slots):
    def plan(refs, me, j):
        peer = _peer(me, j + 1)
        return refs[0], refs[1].at[_linear(me)], peer, refs[1].at[_linear(peer)]

    send, recv, mine1, slots1 = _copies_start([mine, slots], plan, N_DEV - 1, name="small_send")
    return lambda after: _copies_wait(send, recv, [mine1, slots1], plan, N_DEV - 1, after, name="small_recv")[1]


def _small_update(slots, sm, m_sm, v_sm):
    ns = len(SMALL)

    def body(*refs):
        tot = refs[0][0]
        for s in range(1, N_DEV):
            tot = tot + refs[0][s]
        w_refs, m_refs, v_refs = refs[1:ns + 1], refs[ns + 1:2 * ns + 1], refs[2 * ns + 1:3 * ns + 1]
        outs = refs[3 * ns + 1:]
        loss_ref = outs[0]
        g_out, d_out = outs[1:ns + 1], outs[ns + 1:2 * ns + 1]
        nm_out, nv_out = outs[2 * ns + 1:3 * ns + 1], outs[3 * ns + 1:4 * ns + 1]
        loss_ref[...] = tot[SQ_ROW:SQ_ROW + 1, 0:1]
        for r, name in enumerate(SMALL):
            wd = SMALL_W.get(name, D)
            g = tot[r:r + 1, 0:wd]
            w = w_refs[r][...]
            if name == "hgrn_lb":
                mx = jnp.maximum(w[0:1], w[1:2])
                e0, e1 = jnp.exp(w[0:1] - mx), jnp.exp(w[1:2] - mx)
                lb0 = e0 / (e0 + e1)
                g0 = g * lb0 * (1.0 - lb0)
                for i, gi in enumerate((g0, -g0)):
                    d, nm, nv = _adamw_math(w[i:i + 1], gi, m_refs[r][i:i + 1, :], v_refs[r][i:i + 1, :])
                    g_out[r][i:i + 1, :] = gi
                    d_out[r][i:i + 1, :], nm_out[r][i:i + 1, :], nv_out[r][i:i + 1, :] = d, nm, nv
            else:
                d, nm, nv = _adamw_math(w, g, m_refs[r][...], v_refs[r][...])
                g_out[r][...] = g
                d_out[r][...], nm_out[r][...], nv_out[r][...] = d, nm, nv

    shapes = [jax.ShapeDtypeStruct(sm[n].shape, F32) for n in SMALL]
    res = pl.pallas_call(
        body, name="small_update", out_shape=tuple([jax.ShapeDtypeStruct((1, 1), F32)] + shapes * 4),
        compiler_params=_params(),
    )(slots, *[sm[n] for n in SMALL], *[m_sm[n] for n in SMALL], *[v_sm[n] for n in SMALL])
    groups = [dict(zip(SMALL, res[1 + i * ns:1 + (i + 1) * ns])) for i in range(4)]
    return res[0], groups[0], groups[1], groups[2], groups[3]


BIG = ("w_in", "w_gate", "w_up", "w_down", "w_out", "wq_x", "wk_x", "wv_x", "wo_x")
BIG_KEY = dict(w_in="winT", w_gate="wgT", w_up="wuT", w_down="wd", w_out="wout", wq_x="wq", wk_x="wk",
               wv_x="wv", wo_x="wo")
TRANSPOSED = ("w_in", "w_gate", "w_up")
WEIGHTS = ("w_in", "sinks", "hgrn_lb", "hgrn_onorm", "w_out", "g_mix_pre", "g_mix_post", "g_mem", "g_x_pre",
           "g_x_post", "wq_x", "wk_x", "wv_x", "wo_x", "g_ffn_pre", "g_ffn_post", "w_gate", "w_up", "w_down")


def kernel(x, mem, w_in, sinks, hgrn_lb, hgrn_onorm, w_out, g_mix_pre, g_mix_post, g_mem, g_x_pre, g_x_post, wq_x, wk_x, wv_x, wo_x, g_ffn_pre, g_ffn_post, w_gate, w_up, w_down, loss_target, m_w_in, m_sinks, m_hgrn_lb, m_hgrn_onorm, m_w_out, m_g_mix_pre, m_g_mix_post, m_g_mem, m_g_x_pre, m_g_x_post, m_wq_x, m_wk_x, m_wv_x, m_wo_x, m_g_ffn_pre, m_g_ffn_post, m_w_gate, m_w_up, m_w_down, v_w_in, v_sinks, v_hgrn_lb, v_hgrn_onorm, v_w_out, v_g_mix_pre, v_g_mix_post, v_g_mem, v_g_x_pre, v_g_x_post, v_wq_x, v_wk_x, v_wv_x, v_wo_x, v_g_ffn_pre, v_g_ffn_post, v_w_gate, v_w_up, v_w_down):
    given = dict(locals())
    wts = {n: given[n] for n in WEIGHTS}
    ms = {n: given["m_" + n] for n in WEIGHTS}
    vs = {n: given["v_" + n] for n in WEIGHTS}

    def mat(a, name):
        a = a[0]
        return a.T if name in TRANSPOSED else a

    groups = (("w_in",), ("w_out", "wq_x", "wk_x", "wv_x", "wo_x"), ("w_gate", "w_up", "w_down"))
    gathers = []
    first_dep = None
    for tag, group in zip(("w_in", "w_attn", "w_ffn"), groups):
        shards, lands = _prepare_weights([mat(wts[n], n) for n in group], name="prepare_" + tag, dep=first_dep)
        gathers.append(_TwoLevelGather(shards, lands, name=tag))
        first_dep = gathers[-1].dep
    name_of = {k: n for n, k in BIG_KEY.items()}
    gathered = {}

    def milestone(tag, value):
        return gathers[{"swa": 1, "kv": 2}[tag]].pass_on(value)

    def fetch(key, after):
        name = name_of[key]
        if name not in gathered:
            g = [i for i, group in enumerate(groups) if name in group][0]
            if g == 0:
                gathers[0].pass_on(after)
            gathered.update(zip(groups[g], gathers[g].finish(after)))
        return gathered[name]

    sm = {n: wts[n] for n in SMALL}
    started, held = {}, {}
    send_with = {k: group for group in (("wgT", "wuT"), ("wo", "wq", "wk", "wv")) for k in group}

    def emit(key, g):
        held[key] = g
        group = send_with.get(key, (key,))
        if key != group[-1]:
            return None
        flights = _exchange_start([held[k] for k in group], name="grad_send_" + name_of[group[0]])
        started.update({name_of[k]: f for k, f in zip(group, flights)})
        return flights[-1][2]

    grad_x, _, parts = _local_step(x[0], mem[0], loss_target[0], fetch, sm, emit, first_dep=first_dep, milestone=milestone)
    small_finish = _small_exchange(*_small_pack(parts))
    grads, deltas, new_m, new_v = {}, {}, {}, {}
    after = grad_x
    for group in (("w_down",), ("w_gate", "w_up"), ("wo_x", "wq_x", "wk_x", "wv_x", "w_out"), ("w_in",)):
        items = []
        for n in group:
            g_all, land = _exchange_wait(*started[n], after, name="grad_recv_" + n)
            items.append((g_all, land, mat(wts[n], n), mat(ms[n], n), mat(vs[n], n)))
            after = land
        for n, res in zip(group, _sum_adamw(items, name="adamw_" + group[0])):
            after = res[1]
            if n in TRANSPOSED:
                res = [a.T for a in res]
            grads[n], deltas[n], new_m[n], new_v[n] = [a[None] for a in res]
    loss, g_s, d_s, m_s, v_s = _small_update(small_finish(after), sm, {n: ms[n] for n in SMALL},
                                             {n: vs[n] for n in SMALL})
    grads.update(g_s), deltas.update(d_s), new_m.update(m_s), new_v.update(v_s)
    return (loss[0, 0], grad_x[None], *[grads[n] for n in WEIGHTS], *[deltas[n] for n in WEIGHTS],
            *[new_m[n] for n in WEIGHTS], *[new_v[n] for n in WEIGHTS])
```

```python
import functools

import jax
import jax.numpy as jnp
from jax import lax
from jax.experimental import pallas as pl
from jax.experimental.pallas import tpu as pltpu

F32 = jnp.float32
BF16 = jnp.bfloat16

D = 1024
D_IN = 2816
D_FF = 2816
CHUNK = 64
SWA_W = 512
KV_W = 128
HG_W = 512
HD = 128
ZQH, ZFH, ZIH, ZGH = 768, 1280, 1792, 2304
XH, XD = 4, 256
EPS = 1e-6
NEG = -1e30
N_DEV = 8
MESH = pl.DeviceIdType.MESH

LR, B1, B2, AEPS, WD, STEP = 0.001, 0.9, 0.999, 1e-08, 0.01, 10
C1 = 1.0 - B1 ** STEP
C2 = 1.0 - B2 ** STEP

VMEM_LIMIT = 56 * 1024 * 1024


def _params(**kw):
    return pltpu.CompilerParams(vmem_limit_bytes=VMEM_LIMIT, **kw)


def _sig(x):
    return 1.0 / (1.0 + jnp.exp(-x))


def _rowsum8(x):
    r, w = x.shape
    return jnp.sum(x.reshape(r // 8, 8, w), axis=0)


def _dot(a, b, ca, cb, precision=None):
    return lax.dot_general(a, b, (((ca,), (cb,)), ((), ())), preferred_element_type=F32,
                           precision=precision)


ANY_SPEC = pl.BlockSpec(memory_space=pl.ANY)


def _mm_nt(a, b, *, out_dtype, tm, tn, name):
    (m, k), n = a.shape, b.shape[0]
    tm, tn = min(tm, m), min(tn, n)
    assert a.dtype == BF16 and b.dtype == BF16 and m % tm == 0 and n % tn == 0, (name, m, n, tm, tn)

    def body(a_ref, b_ref, o_ref):
        o_ref[...] = _dot(a_ref[...], b_ref[...], 1, 1).astype(out_dtype)

    return pl.pallas_call(
        body, name=name, out_shape=jax.ShapeDtypeStruct((m, n), out_dtype), grid=(n // tn, m // tm),
        in_specs=[pl.BlockSpec((tm, k), lambda j, i: (i, 0)), pl.BlockSpec((tn, k), lambda j, i: (j, 0))],
        out_specs=pl.BlockSpec((tm, tn), lambda j, i: (i, j)),
        compiler_params=_params(dimension_semantics=("parallel", "parallel")),
    )(a, b)


TN_FIRST = 256
TN_REST = 1152
TN_SLICES = 4


def _mm_tn(a_list, b_list, *, name, dep=None):
    na, nbd, (k, m), n = len(a_list), len(b_list), a_list[0].shape, b_list[0].shape[1]
    assert nbd in (1, na) and all(b.dtype == BF16 and b.shape == (k, n) for b in b_list)
    assert all(a.dtype == BF16 and a.shape == (k, m) for a in a_list)
    nbv = min(2, nbd)
    widths = [TN_FIRST, TN_FIRST]
    while sum(widths) < m:
        widths.append(min(TN_REST, m - sum(widths)))
    starts = [sum(widths[:i]) for i in range(len(widths))]
    assert sum(widths) == m
    per = len(widths)
    nb = na * per
    ks = k // TN_SLICES
    ahead = 2
    assert ahead < per
    deps = [] if dep is None else [dep]

    def body(*refs):
        a_hbm, b_hbm, rest = refs[:na], refs[na:na + nbd], refs[na + nbd + len(deps):]
        o_hbm, b_v, rest = rest[:na], rest[na:na + nbv], rest[na + nbv:]
        a_v, o_v, sems = rest[:per], rest[per:-1], rest[-1]
        sliced = []
        for c in range(TN_SLICES):
            rows = pl.ds(c * ks, ks)
            sliced.append((pltpu.make_async_copy(b_hbm[0].at[rows, :], b_v[0].at[rows, :], sems.at[2 * c]),
                           pltpu.make_async_copy(a_hbm[0].at[rows, pl.ds(0, widths[0])], a_v[0].at[rows, :],
                                                 sems.at[2 * c + 1])))
        base = 2 * TN_SLICES - 1
        cols = [pl.ds(starts[i % per], widths[i % per]) for i in range(nb)]
        loads = [None] + [pltpu.make_async_copy(a_hbm[i // per].at[:, cols[i]], a_v[i % per], sems.at[base + i])
                          for i in range(1, nb)]
        stores = [pltpu.make_async_copy(o_v[i % per], o_hbm[i // per].at[cols[i], :], sems.at[base + nb + i])
                  for i in range(nb)]
        next_b = [None] + [pltpu.make_async_copy(b_hbm[j], b_v[j % nbv], sems.at[base + 2 * nb + j])
                           for j in range(1, nbd)]
        for pair in sliced:
            for cp in pair:
                cp.start()
        for i in range(1, 1 + ahead):
            loads[i].start()
        if nbd > 1:
            next_b[1].start()
        acc = None
        for c, pair in enumerate(sliced):
            for cp in pair:
                cp.wait()
            p = _dot(a_v[0][c * ks:(c + 1) * ks, :], b_v[0][c * ks:(c + 1) * ks, :], 0, 0)
            acc = p if acc is None else acc + p
        o_v[0][...] = acc.astype(BF16)
        stores[0].start()
        for i in range(1, nb):
            j = (i // per) % nbd
            if nbd > 1 and i % per == 0:
                next_b[j].wait()
                if j + 1 < nbd:
                    next_b[j + 1].start()
            loads[i].wait()
            if i + ahead < nb:
                loads[i + ahead].start()
            if i >= per:
                stores[i - per].wait()
            o_v[i % per][...] = _dot(a_v[i % per][...], b_v[j % nbv][...], 0, 0).astype(BF16)
            stores[i].start()
        for cp in stores[nb - per:]:
            cp.wait()

    return pl.pallas_call(
        body, name=name, out_shape=tuple(jax.ShapeDtypeStruct((m, n), BF16) for _ in a_list),
        in_specs=[ANY_SPEC] * (na + nbd + len(deps)), out_specs=(ANY_SPEC,) * na,
        scratch_shapes=[pltpu.VMEM((k, n), BF16)] * nbv + [pltpu.VMEM((k, cw), BF16) for cw in widths]
        + [pltpu.VMEM((cw, n), BF16) for cw in widths]
        + [pltpu.SemaphoreType.DMA((2 * TN_SLICES - 1 + 2 * nb + nbd,))],
        compiler_params=_params(),
    )(*a_list, *b_list, *deps)


def _mm_rows(prods, rows_in, vecs_in, epilogue, outs, *, tm, name, dep=None):
    m = prods[0][0].shape[0]
    n = prods[0][1].shape[0] if prods[0][2] else prods[0][1].shape[1]
    tm = min(tm, m)
    assert m % tm == 0
    deps = [] if dep is None else [dep]
    n_p, n_r, n_v = len(prods), len(rows_in), len(vecs_in)

    def body(*refs):
        ab = refs[:2 * n_p]
        row_refs = refs[2 * n_p:2 * n_p + n_r]
        vec_refs = refs[2 * n_p + n_r:2 * n_p + n_r + n_v]
        out_refs = refs[2 * n_p + n_r + n_v + len(deps):]
        p = None
        for j, (_, _, tb) in enumerate(prods):
            t = _dot(ab[2 * j][...].astype(BF16), ab[2 * j + 1][...], 1, 1 if tb else 0)
            p = t if p is None else p + t
        vals = epilogue(p, *[r[...] for r in row_refs], *[v[...] for v in vec_refs])
        for (dtype, kind), o_ref, val in zip(outs, out_refs, vals):
            if kind == "row":
                o_ref[...] = val.astype(dtype)
            else:
                @pl.when(pl.program_id(0) == 0)
                def _(o_ref=o_ref):
                    o_ref[...] = jnp.zeros_like(o_ref)

                o_ref[...] += val

    row = lambda w: pl.BlockSpec((tm, w), lambda i: (i, 0))
    whole = lambda a: pl.BlockSpec(a.shape, lambda i: (0,) * a.ndim, pipeline_mode=pl.Buffered(1))
    in_specs, args = [], []
    for a, b, _ in prods:
        in_specs += [row(a.shape[1]), whole(b)]
        args += [a, b]
    in_specs += [row(r.shape[1]) for r in rows_in] + [whole(v) for v in vecs_in] + [ANY_SPEC] * len(deps)
    return pl.pallas_call(
        body, name=name,
        out_shape=tuple(jax.ShapeDtypeStruct((m, n) if kind == "row" else (8, n), dtype) for dtype, kind in outs),
        grid=(m // tm,), in_specs=in_specs,
        out_specs=tuple(row(n) if kind == "row" else pl.BlockSpec((8, n), lambda i: (0, 0)) for _, kind in outs),
        compiler_params=_params(dimension_semantics=("arbitrary",)),
    )(*args, *rows_in, *vecs_in, *deps)


def _rstd(x):
    return lax.rsqrt(jnp.mean(x * x, axis=-1, keepdims=True) + EPS)


def _norm_bwd(xh, r, t):
    return r * (t - xh * jnp.mean(xh * t, axis=-1, keepdims=True))


ROW_F32, ROW_BF16, SUM_F32 = (F32, "row"), (BF16, "row"), (F32, "sum")


def _then(epilogue, index, tb):
    def run(p, *args):
        vals = epilogue(p, *args[:-1])
        return (*vals, _dot(vals[index].astype(BF16), args[-1], 1, 1 if tb else 0))

    return run


def _ep_post_pre(p, h, g_post, g_pre):
    y = p.astype(BF16)
    yf = y.astype(F32)
    hn = h + yf * _rstd(yf) * g_post
    return y, hn, hn * _rstd(hn) * g_pre


_EP_POST_PRE_OUTS = [ROW_BF16, ROW_F32, ROW_BF16]


def _ep_final_loss(y, h, target, g_post):
    r = _rstd(y)
    yh = y * r
    err = h + yh * g_post - target
    dh = err * (1.0 / D)
    return _rowsum8(err * err), dh, _norm_bwd(yh, r, dh * g_post), _rowsum8(dh * yh)


def _ep_post_pre_bwd(du, dh_out, hn, y, g_post, g_pre):
    r2 = _rstd(hn)
    xh = hn * r2
    dh = dh_out + _norm_bwd(xh, r2, du * g_pre)
    yf = y.astype(F32)
    r1 = _rstd(yf)
    yh = yf * r1
    return dh, _norm_bwd(yh, r1, dh * g_post), _rowsum8(du * xh), _rowsum8(dh * yh)


_EP_POST_PRE_BWD_OUTS = [ROW_F32, ROW_BF16, SUM_F32, SUM_F32]


def _ep_pre_bwd(du, dh_out, x, g):
    r = _rstd(x)
    xh = x * r
    return dh_out + _norm_bwd(xh, r, du * g), _rowsum8(du * xh)


_EP_PRE_BWD_OUTS = [ROW_F32, SUM_F32]


def _prenorm(x, g, *, name, dep=None):
    t, d = x.shape
    tb = min(512, t)
    deps = [] if dep is None else [dep]

    def body(x_ref, g_ref, *rest):
        xf = x_ref[...]
        rest[-1][...] = (xf * _rstd(xf) * g_ref[...]).astype(BF16)

    return pl.pallas_call(
        body, name=name, out_shape=jax.ShapeDtypeStruct((t, d), BF16), grid=(t // tb,),
        in_specs=[pl.BlockSpec((tb, d), lambda i: (i, 0)), pl.BlockSpec((1, d), lambda i: (0, 0))]
        + [ANY_SPEC] * len(deps),
        out_specs=pl.BlockSpec((tb, d), lambda i: (i, 0)), compiler_params=_params(),
    )(x, g, *deps)


QB = 256


def _half_mask(shape, e):
    lane = lax.broadcasted_iota(jnp.int32, shape, len(shape) - 1)
    return (lane // 64) == e


def _place(kv):
    sw = pltpu.roll(kv, 64, 1)
    m0 = _half_mask(kv.shape, 0)
    return [[jnp.where(m0, kv, 0.0).astype(BF16), jnp.where(m0, 0.0, sw).astype(BF16)],
            [jnp.where(m0, sw, 0.0).astype(BF16), jnp.where(m0, 0.0, kv).astype(BF16)]]


SQ = 128
SK = 256


def _swa_valid(i, sb):
    qc = lax.broadcasted_iota(jnp.int32, (SQ, SK), 0) // CHUNK
    kc = lax.broadcasted_iota(jnp.int32, (SQ, SK), 1) // CHUNK - 2
    return (kc <= qc) & (qc <= kc + 2) & (4 * i + 2 * sb + kc >= 0)


def _swa_fwd(z, sinks, t, dep=None):
    nb = t // QB
    deps = [] if dep is None else [dep]

    def body(s_ref, q_ref, kp_ref, kc_ref, vp_ref, vc_ref, *rest):
        o_ref, lse_ref = rest[-2:]
        i = pl.program_id(0)
        kpl = _place(jnp.concatenate([kp_ref[...], kc_ref[...]], axis=0))
        vpl = _place(jnp.concatenate([vp_ref[...], vc_ref[...]], axis=0))
        lane = lax.broadcasted_iota(jnp.int32, (SQ, 128), 1)
        for sb in range(QB // SQ):
            rows, keys = slice(SQ * sb, SQ * (sb + 1)), slice(SQ * sb, SQ * sb + SK)
            valid = _swa_valid(i, sb)
            lse_out = jnp.zeros((SQ, 128), F32)
            for j in range(4):
                qp = q_ref[rows, 128 * j:128 * (j + 1)].astype(BF16)
                acc = jnp.zeros((SQ, 128), F32)
                for e in range(2):
                    h = 2 * j + e
                    kvh = h // 4
                    qm = jnp.where(_half_mask(qp.shape, e), qp, jnp.zeros_like(qp))
                    s = _dot(qm, kpl[kvh][e][keys], 1, 1) * 0.125
                    s = jnp.where(valid, s, NEG)
                    sink = s_ref[0, h]
                    m = jnp.maximum(jnp.max(s, axis=-1, keepdims=True), sink)
                    p = jnp.exp(s - m)
                    l = jnp.sum(p, axis=-1, keepdims=True) + jnp.exp(sink - m)
                    acc = acc + _dot(p.astype(BF16), vpl[kvh][e][keys], 1, 0) * (1.0 / l)
                    lse_out = jnp.where(lane == h, m + jnp.log(l), lse_out)
                o_ref[rows, 128 * j:128 * (j + 1)] = acc.astype(BF16)
            lse_ref[rows, :] = lse_out

    prev = lambda c: pl.BlockSpec((128, 128), lambda i: (jnp.maximum(2 * i - 1, 0), c))
    cur = lambda c: pl.BlockSpec((QB, 128), lambda i: (i, c))
    return pl.pallas_call(
        body, name="swa_fwd",
        out_shape=(jax.ShapeDtypeStruct((t, D), BF16), jax.ShapeDtypeStruct((t, 128), F32)),
        grid=(nb,),
        in_specs=[pl.BlockSpec(memory_space=pltpu.SMEM),
                  pl.BlockSpec((QB, SWA_W), lambda i: (i, 0)), prev(4), cur(4), prev(5), cur(5)]
        + [ANY_SPEC] * len(deps),
        out_specs=(pl.BlockSpec((QB, SWA_W), lambda i: (i, 0)), pl.BlockSpec((QB, 128), lambda i: (i, 0))),
        compiler_params=_params(),
    )(sinks, z, z, z, z, z, *deps)


def _swa_bwd(z, sinks, ymix, lse, dymix, t, dep=None):
    nb = t // QB
    deps = [] if dep is None else [dep]

    def body(s_ref, q_ref, kp_ref, kc_ref, vp_ref, vc_ref, o_ref, do_ref, l_ref, *rest):
        dq_ref, first_ref, second_ref, ds_ref, carry_ref = rest[len(deps):]
        i = pl.program_id(0)
        live = i < nb

        @pl.when(i == 0)
        def _():
            ds_ref[...] = jnp.zeros_like(ds_ref)
            carry_ref[...] = jnp.zeros_like(carry_ref)

        lane = lax.broadcasted_iota(jnp.int32, (8, 128), 1)
        kpl = _place(jnp.concatenate([kp_ref[...], kc_ref[...]], axis=0))
        vpl = _place(jnp.concatenate([vp_ref[...], vc_ref[...]], axis=0))
        nk = QB + 128
        qc = lax.broadcasted_iota(jnp.int32, (QB, nk), 0) // CHUNK
        kc = lax.broadcasted_iota(jnp.int32, (QB, nk), 1) // CHUNK - 2
        valid = (kc <= qc) & (qc <= kc + 2) & (4 * i + kc >= 0) & live
        lse_c = l_ref[...]
        dsink = jnp.zeros((8, 128), F32)
        dk_acc = [[jnp.zeros((128, nk), F32) for _ in range(2)] for _ in range(2)]
        dv_acc = [[jnp.zeros((128, nk), F32) for _ in range(2)] for _ in range(2)]
        dq = []
        for j in range(4):
            cols = slice(128 * j, 128 * (j + 1))
            qp = q_ref[:, cols].astype(BF16)
            dop = do_ref[:, cols]
            prod = dop.astype(F32) * o_ref[:, cols].astype(F32)
            acc = jnp.zeros((QB, 128), F32)
            for e in range(2):
                h = 2 * j + e
                kvh = h // 4
                hm = _half_mask(qp.shape, e)
                qm = jnp.where(hm, qp, jnp.zeros_like(qp))
                dom = jnp.where(hm, dop, jnp.zeros_like(dop))
                dd = jnp.sum(jnp.where(hm, prod, 0.0), axis=-1, keepdims=True)
                lse_h = lse_c[:, h:h + 1]
                s = _dot(qm, kpl[kvh][e], 1, 1) * 0.125
                p = jnp.where(valid, jnp.exp(s - lse_h), 0.0)
                dp = _dot(dom, vpl[kvh][e], 1, 1)
                ds = (p * (dp - dd) * 0.125).astype(BF16)
                acc = acc + _dot(ds, kpl[kvh][e], 1, 0)
                dk_acc[kvh][e] = dk_acc[kvh][e] + _dot(qm, ds, 0, 0)
                dv_acc[kvh][e] = dv_acc[kvh][e] + _dot(dom, p.astype(BF16), 0, 0)
                ps = jnp.where(live, jnp.exp(s_ref[0, h] - lse_h) * dd, 0.0)
                dsink = dsink - jnp.where(lane == h, _rowsum8(jnp.broadcast_to(ps, (QB, 128))), 0.0)
            dq.append(acc.astype(BF16))
        ds_ref[...] += dsink
        dk = (dk_acc[0][0] + dk_acc[1][1] + pltpu.roll(dk_acc[0][1] + dk_acc[1][0], 64, 0)).T
        dv = (dv_acc[0][0] + dv_acc[1][1] + pltpu.roll(dv_acc[0][1] + dv_acc[1][0], 64, 0)).T
        dkv = jnp.concatenate([dk, dv], axis=1)
        second_ref[...] = (carry_ref[...] + dkv[0:128]).astype(BF16)
        carry_ref[...] = dkv[256:384]

        @pl.when(live)
        def _():
            for j in range(4):
                dq_ref[:, 128 * j:128 * (j + 1)] = dq[j]
            first_ref[...] = dkv[128:256].astype(BF16)

    blk = lambda i: jnp.minimum(i, nb - 1)
    prev = lambda c: pl.BlockSpec((128, 128), lambda i: (jnp.maximum(2 * blk(i) - 1, 0), c))
    cur = lambda w, c: pl.BlockSpec((QB, w), lambda i: (blk(i), c))
    half = lambda index: pl.BlockSpec((128, 256), lambda i: (index(i), 0))
    return pl.pallas_call(
        body, name="swa_bwd",
        out_shape=(jax.ShapeDtypeStruct((t, SWA_W), BF16), jax.ShapeDtypeStruct((t // 2, 256), BF16),
                   jax.ShapeDtypeStruct((t // 2, 256), BF16), jax.ShapeDtypeStruct((8, 128), F32)),
        grid=(nb + 1,),
        in_specs=[pl.BlockSpec(memory_space=pltpu.SMEM),
                  cur(SWA_W, 0), prev(4), cur(128, 4), prev(5), cur(128, 5),
                  cur(SWA_W, 0), cur(SWA_W, 0), cur(128, 0)] + [ANY_SPEC] * len(deps),
        out_specs=(cur(SWA_W, 0), half(blk), half(lambda i: jnp.maximum(i - 1, 0)),
                   pl.BlockSpec((8, 128), lambda i: (0, 0))),
        scratch_shapes=[pltpu.VMEM((128, 256), F32)],
        compiler_params=_params(dimension_semantics=("arbitrary",)),
    )(sinks, z, z, z, z, z, ymix, dymix, lse, *deps)


HB = 256


def _lower_bound(lb_ref):
    a = lb_ref[...]
    a0, a1 = a[0:1], a[1:2]
    mx = jnp.maximum(a0, a1)
    e0, e1 = jnp.exp(a0 - mx), jnp.exp(a1 - mx)
    return e0 / (e0 + e1)


def _hgrn_cols(row_block):
    return [pl.BlockSpec((HB, 2 * HD), lambda j, c=base // (2 * HD) + p: (row_block(j), c))
            for base in (ZQH, ZFH, ZIH, ZGH) for p in range(2)]


NCH = HB // CHUNK


def _split3(x):
    hi = x.astype(BF16)
    r1 = x - hi.astype(F32)
    mid = r1.astype(BF16)
    return hi, mid, (r1 - mid.astype(F32)).astype(BF16)


def _blockdiag(lower):
    r = lax.broadcasted_iota(jnp.int32, (HB, HB), 0)
    c = lax.broadcasted_iota(jnp.int32, (HB, HB), 1)
    return (r // CHUNK == c // CHUNK) & ((c <= r) if lower else (c >= r))


def _chunk_sums(mask_bf16, x):
    return sum(_dot(mask_bf16, part, 1, 0) for part in _split3(x))


def _per_chunk_rows(x, row):
    w = x.shape[1]
    picked = x.reshape(NCH, CHUNK, w)[:, row:row + 1, :]
    return jnp.broadcast_to(picked, (NCH, CHUNK, w)).reshape(HB, w)


def _chunk_stack(x, chunk_of_row):
    return jnp.concatenate([jnp.where(chunk_of_row == c, x, jnp.zeros_like(x)) for c in range(NCH)], axis=1)


def _chunk_pick(x, chunk_of_row):
    w = x.shape[1] // NCH
    out = jnp.zeros((HB, w), x.dtype)
    for c in range(NCH):
        out = jnp.where(chunk_of_row == c, x[:, c * w:(c + 1) * w], out)
    return out


def _hgrn_local(q, f, kf, b):
    sq = _sig(q)
    qf = q * sq * (HD ** -0.5)
    b_mid = _per_chunk_rows(b, CHUNK // 2 - 1)
    b_last = _per_chunk_rows(b, CHUNK - 1)
    qm = qf * jnp.exp(b - b_mid)
    km = kf * jnp.exp(b_mid - b)
    kl = kf * jnp.exp(b_last - b)
    qb = qf * jnp.exp(b)
    return dict(sq=sq, b_mid=b_mid, b_last=b_last, qm=qm, km=km, kl=kl, qb=qb)


def _hgrn2_fwd(z, hgrn_lb, onorm, ymix, t, dep=None):
    nb = t // HB
    deps = [] if dep is None else [dep]

    def body(*refs):
        zq, zf, zi, zg = refs[0:2], refs[2:4], refs[4:6], refs[6:8]
        (lb_ref, on_ref), (y_ref, o_ref, sp_ref, st_ref) = refs[8:10], refs[-4:]

        @pl.when(pl.program_id(0) == 0)
        def _():
            st_ref[...] = jnp.zeros_like(st_ref)

        lb_all = _lower_bound(lb_ref)
        gn = on_ref[...]
        low = _blockdiag(True)
        low_b = low.astype(BF16)
        chunk_of_row = lax.broadcasted_iota(jnp.int32, (HB, HD), 0) // CHUNK
        for p in range(2):
            lbp = lb_all[:, 2 * HD * p:2 * HD * (p + 1)]
            fp = lbp + (1.0 - lbp) * _sig(zf[p][...])
            bp = _chunk_sums(low_b, jnp.log(fp))
            for e in range(2):
                h, ls = 2 * p + e, slice(e * HD, (e + 1) * HD)
                f = fp[:, ls]
                w = _hgrn_local(zq[p][:, ls], f, 1.0 - f, bp[:, ls])
                iv = zi[p][:, ls].astype(BF16)
                a = jnp.where(low, _dot(w["qm"].astype(BF16), w["km"].astype(BF16), 1, 1), 0.0)
                o = _dot(a.astype(BF16), iv, 1, 0)
                u = _dot(iv, _chunk_stack(w["kl"].astype(BF16), chunk_of_row), 0, 0)
                decay = jnp.exp(w["b_last"])
                st = st_ref[h]
                states = []
                for c in range(NCH):
                    sp_ref[h, c] = st
                    states.append(st.astype(BF16))
                    st = st * decay[c * CHUNK:c * CHUNK + 1] + u[:, c * HD:(c + 1) * HD]
                st_ref[h] = st
                inter = _dot(w["qb"].astype(BF16), jnp.concatenate(states, axis=0), 1, 1)
                o = o + _chunk_pick(inter, chunk_of_row)
                hs = slice(h * HD, (h + 1) * HD)
                o_ref[:, hs] = o
                gg = zg[p][:, ls]
                y_ref[:, hs] = (o * _rstd(o) * gn * (gg * _sig(gg))).astype(BF16)

    return pl.pallas_call(
        body, name="hgrn_fwd",
        out_shape=(jax.ShapeDtypeStruct((t, D), BF16), jax.ShapeDtypeStruct((t, HG_W), F32),
                   jax.ShapeDtypeStruct((4, t // CHUNK, HD, HD), F32)),
        grid=(nb,),
        in_specs=_hgrn_cols(lambda j: j) + [pl.BlockSpec((2, HG_W), lambda j: (0, 0)),
                                            pl.BlockSpec((1, HD), lambda j: (0, 0)), ANY_SPEC]
        + [ANY_SPEC] * len(deps),
        out_specs=(pl.BlockSpec((HB, HG_W), lambda j: (j, 1)),
                   pl.BlockSpec((HB, HG_W), lambda j: (j, 0)),
                   pl.BlockSpec((4, NCH, HD, HD), lambda j: (0, j, 0, 0))),
        scratch_shapes=[pltpu.VMEM((4, HD, HD), F32)],
        input_output_aliases={10: 0},
        compiler_params=_params(dimension_semantics=("arbitrary",)),
    )(*[z] * 8, hgrn_lb, onorm, ymix, *deps)


def _hgrn2_bwd(z, hgrn_lb, onorm, o_save, sprev, dymix, dza, t):
    nb = t // HB

    def body(*refs):
        zq, zf, zi, zg = refs[0:2], refs[2:4], refs[4:6], refs[6:8]
        (lb_ref, on_ref, o_ref, sp_ref, dy_ref, dqa_ref, first_ref, second_ref,
         dz_ref, dlb_ref, don_ref, dst_ref) = refs[8:]

        @pl.when(pl.program_id(0) == 0)
        def _():
            dst_ref[...] = jnp.zeros_like(dst_ref)
            dlb_ref[...] = jnp.zeros_like(dlb_ref)
            don_ref[...] = jnp.zeros_like(don_ref)

        dz_ref[:, 0:SWA_W] = dqa_ref[...]
        dz_ref[0:HB // 2, SWA_W:ZQH] = first_ref[...]
        dz_ref[HB // 2:HB, SWA_W:ZQH] = second_ref[...]
        lb_all = _lower_bound(lb_ref)
        gn = on_ref[...]
        low, upp = _blockdiag(True), _blockdiag(False)
        upp_b = upp.astype(BF16)
        low_b = low.astype(BF16)
        row = lax.broadcasted_iota(jnp.int32, (HB, HD), 0)
        chunk_of_row = row // CHUNK
        in_chunk = row % CHUNK
        for p in range(2):
            lbp = lb_all[:, 2 * HD * p:2 * HD * (p + 1)]
            sgp = _sig(zf[p][...])
            fp = lbp + (1.0 - lbp) * sgp
            bp = _chunk_sums(low_b, jnp.log(fp))
            db_pair, dkf_pair = [], []
            for e in range(2):
                h, ls, hs = 2 * p + e, slice(e * HD, (e + 1) * HD), slice((2 * p + e) * HD, (2 * p + e + 1) * HD)
                f = fp[:, ls]
                q = zq[p][:, ls]
                w = _hgrn_local(q, f, 1.0 - f, bp[:, ls])
                iv = zi[p][:, ls].astype(BF16)
                gg = zg[p][:, ls]
                o = o_ref[:, hs]
                dout = dy_ref[:, hs].astype(F32)
                sgg = _sig(gg)
                r = _rstd(o)
                oh = o * r
                dyn = dout * (gg * sgg)
                dz_ref[:, ZGH + h * HD:ZGH + (h + 1) * HD] = (
                    dout * oh * gn * (sgg * (1.0 + gg * (1.0 - sgg)))).astype(BF16)
                don_ref[...] += _rowsum8(dyn * oh)
                do = _norm_bwd(oh, r, dyn * gn).astype(BF16)
                qm, km, kl, qb = (w[n].astype(BF16) for n in ("qm", "km", "kl", "qb"))
                decay = jnp.exp(w["b_last"])
                grads_in = _dot(do, _chunk_stack(qb, chunk_of_row), 0, 0)
                dst = dst_ref[h]
                dstn, dd_rows = [None] * NCH, [None] * NCH
                for c in reversed(range(NCH)):
                    dstn[c] = dst.astype(BF16)
                    dd_rows[c] = jnp.sum(dst * sp_ref[h, c], axis=0, keepdims=True)
                    dst = dst * decay[c * CHUNK:c * CHUNK + 1] + grads_in[:, c * HD:(c + 1) * HD]
                dst_ref[h] = dst
                states = jnp.concatenate([sp_ref[h, c].astype(BF16) for c in range(NCH)], axis=0)
                dstn_all = jnp.concatenate(dstn, axis=0)
                dqb = _dot(_chunk_stack(do, chunk_of_row), states, 1, 0)
                at = jnp.where(upp, _dot(km, qm, 1, 1), 0.0)
                di = _dot(at.astype(BF16), do, 1, 0) + _chunk_pick(_dot(kl, dstn_all, 1, 1), chunk_of_row)
                dz_ref[:, ZIH + h * HD:ZIH + (h + 1) * HD] = di.astype(BF16)
                dkl = _dot(_chunk_stack(iv, chunk_of_row), dstn_all, 1, 0)
                da = jnp.where(low, _dot(do, iv, 1, 1), 0.0).astype(BF16)
                dat = jnp.where(upp, _dot(iv, do, 1, 1), 0.0).astype(BF16)
                dqm = _dot(da, km, 1, 0)
                dkm = _dot(dat, qm, 1, 0)
                b = bp[:, ls]
                e1, e2 = jnp.exp(b - w["b_mid"]), jnp.exp(w["b_mid"] - b)
                e3, e4 = jnp.exp(w["b_last"] - b), jnp.exp(b)
                dqf = dqm * e1 + dqb * e4
                dkf_pair.append(dkm * e2 + dkl * e3)
                t_qm, t_km, t_kl = dqm * w["qm"], dkm * w["km"], dkl * w["kl"]
                db = t_qm - t_km - t_kl + dqb * w["qb"]
                db_mid = jnp.sum((t_km - t_qm).reshape(NCH, CHUNK, HD), axis=1, keepdims=True)
                db_last = jnp.sum(t_kl.reshape(NCH, CHUNK, HD), axis=1, keepdims=True)
                db_last = db_last + jnp.stack(dd_rows, axis=0) * jnp.exp(
                    bp[:, ls].reshape(NCH, CHUNK, HD)[:, CHUNK - 1:CHUNK, :])
                spread = lambda v: jnp.broadcast_to(v, (NCH, CHUNK, HD)).reshape(HB, HD)
                db = (db + jnp.where(in_chunk == CHUNK // 2 - 1, spread(db_mid), 0.0)
                      + jnp.where(in_chunk == CHUNK - 1, spread(db_last), 0.0))
                db_pair.append(db)
                sq = w["sq"]
                dz_ref[:, ZQH + h * HD:ZQH + (h + 1) * HD] = (
                    dqf * (HD ** -0.5) * (sq * (1.0 + q * (1.0 - sq)))).astype(BF16)
            dlogf = _chunk_sums(upp_b, jnp.concatenate(db_pair, axis=1))
            dfv = dlogf / fp - jnp.concatenate(dkf_pair, axis=1)
            dz_ref[:, ZFH + 2 * HD * p:ZFH + 2 * HD * (p + 1)] = (dfv * (1.0 - lbp) * sgp * (1.0 - sgp)).astype(BF16)
            dlb_ref[:, 2 * HD * p:2 * HD * (p + 1)] += _rowsum8(dfv * (1.0 - sgp))

    rev = lambda j: nb - 1 - j
    return pl.pallas_call(
        body, name="hgrn_bwd",
        out_shape=(jax.ShapeDtypeStruct((t, D_IN), BF16), jax.ShapeDtypeStruct((8, HG_W), F32),
                   jax.ShapeDtypeStruct((8, HD), F32)),
        grid=(nb,),
        in_specs=_hgrn_cols(rev) + [pl.BlockSpec((2, HG_W), lambda j: (0, 0)), pl.BlockSpec((1, HD), lambda j: (0, 0)),
                                    pl.BlockSpec((HB, HG_W), lambda j: (rev(j), 0)),
                                    pl.BlockSpec((4, NCH, HD, HD), lambda j: (0, rev(j), 0, 0)),
                                    pl.BlockSpec((HB, HG_W), lambda j: (rev(j), 1)),
                                    pl.BlockSpec((HB, SWA_W), lambda j: (rev(j), 0)),
                                    pl.BlockSpec((HB // 2, 2 * KV_W), lambda j: (rev(j), 0)),
                                    pl.BlockSpec((HB // 2, 2 * KV_W), lambda j: (rev(j), 0))],
        out_specs=(pl.BlockSpec((HB, D_IN), lambda j: (rev(j), 0)), pl.BlockSpec((8, HG_W), lambda j: (0, 0)),
                   pl.BlockSpec((8, HD), lambda j: (0, 0))),
        scratch_shapes=[pltpu.VMEM((4, HD, HD), F32)],
        compiler_params=_params(dimension_semantics=("arbitrary",)),
    )(*[z] * 8, hgrn_lb, onorm, o_save, sprev, dymix, *dza)


XB = 512


def _xattn_fwd(q, k, v, wo, h, g_post, g_pre, t, dep=None):
    tb = min(XB, t)
    deps = [] if dep is None else [dep]

    def body(q_ref, k_ref, v_ref, wo_ref, h_ref, gp_ref, gn_ref, *rest):
        o_ref, y_ref, hn_ref, u_ref = rest[len(deps):]
        for hd in range(XH):
            cols = slice(XD * hd, XD * (hd + 1))
            s = _dot(q_ref[:, cols], k_ref[:, cols], 1, 1) * (XD ** -0.5)
            p = jnp.exp(s - jnp.max(s, axis=-1, keepdims=True))
            l = jnp.sum(p, axis=-1, keepdims=True)
            o_ref[:, cols] = (_dot(p.astype(BF16), v_ref[:, cols], 1, 0) * (1.0 / l)).astype(BF16)
        y, hn, u = _ep_post_pre(_dot(o_ref[...], wo_ref[...], 1, 0), h_ref[...], gp_ref[...], gn_ref[...])
        y_ref[...] = y
        hn_ref[...] = hn
        u_ref[...] = u.astype(BF16)

    row = pl.BlockSpec((tb, D), lambda i: (i, 0))
    whole = lambda a: pl.BlockSpec(a.shape, lambda i: (0,) * a.ndim, pipeline_mode=pl.Buffered(1))
    half = jax.ShapeDtypeStruct((t, D), BF16)
    return pl.pallas_call(
        body, name="xattn_fwd", out_shape=(half, half, jax.ShapeDtypeStruct((t, D), F32), half), grid=(t // tb,),
        in_specs=[row, whole(k), whole(v), whole(wo), row, whole(g_post), whole(g_pre)] + [ANY_SPEC] * len(deps),
        out_specs=(row, row, row, row), compiler_params=_params(),
    )(q, k, v, wo, h, g_post, g_pre, *deps)


def _xattn_bwd(q, k, v, do, wq, wout, dh_out, hn, y, g_post, g_pre, t, dep=None):
    tb = min(XB, t)
    deps = [] if dep is None else [dep]

    def body(q_ref, k_ref, v_ref, do_ref, wq_ref, wout_ref, dho_ref, hn_ref, y_ref, gp_ref, gn_ref, *rest):
        dq_ref, dk_ref, dv_ref, dh_ref, dyp_ref, dym_ref, dgn_ref, dgp_ref = rest[len(deps):]

        @pl.when(pl.program_id(0) == 0)
        def _():
            dk_ref[...] = jnp.zeros_like(dk_ref)
            dv_ref[...] = jnp.zeros_like(dv_ref)
            dgn_ref[...] = jnp.zeros_like(dgn_ref)
            dgp_ref[...] = jnp.zeros_like(dgp_ref)

        for h in range(XH):
            cols = slice(XD * h, XD * (h + 1))
            qh, kh, vh, doh = q_ref[:, cols], k_ref[:, cols], v_ref[:, cols], do_ref[:, cols]
            s = _dot(qh, kh, 1, 1) * (XD ** -0.5)
            p = jnp.exp(s - jnp.max(s, axis=-1, keepdims=True))
            p = p * (1.0 / jnp.sum(p, axis=-1, keepdims=True))
            dp = _dot(doh, vh, 1, 1)
            ds = (p * (dp - jnp.sum(p * dp, axis=-1, keepdims=True)) * (XD ** -0.5)).astype(BF16)
            dq_ref[:, cols] = _dot(ds, kh, 1, 0).astype(BF16)
            dk_ref[:, cols] += _dot(ds, qh, 0, 0)
            dv_ref[:, cols] += _dot(p.astype(BF16), doh, 0, 0)
        du = _dot(dq_ref[...], wq_ref[...], 1, 1)
        dh, dyp, dgn, dgp = _ep_post_pre_bwd(du, dho_ref[...], hn_ref[...], y_ref[...], gp_ref[...], gn_ref[...])
        dh_ref[...] = dh
        dyp = dyp.astype(BF16)
        dyp_ref[...] = dyp
        dym_ref[...] = _dot(dyp, wout_ref[...], 1, 1).astype(BF16)
        dgn_ref[...] += dgn
        dgp_ref[...] += dgp

    row = pl.BlockSpec((tb, D), lambda i: (i, 0))
    mem = pl.BlockSpec(k.shape, lambda i: (0, 0))
    whole = lambda a: pl.BlockSpec(a.shape, lambda i: (0,) * a.ndim, pipeline_mode=pl.Buffered(1))
    acc = pl.BlockSpec((8, D), lambda i: (0, 0))
    half = jax.ShapeDtypeStruct((t, D), BF16)
    return pl.pallas_call(
        body, name="xattn_bwd",
        out_shape=(half, jax.ShapeDtypeStruct(k.shape, F32), jax.ShapeDtypeStruct(k.shape, F32),
                   jax.ShapeDtypeStruct((t, D), F32), half, half,
                   jax.ShapeDtypeStruct((8, D), F32), jax.ShapeDtypeStruct((8, D), F32)),
        grid=(t // tb,),
        in_specs=[row, whole(k), whole(v), row, whole(wq), whole(wout), row, row, row, whole(g_post), whole(g_pre)]
        + [ANY_SPEC] * len(deps),
        out_specs=(row, mem, mem, row, row, row, acc, acc),
        compiler_params=_params(dimension_semantics=("arbitrary",)),
    )(q, k, v, do, wq, wout, dh_out, hn, y, g_post, g_pre, *deps)


def _mem_kv(mem, g_mem, wk, wv):
    def body(m_ref, g_ref, wk_ref, wv_ref, mn_ref, k_ref, v_ref):
        m_ = m_ref[...]
        mn = (m_ * _rstd(m_) * g_ref[...]).astype(BF16)
        mn_ref[...] = mn
        k_ref[...] = _dot(mn, wk_ref[...], 1, 0).astype(BF16)
        v_ref[...] = _dot(mn, wv_ref[...], 1, 0).astype(BF16)

    return pl.pallas_call(body, name="mem_kv", out_shape=(jax.ShapeDtypeStruct(mem.shape, BF16),) * 3,
                          compiler_params=_params())(mem, g_mem, wk, wv)


def _mem_kv_bwd(mn, mem, dk, dv, wk, wv, dep=None):
    deps = [] if dep is None else [dep]

    def body(mn_ref, m_ref, dk_ref, dv_ref, wk_ref, wv_ref, *rest):
        gk_ref, gv_ref, dg_ref = rest[len(deps):]
        mn = mn_ref[...]
        dkb, dvb = dk_ref[...].astype(BF16), dv_ref[...].astype(BF16)
        gk_ref[...] = _dot(mn, dkb, 0, 0).astype(BF16)
        gv_ref[...] = _dot(mn, dvb, 0, 0).astype(BF16)
        dmn = _dot(dkb, wk_ref[...], 1, 1) + _dot(dvb, wv_ref[...], 1, 1)
        m_ = m_ref[...]
        dg_ref[...] = _rowsum8(dmn * (m_ * _rstd(m_)))

    vmem = pl.BlockSpec(memory_space=pltpu.VMEM)
    return pl.pallas_call(
        body, name="mem_kv_bwd",
        out_shape=(jax.ShapeDtypeStruct(wk.shape, BF16), jax.ShapeDtypeStruct(wv.shape, BF16),
                   jax.ShapeDtypeStruct((8, D), F32)),
        in_specs=[vmem] * 6 + [ANY_SPEC] * len(deps), out_specs=(vmem,) * 3, compiler_params=_params(),
    )(mn, mem, dk, dv, wk, wv, *deps)


FB = 256


def _ffn_fwd_bwd(u, wgt, wut, wd, h, target, g_last, y_prev, g_post, g_pre, wo, t):
    tb = min(FB, t)

    def body(u_ref, wg_ref, wu_ref, wd_ref, h_ref, t_ref, gl_ref, yp_ref, gp_ref, gn_ref, wo_ref,
             a_ref, dy_ref, dg_ref, dup_ref, dh_ref, dyp_ref, do_ref, sq_ref, dgl_ref, dgn_ref, dgp_ref):
        @pl.when(pl.program_id(0) == 0)
        def _():
            for ref in (sq_ref, dgl_ref, dgn_ref, dgp_ref):
                ref[...] = jnp.zeros_like(ref)

        u_ = u_ref[...]
        g = _dot(u_, wg_ref[...], 1, 1)
        up = _dot(u_, wu_ref[...], 1, 1)
        sg = _sig(g)
        a = (g * sg * up).astype(BF16)
        a_ref[...] = a
        h_ = h_ref[...]
        sq, dh3, dy, dgl = _ep_final_loss(_dot(a, wd_ref[...], 1, 0), h_, t_ref[...], gl_ref[...])
        sq_ref[...] += sq
        dgl_ref[...] += dgl
        dy = dy.astype(BF16)
        dy_ref[...] = dy
        da = _dot(dy, wd_ref[...], 1, 1)
        dup = (da * g * sg).astype(BF16)
        dgate = (da * up * (sg * (1.0 + g * (1.0 - sg)))).astype(BF16)
        dup_ref[...] = dup
        dg_ref[...] = dgate
        du = _dot(dgate, wg_ref[...], 1, 0) + _dot(dup, wu_ref[...], 1, 0)
        dh, dyp, dgn, dgp = _ep_post_pre_bwd(du, dh3, h_, yp_ref[...], gp_ref[...], gn_ref[...])
        dh_ref[...] = dh
        dyp = dyp.astype(BF16)
        dyp_ref[...] = dyp
        do_ref[...] = _dot(dyp, wo_ref[...], 1, 1).astype(BF16)
        dgn_ref[...] += dgn
        dgp_ref[...] += dgp

    row = lambda w: pl.BlockSpec((tb, w), lambda i: (i, 0))
    whole = lambda a: pl.BlockSpec(a.shape, lambda i: (0,) * a.ndim, pipeline_mode=pl.Buffered(1))
    acc = pl.BlockSpec((8, D), lambda i: (0, 0))
    wide, half, sums = (jax.ShapeDtypeStruct((t, D_FF), BF16), jax.ShapeDtypeStruct((t, D), BF16),
                        jax.ShapeDtypeStruct((8, D), F32))
    return pl.pallas_call(
        body, name="ffn_fwd_bwd",
        out_shape=(wide, half, wide, wide, jax.ShapeDtypeStruct((t, D), F32), half, half, sums, sums, sums, sums),
        grid=(t // tb,),
        in_specs=[row(D), whole(wgt), whole(wut), whole(wd), row(D), row(D), whole(g_last), row(D), whole(g_post),
                  whole(g_pre), whole(wo)],
        out_specs=(row(D_FF), row(D), row(D_FF), row(D_FF), row(D), row(D), row(D), acc, acc, acc, acc),
        compiler_params=_params(dimension_semantics=("arbitrary",)),
    )(u, wgt, wut, wd, h, target, g_last, y_prev, g_post, g_pre, wo)


def _local_step(x, mem, target, fetch, sm, emit=None, first_dep=None, milestone=None):
    t = x.shape[0]
    w, gw = {}, {}

    def out(key, g):
        gw[key] = g
        return None if emit is None else emit(key, g)

    def tell(tag, value):
        return None if milestone is None else milestone(tag, value)
    u1 = _prenorm(x, sm["g_mix_pre"], name="prenorm_mix", dep=first_dep)
    w["winT"] = fetch("winT", u1)
    z = _mm_nt(u1, w["winT"], out_dtype=F32, tm=1024, tn=1408, name="mm_z")
    ymix, lse = _swa_fwd(z, sm["sinks"], t)
    ymix, o_h, sprev = _hgrn2_fwd(z, sm["hgrn_lb"], sm["hgrn_onorm"], ymix, t, dep=tell("swa", lse))
    for key in ("wout", "wq", "wk", "wv", "wo"):
        w[key] = fetch(key, ymix)
    y1, h1, u2, qx = _mm_rows([(ymix, w["wout"], False)], [x], [sm["g_mix_post"], sm["g_x_pre"], w["wq"]],
                              _then(_ep_post_pre, 2, False), _EP_POST_PRE_OUTS + [ROW_BF16], tm=512,
                              name="mm_y1_post_qx")
    mn, kx, vx = _mem_kv(mem, sm["g_mem"], w["wk"], w["wv"])
    ox, y2, h2, u3 = _xattn_fwd(qx, kx, vx, w["wo"], h1, sm["g_x_post"], sm["g_ffn_pre"], t, dep=tell("kv", kx))
    for key in ("wgT", "wuT", "wd"):
        w[key] = fetch(key, u3)
    act, dy3, dgate, dup, dh2, dy2, dox, sq, dg_ffn_post, dg_ffn_pre, dg_x_post = _ffn_fwd_bwd(
        u3, w["wgT"], w["wuT"], w["wd"], h2, target, sm["g_ffn_post"], y2, sm["g_x_post"], sm["g_ffn_pre"], w["wo"], t)
    dep = out("wd", *_mm_tn([act], [dy3], name="mm_gwd"))
    gwg, gwu = _mm_tn([dgate, dup], [u3], name="mm_gwg_gwu", dep=dep)
    out("wgT", gwg)
    dep = out("wuT", gwu)
    dqx, dkx, dvx, dh1, dy1, dymix, dg_x_pre, dg_mix_post = _xattn_bwd(
        qx, kx, vx, dox, w["wq"], w["wout"], dh2, h1, y1, sm["g_mix_post"], sm["g_x_pre"], t, dep=dep)
    gwo, gwq, gwout = _mm_tn([ox, u2, ymix], [dy2, dqx, dy1], name="mm_gwo_gwq_gwout")
    gwk, gwv, dg_mem = _mem_kv_bwd(mn, mem, dkx, dvx, w["wk"], w["wv"])
    for key, g in (("wo", gwo), ("wq", gwq), ("wout", gwout), ("wk", gwk), ("wv", gwv)):
        dep = out(key, g)
    *dza, dsinks = _swa_bwd(z, sm["sinks"], ymix, lse, dymix, t, dep=dep)
    dz, dlb, donorm = _hgrn2_bwd(z, sm["hgrn_lb"], sm["hgrn_onorm"], o_h, sprev, dymix, dza, t)
    dep = out("winT", *_mm_tn([dz], [u1], name="mm_gwin"))
    grad_x, dg_mix_pre = _mm_rows([(dz, w["winT"], False)], [dh1, x], [sm["g_mix_pre"]], _ep_pre_bwd,
                                  _EP_PRE_BWD_OUTS, tm=512, name="mm_du1_pre_bwd", dep=dep)
    parts = dict(g_mix_pre=dg_mix_pre, g_mix_post=dg_mix_post, g_mem=dg_mem, g_x_pre=dg_x_pre,
                 g_x_post=dg_x_post, g_ffn_pre=dg_ffn_pre, g_ffn_post=dg_ffn_post,
                 hgrn_onorm=donorm, hgrn_lb=dlb, sinks=dsinks, sq=sq)
    return grad_x, gw, parts


def _position():
    return lax.axis_index("x"), lax.axis_index("y"), lax.axis_index("c")


def _peer(pos, k):
    x, y, c = pos
    return (1 - x if k & 4 else x, 1 - y if k & 2 else y, 1 - c if k & 1 else c)


def _linear(pos):
    x, y, c = pos
    return 4 * x + 2 * y + c


HBM_SPEC = pl.BlockSpec(memory_space=pltpu.HBM)
SEM_SPEC = pl.BlockSpec(memory_space=pltpu.SEMAPHORE)
DATAFLOW = pltpu.SideEffectType.DATAFLOW_SIDE_EFFECTING
SEND_ORDER = (1, 2, 4, 3, 5, 6, 7)


def _in_hbm(a):
    return pltpu.with_memory_space_constraint(a, pltpu.HBM)


def _prepare_weights(shards, *, name, dep=None):
    n = len(shards)
    deps = [] if dep is None else [dep]

    def body(*refs):
        ins, (outs, lands, sem) = refs[:n], (refs[-2 * n - 1:-n - 1], refs[-n - 1:-1], refs[-1])
        me_lin = _linear(_position())
        copies = []
        for a in range(n):
            r = ins[a].shape[0]
            outs[a][...] = ins[a][...].astype(BF16)
            copies.append(pltpu.make_async_copy(outs[a], lands[a].at[pl.ds(me_lin * r, r), :], sem.at[a]))
            copies[-1].start()
        for cp in copies:
            cp.wait()

    vmem = pl.BlockSpec(memory_space=pltpu.VMEM)
    res = pl.pallas_call(
        body, name=name,
        out_shape=tuple(jax.ShapeDtypeStruct(s.shape, BF16) for s in shards)
        + tuple(jax.ShapeDtypeStruct((N_DEV * s.shape[0], s.shape[1]), BF16) for s in shards),
        in_specs=[vmem] * n + [ANY_SPEC] * len(deps), out_specs=tuple([vmem] * n + [ANY_SPEC] * n),
        scratch_shapes=[pltpu.SemaphoreType.DMA((n,))], compiler_params=_params(),
    )(*shards, *deps)
    return res[:n], res[n:]


def _copies_start(arrays, plan, n, *, name):
    na = len(arrays)

    def body(*refs):
        ins, send_sems, recv_sems = refs[:na], refs[na], refs[na + 1]
        me = _position()
        for j in range(n):
            src, dst, peer, _ = plan(ins, me, j)
            pltpu.make_async_remote_copy(src_ref=src, dst_ref=dst, send_sem=send_sems.at[j], recv_sem=recv_sems.at[j],
                                         device_id=peer, device_id_type=MESH).start()

    return pl.pallas_call(
        body, name=name,
        out_shape=(pltpu.SemaphoreType.DMA((n,)), pltpu.SemaphoreType.DMA((n,)))
        + tuple(pltpu.HBM(a.shape, a.dtype) for a in arrays),
        in_specs=(HBM_SPEC,) * na, out_specs=(SEM_SPEC, SEM_SPEC) + (HBM_SPEC,) * na,
        input_output_aliases={i: 2 + i for i in range(na)},
        compiler_params=pltpu.CompilerParams(has_side_effects=DATAFLOW),
    )(*[_in_hbm(a) for a in arrays])


def _copies_wait(send_sems, recv_sems, arrays, plan, n, after, *, name):
    na = len(arrays)

    def body(*refs):
        ins, send_sems, recv_sems = refs[:na], refs[na], refs[na + 1]
        me = _position()
        for j in range(n):
            src, _, peer, landed = plan(ins, me, j)
            copy = pltpu.make_async_remote_copy(src_ref=src, dst_ref=landed, send_sem=send_sems.at[j],
                                                recv_sem=recv_sems.at[j], device_id=peer, device_id_type=MESH)
            copy.wait_send()
            copy.wait_recv()

    return pl.pallas_call(
        body, name=name, out_shape=tuple(pltpu.HBM(a.shape, a.dtype) for a in arrays),
        in_specs=(HBM_SPEC,) * na + (SEM_SPEC, SEM_SPEC, ANY_SPEC), out_specs=(HBM_SPEC,) * na,
        input_output_aliases={i: i for i in range(na)},
        compiler_params=pltpu.CompilerParams(has_side_effects=DATAFLOW),
    )(*arrays, send_sems, recv_sems, after)


SAME_CORE = (2, 4, 6)


class _TwoLevelGather:
    def __init__(self, shards, lands, *, name):
        n = self.n = len(shards)
        self.name = name
        first_peers = (1,) + SAME_CORE

        def rows(ref, pos):
            r = ref.shape[0] // N_DEV
            return ref.at[pl.ds(_linear(pos) * r, r), :]

        def first(refs, me, j):
            a, peer = j // 4, _peer(me, first_peers[j % 4])
            return refs[a], rows(refs[n + a], me), peer, rows(refs[n + a], peer)

        def second(refs, me, j):
            a, sibling = j // 3, _peer(me, 1)
            mine = rows(refs[a], _peer(me, SAME_CORE[j % 3]))
            return mine, mine, sibling, rows(refs[a], _peer(sibling, SAME_CORE[j % 3]))

        self._first, self._second = first, second
        self._flight = _copies_start(list(shards) + list(lands), first, 4 * n, name=name + "_send")
        self.dep = self._flight[2]

    def pass_on(self, after):
        send1, recv1, *arrays = self._flight
        arrays = _copies_wait(send1, recv1, arrays, self._first, 4 * self.n, after, name=self.name + "_recv")
        self._flight = _copies_start(list(arrays[self.n:]), self._second, 3 * self.n, name=self.name + "_pass")
        return self._flight[2]

    def finish(self, after):
        send2, recv2, *lands = self._flight
        return _copies_wait(send2, recv2, lands, self._second, 3 * self.n, after, name=self.name + "_pass_recv")


def _exchange_start(gs, *, name):
    n = len(gs)
    rows = [g.shape[0] // N_DEV for g in gs]
    lands = [lax.empty((N_DEV - 1, r, g.shape[1]), g.dtype) for g, r in zip(gs, rows)]

    def body(*refs):
        g_refs, land_refs = refs[:n], refs[n:2 * n]
        send_sems, recv_sems = refs[2 * n:3 * n], refs[3 * n:4 * n]
        me = _position()
        for a in range(n):
            for k in SEND_ORDER:
                peer = _peer(me, k)
                pltpu.make_async_remote_copy(
                    src_ref=g_refs[a].at[pl.ds(_linear(peer) * rows[a], rows[a]), :],
                    dst_ref=land_refs[a].at[k - 1],
                    send_sem=send_sems[a].at[k - 1], recv_sem=recv_sems[a].at[k - 1],
                    device_id=peer, device_id_type=MESH).start()

    res = pl.pallas_call(
        body, name=name,
        out_shape=tuple(pltpu.SemaphoreType.DMA((N_DEV - 1,)) for _ in range(2 * n))
        + tuple(pltpu.HBM(a.shape, a.dtype) for a in gs + lands),
        in_specs=(HBM_SPEC,) * (2 * n), out_specs=(SEM_SPEC,) * (2 * n) + (HBM_SPEC,) * (2 * n),
        input_output_aliases={i: 2 * n + i for i in range(2 * n)},
        compiler_params=pltpu.CompilerParams(has_side_effects=DATAFLOW),
    )(*[_in_hbm(a) for a in gs + lands])
    return [(res[a], res[n + a], res[2 * n + a], res[3 * n + a]) for a in range(n)]


def _exchange_wait(send_sems, recv_sems, g_thru, land_thru, after, *, name):
    r = land_thru.shape[1]

    def body(g_ref, land_ref, send_sems, recv_sems, after_ref, g_dead, got_ref):
        del after_ref, g_dead, got_ref
        me = _position()
        for k in SEND_ORDER:
            peer = _peer(me, k)
            copy = pltpu.make_async_remote_copy(
                src_ref=g_ref.at[pl.ds(_linear(peer) * r, r), :], dst_ref=land_ref.at[k - 1],
                send_sem=send_sems.at[k - 1], recv_sem=recv_sems.at[k - 1],
                device_id=peer, device_id_type=MESH)
            copy.wait_send()
            copy.wait_recv()

    return pl.pallas_call(
        body, name=name,
        out_shape=(pltpu.HBM(g_thru.shape, g_thru.dtype), pltpu.HBM(land_thru.shape, land_thru.dtype)),
        in_specs=(HBM_SPEC, HBM_SPEC, SEM_SPEC, SEM_SPEC, pl.BlockSpec(memory_space=pl.ANY)),
        out_specs=(HBM_SPEC, HBM_SPEC), input_output_aliases={0: 0, 1: 1},
        compiler_params=pltpu.CompilerParams(has_side_effects=DATAFLOW),
    )(g_thru, land_thru, send_sems, recv_sems, after)


ADAMW_TILE_ROWS = 256


def _adamw_math(w, g, m, v):
    m = B1 * m + (1.0 - B1) * g
    v = B2 * v + (1.0 - B2) * (g * g)
    delta = -LR * ((m / C1) / (jnp.sqrt(v / C2) + AEPS) + WD * w)
    return delta, m, v


def _sum_adamw(items, *, name):
    n = len(items)
    r, d = items[0][2].shape
    assert all(it[2].shape == (r, d) for it in items)
    rc = r // 2 if r > ADAMW_TILE_ROWS else r
    tiles = [(a, r0) for a in range(n) for r0 in range(0, r, rc)]
    n_in, n_out = 5, 4

    def body(*refs):
        ins, outs = refs[:n_in * n], refs[n_in * n:(n_in + n_out) * n]
        land_v, own_v, f32_v, sems = refs[(n_in + n_out) * n:]
        me_lin = _linear(_position())

        def loads(j):
            a, r0 = tiles[j]
            g_all, land, w, m, v = ins[n_in * a:n_in * a + n_in]
            rows = pl.ds(r0, rc)
            pairs = [(land.at[:, rows, :], land_v.at[j]), (g_all.at[pl.ds(me_lin * r + r0, rc), :], own_v.at[j]),
                     (w.at[rows, :], f32_v.at[j, 0]), (m.at[rows, :], f32_v.at[j, 1]), (v.at[rows, :], f32_v.at[j, 2])]
            return [pltpu.make_async_copy(src, dst, sems.at[j, i]) for i, (src, dst) in enumerate(pairs)]

        def stores(j):
            a, r0 = tiles[j]
            return [pltpu.make_async_copy(f32_v.at[j, 3 + i], outs[n_out * a + i].at[pl.ds(r0, rc), :],
                                          sems.at[j, n_in + i]) for i in range(n_out)]

        for j in range(len(tiles)):
            for cp in loads(j):
                cp.start()
        for j in range(len(tiles)):
            for cp in loads(j):
                cp.wait()
            g = land_v[j, 0].astype(F32)
            for s in range(1, N_DEV - 1):
                g = g + land_v[j, s].astype(F32)
            g = own_v[j].astype(F32) + g
            f32_v[j, 3] = g
            f32_v[j, 4], f32_v[j, 5], f32_v[j, 6] = _adamw_math(f32_v[j, 0], g, f32_v[j, 1], f32_v[j, 2])
            for cp in stores(j):
                cp.start()
        for j in range(len(tiles)):
            for cp in stores(j):
                cp.wait()

    nt = len(tiles)
    res = pl.pallas_call(
        body, name=name,
        out_shape=tuple(jax.ShapeDtypeStruct((r, d), F32) for _ in range(n_out * n)),
        in_specs=[ANY_SPEC] * (n_in * n), out_specs=(ANY_SPEC,) * (n_out * n),
        scratch_shapes=[pltpu.VMEM((nt, N_DEV - 1, rc, d), BF16), pltpu.VMEM((nt, rc, d), BF16),
                        pltpu.VMEM((nt, 3 + n_out, rc, d), F32), pltpu.SemaphoreType.DMA((nt, n_in + n_out))],
        compiler_params=_params(),
    )(*[a for it in items for a in it])
    return [res[n_out * a:n_out * a + n_out] for a in range(n)]


SMALL = ("g_mix_pre", "g_mix_post", "g_mem", "g_x_pre", "g_x_post", "g_ffn_pre", "g_ffn_post",
         "hgrn_onorm", "hgrn_lb", "sinks")
SMALL_W = dict(hgrn_onorm=HD, hgrn_lb=HG_W, sinks=8)
SQ_ROW = len(SMALL)
PACK_ROWS = 16


def _small_pack(parts):
    ns = len(SMALL)

    def body(*refs):
        part, mine, slots, sem = refs[:ns + 1], refs[ns + 1], refs[ns + 2], refs[ns + 3]
        mine[...] = jnp.zeros((PACK_ROWS, D), F32)
        for r, name in enumerate(SMALL):
            wd = SMALL_W.get(name, D)
            mine[r:r + 1, 0:wd] = jnp.sum(part[r][...], axis=0, keepdims=True)[:, 0:wd]
        sq = jnp.sum(part[ns][...]) * (0.5 / D)
        mine[SQ_ROW:SQ_ROW + 1, :] = jnp.full((1, D), sq, F32)
        own = pltpu.make_async_copy(mine, slots.at[_linear(_position())], sem)
        own.start()
        own.wait()

    vmem = pl.BlockSpec(memory_space=pltpu.VMEM)
    return pl.pallas_call(
        body, name="small_pack",
        out_shape=(jax.ShapeDtypeStruct((PACK_ROWS, D), F32), jax.ShapeDtypeStruct((N_DEV, PACK_ROWS, D), F32)),
        in_specs=[vmem] * (ns + 1), out_specs=(vmem, ANY_SPEC),
        scratch_shapes=[pltpu.SemaphoreType.DMA(())], compiler_params=_params(),
    )(*[parts[n] for n in SMALL], parts["sq"])


def _small_exchange(mine, slots):
    def plan(refs, me, j):
        peer = _peer(me, j + 1)
        return refs[0], refs[1].at[_linear(me)], peer, refs[1].at[_linear(peer)]

    send, recv, mine1, slots1 = _copies_start([mine, slots], plan, N_DEV - 1, name="small_send")
    return lambda after: _copies_wait(send, recv, [mine1, slots1], plan, N_DEV - 1, after, name="small_recv")[1]


def _small_update(slots, sm, m_sm, v_sm):
    ns = len(SMALL)

    def body(*refs):
        tot = refs[0][0]
        for s in range(1, N_DEV):
            tot = tot + refs[0][s]
        w_refs, m_refs, v_refs = refs[1:ns + 1], refs[ns + 1:2 * ns + 1], refs[2 * ns + 1:3 * ns + 1]
        outs = refs[3 * ns + 1:]
        loss_ref = outs[0]
        g_out, d_out = outs[1:ns + 1], outs[ns + 1:2 * ns + 1]
        nm_out, nv_out = outs[2 * ns + 1:3 * ns + 1], outs[3 * ns + 1:4 * ns + 1]
        loss_ref[...] = tot[SQ_ROW:SQ_ROW + 1, 0:1]
        for r, name in enumerate(SMALL):
            wd = SMALL_W.get(name, D)
            g = tot[r:r + 1, 0:wd]
            w = w_refs[r][...]
            if name == "hgrn_lb":
                mx = jnp.maximum(w[0:1], w[1:2])
                e0, e1 = jnp.exp(w[0:1] - mx), jnp.exp(w[1:2] - mx)
                lb0 = e0 / (e0 + e1)
                g0 = g * lb0 * (1.0 - lb0)
                for i, gi in enumerate((g0, -g0)):
                    d, nm, nv = _adamw_math(w[i:i + 1], gi, m_refs[r][i:i + 1, :], v_refs[r][i:i + 1, :])
                    g_out[r][i:i + 1, :] = gi
                    d_out[r][i:i + 1, :], nm_out[r][i:i + 1, :], nv_out[r][i:i + 1, :] = d, nm, nv
            else:
                d, nm, nv = _adamw_math(w, g, m_refs[r][...], v_refs[r][...])
                g_out[r][...] = g
                d_out[r][...], nm_out[r][...], nv_out[r][...] = d, nm, nv

    shapes = [jax.ShapeDtypeStruct(sm[n].shape, F32) for n in SMALL]
    res = pl.pallas_call(
        body, name="small_update", out_shape=tuple([jax.ShapeDtypeStruct((1, 1), F32)] + shapes * 4),
        compiler_params=_params(),
    )(slots, *[sm[n] for n in SMALL], *[m_sm[n] for n in SMALL], *[v_sm[n] for n in SMALL])
    groups = [dict(zip(SMALL, res[1 + i * ns:1 + (i + 1) * ns])) for i in range(4)]
    return res[0], groups[0], groups[1], groups[2], groups[3]


BIG = ("w_in", "w_gate", "w_up", "w_down", "w_out", "wq_x", "wk_x", "wv_x", "wo_x")
BIG_KEY = dict(w_in="winT", w_gate="wgT", w_up="wuT", w_down="wd", w_out="wout", wq_x="wq", wk_x="wk",
               wv_x="wv", wo_x="wo")
TRANSPOSED = ("w_in", "w_gate", "w_up")
WEIGHTS = ("w_in", "sinks", "hgrn_lb", "hgrn_onorm", "w_out", "g_mix_pre", "g_mix_post", "g_mem", "g_x_pre",
           "g_x_post", "wq_x", "wk_x", "wv_x", "wo_x", "g_ffn_pre", "g_ffn_post", "w_gate", "w_up", "w_down")


def kernel(x, mem, w_in, sinks, hgrn_lb, hgrn_onorm, w_out, g_mix_pre, g_mix_post, g_mem, g_x_pre, g_x_post, wq_x, wk_x, wv_x, wo_x, g_ffn_pre, g_ffn_post, w_gate, w_up, w_down, loss_target, m_w_in, m_sinks, m_hgrn_lb, m_hgrn_onorm, m_w_out, m_g_mix_pre, m_g_mix_post, m_g_mem, m_g_x_pre, m_g_x_post, m_wq_x, m_wk_x, m_wv_x, m_wo_x, m_g_ffn_pre, m_g_ffn_post, m_w_gate, m_w_up, m_w_down, v_w_in, v_sinks, v_hgrn_lb, v_hgrn_onorm, v_w_out, v_g_mix_pre, v_g_mix_post, v_g_mem, v_g_x_pre, v_g_x_post, v_wq_x, v_wk_x, v_wv_x, v_wo_x, v_g_ffn_pre, v_g_ffn_post, v_w_gate, v_w_up, v_w_down):
    given = dict(locals())
    wts = {n: given[n] for n in WEIGHTS}
    ms = {n: given["m_" + n] for n in WEIGHTS}
    vs = {n: given["v_" + n] for n in WEIGHTS}

    def mat(a, name):
        a = a[0]
        return a.T if name in TRANSPOSED else a

    groups = (("w_in",), ("w_out", "wq_x", "wk_x", "wv_x", "wo_x"), ("w_gate", "w_up", "w_down"))
    gathers = []
    first_dep = None
    for tag, group in zip(("w_in", "w_attn", "w_ffn"), groups):
        shards, lands = _prepare_weights([mat(wts[n], n) for n in group], name="prepare_" + tag, dep=first_dep)
        gathers.append(_TwoLevelGather(shards, lands, name=tag))
        first_dep = gathers[-1].dep
    name_of = {k: n for n, k in BIG_KEY.items()}
    gathered = {}

    def milestone(tag, value):
        return gathers[{"swa": 1, "kv": 2}[tag]].pass_on(value)

    def fetch(key, after):
        name = name_of[key]
        if name not in gathered:
            g = [i for i, group in enumerate(groups) if name in group][0]
            if g == 0:
                gathers[0].pass_on(after)
            gathered.update(zip(groups[g], gathers[g].finish(after)))
        return gathered[name]

    sm = {n: wts[n] for n in SMALL}
    started, held = {}, {}
    send_with = {k: group for group in (("wgT", "wuT"), ("wo", "wq", "wout", "wk", "wv")) for k in group}

    def emit(key, g):
        held[key] = g
        group = send_with.get(key, (key,))
        if key != group[-1]:
            return None
        flights = _exchange_start([held[k] for k in group], name="grad_send_" + name_of[group[0]])
        started.update({name_of[k]: f for k, f in zip(group, flights)})
        return flights[-1][2]

    grad_x, _, parts = _local_step(x[0], mem[0], loss_target[0], fetch, sm, emit, first_dep=first_dep, milestone=milestone)
    small_finish = _small_exchange(*_small_pack(parts))
    grads, deltas, new_m, new_v = {}, {}, {}, {}
    after = grad_x
    for group in (("w_down",), ("w_gate", "w_up"), ("wo_x", "wq_x", "wk_x", "wv_x", "w_out"), ("w_in",)):
        items = []
        for n in group:
            g_all, land = _exchange_wait(*started[n], after, name="grad_recv_" + n)
            items.append((g_all, land, mat(wts[n], n), mat(ms[n], n), mat(vs[n], n)))
            after = land
        for n, res in zip(group, _sum_adamw(items, name="adamw_" + group[0])):
            after = res[1]
            if n in TRANSPOSED:
                res = [a.T for a in res]
            grads[n], deltas[n], new_m[n], new_v[n] = [a[None] for a in res]
    loss, g_s, d_s, m_s, v_s = _small_update(small_finish(after), sm, {n: ms[n] for n in SMALL},
                                             {n: vs[n] for n in SMALL})
    grads.update(g_s), deltas.update(d_s), new_m.update(m_s), new_v.update(v_s)
    return (loss[0, 0], grad_x[None], *[grads[n] for n in WEIGHTS], *[deltas[n] for n in WEIGHTS],
            *[new_m[n] for n in WEIGHTS], *[new_v[n] for n in WEIGHTS])
```

```python
import functools

import jax
import jax.numpy as jnp
from jax import lax
from jax.experimental import pallas as pl
from jax.experimental.pallas import tpu as pltpu

F32 = jnp.float32
BF16 = jnp.bfloat16

D = 1024
D_IN = 2816
D_FF = 2816
CHUNK = 64
SWA_W = 512
KV_W = 128
HG_W = 512
HD = 128
ZQH, ZFH, ZIH, ZGH = 768, 1280, 1792, 2304
XH, XD = 4, 256
EPS = 1e-6
NEG = -1e30
N_DEV = 8
MESH = pl.DeviceIdType.MESH

LR, B1, B2, AEPS, WD, STEP = 0.001, 0.9, 0.999, 1e-08, 0.01, 10
C1 = 1.0 - B1 ** STEP
C2 = 1.0 - B2 ** STEP

VMEM_LIMIT = 56 * 1024 * 1024


def _params(**kw):
    return pltpu.CompilerParams(vmem_limit_bytes=VMEM_LIMIT, **kw)


def _sig(x):
    return 1.0 / (1.0 + jnp.exp(-x))


def _rowsum8(x):
    r, w = x.shape
    return jnp.sum(x.reshape(r // 8, 8, w), axis=0)


def _dot(a, b, ca, cb, precision=None):
    return lax.dot_general(a, b, (((ca,), (cb,)), ((), ())), preferred_element_type=F32,
                           precision=precision)


ANY_SPEC = pl.BlockSpec(memory_space=pl.ANY)


def _mm_nt(a, b, *, out_dtype, tm, tn, name):
    (m, k), n = a.shape, b.shape[0]
    tm, tn = min(tm, m), min(tn, n)
    assert a.dtype == BF16 and b.dtype == BF16 and m % tm == 0 and n % tn == 0, (name, m, n, tm, tn)

    def body(a_ref, b_ref, o_ref):
        o_ref[...] = _dot(a_ref[...], b_ref[...], 1, 1).astype(out_dtype)

    return pl.pallas_call(
        body, name=name, out_shape=jax.ShapeDtypeStruct((m, n), out_dtype), grid=(n // tn, m // tm),
        in_specs=[pl.BlockSpec((tm, k), lambda j, i: (i, 0)), pl.BlockSpec((tn, k), lambda j, i: (j, 0))],
        out_specs=pl.BlockSpec((tm, tn), lambda j, i: (i, j)),
        compiler_params=_params(dimension_semantics=("parallel", "parallel")),
    )(a, b)


TN_FIRST = 256
TN_REST = 1152
TN_SLICES = 4


def _mm_tn(a_list, b_list, *, name, dep=None):
    na, nbd, (k, m), n = len(a_list), len(b_list), a_list[0].shape, b_list[0].shape[1]
    assert nbd in (1, na) and all(b.dtype == BF16 and b.shape == (k, n) for b in b_list)
    assert all(a.dtype == BF16 and a.shape == (k, m) for a in a_list)
    nbv = min(2, nbd)
    widths = [TN_FIRST, TN_FIRST]
    while sum(widths) < m:
        widths.append(min(TN_REST, m - sum(widths)))
    starts = [sum(widths[:i]) for i in range(len(widths))]
    assert sum(widths) == m
    per = len(widths)
    nb = na * per
    ks = k // TN_SLICES
    ahead = 2
    assert ahead < per
    deps = [] if dep is None else [dep]

    def body(*refs):
        a_hbm, b_hbm, rest = refs[:na], refs[na:na + nbd], refs[na + nbd + len(deps):]
        o_hbm, b_v, rest = rest[:na], rest[na:na + nbv], rest[na + nbv:]
        a_v, o_v, sems = rest[:per], rest[per:-1], rest[-1]
        sliced = []
        for c in range(TN_SLICES):
            rows = pl.ds(c * ks, ks)
            sliced.append((pltpu.make_async_copy(b_hbm[0].at[rows, :], b_v[0].at[rows, :], sems.at[2 * c]),
                           pltpu.make_async_copy(a_hbm[0].at[rows, pl.ds(0, widths[0])], a_v[0].at[rows, :],
                                                 sems.at[2 * c + 1])))
        base = 2 * TN_SLICES - 1
        cols = [pl.ds(starts[i % per], widths[i % per]) for i in range(nb)]
        loads = [None] + [pltpu.make_async_copy(a_hbm[i // per].at[:, cols[i]], a_v[i % per], sems.at[base + i])
                          for i in range(1, nb)]
        stores = [pltpu.make_async_copy(o_v[i % per], o_hbm[i // per].at[cols[i], :], sems.at[base + nb + i])
                  for i in range(nb)]
        next_b = [None] + [pltpu.make_async_copy(b_hbm[j], b_v[j % nbv], sems.at[base + 2 * nb + j])
                           for j in range(1, nbd)]
        for pair in sliced:
            for cp in pair:
                cp.start()
        for i in range(1, 1 + ahead):
            loads[i].start()
        if nbd > 1:
            next_b[1].start()
        acc = None
        for c, pair in enumerate(sliced):
            for cp in pair:
                cp.wait()
            p = _dot(a_v[0][c * ks:(c + 1) * ks, :], b_v[0][c * ks:(c + 1) * ks, :], 0, 0)
            acc = p if acc is None else acc + p
        o_v[0][...] = acc.astype(BF16)
        stores[0].start()
        for i in range(1, nb):
            j = (i // per) % nbd
            if nbd > 1 and i % per == 0:
                next_b[j].wait()
                if j + 1 < nbd:
                    next_b[j + 1].start()
            loads[i].wait()
            if i + ahead < nb:
                loads[i + ahead].start()
            if i >= per:
                stores[i - per].wait()
            o_v[i % per][...] = _dot(a_v[i % per][...], b_v[j % nbv][...], 0, 0).astype(BF16)
            stores[i].start()
        for cp in stores[nb - per:]:
            cp.wait()

    return pl.pallas_call(
        body, name=name, out_shape=tuple(jax.ShapeDtypeStruct((m, n), BF16) for _ in a_list),
        in_specs=[ANY_SPEC] * (na + nbd + len(deps)), out_specs=(ANY_SPEC,) * na,
        scratch_shapes=[pltpu.VMEM((k, n), BF16)] * nbv + [pltpu.VMEM((k, cw), BF16) for cw in widths]
        + [pltpu.VMEM((cw, n), BF16) for cw in widths]
        + [pltpu.SemaphoreType.DMA((2 * TN_SLICES - 1 + 2 * nb + nbd,))],
        compiler_params=_params(),
    )(*a_list, *b_list, *deps)


def _mm_rows(prods, rows_in, vecs_in, epilogue, outs, *, tm, name, dep=None):
    m = prods[0][0].shape[0]
    n = prods[0][1].shape[0] if prods[0][2] else prods[0][1].shape[1]
    tm = min(tm, m)
    assert m % tm == 0
    deps = [] if dep is None else [dep]
    n_p, n_r, n_v = len(prods), len(rows_in), len(vecs_in)

    def body(*refs):
        ab = refs[:2 * n_p]
        row_refs = refs[2 * n_p:2 * n_p + n_r]
        vec_refs = refs[2 * n_p + n_r:2 * n_p + n_r + n_v]
        out_refs = refs[2 * n_p + n_r + n_v + len(deps):]
        p = None
        for j, (_, _, tb) in enumerate(prods):
            t = _dot(ab[2 * j][...].astype(BF16), ab[2 * j + 1][...], 1, 1 if tb else 0)
            p = t if p is None else p + t
        vals = epilogue(p, *[r[...] for r in row_refs], *[v[...] for v in vec_refs])
        for (dtype, kind), o_ref, val in zip(outs, out_refs, vals):
            if kind == "row":
                o_ref[...] = val.astype(dtype)
            else:
                @pl.when(pl.program_id(0) == 0)
                def _(o_ref=o_ref):
                    o_ref[...] = jnp.zeros_like(o_ref)

                o_ref[...] += val

    row = lambda w: pl.BlockSpec((tm, w), lambda i: (i, 0))
    whole = lambda a: pl.BlockSpec(a.shape, lambda i: (0,) * a.ndim, pipeline_mode=pl.Buffered(1))
    in_specs, args = [], []
    for a, b, _ in prods:
        in_specs += [row(a.shape[1]), whole(b)]
        args += [a, b]
    in_specs += [row(r.shape[1]) for r in rows_in] + [whole(v) for v in vecs_in] + [ANY_SPEC] * len(deps)
    return pl.pallas_call(
        body, name=name,
        out_shape=tuple(jax.ShapeDtypeStruct((m, n) if kind == "row" else (8, n), dtype) for dtype, kind in outs),
        grid=(m // tm,), in_specs=in_specs,
        out_specs=tuple(row(n) if kind == "row" else pl.BlockSpec((8, n), lambda i: (0, 0)) for _, kind in outs),
        compiler_params=_params(dimension_semantics=("arbitrary",)),
    )(*args, *rows_in, *vecs_in, *deps)


def _rstd(x):
    return lax.rsqrt(jnp.mean(x * x, axis=-1, keepdims=True) + EPS)


def _norm_bwd(xh, r, t):
    return r * (t - xh * jnp.mean(xh * t, axis=-1, keepdims=True))


ROW_F32, ROW_BF16, SUM_F32 = (F32, "row"), (BF16, "row"), (F32, "sum")


def _then(epilogue, index, tb):
    def run(p, *args):
        vals = epilogue(p, *args[:-1])
        return (*vals, _dot(vals[index].astype(BF16), args[-1], 1, 1 if tb else 0))

    return run


def _ep_post_pre(p, h, g_post, g_pre):
    y = p.astype(BF16)
    yf = y.astype(F32)
    hn = h + yf * _rstd(yf) * g_post
    return y, hn, hn * _rstd(hn) * g_pre


_EP_POST_PRE_OUTS = [ROW_BF16, ROW_F32, ROW_BF16]


def _ep_final_loss(y, h, target, g_post):
    r = _rstd(y)
    yh = y * r
    err = h + yh * g_post - target
    dh = err * (1.0 / D)
    return _rowsum8(err * err), dh, _norm_bwd(yh, r, dh * g_post), _rowsum8(dh * yh)


def _ep_post_pre_bwd(du, dh_out, hn, y, g_post, g_pre):
    r2 = _rstd(hn)
    xh = hn * r2
    dh = dh_out + _norm_bwd(xh, r2, du * g_pre)
    yf = y.astype(F32)
    r1 = _rstd(yf)
    yh = yf * r1
    return dh, _norm_bwd(yh, r1, dh * g_post), _rowsum8(du * xh), _rowsum8(dh * yh)


_EP_POST_PRE_BWD_OUTS = [ROW_F32, ROW_BF16, SUM_F32, SUM_F32]


def _ep_pre_bwd(du, dh_out, x, g):
    r = _rstd(x)
    xh = x * r
    return dh_out + _norm_bwd(xh, r, du * g), _rowsum8(du * xh)


_EP_PRE_BWD_OUTS = [ROW_F32, SUM_F32]


def _prenorm(x, g, *, name, dep=None):
    t, d = x.shape
    tb = min(512, t)
    deps = [] if dep is None else [dep]

    def body(x_ref, g_ref, *rest):
        xf = x_ref[...]
        rest[-1][...] = (xf * _rstd(xf) * g_ref[...]).astype(BF16)

    return pl.pallas_call(
        body, name=name, out_shape=jax.ShapeDtypeStruct((t, d), BF16), grid=(t // tb,),
        in_specs=[pl.BlockSpec((tb, d), lambda i: (i, 0)), pl.BlockSpec((1, d), lambda i: (0, 0))]
        + [ANY_SPEC] * len(deps),
        out_specs=pl.BlockSpec((tb, d), lambda i: (i, 0)), compiler_params=_params(),
    )(x, g, *deps)


QB = 256


def _half_mask(shape, e):
    lane = lax.broadcasted_iota(jnp.int32, shape, len(shape) - 1)
    return (lane // 64) == e


def _place(kv):
    sw = pltpu.roll(kv, 64, 1)
    m0 = _half_mask(kv.shape, 0)
    return [[jnp.where(m0, kv, 0.0).astype(BF16), jnp.where(m0, 0.0, sw).astype(BF16)],
            [jnp.where(m0, sw, 0.0).astype(BF16), jnp.where(m0, 0.0, kv).astype(BF16)]]


SQ = 128
SK = 256


def _swa_valid(i, sb):
    qc = lax.broadcasted_iota(jnp.int32, (SQ, SK), 0) // CHUNK
    kc = lax.broadcasted_iota(jnp.int32, (SQ, SK), 1) // CHUNK - 2
    return (kc <= qc) & (qc <= kc + 2) & (4 * i + 2 * sb + kc >= 0)


def _swa_fwd(z, sinks, t, dep=None):
    nb = t // QB
    deps = [] if dep is None else [dep]

    def body(s_ref, q_ref, kp_ref, kc_ref, vp_ref, vc_ref, *rest):
        o_ref, lse_ref = rest[-2:]
        i = pl.program_id(0)
        kpl = _place(jnp.concatenate([kp_ref[...], kc_ref[...]], axis=0))
        vpl = _place(jnp.concatenate([vp_ref[...], vc_ref[...]], axis=0))
        lane = lax.broadcasted_iota(jnp.int32, (SQ, 128), 1)
        for sb in range(QB // SQ):
            rows, keys = slice(SQ * sb, SQ * (sb + 1)), slice(SQ * sb, SQ * sb + SK)
            valid = _swa_valid(i, sb)
            lse_out = jnp.zeros((SQ, 128), F32)
            for j in range(4):
                qp = q_ref[rows, 128 * j:128 * (j + 1)].astype(BF16)
                acc = jnp.zeros((SQ, 128), F32)
                for e in range(2):
                    h = 2 * j + e
                    kvh = h // 4
                    qm = jnp.where(_half_mask(qp.shape, e), qp, jnp.zeros_like(qp))
                    s = _dot(qm, kpl[kvh][e][keys], 1, 1) * 0.125
                    s = jnp.where(valid, s, NEG)
                    sink = s_ref[0, h]
                    m = jnp.maximum(jnp.max(s, axis=-1, keepdims=True), sink)
                    p = jnp.exp(s - m)
                    l = jnp.sum(p, axis=-1, keepdims=True) + jnp.exp(sink - m)
                    acc = acc + _dot(p.astype(BF16), vpl[kvh][e][keys], 1, 0) * (1.0 / l)
                    lse_out = jnp.where(lane == h, m + jnp.log(l), lse_out)
                o_ref[rows, 128 * j:128 * (j + 1)] = acc.astype(BF16)
            lse_ref[rows, :] = lse_out

    prev = lambda c: pl.BlockSpec((128, 128), lambda i: (jnp.maximum(2 * i - 1, 0), c))
    cur = lambda c: pl.BlockSpec((QB, 128), lambda i: (i, c))
    return pl.pallas_call(
        body, name="swa_fwd",
        out_shape=(jax.ShapeDtypeStruct((t, D), BF16), jax.ShapeDtypeStruct((t, 128), F32)),
        grid=(nb,),
        in_specs=[pl.BlockSpec(memory_space=pltpu.SMEM),
                  pl.BlockSpec((QB, SWA_W), lambda i: (i, 0)), prev(4), cur(4), prev(5), cur(5)]
        + [ANY_SPEC] * len(deps),
        out_specs=(pl.BlockSpec((QB, SWA_W), lambda i: (i, 0)), pl.BlockSpec((QB, 128), lambda i: (i, 0))),
        compiler_params=_params(),
    )(sinks, z, z, z, z, z, *deps)


def _swa_bwd(z, sinks, ymix, lse, dymix, t, dep=None):
    nb = t // QB
    deps = [] if dep is None else [dep]

    def body(s_ref, q_ref, kp_ref, kc_ref, vp_ref, vc_ref, o_ref, do_ref, l_ref, *rest):
        dq_ref, first_ref, second_ref, ds_ref, carry_ref = rest[len(deps):]
        i = pl.program_id(0)
        live = i < nb

        @pl.when(i == 0)
        def _():
            ds_ref[...] = jnp.zeros_like(ds_ref)
            carry_ref[...] = jnp.zeros_like(carry_ref)

        lane = lax.broadcasted_iota(jnp.int32, (8, 128), 1)
        kpl = _place(jnp.concatenate([kp_ref[...], kc_ref[...]], axis=0))
        vpl = _place(jnp.concatenate([vp_ref[...], vc_ref[...]], axis=0))
        nk = QB + 128
        qc = lax.broadcasted_iota(jnp.int32, (QB, nk), 0) // CHUNK
        kc = lax.broadcasted_iota(jnp.int32, (QB, nk), 1) // CHUNK - 2
        valid = (kc <= qc) & (qc <= kc + 2) & (4 * i + kc >= 0) & live
        lse_c = l_ref[...]
        dsink = jnp.zeros((8, 128), F32)
        dk_acc = [[jnp.zeros((128, nk), F32) for _ in range(2)] for _ in range(2)]
        dv_acc = [[jnp.zeros((128, nk), F32) for _ in range(2)] for _ in range(2)]
        dq = []
        for j in range(4):
            cols = slice(128 * j, 128 * (j + 1))
            qp = q_ref[:, cols].astype(BF16)
            dop = do_ref[:, cols]
            prod = dop.astype(F32) * o_ref[:, cols].astype(F32)
            acc = jnp.zeros((QB, 128), F32)
            for e in range(2):
                h = 2 * j + e
                kvh = h // 4
                hm = _half_mask(qp.shape, e)
                qm = jnp.where(hm, qp, jnp.zeros_like(qp))
                dom = jnp.where(hm, dop, jnp.zeros_like(dop))
                dd = jnp.sum(jnp.where(hm, prod, 0.0), axis=-1, keepdims=True)
                lse_h = lse_c[:, h:h + 1]
                s = _dot(qm, kpl[kvh][e], 1, 1) * 0.125
                p = jnp.where(valid, jnp.exp(s - lse_h), 0.0)
                dp = _dot(dom, vpl[kvh][e], 1, 1)
                ds = (p * (dp - dd) * 0.125).astype(BF16)
                acc = acc + _dot(ds, kpl[kvh][e], 1, 0)
                dk_acc[kvh][e] = dk_acc[kvh][e] + _dot(qm, ds, 0, 0)
                dv_acc[kvh][e] = dv_acc[kvh][e] + _dot(dom, p.astype(BF16), 0, 0)
                ps = jnp.where(live, jnp.exp(s_ref[0, h] - lse_h) * dd, 0.0)
                dsink = dsink - jnp.where(lane == h, _rowsum8(jnp.broadcast_to(ps, (QB, 128))), 0.0)
            dq.append(acc.astype(BF16))
        ds_ref[...] += dsink
        dk = (dk_acc[0][0] + dk_acc[1][1] + pltpu.roll(dk_acc[0][1] + dk_acc[1][0], 64, 0)).T
        dv = (dv_acc[0][0] + dv_acc[1][1] + pltpu.roll(dv_acc[0][1] + dv_acc[1][0], 64, 0)).T
        dkv = jnp.concatenate([dk, dv], axis=1)
        second_ref[...] = (carry_ref[...] + dkv[0:128]).astype(BF16)
        carry_ref[...] = dkv[256:384]

        @pl.when(live)
        def _():
            for j in range(4):
                dq_ref[:, 128 * j:128 * (j + 1)] = dq[j]
            first_ref[...] = dkv[128:256].astype(BF16)

    blk = lambda i: jnp.minimum(i, nb - 1)
    prev = lambda c: pl.BlockSpec((128, 128), lambda i: (jnp.maximum(2 * blk(i) - 1, 0), c))
    cur = lambda w, c: pl.BlockSpec((QB, w), lambda i: (blk(i), c))
    half = lambda index: pl.BlockSpec((128, 256), lambda i: (index(i), 0))
    return pl.pallas_call(
        body, name="swa_bwd",
        out_shape=(jax.ShapeDtypeStruct((t, SWA_W), BF16), jax.ShapeDtypeStruct((t // 2, 256), BF16),
                   jax.ShapeDtypeStruct((t // 2, 256), BF16), jax.ShapeDtypeStruct((8, 128), F32)),
        grid=(nb + 1,),
        in_specs=[pl.BlockSpec(memory_space=pltpu.SMEM),
                  cur(SWA_W, 0), prev(4), cur(128, 4), prev(5), cur(128, 5),
                  cur(SWA_W, 0), cur(SWA_W, 0), cur(128, 0)] + [ANY_SPEC] * len(deps),
        out_specs=(cur(SWA_W, 0), half(blk), half(lambda i: jnp.maximum(i - 1, 0)),
                   pl.BlockSpec((8, 128), lambda i: (0, 0))),
        scratch_shapes=[pltpu.VMEM((128, 256), F32)],
        compiler_params=_params(dimension_semantics=("arbitrary",)),
    )(sinks, z, z, z, z, z, ymix, dymix, lse, *deps)


HB = 256


def _lower_bound(lb_ref):
    a = lb_ref[...]
    a0, a1 = a[0:1], a[1:2]
    mx = jnp.maximum(a0, a1)
    e0, e1 = jnp.exp(a0 - mx), jnp.exp(a1 - mx)
    return e0 / (e0 + e1)


def _hgrn_cols(row_block):
    return [pl.BlockSpec((HB, 2 * HD), lambda j, c=base // (2 * HD) + p: (row_block(j), c))
            for base in (ZQH, ZFH, ZIH, ZGH) for p in range(2)]


NCH = HB // CHUNK


def _split3(x):
    hi = x.astype(BF16)
    r1 = x - hi.astype(F32)
    mid = r1.astype(BF16)
    return hi, mid, (r1 - mid.astype(F32)).astype(BF16)


def _blockdiag(lower):
    r = lax.broadcasted_iota(jnp.int32, (HB, HB), 0)
    c = lax.broadcasted_iota(jnp.int32, (HB, HB), 1)
    return (r // CHUNK == c // CHUNK) & ((c <= r) if lower else (c >= r))


def _chunk_sums(mask_bf16, x):
    return sum(_dot(mask_bf16, part, 1, 0) for part in _split3(x))


def _per_chunk_rows(x, row):
    w = x.shape[1]
    picked = x.reshape(NCH, CHUNK, w)[:, row:row + 1, :]
    return jnp.broadcast_to(picked, (NCH, CHUNK, w)).reshape(HB, w)


def _chunk_stack(x, chunk_of_row):
    return jnp.concatenate([jnp.where(chunk_of_row == c, x, jnp.zeros_like(x)) for c in range(NCH)], axis=1)


def _chunk_pick(x, chunk_of_row):
    w = x.shape[1] // NCH
    out = jnp.zeros((HB, w), x.dtype)
    for c in range(NCH):
        out = jnp.where(chunk_of_row == c, x[:, c * w:(c + 1) * w], out)
    return out


def _hgrn_local(q, f, kf, b):
    sq = _sig(q)
    qf = q * sq * (HD ** -0.5)
    b_mid = _per_chunk_rows(b, CHUNK // 2 - 1)
    b_last = _per_chunk_rows(b, CHUNK - 1)
    qm = qf * jnp.exp(b - b_mid)
    km = kf * jnp.exp(b_mid - b)
    kl = kf * jnp.exp(b_last - b)
    qb = qf * jnp.exp(b)
    return dict(sq=sq, b_mid=b_mid, b_last=b_last, qm=qm, km=km, kl=kl, qb=qb)


def _hgrn2_fwd(z, hgrn_lb, onorm, ymix, t, dep=None):
    nb = t // HB
    deps = [] if dep is None else [dep]

    def body(*refs):
        zq, zf, zi, zg = refs[0:2], refs[2:4], refs[4:6], refs[6:8]
        (lb_ref, on_ref), (y_ref, o_ref, sp_ref, st_ref) = refs[8:10], refs[-4:]

        @pl.when(pl.program_id(0) == 0)
        def _():
            st_ref[...] = jnp.zeros_like(st_ref)

        lb_all = _lower_bound(lb_ref)
        gn = on_ref[...]
        low = _blockdiag(True)
        low_b = low.astype(BF16)
        chunk_of_row = lax.broadcasted_iota(jnp.int32, (HB, HD), 0) // CHUNK
        for p in range(2):
            lbp = lb_all[:, 2 * HD * p:2 * HD * (p + 1)]
            fp = lbp + (1.0 - lbp) * _sig(zf[p][...])
            bp = _chunk_sums(low_b, jnp.log(fp))
            for e in range(2):
                h, ls = 2 * p + e, slice(e * HD, (e + 1) * HD)
                f = fp[:, ls]
                w = _hgrn_local(zq[p][:, ls], f, 1.0 - f, bp[:, ls])
                iv = zi[p][:, ls].astype(BF16)
                a = jnp.where(low, _dot(w["qm"].astype(BF16), w["km"].astype(BF16), 1, 1), 0.0)
                o = _dot(a.astype(BF16), iv, 1, 0)
                u = _dot(iv, _chunk_stack(w["kl"].astype(BF16), chunk_of_row), 0, 0)
                decay = jnp.exp(w["b_last"])
                st = st_ref[h]
                states = []
                for c in range(NCH):
                    sp_ref[h, c] = st
                    states.append(st.astype(BF16))
                    st = st * decay[c * CHUNK:c * CHUNK + 1] + u[:, c * HD:(c + 1) * HD]
                st_ref[h] = st
                inter = _dot(w["qb"].astype(BF16), jnp.concatenate(states, axis=0), 1, 1)
                o = o + _chunk_pick(inter, chunk_of_row)
                hs = slice(h * HD, (h + 1) * HD)
                o_ref[:, hs] = o
                gg = zg[p][:, ls]
                y_ref[:, hs] = (o * _rstd(o) * gn * (gg * _sig(gg))).astype(BF16)

    return pl.pallas_call(
        body, name="hgrn_fwd",
        out_shape=(jax.ShapeDtypeStruct((t, D), BF16), jax.ShapeDtypeStruct((t, HG_W), F32),
                   jax.ShapeDtypeStruct((4, t // CHUNK, HD, HD), F32)),
        grid=(nb,),
        in_specs=_hgrn_cols(lambda j: j) + [pl.BlockSpec((2, HG_W), lambda j: (0, 0)),
                                            pl.BlockSpec((1, HD), lambda j: (0, 0)), ANY_SPEC]
        + [ANY_SPEC] * len(deps),
        out_specs=(pl.BlockSpec((HB, HG_W), lambda j: (j, 1)),
                   pl.BlockSpec((HB, HG_W), lambda j: (j, 0)),
                   pl.BlockSpec((4, NCH, HD, HD), lambda j: (0, j, 0, 0))),
        scratch_shapes=[pltpu.VMEM((4, HD, HD), F32)],
        input_output_aliases={10: 0},
        compiler_params=_params(dimension_semantics=("arbitrary",)),
    )(*[z] * 8, hgrn_lb, onorm, ymix, *deps)


def _hgrn2_bwd(z, hgrn_lb, onorm, o_save, sprev, dymix, dza, t):
    nb = t // HB

    def body(*refs):
        zq, zf, zi, zg = refs[0:2], refs[2:4], refs[4:6], refs[6:8]
        (lb_ref, on_ref, o_ref, sp_ref, dy_ref, dqa_ref, first_ref, second_ref,
         dz_ref, dlb_ref, don_ref, dst_ref) = refs[8:]

        @pl.when(pl.program_id(0) == 0)
        def _():
            dst_ref[...] = jnp.zeros_like(dst_ref)
            dlb_ref[...] = jnp.zeros_like(dlb_ref)
            don_ref[...] = jnp.zeros_like(don_ref)

        dz_ref[:, 0:SWA_W] = dqa_ref[...]
        dz_ref[0:HB // 2, SWA_W:ZQH] = first_ref[...]
        dz_ref[HB // 2:HB, SWA_W:ZQH] = second_ref[...]
        lb_all = _lower_bound(lb_ref)
        gn = on_ref[...]
        low, upp = _blockdiag(True), _blockdiag(False)
        upp_b = upp.astype(BF16)
        low_b = low.astype(BF16)
        row = lax.broadcasted_iota(jnp.int32, (HB, HD), 0)
        chunk_of_row = row // CHUNK
        in_chunk = row % CHUNK
        for p in range(2):
            lbp = lb_all[:, 2 * HD * p:2 * HD * (p + 1)]
            sgp = _sig(zf[p][...])
            fp = lbp + (1.0 - lbp) * sgp
            bp = _chunk_sums(low_b, jnp.log(fp))
            db_pair, dkf_pair = [], []
            for e in range(2):
                h, ls, hs = 2 * p + e, slice(e * HD, (e + 1) * HD), slice((2 * p + e) * HD, (2 * p + e + 1) * HD)
                f = fp[:, ls]
                q = zq[p][:, ls]
                w = _hgrn_local(q, f, 1.0 - f, bp[:, ls])
                iv = zi[p][:, ls].astype(BF16)
                gg = zg[p][:, ls]
                o = o_ref[:, hs]
                dout = dy_ref[:, hs].astype(F32)
                sgg = _sig(gg)
                r = _rstd(o)
                oh = o * r
                dyn = dout * (gg * sgg)
                dz_ref[:, ZGH + h * HD:ZGH + (h + 1) * HD] = (
                    dout * oh * gn * (sgg * (1.0 + gg * (1.0 - sgg)))).astype(BF16)
                don_ref[...] += _rowsum8(dyn * oh)
                do = _norm_bwd(oh, r, dyn * gn).astype(BF16)
                qm, km, kl, qb = (w[n].astype(BF16) for n in ("qm", "km", "kl", "qb"))
                decay = jnp.exp(w["b_last"])
                grads_in = _dot(do, _chunk_stack(qb, chunk_of_row), 0, 0)
                dst = dst_ref[h]
                dstn, dd_rows = [None] * NCH, [None] * NCH
                for c in reversed(range(NCH)):
                    dstn[c] = dst.astype(BF16)
                    dd_rows[c] = jnp.sum(dst * sp_ref[h, c], axis=0, keepdims=True)
                    dst = dst * decay[c * CHUNK:c * CHUNK + 1] + grads_in[:, c * HD:(c + 1) * HD]
                dst_ref[h] = dst
                states = jnp.concatenate([sp_ref[h, c].astype(BF16) for c in range(NCH)], axis=0)
                dstn_all = jnp.concatenate(dstn, axis=0)
                dqb = _dot(_chunk_stack(do, chunk_of_row), states, 1, 0)
                at = jnp.where(upp, _dot(km, qm, 1, 1), 0.0)
                di = _dot(at.astype(BF16), do, 1, 0) + _chunk_pick(_dot(kl, dstn_all, 1, 1), chunk_of_row)
                dz_ref[:, ZIH + h * HD:ZIH + (h + 1) * HD] = di.astype(BF16)
                dkl = _dot(_chunk_stack(iv, chunk_of_row), dstn_all, 1, 0)
                da = jnp.where(low, _dot(do, iv, 1, 1), 0.0).astype(BF16)
                dat = jnp.where(upp, _dot(iv, do, 1, 1), 0.0).astype(BF16)
                dqm = _dot(da, km, 1, 0)
                dkm = _dot(dat, qm, 1, 0)
                b = bp[:, ls]
                e1, e2 = jnp.exp(b - w["b_mid"]), jnp.exp(w["b_mid"] - b)
                e3, e4 = jnp.exp(w["b_last"] - b), jnp.exp(b)
                dqf = dqm * e1 + dqb * e4
                dkf_pair.append(dkm * e2 + dkl * e3)
                t_qm, t_km, t_kl = dqm * w["qm"], dkm * w["km"], dkl * w["kl"]
                db = t_qm - t_km - t_kl + dqb * w["qb"]
                db_mid = jnp.sum((t_km - t_qm).reshape(NCH, CHUNK, HD), axis=1, keepdims=True)
                db_last = jnp.sum(t_kl.reshape(NCH, CHUNK, HD), axis=1, keepdims=True)
                db_last = db_last + jnp.stack(dd_rows, axis=0) * jnp.exp(
                    bp[:, ls].reshape(NCH, CHUNK, HD)[:, CHUNK - 1:CHUNK, :])
                spread = lambda v: jnp.broadcast_to(v, (NCH, CHUNK, HD)).reshape(HB, HD)
                db = (db + jnp.where(in_chunk == CHUNK // 2 - 1, spread(db_mid), 0.0)
                      + jnp.where(in_chunk == CHUNK - 1, spread(db_last), 0.0))
                db_pair.append(db)
                sq = w["sq"]
                dz_ref[:, ZQH + h * HD:ZQH + (h + 1) * HD] = (
                    dqf * (HD ** -0.5) * (sq * (1.0 + q * (1.0 - sq)))).astype(BF16)
            dlogf = _chunk_sums(upp_b, jnp.concatenate(db_pair, axis=1))
            dfv = dlogf / fp - jnp.concatenate(dkf_pair, axis=1)
            dz_ref[:, ZFH + 2 * HD * p:ZFH + 2 * HD * (p + 1)] = (dfv * (1.0 - lbp) * sgp * (1.0 - sgp)).astype(BF16)
            dlb_ref[:, 2 * HD * p:2 * HD * (p + 1)] += _rowsum8(dfv * (1.0 - sgp))

    rev = lambda j: nb - 1 - j
    return pl.pallas_call(
        body, name="hgrn_bwd",
        out_shape=(jax.ShapeDtypeStruct((t, D_IN), BF16), jax.ShapeDtypeStruct((8, HG_W), F32),
                   jax.ShapeDtypeStruct((8, HD), F32)),
        grid=(nb,),
        in_specs=_hgrn_cols(rev) + [pl.BlockSpec((2, HG_W), lambda j: (0, 0)), pl.BlockSpec((1, HD), lambda j: (0, 0)),
                                    pl.BlockSpec((HB, HG_W), lambda j: (rev(j), 0)),
                                    pl.BlockSpec((4, NCH, HD, HD), lambda j: (0, rev(j), 0, 0)),
                                    pl.BlockSpec((HB, HG_W), lambda j: (rev(j), 1)),
                                    pl.BlockSpec((HB, SWA_W), lambda j: (rev(j), 0)),
                                    pl.BlockSpec((HB // 2, 2 * KV_W), lambda j: (rev(j), 0)),
                                    pl.BlockSpec((HB // 2, 2 * KV_W), lambda j: (rev(j), 0))],
        out_specs=(pl.BlockSpec((HB, D_IN), lambda j: (rev(j), 0)), pl.BlockSpec((8, HG_W), lambda j: (0, 0)),
                   pl.BlockSpec((8, HD), lambda j: (0, 0))),
        scratch_shapes=[pltpu.VMEM((4, HD, HD), F32)],
        compiler_params=_params(dimension_semantics=("arbitrary",)),
    )(*[z] * 8, hgrn_lb, onorm, o_save, sprev, dymix, *dza)


XB = 512


def _xattn_fwd(q, k, v, wo, h, g_post, g_pre, t, dep=None):
    tb = min(XB, t)
    deps = [] if dep is None else [dep]

    def body(q_ref, k_ref, v_ref, wo_ref, h_ref, gp_ref, gn_ref, *rest):
        o_ref, y_ref, hn_ref, u_ref = rest[len(deps):]
        for hd in range(XH):
            cols = slice(XD * hd, XD * (hd + 1))
            s = _dot(q_ref[:, cols], k_ref[:, cols], 1, 1) * (XD ** -0.5)
            p = jnp.exp(s - jnp.max(s, axis=-1, keepdims=True))
            l = jnp.sum(p, axis=-1, keepdims=True)
            o_ref[:, cols] = (_dot(p.astype(BF16), v_ref[:, cols], 1, 0) * (1.0 / l)).astype(BF16)
        y, hn, u = _ep_post_pre(_dot(o_ref[...], wo_ref[...], 1, 0), h_ref[...], gp_ref[...], gn_ref[...])
        y_ref[...] = y
        hn_ref[...] = hn
        u_ref[...] = u.astype(BF16)

    row = pl.BlockSpec((tb, D), lambda i: (i, 0))
    whole = lambda a: pl.BlockSpec(a.shape, lambda i: (0,) * a.ndim, pipeline_mode=pl.Buffered(1))
    half = jax.ShapeDtypeStruct((t, D), BF16)
    return pl.pallas_call(
        body, name="xattn_fwd", out_shape=(half, half, jax.ShapeDtypeStruct((t, D), F32), half), grid=(t // tb,),
        in_specs=[row, whole(k), whole(v), whole(wo), row, whole(g_post), whole(g_pre)] + [ANY_SPEC] * len(deps),
        out_specs=(row, row, row, row), compiler_params=_params(),
    )(q, k, v, wo, h, g_post, g_pre, *deps)


def _xattn_bwd(q, k, v, do, wq, wout, dh_out, hn, y, g_post, g_pre, t):
    tb = min(XB, t)

    def body(q_ref, k_ref, v_ref, do_ref, wq_ref, wout_ref, dho_ref, hn_ref, y_ref, gp_ref, gn_ref,
             dq_ref, dk_ref, dv_ref, dh_ref, dyp_ref, dym_ref, dgn_ref, dgp_ref):
        @pl.when(pl.program_id(0) == 0)
        def _():
            dk_ref[...] = jnp.zeros_like(dk_ref)
            dv_ref[...] = jnp.zeros_like(dv_ref)
            dgn_ref[...] = jnp.zeros_like(dgn_ref)
            dgp_ref[...] = jnp.zeros_like(dgp_ref)

        for h in range(XH):
            cols = slice(XD * h, XD * (h + 1))
            qh, kh, vh, doh = q_ref[:, cols], k_ref[:, cols], v_ref[:, cols], do_ref[:, cols]
            s = _dot(qh, kh, 1, 1) * (XD ** -0.5)
            p = jnp.exp(s - jnp.max(s, axis=-1, keepdims=True))
            p = p * (1.0 / jnp.sum(p, axis=-1, keepdims=True))
            dp = _dot(doh, vh, 1, 1)
            ds = (p * (dp - jnp.sum(p * dp, axis=-1, keepdims=True)) * (XD ** -0.5)).astype(BF16)
            dq_ref[:, cols] = _dot(ds, kh, 1, 0).astype(BF16)
            dk_ref[:, cols] += _dot(ds, qh, 0, 0)
            dv_ref[:, cols] += _dot(p.astype(BF16), doh, 0, 0)
        du = _dot(dq_ref[...], wq_ref[...], 1, 1)
        dh, dyp, dgn, dgp = _ep_post_pre_bwd(du, dho_ref[...], hn_ref[...], y_ref[...], gp_ref[...], gn_ref[...])
        dh_ref[...] = dh
        dyp = dyp.astype(BF16)
        dyp_ref[...] = dyp
        dym_ref[...] = _dot(dyp, wout_ref[...], 1, 1).astype(BF16)
        dgn_ref[...] += dgn
        dgp_ref[...] += dgp

    row = pl.BlockSpec((tb, D), lambda i: (i, 0))
    mem = pl.BlockSpec(k.shape, lambda i: (0, 0))
    whole = lambda a: pl.BlockSpec(a.shape, lambda i: (0,) * a.ndim, pipeline_mode=pl.Buffered(1))
    acc = pl.BlockSpec((8, D), lambda i: (0, 0))
    half = jax.ShapeDtypeStruct((t, D), BF16)
    return pl.pallas_call(
        body, name="xattn_bwd",
        out_shape=(half, jax.ShapeDtypeStruct(k.shape, F32), jax.ShapeDtypeStruct(k.shape, F32),
                   jax.ShapeDtypeStruct((t, D), F32), half, half,
                   jax.ShapeDtypeStruct((8, D), F32), jax.ShapeDtypeStruct((8, D), F32)),
        grid=(t // tb,),
        in_specs=[row, whole(k), whole(v), row, whole(wq), whole(wout), row, row, row, whole(g_post), whole(g_pre)],
        out_specs=(row, mem, mem, row, row, row, acc, acc),
        compiler_params=_params(dimension_semantics=("arbitrary",)),
    )(q, k, v, do, wq, wout, dh_out, hn, y, g_post, g_pre)


def _mem_kv(mem, g_mem, wk, wv):
    def body(m_ref, g_ref, wk_ref, wv_ref, mn_ref, k_ref, v_ref):
        m_ = m_ref[...]
        mn = (m_ * _rstd(m_) * g_ref[...]).astype(BF16)
        mn_ref[...] = mn
        k_ref[...] = _dot(mn, wk_ref[...], 1, 0).astype(BF16)
        v_ref[...] = _dot(mn, wv_ref[...], 1, 0).astype(BF16)

    return pl.pallas_call(body, name="mem_kv", out_shape=(jax.ShapeDtypeStruct(mem.shape, BF16),) * 3,
                          compiler_params=_params())(mem, g_mem, wk, wv)


def _mem_kv_bwd(mn, mem, dk, dv, wk, wv, dep=None):
    deps = [] if dep is None else [dep]

    def body(mn_ref, m_ref, dk_ref, dv_ref, wk_ref, wv_ref, *rest):
        gk_ref, gv_ref, dg_ref = rest[len(deps):]
        mn = mn_ref[...]
        dkb, dvb = dk_ref[...].astype(BF16), dv_ref[...].astype(BF16)
        gk_ref[...] = _dot(mn, dkb, 0, 0).astype(BF16)
        gv_ref[...] = _dot(mn, dvb, 0, 0).astype(BF16)
        dmn = _dot(dkb, wk_ref[...], 1, 1) + _dot(dvb, wv_ref[...], 1, 1)
        m_ = m_ref[...]
        dg_ref[...] = _rowsum8(dmn * (m_ * _rstd(m_)))

    vmem = pl.BlockSpec(memory_space=pltpu.VMEM)
    return pl.pallas_call(
        body, name="mem_kv_bwd",
        out_shape=(jax.ShapeDtypeStruct(wk.shape, BF16), jax.ShapeDtypeStruct(wv.shape, BF16),
                   jax.ShapeDtypeStruct((8, D), F32)),
        in_specs=[vmem] * 6 + [ANY_SPEC] * len(deps), out_specs=(vmem,) * 3, compiler_params=_params(),
    )(mn, mem, dk, dv, wk, wv, *deps)


FB = 256


def _ffn_fwd_bwd(u, wgt, wut, wd, h, target, g_last, y_prev, g_post, g_pre, wo, t):
    tb = min(FB, t)

    def body(u_ref, wg_ref, wu_ref, wd_ref, h_ref, t_ref, gl_ref, yp_ref, gp_ref, gn_ref, wo_ref,
             a_ref, dy_ref, dg_ref, dup_ref, dh_ref, dyp_ref, do_ref, sq_ref, dgl_ref, dgn_ref, dgp_ref):
        @pl.when(pl.program_id(0) == 0)
        def _():
            for ref in (sq_ref, dgl_ref, dgn_ref, dgp_ref):
                ref[...] = jnp.zeros_like(ref)

        u_ = u_ref[...]
        g = _dot(u_, wg_ref[...], 1, 1)
        up = _dot(u_, wu_ref[...], 1, 1)
        sg = _sig(g)
        a = (g * sg * up).astype(BF16)
        a_ref[...] = a
        h_ = h_ref[...]
        sq, dh3, dy, dgl = _ep_final_loss(_dot(a, wd_ref[...], 1, 0), h_, t_ref[...], gl_ref[...])
        sq_ref[...] += sq
        dgl_ref[...] += dgl
        dy = dy.astype(BF16)
        dy_ref[...] = dy
        da = _dot(dy, wd_ref[...], 1, 1)
        dup = (da * g * sg).astype(BF16)
        dgate = (da * up * (sg * (1.0 + g * (1.0 - sg)))).astype(BF16)
        dup_ref[...] = dup
        dg_ref[...] = dgate
        du = _dot(dgate, wg_ref[...], 1, 0) + _dot(dup, wu_ref[...], 1, 0)
        dh, dyp, dgn, dgp = _ep_post_pre_bwd(du, dh3, h_, yp_ref[...], gp_ref[...], gn_ref[...])
        dh_ref[...] = dh
        dyp = dyp.astype(BF16)
        dyp_ref[...] = dyp
        do_ref[...] = _dot(dyp, wo_ref[...], 1, 1).astype(BF16)
        dgn_ref[...] += dgn
        dgp_ref[...] += dgp

    row = lambda w: pl.BlockSpec((tb, w), lambda i: (i, 0))
    whole = lambda a: pl.BlockSpec(a.shape, lambda i: (0,) * a.ndim, pipeline_mode=pl.Buffered(1))
    acc = pl.BlockSpec((8, D), lambda i: (0, 0))
    wide, half, sums = (jax.ShapeDtypeStruct((t, D_FF), BF16), jax.ShapeDtypeStruct((t, D), BF16),
                        jax.ShapeDtypeStruct((8, D), F32))
    return pl.pallas_call(
        body, name="ffn_fwd_bwd",
        out_shape=(wide, half, wide, wide, jax.ShapeDtypeStruct((t, D), F32), half, half, sums, sums, sums, sums),
        grid=(t // tb,),
        in_specs=[row(D), whole(wgt), whole(wut), whole(wd), row(D), row(D), whole(g_last), row(D), whole(g_post),
                  whole(g_pre), whole(wo)],
        out_specs=(row(D_FF), row(D), row(D_FF), row(D_FF), row(D), row(D), row(D), acc, acc, acc, acc),
        compiler_params=_params(dimension_semantics=("arbitrary",)),
    )(u, wgt, wut, wd, h, target, g_last, y_prev, g_post, g_pre, wo)


def _local_step(x, mem, target, fetch, sm, emit=None, first_dep=None, milestone=None):
    t = x.shape[0]
    w, gw = {}, {}

    def out(key, g):
        gw[key] = g
        return None if emit is None else emit(key, g)

    def tell(tag, value):
        return None if milestone is None else milestone(tag, value)
    u1 = _prenorm(x, sm["g_mix_pre"], name="prenorm_mix", dep=first_dep)
    w["winT"] = fetch("winT", u1)
    z = _mm_nt(u1, w["winT"], out_dtype=F32, tm=1024, tn=1408, name="mm_z")
    ymix, lse = _swa_fwd(z, sm["sinks"], t)
    ymix, o_h, sprev = _hgrn2_fwd(z, sm["hgrn_lb"], sm["hgrn_onorm"], ymix, t, dep=tell("swa", lse))
    for key in ("wout", "wq", "wk", "wv", "wo"):
        w[key] = fetch(key, ymix)
    y1, h1, u2, qx = _mm_rows([(ymix, w["wout"], False)], [x], [sm["g_mix_post"], sm["g_x_pre"], w["wq"]],
                              _then(_ep_post_pre, 2, False), _EP_POST_PRE_OUTS + [ROW_BF16], tm=512,
                              name="mm_y1_post_qx")
    mn, kx, vx = _mem_kv(mem, sm["g_mem"], w["wk"], w["wv"])
    ox, y2, h2, u3 = _xattn_fwd(qx, kx, vx, w["wo"], h1, sm["g_x_post"], sm["g_ffn_pre"], t, dep=tell("kv", kx))
    for key in ("wgT", "wuT", "wd"):
        w[key] = fetch(key, u3)
    act, dy3, dgate, dup, dh2, dy2, dox, sq, dg_ffn_post, dg_ffn_pre, dg_x_post = _ffn_fwd_bwd(
        u3, w["wgT"], w["wuT"], w["wd"], h2, target, sm["g_ffn_post"], y2, sm["g_x_post"], sm["g_ffn_pre"], w["wo"], t)
    dep = out("wd", *_mm_tn([act], [dy3], name="mm_gwd"))
    gwg, gwu = _mm_tn([dgate, dup], [u3], name="mm_gwg_gwu", dep=dep)
    out("wgT", gwg)
    dep = out("wuT", gwu)
    out("wo", *_mm_tn([ox], [dy2], name="mm_gwo", dep=dep))
    dqx, dkx, dvx, dh1, dy1, dymix, dg_x_pre, dg_mix_post = _xattn_bwd(
        qx, kx, vx, dox, w["wq"], w["wout"], dh2, h1, y1, sm["g_mix_post"], sm["g_x_pre"], t)
    gwq, gwout = _mm_tn([u2, ymix], [dqx, dy1], name="mm_gwq_gwout")
    gwk, gwv, dg_mem = _mem_kv_bwd(mn, mem, dkx, dvx, w["wk"], w["wv"])
    for key, g in (("wq", gwq), ("wout", gwout), ("wk", gwk), ("wv", gwv)):
        dep = out(key, g)
    *dza, dsinks = _swa_bwd(z, sm["sinks"], ymix, lse, dymix, t, dep=dep)
    dz, dlb, donorm = _hgrn2_bwd(z, sm["hgrn_lb"], sm["hgrn_onorm"], o_h, sprev, dymix, dza, t)
    dep = out("winT", *_mm_tn([dz], [u1], name="mm_gwin"))
    grad_x, dg_mix_pre = _mm_rows([(dz, w["winT"], False)], [dh1, x], [sm["g_mix_pre"]], _ep_pre_bwd,
                                  _EP_PRE_BWD_OUTS, tm=512, name="mm_du1_pre_bwd", dep=dep)
    parts = dict(g_mix_pre=dg_mix_pre, g_mix_post=dg_mix_post, g_mem=dg_mem, g_x_pre=dg_x_pre,
                 g_x_post=dg_x_post, g_ffn_pre=dg_ffn_pre, g_ffn_post=dg_ffn_post,
                 hgrn_onorm=donorm, hgrn_lb=dlb, sinks=dsinks, sq=sq)
    return grad_x, gw, parts


def _position():
    return lax.axis_index("x"), lax.axis_index("y"), lax.axis_index("c")


def _peer(pos, k):
    x, y, c = pos
    return (1 - x if k & 4 else x, 1 - y if k & 2 else y, 1 - c if k & 1 else c)


def _linear(pos):
    x, y, c = pos
    return 4 * x + 2 * y + c


HBM_SPEC = pl.BlockSpec(memory_space=pltpu.HBM)
SEM_SPEC = pl.BlockSpec(memory_space=pltpu.SEMAPHORE)
DATAFLOW = pltpu.SideEffectType.DATAFLOW_SIDE_EFFECTING
SEND_ORDER = (1, 2, 4, 3, 5, 6, 7)


def _in_hbm(a):
    return pltpu.with_memory_space_constraint(a, pltpu.HBM)


def _prepare_weights(shards, *, name, dep=None):
    n = len(shards)
    deps = [] if dep is None else [dep]

    def body(*refs):
        ins, (outs, lands, sem) = refs[:n], (refs[-2 * n - 1:-n - 1], refs[-n - 1:-1], refs[-1])
        me_lin = _linear(_position())
        copies = []
        for a in range(n):
            r = ins[a].shape[0]
            outs[a][...] = ins[a][...].astype(BF16)
            copies.append(pltpu.make_async_copy(outs[a], lands[a].at[pl.ds(me_lin * r, r), :], sem.at[a]))
            copies[-1].start()
        for cp in copies:
            cp.wait()

    vmem = pl.BlockSpec(memory_space=pltpu.VMEM)
    res = pl.pallas_call(
        body, name=name,
        out_shape=tuple(jax.ShapeDtypeStruct(s.shape, BF16) for s in shards)
        + tuple(jax.ShapeDtypeStruct((N_DEV * s.shape[0], s.shape[1]), BF16) for s in shards),
        in_specs=[vmem] * n + [ANY_SPEC] * len(deps), out_specs=tuple([vmem] * n + [ANY_SPEC] * n),
        scratch_shapes=[pltpu.SemaphoreType.DMA((n,))], compiler_params=_params(),
    )(*shards, *deps)
    return res[:n], res[n:]


def _copies_start(arrays, plan, n, *, name):
    na = len(arrays)

    def body(*refs):
        ins, send_sems, recv_sems = refs[:na], refs[na], refs[na + 1]
        me = _position()
        for j in range(n):
            src, dst, peer, _ = plan(ins, me, j)
            pltpu.make_async_remote_copy(src_ref=src, dst_ref=dst, send_sem=send_sems.at[j], recv_sem=recv_sems.at[j],
                                         device_id=peer, device_id_type=MESH).start()

    return pl.pallas_call(
        body, name=name,
        out_shape=(pltpu.SemaphoreType.DMA((n,)), pltpu.SemaphoreType.DMA((n,)))
        + tuple(pltpu.HBM(a.shape, a.dtype) for a in arrays),
        in_specs=(HBM_SPEC,) * na, out_specs=(SEM_SPEC, SEM_SPEC) + (HBM_SPEC,) * na,
        input_output_aliases={i: 2 + i for i in range(na)},
        compiler_params=pltpu.CompilerParams(has_side_effects=DATAFLOW),
    )(*[_in_hbm(a) for a in arrays])


def _copies_wait(send_sems, recv_sems, arrays, plan, n, after, *, name):
    na = len(arrays)

    def body(*refs):
        ins, send_sems, recv_sems = refs[:na], refs[na], refs[na + 1]
        me = _position()
        for j in range(n):
            src, _, peer, landed = plan(ins, me, j)
            copy = pltpu.make_async_remote_copy(src_ref=src, dst_ref=landed, send_sem=send_sems.at[j],
                                                recv_sem=recv_sems.at[j], device_id=peer, device_id_type=MESH)
            copy.wait_send()
            copy.wait_recv()

    return pl.pallas_call(
        body, name=name, out_shape=tuple(pltpu.HBM(a.shape, a.dtype) for a in arrays),
        in_specs=(HBM_SPEC,) * na + (SEM_SPEC, SEM_SPEC, ANY_SPEC), out_specs=(HBM_SPEC,) * na,
        input_output_aliases={i: i for i in range(na)},
        compiler_params=pltpu.CompilerParams(has_side_effects=DATAFLOW),
    )(*arrays, send_sems, recv_sems, after)


SAME_CORE = (2, 4, 6)


class _TwoLevelGather:
    def __init__(self, shards, lands, *, name):
        n = self.n = len(shards)
        self.name = name
        first_peers = (1,) + SAME_CORE

        def rows(ref, pos):
            r = ref.shape[0] // N_DEV
            return ref.at[pl.ds(_linear(pos) * r, r), :]

        def first(refs, me, j):
            a, peer = j // 4, _peer(me, first_peers[j % 4])
            return refs[a], rows(refs[n + a], me), peer, rows(refs[n + a], peer)

        def second(refs, me, j):
            a, sibling = j // 3, _peer(me, 1)
            mine = rows(refs[a], _peer(me, SAME_CORE[j % 3]))
            return mine, mine, sibling, rows(refs[a], _peer(sibling, SAME_CORE[j % 3]))

        self._first, self._second = first, second
        self._flight = _copies_start(list(shards) + list(lands), first, 4 * n, name=name + "_send")
        self.dep = self._flight[2]

    def pass_on(self, after):
        send1, recv1, *arrays = self._flight
        arrays = _copies_wait(send1, recv1, arrays, self._first, 4 * self.n, after, name=self.name + "_recv")
        self._flight = _copies_start(list(arrays[self.n:]), self._second, 3 * self.n, name=self.name + "_pass")
        return self._flight[2]

    def finish(self, after):
        send2, recv2, *lands = self._flight
        return _copies_wait(send2, recv2, lands, self._second, 3 * self.n, after, name=self.name + "_pass_recv")


def _exchange_start(gs, *, name):
    n = len(gs)
    rows = [g.shape[0] // N_DEV for g in gs]
    lands = [lax.empty((N_DEV - 1, r, g.shape[1]), g.dtype) for g, r in zip(gs, rows)]

    def body(*refs):
        g_refs, land_refs = refs[:n], refs[n:2 * n]
        send_sems, recv_sems = refs[2 * n:3 * n], refs[3 * n:4 * n]
        me = _position()
        for a in range(n):
            for k in SEND_ORDER:
                peer = _peer(me, k)
                pltpu.make_async_remote_copy(
                    src_ref=g_refs[a].at[pl.ds(_linear(peer) * rows[a], rows[a]), :],
                    dst_ref=land_refs[a].at[k - 1],
                    send_sem=send_sems[a].at[k - 1], recv_sem=recv_sems[a].at[k - 1],
                    device_id=peer, device_id_type=MESH).start()

    res = pl.pallas_call(
        body, name=name,
        out_shape=tuple(pltpu.SemaphoreType.DMA((N_DEV - 1,)) for _ in range(2 * n))
        + tuple(pltpu.HBM(a.shape, a.dtype) for a in gs + lands),
        in_specs=(HBM_SPEC,) * (2 * n), out_specs=(SEM_SPEC,) * (2 * n) + (HBM_SPEC,) * (2 * n),
        input_output_aliases={i: 2 * n + i for i in range(2 * n)},
        compiler_params=pltpu.CompilerParams(has_side_effects=DATAFLOW),
    )(*[_in_hbm(a) for a in gs + lands])
    return [(res[a], res[n + a], res[2 * n + a], res[3 * n + a]) for a in range(n)]


def _exchange_wait(send_sems, recv_sems, g_thru, land_thru, after, *, name):
    r = land_thru.shape[1]

    def body(g_ref, land_ref, send_sems, recv_sems, after_ref, g_dead, got_ref):
        del after_ref, g_dead, got_ref
        me = _position()
        for k in SEND_ORDER:
            peer = _peer(me, k)
            copy = pltpu.make_async_remote_copy(
                src_ref=g_ref.at[pl.ds(_linear(peer) * r, r), :], dst_ref=land_ref.at[k - 1],
                send_sem=send_sems.at[k - 1], recv_sem=recv_sems.at[k - 1],
                device_id=peer, device_id_type=MESH)
            copy.wait_send()
            copy.wait_recv()

    return pl.pallas_call(
        body, name=name,
        out_shape=(pltpu.HBM(g_thru.shape, g_thru.dtype), pltpu.HBM(land_thru.shape, land_thru.dtype)),
        in_specs=(HBM_SPEC, HBM_SPEC, SEM_SPEC, SEM_SPEC, pl.BlockSpec(memory_space=pl.ANY)),
        out_specs=(HBM_SPEC, HBM_SPEC), input_output_aliases={0: 0, 1: 1},
        compiler_params=pltpu.CompilerParams(has_side_effects=DATAFLOW),
    )(g_thru, land_thru, send_sems, recv_sems, after)


ADAMW_TILE_ROWS = 256


def _adamw_math(w, g, m, v):
    m = B1 * m + (1.0 - B1) * g
    v = B2 * v + (1.0 - B2) * (g * g)
    delta = -LR * ((m / C1) / (jnp.sqrt(v / C2) + AEPS) + WD * w)
    return delta, m, v


def _sum_adamw(items, *, name):
    n = len(items)
    r, d = items[0][2].shape
    assert all(it[2].shape == (r, d) for it in items)
    rc = r // 2 if r > ADAMW_TILE_ROWS else r
    tiles = [(a, r0) for a in range(n) for r0 in range(0, r, rc)]
    n_in, n_out = 5, 4

    def body(*refs):
        ins, outs = refs[:n_in * n], refs[n_in * n:(n_in + n_out) * n]
        land_v, own_v, f32_v, sems = refs[(n_in + n_out) * n:]
        me_lin = _linear(_position())

        def loads(j):
            a, r0 = tiles[j]
            g_all, land, w, m, v = ins[n_in * a:n_in * a + n_in]
            rows = pl.ds(r0, rc)
            pairs = [(land.at[:, rows, :], land_v.at[j]), (g_all.at[pl.ds(me_lin * r + r0, rc), :], own_v.at[j]),
                     (w.at[rows, :], f32_v.at[j, 0]), (m.at[rows, :], f32_v.at[j, 1]), (v.at[rows, :], f32_v.at[j, 2])]
            return [pltpu.make_async_copy(src, dst, sems.at[j, i]) for i, (src, dst) in enumerate(pairs)]

        def stores(j):
            a, r0 = tiles[j]
            return [pltpu.make_async_copy(f32_v.at[j, 3 + i], outs[n_out * a + i].at[pl.ds(r0, rc), :],
                                          sems.at[j, n_in + i]) for i in range(n_out)]

        for j in range(len(tiles)):
            for cp in loads(j):
                cp.start()
        for j in range(len(tiles)):
            for cp in loads(j):
                cp.wait()
            g = land_v[j, 0].astype(F32)
            for s in range(1, N_DEV - 1):
                g = g + land_v[j, s].astype(F32)
            g = own_v[j].astype(F32) + g
            f32_v[j, 3] = g
            f32_v[j, 4], f32_v[j, 5], f32_v[j, 6] = _adamw_math(f32_v[j, 0], g, f32_v[j, 1], f32_v[j, 2])
            for cp in stores(j):
                cp.start()
        for j in range(len(tiles)):
            for cp in stores(j):
                cp.wait()

    nt = len(tiles)
    res = pl.pallas_call(
        body, name=name,
        out_shape=tuple(jax.ShapeDtypeStruct((r, d), F32) for _ in range(n_out * n)),
        in_specs=[ANY_SPEC] * (n_in * n), out_specs=(ANY_SPEC,) * (n_out * n),
        scratch_shapes=[pltpu.VMEM((nt, N_DEV - 1, rc, d), BF16), pltpu.VMEM((nt, rc, d), BF16),
                        pltpu.VMEM((nt, 3 + n_out, rc, d), F32), pltpu.SemaphoreType.DMA((nt, n_in + n_out))],
        compiler_params=_params(),
    )(*[a for it in items for a in it])
    return [res[n_out * a:n_out * a + n_out] for a in range(n)]


SMALL = ("g_mix_pre", "g_mix_post", "g_mem", "g_x_pre", "g_x_post", "g_ffn_pre", "g_ffn_post",
         "hgrn_onorm", "hgrn_lb", "sinks")
SMALL_W = dict(hgrn_onorm=HD, hgrn_lb=HG_W, sinks=8)
SQ_ROW = len(SMALL)
PACK_ROWS = 16


def _small_pack(parts):
    ns = len(SMALL)

    def body(*refs):
        part, mine, slots, sem = refs[:ns + 1], refs[ns + 1], refs[ns + 2], refs[ns + 3]
        mine[...] = jnp.zeros((PACK_ROWS, D), F32)
        for r, name in enumerate(SMALL):
            wd = SMALL_W.get(name, D)
            mine[r:r + 1, 0:wd] = jnp.sum(part[r][...], axis=0, keepdims=True)[:, 0:wd]
        sq = jnp.sum(part[ns][...]) * (0.5 / D)
        mine[SQ_ROW:SQ_ROW + 1, :] = jnp.full((1, D), sq, F32)
        own = pltpu.make_async_copy(mine, slots.at[_linear(_position())], sem)
        own.start()
        own.wait()

    vmem = pl.BlockSpec(memory_space=pltpu.VMEM)
    return pl.pallas_call(
        body, name="small_pack",
        out_shape=(jax.ShapeDtypeStruct((PACK_ROWS, D), F32), jax.ShapeDtypeStruct((N_DEV, PACK_ROWS, D), F32)),
        in_specs=[vmem] * (ns + 1), out_specs=(vmem, ANY_SPEC),
        scratch_shapes=[pltpu.SemaphoreType.DMA(())], compiler_params=_params(),
    )(*[parts[n] for n in SMALL], parts["sq"])


def _small_exchange(mine, slots):
    def plan(refs, me, j):
        peer = _peer(me, j + 1)
        return refs[0], refs[1].at[_linear(me)], peer, refs[1].at[_linear(peer)]

    send, recv, mine1, slots1 = _copies_start([mine, slots], plan, N_DEV - 1, name="small_send")
    return lambda after: _copies_wait(send, recv, [mine1, slots1], plan, N_DEV - 1, after, name="small_recv")[1]


def _small_update(slots, sm, m_sm, v_sm):
    ns = len(SMALL)

    def body(*refs):
        tot = refs[0][0]
        for s in range(1, N_DEV):
            tot = tot + refs[0][s]
        w_refs, m_refs, v_refs = refs[1:ns + 1], refs[ns + 1:2 * ns + 1], refs[2 * ns + 1:3 * ns + 1]
        outs = refs[3 * ns + 1:]
        loss_ref = outs[0]
        g_out, d_out = outs[1:ns + 1], outs[ns + 1:2 * ns + 1]
        nm_out, nv_out = outs[2 * ns + 1:3 * ns + 1], outs[3 * ns + 1:4 * ns + 1]
        loss_ref[...] = tot[SQ_ROW:SQ_ROW + 1, 0:1]
        for r, name in enumerate(SMALL):
            wd = SMALL_W.get(name, D)
            g = tot[r:r + 1, 0:wd]
            w = w_refs[r][...]
            if name == "hgrn_lb":
                mx = jnp.maximum(w[0:1], w[1:2])
                e0, e1 = jnp.exp(w[0:1] - mx), jnp.exp(w[1:2] - mx)
                lb0 = e0 / (e0 + e1)
                g0 = g * lb0 * (1.0 - lb0)
                for i, gi in enumerate((g0, -g0)):
                    d, nm, nv = _adamw_math(w[i:i + 1], gi, m_refs[r][i:i + 1, :], v_refs[r][i:i + 1, :])
                    g_out[r][i:i + 1, :] = gi
                    d_out[r][i:i + 1, :], nm_out[r][i:i + 1, :], nv_out[r][i:i + 1, :] = d, nm, nv
            else:
                d, nm, nv = _adamw_math(w, g, m_refs[r][...], v_refs[r][...])
                g_out[r][...] = g
                d_out[r][...], nm_out[r][...], nv_out[r][...] = d, nm, nv

    shapes = [jax.ShapeDtypeStruct(sm[n].shape, F32) for n in SMALL]
    res = pl.pallas_call(
        body, name="small_update", out_shape=tuple([jax.ShapeDtypeStruct((1, 1), F32)] + shapes * 4),
        compiler_params=_params(),
    )(slots, *[sm[n] for n in SMALL], *[m_sm[n] for n in SMALL], *[v_sm[n] for n in SMALL])
    groups = [dict(zip(SMALL, res[1 + i * ns:1 + (i + 1) * ns])) for i in range(4)]
    return res[0], groups[0], groups[1], groups[2], groups[3]


BIG = ("w_in", "w_gate", "w_up", "w_down", "w_out", "wq_x", "wk_x", "wv_x", "wo_x")
BIG_KEY = dict(w_in="winT", w_gate="wgT", w_up="wuT", w_down="wd", w_out="wout", wq_x="wq", wk_x="wk",
               wv_x="wv", wo_x="wo")
TRANSPOSED = ("w_in", "w_gate", "w_up")
WEIGHTS = ("w_in", "sinks", "hgrn_lb", "hgrn_onorm", "w_out", "g_mix_pre", "g_mix_post", "g_mem", "g_x_pre",
           "g_x_post", "wq_x", "wk_x", "wv_x", "wo_x", "g_ffn_pre", "g_ffn_post", "w_gate", "w_up", "w_down")


def kernel(x, mem, w_in, sinks, hgrn_lb, hgrn_onorm, w_out, g_mix_pre, g_mix_post, g_mem, g_x_pre, g_x_post, wq_x, wk_x, wv_x, wo_x, g_ffn_pre, g_ffn_post, w_gate, w_up, w_down, loss_target, m_w_in, m_sinks, m_hgrn_lb, m_hgrn_onorm, m_w_out, m_g_mix_pre, m_g_mix_post, m_g_mem, m_g_x_pre, m_g_x_post, m_wq_x, m_wk_x, m_wv_x, m_wo_x, m_g_ffn_pre, m_g_ffn_post, m_w_gate, m_w_up, m_w_down, v_w_in, v_sinks, v_hgrn_lb, v_hgrn_onorm, v_w_out, v_g_mix_pre, v_g_mix_post, v_g_mem, v_g_x_pre, v_g_x_post, v_wq_x, v_wk_x, v_wv_x, v_wo_x, v_g_ffn_pre, v_g_ffn_post, v_w_gate, v_w_up, v_w_down):
    given = dict(locals())
    wts = {n: given[n] for n in WEIGHTS}
    ms = {n: given["m_" + n] for n in WEIGHTS}
    vs = {n: given["v_" + n] for n in WEIGHTS}

    def mat(a, name):
        a = a[0]
        return a.T if name in TRANSPOSED else a

    groups = (("w_in",), ("w_out", "wq_x", "wk_x", "wv_x", "wo_x"), ("w_gate", "w_up", "w_down"))
    gathers = []
    first_dep = None
    for tag, group in zip(("w_in", "w_attn", "w_ffn"), groups):
        shards, lands = _prepare_weights([mat(wts[n], n) for n in group], name="prepare_" + tag, dep=first_dep)
        gathers.append(_TwoLevelGather(shards, lands, name=tag))
        first_dep = gathers[-1].dep
    name_of = {k: n for n, k in BIG_KEY.items()}
    gathered = {}

    def milestone(tag, value):
        return gathers[{"swa": 1, "kv": 2}[tag]].pass_on(value)

    def fetch(key, after):
        name = name_of[key]
        if name not in gathered:
            g = [i for i, group in enumerate(groups) if name in group][0]
            if g == 0:
                gathers[0].pass_on(after)
            gathered.update(zip(groups[g], gathers[g].finish(after)))
        return gathered[name]

    sm = {n: wts[n] for n in SMALL}
    started, held = {}, {}
    send_with = {k: group for group in (("wgT", "wuT"), ("wo", "wq", "wout", "wk", "wv")) for k in group}

    def emit(key, g):
        held[key] = g
        group = send_with.get(key, (key,))
        if key != group[-1]:
            return None
        flights = _exchange_start([held[k] for k in group], name="grad_send_" + name_of[group[0]])
        started.update({name_of[k]: f for k, f in zip(group, flights)})
        return flights[-1][2]

    grad_x, _, parts = _local_step(x[0], mem[0], loss_target[0], fetch, sm, emit, first_dep=first_dep, milestone=milestone)
    small_finish = _small_exchange(*_small_pack(parts))
    grads, deltas, new_m, new_v = {}, {}, {}, {}
    after = grad_x
    for group in (("w_down",), ("w_gate", "w_up"), ("wo_x", "wq_x", "wk_x", "wv_x", "w_out"), ("w_in",)):
        items = []
        for n in group:
            g_all, land = _exchange_wait(*started[n], after, name="grad_recv_" + n)
            items.append((g_all, land, mat(wts[n], n), mat(ms[n], n), mat(vs[n], n)))
            after = land
        for n, res in zip(group, _sum_adamw(items, name="adamw_" + group[0])):
            after = res[1]
            if n in TRANSPOSED:
                res = [a.T for a in res]
            grads[n], deltas[n], new_m[n], new_v[n] = [a[None] for a in res]
    loss, g_s, d_s, m_s, v_s = _small_update(small_finish(after), sm, {n: ms[n] for n in SMALL},
                                             {n: vs[n] for n in SMALL})
    grads.update(g_s), deltas.update(d_s), new_m.update(m_s), new_v.update(v_s)
    return (loss[0, 0], grad_x[None], *[grads[n] for n in WEIGHTS], *[deltas[n] for n in WEIGHTS],
            *[new_m[n] for n in WEIGHTS], *[new_v[n] for n in WEIGHTS])
```

```python
import functools

import jax
import jax.numpy as jnp
from jax import lax
from jax.experimental import pallas as pl
from jax.experimental.pallas import tpu as pltpu

F32 = jnp.float32
BF16 = jnp.bfloat16

D = 1024
D_IN = 2816
D_FF = 2816
CHUNK = 64
SWA_W = 512
KV_W = 128
HG_W = 512
HD = 128
ZQH, ZFH, ZIH, ZGH = 768, 1280, 1792, 2304
XH, XD = 4, 256
EPS = 1e-6
NEG = -1e30
N_DEV = 8
MESH = pl.DeviceIdType.MESH

LR, B1, B2, AEPS, WD, STEP = 0.001, 0.9, 0.999, 1e-08, 0.01, 10
C1 = 1.0 - B1 ** STEP
C2 = 1.0 - B2 ** STEP

VMEM_LIMIT = 56 * 1024 * 1024


def _params(**kw):
    return pltpu.CompilerParams(vmem_limit_bytes=VMEM_LIMIT, **kw)


def _sig(x):
    return 1.0 / (1.0 + jnp.exp(-x))


def _rowsum8(x):
    r, w = x.shape
    return jnp.sum(x.reshape(r // 8, 8, w), axis=0)


def _dot(a, b, ca, cb, precision=None):
    return lax.dot_general(a, b, (((ca,), (cb,)), ((), ())), preferred_element_type=F32,
                           precision=precision)


ANY_SPEC = pl.BlockSpec(memory_space=pl.ANY)


def _mm_nt(a, b, *, out_dtype, tm, tn, name):
    (m, k), n = a.shape, b.shape[0]
    tm, tn = min(tm, m), min(tn, n)
    assert a.dtype == BF16 and b.dtype == BF16 and m % tm == 0 and n % tn == 0, (name, m, n, tm, tn)

    def body(a_ref, b_ref, o_ref):
        o_ref[...] = _dot(a_ref[...], b_ref[...], 1, 1).astype(out_dtype)

    return pl.pallas_call(
        body, name=name, out_shape=jax.ShapeDtypeStruct((m, n), out_dtype), grid=(n // tn, m // tm),
        in_specs=[pl.BlockSpec((tm, k), lambda j, i: (i, 0)), pl.BlockSpec((tn, k), lambda j, i: (j, 0))],
        out_specs=pl.BlockSpec((tm, tn), lambda j, i: (i, j)),
        compiler_params=_params(dimension_semantics=("parallel", "parallel")),
    )(a, b)


TN_FIRST = 256
TN_REST = 1152
TN_SLICES = 4


def _mm_tn(a_list, b_list, *, name, dep=None):
    na, nbd, (k, m), n = len(a_list), len(b_list), a_list[0].shape, b_list[0].shape[1]
    assert nbd in (1, na) and all(b.dtype == BF16 and b.shape == (k, n) for b in b_list)
    assert all(a.dtype == BF16 and a.shape == (k, m) for a in a_list)
    nbv = min(2, nbd)
    widths = [TN_FIRST, TN_FIRST]
    while sum(widths) < m:
        widths.append(min(TN_REST, m - sum(widths)))
    starts = [sum(widths[:i]) for i in range(len(widths))]
    assert sum(widths) == m
    per = len(widths)
    nb = na * per
    ks = k // TN_SLICES
    ahead = 2
    assert ahead < per
    deps = [] if dep is None else [dep]

    def body(*refs):
        a_hbm, b_hbm, rest = refs[:na], refs[na:na + nbd], refs[na + nbd + len(deps):]
        o_hbm, b_v, rest = rest[:na], rest[na:na + nbv], rest[na + nbv:]
        a_v, o_v, sems = rest[:per], rest[per:-1], rest[-1]
        sliced = []
        for c in range(TN_SLICES):
            rows = pl.ds(c * ks, ks)
            sliced.append((pltpu.make_async_copy(b_hbm[0].at[rows, :], b_v[0].at[rows, :], sems.at[2 * c]),
                           pltpu.make_async_copy(a_hbm[0].at[rows, pl.ds(0, widths[0])], a_v[0].at[rows, :],
                                                 sems.at[2 * c + 1])))
        base = 2 * TN_SLICES - 1
        cols = [pl.ds(starts[i % per], widths[i % per]) for i in range(nb)]
        loads = [None] + [pltpu.make_async_copy(a_hbm[i // per].at[:, cols[i]], a_v[i % per], sems.at[base + i])
                          for i in range(1, nb)]
        stores = [pltpu.make_async_copy(o_v[i % per], o_hbm[i // per].at[cols[i], :], sems.at[base + nb + i])
                  for i in range(nb)]
        next_b = [None] + [pltpu.make_async_copy(b_hbm[j], b_v[j % nbv], sems.at[base + 2 * nb + j])
                           for j in range(1, nbd)]
        for pair in sliced:
            for cp in pair:
                cp.start()
        for i in range(1, 1 + ahead):
            loads[i].start()
        if nbd > 1:
            next_b[1].start()
        acc = None
        for c, pair in enumerate(sliced):
            for cp in pair:
                cp.wait()
            p = _dot(a_v[0][c * ks:(c + 1) * ks, :], b_v[0][c * ks:(c + 1) * ks, :], 0, 0)
            acc = p if acc is None else acc + p
        o_v[0][...] = acc.astype(BF16)
        stores[0].start()
        for i in range(1, nb):
            j = (i // per) % nbd
            if nbd > 1 and i % per == 0:
                next_b[j].wait()
                if j + 1 < nbd:
                    next_b[j + 1].start()
            loads[i].wait()
            if i + ahead < nb:
                loads[i + ahead].start()
            if i >= per:
                stores[i - per].wait()
            o_v[i % per][...] = _dot(a_v[i % per][...], b_v[j % nbv][...], 0, 0).astype(BF16)
            stores[i].start()
        for cp in stores[nb - per:]:
            cp.wait()

    return pl.pallas_call(
        body, name=name, out_shape=tuple(jax.ShapeDtypeStruct((m, n), BF16) for _ in a_list),
        in_specs=[ANY_SPEC] * (na + nbd + len(deps)), out_specs=(ANY_SPEC,) * na,
        scratch_shapes=[pltpu.VMEM((k, n), BF16)] * nbv + [pltpu.VMEM((k, cw), BF16) for cw in widths]
        + [pltpu.VMEM((cw, n), BF16) for cw in widths]
        + [pltpu.SemaphoreType.DMA((2 * TN_SLICES - 1 + 2 * nb + nbd,))],
        compiler_params=_params(),
    )(*a_list, *b_list, *deps)


def _mm_rows(prods, rows_in, vecs_in, epilogue, outs, *, tm, name, dep=None):
    m = prods[0][0].shape[0]
    n = prods[0][1].shape[0] if prods[0][2] else prods[0][1].shape[1]
    tm = min(tm, m)
    assert m % tm == 0
    deps = [] if dep is None else [dep]
    n_p, n_r, n_v = len(prods), len(rows_in), len(vecs_in)

    def body(*refs):
        ab = refs[:2 * n_p]
        row_refs = refs[2 * n_p:2 * n_p + n_r]
        vec_refs = refs[2 * n_p + n_r:2 * n_p + n_r + n_v]
        out_refs = refs[2 * n_p + n_r + n_v + len(deps):]
        p = None
        for j, (_, _, tb) in enumerate(prods):
            t = _dot(ab[2 * j][...].astype(BF16), ab[2 * j + 1][...], 1, 1 if tb else 0)
            p = t if p is None else p + t
        vals = epilogue(p, *[r[...] for r in row_refs], *[v[...] for v in vec_refs])
        for (dtype, kind), o_ref, val in zip(outs, out_refs, vals):
            if kind == "row":
                o_ref[...] = val.astype(dtype)
            else:
                @pl.when(pl.program_id(0) == 0)
                def _(o_ref=o_ref):
                    o_ref[...] = jnp.zeros_like(o_ref)

                o_ref[...] += val

    row = lambda w: pl.BlockSpec((tm, w), lambda i: (i, 0))
    whole = lambda a: pl.BlockSpec(a.shape, lambda i: (0,) * a.ndim, pipeline_mode=pl.Buffered(1))
    in_specs, args = [], []
    for a, b, _ in prods:
        in_specs += [row(a.shape[1]), whole(b)]
        args += [a, b]
    in_specs += [row(r.shape[1]) for r in rows_in] + [whole(v) for v in vecs_in] + [ANY_SPEC] * len(deps)
    return pl.pallas_call(
        body, name=name,
        out_shape=tuple(jax.ShapeDtypeStruct((m, n) if kind == "row" else (8, n), dtype) for dtype, kind in outs),
        grid=(m // tm,), in_specs=in_specs,
        out_specs=tuple(row(n) if kind == "row" else pl.BlockSpec((8, n), lambda i: (0, 0)) for _, kind in outs),
        compiler_params=_params(dimension_semantics=("arbitrary",)),
    )(*args, *rows_in, *vecs_in, *deps)


def _rstd(x):
    return lax.rsqrt(jnp.mean(x * x, axis=-1, keepdims=True) + EPS)


def _norm_bwd(xh, r, t):
    return r * (t - xh * jnp.mean(xh * t, axis=-1, keepdims=True))


ROW_F32, ROW_BF16, SUM_F32 = (F32, "row"), (BF16, "row"), (F32, "sum")


def _then(epilogue, index, tb):
    def run(p, *args):
        vals = epilogue(p, *args[:-1])
        return (*vals, _dot(vals[index].astype(BF16), args[-1], 1, 1 if tb else 0))

    return run


def _ep_post_pre(p, h, g_post, g_pre):
    y = p.astype(BF16)
    yf = y.astype(F32)
    hn = h + yf * _rstd(yf) * g_post
    return y, hn, hn * _rstd(hn) * g_pre


_EP_POST_PRE_OUTS = [ROW_BF16, ROW_F32, ROW_BF16]


def _ep_final_loss(y, h, target, g_post):
    r = _rstd(y)
    yh = y * r
    err = h + yh * g_post - target
    dh = err * (1.0 / D)
    return _rowsum8(err * err), dh, _norm_bwd(yh, r, dh * g_post), _rowsum8(dh * yh)


def _ep_post_pre_bwd(du, dh_out, hn, y, g_post, g_pre):
    r2 = _rstd(hn)
    xh = hn * r2
    dh = dh_out + _norm_bwd(xh, r2, du * g_pre)
    yf = y.astype(F32)
    r1 = _rstd(yf)
    yh = yf * r1
    return dh, _norm_bwd(yh, r1, dh * g_post), _rowsum8(du * xh), _rowsum8(dh * yh)


_EP_POST_PRE_BWD_OUTS = [ROW_F32, ROW_BF16, SUM_F32, SUM_F32]


def _ep_pre_bwd(du, dh_out, x, g):
    r = _rstd(x)
    xh = x * r
    return dh_out + _norm_bwd(xh, r, du * g), _rowsum8(du * xh)


_EP_PRE_BWD_OUTS = [ROW_F32, SUM_F32]


def _prenorm(x, g, *, name, dep=None):
    t, d = x.shape
    tb = min(512, t)
    deps = [] if dep is None else [dep]

    def body(x_ref, g_ref, *rest):
        xf = x_ref[...]
        rest[-1][...] = (xf * _rstd(xf) * g_ref[...]).astype(BF16)

    return pl.pallas_call(
        body, name=name, out_shape=jax.ShapeDtypeStruct((t, d), BF16), grid=(t // tb,),
        in_specs=[pl.BlockSpec((tb, d), lambda i: (i, 0)), pl.BlockSpec((1, d), lambda i: (0, 0))]
        + [ANY_SPEC] * len(deps),
        out_specs=pl.BlockSpec((tb, d), lambda i: (i, 0)), compiler_params=_params(),
    )(x, g, *deps)


QB = 256


def _half_mask(shape, e):
    lane = lax.broadcasted_iota(jnp.int32, shape, len(shape) - 1)
    return (lane // 64) == e


def _place(kv):
    sw = pltpu.roll(kv, 64, 1)
    m0 = _half_mask(kv.shape, 0)
    return [[jnp.where(m0, kv, 0.0).astype(BF16), jnp.where(m0, 0.0, sw).astype(BF16)],
            [jnp.where(m0, sw, 0.0).astype(BF16), jnp.where(m0, 0.0, kv).astype(BF16)]]


SQ = 128
SK = 256


def _swa_valid(i, sb):
    qc = lax.broadcasted_iota(jnp.int32, (SQ, SK), 0) // CHUNK
    kc = lax.broadcasted_iota(jnp.int32, (SQ, SK), 1) // CHUNK - 2
    return (kc <= qc) & (qc <= kc + 2) & (4 * i + 2 * sb + kc >= 0)


def _swa_fwd(z, sinks, t, dep=None):
    nb = t // QB
    deps = [] if dep is None else [dep]

    def body(s_ref, q_ref, kp_ref, kc_ref, vp_ref, vc_ref, *rest):
        o_ref, lse_ref = rest[-2:]
        i = pl.program_id(0)
        kpl = _place(jnp.concatenate([kp_ref[...], kc_ref[...]], axis=0))
        vpl = _place(jnp.concatenate([vp_ref[...], vc_ref[...]], axis=0))
        lane = lax.broadcasted_iota(jnp.int32, (SQ, 128), 1)
        for sb in range(QB // SQ):
            rows, keys = slice(SQ * sb, SQ * (sb + 1)), slice(SQ * sb, SQ * sb + SK)
            valid = _swa_valid(i, sb)
            lse_out = jnp.zeros((SQ, 128), F32)
            for j in range(4):
                qp = q_ref[rows, 128 * j:128 * (j + 1)].astype(BF16)
                acc = jnp.zeros((SQ, 128), F32)
                for e in range(2):
                    h = 2 * j + e
                    kvh = h // 4
                    qm = jnp.where(_half_mask(qp.shape, e), qp, jnp.zeros_like(qp))
                    s = _dot(qm, kpl[kvh][e][keys], 1, 1) * 0.125
                    s = jnp.where(valid, s, NEG)
                    sink = s_ref[0, h]
                    m = jnp.maximum(jnp.max(s, axis=-1, keepdims=True), sink)
                    p = jnp.exp(s - m)
                    l = jnp.sum(p, axis=-1, keepdims=True) + jnp.exp(sink - m)
                    acc = acc + _dot(p.astype(BF16), vpl[kvh][e][keys], 1, 0) * (1.0 / l)
                    lse_out = jnp.where(lane == h, m + jnp.log(l), lse_out)
                o_ref[rows, 128 * j:128 * (j + 1)] = acc.astype(BF16)
            lse_ref[rows, :] = lse_out

    prev = lambda c: pl.BlockSpec((128, 128), lambda i: (jnp.maximum(2 * i - 1, 0), c))
    cur = lambda c: pl.BlockSpec((QB, 128), lambda i: (i, c))
    return pl.pallas_call(
        body, name="swa_fwd",
        out_shape=(jax.ShapeDtypeStruct((t, D), BF16), jax.ShapeDtypeStruct((t, 128), F32)),
        grid=(nb,),
        in_specs=[pl.BlockSpec(memory_space=pltpu.SMEM),
                  pl.BlockSpec((QB, SWA_W), lambda i: (i, 0)), prev(4), cur(4), prev(5), cur(5)]
        + [ANY_SPEC] * len(deps),
        out_specs=(pl.BlockSpec((QB, SWA_W), lambda i: (i, 0)), pl.BlockSpec((QB, 128), lambda i: (i, 0))),
        compiler_params=_params(),
    )(sinks, z, z, z, z, z, *deps)


def _swa_bwd(z, sinks, ymix, lse, dymix, t, dep=None):
    nb = t // QB
    deps = [] if dep is None else [dep]

    def body(s_ref, q_ref, kp_ref, kc_ref, vp_ref, vc_ref, o_ref, do_ref, l_ref, *rest):
        dq_ref, first_ref, second_ref, ds_ref, carry_ref = rest[len(deps):]
        i = pl.program_id(0)
        live = i < nb

        @pl.when(i == 0)
        def _():
            ds_ref[...] = jnp.zeros_like(ds_ref)
            carry_ref[...] = jnp.zeros_like(carry_ref)

        lane = lax.broadcasted_iota(jnp.int32, (8, 128), 1)
        kpl = _place(jnp.concatenate([kp_ref[...], kc_ref[...]], axis=0))
        vpl = _place(jnp.concatenate([vp_ref[...], vc_ref[...]], axis=0))
        nk = QB + 128
        qc = lax.broadcasted_iota(jnp.int32, (QB, nk), 0) // CHUNK
        kc = lax.broadcasted_iota(jnp.int32, (QB, nk), 1) // CHUNK - 2
        valid = (kc <= qc) & (qc <= kc + 2) & (4 * i + kc >= 0) & live
        lse_c = l_ref[...]
        dsink = jnp.zeros((8, 128), F32)
        dk_acc = [[jnp.zeros((128, nk), F32) for _ in range(2)] for _ in range(2)]
        dv_acc = [[jnp.zeros((128, nk), F32) for _ in range(2)] for _ in range(2)]
        dq = []
        for j in range(4):
            cols = slice(128 * j, 128 * (j + 1))
            qp = q_ref[:, cols].astype(BF16)
            dop = do_ref[:, cols]
            prod = dop.astype(F32) * o_ref[:, cols].astype(F32)
            acc = jnp.zeros((QB, 128), F32)
            for e in range(2):
                h = 2 * j + e
                kvh = h // 4
                hm = _half_mask(qp.shape, e)
                qm = jnp.where(hm, qp, jnp.zeros_like(qp))
                dom = jnp.where(hm, dop, jnp.zeros_like(dop))
                dd = jnp.sum(jnp.where(hm, prod, 0.0), axis=-1, keepdims=True)
                lse_h = lse_c[:, h:h + 1]
                s = _dot(qm, kpl[kvh][e], 1, 1) * 0.125
                p = jnp.where(valid, jnp.exp(s - lse_h), 0.0)
                dp = _dot(dom, vpl[kvh][e], 1, 1)
                ds = (p * (dp - dd) * 0.125).astype(BF16)
                acc = acc + _dot(ds, kpl[kvh][e], 1, 0)
                dk_acc[kvh][e] = dk_acc[kvh][e] + _dot(qm, ds, 0, 0)
                dv_acc[kvh][e] = dv_acc[kvh][e] + _dot(dom, p.astype(BF16), 0, 0)
                ps = jnp.where(live, jnp.exp(s_ref[0, h] - lse_h) * dd, 0.0)
                dsink = dsink - jnp.where(lane == h, _rowsum8(jnp.broadcast_to(ps, (QB, 128))), 0.0)
            dq.append(acc.astype(BF16))
        ds_ref[...] += dsink
        dk = (dk_acc[0][0] + dk_acc[1][1] + pltpu.roll(dk_acc[0][1] + dk_acc[1][0], 64, 0)).T
        dv = (dv_acc[0][0] + dv_acc[1][1] + pltpu.roll(dv_acc[0][1] + dv_acc[1][0], 64, 0)).T
        dkv = jnp.concatenate([dk, dv], axis=1)
        second_ref[...] = (carry_ref[...] + dkv[0:128]).astype(BF16)
        carry_ref[...] = dkv[256:384]

        @pl.when(live)
        def _():
            for j in range(4):
                dq_ref[:, 128 * j:128 * (j + 1)] = dq[j]
            first_ref[...] = dkv[128:256].astype(BF16)

    blk = lambda i: jnp.minimum(i, nb - 1)
    prev = lambda c: pl.BlockSpec((128, 128), lambda i: (jnp.maximum(2 * blk(i) - 1, 0), c))
    cur = lambda w, c: pl.BlockSpec((QB, w), lambda i: (blk(i), c))
    half = lambda index: pl.BlockSpec((128, 256), lambda i: (index(i), 0))
    return pl.pallas_call(
        body, name="swa_bwd",
        out_shape=(jax.ShapeDtypeStruct((t, SWA_W), BF16), jax.ShapeDtypeStruct((t // 2, 256), BF16),
                   jax.ShapeDtypeStruct((t // 2, 256), BF16), jax.ShapeDtypeStruct((8, 128), F32)),
        grid=(nb + 1,),
        in_specs=[pl.BlockSpec(memory_space=pltpu.SMEM),
                  cur(SWA_W, 0), prev(4), cur(128, 4), prev(5), cur(128, 5),
                  cur(SWA_W, 0), cur(SWA_W, 0), cur(128, 0)] + [ANY_SPEC] * len(deps),
        out_specs=(cur(SWA_W, 0), half(blk), half(lambda i: jnp.maximum(i - 1, 0)),
                   pl.BlockSpec((8, 128), lambda i: (0, 0))),
        scratch_shapes=[pltpu.VMEM((128, 256), F32)],
        compiler_params=_params(dimension_semantics=("arbitrary",)),
    )(sinks, z, z, z, z, z, ymix, dymix, lse, *deps)


HB = 256


def _lower_bound(lb_ref):
    a = lb_ref[...]
    a0, a1 = a[0:1], a[1:2]
    mx = jnp.maximum(a0, a1)
    e0, e1 = jnp.exp(a0 - mx), jnp.exp(a1 - mx)
    return e0 / (e0 + e1)


def _hgrn_cols(row_block):
    return [pl.BlockSpec((HB, 2 * HD), lambda j, c=base // (2 * HD) + p: (row_block(j), c))
            for base in (ZQH, ZFH, ZIH, ZGH) for p in range(2)]


NCH = HB // CHUNK


def _split3(x):
    hi = x.astype(BF16)
    r1 = x - hi.astype(F32)
    mid = r1.astype(BF16)
    return hi, mid, (r1 - mid.astype(F32)).astype(BF16)


def _blockdiag(lower):
    r = lax.broadcasted_iota(jnp.int32, (HB, HB), 0)
    c = lax.broadcasted_iota(jnp.int32, (HB, HB), 1)
    return (r // CHUNK == c // CHUNK) & ((c <= r) if lower else (c >= r))


def _chunk_sums(mask_bf16, x):
    return sum(_dot(mask_bf16, part, 1, 0) for part in _split3(x))


def _per_chunk_rows(x, row):
    w = x.shape[1]
    picked = x.reshape(NCH, CHUNK, w)[:, row:row + 1, :]
    return jnp.broadcast_to(picked, (NCH, CHUNK, w)).reshape(HB, w)


def _chunk_stack(x, chunk_of_row):
    return jnp.concatenate([jnp.where(chunk_of_row == c, x, jnp.zeros_like(x)) for c in range(NCH)], axis=1)


def _chunk_pick(x, chunk_of_row):
    w = x.shape[1] // NCH
    out = jnp.zeros((HB, w), x.dtype)
    for c in range(NCH):
        out = jnp.where(chunk_of_row == c, x[:, c * w:(c + 1) * w], out)
    return out


def _hgrn_local(q, f, kf, b):
    sq = _sig(q)
    qf = q * sq * (HD ** -0.5)
    b_mid = _per_chunk_rows(b, CHUNK // 2 - 1)
    b_last = _per_chunk_rows(b, CHUNK - 1)
    qm = qf * jnp.exp(b - b_mid)
    km = kf * jnp.exp(b_mid - b)
    kl = kf * jnp.exp(b_last - b)
    qb = qf * jnp.exp(b)
    return dict(sq=sq, b_mid=b_mid, b_last=b_last, qm=qm, km=km, kl=kl, qb=qb)


def _hgrn2_fwd(z, hgrn_lb, onorm, ymix, t, dep=None):
    nb = t // HB
    deps = [] if dep is None else [dep]

    def body(*refs):
        zq, zf, zi, zg = refs[0:2], refs[2:4], refs[4:6], refs[6:8]
        (lb_ref, on_ref), (y_ref, o_ref, sp_ref, st_ref) = refs[8:10], refs[-4:]

        @pl.when(pl.program_id(0) == 0)
        def _():
            st_ref[...] = jnp.zeros_like(st_ref)

        lb_all = _lower_bound(lb_ref)
        gn = on_ref[...]
        low = _blockdiag(True)
        low_b = low.astype(BF16)
        chunk_of_row = lax.broadcasted_iota(jnp.int32, (HB, HD), 0) // CHUNK
        for p in range(2):
            lbp = lb_all[:, 2 * HD * p:2 * HD * (p + 1)]
            fp = lbp + (1.0 - lbp) * _sig(zf[p][...])
            bp = _chunk_sums(low_b, jnp.log(fp))
            for e in range(2):
                h, ls = 2 * p + e, slice(e * HD, (e + 1) * HD)
                f = fp[:, ls]
                w = _hgrn_local(zq[p][:, ls], f, 1.0 - f, bp[:, ls])
                iv = zi[p][:, ls].astype(BF16)
                a = jnp.where(low, _dot(w["qm"].astype(BF16), w["km"].astype(BF16), 1, 1), 0.0)
                o = _dot(a.astype(BF16), iv, 1, 0)
                u = _dot(iv, _chunk_stack(w["kl"].astype(BF16), chunk_of_row), 0, 0)
                decay = jnp.exp(w["b_last"])
                st = st_ref[h]
                states = []
                for c in range(NCH):
                    sp_ref[h, c] = st
                    states.append(st.astype(BF16))
                    st = st * decay[c * CHUNK:c * CHUNK + 1] + u[:, c * HD:(c + 1) * HD]
                st_ref[h] = st
                inter = _dot(w["qb"].astype(BF16), jnp.concatenate(states, axis=0), 1, 1)
                o = o + _chunk_pick(inter, chunk_of_row)
                hs = slice(h * HD, (h + 1) * HD)
                o_ref[:, hs] = o
                gg = zg[p][:, ls]
                y_ref[:, hs] = (o * _rstd(o) * gn * (gg * _sig(gg))).astype(BF16)

    return pl.pallas_call(
        body, name="hgrn_fwd",
        out_shape=(jax.ShapeDtypeStruct((t, D), BF16), jax.ShapeDtypeStruct((t, HG_W), F32),
                   jax.ShapeDtypeStruct((4, t // CHUNK, HD, HD), F32)),
        grid=(nb,),
        in_specs=_hgrn_cols(lambda j: j) + [pl.BlockSpec((2, HG_W), lambda j: (0, 0)),
                                            pl.BlockSpec((1, HD), lambda j: (0, 0)), ANY_SPEC]
        + [ANY_SPEC] * len(deps),
        out_specs=(pl.BlockSpec((HB, HG_W), lambda j: (j, 1)),
                   pl.BlockSpec((HB, HG_W), lambda j: (j, 0)),
                   pl.BlockSpec((4, NCH, HD, HD), lambda j: (0, j, 0, 0))),
        scratch_shapes=[pltpu.VMEM((4, HD, HD), F32)],
        input_output_aliases={10: 0},
        compiler_params=_params(dimension_semantics=("arbitrary",)),
    )(*[z] * 8, hgrn_lb, onorm, ymix, *deps)


def _hgrn2_bwd(z, hgrn_lb, onorm, o_save, sprev, dymix, dza, t):
    nb = t // HB

    def body(*refs):
        zq, zf, zi, zg = refs[0:2], refs[2:4], refs[4:6], refs[6:8]
        (lb_ref, on_ref, o_ref, sp_ref, dy_ref, dqa_ref, first_ref, second_ref,
         dz_ref, dlb_ref, don_ref, dst_ref) = refs[8:]

        @pl.when(pl.program_id(0) == 0)
        def _():
            dst_ref[...] = jnp.zeros_like(dst_ref)
            dlb_ref[...] = jnp.zeros_like(dlb_ref)
            don_ref[...] = jnp.zeros_like(don_ref)

        dz_ref[:, 0:SWA_W] = dqa_ref[...]
        dz_ref[0:HB // 2, SWA_W:ZQH] = first_ref[...]
        dz_ref[HB // 2:HB, SWA_W:ZQH] = second_ref[...]
        lb_all = _lower_bound(lb_ref)
        gn = on_ref[...]
        low, upp = _blockdiag(True), _blockdiag(False)
        upp_b = upp.astype(BF16)
        low_b = low.astype(BF16)
        row = lax.broadcasted_iota(jnp.int32, (HB, HD), 0)
        chunk_of_row = row // CHUNK
        in_chunk = row % CHUNK
        for p in range(2):
            lbp = lb_all[:, 2 * HD * p:2 * HD * (p + 1)]
            sgp = _sig(zf[p][...])
            fp = lbp + (1.0 - lbp) * sgp
            bp = _chunk_sums(low_b, jnp.log(fp))
            db_pair, dkf_pair = [], []
            for e in range(2):
                h, ls, hs = 2 * p + e, slice(e * HD, (e + 1) * HD), slice((2 * p + e) * HD, (2 * p + e + 1) * HD)
                f = fp[:, ls]
                q = zq[p][:, ls]
                w = _hgrn_local(q, f, 1.0 - f, bp[:, ls])
                iv = zi[p][:, ls].astype(BF16)
                gg = zg[p][:, ls]
                o = o_ref[:, hs]
                dout = dy_ref[:, hs].astype(F32)
                sgg = _sig(gg)
                r = _rstd(o)
                oh = o * r
                dyn = dout * (gg * sgg)
                dz_ref[:, ZGH + h * HD:ZGH + (h + 1) * HD] = (
                    dout * oh * gn * (sgg * (1.0 + gg * (1.0 - sgg)))).astype(BF16)
                don_ref[...] += _rowsum8(dyn * oh)
                do = _norm_bwd(oh, r, dyn * gn).astype(BF16)
                qm, km, kl, qb = (w[n].astype(BF16) for n in ("qm", "km", "kl", "qb"))
                decay = jnp.exp(w["b_last"])
                grads_in = _dot(do, _chunk_stack(qb, chunk_of_row), 0, 0)
                dst = dst_ref[h]
                dstn, dd_rows = [None] * NCH, [None] * NCH
                for c in reversed(range(NCH)):
                    dstn[c] = dst.astype(BF16)
                    dd_rows[c] = jnp.sum(dst * sp_ref[h, c], axis=0, keepdims=True)
                    dst = dst * decay[c * CHUNK:c * CHUNK + 1] + grads_in[:, c * HD:(c + 1) * HD]
                dst_ref[h] = dst
                states = jnp.concatenate([sp_ref[h, c].astype(BF16) for c in range(NCH)], axis=0)
                dstn_all = jnp.concatenate(dstn, axis=0)
                dqb = _dot(_chunk_stack(do, chunk_of_row), states, 1, 0)
                at = jnp.where(upp, _dot(km, qm, 1, 1), 0.0)
                di = _dot(at.astype(BF16), do, 1, 0) + _chunk_pick(_dot(kl, dstn_all, 1, 1), chunk_of_row)
                dz_ref[:, ZIH + h * HD:ZIH + (h + 1) * HD] = di.astype(BF16)
                dkl = _dot(_chunk_stack(iv, chunk_of_row), dstn_all, 1, 0)
                da = jnp.where(low, _dot(do, iv, 1, 1), 0.0).astype(BF16)
                dat = jnp.where(upp, _dot(iv, do, 1, 1), 0.0).astype(BF16)
                dqm = _dot(da, km, 1, 0)
                dkm = _dot(dat, qm, 1, 0)
                b = bp[:, ls]
                e1, e2 = jnp.exp(b - w["b_mid"]), jnp.exp(w["b_mid"] - b)
                e3, e4 = jnp.exp(w["b_last"] - b), jnp.exp(b)
                dqf = dqm * e1 + dqb * e4
                dkf_pair.append(dkm * e2 + dkl * e3)
                t_qm, t_km, t_kl = dqm * w["qm"], dkm * w["km"], dkl * w["kl"]
                db = t_qm - t_km - t_kl + dqb * w["qb"]
                db_mid = jnp.sum((t_km - t_qm).reshape(NCH, CHUNK, HD), axis=1, keepdims=True)
                db_last = jnp.sum(t_kl.reshape(NCH, CHUNK, HD), axis=1, keepdims=True)
                db_last = db_last + jnp.stack(dd_rows, axis=0) * jnp.exp(
                    bp[:, ls].reshape(NCH, CHUNK, HD)[:, CHUNK - 1:CHUNK, :])
                spread = lambda v: jnp.broadcast_to(v, (NCH, CHUNK, HD)).reshape(HB, HD)
                db = (db + jnp.where(in_chunk == CHUNK // 2 - 1, spread(db_mid), 0.0)
                      + jnp.where(in_chunk == CHUNK - 1, spread(db_last), 0.0))
                db_pair.append(db)
                sq = w["sq"]
                dz_ref[:, ZQH + h * HD:ZQH + (h + 1) * HD] = (
                    dqf * (HD ** -0.5) * (sq * (1.0 + q * (1.0 - sq)))).astype(BF16)
            dlogf = _chunk_sums(upp_b, jnp.concatenate(db_pair, axis=1))
            dfv = dlogf / fp - jnp.concatenate(dkf_pair, axis=1)
            dz_ref[:, ZFH + 2 * HD * p:ZFH + 2 * HD * (p + 1)] = (dfv * (1.0 - lbp) * sgp * (1.0 - sgp)).astype(BF16)
            dlb_ref[:, 2 * HD * p:2 * HD * (p + 1)] += _rowsum8(dfv * (1.0 - sgp))

    rev = lambda j: nb - 1 - j
    return pl.pallas_call(
        body, name="hgrn_bwd",
        out_shape=(jax.ShapeDtypeStruct((t, D_IN), BF16), jax.ShapeDtypeStruct((8, HG_W), F32),
                   jax.ShapeDtypeStruct((8, HD), F32)),
        grid=(nb,),
        in_specs=_hgrn_cols(rev) + [pl.BlockSpec((2, HG_W), lambda j: (0, 0)), pl.BlockSpec((1, HD), lambda j: (0, 0)),
                                    pl.BlockSpec((HB, HG_W), lambda j: (rev(j), 0)),
                                    pl.BlockSpec((4, NCH, HD, HD), lambda j: (0, rev(j), 0, 0)),
                                    pl.BlockSpec((HB, HG_W), lambda j: (rev(j), 1)),
                                    pl.BlockSpec((HB, SWA_W), lambda j: (rev(j), 0)),
                                    pl.BlockSpec((HB // 2, 2 * KV_W), lambda j: (rev(j), 0)),
                                    pl.BlockSpec((HB // 2, 2 * KV_W), lambda j: (rev(j), 0))],
        out_specs=(pl.BlockSpec((HB, D_IN), lambda j: (rev(j), 0)), pl.BlockSpec((8, HG_W), lambda j: (0, 0)),
                   pl.BlockSpec((8, HD), lambda j: (0, 0))),
        scratch_shapes=[pltpu.VMEM((4, HD, HD), F32)],
        compiler_params=_params(dimension_semantics=("arbitrary",)),
    )(*[z] * 8, hgrn_lb, onorm, o_save, sprev, dymix, *dza)


XB = 512
XRING = 3


def _xattn_fwd(q, k, v, wo, h, g_post, g_pre, t, dep=None):
    tb = min(XB, t)
    deps = [] if dep is None else [dep]

    def body(q_ref, k_ref, v_ref, wo_ref, h_ref, gp_ref, gn_ref, *rest):
        o_ref, y_ref, hn_ref, u_ref = rest[len(deps):]
        for hd in range(XH):
            cols = slice(XD * hd, XD * (hd + 1))
            s = _dot(q_ref[:, cols], k_ref[:, cols], 1, 1) * (XD ** -0.5)
            p = jnp.exp(s - jnp.max(s, axis=-1, keepdims=True))
            l = jnp.sum(p, axis=-1, keepdims=True)
            o_ref[:, cols] = (_dot(p.astype(BF16), v_ref[:, cols], 1, 0) * (1.0 / l)).astype(BF16)
        y, hn, u = _ep_post_pre(_dot(o_ref[...], wo_ref[...], 1, 0), h_ref[...], gp_ref[...], gn_ref[...])
        y_ref[...] = y
        hn_ref[...] = hn
        u_ref[...] = u.astype(BF16)

    row = pl.BlockSpec((tb, D), lambda i: (i, 0))
    whole = lambda a: pl.BlockSpec(a.shape, lambda i: (0,) * a.ndim, pipeline_mode=pl.Buffered(1))
    half = jax.ShapeDtypeStruct((t, D), BF16)
    return pl.pallas_call(
        body, name="xattn_fwd", out_shape=(half, half, jax.ShapeDtypeStruct((t, D), F32), half), grid=(t // tb,),
        in_specs=[row, whole(k), whole(v), whole(wo), row, whole(g_post), whole(g_pre)] + [ANY_SPEC] * len(deps),
        out_specs=(row, row, row, row), compiler_params=_params(),
    )(q, k, v, wo, h, g_post, g_pre, *deps)


def _xattn_bwd(q, k, v, do, wq, wout, dh_out, hn, y, g_post, g_pre, t, dep=None):
    tb = min(XB, t)
    nb = t // tb
    deps = [] if dep is None else [dep]

    def body(q_ref, k_ref, v_ref, do_ref, wq_ref, wout_ref, dho_hbm, hn_hbm, y_ref, gp_ref, gn_ref, *rest):
        dq_ref, dk_ref, dv_ref, dh_ref, dyp_ref, dym_ref, dgn_ref, dgp_ref, ring, sems = rest[len(deps):]
        i = pl.program_id(0)

        def fetch(step, s):
            slot = step % XRING
            return pltpu.make_async_copy((dho_hbm, hn_hbm)[s].at[pl.ds(step * tb, tb), :], ring.at[s, slot],
                                         sems.at[s, slot])

        @pl.when(i == 0)
        def _():
            dk_ref[...] = jnp.zeros_like(dk_ref)
            dv_ref[...] = jnp.zeros_like(dv_ref)
            dgn_ref[...] = jnp.zeros_like(dgn_ref)
            dgp_ref[...] = jnp.zeros_like(dgp_ref)
            for step in range(min(XRING - 1, nb)):
                for s in range(2):
                    fetch(step, s).start()

        @pl.when(i + XRING - 1 < nb)
        def _():
            for s in range(2):
                fetch(i + XRING - 1, s).start()

        for h in range(XH):
            cols = slice(XD * h, XD * (h + 1))
            qh, kh, vh, doh = q_ref[:, cols], k_ref[:, cols], v_ref[:, cols], do_ref[:, cols]
            s = _dot(qh, kh, 1, 1) * (XD ** -0.5)
            p = jnp.exp(s - jnp.max(s, axis=-1, keepdims=True))
            p = p * (1.0 / jnp.sum(p, axis=-1, keepdims=True))
            dp = _dot(doh, vh, 1, 1)
            ds = (p * (dp - jnp.sum(p * dp, axis=-1, keepdims=True)) * (XD ** -0.5)).astype(BF16)
            dq_ref[:, cols] = _dot(ds, kh, 1, 0).astype(BF16)
            dk_ref[:, cols] += _dot(ds, qh, 0, 0)
            dv_ref[:, cols] += _dot(p.astype(BF16), doh, 0, 0)
        du = _dot(dq_ref[...], wq_ref[...], 1, 1)
        for s in range(2):
            fetch(i, s).wait()
        slot = i % XRING
        dh, dyp, dgn, dgp = _ep_post_pre_bwd(du, ring[0, slot], ring[1, slot], y_ref[...], gp_ref[...], gn_ref[...])
        dh_ref[...] = dh
        dyp = dyp.astype(BF16)
        dyp_ref[...] = dyp
        dym_ref[...] = _dot(dyp, wout_ref[...], 1, 1).astype(BF16)
        dgn_ref[...] += dgn
        dgp_ref[...] += dgp

    row = pl.BlockSpec((tb, D), lambda i: (i, 0))
    mem = pl.BlockSpec(k.shape, lambda i: (0, 0))
    whole = lambda a: pl.BlockSpec(a.shape, lambda i: (0,) * a.ndim, pipeline_mode=pl.Buffered(1))
    acc = pl.BlockSpec((8, D), lambda i: (0, 0))
    half = jax.ShapeDtypeStruct((t, D), BF16)
    return pl.pallas_call(
        body, name="xattn_bwd",
        out_shape=(half, jax.ShapeDtypeStruct(k.shape, F32), jax.ShapeDtypeStruct(k.shape, F32),
                   jax.ShapeDtypeStruct((t, D), F32), half, half,
                   jax.ShapeDtypeStruct((8, D), F32), jax.ShapeDtypeStruct((8, D), F32)),
        grid=(t // tb,),
        in_specs=[row, whole(k), whole(v), row, whole(wq), whole(wout), ANY_SPEC, ANY_SPEC, row, whole(g_post),
                  whole(g_pre)] + [ANY_SPEC] * len(deps),
        out_specs=(row, mem, mem, row, row, row, acc, acc),
        scratch_shapes=[pltpu.VMEM((2, XRING, tb, D), F32), pltpu.SemaphoreType.DMA((2, XRING))],
        compiler_params=_params(dimension_semantics=("arbitrary",)),
    )(q, k, v, do, wq, wout, dh_out, hn, y, g_post, g_pre, *deps)


def _mem_kv(mem, g_mem, wk, wv):
    def body(m_ref, g_ref, wk_ref, wv_ref, mn_ref, k_ref, v_ref):
        m_ = m_ref[...]
        mn = (m_ * _rstd(m_) * g_ref[...]).astype(BF16)
        mn_ref[...] = mn
        k_ref[...] = _dot(mn, wk_ref[...], 1, 0).astype(BF16)
        v_ref[...] = _dot(mn, wv_ref[...], 1, 0).astype(BF16)

    return pl.pallas_call(body, name="mem_kv", out_shape=(jax.ShapeDtypeStruct(mem.shape, BF16),) * 3,
                          compiler_params=_params())(mem, g_mem, wk, wv)


def _mem_kv_bwd(mn, mem, dk, dv, wk, wv, dep=None):
    deps = [] if dep is None else [dep]

    def body(mn_ref, m_ref, dk_ref, dv_ref, wk_ref, wv_ref, *rest):
        gk_ref, gv_ref, dg_ref = rest[len(deps):]
        mn = mn_ref[...]
        dkb, dvb = dk_ref[...].astype(BF16), dv_ref[...].astype(BF16)
        gk_ref[...] = _dot(mn, dkb, 0, 0).astype(BF16)
        gv_ref[...] = _dot(mn, dvb, 0, 0).astype(BF16)
        dmn = _dot(dkb, wk_ref[...], 1, 1) + _dot(dvb, wv_ref[...], 1, 1)
        m_ = m_ref[...]
        dg_ref[...] = _rowsum8(dmn * (m_ * _rstd(m_)))

    vmem = pl.BlockSpec(memory_space=pltpu.VMEM)
    return pl.pallas_call(
        body, name="mem_kv_bwd",
        out_shape=(jax.ShapeDtypeStruct(wk.shape, BF16), jax.ShapeDtypeStruct(wv.shape, BF16),
                   jax.ShapeDtypeStruct((8, D), F32)),
        in_specs=[vmem] * 6 + [ANY_SPEC] * len(deps), out_specs=(vmem,) * 3, compiler_params=_params(),
    )(mn, mem, dk, dv, wk, wv, *deps)


FB = 256


def _ffn_fwd_bwd(u, wgt, wut, wd, h, target, g_last, y_prev, g_post, g_pre, wo, t):
    tb = min(FB, t)

    def body(u_ref, wg_ref, wu_ref, wd_ref, h_ref, t_ref, gl_ref, yp_ref, gp_ref, gn_ref, wo_ref,
             a_ref, dy_ref, dg_ref, dup_ref, dh_ref, dyp_ref, do_ref, sq_ref, dgl_ref, dgn_ref, dgp_ref):
        @pl.when(pl.program_id(0) == 0)
        def _():
            for ref in (sq_ref, dgl_ref, dgn_ref, dgp_ref):
                ref[...] = jnp.zeros_like(ref)

        u_ = u_ref[...]
        g = _dot(u_, wg_ref[...], 1, 1)
        up = _dot(u_, wu_ref[...], 1, 1)
        sg = _sig(g)
        a = (g * sg * up).astype(BF16)
        a_ref[...] = a
        h_ = h_ref[...]
        sq, dh3, dy, dgl = _ep_final_loss(_dot(a, wd_ref[...], 1, 0), h_, t_ref[...], gl_ref[...])
        sq_ref[...] += sq
        dgl_ref[...] += dgl
        dy = dy.astype(BF16)
        dy_ref[...] = dy
        da = _dot(dy, wd_ref[...], 1, 1)
        dup = (da * g * sg).astype(BF16)
        dgate = (da * up * (sg * (1.0 + g * (1.0 - sg)))).astype(BF16)
        dup_ref[...] = dup
        dg_ref[...] = dgate
        du = _dot(dgate, wg_ref[...], 1, 0) + _dot(dup, wu_ref[...], 1, 0)
        dh, dyp, dgn, dgp = _ep_post_pre_bwd(du, dh3, h_, yp_ref[...], gp_ref[...], gn_ref[...])
        dh_ref[...] = dh
        dyp = dyp.astype(BF16)
        dyp_ref[...] = dyp
        do_ref[...] = _dot(dyp, wo_ref[...], 1, 1).astype(BF16)
        dgn_ref[...] += dgn
        dgp_ref[...] += dgp

    row = lambda w: pl.BlockSpec((tb, w), lambda i: (i, 0))
    whole = lambda a: pl.BlockSpec(a.shape, lambda i: (0,) * a.ndim, pipeline_mode=pl.Buffered(1))
    acc = pl.BlockSpec((8, D), lambda i: (0, 0))
    wide, half, sums = (jax.ShapeDtypeStruct((t, D_FF), BF16), jax.ShapeDtypeStruct((t, D), BF16),
                        jax.ShapeDtypeStruct((8, D), F32))
    return pl.pallas_call(
        body, name="ffn_fwd_bwd",
        out_shape=(wide, half, wide, wide, jax.ShapeDtypeStruct((t, D), F32), half, half, sums, sums, sums, sums),
        grid=(t // tb,),
        in_specs=[row(D), whole(wgt), whole(wut), whole(wd), row(D), row(D), whole(g_last), row(D), whole(g_post),
                  whole(g_pre), whole(wo)],
        out_specs=(row(D_FF), row(D), row(D_FF), row(D_FF), row(D), row(D), row(D), acc, acc, acc, acc),
        compiler_params=_params(dimension_semantics=("arbitrary",)),
    )(u, wgt, wut, wd, h, target, g_last, y_prev, g_post, g_pre, wo)


def _local_step(x, mem, target, fetch, sm, emit=None, first_dep=None, milestone=None):
    t = x.shape[0]
    w, gw = {}, {}

    def out(key, g):
        gw[key] = g
        return None if emit is None else emit(key, g)

    def tell(tag, value):
        return None if milestone is None else milestone(tag, value)
    u1 = _prenorm(x, sm["g_mix_pre"], name="prenorm_mix", dep=first_dep)
    w["winT"] = fetch("winT", u1)
    z = _mm_nt(u1, w["winT"], out_dtype=F32, tm=1024, tn=1408, name="mm_z")
    ymix, lse = _swa_fwd(z, sm["sinks"], t)
    ymix, o_h, sprev = _hgrn2_fwd(z, sm["hgrn_lb"], sm["hgrn_onorm"], ymix, t, dep=tell("swa", lse))
    for key in ("wout", "wq", "wk", "wv", "wo"):
        w[key] = fetch(key, ymix)
    y1, h1, u2, qx = _mm_rows([(ymix, w["wout"], False)], [x], [sm["g_mix_post"], sm["g_x_pre"], w["wq"]],
                              _then(_ep_post_pre, 2, False), _EP_POST_PRE_OUTS + [ROW_BF16], tm=512,
                              name="mm_y1_post_qx")
    mn, kx, vx = _mem_kv(mem, sm["g_mem"], w["wk"], w["wv"])
    ox, y2, h2, u3 = _xattn_fwd(qx, kx, vx, w["wo"], h1, sm["g_x_post"], sm["g_ffn_pre"], t, dep=tell("kv", kx))
    for key in ("wgT", "wuT", "wd"):
        w[key] = fetch(key, u3)
    act, dy3, dgate, dup, dh2, dy2, dox, sq, dg_ffn_post, dg_ffn_pre, dg_x_post = _ffn_fwd_bwd(
        u3, w["wgT"], w["wuT"], w["wd"], h2, target, sm["g_ffn_post"], y2, sm["g_x_post"], sm["g_ffn_pre"], w["wo"], t)
    dep = out("wd", *_mm_tn([act], [dy3], name="mm_gwd"))
    gwg, gwu = _mm_tn([dgate, dup], [u3], name="mm_gwg_gwu", dep=dep)
    out("wgT", gwg)
    dep = out("wuT", gwu)
    dqx, dkx, dvx, dh1, dy1, dymix, dg_x_pre, dg_mix_post = _xattn_bwd(
        qx, kx, vx, dox, w["wq"], w["wout"], dh2, h1, y1, sm["g_mix_post"], sm["g_x_pre"], t, dep=dep)
    gwo, gwq, gwout = _mm_tn([ox, u2, ymix], [dy2, dqx, dy1], name="mm_gwo_gwq_gwout")
    gwk, gwv, dg_mem = _mem_kv_bwd(mn, mem, dkx, dvx, w["wk"], w["wv"])
    for key, g in (("wo", gwo), ("wq", gwq), ("wout", gwout), ("wk", gwk), ("wv", gwv)):
        dep = out(key, g)
    *dza, dsinks = _swa_bwd(z, sm["sinks"], ymix, lse, dymix, t, dep=dep)
    dz, dlb, donorm = _hgrn2_bwd(z, sm["hgrn_lb"], sm["hgrn_onorm"], o_h, sprev, dymix, dza, t)
    dep = out("winT", *_mm_tn([dz], [u1], name="mm_gwin"))
    grad_x, dg_mix_pre = _mm_rows([(dz, w["winT"], False)], [dh1, x], [sm["g_mix_pre"]], _ep_pre_bwd,
                                  _EP_PRE_BWD_OUTS, tm=512, name="mm_du1_pre_bwd", dep=dep)
    parts = dict(g_mix_pre=dg_mix_pre, g_mix_post=dg_mix_post, g_mem=dg_mem, g_x_pre=dg_x_pre,
                 g_x_post=dg_x_post, g_ffn_pre=dg_ffn_pre, g_ffn_post=dg_ffn_post,
                 hgrn_onorm=donorm, hgrn_lb=dlb, sinks=dsinks, sq=sq)
    return grad_x, gw, parts


def _position():
    return lax.axis_index("x"), lax.axis_index("y"), lax.axis_index("c")


def _peer(pos, k):
    x, y, c = pos
    return (1 - x if k & 4 else x, 1 - y if k & 2 else y, 1 - c if k & 1 else c)


def _linear(pos):
    x, y, c = pos
    return 4 * x + 2 * y + c


HBM_SPEC = pl.BlockSpec(memory_space=pltpu.HBM)
SEM_SPEC = pl.BlockSpec(memory_space=pltpu.SEMAPHORE)
DATAFLOW = pltpu.SideEffectType.DATAFLOW_SIDE_EFFECTING
SEND_ORDER = (1, 2, 4, 3, 5, 6, 7)


def _in_hbm(a):
    return pltpu.with_memory_space_constraint(a, pltpu.HBM)


def _prepare_weights(shards, *, name, dep=None):
    n = len(shards)
    deps = [] if dep is None else [dep]

    def body(*refs):
        ins, (outs, lands, sem) = refs[:n], (refs[-2 * n - 1:-n - 1], refs[-n - 1:-1], refs[-1])
        me_lin = _linear(_position())
        copies = []
        for a in range(n):
            r = ins[a].shape[0]
            outs[a][...] = ins[a][...].astype(BF16)
            copies.append(pltpu.make_async_copy(outs[a], lands[a].at[pl.ds(me_lin * r, r), :], sem.at[a]))
            copies[-1].start()
        for cp in copies:
            cp.wait()

    vmem = pl.BlockSpec(memory_space=pltpu.VMEM)
    res = pl.pallas_call(
        body, name=name,
        out_shape=tuple(jax.ShapeDtypeStruct(s.shape, BF16) for s in shards)
        + tuple(jax.ShapeDtypeStruct((N_DEV * s.shape[0], s.shape[1]), BF16) for s in shards),
        in_specs=[vmem] * n + [ANY_SPEC] * len(deps), out_specs=tuple([vmem] * n + [ANY_SPEC] * n),
        scratch_shapes=[pltpu.SemaphoreType.DMA((n,))], compiler_params=_params(),
    )(*shards, *deps)
    return res[:n], res[n:]


def _copies_start(arrays, plan, n, *, name):
    na = len(arrays)

    def body(*refs):
        ins, send_sems, recv_sems = refs[:na], refs[na], refs[na + 1]
        me = _position()
        for j in range(n):
            src, dst, peer, _ = plan(ins, me, j)
            pltpu.make_async_remote_copy(src_ref=src, dst_ref=dst, send_sem=send_sems.at[j], recv_sem=recv_sems.at[j],
                                         device_id=peer, device_id_type=MESH).start()

    return pl.pallas_call(
        body, name=name,
        out_shape=(pltpu.SemaphoreType.DMA((n,)), pltpu.SemaphoreType.DMA((n,)))
        + tuple(pltpu.HBM(a.shape, a.dtype) for a in arrays),
        in_specs=(HBM_SPEC,) * na, out_specs=(SEM_SPEC, SEM_SPEC) + (HBM_SPEC,) * na,
        input_output_aliases={i: 2 + i for i in range(na)},
        compiler_params=pltpu.CompilerParams(has_side_effects=DATAFLOW),
    )(*[_in_hbm(a) for a in arrays])


def _copies_wait(send_sems, recv_sems, arrays, plan, n, after, *, name):
    na = len(arrays)

    def body(*refs):
        ins, send_sems, recv_sems = refs[:na], refs[na], refs[na + 1]
        me = _position()
        for j in range(n):
            src, _, peer, landed = plan(ins, me, j)
            copy = pltpu.make_async_remote_copy(src_ref=src, dst_ref=landed, send_sem=send_sems.at[j],
                                                recv_sem=recv_sems.at[j], device_id=peer, device_id_type=MESH)
            copy.wait_send()
            copy.wait_recv()

    return pl.pallas_call(
        body, name=name, out_shape=tuple(pltpu.HBM(a.shape, a.dtype) for a in arrays),
        in_specs=(HBM_SPEC,) * na + (SEM_SPEC, SEM_SPEC, ANY_SPEC), out_specs=(HBM_SPEC,) * na,
        input_output_aliases={i: i for i in range(na)},
        compiler_params=pltpu.CompilerParams(has_side_effects=DATAFLOW),
    )(*arrays, send_sems, recv_sems, after)


SAME_CORE = (2, 4, 6)


class _TwoLevelGather:
    def __init__(self, shards, lands, *, name):
        n = self.n = len(shards)
        self.name = name
        first_peers = (1,) + SAME_CORE

        def rows(ref, pos):
            r = ref.shape[0] // N_DEV
            return ref.at[pl.ds(_linear(pos) * r, r), :]

        def first(refs, me, j):
            a, peer = j // 4, _peer(me, first_peers[j % 4])
            return refs[a], rows(refs[n + a], me), peer, rows(refs[n + a], peer)

        def second(refs, me, j):
            a, sibling = j // 3, _peer(me, 1)
            mine = rows(refs[a], _peer(me, SAME_CORE[j % 3]))
            return mine, mine, sibling, rows(refs[a], _peer(sibling, SAME_CORE[j % 3]))

        self._first, self._second = first, second
        self._flight = _copies_start(list(shards) + list(lands), first, 4 * n, name=name + "_send")
        self.dep = self._flight[2]

    def pass_on(self, after):
        send1, recv1, *arrays = self._flight
        arrays = _copies_wait(send1, recv1, arrays, self._first, 4 * self.n, after, name=self.name + "_recv")
        self._flight = _copies_start(list(arrays[self.n:]), self._second, 3 * self.n, name=self.name + "_pass")
        return self._flight[2]

    def finish(self, after):
        send2, recv2, *lands = self._flight
        return _copies_wait(send2, recv2, lands, self._second, 3 * self.n, after, name=self.name + "_pass_recv")


def _exchange_start(gs, *, name):
    n = len(gs)
    rows = [g.shape[0] // N_DEV for g in gs]
    lands = [lax.empty((N_DEV - 1, r, g.shape[1]), g.dtype) for g, r in zip(gs, rows)]

    def body(*refs):
        g_refs, land_refs = refs[:n], refs[n:2 * n]
        send_sems, recv_sems = refs[2 * n:3 * n], refs[3 * n:4 * n]
        me = _position()
        for a in range(n):
            for k in SEND_ORDER:
                peer = _peer(me, k)
                pltpu.make_async_remote_copy(
                    src_ref=g_refs[a].at[pl.ds(_linear(peer) * rows[a], rows[a]), :],
                    dst_ref=land_refs[a].at[k - 1],
                    send_sem=send_sems[a].at[k - 1], recv_sem=recv_sems[a].at[k - 1],
                    device_id=peer, device_id_type=MESH).start()

    res = pl.pallas_call(
        body, name=name,
        out_shape=tuple(pltpu.SemaphoreType.DMA((N_DEV - 1,)) for _ in range(2 * n))
        + tuple(pltpu.HBM(a.shape, a.dtype) for a in gs + lands),
        in_specs=(HBM_SPEC,) * (2 * n), out_specs=(SEM_SPEC,) * (2 * n) + (HBM_SPEC,) * (2 * n),
        input_output_aliases={i: 2 * n + i for i in range(2 * n)},
        compiler_params=pltpu.CompilerParams(has_side_effects=DATAFLOW),
    )(*[_in_hbm(a) for a in gs + lands])
    return [(res[a], res[n + a], res[2 * n + a], res[3 * n + a]) for a in range(n)]


def _exchange_wait(send_sems, recv_sems, g_thru, land_thru, after, *, name):
    r = land_thru.shape[1]

    def body(g_ref, land_ref, send_sems, recv_sems, after_ref, g_dead, got_ref):
        del after_ref, g_dead, got_ref
        me = _position()
        for k in SEND_ORDER:
            peer = _peer(me, k)
            copy = pltpu.make_async_remote_copy(
                src_ref=g_ref.at[pl.ds(_linear(peer) * r, r), :], dst_ref=land_ref.at[k - 1],
                send_sem=send_sems.at[k - 1], recv_sem=recv_sems.at[k - 1],
                device_id=peer, device_id_type=MESH)
            copy.wait_send()
            copy.wait_recv()

    return pl.pallas_call(
        body, name=name,
        out_shape=(pltpu.HBM(g_thru.shape, g_thru.dtype), pltpu.HBM(land_thru.shape, land_thru.dtype)),
        in_specs=(HBM_SPEC, HBM_SPEC, SEM_SPEC, SEM_SPEC, pl.BlockSpec(memory_space=pl.ANY)),
        out_specs=(HBM_SPEC, HBM_SPEC), input_output_aliases={0: 0, 1: 1},
        compiler_params=pltpu.CompilerParams(has_side_effects=DATAFLOW),
    )(g_thru, land_thru, send_sems, recv_sems, after)


ADAMW_TILE_ROWS = 256


def _adamw_math(w, g, m, v):
    m = B1 * m + (1.0 - B1) * g
    v = B2 * v + (1.0 - B2) * (g * g)
    delta = -LR * ((m / C1) / (jnp.sqrt(v / C2) + AEPS) + WD * w)
    return delta, m, v


def _sum_adamw(items, *, name):
    n = len(items)
    r, d = items[0][2].shape
    assert all(it[2].shape == (r, d) for it in items)
    rc = r // 2 if r > ADAMW_TILE_ROWS else r
    tiles = [(a, r0) for a in range(n) for r0 in range(0, r, rc)]
    n_in, n_out = 5, 4

    def body(*refs):
        ins, outs = refs[:n_in * n], refs[n_in * n:(n_in + n_out) * n]
        land_v, own_v, f32_v, sems = refs[(n_in + n_out) * n:]
        me_lin = _linear(_position())

        def loads(j):
            a, r0 = tiles[j]
            g_all, land, w, m, v = ins[n_in * a:n_in * a + n_in]
            rows = pl.ds(r0, rc)
            pairs = [(land.at[:, rows, :], land_v.at[j]), (g_all.at[pl.ds(me_lin * r + r0, rc), :], own_v.at[j]),
                     (w.at[rows, :], f32_v.at[j, 0]), (m.at[rows, :], f32_v.at[j, 1]), (v.at[rows, :], f32_v.at[j, 2])]
            return [pltpu.make_async_copy(src, dst, sems.at[j, i]) for i, (src, dst) in enumerate(pairs)]

        def stores(j):
            a, r0 = tiles[j]
            return [pltpu.make_async_copy(f32_v.at[j, 3 + i], outs[n_out * a + i].at[pl.ds(r0, rc), :],
                                          sems.at[j, n_in + i]) for i in range(n_out)]

        for j in range(len(tiles)):
            for cp in loads(j):
                cp.start()
        for j in range(len(tiles)):
            for cp in loads(j):
                cp.wait()
            g = land_v[j, 0].astype(F32)
            for s in range(1, N_DEV - 1):
                g = g + land_v[j, s].astype(F32)
            g = own_v[j].astype(F32) + g
            f32_v[j, 3] = g
            f32_v[j, 4], f32_v[j, 5], f32_v[j, 6] = _adamw_math(f32_v[j, 0], g, f32_v[j, 1], f32_v[j, 2])
            for cp in stores(j):
                cp.start()
        for j in range(len(tiles)):
            for cp in stores(j):
                cp.wait()

    nt = len(tiles)
    res = pl.pallas_call(
        body, name=name,
        out_shape=tuple(jax.ShapeDtypeStruct((r, d), F32) for _ in range(n_out * n)),
        in_specs=[ANY_SPEC] * (n_in * n), out_specs=(ANY_SPEC,) * (n_out * n),
        scratch_shapes=[pltpu.VMEM((nt, N_DEV - 1, rc, d), BF16), pltpu.VMEM((nt, rc, d), BF16),
                        pltpu.VMEM((nt, 3 + n_out, rc, d), F32), pltpu.SemaphoreType.DMA((nt, n_in + n_out))],
        compiler_params=_params(),
    )(*[a for it in items for a in it])
    return [res[n_out * a:n_out * a + n_out] for a in range(n)]


SMALL = ("g_mix_pre", "g_mix_post", "g_mem", "g_x_pre", "g_x_post", "g_ffn_pre", "g_ffn_post",
         "hgrn_onorm", "hgrn_lb", "sinks")
SMALL_W = dict(hgrn_onorm=HD, hgrn_lb=HG_W, sinks=8)
SQ_ROW = len(SMALL)
PACK_ROWS = 16


def _small_pack(parts):
    ns = len(SMALL)

    def body(*refs):
        part, mine, slots, sem = refs[:ns + 1], refs[ns + 1], refs[ns + 2], refs[ns + 3]
        mine[...] = jnp.zeros((PACK_ROWS, D), F32)
        for r, name in enumerate(SMALL):
            wd = SMALL_W.get(name, D)
            mine[r:r + 1, 0:wd] = jnp.sum(part[r][...], axis=0, keepdims=True)[:, 0:wd]
        sq = jnp.sum(part[ns][...]) * (0.5 / D)
        mine[SQ_ROW:SQ_ROW + 1, :] = jnp.full((1, D), sq, F32)
        own = pltpu.make_async_copy(mine, slots.at[_linear(_position())], sem)
        own.start()
        own.wait()

    vmem = pl.BlockSpec(memory_space=pltpu.VMEM)
    return pl.pallas_call(
        body, name="small_pack",
        out_shape=(jax.ShapeDtypeStruct((PACK_ROWS, D), F32), jax.ShapeDtypeStruct((N_DEV, PACK_ROWS, D), F32)),
        in_specs=[vmem] * (ns + 1), out_specs=(vmem, ANY_SPEC),
        scratch_shapes=[pltpu.SemaphoreType.DMA(())], compiler_params=_params(),
    )(*[parts[n] for n in SMALL], parts["sq"])


def _small_exchange(mine, slots):
    def plan(refs, me, j):
        peer = _peer(me, j + 1)
        return refs[0], refs[1].at[_linear(me)], peer, refs[1].at[_linear(peer)]

    send, recv, mine1, slots1 = _copies_start([mine, slots], plan, N_DEV - 1, name="small_send")
    return lambda after: _copies_wait(send, recv, [mine1, slots1], plan, N_DEV - 1, after, name="small_recv")[1]


def _small_update(slots, sm, m_sm, v_sm):
    ns = len(SMALL)

    def body(*refs):
        tot = refs[0][0]
        for s in range(1, N_DEV):
            tot = tot + refs[0][s]
        w_refs, m_refs, v_refs = refs[1:ns + 1], refs[ns + 1:2 * ns + 1], refs[2 * ns + 1:3 * ns + 1]
        outs = refs[3 * ns + 1:]
        loss_ref = outs[0]
        g_out, d_out = outs[1:ns + 1], outs[ns + 1:2 * ns + 1]
        nm_out, nv_out = outs[2 * ns + 1:3 * ns + 1], outs[3 * ns + 1:4 * ns + 1]
        loss_ref[...] = tot[SQ_ROW:SQ_ROW + 1, 0:1]
        for r, name in enumerate(SMALL):
            wd = SMALL_W.get(name, D)
            g = tot[r:r + 1, 0:wd]
            w = w_refs[r][...]
            if name == "hgrn_lb":
                mx = jnp.maximum(w[0:1], w[1:2])
                e0, e1 = jnp.exp(w[0:1] - mx), jnp.exp(w[1:2] - mx)
                lb0 = e0 / (e0 + e1)
                g0 = g * lb0 * (1.0 - lb0)
                for i, gi in enumerate((g0, -g0)):
                    d, nm, nv = _adamw_math(w[i:i + 1], gi, m_refs[r][i:i + 1, :], v_refs[r][i:i + 1, :])
                    g_out[r][i:i + 1, :] = gi
                    d_out[r][i:i + 1, :], nm_out[r][i:i + 1, :], nv_out[r][i:i + 1, :] = d, nm, nv
            else:
                d, nm, nv = _adamw_math(w, g, m_refs[r][...], v_refs[r][...])
                g_out[r][...] = g
                d_out[r][...], nm_out[r][...], nv_out[r][...] = d, nm, nv

    shapes = [jax.ShapeDtypeStruct(sm[n].shape, F32) for n in SMALL]
    res = pl.pallas_call(
        body, name="small_update", out_shape=tuple([jax.ShapeDtypeStruct((1, 1), F32)] + shapes * 4),
        compiler_params=_params(),
    )(slots, *[sm[n] for n in SMALL], *[m_sm[n] for n in SMALL], *[v_sm[n] for n in SMALL])
    groups = [dict(zip(SMALL, res[1 + i * ns:1 + (i + 1) * ns])) for i in range(4)]
    return res[0], groups[0], groups[1], groups[2], groups[3]


BIG = ("w_in", "w_gate", "w_up", "w_down", "w_out", "wq_x", "wk_x", "wv_x", "wo_x")
BIG_KEY = dict(w_in="winT", w_gate="wgT", w_up="wuT", w_down="wd", w_out="wout", wq_x="wq", wk_x="wk",
               wv_x="wv", wo_x="wo")
TRANSPOSED = ("w_in", "w_gate", "w_up")
WEIGHTS = ("w_in", "sinks", "hgrn_lb", "hgrn_onorm", "w_out", "g_mix_pre", "g_mix_post", "g_mem", "g_x_pre",
           "g_x_post", "wq_x", "wk_x", "wv_x", "wo_x", "g_ffn_pre", "g_ffn_post", "w_gate", "w_up", "w_down")


def kernel(x, mem, w_in, sinks, hgrn_lb, hgrn_onorm, w_out, g_mix_pre, g_mix_post, g_mem, g_x_pre, g_x_post, wq_x, wk_x, wv_x, wo_x, g_ffn_pre, g_ffn_post, w_gate, w_up, w_down, loss_target, m_w_in, m_sinks, m_hgrn_lb, m_hgrn_onorm, m_w_out, m_g_mix_pre, m_g_mix_post, m_g_mem, m_g_x_pre, m_g_x_post, m_wq_x, m_wk_x, m_wv_x, m_wo_x, m_g_ffn_pre, m_g_ffn_post, m_w_gate, m_w_up, m_w_down, v_w_in, v_sinks, v_hgrn_lb, v_hgrn_onorm, v_w_out, v_g_mix_pre, v_g_mix_post, v_g_mem, v_g_x_pre, v_g_x_post, v_wq_x, v_wk_x, v_wv_x, v_wo_x, v_g_ffn_pre, v_g_ffn_post, v_w_gate, v_w_up, v_w_down):
    given = dict(locals())
    wts = {n: given[n] for n in WEIGHTS}
    ms = {n: given["m_" + n] for n in WEIGHTS}
    vs = {n: given["v_" + n] for n in WEIGHTS}

    def mat(a, name):
        a = a[0]
        return a.T if name in TRANSPOSED else a

    groups = (("w_in",), ("w_out", "wq_x", "wk_x", "wv_x", "wo_x"), ("w_gate", "w_up", "w_down"))
    gathers = []
    first_dep = None
    for tag, group in zip(("w_in", "w_attn", "w_ffn"), groups):
        shards, lands = _prepare_weights([mat(wts[n], n) for n in group], name="prepare_" + tag, dep=first_dep)
        gathers.append(_TwoLevelGather(shards, lands, name=tag))
        first_dep = gathers[-1].dep
    name_of = {k: n for n, k in BIG_KEY.items()}
    gathered = {}

    def milestone(tag, value):
        return gathers[{"swa": 1, "kv": 2}[tag]].pass_on(value)

    def fetch(key, after):
        name = name_of[key]
        if name not in gathered:
            g = [i for i, group in enumerate(groups) if name in group][0]
            if g == 0:
                gathers[0].pass_on(after)
            gathered.update(zip(groups[g], gathers[g].finish(after)))
        return gathered[name]

    sm = {n: wts[n] for n in SMALL}
    started, held = {}, {}
    send_with = {k: group for group in (("wgT", "wuT"), ("wo", "wq", "wout", "wk", "wv")) for k in group}

    def emit(key, g):
        held[key] = g
        group = send_with.get(key, (key,))
        if key != group[-1]:
            return None
        flights = _exchange_start([held[k] for k in group], name="grad_send_" + name_of[group[0]])
        started.update({name_of[k]: f for k, f in zip(group, flights)})
        return flights[-1][2]

    grad_x, _, parts = _local_step(x[0], mem[0], loss_target[0], fetch, sm, emit, first_dep=first_dep, milestone=milestone)
    small_finish = _small_exchange(*_small_pack(parts))
    grads, deltas, new_m, new_v = {}, {}, {}, {}
    after = grad_x
    for group in (("w_down",), ("w_gate", "w_up"), ("wo_x", "wq_x", "wk_x", "wv_x", "w_out"), ("w_in",)):
        items = []
        for n in group:
            g_all, land = _exchange_wait(*started[n], after, name="grad_recv_" + n)
            items.append((g_all, land, mat(wts[n], n), mat(ms[n], n), mat(vs[n], n)))
            after = land
        for n, res in zip(group, _sum_adamw(items, name="adamw_" + group[0])):
            after = res[1]
            if n in TRANSPOSED:
                res = [a.T for a in res]
            grads[n], deltas[n], new_m[n], new_v[n] = [a[None] for a in res]
    loss, g_s, d_s, m_s, v_s = _small_update(small_finish(after), sm, {n: ms[n] for n in SMALL},
                                             {n: vs[n] for n in SMALL})
    grads.update(g_s), deltas.update(d_s), new_m.update(m_s), new_v.update(v_s)
    return (loss[0, 0], grad_x[None], *[grads[n] for n in WEIGHTS], *[deltas[n] for n in WEIGHTS],
            *[new_m[n] for n in WEIGHTS], *[new_v[n] for n in WEIGHTS])
```

```python
import functools

import jax
import jax.numpy as jnp
from jax import lax
from jax.experimental import pallas as pl
from jax.experimental.pallas import tpu as pltpu

F32 = jnp.float32
BF16 = jnp.bfloat16

D = 1024
D_IN = 2816
D_FF = 2816
CHUNK = 64
SWA_W = 512
KV_W = 128
HG_W = 512
HD = 128
ZQH, ZFH, ZIH, ZGH = 768, 1280, 1792, 2304
XH, XD = 4, 256
EPS = 1e-6
NEG = -1e30
N_DEV = 8
MESH = pl.DeviceIdType.MESH

LR, B1, B2, AEPS, WD, STEP = 0.001, 0.9, 0.999, 1e-08, 0.01, 10
C1 = 1.0 - B1 ** STEP
C2 = 1.0 - B2 ** STEP

VMEM_LIMIT = 56 * 1024 * 1024


def _params(**kw):
    return pltpu.CompilerParams(vmem_limit_bytes=VMEM_LIMIT, **kw)


def _sig(x):
    return 1.0 / (1.0 + jnp.exp(-x))


def _rowsum8(x):
    r, w = x.shape
    return jnp.sum(x.reshape(r // 8, 8, w), axis=0)


def _dot(a, b, ca, cb, precision=None):
    return lax.dot_general(a, b, (((ca,), (cb,)), ((), ())), preferred_element_type=F32,
                           precision=precision)


ANY_SPEC = pl.BlockSpec(memory_space=pl.ANY)


def _mm_nt(a, b, *, out_dtype, tm, tn, name):
    (m, k), n = a.shape, b.shape[0]
    tm, tn = min(tm, m), min(tn, n)
    assert a.dtype == BF16 and b.dtype == BF16 and m % tm == 0 and n % tn == 0, (name, m, n, tm, tn)

    def body(a_ref, b_ref, o_ref):
        o_ref[...] = _dot(a_ref[...], b_ref[...], 1, 1).astype(out_dtype)

    return pl.pallas_call(
        body, name=name, out_shape=jax.ShapeDtypeStruct((m, n), out_dtype), grid=(n // tn, m // tm),
        in_specs=[pl.BlockSpec((tm, k), lambda j, i: (i, 0)), pl.BlockSpec((tn, k), lambda j, i: (j, 0))],
        out_specs=pl.BlockSpec((tm, tn), lambda j, i: (i, j)),
        compiler_params=_params(dimension_semantics=("parallel", "parallel")),
    )(a, b)


TN_FIRST = 256
TN_REST = 1152
TN_SLICES = 4


def _mm_tn(a_list, b_list, *, name, dep=None):
    na, nbd, (k, m), n = len(a_list), len(b_list), a_list[0].shape, b_list[0].shape[1]
    assert nbd in (1, na) and all(b.dtype == BF16 and b.shape == (k, n) for b in b_list)
    assert all(a.dtype == BF16 and a.shape == (k, m) for a in a_list)
    nbv = min(2, nbd)
    widths = [TN_FIRST, TN_FIRST]
    while sum(widths) < m:
        widths.append(min(TN_REST, m - sum(widths)))
    starts = [sum(widths[:i]) for i in range(len(widths))]
    assert sum(widths) == m
    per = len(widths)
    nb = na * per
    ks = k // TN_SLICES
    ahead = 2
    assert ahead < per
    deps = [] if dep is None else [dep]

    def body(*refs):
        a_hbm, b_hbm, rest = refs[:na], refs[na:na + nbd], refs[na + nbd + len(deps):]
        o_hbm, b_v, rest = rest[:na], rest[na:na + nbv], rest[na + nbv:]
        a_v, o_v, sems = rest[:per], rest[per:-1], rest[-1]
        sliced = []
        for c in range(TN_SLICES):
            rows = pl.ds(c * ks, ks)
            sliced.append((pltpu.make_async_copy(b_hbm[0].at[rows, :], b_v[0].at[rows, :], sems.at[2 * c]),
                           pltpu.make_async_copy(a_hbm[0].at[rows, pl.ds(0, widths[0])], a_v[0].at[rows, :],
                                                 sems.at[2 * c + 1])))
        base = 2 * TN_SLICES - 1
        cols = [pl.ds(starts[i % per], widths[i % per]) for i in range(nb)]
        loads = [None] + [pltpu.make_async_copy(a_hbm[i // per].at[:, cols[i]], a_v[i % per], sems.at[base + i])
                          for i in range(1, nb)]
        stores = [pltpu.make_async_copy(o_v[i % per], o_hbm[i // per].at[cols[i], :], sems.at[base + nb + i])
                  for i in range(nb)]
        next_b = [None] + [pltpu.make_async_copy(b_hbm[j], b_v[j % nbv], sems.at[base + 2 * nb + j])
                           for j in range(1, nbd)]
        for pair in sliced:
            for cp in pair:
                cp.start()
        for i in range(1, 1 + ahead):
            loads[i].start(priority=1)
        if nbd > 1:
            next_b[1].start(priority=1)
        acc = None
        for c, pair in enumerate(sliced):
            for cp in pair:
                cp.wait()
            p = _dot(a_v[0][c * ks:(c + 1) * ks, :], b_v[0][c * ks:(c + 1) * ks, :], 0, 0)
            acc = p if acc is None else acc + p
        o_v[0][...] = acc.astype(BF16)
        stores[0].start()
        for i in range(1, nb):
            j = (i // per) % nbd
            if nbd > 1 and i % per == 0:
                next_b[j].wait()
                if j + 1 < nbd:
                    next_b[j + 1].start()
            loads[i].wait()
            if i + ahead < nb:
                loads[i + ahead].start()
            if i >= per:
                stores[i - per].wait()
            o_v[i % per][...] = _dot(a_v[i % per][...], b_v[j % nbv][...], 0, 0).astype(BF16)
            stores[i].start()
        for cp in stores[nb - per:]:
            cp.wait()

    return pl.pallas_call(
        body, name=name, out_shape=tuple(jax.ShapeDtypeStruct((m, n), BF16) for _ in a_list),
        in_specs=[ANY_SPEC] * (na + nbd + len(deps)), out_specs=(ANY_SPEC,) * na,
        scratch_shapes=[pltpu.VMEM((k, n), BF16)] * nbv + [pltpu.VMEM((k, cw), BF16) for cw in widths]
        + [pltpu.VMEM((cw, n), BF16) for cw in widths]
        + [pltpu.SemaphoreType.DMA((2 * TN_SLICES - 1 + 2 * nb + nbd,))],
        compiler_params=_params(),
    )(*a_list, *b_list, *deps)


def _mm_rows(prods, rows_in, vecs_in, epilogue, outs, *, tm, name, dep=None):
    m = prods[0][0].shape[0]
    n = prods[0][1].shape[0] if prods[0][2] else prods[0][1].shape[1]
    tm = min(tm, m)
    assert m % tm == 0
    deps = [] if dep is None else [dep]
    n_p, n_r, n_v = len(prods), len(rows_in), len(vecs_in)

    def body(*refs):
        ab = refs[:2 * n_p]
        row_refs = refs[2 * n_p:2 * n_p + n_r]
        vec_refs = refs[2 * n_p + n_r:2 * n_p + n_r + n_v]
        out_refs = refs[2 * n_p + n_r + n_v + len(deps):]
        p = None
        for j, (_, _, tb) in enumerate(prods):
            t = _dot(ab[2 * j][...].astype(BF16), ab[2 * j + 1][...], 1, 1 if tb else 0)
            p = t if p is None else p + t
        vals = epilogue(p, *[r[...] for r in row_refs], *[v[...] for v in vec_refs])
        for (dtype, kind), o_ref, val in zip(outs, out_refs, vals):
            if kind == "row":
                o_ref[...] = val.astype(dtype)
            else:
                @pl.when(pl.program_id(0) == 0)
                def _(o_ref=o_ref):
                    o_ref[...] = jnp.zeros_like(o_ref)

                o_ref[...] += val

    row = lambda w: pl.BlockSpec((tm, w), lambda i: (i, 0))
    whole = lambda a: pl.BlockSpec(a.shape, lambda i: (0,) * a.ndim, pipeline_mode=pl.Buffered(1))
    in_specs, args = [], []
    for a, b, _ in prods:
        in_specs += [row(a.shape[1]), whole(b)]
        args += [a, b]
    in_specs += [row(r.shape[1]) for r in rows_in] + [whole(v) for v in vecs_in] + [ANY_SPEC] * len(deps)
    return pl.pallas_call(
        body, name=name,
        out_shape=tuple(jax.ShapeDtypeStruct((m, n) if kind == "row" else (8, n), dtype) for dtype, kind in outs),
        grid=(m // tm,), in_specs=in_specs,
        out_specs=tuple(row(n) if kind == "row" else pl.BlockSpec((8, n), lambda i: (0, 0)) for _, kind in outs),
        compiler_params=_params(dimension_semantics=("arbitrary",)),
    )(*args, *rows_in, *vecs_in, *deps)


def _rstd(x):
    return lax.rsqrt(jnp.mean(x * x, axis=-1, keepdims=True) + EPS)


def _norm_bwd(xh, r, t):
    return r * (t - xh * jnp.mean(xh * t, axis=-1, keepdims=True))


ROW_F32, ROW_BF16, SUM_F32 = (F32, "row"), (BF16, "row"), (F32, "sum")


def _then(epilogue, index, tb):
    def run(p, *args):
        vals = epilogue(p, *args[:-1])
        return (*vals, _dot(vals[index].astype(BF16), args[-1], 1, 1 if tb else 0))

    return run


def _ep_post_pre(p, h, g_post, g_pre):
    y = p.astype(BF16)
    yf = y.astype(F32)
    hn = h + yf * _rstd(yf) * g_post
    return y, hn, hn * _rstd(hn) * g_pre


_EP_POST_PRE_OUTS = [ROW_BF16, ROW_F32, ROW_BF16]


def _ep_final_loss(y, h, target, g_post):
    r = _rstd(y)
    yh = y * r
    err = h + yh * g_post - target
    dh = err * (1.0 / D)
    return _rowsum8(err * err), dh, _norm_bwd(yh, r, dh * g_post), _rowsum8(dh * yh)


def _ep_post_pre_bwd(du, dh_out, hn, y, g_post, g_pre):
    r2 = _rstd(hn)
    xh = hn * r2
    dh = dh_out + _norm_bwd(xh, r2, du * g_pre)
    yf = y.astype(F32)
    r1 = _rstd(yf)
    yh = yf * r1
    return dh, _norm_bwd(yh, r1, dh * g_post), _rowsum8(du * xh), _rowsum8(dh * yh)


_EP_POST_PRE_BWD_OUTS = [ROW_F32, ROW_BF16, SUM_F32, SUM_F32]


def _ep_pre_bwd(du, dh_out, x, g):
    r = _rstd(x)
    xh = x * r
    return dh_out + _norm_bwd(xh, r, du * g), _rowsum8(du * xh)


_EP_PRE_BWD_OUTS = [ROW_F32, SUM_F32]


def _prenorm(x, g, *, name, dep=None):
    t, d = x.shape
    tb = min(512, t)
    deps = [] if dep is None else [dep]

    def body(x_ref, g_ref, *rest):
        xf = x_ref[...]
        rest[-1][...] = (xf * _rstd(xf) * g_ref[...]).astype(BF16)

    return pl.pallas_call(
        body, name=name, out_shape=jax.ShapeDtypeStruct((t, d), BF16), grid=(t // tb,),
        in_specs=[pl.BlockSpec((tb, d), lambda i: (i, 0)), pl.BlockSpec((1, d), lambda i: (0, 0))]
        + [ANY_SPEC] * len(deps),
        out_specs=pl.BlockSpec((tb, d), lambda i: (i, 0)), compiler_params=_params(),
    )(x, g, *deps)


QB = 256


def _half_mask(shape, e):
    lane = lax.broadcasted_iota(jnp.int32, shape, len(shape) - 1)
    return (lane // 64) == e


def _place(kv):
    sw = pltpu.roll(kv, 64, 1)
    m0 = _half_mask(kv.shape, 0)
    return [[jnp.where(m0, kv, 0.0).astype(BF16), jnp.where(m0, 0.0, sw).astype(BF16)],
            [jnp.where(m0, sw, 0.0).astype(BF16), jnp.where(m0, 0.0, kv).astype(BF16)]]


SQ = 128
SK = 256


def _swa_valid(i, sb):
    qc = lax.broadcasted_iota(jnp.int32, (SQ, SK), 0) // CHUNK
    kc = lax.broadcasted_iota(jnp.int32, (SQ, SK), 1) // CHUNK - 2
    return (kc <= qc) & (qc <= kc + 2) & (4 * i + 2 * sb + kc >= 0)


def _swa_fwd(z, sinks, t, dep=None):
    nb = t // QB
    deps = [] if dep is None else [dep]

    def body(s_ref, q_ref, kp_ref, kc_ref, vp_ref, vc_ref, *rest):
        o_ref, lse_ref = rest[-2:]
        i = pl.program_id(0)
        kpl = _place(jnp.concatenate([kp_ref[...], kc_ref[...]], axis=0))
        vpl = _place(jnp.concatenate([vp_ref[...], vc_ref[...]], axis=0))
        lane = lax.broadcasted_iota(jnp.int32, (SQ, 128), 1)
        for sb in range(QB // SQ):
            rows, keys = slice(SQ * sb, SQ * (sb + 1)), slice(SQ * sb, SQ * sb + SK)
            valid = _swa_valid(i, sb)
            lse_out = jnp.zeros((SQ, 128), F32)
            for j in range(4):
                qp = q_ref[rows, 128 * j:128 * (j + 1)].astype(BF16)
                acc = jnp.zeros((SQ, 128), F32)
                for e in range(2):
                    h = 2 * j + e
                    kvh = h // 4
                    qm = jnp.where(_half_mask(qp.shape, e), qp, jnp.zeros_like(qp))
                    s = _dot(qm, kpl[kvh][e][keys], 1, 1) * 0.125
                    s = jnp.where(valid, s, NEG)
                    sink = s_ref[0, h]
                    m = jnp.maximum(jnp.max(s, axis=-1, keepdims=True), sink)
                    p = jnp.exp(s - m)
                    l = jnp.sum(p, axis=-1, keepdims=True) + jnp.exp(sink - m)
                    acc = acc + _dot(p.astype(BF16), vpl[kvh][e][keys], 1, 0) * (1.0 / l)
                    lse_out = jnp.where(lane == h, m + jnp.log(l), lse_out)
                o_ref[rows, 128 * j:128 * (j + 1)] = acc.astype(BF16)
            lse_ref[rows, :] = lse_out

    prev = lambda c: pl.BlockSpec((128, 128), lambda i: (jnp.maximum(2 * i - 1, 0), c))
    cur = lambda c: pl.BlockSpec((QB, 128), lambda i: (i, c))
    return pl.pallas_call(
        body, name="swa_fwd",
        out_shape=(jax.ShapeDtypeStruct((t, D), BF16), jax.ShapeDtypeStruct((t, 128), F32)),
        grid=(nb,),
        in_specs=[pl.BlockSpec(memory_space=pltpu.SMEM),
                  pl.BlockSpec((QB, SWA_W), lambda i: (i, 0)), prev(4), cur(4), prev(5), cur(5)]
        + [ANY_SPEC] * len(deps),
        out_specs=(pl.BlockSpec((QB, SWA_W), lambda i: (i, 0)), pl.BlockSpec((QB, 128), lambda i: (i, 0))),
        compiler_params=_params(),
    )(sinks, z, z, z, z, z, *deps)


def _swa_bwd(z, sinks, ymix, lse, dymix, t, dep=None):
    nb = t // QB
    deps = [] if dep is None else [dep]

    def body(s_ref, q_ref, kp_ref, kc_ref, vp_ref, vc_ref, o_ref, do_ref, l_ref, *rest):
        dq_ref, first_ref, second_ref, ds_ref, carry_ref = rest[len(deps):]
        i = pl.program_id(0)
        live = i < nb

        @pl.when(i == 0)
        def _():
            ds_ref[...] = jnp.zeros_like(ds_ref)
            carry_ref[...] = jnp.zeros_like(carry_ref)

        lane = lax.broadcasted_iota(jnp.int32, (8, 128), 1)
        kpl = _place(jnp.concatenate([kp_ref[...], kc_ref[...]], axis=0))
        vpl = _place(jnp.concatenate([vp_ref[...], vc_ref[...]], axis=0))
        nk = QB + 128
        qc = lax.broadcasted_iota(jnp.int32, (QB, nk), 0) // CHUNK
        kc = lax.broadcasted_iota(jnp.int32, (QB, nk), 1) // CHUNK - 2
        valid = (kc <= qc) & (qc <= kc + 2) & (4 * i + kc >= 0) & live
        lse_c = l_ref[...]
        dsink = jnp.zeros((8, 128), F32)
        dk_acc = [[jnp.zeros((128, nk), F32) for _ in range(2)] for _ in range(2)]
        dv_acc = [[jnp.zeros((128, nk), F32) for _ in range(2)] for _ in range(2)]
        dq = []
        for j in range(4):
            cols = slice(128 * j, 128 * (j + 1))
            qp = q_ref[:, cols].astype(BF16)
            dop = do_ref[:, cols]
            prod = dop.astype(F32) * o_ref[:, cols].astype(F32)
            acc = jnp.zeros((QB, 128), F32)
            for e in range(2):
                h = 2 * j + e
                kvh = h // 4
                hm = _half_mask(qp.shape, e)
                qm = jnp.where(hm, qp, jnp.zeros_like(qp))
                dom = jnp.where(hm, dop, jnp.zeros_like(dop))
                dd = jnp.sum(jnp.where(hm, prod, 0.0), axis=-1, keepdims=True)
                lse_h = lse_c[:, h:h + 1]
                s = _dot(qm, kpl[kvh][e], 1, 1) * 0.125
                p = jnp.where(valid, jnp.exp(s - lse_h), 0.0)
                dp = _dot(dom, vpl[kvh][e], 1, 1)
                ds = (p * (dp - dd) * 0.125).astype(BF16)
                acc = acc + _dot(ds, kpl[kvh][e], 1, 0)
                dk_acc[kvh][e] = dk_acc[kvh][e] + _dot(qm, ds, 0, 0)
                dv_acc[kvh][e] = dv_acc[kvh][e] + _dot(dom, p.astype(BF16), 0, 0)
                ps = jnp.where(live, jnp.exp(s_ref[0, h] - lse_h) * dd, 0.0)
                dsink = dsink - jnp.where(lane == h, _rowsum8(jnp.broadcast_to(ps, (QB, 128))), 0.0)
            dq.append(acc.astype(BF16))
        ds_ref[...] += dsink
        dk = (dk_acc[0][0] + dk_acc[1][1] + pltpu.roll(dk_acc[0][1] + dk_acc[1][0], 64, 0)).T
        dv = (dv_acc[0][0] + dv_acc[1][1] + pltpu.roll(dv_acc[0][1] + dv_acc[1][0], 64, 0)).T
        dkv = jnp.concatenate([dk, dv], axis=1)
        second_ref[...] = (carry_ref[...] + dkv[0:128]).astype(BF16)
        carry_ref[...] = dkv[256:384]

        @pl.when(live)
        def _():
            for j in range(4):
                dq_ref[:, 128 * j:128 * (j + 1)] = dq[j]
            first_ref[...] = dkv[128:256].astype(BF16)

    blk = lambda i: jnp.minimum(i, nb - 1)
    prev = lambda c: pl.BlockSpec((128, 128), lambda i: (jnp.maximum(2 * blk(i) - 1, 0), c))
    cur = lambda w, c: pl.BlockSpec((QB, w), lambda i: (blk(i), c))
    half = lambda index: pl.BlockSpec((128, 256), lambda i: (index(i), 0))
    return pl.pallas_call(
        body, name="swa_bwd",
        out_shape=(jax.ShapeDtypeStruct((t, SWA_W), BF16), jax.ShapeDtypeStruct((t // 2, 256), BF16),
                   jax.ShapeDtypeStruct((t // 2, 256), BF16), jax.ShapeDtypeStruct((8, 128), F32)),
        grid=(nb + 1,),
        in_specs=[pl.BlockSpec(memory_space=pltpu.SMEM),
                  cur(SWA_W, 0), prev(4), cur(128, 4), prev(5), cur(128, 5),
                  cur(SWA_W, 0), cur(SWA_W, 0), cur(128, 0)] + [ANY_SPEC] * len(deps),
        out_specs=(cur(SWA_W, 0), half(blk), half(lambda i: jnp.maximum(i - 1, 0)),
                   pl.BlockSpec((8, 128), lambda i: (0, 0))),
        scratch_shapes=[pltpu.VMEM((128, 256), F32)],
        compiler_params=_params(dimension_semantics=("arbitrary",)),
    )(sinks, z, z, z, z, z, ymix, dymix, lse, *deps)


HB = 256


def _lower_bound(lb_ref):
    a = lb_ref[...]
    a0, a1 = a[0:1], a[1:2]
    mx = jnp.maximum(a0, a1)
    e0, e1 = jnp.exp(a0 - mx), jnp.exp(a1 - mx)
    return e0 / (e0 + e1)


def _hgrn_cols(row_block):
    return [pl.BlockSpec((HB, 2 * HD), lambda j, c=base // (2 * HD) + p: (row_block(j), c))
            for base in (ZQH, ZFH, ZIH, ZGH) for p in range(2)]


NCH = HB // CHUNK


def _split3(x):
    hi = x.astype(BF16)
    r1 = x - hi.astype(F32)
    mid = r1.astype(BF16)
    return hi, mid, (r1 - mid.astype(F32)).astype(BF16)


def _blockdiag(lower):
    r = lax.broadcasted_iota(jnp.int32, (HB, HB), 0)
    c = lax.broadcasted_iota(jnp.int32, (HB, HB), 1)
    return (r // CHUNK == c // CHUNK) & ((c <= r) if lower else (c >= r))


def _chunk_sums(mask_bf16, x):
    return sum(_dot(mask_bf16, part, 1, 0) for part in _split3(x))


def _per_chunk_rows(x, row):
    w = x.shape[1]
    picked = x.reshape(NCH, CHUNK, w)[:, row:row + 1, :]
    return jnp.broadcast_to(picked, (NCH, CHUNK, w)).reshape(HB, w)


def _chunk_stack(x, chunk_of_row):
    return jnp.concatenate([jnp.where(chunk_of_row == c, x, jnp.zeros_like(x)) for c in range(NCH)], axis=1)


def _chunk_pick(x, chunk_of_row):
    w = x.shape[1] // NCH
    out = jnp.zeros((HB, w), x.dtype)
    for c in range(NCH):
        out = jnp.where(chunk_of_row == c, x[:, c * w:(c + 1) * w], out)
    return out


def _hgrn_local(q, f, kf, b):
    sq = _sig(q)
    qf = q * sq * (HD ** -0.5)
    b_mid = _per_chunk_rows(b, CHUNK // 2 - 1)
    b_last = _per_chunk_rows(b, CHUNK - 1)
    qm = qf * jnp.exp(b - b_mid)
    km = kf * jnp.exp(b_mid - b)
    kl = kf * jnp.exp(b_last - b)
    qb = qf * jnp.exp(b)
    return dict(sq=sq, b_mid=b_mid, b_last=b_last, qm=qm, km=km, kl=kl, qb=qb)


def _hgrn2_fwd(z, hgrn_lb, onorm, ymix, t, dep=None):
    nb = t // HB
    deps = [] if dep is None else [dep]

    def body(*refs):
        zq, zf, zi, zg = refs[0:2], refs[2:4], refs[4:6], refs[6:8]
        (lb_ref, on_ref), (y_ref, o_ref, sp_ref, st_ref) = refs[8:10], refs[-4:]

        @pl.when(pl.program_id(0) == 0)
        def _():
            st_ref[...] = jnp.zeros_like(st_ref)

        lb_all = _lower_bound(lb_ref)
        gn = on_ref[...]
        low = _blockdiag(True)
        low_b = low.astype(BF16)
        chunk_of_row = lax.broadcasted_iota(jnp.int32, (HB, HD), 0) // CHUNK
        for p in range(2):
            lbp = lb_all[:, 2 * HD * p:2 * HD * (p + 1)]
            fp = lbp + (1.0 - lbp) * _sig(zf[p][...])
            bp = _chunk_sums(low_b, jnp.log(fp))
            for e in range(2):
                h, ls = 2 * p + e, slice(e * HD, (e + 1) * HD)
                f = fp[:, ls]
                w = _hgrn_local(zq[p][:, ls], f, 1.0 - f, bp[:, ls])
                iv = zi[p][:, ls].astype(BF16)
                a = jnp.where(low, _dot(w["qm"].astype(BF16), w["km"].astype(BF16), 1, 1), 0.0)
                o = _dot(a.astype(BF16), iv, 1, 0)
                u = _dot(iv, _chunk_stack(w["kl"].astype(BF16), chunk_of_row), 0, 0)
                decay = jnp.exp(w["b_last"])
                st = st_ref[h]
                states = []
                for c in range(NCH):
                    sp_ref[h, c] = st
                    states.append(st.astype(BF16))
                    st = st * decay[c * CHUNK:c * CHUNK + 1] + u[:, c * HD:(c + 1) * HD]
                st_ref[h] = st
                inter = _dot(w["qb"].astype(BF16), jnp.concatenate(states, axis=0), 1, 1)
                o = o + _chunk_pick(inter, chunk_of_row)
                hs = slice(h * HD, (h + 1) * HD)
                o_ref[:, hs] = o
                gg = zg[p][:, ls]
                y_ref[:, hs] = (o * _rstd(o) * gn * (gg * _sig(gg))).astype(BF16)

    return pl.pallas_call(
        body, name="hgrn_fwd",
        out_shape=(jax.ShapeDtypeStruct((t, D), BF16), jax.ShapeDtypeStruct((t, HG_W), F32),
                   jax.ShapeDtypeStruct((4, t // CHUNK, HD, HD), F32)),
        grid=(nb,),
        in_specs=_hgrn_cols(lambda j: j) + [pl.BlockSpec((2, HG_W), lambda j: (0, 0)),
                                            pl.BlockSpec((1, HD), lambda j: (0, 0)), ANY_SPEC]
        + [ANY_SPEC] * len(deps),
        out_specs=(pl.BlockSpec((HB, HG_W), lambda j: (j, 1)),
                   pl.BlockSpec((HB, HG_W), lambda j: (j, 0)),
                   pl.BlockSpec((4, NCH, HD, HD), lambda j: (0, j, 0, 0))),
        scratch_shapes=[pltpu.VMEM((4, HD, HD), F32)],
        input_output_aliases={10: 0},
        compiler_params=_params(dimension_semantics=("arbitrary",)),
    )(*[z] * 8, hgrn_lb, onorm, ymix, *deps)


def _hgrn2_bwd(z, hgrn_lb, onorm, o_save, sprev, dymix, dza, t):
    nb = t // HB

    def body(*refs):
        zq, zf, zi, zg = refs[0:2], refs[2:4], refs[4:6], refs[6:8]
        (lb_ref, on_ref, o_ref, sp_ref, dy_ref, dqa_ref, first_ref, second_ref,
         dz_ref, dlb_ref, don_ref, dst_ref) = refs[8:]

        @pl.when(pl.program_id(0) == 0)
        def _():
            dst_ref[...] = jnp.zeros_like(dst_ref)
            dlb_ref[...] = jnp.zeros_like(dlb_ref)
            don_ref[...] = jnp.zeros_like(don_ref)

        dz_ref[:, 0:SWA_W] = dqa_ref[...]
        dz_ref[0:HB // 2, SWA_W:ZQH] = first_ref[...]
        dz_ref[HB // 2:HB, SWA_W:ZQH] = second_ref[...]
        lb_all = _lower_bound(lb_ref)
        gn = on_ref[...]
        low, upp = _blockdiag(True), _blockdiag(False)
        upp_b = upp.astype(BF16)
        low_b = low.astype(BF16)
        row = lax.broadcasted_iota(jnp.int32, (HB, HD), 0)
        chunk_of_row = row // CHUNK
        in_chunk = row % CHUNK
        for p in range(2):
            lbp = lb_all[:, 2 * HD * p:2 * HD * (p + 1)]
            sgp = _sig(zf[p][...])
            fp = lbp + (1.0 - lbp) * sgp
            bp = _chunk_sums(low_b, jnp.log(fp))
            db_pair, dkf_pair = [], []
            for e in range(2):
                h, ls, hs = 2 * p + e, slice(e * HD, (e + 1) * HD), slice((2 * p + e) * HD, (2 * p + e + 1) * HD)
                f = fp[:, ls]
                q = zq[p][:, ls]
                w = _hgrn_local(q, f, 1.0 - f, bp[:, ls])
                iv = zi[p][:, ls].astype(BF16)
                gg = zg[p][:, ls]
                o = o_ref[:, hs]
                dout = dy_ref[:, hs].astype(F32)
                sgg = _sig(gg)
                r = _rstd(o)
                oh = o * r
                dyn = dout * (gg * sgg)
                dz_ref[:, ZGH + h * HD:ZGH + (h + 1) * HD] = (
                    dout * oh * gn * (sgg * (1.0 + gg * (1.0 - sgg)))).astype(BF16)
                don_ref[...] += _rowsum8(dyn * oh)
                do = _norm_bwd(oh, r, dyn * gn).astype(BF16)
                qm, km, kl, qb = (w[n].astype(BF16) for n in ("qm", "km", "kl", "qb"))
                decay = jnp.exp(w["b_last"])
                grads_in = _dot(do, _chunk_stack(qb, chunk_of_row), 0, 0)
                dst = dst_ref[h]
                dstn, dd_rows = [None] * NCH, [None] * NCH
                for c in reversed(range(NCH)):
                    dstn[c] = dst.astype(BF16)
                    dd_rows[c] = jnp.sum(dst * sp_ref[h, c], axis=0, keepdims=True)
                    dst = dst * decay[c * CHUNK:c * CHUNK + 1] + grads_in[:, c * HD:(c + 1) * HD]
                dst_ref[h] = dst
                states = jnp.concatenate([sp_ref[h, c].astype(BF16) for c in range(NCH)], axis=0)
                dstn_all = jnp.concatenate(dstn, axis=0)
                dqb = _dot(_chunk_stack(do, chunk_of_row), states, 1, 0)
                at = jnp.where(upp, _dot(km, qm, 1, 1), 0.0)
                di = _dot(at.astype(BF16), do, 1, 0) + _chunk_pick(_dot(kl, dstn_all, 1, 1), chunk_of_row)
                dz_ref[:, ZIH + h * HD:ZIH + (h + 1) * HD] = di.astype(BF16)
                dkl = _dot(_chunk_stack(iv, chunk_of_row), dstn_all, 1, 0)
                da = jnp.where(low, _dot(do, iv, 1, 1), 0.0).astype(BF16)
                dat = jnp.where(upp, _dot(iv, do, 1, 1), 0.0).astype(BF16)
                dqm = _dot(da, km, 1, 0)
                dkm = _dot(dat, qm, 1, 0)
                b = bp[:, ls]
                e1, e2 = jnp.exp(b - w["b_mid"]), jnp.exp(w["b_mid"] - b)
                e3, e4 = jnp.exp(w["b_last"] - b), jnp.exp(b)
                dqf = dqm * e1 + dqb * e4
                dkf_pair.append(dkm * e2 + dkl * e3)
                t_qm, t_km, t_kl = dqm * w["qm"], dkm * w["km"], dkl * w["kl"]
                db = t_qm - t_km - t_kl + dqb * w["qb"]
                db_mid = jnp.sum((t_km - t_qm).reshape(NCH, CHUNK, HD), axis=1, keepdims=True)
                db_last = jnp.sum(t_kl.reshape(NCH, CHUNK, HD), axis=1, keepdims=True)
                db_last = db_last + jnp.stack(dd_rows, axis=0) * jnp.exp(
                    bp[:, ls].reshape(NCH, CHUNK, HD)[:, CHUNK - 1:CHUNK, :])
                spread = lambda v: jnp.broadcast_to(v, (NCH, CHUNK, HD)).reshape(HB, HD)
                db = (db + jnp.where(in_chunk == CHUNK // 2 - 1, spread(db_mid), 0.0)
                      + jnp.where(in_chunk == CHUNK - 1, spread(db_last), 0.0))
                db_pair.append(db)
                sq = w["sq"]
                dz_ref[:, ZQH + h * HD:ZQH + (h + 1) * HD] = (
                    dqf * (HD ** -0.5) * (sq * (1.0 + q * (1.0 - sq)))).astype(BF16)
            dlogf = _chunk_sums(upp_b, jnp.concatenate(db_pair, axis=1))
            dfv = dlogf / fp - jnp.concatenate(dkf_pair, axis=1)
            dz_ref[:, ZFH + 2 * HD * p:ZFH + 2 * HD * (p + 1)] = (dfv * (1.0 - lbp) * sgp * (1.0 - sgp)).astype(BF16)
            dlb_ref[:, 2 * HD * p:2 * HD * (p + 1)] += _rowsum8(dfv * (1.0 - sgp))

    rev = lambda j: nb - 1 - j
    return pl.pallas_call(
        body, name="hgrn_bwd",
        out_shape=(jax.ShapeDtypeStruct((t, D_IN), BF16), jax.ShapeDtypeStruct((8, HG_W), F32),
                   jax.ShapeDtypeStruct((8, HD), F32)),
        grid=(nb,),
        in_specs=_hgrn_cols(rev) + [pl.BlockSpec((2, HG_W), lambda j: (0, 0)), pl.BlockSpec((1, HD), lambda j: (0, 0)),
                                    pl.BlockSpec((HB, HG_W), lambda j: (rev(j), 0)),
                                    pl.BlockSpec((4, NCH, HD, HD), lambda j: (0, rev(j), 0, 0)),
                                    pl.BlockSpec((HB, HG_W), lambda j: (rev(j), 1)),
                                    pl.BlockSpec((HB, SWA_W), lambda j: (rev(j), 0)),
                                    pl.BlockSpec((HB // 2, 2 * KV_W), lambda j: (rev(j), 0)),
                                    pl.BlockSpec((HB // 2, 2 * KV_W), lambda j: (rev(j), 0))],
        out_specs=(pl.BlockSpec((HB, D_IN), lambda j: (rev(j), 0)), pl.BlockSpec((8, HG_W), lambda j: (0, 0)),
                   pl.BlockSpec((8, HD), lambda j: (0, 0))),
        scratch_shapes=[pltpu.VMEM((4, HD, HD), F32)],
        compiler_params=_params(dimension_semantics=("arbitrary",)),
    )(*[z] * 8, hgrn_lb, onorm, o_save, sprev, dymix, *dza)


XB = 512


def _xattn_fwd(q, k, v, wo, h, g_post, g_pre, t, dep=None):
    tb = min(XB, t)
    deps = [] if dep is None else [dep]

    def body(q_ref, k_ref, v_ref, wo_ref, h_ref, gp_ref, gn_ref, *rest):
        o_ref, y_ref, hn_ref, u_ref = rest[len(deps):]
        for hd in range(XH):
            cols = slice(XD * hd, XD * (hd + 1))
            s = _dot(q_ref[:, cols], k_ref[:, cols], 1, 1) * (XD ** -0.5)
            p = jnp.exp(s - jnp.max(s, axis=-1, keepdims=True))
            l = jnp.sum(p, axis=-1, keepdims=True)
            o_ref[:, cols] = (_dot(p.astype(BF16), v_ref[:, cols], 1, 0) * (1.0 / l)).astype(BF16)
        y, hn, u = _ep_post_pre(_dot(o_ref[...], wo_ref[...], 1, 0), h_ref[...], gp_ref[...], gn_ref[...])
        y_ref[...] = y
        hn_ref[...] = hn
        u_ref[...] = u.astype(BF16)

    row = pl.BlockSpec((tb, D), lambda i: (i, 0))
    whole = lambda a: pl.BlockSpec(a.shape, lambda i: (0,) * a.ndim, pipeline_mode=pl.Buffered(1))
    half = jax.ShapeDtypeStruct((t, D), BF16)
    return pl.pallas_call(
        body, name="xattn_fwd", out_shape=(half, half, jax.ShapeDtypeStruct((t, D), F32), half), grid=(t // tb,),
        in_specs=[row, whole(k), whole(v), whole(wo), row, whole(g_post), whole(g_pre)] + [ANY_SPEC] * len(deps),
        out_specs=(row, row, row, row), compiler_params=_params(),
    )(q, k, v, wo, h, g_post, g_pre, *deps)


def _xattn_bwd(q, k, v, do, wq, wout, dh_out, hn, y, g_post, g_pre, t, dep=None):
    tb = min(XB, t)
    deps = [] if dep is None else [dep]

    def body(q_ref, k_ref, v_ref, do_ref, wq_ref, wout_ref, dho_ref, hn_ref, y_ref, gp_ref, gn_ref, *rest):
        dq_ref, dk_ref, dv_ref, dh_ref, dyp_ref, dym_ref, dgn_ref, dgp_ref = rest[len(deps):]

        @pl.when(pl.program_id(0) == 0)
        def _():
            dk_ref[...] = jnp.zeros_like(dk_ref)
            dv_ref[...] = jnp.zeros_like(dv_ref)
            dgn_ref[...] = jnp.zeros_like(dgn_ref)
            dgp_ref[...] = jnp.zeros_like(dgp_ref)

        for h in range(XH):
            cols = slice(XD * h, XD * (h + 1))
            qh, kh, vh, doh = q_ref[:, cols], k_ref[:, cols], v_ref[:, cols], do_ref[:, cols]
            s = _dot(qh, kh, 1, 1) * (XD ** -0.5)
            p = jnp.exp(s - jnp.max(s, axis=-1, keepdims=True))
            p = p * (1.0 / jnp.sum(p, axis=-1, keepdims=True))
            dp = _dot(doh, vh, 1, 1)
            ds = (p * (dp - jnp.sum(p * dp, axis=-1, keepdims=True)) * (XD ** -0.5)).astype(BF16)
            dq_ref[:, cols] = _dot(ds, kh, 1, 0).astype(BF16)
            dk_ref[:, cols] += _dot(ds, qh, 0, 0)
            dv_ref[:, cols] += _dot(p.astype(BF16), doh, 0, 0)
        du = _dot(dq_ref[...], wq_ref[...], 1, 1)
        dh, dyp, dgn, dgp = _ep_post_pre_bwd(du, dho_ref[...], hn_ref[...], y_ref[...], gp_ref[...], gn_ref[...])
        dh_ref[...] = dh
        dyp = dyp.astype(BF16)
        dyp_ref[...] = dyp
        dym_ref[...] = _dot(dyp, wout_ref[...], 1, 1).astype(BF16)
        dgn_ref[...] += dgn
        dgp_ref[...] += dgp

    row = pl.BlockSpec((tb, D), lambda i: (i, 0))
    mem = pl.BlockSpec(k.shape, lambda i: (0, 0))
    whole = lambda a: pl.BlockSpec(a.shape, lambda i: (0,) * a.ndim, pipeline_mode=pl.Buffered(1))
    acc = pl.BlockSpec((8, D), lambda i: (0, 0))
    half = jax.ShapeDtypeStruct((t, D), BF16)
    return pl.pallas_call(
        body, name="xattn_bwd",
        out_shape=(half, jax.ShapeDtypeStruct(k.shape, F32), jax.ShapeDtypeStruct(k.shape, F32),
                   jax.ShapeDtypeStruct((t, D), F32), half, half,
                   jax.ShapeDtypeStruct((8, D), F32), jax.ShapeDtypeStruct((8, D), F32)),
        grid=(t // tb,),
        in_specs=[row, whole(k), whole(v), row, whole(wq), whole(wout), row, row, row, whole(g_post), whole(g_pre)]
        + [ANY_SPEC] * len(deps),
        out_specs=(row, mem, mem, row, row, row, acc, acc),
        compiler_params=_params(dimension_semantics=("arbitrary",)),
    )(q, k, v, do, wq, wout, dh_out, hn, y, g_post, g_pre, *deps)


def _mem_kv(mem, g_mem, wk, wv):
    def body(m_ref, g_ref, wk_ref, wv_ref, mn_ref, k_ref, v_ref):
        m_ = m_ref[...]
        mn = (m_ * _rstd(m_) * g_ref[...]).astype(BF16)
        mn_ref[...] = mn
        k_ref[...] = _dot(mn, wk_ref[...], 1, 0).astype(BF16)
        v_ref[...] = _dot(mn, wv_ref[...], 1, 0).astype(BF16)

    return pl.pallas_call(body, name="mem_kv", out_shape=(jax.ShapeDtypeStruct(mem.shape, BF16),) * 3,
                          compiler_params=_params())(mem, g_mem, wk, wv)


def _mem_kv_bwd(mn, mem, dk, dv, wk, wv, dep=None):
    deps = [] if dep is None else [dep]

    def body(mn_ref, m_ref, dk_ref, dv_ref, wk_ref, wv_ref, *rest):
        gk_ref, gv_ref, dg_ref = rest[len(deps):]
        mn = mn_ref[...]
        dkb, dvb = dk_ref[...].astype(BF16), dv_ref[...].astype(BF16)
        gk_ref[...] = _dot(mn, dkb, 0, 0).astype(BF16)
        gv_ref[...] = _dot(mn, dvb, 0, 0).astype(BF16)
        dmn = _dot(dkb, wk_ref[...], 1, 1) + _dot(dvb, wv_ref[...], 1, 1)
        m_ = m_ref[...]
        dg_ref[...] = _rowsum8(dmn * (m_ * _rstd(m_)))

    vmem = pl.BlockSpec(memory_space=pltpu.VMEM)
    return pl.pallas_call(
        body, name="mem_kv_bwd",
        out_shape=(jax.ShapeDtypeStruct(wk.shape, BF16), jax.ShapeDtypeStruct(wv.shape, BF16),
                   jax.ShapeDtypeStruct((8, D), F32)),
        in_specs=[vmem] * 6 + [ANY_SPEC] * len(deps), out_specs=(vmem,) * 3, compiler_params=_params(),
    )(mn, mem, dk, dv, wk, wv, *deps)


FB = 256


def _ffn_fwd_bwd(u, wgt, wut, wd, h, target, g_last, y_prev, g_post, g_pre, wo, t):
    tb = min(FB, t)

    def body(u_ref, wg_ref, wu_ref, wd_ref, h_ref, t_ref, gl_ref, yp_ref, gp_ref, gn_ref, wo_ref,
             a_ref, dy_ref, dg_ref, dup_ref, dh_ref, dyp_ref, do_ref, sq_ref, dgl_ref, dgn_ref, dgp_ref):
        @pl.when(pl.program_id(0) == 0)
        def _():
            for ref in (sq_ref, dgl_ref, dgn_ref, dgp_ref):
                ref[...] = jnp.zeros_like(ref)

        u_ = u_ref[...]
        g = _dot(u_, wg_ref[...], 1, 1)
        up = _dot(u_, wu_ref[...], 1, 1)
        sg = _sig(g)
        a = (g * sg * up).astype(BF16)
        a_ref[...] = a
        h_ = h_ref[...]
        sq, dh3, dy, dgl = _ep_final_loss(_dot(a, wd_ref[...], 1, 0), h_, t_ref[...], gl_ref[...])
        sq_ref[...] += sq
        dgl_ref[...] += dgl
        dy = dy.astype(BF16)
        dy_ref[...] = dy
        da = _dot(dy, wd_ref[...], 1, 1)
        dup = (da * g * sg).astype(BF16)
        dgate = (da * up * (sg * (1.0 + g * (1.0 - sg)))).astype(BF16)
        dup_ref[...] = dup
        dg_ref[...] = dgate
        du = _dot(dgate, wg_ref[...], 1, 0) + _dot(dup, wu_ref[...], 1, 0)
        dh, dyp, dgn, dgp = _ep_post_pre_bwd(du, dh3, h_, yp_ref[...], gp_ref[...], gn_ref[...])
        dh_ref[...] = dh
        dyp = dyp.astype(BF16)
        dyp_ref[...] = dyp
        do_ref[...] = _dot(dyp, wo_ref[...], 1, 1).astype(BF16)
        dgn_ref[...] += dgn
        dgp_ref[...] += dgp

    row = lambda w: pl.BlockSpec((tb, w), lambda i: (i, 0))
    whole = lambda a: pl.BlockSpec(a.shape, lambda i: (0,) * a.ndim, pipeline_mode=pl.Buffered(1))
    acc = pl.BlockSpec((8, D), lambda i: (0, 0))
    wide, half, sums = (jax.ShapeDtypeStruct((t, D_FF), BF16), jax.ShapeDtypeStruct((t, D), BF16),
                        jax.ShapeDtypeStruct((8, D), F32))
    return pl.pallas_call(
        body, name="ffn_fwd_bwd",
        out_shape=(wide, half, wide, wide, jax.ShapeDtypeStruct((t, D), F32), half, half, sums, sums, sums, sums),
        grid=(t // tb,),
        in_specs=[row(D), whole(wgt), whole(wut), whole(wd), row(D), row(D), whole(g_last), row(D), whole(g_post),
                  whole(g_pre), whole(wo)],
        out_specs=(row(D_FF), row(D), row(D_FF), row(D_FF), row(D), row(D), row(D), acc, acc, acc, acc),
        compiler_params=_params(dimension_semantics=("arbitrary",)),
    )(u, wgt, wut, wd, h, target, g_last, y_prev, g_post, g_pre, wo)


def _local_step(x, mem, target, fetch, sm, emit=None, first_dep=None, milestone=None):
    t = x.shape[0]
    w, gw = {}, {}

    def out(key, g):
        gw[key] = g
        return None if emit is None else emit(key, g)

    def tell(tag, value):
        return None if milestone is None else milestone(tag, value)
    u1 = _prenorm(x, sm["g_mix_pre"], name="prenorm_mix", dep=first_dep)
    w["winT"] = fetch("winT", u1)
    z = _mm_nt(u1, w["winT"], out_dtype=F32, tm=1024, tn=1408, name="mm_z")
    ymix, lse = _swa_fwd(z, sm["sinks"], t)
    ymix, o_h, sprev = _hgrn2_fwd(z, sm["hgrn_lb"], sm["hgrn_onorm"], ymix, t, dep=tell("swa", lse))
    for key in ("wout", "wq", "wk", "wv", "wo"):
        w[key] = fetch(key, ymix)
    y1, h1, u2, qx = _mm_rows([(ymix, w["wout"], False)], [x], [sm["g_mix_post"], sm["g_x_pre"], w["wq"]],
                              _then(_ep_post_pre, 2, False), _EP_POST_PRE_OUTS + [ROW_BF16], tm=512,
                              name="mm_y1_post_qx")
    mn, kx, vx = _mem_kv(mem, sm["g_mem"], w["wk"], w["wv"])
    ox, y2, h2, u3 = _xattn_fwd(qx, kx, vx, w["wo"], h1, sm["g_x_post"], sm["g_ffn_pre"], t, dep=tell("kv", kx))
    for key in ("wgT", "wuT", "wd"):
        w[key] = fetch(key, u3)
    act, dy3, dgate, dup, dh2, dy2, dox, sq, dg_ffn_post, dg_ffn_pre, dg_x_post = _ffn_fwd_bwd(
        u3, w["wgT"], w["wuT"], w["wd"], h2, target, sm["g_ffn_post"], y2, sm["g_x_post"], sm["g_ffn_pre"], w["wo"], t)
    dep = out("wd", *_mm_tn([act], [dy3], name="mm_gwd"))
    gwg, gwu = _mm_tn([dgate, dup], [u3], name="mm_gwg_gwu", dep=dep)
    out("wgT", gwg)
    dep = out("wuT", gwu)
    dqx, dkx, dvx, dh1, dy1, dymix, dg_x_pre, dg_mix_post = _xattn_bwd(
        qx, kx, vx, dox, w["wq"], w["wout"], dh2, h1, y1, sm["g_mix_post"], sm["g_x_pre"], t, dep=dep)
    gwo, gwq, gwout = _mm_tn([ox, u2, ymix], [dy2, dqx, dy1], name="mm_gwo_gwq_gwout")
    gwk, gwv, dg_mem = _mem_kv_bwd(mn, mem, dkx, dvx, w["wk"], w["wv"])
    for key, g in (("wo", gwo), ("wq", gwq), ("wout", gwout), ("wk", gwk), ("wv", gwv)):
        dep = out(key, g)
    *dza, dsinks = _swa_bwd(z, sm["sinks"], ymix, lse, dymix, t, dep=dep)
    dz, dlb, donorm = _hgrn2_bwd(z, sm["hgrn_lb"], sm["hgrn_onorm"], o_h, sprev, dymix, dza, t)
    dep = out("winT", *_mm_tn([dz], [u1], name="mm_gwin"))
    grad_x, dg_mix_pre = _mm_rows([(dz, w["winT"], False)], [dh1, x], [sm["g_mix_pre"]], _ep_pre_bwd,
                                  _EP_PRE_BWD_OUTS, tm=512, name="mm_du1_pre_bwd", dep=dep)
    parts = dict(g_mix_pre=dg_mix_pre, g_mix_post=dg_mix_post, g_mem=dg_mem, g_x_pre=dg_x_pre,
                 g_x_post=dg_x_post, g_ffn_pre=dg_ffn_pre, g_ffn_post=dg_ffn_post,
                 hgrn_onorm=donorm, hgrn_lb=dlb, sinks=dsinks, sq=sq)
    return grad_x, gw, parts


def _position():
    return lax.axis_index("x"), lax.axis_index("y"), lax.axis_index("c")


def _peer(pos, k):
    x, y, c = pos
    return (1 - x if k & 4 else x, 1 - y if k & 2 else y, 1 - c if k & 1 else c)


def _linear(pos):
    x, y, c = pos
    return 4 * x + 2 * y + c


HBM_SPEC = pl.BlockSpec(memory_space=pltpu.HBM)
SEM_SPEC = pl.BlockSpec(memory_space=pltpu.SEMAPHORE)
DATAFLOW = pltpu.SideEffectType.DATAFLOW_SIDE_EFFECTING
SEND_ORDER = (1, 2, 4, 3, 5, 6, 7)


def _in_hbm(a):
    return pltpu.with_memory_space_constraint(a, pltpu.HBM)


def _prepare_weights(shards, *, name, dep=None):
    n = len(shards)
    deps = [] if dep is None else [dep]

    def body(*refs):
        ins, (outs, lands, sem) = refs[:n], (refs[-2 * n - 1:-n - 1], refs[-n - 1:-1], refs[-1])
        me_lin = _linear(_position())
        copies = []
        for a in range(n):
            r = ins[a].shape[0]
            outs[a][...] = ins[a][...].astype(BF16)
            copies.append(pltpu.make_async_copy(outs[a], lands[a].at[pl.ds(me_lin * r, r), :], sem.at[a]))
            copies[-1].start()
        for cp in copies:
            cp.wait()

    vmem = pl.BlockSpec(memory_space=pltpu.VMEM)
    res = pl.pallas_call(
        body, name=name,
        out_shape=tuple(jax.ShapeDtypeStruct(s.shape, BF16) for s in shards)
        + tuple(jax.ShapeDtypeStruct((N_DEV * s.shape[0], s.shape[1]), BF16) for s in shards),
        in_specs=[vmem] * n + [ANY_SPEC] * len(deps), out_specs=tuple([vmem] * n + [ANY_SPEC] * n),
        scratch_shapes=[pltpu.SemaphoreType.DMA((n,))], compiler_params=_params(),
    )(*shards, *deps)
    return res[:n], res[n:]


def _copies_start(arrays, plan, n, *, name):
    na = len(arrays)

    def body(*refs):
        ins, send_sems, recv_sems = refs[:na], refs[na], refs[na + 1]
        me = _position()
        for j in range(n):
            src, dst, peer, _ = plan(ins, me, j)
            pltpu.make_async_remote_copy(src_ref=src, dst_ref=dst, send_sem=send_sems.at[j], recv_sem=recv_sems.at[j],
                                         device_id=peer, device_id_type=MESH).start()

    return pl.pallas_call(
        body, name=name,
        out_shape=(pltpu.SemaphoreType.DMA((n,)), pltpu.SemaphoreType.DMA((n,)))
        + tuple(pltpu.HBM(a.shape, a.dtype) for a in arrays),
        in_specs=(HBM_SPEC,) * na, out_specs=(SEM_SPEC, SEM_SPEC) + (HBM_SPEC,) * na,
        input_output_aliases={i: 2 + i for i in range(na)},
        compiler_params=pltpu.CompilerParams(has_side_effects=DATAFLOW),
    )(*[_in_hbm(a) for a in arrays])


def _copies_wait(send_sems, recv_sems, arrays, plan, n, after, *, name):
    na = len(arrays)

    def body(*refs):
        ins, send_sems, recv_sems = refs[:na], refs[na], refs[na + 1]
        me = _position()
        for j in range(n):
            src, _, peer, landed = plan(ins, me, j)
            copy = pltpu.make_async_remote_copy(src_ref=src, dst_ref=landed, send_sem=send_sems.at[j],
                                                recv_sem=recv_sems.at[j], device_id=peer, device_id_type=MESH)
            copy.wait_send()
            copy.wait_recv()

    return pl.pallas_call(
        body, name=name, out_shape=tuple(pltpu.HBM(a.shape, a.dtype) for a in arrays),
        in_specs=(HBM_SPEC,) * na + (SEM_SPEC, SEM_SPEC, ANY_SPEC), out_specs=(HBM_SPEC,) * na,
        input_output_aliases={i: i for i in range(na)},
        compiler_params=pltpu.CompilerParams(has_side_effects=DATAFLOW),
    )(*arrays, send_sems, recv_sems, after)


SAME_CORE = (2, 4, 6)


class _TwoLevelGather:
    def __init__(self, shards, lands, *, name):
        n = self.n = len(shards)
        self.name = name
        first_peers = (1,) + SAME_CORE

        def rows(ref, pos):
            r = ref.shape[0] // N_DEV
            return ref.at[pl.ds(_linear(pos) * r, r), :]

        def first(refs, me, j):
            a, peer = j // 4, _peer(me, first_peers[j % 4])
            return refs[a], rows(refs[n + a], me), peer, rows(refs[n + a], peer)

        def second(refs, me, j):
            a, sibling = j // 3, _peer(me, 1)
            mine = rows(refs[a], _peer(me, SAME_CORE[j % 3]))
            return mine, mine, sibling, rows(refs[a], _peer(sibling, SAME_CORE[j % 3]))

        self._first, self._second = first, second
        self._flight = _copies_start(list(shards) + list(lands), first, 4 * n, name=name + "_send")
        self.dep = self._flight[2]

    def pass_on(self, after):
        send1, recv1, *arrays = self._flight
        arrays = _copies_wait(send1, recv1, arrays, self._first, 4 * self.n, after, name=self.name + "_recv")
        self._flight = _copies_start(list(arrays[self.n:]), self._second, 3 * self.n, name=self.name + "_pass")
        return self._flight[2]

    def finish(self, after):
        send2, recv2, *lands = self._flight
        return _copies_wait(send2, recv2, lands, self._second, 3 * self.n, after, name=self.name + "_pass_recv")


def _exchange_start(gs, *, name):
    n = len(gs)
    rows = [g.shape[0] // N_DEV for g in gs]
    lands = [lax.empty((N_DEV - 1, r, g.shape[1]), g.dtype) for g, r in zip(gs, rows)]

    def body(*refs):
        g_refs, land_refs = refs[:n], refs[n:2 * n]
        send_sems, recv_sems = refs[2 * n:3 * n], refs[3 * n:4 * n]
        me = _position()
        for a in range(n):
            for k in SEND_ORDER:
                peer = _peer(me, k)
                pltpu.make_async_remote_copy(
                    src_ref=g_refs[a].at[pl.ds(_linear(peer) * rows[a], rows[a]), :],
                    dst_ref=land_refs[a].at[k - 1],
                    send_sem=send_sems[a].at[k - 1], recv_sem=recv_sems[a].at[k - 1],
                    device_id=peer, device_id_type=MESH).start()

    res = pl.pallas_call(
        body, name=name,
        out_shape=tuple(pltpu.SemaphoreType.DMA((N_DEV - 1,)) for _ in range(2 * n))
        + tuple(pltpu.HBM(a.shape, a.dtype) for a in gs + lands),
        in_specs=(HBM_SPEC,) * (2 * n), out_specs=(SEM_SPEC,) * (2 * n) + (HBM_SPEC,) * (2 * n),
        input_output_aliases={i: 2 * n + i for i in range(2 * n)},
        compiler_params=pltpu.CompilerParams(has_side_effects=DATAFLOW),
    )(*[_in_hbm(a) for a in gs + lands])
    return [(res[a], res[n + a], res[2 * n + a], res[3 * n + a]) for a in range(n)]


def _exchange_wait(send_sems, recv_sems, g_thru, land_thru, after, *, name):
    r = land_thru.shape[1]

    def body(g_ref, land_ref, send_sems, recv_sems, after_ref, g_dead, got_ref):
        del after_ref, g_dead, got_ref
        me = _position()
        for k in SEND_ORDER:
            peer = _peer(me, k)
            copy = pltpu.make_async_remote_copy(
                src_ref=g_ref.at[pl.ds(_linear(peer) * r, r), :], dst_ref=land_ref.at[k - 1],
                send_sem=send_sems.at[k - 1], recv_sem=recv_sems.at[k - 1],
                device_id=peer, device_id_type=MESH)
            copy.wait_send()
            copy.wait_recv()

    return pl.pallas_call(
        body, name=name,
        out_shape=(pltpu.HBM(g_thru.shape, g_thru.dtype), pltpu.HBM(land_thru.shape, land_thru.dtype)),
        in_specs=(HBM_SPEC, HBM_SPEC, SEM_SPEC, SEM_SPEC, pl.BlockSpec(memory_space=pl.ANY)),
        out_specs=(HBM_SPEC, HBM_SPEC), input_output_aliases={0: 0, 1: 1},
        compiler_params=pltpu.CompilerParams(has_side_effects=DATAFLOW),
    )(g_thru, land_thru, send_sems, recv_sems, after)


ADAMW_TILE_ROWS = 256


def _adamw_math(w, g, m, v):
    m = B1 * m + (1.0 - B1) * g
    v = B2 * v + (1.0 - B2) * (g * g)
    delta = -LR * ((m / C1) / (jnp.sqrt(v / C2) + AEPS) + WD * w)
    return delta, m, v


def _sum_adamw(items, *, name):
    n = len(items)
    r, d = items[0][2].shape
    assert all(it[2].shape == (r, d) for it in items)
    rc = r // 2 if r > ADAMW_TILE_ROWS else r
    tiles = [(a, r0) for a in range(n) for r0 in range(0, r, rc)]
    n_in, n_out = 5, 4

    def body(*refs):
        ins, outs = refs[:n_in * n], refs[n_in * n:(n_in + n_out) * n]
        land_v, own_v, f32_v, sems = refs[(n_in + n_out) * n:]
        me_lin = _linear(_position())

        def loads(j):
            a, r0 = tiles[j]
            g_all, land, w, m, v = ins[n_in * a:n_in * a + n_in]
            rows = pl.ds(r0, rc)
            pairs = [(land.at[:, rows, :], land_v.at[j]), (g_all.at[pl.ds(me_lin * r + r0, rc), :], own_v.at[j]),
                     (w.at[rows, :], f32_v.at[j, 0]), (m.at[rows, :], f32_v.at[j, 1]), (v.at[rows, :], f32_v.at[j, 2])]
            return [pltpu.make_async_copy(src, dst, sems.at[j, i]) for i, (src, dst) in enumerate(pairs)]

        def stores(j):
            a, r0 = tiles[j]
            return [pltpu.make_async_copy(f32_v.at[j, 3 + i], outs[n_out * a + i].at[pl.ds(r0, rc), :],
                                          sems.at[j, n_in + i]) for i in range(n_out)]

        for j in range(len(tiles)):
            for cp in loads(j):
                cp.start()
        for j in range(len(tiles)):
            for cp in loads(j):
                cp.wait()
            g = land_v[j, 0].astype(F32)
            for s in range(1, N_DEV - 1):
                g = g + land_v[j, s].astype(F32)
            g = own_v[j].astype(F32) + g
            f32_v[j, 3] = g
            f32_v[j, 4], f32_v[j, 5], f32_v[j, 6] = _adamw_math(f32_v[j, 0], g, f32_v[j, 1], f32_v[j, 2])
            for cp in stores(j):
                cp.start()
        for j in range(len(tiles)):
            for cp in stores(j):
                cp.wait()

    nt = len(tiles)
    res = pl.pallas_call(
        body, name=name,
        out_shape=tuple(jax.ShapeDtypeStruct((r, d), F32) for _ in range(n_out * n)),
        in_specs=[ANY_SPEC] * (n_in * n), out_specs=(ANY_SPEC,) * (n_out * n),
        scratch_shapes=[pltpu.VMEM((nt, N_DEV - 1, rc, d), BF16), pltpu.VMEM((nt, rc, d), BF16),
                        pltpu.VMEM((nt, 3 + n_out, rc, d), F32), pltpu.SemaphoreType.DMA((nt, n_in + n_out))],
        compiler_params=_params(),
    )(*[a for it in items for a in it])
    return [res[n_out * a:n_out * a + n_out] for a in range(n)]


SMALL = ("g_mix_pre", "g_mix_post", "g_mem", "g_x_pre", "g_x_post", "g_ffn_pre", "g_ffn_post",
         "hgrn_onorm", "hgrn_lb", "sinks")
SMALL_W = dict(hgrn_onorm=HD, hgrn_lb=HG_W, sinks=8)
SQ_ROW = len(SMALL)
PACK_ROWS = 16


def _small_pack(parts):
    ns = len(SMALL)

    def body(*refs):
        part, mine, slots, sem = refs[:ns + 1], refs[ns + 1], refs[ns + 2], refs[ns + 3]
        mine[...] = jnp.zeros((PACK_ROWS, D), F32)
        for r, name in enumerate(SMALL):
            wd = SMALL_W.get(name, D)
            mine[r:r + 1, 0:wd] = jnp.sum(part[r][...], axis=0, keepdims=True)[:, 0:wd]
        sq = jnp.sum(part[ns][...]) * (0.5 / D)
        mine[SQ_ROW:SQ_ROW + 1, :] = jnp.full((1, D), sq, F32)
        own = pltpu.make_async_copy(mine, slots.at[_linear(_position())], sem)
        own.start()
        own.wait()

    vmem = pl.BlockSpec(memory_space=pltpu.VMEM)
    return pl.pallas_call(
        body, name="small_pack",
        out_shape=(jax.ShapeDtypeStruct((PACK_ROWS, D), F32), jax.ShapeDtypeStruct((N_DEV, PACK_ROWS, D), F32)),
        in_specs=[vmem] * (ns + 1), out_specs=(vmem, ANY_SPEC),
        scratch_shapes=[pltpu.SemaphoreType.DMA(())], compiler_params=_params(),
    )(*[parts[n] for n in SMALL], parts["sq"])


def _small_exchange(mine, slots):
    def plan(refs, me, j):
        peer = _peer(me, j + 1)
        return refs[0], refs[1].at[_linear(me)], peer, refs[1].at[_linear(peer)]

    send, recv, mine1, slots1 = _copies_start([mine, slots], plan, N_DEV - 1, name="small_send")
    return lambda after: _copies_wait(send, recv, [mine1, slots1], plan, N_DEV - 1, after, name="small_recv")[1]


def _small_update(slots, sm, m_sm, v_sm):
    ns = len(SMALL)

    def body(*refs):
        tot = refs[0][0]
        for s in range(1, N_DEV):
            tot = tot + refs[0][s]
        w_refs, m_refs, v_refs = refs[1:ns + 1], refs[ns + 1:2 * ns + 1], refs[2 * ns + 1:3 * ns + 1]
        outs = refs[3 * ns + 1:]
        loss_ref = outs[0]
        g_out, d_out = outs[1:ns + 1], outs[ns + 1:2 * ns + 1]
        nm_out, nv_out = outs[2 * ns + 1:3 * ns + 1], outs[3 * ns + 1:4 * ns + 1]
        loss_ref[...] = tot[SQ_ROW:SQ_ROW + 1, 0:1]
        for r, name in enumerate(SMALL):
            wd = SMALL_W.get(name, D)
            g = tot[r:r + 1, 0:wd]
            w = w_refs[r][...]
            if name == "hgrn_lb":
                mx = jnp.maximum(w[0:1], w[1:2])
                e0, e1 = jnp.exp(w[0:1] - mx), jnp.exp(w[1:2] - mx)
                lb0 = e0 / (e0 + e1)
                g0 = g * lb0 * (1.0 - lb0)
                for i, gi in enumerate((g0, -g0)):
                    d, nm, nv = _adamw_math(w[i:i + 1], gi, m_refs[r][i:i + 1, :], v_refs[r][i:i + 1, :])
                    g_out[r][i:i + 1, :] = gi
                    d_out[r][i:i + 1, :], nm_out[r][i:i + 1, :], nv_out[r][i:i + 1, :] = d, nm, nv
            else:
                d, nm, nv = _adamw_math(w, g, m_refs[r][...], v_refs[r][...])
                g_out[r][...] = g
                d_out[r][...], nm_out[r][...], nv_out[r][...] = d, nm, nv

    shapes = [jax.ShapeDtypeStruct(sm[n].shape, F32) for n in SMALL]
    res = pl.pallas_call(
        body, name="small_update", out_shape=tuple([jax.ShapeDtypeStruct((1, 1), F32)] + shapes * 4),
        compiler_params=_params(),
    )(slots, *[sm[n] for n in SMALL], *[m_sm[n] for n in SMALL], *[v_sm[n] for n in SMALL])
    groups = [dict(zip(SMALL, res[1 + i * ns:1 + (i + 1) * ns])) for i in range(4)]
    return res[0], groups[0], groups[1], groups[2], groups[3]


BIG = ("w_in", "w_gate", "w_up", "w_down", "w_out", "wq_x", "wk_x", "wv_x", "wo_x")
BIG_KEY = dict(w_in="winT", w_gate="wgT", w_up="wuT", w_down="wd", w_out="wout", wq_x="wq", wk_x="wk",
               wv_x="wv", wo_x="wo")
TRANSPOSED = ("w_in", "w_gate", "w_up")
WEIGHTS = ("w_in", "sinks", "hgrn_lb", "hgrn_onorm", "w_out", "g_mix_pre", "g_mix_post", "g_mem", "g_x_pre",
           "g_x_post", "wq_x", "wk_x", "wv_x", "wo_x", "g_ffn_pre", "g_ffn_post", "w_gate", "w_up", "w_down")


def kernel(x, mem, w_in, sinks, hgrn_lb, hgrn_onorm, w_out, g_mix_pre, g_mix_post, g_mem, g_x_pre, g_x_post, wq_x, wk_x, wv_x, wo_x, g_ffn_pre, g_ffn_post, w_gate, w_up, w_down, loss_target, m_w_in, m_sinks, m_hgrn_lb, m_hgrn_onorm, m_w_out, m_g_mix_pre, m_g_mix_post, m_g_mem, m_g_x_pre, m_g_x_post, m_wq_x, m_wk_x, m_wv_x, m_wo_x, m_g_ffn_pre, m_g_ffn_post, m_w_gate, m_w_up, m_w_down, v_w_in, v_sinks, v_hgrn_lb, v_hgrn_onorm, v_w_out, v_g_mix_pre, v_g_mix_post, v_g_mem, v_g_x_pre, v_g_x_post, v_wq_x, v_wk_x, v_wv_x, v_wo_x, v_g_ffn_pre, v_g_ffn_post, v_w_gate, v_w_up, v_w_down):
    given = dict(locals())
    wts = {n: given[n] for n in WEIGHTS}
    ms = {n: given["m_" + n] for n in WEIGHTS}
    vs = {n: given["v_" + n] for n in WEIGHTS}

    def mat(a, name):
        a = a[0]
        return a.T if name in TRANSPOSED else a

    groups = (("w_in",), ("w_out", "wq_x", "wk_x", "wv_x", "wo_x"), ("w_gate", "w_up", "w_down"))
    gathers = []
    first_dep = None
    for tag, group in zip(("w_in", "w_attn", "w_ffn"), groups):
        shards, lands = _prepare_weights([mat(wts[n], n) for n in group], name="prepare_" + tag, dep=first_dep)
        gathers.append(_TwoLevelGather(shards, lands, name=tag))
        first_dep = gathers[-1].dep
    name_of = {k: n for n, k in BIG_KEY.items()}
    gathered = {}

    def milestone(tag, value):
        return gathers[{"swa": 1, "kv": 2}[tag]].pass_on(value)

    def fetch(key, after):
        name = name_of[key]
        if name not in gathered:
            g = [i for i, group in enumerate(groups) if name in group][0]
            if g == 0:
                gathers[0].pass_on(after)
            gathered.update(zip(groups[g], gathers[g].finish(after)))
        return gathered[name]

    sm = {n: wts[n] for n in SMALL}
    started, held = {}, {}
    send_with = {k: group for group in (("wgT", "wuT"), ("wo", "wq", "wout", "wk", "wv")) for k in group}

    def emit(key, g):
        held[key] = g
        group = send_with.get(key, (key,))
        if key != group[-1]:
            return None
        flights = _exchange_start([held[k] for k in group], name="grad_send_" + name_of[group[0]])
        started.update({name_of[k]: f for k, f in zip(group, flights)})
        return flights[-1][2]

    grad_x, _, parts = _local_step(x[0], mem[0], loss_target[0], fetch, sm, emit, first_dep=first_dep, milestone=milestone)
    small_finish = _small_exchange(*_small_pack(parts))
    grads, deltas, new_m, new_v = {}, {}, {}, {}
    after = grad_x
    for group in (("w_down",), ("w_gate", "w_up"), ("wo_x", "wq_x", "wk_x", "wv_x", "w_out"), ("w_in",)):
        items = []
        for n in group:
            g_all, land = _exchange_wait(*started[n], after, name="grad_recv_" + n)
            items.append((g_all, land, mat(wts[n], n), mat(ms[n], n), mat(vs[n], n)))
            after = land
        for n, res in zip(group, _sum_adamw(items, name="adamw_" + group[0])):
            after = res[1]
            if n in TRANSPOSED:
                res = [a.T for a in res]
            grads[n], deltas[n], new_m[n], new_v[n] = [a[None] for a in res]
    loss, g_s, d_s, m_s, v_s = _small_update(small_finish(after), sm, {n: ms[n] for n in SMALL},
                                             {n: vs[n] for n in SMALL})
    grads.update(g_s), deltas.update(d_s), new_m.update(m_s), new_v.update(v_s)
    return (loss[0, 0], grad_x[None], *[grads[n] for n in WEIGHTS], *[deltas[n] for n in WEIGHTS],
            *[new_m[n] for n in WEIGHTS], *[new_v[n] for n in WEIGHTS])
```
